```python
import jax, jax.numpy as jnp
from jax import lax
import numpy as np

D_MODEL = 1024
BATCH = 8
SEQ = 8192
DEPTH = 1

N_HEADS_A = 8
N_KV_A = 2
HEAD_DIM_A = 64
WINDOW = 128
BLOCK = 128
N_HEADS_B = 8
QK_NOPE = 64
QK_ROPE = 32
V_DIM_B = 64
Q_LORA = 256
KV_LORA = 128
ROPE_THETA = 10000.0
D_FF = 4 * D_MODEL
EPS = 1e-6

WIDTH_A = N_HEADS_A * HEAD_DIM_A
WIDTH_B = N_HEADS_B * V_DIM_B
KV_WIDTH_A = N_KV_A * HEAD_DIM_A
Q_HEAD_B = QK_NOPE + QK_ROPE
KV_HEAD_B = QK_NOPE + V_DIM_B
SPLITS = (D_MODEL, D_MODEL, WIDTH_A, KV_WIDTH_A, KV_WIDTH_A, Q_LORA, KV_LORA, QK_ROPE)
D_IN = int(sum(SPLITS))
SPLIT_IDX = tuple(int(i) for i in np.cumsum(SPLITS)[:-1])

kernel_name = "hybrid_swa_sink_alibi_mla_gated_sqrelu"


def rmsnorm(x, g):
    x32 = x.astype(jnp.float32)
    y = x32 * lax.rsqrt(jnp.mean(x32 * x32, axis=-1, keepdims=True) + EPS)
    return y.astype(x.dtype) * g


def alibi_slopes(n):
    return 2.0 ** (-8.0 * jnp.arange(1, n + 1, dtype=jnp.float32) / n)


def rope(x, pos):
    d = x.shape[-1]
    freqs = ROPE_THETA ** (-jnp.arange(0, d, 2, dtype=jnp.float32) / d)
    ang = pos.astype(jnp.float32)[..., None] * freqs
    cos, sin = jnp.cos(ang)[:, :, None, :], jnp.sin(ang)[:, :, None, :]
    x32 = x.astype(jnp.float32)
    x1, x2 = x32[..., : d // 2], x32[..., d // 2:]
    return jnp.concatenate([x1 * cos - x2 * sin, x2 * cos + x1 * sin], axis=-1).astype(x.dtype)


def swa_sink_alibi_attention(q, k, v, pos, sinks):
    B, S = q.shape[0], q.shape[1]
    nb = S // BLOCK
    G = N_HEADS_A // N_KV_A
    qb = q.reshape(B, nb, BLOCK, N_KV_A, G, HEAD_DIM_A)

    def band(t):
        padded = jnp.pad(t, [(0, 0), (BLOCK, 0)] + [(0, 0)] * (t.ndim - 2))
        prev = padded[:, :S].reshape((B, nb, BLOCK) + t.shape[2:])
        cur = t.reshape((B, nb, BLOCK) + t.shape[2:])
        return jnp.concatenate([prev, cur], axis=2)

    kb, vb, pb = band(k), band(v), band(pos)
    qpos = pos.reshape(B, nb, BLOCK)
    scale = HEAD_DIM_A ** -0.5
    s = jnp.einsum('bnqkgd,bnskd->bnkgqs', qb, kb).astype(jnp.float32) * scale
    dist = jnp.abs(qpos[:, :, :, None] - pb[:, :, None, :]).astype(jnp.float32)
    slopes = alibi_slopes(N_HEADS_A).reshape(N_KV_A, G)
    s = s - slopes[None, None, :, :, None, None] * dist[:, :, None, None]
    qi = jnp.arange(BLOCK)[:, None] + BLOCK
    si = jnp.arange(2 * BLOCK)[None, :]
    diff = qi - si
    valid = (diff >= 0) & (diff < WINDOW)
    not_pad = (jnp.arange(nb)[:, None, None] > 0) | (si[None] >= BLOCK)
    mask = valid[None] & not_pad
    s = jnp.where(mask[None, :, None, None], s, -jnp.inf)
    sink = sinks.astype(jnp.float32).reshape(1, 1, N_KV_A, G, 1, 1)
    m = jnp.maximum(jnp.max(s, axis=-1, keepdims=True), sink)
    e = jnp.exp(s - m)
    p = e / (jnp.sum(e, axis=-1, keepdims=True) + jnp.exp(sink - m))
    o = jnp.einsum('bnkgqs,bnskd->bnqkgd', p.astype(v.dtype), vb)
    return o.reshape(B, S, WIDTH_A)


def mla_attention(q_nope, q_rope, k_nope, k_rope, v):
    B, S = q_nope.shape[0], q_nope.shape[1]
    nb = S // BLOCK
    scale = Q_HEAD_B ** -0.5
    qn = q_nope.reshape(B, nb, BLOCK, N_HEADS_B, QK_NOPE).transpose(1, 0, 2, 3, 4)
    qr = q_rope.reshape(B, nb, BLOCK, N_HEADS_B, QK_ROPE).transpose(1, 0, 2, 3, 4)
    kidx = jnp.arange(S)

    def one_block(args):
        qn_b, qr_b, i = args
        s = (jnp.einsum('bqhd,bshd->bhqs', qn_b, k_nope)
             + jnp.einsum('bqhd,bsd->bhqs', qr_b, k_rope)).astype(jnp.float32) * scale
        qidx = i * BLOCK + jnp.arange(BLOCK)
        s = jnp.where(kidx[None, :] <= qidx[:, None], s, -jnp.inf)
        p = jax.nn.softmax(s, axis=-1)
        return jnp.einsum('bhqs,bshd->bqhd', p.astype(v.dtype), v)

    o = lax.map(one_block, (qn, qr, jnp.arange(nb)))
    return o.transpose(1, 0, 2, 3, 4).reshape(B, S, WIDTH_B)


def _fwd_setup_inputs(seed: int = 0) -> dict:
    key = jax.random.key(seed)
    ks = jax.random.split(key, 20)

    def w(k, shape, fan_in):
        return jax.random.normal(k, shape, jnp.float32) * fan_in ** -0.5

    def gain(k, n):
        return 1.0 + 0.02 * jax.random.normal(k, (DEPTH, n), jnp.float32)

    x = jax.random.normal(ks[0], (BATCH, SEQ, D_MODEL), jnp.float32)
    offset = jax.random.randint(ks[1], (BATCH, 1), 0, 1024, dtype=jnp.int32)
    positions = (offset + jnp.arange(SEQ, dtype=jnp.int32)[None, :]).astype(jnp.int32)
    return {
        "x": x,
        "positions": positions,
        "pre_norm_mix": gain(ks[2], D_MODEL),
        "w_in": w(ks[3], (DEPTH, D_MODEL, D_IN), D_MODEL),
        "q_a_norm": gain(ks[4], Q_LORA),
        "w_q_b": w(ks[5], (DEPTH, Q_LORA, N_HEADS_B * Q_HEAD_B), Q_LORA),
        "kv_a_norm": gain(ks[6], KV_LORA),
        "w_kv_b": w(ks[7], (DEPTH, KV_LORA, N_HEADS_B * KV_HEAD_B), KV_LORA),
        "sinks": jax.random.normal(ks[8], (DEPTH, N_HEADS_A), jnp.float32),
        "w_o_a": w(ks[9], (DEPTH, WIDTH_A, D_MODEL), WIDTH_A),
        "w_o_b": w(ks[10], (DEPTH, WIDTH_B, D_MODEL), WIDTH_B),
        "w_out": w(ks[11], (DEPTH, D_MODEL, D_MODEL), D_MODEL),
        "post_norm_mix": gain(ks[12], D_MODEL),
        "pre_norm_mlp": gain(ks[13], D_MODEL),
        "w_up": w(ks[14], (DEPTH, D_MODEL, D_FF), D_MODEL),
        "w_down": w(ks[15], (DEPTH, D_FF, D_MODEL), D_FF),
        "post_norm_mlp": gain(ks[16], D_MODEL),
    }


def _fwd_reference(x, positions, pre_norm_mix, w_in, q_a_norm, w_q_b, kv_a_norm, w_kv_b, sinks,
              w_o_a, w_o_b, w_out, post_norm_mix, pre_norm_mlp, w_up, w_down, post_norm_mlp):
    B, S = x.shape[0], x.shape[1]
    for l in range(DEPTH):
        h = rmsnorm(x, pre_norm_mix[l])
        proj = h @ w_in[l]
        g_a, g_b, qa, ka, va, cq, ckv, kr = jnp.split(proj, SPLIT_IDX, axis=-1)
        qa = qa.reshape(B, S, N_HEADS_A, HEAD_DIM_A)
        ka = ka.reshape(B, S, N_KV_A, HEAD_DIM_A)
        va = va.reshape(B, S, N_KV_A, HEAD_DIM_A)
        out_a = swa_sink_alibi_attention(qa, ka, va, positions, sinks[l])
        qb = (rmsnorm(cq, q_a_norm[l]) @ w_q_b[l]).reshape(B, S, N_HEADS_B, Q_HEAD_B)
        kvb = (rmsnorm(ckv, kv_a_norm[l]) @ w_kv_b[l]).reshape(B, S, N_HEADS_B, KV_HEAD_B)
        q_nope, q_rope = qb[..., :QK_NOPE], rope(qb[..., QK_NOPE:], positions)
        k_nope, v_b = kvb[..., :QK_NOPE], kvb[..., QK_NOPE:]
        k_rope = rope(kr[:, :, None, :], positions)[:, :, 0, :]
        out_b = mla_attention(q_nope, q_rope, k_nope, k_rope, v_b)
        merged = jax.nn.sigmoid(g_a) * (out_a @ w_o_a[l]) + jax.nn.sigmoid(g_b) * (out_b @ w_o_b[l])
        x = x + rmsnorm(merged @ w_out[l], post_norm_mix[l])
        h2 = rmsnorm(x, pre_norm_mlp[l])
        y = jnp.square(jax.nn.relu(h2 @ w_up[l])) @ w_down[l]
        x = x + rmsnorm(y, post_norm_mlp[l])
    return x


import jax as _jax
import jax.numpy as _jnp

TWIN_FORMAT = 'train_step'
FWD_PARAMS = ['x', 'positions', 'pre_norm_mix', 'w_in', 'q_a_norm', 'w_q_b', 'kv_a_norm', 'w_kv_b', 'sinks', 'w_o_a', 'w_o_b', 'w_out', 'post_norm_mix', 'pre_norm_mlp', 'w_up', 'w_down', 'post_norm_mlp']
TWIN_WEIGHTS = ['pre_norm_mix', 'w_in', 'q_a_norm', 'w_q_b', 'kv_a_norm', 'w_kv_b', 'sinks', 'w_o_a', 'w_o_b', 'w_out', 'post_norm_mix', 'pre_norm_mlp', 'w_up', 'w_down', 'post_norm_mlp']
TWIN_DIFF_INPUT = 'x'
TWIN_INPUTS = ['x', 'positions', 'pre_norm_mix', 'w_in', 'q_a_norm', 'w_q_b', 'kv_a_norm', 'w_kv_b', 'sinks', 'w_o_a', 'w_o_b', 'w_out', 'post_norm_mix', 'pre_norm_mlp', 'w_up', 'w_down', 'post_norm_mlp', 'loss_target', 'm_pre_norm_mix', 'm_w_in', 'm_q_a_norm', 'm_w_q_b', 'm_kv_a_norm', 'm_w_kv_b', 'm_sinks', 'm_w_o_a', 'm_w_o_b', 'm_w_out', 'm_post_norm_mix', 'm_pre_norm_mlp', 'm_w_up', 'm_w_down', 'm_post_norm_mlp', 'v_pre_norm_mix', 'v_w_in', 'v_q_a_norm', 'v_w_q_b', 'v_kv_a_norm', 'v_w_kv_b', 'v_sinks', 'v_w_o_a', 'v_w_o_b', 'v_w_out', 'v_post_norm_mix', 'v_pre_norm_mlp', 'v_w_up', 'v_w_down', 'v_post_norm_mlp']
TWIN_OUTPUTS = ['loss', 'grad_x', 'grad_pre_norm_mix', 'grad_w_in', 'grad_q_a_norm', 'grad_w_q_b', 'grad_kv_a_norm', 'grad_w_kv_b', 'grad_sinks', 'grad_w_o_a', 'grad_w_o_b', 'grad_w_out', 'grad_post_norm_mix', 'grad_pre_norm_mlp', 'grad_w_up', 'grad_w_down', 'grad_post_norm_mlp', 'delta_pre_norm_mix', 'delta_w_in', 'delta_q_a_norm', 'delta_w_q_b', 'delta_kv_a_norm', 'delta_w_kv_b', 'delta_sinks', 'delta_w_o_a', 'delta_w_o_b', 'delta_w_out', 'delta_post_norm_mix', 'delta_pre_norm_mlp', 'delta_w_up', 'delta_w_down', 'delta_post_norm_mlp', 'new_m_pre_norm_mix', 'new_m_w_in', 'new_m_q_a_norm', 'new_m_w_q_b', 'new_m_kv_a_norm', 'new_m_w_kv_b', 'new_m_sinks', 'new_m_w_o_a', 'new_m_w_o_b', 'new_m_w_out', 'new_m_post_norm_mix', 'new_m_pre_norm_mlp', 'new_m_w_up', 'new_m_w_down', 'new_m_post_norm_mlp', 'new_v_pre_norm_mix', 'new_v_w_in', 'new_v_q_a_norm', 'new_v_w_q_b', 'new_v_kv_a_norm', 'new_v_w_kv_b', 'new_v_sinks', 'new_v_w_o_a', 'new_v_w_o_b', 'new_v_w_out', 'new_v_post_norm_mix', 'new_v_pre_norm_mlp', 'new_v_w_up', 'new_v_w_down', 'new_v_post_norm_mlp']
TWIN_LEAF_KINDS = {'loss': 'loss', 'grad_x': 'grad_x', 'grad_pre_norm_mix': 'grad_w', 'grad_w_in': 'grad_w', 'grad_q_a_norm': 'grad_w', 'grad_w_q_b': 'grad_w', 'grad_kv_a_norm': 'grad_w', 'grad_w_kv_b': 'grad_w', 'grad_sinks': 'grad_w', 'grad_w_o_a': 'grad_w', 'grad_w_o_b': 'grad_w', 'grad_w_out': 'grad_w', 'grad_post_norm_mix': 'grad_w', 'grad_pre_norm_mlp': 'grad_w', 'grad_w_up': 'grad_w', 'grad_w_down': 'grad_w', 'grad_post_norm_mlp': 'grad_w', 'delta_pre_norm_mix': 'delta_w', 'delta_w_in': 'delta_w', 'delta_q_a_norm': 'delta_w', 'delta_w_q_b': 'delta_w', 'delta_kv_a_norm': 'delta_w', 'delta_w_kv_b': 'delta_w', 'delta_sinks': 'delta_w', 'delta_w_o_a': 'delta_w', 'delta_w_o_b': 'delta_w', 'delta_w_out': 'delta_w', 'delta_post_norm_mix': 'delta_w', 'delta_pre_norm_mlp': 'delta_w', 'delta_w_up': 'delta_w', 'delta_w_down': 'delta_w', 'delta_post_norm_mlp': 'delta_w', 'new_m_pre_norm_mix': 'new_m', 'new_m_w_in': 'new_m', 'new_m_q_a_norm': 'new_m', 'new_m_w_q_b': 'new_m', 'new_m_kv_a_norm': 'new_m', 'new_m_w_kv_b': 'new_m', 'new_m_sinks': 'new_m', 'new_m_w_o_a': 'new_m', 'new_m_w_o_b': 'new_m', 'new_m_w_out': 'new_m', 'new_m_post_norm_mix': 'new_m', 'new_m_pre_norm_mlp': 'new_m', 'new_m_w_up': 'new_m', 'new_m_w_down': 'new_m', 'new_m_post_norm_mlp': 'new_m', 'new_v_pre_norm_mix': 'new_v', 'new_v_w_in': 'new_v', 'new_v_q_a_norm': 'new_v', 'new_v_w_q_b': 'new_v', 'new_v_kv_a_norm': 'new_v', 'new_v_w_kv_b': 'new_v', 'new_v_sinks': 'new_v', 'new_v_w_o_a': 'new_v', 'new_v_w_o_b': 'new_v', 'new_v_w_out': 'new_v', 'new_v_post_norm_mix': 'new_v', 'new_v_pre_norm_mlp': 'new_v', 'new_v_w_up': 'new_v', 'new_v_w_down': 'new_v', 'new_v_post_norm_mlp': 'new_v'}


def _forward(args):
    return _fwd_reference(*[args[k] for k in FWD_PARAMS])


def _output_shape():
    def fwd():
        inp = _fwd_setup_inputs(0)
        return _fwd_reference(*[inp[k] for k in FWD_PARAMS])
    out = _jax.eval_shape(fwd)
    return out.shape, out.dtype

N_MICROBATCH = 1
ADAM_LR = 0.001
ADAM_B1 = 0.9
ADAM_B2 = 0.999
ADAM_EPS = 1e-08
ADAM_WD = 0.01
ADAM_STEP = 10
PER_EXAMPLE_BATCH_AXIS = {'x': 0, 'positions': 0, 'loss_target': 0}
SHARED_INPUTS = []
_WEIGHT_DTYPES = {'pre_norm_mix': _jnp.float32, 'w_in': _jnp.float32, 'q_a_norm': _jnp.float32, 'w_q_b': _jnp.float32, 'kv_a_norm': _jnp.float32, 'w_kv_b': _jnp.float32, 'sinks': _jnp.float32, 'w_o_a': _jnp.float32, 'w_o_b': _jnp.float32, 'w_out': _jnp.float32, 'post_norm_mix': _jnp.float32, 'pre_norm_mlp': _jnp.float32, 'w_up': _jnp.float32, 'w_down': _jnp.float32, 'post_norm_mlp': _jnp.float32}
MOMENT_SCALE = {'pre_norm_mix': 1.295658e+00, 'w_in': 7.099084e-01, 'q_a_norm': 6.143588e-01, 'w_q_b': 3.471636e-01, 'kv_a_norm': 2.348868e+00, 'w_kv_b': 7.251259e-01, 'sinks': 2.155994e+00, 'w_o_a': 2.090301e+00, 'w_o_b': 6.588847e-01, 'w_out': 2.307746e+00, 'post_norm_mix': 6.415041e+01, 'pre_norm_mlp': 2.177658e+00, 'w_up': 1.063655e+00, 'w_down': 2.222215e+00, 'post_norm_mlp': 6.562936e+01}


def _to_microbatches(a, axis):
    t = _jnp.moveaxis(a, axis, 0)
    t = t.reshape((N_MICROBATCH, t.shape[0] // N_MICROBATCH) + t.shape[1:])
    return _jnp.moveaxis(t, 1, axis + 1)


def setup_inputs(seed: int = 0) -> dict:
    inp = _fwd_setup_inputs(seed)
    key = _jax.random.fold_in(_jax.random.key(seed), 7919)
    shape, _ = _output_shape()
    out = dict(inp)
    out["loss_target"] = _jax.random.normal(_jax.random.fold_in(key, 0), shape, _jnp.float32)
    for i, name in enumerate(TWIN_WEIGHTS):
        w = inp[name].astype(_jnp.float32)
        if MOMENT_SCALE is None:
            s = _jnp.sqrt(_jnp.mean(_jnp.square(w)) + 1e-30)
        else:
            s = MOMENT_SCALE[name]
        km, kv = _jax.random.split(_jax.random.fold_in(key, i + 1))
        out[name] = w
        out["m_" + name] = s * _jax.random.normal(km, w.shape, _jnp.float32)
        out["v_" + name] = (s * s) * _jax.random.uniform(kv, w.shape, _jnp.float32, 0.5, 1.5)
    if N_MICROBATCH > 1:
        for name, axis in PER_EXAMPLE_BATCH_AXIS.items():
            out[name] = _to_microbatches(out[name], axis)
    return {'x': out['x'], 'positions': out['positions'], 'pre_norm_mix': out['pre_norm_mix'], 'w_in': out['w_in'], 'q_a_norm': out['q_a_norm'], 'w_q_b': out['w_q_b'], 'kv_a_norm': out['kv_a_norm'], 'w_kv_b': out['w_kv_b'], 'sinks': out['sinks'], 'w_o_a': out['w_o_a'], 'w_o_b': out['w_o_b'], 'w_out': out['w_out'], 'post_norm_mix': out['post_norm_mix'], 'pre_norm_mlp': out['pre_norm_mlp'], 'w_up': out['w_up'], 'w_down': out['w_down'], 'post_norm_mlp': out['post_norm_mlp'], 'loss_target': out['loss_target'], 'm_pre_norm_mix': out['m_pre_norm_mix'], 'm_w_in': out['m_w_in'], 'm_q_a_norm': out['m_q_a_norm'], 'm_w_q_b': out['m_w_q_b'], 'm_kv_a_norm': out['m_kv_a_norm'], 'm_w_kv_b': out['m_w_kv_b'], 'm_sinks': out['m_sinks'], 'm_w_o_a': out['m_w_o_a'], 'm_w_o_b': out['m_w_o_b'], 'm_w_out': out['m_w_out'], 'm_post_norm_mix': out['m_post_norm_mix'], 'm_pre_norm_mlp': out['m_pre_norm_mlp'], 'm_w_up': out['m_w_up'], 'm_w_down': out['m_w_down'], 'm_post_norm_mlp': out['m_post_norm_mlp'], 'v_pre_norm_mix': out['v_pre_norm_mix'], 'v_w_in': out['v_w_in'], 'v_q_a_norm': out['v_q_a_norm'], 'v_w_q_b': out['v_w_q_b'], 'v_kv_a_norm': out['v_kv_a_norm'], 'v_w_kv_b': out['v_w_kv_b'], 'v_sinks': out['v_sinks'], 'v_w_o_a': out['v_w_o_a'], 'v_w_o_b': out['v_w_o_b'], 'v_w_out': out['v_w_out'], 'v_post_norm_mix': out['v_post_norm_mix'], 'v_pre_norm_mlp': out['v_pre_norm_mlp'], 'v_w_up': out['v_w_up'], 'v_w_down': out['v_w_down'], 'v_post_norm_mlp': out['v_post_norm_mlp']}


def _loss(weights, diff, rest, loss_target):
    with _jax.named_scope("forward"):
        args = {**rest, TWIN_DIFF_INPUT: diff, **{k: w.astype(_WEIGHT_DTYPES[k]) for k, w in weights.items()}}
        y = _forward(args)
    with _jax.named_scope("loss_head"):
        err = _jnp.square(y.astype(_jnp.float32) - loss_target)
        return 0.5 * _jnp.sum(_jnp.mean(err, axis=-1)) if err.ndim else 0.5 * err


def _adamw(w, g, m, v):
    m = ADAM_B1 * m + (1.0 - ADAM_B1) * g
    v = ADAM_B2 * v + (1.0 - ADAM_B2) * _jnp.square(g)
    m_hat = m / (1.0 - ADAM_B1 ** ADAM_STEP)
    v_hat = v / (1.0 - ADAM_B2 ** ADAM_STEP)
    delta = -ADAM_LR * (m_hat / (_jnp.sqrt(v_hat) + ADAM_EPS) + ADAM_WD * w)
    return delta, m, v


def reference(x, positions, pre_norm_mix, w_in, q_a_norm, w_q_b, kv_a_norm, w_kv_b, sinks, w_o_a, w_o_b, w_out, post_norm_mix, pre_norm_mlp, w_up, w_down, post_norm_mlp, loss_target, m_pre_norm_mix, m_w_in, m_q_a_norm, m_w_q_b, m_kv_a_norm, m_w_kv_b, m_sinks, m_w_o_a, m_w_o_b, m_w_out, m_post_norm_mix, m_pre_norm_mlp, m_w_up, m_w_down, m_post_norm_mlp, v_pre_norm_mix, v_w_in, v_q_a_norm, v_w_q_b, v_kv_a_norm, v_w_kv_b, v_sinks, v_w_o_a, v_w_o_b, v_w_out, v_post_norm_mix, v_pre_norm_mlp, v_w_up, v_w_down, v_post_norm_mlp):
    given = dict(x=x, positions=positions, pre_norm_mix=pre_norm_mix, w_in=w_in, q_a_norm=q_a_norm, w_q_b=w_q_b, kv_a_norm=kv_a_norm, w_kv_b=w_kv_b, sinks=sinks, w_o_a=w_o_a, w_o_b=w_o_b, w_out=w_out, post_norm_mix=post_norm_mix, pre_norm_mlp=pre_norm_mlp, w_up=w_up, w_down=w_down, post_norm_mlp=post_norm_mlp, loss_target=loss_target, m_pre_norm_mix=m_pre_norm_mix, m_w_in=m_w_in, m_q_a_norm=m_q_a_norm, m_w_q_b=m_w_q_b, m_kv_a_norm=m_kv_a_norm, m_w_kv_b=m_w_kv_b, m_sinks=m_sinks, m_w_o_a=m_w_o_a, m_w_o_b=m_w_o_b, m_w_out=m_w_out, m_post_norm_mix=m_post_norm_mix, m_pre_norm_mlp=m_pre_norm_mlp, m_w_up=m_w_up, m_w_down=m_w_down, m_post_norm_mlp=m_post_norm_mlp, v_pre_norm_mix=v_pre_norm_mix, v_w_in=v_w_in, v_q_a_norm=v_q_a_norm, v_w_q_b=v_w_q_b, v_kv_a_norm=v_kv_a_norm, v_w_kv_b=v_w_kv_b, v_sinks=v_sinks, v_w_o_a=v_w_o_a, v_w_o_b=v_w_o_b, v_w_out=v_w_out, v_post_norm_mix=v_post_norm_mix, v_pre_norm_mlp=v_pre_norm_mlp, v_w_up=v_w_up, v_w_down=v_w_down, v_post_norm_mlp=v_post_norm_mlp)
    weights = {n: given[n] for n in TWIN_WEIGHTS}
    shared = {n: given[n] for n in SHARED_INPUTS}
    per_example = {n: given[n] for n in ['x', 'positions']}
    grad_fn = _jax.value_and_grad(_loss, argnums=(0, 1))

    def one_microbatch(ex, loss_target):
        ex = dict(ex)
        diff = ex.pop(TWIN_DIFF_INPUT)
        return grad_fn(weights, diff, {**shared, **ex}, loss_target)

    if N_MICROBATCH == 1:
        loss, (grad_w, grad_x) = one_microbatch(per_example, given["loss_target"])
    else:
        def body(carry, xs):
            loss_sum, grad_sum = carry
            l_k, (gw_k, gx_k) = one_microbatch(xs[0], xs[1])
            with _jax.named_scope("update"):
                return (loss_sum + l_k, _jax.tree.map(_jnp.add, grad_sum, gw_k)), gx_k

        init = (_jnp.zeros((), _jnp.float32), _jax.tree.map(_jnp.zeros_like, weights))
        (loss, grad_w), grad_x = _jax.lax.scan(body, init, (per_example, given["loss_target"]))
    with _jax.named_scope("update"):
        delta_w, new_m, new_v = {}, {}, {}
        for n in TWIN_WEIGHTS:
            delta_w[n], new_m[n], new_v[n] = _adamw(weights[n], grad_w[n], given["m_" + n], given["v_" + n])
    return (loss, grad_x, *[grad_w[n] for n in TWIN_WEIGHTS], *[delta_w[n] for n in TWIN_WEIGHTS],
            *[new_m[n] for n in TWIN_WEIGHTS], *[new_v[n] for n in TWIN_WEIGHTS])
```

```python
import functools

import jax
import jax.numpy as jnp
from jax import lax
from jax.experimental import pallas as pl
from jax.experimental.pallas import tpu as pltpu

F32 = jnp.float32
BF16 = jnp.bfloat16
MESH = pl.DeviceIdType.MESH

D_MODEL = 1024
N_HEADS_A = 8
N_KV_A = 2
HEAD_DIM_A = 64
WINDOW = 128
BLOCK = 128
N_HEADS_B = 8
QK_NOPE = 64
QK_ROPE = 32
V_DIM_B = 64
Q_LORA = 256
KV_LORA = 128
ROPE_THETA = 10000.0
D_FF = 4 * D_MODEL
EPS = 1e-6
WIDTH_A = N_HEADS_A * HEAD_DIM_A
Q_HEAD_B = QK_NOPE + QK_ROPE
D_IN = 3232
D_IN_PAD = 3328
HEAD_PAD = 128
MLA_W = N_HEADS_B * HEAD_PAD

ADAM_LR = 0.001
ADAM_B1 = 0.9
ADAM_B2 = 0.999
ADAM_EPS = 1e-08
ADAM_WD = 0.01
ADAM_STEP = 10

NEG = -1e30
N_CHIPS = 4
LANES = 128
VMEM_LIMIT = 56 * 1024 * 1024

BIG = ("w_in", "w_q_b", "w_kv_b", "w_o_a", "w_o_b", "w_out", "w_up", "w_down")
COL_SHARDED = ("w_in", "w_q_b", "w_kv_b", "w_o_a", "w_o_b", "w_up")
SHARD_SHAPES = {"w_in": (1024, 808), "w_q_b": (256, 192), "w_kv_b": (128, 256), "w_o_a": (512, 256),
                "w_o_b": (512, 256), "w_out": (256, 1024), "w_up": (1024, 1024), "w_down": (1024, 1024)}
PACK_ROWS = {n: (s[0] * s[1]) // D_MODEL for n, s in SHARD_SHAPES.items()}
PACK_TOTAL = 3456
SMALL = ("pre_norm_mix", "post_norm_mix", "pre_norm_mlp", "post_norm_mlp", "q_a_norm", "kv_a_norm", "sinks")
WEIGHTS = ("pre_norm_mix", "w_in", "q_a_norm", "w_q_b", "kv_a_norm", "w_kv_b", "sinks", "w_o_a", "w_o_b", "w_out",
           "post_norm_mix", "pre_norm_mlp", "w_up", "w_down", "post_norm_mlp")


def _params(sem=None):
    return pltpu.CompilerParams(dimension_semantics=sem, vmem_limit_bytes=VMEM_LIMIT)


def _dot(a, b):
    return jnp.dot(a, b, preferred_element_type=F32)


def _dot_nt(a, b):
    return lax.dot_general(a, b, (((1,), (1,)), ((), ())), preferred_element_type=F32)


def _dot_tn(a, b):
    return lax.dot_general(a, b, (((0,), (0,)), ((), ())), preferred_element_type=F32)


def _rms(v):
    return lax.rsqrt(jnp.mean(v * v, axis=-1, keepdims=True) + EPS)


def _norm_bwd(dout, n, r, g):
    dn = dout * g
    dx = r * (dn - n * jnp.mean(dn * n, axis=-1, keepdims=True))
    return dx, jnp.sum(dout * n, axis=0, keepdims=True)


def _full(shape):
    return pl.BlockSpec(shape, lambda *_: (0,) * len(shape))


def _proj_fwd(x, g1, w_in_p):
    t = x.shape[0]
    tm = 256

    def body(x_ref, g_ref, w_ref, h_ref, p_ref):
        xv = x_ref[...]
        h = ((xv * _rms(xv)) * g_ref[...]).astype(BF16)
        h_ref[...] = h
        p_ref[...] = _dot(h, w_ref[...])

    return pl.pallas_call(
        body, name="proj_fwd", grid=(t // tm,),
        in_specs=[pl.BlockSpec((tm, D_MODEL), lambda i: (i, 0)), _full((1, D_MODEL)), _full((D_MODEL, D_IN_PAD))],
        out_specs=[pl.BlockSpec((tm, D_MODEL), lambda i: (i, 0)), pl.BlockSpec((tm, D_IN_PAD), lambda i: (i, 0))],
        out_shape=[jax.ShapeDtypeStruct((t, D_MODEL), BF16), jax.ShapeDtypeStruct((t, D_IN_PAD), F32)],
        compiler_params=_params(("parallel",)),
    )(x, g1, w_in_p)


_QA_BLK = 2048 // WIDTH_A
_KA_BLK = 2560 // LANES
_VA_BLK = 2688 // LANES
_CQ_BLK = 2816 // Q_LORA
_CKV_BLK = 3072 // LANES
_KR_BLK = 3200 // LANES


def _swa_scores(q_ref, kb, dist, valid, h):
    kh = h // (N_HEADS_A // N_KV_A)
    q_h = q_ref[:, HEAD_DIM_A * h:HEAD_DIM_A * (h + 1)].astype(BF16)
    k_h = kb[:, HEAD_DIM_A * kh:HEAD_DIM_A * (kh + 1)]
    slope = 2.0 ** (-8.0 * (h + 1) / N_HEADS_A)
    s = _dot_nt(q_h, k_h) * (HEAD_DIM_A ** -0.5) - slope * dist
    return q_h, k_h, jnp.where(valid, s, NEG)


def _swa_band(n, kp_ref, kc_ref, vp_ref, vc_ref, pq_ref, pp_ref, pc_ref):
    kb = jnp.concatenate([kp_ref[...], kc_ref[...]], axis=0).astype(BF16)
    vb = jnp.concatenate([vp_ref[...], vc_ref[...]], axis=0).astype(BF16)
    posk = jnp.concatenate([pp_ref[...], pc_ref[...]], axis=1)
    dist = jnp.abs(pq_ref[...] - posk)
    qi = lax.broadcasted_iota(jnp.int32, (BLOCK, 2 * BLOCK), 0)
    si = lax.broadcasted_iota(jnp.int32, (BLOCK, 2 * BLOCK), 1)
    valid = (si > qi) & (si <= qi + WINDOW) & ((n > 0) | (si >= BLOCK))
    return kb, vb, dist, valid


def _swa_fwd(proj, posc, posr, sinks):
    t = proj.shape[0]
    nb = t // BLOCK

    def body(q_ref, kc_ref, kp_ref, vc_ref, vp_ref, pq_ref, pc_ref, pp_ref, sink_ref, o_ref, l_ref):
        n = pl.program_id(0)
        kb, vb, dist, valid = _swa_band(n, kp_ref, kc_ref, vp_ref, vc_ref, pq_ref, pp_ref, pc_ref)
        lane = lax.broadcasted_iota(jnp.int32, (BLOCK, LANES), 1)
        lse = jnp.zeros((BLOCK, LANES), F32)
        for h in range(N_HEADS_A):
            kh = h // (N_HEADS_A // N_KV_A)
            _, _, s = _swa_scores(q_ref, kb, dist, valid, h)
            sink = sink_ref[0:1, h:h + 1]
            m = jnp.maximum(jnp.max(s, axis=-1, keepdims=True), sink)
            e = jnp.exp(s - m)
            den = jnp.sum(e, axis=-1, keepdims=True) + jnp.exp(sink - m)
            p = e / den
            o_ref[:, HEAD_DIM_A * h:HEAD_DIM_A * (h + 1)] = _dot(p.astype(BF16), vb[:, HEAD_DIM_A * kh:HEAD_DIM_A * (kh + 1)])
            lse = jnp.where(lane == h, m + jnp.log(den), lse)
        l_ref[...] = lse

    cur = lambda n: (n, 0)
    return pl.pallas_call(
        body, name="swa_fwd", grid=(nb,),
        in_specs=[pl.BlockSpec((BLOCK, WIDTH_A), lambda n: (n, _QA_BLK)),
                  pl.BlockSpec((BLOCK, LANES), lambda n: (n, _KA_BLK)),
                  pl.BlockSpec((BLOCK, LANES), lambda n: (jnp.maximum(n - 1, 0), _KA_BLK)),
                  pl.BlockSpec((BLOCK, LANES), lambda n: (n, _VA_BLK)),
                  pl.BlockSpec((BLOCK, LANES), lambda n: (jnp.maximum(n - 1, 0), _VA_BLK)),
                  pl.BlockSpec((BLOCK, 1), cur),
                  pl.BlockSpec((1, BLOCK), lambda n: (0, n)),
                  pl.BlockSpec((1, BLOCK), lambda n: (0, jnp.maximum(n - 1, 0))),
                  _full((1, N_HEADS_A))],
        out_specs=[pl.BlockSpec((BLOCK, WIDTH_A), cur), pl.BlockSpec((BLOCK, LANES), cur)],
        out_shape=[jax.ShapeDtypeStruct((t, WIDTH_A), F32), jax.ShapeDtypeStruct((t, LANES), F32)],
        compiler_params=_params(("parallel",)),
    )(proj, proj, proj, proj, proj, posc, posr, posr, sinks)


def _rope_coeffs(pos, freq):
    ang = pos * freq
    cosv, sinv = jnp.cos(ang), jnp.sin(ang)
    lane = lax.broadcasted_iota(jnp.int32, ang.shape, 1)
    lo = (lane >= QK_NOPE) & (lane < QK_NOPE + QK_ROPE // 2)
    hi = (lane >= QK_NOPE + QK_ROPE // 2) & (lane < QK_NOPE + QK_ROPE)
    c = jnp.where(lane < QK_NOPE, 1.0, jnp.where(lo | hi, cosv, 0.0))
    s = jnp.where(lo, -sinv, jnp.where(hi, sinv, 0.0))
    return c, s, lo, hi


def _rope(xh, c, s, lo):
    up = pltpu.roll(xh, LANES - QK_ROPE // 2, axis=1)
    dn = pltpu.roll(xh, QK_ROPE // 2, axis=1)
    return xh * c + jnp.where(lo, up, dn) * s


def _unrope(dh, c, s, lo, hi):
    g = dh * s
    up = pltpu.roll(g, LANES - QK_ROPE // 2, axis=1)
    dn = pltpu.roll(g, QK_ROPE // 2, axis=1)
    return dh * c + jnp.where(hi, dn, jnp.where(lo, up, 0.0))


def _mla_prep_fwd(proj, posc, freq, qan, kvan, wq, wk, wv):
    t = proj.shape[0]
    tm = 256

    def body(cq_ref, ckv_ref, kr_ref, pos_ref, f_ref, qan_ref, kvan_ref, wq_ref, wk_ref, wv_ref, q_ref, k_ref, v_ref):
        cq = cq_ref[...]
        cqn = ((cq * _rms(cq)) * qan_ref[...]).astype(BF16)
        ckv = ckv_ref[...]
        ckvn = ((ckv * _rms(ckv)) * kvan_ref[...]).astype(BF16)
        qb = _dot(cqn, wq_ref[...])
        kb = _dot(ckvn, wk_ref[...])
        v_ref[...] = _dot(ckvn, wv_ref[...]).astype(BF16)
        c, s, lo, _ = _rope_coeffs(pos_ref[...], f_ref[...])
        kr = _rope(kr_ref[...], c, s, lo)
        for h in range(N_HEADS_B):
            sl = slice(HEAD_PAD * h, HEAD_PAD * (h + 1))
            q_ref[:, sl] = _rope(qb[:, sl], c, s, lo).astype(BF16)
            k_ref[:, sl] = (kb[:, sl] + kr).astype(BF16)

    row = lambda i: (i, 0)
    return pl.pallas_call(
        body, name="mla_prep_fwd", grid=(t // tm,),
        in_specs=[pl.BlockSpec((tm, Q_LORA), lambda i: (i, _CQ_BLK)),
                  pl.BlockSpec((tm, LANES), lambda i: (i, _CKV_BLK)),
                  pl.BlockSpec((tm, LANES), lambda i: (i, _KR_BLK)),
                  pl.BlockSpec((tm, 1), row), _full((1, LANES)), _full((1, Q_LORA)), _full((1, KV_LORA)),
                  _full((Q_LORA, MLA_W)), _full((KV_LORA, MLA_W)), _full((KV_LORA, MLA_W))],
        out_specs=[pl.BlockSpec((tm, MLA_W), row)] * 3,
        out_shape=[jax.ShapeDtypeStruct((t, MLA_W), BF16)] * 3,
        compiler_params=_params(("parallel",)),
    )(proj, proj, proj, posc, freq, qan, kvan, wq, wk, wv)


_TQ = 512
_TK = 512
_MLA_SCALE = Q_HEAD_B ** -0.5


def _causal_scores(q, k, qi, ki):
    s = _dot_nt(q, k) * _MLA_SCALE
    row = lax.broadcasted_iota(jnp.int32, s.shape, 0) + qi * _TQ
    col = lax.broadcasted_iota(jnp.int32, s.shape, 1) + ki * _TK
    return jnp.where(col <= row, s, NEG)


def _mla_fwd(q, k, v):
    t = q.shape[0]
    nq, nk = t // _TQ, t // _TK

    def body(q_ref, k_ref, v_ref, o_ref, l_ref, m_s, l_s, acc_s):
        qi, ki = pl.program_id(1), pl.program_id(2)

        @pl.when(ki == 0)
        def _():
            m_s[...] = jnp.full(m_s.shape, NEG, F32)
            l_s[...] = jnp.zeros(l_s.shape, F32)
            acc_s[...] = jnp.zeros(acc_s.shape, F32)

        @pl.when(ki <= qi)
        def _():
            s = _causal_scores(q_ref[...], k_ref[...], qi, ki)
            m_prev = m_s[...]
            m_new = jnp.maximum(m_prev, jnp.max(s, axis=-1, keepdims=True))
            alpha = jnp.exp(m_prev - m_new)
            p = jnp.exp(s - m_new)
            l_s[...] = alpha * l_s[...] + jnp.sum(p, axis=-1, keepdims=True)
            acc_s[...] = alpha * acc_s[...] + _dot(p.astype(BF16), v_ref[...])
            m_s[...] = m_new

        @pl.when(ki == qi)
        def _():
            o_ref[...] = acc_s[...] / l_s[...]
            l_ref[0] = m_s[...] + jnp.log(l_s[...])

    kv_map = lambda h, qi, ki: (jnp.minimum(ki, qi), h)
    return pl.pallas_call(
        body, name="mla_fwd", grid=(N_HEADS_B, nq, nk),
        in_specs=[pl.BlockSpec((_TQ, HEAD_PAD), lambda h, qi, ki: (qi, h)),
                  pl.BlockSpec((_TK, HEAD_PAD), kv_map), pl.BlockSpec((_TK, HEAD_PAD), kv_map)],
        out_specs=[pl.BlockSpec((_TQ, HEAD_PAD), lambda h, qi, ki: (qi, h)),
                   pl.BlockSpec((1, _TQ, 1), lambda h, qi, ki: (h, qi, 0))],
        out_shape=[jax.ShapeDtypeStruct((t, MLA_W), F32), jax.ShapeDtypeStruct((N_HEADS_B, t, 1), F32)],
        scratch_shapes=[pltpu.VMEM((_TQ, 1), F32), pltpu.VMEM((_TQ, 1), F32), pltpu.VMEM((_TQ, HEAD_PAD), F32)],
        compiler_params=_params(("parallel", "parallel", "arbitrary")),
    )(q, k, v)


def _mix_out_fwd(out_a, out_b, proj, x, w_oa, w_ob, w_out, g2, g3):
    t = x.shape[0]
    tm = 256

    def body(oa_ref, ob_ref, ga_ref, gb_ref, x_ref, woa_ref, wob_ref, wout_ref, g2_ref, g3_ref,
             mg_ref, y_ref, x1_ref, h2_ref):
        oa = _dot(oa_ref[...].astype(BF16), woa_ref[...])
        ob = _dot(ob_ref[...].astype(BF16), wob_ref[...])
        merged = (jax.nn.sigmoid(ga_ref[...]) * oa + jax.nn.sigmoid(gb_ref[...]) * ob).astype(BF16)
        mg_ref[...] = merged
        y = _dot(merged, wout_ref[...])
        y_ref[...] = y
        x1 = x_ref[...] + (y * _rms(y)) * g2_ref[...]
        x1_ref[...] = x1
        h2_ref[...] = ((x1 * _rms(x1)) * g3_ref[...]).astype(BF16)

    row = lambda i: (i, 0)
    blk = pl.BlockSpec((tm, D_MODEL), row)
    return pl.pallas_call(
        body, name="mix_out_fwd", grid=(t // tm,),
        in_specs=[pl.BlockSpec((tm, WIDTH_A), row), blk, pl.BlockSpec((tm, D_MODEL), lambda i: (i, 0)),
                  pl.BlockSpec((tm, D_MODEL), lambda i: (i, 1)), blk,
                  _full((WIDTH_A, D_MODEL)), _full((MLA_W, D_MODEL)), _full((D_MODEL, D_MODEL)),
                  _full((1, D_MODEL)), _full((1, D_MODEL))],
        out_specs=[blk, blk, blk, blk],
        out_shape=[jax.ShapeDtypeStruct((t, D_MODEL), BF16), jax.ShapeDtypeStruct((t, D_MODEL), F32),
                   jax.ShapeDtypeStruct((t, D_MODEL), F32), jax.ShapeDtypeStruct((t, D_MODEL), BF16)],
        compiler_params=_params(("parallel",)),
    )(out_a, out_b, proj, proj, x, w_oa, w_ob, w_out, g2, g3)


def _up_fwd(h2, w_up):
    t = h2.shape[0]
    tm, tn = 512, 1024

    def body(h_ref, w_ref, u_ref, a_ref):
        u = _dot(h_ref[...], w_ref[...])
        u_ref[...] = u
        a_ref[...] = jnp.square(jnp.maximum(u, 0.0)).astype(BF16)

    return pl.pallas_call(
        body, name="up_fwd", grid=(t // tm, D_FF // tn),
        in_specs=[pl.BlockSpec((tm, D_MODEL), lambda i, j: (i, 0)), pl.BlockSpec((D_MODEL, tn), lambda i, j: (0, j))],
        out_specs=[pl.BlockSpec((tm, tn), lambda i, j: (i, j))] * 2,
        out_shape=[jax.ShapeDtypeStruct((t, D_FF), F32), jax.ShapeDtypeStruct((t, D_FF), BF16)],
        compiler_params=_params(("parallel", "parallel")),
    )(h2, w_up)


def _down_fwd_loss(a, w_down, x1, target, g4):
    t = a.shape[0]
    tm, tk = 512, 1024
    nk = D_FF // tk

    def body(a_ref, w_ref, x1_ref, tg_ref, g_ref, dx2_ref, dyd_ref, dg_ref, loss_ref, acc):
        i, k = pl.program_id(0), pl.program_id(1)

        @pl.when((i == 0) & (k == 0))
        def _():
            dg_ref[...] = jnp.zeros(dg_ref.shape, F32)
            loss_ref[...] = jnp.zeros(loss_ref.shape, F32)

        @pl.when(k == 0)
        def _():
            acc[...] = jnp.zeros(acc.shape, F32)

        acc[...] += _dot(a_ref[...], w_ref[...])

        @pl.when(k == nk - 1)
        def _():
            yd = acc[...]
            r = _rms(yd)
            n = yd * r
            diff = (x1_ref[...] + n * g_ref[...]) - tg_ref[...]
            loss_ref[...] += 0.5 * jnp.sum(jnp.mean(diff * diff, axis=-1, keepdims=True), axis=0, keepdims=True)
            dx2 = diff * (1.0 / D_MODEL)
            dx2_ref[...] = dx2
            dyd, dg = _norm_bwd(dx2, n, r, g_ref[...])
            dyd_ref[...] = dyd.astype(BF16)
            dg_ref[...] += dg

    row = lambda i, k: (i, 0)
    return pl.pallas_call(
        body, name="down_fwd_loss", grid=(t // tm, nk),
        in_specs=[pl.BlockSpec((tm, tk), lambda i, k: (i, k)), pl.BlockSpec((tk, D_MODEL), lambda i, k: (k, 0)),
                  pl.BlockSpec((tm, D_MODEL), row), pl.BlockSpec((tm, D_MODEL), row), _full((1, D_MODEL))],
        out_specs=[pl.BlockSpec((tm, D_MODEL), row), pl.BlockSpec((tm, D_MODEL), row), _full((1, D_MODEL)), _full((1, LANES))],
        out_shape=[jax.ShapeDtypeStruct((t, D_MODEL), F32), jax.ShapeDtypeStruct((t, D_MODEL), BF16),
                   jax.ShapeDtypeStruct((1, D_MODEL), F32), jax.ShapeDtypeStruct((1, LANES), F32)],
        scratch_shapes=[pltpu.VMEM((tm, D_MODEL), F32)],
        compiler_params=_params(("arbitrary", "arbitrary")),
    )(a, w_down, x1, target, g4)


def _matmul_tn(a, b, name, tm, tn, tk=1024):
    t, m = a.shape
    n = b.shape[1]
    tk = min(tk, t)
    nk = t // tk

    def body(a_ref, b_ref, o_ref):
        @pl.when(pl.program_id(2) == 0)
        def _():
            o_ref[...] = jnp.zeros(o_ref.shape, F32)

        o_ref[...] += _dot_tn(a_ref[...].astype(BF16), b_ref[...].astype(BF16))

    return pl.pallas_call(
        body, name=name, grid=(m // tm, n // tn, nk),
        in_specs=[pl.BlockSpec((tk, tm), lambda i, j, k: (k, i)), pl.BlockSpec((tk, tn), lambda i, j, k: (k, j))],
        out_specs=pl.BlockSpec((tm, tn), lambda i, j, k: (i, j)),
        out_shape=jax.ShapeDtypeStruct((m, n), F32),
        compiler_params=_params(("parallel", "parallel", "arbitrary")),
    )(a, b)


def _down_bwd(dyd, w_down, u):
    t = dyd.shape[0]
    tm, tn = 512, 1024

    def body(d_ref, w_ref, u_ref, du_ref):
        da = _dot_nt(d_ref[...], w_ref[...])
        du_ref[...] = (da * (2.0 * jnp.maximum(u_ref[...], 0.0))).astype(BF16)

    return pl.pallas_call(
        body, name="down_bwd", grid=(t // tm, D_FF // tn),
        in_specs=[pl.BlockSpec((tm, D_MODEL), lambda i, j: (i, 0)), pl.BlockSpec((tn, D_MODEL), lambda i, j: (j, 0)),
                  pl.BlockSpec((tm, tn), lambda i, j: (i, j))],
        out_specs=pl.BlockSpec((tm, tn), lambda i, j: (i, j)),
        out_shape=jax.ShapeDtypeStruct((t, D_FF), BF16),
        compiler_params=_params(("parallel", "parallel")),
    )(dyd, w_down, u)


def _up_bwd(du, w_up, x1, dx2, y, g3, g2):
    t = du.shape[0]
    tm, tk = 512, 1024
    nk = D_FF // tk

    def body(du_ref, w_ref, x1_ref, dx2_ref, y_ref, g3_ref, g2_ref, dx1_ref, dy_ref, dg3_ref, dg2_ref, acc):
        i, k = pl.program_id(0), pl.program_id(1)

        @pl.when((i == 0) & (k == 0))
        def _():
            dg3_ref[...] = jnp.zeros(dg3_ref.shape, F32)
            dg2_ref[...] = jnp.zeros(dg2_ref.shape, F32)

        @pl.when(k == 0)
        def _():
            acc[...] = jnp.zeros(acc.shape, F32)

        acc[...] += _dot_nt(du_ref[...], w_ref[...])

        @pl.when(k == nk - 1)
        def _():
            x1 = x1_ref[...]
            r3 = _rms(x1)
            d3, dg3 = _norm_bwd(acc[...], x1 * r3, r3, g3_ref[...])
            dx1 = dx2_ref[...] + d3
            dx1_ref[...] = dx1
            dg3_ref[...] += dg3
            y = y_ref[...]
            r2 = _rms(y)
            dy, dg2 = _norm_bwd(dx1, y * r2, r2, g2_ref[...])
            dy_ref[...] = dy.astype(BF16)
            dg2_ref[...] += dg2

    row = lambda i, k: (i, 0)
    blk = pl.BlockSpec((tm, D_MODEL), row)
    return pl.pallas_call(
        body, name="up_bwd", grid=(t // tm, nk),
        in_specs=[pl.BlockSpec((tm, tk), lambda i, k: (i, k)), pl.BlockSpec((D_MODEL, tk), lambda i, k: (0, k)),
                  blk, blk, blk, _full((1, D_MODEL)), _full((1, D_MODEL))],
        out_specs=[blk, blk, _full((1, D_MODEL)), _full((1, D_MODEL))],
        out_shape=[jax.ShapeDtypeStruct((t, D_MODEL), F32), jax.ShapeDtypeStruct((t, D_MODEL), BF16),
                   jax.ShapeDtypeStruct((1, D_MODEL), F32), jax.ShapeDtypeStruct((1, D_MODEL), F32)],
        scratch_shapes=[pltpu.VMEM((tm, D_MODEL), F32)],
        compiler_params=_params(("arbitrary", "arbitrary")),
    )(du, w_up, x1, dx2, y, g3, g2)


def _mix_out_bwd(dy, out_a, out_b, proj, w_oa, w_ob, w_out):
    t = dy.shape[0]
    tm = 256

    def body(dy_ref, oa_ref, ob_ref, ga_ref, gb_ref, woa_ref, wob_ref, wout_ref,
             doa_ref, dob_ref, dga_ref, dgb_ref, da_ref, db_ref, dela_ref, delb_ref):
        dm = _dot_nt(dy_ref[...], wout_ref[...])
        out_a_v, out_b_v = oa_ref[...], ob_ref[...]
        oa = _dot(out_a_v.astype(BF16), woa_ref[...])
        ob = _dot(out_b_v.astype(BF16), wob_ref[...])
        sa, sb = jax.nn.sigmoid(ga_ref[...]), jax.nn.sigmoid(gb_ref[...])
        doa = (dm * sa).astype(BF16)
        dob = (dm * sb).astype(BF16)
        doa_ref[...] = doa
        dob_ref[...] = dob
        dga_ref[...] = (dm * oa * (sa * (1.0 - sa))).astype(BF16)
        dgb_ref[...] = (dm * ob * (sb * (1.0 - sb))).astype(BF16)
        d_out_a = _dot_nt(doa, woa_ref[...])
        d_out_b = _dot_nt(dob, wob_ref[...])
        da_ref[...] = d_out_a
        db_ref[...] = d_out_b
        lane = lax.broadcasted_iota(jnp.int32, (tm, LANES), 1)
        prod_a = d_out_a * out_a_v
        dela = jnp.zeros((tm, LANES), F32)
        for h in range(N_HEADS_A):
            dh = jnp.sum(prod_a[:, HEAD_DIM_A * h:HEAD_DIM_A * (h + 1)], axis=-1, keepdims=True)
            dela = jnp.where(lane == h, dh, dela)
        dela_ref[...] = dela
        prod_b = d_out_b * out_b_v
        for h in range(N_HEADS_B):
            delb_ref[h] = jnp.sum(prod_b[:, HEAD_PAD * h:HEAD_PAD * (h + 1)], axis=-1, keepdims=True)

    row = lambda i: (i, 0)
    blk = pl.BlockSpec((tm, D_MODEL), row)
    return pl.pallas_call(
        body, name="mix_out_bwd", grid=(t // tm,),
        in_specs=[blk, pl.BlockSpec((tm, WIDTH_A), row), blk, pl.BlockSpec((tm, D_MODEL), lambda i: (i, 0)),
                  pl.BlockSpec((tm, D_MODEL), lambda i: (i, 1)),
                  _full((WIDTH_A, D_MODEL)), _full((MLA_W, D_MODEL)), _full((D_MODEL, D_MODEL))],
        out_specs=[blk, blk, blk, blk, pl.BlockSpec((tm, WIDTH_A), row), blk, pl.BlockSpec((tm, LANES), row),
                   pl.BlockSpec((N_HEADS_B, tm, 1), lambda i: (0, i, 0))],
        out_shape=[jax.ShapeDtypeStruct((t, D_MODEL), BF16)] * 4
        + [jax.ShapeDtypeStruct((t, WIDTH_A), F32), jax.ShapeDtypeStruct((t, MLA_W), F32),
           jax.ShapeDtypeStruct((t, LANES), F32), jax.ShapeDtypeStruct((N_HEADS_B, t, 1), F32)],
        compiler_params=_params(("parallel",)),
    )(dy, out_a, out_b, proj, proj, w_oa, w_ob, w_out)


def _mla_bwd(q, k, v, d_out, lse, delta):
    t = q.shape[0]
    nq, nk = t // _TQ, t // _TK

    def body(q_ref, k_ref, v_ref, do_ref, l_ref, d_ref, dq_ref, dk_ref, dv_ref, dk_acc, dv_acc):
        kj, qi = pl.program_id(1), pl.program_id(2)

        @pl.when((kj == 0) & (qi == 0))
        def _():
            dq_ref[...] = jnp.zeros(dq_ref.shape, F32)

        @pl.when(qi == 0)
        def _():
            dk_acc[...] = jnp.zeros(dk_acc.shape, F32)
            dv_acc[...] = jnp.zeros(dv_acc.shape, F32)

        @pl.when(qi >= kj)
        def _():
            qv, kv = q_ref[...], k_ref[...]
            s = _causal_scores(qv, kv, qi, kj)
            p = jnp.exp(s - l_ref[0])
            do = do_ref[...].astype(BF16)
            dv_acc[...] += _dot_tn(p.astype(BF16), do)
            dp = _dot_nt(do, v_ref[...])
            ds = (p * (dp - d_ref[0]) * _MLA_SCALE).astype(BF16)
            dk_acc[...] += _dot_tn(ds, qv)
            rows = pl.ds(pl.multiple_of(qi * _TQ, _TQ), _TQ)
            dq_ref[rows, :] += _dot(ds, kv)

        @pl.when(qi == nq - 1)
        def _():
            dk_ref[...] = dk_acc[...]
            dv_ref[...] = dv_acc[...]

    q_map = lambda h, kj, qi: (jnp.maximum(qi, kj), h)
    kv_map = lambda h, kj, qi: (kj, h)
    stat_map = lambda h, kj, qi: (h, jnp.maximum(qi, kj), 0)
    return pl.pallas_call(
        body, name="mla_bwd", grid=(N_HEADS_B, nk, nq),
        in_specs=[pl.BlockSpec((_TQ, HEAD_PAD), q_map), pl.BlockSpec((_TK, HEAD_PAD), kv_map),
                  pl.BlockSpec((_TK, HEAD_PAD), kv_map), pl.BlockSpec((_TQ, HEAD_PAD), q_map),
                  pl.BlockSpec((1, _TQ, 1), stat_map), pl.BlockSpec((1, _TQ, 1), stat_map)],
        out_specs=[pl.BlockSpec((t, HEAD_PAD), lambda h, kj, qi: (0, h)),
                   pl.BlockSpec((_TK, HEAD_PAD), kv_map), pl.BlockSpec((_TK, HEAD_PAD), kv_map)],
        out_shape=[jax.ShapeDtypeStruct((t, MLA_W), F32)] * 3,
        scratch_shapes=[pltpu.VMEM((_TK, HEAD_PAD), F32), pltpu.VMEM((_TK, HEAD_PAD), F32)],
        compiler_params=_params(("parallel", "arbitrary", "arbitrary")),
    )(q, k, v, d_out, lse, delta)


def _mla_prep_bwd(dq, dk, dv, proj, posc, freq, qan, kvan, wq, wk, wv):
    t = dq.shape[0]
    tm = 256

    def body(dq_ref, dk_ref, dv_ref, cq_ref, ckv_ref, pos_ref, f_ref, qan_ref, kvan_ref, wq_ref, wk_ref, wv_ref,
             dcq_ref, dckv_ref, dkr_ref, dwq_ref, dwk_ref, dwv_ref, dqan_ref, dkvan_ref):
        @pl.when(pl.program_id(0) == 0)
        def _():
            for r in (dwq_ref, dwk_ref, dwv_ref, dqan_ref, dkvan_ref):
                r[...] = jnp.zeros(r.shape, F32)

        cq = cq_ref[...]
        rq = _rms(cq)
        nq_ = cq * rq
        cqn = (nq_ * qan_ref[...]).astype(BF16)
        ckv = ckv_ref[...]
        rkv = _rms(ckv)
        nkv = ckv * rkv
        ckvn = (nkv * kvan_ref[...]).astype(BF16)
        c, s, lo, hi = _rope_coeffs(pos_ref[...], f_ref[...])
        dkv = dk_ref[...]
        dkr = jnp.zeros((tm, LANES), F32)
        dqb = []
        for h in range(N_HEADS_B):
            sl = slice(HEAD_PAD * h, HEAD_PAD * (h + 1))
            dqb.append(_unrope(dq_ref[:, sl], c, s, lo, hi).astype(BF16))
            dkr = dkr + dkv[:, sl]
        dqb = jnp.concatenate(dqb, axis=1)
        dkr_ref[...] = jnp.where(lo | hi, _unrope(dkr, c, s, lo, hi), 0.0).astype(BF16)
        dkb = dkv.astype(BF16)
        dvb = dv_ref[...].astype(BF16)
        dwq_ref[...] += _dot_tn(cqn, dqb)
        dwk_ref[...] += _dot_tn(ckvn, dkb)
        dwv_ref[...] += _dot_tn(ckvn, dvb)
        dcqn = _dot_nt(dqb, wq_ref[...])
        dckvn = _dot_nt(dkb, wk_ref[...]) + _dot_nt(dvb, wv_ref[...])
        dcq, dqan = _norm_bwd(dcqn, nq_, rq, qan_ref[...])
        dckv, dkvan = _norm_bwd(dckvn, nkv, rkv, kvan_ref[...])
        dcq_ref[...] = dcq.astype(BF16)
        dckv_ref[...] = dckv.astype(BF16)
        dqan_ref[...] += dqan
        dkvan_ref[...] += dkvan

    row = lambda i: (i, 0)
    blk = pl.BlockSpec((tm, MLA_W), row)
    return pl.pallas_call(
        body, name="mla_prep_bwd", grid=(t // tm,),
        in_specs=[blk, blk, blk, pl.BlockSpec((tm, Q_LORA), lambda i: (i, _CQ_BLK)),
                  pl.BlockSpec((tm, LANES), lambda i: (i, _CKV_BLK)),
                  pl.BlockSpec((tm, 1), row), _full((1, LANES)), _full((1, Q_LORA)), _full((1, KV_LORA)),
                  _full((Q_LORA, MLA_W)), _full((KV_LORA, MLA_W)), _full((KV_LORA, MLA_W))],
        out_specs=[pl.BlockSpec((tm, Q_LORA), row), pl.BlockSpec((tm, LANES), row), pl.BlockSpec((tm, LANES), row),
                   _full((Q_LORA, MLA_W)), _full((KV_LORA, MLA_W)), _full((KV_LORA, MLA_W)),
                   _full((1, Q_LORA)), _full((1, KV_LORA))],
        out_shape=[jax.ShapeDtypeStruct((t, Q_LORA), BF16), jax.ShapeDtypeStruct((t, LANES), BF16),
                   jax.ShapeDtypeStruct((t, LANES), BF16),
                   jax.ShapeDtypeStruct((Q_LORA, MLA_W), F32), jax.ShapeDtypeStruct((KV_LORA, MLA_W), F32),
                   jax.ShapeDtypeStruct((KV_LORA, MLA_W), F32),
                   jax.ShapeDtypeStruct((1, Q_LORA), F32), jax.ShapeDtypeStruct((1, KV_LORA), F32)],
        compiler_params=_params(("arbitrary",)),
    )(dq, dk, dv, proj, proj, posc, freq, qan, kvan, wq, wk, wv)


def _swa_bwd(proj, d_out, lse, delta, posc, posr, sinks):
    t = proj.shape[0]
    nb = t // BLOCK
    group = N_HEADS_A // N_KV_A

    def body(q_ref, kc_ref, kp_ref, vc_ref, vp_ref, do_ref, l_ref, d_ref, pq_ref, pc_ref, pp_ref, sink_ref,
             dq_ref, dk_ref, dv_ref, ds_ref, dkb_s, dvb_s, dk_carry, dv_carry):
        n = pl.program_id(0)

        @pl.when(n == 0)
        def _():
            ds_ref[...] = jnp.zeros(ds_ref.shape, F32)
            dk_carry[...] = jnp.zeros(dk_carry.shape, F32)
            dv_carry[...] = jnp.zeros(dv_carry.shape, F32)

        @pl.when(n < nb)
        def _():
            kb, vb, dist, valid = _swa_band(n, kp_ref, kc_ref, vp_ref, vc_ref, pq_ref, pp_ref, pc_ref)
            lse_v, del_v = l_ref[...], d_ref[...]
            lane = lax.broadcasted_iota(jnp.int32, (1, LANES), 1)
            dsink = jnp.zeros((1, LANES), F32)
            for kh in range(N_KV_A):
                dk_h = jnp.zeros((2 * BLOCK, HEAD_DIM_A), F32)
                dv_h = jnp.zeros((2 * BLOCK, HEAD_DIM_A), F32)
                for h in range(group * kh, group * (kh + 1)):
                    q_h, k_h, s = _swa_scores(q_ref, kb, dist, valid, h)
                    l_h, d_h = lse_v[:, h:h + 1], del_v[:, h:h + 1]
                    p = jnp.exp(s - l_h)
                    p_sink = jnp.exp(sink_ref[0:1, h:h + 1] - l_h)
                    dsink = jnp.where(lane == h, jnp.sum(-p_sink * d_h, axis=0, keepdims=True), dsink)
                    do_h = do_ref[:, HEAD_DIM_A * h:HEAD_DIM_A * (h + 1)].astype(BF16)
                    dp = _dot_nt(do_h, vb[:, HEAD_DIM_A * kh:HEAD_DIM_A * (kh + 1)])
                    ds = (p * (dp - d_h) * (HEAD_DIM_A ** -0.5)).astype(BF16)
                    dq_ref[:, HEAD_DIM_A * h:HEAD_DIM_A * (h + 1)] = _dot(ds, k_h)
                    dk_h = dk_h + _dot_tn(ds, q_h)
                    dv_h = dv_h + _dot_tn(p.astype(BF16), do_h)
                dkb_s[:, HEAD_DIM_A * kh:HEAD_DIM_A * (kh + 1)] = dk_h
                dvb_s[:, HEAD_DIM_A * kh:HEAD_DIM_A * (kh + 1)] = dv_h
            ds_ref[...] += dsink
            dk_ref[...] = dk_carry[...] + dkb_s[0:BLOCK, :]
            dv_ref[...] = dv_carry[...] + dvb_s[0:BLOCK, :]
            dk_carry[...] = dkb_s[BLOCK:2 * BLOCK, :]
            dv_carry[...] = dvb_s[BLOCK:2 * BLOCK, :]

        @pl.when(n == nb)
        def _():
            dk_ref[...] = dk_carry[...]
            dv_ref[...] = dv_carry[...]

    cur = lambda n: (jnp.minimum(n, nb - 1), 0)
    prv = lambda n: jnp.maximum(jnp.minimum(n, nb - 1) - 1, 0)
    out_prev = lambda n: (jnp.maximum(n - 1, 0), 0)
    return pl.pallas_call(
        body, name="swa_bwd", grid=(nb + 1,),
        in_specs=[pl.BlockSpec((BLOCK, WIDTH_A), lambda n: (jnp.minimum(n, nb - 1), _QA_BLK)),
                  pl.BlockSpec((BLOCK, LANES), lambda n: (jnp.minimum(n, nb - 1), _KA_BLK)),
                  pl.BlockSpec((BLOCK, LANES), lambda n: (prv(n), _KA_BLK)),
                  pl.BlockSpec((BLOCK, LANES), lambda n: (jnp.minimum(n, nb - 1), _VA_BLK)),
                  pl.BlockSpec((BLOCK, LANES), lambda n: (prv(n), _VA_BLK)),
                  pl.BlockSpec((BLOCK, WIDTH_A), cur), pl.BlockSpec((BLOCK, LANES), cur), pl.BlockSpec((BLOCK, LANES), cur),
                  pl.BlockSpec((BLOCK, 1), cur),
                  pl.BlockSpec((1, BLOCK), lambda n: (0, jnp.minimum(n, nb - 1))),
                  pl.BlockSpec((1, BLOCK), lambda n: (0, prv(n))),
                  _full((1, N_HEADS_A))],
        out_specs=[pl.BlockSpec((BLOCK, WIDTH_A), cur), pl.BlockSpec((BLOCK, LANES), out_prev),
                   pl.BlockSpec((BLOCK, LANES), out_prev), _full((1, LANES))],
        out_shape=[jax.ShapeDtypeStruct((t, WIDTH_A), F32), jax.ShapeDtypeStruct((t, LANES), F32),
                   jax.ShapeDtypeStruct((t, LANES), F32), jax.ShapeDtypeStruct((1, LANES), F32)],
        scratch_shapes=[pltpu.VMEM((2 * BLOCK, LANES), F32), pltpu.VMEM((2 * BLOCK, LANES), F32),
                        pltpu.VMEM((BLOCK, LANES), F32), pltpu.VMEM((BLOCK, LANES), F32)],
        compiler_params=_params(("arbitrary",)),
    )(proj, proj, proj, proj, proj, d_out, lse, delta, posc, posr, posr, sinks)


def _in_bwd(dproj, w_in_p, x, dx1, g1):
    t = x.shape[0]
    tm = 256

    def body(dp_ref, w_ref, x_ref, dx1_ref, g_ref, dx_ref, dg_ref):
        @pl.when(pl.program_id(0) == 0)
        def _():
            dg_ref[...] = jnp.zeros(dg_ref.shape, F32)

        dh = _dot_nt(dp_ref[...], w_ref[...])
        xv = x_ref[...]
        r = _rms(xv)
        dx, dg = _norm_bwd(dh, xv * r, r, g_ref[...])
        dx_ref[...] = dx1_ref[...] + dx
        dg_ref[...] += dg

    row = lambda i: (i, 0)
    blk = pl.BlockSpec((tm, D_MODEL), row)
    return pl.pallas_call(
        body, name="in_bwd", grid=(t // tm,),
        in_specs=[pl.BlockSpec((tm, D_IN_PAD), row), _full((D_MODEL, D_IN_PAD)), blk, blk, _full((1, D_MODEL))],
        out_specs=[blk, _full((1, D_MODEL))],
        out_shape=[jax.ShapeDtypeStruct((t, D_MODEL), F32), jax.ShapeDtypeStruct((1, D_MODEL), F32)],
        compiler_params=_params(("arbitrary",)),
    )(dproj, w_in_p, x, dx1, g1)


def _adamw(w, g_parts, m, v, name):
    r = w.shape[0]
    tr = min(r, 128)
    ng = len(g_parts)

    def body(*refs):
        w_ref, g_refs, m_ref, v_ref = refs[0], refs[1:1 + ng], refs[1 + ng], refs[2 + ng]
        g_out, d_out, m_out, v_out = refs[3 + ng:]
        g = g_refs[0][...]
        for gr in g_refs[1:]:
            g = g + gr[...]
        m_new = ADAM_B1 * m_ref[...] + (1.0 - ADAM_B1) * g
        v_new = ADAM_B2 * v_ref[...] + (1.0 - ADAM_B2) * jnp.square(g)
        m_hat = m_new / (1.0 - ADAM_B1 ** ADAM_STEP)
        v_hat = v_new / (1.0 - ADAM_B2 ** ADAM_STEP)
        g_out[...] = g
        d_out[...] = -ADAM_LR * (m_hat / (jnp.sqrt(v_hat) + ADAM_EPS) + ADAM_WD * w_ref[...])
        m_out[...] = m_new
        v_out[...] = v_new

    blk = pl.BlockSpec((tr, D_MODEL), lambda i: (i, 0))
    return pl.pallas_call(
        body, name=name, grid=(r // tr,),
        in_specs=[blk] * (3 + ng), out_specs=[blk] * 4,
        out_shape=[jax.ShapeDtypeStruct((r, D_MODEL), F32)] * 4,
        compiler_params=_params(("parallel",)),
    )(w, *g_parts, m, v)


_HBM = pl.BlockSpec(memory_space=pltpu.HBM)


def _other_chips(x, y):
    return ((1 - x, y), (x, 1 - y), (1 - x, 1 - y))


def _all_gather_chips(packed):
    r, w = packed.shape

    def body(src, out, send_sems, recv_sems, local_sem):
        x, y, c = lax.axis_index("x"), lax.axis_index("y"), lax.axis_index("c")
        me = 2 * x + y
        local = pltpu.make_async_copy(src, out.at[me], local_sem)
        local.start()
        sends = []
        for j, (px, py) in enumerate(_other_chips(x, y)):
            cp = pltpu.make_async_remote_copy(src_ref=src, dst_ref=out.at[me], send_sem=send_sems.at[j],
                                              recv_sem=recv_sems.at[j], device_id=(px, py, c), device_id_type=MESH)
            cp.start()
            sends.append(cp)
        for j, (px, py) in enumerate(_other_chips(x, y)):
            pltpu.make_async_remote_copy(src_ref=src, dst_ref=out.at[2 * px + py], send_sem=send_sems.at[j],
                                         recv_sem=recv_sems.at[j], device_id=(px, py, c), device_id_type=MESH).wait_recv()
        for cp in sends:
            cp.wait_send()
        local.wait()

    return pl.pallas_call(
        body, name="ag_weights", in_specs=[_HBM], out_specs=_HBM,
        out_shape=jax.ShapeDtypeStruct((N_CHIPS, r, w), packed.dtype),
        scratch_shapes=[pltpu.SemaphoreType.DMA((3,)), pltpu.SemaphoreType.DMA((3,)), pltpu.SemaphoreType.DMA(())],
    )(packed)


def _scatter_chips(gp):
    _, r, w = gp.shape

    def body(src, land, send_sems, recv_sems):
        x, y, c = lax.axis_index("x"), lax.axis_index("y"), lax.axis_index("c")
        sends = []
        for j, (px, py) in enumerate(_other_chips(x, y)):
            cp = pltpu.make_async_remote_copy(src_ref=src.at[2 * px + py], dst_ref=land.at[j], send_sem=send_sems.at[j],
                                              recv_sem=recv_sems.at[j], device_id=(px, py, c), device_id_type=MESH)
            cp.start()
            sends.append(cp)
        for cp in sends:
            cp.wait_recv()
        for cp in sends:
            cp.wait_send()

    return pl.pallas_call(
        body, name="rs_scatter", in_specs=[_HBM], out_specs=_HBM,
        out_shape=jax.ShapeDtypeStruct((3, r, w), gp.dtype),
        scratch_shapes=[pltpu.SemaphoreType.DMA((3,)), pltpu.SemaphoreType.DMA((3,))],
    )(gp)


def _sum4(own, land):
    r, w = own.shape
    tr = 128

    def body(o_ref, l_ref, s_ref):
        s_ref[...] = ((o_ref[...] + l_ref[0]) + l_ref[1]) + l_ref[2]

    return pl.pallas_call(
        body, name="rs_sum", grid=(r // tr,),
        in_specs=[pl.BlockSpec((tr, w), lambda i: (i, 0)), pl.BlockSpec((3, tr, w), lambda i: (0, i, 0))],
        out_specs=pl.BlockSpec((tr, w), lambda i: (i, 0)),
        out_shape=jax.ShapeDtypeStruct((r, w), F32),
        compiler_params=_params(("parallel",)),
    )(own, land)


def _swap_sibling(s):
    def body(src, got, send_sem, recv_sem):
        x, y, c = lax.axis_index("x"), lax.axis_index("y"), lax.axis_index("c")
        cp = pltpu.make_async_remote_copy(src_ref=src, dst_ref=got, send_sem=send_sem, recv_sem=recv_sem,
                                          device_id=(x, y, 1 - c), device_id_type=MESH)
        cp.start()
        cp.wait_recv()
        cp.wait_send()

    return pl.pallas_call(
        body, name="rs_swap", in_specs=[_HBM], out_specs=_HBM,
        out_shape=jax.ShapeDtypeStruct(s.shape, s.dtype),
        scratch_shapes=[pltpu.SemaphoreType.DMA(()), pltpu.SemaphoreType.DMA(())],
    )(s)


def _all_reduce_small(part):
    n_dev = 8

    def body(src, out, gath, send_sems, recv_sems):
        x, y, c = lax.axis_index("x"), lax.axis_index("y"), lax.axis_index("c")
        me = 4 * x + 2 * y + c
        gath[me] = src[...]
        peers = []
        for k in range(1, n_dev):
            px = 1 - x if (k >> 2) & 1 else x
            py = 1 - y if (k >> 1) & 1 else y
            pc = 1 - c if k & 1 else c
            peers.append((px, py, pc))
        sends = []
        for j, peer in enumerate(peers):
            cp = pltpu.make_async_remote_copy(src_ref=src, dst_ref=gath.at[me], send_sem=send_sems.at[j],
                                              recv_sem=recv_sems.at[j], device_id=peer, device_id_type=MESH)
            cp.start()
            sends.append(cp)
        for j, (px, py, pc) in enumerate(peers):
            pltpu.make_async_remote_copy(src_ref=src, dst_ref=gath.at[4 * px + 2 * py + pc], send_sem=send_sems.at[j],
                                         recv_sem=recv_sems.at[j], device_id=(px, py, pc), device_id_type=MESH).wait_recv()
        for cp in sends:
            cp.wait_send()
        acc = gath[0]
        for d in range(1, n_dev):
            acc = acc + gath[d]
        out[...] = acc

    vmem = pl.BlockSpec(memory_space=pltpu.VMEM)
    return pl.pallas_call(
        body, name="ar_small", in_specs=[vmem], out_specs=vmem,
        out_shape=jax.ShapeDtypeStruct(part.shape, F32),
        scratch_shapes=[pltpu.VMEM((n_dev,) + part.shape, F32), pltpu.SemaphoreType.DMA((n_dev - 1,)),
                        pltpu.SemaphoreType.DMA((n_dev - 1,))],
    )(part)


def _pack(shards, dtype):
    parts = [shards[n].reshape(PACK_ROWS[n], D_MODEL).astype(dtype) for n in BIG]
    pad = PACK_TOTAL - sum(PACK_ROWS.values())
    return jnp.concatenate(parts + [jnp.zeros((pad, D_MODEL), dtype)], axis=0)


def _unpack(packed):
    out, off = {}, 0
    for n in BIG:
        out[n] = packed[off:off + PACK_ROWS[n]].reshape(SHARD_SHAPES[n])
        off += PACK_ROWS[n]
    return out


def _full_weights(gathered):
    out, off = {}, 0
    for n in BIG:
        r, c = SHARD_SHAPES[n]
        g = gathered[:, off:off + PACK_ROWS[n]].reshape(N_CHIPS, r, c)
        off += PACK_ROWS[n]
        out[n] = jnp.transpose(g, (1, 0, 2)).reshape(r, N_CHIPS * c) if n in COL_SHARDED else g.reshape(N_CHIPS * r, c)
    return out


def _shard_major(full):
    parts = []
    for n in BIG:
        r, c = SHARD_SHAPES[n]
        g = full[n]
        g = jnp.transpose(g.reshape(r, N_CHIPS, c), (1, 0, 2)) if n in COL_SHARDED else g.reshape(N_CHIPS, r, c)
        parts.append(g.reshape(N_CHIPS, PACK_ROWS[n], D_MODEL))
    pad = PACK_TOTAL - sum(PACK_ROWS.values())
    return jnp.concatenate(parts + [jnp.zeros((N_CHIPS, pad, D_MODEL), F32)], axis=1)


def _pad_layouts(w):
    dt = w["w_in"].dtype
    w_in = w["w_in"]
    z = lambda r, c: jnp.zeros((r, c), dt)
    w_in_p = jnp.concatenate([w_in[:, :3200], z(D_MODEL, 64), w_in[:, 3200:], z(D_MODEL, 32)], axis=1)
    wq = w["w_q_b"].reshape(Q_LORA, N_HEADS_B, Q_HEAD_B)
    wq_p = jnp.concatenate([wq, jnp.zeros((Q_LORA, N_HEADS_B, HEAD_PAD - Q_HEAD_B), dt)], axis=2).reshape(Q_LORA, MLA_W)
    wkv = w["w_kv_b"].reshape(KV_LORA, N_HEADS_B, QK_NOPE + V_DIM_B)
    zk = jnp.zeros((KV_LORA, N_HEADS_B, HEAD_PAD - QK_NOPE), dt)
    wk_p = jnp.concatenate([wkv[:, :, :QK_NOPE], zk], axis=2).reshape(KV_LORA, MLA_W)
    wv_p = jnp.concatenate([wkv[:, :, QK_NOPE:], zk], axis=2).reshape(KV_LORA, MLA_W)
    wob = w["w_o_b"].reshape(N_HEADS_B, V_DIM_B, D_MODEL)
    wob_p = jnp.concatenate([wob, jnp.zeros((N_HEADS_B, HEAD_PAD - V_DIM_B, D_MODEL), dt)], axis=1).reshape(MLA_W, D_MODEL)
    return dict(w_in=w_in_p, wq=wq_p, wk=wk_p, wv=wv_p, w_o_a=w["w_o_a"], w_o_b=wob_p, w_out=w["w_out"],
                w_up=w["w_up"], w_down=w["w_down"])


def _unpad_grads(d):
    dw_in = jnp.concatenate([d["w_in"][:, :3200], d["w_in"][:, 3264:3296]], axis=1)
    dwq = d["wq"].reshape(Q_LORA, N_HEADS_B, HEAD_PAD)[:, :, :Q_HEAD_B].reshape(Q_LORA, N_HEADS_B * Q_HEAD_B)
    dwk = d["wk"].reshape(KV_LORA, N_HEADS_B, HEAD_PAD)[:, :, :QK_NOPE]
    dwv = d["wv"].reshape(KV_LORA, N_HEADS_B, HEAD_PAD)[:, :, :V_DIM_B]
    dwkv = jnp.concatenate([dwk, dwv], axis=2).reshape(KV_LORA, N_HEADS_B * (QK_NOPE + V_DIM_B))
    dwob = d["w_o_b"].reshape(N_HEADS_B, HEAD_PAD, D_MODEL)[:, :V_DIM_B].reshape(N_HEADS_B * V_DIM_B, D_MODEL)
    return dict(w_in=dw_in, w_q_b=dwq, w_kv_b=dwkv, w_o_a=d["w_o_a"], w_o_b=dwob, w_out=d["w_out"],
                w_up=d["w_up"], w_down=d["w_down"])


def _rope_freq_lanes():
    freqs = ROPE_THETA ** (-jnp.arange(0, QK_ROPE, 2, dtype=F32) / QK_ROPE)
    return jnp.concatenate([jnp.zeros((QK_NOPE,), F32), freqs, freqs,
                            jnp.zeros((HEAD_PAD - Q_HEAD_B,), F32)]).reshape(1, LANES)


def _local_step(x, positions, target, small, wfull):
    t = x.shape[0]
    wp = _pad_layouts(wfull)
    posr = positions.astype(F32).reshape(1, t)
    posc = posr.reshape(t, 1)
    freq = _rope_freq_lanes()
    g1, g2, g3, g4 = small["pre_norm_mix"], small["post_norm_mix"], small["pre_norm_mlp"], small["post_norm_mlp"]
    qan, kvan, sinks = small["q_a_norm"], small["kv_a_norm"], small["sinks"]

    h, proj = _proj_fwd(x, g1, wp["w_in"])
    out_a, lse_a = _swa_fwd(proj, posc, posr, sinks)
    qm, km, vm = _mla_prep_fwd(proj, posc, freq, qan, kvan, wp["wq"], wp["wk"], wp["wv"])
    out_b, lse_b = _mla_fwd(qm, km, vm)
    merged, y, x1, h2 = _mix_out_fwd(out_a, out_b, proj, x, wp["w_o_a"], wp["w_o_b"], wp["w_out"], g2, g3)
    u, a = _up_fwd(h2, wp["w_up"])
    dx2, dyd, dg4, loss = _down_fwd_loss(a, wp["w_down"], x1, target, g4)

    dw = {}
    dw["w_down"] = _matmul_tn(a, dyd, "dw_down", 512, 1024)
    du = _down_bwd(dyd, wp["w_down"], u)
    dw["w_up"] = _matmul_tn(h2, du, "dw_up", 512, 1024)
    dx1, dy, dg3, dg2 = _up_bwd(du, wp["w_up"], x1, dx2, y, g3, g2)
    dw["w_out"] = _matmul_tn(merged, dy, "dw_out", 512, 1024)
    doa, dob, dga, dgb, d_out_a, d_out_b, del_a, del_b = _mix_out_bwd(
        dy, out_a, out_b, proj, wp["w_o_a"], wp["w_o_b"], wp["w_out"])
    dw["w_o_a"] = _matmul_tn(out_a, doa, "dw_o_a", 512, 1024)
    dw["w_o_b"] = _matmul_tn(out_b, dob, "dw_o_b", 512, 1024)
    dqm, dkm, dvm = _mla_bwd(qm, km, vm, d_out_b, lse_b, del_b)
    dcq, dckv, dkr, dwq, dwk, dwv, dqan, dkvan = _mla_prep_bwd(
        dqm, dkm, dvm, proj, posc, freq, qan, kvan, wp["wq"], wp["wk"], wp["wv"])
    dw["wq"], dw["wk"], dw["wv"] = dwq, dwk, dwv
    dqa, dka, dva, dsinks = _swa_bwd(proj, d_out_a, lse_a, del_a, posc, posr, sinks)
    dproj = jnp.concatenate([dga, dgb, dqa.astype(BF16), dka.astype(BF16), dva.astype(BF16), dcq, dckv, dkr], axis=1)
    dw["w_in"] = _matmul_tn(h, dproj, "dw_in", 512, D_IN_PAD // 2)
    grad_x, dg1 = _in_bwd(dproj, wp["w_in"], x, dx1, g1)

    dsmall = dict(pre_norm_mix=dg1, post_norm_mix=dg2, pre_norm_mlp=dg3, post_norm_mlp=dg4,
                  q_a_norm=dqan, kv_a_norm=dkvan, sinks=dsinks[:, :N_HEADS_A])
    return loss, grad_x, _unpad_grads(dw), dsmall


def _pack_small(p, extra=None):
    tail = jnp.concatenate([p["q_a_norm"], p["kv_a_norm"], p["sinks"],
                            jnp.zeros((1, D_MODEL - Q_LORA - KV_LORA - N_HEADS_A), F32)], axis=1)
    scalar = jnp.zeros((1, D_MODEL), F32)
    if extra is not None:
        scalar = scalar.at[0, 0].set(extra)
    return jnp.concatenate([p["pre_norm_mix"], p["post_norm_mix"], p["pre_norm_mlp"], p["post_norm_mlp"], tail, scalar,
                            jnp.zeros((2, D_MODEL), F32)], axis=0)


def _unpack_small(b):
    return dict(pre_norm_mix=b[0:1], post_norm_mix=b[1:2], pre_norm_mlp=b[2:3], post_norm_mlp=b[3:4],
                q_a_norm=b[4:5, :Q_LORA], kv_a_norm=b[4:5, Q_LORA:Q_LORA + KV_LORA],
                sinks=b[4:5, Q_LORA + KV_LORA:Q_LORA + KV_LORA + N_HEADS_A])


def kernel(x, positions, pre_norm_mix, w_in, q_a_norm, w_q_b, kv_a_norm, w_kv_b, sinks, w_o_a, w_o_b, w_out, post_norm_mix, pre_norm_mlp, w_up, w_down, post_norm_mlp, loss_target, m_pre_norm_mix, m_w_in, m_q_a_norm, m_w_q_b, m_kv_a_norm, m_w_kv_b, m_sinks, m_w_o_a, m_w_o_b, m_w_out, m_post_norm_mix, m_pre_norm_mlp, m_w_up, m_w_down, m_post_norm_mlp, v_pre_norm_mix, v_w_in, v_q_a_norm, v_w_q_b, v_kv_a_norm, v_w_kv_b, v_sinks, v_w_o_a, v_w_o_b, v_w_out, v_post_norm_mix, v_pre_norm_mlp, v_w_up, v_w_down, v_post_norm_mlp):
    w = dict(pre_norm_mix=pre_norm_mix, w_in=w_in[0], q_a_norm=q_a_norm, w_q_b=w_q_b[0], kv_a_norm=kv_a_norm,
             w_kv_b=w_kv_b[0], sinks=sinks, w_o_a=w_o_a[0], w_o_b=w_o_b[0], w_out=w_out[0],
             post_norm_mix=post_norm_mix, pre_norm_mlp=pre_norm_mlp, w_up=w_up[0], w_down=w_down[0],
             post_norm_mlp=post_norm_mlp)
    m = dict(pre_norm_mix=m_pre_norm_mix, w_in=m_w_in[0], q_a_norm=m_q_a_norm, w_q_b=m_w_q_b[0],
             kv_a_norm=m_kv_a_norm, w_kv_b=m_w_kv_b[0], sinks=m_sinks, w_o_a=m_w_o_a[0], w_o_b=m_w_o_b[0],
             w_out=m_w_out[0], post_norm_mix=m_post_norm_mix, pre_norm_mlp=m_pre_norm_mlp, w_up=m_w_up[0],
             w_down=m_w_down[0], post_norm_mlp=m_post_norm_mlp)
    v = dict(pre_norm_mix=v_pre_norm_mix, w_in=v_w_in[0], q_a_norm=v_q_a_norm, w_q_b=v_w_q_b[0],
             kv_a_norm=v_kv_a_norm, w_kv_b=v_w_kv_b[0], sinks=v_sinks, w_o_a=v_w_o_a[0], w_o_b=v_w_o_b[0],
             w_out=v_w_out[0], post_norm_mix=v_post_norm_mix, pre_norm_mlp=v_pre_norm_mlp, w_up=v_w_up[0],
             w_down=v_w_down[0], post_norm_mlp=v_post_norm_mlp)

    gathered = _all_gather_chips(_pack(w, BF16))
    loss, grad_x, dw, dsmall = _local_step(x[0], positions, loss_target[0], w, _full_weights(gathered))

    gp = _shard_major(dw)
    land = _scatter_chips(gp)
    chip = 2 * lax.axis_index("x") + lax.axis_index("y")
    own = lax.dynamic_index_in_dim(gp, chip, axis=0, keepdims=False)
    part = _sum4(own, land)
    g_big, d_big, m_big, v_big = _adamw(_pack(w, F32), [part, _swap_sibling(part)], _pack(m, F32), _pack(v, F32), "adamw_big")

    red = _all_reduce_small(_pack_small(dsmall, loss[0, 0]))
    g_sm, d_sm, m_sm, v_sm = _adamw(_pack_small(w), [red], _pack_small(m), _pack_small(v), "adamw_small")

    outs = []
    for big, sm in ((g_big, g_sm), (d_big, d_sm), (m_big, m_sm), (v_big, v_sm)):
        b, s = _unpack(big), _unpack_small(sm)
        outs.append([b[n][None] if n in b else s[n] for n in WEIGHTS])
    return (red[5, 0], grad_x[None], *outs[0], *outs[1], *outs[2], *outs[3])
```

```python
import functools

import jax
import jax.numpy as jnp
from jax import lax
from jax.experimental import pallas as pl
from jax.experimental.pallas import tpu as pltpu

F32 = jnp.float32
BF16 = jnp.bfloat16
MESH = pl.DeviceIdType.MESH

D_MODEL = 1024
N_HEADS_A = 8
N_KV_A = 2
HEAD_DIM_A = 64
WINDOW = 128
BLOCK = 128
N_HEADS_B = 8
QK_NOPE = 64
QK_ROPE = 32
V_DIM_B = 64
Q_LORA = 256
KV_LORA = 128
ROPE_THETA = 10000.0
D_FF = 4 * D_MODEL
EPS = 1e-6
WIDTH_A = N_HEADS_A * HEAD_DIM_A
Q_HEAD_B = QK_NOPE + QK_ROPE
D_IN = 3232
D_IN_PAD = 3328
HEAD_PAD = 128
MLA_W = N_HEADS_B * HEAD_PAD

ADAM_LR = 0.001
ADAM_B1 = 0.9
ADAM_B2 = 0.999
ADAM_EPS = 1e-08
ADAM_WD = 0.01
ADAM_STEP = 10

NEG = -1e30
N_CHIPS = 4
LANES = 128
VMEM_LIMIT = 56 * 1024 * 1024

BIG = ("w_in", "w_q_b", "w_kv_b", "w_o_a", "w_o_b", "w_out", "w_up", "w_down")
COL_SHARDED = ("w_in", "w_q_b", "w_kv_b", "w_o_a", "w_o_b", "w_up")
SHARD_SHAPES = {"w_in": (1024, 808), "w_q_b": (256, 192), "w_kv_b": (128, 256), "w_o_a": (512, 256),
                "w_o_b": (512, 256), "w_out": (256, 1024), "w_up": (1024, 1024), "w_down": (1024, 1024)}
PACK_ROWS = {n: (s[0] * s[1]) // D_MODEL for n, s in SHARD_SHAPES.items()}
PACK_TOTAL = 3456
SMALL = ("pre_norm_mix", "post_norm_mix", "pre_norm_mlp", "post_norm_mlp", "q_a_norm", "kv_a_norm", "sinks")
WEIGHTS = ("pre_norm_mix", "w_in", "q_a_norm", "w_q_b", "kv_a_norm", "w_kv_b", "sinks", "w_o_a", "w_o_b", "w_out",
           "post_norm_mix", "pre_norm_mlp", "w_up", "w_down", "post_norm_mlp")


def _params(sem=None):
    return pltpu.CompilerParams(dimension_semantics=sem, vmem_limit_bytes=VMEM_LIMIT)


def _dot(a, b):
    return jnp.dot(a, b, preferred_element_type=F32)


def _dot_nt(a, b):
    return lax.dot_general(a, b, (((1,), (1,)), ((), ())), preferred_element_type=F32)


def _dot_tn(a, b):
    return lax.dot_general(a, b, (((0,), (0,)), ((), ())), preferred_element_type=F32)


def _rms(v):
    return lax.rsqrt(jnp.mean(v * v, axis=-1, keepdims=True) + EPS)


def _norm_bwd(dout, n, r, g):
    dn = dout * g
    dx = r * (dn - n * jnp.mean(dn * n, axis=-1, keepdims=True))
    return dx, jnp.sum(dout * n, axis=0, keepdims=True)


def _full(shape):
    return pl.BlockSpec(shape, lambda *_: (0,) * len(shape))


def _proj_fwd(x, g1, w_in_p):
    t = x.shape[0]
    tm = 256

    def body(x_ref, g_ref, w_ref, h_ref, p_ref):
        xv = x_ref[...]
        h = ((xv * _rms(xv)) * g_ref[...]).astype(BF16)
        h_ref[...] = h
        p_ref[...] = _dot(h, w_ref[...])

    return pl.pallas_call(
        body, name="proj_fwd", grid=(t // tm,),
        in_specs=[pl.BlockSpec((tm, D_MODEL), lambda i: (i, 0)), _full((1, D_MODEL)), _full((D_MODEL, D_IN_PAD))],
        out_specs=[pl.BlockSpec((tm, D_MODEL), lambda i: (i, 0)), pl.BlockSpec((tm, D_IN_PAD), lambda i: (i, 0))],
        out_shape=[jax.ShapeDtypeStruct((t, D_MODEL), BF16), jax.ShapeDtypeStruct((t, D_IN_PAD), F32)],
        compiler_params=_params(("parallel",)),
    )(x, g1, w_in_p)


_QA_BLK = 2048 // WIDTH_A
_KA_BLK = 2560 // LANES
_VA_BLK = 2688 // LANES
_CQ_BLK = 2816 // Q_LORA
_CKV_BLK = 3072 // LANES
_KR_BLK = 3200 // LANES


def _swa_scores(q_ref, kb, dist, valid, h):
    kh = h // (N_HEADS_A // N_KV_A)
    q_h = q_ref[:, HEAD_DIM_A * h:HEAD_DIM_A * (h + 1)].astype(BF16)
    k_h = kb[:, HEAD_DIM_A * kh:HEAD_DIM_A * (kh + 1)]
    slope = 2.0 ** (-8.0 * (h + 1) / N_HEADS_A)
    s = _dot_nt(q_h, k_h) * (HEAD_DIM_A ** -0.5) - slope * dist
    return q_h, k_h, jnp.where(valid, s, NEG)


def _swa_band(n, kp_ref, kc_ref, vp_ref, vc_ref, pq_ref, pp_ref, pc_ref):
    kb = jnp.concatenate([kp_ref[...], kc_ref[...]], axis=0).astype(BF16)
    vb = jnp.concatenate([vp_ref[...], vc_ref[...]], axis=0).astype(BF16)
    posk = jnp.concatenate([pp_ref[...], pc_ref[...]], axis=1)
    dist = jnp.abs(pq_ref[...] - posk)
    qi = lax.broadcasted_iota(jnp.int32, (BLOCK, 2 * BLOCK), 0)
    si = lax.broadcasted_iota(jnp.int32, (BLOCK, 2 * BLOCK), 1)
    valid = (si > qi) & (si <= qi + WINDOW) & ((n > 0) | (si >= BLOCK))
    return kb, vb, dist, valid


def _swa_fwd(proj, posc, posr, sinks):
    t = proj.shape[0]
    nb = t // BLOCK

    def body(q_ref, kc_ref, kp_ref, vc_ref, vp_ref, pq_ref, pc_ref, pp_ref, sink_ref, o_ref, l_ref):
        n = pl.program_id(0)
        kb, vb, dist, valid = _swa_band(n, kp_ref, kc_ref, vp_ref, vc_ref, pq_ref, pp_ref, pc_ref)
        lane = lax.broadcasted_iota(jnp.int32, (BLOCK, LANES), 1)
        lse = jnp.zeros((BLOCK, LANES), F32)
        for h in range(N_HEADS_A):
            kh = h // (N_HEADS_A // N_KV_A)
            _, _, s = _swa_scores(q_ref, kb, dist, valid, h)
            sink = sink_ref[0:1, h:h + 1]
            m = jnp.maximum(jnp.max(s, axis=-1, keepdims=True), sink)
            e = jnp.exp(s - m)
            den = jnp.sum(e, axis=-1, keepdims=True) + jnp.exp(sink - m)
            p = e / den
            o_ref[:, HEAD_DIM_A * h:HEAD_DIM_A * (h + 1)] = _dot(p.astype(BF16), vb[:, HEAD_DIM_A * kh:HEAD_DIM_A * (kh + 1)])
            lse = jnp.where(lane == h, m + jnp.log(den), lse)
        l_ref[...] = lse

    cur = lambda n: (n, 0)
    return pl.pallas_call(
        body, name="swa_fwd", grid=(nb,),
        in_specs=[pl.BlockSpec((BLOCK, WIDTH_A), lambda n: (n, _QA_BLK)),
                  pl.BlockSpec((BLOCK, LANES), lambda n: (n, _KA_BLK)),
                  pl.BlockSpec((BLOCK, LANES), lambda n: (jnp.maximum(n - 1, 0), _KA_BLK)),
                  pl.BlockSpec((BLOCK, LANES), lambda n: (n, _VA_BLK)),
                  pl.BlockSpec((BLOCK, LANES), lambda n: (jnp.maximum(n - 1, 0), _VA_BLK)),
                  pl.BlockSpec((BLOCK, 1), cur),
                  pl.BlockSpec((1, BLOCK), lambda n: (0, n)),
                  pl.BlockSpec((1, BLOCK), lambda n: (0, jnp.maximum(n - 1, 0))),
                  _full((1, N_HEADS_A))],
        out_specs=[pl.BlockSpec((BLOCK, WIDTH_A), cur), pl.BlockSpec((BLOCK, LANES), cur)],
        out_shape=[jax.ShapeDtypeStruct((t, WIDTH_A), F32), jax.ShapeDtypeStruct((t, LANES), F32)],
        compiler_params=_params(("parallel",)),
    )(proj, proj, proj, proj, proj, posc, posr, posr, sinks)


def _rope_coeffs(pos, freq):
    ang = pos * freq
    cosv, sinv = jnp.cos(ang), jnp.sin(ang)
    lane = lax.broadcasted_iota(jnp.int32, ang.shape, 1)
    lo = (lane >= QK_NOPE) & (lane < QK_NOPE + QK_ROPE // 2)
    hi = (lane >= QK_NOPE + QK_ROPE // 2) & (lane < QK_NOPE + QK_ROPE)
    c = jnp.where(lane < QK_NOPE, 1.0, jnp.where(lo | hi, cosv, 0.0))
    s = jnp.where(lo, -sinv, jnp.where(hi, sinv, 0.0))
    return c, s, lo, hi


def _rope(xh, c, s, lo):
    up = pltpu.roll(xh, LANES - QK_ROPE // 2, axis=1)
    dn = pltpu.roll(xh, QK_ROPE // 2, axis=1)
    return xh * c + jnp.where(lo, up, dn) * s


def _unrope(dh, c, s, lo, hi):
    g = dh * s
    up = pltpu.roll(g, LANES - QK_ROPE // 2, axis=1)
    dn = pltpu.roll(g, QK_ROPE // 2, axis=1)
    return dh * c + jnp.where(hi, dn, jnp.where(lo, up, 0.0))


_TQ = 512
_MLA_SCALE = Q_HEAD_B ** -0.5


def _mla_prep_fwd(proj, posc, freq, qan, kvan, wq, wk, wv):
    t = proj.shape[0]
    tm = _TQ
    nb = t // tm

    def body(cq_ref, ckv_ref, kr_ref, pos_ref, f_ref, qan_ref, kvan_ref, wq_ref, wk_ref, wv_ref,
             q_ref, k_ref, qt_ref, kt_ref, v_ref, vt_ref):
        cq = cq_ref[...]
        cqn = ((cq * _rms(cq)) * qan_ref[...]).astype(BF16)
        ckv = ckv_ref[...]
        ckvn = ((ckv * _rms(ckv)) * kvan_ref[...]).astype(BF16)
        qb = _dot(cqn, wq_ref[...])
        kb = _dot(ckvn, wk_ref[...])
        vb = _dot(ckvn, wv_ref[...])
        vbt = vb.T
        c, s, lo, _ = _rope_coeffs(pos_ref[...], f_ref[...])
        kr = _rope(kr_ref[...], c, s, lo)
        for h in range(N_HEADS_B):
            sl = slice(HEAD_PAD * h, HEAD_PAD * (h + 1))
            q_h = _rope(qb[:, sl], c, s, lo)
            k_h = kb[:, sl] + kr
            q_ref[:, sl] = q_h.astype(BF16)
            k_ref[:, sl] = k_h.astype(BF16)
            qt_ref[h, 0] = q_h.T.astype(BF16)
            kt_ref[h, 0] = k_h.T.astype(BF16)
            v_ref[h] = vb[:, V_DIM_B * h:V_DIM_B * (h + 1)].astype(BF16)
            vt_ref[h, 0] = vbt[V_DIM_B * h:V_DIM_B * (h + 1), :].astype(BF16)

    row = lambda i: (i, 0)
    blk4 = lambda d: pl.BlockSpec((N_HEADS_B, 1, d, tm), lambda i: (0, i, 0, 0))
    return pl.pallas_call(
        body, name="mla_prep_fwd", grid=(nb,),
        in_specs=[pl.BlockSpec((tm, Q_LORA), lambda i: (i, _CQ_BLK)),
                  pl.BlockSpec((tm, LANES), lambda i: (i, _CKV_BLK)),
                  pl.BlockSpec((tm, LANES), lambda i: (i, _KR_BLK)),
                  pl.BlockSpec((tm, 1), row), _full((1, LANES)), _full((1, Q_LORA)), _full((1, KV_LORA)),
                  _full((Q_LORA, MLA_W)), _full((KV_LORA, MLA_W)), _full((KV_LORA, N_HEADS_B * V_DIM_B))],
        out_specs=[pl.BlockSpec((tm, MLA_W), row), pl.BlockSpec((tm, MLA_W), row), blk4(HEAD_PAD), blk4(HEAD_PAD),
                   pl.BlockSpec((N_HEADS_B, tm, V_DIM_B), lambda i: (0, i, 0)), blk4(V_DIM_B)],
        out_shape=[jax.ShapeDtypeStruct((t, MLA_W), BF16), jax.ShapeDtypeStruct((t, MLA_W), BF16),
                   jax.ShapeDtypeStruct((N_HEADS_B, nb, HEAD_PAD, tm), BF16),
                   jax.ShapeDtypeStruct((N_HEADS_B, nb, HEAD_PAD, tm), BF16),
                   jax.ShapeDtypeStruct((N_HEADS_B, t, V_DIM_B), BF16),
                   jax.ShapeDtypeStruct((N_HEADS_B, nb, V_DIM_B, tm), BF16)],
        compiler_params=_params(("parallel",)),
    )(proj, proj, proj, posc, freq, qan, kvan, wq, wk, wv)


def _scores_t(k, qt, diagonal):
    st = _dot(k, qt) * _MLA_SCALE
    if diagonal:
        key = lax.broadcasted_iota(jnp.int32, st.shape, 0)
        qry = lax.broadcasted_iota(jnp.int32, st.shape, 1)
        st = jnp.where(key <= qry, st, NEG)
    return st


def _mla_fwd(k, qt, vt):
    t = k.shape[0]
    nb = t // _TQ

    def body(k_ref, qt_ref, vt_ref, o_ref, l_ref):
        qi = pl.program_id(1)
        q_t = qt_ref[0, 0]

        def block(kj, carry, diagonal):
            m, l, acc = carry
            st = _scores_t(k_ref[pl.ds(pl.multiple_of(kj * _TQ, _TQ), _TQ), :], q_t, diagonal)
            m_new = jnp.maximum(m, jnp.max(st, axis=0, keepdims=True))
            alpha = jnp.exp(m - m_new)
            p = jnp.exp(st - m_new)
            l = alpha * l + jnp.sum(p, axis=0, keepdims=True)
            acc = alpha * acc + _dot(vt_ref[0, kj], p.astype(BF16))
            return m_new, l, acc

        init = (jnp.full((1, _TQ), NEG, F32), jnp.zeros((1, _TQ), F32), jnp.zeros((V_DIM_B, _TQ), F32))
        carry = lax.fori_loop(0, qi, lambda kj, cr: block(kj, cr, False), init)
        m, l, acc = block(qi, carry, True)
        o_ref[0, 0] = acc / l
        l_ref[0, 0] = m + jnp.log(l)

    return pl.pallas_call(
        body, name="mla_fwd", grid=(N_HEADS_B, nb),
        in_specs=[pl.BlockSpec((t, HEAD_PAD), lambda h, qi: (0, h)),
                  pl.BlockSpec((1, 1, HEAD_PAD, _TQ), lambda h, qi: (h, qi, 0, 0)),
                  pl.BlockSpec((1, nb, V_DIM_B, _TQ), lambda h, qi: (h, 0, 0, 0))],
        out_specs=[pl.BlockSpec((1, 1, V_DIM_B, _TQ), lambda h, qi: (h, qi, 0, 0)),
                   pl.BlockSpec((1, 1, 1, _TQ), lambda h, qi: (h, qi, 0, 0))],
        out_shape=[jax.ShapeDtypeStruct((N_HEADS_B, nb, V_DIM_B, _TQ), F32),
                   jax.ShapeDtypeStruct((N_HEADS_B, nb, 1, _TQ), F32)],
        compiler_params=_params(("parallel", "parallel")),
    )(k, qt, vt)


def _ot_spec(tm, d):
    per = _TQ // tm
    return pl.BlockSpec((N_HEADS_B, 1, d, tm), lambda i: (0, i // per, 0, i % per))


def _mix_out_fwd(out_a, out_bt, proj, x, w_oa, w_ob, w_out, g2, g3):
    t = x.shape[0]
    tm = 256

    def body(oa_ref, obt_ref, ga_ref, gb_ref, x_ref, woa_ref, wob_ref, wout_ref, g2_ref, g3_ref,
             mg_ref, y_ref, x1_ref, h2_ref):
        oa = _dot(oa_ref[...].astype(BF16), woa_ref[...])
        obt = obt_ref[...].reshape(N_HEADS_B * V_DIM_B, tm).astype(BF16)
        ob = _dot_tn(obt, wob_ref[...])
        merged = (jax.nn.sigmoid(ga_ref[...]) * oa + jax.nn.sigmoid(gb_ref[...]) * ob).astype(BF16)
        mg_ref[...] = merged
        y = _dot(merged, wout_ref[...])
        y_ref[...] = y
        x1 = x_ref[...] + (y * _rms(y)) * g2_ref[...]
        x1_ref[...] = x1
        h2_ref[...] = ((x1 * _rms(x1)) * g3_ref[...]).astype(BF16)

    row = lambda i: (i, 0)
    blk = pl.BlockSpec((tm, D_MODEL), row)
    return pl.pallas_call(
        body, name="mix_out_fwd", grid=(t // tm,),
        in_specs=[pl.BlockSpec((tm, WIDTH_A), row), _ot_spec(tm, V_DIM_B), pl.BlockSpec((tm, D_MODEL), lambda i: (i, 0)),
                  pl.BlockSpec((tm, D_MODEL), lambda i: (i, 1)), blk,
                  _full((WIDTH_A, D_MODEL)), _full((N_HEADS_B * V_DIM_B, D_MODEL)), _full((D_MODEL, D_MODEL)),
                  _full((1, D_MODEL)), _full((1, D_MODEL))],
        out_specs=[blk, blk, blk, blk],
        out_shape=[jax.ShapeDtypeStruct((t, D_MODEL), BF16), jax.ShapeDtypeStruct((t, D_MODEL), F32),
                   jax.ShapeDtypeStruct((t, D_MODEL), F32), jax.ShapeDtypeStruct((t, D_MODEL), BF16)],
        compiler_params=_params(("parallel",)),
    )(out_a, out_bt, proj, proj, x, w_oa, w_ob, w_out, g2, g3)


def _up_fwd(h2, w_up):
    t = h2.shape[0]
    tm, tn = 512, 1024

    def body(h_ref, w_ref, u_ref, a_ref):
        u = _dot(h_ref[...], w_ref[...])
        u_ref[...] = u
        a_ref[...] = jnp.square(jnp.maximum(u, 0.0)).astype(BF16)

    return pl.pallas_call(
        body, name="up_fwd", grid=(t // tm, D_FF // tn),
        in_specs=[pl.BlockSpec((tm, D_MODEL), lambda i, j: (i, 0)), pl.BlockSpec((D_MODEL, tn), lambda i, j: (0, j))],
        out_specs=[pl.BlockSpec((tm, tn), lambda i, j: (i, j))] * 2,
        out_shape=[jax.ShapeDtypeStruct((t, D_FF), F32), jax.ShapeDtypeStruct((t, D_FF), BF16)],
        compiler_params=_params(("parallel", "parallel")),
    )(h2, w_up)


def _down_fwd_loss(a, w_down, x1, target, g4):
    t = a.shape[0]
    tm, tk = 512, 1024
    nk = D_FF // tk

    def body(a_ref, w_ref, x1_ref, tg_ref, g_ref, dx2_ref, dyd_ref, dg_ref, loss_ref, acc):
        i, k = pl.program_id(0), pl.program_id(1)

        @pl.when((i == 0) & (k == 0))
        def _():
            dg_ref[...] = jnp.zeros(dg_ref.shape, F32)
            loss_ref[...] = jnp.zeros(loss_ref.shape, F32)

        @pl.when(k == 0)
        def _():
            acc[...] = jnp.zeros(acc.shape, F32)

        acc[...] += _dot(a_ref[...], w_ref[...])

        @pl.when(k == nk - 1)
        def _():
            yd = acc[...]
            r = _rms(yd)
            n = yd * r
            diff = (x1_ref[...] + n * g_ref[...]) - tg_ref[...]
            loss_ref[...] += 0.5 * jnp.sum(jnp.mean(diff * diff, axis=-1, keepdims=True), axis=0, keepdims=True)
            dx2 = diff * (1.0 / D_MODEL)
            dx2_ref[...] = dx2
            dyd, dg = _norm_bwd(dx2, n, r, g_ref[...])
            dyd_ref[...] = dyd.astype(BF16)
            dg_ref[...] += dg

    row = lambda i, k: (i, 0)
    return pl.pallas_call(
        body, name="down_fwd_loss", grid=(t // tm, nk),
        in_specs=[pl.BlockSpec((tm, tk), lambda i, k: (i, k)), pl.BlockSpec((tk, D_MODEL), lambda i, k: (k, 0)),
                  pl.BlockSpec((tm, D_MODEL), row), pl.BlockSpec((tm, D_MODEL), row), _full((1, D_MODEL))],
        out_specs=[pl.BlockSpec((tm, D_MODEL), row), pl.BlockSpec((tm, D_MODEL), row), _full((1, D_MODEL)), _full((1, LANES))],
        out_shape=[jax.ShapeDtypeStruct((t, D_MODEL), F32), jax.ShapeDtypeStruct((t, D_MODEL), BF16),
                   jax.ShapeDtypeStruct((1, D_MODEL), F32), jax.ShapeDtypeStruct((1, LANES), F32)],
        scratch_shapes=[pltpu.VMEM((tm, D_MODEL), F32)],
        compiler_params=_params(("arbitrary", "arbitrary")),
    )(a, w_down, x1, target, g4)


def _matmul_tn(a, b, name, tm, tn, tk=1024):
    t, m = a.shape
    n = b.shape[1]
    tk = min(tk, t)
    nk = t // tk

    def body(a_ref, b_ref, o_ref):
        @pl.when(pl.program_id(2) == 0)
        def _():
            o_ref[...] = jnp.zeros(o_ref.shape, F32)

        o_ref[...] += _dot_tn(a_ref[...].astype(BF16), b_ref[...].astype(BF16))

    return pl.pallas_call(
        body, name=name, grid=(m // tm, n // tn, nk),
        in_specs=[pl.BlockSpec((tk, tm), lambda i, j, k: (k, i)), pl.BlockSpec((tk, tn), lambda i, j, k: (k, j))],
        out_specs=pl.BlockSpec((tm, tn), lambda i, j, k: (i, j)),
        out_shape=jax.ShapeDtypeStruct((m, n), F32),
        compiler_params=_params(("parallel", "parallel", "arbitrary")),
    )(a, b)


def _down_bwd(dyd, w_down, u):
    t = dyd.shape[0]
    tm, tn = 512, 1024

    def body(d_ref, w_ref, u_ref, du_ref):
        da = _dot_nt(d_ref[...], w_ref[...])
        du_ref[...] = (da * (2.0 * jnp.maximum(u_ref[...], 0.0))).astype(BF16)

    return pl.pallas_call(
        body, name="down_bwd", grid=(t // tm, D_FF // tn),
        in_specs=[pl.BlockSpec((tm, D_MODEL), lambda i, j: (i, 0)), pl.BlockSpec((tn, D_MODEL), lambda i, j: (j, 0)),
                  pl.BlockSpec((tm, tn), lambda i, j: (i, j))],
        out_specs=pl.BlockSpec((tm, tn), lambda i, j: (i, j)),
        out_shape=jax.ShapeDtypeStruct((t, D_FF), BF16),
        compiler_params=_params(("parallel", "parallel")),
    )(dyd, w_down, u)


def _up_bwd(du, w_up, x1, dx2, y, g3, g2):
    t = du.shape[0]
    tm, tk = 512, 1024
    nk = D_FF // tk

    def body(du_ref, w_ref, x1_ref, dx2_ref, y_ref, g3_ref, g2_ref, dx1_ref, dy_ref, dg3_ref, dg2_ref, acc):
        i, k = pl.program_id(0), pl.program_id(1)

        @pl.when((i == 0) & (k == 0))
        def _():
            dg3_ref[...] = jnp.zeros(dg3_ref.shape, F32)
            dg2_ref[...] = jnp.zeros(dg2_ref.shape, F32)

        @pl.when(k == 0)
        def _():
            acc[...] = jnp.zeros(acc.shape, F32)

        acc[...] += _dot_nt(du_ref[...], w_ref[...])

        @pl.when(k == nk - 1)
        def _():
            x1 = x1_ref[...]
            r3 = _rms(x1)
            d3, dg3 = _norm_bwd(acc[...], x1 * r3, r3, g3_ref[...])
            dx1 = dx2_ref[...] + d3
            dx1_ref[...] = dx1
            dg3_ref[...] += dg3
            y = y_ref[...]
            r2 = _rms(y)
            dy, dg2 = _norm_bwd(dx1, y * r2, r2, g2_ref[...])
            dy_ref[...] = dy.astype(BF16)
            dg2_ref[...] += dg2

    row = lambda i, k: (i, 0)
    blk = pl.BlockSpec((tm, D_MODEL), row)
    return pl.pallas_call(
        body, name="up_bwd", grid=(t // tm, nk),
        in_specs=[pl.BlockSpec((tm, tk), lambda i, k: (i, k)), pl.BlockSpec((D_MODEL, tk), lambda i, k: (0, k)),
                  blk, blk, blk, _full((1, D_MODEL)), _full((1, D_MODEL))],
        out_specs=[blk, blk, _full((1, D_MODEL)), _full((1, D_MODEL))],
        out_shape=[jax.ShapeDtypeStruct((t, D_MODEL), F32), jax.ShapeDtypeStruct((t, D_MODEL), BF16),
                   jax.ShapeDtypeStruct((1, D_MODEL), F32), jax.ShapeDtypeStruct((1, D_MODEL), F32)],
        scratch_shapes=[pltpu.VMEM((tm, D_MODEL), F32)],
        compiler_params=_params(("arbitrary", "arbitrary")),
    )(du, w_up, x1, dx2, y, g3, g2)


def _mix_out_bwd(dy, out_a, out_bt, proj, w_oa, w_ob, w_out):
    t = dy.shape[0]
    tm = 256
    nb = t // _TQ

    def body(dy_ref, oa_ref, obt_ref, ga_ref, gb_ref, woa_ref, wob_ref, wout_ref,
             doa_ref, dob_ref, dga_ref, dgb_ref, da_ref, db_ref, dbt_ref, dela_ref, delb_ref):
        dm = _dot_nt(dy_ref[...], wout_ref[...])
        out_a_v = oa_ref[...]
        out_bt_v = obt_ref[...].reshape(N_HEADS_B * V_DIM_B, tm)
        oa = _dot(out_a_v.astype(BF16), woa_ref[...])
        ob = _dot_tn(out_bt_v.astype(BF16), wob_ref[...])
        sa, sb = jax.nn.sigmoid(ga_ref[...]), jax.nn.sigmoid(gb_ref[...])
        doa = (dm * sa).astype(BF16)
        dob = (dm * sb).astype(BF16)
        doa_ref[...] = doa
        dob_ref[...] = dob
        dga_ref[...] = (dm * oa * (sa * (1.0 - sa))).astype(BF16)
        dgb_ref[...] = (dm * ob * (sb * (1.0 - sb))).astype(BF16)
        d_out_a = _dot_nt(doa, woa_ref[...])
        da_ref[...] = d_out_a
        lane = lax.broadcasted_iota(jnp.int32, (tm, LANES), 1)
        prod_a = d_out_a * out_a_v
        dela = jnp.zeros((tm, LANES), F32)
        for h in range(N_HEADS_A):
            dh = jnp.sum(prod_a[:, HEAD_DIM_A * h:HEAD_DIM_A * (h + 1)], axis=-1, keepdims=True)
            dela = jnp.where(lane == h, dh, dela)
        dela_ref[...] = dela
        d_out_b = _dot_nt(dob, wob_ref[...])
        d_out_bt = _dot_nt(wob_ref[...], dob)
        prod_bt = d_out_bt * out_bt_v
        for h in range(N_HEADS_B):
            db_ref[h] = d_out_b[:, V_DIM_B * h:V_DIM_B * (h + 1)].astype(BF16)
            dbt_ref[h, 0] = d_out_bt[V_DIM_B * h:V_DIM_B * (h + 1), :].astype(BF16)
            delb_ref[h, 0] = jnp.sum(prod_bt[V_DIM_B * h:V_DIM_B * (h + 1), :], axis=0, keepdims=True)

    row = lambda i: (i, 0)
    blk = pl.BlockSpec((tm, D_MODEL), row)
    return pl.pallas_call(
        body, name="mix_out_bwd", grid=(t // tm,),
        in_specs=[blk, pl.BlockSpec((tm, WIDTH_A), row), _ot_spec(tm, V_DIM_B), pl.BlockSpec((tm, D_MODEL), lambda i: (i, 0)),
                  pl.BlockSpec((tm, D_MODEL), lambda i: (i, 1)),
                  _full((WIDTH_A, D_MODEL)), _full((N_HEADS_B * V_DIM_B, D_MODEL)), _full((D_MODEL, D_MODEL))],
        out_specs=[blk, blk, blk, blk, pl.BlockSpec((tm, WIDTH_A), row),
                   pl.BlockSpec((N_HEADS_B, tm, V_DIM_B), lambda i: (0, i, 0)), _ot_spec(tm, V_DIM_B),
                   pl.BlockSpec((tm, LANES), row), _ot_spec(tm, 1)],
        out_shape=[jax.ShapeDtypeStruct((t, D_MODEL), BF16)] * 4
        + [jax.ShapeDtypeStruct((t, WIDTH_A), F32), jax.ShapeDtypeStruct((N_HEADS_B, t, V_DIM_B), BF16),
           jax.ShapeDtypeStruct((N_HEADS_B, nb, V_DIM_B, _TQ), BF16),
           jax.ShapeDtypeStruct((t, LANES), F32), jax.ShapeDtypeStruct((N_HEADS_B, nb, 1, _TQ), F32)],
        compiler_params=_params(("parallel",)),
    )(dy, out_a, out_bt, proj, proj, w_oa, w_ob, w_out)


def _dw_ob(out_bt, dob):
    t = dob.shape[0]
    nb = t // _TQ

    def body(obt_ref, dob_ref, o_ref):
        @pl.when(pl.program_id(0) == 0)
        def _():
            o_ref[...] = jnp.zeros(o_ref.shape, F32)

        obt = obt_ref[...].reshape(N_HEADS_B * V_DIM_B, _TQ).astype(BF16)
        o_ref[...] += _dot(obt, dob_ref[...])

    return pl.pallas_call(
        body, name="dw_o_b", grid=(nb,),
        in_specs=[pl.BlockSpec((N_HEADS_B, 1, V_DIM_B, _TQ), lambda i: (0, i, 0, 0)),
                  pl.BlockSpec((_TQ, D_MODEL), lambda i: (i, 0))],
        out_specs=_full((N_HEADS_B * V_DIM_B, D_MODEL)),
        out_shape=jax.ShapeDtypeStruct((N_HEADS_B * V_DIM_B, D_MODEL), F32),
        compiler_params=_params(("arbitrary",)),
    )(out_bt, dob)


def _mla_bwd(q, k, qt, kt, v, d_out, d_out_t, lse, delta):
    t = q.shape[0]
    nb = t // _TQ

    def body(k_ref, kt_ref, v_ref, q_ref, qt_ref, do_ref, dot_ref, l_ref, d_ref, dqt_ref, dk_ref, dv_ref):
        kj = pl.program_id(1)

        @pl.when(kj == 0)
        def _():
            dqt_ref[...] = jnp.zeros(dqt_ref.shape, F32)

        kv, k_t, vv = k_ref[...], kt_ref[0, 0], v_ref[0]

        def block(qi, carry, diagonal):
            dk, dv = carry
            rows = pl.ds(pl.multiple_of(qi * _TQ, _TQ), _TQ)
            st = _scores_t(kv, qt_ref[0, qi], diagonal)
            pt = jnp.exp(st - l_ref[0, qi])
            dv = dv + _dot(pt.astype(BF16), do_ref[0, rows, :])
            dpt = _dot(vv, dot_ref[0, qi])
            dst = (pt * (dpt - d_ref[0, qi]) * _MLA_SCALE).astype(BF16)
            dk = dk + _dot(dst, q_ref[rows, :])
            dqt_ref[0, qi] += _dot(k_t, dst)
            return dk, dv

        init = (jnp.zeros((_TQ, HEAD_PAD), F32), jnp.zeros((_TQ, V_DIM_B), F32))
        carry = block(kj, init, True)
        dk, dv = lax.fori_loop(kj + 1, nb, lambda qi, cr: block(qi, cr, False), carry)
        dk_ref[...] = dk
        dv_ref[0] = dv

    head4 = lambda d: pl.BlockSpec((1, nb, d, _TQ), lambda h, kj: (h, 0, 0, 0))
    return pl.pallas_call(
        body, name="mla_bwd", grid=(N_HEADS_B, nb),
        in_specs=[pl.BlockSpec((_TQ, HEAD_PAD), lambda h, kj: (kj, h)),
                  pl.BlockSpec((1, 1, HEAD_PAD, _TQ), lambda h, kj: (h, kj, 0, 0)),
                  pl.BlockSpec((1, _TQ, V_DIM_B), lambda h, kj: (h, kj, 0)),
                  pl.BlockSpec((t, HEAD_PAD), lambda h, kj: (0, h)), head4(HEAD_PAD),
                  pl.BlockSpec((1, t, V_DIM_B), lambda h, kj: (h, 0, 0)), head4(V_DIM_B), head4(1), head4(1)],
        out_specs=[head4(HEAD_PAD), pl.BlockSpec((_TQ, HEAD_PAD), lambda h, kj: (kj, h)),
                   pl.BlockSpec((1, _TQ, V_DIM_B), lambda h, kj: (h, kj, 0))],
        out_shape=[jax.ShapeDtypeStruct((N_HEADS_B, nb, HEAD_PAD, _TQ), F32), jax.ShapeDtypeStruct((t, MLA_W), F32),
                   jax.ShapeDtypeStruct((N_HEADS_B, t, V_DIM_B), F32)],
        compiler_params=_params(("parallel", "arbitrary")),
    )(k, kt, v, q, qt, d_out, d_out_t, lse, delta)


def _mla_prep_bwd(dqt, dk, dv, proj, posc, freq, qan, kvan, wq, wk, wv):
    t = dk.shape[0]
    tm = _TQ

    def body(dqt_ref, dk_ref, dv_ref, cq_ref, ckv_ref, pos_ref, f_ref, qan_ref, kvan_ref, wq_ref, wk_ref, wv_ref,
             dcq_ref, dckv_ref, dkr_ref, dwq_ref, dwk_ref, dwv_ref, dqan_ref, dkvan_ref):
        @pl.when(pl.program_id(0) == 0)
        def _():
            for r in (dwq_ref, dwk_ref, dwv_ref, dqan_ref, dkvan_ref):
                r[...] = jnp.zeros(r.shape, F32)

        cq = cq_ref[...]
        rq = _rms(cq)
        nq_ = cq * rq
        cqn = (nq_ * qan_ref[...]).astype(BF16)
        ckv = ckv_ref[...]
        rkv = _rms(ckv)
        nkv = ckv * rkv
        ckvn = (nkv * kvan_ref[...]).astype(BF16)
        c, s, lo, hi = _rope_coeffs(pos_ref[...], f_ref[...])
        dkv = dk_ref[...]
        dkr = jnp.zeros((tm, LANES), F32)
        dqb = []
        for h in range(N_HEADS_B):
            dqb.append(_unrope(dqt_ref[h, 0].T, c, s, lo, hi).astype(BF16))
            dkr = dkr + dkv[:, HEAD_PAD * h:HEAD_PAD * (h + 1)]
        dqb = jnp.concatenate(dqb, axis=1)
        dkr_ref[...] = jnp.where(lo | hi, _unrope(dkr, c, s, lo, hi), 0.0).astype(BF16)
        dkb = dkv.astype(BF16)
        dvb = jnp.concatenate([dv_ref[h] for h in range(N_HEADS_B)], axis=1).astype(BF16)
        dwq_ref[...] += _dot_tn(cqn, dqb)
        dwk_ref[...] += _dot_tn(ckvn, dkb)
        dwv_ref[...] += _dot_tn(ckvn, dvb)
        dcqn = _dot_nt(dqb, wq_ref[...])
        dckvn = _dot_nt(dkb, wk_ref[...]) + _dot_nt(dvb, wv_ref[...])
        dcq, dqan = _norm_bwd(dcqn, nq_, rq, qan_ref[...])
        dckv, dkvan = _norm_bwd(dckvn, nkv, rkv, kvan_ref[...])
        dcq_ref[...] = dcq.astype(BF16)
        dckv_ref[...] = dckv.astype(BF16)
        dqan_ref[...] += dqan
        dkvan_ref[...] += dkvan

    row = lambda i: (i, 0)
    vw = N_HEADS_B * V_DIM_B
    return pl.pallas_call(
        body, name="mla_prep_bwd", grid=(t // tm,),
        in_specs=[pl.BlockSpec((N_HEADS_B, 1, HEAD_PAD, tm), lambda i: (0, i, 0, 0)), pl.BlockSpec((tm, MLA_W), row),
                  pl.BlockSpec((N_HEADS_B, tm, V_DIM_B), lambda i: (0, i, 0)),
                  pl.BlockSpec((tm, Q_LORA), lambda i: (i, _CQ_BLK)),
                  pl.BlockSpec((tm, LANES), lambda i: (i, _CKV_BLK)),
                  pl.BlockSpec((tm, 1), row), _full((1, LANES)), _full((1, Q_LORA)), _full((1, KV_LORA)),
                  _full((Q_LORA, MLA_W)), _full((KV_LORA, MLA_W)), _full((KV_LORA, vw))],
        out_specs=[pl.BlockSpec((tm, Q_LORA), row), pl.BlockSpec((tm, LANES), row), pl.BlockSpec((tm, LANES), row),
                   _full((Q_LORA, MLA_W)), _full((KV_LORA, MLA_W)), _full((KV_LORA, vw)),
                   _full((1, Q_LORA)), _full((1, KV_LORA))],
        out_shape=[jax.ShapeDtypeStruct((t, Q_LORA), BF16), jax.ShapeDtypeStruct((t, LANES), BF16),
                   jax.ShapeDtypeStruct((t, LANES), BF16),
                   jax.ShapeDtypeStruct((Q_LORA, MLA_W), F32), jax.ShapeDtypeStruct((KV_LORA, MLA_W), F32),
                   jax.ShapeDtypeStruct((KV_LORA, vw), F32),
                   jax.ShapeDtypeStruct((1, Q_LORA), F32), jax.ShapeDtypeStruct((1, KV_LORA), F32)],
        compiler_params=_params(("arbitrary",)),
    )(dqt, dk, dv, proj, proj, posc, freq, qan, kvan, wq, wk, wv)


def _swa_bwd(proj, d_out, lse, delta, posc, posr, sinks):
    t = proj.shape[0]
    nb = t // BLOCK
    group = N_HEADS_A // N_KV_A

    def body(q_ref, kc_ref, kp_ref, vc_ref, vp_ref, do_ref, l_ref, d_ref, pq_ref, pc_ref, pp_ref, sink_ref,
             dq_ref, dk_ref, dv_ref, ds_ref, dkb_s, dvb_s, dk_carry, dv_carry):
        n = pl.program_id(0)

        @pl.when(n == 0)
        def _():
            ds_ref[...] = jnp.zeros(ds_ref.shape, F32)
            dk_carry[...] = jnp.zeros(dk_carry.shape, F32)
            dv_carry[...] = jnp.zeros(dv_carry.shape, F32)

        @pl.when(n < nb)
        def _():
            kb, vb, dist, valid = _swa_band(n, kp_ref, kc_ref, vp_ref, vc_ref, pq_ref, pp_ref, pc_ref)
            lse_v, del_v = l_ref[...], d_ref[...]
            lane = lax.broadcasted_iota(jnp.int32, (1, LANES), 1)
            dsink = jnp.zeros((1, LANES), F32)
            for kh in range(N_KV_A):
                dk_h = jnp.zeros((2 * BLOCK, HEAD_DIM_A), F32)
                dv_h = jnp.zeros((2 * BLOCK, HEAD_DIM_A), F32)
                for h in range(group * kh, group * (kh + 1)):
                    q_h, k_h, s = _swa_scores(q_ref, kb, dist, valid, h)
                    l_h, d_h = lse_v[:, h:h + 1], del_v[:, h:h + 1]
                    p = jnp.exp(s - l_h)
                    p_sink = jnp.exp(sink_ref[0:1, h:h + 1] - l_h)
                    dsink = jnp.where(lane == h, jnp.sum(-p_sink * d_h, axis=0, keepdims=True), dsink)
                    do_h = do_ref[:, HEAD_DIM_A * h:HEAD_DIM_A * (h + 1)].astype(BF16)
                    dp = _dot_nt(do_h, vb[:, HEAD_DIM_A * kh:HEAD_DIM_A * (kh + 1)])
                    ds = (p * (dp - d_h) * (HEAD_DIM_A ** -0.5)).astype(BF16)
                    dq_ref[:, HEAD_DIM_A * h:HEAD_DIM_A * (h + 1)] = _dot(ds, k_h)
                    dk_h = dk_h + _dot_tn(ds, q_h)
                    dv_h = dv_h + _dot_tn(p.astype(BF16), do_h)
                dkb_s[:, HEAD_DIM_A * kh:HEAD_DIM_A * (kh + 1)] = dk_h
                dvb_s[:, HEAD_DIM_A * kh:HEAD_DIM_A * (kh + 1)] = dv_h
            ds_ref[...] += dsink
            dk_ref[...] = dk_carry[...] + dkb_s[0:BLOCK, :]
            dv_ref[...] = dv_carry[...] + dvb_s[0:BLOCK, :]
            dk_carry[...] = dkb_s[BLOCK:2 * BLOCK, :]
            dv_carry[...] = dvb_s[BLOCK:2 * BLOCK, :]

        @pl.when(n == nb)
        def _():
            dk_ref[...] = dk_carry[...]
            dv_ref[...] = dv_carry[...]

    cur = lambda n: (jnp.minimum(n, nb - 1), 0)
    prv = lambda n: jnp.maximum(jnp.minimum(n, nb - 1) - 1, 0)
    out_prev = lambda n: (jnp.maximum(n - 1, 0), 0)
    return pl.pallas_call(
        body, name="swa_bwd", grid=(nb + 1,),
        in_specs=[pl.BlockSpec((BLOCK, WIDTH_A), lambda n: (jnp.minimum(n, nb - 1), _QA_BLK)),
                  pl.BlockSpec((BLOCK, LANES), lambda n: (jnp.minimum(n, nb - 1), _KA_BLK)),
                  pl.BlockSpec((BLOCK, LANES), lambda n: (prv(n), _KA_BLK)),
                  pl.BlockSpec((BLOCK, LANES), lambda n: (jnp.minimum(n, nb - 1), _VA_BLK)),
                  pl.BlockSpec((BLOCK, LANES), lambda n: (prv(n), _VA_BLK)),
                  pl.BlockSpec((BLOCK, WIDTH_A), cur), pl.BlockSpec((BLOCK, LANES), cur), pl.BlockSpec((BLOCK, LANES), cur),
                  pl.BlockSpec((BLOCK, 1), cur),
                  pl.BlockSpec((1, BLOCK), lambda n: (0, jnp.minimum(n, nb - 1))),
                  pl.BlockSpec((1, BLOCK), lambda n: (0, prv(n))),
                  _full((1, N_HEADS_A))],
        out_specs=[pl.BlockSpec((BLOCK, WIDTH_A), cur), pl.BlockSpec((BLOCK, LANES), out_prev),
                   pl.BlockSpec((BLOCK, LANES), out_prev), _full((1, LANES))],
        out_shape=[jax.ShapeDtypeStruct((t, WIDTH_A), F32), jax.ShapeDtypeStruct((t, LANES), F32),
                   jax.ShapeDtypeStruct((t, LANES), F32), jax.ShapeDtypeStruct((1, LANES), F32)],
        scratch_shapes=[pltpu.VMEM((2 * BLOCK, LANES), F32), pltpu.VMEM((2 * BLOCK, LANES), F32),
                        pltpu.VMEM((BLOCK, LANES), F32), pltpu.VMEM((BLOCK, LANES), F32)],
        compiler_params=_params(("arbitrary",)),
    )(proj, proj, proj, proj, proj, d_out, lse, delta, posc, posr, posr, sinks)


def _in_bwd(dproj, w_in_p, x, dx1, g1):
    t = x.shape[0]
    tm = 256

    def body(dp_ref, w_ref, x_ref, dx1_ref, g_ref, dx_ref, dg_ref):
        @pl.when(pl.program_id(0) == 0)
        def _():
            dg_ref[...] = jnp.zeros(dg_ref.shape, F32)

        dh = _dot_nt(dp_ref[...], w_ref[...])
        xv = x_ref[...]
        r = _rms(xv)
        dx, dg = _norm_bwd(dh, xv * r, r, g_ref[...])
        dx_ref[...] = dx1_ref[...] + dx
        dg_ref[...] += dg

    row = lambda i: (i, 0)
    blk = pl.BlockSpec((tm, D_MODEL), row)
    return pl.pallas_call(
        body, name="in_bwd", grid=(t // tm,),
        in_specs=[pl.BlockSpec((tm, D_IN_PAD), row), _full((D_MODEL, D_IN_PAD)), blk, blk, _full((1, D_MODEL))],
        out_specs=[blk, _full((1, D_MODEL))],
        out_shape=[jax.ShapeDtypeStruct((t, D_MODEL), F32), jax.ShapeDtypeStruct((1, D_MODEL), F32)],
        compiler_params=_params(("arbitrary",)),
    )(dproj, w_in_p, x, dx1, g1)


def _adamw(w, g_parts, m, v, name):
    r = w.shape[0]
    tr = min(r, 128)
    ng = len(g_parts)

    def body(*refs):
        w_ref, g_refs, m_ref, v_ref = refs[0], refs[1:1 + ng], refs[1 + ng], refs[2 + ng]
        g_out, d_out, m_out, v_out = refs[3 + ng:]
        g = g_refs[0][...]
        for gr in g_refs[1:]:
            g = g + gr[...]
        m_new = ADAM_B1 * m_ref[...] + (1.0 - ADAM_B1) * g
        v_new = ADAM_B2 * v_ref[...] + (1.0 - ADAM_B2) * jnp.square(g)
        m_hat = m_new / (1.0 - ADAM_B1 ** ADAM_STEP)
        v_hat = v_new / (1.0 - ADAM_B2 ** ADAM_STEP)
        g_out[...] = g
        d_out[...] = -ADAM_LR * (m_hat / (jnp.sqrt(v_hat) + ADAM_EPS) + ADAM_WD * w_ref[...])
        m_out[...] = m_new
        v_out[...] = v_new

    blk = pl.BlockSpec((tr, D_MODEL), lambda i: (i, 0))
    return pl.pallas_call(
        body, name=name, grid=(r // tr,),
        in_specs=[blk] * (3 + ng), out_specs=[blk] * 4,
        out_shape=[jax.ShapeDtypeStruct((r, D_MODEL), F32)] * 4,
        compiler_params=_params(("parallel",)),
    )(w, *g_parts, m, v)


_HBM = pl.BlockSpec(memory_space=pltpu.HBM)


def _other_chips(x, y):
    return ((1 - x, y), (x, 1 - y), (1 - x, 1 - y))


def _all_gather_chips(packed):
    r, w = packed.shape

    def body(src, out, send_sems, recv_sems, local_sem):
        x, y, c = lax.axis_index("x"), lax.axis_index("y"), lax.axis_index("c")
        me = 2 * x + y
        local = pltpu.make_async_copy(src, out.at[me], local_sem)
        local.start()
        sends = []
        for j, (px, py) in enumerate(_other_chips(x, y)):
            cp = pltpu.make_async_remote_copy(src_ref=src, dst_ref=out.at[me], send_sem=send_sems.at[j],
                                              recv_sem=recv_sems.at[j], device_id=(px, py, c), device_id_type=MESH)
            cp.start()
            sends.append(cp)
        for j, (px, py) in enumerate(_other_chips(x, y)):
            pltpu.make_async_remote_copy(src_ref=src, dst_ref=out.at[2 * px + py], send_sem=send_sems.at[j],
                                         recv_sem=recv_sems.at[j], device_id=(px, py, c), device_id_type=MESH).wait_recv()
        for cp in sends:
            cp.wait_send()
        local.wait()

    return pl.pallas_call(
        body, name="ag_weights", in_specs=[_HBM], out_specs=_HBM,
        out_shape=jax.ShapeDtypeStruct((N_CHIPS, r, w), packed.dtype),
        scratch_shapes=[pltpu.SemaphoreType.DMA((3,)), pltpu.SemaphoreType.DMA((3,)), pltpu.SemaphoreType.DMA(())],
    )(packed)


def _scatter_chips(gp):
    _, r, w = gp.shape

    def body(src, land, send_sems, recv_sems):
        x, y, c = lax.axis_index("x"), lax.axis_index("y"), lax.axis_index("c")
        sends = []
        for j, (px, py) in enumerate(_other_chips(x, y)):
            cp = pltpu.make_async_remote_copy(src_ref=src.at[2 * px + py], dst_ref=land.at[j], send_sem=send_sems.at[j],
                                              recv_sem=recv_sems.at[j], device_id=(px, py, c), device_id_type=MESH)
            cp.start()
            sends.append(cp)
        for cp in sends:
            cp.wait_recv()
        for cp in sends:
            cp.wait_send()

    return pl.pallas_call(
        body, name="rs_scatter", in_specs=[_HBM], out_specs=_HBM,
        out_shape=jax.ShapeDtypeStruct((3, r, w), gp.dtype),
        scratch_shapes=[pltpu.SemaphoreType.DMA((3,)), pltpu.SemaphoreType.DMA((3,))],
    )(gp)


def _sum4(own, land):
    r, w = own.shape
    tr = 128

    def body(o_ref, l_ref, s_ref):
        s_ref[...] = ((o_ref[...] + l_ref[0]) + l_ref[1]) + l_ref[2]

    return pl.pallas_call(
        body, name="rs_sum", grid=(r // tr,),
        in_specs=[pl.BlockSpec((tr, w), lambda i: (i, 0)), pl.BlockSpec((3, tr, w), lambda i: (0, i, 0))],
        out_specs=pl.BlockSpec((tr, w), lambda i: (i, 0)),
        out_shape=jax.ShapeDtypeStruct((r, w), F32),
        compiler_params=_params(("parallel",)),
    )(own, land)


def _swap_sibling(s):
    def body(src, got, send_sem, recv_sem):
        x, y, c = lax.axis_index("x"), lax.axis_index("y"), lax.axis_index("c")
        cp = pltpu.make_async_remote_copy(src_ref=src, dst_ref=got, send_sem=send_sem, recv_sem=recv_sem,
                                          device_id=(x, y, 1 - c), device_id_type=MESH)
        cp.start()
        cp.wait_recv()
        cp.wait_send()

    return pl.pallas_call(
        body, name="rs_swap", in_specs=[_HBM], out_specs=_HBM,
        out_shape=jax.ShapeDtypeStruct(s.shape, s.dtype),
        scratch_shapes=[pltpu.SemaphoreType.DMA(()), pltpu.SemaphoreType.DMA(())],
    )(s)


def _all_reduce_small(part):
    n_dev = 8

    def body(src, out, gath, send_sems, recv_sems):
        x, y, c = lax.axis_index("x"), lax.axis_index("y"), lax.axis_index("c")
        me = 4 * x + 2 * y + c
        gath[me] = src[...]
        peers = []
        for k in range(1, n_dev):
            px = 1 - x if (k >> 2) & 1 else x
            py = 1 - y if (k >> 1) & 1 else y
            pc = 1 - c if k & 1 else c
            peers.append((px, py, pc))
        sends = []
        for j, peer in enumerate(peers):
            cp = pltpu.make_async_remote_copy(src_ref=src, dst_ref=gath.at[me], send_sem=send_sems.at[j],
                                              recv_sem=recv_sems.at[j], device_id=peer, device_id_type=MESH)
            cp.start()
            sends.append(cp)
        for j, (px, py, pc) in enumerate(peers):
            pltpu.make_async_remote_copy(src_ref=src, dst_ref=gath.at[4 * px + 2 * py + pc], send_sem=send_sems.at[j],
                                         recv_sem=recv_sems.at[j], device_id=(px, py, pc), device_id_type=MESH).wait_recv()
        for cp in sends:
            cp.wait_send()
        acc = gath[0]
        for d in range(1, n_dev):
            acc = acc + gath[d]
        out[...] = acc

    vmem = pl.BlockSpec(memory_space=pltpu.VMEM)
    return pl.pallas_call(
        body, name="ar_small", in_specs=[vmem], out_specs=vmem,
        out_shape=jax.ShapeDtypeStruct(part.shape, F32),
        scratch_shapes=[pltpu.VMEM((n_dev,) + part.shape, F32), pltpu.SemaphoreType.DMA((n_dev - 1,)),
                        pltpu.SemaphoreType.DMA((n_dev - 1,))],
    )(part)


def _pack(shards, dtype):
    parts = [shards[n].reshape(PACK_ROWS[n], D_MODEL).astype(dtype) for n in BIG]
    pad = PACK_TOTAL - sum(PACK_ROWS.values())
    return jnp.concatenate(parts + [jnp.zeros((pad, D_MODEL), dtype)], axis=0)


def _unpack(packed):
    out, off = {}, 0
    for n in BIG:
        out[n] = packed[off:off + PACK_ROWS[n]].reshape(SHARD_SHAPES[n])
        off += PACK_ROWS[n]
    return out


def _full_weights(gathered):
    out, off = {}, 0
    for n in BIG:
        r, c = SHARD_SHAPES[n]
        g = gathered[:, off:off + PACK_ROWS[n]].reshape(N_CHIPS, r, c)
        off += PACK_ROWS[n]
        out[n] = jnp.transpose(g, (1, 0, 2)).reshape(r, N_CHIPS * c) if n in COL_SHARDED else g.reshape(N_CHIPS * r, c)
    return out


def _shard_major(full):
    parts = []
    for n in BIG:
        r, c = SHARD_SHAPES[n]
        g = full[n]
        g = jnp.transpose(g.reshape(r, N_CHIPS, c), (1, 0, 2)) if n in COL_SHARDED else g.reshape(N_CHIPS, r, c)
        parts.append(g.reshape(N_CHIPS, PACK_ROWS[n], D_MODEL))
    pad = PACK_TOTAL - sum(PACK_ROWS.values())
    return jnp.concatenate(parts + [jnp.zeros((N_CHIPS, pad, D_MODEL), F32)], axis=1)


def _pad_layouts(w):
    dt = w["w_in"].dtype
    w_in = w["w_in"]
    z = lambda r, c: jnp.zeros((r, c), dt)
    w_in_p = jnp.concatenate([w_in[:, :3200], z(D_MODEL, 64), w_in[:, 3200:], z(D_MODEL, 32)], axis=1)
    wq = w["w_q_b"].reshape(Q_LORA, N_HEADS_B, Q_HEAD_B)
    wq_p = jnp.concatenate([wq, jnp.zeros((Q_LORA, N_HEADS_B, HEAD_PAD - Q_HEAD_B), dt)], axis=2).reshape(Q_LORA, MLA_W)
    wkv = w["w_kv_b"].reshape(KV_LORA, N_HEADS_B, QK_NOPE + V_DIM_B)
    zk = jnp.zeros((KV_LORA, N_HEADS_B, HEAD_PAD - QK_NOPE), dt)
    wk_p = jnp.concatenate([wkv[:, :, :QK_NOPE], zk], axis=2).reshape(KV_LORA, MLA_W)
    wv = wkv[:, :, QK_NOPE:].reshape(KV_LORA, N_HEADS_B * V_DIM_B)
    return dict(w_in=w_in_p, wq=wq_p, wk=wk_p, wv=wv, w_o_a=w["w_o_a"], w_o_b=w["w_o_b"], w_out=w["w_out"],
                w_up=w["w_up"], w_down=w["w_down"])


def _unpad_grads(d):
    dw_in = jnp.concatenate([d["w_in"][:, :3200], d["w_in"][:, 3264:3296]], axis=1)
    dwq = d["wq"].reshape(Q_LORA, N_HEADS_B, HEAD_PAD)[:, :, :Q_HEAD_B].reshape(Q_LORA, N_HEADS_B * Q_HEAD_B)
    dwk = d["wk"].reshape(KV_LORA, N_HEADS_B, HEAD_PAD)[:, :, :QK_NOPE]
    dwv = d["wv"].reshape(KV_LORA, N_HEADS_B, V_DIM_B)
    dwkv = jnp.concatenate([dwk, dwv], axis=2).reshape(KV_LORA, N_HEADS_B * (QK_NOPE + V_DIM_B))
    return dict(w_in=dw_in, w_q_b=dwq, w_kv_b=dwkv, w_o_a=d["w_o_a"], w_o_b=d["w_o_b"], w_out=d["w_out"],
                w_up=d["w_up"], w_down=d["w_down"])


def _rope_freq_lanes():
    freqs = ROPE_THETA ** (-jnp.arange(0, QK_ROPE, 2, dtype=F32) / QK_ROPE)
    return jnp.concatenate([jnp.zeros((QK_NOPE,), F32), freqs, freqs,
                            jnp.zeros((HEAD_PAD - Q_HEAD_B,), F32)]).reshape(1, LANES)


def _local_step(x, positions, target, small, wfull):
    t = x.shape[0]
    wp = _pad_layouts(wfull)
    posr = positions.astype(F32).reshape(1, t)
    posc = posr.reshape(t, 1)
    freq = _rope_freq_lanes()
    g1, g2, g3, g4 = small["pre_norm_mix"], small["post_norm_mix"], small["pre_norm_mlp"], small["post_norm_mlp"]
    qan, kvan, sinks = small["q_a_norm"], small["kv_a_norm"], small["sinks"]

    h, proj = _proj_fwd(x, g1, wp["w_in"])
    out_a, lse_a = _swa_fwd(proj, posc, posr, sinks)
    qm, km, qt, kt, vm, vt = _mla_prep_fwd(proj, posc, freq, qan, kvan, wp["wq"], wp["wk"], wp["wv"])
    out_bt, lse_b = _mla_fwd(km, qt, vt)
    merged, y, x1, h2 = _mix_out_fwd(out_a, out_bt, proj, x, wp["w_o_a"], wp["w_o_b"], wp["w_out"], g2, g3)
    u, a = _up_fwd(h2, wp["w_up"])
    dx2, dyd, dg4, loss = _down_fwd_loss(a, wp["w_down"], x1, target, g4)

    dw = {}
    dw["w_down"] = _matmul_tn(a, dyd, "dw_down", 512, 1024)
    du = _down_bwd(dyd, wp["w_down"], u)
    dw["w_up"] = _matmul_tn(h2, du, "dw_up", 512, 1024)
    dx1, dy, dg3, dg2 = _up_bwd(du, wp["w_up"], x1, dx2, y, g3, g2)
    dw["w_out"] = _matmul_tn(merged, dy, "dw_out", 512, 1024)
    doa, dob, dga, dgb, d_out_a, d_out_b, d_out_bt, del_a, del_b = _mix_out_bwd(
        dy, out_a, out_bt, proj, wp["w_o_a"], wp["w_o_b"], wp["w_out"])
    dw["w_o_a"] = _matmul_tn(out_a, doa, "dw_o_a", 512, 1024)
    dw["w_o_b"] = _dw_ob(out_bt, dob)
    dqm, dkm, dvm = _mla_bwd(qm, km, qt, kt, vm, d_out_b, d_out_bt, lse_b, del_b)
    dcq, dckv, dkr, dwq, dwk, dwv, dqan, dkvan = _mla_prep_bwd(
        dqm, dkm, dvm, proj, posc, freq, qan, kvan, wp["wq"], wp["wk"], wp["wv"])
    dw["wq"], dw["wk"], dw["wv"] = dwq, dwk, dwv
    dqa, dka, dva, dsinks = _swa_bwd(proj, d_out_a, lse_a, del_a, posc, posr, sinks)
    dproj = jnp.concatenate([dga, dgb, dqa.astype(BF16), dka.astype(BF16), dva.astype(BF16), dcq, dckv, dkr], axis=1)
    dw["w_in"] = _matmul_tn(h, dproj, "dw_in", 512, D_IN_PAD // 2)
    grad_x, dg1 = _in_bwd(dproj, wp["w_in"], x, dx1, g1)

    dsmall = dict(pre_norm_mix=dg1, post_norm_mix=dg2, pre_norm_mlp=dg3, post_norm_mlp=dg4,
                  q_a_norm=dqan, kv_a_norm=dkvan, sinks=dsinks[:, :N_HEADS_A])
    return loss, grad_x, _unpad_grads(dw), dsmall


def _pack_small(p, extra=None):
    tail = jnp.concatenate([p["q_a_norm"], p["kv_a_norm"], p["sinks"],
                            jnp.zeros((1, D_MODEL - Q_LORA - KV_LORA - N_HEADS_A), F32)], axis=1)
    scalar = jnp.zeros((1, D_MODEL), F32)
    if extra is not None:
        scalar = scalar.at[0, 0].set(extra)
    return jnp.concatenate([p["pre_norm_mix"], p["post_norm_mix"], p["pre_norm_mlp"], p["post_norm_mlp"], tail, scalar,
                            jnp.zeros((2, D_MODEL), F32)], axis=0)


def _unpack_small(b):
    return dict(pre_norm_mix=b[0:1], post_norm_mix=b[1:2], pre_norm_mlp=b[2:3], post_norm_mlp=b[3:4],
                q_a_norm=b[4:5, :Q_LORA], kv_a_norm=b[4:5, Q_LORA:Q_LORA + KV_LORA],
                sinks=b[4:5, Q_LORA + KV_LORA:Q_LORA + KV_LORA + N_HEADS_A])


def kernel(x, positions, pre_norm_mix, w_in, q_a_norm, w_q_b, kv_a_norm, w_kv_b, sinks, w_o_a, w_o_b, w_out, post_norm_mix, pre_norm_mlp, w_up, w_down, post_norm_mlp, loss_target, m_pre_norm_mix, m_w_in, m_q_a_norm, m_w_q_b, m_kv_a_norm, m_w_kv_b, m_sinks, m_w_o_a, m_w_o_b, m_w_out, m_post_norm_mix, m_pre_norm_mlp, m_w_up, m_w_down, m_post_norm_mlp, v_pre_norm_mix, v_w_in, v_q_a_norm, v_w_q_b, v_kv_a_norm, v_w_kv_b, v_sinks, v_w_o_a, v_w_o_b, v_w_out, v_post_norm_mix, v_pre_norm_mlp, v_w_up, v_w_down, v_post_norm_mlp):
    w = dict(pre_norm_mix=pre_norm_mix, w_in=w_in[0], q_a_norm=q_a_norm, w_q_b=w_q_b[0], kv_a_norm=kv_a_norm,
             w_kv_b=w_kv_b[0], sinks=sinks, w_o_a=w_o_a[0], w_o_b=w_o_b[0], w_out=w_out[0],
             post_norm_mix=post_norm_mix, pre_norm_mlp=pre_norm_mlp, w_up=w_up[0], w_down=w_down[0],
             post_norm_mlp=post_norm_mlp)
    m = dict(pre_norm_mix=m_pre_norm_mix, w_in=m_w_in[0], q_a_norm=m_q_a_norm, w_q_b=m_w_q_b[0],
             kv_a_norm=m_kv_a_norm, w_kv_b=m_w_kv_b[0], sinks=m_sinks, w_o_a=m_w_o_a[0], w_o_b=m_w_o_b[0],
             w_out=m_w_out[0], post_norm_mix=m_post_norm_mix, pre_norm_mlp=m_pre_norm_mlp, w_up=m_w_up[0],
             w_down=m_w_down[0], post_norm_mlp=m_post_norm_mlp)
    v = dict(pre_norm_mix=v_pre_norm_mix, w_in=v_w_in[0], q_a_norm=v_q_a_norm, w_q_b=v_w_q_b[0],
             kv_a_norm=v_kv_a_norm, w_kv_b=v_w_kv_b[0], sinks=v_sinks, w_o_a=v_w_o_a[0], w_o_b=v_w_o_b[0],
             w_out=v_w_out[0], post_norm_mix=v_post_norm_mix, pre_norm_mlp=v_pre_norm_mlp, w_up=v_w_up[0],
             w_down=v_w_down[0], post_norm_mlp=v_post_norm_mlp)

    gathered = _all_gather_chips(_pack(w, BF16))
    loss, grad_x, dw, dsmall = _local_step(x[0], positions, loss_target[0], w, _full_weights(gathered))

    gp = _shard_major(dw)
    land = _scatter_chips(gp)
    chip = 2 * lax.axis_index("x") + lax.axis_index("y")
    own = lax.dynamic_index_in_dim(gp, chip, axis=0, keepdims=False)
    part = _sum4(own, land)
    g_big, d_big, m_big, v_big = _adamw(_pack(w, F32), [part, _swap_sibling(part)], _pack(m, F32), _pack(v, F32), "adamw_big")

    red = _all_reduce_small(_pack_small(dsmall, loss[0, 0]))
    g_sm, d_sm, m_sm, v_sm = _adamw(_pack_small(w), [red], _pack_small(m), _pack_small(v), "adamw_small")

    outs = []
    for big, sm in ((g_big, g_sm), (d_big, d_sm), (m_big, m_sm), (v_big, v_sm)):
        b, s = _unpack(big), _unpack_small(sm)
        outs.append([b[n][None] if n in b else s[n] for n in WEIGHTS])
    return (red[5, 0], grad_x[None], *outs[0], *outs[1], *outs[2], *outs[3])
```

```python
import functools

import jax
import jax.numpy as jnp
from jax import lax
from jax.experimental import pallas as pl
from jax.experimental.pallas import tpu as pltpu

F32 = jnp.float32
BF16 = jnp.bfloat16
MESH = pl.DeviceIdType.MESH

D_MODEL = 1024
N_HEADS_A = 8
N_KV_A = 2
HEAD_DIM_A = 64
WINDOW = 128
BLOCK = 128
N_HEADS_B = 8
QK_NOPE = 64
QK_ROPE = 32
V_DIM_B = 64
Q_LORA = 256
KV_LORA = 128
ROPE_THETA = 10000.0
D_FF = 4 * D_MODEL
EPS = 1e-6
WIDTH_A = N_HEADS_A * HEAD_DIM_A
Q_HEAD_B = QK_NOPE + QK_ROPE
D_IN = 3232
D_IN_PAD = 3328
HEAD_PAD = 128
MLA_W = N_HEADS_B * HEAD_PAD

ADAM_LR = 0.001
ADAM_B1 = 0.9
ADAM_B2 = 0.999
ADAM_EPS = 1e-08
ADAM_WD = 0.01
ADAM_STEP = 10

NEG = -1e30
N_CHIPS = 4
LANES = 128
VMEM_LIMIT = 56 * 1024 * 1024

BIG = ("w_in", "w_q_b", "w_kv_b", "w_o_a", "w_o_b", "w_out", "w_up", "w_down")
COL_SHARDED = ("w_in", "w_q_b", "w_kv_b", "w_o_a", "w_o_b", "w_up")
SHARD_SHAPES = {"w_in": (1024, 808), "w_q_b": (256, 192), "w_kv_b": (128, 256), "w_o_a": (512, 256),
                "w_o_b": (512, 256), "w_out": (256, 1024), "w_up": (1024, 1024), "w_down": (1024, 1024)}
PACK_ROWS = {n: (s[0] * s[1]) // D_MODEL for n, s in SHARD_SHAPES.items()}
GROUP_A = ("w_in", "w_q_b", "w_kv_b")
GROUP_B = ("w_o_a", "w_o_b", "w_out", "w_up", "w_down")
SMALL = ("pre_norm_mix", "post_norm_mix", "pre_norm_mlp", "post_norm_mlp", "q_a_norm", "kv_a_norm", "sinks")
WEIGHTS = ("pre_norm_mix", "w_in", "q_a_norm", "w_q_b", "kv_a_norm", "w_kv_b", "sinks", "w_o_a", "w_o_b", "w_out",
           "post_norm_mix", "pre_norm_mlp", "w_up", "w_down", "post_norm_mlp")


def _params(sem=None):
    return pltpu.CompilerParams(dimension_semantics=sem, vmem_limit_bytes=VMEM_LIMIT)


def _dot(a, b):
    return jnp.dot(a, b, preferred_element_type=F32)


def _dot_nt(a, b):
    return lax.dot_general(a, b, (((1,), (1,)), ((), ())), preferred_element_type=F32)


def _dot_tn(a, b):
    return lax.dot_general(a, b, (((0,), (0,)), ((), ())), preferred_element_type=F32)


def _rms(v):
    return lax.rsqrt(jnp.mean(v * v, axis=-1, keepdims=True) + EPS)


def _norm_bwd(dout, n, r, g):
    dn = dout * g
    dx = r * (dn - n * jnp.mean(dn * n, axis=-1, keepdims=True))
    return dx, jnp.sum(dout * n, axis=0, keepdims=True)


def _full(shape):
    return pl.BlockSpec(shape, lambda *_: (0,) * len(shape))


def _proj_fwd(x, g1, w_in_p):
    t = x.shape[0]
    tm = 256

    def body(x_ref, g_ref, w_ref, h_ref, p_ref):
        xv = x_ref[...]
        h = ((xv * _rms(xv)) * g_ref[...]).astype(BF16)
        h_ref[...] = h
        p_ref[...] = _dot(h, w_ref[...])

    return pl.pallas_call(
        body, name="proj_fwd", grid=(t // tm,),
        in_specs=[pl.BlockSpec((tm, D_MODEL), lambda i: (i, 0)), _full((1, D_MODEL)), _full((D_MODEL, D_IN_PAD))],
        out_specs=[pl.BlockSpec((tm, D_MODEL), lambda i: (i, 0)), pl.BlockSpec((tm, D_IN_PAD), lambda i: (i, 0))],
        out_shape=[jax.ShapeDtypeStruct((t, D_MODEL), BF16), jax.ShapeDtypeStruct((t, D_IN_PAD), F32)],
        compiler_params=_params(("parallel",)),
    )(x, g1, w_in_p)


_QA_BLK = 2048 // WIDTH_A
_KA_BLK = 2560 // LANES
_VA_BLK = 2688 // LANES
_CQ_BLK = 2816 // Q_LORA
_CKV_BLK = 3072 // LANES
_KR_BLK = 3200 // LANES


def _swa_scores(q_ref, kb, dist, valid, h):
    kh = h // (N_HEADS_A // N_KV_A)
    q_h = q_ref[:, HEAD_DIM_A * h:HEAD_DIM_A * (h + 1)].astype(BF16)
    k_h = kb[:, HEAD_DIM_A * kh:HEAD_DIM_A * (kh + 1)]
    slope = 2.0 ** (-8.0 * (h + 1) / N_HEADS_A)
    s = _dot_nt(q_h, k_h) * (HEAD_DIM_A ** -0.5) - slope * dist
    return q_h, k_h, jnp.where(valid, s, NEG)


def _swa_band(n, kp_ref, kc_ref, vp_ref, vc_ref, pq_ref, pp_ref, pc_ref):
    kb = jnp.concatenate([kp_ref[...], kc_ref[...]], axis=0).astype(BF16)
    vb = jnp.concatenate([vp_ref[...], vc_ref[...]], axis=0).astype(BF16)
    posk = jnp.concatenate([pp_ref[...], pc_ref[...]], axis=1)
    dist = jnp.abs(pq_ref[...] - posk)
    qi = lax.broadcasted_iota(jnp.int32, (BLOCK, 2 * BLOCK), 0)
    si = lax.broadcasted_iota(jnp.int32, (BLOCK, 2 * BLOCK), 1)
    valid = (si > qi) & (si <= qi + WINDOW) & ((n > 0) | (si >= BLOCK))
    return kb, vb, dist, valid


def _swa_fwd(proj, posc, posr, sinks):
    t = proj.shape[0]
    nb = t // BLOCK

    def body(q_ref, kc_ref, kp_ref, vc_ref, vp_ref, pq_ref, pc_ref, pp_ref, sink_ref, o_ref, l_ref):
        n = pl.program_id(0)
        kb, vb, dist, valid = _swa_band(n, kp_ref, kc_ref, vp_ref, vc_ref, pq_ref, pp_ref, pc_ref)
        lane = lax.broadcasted_iota(jnp.int32, (BLOCK, LANES), 1)
        lse = jnp.zeros((BLOCK, LANES), F32)
        for h in range(N_HEADS_A):
            kh = h // (N_HEADS_A // N_KV_A)
            _, _, s = _swa_scores(q_ref, kb, dist, valid, h)
            sink = sink_ref[0:1, h:h + 1]
            m = jnp.maximum(jnp.max(s, axis=-1, keepdims=True), sink)
            e = jnp.exp(s - m)
            den = jnp.sum(e, axis=-1, keepdims=True) + jnp.exp(sink - m)
            p = e / den
            o_ref[:, HEAD_DIM_A * h:HEAD_DIM_A * (h + 1)] = _dot(p.astype(BF16), vb[:, HEAD_DIM_A * kh:HEAD_DIM_A * (kh + 1)])
            lse = jnp.where(lane == h, m + jnp.log(den), lse)
        l_ref[...] = lse

    cur = lambda n: (n, 0)
    return pl.pallas_call(
        body, name="swa_fwd", grid=(nb,),
        in_specs=[pl.BlockSpec((BLOCK, WIDTH_A), lambda n: (n, _QA_BLK)),
                  pl.BlockSpec((BLOCK, LANES), lambda n: (n, _KA_BLK)),
                  pl.BlockSpec((BLOCK, LANES), lambda n: (jnp.maximum(n - 1, 0), _KA_BLK)),
                  pl.BlockSpec((BLOCK, LANES), lambda n: (n, _VA_BLK)),
                  pl.BlockSpec((BLOCK, LANES), lambda n: (jnp.maximum(n - 1, 0), _VA_BLK)),
                  pl.BlockSpec((BLOCK, 1), cur),
                  pl.BlockSpec((1, BLOCK), lambda n: (0, n)),
                  pl.BlockSpec((1, BLOCK), lambda n: (0, jnp.maximum(n - 1, 0))),
                  _full((1, N_HEADS_A))],
        out_specs=[pl.BlockSpec((BLOCK, WIDTH_A), cur), pl.BlockSpec((BLOCK, LANES), cur)],
        out_shape=[jax.ShapeDtypeStruct((t, WIDTH_A), F32), jax.ShapeDtypeStruct((t, LANES), F32)],
        compiler_params=_params(("parallel",)),
    )(proj, proj, proj, proj, proj, posc, posr, posr, sinks)


def _rope_coeffs(pos, freq):
    ang = pos * freq
    cosv, sinv = jnp.cos(ang), jnp.sin(ang)
    lane = lax.broadcasted_iota(jnp.int32, ang.shape, 1)
    lo = (lane >= QK_NOPE) & (lane < QK_NOPE + QK_ROPE // 2)
    hi = (lane >= QK_NOPE + QK_ROPE // 2) & (lane < QK_NOPE + QK_ROPE)
    c = jnp.where(lane < QK_NOPE, 1.0, jnp.where(lo | hi, cosv, 0.0))
    s = jnp.where(lo, -sinv, jnp.where(hi, sinv, 0.0))
    return c, s, lo, hi


def _rope(xh, c, s, lo):
    up = pltpu.roll(xh, LANES - QK_ROPE // 2, axis=1)
    dn = pltpu.roll(xh, QK_ROPE // 2, axis=1)
    return xh * c + jnp.where(lo, up, dn) * s


def _unrope(dh, c, s, lo, hi):
    g = dh * s
    up = pltpu.roll(g, LANES - QK_ROPE // 2, axis=1)
    dn = pltpu.roll(g, QK_ROPE // 2, axis=1)
    return dh * c + jnp.where(hi, dn, jnp.where(lo, up, 0.0))


_TQ = 512
_MLA_SCALE = Q_HEAD_B ** -0.5


def _mla_prep_fwd(proj, posc, freq, qan, kvan, wq, wk, wv):
    t = proj.shape[0]
    tm = _TQ
    nb = t // tm

    def body(cq_ref, ckv_ref, kr_ref, pos_ref, f_ref, qan_ref, kvan_ref, wq_ref, wk_ref, wv_ref,
             q_ref, k_ref, qt_ref, kt_ref, v_ref, vt_ref):
        cq = cq_ref[...]
        cqn = ((cq * _rms(cq)) * qan_ref[...]).astype(BF16)
        ckv = ckv_ref[...]
        ckvn = ((ckv * _rms(ckv)) * kvan_ref[...]).astype(BF16)
        qb = _dot(cqn, wq_ref[...])
        kb = _dot(ckvn, wk_ref[...])
        vb = _dot(ckvn, wv_ref[...])
        vbt = vb.T
        c, s, lo, _ = _rope_coeffs(pos_ref[...], f_ref[...])
        kr = _rope(kr_ref[...], c, s, lo)
        for h in range(N_HEADS_B):
            sl = slice(HEAD_PAD * h, HEAD_PAD * (h + 1))
            q_h = _rope(qb[:, sl], c, s, lo)
            k_h = kb[:, sl] + kr
            q_ref[:, sl] = q_h.astype(BF16)
            k_ref[:, sl] = k_h.astype(BF16)
            qt_ref[h, 0] = q_h.T.astype(BF16)
            kt_ref[h, 0] = k_h.T.astype(BF16)
            v_ref[h] = vb[:, V_DIM_B * h:V_DIM_B * (h + 1)].astype(BF16)
            vt_ref[h, 0] = vbt[V_DIM_B * h:V_DIM_B * (h + 1), :].astype(BF16)

    row = lambda i: (i, 0)
    blk4 = lambda d: pl.BlockSpec((N_HEADS_B, 1, d, tm), lambda i: (0, i, 0, 0))
    return pl.pallas_call(
        body, name="mla_prep_fwd", grid=(nb,),
        in_specs=[pl.BlockSpec((tm, Q_LORA), lambda i: (i, _CQ_BLK)),
                  pl.BlockSpec((tm, LANES), lambda i: (i, _CKV_BLK)),
                  pl.BlockSpec((tm, LANES), lambda i: (i, _KR_BLK)),
                  pl.BlockSpec((tm, 1), row), _full((1, LANES)), _full((1, Q_LORA)), _full((1, KV_LORA)),
                  _full((Q_LORA, MLA_W)), _full((KV_LORA, MLA_W)), _full((KV_LORA, N_HEADS_B * V_DIM_B))],
        out_specs=[pl.BlockSpec((tm, MLA_W), row), pl.BlockSpec((tm, MLA_W), row), blk4(HEAD_PAD), blk4(HEAD_PAD),
                   pl.BlockSpec((N_HEADS_B, tm, V_DIM_B), lambda i: (0, i, 0)), blk4(V_DIM_B)],
        out_shape=[jax.ShapeDtypeStruct((t, MLA_W), BF16), jax.ShapeDtypeStruct((t, MLA_W), BF16),
                   jax.ShapeDtypeStruct((N_HEADS_B, nb, HEAD_PAD, tm), BF16),
                   jax.ShapeDtypeStruct((N_HEADS_B, nb, HEAD_PAD, tm), BF16),
                   jax.ShapeDtypeStruct((N_HEADS_B, t, V_DIM_B), BF16),
                   jax.ShapeDtypeStruct((N_HEADS_B, nb, V_DIM_B, tm), BF16)],
        compiler_params=_params(("parallel",)),
    )(proj, proj, proj, posc, freq, qan, kvan, wq, wk, wv)


def _scores_t(k, qt, diagonal):
    st = _dot(k, qt) * _MLA_SCALE
    if diagonal:
        key = lax.broadcasted_iota(jnp.int32, st.shape, 0)
        qry = lax.broadcasted_iota(jnp.int32, st.shape, 1)
        st = jnp.where(key <= qry, st, NEG)
    return st


def _mla_fwd(k, qt, vt, w_src):
    t = k.shape[0]
    nb = t // _TQ

    def body(k_ref, qt_ref, vt_ref, w_ref, o_ref, l_ref, wg_ref, send_sems, recv_sems, local_sem):
        qi = pl.program_id(1)
        first = (pl.program_id(0) == 0) & (qi == 0)
        last = (pl.program_id(0) == N_HEADS_B - 1) & (qi == nb - 1)

        @pl.when(first)
        def _():
            _gather_start(w_ref, wg_ref, send_sems, recv_sems, local_sem)

        q_t = qt_ref[0, 0]

        def block(kj, carry, diagonal):
            m, l, acc = carry
            st = _scores_t(k_ref[pl.ds(pl.multiple_of(kj * _TQ, _TQ), _TQ), :], q_t, diagonal)
            m_new = jnp.maximum(m, jnp.max(st, axis=0, keepdims=True))
            alpha = jnp.exp(m - m_new)
            p = jnp.exp(st - m_new)
            l = alpha * l + jnp.sum(p, axis=0, keepdims=True)
            acc = alpha * acc + _dot(vt_ref[0, kj], p.astype(BF16))
            return m_new, l, acc

        init = (jnp.full((1, _TQ), NEG, F32), jnp.zeros((1, _TQ), F32), jnp.zeros((V_DIM_B, _TQ), F32))
        carry = lax.fori_loop(0, qi, lambda kj, cr: block(kj, cr, False), init)
        m, l, acc = block(qi, carry, True)
        o_ref[0, 0] = acc / l
        l_ref[0, 0] = m + jnp.log(l)

        @pl.when(last)
        def _():
            _gather_wait(w_ref, wg_ref, send_sems, recv_sems, local_sem)

    return pl.pallas_call(
        body, name="mla_fwd", grid=(N_HEADS_B, nb),
        in_specs=[pl.BlockSpec((t, HEAD_PAD), lambda h, qi: (0, h)),
                  pl.BlockSpec((1, 1, HEAD_PAD, _TQ), lambda h, qi: (h, qi, 0, 0)),
                  pl.BlockSpec((1, nb, V_DIM_B, _TQ), lambda h, qi: (h, 0, 0, 0)), _HBM],
        out_specs=[pl.BlockSpec((1, 1, V_DIM_B, _TQ), lambda h, qi: (h, qi, 0, 0)),
                   pl.BlockSpec((1, 1, 1, _TQ), lambda h, qi: (h, qi, 0, 0)), _HBM],
        out_shape=[jax.ShapeDtypeStruct((N_HEADS_B, nb, V_DIM_B, _TQ), F32),
                   jax.ShapeDtypeStruct((N_HEADS_B, nb, 1, _TQ), F32),
                   jax.ShapeDtypeStruct((N_CHIPS,) + w_src.shape, w_src.dtype)],
        scratch_shapes=[pltpu.SemaphoreType.DMA((3,)), pltpu.SemaphoreType.DMA((3,)), pltpu.SemaphoreType.DMA(())],
        compiler_params=_params(("arbitrary", "arbitrary")),
    )(k, qt, vt, w_src)


def _ot_spec(tm, d):
    per = _TQ // tm
    return pl.BlockSpec((N_HEADS_B, 1, d, tm), lambda i: (0, i // per, 0, i % per))


def _mix_out_fwd(out_a, out_bt, proj, x, w_oa, w_ob, w_out, g2, g3):
    t = x.shape[0]
    tm = 256

    def body(oa_ref, obt_ref, ga_ref, gb_ref, x_ref, woa_ref, wob_ref, wout_ref, g2_ref, g3_ref,
             mg_ref, y_ref, x1_ref, h2_ref):
        oa = _dot(oa_ref[...].astype(BF16), woa_ref[...])
        obt = obt_ref[...].reshape(N_HEADS_B * V_DIM_B, tm).astype(BF16)
        ob = _dot_tn(obt, wob_ref[...])
        merged = (jax.nn.sigmoid(ga_ref[...]) * oa + jax.nn.sigmoid(gb_ref[...]) * ob).astype(BF16)
        mg_ref[...] = merged
        y = _dot(merged, wout_ref[...])
        y_ref[...] = y
        x1 = x_ref[...] + (y * _rms(y)) * g2_ref[...]
        x1_ref[...] = x1
        h2_ref[...] = ((x1 * _rms(x1)) * g3_ref[...]).astype(BF16)

    row = lambda i: (i, 0)
    blk = pl.BlockSpec((tm, D_MODEL), row)
    return pl.pallas_call(
        body, name="mix_out_fwd", grid=(t // tm,),
        in_specs=[pl.BlockSpec((tm, WIDTH_A), row), _ot_spec(tm, V_DIM_B), pl.BlockSpec((tm, D_MODEL), lambda i: (i, 0)),
                  pl.BlockSpec((tm, D_MODEL), lambda i: (i, 1)), blk,
                  _full((WIDTH_A, D_MODEL)), _full((N_HEADS_B * V_DIM_B, D_MODEL)), _full((D_MODEL, D_MODEL)),
                  _full((1, D_MODEL)), _full((1, D_MODEL))],
        out_specs=[blk, blk, blk, blk],
        out_shape=[jax.ShapeDtypeStruct((t, D_MODEL), BF16), jax.ShapeDtypeStruct((t, D_MODEL), F32),
                   jax.ShapeDtypeStruct((t, D_MODEL), F32), jax.ShapeDtypeStruct((t, D_MODEL), BF16)],
        compiler_params=_params(("parallel",)),
    )(out_a, out_bt, proj, proj, x, w_oa, w_ob, w_out, g2, g3)


def _up_fwd(h2, w_up):
    t = h2.shape[0]
    tm, tn = 512, 1024

    def body(h_ref, w_ref, u_ref, a_ref):
        u = _dot(h_ref[...], w_ref[...])
        u_ref[...] = u
        a_ref[...] = jnp.square(jnp.maximum(u, 0.0)).astype(BF16)

    return pl.pallas_call(
        body, name="up_fwd", grid=(t // tm, D_FF // tn),
        in_specs=[pl.BlockSpec((tm, D_MODEL), lambda i, j: (i, 0)), pl.BlockSpec((D_MODEL, tn), lambda i, j: (0, j))],
        out_specs=[pl.BlockSpec((tm, tn), lambda i, j: (i, j))] * 2,
        out_shape=[jax.ShapeDtypeStruct((t, D_FF), F32), jax.ShapeDtypeStruct((t, D_FF), BF16)],
        compiler_params=_params(("parallel", "parallel")),
    )(h2, w_up)


def _down_fwd_loss(a, w_down, x1, target, g4):
    t = a.shape[0]
    tm, tk = 512, 1024
    nk = D_FF // tk

    def body(a_ref, w_ref, x1_ref, tg_ref, g_ref, dx2_ref, dyd_ref, dg_ref, loss_ref, acc):
        i, k = pl.program_id(0), pl.program_id(1)

        @pl.when((i == 0) & (k == 0))
        def _():
            dg_ref[...] = jnp.zeros(dg_ref.shape, F32)
            loss_ref[...] = jnp.zeros(loss_ref.shape, F32)

        @pl.when(k == 0)
        def _():
            acc[...] = jnp.zeros(acc.shape, F32)

        acc[...] += _dot(a_ref[...], w_ref[...])

        @pl.when(k == nk - 1)
        def _():
            yd = acc[...]
            r = _rms(yd)
            n = yd * r
            diff = (x1_ref[...] + n * g_ref[...]) - tg_ref[...]
            loss_ref[...] += 0.5 * jnp.sum(jnp.mean(diff * diff, axis=-1, keepdims=True), axis=0, keepdims=True)
            dx2 = diff * (1.0 / D_MODEL)
            dx2_ref[...] = dx2
            dyd, dg = _norm_bwd(dx2, n, r, g_ref[...])
            dyd_ref[...] = dyd.astype(BF16)
            dg_ref[...] += dg

    row = lambda i, k: (i, 0)
    return pl.pallas_call(
        body, name="down_fwd_loss", grid=(t // tm, nk),
        in_specs=[pl.BlockSpec((tm, tk), lambda i, k: (i, k)), pl.BlockSpec((tk, D_MODEL), lambda i, k: (k, 0)),
                  pl.BlockSpec((tm, D_MODEL), row), pl.BlockSpec((tm, D_MODEL), row), _full((1, D_MODEL))],
        out_specs=[pl.BlockSpec((tm, D_MODEL), row), pl.BlockSpec((tm, D_MODEL), row), _full((1, D_MODEL)), _full((1, LANES))],
        out_shape=[jax.ShapeDtypeStruct((t, D_MODEL), F32), jax.ShapeDtypeStruct((t, D_MODEL), BF16),
                   jax.ShapeDtypeStruct((1, D_MODEL), F32), jax.ShapeDtypeStruct((1, LANES), F32)],
        scratch_shapes=[pltpu.VMEM((tm, D_MODEL), F32)],
        compiler_params=_params(("arbitrary", "arbitrary")),
    )(a, w_down, x1, target, g4)


def _matmul_tn(a, b, name, tm, tn, tk=1024):
    t, m = a.shape
    n = b.shape[1]
    tk = min(tk, t)
    nk = t // tk

    def body(a_ref, b_ref, o_ref):
        @pl.when(pl.program_id(2) == 0)
        def _():
            o_ref[...] = jnp.zeros(o_ref.shape, F32)

        o_ref[...] += _dot_tn(a_ref[...].astype(BF16), b_ref[...].astype(BF16))

    return pl.pallas_call(
        body, name=name, grid=(m // tm, n // tn, nk),
        in_specs=[pl.BlockSpec((tk, tm), lambda i, j, k: (k, i)), pl.BlockSpec((tk, tn), lambda i, j, k: (k, j))],
        out_specs=pl.BlockSpec((tm, tn), lambda i, j, k: (i, j)),
        out_shape=jax.ShapeDtypeStruct((m, n), F32),
        compiler_params=_params(("parallel", "parallel", "arbitrary")),
    )(a, b)


def _down_bwd(dyd, w_down, u):
    t = dyd.shape[0]
    tm, tn = 512, 1024

    def body(d_ref, w_ref, u_ref, du_ref):
        da = _dot_nt(d_ref[...], w_ref[...])
        du_ref[...] = (da * (2.0 * jnp.maximum(u_ref[...], 0.0))).astype(BF16)

    return pl.pallas_call(
        body, name="down_bwd", grid=(t // tm, D_FF // tn),
        in_specs=[pl.BlockSpec((tm, D_MODEL), lambda i, j: (i, 0)), pl.BlockSpec((tn, D_MODEL), lambda i, j: (j, 0)),
                  pl.BlockSpec((tm, tn), lambda i, j: (i, j))],
        out_specs=pl.BlockSpec((tm, tn), lambda i, j: (i, j)),
        out_shape=jax.ShapeDtypeStruct((t, D_FF), BF16),
        compiler_params=_params(("parallel", "parallel")),
    )(dyd, w_down, u)


def _up_bwd(du, w_up, x1, dx2, y, g3, g2):
    t = du.shape[0]
    tm, tk = 512, 1024
    nk = D_FF // tk

    def body(du_ref, w_ref, x1_ref, dx2_ref, y_ref, g3_ref, g2_ref, dx1_ref, dy_ref, dg3_ref, dg2_ref, acc):
        i, k = pl.program_id(0), pl.program_id(1)

        @pl.when((i == 0) & (k == 0))
        def _():
            dg3_ref[...] = jnp.zeros(dg3_ref.shape, F32)
            dg2_ref[...] = jnp.zeros(dg2_ref.shape, F32)

        @pl.when(k == 0)
        def _():
            acc[...] = jnp.zeros(acc.shape, F32)

        acc[...] += _dot_nt(du_ref[...], w_ref[...])

        @pl.when(k == nk - 1)
        def _():
            x1 = x1_ref[...]
            r3 = _rms(x1)
            d3, dg3 = _norm_bwd(acc[...], x1 * r3, r3, g3_ref[...])
            dx1 = dx2_ref[...] + d3
            dx1_ref[...] = dx1
            dg3_ref[...] += dg3
            y = y_ref[...]
            r2 = _rms(y)
            dy, dg2 = _norm_bwd(dx1, y * r2, r2, g2_ref[...])
            dy_ref[...] = dy.astype(BF16)
            dg2_ref[...] += dg2

    row = lambda i, k: (i, 0)
    blk = pl.BlockSpec((tm, D_MODEL), row)
    return pl.pallas_call(
        body, name="up_bwd", grid=(t // tm, nk),
        in_specs=[pl.BlockSpec((tm, tk), lambda i, k: (i, k)), pl.BlockSpec((D_MODEL, tk), lambda i, k: (0, k)),
                  blk, blk, blk, _full((1, D_MODEL)), _full((1, D_MODEL))],
        out_specs=[blk, blk, _full((1, D_MODEL)), _full((1, D_MODEL))],
        out_shape=[jax.ShapeDtypeStruct((t, D_MODEL), F32), jax.ShapeDtypeStruct((t, D_MODEL), BF16),
                   jax.ShapeDtypeStruct((1, D_MODEL), F32), jax.ShapeDtypeStruct((1, D_MODEL), F32)],
        scratch_shapes=[pltpu.VMEM((tm, D_MODEL), F32)],
        compiler_params=_params(("arbitrary", "arbitrary")),
    )(du, w_up, x1, dx2, y, g3, g2)


def _mix_out_bwd(dy, out_a, out_bt, proj, w_oa, w_ob, w_out):
    t = dy.shape[0]
    tm = 256
    nb = t // _TQ

    def body(dy_ref, oa_ref, obt_ref, ga_ref, gb_ref, woa_ref, wob_ref, wout_ref,
             doa_ref, dob_ref, dga_ref, dgb_ref, da_ref, db_ref, dbt_ref, dela_ref, delb_ref):
        dm = _dot_nt(dy_ref[...], wout_ref[...])
        out_a_v = oa_ref[...]
        out_bt_v = obt_ref[...].reshape(N_HEADS_B * V_DIM_B, tm)
        oa = _dot(out_a_v.astype(BF16), woa_ref[...])
        ob = _dot_tn(out_bt_v.astype(BF16), wob_ref[...])
        sa, sb = jax.nn.sigmoid(ga_ref[...]), jax.nn.sigmoid(gb_ref[...])
        doa = (dm * sa).astype(BF16)
        dob = (dm * sb).astype(BF16)
        doa_ref[...] = doa
        dob_ref[...] = dob
        dga_ref[...] = (dm * oa * (sa * (1.0 - sa))).astype(BF16)
        dgb_ref[...] = (dm * ob * (sb * (1.0 - sb))).astype(BF16)
        d_out_a = _dot_nt(doa, woa_ref[...])
        da_ref[...] = d_out_a
        lane = lax.broadcasted_iota(jnp.int32, (tm, LANES), 1)
        prod_a = d_out_a * out_a_v
        dela = jnp.zeros((tm, LANES), F32)
        for h in range(N_HEADS_A):
            dh = jnp.sum(prod_a[:, HEAD_DIM_A * h:HEAD_DIM_A * (h + 1)], axis=-1, keepdims=True)
            dela = jnp.where(lane == h, dh, dela)
        dela_ref[...] = dela
        d_out_b = _dot_nt(dob, wob_ref[...])
        d_out_bt = _dot_nt(wob_ref[...], dob)
        prod_bt = d_out_bt * out_bt_v
        for h in range(N_HEADS_B):
            db_ref[h] = d_out_b[:, V_DIM_B * h:V_DIM_B * (h + 1)].astype(BF16)
            dbt_ref[h, 0] = d_out_bt[V_DIM_B * h:V_DIM_B * (h + 1), :].astype(BF16)
            delb_ref[h, 0] = jnp.sum(prod_bt[V_DIM_B * h:V_DIM_B * (h + 1), :], axis=0, keepdims=True)

    row = lambda i: (i, 0)
    blk = pl.BlockSpec((tm, D_MODEL), row)
    return pl.pallas_call(
        body, name="mix_out_bwd", grid=(t // tm,),
        in_specs=[blk, pl.BlockSpec((tm, WIDTH_A), row), _ot_spec(tm, V_DIM_B), pl.BlockSpec((tm, D_MODEL), lambda i: (i, 0)),
                  pl.BlockSpec((tm, D_MODEL), lambda i: (i, 1)),
                  _full((WIDTH_A, D_MODEL)), _full((N_HEADS_B * V_DIM_B, D_MODEL)), _full((D_MODEL, D_MODEL))],
        out_specs=[blk, blk, blk, blk, pl.BlockSpec((tm, WIDTH_A), row),
                   pl.BlockSpec((N_HEADS_B, tm, V_DIM_B), lambda i: (0, i, 0)), _ot_spec(tm, V_DIM_B),
                   pl.BlockSpec((tm, LANES), row), _ot_spec(tm, 1)],
        out_shape=[jax.ShapeDtypeStruct((t, D_MODEL), BF16)] * 4
        + [jax.ShapeDtypeStruct((t, WIDTH_A), F32), jax.ShapeDtypeStruct((N_HEADS_B, t, V_DIM_B), BF16),
           jax.ShapeDtypeStruct((N_HEADS_B, nb, V_DIM_B, _TQ), BF16),
           jax.ShapeDtypeStruct((t, LANES), F32), jax.ShapeDtypeStruct((N_HEADS_B, nb, 1, _TQ), F32)],
        compiler_params=_params(("parallel",)),
    )(dy, out_a, out_bt, proj, proj, w_oa, w_ob, w_out)


def _dw_ob(out_bt, dob):
    t = dob.shape[0]
    nb = t // _TQ

    def body(obt_ref, dob_ref, o_ref):
        @pl.when(pl.program_id(0) == 0)
        def _():
            o_ref[...] = jnp.zeros(o_ref.shape, F32)

        obt = obt_ref[...].reshape(N_HEADS_B * V_DIM_B, _TQ).astype(BF16)
        o_ref[...] += _dot(obt, dob_ref[...])

    return pl.pallas_call(
        body, name="dw_o_b", grid=(nb,),
        in_specs=[pl.BlockSpec((N_HEADS_B, 1, V_DIM_B, _TQ), lambda i: (0, i, 0, 0)),
                  pl.BlockSpec((_TQ, D_MODEL), lambda i: (i, 0))],
        out_specs=_full((N_HEADS_B * V_DIM_B, D_MODEL)),
        out_shape=jax.ShapeDtypeStruct((N_HEADS_B * V_DIM_B, D_MODEL), F32),
        compiler_params=_params(("arbitrary",)),
    )(out_bt, dob)


def _mla_bwd(q, k, qt, kt, v, d_out, d_out_t, lse, delta, gp):
    t = q.shape[0]
    nb = t // _TQ

    def body(k_ref, kt_ref, v_ref, q_ref, qt_ref, do_ref, dot_ref, l_ref, d_ref, gp_ref,
             dqt_ref, dk_ref, dv_ref, land_ref, send_sems, recv_sems):
        kj = pl.program_id(1)

        @pl.when((pl.program_id(0) == 0) & (kj == 0))
        def _():
            _scatter_start(gp_ref, land_ref, send_sems, recv_sems)

        @pl.when(kj == 0)
        def _():
            dqt_ref[...] = jnp.zeros(dqt_ref.shape, F32)

        kv, k_t, vv = k_ref[...], kt_ref[0, 0], v_ref[0]

        def block(qi, carry, diagonal):
            dk, dv = carry
            rows = pl.ds(pl.multiple_of(qi * _TQ, _TQ), _TQ)
            st = _scores_t(kv, qt_ref[0, qi], diagonal)
            pt = jnp.exp(st - l_ref[0, qi])
            dv = dv + _dot(pt.astype(BF16), do_ref[0, rows, :])
            dpt = _dot(vv, dot_ref[0, qi])
            dst = (pt * (dpt - d_ref[0, qi]) * _MLA_SCALE).astype(BF16)
            dk = dk + _dot(dst, q_ref[rows, :])
            dqt_ref[0, qi] += _dot(k_t, dst)
            return dk, dv

        init = (jnp.zeros((_TQ, HEAD_PAD), F32), jnp.zeros((_TQ, V_DIM_B), F32))
        carry = block(kj, init, True)
        dk, dv = lax.fori_loop(kj + 1, nb, lambda qi, cr: block(qi, cr, False), carry)
        dk_ref[...] = dk
        dv_ref[0] = dv

        @pl.when((pl.program_id(0) == N_HEADS_B - 1) & (kj == nb - 1))
        def _():
            _scatter_wait(gp_ref, land_ref, send_sems, recv_sems)

    head4 = lambda d: pl.BlockSpec((1, nb, d, _TQ), lambda h, kj: (h, 0, 0, 0))
    return pl.pallas_call(
        body, name="mla_bwd", grid=(N_HEADS_B, nb),
        in_specs=[pl.BlockSpec((_TQ, HEAD_PAD), lambda h, kj: (kj, h)),
                  pl.BlockSpec((1, 1, HEAD_PAD, _TQ), lambda h, kj: (h, kj, 0, 0)),
                  pl.BlockSpec((1, _TQ, V_DIM_B), lambda h, kj: (h, kj, 0)),
                  pl.BlockSpec((t, HEAD_PAD), lambda h, kj: (0, h)), head4(HEAD_PAD),
                  pl.BlockSpec((1, t, V_DIM_B), lambda h, kj: (h, 0, 0)), head4(V_DIM_B), head4(1), head4(1), _HBM],
        out_specs=[head4(HEAD_PAD), pl.BlockSpec((_TQ, HEAD_PAD), lambda h, kj: (kj, h)),
                   pl.BlockSpec((1, _TQ, V_DIM_B), lambda h, kj: (h, kj, 0)), _HBM],
        out_shape=[jax.ShapeDtypeStruct((N_HEADS_B, nb, HEAD_PAD, _TQ), F32), jax.ShapeDtypeStruct((t, MLA_W), F32),
                   jax.ShapeDtypeStruct((N_HEADS_B, t, V_DIM_B), F32),
                   jax.ShapeDtypeStruct((3,) + gp.shape[1:], gp.dtype)],
        scratch_shapes=[pltpu.SemaphoreType.DMA((3,)), pltpu.SemaphoreType.DMA((3,))],
        compiler_params=_params(("arbitrary", "arbitrary")),
    )(k, kt, v, q, qt, d_out, d_out_t, lse, delta, gp)


def _mla_prep_bwd(dqt, dk, dv, proj, posc, freq, qan, kvan, wq, wk, wv):
    t = dk.shape[0]
    tm = _TQ

    def body(dqt_ref, dk_ref, dv_ref, cq_ref, ckv_ref, pos_ref, f_ref, qan_ref, kvan_ref, wq_ref, wk_ref, wv_ref,
             dcq_ref, dckv_ref, dkr_ref, dwq_ref, dwk_ref, dwv_ref, dqan_ref, dkvan_ref):
        @pl.when(pl.program_id(0) == 0)
        def _():
            for r in (dwq_ref, dwk_ref, dwv_ref, dqan_ref, dkvan_ref):
                r[...] = jnp.zeros(r.shape, F32)

        cq = cq_ref[...]
        rq = _rms(cq)
        nq_ = cq * rq
        cqn = (nq_ * qan_ref[...]).astype(BF16)
        ckv = ckv_ref[...]
        rkv = _rms(ckv)
        nkv = ckv * rkv
        ckvn = (nkv * kvan_ref[...]).astype(BF16)
        c, s, lo, hi = _rope_coeffs(pos_ref[...], f_ref[...])
        dkv = dk_ref[...]
        dkr = jnp.zeros((tm, LANES), F32)
        dqb = []
        for h in range(N_HEADS_B):
            dqb.append(_unrope(dqt_ref[h, 0].T, c, s, lo, hi).astype(BF16))
            dkr = dkr + dkv[:, HEAD_PAD * h:HEAD_PAD * (h + 1)]
        dqb = jnp.concatenate(dqb, axis=1)
        dkr_ref[...] = jnp.where(lo | hi, _unrope(dkr, c, s, lo, hi), 0.0).astype(BF16)
        dkb = dkv.astype(BF16)
        dvb = jnp.concatenate([dv_ref[h] for h in range(N_HEADS_B)], axis=1).astype(BF16)
        dwq_ref[...] += _dot_tn(cqn, dqb)
        dwk_ref[...] += _dot_tn(ckvn, dkb)
        dwv_ref[...] += _dot_tn(ckvn, dvb)
        dcqn = _dot_nt(dqb, wq_ref[...])
        dckvn = _dot_nt(dkb, wk_ref[...]) + _dot_nt(dvb, wv_ref[...])
        dcq, dqan = _norm_bwd(dcqn, nq_, rq, qan_ref[...])
        dckv, dkvan = _norm_bwd(dckvn, nkv, rkv, kvan_ref[...])
        dcq_ref[...] = dcq.astype(BF16)
        dckv_ref[...] = dckv.astype(BF16)
        dqan_ref[...] += dqan
        dkvan_ref[...] += dkvan

    row = lambda i: (i, 0)
    vw = N_HEADS_B * V_DIM_B
    return pl.pallas_call(
        body, name="mla_prep_bwd", grid=(t // tm,),
        in_specs=[pl.BlockSpec((N_HEADS_B, 1, HEAD_PAD, tm), lambda i: (0, i, 0, 0)), pl.BlockSpec((tm, MLA_W), row),
                  pl.BlockSpec((N_HEADS_B, tm, V_DIM_B), lambda i: (0, i, 0)),
                  pl.BlockSpec((tm, Q_LORA), lambda i: (i, _CQ_BLK)),
                  pl.BlockSpec((tm, LANES), lambda i: (i, _CKV_BLK)),
                  pl.BlockSpec((tm, 1), row), _full((1, LANES)), _full((1, Q_LORA)), _full((1, KV_LORA)),
                  _full((Q_LORA, MLA_W)), _full((KV_LORA, MLA_W)), _full((KV_LORA, vw))],
        out_specs=[pl.BlockSpec((tm, Q_LORA), row), pl.BlockSpec((tm, LANES), row), pl.BlockSpec((tm, LANES), row),
                   _full((Q_LORA, MLA_W)), _full((KV_LORA, MLA_W)), _full((KV_LORA, vw)),
                   _full((1, Q_LORA)), _full((1, KV_LORA))],
        out_shape=[jax.ShapeDtypeStruct((t, Q_LORA), BF16), jax.ShapeDtypeStruct((t, LANES), BF16),
                   jax.ShapeDtypeStruct((t, LANES), BF16),
                   jax.ShapeDtypeStruct((Q_LORA, MLA_W), F32), jax.ShapeDtypeStruct((KV_LORA, MLA_W), F32),
                   jax.ShapeDtypeStruct((KV_LORA, vw), F32),
                   jax.ShapeDtypeStruct((1, Q_LORA), F32), jax.ShapeDtypeStruct((1, KV_LORA), F32)],
        compiler_params=_params(("arbitrary",)),
    )(dqt, dk, dv, proj, proj, posc, freq, qan, kvan, wq, wk, wv)


def _swa_bwd(proj, d_out, lse, delta, posc, posr, sinks):
    t = proj.shape[0]
    nb = t // BLOCK
    group = N_HEADS_A // N_KV_A

    def body(q_ref, kc_ref, kp_ref, vc_ref, vp_ref, do_ref, l_ref, d_ref, pq_ref, pc_ref, pp_ref, sink_ref,
             dq_ref, dk_ref, dv_ref, ds_ref, dkb_s, dvb_s, dk_carry, dv_carry):
        n = pl.program_id(0)

        @pl.when(n == 0)
        def _():
            ds_ref[...] = jnp.zeros(ds_ref.shape, F32)
            dk_carry[...] = jnp.zeros(dk_carry.shape, F32)
            dv_carry[...] = jnp.zeros(dv_carry.shape, F32)

        @pl.when(n < nb)
        def _():
            kb, vb, dist, valid = _swa_band(n, kp_ref, kc_ref, vp_ref, vc_ref, pq_ref, pp_ref, pc_ref)
            lse_v, del_v = l_ref[...], d_ref[...]
            lane = lax.broadcasted_iota(jnp.int32, (1, LANES), 1)
            dsink = jnp.zeros((1, LANES), F32)
            for kh in range(N_KV_A):
                dk_h = jnp.zeros((2 * BLOCK, HEAD_DIM_A), F32)
                dv_h = jnp.zeros((2 * BLOCK, HEAD_DIM_A), F32)
                for h in range(group * kh, group * (kh + 1)):
                    q_h, k_h, s = _swa_scores(q_ref, kb, dist, valid, h)
                    l_h, d_h = lse_v[:, h:h + 1], del_v[:, h:h + 1]
                    p = jnp.exp(s - l_h)
                    p_sink = jnp.exp(sink_ref[0:1, h:h + 1] - l_h)
                    dsink = jnp.where(lane == h, jnp.sum(-p_sink * d_h, axis=0, keepdims=True), dsink)
                    do_h = do_ref[:, HEAD_DIM_A * h:HEAD_DIM_A * (h + 1)].astype(BF16)
                    dp = _dot_nt(do_h, vb[:, HEAD_DIM_A * kh:HEAD_DIM_A * (kh + 1)])
                    ds = (p * (dp - d_h) * (HEAD_DIM_A ** -0.5)).astype(BF16)
                    dq_ref[:, HEAD_DIM_A * h:HEAD_DIM_A * (h + 1)] = _dot(ds, k_h)
                    dk_h = dk_h + _dot_tn(ds, q_h)
                    dv_h = dv_h + _dot_tn(p.astype(BF16), do_h)
                dkb_s[:, HEAD_DIM_A * kh:HEAD_DIM_A * (kh + 1)] = dk_h
                dvb_s[:, HEAD_DIM_A * kh:HEAD_DIM_A * (kh + 1)] = dv_h
            ds_ref[...] += dsink
            dk_ref[...] = dk_carry[...] + dkb_s[0:BLOCK, :]
            dv_ref[...] = dv_carry[...] + dvb_s[0:BLOCK, :]
            dk_carry[...] = dkb_s[BLOCK:2 * BLOCK, :]
            dv_carry[...] = dvb_s[BLOCK:2 * BLOCK, :]

        @pl.when(n == nb)
        def _():
            dk_ref[...] = dk_carry[...]
            dv_ref[...] = dv_carry[...]

    cur = lambda n: (jnp.minimum(n, nb - 1), 0)
    prv = lambda n: jnp.maximum(jnp.minimum(n, nb - 1) - 1, 0)
    out_prev = lambda n: (jnp.maximum(n - 1, 0), 0)
    return pl.pallas_call(
        body, name="swa_bwd", grid=(nb + 1,),
        in_specs=[pl.BlockSpec((BLOCK, WIDTH_A), lambda n: (jnp.minimum(n, nb - 1), _QA_BLK)),
                  pl.BlockSpec((BLOCK, LANES), lambda n: (jnp.minimum(n, nb - 1), _KA_BLK)),
                  pl.BlockSpec((BLOCK, LANES), lambda n: (prv(n), _KA_BLK)),
                  pl.BlockSpec((BLOCK, LANES), lambda n: (jnp.minimum(n, nb - 1), _VA_BLK)),
                  pl.BlockSpec((BLOCK, LANES), lambda n: (prv(n), _VA_BLK)),
                  pl.BlockSpec((BLOCK, WIDTH_A), cur), pl.BlockSpec((BLOCK, LANES), cur), pl.BlockSpec((BLOCK, LANES), cur),
                  pl.BlockSpec((BLOCK, 1), cur),
                  pl.BlockSpec((1, BLOCK), lambda n: (0, jnp.minimum(n, nb - 1))),
                  pl.BlockSpec((1, BLOCK), lambda n: (0, prv(n))),
                  _full((1, N_HEADS_A))],
        out_specs=[pl.BlockSpec((BLOCK, WIDTH_A), cur), pl.BlockSpec((BLOCK, LANES), out_prev),
                   pl.BlockSpec((BLOCK, LANES), out_prev), _full((1, LANES))],
        out_shape=[jax.ShapeDtypeStruct((t, WIDTH_A), F32), jax.ShapeDtypeStruct((t, LANES), F32),
                   jax.ShapeDtypeStruct((t, LANES), F32), jax.ShapeDtypeStruct((1, LANES), F32)],
        scratch_shapes=[pltpu.VMEM((2 * BLOCK, LANES), F32), pltpu.VMEM((2 * BLOCK, LANES), F32),
                        pltpu.VMEM((BLOCK, LANES), F32), pltpu.VMEM((BLOCK, LANES), F32)],
        compiler_params=_params(("arbitrary",)),
    )(proj, proj, proj, proj, proj, d_out, lse, delta, posc, posr, posr, sinks)


def _in_bwd(dproj, w_in_p, x, dx1, g1, gp):
    t = x.shape[0]
    tm = 256
    steps = t // tm

    def body(dp_ref, w_ref, x_ref, dx1_ref, g_ref, gp_ref, dx_ref, dg_ref, land_ref, send_sems, recv_sems):
        i = pl.program_id(0)

        @pl.when(i == 0)
        def _():
            dg_ref[...] = jnp.zeros(dg_ref.shape, F32)
            _scatter_start(gp_ref, land_ref, send_sems, recv_sems)

        dh = _dot_nt(dp_ref[...], w_ref[...])
        xv = x_ref[...]
        r = _rms(xv)
        dx, dg = _norm_bwd(dh, xv * r, r, g_ref[...])
        dx_ref[...] = dx1_ref[...] + dx
        dg_ref[...] += dg

        @pl.when(i == steps - 1)
        def _():
            _scatter_wait(gp_ref, land_ref, send_sems, recv_sems)

    row = lambda i: (i, 0)
    blk = pl.BlockSpec((tm, D_MODEL), row)
    return pl.pallas_call(
        body, name="in_bwd", grid=(steps,),
        in_specs=[pl.BlockSpec((tm, D_IN_PAD), row), _full((D_MODEL, D_IN_PAD)), blk, blk, _full((1, D_MODEL)), _HBM],
        out_specs=[blk, _full((1, D_MODEL)), _HBM],
        out_shape=[jax.ShapeDtypeStruct((t, D_MODEL), F32), jax.ShapeDtypeStruct((1, D_MODEL), F32),
                   jax.ShapeDtypeStruct((3,) + gp.shape[1:], gp.dtype)],
        scratch_shapes=[pltpu.SemaphoreType.DMA((3,)), pltpu.SemaphoreType.DMA((3,))],
        compiler_params=_params(("arbitrary",)),
    )(dproj, w_in_p, x, dx1, g1, gp)


def _adamw(w, g_parts, m, v, name):
    r = w.shape[0]
    tr = min(r, 128)
    ng = len(g_parts)

    def body(*refs):
        w_ref, g_refs, m_ref, v_ref = refs[0], refs[1:1 + ng], refs[1 + ng], refs[2 + ng]
        g_out, d_out, m_out, v_out = refs[3 + ng:]
        g = g_refs[0][...]
        for gr in g_refs[1:]:
            g = g + gr[...]
        m_new = ADAM_B1 * m_ref[...] + (1.0 - ADAM_B1) * g
        v_new = ADAM_B2 * v_ref[...] + (1.0 - ADAM_B2) * jnp.square(g)
        m_hat = m_new / (1.0 - ADAM_B1 ** ADAM_STEP)
        v_hat = v_new / (1.0 - ADAM_B2 ** ADAM_STEP)
        g_out[...] = g
        d_out[...] = -ADAM_LR * (m_hat / (jnp.sqrt(v_hat) + ADAM_EPS) + ADAM_WD * w_ref[...])
        m_out[...] = m_new
        v_out[...] = v_new

    blk = pl.BlockSpec((tr, D_MODEL), lambda i: (i, 0))
    return pl.pallas_call(
        body, name=name, grid=(r // tr,),
        in_specs=[blk] * (3 + ng), out_specs=[blk] * 4,
        out_shape=[jax.ShapeDtypeStruct((r, D_MODEL), F32)] * 4,
        compiler_params=_params(("parallel",)),
    )(w, *g_parts, m, v)


_HBM = pl.BlockSpec(memory_space=pltpu.HBM)


def _other_chips(x, y):
    return ((1 - x, y), (x, 1 - y), (1 - x, 1 - y))


def _gather_copies(src, out, send_sems, recv_sems, local_sem):
    x, y, c = lax.axis_index("x"), lax.axis_index("y"), lax.axis_index("c")
    me = 2 * x + y
    local = pltpu.make_async_copy(src, out.at[me], local_sem)

    def copies(arriving):
        return [pltpu.make_async_remote_copy(src_ref=src, dst_ref=out.at[2 * px + py if arriving else me],
                                             send_sem=send_sems.at[j], recv_sem=recv_sems.at[j], device_id=(px, py, c),
                                             device_id_type=MESH)
                for j, (px, py) in enumerate(_other_chips(x, y))]

    return local, copies


def _gather_start(src, out, send_sems, recv_sems, local_sem):
    local, copies = _gather_copies(src, out, send_sems, recv_sems, local_sem)
    local.start()
    for cp in copies(False):
        cp.start()


def _gather_wait(src, out, send_sems, recv_sems, local_sem):
    local, copies = _gather_copies(src, out, send_sems, recv_sems, local_sem)
    for cp in copies(True):
        cp.wait_recv()
    for cp in copies(False):
        cp.wait_send()
    local.wait()


def _scatter_copies(src, land, send_sems, recv_sems):
    x, y, c = lax.axis_index("x"), lax.axis_index("y"), lax.axis_index("c")
    return [pltpu.make_async_remote_copy(src_ref=src.at[2 * px + py], dst_ref=land.at[j], send_sem=send_sems.at[j],
                                         recv_sem=recv_sems.at[j], device_id=(px, py, c), device_id_type=MESH)
            for j, (px, py) in enumerate(_other_chips(x, y))]


def _scatter_start(src, land, send_sems, recv_sems):
    for cp in _scatter_copies(src, land, send_sems, recv_sems):
        cp.start()


def _scatter_wait(src, land, send_sems, recv_sems):
    copies = _scatter_copies(src, land, send_sems, recv_sems)
    for cp in copies:
        cp.wait_recv()
    for cp in copies:
        cp.wait_send()


def _all_gather_chips(packed):
    def body(src, out, send_sems, recv_sems, local_sem):
        _gather_start(src, out, send_sems, recv_sems, local_sem)
        _gather_wait(src, out, send_sems, recv_sems, local_sem)

    return pl.pallas_call(
        body, name="ag_weights", in_specs=[_HBM], out_specs=_HBM,
        out_shape=jax.ShapeDtypeStruct((N_CHIPS,) + packed.shape, packed.dtype),
        scratch_shapes=[pltpu.SemaphoreType.DMA((3,)), pltpu.SemaphoreType.DMA((3,)), pltpu.SemaphoreType.DMA(())],
    )(packed)


def _sum4(own, land, name):
    r, w = own.shape
    tr = 128

    def body(o_ref, l_ref, s_ref):
        s_ref[...] = ((o_ref[...] + l_ref[0]) + l_ref[1]) + l_ref[2]

    return pl.pallas_call(
        body, name=name, grid=(r // tr,),
        in_specs=[pl.BlockSpec((tr, w), lambda i: (i, 0)), pl.BlockSpec((3, tr, w), lambda i: (0, i, 0))],
        out_specs=pl.BlockSpec((tr, w), lambda i: (i, 0)),
        out_shape=jax.ShapeDtypeStruct((r, w), F32),
        compiler_params=_params(("parallel",)),
    )(own, land)


def _swap_sibling(s, name):
    def body(src, got, send_sem, recv_sem):
        x, y, c = lax.axis_index("x"), lax.axis_index("y"), lax.axis_index("c")
        cp = pltpu.make_async_remote_copy(src_ref=src, dst_ref=got, send_sem=send_sem, recv_sem=recv_sem,
                                          device_id=(x, y, 1 - c), device_id_type=MESH)
        cp.start()
        cp.wait_recv()
        cp.wait_send()

    return pl.pallas_call(
        body, name=name, in_specs=[_HBM], out_specs=_HBM,
        out_shape=jax.ShapeDtypeStruct(s.shape, s.dtype),
        scratch_shapes=[pltpu.SemaphoreType.DMA(()), pltpu.SemaphoreType.DMA(())],
    )(s)


def _all_reduce_small(part):
    n_dev = 8

    def body(src, out, gath, send_sems, recv_sems):
        x, y, c = lax.axis_index("x"), lax.axis_index("y"), lax.axis_index("c")
        me = 4 * x + 2 * y + c
        gath[me] = src[...]
        peers = []
        for k in range(1, n_dev):
            px = 1 - x if (k >> 2) & 1 else x
            py = 1 - y if (k >> 1) & 1 else y
            pc = 1 - c if k & 1 else c
            peers.append((px, py, pc))
        sends = []
        for j, peer in enumerate(peers):
            cp = pltpu.make_async_remote_copy(src_ref=src, dst_ref=gath.at[me], send_sem=send_sems.at[j],
                                              recv_sem=recv_sems.at[j], device_id=peer, device_id_type=MESH)
            cp.start()
            sends.append(cp)
        for j, (px, py, pc) in enumerate(peers):
            pltpu.make_async_remote_copy(src_ref=src, dst_ref=gath.at[4 * px + 2 * py + pc], send_sem=send_sems.at[j],
                                         recv_sem=recv_sems.at[j], device_id=(px, py, pc), device_id_type=MESH).wait_recv()
        for cp in sends:
            cp.wait_send()
        acc = gath[0]
        for d in range(1, n_dev):
            acc = acc + gath[d]
        out[...] = acc

    vmem = pl.BlockSpec(memory_space=pltpu.VMEM)
    return pl.pallas_call(
        body, name="ar_small", in_specs=[vmem], out_specs=vmem,
        out_shape=jax.ShapeDtypeStruct(part.shape, F32),
        scratch_shapes=[pltpu.VMEM((n_dev,) + part.shape, F32), pltpu.SemaphoreType.DMA((n_dev - 1,)),
                        pltpu.SemaphoreType.DMA((n_dev - 1,))],
    )(part)


def _pad_rows(group):
    rows = sum(PACK_ROWS[n] for n in group)
    return -rows % LANES


def _pack(group, shards, dtype):
    parts = [shards[n].reshape(PACK_ROWS[n], D_MODEL).astype(dtype) for n in group]
    pad = _pad_rows(group)
    if pad:
        parts.append(jnp.zeros((pad, D_MODEL), dtype))
    return jnp.concatenate(parts, axis=0)


def _unpack(group, packed):
    out, off = {}, 0
    for n in group:
        out[n] = packed[off:off + PACK_ROWS[n]].reshape(SHARD_SHAPES[n])
        off += PACK_ROWS[n]
    return out


def _full_weights(group, gathered):
    out, off = {}, 0
    for n in group:
        r, c = SHARD_SHAPES[n]
        g = gathered[:, off:off + PACK_ROWS[n]].reshape(N_CHIPS, r, c)
        off += PACK_ROWS[n]
        out[n] = jnp.transpose(g, (1, 0, 2)).reshape(r, N_CHIPS * c) if n in COL_SHARDED else g.reshape(N_CHIPS * r, c)
    return out


def _shard_major(group, full):
    parts = []
    for n in group:
        r, c = SHARD_SHAPES[n]
        g = full[n]
        g = jnp.transpose(g.reshape(r, N_CHIPS, c), (1, 0, 2)) if n in COL_SHARDED else g.reshape(N_CHIPS, r, c)
        parts.append(g.reshape(N_CHIPS, PACK_ROWS[n], D_MODEL))
    pad = _pad_rows(group)
    if pad:
        parts.append(jnp.zeros((N_CHIPS, pad, D_MODEL), F32))
    return jnp.concatenate(parts, axis=1)


def _pad_layouts(w):
    dt = w["w_in"].dtype
    w_in = w["w_in"]
    z = lambda r, c: jnp.zeros((r, c), dt)
    w_in_p = jnp.concatenate([w_in[:, :3200], z(D_MODEL, 64), w_in[:, 3200:], z(D_MODEL, 32)], axis=1)
    wq = w["w_q_b"].reshape(Q_LORA, N_HEADS_B, Q_HEAD_B)
    wq_p = jnp.concatenate([wq, jnp.zeros((Q_LORA, N_HEADS_B, HEAD_PAD - Q_HEAD_B), dt)], axis=2).reshape(Q_LORA, MLA_W)
    wkv = w["w_kv_b"].reshape(KV_LORA, N_HEADS_B, QK_NOPE + V_DIM_B)
    zk = jnp.zeros((KV_LORA, N_HEADS_B, HEAD_PAD - QK_NOPE), dt)
    wk_p = jnp.concatenate([wkv[:, :, :QK_NOPE], zk], axis=2).reshape(KV_LORA, MLA_W)
    wv = wkv[:, :, QK_NOPE:].reshape(KV_LORA, N_HEADS_B * V_DIM_B)
    return dict(w_in=w_in_p, wq=wq_p, wk=wk_p, wv=wv)


def _unpad_grads(d):
    dw_in = jnp.concatenate([d["w_in"][:, :3200], d["w_in"][:, 3264:3296]], axis=1)
    dwq = d["wq"].reshape(Q_LORA, N_HEADS_B, HEAD_PAD)[:, :, :Q_HEAD_B].reshape(Q_LORA, N_HEADS_B * Q_HEAD_B)
    dwk = d["wk"].reshape(KV_LORA, N_HEADS_B, HEAD_PAD)[:, :, :QK_NOPE]
    dwv = d["wv"].reshape(KV_LORA, N_HEADS_B, V_DIM_B)
    dwkv = jnp.concatenate([dwk, dwv], axis=2).reshape(KV_LORA, N_HEADS_B * (QK_NOPE + V_DIM_B))
    return dict(w_in=dw_in, w_q_b=dwq, w_kv_b=dwkv)


def _rope_freq_lanes():
    freqs = ROPE_THETA ** (-jnp.arange(0, QK_ROPE, 2, dtype=F32) / QK_ROPE)
    return jnp.concatenate([jnp.zeros((QK_NOPE,), F32), freqs, freqs,
                            jnp.zeros((HEAD_PAD - Q_HEAD_B,), F32)]).reshape(1, LANES)


def _fwd_bwd(x, positions, target, w):
    t = x.shape[0]
    small = w
    wp = _pad_layouts(_full_weights(GROUP_A, _all_gather_chips(_pack(GROUP_A, w, BF16))))
    chip = 2 * lax.axis_index("x") + lax.axis_index("y")
    posr = positions.astype(F32).reshape(1, t)
    posc = posr.reshape(t, 1)
    freq = _rope_freq_lanes()
    g1, g2, g3, g4 = small["pre_norm_mix"], small["post_norm_mix"], small["pre_norm_mlp"], small["post_norm_mlp"]
    qan, kvan, sinks = small["q_a_norm"], small["kv_a_norm"], small["sinks"]

    h, proj = _proj_fwd(x, g1, wp["w_in"])
    out_a, lse_a = _swa_fwd(proj, posc, posr, sinks)
    qm, km, qt, kt, vm, vt = _mla_prep_fwd(proj, posc, freq, qan, kvan, wp["wq"], wp["wk"], wp["wv"])
    out_bt, lse_b, gathered_b = _mla_fwd(km, qt, vt, _pack(GROUP_B, w, BF16))
    wb = _full_weights(GROUP_B, gathered_b)
    merged, y, x1, h2 = _mix_out_fwd(out_a, out_bt, proj, x, wb["w_o_a"], wb["w_o_b"], wb["w_out"], g2, g3)
    u, a = _up_fwd(h2, wb["w_up"])
    dx2, dyd, dg4, loss = _down_fwd_loss(a, wb["w_down"], x1, target, g4)

    dwb = {}
    dwb["w_down"] = _matmul_tn(a, dyd, "dw_down", 512, 1024)
    du = _down_bwd(dyd, wb["w_down"], u)
    dwb["w_up"] = _matmul_tn(h2, du, "dw_up", 512, 1024)
    dx1, dy, dg3, dg2 = _up_bwd(du, wb["w_up"], x1, dx2, y, g3, g2)
    dwb["w_out"] = _matmul_tn(merged, dy, "dw_out", 512, 1024)
    doa, dob, dga, dgb, d_out_a, d_out_b, d_out_bt, del_a, del_b = _mix_out_bwd(
        dy, out_a, out_bt, proj, wb["w_o_a"], wb["w_o_b"], wb["w_out"])
    dwb["w_o_a"] = _matmul_tn(out_a, doa, "dw_o_a", 512, 1024)
    dwb["w_o_b"] = _dw_ob(out_bt, dob)
    gp_b = _shard_major(GROUP_B, dwb)
    dqm, dkm, dvm, land_b = _mla_bwd(qm, km, qt, kt, vm, d_out_b, d_out_bt, lse_b, del_b, gp_b)
    dcq, dckv, dkr, dwq, dwk, dwv, dqan, dkvan = _mla_prep_bwd(
        dqm, dkm, dvm, proj, posc, freq, qan, kvan, wp["wq"], wp["wk"], wp["wv"])
    dqa, dka, dva, dsinks = _swa_bwd(proj, d_out_a, lse_a, del_a, posc, posr, sinks)
    dproj = jnp.concatenate([dga, dgb, dqa.astype(BF16), dka.astype(BF16), dva.astype(BF16), dcq, dckv, dkr], axis=1)
    dw_in = _matmul_tn(h, dproj, "dw_in", 512, D_IN_PAD // 2)
    gp_a = _shard_major(GROUP_A, _unpad_grads(dict(w_in=dw_in, wq=dwq, wk=dwk, wv=dwv)))
    grad_x, dg1, land_a = _in_bwd(dproj, wp["w_in"], x, dx1, g1, gp_a)

    own = lambda gp: lax.dynamic_index_in_dim(gp, chip, axis=0, keepdims=False)
    dsmall = dict(pre_norm_mix=dg1, post_norm_mix=dg2, pre_norm_mlp=dg3, post_norm_mlp=dg4,
                  q_a_norm=dqan, kv_a_norm=dkvan, sinks=dsinks[:, :N_HEADS_A])
    return loss, grad_x, {GROUP_A: (own(gp_a), land_a), GROUP_B: (own(gp_b), land_b)}, dsmall


def _pack_small(p, extra=None):
    tail = jnp.concatenate([p["q_a_norm"], p["kv_a_norm"], p["sinks"],
                            jnp.zeros((1, D_MODEL - Q_LORA - KV_LORA - N_HEADS_A), F32)], axis=1)
    scalar = jnp.zeros((1, D_MODEL), F32)
    if extra is not None:
        scalar = scalar.at[0, 0].set(extra)
    return jnp.concatenate([p["pre_norm_mix"], p["post_norm_mix"], p["pre_norm_mlp"], p["post_norm_mlp"], tail, scalar,
                            jnp.zeros((2, D_MODEL), F32)], axis=0)


def _unpack_small(b):
    return dict(pre_norm_mix=b[0:1], post_norm_mix=b[1:2], pre_norm_mlp=b[2:3], post_norm_mlp=b[3:4],
                q_a_norm=b[4:5, :Q_LORA], kv_a_norm=b[4:5, Q_LORA:Q_LORA + KV_LORA],
                sinks=b[4:5, Q_LORA + KV_LORA:Q_LORA + KV_LORA + N_HEADS_A])


def kernel(x, positions, pre_norm_mix, w_in, q_a_norm, w_q_b, kv_a_norm, w_kv_b, sinks, w_o_a, w_o_b, w_out, post_norm_mix, pre_norm_mlp, w_up, w_down, post_norm_mlp, loss_target, m_pre_norm_mix, m_w_in, m_q_a_norm, m_w_q_b, m_kv_a_norm, m_w_kv_b, m_sinks, m_w_o_a, m_w_o_b, m_w_out, m_post_norm_mix, m_pre_norm_mlp, m_w_up, m_w_down, m_post_norm_mlp, v_pre_norm_mix, v_w_in, v_q_a_norm, v_w_q_b, v_kv_a_norm, v_w_kv_b, v_sinks, v_w_o_a, v_w_o_b, v_w_out, v_post_norm_mix, v_pre_norm_mlp, v_w_up, v_w_down, v_post_norm_mlp):
    w = dict(pre_norm_mix=pre_norm_mix, w_in=w_in[0], q_a_norm=q_a_norm, w_q_b=w_q_b[0], kv_a_norm=kv_a_norm,
             w_kv_b=w_kv_b[0], sinks=sinks, w_o_a=w_o_a[0], w_o_b=w_o_b[0], w_out=w_out[0],
             post_norm_mix=post_norm_mix, pre_norm_mlp=pre_norm_mlp, w_up=w_up[0], w_down=w_down[0],
             post_norm_mlp=post_norm_mlp)
    m = dict(pre_norm_mix=m_pre_norm_mix, w_in=m_w_in[0], q_a_norm=m_q_a_norm, w_q_b=m_w_q_b[0],
             kv_a_norm=m_kv_a_norm, w_kv_b=m_w_kv_b[0], sinks=m_sinks, w_o_a=m_w_o_a[0], w_o_b=m_w_o_b[0],
             w_out=m_w_out[0], post_norm_mix=m_post_norm_mix, pre_norm_mlp=m_pre_norm_mlp, w_up=m_w_up[0],
             w_down=m_w_down[0], post_norm_mlp=m_post_norm_mlp)
    v = dict(pre_norm_mix=v_pre_norm_mix, w_in=v_w_in[0], q_a_norm=v_q_a_norm, w_q_b=v_w_q_b[0],
             kv_a_norm=v_kv_a_norm, w_kv_b=v_w_kv_b[0], sinks=v_sinks, w_o_a=v_w_o_a[0], w_o_b=v_w_o_b[0],
             w_out=v_w_out[0], post_norm_mix=v_post_norm_mix, pre_norm_mlp=v_pre_norm_mlp, w_up=v_w_up[0],
             w_down=v_w_down[0], post_norm_mlp=v_post_norm_mlp)

    loss, grad_x, blocks, dsmall = _fwd_bwd(x[0], positions, loss_target[0], w)

    red = _all_reduce_small(_pack_small(dsmall, loss[0, 0]))
    results = [_adamw(_pack_small(w), [red], _pack_small(m), _pack_small(v), "adamw_small")]
    for group, tag in ((GROUP_A, "a"), (GROUP_B, "b")):
        own, land = blocks[group]
        part = _sum4(own, land, "rs_sum_" + tag)
        results.append(_adamw(_pack(group, w, F32), [part, _swap_sibling(part, "rs_swap_" + tag)],
                              _pack(group, m, F32), _pack(group, v, F32), "adamw_" + tag))

    outs = []
    for k in range(4):
        by_name = dict(_unpack_small(results[0][k]))
        by_name.update({n: a[None] for n, a in _unpack(GROUP_A, results[1][k]).items()})
        by_name.update({n: a[None] for n, a in _unpack(GROUP_B, results[2][k]).items()})
        outs.extend(by_name[n] for n in WEIGHTS)
    return (red[5, 0], grad_x[None], *outs)
```

```python
import functools

import jax
import jax.numpy as jnp
from jax import lax
from jax.experimental import pallas as pl
from jax.experimental.pallas import tpu as pltpu

F32 = jnp.float32
BF16 = jnp.bfloat16
MESH = pl.DeviceIdType.MESH

D_MODEL = 1024
N_HEADS_A = 8
N_KV_A = 2
HEAD_DIM_A = 64
WINDOW = 128
BLOCK = 128
N_HEADS_B = 8
QK_NOPE = 64
QK_ROPE = 32
V_DIM_B = 64
Q_LORA = 256
KV_LORA = 128
ROPE_THETA = 10000.0
D_FF = 4 * D_MODEL
EPS = 1e-6
WIDTH_A = N_HEADS_A * HEAD_DIM_A
Q_HEAD_B = QK_NOPE + QK_ROPE
D_IN = 3232
D_IN_PAD = 3328
HEAD_PAD = 128
MLA_W = N_HEADS_B * HEAD_PAD

ADAM_LR = 0.001
ADAM_B1 = 0.9
ADAM_B2 = 0.999
ADAM_EPS = 1e-08
ADAM_WD = 0.01
ADAM_STEP = 10

NEG = -1e30
N_CHIPS = 4
LANES = 128
VMEM_LIMIT = 56 * 1024 * 1024

BIG = ("w_in", "w_q_b", "w_kv_b", "w_o_a", "w_o_b", "w_out", "w_up", "w_down")
KEPT_SHARD_MAJOR = ("w_up",)
COL_SHARDED = ("w_in", "w_q_b", "w_kv_b", "w_o_a", "w_o_b", "w_up")
SHARD_SHAPES = {"w_in": (1024, 808), "w_q_b": (256, 192), "w_kv_b": (128, 256), "w_o_a": (512, 256),
                "w_o_b": (512, 256), "w_out": (256, 1024), "w_up": (1024, 1024), "w_down": (1024, 1024)}
PACK_ROWS = {n: (s[0] * s[1]) // D_MODEL for n, s in SHARD_SHAPES.items()}
GROUP_A = ("w_in", "w_q_b", "w_kv_b")
GROUP_B = ("w_o_a", "w_o_b", "w_out", "w_up", "w_down")
SMALL = ("pre_norm_mix", "post_norm_mix", "pre_norm_mlp", "post_norm_mlp", "q_a_norm", "kv_a_norm", "sinks")
WEIGHTS = ("pre_norm_mix", "w_in", "q_a_norm", "w_q_b", "kv_a_norm", "w_kv_b", "sinks", "w_o_a", "w_o_b", "w_out",
           "post_norm_mix", "pre_norm_mlp", "w_up", "w_down", "post_norm_mlp")


def _params(sem=None):
    return pltpu.CompilerParams(dimension_semantics=sem, vmem_limit_bytes=VMEM_LIMIT)


def _dot(a, b):
    return jnp.dot(a, b, preferred_element_type=F32)


def _dot_nt(a, b):
    return lax.dot_general(a, b, (((1,), (1,)), ((), ())), preferred_element_type=F32)


def _dot_tn(a, b):
    return lax.dot_general(a, b, (((0,), (0,)), ((), ())), preferred_element_type=F32)


def _rms(v):
    return lax.rsqrt(jnp.mean(v * v, axis=-1, keepdims=True) + EPS)


def _norm_bwd(dout, n, r, g):
    dn = dout * g
    dx = r * (dn - n * jnp.mean(dn * n, axis=-1, keepdims=True))
    return dx, jnp.sum(dout * n, axis=0, keepdims=True)


def _full(shape):
    return pl.BlockSpec(shape, lambda *_: (0,) * len(shape))


def _proj_fwd(x, g1, w_in_p):
    t = x.shape[0]
    tm = 256

    def body(x_ref, g_ref, w_ref, h_ref, p_ref):
        xv = x_ref[...]
        h = ((xv * _rms(xv)) * g_ref[...]).astype(BF16)
        h_ref[...] = h
        p_ref[...] = _dot(h, w_ref[...])

    return pl.pallas_call(
        body, name="proj_fwd", grid=(t // tm,),
        in_specs=[pl.BlockSpec((tm, D_MODEL), lambda i: (i, 0)), _full((1, D_MODEL)), _full((D_MODEL, D_IN_PAD))],
        out_specs=[pl.BlockSpec((tm, D_MODEL), lambda i: (i, 0)), pl.BlockSpec((tm, D_IN_PAD), lambda i: (i, 0))],
        out_shape=[jax.ShapeDtypeStruct((t, D_MODEL), BF16), jax.ShapeDtypeStruct((t, D_IN_PAD), F32)],
        compiler_params=_params(("parallel",)),
    )(x, g1, w_in_p)


_QA_BLK = 2048 // WIDTH_A
_KA_BLK = 2560 // LANES
_VA_BLK = 2688 // LANES
_CQ_BLK = 2816 // Q_LORA
_CKV_BLK = 3072 // LANES
_KR_BLK = 3200 // LANES


_GROUP_A = N_HEADS_A // N_KV_A
_SWA_SCALE = HEAD_DIM_A ** -0.5


def _head_cols(v, h):
    return v[:, HEAD_DIM_A * h:HEAD_DIM_A * (h + 1)]


def _head_rows(v, h):
    return v[HEAD_DIM_A * h:HEAD_DIM_A * (h + 1), :]


def _swa_band(n, kp_ref, kc_ref, vp_ref, vc_ref, pq_ref, pp_ref, pc_ref):
    kb = jnp.concatenate([kp_ref[...], kc_ref[...]], axis=0)
    vb = jnp.concatenate([vp_ref[...], vc_ref[...]], axis=0)
    posk = jnp.concatenate([pp_ref[...], pc_ref[...]], axis=0)
    dist = jnp.abs(posk - pq_ref[...])
    ki = lax.broadcasted_iota(jnp.int32, (2 * BLOCK, BLOCK), 0)
    qi = lax.broadcasted_iota(jnp.int32, (2 * BLOCK, BLOCK), 1)
    valid = (ki > qi) & (ki <= qi + WINDOW) & ((n > 0) | (ki >= BLOCK))
    return kb, vb, dist, valid


def _swa_scores_t(st_g, j, h, dist, valid):
    slope = 2.0 ** (-8.0 * (h + 1) / N_HEADS_A)
    st = st_g[:, BLOCK * j:BLOCK * (j + 1)] * _SWA_SCALE - slope * dist
    return jnp.where(valid, st, NEG)


def _group_t(xt, kh):
    return jnp.concatenate([_head_rows(xt, _GROUP_A * kh + j) for j in range(_GROUP_A)], axis=1).astype(BF16)


def _swa_fwd(proj, posc, posr, sinks):
    t = proj.shape[0]
    nb = t // BLOCK

    def body(q_ref, kc_ref, kp_ref, vc_ref, vp_ref, pq_ref, pc_ref, pp_ref, sink_ref, o_ref, l_ref):
        n = pl.program_id(0)
        kb, vb, dist, valid = _swa_band(n, kp_ref, kc_ref, vp_ref, vc_ref, pq_ref, pp_ref, pc_ref)
        q_t, vb_t = q_ref[...].T, vb.T
        out_t, lse = [], []
        for kh in range(N_KV_A):
            st_g = _dot(_head_cols(kb, kh).astype(BF16), _group_t(q_t, kh))
            ps = []
            for j in range(_GROUP_A):
                h = _GROUP_A * kh + j
                st = _swa_scores_t(st_g, j, h, dist, valid)
                sink = sink_ref[0:1, h:h + 1]
                m = jnp.maximum(jnp.max(st, axis=0, keepdims=True), sink)
                e = jnp.exp(st - m)
                den = jnp.sum(e, axis=0, keepdims=True) + jnp.exp(sink - m)
                ps.append((e / den).astype(BF16))
                lse.append(m + jnp.log(den))
            o_g = _dot(_head_rows(vb_t, kh).astype(BF16), jnp.concatenate(ps, axis=1))
            out_t.extend(o_g[:, BLOCK * j:BLOCK * (j + 1)] for j in range(_GROUP_A))
        o_ref[...] = jnp.concatenate(out_t, axis=0).T
        l_ref[...] = jnp.concatenate(lse, axis=0)

    cur = lambda n: (n, 0)
    prev = lambda n: jnp.maximum(n - 1, 0)
    return pl.pallas_call(
        body, name="swa_fwd", grid=(nb,),
        in_specs=[pl.BlockSpec((BLOCK, WIDTH_A), lambda n: (n, _QA_BLK)),
                  pl.BlockSpec((BLOCK, LANES), lambda n: (n, _KA_BLK)),
                  pl.BlockSpec((BLOCK, LANES), lambda n: (prev(n), _KA_BLK)),
                  pl.BlockSpec((BLOCK, LANES), lambda n: (n, _VA_BLK)),
                  pl.BlockSpec((BLOCK, LANES), lambda n: (prev(n), _VA_BLK)),
                  pl.BlockSpec((1, BLOCK), lambda n: (0, n)),
                  pl.BlockSpec((BLOCK, 1), cur),
                  pl.BlockSpec((BLOCK, 1), lambda n: (prev(n), 0)),
                  _full((1, N_HEADS_A))],
        out_specs=[pl.BlockSpec((BLOCK, WIDTH_A), cur), pl.BlockSpec((N_HEADS_A, BLOCK), lambda n: (0, n))],
        out_shape=[jax.ShapeDtypeStruct((t, WIDTH_A), F32), jax.ShapeDtypeStruct((N_HEADS_A, t), F32)],
        compiler_params=_params(("parallel",)),
    )(proj, proj, proj, proj, proj, posr, posc, posc, sinks)


def _rope_coeffs(pos, freq):
    ang = pos * freq
    cosv, sinv = jnp.cos(ang), jnp.sin(ang)
    lane = lax.broadcasted_iota(jnp.int32, ang.shape, 1)
    lo = (lane >= QK_NOPE) & (lane < QK_NOPE + QK_ROPE // 2)
    hi = (lane >= QK_NOPE + QK_ROPE // 2) & (lane < QK_NOPE + QK_ROPE)
    c = jnp.where(lane < QK_NOPE, 1.0, jnp.where(lo | hi, cosv, 0.0))
    s = jnp.where(lo, -sinv, jnp.where(hi, sinv, 0.0))
    return c, s, lo, hi


def _rope(xh, c, s, lo):
    up = pltpu.roll(xh, LANES - QK_ROPE // 2, axis=1)
    dn = pltpu.roll(xh, QK_ROPE // 2, axis=1)
    return xh * c + jnp.where(lo, up, dn) * s


def _unrope(dh, c, s, lo, hi):
    g = dh * s
    up = pltpu.roll(g, LANES - QK_ROPE // 2, axis=1)
    dn = pltpu.roll(g, QK_ROPE // 2, axis=1)
    return dh * c + jnp.where(hi, dn, jnp.where(lo, up, 0.0))


_TQ = 512
_MLA_SCALE = Q_HEAD_B ** -0.5


def _mla_prep_fwd(proj, posc, freq, qan, kvan, wq, wk, wv):
    t = proj.shape[0]
    tm = _TQ
    nb = t // tm

    def body(cq_ref, ckv_ref, kr_ref, pos_ref, f_ref, qan_ref, kvan_ref, wq_ref, wk_ref, wv_ref,
             q_ref, k_ref, qt_ref, kt_ref, v_ref, vt_ref):
        cq = cq_ref[...]
        cqn = ((cq * _rms(cq)) * qan_ref[...]).astype(BF16)
        ckv = ckv_ref[...]
        ckvn = ((ckv * _rms(ckv)) * kvan_ref[...]).astype(BF16)
        qb = _dot(cqn, wq_ref[...])
        kb = _dot(ckvn, wk_ref[...])
        vb = _dot(ckvn, wv_ref[...])
        vbt = vb.T
        c, s, lo, _ = _rope_coeffs(pos_ref[...], f_ref[...])
        kr = _rope(kr_ref[...], c, s, lo)
        for h in range(N_HEADS_B):
            sl = slice(HEAD_PAD * h, HEAD_PAD * (h + 1))
            q_h = _rope(qb[:, sl], c, s, lo)
            k_h = kb[:, sl] + kr
            q_ref[:, sl] = q_h.astype(BF16)
            k_ref[:, sl] = k_h.astype(BF16)
            qt_ref[h, 0] = q_h.T.astype(BF16)
            kt_ref[h, 0] = k_h.T.astype(BF16)
            v_ref[h] = vb[:, V_DIM_B * h:V_DIM_B * (h + 1)].astype(BF16)
            vt_ref[h, 0] = vbt[V_DIM_B * h:V_DIM_B * (h + 1), :].astype(BF16)

    row = lambda i: (i, 0)
    blk4 = lambda d: pl.BlockSpec((N_HEADS_B, 1, d, tm), lambda i: (0, i, 0, 0))
    return pl.pallas_call(
        body, name="mla_prep_fwd", grid=(nb,),
        in_specs=[pl.BlockSpec((tm, Q_LORA), lambda i: (i, _CQ_BLK)),
                  pl.BlockSpec((tm, LANES), lambda i: (i, _CKV_BLK)),
                  pl.BlockSpec((tm, LANES), lambda i: (i, _KR_BLK)),
                  pl.BlockSpec((tm, 1), row), _full((1, LANES)), _full((1, Q_LORA)), _full((1, KV_LORA)),
                  _full((Q_LORA, MLA_W)), _full((KV_LORA, MLA_W)), _full((KV_LORA, N_HEADS_B * V_DIM_B))],
        out_specs=[pl.BlockSpec((tm, MLA_W), row), pl.BlockSpec((tm, MLA_W), row), blk4(HEAD_PAD), blk4(HEAD_PAD),
                   pl.BlockSpec((N_HEADS_B, tm, V_DIM_B), lambda i: (0, i, 0)), blk4(V_DIM_B)],
        out_shape=[jax.ShapeDtypeStruct((t, MLA_W), BF16), jax.ShapeDtypeStruct((t, MLA_W), BF16),
                   jax.ShapeDtypeStruct((N_HEADS_B, nb, HEAD_PAD, tm), BF16),
                   jax.ShapeDtypeStruct((N_HEADS_B, nb, HEAD_PAD, tm), BF16),
                   jax.ShapeDtypeStruct((N_HEADS_B, t, V_DIM_B), BF16),
                   jax.ShapeDtypeStruct((N_HEADS_B, nb, V_DIM_B, tm), BF16)],
        compiler_params=_params(("parallel",)),
    )(proj, proj, proj, posc, freq, qan, kvan, wq, wk, wv)


def _scores_t(k, qt, diagonal):
    st = _dot(k, qt) * _MLA_SCALE
    if diagonal:
        key = lax.broadcasted_iota(jnp.int32, st.shape, 0)
        qry = lax.broadcasted_iota(jnp.int32, st.shape, 1)
        st = jnp.where(key <= qry, st, NEG)
    return st


def _mla_fwd(k, qt, vt, w_src):
    t = k.shape[0]
    nb = t // _TQ

    def body(k_ref, qt_ref, vt_ref, w_ref, o_ref, l_ref, wg_ref, send_sems, recv_sems, local_sem):
        qi = pl.program_id(1)
        first = (pl.program_id(0) == 0) & (qi == 0)
        last = (pl.program_id(0) == N_HEADS_B - 1) & (qi == nb - 1)

        @pl.when(first)
        def _():
            _gather_start(w_ref, wg_ref, send_sems, recv_sems, local_sem)

        q_t = qt_ref[0, 0]

        def block(kj, carry, diagonal):
            m, l, acc = carry
            st = _scores_t(k_ref[pl.ds(pl.multiple_of(kj * _TQ, _TQ), _TQ), :], q_t, diagonal)
            m_new = jnp.maximum(m, jnp.max(st, axis=0, keepdims=True))
            alpha = jnp.exp(m - m_new)
            p = jnp.exp(st - m_new)
            l = alpha * l + jnp.sum(p, axis=0, keepdims=True)
            acc = alpha * acc + _dot(vt_ref[0, kj], p.astype(BF16))
            return m_new, l, acc

        init = (jnp.full((1, _TQ), NEG, F32), jnp.zeros((1, _TQ), F32), jnp.zeros((V_DIM_B, _TQ), F32))
        carry = lax.fori_loop(0, qi, lambda kj, cr: block(kj, cr, False), init)
        m, l, acc = block(qi, carry, True)
        o_ref[0, 0] = acc / l
        l_ref[0, 0] = m + jnp.log(l)

        @pl.when(last)
        def _():
            _gather_wait(w_ref, wg_ref, send_sems, recv_sems, local_sem)

    return pl.pallas_call(
        body, name="mla_fwd", grid=(N_HEADS_B, nb),
        in_specs=[pl.BlockSpec((t, HEAD_PAD), lambda h, qi: (0, h)),
                  pl.BlockSpec((1, 1, HEAD_PAD, _TQ), lambda h, qi: (h, qi, 0, 0)),
                  pl.BlockSpec((1, nb, V_DIM_B, _TQ), lambda h, qi: (h, 0, 0, 0)), _HBM],
        out_specs=[pl.BlockSpec((1, 1, V_DIM_B, _TQ), lambda h, qi: (h, qi, 0, 0)),
                   pl.BlockSpec((1, 1, 1, _TQ), lambda h, qi: (h, qi, 0, 0)), _HBM],
        out_shape=[jax.ShapeDtypeStruct((N_HEADS_B, nb, V_DIM_B, _TQ), F32),
                   jax.ShapeDtypeStruct((N_HEADS_B, nb, 1, _TQ), F32),
                   jax.ShapeDtypeStruct((N_CHIPS,) + w_src.shape, w_src.dtype)],
        scratch_shapes=[pltpu.SemaphoreType.DMA((3,)), pltpu.SemaphoreType.DMA((3,)), pltpu.SemaphoreType.DMA(())],
        compiler_params=_params(("arbitrary", "arbitrary")),
    )(k, qt, vt, w_src)


def _ot_spec(tm, d):
    per = _TQ // tm
    return pl.BlockSpec((N_HEADS_B, 1, d, tm), lambda i: (0, i // per, 0, i % per))


def _mix_out_fwd(out_a, out_bt, proj, x, w_oa, w_ob, w_out, g2, g3):
    t = x.shape[0]
    tm = 256

    def body(oa_ref, obt_ref, ga_ref, gb_ref, x_ref, woa_ref, wob_ref, wout_ref, g2_ref, g3_ref,
             mg_ref, y_ref, x1_ref, h2_ref):
        oa = _dot(oa_ref[...].astype(BF16), woa_ref[...])
        obt = obt_ref[...].reshape(N_HEADS_B * V_DIM_B, tm).astype(BF16)
        ob = _dot_tn(obt, wob_ref[...])
        merged = (jax.nn.sigmoid(ga_ref[...]) * oa + jax.nn.sigmoid(gb_ref[...]) * ob).astype(BF16)
        mg_ref[...] = merged
        y = _dot(merged, wout_ref[...])
        y_ref[...] = y
        x1 = x_ref[...] + (y * _rms(y)) * g2_ref[...]
        x1_ref[...] = x1
        h2_ref[...] = ((x1 * _rms(x1)) * g3_ref[...]).astype(BF16)

    row = lambda i: (i, 0)
    blk = pl.BlockSpec((tm, D_MODEL), row)
    return pl.pallas_call(
        body, name="mix_out_fwd", grid=(t // tm,),
        in_specs=[pl.BlockSpec((tm, WIDTH_A), row), _ot_spec(tm, V_DIM_B), pl.BlockSpec((tm, D_MODEL), lambda i: (i, 0)),
                  pl.BlockSpec((tm, D_MODEL), lambda i: (i, 1)), blk,
                  _full((WIDTH_A, D_MODEL)), _full((N_HEADS_B * V_DIM_B, D_MODEL)), _full((D_MODEL, D_MODEL)),
                  _full((1, D_MODEL)), _full((1, D_MODEL))],
        out_specs=[blk, blk, blk, blk],
        out_shape=[jax.ShapeDtypeStruct((t, D_MODEL), BF16), jax.ShapeDtypeStruct((t, D_MODEL), F32),
                   jax.ShapeDtypeStruct((t, D_MODEL), F32), jax.ShapeDtypeStruct((t, D_MODEL), BF16)],
        compiler_params=_params(("parallel",)),
    )(out_a, out_bt, proj, proj, x, w_oa, w_ob, w_out, g2, g3)


_TM_MLP = 256


def _up_fwd(h2, w_up):
    t = h2.shape[0]
    tm = _TM_MLP

    def body(h_ref, w_ref, a_ref):
        hv = h_ref[...]
        for j in range(N_CHIPS):
            u = _dot(hv, w_ref[j])
            a_ref[:, D_MODEL * j:D_MODEL * (j + 1)] = jnp.square(jnp.maximum(u, 0.0)).astype(BF16)

    return pl.pallas_call(
        body, name="up_fwd", grid=(t // tm,),
        in_specs=[pl.BlockSpec((tm, D_MODEL), lambda i: (i, 0)), _full((N_CHIPS, D_MODEL, D_MODEL))],
        out_specs=pl.BlockSpec((tm, D_FF), lambda i: (i, 0)),
        out_shape=jax.ShapeDtypeStruct((t, D_FF), BF16),
        compiler_params=_params(("parallel",)),
    )(h2, w_up)


def _down_fwd_loss(a, w_down, x1, target, g4):
    t = a.shape[0]
    tm = _TM_MLP

    def body(a_ref, w_ref, x1_ref, tg_ref, g_ref, dx2_ref, dyd_ref, dg_ref, loss_ref):
        @pl.when(pl.program_id(0) == 0)
        def _():
            dg_ref[...] = jnp.zeros(dg_ref.shape, F32)
            loss_ref[...] = jnp.zeros(loss_ref.shape, F32)

        yd = _dot(a_ref[...], w_ref[...])
        r = _rms(yd)
        n = yd * r
        diff = (x1_ref[...] + n * g_ref[...]) - tg_ref[...]
        loss_ref[...] += 0.5 * jnp.sum(jnp.mean(diff * diff, axis=-1, keepdims=True), axis=0, keepdims=True)
        dx2 = diff * (1.0 / D_MODEL)
        dx2_ref[...] = dx2
        dyd, dg = _norm_bwd(dx2, n, r, g_ref[...])
        dyd_ref[...] = dyd.astype(BF16)
        dg_ref[...] += dg

    row = lambda i: (i, 0)
    blk = pl.BlockSpec((tm, D_MODEL), row)
    return pl.pallas_call(
        body, name="down_fwd_loss", grid=(t // tm,),
        in_specs=[pl.BlockSpec((tm, D_FF), row), _full((D_FF, D_MODEL)), blk, blk, _full((1, D_MODEL))],
        out_specs=[blk, blk, _full((1, D_MODEL)), _full((1, LANES))],
        out_shape=[jax.ShapeDtypeStruct((t, D_MODEL), F32), jax.ShapeDtypeStruct((t, D_MODEL), BF16),
                   jax.ShapeDtypeStruct((1, D_MODEL), F32), jax.ShapeDtypeStruct((1, LANES), F32)],
        compiler_params=_params(("arbitrary",)),
    )(a, w_down, x1, target, g4)


def _matmul_tn(a, b, name, tm, tn, tk=1024, shard_major=False):
    t, m = a.shape
    n = b.shape[1]
    tk = min(tk, t)
    nk = t // tk

    def body(a_ref, b_ref, o_ref):
        @pl.when(pl.program_id(2) == 0)
        def _():
            o_ref[...] = jnp.zeros(o_ref.shape, F32)

        acc = _dot_tn(a_ref[...].astype(BF16), b_ref[...].astype(BF16))
        o_ref[...] += acc[None] if shard_major else acc

    if shard_major:
        out_spec = pl.BlockSpec((1, tm, tn), lambda i, j, k: (j, i, 0))
        out_shape = jax.ShapeDtypeStruct((n // tn, m, tn), F32)
    else:
        out_spec = pl.BlockSpec((tm, tn), lambda i, j, k: (i, j))
        out_shape = jax.ShapeDtypeStruct((m, n), F32)
    return pl.pallas_call(
        body, name=name, grid=(m // tm, n // tn, nk),
        in_specs=[pl.BlockSpec((tk, tm), lambda i, j, k: (k, i)), pl.BlockSpec((tk, tn), lambda i, j, k: (k, j))],
        out_specs=out_spec, out_shape=out_shape,
        compiler_params=_params(("parallel", "parallel", "arbitrary")),
    )(a, b)


def _down_bwd(dyd, w_down, a):
    t = dyd.shape[0]
    tm = _TM_MLP

    def body(d_ref, w_ref, a_ref, du_ref):
        da = _dot_nt(d_ref[...], w_ref[...])
        du_ref[...] = (da * (2.0 * jnp.sqrt(a_ref[...].astype(F32)))).astype(BF16)

    row = lambda i: (i, 0)
    return pl.pallas_call(
        body, name="down_bwd", grid=(t // tm,),
        in_specs=[pl.BlockSpec((tm, D_MODEL), row), _full((D_FF, D_MODEL)), pl.BlockSpec((tm, D_FF), row)],
        out_specs=pl.BlockSpec((tm, D_FF), row),
        out_shape=jax.ShapeDtypeStruct((t, D_FF), BF16),
        compiler_params=_params(("parallel",)),
    )(dyd, w_down, a)


def _up_bwd(du, w_up, x1, dx2, y, g3, g2):
    t = du.shape[0]
    tm = _TM_MLP

    def body(du_ref, w_ref, x1_ref, dx2_ref, y_ref, g3_ref, g2_ref, dx1_ref, dy_ref, dg3_ref, dg2_ref):
        @pl.when(pl.program_id(0) == 0)
        def _():
            dg3_ref[...] = jnp.zeros(dg3_ref.shape, F32)
            dg2_ref[...] = jnp.zeros(dg2_ref.shape, F32)

        dh2 = _dot_nt(du_ref[:, 0:D_MODEL], w_ref[0])
        for j in range(1, N_CHIPS):
            dh2 = dh2 + _dot_nt(du_ref[:, D_MODEL * j:D_MODEL * (j + 1)], w_ref[j])
        x1 = x1_ref[...]
        r3 = _rms(x1)
        d3, dg3 = _norm_bwd(dh2, x1 * r3, r3, g3_ref[...])
        dx1 = dx2_ref[...] + d3
        dx1_ref[...] = dx1
        dg3_ref[...] += dg3
        y = y_ref[...]
        r2 = _rms(y)
        dy, dg2 = _norm_bwd(dx1, y * r2, r2, g2_ref[...])
        dy_ref[...] = dy.astype(BF16)
        dg2_ref[...] += dg2

    row = lambda i: (i, 0)
    blk = pl.BlockSpec((tm, D_MODEL), row)
    return pl.pallas_call(
        body, name="up_bwd", grid=(t // tm,),
        in_specs=[pl.BlockSpec((tm, D_FF), row), _full((N_CHIPS, D_MODEL, D_MODEL)),
                  blk, blk, blk, _full((1, D_MODEL)), _full((1, D_MODEL))],
        out_specs=[blk, blk, _full((1, D_MODEL)), _full((1, D_MODEL))],
        out_shape=[jax.ShapeDtypeStruct((t, D_MODEL), F32), jax.ShapeDtypeStruct((t, D_MODEL), BF16),
                   jax.ShapeDtypeStruct((1, D_MODEL), F32), jax.ShapeDtypeStruct((1, D_MODEL), F32)],
        compiler_params=_params(("arbitrary",)),
    )(du, w_up, x1, dx2, y, g3, g2)


def _mix_out_bwd(dy, out_a, out_bt, proj, w_oa, w_ob, w_out):
    t = dy.shape[0]
    tm = 256
    nb = t // _TQ

    def body(dy_ref, oa_ref, obt_ref, ga_ref, gb_ref, woa_ref, wob_ref, wout_ref,
             doa_ref, dob_ref, dga_ref, dgb_ref, da_ref, db_ref, dbt_ref, dela_ref, delb_ref):
        dm = _dot_nt(dy_ref[...], wout_ref[...])
        out_a_v = oa_ref[...]
        out_bt_v = obt_ref[...].reshape(N_HEADS_B * V_DIM_B, tm)
        oa = _dot(out_a_v.astype(BF16), woa_ref[...])
        ob = _dot_tn(out_bt_v.astype(BF16), wob_ref[...])
        sa, sb = jax.nn.sigmoid(ga_ref[...]), jax.nn.sigmoid(gb_ref[...])
        doa = (dm * sa).astype(BF16)
        dob = (dm * sb).astype(BF16)
        doa_ref[...] = doa
        dob_ref[...] = dob
        dga_ref[...] = (dm * oa * (sa * (1.0 - sa))).astype(BF16)
        dgb_ref[...] = (dm * ob * (sb * (1.0 - sb))).astype(BF16)
        d_out_a = _dot_nt(doa, woa_ref[...])
        da_ref[...] = d_out_a
        prod_at = (d_out_a * out_a_v).T
        dela_ref[...] = jnp.concatenate(
            [jnp.sum(_head_rows(prod_at, h), axis=0, keepdims=True) for h in range(N_HEADS_A)], axis=0)
        d_out_b = _dot_nt(dob, wob_ref[...])
        d_out_bt = _dot_nt(wob_ref[...], dob)
        prod_bt = d_out_bt * out_bt_v
        for h in range(N_HEADS_B):
            db_ref[h] = d_out_b[:, V_DIM_B * h:V_DIM_B * (h + 1)].astype(BF16)
            dbt_ref[h, 0] = d_out_bt[V_DIM_B * h:V_DIM_B * (h + 1), :].astype(BF16)
            delb_ref[h, 0] = jnp.sum(prod_bt[V_DIM_B * h:V_DIM_B * (h + 1), :], axis=0, keepdims=True)

    row = lambda i: (i, 0)
    blk = pl.BlockSpec((tm, D_MODEL), row)
    return pl.pallas_call(
        body, name="mix_out_bwd", grid=(t // tm,),
        in_specs=[blk, pl.BlockSpec((tm, WIDTH_A), row), _ot_spec(tm, V_DIM_B), pl.BlockSpec((tm, D_MODEL), lambda i: (i, 0)),
                  pl.BlockSpec((tm, D_MODEL), lambda i: (i, 1)),
                  _full((WIDTH_A, D_MODEL)), _full((N_HEADS_B * V_DIM_B, D_MODEL)), _full((D_MODEL, D_MODEL))],
        out_specs=[blk, blk, blk, blk, pl.BlockSpec((tm, WIDTH_A), row),
                   pl.BlockSpec((N_HEADS_B, tm, V_DIM_B), lambda i: (0, i, 0)), _ot_spec(tm, V_DIM_B),
                   pl.BlockSpec((N_HEADS_A, tm), lambda i: (0, i)), _ot_spec(tm, 1)],
        out_shape=[jax.ShapeDtypeStruct((t, D_MODEL), BF16)] * 4
        + [jax.ShapeDtypeStruct((t, WIDTH_A), F32), jax.ShapeDtypeStruct((N_HEADS_B, t, V_DIM_B), BF16),
           jax.ShapeDtypeStruct((N_HEADS_B, nb, V_DIM_B, _TQ), BF16),
           jax.ShapeDtypeStruct((N_HEADS_A, t), F32), jax.ShapeDtypeStruct((N_HEADS_B, nb, 1, _TQ), F32)],
        compiler_params=_params(("parallel",)),
    )(dy, out_a, out_bt, proj, proj, w_oa, w_ob, w_out)


def _dw_ob(out_bt, dob):
    t = dob.shape[0]
    nb = t // _TQ

    def body(obt_ref, dob_ref, o_ref):
        @pl.when(pl.program_id(0) == 0)
        def _():
            o_ref[...] = jnp.zeros(o_ref.shape, F32)

        obt = obt_ref[...].reshape(N_HEADS_B * V_DIM_B, _TQ).astype(BF16)
        o_ref[...] += _dot(obt, dob_ref[...])

    return pl.pallas_call(
        body, name="dw_o_b", grid=(nb,),
        in_specs=[pl.BlockSpec((N_HEADS_B, 1, V_DIM_B, _TQ), lambda i: (0, i, 0, 0)),
                  pl.BlockSpec((_TQ, D_MODEL), lambda i: (i, 0))],
        out_specs=_full((N_HEADS_B * V_DIM_B, D_MODEL)),
        out_shape=jax.ShapeDtypeStruct((N_HEADS_B * V_DIM_B, D_MODEL), F32),
        compiler_params=_params(("arbitrary",)),
    )(out_bt, dob)


def _mla_bwd(q, k, qt, kt, v, d_out, d_out_t, lse, delta, gp):
    t = q.shape[0]
    nb = t // _TQ

    def body(k_ref, kt_ref, v_ref, q_ref, qt_ref, do_ref, dot_ref, l_ref, d_ref, gp_ref,
             dqt_ref, dk_ref, dv_ref, land_ref, send_sems, recv_sems):
        kj = pl.program_id(1)

        @pl.when((pl.program_id(0) == 0) & (kj == 0))
        def _():
            _scatter_start(gp_ref, land_ref, send_sems, recv_sems)

        @pl.when(kj == 0)
        def _():
            dqt_ref[...] = jnp.zeros(dqt_ref.shape, F32)

        kv, k_t, vv = k_ref[...], kt_ref[0, 0], v_ref[0]

        def block(qi, carry, diagonal):
            dk, dv = carry
            rows = pl.ds(pl.multiple_of(qi * _TQ, _TQ), _TQ)
            st = _scores_t(kv, qt_ref[0, qi], diagonal)
            pt = jnp.exp(st - l_ref[0, qi])
            dv = dv + _dot(pt.astype(BF16), do_ref[0, rows, :])
            dpt = _dot(vv, dot_ref[0, qi])
            dst = (pt * (dpt - d_ref[0, qi]) * _MLA_SCALE).astype(BF16)
            dk = dk + _dot(dst, q_ref[rows, :])
            dqt_ref[0, qi] += _dot(k_t, dst)
            return dk, dv

        init = (jnp.zeros((_TQ, HEAD_PAD), F32), jnp.zeros((_TQ, V_DIM_B), F32))
        carry = block(kj, init, True)
        dk, dv = lax.fori_loop(kj + 1, nb, lambda qi, cr: block(qi, cr, False), carry)
        dk_ref[...] = dk
        dv_ref[0] = dv

        @pl.when((pl.program_id(0) == N_HEADS_B - 1) & (kj == nb - 1))
        def _():
            _scatter_wait(gp_ref, land_ref, send_sems, recv_sems)

    head4 = lambda d: pl.BlockSpec((1, nb, d, _TQ), lambda h, kj: (h, 0, 0, 0))
    return pl.pallas_call(
        body, name="mla_bwd", grid=(N_HEADS_B, nb),
        in_specs=[pl.BlockSpec((_TQ, HEAD_PAD), lambda h, kj: (kj, h)),
                  pl.BlockSpec((1, 1, HEAD_PAD, _TQ), lambda h, kj: (h, kj, 0, 0)),
                  pl.BlockSpec((1, _TQ, V_DIM_B), lambda h, kj: (h, kj, 0)),
                  pl.BlockSpec((t, HEAD_PAD), lambda h, kj: (0, h)), head4(HEAD_PAD),
                  pl.BlockSpec((1, t, V_DIM_B), lambda h, kj: (h, 0, 0)), head4(V_DIM_B), head4(1), head4(1), _HBM],
        out_specs=[head4(HEAD_PAD), pl.BlockSpec((_TQ, HEAD_PAD), lambda h, kj: (kj, h)),
                   pl.BlockSpec((1, _TQ, V_DIM_B), lambda h, kj: (h, kj, 0)), _HBM],
        out_shape=[jax.ShapeDtypeStruct((N_HEADS_B, nb, HEAD_PAD, _TQ), F32), jax.ShapeDtypeStruct((t, MLA_W), F32),
                   jax.ShapeDtypeStruct((N_HEADS_B, t, V_DIM_B), F32),
                   jax.ShapeDtypeStruct((3,) + gp.shape[1:], gp.dtype)],
        scratch_shapes=[pltpu.SemaphoreType.DMA((3,)), pltpu.SemaphoreType.DMA((3,))],
        compiler_params=_params(("arbitrary", "arbitrary")),
    )(k, kt, v, q, qt, d_out, d_out_t, lse, delta, gp)


def _mla_prep_bwd(dqt, dk, dv, proj, posc, freq, qan, kvan, wq, wk, wv):
    t = dk.shape[0]
    tm = _TQ

    def body(dqt_ref, dk_ref, dv_ref, cq_ref, ckv_ref, pos_ref, f_ref, qan_ref, kvan_ref, wq_ref, wk_ref, wv_ref,
             dcq_ref, dckv_ref, dkr_ref, dwq_ref, dwk_ref, dwv_ref, dqan_ref, dkvan_ref):
        @pl.when(pl.program_id(0) == 0)
        def _():
            for r in (dwq_ref, dwk_ref, dwv_ref, dqan_ref, dkvan_ref):
                r[...] = jnp.zeros(r.shape, F32)

        cq = cq_ref[...]
        rq = _rms(cq)
        nq_ = cq * rq
        cqn = (nq_ * qan_ref[...]).astype(BF16)
        ckv = ckv_ref[...]
        rkv = _rms(ckv)
        nkv = ckv * rkv
        ckvn = (nkv * kvan_ref[...]).astype(BF16)
        c, s, lo, hi = _rope_coeffs(pos_ref[...], f_ref[...])
        dkv = dk_ref[...]
        dkr = jnp.zeros((tm, LANES), F32)
        dqb = []
        for h in range(N_HEADS_B):
            dqb.append(_unrope(dqt_ref[h, 0].T, c, s, lo, hi).astype(BF16))
            dkr = dkr + dkv[:, HEAD_PAD * h:HEAD_PAD * (h + 1)]
        dqb = jnp.concatenate(dqb, axis=1)
        dkr_ref[...] = jnp.where(lo | hi, _unrope(dkr, c, s, lo, hi), 0.0).astype(BF16)
        dkb = dkv.astype(BF16)
        dvb = jnp.concatenate([dv_ref[h] for h in range(N_HEADS_B)], axis=1).astype(BF16)
        dwq_ref[...] += _dot_tn(cqn, dqb)
        dwk_ref[...] += _dot_tn(ckvn, dkb)
        dwv_ref[...] += _dot_tn(ckvn, dvb)
        dcqn = _dot_nt(dqb, wq_ref[...])
        dckvn = _dot_nt(dkb, wk_ref[...]) + _dot_nt(dvb, wv_ref[...])
        dcq, dqan = _norm_bwd(dcqn, nq_, rq, qan_ref[...])
        dckv, dkvan = _norm_bwd(dckvn, nkv, rkv, kvan_ref[...])
        dcq_ref[...] = dcq.astype(BF16)
        dckv_ref[...] = dckv.astype(BF16)
        dqan_ref[...] += dqan
        dkvan_ref[...] += dkvan

    row = lambda i: (i, 0)
    vw = N_HEADS_B * V_DIM_B
    return pl.pallas_call(
        body, name="mla_prep_bwd", grid=(t // tm,),
        in_specs=[pl.BlockSpec((N_HEADS_B, 1, HEAD_PAD, tm), lambda i: (0, i, 0, 0)), pl.BlockSpec((tm, MLA_W), row),
                  pl.BlockSpec((N_HEADS_B, tm, V_DIM_B), lambda i: (0, i, 0)),
                  pl.BlockSpec((tm, Q_LORA), lambda i: (i, _CQ_BLK)),
                  pl.BlockSpec((tm, LANES), lambda i: (i, _CKV_BLK)),
                  pl.BlockSpec((tm, 1), row), _full((1, LANES)), _full((1, Q_LORA)), _full((1, KV_LORA)),
                  _full((Q_LORA, MLA_W)), _full((KV_LORA, MLA_W)), _full((KV_LORA, vw))],
        out_specs=[pl.BlockSpec((tm, Q_LORA), row), pl.BlockSpec((tm, LANES), row), pl.BlockSpec((tm, LANES), row),
                   _full((Q_LORA, MLA_W)), _full((KV_LORA, MLA_W)), _full((KV_LORA, vw)),
                   _full((1, Q_LORA)), _full((1, KV_LORA))],
        out_shape=[jax.ShapeDtypeStruct((t, Q_LORA), BF16), jax.ShapeDtypeStruct((t, LANES), BF16),
                   jax.ShapeDtypeStruct((t, LANES), BF16),
                   jax.ShapeDtypeStruct((Q_LORA, MLA_W), F32), jax.ShapeDtypeStruct((KV_LORA, MLA_W), F32),
                   jax.ShapeDtypeStruct((KV_LORA, vw), F32),
                   jax.ShapeDtypeStruct((1, Q_LORA), F32), jax.ShapeDtypeStruct((1, KV_LORA), F32)],
        compiler_params=_params(("arbitrary",)),
    )(dqt, dk, dv, proj, proj, posc, freq, qan, kvan, wq, wk, wv)


def _swa_bwd(proj, d_out, lse, delta, posc, posr, sinks):
    t = proj.shape[0]
    nb = t // BLOCK

    def body(q_ref, kc_ref, kp_ref, vc_ref, vp_ref, do_ref, l_ref, d_ref, pq_ref, pc_ref, pp_ref, sink_ref,
             dq_ref, dk_ref, dv_ref, ds_ref, dkb_s, dvb_s, dk_carry, dv_carry):
        n = pl.program_id(0)

        @pl.when(n == 0)
        def _():
            ds_ref[...] = jnp.zeros(ds_ref.shape, F32)
            dk_carry[...] = jnp.zeros(dk_carry.shape, F32)
            dv_carry[...] = jnp.zeros(dv_carry.shape, F32)

        @pl.when(n < nb)
        def _():
            kb, vb, dist, valid = _swa_band(n, kp_ref, kc_ref, vp_ref, vc_ref, pq_ref, pp_ref, pc_ref)
            qv, dov = q_ref[...], do_ref[...]
            q_t, do_t, kb_t = qv.T, dov.T, kb.T
            lane = lax.broadcasted_iota(jnp.int32, (1, LANES), 1)
            dsink = jnp.zeros((1, LANES), F32)
            dq_t = []
            for kh in range(N_KV_A):
                heads = range(_GROUP_A * kh, _GROUP_A * (kh + 1))
                st_g = _dot(_head_cols(kb, kh).astype(BF16), _group_t(q_t, kh))
                dpt_g = _dot(_head_cols(vb, kh).astype(BF16), _group_t(do_t, kh))
                pts, dsts = [], []
                for j, h in enumerate(heads):
                    st = _swa_scores_t(st_g, j, h, dist, valid)
                    l_h, d_h = l_ref[h:h + 1, :], d_ref[h:h + 1, :]
                    pt = jnp.exp(st - l_h)
                    p_sink = jnp.exp(sink_ref[0:1, h:h + 1] - l_h)
                    dsink = jnp.where(lane == h, jnp.sum(-p_sink * d_h, axis=1, keepdims=True), dsink)
                    dst = pt * (dpt_g[:, BLOCK * j:BLOCK * (j + 1)] - d_h) * _SWA_SCALE
                    pts.append(pt.astype(BF16))
                    dsts.append(dst.astype(BF16))
                pt_g, dst_g = jnp.concatenate(pts, axis=1), jnp.concatenate(dsts, axis=1)
                q_g = jnp.concatenate([_head_cols(qv, h) for h in heads], axis=0).astype(BF16)
                do_g = jnp.concatenate([_head_cols(dov, h) for h in heads], axis=0).astype(BF16)
                dkb_s[:, HEAD_DIM_A * kh:HEAD_DIM_A * (kh + 1)] = _dot(dst_g, q_g)
                dvb_s[:, HEAD_DIM_A * kh:HEAD_DIM_A * (kh + 1)] = _dot(pt_g, do_g)
                dq_g = _dot(_head_rows(kb_t, kh).astype(BF16), dst_g)
                dq_t.extend(dq_g[:, BLOCK * j:BLOCK * (j + 1)] for j in range(_GROUP_A))
            dq_ref[...] = jnp.concatenate(dq_t, axis=0).T
            ds_ref[...] += dsink
            dk_ref[...] = dk_carry[...] + dkb_s[0:BLOCK, :]
            dv_ref[...] = dv_carry[...] + dvb_s[0:BLOCK, :]
            dk_carry[...] = dkb_s[BLOCK:2 * BLOCK, :]
            dv_carry[...] = dvb_s[BLOCK:2 * BLOCK, :]

        @pl.when(n == nb)
        def _():
            dk_ref[...] = dk_carry[...]
            dv_ref[...] = dv_carry[...]

    cur = lambda n: (jnp.minimum(n, nb - 1), 0)
    cur_t = lambda n: (0, jnp.minimum(n, nb - 1))
    prv = lambda n: jnp.maximum(jnp.minimum(n, nb - 1) - 1, 0)
    out_prev = lambda n: (jnp.maximum(n - 1, 0), 0)
    return pl.pallas_call(
        body, name="swa_bwd", grid=(nb + 1,),
        in_specs=[pl.BlockSpec((BLOCK, WIDTH_A), lambda n: (jnp.minimum(n, nb - 1), _QA_BLK)),
                  pl.BlockSpec((BLOCK, LANES), lambda n: (jnp.minimum(n, nb - 1), _KA_BLK)),
                  pl.BlockSpec((BLOCK, LANES), lambda n: (prv(n), _KA_BLK)),
                  pl.BlockSpec((BLOCK, LANES), lambda n: (jnp.minimum(n, nb - 1), _VA_BLK)),
                  pl.BlockSpec((BLOCK, LANES), lambda n: (prv(n), _VA_BLK)),
                  pl.BlockSpec((BLOCK, WIDTH_A), cur), pl.BlockSpec((N_HEADS_A, BLOCK), cur_t),
                  pl.BlockSpec((N_HEADS_A, BLOCK), cur_t), pl.BlockSpec((1, BLOCK), cur_t),
                  pl.BlockSpec((BLOCK, 1), cur), pl.BlockSpec((BLOCK, 1), lambda n: (prv(n), 0)),
                  _full((1, N_HEADS_A))],
        out_specs=[pl.BlockSpec((BLOCK, WIDTH_A), cur), pl.BlockSpec((BLOCK, LANES), out_prev),
                   pl.BlockSpec((BLOCK, LANES), out_prev), _full((1, LANES))],
        out_shape=[jax.ShapeDtypeStruct((t, WIDTH_A), F32), jax.ShapeDtypeStruct((t, LANES), F32),
                   jax.ShapeDtypeStruct((t, LANES), F32), jax.ShapeDtypeStruct((1, LANES), F32)],
        scratch_shapes=[pltpu.VMEM((2 * BLOCK, LANES), F32), pltpu.VMEM((2 * BLOCK, LANES), F32),
                        pltpu.VMEM((BLOCK, LANES), F32), pltpu.VMEM((BLOCK, LANES), F32)],
        compiler_params=_params(("arbitrary",)),
    )(proj, proj, proj, proj, proj, d_out, lse, delta, posr, posc, posc, sinks)


def _in_bwd(dproj, w_in_p, x, dx1, g1, gp):
    t = x.shape[0]
    tm = 256
    steps = t // tm

    def body(dp_ref, w_ref, x_ref, dx1_ref, g_ref, gp_ref, dx_ref, dg_ref, land_ref, send_sems, recv_sems):
        i = pl.program_id(0)

        @pl.when(i == 0)
        def _():
            dg_ref[...] = jnp.zeros(dg_ref.shape, F32)
            _scatter_start(gp_ref, land_ref, send_sems, recv_sems)

        dh = _dot_nt(dp_ref[...], w_ref[...])
        xv = x_ref[...]
        r = _rms(xv)
        dx, dg = _norm_bwd(dh, xv * r, r, g_ref[...])
        dx_ref[...] = dx1_ref[...] + dx
        dg_ref[...] += dg

        @pl.when(i == steps - 1)
        def _():
            _scatter_wait(gp_ref, land_ref, send_sems, recv_sems)

    row = lambda i: (i, 0)
    blk = pl.BlockSpec((tm, D_MODEL), row)
    return pl.pallas_call(
        body, name="in_bwd", grid=(steps,),
        in_specs=[pl.BlockSpec((tm, D_IN_PAD), row), _full((D_MODEL, D_IN_PAD)), blk, blk, _full((1, D_MODEL)), _HBM],
        out_specs=[blk, _full((1, D_MODEL)), _HBM],
        out_shape=[jax.ShapeDtypeStruct((t, D_MODEL), F32), jax.ShapeDtypeStruct((1, D_MODEL), F32),
                   jax.ShapeDtypeStruct((3,) + gp.shape[1:], gp.dtype)],
        scratch_shapes=[pltpu.SemaphoreType.DMA((3,)), pltpu.SemaphoreType.DMA((3,))],
        compiler_params=_params(("arbitrary",)),
    )(dproj, w_in_p, x, dx1, g1, gp)


def _adamw(w, g_parts, m, v, name):
    r = w.shape[0]
    tr = min(r, 128)
    ng = len(g_parts)

    def body(*refs):
        w_ref, g_refs, m_ref, v_ref = refs[0], refs[1:1 + ng], refs[1 + ng], refs[2 + ng]
        g_out, d_out, m_out, v_out = refs[3 + ng:]
        g = g_refs[0][...]
        for gr in g_refs[1:]:
            g = g + gr[...]
        m_new = ADAM_B1 * m_ref[...] + (1.0 - ADAM_B1) * g
        v_new = ADAM_B2 * v_ref[...] + (1.0 - ADAM_B2) * jnp.square(g)
        m_hat = m_new / (1.0 - ADAM_B1 ** ADAM_STEP)
        v_hat = v_new / (1.0 - ADAM_B2 ** ADAM_STEP)
        g_out[...] = g
        d_out[...] = -ADAM_LR * (m_hat / (jnp.sqrt(v_hat) + ADAM_EPS) + ADAM_WD * w_ref[...])
        m_out[...] = m_new
        v_out[...] = v_new

    blk = pl.BlockSpec((tr, D_MODEL), lambda i: (i, 0))
    return pl.pallas_call(
        body, name=name, grid=(r // tr,),
        in_specs=[blk] * (3 + ng), out_specs=[blk] * 4,
        out_shape=[jax.ShapeDtypeStruct((r, D_MODEL), F32)] * 4,
        compiler_params=_params(("parallel",)),
    )(w, *g_parts, m, v)


_HBM = pl.BlockSpec(memory_space=pltpu.HBM)


def _other_chips(x, y):
    return ((1 - x, y), (x, 1 - y), (1 - x, 1 - y))


def _gather_copies(src, out, send_sems, recv_sems, local_sem):
    x, y, c = lax.axis_index("x"), lax.axis_index("y"), lax.axis_index("c")
    me = 2 * x + y
    local = pltpu.make_async_copy(src, out.at[me], local_sem)

    def copies(arriving):
        return [pltpu.make_async_remote_copy(src_ref=src, dst_ref=out.at[2 * px + py if arriving else me],
                                             send_sem=send_sems.at[j], recv_sem=recv_sems.at[j], device_id=(px, py, c),
                                             device_id_type=MESH)
                for j, (px, py) in enumerate(_other_chips(x, y))]

    return local, copies


def _gather_start(src, out, send_sems, recv_sems, local_sem):
    local, copies = _gather_copies(src, out, send_sems, recv_sems, local_sem)
    local.start()
    for cp in copies(False):
        cp.start()


def _gather_wait(src, out, send_sems, recv_sems, local_sem):
    local, copies = _gather_copies(src, out, send_sems, recv_sems, local_sem)
    for cp in copies(True):
        cp.wait_recv()
    for cp in copies(False):
        cp.wait_send()
    local.wait()


def _scatter_copies(src, land, send_sems, recv_sems):
    x, y, c = lax.axis_index("x"), lax.axis_index("y"), lax.axis_index("c")
    return [pltpu.make_async_remote_copy(src_ref=src.at[2 * px + py], dst_ref=land.at[j], send_sem=send_sems.at[j],
                                         recv_sem=recv_sems.at[j], device_id=(px, py, c), device_id_type=MESH)
            for j, (px, py) in enumerate(_other_chips(x, y))]


def _scatter_start(src, land, send_sems, recv_sems):
    for cp in _scatter_copies(src, land, send_sems, recv_sems):
        cp.start()


def _scatter_wait(src, land, send_sems, recv_sems):
    copies = _scatter_copies(src, land, send_sems, recv_sems)
    for cp in copies:
        cp.wait_recv()
    for cp in copies:
        cp.wait_send()


def _all_gather_chips(packed):
    def body(src, out, send_sems, recv_sems, local_sem):
        _gather_start(src, out, send_sems, recv_sems, local_sem)
        _gather_wait(src, out, send_sems, recv_sems, local_sem)

    return pl.pallas_call(
        body, name="ag_weights", in_specs=[_HBM], out_specs=_HBM,
        out_shape=jax.ShapeDtypeStruct((N_CHIPS,) + packed.shape, packed.dtype),
        scratch_shapes=[pltpu.SemaphoreType.DMA((3,)), pltpu.SemaphoreType.DMA((3,)), pltpu.SemaphoreType.DMA(())],
    )(packed)


def _sum4(own, land, name):
    r, w = own.shape
    tr = 128

    def body(o_ref, l_ref, s_ref):
        s_ref[...] = ((o_ref[...] + l_ref[0]) + l_ref[1]) + l_ref[2]

    return pl.pallas_call(
        body, name=name, grid=(r // tr,),
        in_specs=[pl.BlockSpec((tr, w), lambda i: (i, 0)), pl.BlockSpec((3, tr, w), lambda i: (0, i, 0))],
        out_specs=pl.BlockSpec((tr, w), lambda i: (i, 0)),
        out_shape=jax.ShapeDtypeStruct((r, w), F32),
        compiler_params=_params(("parallel",)),
    )(own, land)


def _swap_sibling(s, name):
    def body(src, got, send_sem, recv_sem):
        x, y, c = lax.axis_index("x"), lax.axis_index("y"), lax.axis_index("c")
        cp = pltpu.make_async_remote_copy(src_ref=src, dst_ref=got, send_sem=send_sem, recv_sem=recv_sem,
                                          device_id=(x, y, 1 - c), device_id_type=MESH)
        cp.start()
        cp.wait_recv()
        cp.wait_send()

    return pl.pallas_call(
        body, name=name, in_specs=[_HBM], out_specs=_HBM,
        out_shape=jax.ShapeDtypeStruct(s.shape, s.dtype),
        scratch_shapes=[pltpu.SemaphoreType.DMA(()), pltpu.SemaphoreType.DMA(())],
    )(s)


def _all_reduce_small(part):
    n_dev = 8

    def body(src, out, gath, send_sems, recv_sems):
        x, y, c = lax.axis_index("x"), lax.axis_index("y"), lax.axis_index("c")
        me = 4 * x + 2 * y + c
        gath[me] = src[...]
        peers = []
        for k in range(1, n_dev):
            px = 1 - x if (k >> 2) & 1 else x
            py = 1 - y if (k >> 1) & 1 else y
            pc = 1 - c if k & 1 else c
            peers.append((px, py, pc))
        sends = []
        for j, peer in enumerate(peers):
            cp = pltpu.make_async_remote_copy(src_ref=src, dst_ref=gath.at[me], send_sem=send_sems.at[j],
                                              recv_sem=recv_sems.at[j], device_id=peer, device_id_type=MESH)
            cp.start()
            sends.append(cp)
        for j, (px, py, pc) in enumerate(peers):
            pltpu.make_async_remote_copy(src_ref=src, dst_ref=gath.at[4 * px + 2 * py + pc], send_sem=send_sems.at[j],
                                         recv_sem=recv_sems.at[j], device_id=(px, py, pc), device_id_type=MESH).wait_recv()
        for cp in sends:
            cp.wait_send()
        acc = gath[0]
        for d in range(1, n_dev):
            acc = acc + gath[d]
        out[...] = acc

    vmem = pl.BlockSpec(memory_space=pltpu.VMEM)
    return pl.pallas_call(
        body, name="ar_small", in_specs=[vmem], out_specs=vmem,
        out_shape=jax.ShapeDtypeStruct(part.shape, F32),
        scratch_shapes=[pltpu.VMEM((n_dev,) + part.shape, F32), pltpu.SemaphoreType.DMA((n_dev - 1,)),
                        pltpu.SemaphoreType.DMA((n_dev - 1,))],
    )(part)


def _pad_rows(group):
    rows = sum(PACK_ROWS[n] for n in group)
    return -rows % LANES


def _pack(group, shards, dtype):
    parts = [shards[n].reshape(PACK_ROWS[n], D_MODEL).astype(dtype) for n in group]
    pad = _pad_rows(group)
    if pad:
        parts.append(jnp.zeros((pad, D_MODEL), dtype))
    return jnp.concatenate(parts, axis=0)


def _unpack(group, packed):
    out, off = {}, 0
    for n in group:
        out[n] = packed[off:off + PACK_ROWS[n]].reshape(SHARD_SHAPES[n])
        off += PACK_ROWS[n]
    return out


def _full_weights(group, gathered):
    out, off = {}, 0
    for n in group:
        r, c = SHARD_SHAPES[n]
        g = gathered[:, off:off + PACK_ROWS[n]].reshape(N_CHIPS, r, c)
        off += PACK_ROWS[n]
        if n in KEPT_SHARD_MAJOR:
            out[n] = g
        elif n in COL_SHARDED:
            out[n] = jnp.transpose(g, (1, 0, 2)).reshape(r, N_CHIPS * c)
        else:
            out[n] = g.reshape(N_CHIPS * r, c)
    return out


def _shard_major(group, full):
    parts = []
    for n in group:
        r, c = SHARD_SHAPES[n]
        g = full[n]
        if n in KEPT_SHARD_MAJOR:
            pass
        elif n in COL_SHARDED:
            g = jnp.transpose(g.reshape(r, N_CHIPS, c), (1, 0, 2))
        else:
            g = g.reshape(N_CHIPS, r, c)
        parts.append(g.reshape(N_CHIPS, PACK_ROWS[n], D_MODEL))
    pad = _pad_rows(group)
    if pad:
        parts.append(jnp.zeros((N_CHIPS, pad, D_MODEL), F32))
    return jnp.concatenate(parts, axis=1)


def _pad_layouts(w):
    dt = w["w_in"].dtype
    w_in = w["w_in"]
    z = lambda r, c: jnp.zeros((r, c), dt)
    w_in_p = jnp.concatenate([w_in[:, :3200], z(D_MODEL, 64), w_in[:, 3200:], z(D_MODEL, 32)], axis=1)
    wq = w["w_q_b"].reshape(Q_LORA, N_HEADS_B, Q_HEAD_B)
    wq_p = jnp.concatenate([wq, jnp.zeros((Q_LORA, N_HEADS_B, HEAD_PAD - Q_HEAD_B), dt)], axis=2).reshape(Q_LORA, MLA_W)
    wkv = w["w_kv_b"].reshape(KV_LORA, N_HEADS_B, QK_NOPE + V_DIM_B)
    zk = jnp.zeros((KV_LORA, N_HEADS_B, HEAD_PAD - QK_NOPE), dt)
    wk_p = jnp.concatenate([wkv[:, :, :QK_NOPE], zk], axis=2).reshape(KV_LORA, MLA_W)
    wv = wkv[:, :, QK_NOPE:].reshape(KV_LORA, N_HEADS_B * V_DIM_B)
    return dict(w_in=w_in_p, wq=wq_p, wk=wk_p, wv=wv)


def _unpad_grads(d):
    dw_in = jnp.concatenate([d["w_in"][:, :3200], d["w_in"][:, 3264:3296]], axis=1)
    dwq = d["wq"].reshape(Q_LORA, N_HEADS_B, HEAD_PAD)[:, :, :Q_HEAD_B].reshape(Q_LORA, N_HEADS_B * Q_HEAD_B)
    dwk = d["wk"].reshape(KV_LORA, N_HEADS_B, HEAD_PAD)[:, :, :QK_NOPE]
    dwv = d["wv"].reshape(KV_LORA, N_HEADS_B, V_DIM_B)
    dwkv = jnp.concatenate([dwk, dwv], axis=2).reshape(KV_LORA, N_HEADS_B * (QK_NOPE + V_DIM_B))
    return dict(w_in=dw_in, w_q_b=dwq, w_kv_b=dwkv)


def _rope_freq_lanes():
    freqs = ROPE_THETA ** (-jnp.arange(0, QK_ROPE, 2, dtype=F32) / QK_ROPE)
    return jnp.concatenate([jnp.zeros((QK_NOPE,), F32), freqs, freqs,
                            jnp.zeros((HEAD_PAD - Q_HEAD_B,), F32)]).reshape(1, LANES)


def _fwd_bwd(x, positions, target, w):
    t = x.shape[0]
    small = w
    wp = _pad_layouts(_full_weights(GROUP_A, _all_gather_chips(_pack(GROUP_A, w, BF16))))
    chip = 2 * lax.axis_index("x") + lax.axis_index("y")
    posr = positions.astype(F32).reshape(1, t)
    posc = posr.reshape(t, 1)
    freq = _rope_freq_lanes()
    g1, g2, g3, g4 = small["pre_norm_mix"], small["post_norm_mix"], small["pre_norm_mlp"], small["post_norm_mlp"]
    qan, kvan, sinks = small["q_a_norm"], small["kv_a_norm"], small["sinks"]

    h, proj = _proj_fwd(x, g1, wp["w_in"])
    out_a, lse_a = _swa_fwd(proj, posc, posr, sinks)
    qm, km, qt, kt, vm, vt = _mla_prep_fwd(proj, posc, freq, qan, kvan, wp["wq"], wp["wk"], wp["wv"])
    out_bt, lse_b, gathered_b = _mla_fwd(km, qt, vt, _pack(GROUP_B, w, BF16))
    wb = _full_weights(GROUP_B, gathered_b)
    merged, y, x1, h2 = _mix_out_fwd(out_a, out_bt, proj, x, wb["w_o_a"], wb["w_o_b"], wb["w_out"], g2, g3)
    a = _up_fwd(h2, wb["w_up"])
    dx2, dyd, dg4, loss = _down_fwd_loss(a, wb["w_down"], x1, target, g4)

    dwb = {}
    dwb["w_down"] = _matmul_tn(a, dyd, "dw_down", 512, 1024)
    du = _down_bwd(dyd, wb["w_down"], a)
    dwb["w_up"] = _matmul_tn(h2, du, "dw_up", 512, 1024, shard_major=True)
    dx1, dy, dg3, dg2 = _up_bwd(du, wb["w_up"], x1, dx2, y, g3, g2)
    dwb["w_out"] = _matmul_tn(merged, dy, "dw_out", 512, 1024)
    doa, dob, dga, dgb, d_out_a, d_out_b, d_out_bt, del_a, del_b = _mix_out_bwd(
        dy, out_a, out_bt, proj, wb["w_o_a"], wb["w_o_b"], wb["w_out"])
    dwb["w_o_a"] = _matmul_tn(out_a, doa, "dw_o_a", 512, 1024)
    dwb["w_o_b"] = _dw_ob(out_bt, dob)
    gp_b = _shard_major(GROUP_B, dwb)
    dqm, dkm, dvm, land_b = _mla_bwd(qm, km, qt, kt, vm, d_out_b, d_out_bt, lse_b, del_b, gp_b)
    dcq, dckv, dkr, dwq, dwk, dwv, dqan, dkvan = _mla_prep_bwd(
        dqm, dkm, dvm, proj, posc, freq, qan, kvan, wp["wq"], wp["wk"], wp["wv"])
    dqa, dka, dva, dsinks = _swa_bwd(proj, d_out_a, lse_a, del_a, posc, posr, sinks)
    dproj = jnp.concatenate([dga, dgb, dqa.astype(BF16), dka.astype(BF16), dva.astype(BF16), dcq, dckv, dkr], axis=1)
    dw_in = _matmul_tn(h, dproj, "dw_in", 512, D_IN_PAD // 2)
    gp_a = _shard_major(GROUP_A, _unpad_grads(dict(w_in=dw_in, wq=dwq, wk=dwk, wv=dwv)))
    grad_x, dg1, land_a = _in_bwd(dproj, wp["w_in"], x, dx1, g1, gp_a)

    own = lambda gp: lax.dynamic_index_in_dim(gp, chip, axis=0, keepdims=False)
    dsmall = dict(pre_norm_mix=dg1, post_norm_mix=dg2, pre_norm_mlp=dg3, post_norm_mlp=dg4,
                  q_a_norm=dqan, kv_a_norm=dkvan, sinks=dsinks[:, :N_HEADS_A])
    return loss, grad_x, {GROUP_A: (own(gp_a), land_a), GROUP_B: (own(gp_b), land_b)}, dsmall


def _pack_small(p, extra=None):
    tail = jnp.concatenate([p["q_a_norm"], p["kv_a_norm"], p["sinks"],
                            jnp.zeros((1, D_MODEL - Q_LORA - KV_LORA - N_HEADS_A), F32)], axis=1)
    scalar = jnp.zeros((1, D_MODEL), F32)
    if extra is not None:
        scalar = scalar.at[0, 0].set(extra)
    return jnp.concatenate([p["pre_norm_mix"], p["post_norm_mix"], p["pre_norm_mlp"], p["post_norm_mlp"], tail, scalar,
                            jnp.zeros((2, D_MODEL), F32)], axis=0)


def _unpack_small(b):
    return dict(pre_norm_mix=b[0:1], post_norm_mix=b[1:2], pre_norm_mlp=b[2:3], post_norm_mlp=b[3:4],
                q_a_norm=b[4:5, :Q_LORA], kv_a_norm=b[4:5, Q_LORA:Q_LORA + KV_LORA],
                sinks=b[4:5, Q_LORA + KV_LORA:Q_LORA + KV_LORA + N_HEADS_A])


def kernel(x, positions, pre_norm_mix, w_in, q_a_norm, w_q_b, kv_a_norm, w_kv_b, sinks, w_o_a, w_o_b, w_out, post_norm_mix, pre_norm_mlp, w_up, w_down, post_norm_mlp, loss_target, m_pre_norm_mix, m_w_in, m_q_a_norm, m_w_q_b, m_kv_a_norm, m_w_kv_b, m_sinks, m_w_o_a, m_w_o_b, m_w_out, m_post_norm_mix, m_pre_norm_mlp, m_w_up, m_w_down, m_post_norm_mlp, v_pre_norm_mix, v_w_in, v_q_a_norm, v_w_q_b, v_kv_a_norm, v_w_kv_b, v_sinks, v_w_o_a, v_w_o_b, v_w_out, v_post_norm_mix, v_pre_norm_mlp, v_w_up, v_w_down, v_post_norm_mlp):
    w = dict(pre_norm_mix=pre_norm_mix, w_in=w_in[0], q_a_norm=q_a_norm, w_q_b=w_q_b[0], kv_a_norm=kv_a_norm,
             w_kv_b=w_kv_b[0], sinks=sinks, w_o_a=w_o_a[0], w_o_b=w_o_b[0], w_out=w_out[0],
             post_norm_mix=post_norm_mix, pre_norm_mlp=pre_norm_mlp, w_up=w_up[0], w_down=w_down[0],
             post_norm_mlp=post_norm_mlp)
    m = dict(pre_norm_mix=m_pre_norm_mix, w_in=m_w_in[0], q_a_norm=m_q_a_norm, w_q_b=m_w_q_b[0],
             kv_a_norm=m_kv_a_norm, w_kv_b=m_w_kv_b[0], sinks=m_sinks, w_o_a=m_w_o_a[0], w_o_b=m_w_o_b[0],
             w_out=m_w_out[0], post_norm_mix=m_post_norm_mix, pre_norm_mlp=m_pre_norm_mlp, w_up=m_w_up[0],
             w_down=m_w_down[0], post_norm_mlp=m_post_norm_mlp)
    v = dict(pre_norm_mix=v_pre_norm_mix, w_in=v_w_in[0], q_a_norm=v_q_a_norm, w_q_b=v_w_q_b[0],
             kv_a_norm=v_kv_a_norm, w_kv_b=v_w_kv_b[0], sinks=v_sinks, w_o_a=v_w_o_a[0], w_o_b=v_w_o_b[0],
             w_out=v_w_out[0], post_norm_mix=v_post_norm_mix, pre_norm_mlp=v_pre_norm_mlp, w_up=v_w_up[0],
             w_down=v_w_down[0], post_norm_mlp=v_post_norm_mlp)

    loss, grad_x, blocks, dsmall = _fwd_bwd(x[0], positions, loss_target[0], w)

    red = _all_reduce_small(_pack_small(dsmall, loss[0, 0]))
    results = [_adamw(_pack_small(w), [red], _pack_small(m), _pack_small(v), "adamw_small")]
    for group, tag in ((GROUP_A, "a"), (GROUP_B, "b")):
        own, land = blocks[group]
        part = _sum4(own, land, "rs_sum_" + tag)
        results.append(_adamw(_pack(group, w, F32), [part, _swap_sibling(part, "rs_swap_" + tag)],
                              _pack(group, m, F32), _pack(group, v, F32), "adamw_" + tag))

    outs = []
    for k in range(4):
        by_name = dict(_unpack_small(results[0][k]))
        by_name.update({n: a[None] for n, a in _unpack(GROUP_A, results[1][k]).items()})
        by_name.update({n: a[None] for n, a in _unpack(GROUP_B, results[2][k]).items()})
        outs.extend(by_name[n] for n in WEIGHTS)
    return (red[5, 0], grad_x[None], *outs)
```

```python
import functools

import jax
import jax.numpy as jnp
from jax import lax
from jax.experimental import pallas as pl
from jax.experimental.pallas import tpu as pltpu

F32 = jnp.float32
BF16 = jnp.bfloat16
MESH = pl.DeviceIdType.MESH

D_MODEL = 1024
N_HEADS_A = 8
N_KV_A = 2
HEAD_DIM_A = 64
WINDOW = 128
BLOCK = 128
N_HEADS_B = 8
QK_NOPE = 64
QK_ROPE = 32
V_DIM_B = 64
Q_LORA = 256
KV_LORA = 128
ROPE_THETA = 10000.0
D_FF = 4 * D_MODEL
EPS = 1e-6
WIDTH_A = N_HEADS_A * HEAD_DIM_A
Q_HEAD_B = QK_NOPE + QK_ROPE
D_IN = 3232
D_IN_PAD = 3328
HEAD_PAD = 128
MLA_W = N_HEADS_B * HEAD_PAD

ADAM_LR = 0.001
ADAM_B1 = 0.9
ADAM_B2 = 0.999
ADAM_EPS = 1e-08
ADAM_WD = 0.01
ADAM_STEP = 10

NEG = -1e30
N_CHIPS = 4
LANES = 128
VMEM_LIMIT = 56 * 1024 * 1024

BIG = ("w_in", "w_q_b", "w_kv_b", "w_o_a", "w_o_b", "w_out", "w_up", "w_down")
KEPT_SHARD_MAJOR = ("w_up",)
COL_SHARDED = ("w_in", "w_q_b", "w_kv_b", "w_o_a", "w_o_b", "w_up")
SHARD_SHAPES = {"w_in": (1024, 808), "w_q_b": (256, 192), "w_kv_b": (128, 256), "w_o_a": (512, 256),
                "w_o_b": (512, 256), "w_out": (256, 1024), "w_up": (1024, 1024), "w_down": (1024, 1024)}
PACK_ROWS = {n: (s[0] * s[1]) // D_MODEL for n, s in SHARD_SHAPES.items()}
GROUP_A = ("w_in", "w_q_b", "w_kv_b")
GROUP_B = ("w_o_a", "w_o_b", "w_out", "w_up", "w_down")
SMALL = ("pre_norm_mix", "post_norm_mix", "pre_norm_mlp", "post_norm_mlp", "q_a_norm", "kv_a_norm", "sinks")
WEIGHTS = ("pre_norm_mix", "w_in", "q_a_norm", "w_q_b", "kv_a_norm", "w_kv_b", "sinks", "w_o_a", "w_o_b", "w_out",
           "post_norm_mix", "pre_norm_mlp", "w_up", "w_down", "post_norm_mlp")


def _params(sem=None):
    return pltpu.CompilerParams(dimension_semantics=sem, vmem_limit_bytes=VMEM_LIMIT)


def _dot(a, b):
    return jnp.dot(a, b, preferred_element_type=F32)


def _dot_nt(a, b):
    return lax.dot_general(a, b, (((1,), (1,)), ((), ())), preferred_element_type=F32)


def _dot_tn(a, b):
    return lax.dot_general(a, b, (((0,), (0,)), ((), ())), preferred_element_type=F32)


def _rms(v):
    return lax.rsqrt(jnp.mean(v * v, axis=-1, keepdims=True) + EPS)


def _norm_bwd(dout, n, r, g):
    dn = dout * g
    dx = r * (dn - n * jnp.mean(dn * n, axis=-1, keepdims=True))
    return dx, jnp.sum(dout * n, axis=0, keepdims=True)


def _full(shape):
    return pl.BlockSpec(shape, lambda *_: (0,) * len(shape))


def _proj_fwd(x, g1, w_in_p):
    t = x.shape[0]
    tm = 256

    def body(x_ref, g_ref, w_ref, h_ref, p_ref):
        xv = x_ref[...]
        h = ((xv * _rms(xv)) * g_ref[...]).astype(BF16)
        h_ref[...] = h
        p_ref[...] = _dot(h, w_ref[...])

    return pl.pallas_call(
        body, name="proj_fwd", grid=(t // tm,),
        in_specs=[pl.BlockSpec((tm, D_MODEL), lambda i: (i, 0)), _full((1, D_MODEL)), _full((D_MODEL, D_IN_PAD))],
        out_specs=[pl.BlockSpec((tm, D_MODEL), lambda i: (i, 0)), pl.BlockSpec((tm, D_IN_PAD), lambda i: (i, 0))],
        out_shape=[jax.ShapeDtypeStruct((t, D_MODEL), BF16), jax.ShapeDtypeStruct((t, D_IN_PAD), F32)],
        compiler_params=_params(("parallel",)),
    )(x, g1, w_in_p)


_QA_BLK = 2048 // WIDTH_A
_KA_BLK = 2560 // LANES
_VA_BLK = 2688 // LANES
_CQ_BLK = 2816 // Q_LORA
_CKV_BLK = 3072 // LANES
_KR_BLK = 3200 // LANES


_GROUP_A = N_HEADS_A // N_KV_A
_SWA_SCALE = HEAD_DIM_A ** -0.5


def _head_cols(v, h):
    return v[:, HEAD_DIM_A * h:HEAD_DIM_A * (h + 1)]


def _head_rows(v, h):
    return v[HEAD_DIM_A * h:HEAD_DIM_A * (h + 1), :]


def _swa_band(n, kp_ref, kc_ref, vp_ref, vc_ref, pq_ref, pp_ref, pc_ref):
    kb = jnp.concatenate([kp_ref[...], kc_ref[...]], axis=0)
    vb = jnp.concatenate([vp_ref[...], vc_ref[...]], axis=0)
    posk = jnp.concatenate([pp_ref[...], pc_ref[...]], axis=0)
    dist = jnp.abs(posk - pq_ref[...])
    ki = lax.broadcasted_iota(jnp.int32, (2 * BLOCK, BLOCK), 0)
    qi = lax.broadcasted_iota(jnp.int32, (2 * BLOCK, BLOCK), 1)
    valid = (ki > qi) & (ki <= qi + WINDOW) & ((n > 0) | (ki >= BLOCK))
    return kb, vb, dist, valid


def _swa_scores_t(st_g, j, h, dist, valid):
    slope = 2.0 ** (-8.0 * (h + 1) / N_HEADS_A)
    st = st_g[:, BLOCK * j:BLOCK * (j + 1)] * _SWA_SCALE - slope * dist
    return jnp.where(valid, st, NEG)


def _group_t(xt, kh):
    return jnp.concatenate([_head_rows(xt, _GROUP_A * kh + j) for j in range(_GROUP_A)], axis=1).astype(BF16)


def _swa_fwd(proj, posc, posr, sinks):
    t = proj.shape[0]
    nb = t // BLOCK

    def body(q_ref, kc_ref, kp_ref, vc_ref, vp_ref, pq_ref, pc_ref, pp_ref, sink_ref, o_ref, l_ref):
        n = pl.program_id(0)
        kb, vb, dist, valid = _swa_band(n, kp_ref, kc_ref, vp_ref, vc_ref, pq_ref, pp_ref, pc_ref)
        q_t, vb_t = q_ref[...].T, vb.T
        out_t, lse = [], []
        for kh in range(N_KV_A):
            st_g = _dot(_head_cols(kb, kh).astype(BF16), _group_t(q_t, kh))
            ps = []
            for j in range(_GROUP_A):
                h = _GROUP_A * kh + j
                st = _swa_scores_t(st_g, j, h, dist, valid)
                sink = sink_ref[0:1, h:h + 1]
                m = jnp.maximum(jnp.max(st, axis=0, keepdims=True), sink)
                e = jnp.exp(st - m)
                den = jnp.sum(e, axis=0, keepdims=True) + jnp.exp(sink - m)
                ps.append((e / den).astype(BF16))
                lse.append(m + jnp.log(den))
            o_g = _dot(_head_rows(vb_t, kh).astype(BF16), jnp.concatenate(ps, axis=1))
            out_t.extend(o_g[:, BLOCK * j:BLOCK * (j + 1)] for j in range(_GROUP_A))
        o_ref[...] = jnp.concatenate(out_t, axis=0).T
        l_ref[...] = jnp.concatenate(lse, axis=0)

    cur = lambda n: (n, 0)
    prev = lambda n: jnp.maximum(n - 1, 0)
    return pl.pallas_call(
        body, name="swa_fwd", grid=(nb,),
        in_specs=[pl.BlockSpec((BLOCK, WIDTH_A), lambda n: (n, _QA_BLK)),
                  pl.BlockSpec((BLOCK, LANES), lambda n: (n, _KA_BLK)),
                  pl.BlockSpec((BLOCK, LANES), lambda n: (prev(n), _KA_BLK)),
                  pl.BlockSpec((BLOCK, LANES), lambda n: (n, _VA_BLK)),
                  pl.BlockSpec((BLOCK, LANES), lambda n: (prev(n), _VA_BLK)),
                  pl.BlockSpec((1, BLOCK), lambda n: (0, n)),
                  pl.BlockSpec((BLOCK, 1), cur),
                  pl.BlockSpec((BLOCK, 1), lambda n: (prev(n), 0)),
                  _full((1, N_HEADS_A))],
        out_specs=[pl.BlockSpec((BLOCK, WIDTH_A), cur), pl.BlockSpec((N_HEADS_A, BLOCK), lambda n: (0, n))],
        out_shape=[jax.ShapeDtypeStruct((t, WIDTH_A), F32), jax.ShapeDtypeStruct((N_HEADS_A, t), F32)],
        compiler_params=_params(("parallel",)),
    )(proj, proj, proj, proj, proj, posr, posc, posc, sinks)


def _rope_coeffs(pos, freq):
    ang = pos * freq
    cosv, sinv = jnp.cos(ang), jnp.sin(ang)
    lane = lax.broadcasted_iota(jnp.int32, ang.shape, 1)
    lo = (lane >= QK_NOPE) & (lane < QK_NOPE + QK_ROPE // 2)
    hi = (lane >= QK_NOPE + QK_ROPE // 2) & (lane < QK_NOPE + QK_ROPE)
    c = jnp.where(lane < QK_NOPE, 1.0, jnp.where(lo | hi, cosv, 0.0))
    s = jnp.where(lo, -sinv, jnp.where(hi, sinv, 0.0))
    return c, s, lo, hi


def _rope(xh, c, s, lo):
    up = pltpu.roll(xh, LANES - QK_ROPE // 2, axis=1)
    dn = pltpu.roll(xh, QK_ROPE // 2, axis=1)
    return xh * c + jnp.where(lo, up, dn) * s


def _unrope(dh, c, s, lo, hi):
    g = dh * s
    up = pltpu.roll(g, LANES - QK_ROPE // 2, axis=1)
    dn = pltpu.roll(g, QK_ROPE // 2, axis=1)
    return dh * c + jnp.where(hi, dn, jnp.where(lo, up, 0.0))


_TQ = 512
_MLA_SCALE = Q_HEAD_B ** -0.5


def _mla_prep_fwd(proj, posc, freq, qan, kvan, wq, wk, wv):
    t = proj.shape[0]
    tm = _TQ
    nb = t // tm

    def body(cq_ref, ckv_ref, kr_ref, pos_ref, f_ref, qan_ref, kvan_ref, wq_ref, wk_ref, wv_ref,
             q_ref, k_ref, qt_ref, kt_ref, v_ref, vt_ref):
        cq = cq_ref[...]
        cqn = ((cq * _rms(cq)) * qan_ref[...]).astype(BF16)
        ckv = ckv_ref[...]
        ckvn = ((ckv * _rms(ckv)) * kvan_ref[...]).astype(BF16)
        qb = _dot(cqn, wq_ref[...])
        kb = _dot(ckvn, wk_ref[...])
        vb = _dot(ckvn, wv_ref[...])
        vbt = vb.T
        c, s, lo, _ = _rope_coeffs(pos_ref[...], f_ref[...])
        kr = _rope(kr_ref[...], c, s, lo)
        for h in range(N_HEADS_B):
            sl = slice(HEAD_PAD * h, HEAD_PAD * (h + 1))
            q_h = _rope(qb[:, sl], c, s, lo)
            k_h = kb[:, sl] + kr
            q_ref[:, sl] = q_h.astype(BF16)
            k_ref[:, sl] = k_h.astype(BF16)
            qt_ref[h, 0] = q_h.T.astype(BF16)
            kt_ref[h, 0] = k_h.T.astype(BF16)
            v_ref[h] = vb[:, V_DIM_B * h:V_DIM_B * (h + 1)].astype(BF16)
            vt_ref[h, 0] = vbt[V_DIM_B * h:V_DIM_B * (h + 1), :].astype(BF16)

    row = lambda i: (i, 0)
    blk4 = lambda d: pl.BlockSpec((N_HEADS_B, 1, d, tm), lambda i: (0, i, 0, 0))
    return pl.pallas_call(
        body, name="mla_prep_fwd", grid=(nb,),
        in_specs=[pl.BlockSpec((tm, Q_LORA), lambda i: (i, _CQ_BLK)),
                  pl.BlockSpec((tm, LANES), lambda i: (i, _CKV_BLK)),
                  pl.BlockSpec((tm, LANES), lambda i: (i, _KR_BLK)),
                  pl.BlockSpec((tm, 1), row), _full((1, LANES)), _full((1, Q_LORA)), _full((1, KV_LORA)),
                  _full((Q_LORA, MLA_W)), _full((KV_LORA, MLA_W)), _full((KV_LORA, N_HEADS_B * V_DIM_B))],
        out_specs=[pl.BlockSpec((tm, MLA_W), row), pl.BlockSpec((tm, MLA_W), row), blk4(HEAD_PAD), blk4(HEAD_PAD),
                   pl.BlockSpec((N_HEADS_B, tm, V_DIM_B), lambda i: (0, i, 0)), blk4(V_DIM_B)],
        out_shape=[jax.ShapeDtypeStruct((t, MLA_W), BF16), jax.ShapeDtypeStruct((t, MLA_W), BF16),
                   jax.ShapeDtypeStruct((N_HEADS_B, nb, HEAD_PAD, tm), BF16),
                   jax.ShapeDtypeStruct((N_HEADS_B, nb, HEAD_PAD, tm), BF16),
                   jax.ShapeDtypeStruct((N_HEADS_B, t, V_DIM_B), BF16),
                   jax.ShapeDtypeStruct((N_HEADS_B, nb, V_DIM_B, tm), BF16)],
        compiler_params=_params(("parallel",)),
    )(proj, proj, proj, posc, freq, qan, kvan, wq, wk, wv)


_LOG2E = 1.4426950408889634
_MLA_SCALE2 = _MLA_SCALE * _LOG2E


def _scores_t(k, qt, diagonal):
    st = _dot(k, qt) * _MLA_SCALE2
    if diagonal:
        key = lax.broadcasted_iota(jnp.int32, st.shape, 0)
        qry = lax.broadcasted_iota(jnp.int32, st.shape, 1)
        st = jnp.where(key <= qry, st, NEG)
    return st


def _mla_fwd(k, qt, vt, w_src):
    t = k.shape[0]
    nb = t // _TQ

    def body(k_ref, qt_ref, vt_ref, w_ref, o_ref, l_ref, wg_ref, send_sems, recv_sems, local_sem):
        qi = pl.program_id(1)
        first = (pl.program_id(0) == 0) & (qi == 0)
        last = (pl.program_id(0) == N_HEADS_B - 1) & (qi == nb - 1)

        @pl.when(first)
        def _():
            _gather_start(w_ref, wg_ref, send_sems, recv_sems, local_sem)

        q_t = qt_ref[0, 0]

        def scores(kj, diagonal=False):
            return _scores_t(k_ref[pl.ds(pl.multiple_of(kj * _TQ, _TQ), _TQ), :], q_t, diagonal)

        def update(carry, st, kj):
            m, l, acc = carry
            m_new = jnp.maximum(m, jnp.max(st, axis=0, keepdims=True))
            alpha = jnp.exp2(m - m_new)
            p = jnp.exp2(st - m_new)
            l = alpha * l + jnp.sum(p, axis=0, keepdims=True)
            acc = alpha * acc + _dot(vt_ref[0, kj], p.astype(BF16))
            return m_new, l, acc

        def pair(i, carry):
            st_a, st_b = scores(2 * i), scores(2 * i + 1)
            return update(update(carry, st_a, 2 * i), st_b, 2 * i + 1)

        init = (jnp.full((1, _TQ), NEG, F32), jnp.zeros((1, _TQ), F32), jnp.zeros((V_DIM_B, _TQ), F32))
        carry = lax.fori_loop(0, qi // 2, pair, init)
        carry = lax.fori_loop(2 * (qi // 2), qi, lambda kj, cr: update(cr, scores(kj), kj), carry)
        m, l, acc = update(carry, scores(qi, True), qi)
        o_ref[0, 0] = acc / l
        l_ref[0, 0] = m + jnp.log(l) * _LOG2E

        @pl.when(last)
        def _():
            _gather_wait(w_ref, wg_ref, send_sems, recv_sems, local_sem)

    return pl.pallas_call(
        body, name="mla_fwd", grid=(N_HEADS_B, nb),
        in_specs=[pl.BlockSpec((t, HEAD_PAD), lambda h, qi: (0, h)),
                  pl.BlockSpec((1, 1, HEAD_PAD, _TQ), lambda h, qi: (h, qi, 0, 0)),
                  pl.BlockSpec((1, nb, V_DIM_B, _TQ), lambda h, qi: (h, 0, 0, 0)), _HBM],
        out_specs=[pl.BlockSpec((1, 1, V_DIM_B, _TQ), lambda h, qi: (h, qi, 0, 0)),
                   pl.BlockSpec((1, 1, 1, _TQ), lambda h, qi: (h, qi, 0, 0)), _HBM],
        out_shape=[jax.ShapeDtypeStruct((N_HEADS_B, nb, V_DIM_B, _TQ), F32),
                   jax.ShapeDtypeStruct((N_HEADS_B, nb, 1, _TQ), F32),
                   jax.ShapeDtypeStruct((N_CHIPS,) + w_src.shape, w_src.dtype)],
        scratch_shapes=[pltpu.SemaphoreType.DMA((3,)), pltpu.SemaphoreType.DMA((3,)), pltpu.SemaphoreType.DMA(())],
        compiler_params=_params(("arbitrary", "arbitrary")),
    )(k, qt, vt, w_src)


def _ot_spec(tm, d):
    per = _TQ // tm
    return pl.BlockSpec((N_HEADS_B, 1, d, tm), lambda i: (0, i // per, 0, i % per))


def _mix_out_fwd(out_a, out_bt, proj, x, w_oa, w_ob, w_out, g2, g3):
    t = x.shape[0]
    tm = 256

    def body(oa_ref, obt_ref, ga_ref, gb_ref, x_ref, woa_ref, wob_ref, wout_ref, g2_ref, g3_ref,
             mg_ref, y_ref, x1_ref, h2_ref):
        oa = _dot(oa_ref[...].astype(BF16), woa_ref[...])
        obt = obt_ref[...].reshape(N_HEADS_B * V_DIM_B, tm).astype(BF16)
        ob = _dot_tn(obt, wob_ref[...])
        merged = (jax.nn.sigmoid(ga_ref[...]) * oa + jax.nn.sigmoid(gb_ref[...]) * ob).astype(BF16)
        mg_ref[...] = merged
        y = _dot(merged, wout_ref[...])
        y_ref[...] = y
        x1 = x_ref[...] + (y * _rms(y)) * g2_ref[...]
        x1_ref[...] = x1
        h2_ref[...] = ((x1 * _rms(x1)) * g3_ref[...]).astype(BF16)

    row = lambda i: (i, 0)
    blk = pl.BlockSpec((tm, D_MODEL), row)
    return pl.pallas_call(
        body, name="mix_out_fwd", grid=(t // tm,),
        in_specs=[pl.BlockSpec((tm, WIDTH_A), row), _ot_spec(tm, V_DIM_B), pl.BlockSpec((tm, D_MODEL), lambda i: (i, 0)),
                  pl.BlockSpec((tm, D_MODEL), lambda i: (i, 1)), blk,
                  _full((WIDTH_A, D_MODEL)), _full((N_HEADS_B * V_DIM_B, D_MODEL)), _full((D_MODEL, D_MODEL)),
                  _full((1, D_MODEL)), _full((1, D_MODEL))],
        out_specs=[blk, blk, blk, blk],
        out_shape=[jax.ShapeDtypeStruct((t, D_MODEL), BF16), jax.ShapeDtypeStruct((t, D_MODEL), F32),
                   jax.ShapeDtypeStruct((t, D_MODEL), F32), jax.ShapeDtypeStruct((t, D_MODEL), BF16)],
        compiler_params=_params(("parallel",)),
    )(out_a, out_bt, proj, proj, x, w_oa, w_ob, w_out, g2, g3)


_TM_MLP = 256


def _up_fwd(h2, w_up):
    t = h2.shape[0]
    tm = _TM_MLP

    def body(h_ref, w_ref, a_ref):
        hv = h_ref[...]
        for j in range(N_CHIPS):
            u = _dot(hv, w_ref[j])
            a_ref[:, D_MODEL * j:D_MODEL * (j + 1)] = jnp.square(jnp.maximum(u, 0.0)).astype(BF16)

    return pl.pallas_call(
        body, name="up_fwd", grid=(t // tm,),
        in_specs=[pl.BlockSpec((tm, D_MODEL), lambda i: (i, 0)), _full((N_CHIPS, D_MODEL, D_MODEL))],
        out_specs=pl.BlockSpec((tm, D_FF), lambda i: (i, 0)),
        out_shape=jax.ShapeDtypeStruct((t, D_FF), BF16),
        compiler_params=_params(("parallel",)),
    )(h2, w_up)


def _down_fwd_loss(a, w_down, x1, target, g4):
    t = a.shape[0]
    tm = _TM_MLP

    def body(a_ref, w_ref, x1_ref, tg_ref, g_ref, dx2_ref, dyd_ref, dg_ref, loss_ref):
        @pl.when(pl.program_id(0) == 0)
        def _():
            dg_ref[...] = jnp.zeros(dg_ref.shape, F32)
            loss_ref[...] = jnp.zeros(loss_ref.shape, F32)

        yd = _dot(a_ref[...], w_ref[...])
        r = _rms(yd)
        n = yd * r
        diff = (x1_ref[...] + n * g_ref[...]) - tg_ref[...]
        loss_ref[...] += 0.5 * jnp.sum(jnp.mean(diff * diff, axis=-1, keepdims=True), axis=0, keepdims=True)
        dx2 = diff * (1.0 / D_MODEL)
        dx2_ref[...] = dx2
        dyd, dg = _norm_bwd(dx2, n, r, g_ref[...])
        dyd_ref[...] = dyd.astype(BF16)
        dg_ref[...] += dg

    row = lambda i: (i, 0)
    blk = pl.BlockSpec((tm, D_MODEL), row)
    return pl.pallas_call(
        body, name="down_fwd_loss", grid=(t // tm,),
        in_specs=[pl.BlockSpec((tm, D_FF), row), _full((D_FF, D_MODEL)), blk, blk, _full((1, D_MODEL))],
        out_specs=[blk, blk, _full((1, D_MODEL)), _full((1, LANES))],
        out_shape=[jax.ShapeDtypeStruct((t, D_MODEL), F32), jax.ShapeDtypeStruct((t, D_MODEL), BF16),
                   jax.ShapeDtypeStruct((1, D_MODEL), F32), jax.ShapeDtypeStruct((1, LANES), F32)],
        compiler_params=_params(("arbitrary",)),
    )(a, w_down, x1, target, g4)


def _matmul_tn(a, b, name, tm, tn, tk=1024, shard_major=False):
    t, m = a.shape
    n = b.shape[1]
    tk = min(tk, t)
    nk = t // tk

    def body(a_ref, b_ref, o_ref):
        @pl.when(pl.program_id(2) == 0)
        def _():
            o_ref[...] = jnp.zeros(o_ref.shape, F32)

        acc = _dot_tn(a_ref[...].astype(BF16), b_ref[...].astype(BF16))
        o_ref[...] += acc[None] if shard_major else acc

    if shard_major:
        out_spec = pl.BlockSpec((1, tm, tn), lambda i, j, k: (j, i, 0))
        out_shape = jax.ShapeDtypeStruct((n // tn, m, tn), F32)
    else:
        out_spec = pl.BlockSpec((tm, tn), lambda i, j, k: (i, j))
        out_shape = jax.ShapeDtypeStruct((m, n), F32)
    return pl.pallas_call(
        body, name=name, grid=(m // tm, n // tn, nk),
        in_specs=[pl.BlockSpec((tk, tm), lambda i, j, k: (k, i)), pl.BlockSpec((tk, tn), lambda i, j, k: (k, j))],
        out_specs=out_spec, out_shape=out_shape,
        compiler_params=_params(("parallel", "parallel", "arbitrary")),
    )(a, b)


def _down_bwd(dyd, w_down, a):
    t = dyd.shape[0]
    tm = _TM_MLP

    def body(d_ref, w_ref, a_ref, du_ref):
        da = _dot_nt(d_ref[...], w_ref[...])
        du_ref[...] = (da * (2.0 * jnp.sqrt(a_ref[...].astype(F32)))).astype(BF16)

    row = lambda i: (i, 0)
    return pl.pallas_call(
        body, name="down_bwd", grid=(t // tm,),
        in_specs=[pl.BlockSpec((tm, D_MODEL), row), _full((D_FF, D_MODEL)), pl.BlockSpec((tm, D_FF), row)],
        out_specs=pl.BlockSpec((tm, D_FF), row),
        out_shape=jax.ShapeDtypeStruct((t, D_FF), BF16),
        compiler_params=_params(("parallel",)),
    )(dyd, w_down, a)


def _up_bwd(du, w_up, x1, dx2, y, g3, g2):
    t = du.shape[0]
    tm = _TM_MLP

    def body(du_ref, w_ref, x1_ref, dx2_ref, y_ref, g3_ref, g2_ref, dx1_ref, dy_ref, dg3_ref, dg2_ref):
        @pl.when(pl.program_id(0) == 0)
        def _():
            dg3_ref[...] = jnp.zeros(dg3_ref.shape, F32)
            dg2_ref[...] = jnp.zeros(dg2_ref.shape, F32)

        dh2 = _dot_nt(du_ref[:, 0:D_MODEL], w_ref[0])
        for j in range(1, N_CHIPS):
            dh2 = dh2 + _dot_nt(du_ref[:, D_MODEL * j:D_MODEL * (j + 1)], w_ref[j])
        x1 = x1_ref[...]
        r3 = _rms(x1)
        d3, dg3 = _norm_bwd(dh2, x1 * r3, r3, g3_ref[...])
        dx1 = dx2_ref[...] + d3
        dx1_ref[...] = dx1
        dg3_ref[...] += dg3
        y = y_ref[...]
        r2 = _rms(y)
        dy, dg2 = _norm_bwd(dx1, y * r2, r2, g2_ref[...])
        dy_ref[...] = dy.astype(BF16)
        dg2_ref[...] += dg2

    row = lambda i: (i, 0)
    blk = pl.BlockSpec((tm, D_MODEL), row)
    return pl.pallas_call(
        body, name="up_bwd", grid=(t // tm,),
        in_specs=[pl.BlockSpec((tm, D_FF), row), _full((N_CHIPS, D_MODEL, D_MODEL)),
                  blk, blk, blk, _full((1, D_MODEL)), _full((1, D_MODEL))],
        out_specs=[blk, blk, _full((1, D_MODEL)), _full((1, D_MODEL))],
        out_shape=[jax.ShapeDtypeStruct((t, D_MODEL), F32), jax.ShapeDtypeStruct((t, D_MODEL), BF16),
                   jax.ShapeDtypeStruct((1, D_MODEL), F32), jax.ShapeDtypeStruct((1, D_MODEL), F32)],
        compiler_params=_params(("arbitrary",)),
    )(du, w_up, x1, dx2, y, g3, g2)


def _mix_out_bwd(dy, out_a, out_bt, proj, w_oa, w_ob, w_out):
    t = dy.shape[0]
    tm = 256
    nb = t // _TQ

    def body(dy_ref, oa_ref, obt_ref, ga_ref, gb_ref, woa_ref, wob_ref, wout_ref,
             doa_ref, dob_ref, dga_ref, dgb_ref, da_ref, db_ref, dbt_ref, dela_ref, delb_ref):
        dm = _dot_nt(dy_ref[...], wout_ref[...])
        out_a_v = oa_ref[...]
        out_bt_v = obt_ref[...].reshape(N_HEADS_B * V_DIM_B, tm)
        oa = _dot(out_a_v.astype(BF16), woa_ref[...])
        ob = _dot_tn(out_bt_v.astype(BF16), wob_ref[...])
        sa, sb = jax.nn.sigmoid(ga_ref[...]), jax.nn.sigmoid(gb_ref[...])
        doa = (dm * sa).astype(BF16)
        dob = (dm * sb).astype(BF16)
        doa_ref[...] = doa
        dob_ref[...] = dob
        dga_ref[...] = (dm * oa * (sa * (1.0 - sa))).astype(BF16)
        dgb_ref[...] = (dm * ob * (sb * (1.0 - sb))).astype(BF16)
        d_out_a = _dot_nt(doa, woa_ref[...])
        da_ref[...] = d_out_a
        prod_at = (d_out_a * out_a_v).T
        dela_ref[...] = jnp.concatenate(
            [jnp.sum(_head_rows(prod_at, h), axis=0, keepdims=True) for h in range(N_HEADS_A)], axis=0)
        d_out_b = _dot_nt(dob, wob_ref[...])
        d_out_bt = _dot_nt(wob_ref[...], dob)
        prod_bt = d_out_bt * out_bt_v
        for h in range(N_HEADS_B):
            db_ref[h] = d_out_b[:, V_DIM_B * h:V_DIM_B * (h + 1)].astype(BF16)
            dbt_ref[h, 0] = d_out_bt[V_DIM_B * h:V_DIM_B * (h + 1), :].astype(BF16)
            delb_ref[h, 0] = jnp.sum(prod_bt[V_DIM_B * h:V_DIM_B * (h + 1), :], axis=0, keepdims=True)

    row = lambda i: (i, 0)
    blk = pl.BlockSpec((tm, D_MODEL), row)
    return pl.pallas_call(
        body, name="mix_out_bwd", grid=(t // tm,),
        in_specs=[blk, pl.BlockSpec((tm, WIDTH_A), row), _ot_spec(tm, V_DIM_B), pl.BlockSpec((tm, D_MODEL), lambda i: (i, 0)),
                  pl.BlockSpec((tm, D_MODEL), lambda i: (i, 1)),
                  _full((WIDTH_A, D_MODEL)), _full((N_HEADS_B * V_DIM_B, D_MODEL)), _full((D_MODEL, D_MODEL))],
        out_specs=[blk, blk, blk, blk, pl.BlockSpec((tm, WIDTH_A), row),
                   pl.BlockSpec((N_HEADS_B, tm, V_DIM_B), lambda i: (0, i, 0)), _ot_spec(tm, V_DIM_B),
                   pl.BlockSpec((N_HEADS_A, tm), lambda i: (0, i)), _ot_spec(tm, 1)],
        out_shape=[jax.ShapeDtypeStruct((t, D_MODEL), BF16)] * 4
        + [jax.ShapeDtypeStruct((t, WIDTH_A), F32), jax.ShapeDtypeStruct((N_HEADS_B, t, V_DIM_B), BF16),
           jax.ShapeDtypeStruct((N_HEADS_B, nb, V_DIM_B, _TQ), BF16),
           jax.ShapeDtypeStruct((N_HEADS_A, t), F32), jax.ShapeDtypeStruct((N_HEADS_B, nb, 1, _TQ), F32)],
        compiler_params=_params(("parallel",)),
    )(dy, out_a, out_bt, proj, proj, w_oa, w_ob, w_out)


def _dw_ob(out_bt, dob):
    t = dob.shape[0]
    nb = t // _TQ

    def body(obt_ref, dob_ref, o_ref):
        @pl.when(pl.program_id(0) == 0)
        def _():
            o_ref[...] = jnp.zeros(o_ref.shape, F32)

        obt = obt_ref[...].reshape(N_HEADS_B * V_DIM_B, _TQ).astype(BF16)
        o_ref[...] += _dot(obt, dob_ref[...])

    return pl.pallas_call(
        body, name="dw_o_b", grid=(nb,),
        in_specs=[pl.BlockSpec((N_HEADS_B, 1, V_DIM_B, _TQ), lambda i: (0, i, 0, 0)),
                  pl.BlockSpec((_TQ, D_MODEL), lambda i: (i, 0))],
        out_specs=_full((N_HEADS_B * V_DIM_B, D_MODEL)),
        out_shape=jax.ShapeDtypeStruct((N_HEADS_B * V_DIM_B, D_MODEL), F32),
        compiler_params=_params(("arbitrary",)),
    )(out_bt, dob)


def _mla_bwd(q, k, qt, kt, v, d_out, d_out_t, lse, delta, gp):
    t = q.shape[0]
    nb = t // _TQ

    def body(k_ref, kt_ref, v_ref, q_ref, qt_ref, do_ref, dot_ref, l_ref, d_ref, gp_ref,
             dqt_ref, dk_ref, dv_ref, land_ref, send_sems, recv_sems):
        kj = pl.program_id(1)

        @pl.when((pl.program_id(0) == 0) & (kj == 0))
        def _():
            _scatter_start(gp_ref, land_ref, send_sems, recv_sems)

        @pl.when(kj == 0)
        def _():
            dqt_ref[...] = jnp.zeros(dqt_ref.shape, F32)

        kv, k_t, vv = k_ref[...], kt_ref[0, 0], v_ref[0]

        def products(qi, diagonal=False):
            return _scores_t(kv, qt_ref[0, qi], diagonal), _dot(vv, dot_ref[0, qi])

        def update(carry, prods, qi):
            dk, dv = carry
            st, dpt = prods
            rows = pl.ds(pl.multiple_of(qi * _TQ, _TQ), _TQ)
            pt = jnp.exp2(st - l_ref[0, qi])
            dv = dv + _dot(pt.astype(BF16), do_ref[0, rows, :])
            dst = (pt * (dpt - d_ref[0, qi]) * _MLA_SCALE).astype(BF16)
            dk = dk + _dot(dst, q_ref[rows, :])
            dqt_ref[0, qi] += _dot(k_t, dst)
            return dk, dv

        def pair(i, carry):
            qa = kj + 1 + 2 * i
            pa, pb = products(qa), products(qa + 1)
            return update(update(carry, pa, qa), pb, qa + 1)

        init = (jnp.zeros((_TQ, HEAD_PAD), F32), jnp.zeros((_TQ, V_DIM_B), F32))
        carry = update(init, products(kj, True), kj)
        pairs = (nb - 1 - kj) // 2
        carry = lax.fori_loop(0, pairs, pair, carry)
        dk, dv = lax.fori_loop(kj + 1 + 2 * pairs, nb, lambda qi, cr: update(cr, products(qi), qi), carry)
        dk_ref[...] = dk
        dv_ref[0] = dv

        @pl.when((pl.program_id(0) == N_HEADS_B - 1) & (kj == nb - 1))
        def _():
            _scatter_wait(gp_ref, land_ref, send_sems, recv_sems)

    head4 = lambda d: pl.BlockSpec((1, nb, d, _TQ), lambda h, kj: (h, 0, 0, 0))
    return pl.pallas_call(
        body, name="mla_bwd", grid=(N_HEADS_B, nb),
        in_specs=[pl.BlockSpec((_TQ, HEAD_PAD), lambda h, kj: (kj, h)),
                  pl.BlockSpec((1, 1, HEAD_PAD, _TQ), lambda h, kj: (h, kj, 0, 0)),
                  pl.BlockSpec((1, _TQ, V_DIM_B), lambda h, kj: (h, kj, 0)),
                  pl.BlockSpec((t, HEAD_PAD), lambda h, kj: (0, h)), head4(HEAD_PAD),
                  pl.BlockSpec((1, t, V_DIM_B), lambda h, kj: (h, 0, 0)), head4(V_DIM_B), head4(1), head4(1), _HBM],
        out_specs=[head4(HEAD_PAD), pl.BlockSpec((_TQ, HEAD_PAD), lambda h, kj: (kj, h)),
                   pl.BlockSpec((1, _TQ, V_DIM_B), lambda h, kj: (h, kj, 0)), _HBM],
        out_shape=[jax.ShapeDtypeStruct((N_HEADS_B, nb, HEAD_PAD, _TQ), F32), jax.ShapeDtypeStruct((t, MLA_W), F32),
                   jax.ShapeDtypeStruct((N_HEADS_B, t, V_DIM_B), F32),
                   jax.ShapeDtypeStruct((3,) + gp.shape[1:], gp.dtype)],
        scratch_shapes=[pltpu.SemaphoreType.DMA((3,)), pltpu.SemaphoreType.DMA((3,))],
        compiler_params=_params(("arbitrary", "arbitrary")),
    )(k, kt, v, q, qt, d_out, d_out_t, lse, delta, gp)


def _mla_prep_bwd(dqt, dk, dv, proj, posc, freq, qan, kvan, wq, wk, wv):
    t = dk.shape[0]
    tm = _TQ

    def body(dqt_ref, dk_ref, dv_ref, cq_ref, ckv_ref, pos_ref, f_ref, qan_ref, kvan_ref, wq_ref, wk_ref, wv_ref,
             dcq_ref, dckv_ref, dkr_ref, dwq_ref, dwk_ref, dwv_ref, dqan_ref, dkvan_ref):
        @pl.when(pl.program_id(0) == 0)
        def _():
            for r in (dwq_ref, dwk_ref, dwv_ref, dqan_ref, dkvan_ref):
                r[...] = jnp.zeros(r.shape, F32)

        cq = cq_ref[...]
        rq = _rms(cq)
        nq_ = cq * rq
        cqn = (nq_ * qan_ref[...]).astype(BF16)
        ckv = ckv_ref[...]
        rkv = _rms(ckv)
        nkv = ckv * rkv
        ckvn = (nkv * kvan_ref[...]).astype(BF16)
        c, s, lo, hi = _rope_coeffs(pos_ref[...], f_ref[...])
        dkv = dk_ref[...]
        dkr = jnp.zeros((tm, LANES), F32)
        dqb = []
        for h in range(N_HEADS_B):
            dqb.append(_unrope(dqt_ref[h, 0].T, c, s, lo, hi).astype(BF16))
            dkr = dkr + dkv[:, HEAD_PAD * h:HEAD_PAD * (h + 1)]
        dqb = jnp.concatenate(dqb, axis=1)
        dkr_ref[...] = jnp.where(lo | hi, _unrope(dkr, c, s, lo, hi), 0.0).astype(BF16)
        dkb = dkv.astype(BF16)
        dvb = jnp.concatenate([dv_ref[h] for h in range(N_HEADS_B)], axis=1).astype(BF16)
        dwq_ref[...] += _dot_tn(cqn, dqb)
        dwk_ref[...] += _dot_tn(ckvn, dkb)
        dwv_ref[...] += _dot_tn(ckvn, dvb)
        dcqn = _dot_nt(dqb, wq_ref[...])
        dckvn = _dot_nt(dkb, wk_ref[...]) + _dot_nt(dvb, wv_ref[...])
        dcq, dqan = _norm_bwd(dcqn, nq_, rq, qan_ref[...])
        dckv, dkvan = _norm_bwd(dckvn, nkv, rkv, kvan_ref[...])
        dcq_ref[...] = dcq.astype(BF16)
        dckv_ref[...] = dckv.astype(BF16)
        dqan_ref[...] += dqan
        dkvan_ref[...] += dkvan

    row = lambda i: (i, 0)
    vw = N_HEADS_B * V_DIM_B
    return pl.pallas_call(
        body, name="mla_prep_bwd", grid=(t // tm,),
        in_specs=[pl.BlockSpec((N_HEADS_B, 1, HEAD_PAD, tm), lambda i: (0, i, 0, 0)), pl.BlockSpec((tm, MLA_W), row),
                  pl.BlockSpec((N_HEADS_B, tm, V_DIM_B), lambda i: (0, i, 0)),
                  pl.BlockSpec((tm, Q_LORA), lambda i: (i, _CQ_BLK)),
                  pl.BlockSpec((tm, LANES), lambda i: (i, _CKV_BLK)),
                  pl.BlockSpec((tm, 1), row), _full((1, LANES)), _full((1, Q_LORA)), _full((1, KV_LORA)),
                  _full((Q_LORA, MLA_W)), _full((KV_LORA, MLA_W)), _full((KV_LORA, vw))],
        out_specs=[pl.BlockSpec((tm, Q_LORA), row), pl.BlockSpec((tm, LANES), row), pl.BlockSpec((tm, LANES), row),
                   _full((Q_LORA, MLA_W)), _full((KV_LORA, MLA_W)), _full((KV_LORA, vw)),
                   _full((1, Q_LORA)), _full((1, KV_LORA))],
        out_shape=[jax.ShapeDtypeStruct((t, Q_LORA), BF16), jax.ShapeDtypeStruct((t, LANES), BF16),
                   jax.ShapeDtypeStruct((t, LANES), BF16),
                   jax.ShapeDtypeStruct((Q_LORA, MLA_W), F32), jax.ShapeDtypeStruct((KV_LORA, MLA_W), F32),
                   jax.ShapeDtypeStruct((KV_LORA, vw), F32),
                   jax.ShapeDtypeStruct((1, Q_LORA), F32), jax.ShapeDtypeStruct((1, KV_LORA), F32)],
        compiler_params=_params(("arbitrary",)),
    )(dqt, dk, dv, proj, proj, posc, freq, qan, kvan, wq, wk, wv)


def _swa_bwd(proj, d_out, lse, delta, posc, posr, sinks):
    t = proj.shape[0]
    nb = t // BLOCK

    def body(q_ref, kc_ref, kp_ref, vc_ref, vp_ref, do_ref, l_ref, d_ref, pq_ref, pc_ref, pp_ref, sink_ref,
             dq_ref, dk_ref, dv_ref, ds_ref, dkb_s, dvb_s, dk_carry, dv_carry):
        n = pl.program_id(0)

        @pl.when(n == 0)
        def _():
            ds_ref[...] = jnp.zeros(ds_ref.shape, F32)
            dk_carry[...] = jnp.zeros(dk_carry.shape, F32)
            dv_carry[...] = jnp.zeros(dv_carry.shape, F32)

        @pl.when(n < nb)
        def _():
            kb, vb, dist, valid = _swa_band(n, kp_ref, kc_ref, vp_ref, vc_ref, pq_ref, pp_ref, pc_ref)
            qv, dov = q_ref[...], do_ref[...]
            q_t, do_t, kb_t = qv.T, dov.T, kb.T
            lane = lax.broadcasted_iota(jnp.int32, (1, LANES), 1)
            dsink = jnp.zeros((1, LANES), F32)
            dq_t = []
            for kh in range(N_KV_A):
                heads = range(_GROUP_A * kh, _GROUP_A * (kh + 1))
                st_g = _dot(_head_cols(kb, kh).astype(BF16), _group_t(q_t, kh))
                dpt_g = _dot(_head_cols(vb, kh).astype(BF16), _group_t(do_t, kh))
                pts, dsts = [], []
                for j, h in enumerate(heads):
                    st = _swa_scores_t(st_g, j, h, dist, valid)
                    l_h, d_h = l_ref[h:h + 1, :], d_ref[h:h + 1, :]
                    pt = jnp.exp(st - l_h)
                    p_sink = jnp.exp(sink_ref[0:1, h:h + 1] - l_h)
                    dsink = jnp.where(lane == h, jnp.sum(-p_sink * d_h, axis=1, keepdims=True), dsink)
                    dst = pt * (dpt_g[:, BLOCK * j:BLOCK * (j + 1)] - d_h) * _SWA_SCALE
                    pts.append(pt.astype(BF16))
                    dsts.append(dst.astype(BF16))
                pt_g, dst_g = jnp.concatenate(pts, axis=1), jnp.concatenate(dsts, axis=1)
                q_g = jnp.concatenate([_head_cols(qv, h) for h in heads], axis=0).astype(BF16)
                do_g = jnp.concatenate([_head_cols(dov, h) for h in heads], axis=0).astype(BF16)
                dkb_s[:, HEAD_DIM_A * kh:HEAD_DIM_A * (kh + 1)] = _dot(dst_g, q_g)
                dvb_s[:, HEAD_DIM_A * kh:HEAD_DIM_A * (kh + 1)] = _dot(pt_g, do_g)
                dq_g = _dot(_head_rows(kb_t, kh).astype(BF16), dst_g)
                dq_t.extend(dq_g[:, BLOCK * j:BLOCK * (j + 1)] for j in range(_GROUP_A))
            dq_ref[...] = jnp.concatenate(dq_t, axis=0).T
            ds_ref[...] += dsink
            dk_ref[...] = dk_carry[...] + dkb_s[0:BLOCK, :]
            dv_ref[...] = dv_carry[...] + dvb_s[0:BLOCK, :]
            dk_carry[...] = dkb_s[BLOCK:2 * BLOCK, :]
            dv_carry[...] = dvb_s[BLOCK:2 * BLOCK, :]

        @pl.when(n == nb)
        def _():
            dk_ref[...] = dk_carry[...]
            dv_ref[...] = dv_carry[...]

    cur = lambda n: (jnp.minimum(n, nb - 1), 0)
    cur_t = lambda n: (0, jnp.minimum(n, nb - 1))
    prv = lambda n: jnp.maximum(jnp.minimum(n, nb - 1) - 1, 0)
    out_prev = lambda n: (jnp.maximum(n - 1, 0), 0)
    return pl.pallas_call(
        body, name="swa_bwd", grid=(nb + 1,),
        in_specs=[pl.BlockSpec((BLOCK, WIDTH_A), lambda n: (jnp.minimum(n, nb - 1), _QA_BLK)),
                  pl.BlockSpec((BLOCK, LANES), lambda n: (jnp.minimum(n, nb - 1), _KA_BLK)),
                  pl.BlockSpec((BLOCK, LANES), lambda n: (prv(n), _KA_BLK)),
                  pl.BlockSpec((BLOCK, LANES), lambda n: (jnp.minimum(n, nb - 1), _VA_BLK)),
                  pl.BlockSpec((BLOCK, LANES), lambda n: (prv(n), _VA_BLK)),
                  pl.BlockSpec((BLOCK, WIDTH_A), cur), pl.BlockSpec((N_HEADS_A, BLOCK), cur_t),
                  pl.BlockSpec((N_HEADS_A, BLOCK), cur_t), pl.BlockSpec((1, BLOCK), cur_t),
                  pl.BlockSpec((BLOCK, 1), cur), pl.BlockSpec((BLOCK, 1), lambda n: (prv(n), 0)),
                  _full((1, N_HEADS_A))],
        out_specs=[pl.BlockSpec((BLOCK, WIDTH_A), cur), pl.BlockSpec((BLOCK, LANES), out_prev),
                   pl.BlockSpec((BLOCK, LANES), out_prev), _full((1, LANES))],
        out_shape=[jax.ShapeDtypeStruct((t, WIDTH_A), F32), jax.ShapeDtypeStruct((t, LANES), F32),
                   jax.ShapeDtypeStruct((t, LANES), F32), jax.ShapeDtypeStruct((1, LANES), F32)],
        scratch_shapes=[pltpu.VMEM((2 * BLOCK, LANES), F32), pltpu.VMEM((2 * BLOCK, LANES), F32),
                        pltpu.VMEM((BLOCK, LANES), F32), pltpu.VMEM((BLOCK, LANES), F32)],
        compiler_params=_params(("arbitrary",)),
    )(proj, proj, proj, proj, proj, d_out, lse, delta, posr, posc, posc, sinks)


def _in_bwd(dproj, w_in_p, x, dx1, g1, gp):
    t = x.shape[0]
    tm = 256
    steps = t // tm

    def body(dp_ref, w_ref, x_ref, dx1_ref, g_ref, gp_ref, dx_ref, dg_ref, land_ref, send_sems, recv_sems):
        i = pl.program_id(0)

        @pl.when(i == 0)
        def _():
            dg_ref[...] = jnp.zeros(dg_ref.shape, F32)
            _scatter_start(gp_ref, land_ref, send_sems, recv_sems)

        dh = _dot_nt(dp_ref[...], w_ref[...])
        xv = x_ref[...]
        r = _rms(xv)
        dx, dg = _norm_bwd(dh, xv * r, r, g_ref[...])
        dx_ref[...] = dx1_ref[...] + dx
        dg_ref[...] += dg

        @pl.when(i == steps - 1)
        def _():
            _scatter_wait(gp_ref, land_ref, send_sems, recv_sems)

    row = lambda i: (i, 0)
    blk = pl.BlockSpec((tm, D_MODEL), row)
    return pl.pallas_call(
        body, name="in_bwd", grid=(steps,),
        in_specs=[pl.BlockSpec((tm, D_IN_PAD), row), _full((D_MODEL, D_IN_PAD)), blk, blk, _full((1, D_MODEL)), _HBM],
        out_specs=[blk, _full((1, D_MODEL)), _HBM],
        out_shape=[jax.ShapeDtypeStruct((t, D_MODEL), F32), jax.ShapeDtypeStruct((1, D_MODEL), F32),
                   jax.ShapeDtypeStruct((3,) + gp.shape[1:], gp.dtype)],
        scratch_shapes=[pltpu.SemaphoreType.DMA((3,)), pltpu.SemaphoreType.DMA((3,))],
        compiler_params=_params(("arbitrary",)),
    )(dproj, w_in_p, x, dx1, g1, gp)


def _adamw(w, g_parts, m, v, name):
    r = w.shape[0]
    tr = min(r, 128)
    ng = len(g_parts)

    def body(*refs):
        w_ref, g_refs, m_ref, v_ref = refs[0], refs[1:1 + ng], refs[1 + ng], refs[2 + ng]
        g_out, d_out, m_out, v_out = refs[3 + ng:]
        g = g_refs[0][...]
        for gr in g_refs[1:]:
            g = g + gr[...]
        m_new = ADAM_B1 * m_ref[...] + (1.0 - ADAM_B1) * g
        v_new = ADAM_B2 * v_ref[...] + (1.0 - ADAM_B2) * jnp.square(g)
        m_hat = m_new / (1.0 - ADAM_B1 ** ADAM_STEP)
        v_hat = v_new / (1.0 - ADAM_B2 ** ADAM_STEP)
        g_out[...] = g
        d_out[...] = -ADAM_LR * (m_hat / (jnp.sqrt(v_hat) + ADAM_EPS) + ADAM_WD * w_ref[...])
        m_out[...] = m_new
        v_out[...] = v_new

    blk = pl.BlockSpec((tr, D_MODEL), lambda i: (i, 0))
    return pl.pallas_call(
        body, name=name, grid=(r // tr,),
        in_specs=[blk] * (3 + ng), out_specs=[blk] * 4,
        out_shape=[jax.ShapeDtypeStruct((r, D_MODEL), F32)] * 4,
        compiler_params=_params(("parallel",)),
    )(w, *g_parts, m, v)


_HBM = pl.BlockSpec(memory_space=pltpu.HBM)


def _other_chips(x, y):
    return ((1 - x, y), (x, 1 - y), (1 - x, 1 - y))


def _gather_copies(src, out, send_sems, recv_sems, local_sem):
    x, y, c = lax.axis_index("x"), lax.axis_index("y"), lax.axis_index("c")
    me = 2 * x + y
    local = pltpu.make_async_copy(src, out.at[me], local_sem)

    def copies(arriving):
        return [pltpu.make_async_remote_copy(src_ref=src, dst_ref=out.at[2 * px + py if arriving else me],
                                             send_sem=send_sems.at[j], recv_sem=recv_sems.at[j], device_id=(px, py, c),
                                             device_id_type=MESH)
                for j, (px, py) in enumerate(_other_chips(x, y))]

    return local, copies


def _gather_start(src, out, send_sems, recv_sems, local_sem):
    local, copies = _gather_copies(src, out, send_sems, recv_sems, local_sem)
    local.start()
    for cp in copies(False):
        cp.start()


def _gather_wait(src, out, send_sems, recv_sems, local_sem):
    local, copies = _gather_copies(src, out, send_sems, recv_sems, local_sem)
    for cp in copies(True):
        cp.wait_recv()
    for cp in copies(False):
        cp.wait_send()
    local.wait()


def _scatter_copies(src, land, send_sems, recv_sems):
    x, y, c = lax.axis_index("x"), lax.axis_index("y"), lax.axis_index("c")
    return [pltpu.make_async_remote_copy(src_ref=src.at[2 * px + py], dst_ref=land.at[j], send_sem=send_sems.at[j],
                                         recv_sem=recv_sems.at[j], device_id=(px, py, c), device_id_type=MESH)
            for j, (px, py) in enumerate(_other_chips(x, y))]


def _scatter_start(src, land, send_sems, recv_sems):
    for cp in _scatter_copies(src, land, send_sems, recv_sems):
        cp.start()


def _scatter_wait(src, land, send_sems, recv_sems):
    copies = _scatter_copies(src, land, send_sems, recv_sems)
    for cp in copies:
        cp.wait_recv()
    for cp in copies:
        cp.wait_send()


def _all_gather_chips(packed):
    def body(src, out, send_sems, recv_sems, local_sem):
        _gather_start(src, out, send_sems, recv_sems, local_sem)
        _gather_wait(src, out, send_sems, recv_sems, local_sem)

    return pl.pallas_call(
        body, name="ag_weights", in_specs=[_HBM], out_specs=_HBM,
        out_shape=jax.ShapeDtypeStruct((N_CHIPS,) + packed.shape, packed.dtype),
        scratch_shapes=[pltpu.SemaphoreType.DMA((3,)), pltpu.SemaphoreType.DMA((3,)), pltpu.SemaphoreType.DMA(())],
    )(packed)


def _sum4(own, land, name):
    r, w = own.shape
    tr = 128

    def body(o_ref, l_ref, s_ref):
        s_ref[...] = ((o_ref[...] + l_ref[0]) + l_ref[1]) + l_ref[2]

    return pl.pallas_call(
        body, name=name, grid=(r // tr,),
        in_specs=[pl.BlockSpec((tr, w), lambda i: (i, 0)), pl.BlockSpec((3, tr, w), lambda i: (0, i, 0))],
        out_specs=pl.BlockSpec((tr, w), lambda i: (i, 0)),
        out_shape=jax.ShapeDtypeStruct((r, w), F32),
        compiler_params=_params(("parallel",)),
    )(own, land)


def _swap_sibling(s, name):
    def body(src, got, send_sem, recv_sem):
        x, y, c = lax.axis_index("x"), lax.axis_index("y"), lax.axis_index("c")
        cp = pltpu.make_async_remote_copy(src_ref=src, dst_ref=got, send_sem=send_sem, recv_sem=recv_sem,
                                          device_id=(x, y, 1 - c), device_id_type=MESH)
        cp.start()
        cp.wait_recv()
        cp.wait_send()

    return pl.pallas_call(
        body, name=name, in_specs=[_HBM], out_specs=_HBM,
        out_shape=jax.ShapeDtypeStruct(s.shape, s.dtype),
        scratch_shapes=[pltpu.SemaphoreType.DMA(()), pltpu.SemaphoreType.DMA(())],
    )(s)


def _all_reduce_small(part):
    n_dev = 8

    def body(src, out, gath, send_sems, recv_sems):
        x, y, c = lax.axis_index("x"), lax.axis_index("y"), lax.axis_index("c")
        me = 4 * x + 2 * y + c
        gath[me] = src[...]
        peers = []
        for k in range(1, n_dev):
            px = 1 - x if (k >> 2) & 1 else x
            py = 1 - y if (k >> 1) & 1 else y
            pc = 1 - c if k & 1 else c
            peers.append((px, py, pc))
        sends = []
        for j, peer in enumerate(peers):
            cp = pltpu.make_async_remote_copy(src_ref=src, dst_ref=gath.at[me], send_sem=send_sems.at[j],
                                              recv_sem=recv_sems.at[j], device_id=peer, device_id_type=MESH)
            cp.start()
            sends.append(cp)
        for j, (px, py, pc) in enumerate(peers):
            pltpu.make_async_remote_copy(src_ref=src, dst_ref=gath.at[4 * px + 2 * py + pc], send_sem=send_sems.at[j],
                                         recv_sem=recv_sems.at[j], device_id=(px, py, pc), device_id_type=MESH).wait_recv()
        for cp in sends:
            cp.wait_send()
        acc = gath[0]
        for d in range(1, n_dev):
            acc = acc + gath[d]
        out[...] = acc

    vmem = pl.BlockSpec(memory_space=pltpu.VMEM)
    return pl.pallas_call(
        body, name="ar_small", in_specs=[vmem], out_specs=vmem,
        out_shape=jax.ShapeDtypeStruct(part.shape, F32),
        scratch_shapes=[pltpu.VMEM((n_dev,) + part.shape, F32), pltpu.SemaphoreType.DMA((n_dev - 1,)),
                        pltpu.SemaphoreType.DMA((n_dev - 1,))],
    )(part)


def _pad_rows(group):
    rows = sum(PACK_ROWS[n] for n in group)
    return -rows % LANES


def _pack(group, shards, dtype):
    parts = [shards[n].reshape(PACK_ROWS[n], D_MODEL).astype(dtype) for n in group]
    pad = _pad_rows(group)
    if pad:
        parts.append(jnp.zeros((pad, D_MODEL), dtype))
    return jnp.concatenate(parts, axis=0)


def _unpack(group, packed):
    out, off = {}, 0
    for n in group:
        out[n] = packed[off:off + PACK_ROWS[n]].reshape(SHARD_SHAPES[n])
        off += PACK_ROWS[n]
    return out


def _full_weights(group, gathered):
    out, off = {}, 0
    for n in group:
        r, c = SHARD_SHAPES[n]
        g = gathered[:, off:off + PACK_ROWS[n]].reshape(N_CHIPS, r, c)
        off += PACK_ROWS[n]
        if n in KEPT_SHARD_MAJOR:
            out[n] = g
        elif n in COL_SHARDED:
            out[n] = jnp.transpose(g, (1, 0, 2)).reshape(r, N_CHIPS * c)
        else:
            out[n] = g.reshape(N_CHIPS * r, c)
    return out


def _shard_major(group, full):
    parts = []
    for n in group:
        r, c = SHARD_SHAPES[n]
        g = full[n]
        if n in KEPT_SHARD_MAJOR:
            pass
        elif n in COL_SHARDED:
            g = jnp.transpose(g.reshape(r, N_CHIPS, c), (1, 0, 2))
        else:
            g = g.reshape(N_CHIPS, r, c)
        parts.append(g.reshape(N_CHIPS, PACK_ROWS[n], D_MODEL))
    pad = _pad_rows(group)
    if pad:
        parts.append(jnp.zeros((N_CHIPS, pad, D_MODEL), F32))
    return jnp.concatenate(parts, axis=1)


def _pad_layouts(w):
    dt = w["w_in"].dtype
    w_in = w["w_in"]
    z = lambda r, c: jnp.zeros((r, c), dt)
    w_in_p = jnp.concatenate([w_in[:, :3200], z(D_MODEL, 64), w_in[:, 3200:], z(D_MODEL, 32)], axis=1)
    wq = w["w_q_b"].reshape(Q_LORA, N_HEADS_B, Q_HEAD_B)
    wq_p = jnp.concatenate([wq, jnp.zeros((Q_LORA, N_HEADS_B, HEAD_PAD - Q_HEAD_B), dt)], axis=2).reshape(Q_LORA, MLA_W)
    wkv = w["w_kv_b"].reshape(KV_LORA, N_HEADS_B, QK_NOPE + V_DIM_B)
    zk = jnp.zeros((KV_LORA, N_HEADS_B, HEAD_PAD - QK_NOPE), dt)
    wk_p = jnp.concatenate([wkv[:, :, :QK_NOPE], zk], axis=2).reshape(KV_LORA, MLA_W)
    wv = wkv[:, :, QK_NOPE:].reshape(KV_LORA, N_HEADS_B * V_DIM_B)
    return dict(w_in=w_in_p, wq=wq_p, wk=wk_p, wv=wv)


def _unpad_grads(d):
    dw_in = jnp.concatenate([d["w_in"][:, :3200], d["w_in"][:, 3264:3296]], axis=1)
    dwq = d["wq"].reshape(Q_LORA, N_HEADS_B, HEAD_PAD)[:, :, :Q_HEAD_B].reshape(Q_LORA, N_HEADS_B * Q_HEAD_B)
    dwk = d["wk"].reshape(KV_LORA, N_HEADS_B, HEAD_PAD)[:, :, :QK_NOPE]
    dwv = d["wv"].reshape(KV_LORA, N_HEADS_B, V_DIM_B)
    dwkv = jnp.concatenate([dwk, dwv], axis=2).reshape(KV_LORA, N_HEADS_B * (QK_NOPE + V_DIM_B))
    return dict(w_in=dw_in, w_q_b=dwq, w_kv_b=dwkv)


def _rope_freq_lanes():
    freqs = ROPE_THETA ** (-jnp.arange(0, QK_ROPE, 2, dtype=F32) / QK_ROPE)
    return jnp.concatenate([jnp.zeros((QK_NOPE,), F32), freqs, freqs,
                            jnp.zeros((HEAD_PAD - Q_HEAD_B,), F32)]).reshape(1, LANES)


def _fwd_bwd(x, positions, target, w):
    t = x.shape[0]
    small = w
    wp = _pad_layouts(_full_weights(GROUP_A, _all_gather_chips(_pack(GROUP_A, w, BF16))))
    chip = 2 * lax.axis_index("x") + lax.axis_index("y")
    posr = positions.astype(F32).reshape(1, t)
    posc = posr.reshape(t, 1)
    freq = _rope_freq_lanes()
    g1, g2, g3, g4 = small["pre_norm_mix"], small["post_norm_mix"], small["pre_norm_mlp"], small["post_norm_mlp"]
    qan, kvan, sinks = small["q_a_norm"], small["kv_a_norm"], small["sinks"]

    h, proj = _proj_fwd(x, g1, wp["w_in"])
    out_a, lse_a = _swa_fwd(proj, posc, posr, sinks)
    qm, km, qt, kt, vm, vt = _mla_prep_fwd(proj, posc, freq, qan, kvan, wp["wq"], wp["wk"], wp["wv"])
    out_bt, lse_b, gathered_b = _mla_fwd(km, qt, vt, _pack(GROUP_B, w, BF16))
    wb = _full_weights(GROUP_B, gathered_b)
    merged, y, x1, h2 = _mix_out_fwd(out_a, out_bt, proj, x, wb["w_o_a"], wb["w_o_b"], wb["w_out"], g2, g3)
    a = _up_fwd(h2, wb["w_up"])
    dx2, dyd, dg4, loss = _down_fwd_loss(a, wb["w_down"], x1, target, g4)

    dwb = {}
    dwb["w_down"] = _matmul_tn(a, dyd, "dw_down", 512, 1024)
    du = _down_bwd(dyd, wb["w_down"], a)
    dwb["w_up"] = _matmul_tn(h2, du, "dw_up", 512, 1024, shard_major=True)
    dx1, dy, dg3, dg2 = _up_bwd(du, wb["w_up"], x1, dx2, y, g3, g2)
    dwb["w_out"] = _matmul_tn(merged, dy, "dw_out", 512, 1024)
    doa, dob, dga, dgb, d_out_a, d_out_b, d_out_bt, del_a, del_b = _mix_out_bwd(
        dy, out_a, out_bt, proj, wb["w_o_a"], wb["w_o_b"], wb["w_out"])
    dwb["w_o_a"] = _matmul_tn(out_a, doa, "dw_o_a", 512, 1024)
    dwb["w_o_b"] = _dw_ob(out_bt, dob)
    gp_b = _shard_major(GROUP_B, dwb)
    dqm, dkm, dvm, land_b = _mla_bwd(qm, km, qt, kt, vm, d_out_b, d_out_bt, lse_b, del_b, gp_b)
    dcq, dckv, dkr, dwq, dwk, dwv, dqan, dkvan = _mla_prep_bwd(
        dqm, dkm, dvm, proj, posc, freq, qan, kvan, wp["wq"], wp["wk"], wp["wv"])
    dqa, dka, dva, dsinks = _swa_bwd(proj, d_out_a, lse_a, del_a, posc, posr, sinks)
    dproj = jnp.concatenate([dga, dgb, dqa.astype(BF16), dka.astype(BF16), dva.astype(BF16), dcq, dckv, dkr], axis=1)
    dw_in = _matmul_tn(h, dproj, "dw_in", 512, D_IN_PAD // 2)
    gp_a = _shard_major(GROUP_A, _unpad_grads(dict(w_in=dw_in, wq=dwq, wk=dwk, wv=dwv)))
    grad_x, dg1, land_a = _in_bwd(dproj, wp["w_in"], x, dx1, g1, gp_a)

    own = lambda gp: lax.dynamic_index_in_dim(gp, chip, axis=0, keepdims=False)
    dsmall = dict(pre_norm_mix=dg1, post_norm_mix=dg2, pre_norm_mlp=dg3, post_norm_mlp=dg4,
                  q_a_norm=dqan, kv_a_norm=dkvan, sinks=dsinks[:, :N_HEADS_A])
    return loss, grad_x, {GROUP_A: (own(gp_a), land_a), GROUP_B: (own(gp_b), land_b)}, dsmall


def _pack_small(p, extra=None):
    tail = jnp.concatenate([p["q_a_norm"], p["kv_a_norm"], p["sinks"],
                            jnp.zeros((1, D_MODEL - Q_LORA - KV_LORA - N_HEADS_A), F32)], axis=1)
    scalar = jnp.zeros((1, D_MODEL), F32)
    if extra is not None:
        scalar = scalar.at[0, 0].set(extra)
    return jnp.concatenate([p["pre_norm_mix"], p["post_norm_mix"], p["pre_norm_mlp"], p["post_norm_mlp"], tail, scalar,
                            jnp.zeros((2, D_MODEL), F32)], axis=0)


def _unpack_small(b):
    return dict(pre_norm_mix=b[0:1], post_norm_mix=b[1:2], pre_norm_mlp=b[2:3], post_norm_mlp=b[3:4],
                q_a_norm=b[4:5, :Q_LORA], kv_a_norm=b[4:5, Q_LORA:Q_LORA + KV_LORA],
                sinks=b[4:5, Q_LORA + KV_LORA:Q_LORA + KV_LORA + N_HEADS_A])


def kernel(x, positions, pre_norm_mix, w_in, q_a_norm, w_q_b, kv_a_norm, w_kv_b, sinks, w_o_a, w_o_b, w_out, post_norm_mix, pre_norm_mlp, w_up, w_down, post_norm_mlp, loss_target, m_pre_norm_mix, m_w_in, m_q_a_norm, m_w_q_b, m_kv_a_norm, m_w_kv_b, m_sinks, m_w_o_a, m_w_o_b, m_w_out, m_post_norm_mix, m_pre_norm_mlp, m_w_up, m_w_down, m_post_norm_mlp, v_pre_norm_mix, v_w_in, v_q_a_norm, v_w_q_b, v_kv_a_norm, v_w_kv_b, v_sinks, v_w_o_a, v_w_o_b, v_w_out, v_post_norm_mix, v_pre_norm_mlp, v_w_up, v_w_down, v_post_norm_mlp):
    w = dict(pre_norm_mix=pre_norm_mix, w_in=w_in[0], q_a_norm=q_a_norm, w_q_b=w_q_b[0], kv_a_norm=kv_a_norm,
             w_kv_b=w_kv_b[0], sinks=sinks, w_o_a=w_o_a[0], w_o_b=w_o_b[0], w_out=w_out[0],
             post_norm_mix=post_norm_mix, pre_norm_mlp=pre_norm_mlp, w_up=w_up[0], w_down=w_down[0],
             post_norm_mlp=post_norm_mlp)
    m = dict(pre_norm_mix=m_pre_norm_mix, w_in=m_w_in[0], q_a_norm=m_q_a_norm, w_q_b=m_w_q_b[0],
             kv_a_norm=m_kv_a_norm, w_kv_b=m_w_kv_b[0], sinks=m_sinks, w_o_a=m_w_o_a[0], w_o_b=m_w_o_b[0],
             w_out=m_w_out[0], post_norm_mix=m_post_norm_mix, pre_norm_mlp=m_pre_norm_mlp, w_up=m_w_up[0],
             w_down=m_w_down[0], post_norm_mlp=m_post_norm_mlp)
    v = dict(pre_norm_mix=v_pre_norm_mix, w_in=v_w_in[0], q_a_norm=v_q_a_norm, w_q_b=v_w_q_b[0],
             kv_a_norm=v_kv_a_norm, w_kv_b=v_w_kv_b[0], sinks=v_sinks, w_o_a=v_w_o_a[0], w_o_b=v_w_o_b[0],
             w_out=v_w_out[0], post_norm_mix=v_post_norm_mix, pre_norm_mlp=v_pre_norm_mlp, w_up=v_w_up[0],
             w_down=v_w_down[0], post_norm_mlp=v_post_norm_mlp)

    loss, grad_x, blocks, dsmall = _fwd_bwd(x[0], positions, loss_target[0], w)

    red = _all_reduce_small(_pack_small(dsmall, loss[0, 0]))
    results = [_adamw(_pack_small(w), [red], _pack_small(m), _pack_small(v), "adamw_small")]
    for group, tag in ((GROUP_A, "a"), (GROUP_B, "b")):
        own, land = blocks[group]
        part = _sum4(own, land, "rs_sum_" + tag)
        results.append(_adamw(_pack(group, w, F32), [part, _swap_sibling(part, "rs_swap_" + tag)],
                              _pack(group, m, F32), _pack(group, v, F32), "adamw_" + tag))

    outs = []
    for k in range(4):
        by_name = dict(_unpack_small(results[0][k]))
        by_name.update({n: a[None] for n, a in _unpack(GROUP_A, results[1][k]).items()})
        by_name.update({n: a[None] for n, a in _unpack(GROUP_B, results[2][k]).items()})
        outs.extend(by_name[n] for n in WEIGHTS)
    return (red[5, 0], grad_x[None], *outs)
```

```python
import jax
import jax.numpy as jnp
from jax import lax
from jax.experimental import pallas as pl
from jax.experimental.pallas import tpu as pltpu

F32 = jnp.float32
BF16 = jnp.bfloat16
MESH = pl.DeviceIdType.MESH

D_MODEL = 1024
N_HEADS_A = 8
N_KV_A = 2
HEAD_DIM_A = 64
WINDOW = 128
BLOCK = 128
N_HEADS_B = 8
QK_NOPE = 64
QK_ROPE = 32
V_DIM_B = 64
Q_LORA = 256
KV_LORA = 128
ROPE_THETA = 10000.0
D_FF = 4 * D_MODEL
EPS = 1e-6
WIDTH_A = N_HEADS_A * HEAD_DIM_A
Q_HEAD_B = QK_NOPE + QK_ROPE
D_IN_PAD = 3328
HEAD_PAD = 128
MLA_W = N_HEADS_B * HEAD_PAD

ADAM_LR = 0.001
ADAM_B1 = 0.9
ADAM_B2 = 0.999
ADAM_EPS = 1e-08
ADAM_WD = 0.01
ADAM_STEP = 10

NEG = -1e30
N_CHIPS = 4
LANES = 128
VMEM_LIMIT = 56 * 1024 * 1024

SHARD_SHAPES = {"w_in": (1024, 808), "w_q_b": (256, 192), "w_kv_b": (128, 256), "w_o_a": (512, 256),
                "w_o_b": (512, 256), "w_out": (256, 1024), "w_up": (1024, 1024), "w_down": (1024, 1024)}
PACK_ROWS = {n: (s[0] * s[1]) // D_MODEL for n, s in SHARD_SHAPES.items()}
GROUP_A = ("w_in", "w_q_b", "w_kv_b")
GROUP_B = ("w_up", "w_down", "w_out", "w_o_a", "w_o_b")
WEIGHTS = ("pre_norm_mix", "w_in", "q_a_norm", "w_q_b", "kv_a_norm", "w_kv_b", "sinks", "w_o_a", "w_o_b", "w_out",
           "post_norm_mix", "pre_norm_mlp", "w_up", "w_down", "post_norm_mlp")


def _params(sem=None):
    return pltpu.CompilerParams(dimension_semantics=sem, vmem_limit_bytes=VMEM_LIMIT)


def _dot(a, b):
    return jnp.dot(a, b, preferred_element_type=F32)


def _dot_nt(a, b):
    return lax.dot_general(a, b, (((1,), (1,)), ((), ())), preferred_element_type=F32)


def _dot_tn(a, b):
    return lax.dot_general(a, b, (((0,), (0,)), ((), ())), preferred_element_type=F32)


def _rms(v):
    return lax.rsqrt(jnp.mean(v * v, axis=-1, keepdims=True) + EPS)


def _norm_bwd(dout, n, r, g):
    dn = dout * g
    dx = r * (dn - n * jnp.mean(dn * n, axis=-1, keepdims=True))
    return dx, jnp.sum(dout * n, axis=0, keepdims=True)


def _full(shape):
    return pl.BlockSpec(shape, lambda *_: (0,) * len(shape))


def _row_offset(group, name):
    return sum(PACK_ROWS[n] for n in group[:group.index(name)])


def _wb_spec(name):
    rows = PACK_ROWS[name]
    return pl.BlockSpec((N_CHIPS, rows, D_MODEL), lambda *_: (0, _row_offset(GROUP_B, name) // rows, 0))


def _proj_fwd(x, g1, w_in_t):
    t = x.shape[0]
    tm = 256

    def body(x_ref, g_ref, w_ref, h_ref, p_ref):
        xv = x_ref[...]
        h = ((xv * _rms(xv)) * g_ref[...]).astype(BF16)
        h_ref[...] = h
        p_ref[...] = _dot_nt(h, w_ref[...])

    return pl.pallas_call(
        body, name="proj_fwd", grid=(t // tm,),
        in_specs=[pl.BlockSpec((tm, D_MODEL), lambda i: (i, 0)), _full((1, D_MODEL)), _full((D_IN_PAD, D_MODEL))],
        out_specs=[pl.BlockSpec((tm, D_MODEL), lambda i: (i, 0)), pl.BlockSpec((tm, D_IN_PAD), lambda i: (i, 0))],
        out_shape=[jax.ShapeDtypeStruct((t, D_MODEL), BF16), jax.ShapeDtypeStruct((t, D_IN_PAD), F32)],
        compiler_params=_params(("parallel",)),
    )(x, g1, w_in_t)


_QA_BLK = 2048 // WIDTH_A
_KA_BLK = 2560 // LANES
_VA_BLK = 2688 // LANES
_CQ_BLK = 2816 // Q_LORA
_CKV_BLK = 3072 // LANES
_KR_BLK = 3200 // LANES


_GROUP_A = N_HEADS_A // N_KV_A
_SWA_SCALE = HEAD_DIM_A ** -0.5


def _head_cols(v, h):
    return v[:, HEAD_DIM_A * h:HEAD_DIM_A * (h + 1)]


def _head_rows(v, h):
    return v[HEAD_DIM_A * h:HEAD_DIM_A * (h + 1), :]


def _swa_band(n, kp_ref, kc_ref, vp_ref, vc_ref, pq_ref, pp_ref, pc_ref):
    kb = jnp.concatenate([kp_ref[...], kc_ref[...]], axis=0)
    vb = jnp.concatenate([vp_ref[...], vc_ref[...]], axis=0)
    posk = jnp.concatenate([pp_ref[...], pc_ref[...]], axis=0)
    dist = jnp.abs(posk - pq_ref[...])
    ki = lax.broadcasted_iota(jnp.int32, (2 * BLOCK, BLOCK), 0)
    qi = lax.broadcasted_iota(jnp.int32, (2 * BLOCK, BLOCK), 1)
    valid = (ki > qi) & (ki <= qi + WINDOW) & ((n > 0) | (ki >= BLOCK))
    return kb, vb, dist, valid


def _swa_scores_t(st_g, j, h, dist, valid):
    slope = 2.0 ** (-8.0 * (h + 1) / N_HEADS_A)
    st = st_g[:, BLOCK * j:BLOCK * (j + 1)] * _SWA_SCALE - slope * dist
    return jnp.where(valid, st, NEG)


def _group_t(xt, kh):
    return jnp.concatenate([_head_rows(xt, _GROUP_A * kh + j) for j in range(_GROUP_A)], axis=1).astype(BF16)


def _swa_fwd(proj, posc, posr, sinks):
    t = proj.shape[0]
    nb = t // BLOCK

    def body(q_ref, kc_ref, kp_ref, vc_ref, vp_ref, pq_ref, pc_ref, pp_ref, sink_ref, o_ref, l_ref):
        n = pl.program_id(0)
        kb, vb, dist, valid = _swa_band(n, kp_ref, kc_ref, vp_ref, vc_ref, pq_ref, pp_ref, pc_ref)
        q_t, vb_t = q_ref[...].T, vb.T
        out_t, lse = [], []
        for kh in range(N_KV_A):
            st_g = _dot(_head_cols(kb, kh).astype(BF16), _group_t(q_t, kh))
            ps = []
            for j in range(_GROUP_A):
                h = _GROUP_A * kh + j
                st = _swa_scores_t(st_g, j, h, dist, valid)
                sink = sink_ref[0:1, h:h + 1]
                m = jnp.maximum(jnp.max(st, axis=0, keepdims=True), sink)
                e = jnp.exp(st - m)
                den = jnp.sum(e, axis=0, keepdims=True) + jnp.exp(sink - m)
                ps.append((e / den).astype(BF16))
                lse.append(m + jnp.log(den))
            o_g = _dot(_head_rows(vb_t, kh).astype(BF16), jnp.concatenate(ps, axis=1))
            out_t.extend(o_g[:, BLOCK * j:BLOCK * (j + 1)] for j in range(_GROUP_A))
        o_ref[...] = jnp.concatenate(out_t, axis=0).T
        l_ref[...] = jnp.concatenate(lse, axis=0)

    cur = lambda n: (n, 0)
    prev = lambda n: jnp.maximum(n - 1, 0)
    return pl.pallas_call(
        body, name="swa_fwd", grid=(nb,),
        in_specs=[pl.BlockSpec((BLOCK, WIDTH_A), lambda n: (n, _QA_BLK)),
                  pl.BlockSpec((BLOCK, LANES), lambda n: (n, _KA_BLK)),
                  pl.BlockSpec((BLOCK, LANES), lambda n: (prev(n), _KA_BLK)),
                  pl.BlockSpec((BLOCK, LANES), lambda n: (n, _VA_BLK)),
                  pl.BlockSpec((BLOCK, LANES), lambda n: (prev(n), _VA_BLK)),
                  pl.BlockSpec((1, BLOCK), lambda n: (0, n)),
                  pl.BlockSpec((BLOCK, 1), cur),
                  pl.BlockSpec((BLOCK, 1), lambda n: (prev(n), 0)),
                  _full((1, N_HEADS_A))],
        out_specs=[pl.BlockSpec((BLOCK, WIDTH_A), cur), pl.BlockSpec((N_HEADS_A, BLOCK), lambda n: (0, n))],
        out_shape=[jax.ShapeDtypeStruct((t, WIDTH_A), F32), jax.ShapeDtypeStruct((N_HEADS_A, t), F32)],
        compiler_params=_params(("parallel",)),
    )(proj, proj, proj, proj, proj, posr, posc, posc, sinks)


def _rope_coeffs(pos, freq):
    ang = pos * freq
    cosv, sinv = jnp.cos(ang), jnp.sin(ang)
    lane = lax.broadcasted_iota(jnp.int32, ang.shape, 1)
    lo = (lane >= QK_NOPE) & (lane < QK_NOPE + QK_ROPE // 2)
    hi = (lane >= QK_NOPE + QK_ROPE // 2) & (lane < QK_NOPE + QK_ROPE)
    c = jnp.where(lane < QK_NOPE, 1.0, jnp.where(lo | hi, cosv, 0.0))
    s = jnp.where(lo, -sinv, jnp.where(hi, sinv, 0.0))
    return c, s, lo, hi


def _rope(xh, c, s, lo):
    up = pltpu.roll(xh, LANES - QK_ROPE // 2, axis=1)
    dn = pltpu.roll(xh, QK_ROPE // 2, axis=1)
    return xh * c + jnp.where(lo, up, dn) * s


def _unrope(dh, c, s, lo, hi):
    g = dh * s
    up = pltpu.roll(g, LANES - QK_ROPE // 2, axis=1)
    dn = pltpu.roll(g, QK_ROPE // 2, axis=1)
    return dh * c + jnp.where(hi, dn, jnp.where(lo, up, 0.0))


_TQ = 512
_MLA_SCALE = Q_HEAD_B ** -0.5


def _mla_prep_fwd(proj, posc, freq, qan, kvan, wq, wk, wv):
    t = proj.shape[0]
    tm = _TQ
    nb = t // tm

    def body(cq_ref, ckv_ref, kr_ref, pos_ref, f_ref, qan_ref, kvan_ref, wq_ref, wk_ref, wv_ref,
             q_ref, k_ref, qt_ref, kt_ref, v_ref, vt_ref):
        cq = cq_ref[...]
        cqn = ((cq * _rms(cq)) * qan_ref[...]).astype(BF16)
        ckv = ckv_ref[...]
        ckvn = ((ckv * _rms(ckv)) * kvan_ref[...]).astype(BF16)
        qb = _dot(cqn, wq_ref[...])
        kb = _dot(ckvn, wk_ref[...])
        vb = _dot(ckvn, wv_ref[...])
        vbt = vb.T
        c, s, lo, _ = _rope_coeffs(pos_ref[...], f_ref[...])
        kr = _rope(kr_ref[...], c, s, lo)
        for h in range(N_HEADS_B):
            sl = slice(HEAD_PAD * h, HEAD_PAD * (h + 1))
            q_h = _rope(qb[:, sl], c, s, lo)
            k_h = kb[:, sl] + kr
            q_ref[:, sl] = q_h.astype(BF16)
            k_ref[:, sl] = k_h.astype(BF16)
            qt_ref[h, 0] = q_h.T.astype(BF16)
            kt_ref[h, 0] = k_h.T.astype(BF16)
            v_ref[h] = vb[:, V_DIM_B * h:V_DIM_B * (h + 1)].astype(BF16)
            vt_ref[h, 0] = vbt[V_DIM_B * h:V_DIM_B * (h + 1), :].astype(BF16)

    row = lambda i: (i, 0)
    blk4 = lambda d: pl.BlockSpec((N_HEADS_B, 1, d, tm), lambda i: (0, i, 0, 0))
    return pl.pallas_call(
        body, name="mla_prep_fwd", grid=(nb,),
        in_specs=[pl.BlockSpec((tm, Q_LORA), lambda i: (i, _CQ_BLK)),
                  pl.BlockSpec((tm, LANES), lambda i: (i, _CKV_BLK)),
                  pl.BlockSpec((tm, LANES), lambda i: (i, _KR_BLK)),
                  pl.BlockSpec((tm, 1), row), _full((1, LANES)), _full((1, Q_LORA)), _full((1, KV_LORA)),
                  _full((Q_LORA, MLA_W)), _full((KV_LORA, MLA_W)), _full((KV_LORA, N_HEADS_B * V_DIM_B))],
        out_specs=[pl.BlockSpec((tm, MLA_W), row), pl.BlockSpec((tm, MLA_W), row), blk4(HEAD_PAD), blk4(HEAD_PAD),
                   pl.BlockSpec((N_HEADS_B, tm, V_DIM_B), lambda i: (0, i, 0)), blk4(V_DIM_B)],
        out_shape=[jax.ShapeDtypeStruct((t, MLA_W), BF16), jax.ShapeDtypeStruct((t, MLA_W), BF16),
                   jax.ShapeDtypeStruct((N_HEADS_B, nb, HEAD_PAD, tm), BF16),
                   jax.ShapeDtypeStruct((N_HEADS_B, nb, HEAD_PAD, tm), BF16),
                   jax.ShapeDtypeStruct((N_HEADS_B, t, V_DIM_B), BF16),
                   jax.ShapeDtypeStruct((N_HEADS_B, nb, V_DIM_B, tm), BF16)],
        compiler_params=_params(("parallel",)),
    )(proj, proj, proj, posc, freq, qan, kvan, wq, wk, wv)


_LOG2E = 1.4426950408889634
_MLA_SCALE2 = _MLA_SCALE * _LOG2E


def _scores_t(k, qt, diagonal):
    st = _dot(k, qt) * _MLA_SCALE2
    if diagonal:
        key = lax.broadcasted_iota(jnp.int32, st.shape, 0)
        qry = lax.broadcasted_iota(jnp.int32, st.shape, 1)
        st = jnp.where(key <= qry, st, NEG)
    return st


def _mla_fwd(k, qt, vt, w_src):
    t = k.shape[0]
    nb = t // _TQ

    def body(k_ref, qt_ref, vt_ref, w_ref, o_ref, l_ref, wg_ref, send_sems, recv_sems, local_sem):
        qi = pl.program_id(1)
        first = (pl.program_id(0) == 0) & (qi == 0)
        last = (pl.program_id(0) == N_HEADS_B - 1) & (qi == nb - 1)

        @pl.when(first)
        def _():
            _gather_start(w_ref, wg_ref, send_sems, recv_sems, local_sem)

        q_t = qt_ref[0, 0]

        def scores(kj, diagonal=False):
            return _scores_t(k_ref[pl.ds(pl.multiple_of(kj * _TQ, _TQ), _TQ), :], q_t, diagonal)

        def update(carry, st, kj):
            m, l, acc = carry
            m_new = jnp.maximum(m, jnp.max(st, axis=0, keepdims=True))
            alpha = jnp.exp2(m - m_new)
            p = jnp.exp2(st - m_new)
            l = alpha * l + jnp.sum(p, axis=0, keepdims=True)
            acc = alpha * acc + _dot(vt_ref[0, kj], p.astype(BF16))
            return m_new, l, acc

        def pair(i, carry):
            st_a, st_b = scores(2 * i), scores(2 * i + 1)
            return update(update(carry, st_a, 2 * i), st_b, 2 * i + 1)

        init = (jnp.full((1, _TQ), NEG, F32), jnp.zeros((1, _TQ), F32), jnp.zeros((V_DIM_B, _TQ), F32))
        carry = lax.fori_loop(0, qi // 2, pair, init)
        carry = lax.fori_loop(2 * (qi // 2), qi, lambda kj, cr: update(cr, scores(kj), kj), carry)
        m, l, acc = update(carry, scores(qi, True), qi)
        o_ref[0, 0] = acc / l
        l_ref[0, 0] = m + jnp.log(l) * _LOG2E

        @pl.when(last)
        def _():
            _gather_wait(w_ref, wg_ref, send_sems, recv_sems, local_sem)

    return pl.pallas_call(
        body, name="mla_fwd", grid=(N_HEADS_B, nb),
        in_specs=[pl.BlockSpec((t, HEAD_PAD), lambda h, qi: (0, h)),
                  pl.BlockSpec((1, 1, HEAD_PAD, _TQ), lambda h, qi: (h, qi, 0, 0)),
                  pl.BlockSpec((1, nb, V_DIM_B, _TQ), lambda h, qi: (h, 0, 0, 0)), _HBM],
        out_specs=[pl.BlockSpec((1, 1, V_DIM_B, _TQ), lambda h, qi: (h, qi, 0, 0)),
                   pl.BlockSpec((1, 1, 1, _TQ), lambda h, qi: (h, qi, 0, 0)), _HBM],
        out_shape=[jax.ShapeDtypeStruct((N_HEADS_B, nb, V_DIM_B, _TQ), F32),
                   jax.ShapeDtypeStruct((N_HEADS_B, nb, 1, _TQ), F32),
                   jax.ShapeDtypeStruct((N_CHIPS,) + w_src.shape, w_src.dtype)],
        scratch_shapes=[pltpu.SemaphoreType.DMA((3,)), pltpu.SemaphoreType.DMA((3,)), pltpu.SemaphoreType.DMA(())],
        compiler_params=_params(("arbitrary", "arbitrary")),
    )(k, qt, vt, w_src)


def _ot_spec(tm, d):
    per = _TQ // tm
    return pl.BlockSpec((N_HEADS_B, 1, d, tm), lambda i: (0, i // per, 0, i % per))


def _mix_out_fwd(out_a, out_bt, proj, x, w_oa, w_ob, wb, g2, g3):
    t = x.shape[0]
    tm = 256

    def body(oa_ref, obt_ref, ga_ref, gb_ref, x_ref, woa_ref, wob_ref, wout_ref, g2_ref, g3_ref,
             mg_ref, y_ref, x1_ref, h2_ref):
        oa = _dot(oa_ref[...].astype(BF16), woa_ref[...])
        obt = obt_ref[...].reshape(N_HEADS_B * V_DIM_B, tm).astype(BF16)
        ob = _dot_tn(obt, wob_ref[...])
        merged = (jax.nn.sigmoid(ga_ref[...]) * oa + jax.nn.sigmoid(gb_ref[...]) * ob).astype(BF16)
        mg_ref[...] = merged
        y = _dot(merged, wout_ref[...].reshape(D_MODEL, D_MODEL))
        y_ref[...] = y
        x1 = x_ref[...] + (y * _rms(y)) * g2_ref[...]
        x1_ref[...] = x1
        h2_ref[...] = ((x1 * _rms(x1)) * g3_ref[...]).astype(BF16)

    row = lambda i: (i, 0)
    blk = pl.BlockSpec((tm, D_MODEL), row)
    return pl.pallas_call(
        body, name="mix_out_fwd", grid=(t // tm,),
        in_specs=[pl.BlockSpec((tm, WIDTH_A), row), _ot_spec(tm, V_DIM_B), pl.BlockSpec((tm, D_MODEL), lambda i: (i, 0)),
                  pl.BlockSpec((tm, D_MODEL), lambda i: (i, 1)), blk,
                  _full((WIDTH_A, D_MODEL)), _full((N_HEADS_B * V_DIM_B, D_MODEL)), _wb_spec("w_out"),
                  _full((1, D_MODEL)), _full((1, D_MODEL))],
        out_specs=[blk, blk, blk, blk],
        out_shape=[jax.ShapeDtypeStruct((t, D_MODEL), BF16), jax.ShapeDtypeStruct((t, D_MODEL), F32),
                   jax.ShapeDtypeStruct((t, D_MODEL), F32), jax.ShapeDtypeStruct((t, D_MODEL), BF16)],
        compiler_params=_params(("parallel",)),
    )(out_a, out_bt, proj, proj, x, w_oa, w_ob, wb, g2, g3)


_TM_MLP = 256


def _up_fwd(h2, wb):
    t = h2.shape[0]
    tm = _TM_MLP

    def body(h_ref, w_ref, a_ref):
        hv = h_ref[...]
        for j in range(N_CHIPS):
            u = _dot(hv, w_ref[j])
            a_ref[:, D_MODEL * j:D_MODEL * (j + 1)] = jnp.square(jnp.maximum(u, 0.0)).astype(BF16)

    return pl.pallas_call(
        body, name="up_fwd", grid=(t // tm,),
        in_specs=[pl.BlockSpec((tm, D_MODEL), lambda i: (i, 0)), _wb_spec("w_up")],
        out_specs=pl.BlockSpec((tm, D_FF), lambda i: (i, 0)),
        out_shape=jax.ShapeDtypeStruct((t, D_FF), BF16),
        compiler_params=_params(("parallel",)),
    )(h2, wb)


def _down_fwd_loss(a, wb, x1, target, g4):
    t = a.shape[0]
    tm = _TM_MLP

    def body(a_ref, w_ref, x1_ref, tg_ref, g_ref, dx2_ref, dyd_ref, dg_ref, loss_ref):
        @pl.when(pl.program_id(0) == 0)
        def _():
            dg_ref[...] = jnp.zeros(dg_ref.shape, F32)
            loss_ref[...] = jnp.zeros(loss_ref.shape, F32)

        yd = _dot(a_ref[...], w_ref[...].reshape(D_FF, D_MODEL))
        r = _rms(yd)
        n = yd * r
        diff = (x1_ref[...] + n * g_ref[...]) - tg_ref[...]
        loss_ref[...] += 0.5 * jnp.sum(jnp.mean(diff * diff, axis=-1, keepdims=True), axis=0, keepdims=True)
        dx2 = diff * (1.0 / D_MODEL)
        dx2_ref[...] = dx2
        dyd, dg = _norm_bwd(dx2, n, r, g_ref[...])
        dyd_ref[...] = dyd.astype(BF16)
        dg_ref[...] += dg

    row = lambda i: (i, 0)
    blk = pl.BlockSpec((tm, D_MODEL), row)
    return pl.pallas_call(
        body, name="down_fwd_loss", grid=(t // tm,),
        in_specs=[pl.BlockSpec((tm, D_FF), row), _wb_spec("w_down"), blk, blk, _full((1, D_MODEL))],
        out_specs=[blk, blk, _full((1, D_MODEL)), _full((1, LANES))],
        out_shape=[jax.ShapeDtypeStruct((t, D_MODEL), F32), jax.ShapeDtypeStruct((t, D_MODEL), BF16),
                   jax.ShapeDtypeStruct((1, D_MODEL), F32), jax.ShapeDtypeStruct((1, LANES), F32)],
        compiler_params=_params(("arbitrary",)),
    )(a, wb, x1, target, g4)


def _matmul_tn(a, b, name, tm, tn, tk=1024, shard_major=False):
    t, m = a.shape
    n = b.shape[1]
    tk = min(tk, t)
    nk = t // tk

    def body(a_ref, b_ref, o_ref):
        @pl.when(pl.program_id(2) == 0)
        def _():
            o_ref[...] = jnp.zeros(o_ref.shape, F32)

        acc = _dot_tn(a_ref[...].astype(BF16), b_ref[...].astype(BF16))
        o_ref[...] += acc[None] if shard_major else acc

    if shard_major:
        out_spec = pl.BlockSpec((1, tm, tn), lambda i, j, k: (j, i, 0))
        out_shape = jax.ShapeDtypeStruct((n // tn, m, tn), F32)
    else:
        out_spec = pl.BlockSpec((tm, tn), lambda i, j, k: (i, j))
        out_shape = jax.ShapeDtypeStruct((m, n), F32)
    return pl.pallas_call(
        body, name=name, grid=(m // tm, n // tn, nk),
        in_specs=[pl.BlockSpec((tk, tm), lambda i, j, k: (k, i)), pl.BlockSpec((tk, tn), lambda i, j, k: (k, j))],
        out_specs=out_spec, out_shape=out_shape,
        compiler_params=_params(("parallel", "parallel", "arbitrary")),
    )(a, b)


def _down_bwd(dyd, wb, a):
    t = dyd.shape[0]
    tm = _TM_MLP

    def body(d_ref, w_ref, a_ref, du_ref):
        da = _dot_nt(d_ref[...], w_ref[...].reshape(D_FF, D_MODEL))
        du_ref[...] = (da * (2.0 * jnp.sqrt(a_ref[...].astype(F32)))).astype(BF16)

    row = lambda i: (i, 0)
    return pl.pallas_call(
        body, name="down_bwd", grid=(t // tm,),
        in_specs=[pl.BlockSpec((tm, D_MODEL), row), _wb_spec("w_down"), pl.BlockSpec((tm, D_FF), row)],
        out_specs=pl.BlockSpec((tm, D_FF), row),
        out_shape=jax.ShapeDtypeStruct((t, D_FF), BF16),
        compiler_params=_params(("parallel",)),
    )(dyd, wb, a)


def _up_bwd(du, wb, x1, dx2, y, g3, g2):
    t = du.shape[0]
    tm = _TM_MLP

    def body(du_ref, w_ref, x1_ref, dx2_ref, y_ref, g3_ref, g2_ref, dx1_ref, dy_ref, dg3_ref, dg2_ref):
        @pl.when(pl.program_id(0) == 0)
        def _():
            dg3_ref[...] = jnp.zeros(dg3_ref.shape, F32)
            dg2_ref[...] = jnp.zeros(dg2_ref.shape, F32)

        dh2 = _dot_nt(du_ref[:, 0:D_MODEL], w_ref[0])
        for j in range(1, N_CHIPS):
            dh2 = dh2 + _dot_nt(du_ref[:, D_MODEL * j:D_MODEL * (j + 1)], w_ref[j])
        x1 = x1_ref[...]
        r3 = _rms(x1)
        d3, dg3 = _norm_bwd(dh2, x1 * r3, r3, g3_ref[...])
        dx1 = dx2_ref[...] + d3
        dx1_ref[...] = dx1
        dg3_ref[...] += dg3
        y = y_ref[...]
        r2 = _rms(y)
        dy, dg2 = _norm_bwd(dx1, y * r2, r2, g2_ref[...])
        dy_ref[...] = dy.astype(BF16)
        dg2_ref[...] += dg2

    row = lambda i: (i, 0)
    blk = pl.BlockSpec((tm, D_MODEL), row)
    return pl.pallas_call(
        body, name="up_bwd", grid=(t // tm,),
        in_specs=[pl.BlockSpec((tm, D_FF), row), _wb_spec("w_up"),
                  blk, blk, blk, _full((1, D_MODEL)), _full((1, D_MODEL))],
        out_specs=[blk, blk, _full((1, D_MODEL)), _full((1, D_MODEL))],
        out_shape=[jax.ShapeDtypeStruct((t, D_MODEL), F32), jax.ShapeDtypeStruct((t, D_MODEL), BF16),
                   jax.ShapeDtypeStruct((1, D_MODEL), F32), jax.ShapeDtypeStruct((1, D_MODEL), F32)],
        compiler_params=_params(("arbitrary",)),
    )(du, wb, x1, dx2, y, g3, g2)


def _mix_out_bwd(dy, out_a, out_bt, proj, w_oa, w_ob, wb):
    t = dy.shape[0]
    tm = 256
    nb = t // _TQ

    def body(dy_ref, oa_ref, obt_ref, ga_ref, gb_ref, woa_ref, wob_ref, wout_ref,
             doa_ref, dob_ref, dga_ref, dgb_ref, da_ref, db_ref, dbt_ref, dela_ref, delb_ref):
        dm = _dot_nt(dy_ref[...], wout_ref[...].reshape(D_MODEL, D_MODEL))
        out_a_v = oa_ref[...]
        out_bt_v = obt_ref[...].reshape(N_HEADS_B * V_DIM_B, tm)
        oa = _dot(out_a_v.astype(BF16), woa_ref[...])
        ob = _dot_tn(out_bt_v.astype(BF16), wob_ref[...])
        sa, sb = jax.nn.sigmoid(ga_ref[...]), jax.nn.sigmoid(gb_ref[...])
        doa = (dm * sa).astype(BF16)
        dob = (dm * sb).astype(BF16)
        doa_ref[...] = doa
        dob_ref[...] = dob
        dga_ref[...] = (dm * oa * (sa * (1.0 - sa))).astype(BF16)
        dgb_ref[...] = (dm * ob * (sb * (1.0 - sb))).astype(BF16)
        d_out_a = _dot_nt(doa, woa_ref[...])
        da_ref[...] = d_out_a
        prod_at = (d_out_a * out_a_v).T
        dela_ref[...] = jnp.concatenate(
            [jnp.sum(_head_rows(prod_at, h), axis=0, keepdims=True) for h in range(N_HEADS_A)], axis=0)
        d_out_b = _dot_nt(dob, wob_ref[...])
        d_out_bt = _dot_nt(wob_ref[...], dob)
        prod_bt = d_out_bt * out_bt_v
        for h in range(N_HEADS_B):
            db_ref[h] = d_out_b[:, V_DIM_B * h:V_DIM_B * (h + 1)].astype(BF16)
            dbt_ref[h, 0] = d_out_bt[V_DIM_B * h:V_DIM_B * (h + 1), :].astype(BF16)
            delb_ref[h, 0] = jnp.sum(prod_bt[V_DIM_B * h:V_DIM_B * (h + 1), :], axis=0, keepdims=True)

    row = lambda i: (i, 0)
    blk = pl.BlockSpec((tm, D_MODEL), row)
    return pl.pallas_call(
        body, name="mix_out_bwd", grid=(t // tm,),
        in_specs=[blk, pl.BlockSpec((tm, WIDTH_A), row), _ot_spec(tm, V_DIM_B), pl.BlockSpec((tm, D_MODEL), lambda i: (i, 0)),
                  pl.BlockSpec((tm, D_MODEL), lambda i: (i, 1)),
                  _full((WIDTH_A, D_MODEL)), _full((N_HEADS_B * V_DIM_B, D_MODEL)), _wb_spec("w_out")],
        out_specs=[blk, blk, blk, blk, pl.BlockSpec((tm, WIDTH_A), row),
                   pl.BlockSpec((N_HEADS_B, tm, V_DIM_B), lambda i: (0, i, 0)), _ot_spec(tm, V_DIM_B),
                   pl.BlockSpec((N_HEADS_A, tm), lambda i: (0, i)), _ot_spec(tm, 1)],
        out_shape=[jax.ShapeDtypeStruct((t, D_MODEL), BF16)] * 4
        + [jax.ShapeDtypeStruct((t, WIDTH_A), F32), jax.ShapeDtypeStruct((N_HEADS_B, t, V_DIM_B), BF16),
           jax.ShapeDtypeStruct((N_HEADS_B, nb, V_DIM_B, _TQ), BF16),
           jax.ShapeDtypeStruct((N_HEADS_A, t), F32), jax.ShapeDtypeStruct((N_HEADS_B, nb, 1, _TQ), F32)],
        compiler_params=_params(("parallel",)),
    )(dy, out_a, out_bt, proj, proj, w_oa, w_ob, wb)


def _dw_ob(out_bt, dob):
    t = dob.shape[0]
    nb = t // _TQ

    def body(obt_ref, dob_ref, o_ref):
        @pl.when(pl.program_id(0) == 0)
        def _():
            o_ref[...] = jnp.zeros(o_ref.shape, F32)

        obt = obt_ref[...].reshape(N_HEADS_B * V_DIM_B, _TQ).astype(BF16)
        o_ref[...] += _dot(obt, dob_ref[...])

    return pl.pallas_call(
        body, name="dw_o_b", grid=(nb,),
        in_specs=[pl.BlockSpec((N_HEADS_B, 1, V_DIM_B, _TQ), lambda i: (0, i, 0, 0)),
                  pl.BlockSpec((_TQ, D_MODEL), lambda i: (i, 0))],
        out_specs=_full((N_HEADS_B * V_DIM_B, D_MODEL)),
        out_shape=jax.ShapeDtypeStruct((N_HEADS_B * V_DIM_B, D_MODEL), F32),
        compiler_params=_params(("arbitrary",)),
    )(out_bt, dob)


def _mla_bwd(q, k, qt, kt, v, d_out, d_out_t, lse, delta, gp):
    t = q.shape[0]
    nb = t // _TQ

    def body(k_ref, kt_ref, v_ref, q_ref, qt_ref, do_ref, dot_ref, l_ref, d_ref, gp_ref,
             dqt_ref, dk_ref, dv_ref, land_ref, send_sems, recv_sems):
        kj = pl.program_id(1)

        @pl.when((pl.program_id(0) == 0) & (kj == 0))
        def _():
            _scatter_start(gp_ref, land_ref, send_sems, recv_sems)

        @pl.when(kj == 0)
        def _():
            dqt_ref[...] = jnp.zeros(dqt_ref.shape, F32)

        kv, k_t, vv = k_ref[...], kt_ref[0, 0], v_ref[0]

        def products(qi, diagonal=False):
            return _scores_t(kv, qt_ref[0, qi], diagonal), _dot(vv, dot_ref[0, qi])

        def update(carry, prods, qi):
            dk, dv = carry
            st, dpt = prods
            rows = pl.ds(pl.multiple_of(qi * _TQ, _TQ), _TQ)
            pt = jnp.exp2(st - l_ref[0, qi])
            dv = dv + _dot(pt.astype(BF16), do_ref[0, rows, :])
            dst = (pt * (dpt - d_ref[0, qi]) * _MLA_SCALE).astype(BF16)
            dk = dk + _dot(dst, q_ref[rows, :])
            dqt_ref[0, qi] += _dot(k_t, dst)
            return dk, dv

        def pair(i, carry):
            qa = kj + 1 + 2 * i
            pa, pb = products(qa), products(qa + 1)
            return update(update(carry, pa, qa), pb, qa + 1)

        init = (jnp.zeros((_TQ, HEAD_PAD), F32), jnp.zeros((_TQ, V_DIM_B), F32))
        carry = update(init, products(kj, True), kj)
        pairs = (nb - 1 - kj) // 2
        carry = lax.fori_loop(0, pairs, pair, carry)
        dk, dv = lax.fori_loop(kj + 1 + 2 * pairs, nb, lambda qi, cr: update(cr, products(qi), qi), carry)
        dk_ref[...] = dk
        dv_ref[0] = dv

        @pl.when((pl.program_id(0) == N_HEADS_B - 1) & (kj == nb - 1))
        def _():
            _scatter_wait(gp_ref, land_ref, send_sems, recv_sems)

    head4 = lambda d: pl.BlockSpec((1, nb, d, _TQ), lambda h, kj: (h, 0, 0, 0))
    return pl.pallas_call(
        body, name="mla_bwd", grid=(N_HEADS_B, nb),
        in_specs=[pl.BlockSpec((_TQ, HEAD_PAD), lambda h, kj: (kj, h)),
                  pl.BlockSpec((1, 1, HEAD_PAD, _TQ), lambda h, kj: (h, kj, 0, 0)),
                  pl.BlockSpec((1, _TQ, V_DIM_B), lambda h, kj: (h, kj, 0)),
                  pl.BlockSpec((t, HEAD_PAD), lambda h, kj: (0, h)), head4(HEAD_PAD),
                  pl.BlockSpec((1, t, V_DIM_B), lambda h, kj: (h, 0, 0)), head4(V_DIM_B), head4(1), head4(1), _HBM],
        out_specs=[head4(HEAD_PAD), pl.BlockSpec((_TQ, HEAD_PAD), lambda h, kj: (kj, h)),
                   pl.BlockSpec((1, _TQ, V_DIM_B), lambda h, kj: (h, kj, 0)), _HBM],
        out_shape=[jax.ShapeDtypeStruct((N_HEADS_B, nb, HEAD_PAD, _TQ), F32), jax.ShapeDtypeStruct((t, MLA_W), F32),
                   jax.ShapeDtypeStruct((N_HEADS_B, t, V_DIM_B), F32),
                   jax.ShapeDtypeStruct((3,) + gp.shape[1:], gp.dtype)],
        scratch_shapes=[pltpu.SemaphoreType.DMA((3,)), pltpu.SemaphoreType.DMA((3,))],
        compiler_params=_params(("arbitrary", "arbitrary")),
    )(k, kt, v, q, qt, d_out, d_out_t, lse, delta, gp)


def _mla_prep_bwd(dqt, dk, dv, proj, posc, freq, qan, kvan, wq, wk, wv):
    t = dk.shape[0]
    tm = _TQ

    def body(dqt_ref, dk_ref, dv_ref, cq_ref, ckv_ref, pos_ref, f_ref, qan_ref, kvan_ref, wq_ref, wk_ref, wv_ref,
             dcq_ref, dckv_ref, dkr_ref, dwq_ref, dwk_ref, dwv_ref, dqan_ref, dkvan_ref):
        @pl.when(pl.program_id(0) == 0)
        def _():
            for r in (dwq_ref, dwk_ref, dwv_ref, dqan_ref, dkvan_ref):
                r[...] = jnp.zeros(r.shape, F32)

        cq = cq_ref[...]
        rq = _rms(cq)
        nq_ = cq * rq
        cqn = (nq_ * qan_ref[...]).astype(BF16)
        ckv = ckv_ref[...]
        rkv = _rms(ckv)
        nkv = ckv * rkv
        ckvn = (nkv * kvan_ref[...]).astype(BF16)
        c, s, lo, hi = _rope_coeffs(pos_ref[...], f_ref[...])
        dkv = dk_ref[...]
        dkr = jnp.zeros((tm, LANES), F32)
        dqb = []
        for h in range(N_HEADS_B):
            dqb.append(_unrope(dqt_ref[h, 0].T, c, s, lo, hi).astype(BF16))
            dkr = dkr + dkv[:, HEAD_PAD * h:HEAD_PAD * (h + 1)]
        dqb = jnp.concatenate(dqb, axis=1)
        dkr_ref[...] = jnp.where(lo | hi, _unrope(dkr, c, s, lo, hi), 0.0).astype(BF16)
        dkb = dkv.astype(BF16)
        dvb = jnp.concatenate([dv_ref[h] for h in range(N_HEADS_B)], axis=1).astype(BF16)
        dwq_ref[...] += _dot_tn(cqn, dqb)
        dwk_ref[...] += _dot_tn(ckvn, dkb)
        dwv_ref[...] += _dot_tn(ckvn, dvb)
        dcqn = _dot_nt(dqb, wq_ref[...])
        dckvn = _dot_nt(dkb, wk_ref[...]) + _dot_nt(dvb, wv_ref[...])
        dcq, dqan = _norm_bwd(dcqn, nq_, rq, qan_ref[...])
        dckv, dkvan = _norm_bwd(dckvn, nkv, rkv, kvan_ref[...])
        dcq_ref[...] = dcq.astype(BF16)
        dckv_ref[...] = dckv.astype(BF16)
        dqan_ref[...] += dqan
        dkvan_ref[...] += dkvan

    row = lambda i: (i, 0)
    vw = N_HEADS_B * V_DIM_B
    return pl.pallas_call(
        body, name="mla_prep_bwd", grid=(t // tm,),
        in_specs=[pl.BlockSpec((N_HEADS_B, 1, HEAD_PAD, tm), lambda i: (0, i, 0, 0)), pl.BlockSpec((tm, MLA_W), row),
                  pl.BlockSpec((N_HEADS_B, tm, V_DIM_B), lambda i: (0, i, 0)),
                  pl.BlockSpec((tm, Q_LORA), lambda i: (i, _CQ_BLK)),
                  pl.BlockSpec((tm, LANES), lambda i: (i, _CKV_BLK)),
                  pl.BlockSpec((tm, 1), row), _full((1, LANES)), _full((1, Q_LORA)), _full((1, KV_LORA)),
                  _full((Q_LORA, MLA_W)), _full((KV_LORA, MLA_W)), _full((KV_LORA, vw))],
        out_specs=[pl.BlockSpec((tm, Q_LORA), row), pl.BlockSpec((tm, LANES), row), pl.BlockSpec((tm, LANES), row),
                   _full((Q_LORA, MLA_W)), _full((KV_LORA, MLA_W)), _full((KV_LORA, vw)),
                   _full((1, Q_LORA)), _full((1, KV_LORA))],
        out_shape=[jax.ShapeDtypeStruct((t, Q_LORA), BF16), jax.ShapeDtypeStruct((t, LANES), BF16),
                   jax.ShapeDtypeStruct((t, LANES), BF16),
                   jax.ShapeDtypeStruct((Q_LORA, MLA_W), F32), jax.ShapeDtypeStruct((KV_LORA, MLA_W), F32),
                   jax.ShapeDtypeStruct((KV_LORA, vw), F32),
                   jax.ShapeDtypeStruct((1, Q_LORA), F32), jax.ShapeDtypeStruct((1, KV_LORA), F32)],
        compiler_params=_params(("arbitrary",)),
    )(dqt, dk, dv, proj, proj, posc, freq, qan, kvan, wq, wk, wv)


def _swa_bwd(proj, d_out, lse, delta, posc, posr, sinks):
    t = proj.shape[0]
    nb = t // BLOCK

    def body(q_ref, kc_ref, kp_ref, vc_ref, vp_ref, do_ref, l_ref, d_ref, pq_ref, pc_ref, pp_ref, sink_ref,
             dq_ref, dk_ref, dv_ref, ds_ref, dkb_s, dvb_s, dk_carry, dv_carry):
        n = pl.program_id(0)

        @pl.when(n == 0)
        def _():
            ds_ref[...] = jnp.zeros(ds_ref.shape, F32)
            dk_carry[...] = jnp.zeros(dk_carry.shape, F32)
            dv_carry[...] = jnp.zeros(dv_carry.shape, F32)

        @pl.when(n < nb)
        def _():
            kb, vb, dist, valid = _swa_band(n, kp_ref, kc_ref, vp_ref, vc_ref, pq_ref, pp_ref, pc_ref)
            qv, dov = q_ref[...], do_ref[...]
            q_t, do_t, kb_t = qv.T, dov.T, kb.T
            lane = lax.broadcasted_iota(jnp.int32, (1, LANES), 1)
            dsink = jnp.zeros((1, LANES), F32)
            dq_t = []
            for kh in range(N_KV_A):
                heads = range(_GROUP_A * kh, _GROUP_A * (kh + 1))
                st_g = _dot(_head_cols(kb, kh).astype(BF16), _group_t(q_t, kh))
                dpt_g = _dot(_head_cols(vb, kh).astype(BF16), _group_t(do_t, kh))
                pts, dsts = [], []
                for j, h in enumerate(heads):
                    st = _swa_scores_t(st_g, j, h, dist, valid)
                    l_h, d_h = l_ref[h:h + 1, :], d_ref[h:h + 1, :]
                    pt = jnp.exp(st - l_h)
                    p_sink = jnp.exp(sink_ref[0:1, h:h + 1] - l_h)
                    dsink = jnp.where(lane == h, jnp.sum(-p_sink * d_h, axis=1, keepdims=True), dsink)
                    dst = pt * (dpt_g[:, BLOCK * j:BLOCK * (j + 1)] - d_h) * _SWA_SCALE
                    pts.append(pt.astype(BF16))
                    dsts.append(dst.astype(BF16))
                pt_g, dst_g = jnp.concatenate(pts, axis=1), jnp.concatenate(dsts, axis=1)
                q_g = jnp.concatenate([_head_cols(qv, h) for h in heads], axis=0).astype(BF16)
                do_g = jnp.concatenate([_head_cols(dov, h) for h in heads], axis=0).astype(BF16)
                dkb_s[:, HEAD_DIM_A * kh:HEAD_DIM_A * (kh + 1)] = _dot(dst_g, q_g)
                dvb_s[:, HEAD_DIM_A * kh:HEAD_DIM_A * (kh + 1)] = _dot(pt_g, do_g)
                dq_g = _dot(_head_rows(kb_t, kh).astype(BF16), dst_g)
                dq_t.extend(dq_g[:, BLOCK * j:BLOCK * (j + 1)] for j in range(_GROUP_A))
            dq_ref[...] = jnp.concatenate(dq_t, axis=0).T
            ds_ref[...] += dsink
            dk_ref[...] = dk_carry[...] + dkb_s[0:BLOCK, :]
            dv_ref[...] = dv_carry[...] + dvb_s[0:BLOCK, :]
            dk_carry[...] = dkb_s[BLOCK:2 * BLOCK, :]
            dv_carry[...] = dvb_s[BLOCK:2 * BLOCK, :]

        @pl.when(n == nb)
        def _():
            dk_ref[...] = dk_carry[...]
            dv_ref[...] = dv_carry[...]

    cur = lambda n: (jnp.minimum(n, nb - 1), 0)
    cur_t = lambda n: (0, jnp.minimum(n, nb - 1))
    prv = lambda n: jnp.maximum(jnp.minimum(n, nb - 1) - 1, 0)
    out_prev = lambda n: (jnp.maximum(n - 1, 0), 0)
    return pl.pallas_call(
        body, name="swa_bwd", grid=(nb + 1,),
        in_specs=[pl.BlockSpec((BLOCK, WIDTH_A), lambda n: (jnp.minimum(n, nb - 1), _QA_BLK)),
                  pl.BlockSpec((BLOCK, LANES), lambda n: (jnp.minimum(n, nb - 1), _KA_BLK)),
                  pl.BlockSpec((BLOCK, LANES), lambda n: (prv(n), _KA_BLK)),
                  pl.BlockSpec((BLOCK, LANES), lambda n: (jnp.minimum(n, nb - 1), _VA_BLK)),
                  pl.BlockSpec((BLOCK, LANES), lambda n: (prv(n), _VA_BLK)),
                  pl.BlockSpec((BLOCK, WIDTH_A), cur), pl.BlockSpec((N_HEADS_A, BLOCK), cur_t),
                  pl.BlockSpec((N_HEADS_A, BLOCK), cur_t), pl.BlockSpec((1, BLOCK), cur_t),
                  pl.BlockSpec((BLOCK, 1), cur), pl.BlockSpec((BLOCK, 1), lambda n: (prv(n), 0)),
                  _full((1, N_HEADS_A))],
        out_specs=[pl.BlockSpec((BLOCK, WIDTH_A), cur), pl.BlockSpec((BLOCK, LANES), out_prev),
                   pl.BlockSpec((BLOCK, LANES), out_prev), _full((1, LANES))],
        out_shape=[jax.ShapeDtypeStruct((t, WIDTH_A), F32), jax.ShapeDtypeStruct((t, LANES), F32),
                   jax.ShapeDtypeStruct((t, LANES), F32), jax.ShapeDtypeStruct((1, LANES), F32)],
        scratch_shapes=[pltpu.VMEM((2 * BLOCK, LANES), F32), pltpu.VMEM((2 * BLOCK, LANES), F32),
                        pltpu.VMEM((BLOCK, LANES), F32), pltpu.VMEM((BLOCK, LANES), F32)],
        compiler_params=_params(("arbitrary",)),
    )(proj, proj, proj, proj, proj, d_out, lse, delta, posr, posc, posc, sinks)


def _in_bwd(dproj, w_in_t, x, dx1, g1, gp):
    t = x.shape[0]
    tm = 256
    steps = t // tm

    def body(dp_ref, w_ref, x_ref, dx1_ref, g_ref, gp_ref, dx_ref, dg_ref, land_ref, send_sems, recv_sems):
        i = pl.program_id(0)

        @pl.when(i == 0)
        def _():
            dg_ref[...] = jnp.zeros(dg_ref.shape, F32)
            _scatter_start(gp_ref, land_ref, send_sems, recv_sems)

        dh = _dot(dp_ref[...], w_ref[...])
        xv = x_ref[...]
        r = _rms(xv)
        dx, dg = _norm_bwd(dh, xv * r, r, g_ref[...])
        dx_ref[...] = dx1_ref[...] + dx
        dg_ref[...] += dg

        @pl.when(i == steps - 1)
        def _():
            _scatter_wait(gp_ref, land_ref, send_sems, recv_sems)

    row = lambda i: (i, 0)
    blk = pl.BlockSpec((tm, D_MODEL), row)
    return pl.pallas_call(
        body, name="in_bwd", grid=(steps,),
        in_specs=[pl.BlockSpec((tm, D_IN_PAD), row), _full((D_IN_PAD, D_MODEL)), blk, blk, _full((1, D_MODEL)), _HBM],
        out_specs=[blk, _full((1, D_MODEL)), _HBM],
        out_shape=[jax.ShapeDtypeStruct((t, D_MODEL), F32), jax.ShapeDtypeStruct((1, D_MODEL), F32),
                   jax.ShapeDtypeStruct((3,) + gp.shape[1:], gp.dtype)],
        scratch_shapes=[pltpu.SemaphoreType.DMA((3,)), pltpu.SemaphoreType.DMA((3,))],
        compiler_params=_params(("arbitrary",)),
    )(dproj, w_in_t, x, dx1, g1, gp)


def _adamw(w, g_parts, m, v, name, block, g_row_off=0):
    r, c = w.shape
    br, bc = block
    ng = len(g_parts)

    def body(*refs):
        w_ref, g_refs, m_ref, v_ref = refs[0], refs[1:1 + ng], refs[1 + ng], refs[2 + ng]
        g_out, d_out, m_out, v_out = refs[3 + ng:]
        g = g_refs[0][...]
        for gr in g_refs[1:]:
            g = g + gr[...]
        m_new = ADAM_B1 * m_ref[...] + (1.0 - ADAM_B1) * g
        v_new = ADAM_B2 * v_ref[...] + (1.0 - ADAM_B2) * jnp.square(g)
        m_hat = m_new / (1.0 - ADAM_B1 ** ADAM_STEP)
        v_hat = v_new / (1.0 - ADAM_B2 ** ADAM_STEP)
        g_out[...] = g
        d_out[...] = -ADAM_LR * (m_hat / (jnp.sqrt(v_hat) + ADAM_EPS) + ADAM_WD * w_ref[...])
        m_out[...] = m_new
        v_out[...] = v_new

    assert g_row_off % br == 0 and r % br == 0 and c % bc == 0
    blk = pl.BlockSpec(block, lambda i, j: (i, j))
    g_blk = pl.BlockSpec(block, lambda i, j: (i + g_row_off // br, j))
    return pl.pallas_call(
        body, name=name, grid=(r // br, c // bc),
        in_specs=[blk] + [g_blk] * ng + [blk, blk], out_specs=[blk] * 4,
        out_shape=[jax.ShapeDtypeStruct((r, c), F32)] * 4,
        compiler_params=_params(("parallel", "parallel")),
    )(w, *g_parts, m, v)


_HBM = pl.BlockSpec(memory_space=pltpu.HBM)


def _other_chips(x, y):
    return ((1 - x, y), (x, 1 - y), (1 - x, 1 - y))


def _gather_copies(src, out, send_sems, recv_sems, local_sem):
    x, y, c = lax.axis_index("x"), lax.axis_index("y"), lax.axis_index("c")
    me = 2 * x + y
    local = pltpu.make_async_copy(src, out.at[me], local_sem)

    def copies(arriving):
        return [pltpu.make_async_remote_copy(src_ref=src, dst_ref=out.at[2 * px + py if arriving else me],
                                             send_sem=send_sems.at[j], recv_sem=recv_sems.at[j], device_id=(px, py, c),
                                             device_id_type=MESH)
                for j, (px, py) in enumerate(_other_chips(x, y))]

    return local, copies


def _gather_start(src, out, send_sems, recv_sems, local_sem):
    local, copies = _gather_copies(src, out, send_sems, recv_sems, local_sem)
    local.start()
    for cp in copies(False):
        cp.start()


def _gather_wait(src, out, send_sems, recv_sems, local_sem):
    local, copies = _gather_copies(src, out, send_sems, recv_sems, local_sem)
    for cp in copies(True):
        cp.wait_recv()
    for cp in copies(False):
        cp.wait_send()
    local.wait()


def _scatter_copies(src, land, send_sems, recv_sems):
    x, y, c = lax.axis_index("x"), lax.axis_index("y"), lax.axis_index("c")
    return [pltpu.make_async_remote_copy(src_ref=src.at[2 * px + py], dst_ref=land.at[j], send_sem=send_sems.at[j],
                                         recv_sem=recv_sems.at[j], device_id=(px, py, c), device_id_type=MESH)
            for j, (px, py) in enumerate(_other_chips(x, y))]


def _scatter_start(src, land, send_sems, recv_sems):
    for cp in _scatter_copies(src, land, send_sems, recv_sems):
        cp.start()


def _scatter_wait(src, land, send_sems, recv_sems):
    copies = _scatter_copies(src, land, send_sems, recv_sems)
    for cp in copies:
        cp.wait_recv()
    for cp in copies:
        cp.wait_send()


def _all_gather_chips(packed):
    def body(src, out, send_sems, recv_sems, local_sem):
        _gather_start(src, out, send_sems, recv_sems, local_sem)
        _gather_wait(src, out, send_sems, recv_sems, local_sem)

    return pl.pallas_call(
        body, name="ag_weights", in_specs=[_HBM], out_specs=_HBM,
        out_shape=jax.ShapeDtypeStruct((N_CHIPS,) + packed.shape, packed.dtype),
        scratch_shapes=[pltpu.SemaphoreType.DMA((3,)), pltpu.SemaphoreType.DMA((3,)), pltpu.SemaphoreType.DMA(())],
    )(packed)


def _sum4(gp, land, chip, name):
    _, r, w = gp.shape
    tr = 128

    def body(chip_ref, o_ref, l_ref, s_ref):
        s_ref[...] = ((o_ref[0] + l_ref[0].astype(F32)) + l_ref[1].astype(F32)) + l_ref[2].astype(F32)

    return pl.pallas_call(
        body, name=name,
        grid_spec=pltpu.PrefetchScalarGridSpec(
            num_scalar_prefetch=1, grid=(r // tr,),
            in_specs=[pl.BlockSpec((1, tr, w), lambda i, chip_ref: (chip_ref[0], i, 0)),
                      pl.BlockSpec((3, tr, w), lambda i, chip_ref: (0, i, 0))],
            out_specs=pl.BlockSpec((tr, w), lambda i, chip_ref: (i, 0))),
        out_shape=jax.ShapeDtypeStruct((r, w), F32),
        compiler_params=_params(("parallel",)),
    )(chip, gp, land)


def _swap_sibling(s, name):
    def body(src, got, send_sem, recv_sem):
        x, y, c = lax.axis_index("x"), lax.axis_index("y"), lax.axis_index("c")
        cp = pltpu.make_async_remote_copy(src_ref=src, dst_ref=got, send_sem=send_sem, recv_sem=recv_sem,
                                          device_id=(x, y, 1 - c), device_id_type=MESH)
        cp.start()
        cp.wait_recv()
        cp.wait_send()

    return pl.pallas_call(
        body, name=name, in_specs=[_HBM], out_specs=_HBM,
        out_shape=jax.ShapeDtypeStruct(s.shape, s.dtype),
        scratch_shapes=[pltpu.SemaphoreType.DMA(()), pltpu.SemaphoreType.DMA(())],
    )(s)


def _all_reduce_small(part):
    n_dev = 8

    def body(src, out, gath, send_sems, recv_sems):
        x, y, c = lax.axis_index("x"), lax.axis_index("y"), lax.axis_index("c")
        me = 4 * x + 2 * y + c
        gath[me] = src[...]
        peers = []
        for k in range(1, n_dev):
            px = 1 - x if (k >> 2) & 1 else x
            py = 1 - y if (k >> 1) & 1 else y
            pc = 1 - c if k & 1 else c
            peers.append((px, py, pc))
        sends = []
        for j, peer in enumerate(peers):
            cp = pltpu.make_async_remote_copy(src_ref=src, dst_ref=gath.at[me], send_sem=send_sems.at[j],
                                              recv_sem=recv_sems.at[j], device_id=peer, device_id_type=MESH)
            cp.start()
            sends.append(cp)
        for j, (px, py, pc) in enumerate(peers):
            pltpu.make_async_remote_copy(src_ref=src, dst_ref=gath.at[4 * px + 2 * py + pc], send_sem=send_sems.at[j],
                                         recv_sem=recv_sems.at[j], device_id=(px, py, pc), device_id_type=MESH).wait_recv()
        for cp in sends:
            cp.wait_send()
        acc = gath[0]
        for d in range(1, n_dev):
            acc = acc + gath[d]
        out[...] = acc

    vmem = pl.BlockSpec(memory_space=pltpu.VMEM)
    return pl.pallas_call(
        body, name="ar_small", in_specs=[vmem], out_specs=vmem,
        out_shape=jax.ShapeDtypeStruct(part.shape, F32),
        scratch_shapes=[pltpu.VMEM((n_dev,) + part.shape, F32), pltpu.SemaphoreType.DMA((n_dev - 1,)),
                        pltpu.SemaphoreType.DMA((n_dev - 1,))],
    )(part)


_W_IN_ROWS = SHARD_SHAPES["w_in"][1]
_KR_ROW = 3200
_KR_PAD_ROW = _KR_BLK * LANES + QK_NOPE


def _shard_rows(name, a):
    return jnp.transpose(a) if name == "w_in" else a.reshape(PACK_ROWS[name], D_MODEL)


def _pack(group, shards, dtype):
    parts = [_shard_rows(n, shards[n]).astype(dtype) for n in group]
    pad = -sum(PACK_ROWS[n] for n in group) % LANES
    if pad:
        parts.append(jnp.zeros((pad, D_MODEL), dtype))
    return jnp.concatenate(parts, axis=0)


def _col_sharded_full(g, name, group):
    r, c = SHARD_SHAPES[name]
    off = _row_offset(group, name)
    blocks = g[:, off:off + PACK_ROWS[name]].reshape(N_CHIPS, r, c)
    return jnp.transpose(blocks, (1, 0, 2)).reshape(r, N_CHIPS * c)


def _col_sharded_blocks(d, name):
    r, c = SHARD_SHAPES[name]
    return jnp.transpose(d.reshape(r, N_CHIPS, c), (1, 0, 2)).reshape(N_CHIPS, PACK_ROWS[name], D_MODEL)


def _weights_a(g):
    dt = g.dtype
    w_in_t = g[:, :_W_IN_ROWS].reshape(N_CHIPS * _W_IN_ROWS, D_MODEL)
    z = lambda n: jnp.zeros((n, D_MODEL), dt)
    w_in_t = jnp.concatenate([w_in_t[:_KR_ROW], z(_KR_PAD_ROW - _KR_ROW), w_in_t[_KR_ROW:],
                              z(D_IN_PAD - _KR_PAD_ROW - QK_ROPE)], axis=0)
    wq = _col_sharded_full(g, "w_q_b", GROUP_A).reshape(Q_LORA, N_HEADS_B, Q_HEAD_B)
    wq_p = jnp.concatenate([wq, jnp.zeros((Q_LORA, N_HEADS_B, HEAD_PAD - Q_HEAD_B), dt)], axis=2).reshape(Q_LORA, MLA_W)
    wkv = _col_sharded_full(g, "w_kv_b", GROUP_A).reshape(KV_LORA, N_HEADS_B, QK_NOPE + V_DIM_B)
    zk = jnp.zeros((KV_LORA, N_HEADS_B, HEAD_PAD - QK_NOPE), dt)
    wk_p = jnp.concatenate([wkv[:, :, :QK_NOPE], zk], axis=2).reshape(KV_LORA, MLA_W)
    wv = wkv[:, :, QK_NOPE:].reshape(KV_LORA, N_HEADS_B * V_DIM_B)
    return dict(w_in=w_in_t, wq=wq_p, wk=wk_p, wv=wv)


def _grad_blocks_a(dw_in_t, dwq_p, dwk_p, dwv):
    dw_in = jnp.concatenate([dw_in_t[:_KR_ROW], dw_in_t[_KR_PAD_ROW:_KR_PAD_ROW + QK_ROPE]], axis=0)
    dwq = dwq_p.reshape(Q_LORA, N_HEADS_B, HEAD_PAD)[:, :, :Q_HEAD_B].reshape(Q_LORA, N_HEADS_B * Q_HEAD_B)
    dwk = dwk_p.reshape(KV_LORA, N_HEADS_B, HEAD_PAD)[:, :, :QK_NOPE]
    dwkv = jnp.concatenate([dwk, dwv.reshape(KV_LORA, N_HEADS_B, V_DIM_B)], axis=2)
    dwkv = dwkv.reshape(KV_LORA, N_HEADS_B * (QK_NOPE + V_DIM_B))
    pad = -sum(PACK_ROWS[n] for n in GROUP_A) % LANES
    return jnp.concatenate([dw_in.reshape(N_CHIPS, _W_IN_ROWS, D_MODEL), _col_sharded_blocks(dwq, "w_q_b"),
                            _col_sharded_blocks(dwkv, "w_kv_b"), jnp.zeros((N_CHIPS, pad, D_MODEL), F32)], axis=1)


def _rope_freq_lanes():
    freqs = ROPE_THETA ** (-jnp.arange(0, QK_ROPE, 2, dtype=F32) / QK_ROPE)
    return jnp.concatenate([jnp.zeros((QK_NOPE,), F32), freqs, freqs,
                            jnp.zeros((HEAD_PAD - Q_HEAD_B,), F32)]).reshape(1, LANES)


def _fwd_bwd(x, positions, target, w):
    t = x.shape[0]
    wa = _weights_a(_all_gather_chips(_pack(GROUP_A, w, BF16)))
    posr = positions.astype(F32).reshape(1, t)
    posc = posr.reshape(t, 1)
    freq = _rope_freq_lanes()
    g1, g2, g3, g4 = w["pre_norm_mix"], w["post_norm_mix"], w["pre_norm_mlp"], w["post_norm_mlp"]
    qan, kvan, sinks = w["q_a_norm"], w["kv_a_norm"], w["sinks"]

    h, proj = _proj_fwd(x, g1, wa["w_in"])
    out_a, lse_a = _swa_fwd(proj, posc, posr, sinks)
    qm, km, qt, kt, vm, vt = _mla_prep_fwd(proj, posc, freq, qan, kvan, wa["wq"], wa["wk"], wa["wv"])
    out_bt, lse_b, wb = _mla_fwd(km, qt, vt, _pack(GROUP_B, w, BF16))
    w_oa, w_ob = _col_sharded_full(wb, "w_o_a", GROUP_B), _col_sharded_full(wb, "w_o_b", GROUP_B)
    merged, y, x1, h2 = _mix_out_fwd(out_a, out_bt, proj, x, w_oa, w_ob, wb, g2, g3)
    a = _up_fwd(h2, wb)
    dx2, dyd, dg4, loss = _down_fwd_loss(a, wb, x1, target, g4)

    dw_down = _matmul_tn(a, dyd, "dw_down", 1024, 1024, tk=512)
    du = _down_bwd(dyd, wb, a)
    dw_up = _matmul_tn(h2, du, "dw_up", 1024, 1024, tk=512, shard_major=True)
    dx1, dy, dg3, dg2 = _up_bwd(du, wb, x1, dx2, y, g3, g2)
    dw_out = _matmul_tn(merged, dy, "dw_out", 1024, 1024, tk=512)
    doa, dob, dga, dgb, d_out_a, d_out_b, d_out_bt, del_a, del_b = _mix_out_bwd(dy, out_a, out_bt, proj, w_oa, w_ob, wb)
    dw_oa = _matmul_tn(out_a, doa, "dw_o_a", 512, 1024)
    dw_ob = _dw_ob(out_bt, dob)
    gp_b = jnp.concatenate([dw_up, dw_down.reshape(N_CHIPS, -1, D_MODEL), dw_out.reshape(N_CHIPS, -1, D_MODEL),
                            _col_sharded_blocks(dw_oa, "w_o_a"), _col_sharded_blocks(dw_ob, "w_o_b")], axis=1)
    dqm, dkm, dvm, land_b = _mla_bwd(qm, km, qt, kt, vm, d_out_b, d_out_bt, lse_b, del_b, gp_b)
    dcq, dckv, dkr, dwq, dwk, dwv, dqan, dkvan = _mla_prep_bwd(
        dqm, dkm, dvm, proj, posc, freq, qan, kvan, wa["wq"], wa["wk"], wa["wv"])
    dqa, dka, dva, dsinks = _swa_bwd(proj, d_out_a, lse_a, del_a, posc, posr, sinks)
    dproj = jnp.concatenate([dga, dgb, dqa.astype(BF16), dka.astype(BF16), dva.astype(BF16), dcq, dckv, dkr], axis=1)
    dw_in_t = _matmul_tn(dproj, h, "dw_in", D_IN_PAD // 2, 1024, tk=512)
    gp_a = _grad_blocks_a(dw_in_t, dwq, dwk, dwv)
    grad_x, dg1, land_a = _in_bwd(dproj, wa["w_in"], x, dx1, g1, gp_a)

    dsmall = dict(pre_norm_mix=dg1, post_norm_mix=dg2, pre_norm_mlp=dg3, post_norm_mlp=dg4,
                  q_a_norm=dqan, kv_a_norm=dkvan, sinks=dsinks[:, :N_HEADS_A])
    return loss, grad_x, {GROUP_A: (gp_a, land_a), GROUP_B: (gp_b, land_b)}, dsmall


def _pack_small(p, extra=None):
    tail = jnp.concatenate([p["q_a_norm"], p["kv_a_norm"], p["sinks"],
                            jnp.zeros((1, D_MODEL - Q_LORA - KV_LORA - N_HEADS_A), F32)], axis=1)
    scalar = jnp.zeros((1, D_MODEL), F32)
    if extra is not None:
        scalar = scalar.at[0, 0].set(extra)
    return jnp.concatenate([p["pre_norm_mix"], p["post_norm_mix"], p["pre_norm_mlp"], p["post_norm_mlp"], tail, scalar,
                            jnp.zeros((2, D_MODEL), F32)], axis=0)


def _unpack_small(b):
    return dict(pre_norm_mix=b[0:1], post_norm_mix=b[1:2], pre_norm_mlp=b[2:3], post_norm_mlp=b[3:4],
                q_a_norm=b[4:5, :Q_LORA], kv_a_norm=b[4:5, Q_LORA:Q_LORA + KV_LORA],
                sinks=b[4:5, Q_LORA + KV_LORA:Q_LORA + KV_LORA + N_HEADS_A])


def kernel(x, positions, pre_norm_mix, w_in, q_a_norm, w_q_b, kv_a_norm, w_kv_b, sinks, w_o_a, w_o_b, w_out, post_norm_mix, pre_norm_mlp, w_up, w_down, post_norm_mlp, loss_target, m_pre_norm_mix, m_w_in, m_q_a_norm, m_w_q_b, m_kv_a_norm, m_w_kv_b, m_sinks, m_w_o_a, m_w_o_b, m_w_out, m_post_norm_mix, m_pre_norm_mlp, m_w_up, m_w_down, m_post_norm_mlp, v_pre_norm_mix, v_w_in, v_q_a_norm, v_w_q_b, v_kv_a_norm, v_w_kv_b, v_sinks, v_w_o_a, v_w_o_b, v_w_out, v_post_norm_mix, v_pre_norm_mlp, v_w_up, v_w_down, v_post_norm_mlp):
    w = dict(pre_norm_mix=pre_norm_mix, w_in=w_in[0], q_a_norm=q_a_norm, w_q_b=w_q_b[0], kv_a_norm=kv_a_norm,
             w_kv_b=w_kv_b[0], sinks=sinks, w_o_a=w_o_a[0], w_o_b=w_o_b[0], w_out=w_out[0],
             post_norm_mix=post_norm_mix, pre_norm_mlp=pre_norm_mlp, w_up=w_up[0], w_down=w_down[0],
             post_norm_mlp=post_norm_mlp)
    m = dict(pre_norm_mix=m_pre_norm_mix, w_in=m_w_in[0], q_a_norm=m_q_a_norm, w_q_b=m_w_q_b[0],
             kv_a_norm=m_kv_a_norm, w_kv_b=m_w_kv_b[0], sinks=m_sinks, w_o_a=m_w_o_a[0], w_o_b=m_w_o_b[0],
             w_out=m_w_out[0], post_norm_mix=m_post_norm_mix, pre_norm_mlp=m_pre_norm_mlp, w_up=m_w_up[0],
             w_down=m_w_down[0], post_norm_mlp=m_post_norm_mlp)
    v = dict(pre_norm_mix=v_pre_norm_mix, w_in=v_w_in[0], q_a_norm=v_q_a_norm, w_q_b=v_w_q_b[0],
             kv_a_norm=v_kv_a_norm, w_kv_b=v_w_kv_b[0], sinks=v_sinks, w_o_a=v_w_o_a[0], w_o_b=v_w_o_b[0],
             w_out=v_w_out[0], post_norm_mix=v_post_norm_mix, pre_norm_mlp=v_pre_norm_mlp, w_up=v_w_up[0],
             w_down=v_w_down[0], post_norm_mlp=v_post_norm_mlp)

    loss, grad_x, blocks, dsmall = _fwd_bwd(x[0], positions, loss_target[0], w)

    red = _all_reduce_small(_pack_small(dsmall, loss[0, 0]))
    small = [_unpack_small(o) for o in
             _adamw(_pack_small(w), [red], _pack_small(m), _pack_small(v), "adamw_small", (8, D_MODEL))]

    chip = (2 * lax.axis_index("x") + lax.axis_index("y")).astype(jnp.int32).reshape(1)
    reduced = {}
    for group, tag in ((GROUP_A, "a"), (GROUP_B, "b")):
        gp, land = blocks[group]
        part = _sum4(gp, land, chip, "rs_sum_" + tag)
        reduced[group] = [part, _swap_sibling(part, "rs_swap_" + tag)]

    big = {}
    tr = jnp.transpose
    big["w_in"] = [tr(o)[None] for o in _adamw(tr(w["w_in"]), reduced[GROUP_A], tr(m["w_in"]), tr(v["w_in"]),
                                               "adamw_w_in", (_W_IN_ROWS, 256))]
    for n in ("w_up", "w_down", "w_out"):
        big[n] = [o[None] for o in _adamw(w[n], reduced[GROUP_B], m[n], v[n], "adamw_" + n, (128, D_MODEL),
                                          _row_offset(GROUP_B, n))]
    for group, names in ((GROUP_A, ("w_q_b", "w_kv_b")), (GROUP_B, ("w_o_a", "w_o_b"))):
        for n in names:
            off = _row_offset(group, n)
            g_parts = [p[off:off + PACK_ROWS[n]].reshape(SHARD_SHAPES[n]) for p in reduced[group]]
            big[n] = [o[None] for o in _adamw(w[n], g_parts, m[n], v[n], "adamw_" + n, SHARD_SHAPES[n])]

    outs = [big[n][k] if n in big else small[k][n] for k in range(4) for n in WEIGHTS]
    return (red[5, 0], grad_x[None], *outs)
```

```python
import jax
import jax.numpy as jnp
from jax import lax
from jax.experimental import pallas as pl
from jax.experimental.pallas import tpu as pltpu

F32 = jnp.float32
BF16 = jnp.bfloat16
MESH = pl.DeviceIdType.MESH

D_MODEL = 1024
N_HEADS_A = 8
N_KV_A = 2
HEAD_DIM_A = 64
WINDOW = 128
BLOCK = 128
N_HEADS_B = 8
QK_NOPE = 64
QK_ROPE = 32
V_DIM_B = 64
Q_LORA = 256
KV_LORA = 128
ROPE_THETA = 10000.0
D_FF = 4 * D_MODEL
EPS = 1e-6
WIDTH_A = N_HEADS_A * HEAD_DIM_A
Q_HEAD_B = QK_NOPE + QK_ROPE
D_IN_PAD = 3328
HEAD_PAD = 128
MLA_W = N_HEADS_B * HEAD_PAD

ADAM_LR = 0.001
ADAM_B1 = 0.9
ADAM_B2 = 0.999
ADAM_EPS = 1e-08
ADAM_WD = 0.01
ADAM_STEP = 10

NEG = -1e30
N_CHIPS = 4
LANES = 128
VMEM_LIMIT = 56 * 1024 * 1024

SHARD_SHAPES = {"w_in": (1024, 808), "w_q_b": (256, 192), "w_kv_b": (128, 256), "w_o_a": (512, 256),
                "w_o_b": (512, 256), "w_out": (256, 1024), "w_up": (1024, 1024), "w_down": (1024, 1024)}
PACK_ROWS = {n: (s[0] * s[1]) // D_MODEL for n, s in SHARD_SHAPES.items()}
GROUP_A = ("w_in", "w_q_b", "w_kv_b")
GROUP_B = ("w_up", "w_down", "w_out", "w_o_a", "w_o_b")
WEIGHTS = ("pre_norm_mix", "w_in", "q_a_norm", "w_q_b", "kv_a_norm", "w_kv_b", "sinks", "w_o_a", "w_o_b", "w_out",
           "post_norm_mix", "pre_norm_mlp", "w_up", "w_down", "post_norm_mlp")


def _params(sem=None):
    return pltpu.CompilerParams(dimension_semantics=sem, vmem_limit_bytes=VMEM_LIMIT)


def _dot(a, b):
    return jnp.dot(a, b, preferred_element_type=F32)


def _dot_nt(a, b):
    return lax.dot_general(a, b, (((1,), (1,)), ((), ())), preferred_element_type=F32)


def _dot_tn(a, b):
    return lax.dot_general(a, b, (((0,), (0,)), ((), ())), preferred_element_type=F32)


def _rms(v):
    return lax.rsqrt(jnp.mean(v * v, axis=-1, keepdims=True) + EPS)


def _norm_bwd(dout, n, r, g):
    dn = dout * g
    dx = r * (dn - n * jnp.mean(dn * n, axis=-1, keepdims=True))
    return dx, jnp.sum(dout * n, axis=0, keepdims=True)


def _full(shape):
    return pl.BlockSpec(shape, lambda *_: (0,) * len(shape))


def _row_offset(group, name):
    return sum(PACK_ROWS[n] for n in group[:group.index(name)])


def _wb_spec(name):
    rows = PACK_ROWS[name]
    return pl.BlockSpec((N_CHIPS, rows, D_MODEL), lambda *_: (0, _row_offset(GROUP_B, name) // rows, 0))


def _proj_fwd(x, g1, w_in_t):
    t = x.shape[0]
    tm = 256

    def body(x_ref, g_ref, w_ref, h_ref, p_ref):
        xv = x_ref[...]
        h = ((xv * _rms(xv)) * g_ref[...]).astype(BF16)
        h_ref[...] = h
        p_ref[...] = _dot_nt(h, w_ref[...])

    return pl.pallas_call(
        body, name="proj_fwd", grid=(t // tm,),
        in_specs=[pl.BlockSpec((tm, D_MODEL), lambda i: (i, 0)), _full((1, D_MODEL)), _full((D_IN_PAD, D_MODEL))],
        out_specs=[pl.BlockSpec((tm, D_MODEL), lambda i: (i, 0)), pl.BlockSpec((tm, D_IN_PAD), lambda i: (i, 0))],
        out_shape=[jax.ShapeDtypeStruct((t, D_MODEL), BF16), jax.ShapeDtypeStruct((t, D_IN_PAD), F32)],
        compiler_params=_params(("parallel",)),
    )(x, g1, w_in_t)


_QA_BLK = 2048 // WIDTH_A
_KA_BLK = 2560 // LANES
_VA_BLK = 2688 // LANES
_CQ_BLK = 2816 // Q_LORA
_CKV_BLK = 3072 // LANES
_KR_BLK = 3200 // LANES


_GROUP_A = N_HEADS_A // N_KV_A
_SWA_SCALE = HEAD_DIM_A ** -0.5


def _head_cols(v, h):
    return v[:, HEAD_DIM_A * h:HEAD_DIM_A * (h + 1)]


def _head_rows(v, h):
    return v[HEAD_DIM_A * h:HEAD_DIM_A * (h + 1), :]


def _swa_band(n, kp_ref, kc_ref, vp_ref, vc_ref, pq_ref, pp_ref, pc_ref):
    kb = jnp.concatenate([kp_ref[...], kc_ref[...]], axis=0)
    vb = jnp.concatenate([vp_ref[...], vc_ref[...]], axis=0)
    posk = jnp.concatenate([pp_ref[...], pc_ref[...]], axis=0)
    dist = jnp.abs(posk - pq_ref[...])
    ki = lax.broadcasted_iota(jnp.int32, (2 * BLOCK, BLOCK), 0)
    qi = lax.broadcasted_iota(jnp.int32, (2 * BLOCK, BLOCK), 1)
    valid = (ki > qi) & (ki <= qi + WINDOW) & ((n > 0) | (ki >= BLOCK))
    return kb, vb, dist, valid


def _swa_scores_t(st_g, j, h, dist, valid):
    slope = 2.0 ** (-8.0 * (h + 1) / N_HEADS_A)
    st = st_g[:, BLOCK * j:BLOCK * (j + 1)] * _SWA_SCALE - slope * dist
    return jnp.where(valid, st, NEG)


def _group_t(xt, kh):
    return jnp.concatenate([_head_rows(xt, _GROUP_A * kh + j) for j in range(_GROUP_A)], axis=1).astype(BF16)


def _swa_fwd(proj, posc, posr, sinks):
    t = proj.shape[0]
    nb = t // BLOCK

    def body(q_ref, kc_ref, kp_ref, vc_ref, vp_ref, pq_ref, pc_ref, pp_ref, sink_ref, o_ref, l_ref):
        n = pl.program_id(0)
        kb, vb, dist, valid = _swa_band(n, kp_ref, kc_ref, vp_ref, vc_ref, pq_ref, pp_ref, pc_ref)
        q_t, vb_t = q_ref[...].T, vb.T
        out_t, lse = [], []
        for kh in range(N_KV_A):
            st_g = _dot(_head_cols(kb, kh).astype(BF16), _group_t(q_t, kh))
            ps = []
            for j in range(_GROUP_A):
                h = _GROUP_A * kh + j
                st = _swa_scores_t(st_g, j, h, dist, valid)
                sink = sink_ref[0:1, h:h + 1]
                m = jnp.maximum(jnp.max(st, axis=0, keepdims=True), sink)
                e = jnp.exp(st - m)
                den = jnp.sum(e, axis=0, keepdims=True) + jnp.exp(sink - m)
                ps.append((e / den).astype(BF16))
                lse.append(m + jnp.log(den))
            o_g = _dot(_head_rows(vb_t, kh).astype(BF16), jnp.concatenate(ps, axis=1))
            out_t.extend(o_g[:, BLOCK * j:BLOCK * (j + 1)] for j in range(_GROUP_A))
        o_ref[...] = jnp.concatenate(out_t, axis=0).T
        l_ref[...] = jnp.concatenate(lse, axis=0)

    cur = lambda n: (n, 0)
    prev = lambda n: jnp.maximum(n - 1, 0)
    return pl.pallas_call(
        body, name="swa_fwd", grid=(nb,),
        in_specs=[pl.BlockSpec((BLOCK, WIDTH_A), lambda n: (n, _QA_BLK)),
                  pl.BlockSpec((BLOCK, LANES), lambda n: (n, _KA_BLK)),
                  pl.BlockSpec((BLOCK, LANES), lambda n: (prev(n), _KA_BLK)),
                  pl.BlockSpec((BLOCK, LANES), lambda n: (n, _VA_BLK)),
                  pl.BlockSpec((BLOCK, LANES), lambda n: (prev(n), _VA_BLK)),
                  pl.BlockSpec((1, BLOCK), lambda n: (0, n)),
                  pl.BlockSpec((BLOCK, 1), cur),
                  pl.BlockSpec((BLOCK, 1), lambda n: (prev(n), 0)),
                  _full((1, N_HEADS_A))],
        out_specs=[pl.BlockSpec((BLOCK, WIDTH_A), cur), pl.BlockSpec((N_HEADS_A, BLOCK), lambda n: (0, n))],
        out_shape=[jax.ShapeDtypeStruct((t, WIDTH_A), F32), jax.ShapeDtypeStruct((N_HEADS_A, t), F32)],
        compiler_params=_params(("parallel",)),
    )(proj, proj, proj, proj, proj, posr, posc, posc, sinks)


def _rope_coeffs(pos, freq):
    ang = pos * freq
    cosv, sinv = jnp.cos(ang), jnp.sin(ang)
    lane = lax.broadcasted_iota(jnp.int32, ang.shape, 1)
    lo = (lane >= QK_NOPE) & (lane < QK_NOPE + QK_ROPE // 2)
    hi = (lane >= QK_NOPE + QK_ROPE // 2) & (lane < QK_NOPE + QK_ROPE)
    c = jnp.where(lane < QK_NOPE, 1.0, jnp.where(lo | hi, cosv, 0.0))
    s = jnp.where(lo, -sinv, jnp.where(hi, sinv, 0.0))
    return c, s, lo, hi


def _rope(xh, c, s, lo):
    up = pltpu.roll(xh, LANES - QK_ROPE // 2, axis=1)
    dn = pltpu.roll(xh, QK_ROPE // 2, axis=1)
    return xh * c + jnp.where(lo, up, dn) * s


def _unrope(dh, c, s, lo, hi):
    g = dh * s
    up = pltpu.roll(g, LANES - QK_ROPE // 2, axis=1)
    dn = pltpu.roll(g, QK_ROPE // 2, axis=1)
    return dh * c + jnp.where(hi, dn, jnp.where(lo, up, 0.0))


_TQ = 512
_MLA_SCALE = Q_HEAD_B ** -0.5


def _mla_prep_fwd(proj, posc, freq, qan, kvan, wq, wk, wv):
    t = proj.shape[0]
    tm = _TQ
    nb = t // tm

    def body(cq_ref, ckv_ref, kr_ref, pos_ref, f_ref, qan_ref, kvan_ref, wq_ref, wk_ref, wv_ref,
             q_ref, k_ref, qt_ref, kt_ref, v_ref, vt_ref):
        cq = cq_ref[...]
        cqn = ((cq * _rms(cq)) * qan_ref[...]).astype(BF16)
        ckv = ckv_ref[...]
        ckvn = ((ckv * _rms(ckv)) * kvan_ref[...]).astype(BF16)
        qb = _dot(cqn, wq_ref[...])
        kb = _dot(ckvn, wk_ref[...])
        vb = _dot(ckvn, wv_ref[...])
        vbt = vb.T
        c, s, lo, _ = _rope_coeffs(pos_ref[...], f_ref[...])
        kr = _rope(kr_ref[...], c, s, lo)
        for h in range(N_HEADS_B):
            sl = slice(HEAD_PAD * h, HEAD_PAD * (h + 1))
            q_h = _rope(qb[:, sl], c, s, lo)
            k_h = kb[:, sl] + kr
            q_ref[:, sl] = q_h.astype(BF16)
            k_ref[:, sl] = k_h.astype(BF16)
            qt_ref[h, 0] = q_h.T.astype(BF16)
            kt_ref[h, 0] = k_h.T.astype(BF16)
            v_ref[h] = vb[:, V_DIM_B * h:V_DIM_B * (h + 1)].astype(BF16)
            vt_ref[h, 0] = vbt[V_DIM_B * h:V_DIM_B * (h + 1), :].astype(BF16)

    row = lambda i: (i, 0)
    blk4 = lambda d: pl.BlockSpec((N_HEADS_B, 1, d, tm), lambda i: (0, i, 0, 0))
    return pl.pallas_call(
        body, name="mla_prep_fwd", grid=(nb,),
        in_specs=[pl.BlockSpec((tm, Q_LORA), lambda i: (i, _CQ_BLK)),
                  pl.BlockSpec((tm, LANES), lambda i: (i, _CKV_BLK)),
                  pl.BlockSpec((tm, LANES), lambda i: (i, _KR_BLK)),
                  pl.BlockSpec((tm, 1), row), _full((1, LANES)), _full((1, Q_LORA)), _full((1, KV_LORA)),
                  _full((Q_LORA, MLA_W)), _full((KV_LORA, MLA_W)), _full((KV_LORA, N_HEADS_B * V_DIM_B))],
        out_specs=[pl.BlockSpec((tm, MLA_W), row), pl.BlockSpec((tm, MLA_W), row), blk4(HEAD_PAD), blk4(HEAD_PAD),
                   pl.BlockSpec((N_HEADS_B, tm, V_DIM_B), lambda i: (0, i, 0)), blk4(V_DIM_B)],
        out_shape=[jax.ShapeDtypeStruct((t, MLA_W), BF16), jax.ShapeDtypeStruct((t, MLA_W), BF16),
                   jax.ShapeDtypeStruct((N_HEADS_B, nb, HEAD_PAD, tm), BF16),
                   jax.ShapeDtypeStruct((N_HEADS_B, nb, HEAD_PAD, tm), BF16),
                   jax.ShapeDtypeStruct((N_HEADS_B, t, V_DIM_B), BF16),
                   jax.ShapeDtypeStruct((N_HEADS_B, nb, V_DIM_B, tm), BF16)],
        compiler_params=_params(("parallel",)),
    )(proj, proj, proj, posc, freq, qan, kvan, wq, wk, wv)


_LOG2E = 1.4426950408889634
_MLA_SCALE2 = _MLA_SCALE * _LOG2E


def _scores_t(k, qt, diagonal):
    st = _dot(k, qt) * _MLA_SCALE2
    if diagonal:
        key = lax.broadcasted_iota(jnp.int32, st.shape, 0)
        qry = lax.broadcasted_iota(jnp.int32, st.shape, 1)
        st = jnp.where(key <= qry, st, NEG)
    return st


def _mla_fwd(k, qt, vt, w_src):
    t = k.shape[0]
    nb = t // _TQ

    def body(k_ref, qt_ref, vt_ref, w_ref, o_ref, l_ref, wg_ref, raw_a, raw_b, send_sems, recv_sems, local_sem):
        qi = pl.program_id(1)
        first = (pl.program_id(0) == 0) & (qi == 0)
        last = (pl.program_id(0) == N_HEADS_B - 1) & (qi == nb - 1)

        @pl.when(first)
        def _():
            _gather_start(w_ref, wg_ref, send_sems, recv_sems, local_sem)

        q_t = qt_ref[0, 0]

        def product(kj):
            return _dot(k_ref[pl.ds(pl.multiple_of(kj * _TQ, _TQ), _TQ), :], q_t)

        def update(stats, raw_ref, kj, diagonal=False):
            m, l, acc = stats
            st = raw_ref[...] * _MLA_SCALE2
            if diagonal:
                key = lax.broadcasted_iota(jnp.int32, st.shape, 0)
                qry = lax.broadcasted_iota(jnp.int32, st.shape, 1)
                st = jnp.where(key <= qry, st, NEG)
            m_new = jnp.maximum(m, jnp.max(st, axis=0, keepdims=True))
            alpha = jnp.exp2(m - m_new)
            p = jnp.exp2(st - m_new)
            l = alpha * l + jnp.sum(p, axis=0, keepdims=True)
            acc = alpha * acc + _dot(vt_ref[0, kj], p.astype(BF16))
            return m_new, l, acc

        def trip(i, stats):
            raw_b[...] = product(2 * i + 1)
            stats = update(stats, raw_a, 2 * i)
            raw_a[...] = product(2 * i + 2)
            return update(stats, raw_b, 2 * i + 1)

        def tail_even(stats):
            return update(stats, raw_a, qi, True)

        def tail_odd(stats):
            raw_b[...] = product(qi)
            return update(update(stats, raw_a, qi - 1), raw_b, qi, True)

        init = (jnp.full((1, _TQ), NEG, F32), jnp.zeros((1, _TQ), F32), jnp.zeros((V_DIM_B, _TQ), F32))
        raw_a[...] = product(0)
        stats = lax.fori_loop(0, qi // 2, trip, init)
        m, l, acc = lax.cond(qi % 2 == 0, tail_even, tail_odd, stats)
        o_ref[0, 0] = acc / l
        l_ref[0, 0] = m + jnp.log(l) * _LOG2E

        @pl.when(last)
        def _():
            _gather_wait(w_ref, wg_ref, send_sems, recv_sems, local_sem)

    return pl.pallas_call(
        body, name="mla_fwd", grid=(N_HEADS_B, nb),
        in_specs=[pl.BlockSpec((t, HEAD_PAD), lambda h, qi: (0, h)),
                  pl.BlockSpec((1, 1, HEAD_PAD, _TQ), lambda h, qi: (h, qi, 0, 0)),
                  pl.BlockSpec((1, nb, V_DIM_B, _TQ), lambda h, qi: (h, 0, 0, 0)), _HBM],
        out_specs=[pl.BlockSpec((1, 1, V_DIM_B, _TQ), lambda h, qi: (h, qi, 0, 0)),
                   pl.BlockSpec((1, 1, 1, _TQ), lambda h, qi: (h, qi, 0, 0)), _HBM],
        out_shape=[jax.ShapeDtypeStruct((N_HEADS_B, nb, V_DIM_B, _TQ), F32),
                   jax.ShapeDtypeStruct((N_HEADS_B, nb, 1, _TQ), F32),
                   jax.ShapeDtypeStruct((N_CHIPS,) + w_src.shape, w_src.dtype)],
        scratch_shapes=[pltpu.VMEM((_TQ, _TQ), F32), pltpu.VMEM((_TQ, _TQ), F32),
                        pltpu.SemaphoreType.DMA((3,)), pltpu.SemaphoreType.DMA((3,)), pltpu.SemaphoreType.DMA(())],
        compiler_params=_params(("arbitrary", "arbitrary")),
    )(k, qt, vt, w_src)


def _ot_spec(tm, d):
    per = _TQ // tm
    return pl.BlockSpec((N_HEADS_B, 1, d, tm), lambda i: (0, i // per, 0, i % per))


def _mix_out_fwd(out_a, out_bt, proj, x, w_oa, w_ob, wb, g2, g3):
    t = x.shape[0]
    tm = 256

    def body(oa_ref, obt_ref, ga_ref, gb_ref, x_ref, woa_ref, wob_ref, wout_ref, g2_ref, g3_ref,
             mg_ref, y_ref, x1_ref, h2_ref):
        oa = _dot(oa_ref[...].astype(BF16), woa_ref[...])
        obt = obt_ref[...].reshape(N_HEADS_B * V_DIM_B, tm).astype(BF16)
        ob = _dot_tn(obt, wob_ref[...])
        merged = (jax.nn.sigmoid(ga_ref[...]) * oa + jax.nn.sigmoid(gb_ref[...]) * ob).astype(BF16)
        mg_ref[...] = merged
        y = _dot(merged, wout_ref[...].reshape(D_MODEL, D_MODEL))
        y_ref[...] = y
        x1 = x_ref[...] + (y * _rms(y)) * g2_ref[...]
        x1_ref[...] = x1
        h2_ref[...] = ((x1 * _rms(x1)) * g3_ref[...]).astype(BF16)

    row = lambda i: (i, 0)
    blk = pl.BlockSpec((tm, D_MODEL), row)
    return pl.pallas_call(
        body, name="mix_out_fwd", grid=(t // tm,),
        in_specs=[pl.BlockSpec((tm, WIDTH_A), row), _ot_spec(tm, V_DIM_B), pl.BlockSpec((tm, D_MODEL), lambda i: (i, 0)),
                  pl.BlockSpec((tm, D_MODEL), lambda i: (i, 1)), blk,
                  _full((WIDTH_A, D_MODEL)), _full((N_HEADS_B * V_DIM_B, D_MODEL)), _wb_spec("w_out"),
                  _full((1, D_MODEL)), _full((1, D_MODEL))],
        out_specs=[blk, blk, blk, blk],
        out_shape=[jax.ShapeDtypeStruct((t, D_MODEL), BF16), jax.ShapeDtypeStruct((t, D_MODEL), F32),
                   jax.ShapeDtypeStruct((t, D_MODEL), F32), jax.ShapeDtypeStruct((t, D_MODEL), BF16)],
        compiler_params=_params(("parallel",)),
    )(out_a, out_bt, proj, proj, x, w_oa, w_ob, wb, g2, g3)


_TM_MLP = 256


def _up_fwd(h2, wb):
    t = h2.shape[0]
    tm = _TM_MLP

    def body(h_ref, w_ref, a_ref):
        hv = h_ref[...]
        for j in range(N_CHIPS):
            u = _dot(hv, w_ref[j])
            a_ref[:, D_MODEL * j:D_MODEL * (j + 1)] = jnp.square(jnp.maximum(u, 0.0)).astype(BF16)

    return pl.pallas_call(
        body, name="up_fwd", grid=(t // tm,),
        in_specs=[pl.BlockSpec((tm, D_MODEL), lambda i: (i, 0)), _wb_spec("w_up")],
        out_specs=pl.BlockSpec((tm, D_FF), lambda i: (i, 0)),
        out_shape=jax.ShapeDtypeStruct((t, D_FF), BF16),
        compiler_params=_params(("parallel",)),
    )(h2, wb)


def _down_fwd_loss(a, wb, x1, target, g4):
    t = a.shape[0]
    tm = _TM_MLP

    def body(a_ref, w_ref, x1_ref, tg_ref, g_ref, dx2_ref, dyd_ref, dg_ref, loss_ref):
        @pl.when(pl.program_id(0) == 0)
        def _():
            dg_ref[...] = jnp.zeros(dg_ref.shape, F32)
            loss_ref[...] = jnp.zeros(loss_ref.shape, F32)

        yd = _dot(a_ref[...], w_ref[...].reshape(D_FF, D_MODEL))
        r = _rms(yd)
        n = yd * r
        diff = (x1_ref[...] + n * g_ref[...]) - tg_ref[...]
        loss_ref[...] += 0.5 * jnp.sum(jnp.mean(diff * diff, axis=-1, keepdims=True), axis=0, keepdims=True)
        dx2 = diff * (1.0 / D_MODEL)
        dx2_ref[...] = dx2
        dyd, dg = _norm_bwd(dx2, n, r, g_ref[...])
        dyd_ref[...] = dyd.astype(BF16)
        dg_ref[...] += dg

    row = lambda i: (i, 0)
    blk = pl.BlockSpec((tm, D_MODEL), row)
    return pl.pallas_call(
        body, name="down_fwd_loss", grid=(t // tm,),
        in_specs=[pl.BlockSpec((tm, D_FF), row), _wb_spec("w_down"), blk, blk, _full((1, D_MODEL))],
        out_specs=[blk, blk, _full((1, D_MODEL)), _full((1, LANES))],
        out_shape=[jax.ShapeDtypeStruct((t, D_MODEL), F32), jax.ShapeDtypeStruct((t, D_MODEL), BF16),
                   jax.ShapeDtypeStruct((1, D_MODEL), F32), jax.ShapeDtypeStruct((1, LANES), F32)],
        compiler_params=_params(("arbitrary",)),
    )(a, wb, x1, target, g4)


def _matmul_tn(a, b, name, tm, tn, tk=1024, shard_major=False):
    t, m = a.shape
    n = b.shape[1]
    tk = min(tk, t)
    nk = t // tk

    def body(a_ref, b_ref, o_ref):
        @pl.when(pl.program_id(2) == 0)
        def _():
            o_ref[...] = jnp.zeros(o_ref.shape, F32)

        acc = _dot_tn(a_ref[...].astype(BF16), b_ref[...].astype(BF16))
        o_ref[...] += acc[None] if shard_major else acc

    if shard_major:
        out_spec = pl.BlockSpec((1, tm, tn), lambda i, j, k: (j, i, 0))
        out_shape = jax.ShapeDtypeStruct((n // tn, m, tn), F32)
    else:
        out_spec = pl.BlockSpec((tm, tn), lambda i, j, k: (i, j))
        out_shape = jax.ShapeDtypeStruct((m, n), F32)
    return pl.pallas_call(
        body, name=name, grid=(m // tm, n // tn, nk),
        in_specs=[pl.BlockSpec((tk, tm), lambda i, j, k: (k, i)), pl.BlockSpec((tk, tn), lambda i, j, k: (k, j))],
        out_specs=out_spec, out_shape=out_shape,
        compiler_params=_params(("parallel", "parallel", "arbitrary")),
    )(a, b)


def _down_bwd(dyd, wb, a):
    t = dyd.shape[0]
    tm = _TM_MLP

    def body(d_ref, w_ref, a_ref, du_ref):
        da = _dot_nt(d_ref[...], w_ref[...].reshape(D_FF, D_MODEL))
        du_ref[...] = (da * (2.0 * jnp.sqrt(a_ref[...].astype(F32)))).astype(BF16)

    row = lambda i: (i, 0)
    return pl.pallas_call(
        body, name="down_bwd", grid=(t // tm,),
        in_specs=[pl.BlockSpec((tm, D_MODEL), row), _wb_spec("w_down"), pl.BlockSpec((tm, D_FF), row)],
        out_specs=pl.BlockSpec((tm, D_FF), row),
        out_shape=jax.ShapeDtypeStruct((t, D_FF), BF16),
        compiler_params=_params(("parallel",)),
    )(dyd, wb, a)


def _up_bwd(du, wb, x1, dx2, y, g3, g2):
    t = du.shape[0]
    tm = _TM_MLP

    def body(du_ref, w_ref, x1_ref, dx2_ref, y_ref, g3_ref, g2_ref, dx1_ref, dy_ref, dg3_ref, dg2_ref):
        @pl.when(pl.program_id(0) == 0)
        def _():
            dg3_ref[...] = jnp.zeros(dg3_ref.shape, F32)
            dg2_ref[...] = jnp.zeros(dg2_ref.shape, F32)

        dh2 = _dot_nt(du_ref[:, 0:D_MODEL], w_ref[0])
        for j in range(1, N_CHIPS):
            dh2 = dh2 + _dot_nt(du_ref[:, D_MODEL * j:D_MODEL * (j + 1)], w_ref[j])
        x1 = x1_ref[...]
        r3 = _rms(x1)
        d3, dg3 = _norm_bwd(dh2, x1 * r3, r3, g3_ref[...])
        dx1 = dx2_ref[...] + d3
        dx1_ref[...] = dx1
        dg3_ref[...] += dg3
        y = y_ref[...]
        r2 = _rms(y)
        dy, dg2 = _norm_bwd(dx1, y * r2, r2, g2_ref[...])
        dy_ref[...] = dy.astype(BF16)
        dg2_ref[...] += dg2

    row = lambda i: (i, 0)
    blk = pl.BlockSpec((tm, D_MODEL), row)
    return pl.pallas_call(
        body, name="up_bwd", grid=(t // tm,),
        in_specs=[pl.BlockSpec((tm, D_FF), row), _wb_spec("w_up"),
                  blk, blk, blk, _full((1, D_MODEL)), _full((1, D_MODEL))],
        out_specs=[blk, blk, _full((1, D_MODEL)), _full((1, D_MODEL))],
        out_shape=[jax.ShapeDtypeStruct((t, D_MODEL), F32), jax.ShapeDtypeStruct((t, D_MODEL), BF16),
                   jax.ShapeDtypeStruct((1, D_MODEL), F32), jax.ShapeDtypeStruct((1, D_MODEL), F32)],
        compiler_params=_params(("arbitrary",)),
    )(du, wb, x1, dx2, y, g3, g2)


def _mix_out_bwd(dy, out_a, out_bt, proj, w_oa, w_ob, wb):
    t = dy.shape[0]
    tm = 256
    nb = t // _TQ

    def body(dy_ref, oa_ref, obt_ref, ga_ref, gb_ref, woa_ref, wob_ref, wout_ref,
             doa_ref, dob_ref, dga_ref, dgb_ref, da_ref, db_ref, dbt_ref, dela_ref, delb_ref):
        dm = _dot_nt(dy_ref[...], wout_ref[...].reshape(D_MODEL, D_MODEL))
        out_a_v = oa_ref[...]
        out_bt_v = obt_ref[...].reshape(N_HEADS_B * V_DIM_B, tm)
        oa = _dot(out_a_v.astype(BF16), woa_ref[...])
        ob = _dot_tn(out_bt_v.astype(BF16), wob_ref[...])
        sa, sb = jax.nn.sigmoid(ga_ref[...]), jax.nn.sigmoid(gb_ref[...])
        doa = (dm * sa).astype(BF16)
        dob = (dm * sb).astype(BF16)
        doa_ref[...] = doa
        dob_ref[...] = dob
        dga_ref[...] = (dm * oa * (sa * (1.0 - sa))).astype(BF16)
        dgb_ref[...] = (dm * ob * (sb * (1.0 - sb))).astype(BF16)
        d_out_a = _dot_nt(doa, woa_ref[...])
        da_ref[...] = d_out_a
        prod_at = (d_out_a * out_a_v).T
        dela_ref[...] = jnp.concatenate(
            [jnp.sum(_head_rows(prod_at, h), axis=0, keepdims=True) for h in range(N_HEADS_A)], axis=0)
        d_out_b = _dot_nt(dob, wob_ref[...])
        d_out_bt = _dot_nt(wob_ref[...], dob)
        prod_bt = d_out_bt * out_bt_v
        for h in range(N_HEADS_B):
            db_ref[h] = d_out_b[:, V_DIM_B * h:V_DIM_B * (h + 1)].astype(BF16)
            dbt_ref[h, 0] = d_out_bt[V_DIM_B * h:V_DIM_B * (h + 1), :].astype(BF16)
            delb_ref[h, 0] = jnp.sum(prod_bt[V_DIM_B * h:V_DIM_B * (h + 1), :], axis=0, keepdims=True)

    row = lambda i: (i, 0)
    blk = pl.BlockSpec((tm, D_MODEL), row)
    return pl.pallas_call(
        body, name="mix_out_bwd", grid=(t // tm,),
        in_specs=[blk, pl.BlockSpec((tm, WIDTH_A), row), _ot_spec(tm, V_DIM_B), pl.BlockSpec((tm, D_MODEL), lambda i: (i, 0)),
                  pl.BlockSpec((tm, D_MODEL), lambda i: (i, 1)),
                  _full((WIDTH_A, D_MODEL)), _full((N_HEADS_B * V_DIM_B, D_MODEL)), _wb_spec("w_out")],
        out_specs=[blk, blk, blk, blk, pl.BlockSpec((tm, WIDTH_A), row),
                   pl.BlockSpec((N_HEADS_B, tm, V_DIM_B), lambda i: (0, i, 0)), _ot_spec(tm, V_DIM_B),
                   pl.BlockSpec((N_HEADS_A, tm), lambda i: (0, i)), _ot_spec(tm, 1)],
        out_shape=[jax.ShapeDtypeStruct((t, D_MODEL), BF16)] * 4
        + [jax.ShapeDtypeStruct((t, WIDTH_A), F32), jax.ShapeDtypeStruct((N_HEADS_B, t, V_DIM_B), BF16),
           jax.ShapeDtypeStruct((N_HEADS_B, nb, V_DIM_B, _TQ), BF16),
           jax.ShapeDtypeStruct((N_HEADS_A, t), F32), jax.ShapeDtypeStruct((N_HEADS_B, nb, 1, _TQ), F32)],
        compiler_params=_params(("parallel",)),
    )(dy, out_a, out_bt, proj, proj, w_oa, w_ob, wb)


def _dw_ob(out_bt, dob):
    t = dob.shape[0]
    nb = t // _TQ

    def body(obt_ref, dob_ref, o_ref):
        @pl.when(pl.program_id(0) == 0)
        def _():
            o_ref[...] = jnp.zeros(o_ref.shape, F32)

        obt = obt_ref[...].reshape(N_HEADS_B * V_DIM_B, _TQ).astype(BF16)
        o_ref[...] += _dot(obt, dob_ref[...])

    return pl.pallas_call(
        body, name="dw_o_b", grid=(nb,),
        in_specs=[pl.BlockSpec((N_HEADS_B, 1, V_DIM_B, _TQ), lambda i: (0, i, 0, 0)),
                  pl.BlockSpec((_TQ, D_MODEL), lambda i: (i, 0))],
        out_specs=_full((N_HEADS_B * V_DIM_B, D_MODEL)),
        out_shape=jax.ShapeDtypeStruct((N_HEADS_B * V_DIM_B, D_MODEL), F32),
        compiler_params=_params(("arbitrary",)),
    )(out_bt, dob)


def _mla_bwd(q, k, qt, kt, v, d_out, d_out_t, lse, delta, gp):
    t = q.shape[0]
    nb = t // _TQ

    def body(k_ref, kt_ref, v_ref, q_ref, qt_ref, do_ref, dot_ref, l_ref, d_ref, gp_ref,
             dqt_ref, dk_ref, dv_ref, land_ref, send_sems, recv_sems):
        kj = pl.program_id(1)

        @pl.when((pl.program_id(0) == 0) & (kj == 0))
        def _():
            _scatter_start(gp_ref, land_ref, send_sems, recv_sems)

        @pl.when(kj == 0)
        def _():
            dqt_ref[...] = jnp.zeros(dqt_ref.shape, F32)

        kv, k_t, vv = k_ref[...], kt_ref[0, 0], v_ref[0]

        def products(qi, diagonal=False):
            return _scores_t(kv, qt_ref[0, qi], diagonal), _dot(vv, dot_ref[0, qi])

        def update(carry, prods, qi):
            dk, dv = carry
            st, dpt = prods
            rows = pl.ds(pl.multiple_of(qi * _TQ, _TQ), _TQ)
            pt = jnp.exp2(st - l_ref[0, qi])
            dv = dv + _dot(pt.astype(BF16), do_ref[0, rows, :])
            dst = (pt * (dpt - d_ref[0, qi]) * _MLA_SCALE).astype(BF16)
            dk = dk + _dot(dst, q_ref[rows, :])
            dqt_ref[0, qi] += _dot(k_t, dst)
            return dk, dv

        def pair(i, carry):
            qa = kj + 1 + 2 * i
            pa, pb = products(qa), products(qa + 1)
            return update(update(carry, pa, qa), pb, qa + 1)

        init = (jnp.zeros((_TQ, HEAD_PAD), F32), jnp.zeros((_TQ, V_DIM_B), F32))
        carry = update(init, products(kj, True), kj)
        pairs = (nb - 1 - kj) // 2
        carry = lax.fori_loop(0, pairs, pair, carry)
        dk, dv = lax.fori_loop(kj + 1 + 2 * pairs, nb, lambda qi, cr: update(cr, products(qi), qi), carry)
        dk_ref[...] = dk
        dv_ref[0] = dv

        @pl.when((pl.program_id(0) == N_HEADS_B - 1) & (kj == nb - 1))
        def _():
            _scatter_wait(gp_ref, land_ref, send_sems, recv_sems)

    head4 = lambda d: pl.BlockSpec((1, nb, d, _TQ), lambda h, kj: (h, 0, 0, 0))
    return pl.pallas_call(
        body, name="mla_bwd", grid=(N_HEADS_B, nb),
        in_specs=[pl.BlockSpec((_TQ, HEAD_PAD), lambda h, kj: (kj, h)),
                  pl.BlockSpec((1, 1, HEAD_PAD, _TQ), lambda h, kj: (h, kj, 0, 0)),
                  pl.BlockSpec((1, _TQ, V_DIM_B), lambda h, kj: (h, kj, 0)),
                  pl.BlockSpec((t, HEAD_PAD), lambda h, kj: (0, h)), head4(HEAD_PAD),
                  pl.BlockSpec((1, t, V_DIM_B), lambda h, kj: (h, 0, 0)), head4(V_DIM_B), head4(1), head4(1), _HBM],
        out_specs=[head4(HEAD_PAD), pl.BlockSpec((_TQ, HEAD_PAD), lambda h, kj: (kj, h)),
                   pl.BlockSpec((1, _TQ, V_DIM_B), lambda h, kj: (h, kj, 0)), _HBM],
        out_shape=[jax.ShapeDtypeStruct((N_HEADS_B, nb, HEAD_PAD, _TQ), F32), jax.ShapeDtypeStruct((t, MLA_W), F32),
                   jax.ShapeDtypeStruct((N_HEADS_B, t, V_DIM_B), F32),
                   jax.ShapeDtypeStruct((3,) + gp.shape[1:], gp.dtype)],
        scratch_shapes=[pltpu.SemaphoreType.DMA((3,)), pltpu.SemaphoreType.DMA((3,))],
        compiler_params=_params(("arbitrary", "arbitrary")),
    )(k, kt, v, q, qt, d_out, d_out_t, lse, delta, gp)


def _mla_prep_bwd(dqt, dk, dv, proj, posc, freq, qan, kvan, wq, wk, wv):
    t = dk.shape[0]
    tm = _TQ

    def body(dqt_ref, dk_ref, dv_ref, cq_ref, ckv_ref, pos_ref, f_ref, qan_ref, kvan_ref, wq_ref, wk_ref, wv_ref,
             dcq_ref, dckv_ref, dkr_ref, dwq_ref, dwk_ref, dwv_ref, dqan_ref, dkvan_ref):
        @pl.when(pl.program_id(0) == 0)
        def _():
            for r in (dwq_ref, dwk_ref, dwv_ref, dqan_ref, dkvan_ref):
                r[...] = jnp.zeros(r.shape, F32)

        cq = cq_ref[...]
        rq = _rms(cq)
        nq_ = cq * rq
        cqn = (nq_ * qan_ref[...]).astype(BF16)
        ckv = ckv_ref[...]
        rkv = _rms(ckv)
        nkv = ckv * rkv
        ckvn = (nkv * kvan_ref[...]).astype(BF16)
        c, s, lo, hi = _rope_coeffs(pos_ref[...], f_ref[...])
        dkv = dk_ref[...]
        dkr = jnp.zeros((tm, LANES), F32)
        dqb = []
        for h in range(N_HEADS_B):
            dqb.append(_unrope(dqt_ref[h, 0].T, c, s, lo, hi).astype(BF16))
            dkr = dkr + dkv[:, HEAD_PAD * h:HEAD_PAD * (h + 1)]
        dqb = jnp.concatenate(dqb, axis=1)
        dkr_ref[...] = jnp.where(lo | hi, _unrope(dkr, c, s, lo, hi), 0.0).astype(BF16)
        dkb = dkv.astype(BF16)
        dvb = jnp.concatenate([dv_ref[h] for h in range(N_HEADS_B)], axis=1).astype(BF16)
        dwq_ref[...] += _dot_tn(cqn, dqb)
        dwk_ref[...] += _dot_tn(ckvn, dkb)
        dwv_ref[...] += _dot_tn(ckvn, dvb)
        dcqn = _dot_nt(dqb, wq_ref[...])
        dckvn = _dot_nt(dkb, wk_ref[...]) + _dot_nt(dvb, wv_ref[...])
        dcq, dqan = _norm_bwd(dcqn, nq_, rq, qan_ref[...])
        dckv, dkvan = _norm_bwd(dckvn, nkv, rkv, kvan_ref[...])
        dcq_ref[...] = dcq.astype(BF16)
        dckv_ref[...] = dckv.astype(BF16)
        dqan_ref[...] += dqan
        dkvan_ref[...] += dkvan

    row = lambda i: (i, 0)
    vw = N_HEADS_B * V_DIM_B
    return pl.pallas_call(
        body, name="mla_prep_bwd", grid=(t // tm,),
        in_specs=[pl.BlockSpec((N_HEADS_B, 1, HEAD_PAD, tm), lambda i: (0, i, 0, 0)), pl.BlockSpec((tm, MLA_W), row),
                  pl.BlockSpec((N_HEADS_B, tm, V_DIM_B), lambda i: (0, i, 0)),
                  pl.BlockSpec((tm, Q_LORA), lambda i: (i, _CQ_BLK)),
                  pl.BlockSpec((tm, LANES), lambda i: (i, _CKV_BLK)),
                  pl.BlockSpec((tm, 1), row), _full((1, LANES)), _full((1, Q_LORA)), _full((1, KV_LORA)),
                  _full((Q_LORA, MLA_W)), _full((KV_LORA, MLA_W)), _full((KV_LORA, vw))],
        out_specs=[pl.BlockSpec((tm, Q_LORA), row), pl.BlockSpec((tm, LANES), row), pl.BlockSpec((tm, LANES), row),
                   _full((Q_LORA, MLA_W)), _full((KV_LORA, MLA_W)), _full((KV_LORA, vw)),
                   _full((1, Q_LORA)), _full((1, KV_LORA))],
        out_shape=[jax.ShapeDtypeStruct((t, Q_LORA), BF16), jax.ShapeDtypeStruct((t, LANES), BF16),
                   jax.ShapeDtypeStruct((t, LANES), BF16),
                   jax.ShapeDtypeStruct((Q_LORA, MLA_W), F32), jax.ShapeDtypeStruct((KV_LORA, MLA_W), F32),
                   jax.ShapeDtypeStruct((KV_LORA, vw), F32),
                   jax.ShapeDtypeStruct((1, Q_LORA), F32), jax.ShapeDtypeStruct((1, KV_LORA), F32)],
        compiler_params=_params(("arbitrary",)),
    )(dqt, dk, dv, proj, proj, posc, freq, qan, kvan, wq, wk, wv)


def _swa_bwd(proj, d_out, lse, delta, posc, posr, sinks):
    t = proj.shape[0]
    nb = t // BLOCK

    def body(q_ref, kc_ref, kp_ref, vc_ref, vp_ref, do_ref, l_ref, d_ref, pq_ref, pc_ref, pp_ref, sink_ref,
             dq_ref, dk_ref, dv_ref, ds_ref, dkb_s, dvb_s, dk_carry, dv_carry):
        n = pl.program_id(0)

        @pl.when(n == 0)
        def _():
            ds_ref[...] = jnp.zeros(ds_ref.shape, F32)
            dk_carry[...] = jnp.zeros(dk_carry.shape, F32)
            dv_carry[...] = jnp.zeros(dv_carry.shape, F32)

        @pl.when(n < nb)
        def _():
            kb, vb, dist, valid = _swa_band(n, kp_ref, kc_ref, vp_ref, vc_ref, pq_ref, pp_ref, pc_ref)
            qv, dov = q_ref[...], do_ref[...]
            q_t, do_t, kb_t = qv.T, dov.T, kb.T
            lane = lax.broadcasted_iota(jnp.int32, (1, LANES), 1)
            dsink = jnp.zeros((1, LANES), F32)
            dq_t = []
            for kh in range(N_KV_A):
                heads = range(_GROUP_A * kh, _GROUP_A * (kh + 1))
                st_g = _dot(_head_cols(kb, kh).astype(BF16), _group_t(q_t, kh))
                dpt_g = _dot(_head_cols(vb, kh).astype(BF16), _group_t(do_t, kh))
                pts, dsts = [], []
                for j, h in enumerate(heads):
                    st = _swa_scores_t(st_g, j, h, dist, valid)
                    l_h, d_h = l_ref[h:h + 1, :], d_ref[h:h + 1, :]
                    pt = jnp.exp(st - l_h)
                    p_sink = jnp.exp(sink_ref[0:1, h:h + 1] - l_h)
                    dsink = jnp.where(lane == h, jnp.sum(-p_sink * d_h, axis=1, keepdims=True), dsink)
                    dst = pt * (dpt_g[:, BLOCK * j:BLOCK * (j + 1)] - d_h) * _SWA_SCALE
                    pts.append(pt.astype(BF16))
                    dsts.append(dst.astype(BF16))
                pt_g, dst_g = jnp.concatenate(pts, axis=1), jnp.concatenate(dsts, axis=1)
                q_g = jnp.concatenate([_head_cols(qv, h) for h in heads], axis=0).astype(BF16)
                do_g = jnp.concatenate([_head_cols(dov, h) for h in heads], axis=0).astype(BF16)
                dkb_s[:, HEAD_DIM_A * kh:HEAD_DIM_A * (kh + 1)] = _dot(dst_g, q_g)
                dvb_s[:, HEAD_DIM_A * kh:HEAD_DIM_A * (kh + 1)] = _dot(pt_g, do_g)
                dq_g = _dot(_head_rows(kb_t, kh).astype(BF16), dst_g)
                dq_t.extend(dq_g[:, BLOCK * j:BLOCK * (j + 1)] for j in range(_GROUP_A))
            dq_ref[...] = jnp.concatenate(dq_t, axis=0).T
            ds_ref[...] += dsink
            dk_ref[...] = dk_carry[...] + dkb_s[0:BLOCK, :]
            dv_ref[...] = dv_carry[...] + dvb_s[0:BLOCK, :]
            dk_carry[...] = dkb_s[BLOCK:2 * BLOCK, :]
            dv_carry[...] = dvb_s[BLOCK:2 * BLOCK, :]

        @pl.when(n == nb)
        def _():
            dk_ref[...] = dk_carry[...]
            dv_ref[...] = dv_carry[...]

    cur = lambda n: (jnp.minimum(n, nb - 1), 0)
    cur_t = lambda n: (0, jnp.minimum(n, nb - 1))
    prv = lambda n: jnp.maximum(jnp.minimum(n, nb - 1) - 1, 0)
    out_prev = lambda n: (jnp.maximum(n - 1, 0), 0)
    return pl.pallas_call(
        body, name="swa_bwd", grid=(nb + 1,),
        in_specs=[pl.BlockSpec((BLOCK, WIDTH_A), lambda n: (jnp.minimum(n, nb - 1), _QA_BLK)),
                  pl.BlockSpec((BLOCK, LANES), lambda n: (jnp.minimum(n, nb - 1), _KA_BLK)),
                  pl.BlockSpec((BLOCK, LANES), lambda n: (prv(n), _KA_BLK)),
                  pl.BlockSpec((BLOCK, LANES), lambda n: (jnp.minimum(n, nb - 1), _VA_BLK)),
                  pl.BlockSpec((BLOCK, LANES), lambda n: (prv(n), _VA_BLK)),
                  pl.BlockSpec((BLOCK, WIDTH_A), cur), pl.BlockSpec((N_HEADS_A, BLOCK), cur_t),
                  pl.BlockSpec((N_HEADS_A, BLOCK), cur_t), pl.BlockSpec((1, BLOCK), cur_t),
                  pl.BlockSpec((BLOCK, 1), cur), pl.BlockSpec((BLOCK, 1), lambda n: (prv(n), 0)),
                  _full((1, N_HEADS_A))],
        out_specs=[pl.BlockSpec((BLOCK, WIDTH_A), cur), pl.BlockSpec((BLOCK, LANES), out_prev),
                   pl.BlockSpec((BLOCK, LANES), out_prev), _full((1, LANES))],
        out_shape=[jax.ShapeDtypeStruct((t, WIDTH_A), F32), jax.ShapeDtypeStruct((t, LANES), F32),
                   jax.ShapeDtypeStruct((t, LANES), F32), jax.ShapeDtypeStruct((1, LANES), F32)],
        scratch_shapes=[pltpu.VMEM((2 * BLOCK, LANES), F32), pltpu.VMEM((2 * BLOCK, LANES), F32),
                        pltpu.VMEM((BLOCK, LANES), F32), pltpu.VMEM((BLOCK, LANES), F32)],
        compiler_params=_params(("arbitrary",)),
    )(proj, proj, proj, proj, proj, d_out, lse, delta, posr, posc, posc, sinks)


def _in_bwd(dproj, w_in_t, x, dx1, g1, gp):
    t = x.shape[0]
    tm = 256
    steps = t // tm

    def body(dp_ref, w_ref, x_ref, dx1_ref, g_ref, gp_ref, dx_ref, dg_ref, land_ref, send_sems, recv_sems):
        i = pl.program_id(0)

        @pl.when(i == 0)
        def _():
            dg_ref[...] = jnp.zeros(dg_ref.shape, F32)
            _scatter_start(gp_ref, land_ref, send_sems, recv_sems)

        dh = _dot(dp_ref[...], w_ref[...])
        xv = x_ref[...]
        r = _rms(xv)
        dx, dg = _norm_bwd(dh, xv * r, r, g_ref[...])
        dx_ref[...] = dx1_ref[...] + dx
        dg_ref[...] += dg

        @pl.when(i == steps - 1)
        def _():
            _scatter_wait(gp_ref, land_ref, send_sems, recv_sems)

    row = lambda i: (i, 0)
    blk = pl.BlockSpec((tm, D_MODEL), row)
    return pl.pallas_call(
        body, name="in_bwd", grid=(steps,),
        in_specs=[pl.BlockSpec((tm, D_IN_PAD), row), _full((D_IN_PAD, D_MODEL)), blk, blk, _full((1, D_MODEL)), _HBM],
        out_specs=[blk, _full((1, D_MODEL)), _HBM],
        out_shape=[jax.ShapeDtypeStruct((t, D_MODEL), F32), jax.ShapeDtypeStruct((1, D_MODEL), F32),
                   jax.ShapeDtypeStruct((3,) + gp.shape[1:], gp.dtype)],
        scratch_shapes=[pltpu.SemaphoreType.DMA((3,)), pltpu.SemaphoreType.DMA((3,))],
        compiler_params=_params(("arbitrary",)),
    )(dproj, w_in_t, x, dx1, g1, gp)


def _adamw(w, g_parts, m, v, name, block, g_row_off=0):
    r, c = w.shape
    br, bc = block
    ng = len(g_parts)

    def body(*refs):
        w_ref, g_refs, m_ref, v_ref = refs[0], refs[1:1 + ng], refs[1 + ng], refs[2 + ng]
        g_out, d_out, m_out, v_out = refs[3 + ng:]
        g = g_refs[0][...]
        for gr in g_refs[1:]:
            g = g + gr[...]
        m_new = ADAM_B1 * m_ref[...] + (1.0 - ADAM_B1) * g
        v_new = ADAM_B2 * v_ref[...] + (1.0 - ADAM_B2) * jnp.square(g)
        m_hat = m_new / (1.0 - ADAM_B1 ** ADAM_STEP)
        v_hat = v_new / (1.0 - ADAM_B2 ** ADAM_STEP)
        g_out[...] = g
        d_out[...] = -ADAM_LR * (m_hat / (jnp.sqrt(v_hat) + ADAM_EPS) + ADAM_WD * w_ref[...])
        m_out[...] = m_new
        v_out[...] = v_new

    assert g_row_off % br == 0 and r % br == 0 and c % bc == 0
    blk = pl.BlockSpec(block, lambda i, j: (i, j))
    g_blk = pl.BlockSpec(block, lambda i, j: (i + g_row_off // br, j))
    return pl.pallas_call(
        body, name=name, grid=(r // br, c // bc),
        in_specs=[blk] + [g_blk] * ng + [blk, blk], out_specs=[blk] * 4,
        out_shape=[jax.ShapeDtypeStruct((r, c), F32)] * 4,
        compiler_params=_params(("parallel", "parallel")),
    )(w, *g_parts, m, v)


_HBM = pl.BlockSpec(memory_space=pltpu.HBM)


def _other_chips(x, y):
    return ((1 - x, y), (x, 1 - y), (1 - x, 1 - y))


def _gather_copies(src, out, send_sems, recv_sems, local_sem):
    x, y, c = lax.axis_index("x"), lax.axis_index("y"), lax.axis_index("c")
    me = 2 * x + y
    local = pltpu.make_async_copy(src, out.at[me], local_sem)

    def copies(arriving):
        return [pltpu.make_async_remote_copy(src_ref=src, dst_ref=out.at[2 * px + py if arriving else me],
                                             send_sem=send_sems.at[j], recv_sem=recv_sems.at[j], device_id=(px, py, c),
                                             device_id_type=MESH)
                for j, (px, py) in enumerate(_other_chips(x, y))]

    return local, copies


def _gather_start(src, out, send_sems, recv_sems, local_sem):
    local, copies = _gather_copies(src, out, send_sems, recv_sems, local_sem)
    local.start()
    for cp in copies(False):
        cp.start()


def _gather_wait(src, out, send_sems, recv_sems, local_sem):
    local, copies = _gather_copies(src, out, send_sems, recv_sems, local_sem)
    for cp in copies(True):
        cp.wait_recv()
    for cp in copies(False):
        cp.wait_send()
    local.wait()


def _scatter_copies(src, land, send_sems, recv_sems):
    x, y, c = lax.axis_index("x"), lax.axis_index("y"), lax.axis_index("c")
    return [pltpu.make_async_remote_copy(src_ref=src.at[2 * px + py], dst_ref=land.at[j], send_sem=send_sems.at[j],
                                         recv_sem=recv_sems.at[j], device_id=(px, py, c), device_id_type=MESH)
            for j, (px, py) in enumerate(_other_chips(x, y))]


def _scatter_start(src, land, send_sems, recv_sems):
    for cp in _scatter_copies(src, land, send_sems, recv_sems):
        cp.start()


def _scatter_wait(src, land, send_sems, recv_sems):
    copies = _scatter_copies(src, land, send_sems, recv_sems)
    for cp in copies:
        cp.wait_recv()
    for cp in copies:
        cp.wait_send()


def _all_gather_chips(packed):
    def body(src, out, send_sems, recv_sems, local_sem):
        _gather_start(src, out, send_sems, recv_sems, local_sem)
        _gather_wait(src, out, send_sems, recv_sems, local_sem)

    return pl.pallas_call(
        body, name="ag_weights", in_specs=[_HBM], out_specs=_HBM,
        out_shape=jax.ShapeDtypeStruct((N_CHIPS,) + packed.shape, packed.dtype),
        scratch_shapes=[pltpu.SemaphoreType.DMA((3,)), pltpu.SemaphoreType.DMA((3,)), pltpu.SemaphoreType.DMA(())],
    )(packed)


def _sum4(gp, land, chip, name):
    _, r, w = gp.shape
    tr = 128

    def body(chip_ref, o_ref, l_ref, s_ref):
        s_ref[...] = ((o_ref[0] + l_ref[0].astype(F32)) + l_ref[1].astype(F32)) + l_ref[2].astype(F32)

    return pl.pallas_call(
        body, name=name,
        grid_spec=pltpu.PrefetchScalarGridSpec(
            num_scalar_prefetch=1, grid=(r // tr,),
            in_specs=[pl.BlockSpec((1, tr, w), lambda i, chip_ref: (chip_ref[0], i, 0)),
                      pl.BlockSpec((3, tr, w), lambda i, chip_ref: (0, i, 0))],
            out_specs=pl.BlockSpec((tr, w), lambda i, chip_ref: (i, 0))),
        out_shape=jax.ShapeDtypeStruct((r, w), F32),
        compiler_params=_params(("parallel",)),
    )(chip, gp, land)


def _swap_sibling(s, name):
    def body(src, got, send_sem, recv_sem):
        x, y, c = lax.axis_index("x"), lax.axis_index("y"), lax.axis_index("c")
        cp = pltpu.make_async_remote_copy(src_ref=src, dst_ref=got, send_sem=send_sem, recv_sem=recv_sem,
                                          device_id=(x, y, 1 - c), device_id_type=MESH)
        cp.start()
        cp.wait_recv()
        cp.wait_send()

    return pl.pallas_call(
        body, name=name, in_specs=[_HBM], out_specs=_HBM,
        out_shape=jax.ShapeDtypeStruct(s.shape, s.dtype),
        scratch_shapes=[pltpu.SemaphoreType.DMA(()), pltpu.SemaphoreType.DMA(())],
    )(s)


def _all_reduce_small(part):
    n_dev = 8

    def body(src, out, gath, send_sems, recv_sems):
        x, y, c = lax.axis_index("x"), lax.axis_index("y"), lax.axis_index("c")
        me = 4 * x + 2 * y + c
        gath[me] = src[...]
        peers = []
        for k in range(1, n_dev):
            px = 1 - x if (k >> 2) & 1 else x
            py = 1 - y if (k >> 1) & 1 else y
            pc = 1 - c if k & 1 else c
            peers.append((px, py, pc))
        sends = []
        for j, peer in enumerate(peers):
            cp = pltpu.make_async_remote_copy(src_ref=src, dst_ref=gath.at[me], send_sem=send_sems.at[j],
                                              recv_sem=recv_sems.at[j], device_id=peer, device_id_type=MESH)
            cp.start()
            sends.append(cp)
        for j, (px, py, pc) in enumerate(peers):
            pltpu.make_async_remote_copy(src_ref=src, dst_ref=gath.at[4 * px + 2 * py + pc], send_sem=send_sems.at[j],
                                         recv_sem=recv_sems.at[j], device_id=(px, py, pc), device_id_type=MESH).wait_recv()
        for cp in sends:
            cp.wait_send()
        acc = gath[0]
        for d in range(1, n_dev):
            acc = acc + gath[d]
        out[...] = acc

    vmem = pl.BlockSpec(memory_space=pltpu.VMEM)
    return pl.pallas_call(
        body, name="ar_small", in_specs=[vmem], out_specs=vmem,
        out_shape=jax.ShapeDtypeStruct(part.shape, F32),
        scratch_shapes=[pltpu.VMEM((n_dev,) + part.shape, F32), pltpu.SemaphoreType.DMA((n_dev - 1,)),
                        pltpu.SemaphoreType.DMA((n_dev - 1,))],
    )(part)


_W_IN_ROWS = SHARD_SHAPES["w_in"][1]
_KR_ROW = 3200
_KR_PAD_ROW = _KR_BLK * LANES + QK_NOPE


def _shard_rows(name, a):
    return jnp.transpose(a) if name == "w_in" else a.reshape(PACK_ROWS[name], D_MODEL)


def _pack(group, shards, dtype):
    parts = [_shard_rows(n, shards[n]).astype(dtype) for n in group]
    pad = -sum(PACK_ROWS[n] for n in group) % LANES
    if pad:
        parts.append(jnp.zeros((pad, D_MODEL), dtype))
    return jnp.concatenate(parts, axis=0)


def _col_sharded_full(g, name, group):
    r, c = SHARD_SHAPES[name]
    off = _row_offset(group, name)
    blocks = g[:, off:off + PACK_ROWS[name]].reshape(N_CHIPS, r, c)
    return jnp.transpose(blocks, (1, 0, 2)).reshape(r, N_CHIPS * c)


def _col_sharded_blocks(d, name):
    r, c = SHARD_SHAPES[name]
    return jnp.transpose(d.reshape(r, N_CHIPS, c), (1, 0, 2)).reshape(N_CHIPS, PACK_ROWS[name], D_MODEL)


def _weights_a(g):
    dt = g.dtype
    w_in_t = g[:, :_W_IN_ROWS].reshape(N_CHIPS * _W_IN_ROWS, D_MODEL)
    z = lambda n: jnp.zeros((n, D_MODEL), dt)
    w_in_t = jnp.concatenate([w_in_t[:_KR_ROW], z(_KR_PAD_ROW - _KR_ROW), w_in_t[_KR_ROW:],
                              z(D_IN_PAD - _KR_PAD_ROW - QK_ROPE)], axis=0)
    wq = _col_sharded_full(g, "w_q_b", GROUP_A).reshape(Q_LORA, N_HEADS_B, Q_HEAD_B)
    wq_p = jnp.concatenate([wq, jnp.zeros((Q_LORA, N_HEADS_B, HEAD_PAD - Q_HEAD_B), dt)], axis=2).reshape(Q_LORA, MLA_W)
    wkv = _col_sharded_full(g, "w_kv_b", GROUP_A).reshape(KV_LORA, N_HEADS_B, QK_NOPE + V_DIM_B)
    zk = jnp.zeros((KV_LORA, N_HEADS_B, HEAD_PAD - QK_NOPE), dt)
    wk_p = jnp.concatenate([wkv[:, :, :QK_NOPE], zk], axis=2).reshape(KV_LORA, MLA_W)
    wv = wkv[:, :, QK_NOPE:].reshape(KV_LORA, N_HEADS_B * V_DIM_B)
    return dict(w_in=w_in_t, wq=wq_p, wk=wk_p, wv=wv)


def _grad_blocks_a(dw_in_t, dwq_p, dwk_p, dwv):
    dw_in = jnp.concatenate([dw_in_t[:_KR_ROW], dw_in_t[_KR_PAD_ROW:_KR_PAD_ROW + QK_ROPE]], axis=0)
    dwq = dwq_p.reshape(Q_LORA, N_HEADS_B, HEAD_PAD)[:, :, :Q_HEAD_B].reshape(Q_LORA, N_HEADS_B * Q_HEAD_B)
    dwk = dwk_p.reshape(KV_LORA, N_HEADS_B, HEAD_PAD)[:, :, :QK_NOPE]
    dwkv = jnp.concatenate([dwk, dwv.reshape(KV_LORA, N_HEADS_B, V_DIM_B)], axis=2)
    dwkv = dwkv.reshape(KV_LORA, N_HEADS_B * (QK_NOPE + V_DIM_B))
    pad = -sum(PACK_ROWS[n] for n in GROUP_A) % LANES
    return jnp.concatenate([dw_in.reshape(N_CHIPS, _W_IN_ROWS, D_MODEL), _col_sharded_blocks(dwq, "w_q_b"),
                            _col_sharded_blocks(dwkv, "w_kv_b"), jnp.zeros((N_CHIPS, pad, D_MODEL), F32)], axis=1)


def _rope_freq_lanes():
    freqs = ROPE_THETA ** (-jnp.arange(0, QK_ROPE, 2, dtype=F32) / QK_ROPE)
    return jnp.concatenate([jnp.zeros((QK_NOPE,), F32), freqs, freqs,
                            jnp.zeros((HEAD_PAD - Q_HEAD_B,), F32)]).reshape(1, LANES)


def _fwd_bwd(x, positions, target, w):
    t = x.shape[0]
    wa = _weights_a(_all_gather_chips(_pack(GROUP_A, w, BF16)))
    posr = positions.astype(F32).reshape(1, t)
    posc = posr.reshape(t, 1)
    freq = _rope_freq_lanes()
    g1, g2, g3, g4 = w["pre_norm_mix"], w["post_norm_mix"], w["pre_norm_mlp"], w["post_norm_mlp"]
    qan, kvan, sinks = w["q_a_norm"], w["kv_a_norm"], w["sinks"]

    h, proj = _proj_fwd(x, g1, wa["w_in"])
    out_a, lse_a = _swa_fwd(proj, posc, posr, sinks)
    qm, km, qt, kt, vm, vt = _mla_prep_fwd(proj, posc, freq, qan, kvan, wa["wq"], wa["wk"], wa["wv"])
    out_bt, lse_b, wb = _mla_fwd(km, qt, vt, _pack(GROUP_B, w, BF16))
    w_oa, w_ob = _col_sharded_full(wb, "w_o_a", GROUP_B), _col_sharded_full(wb, "w_o_b", GROUP_B)
    merged, y, x1, h2 = _mix_out_fwd(out_a, out_bt, proj, x, w_oa, w_ob, wb, g2, g3)
    a = _up_fwd(h2, wb)
    dx2, dyd, dg4, loss = _down_fwd_loss(a, wb, x1, target, g4)

    dw_down = _matmul_tn(a, dyd, "dw_down", 1024, 1024, tk=512)
    du = _down_bwd(dyd, wb, a)
    dw_up = _matmul_tn(h2, du, "dw_up", 1024, 1024, tk=512, shard_major=True)
    dx1, dy, dg3, dg2 = _up_bwd(du, wb, x1, dx2, y, g3, g2)
    dw_out = _matmul_tn(merged, dy, "dw_out", 1024, 1024, tk=512)
    doa, dob, dga, dgb, d_out_a, d_out_b, d_out_bt, del_a, del_b = _mix_out_bwd(dy, out_a, out_bt, proj, w_oa, w_ob, wb)
    dw_oa = _matmul_tn(out_a, doa, "dw_o_a", 512, 1024)
    dw_ob = _dw_ob(out_bt, dob)
    gp_b = jnp.concatenate([dw_up, dw_down.reshape(N_CHIPS, -1, D_MODEL), dw_out.reshape(N_CHIPS, -1, D_MODEL),
                            _col_sharded_blocks(dw_oa, "w_o_a"), _col_sharded_blocks(dw_ob, "w_o_b")], axis=1)
    dqm, dkm, dvm, land_b = _mla_bwd(qm, km, qt, kt, vm, d_out_b, d_out_bt, lse_b, del_b, gp_b)
    dcq, dckv, dkr, dwq, dwk, dwv, dqan, dkvan = _mla_prep_bwd(
        dqm, dkm, dvm, proj, posc, freq, qan, kvan, wa["wq"], wa["wk"], wa["wv"])
    dqa, dka, dva, dsinks = _swa_bwd(proj, d_out_a, lse_a, del_a, posc, posr, sinks)
    dproj = jnp.concatenate([dga, dgb, dqa.astype(BF16), dka.astype(BF16), dva.astype(BF16), dcq, dckv, dkr], axis=1)
    dw_in_t = _matmul_tn(dproj, h, "dw_in", D_IN_PAD // 2, 1024, tk=512)
    gp_a = _grad_blocks_a(dw_in_t, dwq, dwk, dwv)
    grad_x, dg1, land_a = _in_bwd(dproj, wa["w_in"], x, dx1, g1, gp_a.astype(BF16))

    dsmall = dict(pre_norm_mix=dg1, post_norm_mix=dg2, pre_norm_mlp=dg3, post_norm_mlp=dg4,
                  q_a_norm=dqan, kv_a_norm=dkvan, sinks=dsinks[:, :N_HEADS_A])
    return loss, grad_x, {GROUP_A: (gp_a, land_a), GROUP_B: (gp_b, land_b)}, dsmall


def _pack_small(p, extra=None):
    tail = jnp.concatenate([p["q_a_norm"], p["kv_a_norm"], p["sinks"],
                            jnp.zeros((1, D_MODEL - Q_LORA - KV_LORA - N_HEADS_A), F32)], axis=1)
    scalar = jnp.zeros((1, D_MODEL), F32)
    if extra is not None:
        scalar = scalar.at[0, 0].set(extra)
    return jnp.concatenate([p["pre_norm_mix"], p["post_norm_mix"], p["pre_norm_mlp"], p["post_norm_mlp"], tail, scalar,
                            jnp.zeros((2, D_MODEL), F32)], axis=0)


def _unpack_small(b):
    return dict(pre_norm_mix=b[0:1], post_norm_mix=b[1:2], pre_norm_mlp=b[2:3], post_norm_mlp=b[3:4],
                q_a_norm=b[4:5, :Q_LORA], kv_a_norm=b[4:5, Q_LORA:Q_LORA + KV_LORA],
                sinks=b[4:5, Q_LORA + KV_LORA:Q_LORA + KV_LORA + N_HEADS_A])


def kernel(x, positions, pre_norm_mix, w_in, q_a_norm, w_q_b, kv_a_norm, w_kv_b, sinks, w_o_a, w_o_b, w_out, post_norm_mix, pre_norm_mlp, w_up, w_down, post_norm_mlp, loss_target, m_pre_norm_mix, m_w_in, m_q_a_norm, m_w_q_b, m_kv_a_norm, m_w_kv_b, m_sinks, m_w_o_a, m_w_o_b, m_w_out, m_post_norm_mix, m_pre_norm_mlp, m_w_up, m_w_down, m_post_norm_mlp, v_pre_norm_mix, v_w_in, v_q_a_norm, v_w_q_b, v_kv_a_norm, v_w_kv_b, v_sinks, v_w_o_a, v_w_o_b, v_w_out, v_post_norm_mix, v_pre_norm_mlp, v_w_up, v_w_down, v_post_norm_mlp):
    w = dict(pre_norm_mix=pre_norm_mix, w_in=w_in[0], q_a_norm=q_a_norm, w_q_b=w_q_b[0], kv_a_norm=kv_a_norm,
             w_kv_b=w_kv_b[0], sinks=sinks, w_o_a=w_o_a[0], w_o_b=w_o_b[0], w_out=w_out[0],
             post_norm_mix=post_norm_mix, pre_norm_mlp=pre_norm_mlp, w_up=w_up[0], w_down=w_down[0],
             post_norm_mlp=post_norm_mlp)
    m = dict(pre_norm_mix=m_pre_norm_mix, w_in=m_w_in[0], q_a_norm=m_q_a_norm, w_q_b=m_w_q_b[0],
             kv_a_norm=m_kv_a_norm, w_kv_b=m_w_kv_b[0], sinks=m_sinks, w_o_a=m_w_o_a[0], w_o_b=m_w_o_b[0],
             w_out=m_w_out[0], post_norm_mix=m_post_norm_mix, pre_norm_mlp=m_pre_norm_mlp, w_up=m_w_up[0],
             w_down=m_w_down[0], post_norm_mlp=m_post_norm_mlp)
    v = dict(pre_norm_mix=v_pre_norm_mix, w_in=v_w_in[0], q_a_norm=v_q_a_norm, w_q_b=v_w_q_b[0],
             kv_a_norm=v_kv_a_norm, w_kv_b=v_w_kv_b[0], sinks=v_sinks, w_o_a=v_w_o_a[0], w_o_b=v_w_o_b[0],
             w_out=v_w_out[0], post_norm_mix=v_post_norm_mix, pre_norm_mlp=v_pre_norm_mlp, w_up=v_w_up[0],
             w_down=v_w_down[0], post_norm_mlp=v_post_norm_mlp)

    loss, grad_x, blocks, dsmall = _fwd_bwd(x[0], positions, loss_target[0], w)

    red = _all_reduce_small(_pack_small(dsmall, loss[0, 0]))
    small = [_unpack_small(o) for o in
             _adamw(_pack_small(w), [red], _pack_small(m), _pack_small(v), "adamw_small", (8, D_MODEL))]

    chip = (2 * lax.axis_index("x") + lax.axis_index("y")).astype(jnp.int32).reshape(1)
    reduced = {}
    for group, tag in ((GROUP_A, "a"), (GROUP_B, "b")):
        gp, land = blocks[group]
        part = _sum4(gp, land, chip, "rs_sum_" + tag)
        reduced[group] = [part, _swap_sibling(part, "rs_swap_" + tag)]

    big = {}
    tr = jnp.transpose
    big["w_in"] = [tr(o)[None] for o in _adamw(tr(w["w_in"]), reduced[GROUP_A], tr(m["w_in"]), tr(v["w_in"]),
                                               "adamw_w_in", (_W_IN_ROWS, 256))]
    for n in ("w_up", "w_down", "w_out"):
        big[n] = [o[None] for o in _adamw(w[n], reduced[GROUP_B], m[n], v[n], "adamw_" + n, (128, D_MODEL),
                                          _row_offset(GROUP_B, n))]
    for group, names in ((GROUP_A, ("w_q_b", "w_kv_b")), (GROUP_B, ("w_o_a", "w_o_b"))):
        for n in names:
            off = _row_offset(group, n)
            g_parts = [p[off:off + PACK_ROWS[n]].reshape(SHARD_SHAPES[n]) for p in reduced[group]]
            big[n] = [o[None] for o in _adamw(w[n], g_parts, m[n], v[n], "adamw_" + n, SHARD_SHAPES[n])]

    outs = [big[n][k] if n in big else small[k][n] for k in range(4) for n in WEIGHTS]
    return (red[5, 0], grad_x[None], *outs)
```

```python
import jax
import jax.numpy as jnp
from jax import lax
from jax.experimental import pallas as pl
from jax.experimental.pallas import tpu as pltpu

F32 = jnp.float32
BF16 = jnp.bfloat16
MESH = pl.DeviceIdType.MESH

D_MODEL = 1024
N_HEADS_A = 8
N_KV_A = 2
HEAD_DIM_A = 64
WINDOW = 128
BLOCK = 128
N_HEADS_B = 8
QK_NOPE = 64
QK_ROPE = 32
V_DIM_B = 64
Q_LORA = 256
KV_LORA = 128
ROPE_THETA = 10000.0
D_FF = 4 * D_MODEL
EPS = 1e-6
WIDTH_A = N_HEADS_A * HEAD_DIM_A
Q_HEAD_B = QK_NOPE + QK_ROPE
D_IN_PAD = 3328
HEAD_PAD = 128
MLA_W = N_HEADS_B * HEAD_PAD

ADAM_LR = 0.001
ADAM_B1 = 0.9
ADAM_B2 = 0.999
ADAM_EPS = 1e-08
ADAM_WD = 0.01
ADAM_STEP = 10

NEG = -1e30
N_CHIPS = 4
LANES = 128
VMEM_LIMIT = 56 * 1024 * 1024

SHARD_SHAPES = {"w_in": (1024, 808), "w_q_b": (256, 192), "w_kv_b": (128, 256), "w_o_a": (512, 256),
                "w_o_b": (512, 256), "w_out": (256, 1024), "w_up": (1024, 1024), "w_down": (1024, 1024)}
PACK_ROWS = {n: (s[0] * s[1]) // D_MODEL for n, s in SHARD_SHAPES.items()}
GROUP_A = ("w_in", "w_q_b", "w_kv_b")
GROUP_B = ("w_up", "w_down", "w_out", "w_o_a", "w_o_b")
WEIGHTS = ("pre_norm_mix", "w_in", "q_a_norm", "w_q_b", "kv_a_norm", "w_kv_b", "sinks", "w_o_a", "w_o_b", "w_out",
           "post_norm_mix", "pre_norm_mlp", "w_up", "w_down", "post_norm_mlp")


def _params(sem=None):
    return pltpu.CompilerParams(dimension_semantics=sem, vmem_limit_bytes=VMEM_LIMIT)


def _dot(a, b):
    return jnp.dot(a, b, preferred_element_type=F32)


def _dot_nt(a, b):
    return lax.dot_general(a, b, (((1,), (1,)), ((), ())), preferred_element_type=F32)


def _dot_tn(a, b):
    return lax.dot_general(a, b, (((0,), (0,)), ((), ())), preferred_element_type=F32)


def _rms(v):
    return lax.rsqrt(jnp.mean(v * v, axis=-1, keepdims=True) + EPS)


def _norm_bwd(dout, n, r, g):
    dn = dout * g
    dx = r * (dn - n * jnp.mean(dn * n, axis=-1, keepdims=True))
    return dx, jnp.sum(dout * n, axis=0, keepdims=True)


def _full(shape):
    return pl.BlockSpec(shape, lambda *_: (0,) * len(shape))


def _row_offset(group, name):
    return sum(PACK_ROWS[n] for n in group[:group.index(name)])


def _wb_spec(name):
    rows = PACK_ROWS[name]
    return pl.BlockSpec((N_CHIPS, rows, D_MODEL), lambda *_: (0, _row_offset(GROUP_B, name) // rows, 0))


def _proj_fwd(x, g1, w_in_t):
    t = x.shape[0]
    tm = 256

    def body(x_ref, g_ref, w_ref, h_ref, p_ref):
        xv = x_ref[...]
        h = ((xv * _rms(xv)) * g_ref[...]).astype(BF16)
        h_ref[...] = h
        p_ref[...] = _dot_nt(h, w_ref[...])

    return pl.pallas_call(
        body, name="proj_fwd", grid=(t // tm,),
        in_specs=[pl.BlockSpec((tm, D_MODEL), lambda i: (i, 0)), _full((1, D_MODEL)), _full((D_IN_PAD, D_MODEL))],
        out_specs=[pl.BlockSpec((tm, D_MODEL), lambda i: (i, 0)), pl.BlockSpec((tm, D_IN_PAD), lambda i: (i, 0))],
        out_shape=[jax.ShapeDtypeStruct((t, D_MODEL), BF16), jax.ShapeDtypeStruct((t, D_IN_PAD), F32)],
        compiler_params=_params(("parallel",)),
    )(x, g1, w_in_t)


_QA_BLK = 2048 // WIDTH_A
_KA_BLK = 2560 // LANES
_VA_BLK = 2688 // LANES
_CQ_BLK = 2816 // Q_LORA
_CKV_BLK = 3072 // LANES
_KR_BLK = 3200 // LANES


_GROUP_A = N_HEADS_A // N_KV_A
_SWA_SCALE = HEAD_DIM_A ** -0.5


def _head_cols(v, h):
    return v[:, HEAD_DIM_A * h:HEAD_DIM_A * (h + 1)]


def _head_rows(v, h):
    return v[HEAD_DIM_A * h:HEAD_DIM_A * (h + 1), :]


def _swa_band(n, kp_ref, kc_ref, vp_ref, vc_ref, pq_ref, pp_ref, pc_ref):
    kb = jnp.concatenate([kp_ref[...], kc_ref[...]], axis=0)
    vb = jnp.concatenate([vp_ref[...], vc_ref[...]], axis=0)
    posk = jnp.concatenate([pp_ref[...], pc_ref[...]], axis=0)
    dist = jnp.abs(posk - pq_ref[...])
    ki = lax.broadcasted_iota(jnp.int32, (2 * BLOCK, BLOCK), 0)
    qi = lax.broadcasted_iota(jnp.int32, (2 * BLOCK, BLOCK), 1)
    valid = (ki > qi) & (ki <= qi + WINDOW) & ((n > 0) | (ki >= BLOCK))
    return kb, vb, dist, valid


def _swa_scores_t(st_g, j, h, dist, valid):
    slope = 2.0 ** (-8.0 * (h + 1) / N_HEADS_A)
    st = st_g[:, BLOCK * j:BLOCK * (j + 1)] * _SWA_SCALE - slope * dist
    return jnp.where(valid, st, NEG)


def _group_t(xt, kh):
    return jnp.concatenate([_head_rows(xt, _GROUP_A * kh + j) for j in range(_GROUP_A)], axis=1).astype(BF16)


def _swa_fwd(proj, posc, posr, sinks):
    t = proj.shape[0]
    nb = t // BLOCK

    def body(q_ref, kc_ref, kp_ref, vc_ref, vp_ref, pq_ref, pc_ref, pp_ref, sink_ref, o_ref, l_ref):
        n = pl.program_id(0)
        kb, vb, dist, valid = _swa_band(n, kp_ref, kc_ref, vp_ref, vc_ref, pq_ref, pp_ref, pc_ref)
        q_t, vb_t = q_ref[...].T, vb.T
        out_t, lse = [], []
        for kh in range(N_KV_A):
            st_g = _dot(_head_cols(kb, kh).astype(BF16), _group_t(q_t, kh))
            ps = []
            for j in range(_GROUP_A):
                h = _GROUP_A * kh + j
                st = _swa_scores_t(st_g, j, h, dist, valid)
                sink = sink_ref[0:1, h:h + 1]
                m = jnp.maximum(jnp.max(st, axis=0, keepdims=True), sink)
                e = jnp.exp(st - m)
                den = jnp.sum(e, axis=0, keepdims=True) + jnp.exp(sink - m)
                ps.append((e / den).astype(BF16))
                lse.append(m + jnp.log(den))
            o_g = _dot(_head_rows(vb_t, kh).astype(BF16), jnp.concatenate(ps, axis=1))
            out_t.extend(o_g[:, BLOCK * j:BLOCK * (j + 1)] for j in range(_GROUP_A))
        o_ref[...] = jnp.concatenate(out_t, axis=0).T
        l_ref[...] = jnp.concatenate(lse, axis=0)

    cur = lambda n: (n, 0)
    prev = lambda n: jnp.maximum(n - 1, 0)
    return pl.pallas_call(
        body, name="swa_fwd", grid=(nb,),
        in_specs=[pl.BlockSpec((BLOCK, WIDTH_A), lambda n: (n, _QA_BLK)),
                  pl.BlockSpec((BLOCK, LANES), lambda n: (n, _KA_BLK)),
                  pl.BlockSpec((BLOCK, LANES), lambda n: (prev(n), _KA_BLK)),
                  pl.BlockSpec((BLOCK, LANES), lambda n: (n, _VA_BLK)),
                  pl.BlockSpec((BLOCK, LANES), lambda n: (prev(n), _VA_BLK)),
                  pl.BlockSpec((1, BLOCK), lambda n: (0, n)),
                  pl.BlockSpec((BLOCK, 1), cur),
                  pl.BlockSpec((BLOCK, 1), lambda n: (prev(n), 0)),
                  _full((1, N_HEADS_A))],
        out_specs=[pl.BlockSpec((BLOCK, WIDTH_A), cur), pl.BlockSpec((N_HEADS_A, BLOCK), lambda n: (0, n))],
        out_shape=[jax.ShapeDtypeStruct((t, WIDTH_A), F32), jax.ShapeDtypeStruct((N_HEADS_A, t), F32)],
        compiler_params=_params(("parallel",)),
    )(proj, proj, proj, proj, proj, posr, posc, posc, sinks)


def _rope_coeffs(pos, freq):
    ang = pos * freq
    cosv, sinv = jnp.cos(ang), jnp.sin(ang)
    lane = lax.broadcasted_iota(jnp.int32, ang.shape, 1)
    lo = (lane >= QK_NOPE) & (lane < QK_NOPE + QK_ROPE // 2)
    hi = (lane >= QK_NOPE + QK_ROPE // 2) & (lane < QK_NOPE + QK_ROPE)
    c = jnp.where(lane < QK_NOPE, 1.0, jnp.where(lo | hi, cosv, 0.0))
    s = jnp.where(lo, -sinv, jnp.where(hi, sinv, 0.0))
    return c, s, lo, hi


def _rope(xh, c, s, lo):
    up = pltpu.roll(xh, LANES - QK_ROPE // 2, axis=1)
    dn = pltpu.roll(xh, QK_ROPE // 2, axis=1)
    return xh * c + jnp.where(lo, up, dn) * s


def _unrope(dh, c, s, lo, hi):
    g = dh * s
    up = pltpu.roll(g, LANES - QK_ROPE // 2, axis=1)
    dn = pltpu.roll(g, QK_ROPE // 2, axis=1)
    return dh * c + jnp.where(hi, dn, jnp.where(lo, up, 0.0))


_TQ = 512
_MLA_SCALE = Q_HEAD_B ** -0.5


def _mla_prep_fwd(proj, posc, freq, qan, kvan, wq, wk, wv):
    t = proj.shape[0]
    tm = _TQ
    nb = t // tm

    def body(cq_ref, ckv_ref, kr_ref, pos_ref, f_ref, qan_ref, kvan_ref, wq_ref, wk_ref, wv_ref,
             q_ref, k_ref, qt_ref, kt_ref, v_ref, vt_ref):
        cq = cq_ref[...]
        cqn = ((cq * _rms(cq)) * qan_ref[...]).astype(BF16)
        ckv = ckv_ref[...]
        ckvn = ((ckv * _rms(ckv)) * kvan_ref[...]).astype(BF16)
        qb = _dot(cqn, wq_ref[...])
        kb = _dot(ckvn, wk_ref[...])
        vb = _dot(ckvn, wv_ref[...])
        vbt = vb.T
        c, s, lo, _ = _rope_coeffs(pos_ref[...], f_ref[...])
        kr = _rope(kr_ref[...], c, s, lo)
        for h in range(N_HEADS_B):
            sl = slice(HEAD_PAD * h, HEAD_PAD * (h + 1))
            q_h = _rope(qb[:, sl], c, s, lo)
            k_h = kb[:, sl] + kr
            q_ref[:, sl] = q_h.astype(BF16)
            k_ref[:, sl] = k_h.astype(BF16)
            qt_ref[h, 0] = q_h.T.astype(BF16)
            kt_ref[h, 0] = k_h.T.astype(BF16)
            v_ref[h] = vb[:, V_DIM_B * h:V_DIM_B * (h + 1)].astype(BF16)
            vt_ref[h, 0] = vbt[V_DIM_B * h:V_DIM_B * (h + 1), :].astype(BF16)

    row = lambda i: (i, 0)
    blk4 = lambda d: pl.BlockSpec((N_HEADS_B, 1, d, tm), lambda i: (0, i, 0, 0))
    return pl.pallas_call(
        body, name="mla_prep_fwd", grid=(nb,),
        in_specs=[pl.BlockSpec((tm, Q_LORA), lambda i: (i, _CQ_BLK)),
                  pl.BlockSpec((tm, LANES), lambda i: (i, _CKV_BLK)),
                  pl.BlockSpec((tm, LANES), lambda i: (i, _KR_BLK)),
                  pl.BlockSpec((tm, 1), row), _full((1, LANES)), _full((1, Q_LORA)), _full((1, KV_LORA)),
                  _full((Q_LORA, MLA_W)), _full((KV_LORA, MLA_W)), _full((KV_LORA, N_HEADS_B * V_DIM_B))],
        out_specs=[pl.BlockSpec((tm, MLA_W), row), pl.BlockSpec((tm, MLA_W), row), blk4(HEAD_PAD), blk4(HEAD_PAD),
                   pl.BlockSpec((N_HEADS_B, tm, V_DIM_B), lambda i: (0, i, 0)), blk4(V_DIM_B)],
        out_shape=[jax.ShapeDtypeStruct((t, MLA_W), BF16), jax.ShapeDtypeStruct((t, MLA_W), BF16),
                   jax.ShapeDtypeStruct((N_HEADS_B, nb, HEAD_PAD, tm), BF16),
                   jax.ShapeDtypeStruct((N_HEADS_B, nb, HEAD_PAD, tm), BF16),
                   jax.ShapeDtypeStruct((N_HEADS_B, t, V_DIM_B), BF16),
                   jax.ShapeDtypeStruct((N_HEADS_B, nb, V_DIM_B, tm), BF16)],
        compiler_params=_params(("parallel",)),
    )(proj, proj, proj, posc, freq, qan, kvan, wq, wk, wv)


_LOG2E = 1.4426950408889634
_MLA_SCALE2 = _MLA_SCALE * _LOG2E


def _scores_t(k, qt, diagonal):
    st = _dot(k, qt) * _MLA_SCALE2
    if diagonal:
        key = lax.broadcasted_iota(jnp.int32, st.shape, 0)
        qry = lax.broadcasted_iota(jnp.int32, st.shape, 1)
        st = jnp.where(key <= qry, st, NEG)
    return st


def _mla_fwd(k, qt, vt, w_src):
    t = k.shape[0]
    nb = t // _TQ

    def body(k_ref, qt_ref, vt_ref, w_ref, o_ref, l_ref, wg_ref, raw_a, raw_b, send_sems, recv_sems, local_sem):
        qi = pl.program_id(1)
        first = (pl.program_id(0) == 0) & (qi == 0)
        last = (pl.program_id(0) == N_HEADS_B - 1) & (qi == nb - 1)

        @pl.when(first)
        def _():
            _gather_start(w_ref, wg_ref, send_sems, recv_sems, local_sem)

        q_t = qt_ref[0, 0]

        def product(kj):
            return _dot(k_ref[pl.ds(pl.multiple_of(kj * _TQ, _TQ), _TQ), :], q_t)

        def update(stats, raw_ref, kj, diagonal=False):
            m, l, acc = stats
            st = raw_ref[...] * _MLA_SCALE2
            if diagonal:
                key = lax.broadcasted_iota(jnp.int32, st.shape, 0)
                qry = lax.broadcasted_iota(jnp.int32, st.shape, 1)
                st = jnp.where(key <= qry, st, NEG)
            m_new = jnp.maximum(m, jnp.max(st, axis=0, keepdims=True))
            alpha = jnp.exp2(m - m_new)
            p = jnp.exp2(st - m_new)
            l = alpha * l + jnp.sum(p, axis=0, keepdims=True)
            acc = alpha * acc + _dot(vt_ref[0, kj], p.astype(BF16))
            return m_new, l, acc

        def trip(i, stats):
            raw_b[...] = product(2 * i + 1)
            stats = update(stats, raw_a, 2 * i)
            raw_a[...] = product(2 * i + 2)
            return update(stats, raw_b, 2 * i + 1)

        def tail_even(stats):
            return update(stats, raw_a, qi, True)

        def tail_odd(stats):
            raw_b[...] = product(qi)
            return update(update(stats, raw_a, qi - 1), raw_b, qi, True)

        init = (jnp.full((1, _TQ), NEG, F32), jnp.zeros((1, _TQ), F32), jnp.zeros((V_DIM_B, _TQ), F32))
        raw_a[...] = product(0)
        stats = lax.fori_loop(0, qi // 2, trip, init)
        m, l, acc = lax.cond(qi % 2 == 0, tail_even, tail_odd, stats)
        o_ref[0, 0] = acc / l
        l_ref[0, 0] = m + jnp.log(l) * _LOG2E

        @pl.when(last)
        def _():
            _gather_wait(w_ref, wg_ref, send_sems, recv_sems, local_sem)

    return pl.pallas_call(
        body, name="mla_fwd", grid=(N_HEADS_B, nb),
        in_specs=[pl.BlockSpec((t, HEAD_PAD), lambda h, qi: (0, h)),
                  pl.BlockSpec((1, 1, HEAD_PAD, _TQ), lambda h, qi: (h, qi, 0, 0)),
                  pl.BlockSpec((1, nb, V_DIM_B, _TQ), lambda h, qi: (h, 0, 0, 0)), _HBM],
        out_specs=[pl.BlockSpec((1, 1, V_DIM_B, _TQ), lambda h, qi: (h, qi, 0, 0)),
                   pl.BlockSpec((1, 1, 1, _TQ), lambda h, qi: (h, qi, 0, 0)), _HBM],
        out_shape=[jax.ShapeDtypeStruct((N_HEADS_B, nb, V_DIM_B, _TQ), F32),
                   jax.ShapeDtypeStruct((N_HEADS_B, nb, 1, _TQ), F32),
                   jax.ShapeDtypeStruct((N_CHIPS,) + w_src.shape, w_src.dtype)],
        scratch_shapes=[pltpu.VMEM((_TQ, _TQ), F32), pltpu.VMEM((_TQ, _TQ), F32),
                        pltpu.SemaphoreType.DMA((3,)), pltpu.SemaphoreType.DMA((3,)), pltpu.SemaphoreType.DMA(())],
        compiler_params=_params(("arbitrary", "arbitrary")),
    )(k, qt, vt, w_src)


def _ot_spec(tm, d):
    per = _TQ // tm
    return pl.BlockSpec((N_HEADS_B, 1, d, tm), lambda i: (0, i // per, 0, i % per))


def _mix_out_fwd(out_a, out_bt, proj, x, w_oa, w_ob, wb, g2, g3):
    t = x.shape[0]
    tm = 256

    def body(oa_ref, obt_ref, ga_ref, gb_ref, x_ref, woa_ref, wob_ref, wout_ref, g2_ref, g3_ref,
             mg_ref, y_ref, x1_ref, h2_ref):
        oa = _dot(oa_ref[...].astype(BF16), woa_ref[...])
        obt = obt_ref[...].reshape(N_HEADS_B * V_DIM_B, tm).astype(BF16)
        ob = _dot_tn(obt, wob_ref[...])
        merged = (jax.nn.sigmoid(ga_ref[...]) * oa + jax.nn.sigmoid(gb_ref[...]) * ob).astype(BF16)
        mg_ref[...] = merged
        y = _dot(merged, wout_ref[...].reshape(D_MODEL, D_MODEL))
        y_ref[...] = y
        x1 = x_ref[...] + (y * _rms(y)) * g2_ref[...]
        x1_ref[...] = x1
        h2_ref[...] = ((x1 * _rms(x1)) * g3_ref[...]).astype(BF16)

    row = lambda i: (i, 0)
    blk = pl.BlockSpec((tm, D_MODEL), row)
    return pl.pallas_call(
        body, name="mix_out_fwd", grid=(t // tm,),
        in_specs=[pl.BlockSpec((tm, WIDTH_A), row), _ot_spec(tm, V_DIM_B), pl.BlockSpec((tm, D_MODEL), lambda i: (i, 0)),
                  pl.BlockSpec((tm, D_MODEL), lambda i: (i, 1)), blk,
                  _full((WIDTH_A, D_MODEL)), _full((N_HEADS_B * V_DIM_B, D_MODEL)), _wb_spec("w_out"),
                  _full((1, D_MODEL)), _full((1, D_MODEL))],
        out_specs=[blk, blk, blk, blk],
        out_shape=[jax.ShapeDtypeStruct((t, D_MODEL), BF16), jax.ShapeDtypeStruct((t, D_MODEL), F32),
                   jax.ShapeDtypeStruct((t, D_MODEL), F32), jax.ShapeDtypeStruct((t, D_MODEL), BF16)],
        compiler_params=_params(("parallel",)),
    )(out_a, out_bt, proj, proj, x, w_oa, w_ob, wb, g2, g3)


_TM_MLP = 512


def _up_fwd(h2, wb):
    t = h2.shape[0]
    tm = _TM_MLP

    def body(h_ref, w_ref, a_ref):
        hv = h_ref[...]
        for j in range(N_CHIPS):
            u = _dot(hv, w_ref[j])
            a_ref[:, D_MODEL * j:D_MODEL * (j + 1)] = jnp.square(jnp.maximum(u, 0.0)).astype(BF16)

    return pl.pallas_call(
        body, name="up_fwd", grid=(t // tm,),
        in_specs=[pl.BlockSpec((tm, D_MODEL), lambda i: (i, 0)), _wb_spec("w_up")],
        out_specs=pl.BlockSpec((tm, D_FF), lambda i: (i, 0)),
        out_shape=jax.ShapeDtypeStruct((t, D_FF), BF16),
        compiler_params=_params(("parallel",)),
    )(h2, wb)


def _down_fwd_loss(a, wb, x1, target, g4):
    t = a.shape[0]
    tm = _TM_MLP

    def body(a_ref, w_ref, x1_ref, tg_ref, g_ref, dx2_ref, dyd_ref, dg_ref, loss_ref):
        @pl.when(pl.program_id(0) == 0)
        def _():
            dg_ref[...] = jnp.zeros(dg_ref.shape, F32)
            loss_ref[...] = jnp.zeros(loss_ref.shape, F32)

        yd = _dot(a_ref[...], w_ref[...].reshape(D_FF, D_MODEL))
        r = _rms(yd)
        n = yd * r
        diff = (x1_ref[...] + n * g_ref[...]) - tg_ref[...]
        loss_ref[...] += 0.5 * jnp.sum(jnp.mean(diff * diff, axis=-1, keepdims=True), axis=0, keepdims=True)
        dx2 = diff * (1.0 / D_MODEL)
        dx2_ref[...] = dx2
        dyd, dg = _norm_bwd(dx2, n, r, g_ref[...])
        dyd_ref[...] = dyd.astype(BF16)
        dg_ref[...] += dg

    row = lambda i: (i, 0)
    blk = pl.BlockSpec((tm, D_MODEL), row)
    return pl.pallas_call(
        body, name="down_fwd_loss", grid=(t // tm,),
        in_specs=[pl.BlockSpec((tm, D_FF), row), _wb_spec("w_down"), blk, blk, _full((1, D_MODEL))],
        out_specs=[blk, blk, _full((1, D_MODEL)), _full((1, LANES))],
        out_shape=[jax.ShapeDtypeStruct((t, D_MODEL), F32), jax.ShapeDtypeStruct((t, D_MODEL), BF16),
                   jax.ShapeDtypeStruct((1, D_MODEL), F32), jax.ShapeDtypeStruct((1, LANES), F32)],
        compiler_params=_params(("arbitrary",)),
    )(a, wb, x1, target, g4)


def _matmul_tn(a, b, name, tm, tn, tk=1024, shard_major=False):
    t, m = a.shape
    n = b.shape[1]
    tk = min(tk, t)
    nk = t // tk

    def body(a_ref, b_ref, o_ref):
        @pl.when(pl.program_id(2) == 0)
        def _():
            o_ref[...] = jnp.zeros(o_ref.shape, F32)

        acc = _dot_tn(a_ref[...].astype(BF16), b_ref[...].astype(BF16))
        o_ref[...] += acc[None] if shard_major else acc

    if shard_major:
        out_spec = pl.BlockSpec((1, tm, tn), lambda i, j, k: (j, i, 0))
        out_shape = jax.ShapeDtypeStruct((n // tn, m, tn), F32)
    else:
        out_spec = pl.BlockSpec((tm, tn), lambda i, j, k: (i, j))
        out_shape = jax.ShapeDtypeStruct((m, n), F32)
    return pl.pallas_call(
        body, name=name, grid=(m // tm, n // tn, nk),
        in_specs=[pl.BlockSpec((tk, tm), lambda i, j, k: (k, i)), pl.BlockSpec((tk, tn), lambda i, j, k: (k, j))],
        out_specs=out_spec, out_shape=out_shape,
        compiler_params=_params(("parallel", "parallel", "arbitrary")),
    )(a, b)


def _down_bwd(dyd, wb, a):
    t = dyd.shape[0]
    tm = _TM_MLP

    def body(d_ref, w_ref, a_ref, du_ref):
        da = _dot_nt(d_ref[...], w_ref[...].reshape(D_FF, D_MODEL))
        du_ref[...] = (da * (2.0 * jnp.sqrt(a_ref[...].astype(F32)))).astype(BF16)

    row = lambda i: (i, 0)
    return pl.pallas_call(
        body, name="down_bwd", grid=(t // tm,),
        in_specs=[pl.BlockSpec((tm, D_MODEL), row), _wb_spec("w_down"), pl.BlockSpec((tm, D_FF), row)],
        out_specs=pl.BlockSpec((tm, D_FF), row),
        out_shape=jax.ShapeDtypeStruct((t, D_FF), BF16),
        compiler_params=_params(("parallel",)),
    )(dyd, wb, a)


def _up_bwd(du, wb, x1, dx2, y, g3, g2):
    t = du.shape[0]
    tm = _TM_MLP

    def body(du_ref, w_ref, x1_ref, dx2_ref, y_ref, g3_ref, g2_ref, dx1_ref, dy_ref, dg3_ref, dg2_ref):
        @pl.when(pl.program_id(0) == 0)
        def _():
            dg3_ref[...] = jnp.zeros(dg3_ref.shape, F32)
            dg2_ref[...] = jnp.zeros(dg2_ref.shape, F32)

        dh2 = _dot_nt(du_ref[:, 0:D_MODEL], w_ref[0])
        for j in range(1, N_CHIPS):
            dh2 = dh2 + _dot_nt(du_ref[:, D_MODEL * j:D_MODEL * (j + 1)], w_ref[j])
        x1 = x1_ref[...]
        r3 = _rms(x1)
        d3, dg3 = _norm_bwd(dh2, x1 * r3, r3, g3_ref[...])
        dx1 = dx2_ref[...] + d3
        dx1_ref[...] = dx1
        dg3_ref[...] += dg3
        y = y_ref[...]
        r2 = _rms(y)
        dy, dg2 = _norm_bwd(dx1, y * r2, r2, g2_ref[...])
        dy_ref[...] = dy.astype(BF16)
        dg2_ref[...] += dg2

    row = lambda i: (i, 0)
    blk = pl.BlockSpec((tm, D_MODEL), row)
    return pl.pallas_call(
        body, name="up_bwd", grid=(t // tm,),
        in_specs=[pl.BlockSpec((tm, D_FF), row), _wb_spec("w_up"),
                  blk, blk, blk, _full((1, D_MODEL)), _full((1, D_MODEL))],
        out_specs=[blk, blk, _full((1, D_MODEL)), _full((1, D_MODEL))],
        out_shape=[jax.ShapeDtypeStruct((t, D_MODEL), F32), jax.ShapeDtypeStruct((t, D_MODEL), BF16),
                   jax.ShapeDtypeStruct((1, D_MODEL), F32), jax.ShapeDtypeStruct((1, D_MODEL), F32)],
        compiler_params=_params(("arbitrary",)),
    )(du, wb, x1, dx2, y, g3, g2)


def _mix_out_bwd(dy, out_a, out_bt, proj, w_oa, w_ob, wb):
    t = dy.shape[0]
    tm = 256
    nb = t // _TQ

    def body(dy_ref, oa_ref, obt_ref, ga_ref, gb_ref, woa_ref, wob_ref, wout_ref,
             doa_ref, dob_ref, dga_ref, dgb_ref, da_ref, db_ref, dbt_ref, dela_ref, delb_ref):
        dm = _dot_nt(dy_ref[...], wout_ref[...].reshape(D_MODEL, D_MODEL))
        out_a_v = oa_ref[...]
        out_bt_v = obt_ref[...].reshape(N_HEADS_B * V_DIM_B, tm)
        oa = _dot(out_a_v.astype(BF16), woa_ref[...])
        ob = _dot_tn(out_bt_v.astype(BF16), wob_ref[...])
        sa, sb = jax.nn.sigmoid(ga_ref[...]), jax.nn.sigmoid(gb_ref[...])
        doa = (dm * sa).astype(BF16)
        dob = (dm * sb).astype(BF16)
        doa_ref[...] = doa
        dob_ref[...] = dob
        dga_ref[...] = (dm * oa * (sa * (1.0 - sa))).astype(BF16)
        dgb_ref[...] = (dm * ob * (sb * (1.0 - sb))).astype(BF16)
        d_out_a = _dot_nt(doa, woa_ref[...])
        da_ref[...] = d_out_a
        prod_at = (d_out_a * out_a_v).T
        dela_ref[...] = jnp.concatenate(
            [jnp.sum(_head_rows(prod_at, h), axis=0, keepdims=True) for h in range(N_HEADS_A)], axis=0)
        d_out_b = _dot_nt(dob, wob_ref[...])
        d_out_bt = _dot_nt(wob_ref[...], dob)
        prod_bt = d_out_bt * out_bt_v
        for h in range(N_HEADS_B):
            db_ref[h] = d_out_b[:, V_DIM_B * h:V_DIM_B * (h + 1)].astype(BF16)
            dbt_ref[h, 0] = d_out_bt[V_DIM_B * h:V_DIM_B * (h + 1), :].astype(BF16)
            delb_ref[h, 0] = jnp.sum(prod_bt[V_DIM_B * h:V_DIM_B * (h + 1), :], axis=0, keepdims=True)

    row = lambda i: (i, 0)
    blk = pl.BlockSpec((tm, D_MODEL), row)
    return pl.pallas_call(
        body, name="mix_out_bwd", grid=(t // tm,),
        in_specs=[blk, pl.BlockSpec((tm, WIDTH_A), row), _ot_spec(tm, V_DIM_B), pl.BlockSpec((tm, D_MODEL), lambda i: (i, 0)),
                  pl.BlockSpec((tm, D_MODEL), lambda i: (i, 1)),
                  _full((WIDTH_A, D_MODEL)), _full((N_HEADS_B * V_DIM_B, D_MODEL)), _wb_spec("w_out")],
        out_specs=[blk, blk, blk, blk, pl.BlockSpec((tm, WIDTH_A), row),
                   pl.BlockSpec((N_HEADS_B, tm, V_DIM_B), lambda i: (0, i, 0)), _ot_spec(tm, V_DIM_B),
                   pl.BlockSpec((N_HEADS_A, tm), lambda i: (0, i)), _ot_spec(tm, 1)],
        out_shape=[jax.ShapeDtypeStruct((t, D_MODEL), BF16)] * 4
        + [jax.ShapeDtypeStruct((t, WIDTH_A), F32), jax.ShapeDtypeStruct((N_HEADS_B, t, V_DIM_B), BF16),
           jax.ShapeDtypeStruct((N_HEADS_B, nb, V_DIM_B, _TQ), BF16),
           jax.ShapeDtypeStruct((N_HEADS_A, t), F32), jax.ShapeDtypeStruct((N_HEADS_B, nb, 1, _TQ), F32)],
        compiler_params=_params(("parallel",)),
    )(dy, out_a, out_bt, proj, proj, w_oa, w_ob, wb)


def _dw_ob(out_bt, dob):
    t = dob.shape[0]
    nb = t // _TQ

    def body(obt_ref, dob_ref, o_ref):
        @pl.when(pl.program_id(0) == 0)
        def _():
            o_ref[...] = jnp.zeros(o_ref.shape, F32)

        obt = obt_ref[...].reshape(N_HEADS_B * V_DIM_B, _TQ).astype(BF16)
        o_ref[...] += _dot(obt, dob_ref[...])

    return pl.pallas_call(
        body, name="dw_o_b", grid=(nb,),
        in_specs=[pl.BlockSpec((N_HEADS_B, 1, V_DIM_B, _TQ), lambda i: (0, i, 0, 0)),
                  pl.BlockSpec((_TQ, D_MODEL), lambda i: (i, 0))],
        out_specs=_full((N_HEADS_B * V_DIM_B, D_MODEL)),
        out_shape=jax.ShapeDtypeStruct((N_HEADS_B * V_DIM_B, D_MODEL), F32),
        compiler_params=_params(("arbitrary",)),
    )(out_bt, dob)


def _mla_bwd(q, k, qt, kt, v, d_out, d_out_t, lse, delta, gp):
    t = q.shape[0]
    nb = t // _TQ

    def body(k_ref, kt_ref, v_ref, q_ref, qt_ref, do_ref, dot_ref, l_ref, d_ref, gp_ref,
             dqt_ref, dk_ref, dv_ref, land_ref, send_sems, recv_sems):
        kj = pl.program_id(1)

        @pl.when((pl.program_id(0) == 0) & (kj == 0))
        def _():
            _scatter_start(gp_ref, land_ref, send_sems, recv_sems)

        @pl.when(kj == 0)
        def _():
            dqt_ref[...] = jnp.zeros(dqt_ref.shape, F32)

        kv, k_t, vv = k_ref[...], kt_ref[0, 0], v_ref[0]

        def products(qi, diagonal=False):
            return _scores_t(kv, qt_ref[0, qi], diagonal), _dot(vv, dot_ref[0, qi])

        def update(carry, prods, qi):
            dk, dv = carry
            st, dpt = prods
            rows = pl.ds(pl.multiple_of(qi * _TQ, _TQ), _TQ)
            pt = jnp.exp2(st - l_ref[0, qi])
            dv = dv + _dot(pt.astype(BF16), do_ref[0, rows, :])
            dst = (pt * (dpt - d_ref[0, qi]) * _MLA_SCALE).astype(BF16)
            dk = dk + _dot(dst, q_ref[rows, :])
            dqt_ref[0, qi] += _dot(k_t, dst)
            return dk, dv

        def pair(i, carry):
            qa = kj + 1 + 2 * i
            pa, pb = products(qa), products(qa + 1)
            return update(update(carry, pa, qa), pb, qa + 1)

        init = (jnp.zeros((_TQ, HEAD_PAD), F32), jnp.zeros((_TQ, V_DIM_B), F32))
        carry = update(init, products(kj, True), kj)
        pairs = (nb - 1 - kj) // 2
        carry = lax.fori_loop(0, pairs, pair, carry)
        dk, dv = lax.fori_loop(kj + 1 + 2 * pairs, nb, lambda qi, cr: update(cr, products(qi), qi), carry)
        dk_ref[...] = dk
        dv_ref[0] = dv

        @pl.when((pl.program_id(0) == N_HEADS_B - 1) & (kj == nb - 1))
        def _():
            _scatter_wait(gp_ref, land_ref, send_sems, recv_sems)

    head4 = lambda d: pl.BlockSpec((1, nb, d, _TQ), lambda h, kj: (h, 0, 0, 0))
    return pl.pallas_call(
        body, name="mla_bwd", grid=(N_HEADS_B, nb),
        in_specs=[pl.BlockSpec((_TQ, HEAD_PAD), lambda h, kj: (kj, h)),
                  pl.BlockSpec((1, 1, HEAD_PAD, _TQ), lambda h, kj: (h, kj, 0, 0)),
                  pl.BlockSpec((1, _TQ, V_DIM_B), lambda h, kj: (h, kj, 0)),
                  pl.BlockSpec((t, HEAD_PAD), lambda h, kj: (0, h)), head4(HEAD_PAD),
                  pl.BlockSpec((1, t, V_DIM_B), lambda h, kj: (h, 0, 0)), head4(V_DIM_B), head4(1), head4(1), _HBM],
        out_specs=[head4(HEAD_PAD), pl.BlockSpec((_TQ, HEAD_PAD), lambda h, kj: (kj, h)),
                   pl.BlockSpec((1, _TQ, V_DIM_B), lambda h, kj: (h, kj, 0)), _HBM],
        out_shape=[jax.ShapeDtypeStruct((N_HEADS_B, nb, HEAD_PAD, _TQ), F32), jax.ShapeDtypeStruct((t, MLA_W), F32),
                   jax.ShapeDtypeStruct((N_HEADS_B, t, V_DIM_B), F32),
                   jax.ShapeDtypeStruct((3,) + gp.shape[1:], gp.dtype)],
        scratch_shapes=[pltpu.SemaphoreType.DMA((3,)), pltpu.SemaphoreType.DMA((3,))],
        compiler_params=_params(("arbitrary", "arbitrary")),
    )(k, kt, v, q, qt, d_out, d_out_t, lse, delta, gp)


def _mla_prep_bwd(dqt, dk, dv, proj, posc, freq, qan, kvan, wq, wk, wv):
    t = dk.shape[0]
    tm = _TQ

    def body(dqt_ref, dk_ref, dv_ref, cq_ref, ckv_ref, pos_ref, f_ref, qan_ref, kvan_ref, wq_ref, wk_ref, wv_ref,
             dcq_ref, dckv_ref, dkr_ref, dwq_ref, dwk_ref, dwv_ref, dqan_ref, dkvan_ref):
        @pl.when(pl.program_id(0) == 0)
        def _():
            for r in (dwq_ref, dwk_ref, dwv_ref, dqan_ref, dkvan_ref):
                r[...] = jnp.zeros(r.shape, F32)

        cq = cq_ref[...]
        rq = _rms(cq)
        nq_ = cq * rq
        cqn = (nq_ * qan_ref[...]).astype(BF16)
        ckv = ckv_ref[...]
        rkv = _rms(ckv)
        nkv = ckv * rkv
        ckvn = (nkv * kvan_ref[...]).astype(BF16)
        c, s, lo, hi = _rope_coeffs(pos_ref[...], f_ref[...])
        dkv = dk_ref[...]
        dkr = jnp.zeros((tm, LANES), F32)
        dqb = []
        for h in range(N_HEADS_B):
            dqb.append(_unrope(dqt_ref[h, 0].T, c, s, lo, hi).astype(BF16))
            dkr = dkr + dkv[:, HEAD_PAD * h:HEAD_PAD * (h + 1)]
        dqb = jnp.concatenate(dqb, axis=1)
        dkr_ref[...] = jnp.where(lo | hi, _unrope(dkr, c, s, lo, hi), 0.0).astype(BF16)
        dkb = dkv.astype(BF16)
        dvb = jnp.concatenate([dv_ref[h] for h in range(N_HEADS_B)], axis=1).astype(BF16)
        dwq_ref[...] += _dot_tn(cqn, dqb)
        dwk_ref[...] += _dot_tn(ckvn, dkb)
        dwv_ref[...] += _dot_tn(ckvn, dvb)
        dcqn = _dot_nt(dqb, wq_ref[...])
        dckvn = _dot_nt(dkb, wk_ref[...]) + _dot_nt(dvb, wv_ref[...])
        dcq, dqan = _norm_bwd(dcqn, nq_, rq, qan_ref[...])
        dckv, dkvan = _norm_bwd(dckvn, nkv, rkv, kvan_ref[...])
        dcq_ref[...] = dcq.astype(BF16)
        dckv_ref[...] = dckv.astype(BF16)
        dqan_ref[...] += dqan
        dkvan_ref[...] += dkvan

    row = lambda i: (i, 0)
    vw = N_HEADS_B * V_DIM_B
    return pl.pallas_call(
        body, name="mla_prep_bwd", grid=(t // tm,),
        in_specs=[pl.BlockSpec((N_HEADS_B, 1, HEAD_PAD, tm), lambda i: (0, i, 0, 0)), pl.BlockSpec((tm, MLA_W), row),
                  pl.BlockSpec((N_HEADS_B, tm, V_DIM_B), lambda i: (0, i, 0)),
                  pl.BlockSpec((tm, Q_LORA), lambda i: (i, _CQ_BLK)),
                  pl.BlockSpec((tm, LANES), lambda i: (i, _CKV_BLK)),
                  pl.BlockSpec((tm, 1), row), _full((1, LANES)), _full((1, Q_LORA)), _full((1, KV_LORA)),
                  _full((Q_LORA, MLA_W)), _full((KV_LORA, MLA_W)), _full((KV_LORA, vw))],
        out_specs=[pl.BlockSpec((tm, Q_LORA), row), pl.BlockSpec((tm, LANES), row), pl.BlockSpec((tm, LANES), row),
                   _full((Q_LORA, MLA_W)), _full((KV_LORA, MLA_W)), _full((KV_LORA, vw)),
                   _full((1, Q_LORA)), _full((1, KV_LORA))],
        out_shape=[jax.ShapeDtypeStruct((t, Q_LORA), BF16), jax.ShapeDtypeStruct((t, LANES), BF16),
                   jax.ShapeDtypeStruct((t, LANES), BF16),
                   jax.ShapeDtypeStruct((Q_LORA, MLA_W), F32), jax.ShapeDtypeStruct((KV_LORA, MLA_W), F32),
                   jax.ShapeDtypeStruct((KV_LORA, vw), F32),
                   jax.ShapeDtypeStruct((1, Q_LORA), F32), jax.ShapeDtypeStruct((1, KV_LORA), F32)],
        compiler_params=_params(("arbitrary",)),
    )(dqt, dk, dv, proj, proj, posc, freq, qan, kvan, wq, wk, wv)


def _swa_bwd(proj, d_out, lse, delta, posc, posr, sinks):
    t = proj.shape[0]
    nb = t // BLOCK

    def body(q_ref, kc_ref, kp_ref, vc_ref, vp_ref, do_ref, l_ref, d_ref, pq_ref, pc_ref, pp_ref, sink_ref,
             dq_ref, dk_ref, dv_ref, ds_ref, dkb_s, dvb_s, dk_carry, dv_carry):
        n = pl.program_id(0)

        @pl.when(n == 0)
        def _():
            ds_ref[...] = jnp.zeros(ds_ref.shape, F32)
            dk_carry[...] = jnp.zeros(dk_carry.shape, F32)
            dv_carry[...] = jnp.zeros(dv_carry.shape, F32)

        @pl.when(n < nb)
        def _():
            kb, vb, dist, valid = _swa_band(n, kp_ref, kc_ref, vp_ref, vc_ref, pq_ref, pp_ref, pc_ref)
            qv, dov = q_ref[...], do_ref[...]
            q_t, do_t, kb_t = qv.T, dov.T, kb.T
            lane = lax.broadcasted_iota(jnp.int32, (1, LANES), 1)
            dsink = jnp.zeros((1, LANES), F32)
            dq_t = []
            for kh in range(N_KV_A):
                heads = range(_GROUP_A * kh, _GROUP_A * (kh + 1))
                st_g = _dot(_head_cols(kb, kh).astype(BF16), _group_t(q_t, kh))
                dpt_g = _dot(_head_cols(vb, kh).astype(BF16), _group_t(do_t, kh))
                pts, dsts = [], []
                for j, h in enumerate(heads):
                    st = _swa_scores_t(st_g, j, h, dist, valid)
                    l_h, d_h = l_ref[h:h + 1, :], d_ref[h:h + 1, :]
                    pt = jnp.exp(st - l_h)
                    p_sink = jnp.exp(sink_ref[0:1, h:h + 1] - l_h)
                    dsink = jnp.where(lane == h, jnp.sum(-p_sink * d_h, axis=1, keepdims=True), dsink)
                    dst = pt * (dpt_g[:, BLOCK * j:BLOCK * (j + 1)] - d_h) * _SWA_SCALE
                    pts.append(pt.astype(BF16))
                    dsts.append(dst.astype(BF16))
                pt_g, dst_g = jnp.concatenate(pts, axis=1), jnp.concatenate(dsts, axis=1)
                q_g = jnp.concatenate([_head_cols(qv, h) for h in heads], axis=0).astype(BF16)
                do_g = jnp.concatenate([_head_cols(dov, h) for h in heads], axis=0).astype(BF16)
                dkb_s[:, HEAD_DIM_A * kh:HEAD_DIM_A * (kh + 1)] = _dot(dst_g, q_g)
                dvb_s[:, HEAD_DIM_A * kh:HEAD_DIM_A * (kh + 1)] = _dot(pt_g, do_g)
                dq_g = _dot(_head_rows(kb_t, kh).astype(BF16), dst_g)
                dq_t.extend(dq_g[:, BLOCK * j:BLOCK * (j + 1)] for j in range(_GROUP_A))
            dq_ref[...] = jnp.concatenate(dq_t, axis=0).T
            ds_ref[...] += dsink
            dk_ref[...] = dk_carry[...] + dkb_s[0:BLOCK, :]
            dv_ref[...] = dv_carry[...] + dvb_s[0:BLOCK, :]
            dk_carry[...] = dkb_s[BLOCK:2 * BLOCK, :]
            dv_carry[...] = dvb_s[BLOCK:2 * BLOCK, :]

        @pl.when(n == nb)
        def _():
            dk_ref[...] = dk_carry[...]
            dv_ref[...] = dv_carry[...]

    cur = lambda n: (jnp.minimum(n, nb - 1), 0)
    cur_t = lambda n: (0, jnp.minimum(n, nb - 1))
    prv = lambda n: jnp.maximum(jnp.minimum(n, nb - 1) - 1, 0)
    out_prev = lambda n: (jnp.maximum(n - 1, 0), 0)
    return pl.pallas_call(
        body, name="swa_bwd", grid=(nb + 1,),
        in_specs=[pl.BlockSpec((BLOCK, WIDTH_A), lambda n: (jnp.minimum(n, nb - 1), _QA_BLK)),
                  pl.BlockSpec((BLOCK, LANES), lambda n: (jnp.minimum(n, nb - 1), _KA_BLK)),
                  pl.BlockSpec((BLOCK, LANES), lambda n: (prv(n), _KA_BLK)),
                  pl.BlockSpec((BLOCK, LANES), lambda n: (jnp.minimum(n, nb - 1), _VA_BLK)),
                  pl.BlockSpec((BLOCK, LANES), lambda n: (prv(n), _VA_BLK)),
                  pl.BlockSpec((BLOCK, WIDTH_A), cur), pl.BlockSpec((N_HEADS_A, BLOCK), cur_t),
                  pl.BlockSpec((N_HEADS_A, BLOCK), cur_t), pl.BlockSpec((1, BLOCK), cur_t),
                  pl.BlockSpec((BLOCK, 1), cur), pl.BlockSpec((BLOCK, 1), lambda n: (prv(n), 0)),
                  _full((1, N_HEADS_A))],
        out_specs=[pl.BlockSpec((BLOCK, WIDTH_A), cur), pl.BlockSpec((BLOCK, LANES), out_prev),
                   pl.BlockSpec((BLOCK, LANES), out_prev), _full((1, LANES))],
        out_shape=[jax.ShapeDtypeStruct((t, WIDTH_A), F32), jax.ShapeDtypeStruct((t, LANES), F32),
                   jax.ShapeDtypeStruct((t, LANES), F32), jax.ShapeDtypeStruct((1, LANES), F32)],
        scratch_shapes=[pltpu.VMEM((2 * BLOCK, LANES), F32), pltpu.VMEM((2 * BLOCK, LANES), F32),
                        pltpu.VMEM((BLOCK, LANES), F32), pltpu.VMEM((BLOCK, LANES), F32)],
        compiler_params=_params(("arbitrary",)),
    )(proj, proj, proj, proj, proj, d_out, lse, delta, posr, posc, posc, sinks)


def _in_bwd(dproj, w_in_t, x, dx1, g1, gp):
    t = x.shape[0]
    tm = 256
    steps = t // tm

    def body(dp_ref, w_ref, x_ref, dx1_ref, g_ref, gp_ref, dx_ref, dg_ref, land_ref, send_sems, recv_sems):
        i = pl.program_id(0)

        @pl.when(i == 0)
        def _():
            dg_ref[...] = jnp.zeros(dg_ref.shape, F32)
            _scatter_start(gp_ref, land_ref, send_sems, recv_sems)

        dh = _dot(dp_ref[...], w_ref[...])
        xv = x_ref[...]
        r = _rms(xv)
        dx, dg = _norm_bwd(dh, xv * r, r, g_ref[...])
        dx_ref[...] = dx1_ref[...] + dx
        dg_ref[...] += dg

        @pl.when(i == steps - 1)
        def _():
            _scatter_wait(gp_ref, land_ref, send_sems, recv_sems)

    row = lambda i: (i, 0)
    blk = pl.BlockSpec((tm, D_MODEL), row)
    return pl.pallas_call(
        body, name="in_bwd", grid=(steps,),
        in_specs=[pl.BlockSpec((tm, D_IN_PAD), row), _full((D_IN_PAD, D_MODEL)), blk, blk, _full((1, D_MODEL)), _HBM],
        out_specs=[blk, _full((1, D_MODEL)), _HBM],
        out_shape=[jax.ShapeDtypeStruct((t, D_MODEL), F32), jax.ShapeDtypeStruct((1, D_MODEL), F32),
                   jax.ShapeDtypeStruct((3,) + gp.shape[1:], gp.dtype)],
        scratch_shapes=[pltpu.SemaphoreType.DMA((3,)), pltpu.SemaphoreType.DMA((3,))],
        compiler_params=_params(("arbitrary",)),
    )(dproj, w_in_t, x, dx1, g1, gp)


def _adamw_store(w, g, m, v, out_refs):
    g_out, d_out, m_out, v_out = out_refs
    m_new = ADAM_B1 * m + (1.0 - ADAM_B1) * g
    v_new = ADAM_B2 * v + (1.0 - ADAM_B2) * jnp.square(g)
    m_hat = m_new / (1.0 - ADAM_B1 ** ADAM_STEP)
    v_hat = v_new / (1.0 - ADAM_B2 ** ADAM_STEP)
    g_out[...] = g
    d_out[...] = -ADAM_LR * (m_hat / (jnp.sqrt(v_hat) + ADAM_EPS) + ADAM_WD * w)
    m_out[...] = m_new
    v_out[...] = v_new


_SMALL_SLOTS = {"pre_norm_mix": (0, 0, D_MODEL), "post_norm_mix": (1, 0, D_MODEL), "pre_norm_mlp": (2, 0, D_MODEL),
                "post_norm_mlp": (3, 0, D_MODEL), "q_a_norm": (4, 0, Q_LORA), "kv_a_norm": (4, Q_LORA, KV_LORA),
                "sinks": (4, Q_LORA + KV_LORA, N_HEADS_A)}
_LOSS_ROW = 5


def _adamw_small(red, w, m, v):
    names = tuple(_SMALL_SLOTS)
    n = len(names)

    def body(*refs):
        red_ref, ws, ms, vs, outs = refs[0], refs[1:1 + n], refs[1 + n:1 + 2 * n], refs[1 + 2 * n:1 + 3 * n], refs[1 + 3 * n:]
        for k, name in enumerate(names):
            row, lane, width = _SMALL_SLOTS[name]
            g = red_ref[row:row + 1, lane:lane + width]
            _adamw_store(ws[k][...], g, ms[k][...], vs[k][...], outs[4 * k:4 * k + 4])

    vmem = pl.BlockSpec(memory_space=pltpu.VMEM)
    res = pl.pallas_call(
        body, name="adamw_small", in_specs=[vmem] * (1 + 3 * n), out_specs=[vmem] * (4 * n),
        out_shape=[jax.ShapeDtypeStruct(w[name].shape, F32) for name in names for _ in range(4)],
    )(red, *[w[k] for k in names], *[m[k] for k in names], *[v[k] for k in names])
    return {name: res[4 * k:4 * k + 4] for k, name in enumerate(names)}


def _adamw(w, g_parts, m, v, name, block, g_row_off=0):
    r, c = w.shape
    br, bc = block
    ng = len(g_parts)

    def body(*refs):
        w_ref, g_refs, m_ref, v_ref = refs[0], refs[1:1 + ng], refs[1 + ng], refs[2 + ng]
        g = g_refs[0][...]
        for gr in g_refs[1:]:
            g = g + gr[...]
        _adamw_store(w_ref[...], g, m_ref[...], v_ref[...], refs[3 + ng:])

    assert g_row_off % br == 0 and r % br == 0 and c % bc == 0
    blk = pl.BlockSpec(block, lambda i, j: (i, j))
    g_blk = pl.BlockSpec(block, lambda i, j: (i + g_row_off // br, j))
    return pl.pallas_call(
        body, name=name, grid=(r // br, c // bc),
        in_specs=[blk] + [g_blk] * ng + [blk, blk], out_specs=[blk] * 4,
        out_shape=[jax.ShapeDtypeStruct((r, c), F32)] * 4,
        compiler_params=_params(("parallel", "parallel")),
    )(w, *g_parts, m, v)


_HBM = pl.BlockSpec(memory_space=pltpu.HBM)


def _other_chips(x, y):
    return ((1 - x, y), (x, 1 - y), (1 - x, 1 - y))


def _gather_copies(src, out, send_sems, recv_sems, local_sem):
    x, y, c = lax.axis_index("x"), lax.axis_index("y"), lax.axis_index("c")
    me = 2 * x + y
    local = pltpu.make_async_copy(src, out.at[me], local_sem)

    def copies(arriving):
        return [pltpu.make_async_remote_copy(src_ref=src, dst_ref=out.at[2 * px + py if arriving else me],
                                             send_sem=send_sems.at[j], recv_sem=recv_sems.at[j], device_id=(px, py, c),
                                             device_id_type=MESH)
                for j, (px, py) in enumerate(_other_chips(x, y))]

    return local, copies


def _gather_start(src, out, send_sems, recv_sems, local_sem):
    local, copies = _gather_copies(src, out, send_sems, recv_sems, local_sem)
    local.start()
    for cp in copies(False):
        cp.start()


def _gather_wait(src, out, send_sems, recv_sems, local_sem):
    local, copies = _gather_copies(src, out, send_sems, recv_sems, local_sem)
    for cp in copies(True):
        cp.wait_recv()
    for cp in copies(False):
        cp.wait_send()
    local.wait()


def _scatter_copies(src, land, send_sems, recv_sems):
    x, y, c = lax.axis_index("x"), lax.axis_index("y"), lax.axis_index("c")
    return [pltpu.make_async_remote_copy(src_ref=src.at[2 * px + py], dst_ref=land.at[j], send_sem=send_sems.at[j],
                                         recv_sem=recv_sems.at[j], device_id=(px, py, c), device_id_type=MESH)
            for j, (px, py) in enumerate(_other_chips(x, y))]


def _scatter_start(src, land, send_sems, recv_sems):
    for cp in _scatter_copies(src, land, send_sems, recv_sems):
        cp.start()


def _scatter_wait(src, land, send_sems, recv_sems):
    copies = _scatter_copies(src, land, send_sems, recv_sems)
    for cp in copies:
        cp.wait_recv()
    for cp in copies:
        cp.wait_send()


def _all_gather_chips(packed):
    r = packed.shape[0]
    half = r // 2

    def body(src, out, ici_send, ici_recv, d2d_send, d2d_recv, local_sem):
        x, y, c = lax.axis_index("x"), lax.axis_index("y"), lax.axis_index("c")
        me = 2 * x + y
        mine = pl.ds(pl.multiple_of(c * half, 16), half)
        theirs = pl.ds(pl.multiple_of((1 - c) * half, 16), half)
        chips = _other_chips(x, y)
        local = pltpu.make_async_copy(src, out.at[me], local_sem)
        local.start()
        sends = [pltpu.make_async_remote_copy(src_ref=src.at[mine], dst_ref=out.at[me, mine], send_sem=ici_send.at[j],
                                              recv_sem=ici_recv.at[j], device_id=(px, py, c), device_id_type=MESH)
                 for j, (px, py) in enumerate(chips)]
        for cp in sends:
            cp.start()
        passed = []
        for j, (px, py) in enumerate(chips):
            block = 2 * px + py
            pltpu.make_async_remote_copy(src_ref=src.at[mine], dst_ref=out.at[block, mine], send_sem=ici_send.at[j],
                                         recv_sem=ici_recv.at[j], device_id=(px, py, c), device_id_type=MESH).wait_recv()
            cp = pltpu.make_async_remote_copy(src_ref=out.at[block, mine], dst_ref=out.at[block, mine],
                                              send_sem=d2d_send.at[j], recv_sem=d2d_recv.at[j],
                                              device_id=(x, y, 1 - c), device_id_type=MESH)
            cp.start()
            passed.append(cp)
        for j, (px, py) in enumerate(chips):
            block = 2 * px + py
            pltpu.make_async_remote_copy(src_ref=out.at[block, theirs], dst_ref=out.at[block, theirs],
                                         send_sem=d2d_send.at[j], recv_sem=d2d_recv.at[j],
                                         device_id=(x, y, 1 - c), device_id_type=MESH).wait_recv()
        for cp in sends + passed:
            cp.wait_send()
        local.wait()

    sems = pltpu.SemaphoreType.DMA((3,))
    return pl.pallas_call(
        body, name="ag_weights", in_specs=[_HBM], out_specs=_HBM,
        out_shape=jax.ShapeDtypeStruct((N_CHIPS,) + packed.shape, packed.dtype),
        scratch_shapes=[sems, sems, sems, sems, pltpu.SemaphoreType.DMA(())],
    )(packed)


def _sum4(gp, land, chip, name):
    _, r, w = gp.shape
    tr = 128

    def body(chip_ref, o_ref, l_ref, s_ref):
        s_ref[...] = ((o_ref[0] + l_ref[0].astype(F32)) + l_ref[1].astype(F32)) + l_ref[2].astype(F32)

    return pl.pallas_call(
        body, name=name,
        grid_spec=pltpu.PrefetchScalarGridSpec(
            num_scalar_prefetch=1, grid=(r // tr,),
            in_specs=[pl.BlockSpec((1, tr, w), lambda i, chip_ref: (chip_ref[0], i, 0)),
                      pl.BlockSpec((3, tr, w), lambda i, chip_ref: (0, i, 0))],
            out_specs=pl.BlockSpec((tr, w), lambda i, chip_ref: (i, 0))),
        out_shape=jax.ShapeDtypeStruct((r, w), F32),
        compiler_params=_params(("parallel",)),
    )(chip, gp, land)


def _swap_sibling(s, name):
    def body(src, got, send_sem, recv_sem):
        x, y, c = lax.axis_index("x"), lax.axis_index("y"), lax.axis_index("c")
        cp = pltpu.make_async_remote_copy(src_ref=src, dst_ref=got, send_sem=send_sem, recv_sem=recv_sem,
                                          device_id=(x, y, 1 - c), device_id_type=MESH)
        cp.start()
        cp.wait_recv()
        cp.wait_send()

    return pl.pallas_call(
        body, name=name, in_specs=[_HBM], out_specs=_HBM,
        out_shape=jax.ShapeDtypeStruct(s.shape, s.dtype),
        scratch_shapes=[pltpu.SemaphoreType.DMA(()), pltpu.SemaphoreType.DMA(())],
    )(s)


def _all_reduce_small(dsmall, loss):
    n_dev = 8
    names = tuple(_SMALL_SLOTS)
    shape = (8, D_MODEL)

    def body(*refs):
        parts, loss_ref = refs[:len(names)], refs[len(names)]
        out, src, gath, send_sems, recv_sems = refs[len(names) + 1:]
        x, y, c = lax.axis_index("x"), lax.axis_index("y"), lax.axis_index("c")
        me = 4 * x + 2 * y + c
        src[...] = jnp.zeros(shape, F32)
        for name, part in zip(names, parts):
            row, lane, _ = _SMALL_SLOTS[name]
            src[row:row + 1, lane:lane + part.shape[1]] = part[...]
        src[_LOSS_ROW:_LOSS_ROW + 1, 0:LANES] = loss_ref[...]
        gath[me] = src[...]
        peers = []
        for k in range(1, n_dev):
            px = 1 - x if (k >> 2) & 1 else x
            py = 1 - y if (k >> 1) & 1 else y
            pc = 1 - c if k & 1 else c
            peers.append((px, py, pc))
        sends = []
        for j, peer in enumerate(peers):
            cp = pltpu.make_async_remote_copy(src_ref=src, dst_ref=gath.at[me], send_sem=send_sems.at[j],
                                              recv_sem=recv_sems.at[j], device_id=peer, device_id_type=MESH)
            cp.start()
            sends.append(cp)
        for j, (px, py, pc) in enumerate(peers):
            pltpu.make_async_remote_copy(src_ref=src, dst_ref=gath.at[4 * px + 2 * py + pc], send_sem=send_sems.at[j],
                                         recv_sem=recv_sems.at[j], device_id=(px, py, pc), device_id_type=MESH).wait_recv()
        for cp in sends:
            cp.wait_send()
        acc = gath[0]
        for d in range(1, n_dev):
            acc = acc + gath[d]
        out[...] = acc

    vmem = pl.BlockSpec(memory_space=pltpu.VMEM)
    return pl.pallas_call(
        body, name="ar_small", in_specs=[vmem] * (len(names) + 1), out_specs=vmem,
        out_shape=jax.ShapeDtypeStruct(shape, F32),
        scratch_shapes=[pltpu.VMEM(shape, F32), pltpu.VMEM((n_dev,) + shape, F32),
                        pltpu.SemaphoreType.DMA((n_dev - 1,)), pltpu.SemaphoreType.DMA((n_dev - 1,))],
    )(*[dsmall[k] for k in names], loss)


_W_IN_ROWS = SHARD_SHAPES["w_in"][1]
_KR_ROW = 3200
_KR_PAD_ROW = _KR_BLK * LANES + QK_NOPE


def _shard_rows(name, a):
    return jnp.transpose(a) if name == "w_in" else a.reshape(PACK_ROWS[name], D_MODEL)


def _pack(group, shards, dtype):
    parts = [_shard_rows(n, shards[n]).astype(dtype) for n in group]
    pad = -sum(PACK_ROWS[n] for n in group) % LANES
    if pad:
        parts.append(jnp.zeros((pad, D_MODEL), dtype))
    return jnp.concatenate(parts, axis=0)


def _col_sharded_full(g, name, group):
    r, c = SHARD_SHAPES[name]
    off = _row_offset(group, name)
    blocks = g[:, off:off + PACK_ROWS[name]].reshape(N_CHIPS, r, c)
    return jnp.transpose(blocks, (1, 0, 2)).reshape(r, N_CHIPS * c)


def _col_sharded_blocks(d, name):
    r, c = SHARD_SHAPES[name]
    return jnp.transpose(d.reshape(r, N_CHIPS, c), (1, 0, 2)).reshape(N_CHIPS, PACK_ROWS[name], D_MODEL)


def _weights_a(g):
    dt = g.dtype
    w_in_t = g[:, :_W_IN_ROWS].reshape(N_CHIPS * _W_IN_ROWS, D_MODEL)
    z = lambda n: jnp.zeros((n, D_MODEL), dt)
    w_in_t = jnp.concatenate([w_in_t[:_KR_ROW], z(_KR_PAD_ROW - _KR_ROW), w_in_t[_KR_ROW:],
                              z(D_IN_PAD - _KR_PAD_ROW - QK_ROPE)], axis=0)
    wq = _col_sharded_full(g, "w_q_b", GROUP_A).reshape(Q_LORA, N_HEADS_B, Q_HEAD_B)
    wq_p = jnp.concatenate([wq, jnp.zeros((Q_LORA, N_HEADS_B, HEAD_PAD - Q_HEAD_B), dt)], axis=2).reshape(Q_LORA, MLA_W)
    wkv = _col_sharded_full(g, "w_kv_b", GROUP_A).reshape(KV_LORA, N_HEADS_B, QK_NOPE + V_DIM_B)
    zk = jnp.zeros((KV_LORA, N_HEADS_B, HEAD_PAD - QK_NOPE), dt)
    wk_p = jnp.concatenate([wkv[:, :, :QK_NOPE], zk], axis=2).reshape(KV_LORA, MLA_W)
    wv = wkv[:, :, QK_NOPE:].reshape(KV_LORA, N_HEADS_B * V_DIM_B)
    return dict(w_in=w_in_t, wq=wq_p, wk=wk_p, wv=wv)


def _grad_blocks_a(dw_in_t, dwq_p, dwk_p, dwv):
    dw_in = jnp.concatenate([dw_in_t[:_KR_ROW], dw_in_t[_KR_PAD_ROW:_KR_PAD_ROW + QK_ROPE]], axis=0)
    dwq = dwq_p.reshape(Q_LORA, N_HEADS_B, HEAD_PAD)[:, :, :Q_HEAD_B].reshape(Q_LORA, N_HEADS_B * Q_HEAD_B)
    dwk = dwk_p.reshape(KV_LORA, N_HEADS_B, HEAD_PAD)[:, :, :QK_NOPE]
    dwkv = jnp.concatenate([dwk, dwv.reshape(KV_LORA, N_HEADS_B, V_DIM_B)], axis=2)
    dwkv = dwkv.reshape(KV_LORA, N_HEADS_B * (QK_NOPE + V_DIM_B))
    pad = -sum(PACK_ROWS[n] for n in GROUP_A) % LANES
    return jnp.concatenate([dw_in.reshape(N_CHIPS, _W_IN_ROWS, D_MODEL), _col_sharded_blocks(dwq, "w_q_b"),
                            _col_sharded_blocks(dwkv, "w_kv_b"), jnp.zeros((N_CHIPS, pad, D_MODEL), F32)], axis=1)


def _rope_freq_lanes():
    freqs = ROPE_THETA ** (-jnp.arange(0, QK_ROPE, 2, dtype=F32) / QK_ROPE)
    return jnp.concatenate([jnp.zeros((QK_NOPE,), F32), freqs, freqs,
                            jnp.zeros((HEAD_PAD - Q_HEAD_B,), F32)]).reshape(1, LANES)


def _fwd_bwd(x, positions, target, w):
    t = x.shape[0]
    wa = _weights_a(_all_gather_chips(_pack(GROUP_A, w, BF16)))
    posr = positions.astype(F32).reshape(1, t)
    posc = posr.reshape(t, 1)
    freq = _rope_freq_lanes()
    g1, g2, g3, g4 = w["pre_norm_mix"], w["post_norm_mix"], w["pre_norm_mlp"], w["post_norm_mlp"]
    qan, kvan, sinks = w["q_a_norm"], w["kv_a_norm"], w["sinks"]

    h, proj = _proj_fwd(x, g1, wa["w_in"])
    out_a, lse_a = _swa_fwd(proj, posc, posr, sinks)
    qm, km, qt, kt, vm, vt = _mla_prep_fwd(proj, posc, freq, qan, kvan, wa["wq"], wa["wk"], wa["wv"])
    out_bt, lse_b, wb = _mla_fwd(km, qt, vt, _pack(GROUP_B, w, BF16))
    w_oa, w_ob = _col_sharded_full(wb, "w_o_a", GROUP_B), _col_sharded_full(wb, "w_o_b", GROUP_B)
    merged, y, x1, h2 = _mix_out_fwd(out_a, out_bt, proj, x, w_oa, w_ob, wb, g2, g3)
    a = _up_fwd(h2, wb)
    dx2, dyd, dg4, loss = _down_fwd_loss(a, wb, x1, target, g4)

    dw_down = _matmul_tn(a, dyd, "dw_down", 1024, 1024, tk=512)
    du = _down_bwd(dyd, wb, a)
    dw_up = _matmul_tn(h2, du, "dw_up", 1024, 1024, tk=512, shard_major=True)
    dx1, dy, dg3, dg2 = _up_bwd(du, wb, x1, dx2, y, g3, g2)
    dw_out = _matmul_tn(merged, dy, "dw_out", 1024, 1024, tk=512)
    doa, dob, dga, dgb, d_out_a, d_out_b, d_out_bt, del_a, del_b = _mix_out_bwd(dy, out_a, out_bt, proj, w_oa, w_ob, wb)
    dw_oa = _matmul_tn(out_a, doa, "dw_o_a", 512, 1024)
    dw_ob = _dw_ob(out_bt, dob)
    gp_b = jnp.concatenate([dw_up, dw_down.reshape(N_CHIPS, -1, D_MODEL), dw_out.reshape(N_CHIPS, -1, D_MODEL),
                            _col_sharded_blocks(dw_oa, "w_o_a"), _col_sharded_blocks(dw_ob, "w_o_b")], axis=1)
    dqm, dkm, dvm, land_b = _mla_bwd(qm, km, qt, kt, vm, d_out_b, d_out_bt, lse_b, del_b, gp_b)
    dcq, dckv, dkr, dwq, dwk, dwv, dqan, dkvan = _mla_prep_bwd(
        dqm, dkm, dvm, proj, posc, freq, qan, kvan, wa["wq"], wa["wk"], wa["wv"])
    dqa, dka, dva, dsinks = _swa_bwd(proj, d_out_a, lse_a, del_a, posc, posr, sinks)
    dproj = jnp.concatenate([dga, dgb, dqa.astype(BF16), dka.astype(BF16), dva.astype(BF16), dcq, dckv, dkr], axis=1)
    dw_in_t = _matmul_tn(dproj, h, "dw_in", D_IN_PAD // 2, 1024, tk=512)
    gp_a = _grad_blocks_a(dw_in_t, dwq, dwk, dwv)
    grad_x, dg1, land_a = _in_bwd(dproj, wa["w_in"], x, dx1, g1, gp_a.astype(BF16))

    dsmall = dict(pre_norm_mix=dg1, post_norm_mix=dg2, pre_norm_mlp=dg3, post_norm_mlp=dg4,
                  q_a_norm=dqan, kv_a_norm=dkvan, sinks=dsinks)
    return loss, grad_x, {GROUP_A: (gp_a, land_a), GROUP_B: (gp_b, land_b)}, dsmall


def kernel(x, positions, pre_norm_mix, w_in, q_a_norm, w_q_b, kv_a_norm, w_kv_b, sinks, w_o_a, w_o_b, w_out, post_norm_mix, pre_norm_mlp, w_up, w_down, post_norm_mlp, loss_target, m_pre_norm_mix, m_w_in, m_q_a_norm, m_w_q_b, m_kv_a_norm, m_w_kv_b, m_sinks, m_w_o_a, m_w_o_b, m_w_out, m_post_norm_mix, m_pre_norm_mlp, m_w_up, m_w_down, m_post_norm_mlp, v_pre_norm_mix, v_w_in, v_q_a_norm, v_w_q_b, v_kv_a_norm, v_w_kv_b, v_sinks, v_w_o_a, v_w_o_b, v_w_out, v_post_norm_mix, v_pre_norm_mlp, v_w_up, v_w_down, v_post_norm_mlp):
    w = dict(pre_norm_mix=pre_norm_mix, w_in=w_in[0], q_a_norm=q_a_norm, w_q_b=w_q_b[0], kv_a_norm=kv_a_norm,
             w_kv_b=w_kv_b[0], sinks=sinks, w_o_a=w_o_a[0], w_o_b=w_o_b[0], w_out=w_out[0],
             post_norm_mix=post_norm_mix, pre_norm_mlp=pre_norm_mlp, w_up=w_up[0], w_down=w_down[0],
             post_norm_mlp=post_norm_mlp)
    m = dict(pre_norm_mix=m_pre_norm_mix, w_in=m_w_in[0], q_a_norm=m_q_a_norm, w_q_b=m_w_q_b[0],
             kv_a_norm=m_kv_a_norm, w_kv_b=m_w_kv_b[0], sinks=m_sinks, w_o_a=m_w_o_a[0], w_o_b=m_w_o_b[0],
             w_out=m_w_out[0], post_norm_mix=m_post_norm_mix, pre_norm_mlp=m_pre_norm_mlp, w_up=m_w_up[0],
             w_down=m_w_down[0], post_norm_mlp=m_post_norm_mlp)
    v = dict(pre_norm_mix=v_pre_norm_mix, w_in=v_w_in[0], q_a_norm=v_q_a_norm, w_q_b=v_w_q_b[0],
             kv_a_norm=v_kv_a_norm, w_kv_b=v_w_kv_b[0], sinks=v_sinks, w_o_a=v_w_o_a[0], w_o_b=v_w_o_b[0],
             w_out=v_w_out[0], post_norm_mix=v_post_norm_mix, pre_norm_mlp=v_pre_norm_mlp, w_up=v_w_up[0],
             w_down=v_w_down[0], post_norm_mlp=v_post_norm_mlp)

    loss, grad_x, blocks, dsmall = _fwd_bwd(x[0], positions, loss_target[0], w)

    red = _all_reduce_small(dsmall, loss)
    small = _adamw_small(red, w, m, v)

    chip = (2 * lax.axis_index("x") + lax.axis_index("y")).astype(jnp.int32).reshape(1)
    reduced = {}
    for group, tag in ((GROUP_A, "a"), (GROUP_B, "b")):
        gp, land = blocks[group]
        part = _sum4(gp, land, chip, "rs_sum_" + tag)
        reduced[group] = [part, _swap_sibling(part, "rs_swap_" + tag)]

    big = {}
    tr = jnp.transpose
    big["w_in"] = [tr(o)[None] for o in _adamw(tr(w["w_in"]), reduced[GROUP_A], tr(m["w_in"]), tr(v["w_in"]),
                                               "adamw_w_in", (_W_IN_ROWS, 256))]
    for n in ("w_up", "w_down", "w_out"):
        big[n] = [o[None] for o in _adamw(w[n], reduced[GROUP_B], m[n], v[n], "adamw_" + n, (128, D_MODEL),
                                          _row_offset(GROUP_B, n))]
    for group, names in ((GROUP_A, ("w_q_b", "w_kv_b")), (GROUP_B, ("w_o_a", "w_o_b"))):
        for n in names:
            off = _row_offset(group, n)
            g_parts = [p[off:off + PACK_ROWS[n]].reshape(SHARD_SHAPES[n]) for p in reduced[group]]
            big[n] = [o[None] for o in _adamw(w[n], g_parts, m[n], v[n], "adamw_" + n, SHARD_SHAPES[n])]

    outs = [big[n][k] if n in big else small[n][k] for k in range(4) for n in WEIGHTS]
    return (red[_LOSS_ROW, 0], grad_x[None], *outs)
```

```python
import jax
import jax.numpy as jnp
from jax import lax
from jax.experimental import pallas as pl
from jax.experimental.pallas import tpu as pltpu

F32 = jnp.float32
BF16 = jnp.bfloat16
MESH = pl.DeviceIdType.MESH

D_MODEL = 1024
N_HEADS_A = 8
N_KV_A = 2
HEAD_DIM_A = 64
WINDOW = 128
BLOCK = 128
N_HEADS_B = 8
QK_NOPE = 64
QK_ROPE = 32
V_DIM_B = 64
Q_LORA = 256
KV_LORA = 128
ROPE_THETA = 10000.0
D_FF = 4 * D_MODEL
EPS = 1e-6
WIDTH_A = N_HEADS_A * HEAD_DIM_A
Q_HEAD_B = QK_NOPE + QK_ROPE
D_IN_PAD = 3328
HEAD_PAD = 128
MLA_W = N_HEADS_B * HEAD_PAD

ADAM_LR = 0.001
ADAM_B1 = 0.9
ADAM_B2 = 0.999
ADAM_EPS = 1e-08
ADAM_WD = 0.01
ADAM_STEP = 10

NEG = -1e30
N_CHIPS = 4
LANES = 128
VMEM_LIMIT = 56 * 1024 * 1024

SHARD_SHAPES = {"w_in": (1024, 808), "w_q_b": (256, 192), "w_kv_b": (128, 256), "w_o_a": (512, 256),
                "w_o_b": (512, 256), "w_out": (256, 1024), "w_up": (1024, 1024), "w_down": (1024, 1024)}
PACK_ROWS = {n: (s[0] * s[1]) // D_MODEL for n, s in SHARD_SHAPES.items()}
GROUP_A = ("w_in", "w_q_b", "w_kv_b")
GROUP_B = ("w_up", "w_down", "w_out", "w_o_a", "w_o_b")
WEIGHTS = ("pre_norm_mix", "w_in", "q_a_norm", "w_q_b", "kv_a_norm", "w_kv_b", "sinks", "w_o_a", "w_o_b", "w_out",
           "post_norm_mix", "pre_norm_mlp", "w_up", "w_down", "post_norm_mlp")


def _params(sem=None):
    return pltpu.CompilerParams(dimension_semantics=sem, vmem_limit_bytes=VMEM_LIMIT)


def _dot(a, b):
    return jnp.dot(a, b, preferred_element_type=F32)


def _dot_nt(a, b):
    return lax.dot_general(a, b, (((1,), (1,)), ((), ())), preferred_element_type=F32)


def _dot_tn(a, b):
    return lax.dot_general(a, b, (((0,), (0,)), ((), ())), preferred_element_type=F32)


def _rms(v):
    return lax.rsqrt(jnp.mean(v * v, axis=-1, keepdims=True) + EPS)


def _norm_bwd(dout, n, r, g):
    dn = dout * g
    dx = r * (dn - n * jnp.mean(dn * n, axis=-1, keepdims=True))
    return dx, jnp.sum(dout * n, axis=0, keepdims=True)


def _full(shape):
    return pl.BlockSpec(shape, lambda *_: (0,) * len(shape))


def _row_offset(group, name):
    return sum(PACK_ROWS[n] for n in group[:group.index(name)])


def _wb_spec(name):
    rows = PACK_ROWS[name]
    return pl.BlockSpec((N_CHIPS, rows, D_MODEL), lambda *_: (0, _row_offset(GROUP_B, name) // rows, 0))


def _proj_fwd(x, g1, w_in_t):
    t = x.shape[0]
    tm = 256

    def body(x_ref, g_ref, w_ref, h_ref, p_ref):
        xv = x_ref[...]
        h = ((xv * _rms(xv)) * g_ref[...]).astype(BF16)
        h_ref[...] = h
        p_ref[...] = _dot_nt(h, w_ref[...])

    return pl.pallas_call(
        body, name="proj_fwd", grid=(t // tm,),
        in_specs=[pl.BlockSpec((tm, D_MODEL), lambda i: (i, 0)), _full((1, D_MODEL)), _full((D_IN_PAD, D_MODEL))],
        out_specs=[pl.BlockSpec((tm, D_MODEL), lambda i: (i, 0)), pl.BlockSpec((tm, D_IN_PAD), lambda i: (i, 0))],
        out_shape=[jax.ShapeDtypeStruct((t, D_MODEL), BF16), jax.ShapeDtypeStruct((t, D_IN_PAD), F32)],
        compiler_params=_params(("parallel",)),
    )(x, g1, w_in_t)


_QA_BLK = 2048 // WIDTH_A
_KA_BLK = 2560 // LANES
_VA_BLK = 2688 // LANES
_CQ_BLK = 2816 // Q_LORA
_CKV_BLK = 3072 // LANES
_KR_BLK = 3200 // LANES


_GROUP_A = N_HEADS_A // N_KV_A
_SWA_SCALE = HEAD_DIM_A ** -0.5


def _head_cols(v, h):
    return v[:, HEAD_DIM_A * h:HEAD_DIM_A * (h + 1)]


def _head_rows(v, h):
    return v[HEAD_DIM_A * h:HEAD_DIM_A * (h + 1), :]


def _swa_band(n, kp_ref, kc_ref, vp_ref, vc_ref, pq_ref, pp_ref, pc_ref):
    kb = jnp.concatenate([kp_ref[...], kc_ref[...]], axis=0)
    vb = jnp.concatenate([vp_ref[...], vc_ref[...]], axis=0)
    posk = jnp.concatenate([pp_ref[...], pc_ref[...]], axis=0)
    dist = jnp.abs(posk - pq_ref[...])
    ki = lax.broadcasted_iota(jnp.int32, (2 * BLOCK, BLOCK), 0)
    qi = lax.broadcasted_iota(jnp.int32, (2 * BLOCK, BLOCK), 1)
    valid = (ki > qi) & (ki <= qi + WINDOW) & ((n > 0) | (ki >= BLOCK))
    return kb, vb, dist, valid


def _swa_scores_t(st_g, j, h, dist, valid):
    slope = 2.0 ** (-8.0 * (h + 1) / N_HEADS_A)
    st = st_g[:, BLOCK * j:BLOCK * (j + 1)] * _SWA_SCALE - slope * dist
    return jnp.where(valid, st, NEG)


def _group_t(xt, kh):
    return jnp.concatenate([_head_rows(xt, _GROUP_A * kh + j) for j in range(_GROUP_A)], axis=1).astype(BF16)


def _swa_fwd(proj, posc, posr, sinks):
    t = proj.shape[0]
    nb = t // BLOCK

    def body(q_ref, kc_ref, kp_ref, vc_ref, vp_ref, pq_ref, pc_ref, pp_ref, sink_ref, o_ref, l_ref):
        n = pl.program_id(0)
        kb, vb, dist, valid = _swa_band(n, kp_ref, kc_ref, vp_ref, vc_ref, pq_ref, pp_ref, pc_ref)
        q_t, vb_t = q_ref[...].T, vb.T
        out_t, lse = [], []
        for kh in range(N_KV_A):
            st_g = _dot(_head_cols(kb, kh).astype(BF16), _group_t(q_t, kh))
            ps = []
            for j in range(_GROUP_A):
                h = _GROUP_A * kh + j
                st = _swa_scores_t(st_g, j, h, dist, valid)
                sink = sink_ref[0:1, h:h + 1]
                m = jnp.maximum(jnp.max(st, axis=0, keepdims=True), sink)
                e = jnp.exp(st - m)
                den = jnp.sum(e, axis=0, keepdims=True) + jnp.exp(sink - m)
                ps.append((e / den).astype(BF16))
                lse.append(m + jnp.log(den))
            o_g = _dot(_head_rows(vb_t, kh).astype(BF16), jnp.concatenate(ps, axis=1))
            out_t.extend(o_g[:, BLOCK * j:BLOCK * (j + 1)] for j in range(_GROUP_A))
        o_ref[...] = jnp.concatenate(out_t, axis=0).T
        l_ref[...] = jnp.concatenate(lse, axis=0)

    cur = lambda n: (n, 0)
    prev = lambda n: jnp.maximum(n - 1, 0)
    return pl.pallas_call(
        body, name="swa_fwd", grid=(nb,),
        in_specs=[pl.BlockSpec((BLOCK, WIDTH_A), lambda n: (n, _QA_BLK)),
                  pl.BlockSpec((BLOCK, LANES), lambda n: (n, _KA_BLK)),
                  pl.BlockSpec((BLOCK, LANES), lambda n: (prev(n), _KA_BLK)),
                  pl.BlockSpec((BLOCK, LANES), lambda n: (n, _VA_BLK)),
                  pl.BlockSpec((BLOCK, LANES), lambda n: (prev(n), _VA_BLK)),
                  pl.BlockSpec((1, BLOCK), lambda n: (0, n)),
                  pl.BlockSpec((BLOCK, 1), cur),
                  pl.BlockSpec((BLOCK, 1), lambda n: (prev(n), 0)),
                  _full((1, N_HEADS_A))],
        out_specs=[pl.BlockSpec((BLOCK, WIDTH_A), cur), pl.BlockSpec((N_HEADS_A, BLOCK), lambda n: (0, n))],
        out_shape=[jax.ShapeDtypeStruct((t, WIDTH_A), F32), jax.ShapeDtypeStruct((N_HEADS_A, t), F32)],
        compiler_params=_params(("parallel",)),
    )(proj, proj, proj, proj, proj, posr, posc, posc, sinks)


def _rope_coeffs(pos, freq):
    ang = pos * freq
    cosv, sinv = jnp.cos(ang), jnp.sin(ang)
    lane = lax.broadcasted_iota(jnp.int32, ang.shape, 1)
    lo = (lane >= QK_NOPE) & (lane < QK_NOPE + QK_ROPE // 2)
    hi = (lane >= QK_NOPE + QK_ROPE // 2) & (lane < QK_NOPE + QK_ROPE)
    c = jnp.where(lane < QK_NOPE, 1.0, jnp.where(lo | hi, cosv, 0.0))
    s = jnp.where(lo, -sinv, jnp.where(hi, sinv, 0.0))
    return c, s, lo, hi


def _rope(xh, c, s, lo):
    up = pltpu.roll(xh, LANES - QK_ROPE // 2, axis=1)
    dn = pltpu.roll(xh, QK_ROPE // 2, axis=1)
    return xh * c + jnp.where(lo, up, dn) * s


def _unrope(dh, c, s, lo, hi):
    g = dh * s
    up = pltpu.roll(g, LANES - QK_ROPE // 2, axis=1)
    dn = pltpu.roll(g, QK_ROPE // 2, axis=1)
    return dh * c + jnp.where(hi, dn, jnp.where(lo, up, 0.0))


_TQ = 512
_MLA_SCALE = Q_HEAD_B ** -0.5


def _mla_prep_fwd(proj, posc, freq, qan, kvan, wq, wk, wv):
    t = proj.shape[0]
    tm = _TQ
    nb = t // tm

    def body(cq_ref, ckv_ref, kr_ref, pos_ref, f_ref, qan_ref, kvan_ref, wq_ref, wk_ref, wv_ref,
             q_ref, k_ref, qt_ref, kt_ref, v_ref, vt_ref):
        cq = cq_ref[...]
        cqn = ((cq * _rms(cq)) * qan_ref[...]).astype(BF16)
        ckv = ckv_ref[...]
        ckvn = ((ckv * _rms(ckv)) * kvan_ref[...]).astype(BF16)
        qb = _dot(cqn, wq_ref[...])
        kb = _dot(ckvn, wk_ref[...])
        vb = _dot(ckvn, wv_ref[...])
        vbt = vb.T
        c, s, lo, _ = _rope_coeffs(pos_ref[...], f_ref[...])
        kr = _rope(kr_ref[...], c, s, lo)
        for h in range(N_HEADS_B):
            sl = slice(HEAD_PAD * h, HEAD_PAD * (h + 1))
            q_h = _rope(qb[:, sl], c, s, lo)
            k_h = kb[:, sl] + kr
            q_ref[:, sl] = q_h.astype(BF16)
            k_ref[:, sl] = k_h.astype(BF16)
            qt_ref[h, 0] = q_h.T.astype(BF16)
            kt_ref[h, 0] = k_h.T.astype(BF16)
            v_ref[h] = vb[:, V_DIM_B * h:V_DIM_B * (h + 1)].astype(BF16)
            vt_ref[h, 0] = vbt[V_DIM_B * h:V_DIM_B * (h + 1), :].astype(BF16)

    row = lambda i: (i, 0)
    blk4 = lambda d: pl.BlockSpec((N_HEADS_B, 1, d, tm), lambda i: (0, i, 0, 0))
    return pl.pallas_call(
        body, name="mla_prep_fwd", grid=(nb,),
        in_specs=[pl.BlockSpec((tm, Q_LORA), lambda i: (i, _CQ_BLK)),
                  pl.BlockSpec((tm, LANES), lambda i: (i, _CKV_BLK)),
                  pl.BlockSpec((tm, LANES), lambda i: (i, _KR_BLK)),
                  pl.BlockSpec((tm, 1), row), _full((1, LANES)), _full((1, Q_LORA)), _full((1, KV_LORA)),
                  _full((Q_LORA, MLA_W)), _full((KV_LORA, MLA_W)), _full((KV_LORA, N_HEADS_B * V_DIM_B))],
        out_specs=[pl.BlockSpec((tm, MLA_W), row), pl.BlockSpec((tm, MLA_W), row), blk4(HEAD_PAD), blk4(HEAD_PAD),
                   pl.BlockSpec((N_HEADS_B, tm, V_DIM_B), lambda i: (0, i, 0)), blk4(V_DIM_B)],
        out_shape=[jax.ShapeDtypeStruct((t, MLA_W), BF16), jax.ShapeDtypeStruct((t, MLA_W), BF16),
                   jax.ShapeDtypeStruct((N_HEADS_B, nb, HEAD_PAD, tm), BF16),
                   jax.ShapeDtypeStruct((N_HEADS_B, nb, HEAD_PAD, tm), BF16),
                   jax.ShapeDtypeStruct((N_HEADS_B, t, V_DIM_B), BF16),
                   jax.ShapeDtypeStruct((N_HEADS_B, nb, V_DIM_B, tm), BF16)],
        compiler_params=_params(("parallel",)),
    )(proj, proj, proj, posc, freq, qan, kvan, wq, wk, wv)


_LOG2E = 1.4426950408889634
_MLA_SCALE2 = _MLA_SCALE * _LOG2E


def _scores_t(k, qt, diagonal):
    st = _dot(k, qt) * _MLA_SCALE2
    if diagonal:
        key = lax.broadcasted_iota(jnp.int32, st.shape, 0)
        qry = lax.broadcasted_iota(jnp.int32, st.shape, 1)
        st = jnp.where(key <= qry, st, NEG)
    return st


def _mla_fwd(k, qt, vt, w_src):
    t = k.shape[0]
    nb = t // _TQ

    def body(k_ref, qt_ref, vt_ref, w_ref, o_ref, l_ref, wg_ref, raw_a, raw_b, send_sems, recv_sems, local_sem):
        qi = pl.program_id(1)
        first = (pl.program_id(0) == 0) & (qi == 0)
        last = (pl.program_id(0) == N_HEADS_B - 1) & (qi == nb - 1)

        @pl.when(first)
        def _():
            _gather_start(w_ref, wg_ref, send_sems, recv_sems, local_sem)

        q_t = qt_ref[0, 0]

        def product(kj):
            return _dot(k_ref[pl.ds(pl.multiple_of(kj * _TQ, _TQ), _TQ), :], q_t)

        def update(stats, raw_ref, kj, diagonal=False):
            m, l, acc = stats
            st = raw_ref[...] * _MLA_SCALE2
            if diagonal:
                key = lax.broadcasted_iota(jnp.int32, st.shape, 0)
                qry = lax.broadcasted_iota(jnp.int32, st.shape, 1)
                st = jnp.where(key <= qry, st, NEG)
            m_new = jnp.maximum(m, jnp.max(st, axis=0, keepdims=True))
            alpha = jnp.exp2(m - m_new)
            p = jnp.exp2(st - m_new)
            l = alpha * l + jnp.sum(p, axis=0, keepdims=True)
            acc = alpha * acc + _dot(vt_ref[0, kj], p.astype(BF16))
            return m_new, l, acc

        def trip(i, stats):
            raw_b[...] = product(2 * i + 1)
            stats = update(stats, raw_a, 2 * i)
            raw_a[...] = product(2 * i + 2)
            return update(stats, raw_b, 2 * i + 1)

        def tail_even(stats):
            return update(stats, raw_a, qi, True)

        def tail_odd(stats):
            raw_b[...] = product(qi)
            return update(update(stats, raw_a, qi - 1), raw_b, qi, True)

        init = (jnp.full((1, _TQ), NEG, F32), jnp.zeros((1, _TQ), F32), jnp.zeros((V_DIM_B, _TQ), F32))
        raw_a[...] = product(0)
        stats = lax.fori_loop(0, qi // 2, trip, init)
        m, l, acc = lax.cond(qi % 2 == 0, tail_even, tail_odd, stats)
        o_ref[0, 0] = acc / l
        l_ref[0, 0] = m + jnp.log(l) * _LOG2E

        @pl.when(last)
        def _():
            _gather_wait(w_ref, wg_ref, send_sems, recv_sems, local_sem)

    return pl.pallas_call(
        body, name="mla_fwd", grid=(N_HEADS_B, nb),
        in_specs=[pl.BlockSpec((t, HEAD_PAD), lambda h, qi: (0, h)),
                  pl.BlockSpec((1, 1, HEAD_PAD, _TQ), lambda h, qi: (h, qi, 0, 0)),
                  pl.BlockSpec((1, nb, V_DIM_B, _TQ), lambda h, qi: (h, 0, 0, 0)), _HBM],
        out_specs=[pl.BlockSpec((1, 1, V_DIM_B, _TQ), lambda h, qi: (h, qi, 0, 0)),
                   pl.BlockSpec((1, 1, 1, _TQ), lambda h, qi: (h, qi, 0, 0)), _HBM],
        out_shape=[jax.ShapeDtypeStruct((N_HEADS_B, nb, V_DIM_B, _TQ), F32),
                   jax.ShapeDtypeStruct((N_HEADS_B, nb, 1, _TQ), F32),
                   jax.ShapeDtypeStruct((N_CHIPS,) + w_src.shape, w_src.dtype)],
        scratch_shapes=[pltpu.VMEM((_TQ, _TQ), F32), pltpu.VMEM((_TQ, _TQ), F32),
                        pltpu.SemaphoreType.DMA((3,)), pltpu.SemaphoreType.DMA((3,)), pltpu.SemaphoreType.DMA(())],
        compiler_params=_params(("arbitrary", "arbitrary")),
    )(k, qt, vt, w_src)


def _ot_spec(tm, d):
    per = _TQ // tm
    return pl.BlockSpec((N_HEADS_B, 1, d, tm), lambda i: (0, i // per, 0, i % per))


def _mix_out_fwd(out_a, out_bt, proj, x, w_oa, w_ob, wb, g2, g3):
    t = x.shape[0]
    tm = 256

    def body(oa_ref, obt_ref, ga_ref, gb_ref, x_ref, woa_ref, wob_ref, wout_ref, g2_ref, g3_ref,
             mg_ref, y_ref, x1_ref, h2_ref):
        oa = _dot(oa_ref[...].astype(BF16), woa_ref[...])
        obt = obt_ref[...].reshape(N_HEADS_B * V_DIM_B, tm).astype(BF16)
        ob = _dot_tn(obt, wob_ref[...])
        merged = (jax.nn.sigmoid(ga_ref[...]) * oa + jax.nn.sigmoid(gb_ref[...]) * ob).astype(BF16)
        mg_ref[...] = merged
        y = _dot(merged, wout_ref[...].reshape(D_MODEL, D_MODEL))
        y_ref[...] = y
        x1 = x_ref[...] + (y * _rms(y)) * g2_ref[...]
        x1_ref[...] = x1
        h2_ref[...] = ((x1 * _rms(x1)) * g3_ref[...]).astype(BF16)

    row = lambda i: (i, 0)
    blk = pl.BlockSpec((tm, D_MODEL), row)
    return pl.pallas_call(
        body, name="mix_out_fwd", grid=(t // tm,),
        in_specs=[pl.BlockSpec((tm, WIDTH_A), row), _ot_spec(tm, V_DIM_B), pl.BlockSpec((tm, D_MODEL), lambda i: (i, 0)),
                  pl.BlockSpec((tm, D_MODEL), lambda i: (i, 1)), blk,
                  _full((WIDTH_A, D_MODEL)), _full((N_HEADS_B * V_DIM_B, D_MODEL)), _wb_spec("w_out"),
                  _full((1, D_MODEL)), _full((1, D_MODEL))],
        out_specs=[blk, blk, blk, blk],
        out_shape=[jax.ShapeDtypeStruct((t, D_MODEL), BF16), jax.ShapeDtypeStruct((t, D_MODEL), F32),
                   jax.ShapeDtypeStruct((t, D_MODEL), F32), jax.ShapeDtypeStruct((t, D_MODEL), BF16)],
        compiler_params=_params(("parallel",)),
    )(out_a, out_bt, proj, proj, x, w_oa, w_ob, wb, g2, g3)


_TM_MLP = 512


def _up_fwd(h2, wb):
    t = h2.shape[0]
    tm = _TM_MLP

    def body(h_ref, w_ref, a_ref):
        hv = h_ref[...]
        for j in range(N_CHIPS):
            u = _dot(hv, w_ref[j])
            a_ref[:, D_MODEL * j:D_MODEL * (j + 1)] = jnp.square(jnp.maximum(u, 0.0)).astype(BF16)

    return pl.pallas_call(
        body, name="up_fwd", grid=(t // tm,),
        in_specs=[pl.BlockSpec((tm, D_MODEL), lambda i: (i, 0)), _wb_spec("w_up")],
        out_specs=pl.BlockSpec((tm, D_FF), lambda i: (i, 0)),
        out_shape=jax.ShapeDtypeStruct((t, D_FF), BF16),
        compiler_params=_params(("parallel",)),
    )(h2, wb)


def _down_fwd_loss(a, wb, x1, target, g4):
    t = a.shape[0]
    tm = _TM_MLP

    def body(a_ref, w_ref, x1_ref, tg_ref, g_ref, dx2_ref, dyd_ref, dg_ref, loss_ref):
        @pl.when(pl.program_id(0) == 0)
        def _():
            dg_ref[...] = jnp.zeros(dg_ref.shape, F32)
            loss_ref[...] = jnp.zeros(loss_ref.shape, F32)

        yd = _dot(a_ref[...], w_ref[...].reshape(D_FF, D_MODEL))
        r = _rms(yd)
        n = yd * r
        diff = (x1_ref[...] + n * g_ref[...]) - tg_ref[...]
        loss_ref[...] += 0.5 * jnp.sum(jnp.mean(diff * diff, axis=-1, keepdims=True), axis=0, keepdims=True)
        dx2 = diff * (1.0 / D_MODEL)
        dx2_ref[...] = dx2
        dyd, dg = _norm_bwd(dx2, n, r, g_ref[...])
        dyd_ref[...] = dyd.astype(BF16)
        dg_ref[...] += dg

    row = lambda i: (i, 0)
    blk = pl.BlockSpec((tm, D_MODEL), row)
    return pl.pallas_call(
        body, name="down_fwd_loss", grid=(t // tm,),
        in_specs=[pl.BlockSpec((tm, D_FF), row), _wb_spec("w_down"), blk, blk, _full((1, D_MODEL))],
        out_specs=[blk, blk, _full((1, D_MODEL)), _full((1, LANES))],
        out_shape=[jax.ShapeDtypeStruct((t, D_MODEL), F32), jax.ShapeDtypeStruct((t, D_MODEL), BF16),
                   jax.ShapeDtypeStruct((1, D_MODEL), F32), jax.ShapeDtypeStruct((1, LANES), F32)],
        compiler_params=_params(("arbitrary",)),
    )(a, wb, x1, target, g4)


def _matmul_tn(a, b, name, tm, tn, tk=1024, shard_major=False):
    t, m = a.shape
    n = b.shape[1]
    tk = min(tk, t)
    nk = t // tk

    def body(a_ref, b_ref, o_ref):
        @pl.when(pl.program_id(2) == 0)
        def _():
            o_ref[...] = jnp.zeros(o_ref.shape, F32)

        acc = _dot_tn(a_ref[...].astype(BF16), b_ref[...].astype(BF16))
        o_ref[...] += acc[None] if shard_major else acc

    if shard_major:
        out_spec = pl.BlockSpec((1, tm, tn), lambda i, j, k: (j, i, 0))
        out_shape = jax.ShapeDtypeStruct((n // tn, m, tn), F32)
    else:
        out_spec = pl.BlockSpec((tm, tn), lambda i, j, k: (i, j))
        out_shape = jax.ShapeDtypeStruct((m, n), F32)
    return pl.pallas_call(
        body, name=name, grid=(m // tm, n // tn, nk),
        in_specs=[pl.BlockSpec((tk, tm), lambda i, j, k: (k, i)), pl.BlockSpec((tk, tn), lambda i, j, k: (k, j))],
        out_specs=out_spec, out_shape=out_shape,
        compiler_params=_params(("parallel", "parallel", "arbitrary")),
    )(a, b)


def _dw_into_blocks(a, b, weight, tm, tk, buf=None):
    t, m = a.shape
    n = b.shape[1]
    nk = t // tk
    rows = PACK_ROWS[weight]
    first = _row_offset(GROUP_B, weight) // tm
    per_chip = rows // tm
    if weight == "w_up":
        out_map = lambda i, j, k: (j, first + i, 0)
    else:
        out_map = lambda i, j, k: (i // per_chip, first + i % per_chip, 0)

    def body(a_ref, b_ref, *rest):
        o_ref = rest[-1]

        @pl.when(pl.program_id(2) == 0)
        def _():
            o_ref[...] = jnp.zeros(o_ref.shape, F32)

        o_ref[...] += _dot_tn(a_ref[...].astype(BF16), b_ref[...].astype(BF16))[None]

    in_specs = [pl.BlockSpec((tk, tm), lambda i, j, k: (k, i)), pl.BlockSpec((tk, D_MODEL), lambda i, j, k: (k, j))]
    operands = [a, b]
    if buf is not None:
        in_specs.append(pl.BlockSpec(memory_space=pl.ANY))
        operands.append(buf)
    total = sum(PACK_ROWS[w] for w in GROUP_B)
    return pl.pallas_call(
        body, name="dw_" + weight[2:], grid=(m // tm, n // D_MODEL, nk),
        in_specs=in_specs, out_specs=pl.BlockSpec((1, tm, D_MODEL), out_map),
        out_shape=jax.ShapeDtypeStruct((N_CHIPS, total, D_MODEL), F32),
        input_output_aliases={} if buf is None else {2: 0},
        compiler_params=_params(("parallel", "parallel", "arbitrary")),
    )(*operands)


def _down_bwd(dyd, wb, a):
    t = dyd.shape[0]
    tm = _TM_MLP

    def body(d_ref, w_ref, a_ref, du_ref):
        da = _dot_nt(d_ref[...], w_ref[...].reshape(D_FF, D_MODEL))
        du_ref[...] = (da * (2.0 * jnp.sqrt(a_ref[...].astype(F32)))).astype(BF16)

    row = lambda i: (i, 0)
    return pl.pallas_call(
        body, name="down_bwd", grid=(t // tm,),
        in_specs=[pl.BlockSpec((tm, D_MODEL), row), _wb_spec("w_down"), pl.BlockSpec((tm, D_FF), row)],
        out_specs=pl.BlockSpec((tm, D_FF), row),
        out_shape=jax.ShapeDtypeStruct((t, D_FF), BF16),
        compiler_params=_params(("parallel",)),
    )(dyd, wb, a)


def _up_bwd(du, wb, x1, dx2, y, g3, g2):
    t = du.shape[0]
    tm = _TM_MLP

    def body(du_ref, w_ref, x1_ref, dx2_ref, y_ref, g3_ref, g2_ref, dx1_ref, dy_ref, dg3_ref, dg2_ref):
        @pl.when(pl.program_id(0) == 0)
        def _():
            dg3_ref[...] = jnp.zeros(dg3_ref.shape, F32)
            dg2_ref[...] = jnp.zeros(dg2_ref.shape, F32)

        dh2 = _dot_nt(du_ref[:, 0:D_MODEL], w_ref[0])
        for j in range(1, N_CHIPS):
            dh2 = dh2 + _dot_nt(du_ref[:, D_MODEL * j:D_MODEL * (j + 1)], w_ref[j])
        x1 = x1_ref[...]
        r3 = _rms(x1)
        d3, dg3 = _norm_bwd(dh2, x1 * r3, r3, g3_ref[...])
        dx1 = dx2_ref[...] + d3
        dx1_ref[...] = dx1
        dg3_ref[...] += dg3
        y = y_ref[...]
        r2 = _rms(y)
        dy, dg2 = _norm_bwd(dx1, y * r2, r2, g2_ref[...])
        dy_ref[...] = dy.astype(BF16)
        dg2_ref[...] += dg2

    row = lambda i: (i, 0)
    blk = pl.BlockSpec((tm, D_MODEL), row)
    return pl.pallas_call(
        body, name="up_bwd", grid=(t // tm,),
        in_specs=[pl.BlockSpec((tm, D_FF), row), _wb_spec("w_up"),
                  blk, blk, blk, _full((1, D_MODEL)), _full((1, D_MODEL))],
        out_specs=[blk, blk, _full((1, D_MODEL)), _full((1, D_MODEL))],
        out_shape=[jax.ShapeDtypeStruct((t, D_MODEL), F32), jax.ShapeDtypeStruct((t, D_MODEL), BF16),
                   jax.ShapeDtypeStruct((1, D_MODEL), F32), jax.ShapeDtypeStruct((1, D_MODEL), F32)],
        compiler_params=_params(("arbitrary",)),
    )(du, wb, x1, dx2, y, g3, g2)


def _mix_out_bwd(dy, out_a, out_bt, proj, w_oa, w_ob, wb):
    t = dy.shape[0]
    tm = 256
    nb = t // _TQ

    def body(dy_ref, oa_ref, obt_ref, ga_ref, gb_ref, woa_ref, wob_ref, wout_ref,
             doa_ref, dob_ref, dga_ref, dgb_ref, da_ref, db_ref, dbt_ref, dela_ref, delb_ref):
        dm = _dot_nt(dy_ref[...], wout_ref[...].reshape(D_MODEL, D_MODEL))
        out_a_v = oa_ref[...]
        out_bt_v = obt_ref[...].reshape(N_HEADS_B * V_DIM_B, tm)
        oa = _dot(out_a_v.astype(BF16), woa_ref[...])
        ob = _dot_tn(out_bt_v.astype(BF16), wob_ref[...])
        sa, sb = jax.nn.sigmoid(ga_ref[...]), jax.nn.sigmoid(gb_ref[...])
        doa = (dm * sa).astype(BF16)
        dob = (dm * sb).astype(BF16)
        doa_ref[...] = doa
        dob_ref[...] = dob
        dga_ref[...] = (dm * oa * (sa * (1.0 - sa))).astype(BF16)
        dgb_ref[...] = (dm * ob * (sb * (1.0 - sb))).astype(BF16)
        d_out_a = _dot_nt(doa, woa_ref[...])
        da_ref[...] = d_out_a
        prod_at = (d_out_a * out_a_v).T
        dela_ref[...] = jnp.concatenate(
            [jnp.sum(_head_rows(prod_at, h), axis=0, keepdims=True) for h in range(N_HEADS_A)], axis=0)
        d_out_b = _dot_nt(dob, wob_ref[...])
        d_out_bt = _dot_nt(wob_ref[...], dob)
        prod_bt = d_out_bt * out_bt_v
        for h in range(N_HEADS_B):
            db_ref[h] = d_out_b[:, V_DIM_B * h:V_DIM_B * (h + 1)].astype(BF16)
            dbt_ref[h, 0] = d_out_bt[V_DIM_B * h:V_DIM_B * (h + 1), :].astype(BF16)
            delb_ref[h, 0] = jnp.sum(prod_bt[V_DIM_B * h:V_DIM_B * (h + 1), :], axis=0, keepdims=True)

    row = lambda i: (i, 0)
    blk = pl.BlockSpec((tm, D_MODEL), row)
    return pl.pallas_call(
        body, name="mix_out_bwd", grid=(t // tm,),
        in_specs=[blk, pl.BlockSpec((tm, WIDTH_A), row), _ot_spec(tm, V_DIM_B), pl.BlockSpec((tm, D_MODEL), lambda i: (i, 0)),
                  pl.BlockSpec((tm, D_MODEL), lambda i: (i, 1)),
                  _full((WIDTH_A, D_MODEL)), _full((N_HEADS_B * V_DIM_B, D_MODEL)), _wb_spec("w_out")],
        out_specs=[blk, blk, blk, blk, pl.BlockSpec((tm, WIDTH_A), row),
                   pl.BlockSpec((N_HEADS_B, tm, V_DIM_B), lambda i: (0, i, 0)), _ot_spec(tm, V_DIM_B),
                   pl.BlockSpec((N_HEADS_A, tm), lambda i: (0, i)), _ot_spec(tm, 1)],
        out_shape=[jax.ShapeDtypeStruct((t, D_MODEL), BF16)] * 4
        + [jax.ShapeDtypeStruct((t, WIDTH_A), F32), jax.ShapeDtypeStruct((N_HEADS_B, t, V_DIM_B), BF16),
           jax.ShapeDtypeStruct((N_HEADS_B, nb, V_DIM_B, _TQ), BF16),
           jax.ShapeDtypeStruct((N_HEADS_A, t), F32), jax.ShapeDtypeStruct((N_HEADS_B, nb, 1, _TQ), F32)],
        compiler_params=_params(("parallel",)),
    )(dy, out_a, out_bt, proj, proj, w_oa, w_ob, wb)


def _dw_ob(out_bt, dob):
    t = dob.shape[0]
    nb = t // _TQ

    def body(obt_ref, dob_ref, o_ref):
        @pl.when(pl.program_id(0) == 0)
        def _():
            o_ref[...] = jnp.zeros(o_ref.shape, F32)

        obt = obt_ref[...].reshape(N_HEADS_B * V_DIM_B, _TQ).astype(BF16)
        o_ref[...] += _dot(obt, dob_ref[...])

    return pl.pallas_call(
        body, name="dw_o_b", grid=(nb,),
        in_specs=[pl.BlockSpec((N_HEADS_B, 1, V_DIM_B, _TQ), lambda i: (0, i, 0, 0)),
                  pl.BlockSpec((_TQ, D_MODEL), lambda i: (i, 0))],
        out_specs=_full((N_HEADS_B * V_DIM_B, D_MODEL)),
        out_shape=jax.ShapeDtypeStruct((N_HEADS_B * V_DIM_B, D_MODEL), F32),
        compiler_params=_params(("arbitrary",)),
    )(out_bt, dob)


def _mla_bwd(q, k, qt, kt, v, d_out, d_out_t, lse, delta, gp):
    t = q.shape[0]
    nb = t // _TQ

    def body(k_ref, kt_ref, v_ref, q_ref, qt_ref, do_ref, dot_ref, l_ref, d_ref, gp_ref,
             dqt_ref, dk_ref, dv_ref, land_ref, send_sems, recv_sems):
        kj = pl.program_id(1)

        @pl.when((pl.program_id(0) == 0) & (kj == 0))
        def _():
            _scatter_start(gp_ref, land_ref, send_sems, recv_sems)

        @pl.when(kj == 0)
        def _():
            dqt_ref[...] = jnp.zeros(dqt_ref.shape, F32)

        kv, k_t, vv = k_ref[...], kt_ref[0, 0], v_ref[0]

        def products(qi, diagonal=False):
            return _scores_t(kv, qt_ref[0, qi], diagonal), _dot(vv, dot_ref[0, qi])

        def update(carry, prods, qi):
            dk, dv = carry
            st, dpt = prods
            rows = pl.ds(pl.multiple_of(qi * _TQ, _TQ), _TQ)
            pt = jnp.exp2(st - l_ref[0, qi])
            dv = dv + _dot(pt.astype(BF16), do_ref[0, rows, :])
            dst = (pt * (dpt - d_ref[0, qi]) * _MLA_SCALE).astype(BF16)
            dk = dk + _dot(dst, q_ref[rows, :])
            dqt_ref[0, qi] += _dot(k_t, dst)
            return dk, dv

        def pair(i, carry):
            qa = kj + 1 + 2 * i
            pa, pb = products(qa), products(qa + 1)
            return update(update(carry, pa, qa), pb, qa + 1)

        init = (jnp.zeros((_TQ, HEAD_PAD), F32), jnp.zeros((_TQ, V_DIM_B), F32))
        carry = update(init, products(kj, True), kj)
        pairs = (nb - 1 - kj) // 2
        carry = lax.fori_loop(0, pairs, pair, carry)
        dk, dv = lax.fori_loop(kj + 1 + 2 * pairs, nb, lambda qi, cr: update(cr, products(qi), qi), carry)
        dk_ref[...] = dk
        dv_ref[0] = dv

        @pl.when((pl.program_id(0) == N_HEADS_B - 1) & (kj == nb - 1))
        def _():
            _scatter_wait(gp_ref, land_ref, send_sems, recv_sems)

    head4 = lambda d: pl.BlockSpec((1, nb, d, _TQ), lambda h, kj: (h, 0, 0, 0))
    return pl.pallas_call(
        body, name="mla_bwd", grid=(N_HEADS_B, nb),
        in_specs=[pl.BlockSpec((_TQ, HEAD_PAD), lambda h, kj: (kj, h)),
                  pl.BlockSpec((1, 1, HEAD_PAD, _TQ), lambda h, kj: (h, kj, 0, 0)),
                  pl.BlockSpec((1, _TQ, V_DIM_B), lambda h, kj: (h, kj, 0)),
                  pl.BlockSpec((t, HEAD_PAD), lambda h, kj: (0, h)), head4(HEAD_PAD),
                  pl.BlockSpec((1, t, V_DIM_B), lambda h, kj: (h, 0, 0)), head4(V_DIM_B), head4(1), head4(1), _HBM],
        out_specs=[head4(HEAD_PAD), pl.BlockSpec((_TQ, HEAD_PAD), lambda h, kj: (kj, h)),
                   pl.BlockSpec((1, _TQ, V_DIM_B), lambda h, kj: (h, kj, 0)), _HBM],
        out_shape=[jax.ShapeDtypeStruct((N_HEADS_B, nb, HEAD_PAD, _TQ), F32), jax.ShapeDtypeStruct((t, MLA_W), F32),
                   jax.ShapeDtypeStruct((N_HEADS_B, t, V_DIM_B), F32),
                   jax.ShapeDtypeStruct((3,) + gp.shape[1:], gp.dtype)],
        scratch_shapes=[pltpu.SemaphoreType.DMA((3,)), pltpu.SemaphoreType.DMA((3,))],
        compiler_params=_params(("arbitrary", "arbitrary")),
    )(k, kt, v, q, qt, d_out, d_out_t, lse, delta, gp)


def _mla_prep_bwd(dqt, dk, dv, proj, posc, freq, qan, kvan, wq, wk, wv):
    t = dk.shape[0]
    tm = _TQ

    def body(dqt_ref, dk_ref, dv_ref, cq_ref, ckv_ref, pos_ref, f_ref, qan_ref, kvan_ref, wq_ref, wk_ref, wv_ref,
             dcq_ref, dckv_ref, dkr_ref, dwq_ref, dwk_ref, dwv_ref, dqan_ref, dkvan_ref):
        @pl.when(pl.program_id(0) == 0)
        def _():
            for r in (dwq_ref, dwk_ref, dwv_ref, dqan_ref, dkvan_ref):
                r[...] = jnp.zeros(r.shape, F32)

        cq = cq_ref[...]
        rq = _rms(cq)
        nq_ = cq * rq
        cqn = (nq_ * qan_ref[...]).astype(BF16)
        ckv = ckv_ref[...]
        rkv = _rms(ckv)
        nkv = ckv * rkv
        ckvn = (nkv * kvan_ref[...]).astype(BF16)
        c, s, lo, hi = _rope_coeffs(pos_ref[...], f_ref[...])
        dkv = dk_ref[...]
        dkr = jnp.zeros((tm, LANES), F32)
        dqb = []
        for h in range(N_HEADS_B):
            dqb.append(_unrope(dqt_ref[h, 0].T, c, s, lo, hi).astype(BF16))
            dkr = dkr + dkv[:, HEAD_PAD * h:HEAD_PAD * (h + 1)]
        dqb = jnp.concatenate(dqb, axis=1)
        dkr_ref[...] = jnp.where(lo | hi, _unrope(dkr, c, s, lo, hi), 0.0).astype(BF16)
        dkb = dkv.astype(BF16)
        dvb = jnp.concatenate([dv_ref[h] for h in range(N_HEADS_B)], axis=1).astype(BF16)
        dwq_ref[...] += _dot_tn(cqn, dqb)
        dwk_ref[...] += _dot_tn(ckvn, dkb)
        dwv_ref[...] += _dot_tn(ckvn, dvb)
        dcqn = _dot_nt(dqb, wq_ref[...])
        dckvn = _dot_nt(dkb, wk_ref[...]) + _dot_nt(dvb, wv_ref[...])
        dcq, dqan = _norm_bwd(dcqn, nq_, rq, qan_ref[...])
        dckv, dkvan = _norm_bwd(dckvn, nkv, rkv, kvan_ref[...])
        dcq_ref[...] = dcq.astype(BF16)
        dckv_ref[...] = dckv.astype(BF16)
        dqan_ref[...] += dqan
        dkvan_ref[...] += dkvan

    row = lambda i: (i, 0)
    vw = N_HEADS_B * V_DIM_B
    return pl.pallas_call(
        body, name="mla_prep_bwd", grid=(t // tm,),
        in_specs=[pl.BlockSpec((N_HEADS_B, 1, HEAD_PAD, tm), lambda i: (0, i, 0, 0)), pl.BlockSpec((tm, MLA_W), row),
                  pl.BlockSpec((N_HEADS_B, tm, V_DIM_B), lambda i: (0, i, 0)),
                  pl.BlockSpec((tm, Q_LORA), lambda i: (i, _CQ_BLK)),
                  pl.BlockSpec((tm, LANES), lambda i: (i, _CKV_BLK)),
                  pl.BlockSpec((tm, 1), row), _full((1, LANES)), _full((1, Q_LORA)), _full((1, KV_LORA)),
                  _full((Q_LORA, MLA_W)), _full((KV_LORA, MLA_W)), _full((KV_LORA, vw))],
        out_specs=[pl.BlockSpec((tm, Q_LORA), row), pl.BlockSpec((tm, LANES), row), pl.BlockSpec((tm, LANES), row),
                   _full((Q_LORA, MLA_W)), _full((KV_LORA, MLA_W)), _full((KV_LORA, vw)),
                   _full((1, Q_LORA)), _full((1, KV_LORA))],
        out_shape=[jax.ShapeDtypeStruct((t, Q_LORA), BF16), jax.ShapeDtypeStruct((t, LANES), BF16),
                   jax.ShapeDtypeStruct((t, LANES), BF16),
                   jax.ShapeDtypeStruct((Q_LORA, MLA_W), F32), jax.ShapeDtypeStruct((KV_LORA, MLA_W), F32),
                   jax.ShapeDtypeStruct((KV_LORA, vw), F32),
                   jax.ShapeDtypeStruct((1, Q_LORA), F32), jax.ShapeDtypeStruct((1, KV_LORA), F32)],
        compiler_params=_params(("arbitrary",)),
    )(dqt, dk, dv, proj, proj, posc, freq, qan, kvan, wq, wk, wv)


def _swa_bwd(proj, d_out, lse, delta, posc, posr, sinks):
    t = proj.shape[0]
    nb = t // BLOCK

    def body(q_ref, kc_ref, kp_ref, vc_ref, vp_ref, do_ref, l_ref, d_ref, pq_ref, pc_ref, pp_ref, sink_ref,
             dq_ref, dk_ref, dv_ref, ds_ref, dkb_s, dvb_s, dk_carry, dv_carry):
        n = pl.program_id(0)

        @pl.when(n == 0)
        def _():
            ds_ref[...] = jnp.zeros(ds_ref.shape, F32)
            dk_carry[...] = jnp.zeros(dk_carry.shape, F32)
            dv_carry[...] = jnp.zeros(dv_carry.shape, F32)

        @pl.when(n < nb)
        def _():
            kb, vb, dist, valid = _swa_band(n, kp_ref, kc_ref, vp_ref, vc_ref, pq_ref, pp_ref, pc_ref)
            qv, dov = q_ref[...], do_ref[...]
            q_t, do_t, kb_t = qv.T, dov.T, kb.T
            lane = lax.broadcasted_iota(jnp.int32, (1, LANES), 1)
            dsink = jnp.zeros((1, LANES), F32)
            dq_t = []
            for kh in range(N_KV_A):
                heads = range(_GROUP_A * kh, _GROUP_A * (kh + 1))
                st_g = _dot(_head_cols(kb, kh).astype(BF16), _group_t(q_t, kh))
                dpt_g = _dot(_head_cols(vb, kh).astype(BF16), _group_t(do_t, kh))
                pts, dsts = [], []
                for j, h in enumerate(heads):
                    st = _swa_scores_t(st_g, j, h, dist, valid)
                    l_h, d_h = l_ref[h:h + 1, :], d_ref[h:h + 1, :]
                    pt = jnp.exp(st - l_h)
                    p_sink = jnp.exp(sink_ref[0:1, h:h + 1] - l_h)
                    dsink = jnp.where(lane == h, jnp.sum(-p_sink * d_h, axis=1, keepdims=True), dsink)
                    dst = pt * (dpt_g[:, BLOCK * j:BLOCK * (j + 1)] - d_h) * _SWA_SCALE
                    pts.append(pt.astype(BF16))
                    dsts.append(dst.astype(BF16))
                pt_g, dst_g = jnp.concatenate(pts, axis=1), jnp.concatenate(dsts, axis=1)
                q_g = jnp.concatenate([_head_cols(qv, h) for h in heads], axis=0).astype(BF16)
                do_g = jnp.concatenate([_head_cols(dov, h) for h in heads], axis=0).astype(BF16)
                dkb_s[:, HEAD_DIM_A * kh:HEAD_DIM_A * (kh + 1)] = _dot(dst_g, q_g)
                dvb_s[:, HEAD_DIM_A * kh:HEAD_DIM_A * (kh + 1)] = _dot(pt_g, do_g)
                dq_g = _dot(_head_rows(kb_t, kh).astype(BF16), dst_g)
                dq_t.extend(dq_g[:, BLOCK * j:BLOCK * (j + 1)] for j in range(_GROUP_A))
            dq_ref[...] = jnp.concatenate(dq_t, axis=0).T
            ds_ref[...] += dsink
            dk_ref[...] = dk_carry[...] + dkb_s[0:BLOCK, :]
            dv_ref[...] = dv_carry[...] + dvb_s[0:BLOCK, :]
            dk_carry[...] = dkb_s[BLOCK:2 * BLOCK, :]
            dv_carry[...] = dvb_s[BLOCK:2 * BLOCK, :]

        @pl.when(n == nb)
        def _():
            dk_ref[...] = dk_carry[...]
            dv_ref[...] = dv_carry[...]

    cur = lambda n: (jnp.minimum(n, nb - 1), 0)
    cur_t = lambda n: (0, jnp.minimum(n, nb - 1))
    prv = lambda n: jnp.maximum(jnp.minimum(n, nb - 1) - 1, 0)
    out_prev = lambda n: (jnp.maximum(n - 1, 0), 0)
    return pl.pallas_call(
        body, name="swa_bwd", grid=(nb + 1,),
        in_specs=[pl.BlockSpec((BLOCK, WIDTH_A), lambda n: (jnp.minimum(n, nb - 1), _QA_BLK)),
                  pl.BlockSpec((BLOCK, LANES), lambda n: (jnp.minimum(n, nb - 1), _KA_BLK)),
                  pl.BlockSpec((BLOCK, LANES), lambda n: (prv(n), _KA_BLK)),
                  pl.BlockSpec((BLOCK, LANES), lambda n: (jnp.minimum(n, nb - 1), _VA_BLK)),
                  pl.BlockSpec((BLOCK, LANES), lambda n: (prv(n), _VA_BLK)),
                  pl.BlockSpec((BLOCK, WIDTH_A), cur), pl.BlockSpec((N_HEADS_A, BLOCK), cur_t),
                  pl.BlockSpec((N_HEADS_A, BLOCK), cur_t), pl.BlockSpec((1, BLOCK), cur_t),
                  pl.BlockSpec((BLOCK, 1), cur), pl.BlockSpec((BLOCK, 1), lambda n: (prv(n), 0)),
                  _full((1, N_HEADS_A))],
        out_specs=[pl.BlockSpec((BLOCK, WIDTH_A), cur), pl.BlockSpec((BLOCK, LANES), out_prev),
                   pl.BlockSpec((BLOCK, LANES), out_prev), _full((1, LANES))],
        out_shape=[jax.ShapeDtypeStruct((t, WIDTH_A), F32), jax.ShapeDtypeStruct((t, LANES), F32),
                   jax.ShapeDtypeStruct((t, LANES), F32), jax.ShapeDtypeStruct((1, LANES), F32)],
        scratch_shapes=[pltpu.VMEM((2 * BLOCK, LANES), F32), pltpu.VMEM((2 * BLOCK, LANES), F32),
                        pltpu.VMEM((BLOCK, LANES), F32), pltpu.VMEM((BLOCK, LANES), F32)],
        compiler_params=_params(("arbitrary",)),
    )(proj, proj, proj, proj, proj, d_out, lse, delta, posr, posc, posc, sinks)


def _in_bwd(dproj, w_in_t, x, dx1, g1, gp):
    t = x.shape[0]
    tm = 256
    steps = t // tm

    def body(dp_ref, w_ref, x_ref, dx1_ref, g_ref, gp_ref, dx_ref, dg_ref, land_ref, send_sems, recv_sems):
        i = pl.program_id(0)

        @pl.when(i == 0)
        def _():
            dg_ref[...] = jnp.zeros(dg_ref.shape, F32)
            _scatter_start(gp_ref, land_ref, send_sems, recv_sems)

        dh = _dot(dp_ref[...], w_ref[...])
        xv = x_ref[...]
        r = _rms(xv)
        dx, dg = _norm_bwd(dh, xv * r, r, g_ref[...])
        dx_ref[...] = dx1_ref[...] + dx
        dg_ref[...] += dg

        @pl.when(i == steps - 1)
        def _():
            _scatter_wait(gp_ref, land_ref, send_sems, recv_sems)

    row = lambda i: (i, 0)
    blk = pl.BlockSpec((tm, D_MODEL), row)
    return pl.pallas_call(
        body, name="in_bwd", grid=(steps,),
        in_specs=[pl.BlockSpec((tm, D_IN_PAD), row), _full((D_IN_PAD, D_MODEL)), blk, blk, _full((1, D_MODEL)), _HBM],
        out_specs=[blk, _full((1, D_MODEL)), _HBM],
        out_shape=[jax.ShapeDtypeStruct((t, D_MODEL), F32), jax.ShapeDtypeStruct((1, D_MODEL), F32),
                   jax.ShapeDtypeStruct((3,) + gp.shape[1:], gp.dtype)],
        scratch_shapes=[pltpu.SemaphoreType.DMA((3,)), pltpu.SemaphoreType.DMA((3,))],
        compiler_params=_params(("arbitrary",)),
    )(dproj, w_in_t, x, dx1, g1, gp)


def _adamw_store(w, g, m, v, out_refs):
    g_out, d_out, m_out, v_out = out_refs
    m_new = ADAM_B1 * m + (1.0 - ADAM_B1) * g
    v_new = ADAM_B2 * v + (1.0 - ADAM_B2) * jnp.square(g)
    m_hat = m_new / (1.0 - ADAM_B1 ** ADAM_STEP)
    v_hat = v_new / (1.0 - ADAM_B2 ** ADAM_STEP)
    g_out[...] = g
    d_out[...] = -ADAM_LR * (m_hat / (jnp.sqrt(v_hat) + ADAM_EPS) + ADAM_WD * w)
    m_out[...] = m_new
    v_out[...] = v_new


_SMALL_SLOTS = {"pre_norm_mix": (0, 0, D_MODEL), "post_norm_mix": (1, 0, D_MODEL), "pre_norm_mlp": (2, 0, D_MODEL),
                "post_norm_mlp": (3, 0, D_MODEL), "q_a_norm": (4, 0, Q_LORA), "kv_a_norm": (4, Q_LORA, KV_LORA),
                "sinks": (4, Q_LORA + KV_LORA, N_HEADS_A)}
_LOSS_ROW = 5


def _adamw_small(red, w, m, v):
    names = tuple(_SMALL_SLOTS)
    n = len(names)

    def body(*refs):
        red_ref, ws, ms, vs, outs = refs[0], refs[1:1 + n], refs[1 + n:1 + 2 * n], refs[1 + 2 * n:1 + 3 * n], refs[1 + 3 * n:]
        for k, name in enumerate(names):
            row, lane, width = _SMALL_SLOTS[name]
            g = red_ref[row:row + 1, lane:lane + width]
            _adamw_store(ws[k][...], g, ms[k][...], vs[k][...], outs[4 * k:4 * k + 4])

    vmem = pl.BlockSpec(memory_space=pltpu.VMEM)
    res = pl.pallas_call(
        body, name="adamw_small", in_specs=[vmem] * (1 + 3 * n), out_specs=[vmem] * (4 * n),
        out_shape=[jax.ShapeDtypeStruct(w[name].shape, F32) for name in names for _ in range(4)],
    )(red, *[w[k] for k in names], *[m[k] for k in names], *[v[k] for k in names])
    return {name: res[4 * k:4 * k + 4] for k, name in enumerate(names)}


def _adamw(w, g_parts, m, v, name, block, g_row_off=0):
    r, c = w.shape
    br, bc = block
    ng = len(g_parts)

    def body(*refs):
        w_ref, g_refs, m_ref, v_ref = refs[0], refs[1:1 + ng], refs[1 + ng], refs[2 + ng]
        g = g_refs[0][...]
        for gr in g_refs[1:]:
            g = g + gr[...]
        _adamw_store(w_ref[...], g, m_ref[...], v_ref[...], refs[3 + ng:])

    assert g_row_off % br == 0 and r % br == 0 and c % bc == 0
    blk = pl.BlockSpec(block, lambda i, j: (i, j))
    g_blk = pl.BlockSpec(block, lambda i, j: (i + g_row_off // br, j))
    return pl.pallas_call(
        body, name=name, grid=(r // br, c // bc),
        in_specs=[blk] + [g_blk] * ng + [blk, blk], out_specs=[blk] * 4,
        out_shape=[jax.ShapeDtypeStruct((r, c), F32)] * 4,
        compiler_params=_params(("parallel", "parallel")),
    )(w, *g_parts, m, v)


_HBM = pl.BlockSpec(memory_space=pltpu.HBM)


def _other_chips(x, y):
    return ((1 - x, y), (x, 1 - y), (1 - x, 1 - y))


def _gather_copies(src, out, send_sems, recv_sems, local_sem):
    x, y, c = lax.axis_index("x"), lax.axis_index("y"), lax.axis_index("c")
    me = 2 * x + y
    local = pltpu.make_async_copy(src, out.at[me], local_sem)

    def copies(arriving):
        return [pltpu.make_async_remote_copy(src_ref=src, dst_ref=out.at[2 * px + py if arriving else me],
                                             send_sem=send_sems.at[j], recv_sem=recv_sems.at[j], device_id=(px, py, c),
                                             device_id_type=MESH)
                for j, (px, py) in enumerate(_other_chips(x, y))]

    return local, copies


def _gather_start(src, out, send_sems, recv_sems, local_sem):
    local, copies = _gather_copies(src, out, send_sems, recv_sems, local_sem)
    local.start()
    for cp in copies(False):
        cp.start()


def _gather_wait(src, out, send_sems, recv_sems, local_sem):
    local, copies = _gather_copies(src, out, send_sems, recv_sems, local_sem)
    for cp in copies(True):
        cp.wait_recv()
    for cp in copies(False):
        cp.wait_send()
    local.wait()


def _scatter_copies(src, land, send_sems, recv_sems):
    x, y, c = lax.axis_index("x"), lax.axis_index("y"), lax.axis_index("c")
    return [pltpu.make_async_remote_copy(src_ref=src.at[2 * px + py], dst_ref=land.at[j], send_sem=send_sems.at[j],
                                         recv_sem=recv_sems.at[j], device_id=(px, py, c), device_id_type=MESH)
            for j, (px, py) in enumerate(_other_chips(x, y))]


def _scatter_start(src, land, send_sems, recv_sems):
    for cp in _scatter_copies(src, land, send_sems, recv_sems):
        cp.start()


def _scatter_wait(src, land, send_sems, recv_sems):
    copies = _scatter_copies(src, land, send_sems, recv_sems)
    for cp in copies:
        cp.wait_recv()
    for cp in copies:
        cp.wait_send()


def _all_gather_chips(packed):
    r = packed.shape[0]
    half = r // 2

    def body(src, out, ici_send, ici_recv, d2d_send, d2d_recv, local_sem):
        x, y, c = lax.axis_index("x"), lax.axis_index("y"), lax.axis_index("c")
        me = 2 * x + y
        mine = pl.ds(pl.multiple_of(c * half, 16), half)
        theirs = pl.ds(pl.multiple_of((1 - c) * half, 16), half)
        chips = _other_chips(x, y)
        local = pltpu.make_async_copy(src, out.at[me], local_sem)
        local.start()
        sends = [pltpu.make_async_remote_copy(src_ref=src.at[mine], dst_ref=out.at[me, mine], send_sem=ici_send.at[j],
                                              recv_sem=ici_recv.at[j], device_id=(px, py, c), device_id_type=MESH)
                 for j, (px, py) in enumerate(chips)]
        for cp in sends:
            cp.start()
        passed = []
        for j, (px, py) in enumerate(chips):
            block = 2 * px + py
            pltpu.make_async_remote_copy(src_ref=src.at[mine], dst_ref=out.at[block, mine], send_sem=ici_send.at[j],
                                         recv_sem=ici_recv.at[j], device_id=(px, py, c), device_id_type=MESH).wait_recv()
            cp = pltpu.make_async_remote_copy(src_ref=out.at[block, mine], dst_ref=out.at[block, mine],
                                              send_sem=d2d_send.at[j], recv_sem=d2d_recv.at[j],
                                              device_id=(x, y, 1 - c), device_id_type=MESH)
            cp.start()
            passed.append(cp)
        for j, (px, py) in enumerate(chips):
            block = 2 * px + py
            pltpu.make_async_remote_copy(src_ref=out.at[block, theirs], dst_ref=out.at[block, theirs],
                                         send_sem=d2d_send.at[j], recv_sem=d2d_recv.at[j],
                                         device_id=(x, y, 1 - c), device_id_type=MESH).wait_recv()
        for cp in sends + passed:
            cp.wait_send()
        local.wait()

    sems = pltpu.SemaphoreType.DMA((3,))
    return pl.pallas_call(
        body, name="ag_weights", in_specs=[_HBM], out_specs=_HBM,
        out_shape=jax.ShapeDtypeStruct((N_CHIPS,) + packed.shape, packed.dtype),
        scratch_shapes=[sems, sems, sems, sems, pltpu.SemaphoreType.DMA(())],
    )(packed)


def _sum4(gp, land, chip, name):
    _, r, w = gp.shape
    tr = 128

    def body(chip_ref, o_ref, l_ref, s_ref):
        s_ref[...] = ((o_ref[0] + l_ref[0].astype(F32)) + l_ref[1].astype(F32)) + l_ref[2].astype(F32)

    return pl.pallas_call(
        body, name=name,
        grid_spec=pltpu.PrefetchScalarGridSpec(
            num_scalar_prefetch=1, grid=(r // tr,),
            in_specs=[pl.BlockSpec((1, tr, w), lambda i, chip_ref: (chip_ref[0], i, 0)),
                      pl.BlockSpec((3, tr, w), lambda i, chip_ref: (0, i, 0))],
            out_specs=pl.BlockSpec((tr, w), lambda i, chip_ref: (i, 0))),
        out_shape=jax.ShapeDtypeStruct((r, w), F32),
        compiler_params=_params(("parallel",)),
    )(chip, gp, land)


def _swap_sibling(s, name):
    def body(src, got, send_sem, recv_sem):
        x, y, c = lax.axis_index("x"), lax.axis_index("y"), lax.axis_index("c")
        cp = pltpu.make_async_remote_copy(src_ref=src, dst_ref=got, send_sem=send_sem, recv_sem=recv_sem,
                                          device_id=(x, y, 1 - c), device_id_type=MESH)
        cp.start()
        cp.wait_recv()
        cp.wait_send()

    return pl.pallas_call(
        body, name=name, in_specs=[_HBM], out_specs=_HBM,
        out_shape=jax.ShapeDtypeStruct(s.shape, s.dtype),
        scratch_shapes=[pltpu.SemaphoreType.DMA(()), pltpu.SemaphoreType.DMA(())],
    )(s)


def _all_reduce_small(dsmall, loss):
    n_dev = 8
    names = tuple(_SMALL_SLOTS)
    shape = (8, D_MODEL)

    def body(*refs):
        parts, loss_ref = refs[:len(names)], refs[len(names)]
        out, src, gath, send_sems, recv_sems = refs[len(names) + 1:]
        x, y, c = lax.axis_index("x"), lax.axis_index("y"), lax.axis_index("c")
        me = 4 * x + 2 * y + c
        src[...] = jnp.zeros(shape, F32)
        for name, part in zip(names, parts):
            row, lane, _ = _SMALL_SLOTS[name]
            src[row:row + 1, lane:lane + part.shape[1]] = part[...]
        src[_LOSS_ROW:_LOSS_ROW + 1, 0:LANES] = loss_ref[...]
        gath[me] = src[...]
        peers = []
        for k in range(1, n_dev):
            px = 1 - x if (k >> 2) & 1 else x
            py = 1 - y if (k >> 1) & 1 else y
            pc = 1 - c if k & 1 else c
            peers.append((px, py, pc))
        sends = []
        for j, peer in enumerate(peers):
            cp = pltpu.make_async_remote_copy(src_ref=src, dst_ref=gath.at[me], send_sem=send_sems.at[j],
                                              recv_sem=recv_sems.at[j], device_id=peer, device_id_type=MESH)
            cp.start()
            sends.append(cp)
        for j, (px, py, pc) in enumerate(peers):
            pltpu.make_async_remote_copy(src_ref=src, dst_ref=gath.at[4 * px + 2 * py + pc], send_sem=send_sems.at[j],
                                         recv_sem=recv_sems.at[j], device_id=(px, py, pc), device_id_type=MESH).wait_recv()
        for cp in sends:
            cp.wait_send()
        acc = gath[0]
        for d in range(1, n_dev):
            acc = acc + gath[d]
        out[...] = acc

    vmem = pl.BlockSpec(memory_space=pltpu.VMEM)
    return pl.pallas_call(
        body, name="ar_small", in_specs=[vmem] * (len(names) + 1), out_specs=vmem,
        out_shape=jax.ShapeDtypeStruct(shape, F32),
        scratch_shapes=[pltpu.VMEM(shape, F32), pltpu.VMEM((n_dev,) + shape, F32),
                        pltpu.SemaphoreType.DMA((n_dev - 1,)), pltpu.SemaphoreType.DMA((n_dev - 1,))],
    )(*[dsmall[k] for k in names], loss)


_W_IN_ROWS = SHARD_SHAPES["w_in"][1]
_KR_ROW = 3200
_KR_PAD_ROW = _KR_BLK * LANES + QK_NOPE


def _shard_rows(name, a):
    return jnp.transpose(a) if name == "w_in" else a.reshape(PACK_ROWS[name], D_MODEL)


def _pack(group, shards, dtype):
    parts = [_shard_rows(n, shards[n]).astype(dtype) for n in group]
    pad = -sum(PACK_ROWS[n] for n in group) % LANES
    if pad:
        parts.append(jnp.zeros((pad, D_MODEL), dtype))
    return jnp.concatenate(parts, axis=0)


def _col_sharded_full(g, name, group):
    r, c = SHARD_SHAPES[name]
    off = _row_offset(group, name)
    blocks = g[:, off:off + PACK_ROWS[name]].reshape(N_CHIPS, r, c)
    return jnp.transpose(blocks, (1, 0, 2)).reshape(r, N_CHIPS * c)


def _col_sharded_blocks(d, name):
    r, c = SHARD_SHAPES[name]
    return jnp.transpose(d.reshape(r, N_CHIPS, c), (1, 0, 2)).reshape(N_CHIPS, PACK_ROWS[name], D_MODEL)


def _weights_a(g):
    dt = g.dtype
    w_in_t = g[:, :_W_IN_ROWS].reshape(N_CHIPS * _W_IN_ROWS, D_MODEL)
    z = lambda n: jnp.zeros((n, D_MODEL), dt)
    w_in_t = jnp.concatenate([w_in_t[:_KR_ROW], z(_KR_PAD_ROW - _KR_ROW), w_in_t[_KR_ROW:],
                              z(D_IN_PAD - _KR_PAD_ROW - QK_ROPE)], axis=0)
    wq = _col_sharded_full(g, "w_q_b", GROUP_A).reshape(Q_LORA, N_HEADS_B, Q_HEAD_B)
    wq_p = jnp.concatenate([wq, jnp.zeros((Q_LORA, N_HEADS_B, HEAD_PAD - Q_HEAD_B), dt)], axis=2).reshape(Q_LORA, MLA_W)
    wkv = _col_sharded_full(g, "w_kv_b", GROUP_A).reshape(KV_LORA, N_HEADS_B, QK_NOPE + V_DIM_B)
    zk = jnp.zeros((KV_LORA, N_HEADS_B, HEAD_PAD - QK_NOPE), dt)
    wk_p = jnp.concatenate([wkv[:, :, :QK_NOPE], zk], axis=2).reshape(KV_LORA, MLA_W)
    wv = wkv[:, :, QK_NOPE:].reshape(KV_LORA, N_HEADS_B * V_DIM_B)
    return dict(w_in=w_in_t, wq=wq_p, wk=wk_p, wv=wv)


def _grad_blocks_a(dw_in_t, dwq_p, dwk_p, dwv):
    dw_in = jnp.concatenate([dw_in_t[:_KR_ROW], dw_in_t[_KR_PAD_ROW:_KR_PAD_ROW + QK_ROPE]], axis=0)
    dwq = dwq_p.reshape(Q_LORA, N_HEADS_B, HEAD_PAD)[:, :, :Q_HEAD_B].reshape(Q_LORA, N_HEADS_B * Q_HEAD_B)
    dwk = dwk_p.reshape(KV_LORA, N_HEADS_B, HEAD_PAD)[:, :, :QK_NOPE]
    dwkv = jnp.concatenate([dwk, dwv.reshape(KV_LORA, N_HEADS_B, V_DIM_B)], axis=2)
    dwkv = dwkv.reshape(KV_LORA, N_HEADS_B * (QK_NOPE + V_DIM_B))
    pad = -sum(PACK_ROWS[n] for n in GROUP_A) % LANES
    return jnp.concatenate([dw_in.reshape(N_CHIPS, _W_IN_ROWS, D_MODEL), _col_sharded_blocks(dwq, "w_q_b"),
                            _col_sharded_blocks(dwkv, "w_kv_b"), jnp.zeros((N_CHIPS, pad, D_MODEL), F32)], axis=1)


def _rope_freq_lanes():
    freqs = ROPE_THETA ** (-jnp.arange(0, QK_ROPE, 2, dtype=F32) / QK_ROPE)
    return jnp.concatenate([jnp.zeros((QK_NOPE,), F32), freqs, freqs,
                            jnp.zeros((HEAD_PAD - Q_HEAD_B,), F32)]).reshape(1, LANES)


def _fwd_bwd(x, positions, target, w):
    t = x.shape[0]
    wa = _weights_a(_all_gather_chips(_pack(GROUP_A, w, BF16)))
    posr = positions.astype(F32).reshape(1, t)
    posc = posr.reshape(t, 1)
    freq = _rope_freq_lanes()
    g1, g2, g3, g4 = w["pre_norm_mix"], w["post_norm_mix"], w["pre_norm_mlp"], w["post_norm_mlp"]
    qan, kvan, sinks = w["q_a_norm"], w["kv_a_norm"], w["sinks"]

    h, proj = _proj_fwd(x, g1, wa["w_in"])
    out_a, lse_a = _swa_fwd(proj, posc, posr, sinks)
    qm, km, qt, kt, vm, vt = _mla_prep_fwd(proj, posc, freq, qan, kvan, wa["wq"], wa["wk"], wa["wv"])
    out_bt, lse_b, wb = _mla_fwd(km, qt, vt, _pack(GROUP_B, w, BF16))
    w_oa, w_ob = _col_sharded_full(wb, "w_o_a", GROUP_B), _col_sharded_full(wb, "w_o_b", GROUP_B)
    merged, y, x1, h2 = _mix_out_fwd(out_a, out_bt, proj, x, w_oa, w_ob, wb, g2, g3)
    a = _up_fwd(h2, wb)
    dx2, dyd, dg4, loss = _down_fwd_loss(a, wb, x1, target, g4)

    gp_b = _dw_into_blocks(a, dyd, "w_down", 1024, 512)
    du = _down_bwd(dyd, wb, a)
    gp_b = _dw_into_blocks(h2, du, "w_up", 1024, 512, gp_b)
    dx1, dy, dg3, dg2 = _up_bwd(du, wb, x1, dx2, y, g3, g2)
    gp_b = _dw_into_blocks(merged, dy, "w_out", 256, 1024, gp_b)
    doa, dob, dga, dgb, d_out_a, d_out_b, d_out_bt, del_a, del_b = _mix_out_bwd(dy, out_a, out_bt, proj, w_oa, w_ob, wb)
    dw_oa = _matmul_tn(out_a, doa, "dw_o_a", 512, 1024)
    dw_ob = _dw_ob(out_bt, dob)
    small_b = jnp.concatenate([_col_sharded_blocks(dw_oa, "w_o_a"), _col_sharded_blocks(dw_ob, "w_o_b")], axis=1)
    gp_b = lax.dynamic_update_slice(gp_b, small_b, (0, _row_offset(GROUP_B, "w_o_a"), 0))
    dqm, dkm, dvm, land_b = _mla_bwd(qm, km, qt, kt, vm, d_out_b, d_out_bt, lse_b, del_b, gp_b)
    dcq, dckv, dkr, dwq, dwk, dwv, dqan, dkvan = _mla_prep_bwd(
        dqm, dkm, dvm, proj, posc, freq, qan, kvan, wa["wq"], wa["wk"], wa["wv"])
    dqa, dka, dva, dsinks = _swa_bwd(proj, d_out_a, lse_a, del_a, posc, posr, sinks)
    dproj = jnp.concatenate([dga, dgb, dqa.astype(BF16), dka.astype(BF16), dva.astype(BF16), dcq, dckv, dkr], axis=1)
    dw_in_t = _matmul_tn(dproj, h, "dw_in", D_IN_PAD // 2, 1024, tk=512)
    gp_a = _grad_blocks_a(dw_in_t, dwq, dwk, dwv)
    grad_x, dg1, land_a = _in_bwd(dproj, wa["w_in"], x, dx1, g1, gp_a.astype(BF16))

    dsmall = dict(pre_norm_mix=dg1, post_norm_mix=dg2, pre_norm_mlp=dg3, post_norm_mlp=dg4,
                  q_a_norm=dqan, kv_a_norm=dkvan, sinks=dsinks)
    return loss, grad_x, {GROUP_A: (gp_a, land_a), GROUP_B: (gp_b, land_b)}, dsmall


def kernel(x, positions, pre_norm_mix, w_in, q_a_norm, w_q_b, kv_a_norm, w_kv_b, sinks, w_o_a, w_o_b, w_out, post_norm_mix, pre_norm_mlp, w_up, w_down, post_norm_mlp, loss_target, m_pre_norm_mix, m_w_in, m_q_a_norm, m_w_q_b, m_kv_a_norm, m_w_kv_b, m_sinks, m_w_o_a, m_w_o_b, m_w_out, m_post_norm_mix, m_pre_norm_mlp, m_w_up, m_w_down, m_post_norm_mlp, v_pre_norm_mix, v_w_in, v_q_a_norm, v_w_q_b, v_kv_a_norm, v_w_kv_b, v_sinks, v_w_o_a, v_w_o_b, v_w_out, v_post_norm_mix, v_pre_norm_mlp, v_w_up, v_w_down, v_post_norm_mlp):
    w = dict(pre_norm_mix=pre_norm_mix, w_in=w_in[0], q_a_norm=q_a_norm, w_q_b=w_q_b[0], kv_a_norm=kv_a_norm,
             w_kv_b=w_kv_b[0], sinks=sinks, w_o_a=w_o_a[0], w_o_b=w_o_b[0], w_out=w_out[0],
             post_norm_mix=post_norm_mix, pre_norm_mlp=pre_norm_mlp, w_up=w_up[0], w_down=w_down[0],
             post_norm_mlp=post_norm_mlp)
    m = dict(pre_norm_mix=m_pre_norm_mix, w_in=m_w_in[0], q_a_norm=m_q_a_norm, w_q_b=m_w_q_b[0],
             kv_a_norm=m_kv_a_norm, w_kv_b=m_w_kv_b[0], sinks=m_sinks, w_o_a=m_w_o_a[0], w_o_b=m_w_o_b[0],
             w_out=m_w_out[0], post_norm_mix=m_post_norm_mix, pre_norm_mlp=m_pre_norm_mlp, w_up=m_w_up[0],
             w_down=m_w_down[0], post_norm_mlp=m_post_norm_mlp)
    v = dict(pre_norm_mix=v_pre_norm_mix, w_in=v_w_in[0], q_a_norm=v_q_a_norm, w_q_b=v_w_q_b[0],
             kv_a_norm=v_kv_a_norm, w_kv_b=v_w_kv_b[0], sinks=v_sinks, w_o_a=v_w_o_a[0], w_o_b=v_w_o_b[0],
             w_out=v_w_out[0], post_norm_mix=v_post_norm_mix, pre_norm_mlp=v_pre_norm_mlp, w_up=v_w_up[0],
             w_down=v_w_down[0], post_norm_mlp=v_post_norm_mlp)

    loss, grad_x, blocks, dsmall = _fwd_bwd(x[0], positions, loss_target[0], w)

    red = _all_reduce_small(dsmall, loss)
    small = _adamw_small(red, w, m, v)

    chip = (2 * lax.axis_index("x") + lax.axis_index("y")).astype(jnp.int32).reshape(1)
    reduced = {}
    for group, tag in ((GROUP_A, "a"), (GROUP_B, "b")):
        gp, land = blocks[group]
        part = _sum4(gp, land, chip, "rs_sum_" + tag)
        reduced[group] = [part, _swap_sibling(part, "rs_swap_" + tag)]

    big = {}
    tr = jnp.transpose
    big["w_in"] = [tr(o)[None] for o in _adamw(tr(w["w_in"]), reduced[GROUP_A], tr(m["w_in"]), tr(v["w_in"]),
                                               "adamw_w_in", (_W_IN_ROWS, 256))]
    for n in ("w_up", "w_down", "w_out"):
        big[n] = [o[None] for o in _adamw(w[n], reduced[GROUP_B], m[n], v[n], "adamw_" + n, (128, D_MODEL),
                                          _row_offset(GROUP_B, n))]
    for group, names in ((GROUP_A, ("w_q_b", "w_kv_b")), (GROUP_B, ("w_o_a", "w_o_b"))):
        for n in names:
            off = _row_offset(group, n)
            g_parts = [p[off:off + PACK_ROWS[n]].reshape(SHARD_SHAPES[n]) for p in reduced[group]]
            big[n] = [o[None] for o in _adamw(w[n], g_parts, m[n], v[n], "adamw_" + n, SHARD_SHAPES[n])]

    outs = [big[n][k] if n in big else small[n][k] for k in range(4) for n in WEIGHTS]
    return (red[_LOSS_ROW, 0], grad_x[None], *outs)
```

```python
import jax
import jax.numpy as jnp
from jax import lax
from jax.experimental import pallas as pl
from jax.experimental.pallas import tpu as pltpu

F32 = jnp.float32
BF16 = jnp.bfloat16
MESH = pl.DeviceIdType.MESH

D_MODEL = 1024
N_HEADS_A = 8
N_KV_A = 2
HEAD_DIM_A = 64
WINDOW = 128
BLOCK = 128
N_HEADS_B = 8
QK_NOPE = 64
QK_ROPE = 32
V_DIM_B = 64
Q_LORA = 256
KV_LORA = 128
ROPE_THETA = 10000.0
D_FF = 4 * D_MODEL
EPS = 1e-6
WIDTH_A = N_HEADS_A * HEAD_DIM_A
Q_HEAD_B = QK_NOPE + QK_ROPE
D_IN_PAD = 3328
HEAD_PAD = 128
MLA_W = N_HEADS_B * HEAD_PAD

ADAM_LR = 0.001
ADAM_B1 = 0.9
ADAM_B2 = 0.999
ADAM_EPS = 1e-08
ADAM_WD = 0.01
ADAM_STEP = 10

NEG = -1e30
N_CHIPS = 4
LANES = 128
VMEM_LIMIT = 56 * 1024 * 1024

SHARD_SHAPES = {"w_in": (1024, 808), "w_q_b": (256, 192), "w_kv_b": (128, 256), "w_o_a": (512, 256),
                "w_o_b": (512, 256), "w_out": (256, 1024), "w_up": (1024, 1024), "w_down": (1024, 1024)}
PACK_ROWS = {n: (s[0] * s[1]) // D_MODEL for n, s in SHARD_SHAPES.items()}
GROUP_A = ("w_in", "w_q_b", "w_kv_b")
GROUP_B = ("w_up", "w_down", "w_out", "w_o_a", "w_o_b")
WEIGHTS = ("pre_norm_mix", "w_in", "q_a_norm", "w_q_b", "kv_a_norm", "w_kv_b", "sinks", "w_o_a", "w_o_b", "w_out",
           "post_norm_mix", "pre_norm_mlp", "w_up", "w_down", "post_norm_mlp")


def _params(sem=None):
    return pltpu.CompilerParams(dimension_semantics=sem, vmem_limit_bytes=VMEM_LIMIT)


def _dot(a, b):
    return jnp.dot(a, b, preferred_element_type=F32)


def _dot_nt(a, b):
    return lax.dot_general(a, b, (((1,), (1,)), ((), ())), preferred_element_type=F32)


def _dot_tn(a, b):
    return lax.dot_general(a, b, (((0,), (0,)), ((), ())), preferred_element_type=F32)


def _rms(v):
    return lax.rsqrt(jnp.mean(v * v, axis=-1, keepdims=True) + EPS)


def _norm_bwd(dout, n, r, g):
    dn = dout * g
    dx = r * (dn - n * jnp.mean(dn * n, axis=-1, keepdims=True))
    return dx, jnp.sum(dout * n, axis=0, keepdims=True)


def _full(shape):
    return pl.BlockSpec(shape, lambda *_: (0,) * len(shape))


def _row_offset(group, name):
    return sum(PACK_ROWS[n] for n in group[:group.index(name)])


def _wb_spec(name):
    rows = PACK_ROWS[name]
    return pl.BlockSpec((N_CHIPS, rows, D_MODEL), lambda *_: (0, _row_offset(GROUP_B, name) // rows, 0))


def _proj_fwd(x, g1, w_in_t):
    t = x.shape[0]
    tm = 256

    def body(x_ref, g_ref, w_ref, h_ref, p_ref):
        xv = x_ref[...]
        h = ((xv * _rms(xv)) * g_ref[...]).astype(BF16)
        h_ref[...] = h
        p_ref[...] = _dot_nt(h, w_ref[...])

    return pl.pallas_call(
        body, name="proj_fwd", grid=(t // tm,),
        in_specs=[pl.BlockSpec((tm, D_MODEL), lambda i: (i, 0)), _full((1, D_MODEL)), _full((D_IN_PAD, D_MODEL))],
        out_specs=[pl.BlockSpec((tm, D_MODEL), lambda i: (i, 0)), pl.BlockSpec((tm, D_IN_PAD), lambda i: (i, 0))],
        out_shape=[jax.ShapeDtypeStruct((t, D_MODEL), BF16), jax.ShapeDtypeStruct((t, D_IN_PAD), F32)],
        compiler_params=_params(("parallel",)),
    )(x, g1, w_in_t)


_QA_BLK = 2048 // WIDTH_A
_KA_BLK = 2560 // LANES
_VA_BLK = 2688 // LANES
_CQ_BLK = 2816 // Q_LORA
_CKV_BLK = 3072 // LANES
_KR_BLK = 3200 // LANES


_GROUP_A = N_HEADS_A // N_KV_A
_SWA_SCALE = HEAD_DIM_A ** -0.5
_LOG2E = 1.4426950408889634


def _head_cols(v, h):
    return v[:, HEAD_DIM_A * h:HEAD_DIM_A * (h + 1)]


def _head_rows(v, h):
    return v[HEAD_DIM_A * h:HEAD_DIM_A * (h + 1), :]


def _swa_band(n, kp_ref, kc_ref, vp_ref, vc_ref, pq_ref, pp_ref, pc_ref):
    kb = jnp.concatenate([kp_ref[...], kc_ref[...]], axis=0)
    vb = jnp.concatenate([vp_ref[...], vc_ref[...]], axis=0)
    posk = jnp.concatenate([pp_ref[...], pc_ref[...]], axis=0)
    dist = jnp.abs(posk - pq_ref[...])
    ki = lax.broadcasted_iota(jnp.int32, (2 * BLOCK, BLOCK), 0)
    qi = lax.broadcasted_iota(jnp.int32, (2 * BLOCK, BLOCK), 1)
    valid = (ki > qi) & (ki <= qi + WINDOW) & ((n > 0) | (ki >= BLOCK))
    return kb, vb, dist, valid


def _swa_scores_t(st_g, j, h, dist, valid):
    slope = 2.0 ** (-8.0 * (h + 1) / N_HEADS_A)
    st = st_g[:, BLOCK * j:BLOCK * (j + 1)] * (_SWA_SCALE * _LOG2E) - (slope * _LOG2E) * dist
    return jnp.where(valid, st, NEG)


def _group_t(xt, kh):
    return jnp.concatenate([_head_rows(xt, _GROUP_A * kh + j) for j in range(_GROUP_A)], axis=1).astype(BF16)


def _swa_fwd(proj, posc, posr, sinks):
    t = proj.shape[0]
    nb = t // BLOCK

    def body(q_ref, kc_ref, kp_ref, vc_ref, vp_ref, pq_ref, pc_ref, pp_ref, sink_ref, o_ref, l_ref):
        n = pl.program_id(0)
        kb, vb, dist, valid = _swa_band(n, kp_ref, kc_ref, vp_ref, vc_ref, pq_ref, pp_ref, pc_ref)
        q_t, vb_t = q_ref[...].T, vb.T
        out_t, lse = [], []
        for kh in range(N_KV_A):
            st_g = _dot(_head_cols(kb, kh).astype(BF16), _group_t(q_t, kh))
            ps = []
            for j in range(_GROUP_A):
                h = _GROUP_A * kh + j
                st = _swa_scores_t(st_g, j, h, dist, valid)
                sink = sink_ref[0:1, h:h + 1] * _LOG2E
                m = jnp.maximum(jnp.max(st, axis=0, keepdims=True), sink)
                e = jnp.exp2(st - m)
                den = jnp.sum(e, axis=0, keepdims=True) + jnp.exp2(sink - m)
                ps.append((e * (1.0 / den)).astype(BF16))
                lse.append(m + jnp.log(den) * _LOG2E)
            o_g = _dot(_head_rows(vb_t, kh).astype(BF16), jnp.concatenate(ps, axis=1))
            out_t.extend(o_g[:, BLOCK * j:BLOCK * (j + 1)] for j in range(_GROUP_A))
        o_ref[...] = jnp.concatenate(out_t, axis=0).T
        l_ref[...] = jnp.concatenate(lse, axis=0)

    cur = lambda n: (n, 0)
    prev = lambda n: jnp.maximum(n - 1, 0)
    return pl.pallas_call(
        body, name="swa_fwd", grid=(nb,),
        in_specs=[pl.BlockSpec((BLOCK, WIDTH_A), lambda n: (n, _QA_BLK)),
                  pl.BlockSpec((BLOCK, LANES), lambda n: (n, _KA_BLK)),
                  pl.BlockSpec((BLOCK, LANES), lambda n: (prev(n), _KA_BLK)),
                  pl.BlockSpec((BLOCK, LANES), lambda n: (n, _VA_BLK)),
                  pl.BlockSpec((BLOCK, LANES), lambda n: (prev(n), _VA_BLK)),
                  pl.BlockSpec((1, BLOCK), lambda n: (0, n)),
                  pl.BlockSpec((BLOCK, 1), cur),
                  pl.BlockSpec((BLOCK, 1), lambda n: (prev(n), 0)),
                  _full((1, N_HEADS_A))],
        out_specs=[pl.BlockSpec((BLOCK, WIDTH_A), cur), pl.BlockSpec((N_HEADS_A, BLOCK), lambda n: (0, n))],
        out_shape=[jax.ShapeDtypeStruct((t, WIDTH_A), F32), jax.ShapeDtypeStruct((N_HEADS_A, t), F32)],
        compiler_params=_params(("parallel",)),
    )(proj, proj, proj, proj, proj, posr, posc, posc, sinks)


def _rope_coeffs(pos, freq):
    ang = pos * freq
    cosv, sinv = jnp.cos(ang), jnp.sin(ang)
    lane = lax.broadcasted_iota(jnp.int32, ang.shape, 1)
    lo = (lane >= QK_NOPE) & (lane < QK_NOPE + QK_ROPE // 2)
    hi = (lane >= QK_NOPE + QK_ROPE // 2) & (lane < QK_NOPE + QK_ROPE)
    c = jnp.where(lane < QK_NOPE, 1.0, jnp.where(lo | hi, cosv, 0.0))
    s = jnp.where(lo, -sinv, jnp.where(hi, sinv, 0.0))
    return c, s, lo, hi


def _rope(xh, c, s, lo):
    up = pltpu.roll(xh, LANES - QK_ROPE // 2, axis=1)
    dn = pltpu.roll(xh, QK_ROPE // 2, axis=1)
    return xh * c + jnp.where(lo, up, dn) * s


def _unrope(dh, c, s, lo, hi):
    g = dh * s
    up = pltpu.roll(g, LANES - QK_ROPE // 2, axis=1)
    dn = pltpu.roll(g, QK_ROPE // 2, axis=1)
    return dh * c + jnp.where(hi, dn, jnp.where(lo, up, 0.0))


_TQ = 512
_MLA_SCALE = Q_HEAD_B ** -0.5


def _mla_prep_fwd(proj, posc, freq, qan, kvan, wq, wk, wv):
    t = proj.shape[0]
    tm = _TQ
    nb = t // tm

    def body(cq_ref, ckv_ref, kr_ref, pos_ref, f_ref, qan_ref, kvan_ref, wq_ref, wk_ref, wv_ref,
             q_ref, k_ref, qt_ref, kt_ref, v_ref, vt_ref):
        cq = cq_ref[...]
        cqn = ((cq * _rms(cq)) * qan_ref[...]).astype(BF16)
        ckv = ckv_ref[...]
        ckvn = ((ckv * _rms(ckv)) * kvan_ref[...]).astype(BF16)
        qb = _dot(cqn, wq_ref[...])
        kb = _dot(ckvn, wk_ref[...])
        vb = _dot(ckvn, wv_ref[...])
        vbt = vb.T
        c, s, lo, _ = _rope_coeffs(pos_ref[...], f_ref[...])
        kr = _rope(kr_ref[...], c, s, lo)
        for h in range(N_HEADS_B):
            sl = slice(HEAD_PAD * h, HEAD_PAD * (h + 1))
            q_h = _rope(qb[:, sl], c, s, lo)
            k_h = kb[:, sl] + kr
            q_ref[:, sl] = q_h.astype(BF16)
            k_ref[:, sl] = k_h.astype(BF16)
            qt_ref[h, 0] = q_h.T.astype(BF16)
            kt_ref[h, 0] = k_h.T.astype(BF16)
            v_ref[h] = vb[:, V_DIM_B * h:V_DIM_B * (h + 1)].astype(BF16)
            vt_ref[h, 0] = vbt[V_DIM_B * h:V_DIM_B * (h + 1), :].astype(BF16)

    row = lambda i: (i, 0)
    blk4 = lambda d: pl.BlockSpec((N_HEADS_B, 1, d, tm), lambda i: (0, i, 0, 0))
    return pl.pallas_call(
        body, name="mla_prep_fwd", grid=(nb,),
        in_specs=[pl.BlockSpec((tm, Q_LORA), lambda i: (i, _CQ_BLK)),
                  pl.BlockSpec((tm, LANES), lambda i: (i, _CKV_BLK)),
                  pl.BlockSpec((tm, LANES), lambda i: (i, _KR_BLK)),
                  pl.BlockSpec((tm, 1), row), _full((1, LANES)), _full((1, Q_LORA)), _full((1, KV_LORA)),
                  _full((Q_LORA, MLA_W)), _full((KV_LORA, MLA_W)), _full((KV_LORA, N_HEADS_B * V_DIM_B))],
        out_specs=[pl.BlockSpec((tm, MLA_W), row), pl.BlockSpec((tm, MLA_W), row), blk4(HEAD_PAD), blk4(HEAD_PAD),
                   pl.BlockSpec((N_HEADS_B, tm, V_DIM_B), lambda i: (0, i, 0)), blk4(V_DIM_B)],
        out_shape=[jax.ShapeDtypeStruct((t, MLA_W), BF16), jax.ShapeDtypeStruct((t, MLA_W), BF16),
                   jax.ShapeDtypeStruct((N_HEADS_B, nb, HEAD_PAD, tm), BF16),
                   jax.ShapeDtypeStruct((N_HEADS_B, nb, HEAD_PAD, tm), BF16),
                   jax.ShapeDtypeStruct((N_HEADS_B, t, V_DIM_B), BF16),
                   jax.ShapeDtypeStruct((N_HEADS_B, nb, V_DIM_B, tm), BF16)],
        compiler_params=_params(("parallel",)),
    )(proj, proj, proj, posc, freq, qan, kvan, wq, wk, wv)


_MLA_SCALE2 = _MLA_SCALE * _LOG2E


def _scores_t(k, qt, diagonal):
    st = _dot(k, qt) * _MLA_SCALE2
    if diagonal:
        key = lax.broadcasted_iota(jnp.int32, st.shape, 0)
        qry = lax.broadcasted_iota(jnp.int32, st.shape, 1)
        st = jnp.where(key <= qry, st, NEG)
    return st


def _mla_fwd(k, qt, vt, w_src):
    t = k.shape[0]
    nb = t // _TQ

    def body(k_ref, qt_ref, vt_ref, w_ref, o_ref, l_ref, wg_ref, raw_a, raw_b, send_sems, recv_sems, local_sem):
        qi = pl.program_id(1)
        first = (pl.program_id(0) == 0) & (qi == 0)
        last = (pl.program_id(0) == N_HEADS_B - 1) & (qi == nb - 1)

        @pl.when(first)
        def _():
            _gather_start(w_ref, wg_ref, send_sems, recv_sems, local_sem)

        q_t = qt_ref[0, 0]

        def product(kj):
            return _dot(k_ref[pl.ds(pl.multiple_of(kj * _TQ, _TQ), _TQ), :], q_t)

        def update(stats, raw_ref, kj, diagonal=False):
            m, l, acc = stats
            raw = raw_ref[...]
            if diagonal:
                key = lax.broadcasted_iota(jnp.int32, raw.shape, 0)
                qry = lax.broadcasted_iota(jnp.int32, raw.shape, 1)
                raw = jnp.where(key <= qry, raw, NEG)
            m_new = jnp.maximum(m, jnp.max(raw, axis=0, keepdims=True) * _MLA_SCALE2)
            alpha = jnp.exp2(m - m_new)
            p = jnp.exp2(raw * _MLA_SCALE2 - m_new)
            l = alpha * l + jnp.sum(p, axis=0, keepdims=True)
            acc = alpha * acc + _dot(vt_ref[0, kj], p.astype(BF16))
            return m_new, l, acc

        def trip(i, stats):
            raw_b[...] = product(2 * i + 1)
            stats = update(stats, raw_a, 2 * i)
            raw_a[...] = product(2 * i + 2)
            return update(stats, raw_b, 2 * i + 1)

        def tail_even(stats):
            return update(stats, raw_a, qi, True)

        def tail_odd(stats):
            raw_b[...] = product(qi)
            return update(update(stats, raw_a, qi - 1), raw_b, qi, True)

        init = (jnp.full((1, _TQ), NEG, F32), jnp.zeros((1, _TQ), F32), jnp.zeros((V_DIM_B, _TQ), F32))
        raw_a[...] = product(0)
        stats = lax.fori_loop(0, qi // 2, trip, init)
        m, l, acc = lax.cond(qi % 2 == 0, tail_even, tail_odd, stats)
        o_ref[0, 0] = acc / l
        l_ref[0, 0] = m + jnp.log(l) * _LOG2E

        @pl.when(last)
        def _():
            _gather_wait(w_ref, wg_ref, send_sems, recv_sems, local_sem)

    return pl.pallas_call(
        body, name="mla_fwd", grid=(N_HEADS_B, nb),
        in_specs=[pl.BlockSpec((t, HEAD_PAD), lambda h, qi: (0, h)),
                  pl.BlockSpec((1, 1, HEAD_PAD, _TQ), lambda h, qi: (h, qi, 0, 0)),
                  pl.BlockSpec((1, nb, V_DIM_B, _TQ), lambda h, qi: (h, 0, 0, 0)), _HBM],
        out_specs=[pl.BlockSpec((1, 1, V_DIM_B, _TQ), lambda h, qi: (h, qi, 0, 0)),
                   pl.BlockSpec((1, 1, 1, _TQ), lambda h, qi: (h, qi, 0, 0)), _HBM],
        out_shape=[jax.ShapeDtypeStruct((N_HEADS_B, nb, V_DIM_B, _TQ), F32),
                   jax.ShapeDtypeStruct((N_HEADS_B, nb, 1, _TQ), F32),
                   jax.ShapeDtypeStruct((N_CHIPS,) + w_src.shape, w_src.dtype)],
        scratch_shapes=[pltpu.VMEM((_TQ, _TQ), F32), pltpu.VMEM((_TQ, _TQ), F32),
                        pltpu.SemaphoreType.DMA((3,)), pltpu.SemaphoreType.DMA((3,)), pltpu.SemaphoreType.DMA(())],
        compiler_params=_params(("arbitrary", "arbitrary")),
    )(k, qt, vt, w_src)


def _ot_spec(tm, d):
    per = _TQ // tm
    return pl.BlockSpec((N_HEADS_B, 1, d, tm), lambda i: (0, i // per, 0, i % per))


def _mix_out_fwd(out_a, out_bt, proj, x, w_oa, w_ob, wb, g2, g3):
    t = x.shape[0]
    tm = 256

    def body(oa_ref, obt_ref, ga_ref, gb_ref, x_ref, woa_ref, wob_ref, wout_ref, g2_ref, g3_ref,
             mg_ref, y_ref, x1_ref, h2_ref):
        oa = _dot(oa_ref[...].astype(BF16), woa_ref[...])
        obt = obt_ref[...].reshape(N_HEADS_B * V_DIM_B, tm).astype(BF16)
        ob = _dot_tn(obt, wob_ref[...])
        merged = (jax.nn.sigmoid(ga_ref[...]) * oa + jax.nn.sigmoid(gb_ref[...]) * ob).astype(BF16)
        mg_ref[...] = merged
        y = _dot(merged, wout_ref[...].reshape(D_MODEL, D_MODEL))
        y_ref[...] = y
        x1 = x_ref[...] + (y * _rms(y)) * g2_ref[...]
        x1_ref[...] = x1
        h2_ref[...] = ((x1 * _rms(x1)) * g3_ref[...]).astype(BF16)

    row = lambda i: (i, 0)
    blk = pl.BlockSpec((tm, D_MODEL), row)
    return pl.pallas_call(
        body, name="mix_out_fwd", grid=(t // tm,),
        in_specs=[pl.BlockSpec((tm, WIDTH_A), row), _ot_spec(tm, V_DIM_B), pl.BlockSpec((tm, D_MODEL), lambda i: (i, 0)),
                  pl.BlockSpec((tm, D_MODEL), lambda i: (i, 1)), blk,
                  _full((WIDTH_A, D_MODEL)), _full((N_HEADS_B * V_DIM_B, D_MODEL)), _wb_spec("w_out"),
                  _full((1, D_MODEL)), _full((1, D_MODEL))],
        out_specs=[blk, blk, blk, blk],
        out_shape=[jax.ShapeDtypeStruct((t, D_MODEL), BF16), jax.ShapeDtypeStruct((t, D_MODEL), F32),
                   jax.ShapeDtypeStruct((t, D_MODEL), F32), jax.ShapeDtypeStruct((t, D_MODEL), BF16)],
        compiler_params=_params(("parallel",)),
    )(out_a, out_bt, proj, proj, x, w_oa, w_ob, wb, g2, g3)


_TM_MLP = 512


def _up_fwd(h2, wb):
    t = h2.shape[0]
    tm = _TM_MLP

    def body(h_ref, w_ref, a_ref):
        hv = h_ref[...]
        for j in range(N_CHIPS):
            u = _dot(hv, w_ref[j])
            a_ref[:, D_MODEL * j:D_MODEL * (j + 1)] = jnp.square(jnp.maximum(u, 0.0)).astype(BF16)

    return pl.pallas_call(
        body, name="up_fwd", grid=(t // tm,),
        in_specs=[pl.BlockSpec((tm, D_MODEL), lambda i: (i, 0)), _wb_spec("w_up")],
        out_specs=pl.BlockSpec((tm, D_FF), lambda i: (i, 0)),
        out_shape=jax.ShapeDtypeStruct((t, D_FF), BF16),
        compiler_params=_params(("parallel",)),
    )(h2, wb)


def _down_fwd_loss(a, wb, x1, target, g4):
    t = a.shape[0]
    tm = _TM_MLP

    def body(a_ref, w_ref, x1_ref, tg_ref, g_ref, dx2_ref, dyd_ref, dg_ref, loss_ref):
        @pl.when(pl.program_id(0) == 0)
        def _():
            dg_ref[...] = jnp.zeros(dg_ref.shape, F32)
            loss_ref[...] = jnp.zeros(loss_ref.shape, F32)

        yd = _dot(a_ref[...], w_ref[...].reshape(D_FF, D_MODEL))
        r = _rms(yd)
        n = yd * r
        diff = (x1_ref[...] + n * g_ref[...]) - tg_ref[...]
        loss_ref[...] += 0.5 * jnp.sum(jnp.mean(diff * diff, axis=-1, keepdims=True), axis=0, keepdims=True)
        dx2 = diff * (1.0 / D_MODEL)
        dx2_ref[...] = dx2
        dyd, dg = _norm_bwd(dx2, n, r, g_ref[...])
        dyd_ref[...] = dyd.astype(BF16)
        dg_ref[...] += dg

    row = lambda i: (i, 0)
    blk = pl.BlockSpec((tm, D_MODEL), row)
    return pl.pallas_call(
        body, name="down_fwd_loss", grid=(t // tm,),
        in_specs=[pl.BlockSpec((tm, D_FF), row), _wb_spec("w_down"), blk, blk, _full((1, D_MODEL))],
        out_specs=[blk, blk, _full((1, D_MODEL)), _full((1, LANES))],
        out_shape=[jax.ShapeDtypeStruct((t, D_MODEL), F32), jax.ShapeDtypeStruct((t, D_MODEL), BF16),
                   jax.ShapeDtypeStruct((1, D_MODEL), F32), jax.ShapeDtypeStruct((1, LANES), F32)],
        compiler_params=_params(("arbitrary",)),
    )(a, wb, x1, target, g4)


def _matmul_tn(a, b, name, tm, tn, tk=1024):
    t, m = a.shape
    n = b.shape[1]
    tk = min(tk, t)
    nk = t // tk

    def body(a_ref, b_ref, o_ref):
        @pl.when(pl.program_id(2) == 0)
        def _():
            o_ref[...] = jnp.zeros(o_ref.shape, F32)

        o_ref[...] += _dot_tn(a_ref[...].astype(BF16), b_ref[...].astype(BF16))

    return pl.pallas_call(
        body, name=name, grid=(m // tm, n // tn, nk),
        in_specs=[pl.BlockSpec((tk, tm), lambda i, j, k: (k, i)), pl.BlockSpec((tk, tn), lambda i, j, k: (k, j))],
        out_specs=pl.BlockSpec((tm, tn), lambda i, j, k: (i, j)),
        out_shape=jax.ShapeDtypeStruct((m, n), F32),
        compiler_params=_params(("parallel", "parallel", "arbitrary")),
    )(a, b)


def _dw_into_blocks(a, b, weight, tm, tk, buf=None):
    t, m = a.shape
    n = b.shape[1]
    nk = t // tk
    rows = PACK_ROWS[weight]
    first = _row_offset(GROUP_B, weight) // tm
    per_chip = rows // tm
    if weight == "w_up":
        out_map = lambda i, j, k: (j, first + i, 0)
    else:
        out_map = lambda i, j, k: (i // per_chip, first + i % per_chip, 0)

    def body(a_ref, b_ref, *rest):
        o_ref = rest[-1]

        @pl.when(pl.program_id(2) == 0)
        def _():
            o_ref[...] = jnp.zeros(o_ref.shape, F32)

        o_ref[...] += _dot_tn(a_ref[...].astype(BF16), b_ref[...].astype(BF16))[None]

    in_specs = [pl.BlockSpec((tk, tm), lambda i, j, k: (k, i)), pl.BlockSpec((tk, D_MODEL), lambda i, j, k: (k, j))]
    operands = [a, b]
    if buf is not None:
        in_specs.append(pl.BlockSpec(memory_space=pl.ANY))
        operands.append(buf)
    total = sum(PACK_ROWS[w] for w in GROUP_B)
    return pl.pallas_call(
        body, name="dw_" + weight[2:], grid=(m // tm, n // D_MODEL, nk),
        in_specs=in_specs, out_specs=pl.BlockSpec((1, tm, D_MODEL), out_map),
        out_shape=jax.ShapeDtypeStruct((N_CHIPS, total, D_MODEL), F32),
        input_output_aliases={} if buf is None else {2: 0},
        compiler_params=_params(("parallel", "parallel", "arbitrary")),
    )(*operands)


def _down_bwd(dyd, wb, a):
    t = dyd.shape[0]
    tm = _TM_MLP

    def body(d_ref, w_ref, a_ref, du_ref):
        da = _dot_nt(d_ref[...], w_ref[...].reshape(D_FF, D_MODEL))
        du_ref[...] = (da * (2.0 * jnp.sqrt(a_ref[...].astype(F32)))).astype(BF16)

    row = lambda i: (i, 0)
    return pl.pallas_call(
        body, name="down_bwd", grid=(t // tm,),
        in_specs=[pl.BlockSpec((tm, D_MODEL), row), _wb_spec("w_down"), pl.BlockSpec((tm, D_FF), row)],
        out_specs=pl.BlockSpec((tm, D_FF), row),
        out_shape=jax.ShapeDtypeStruct((t, D_FF), BF16),
        compiler_params=_params(("parallel",)),
    )(dyd, wb, a)


def _up_bwd(du, wb, x1, dx2, y, g3, g2):
    t = du.shape[0]
    tm = _TM_MLP

    def body(du_ref, w_ref, x1_ref, dx2_ref, y_ref, g3_ref, g2_ref, dx1_ref, dy_ref, dg3_ref, dg2_ref):
        @pl.when(pl.program_id(0) == 0)
        def _():
            dg3_ref[...] = jnp.zeros(dg3_ref.shape, F32)
            dg2_ref[...] = jnp.zeros(dg2_ref.shape, F32)

        dh2 = _dot_nt(du_ref[:, 0:D_MODEL], w_ref[0])
        for j in range(1, N_CHIPS):
            dh2 = dh2 + _dot_nt(du_ref[:, D_MODEL * j:D_MODEL * (j + 1)], w_ref[j])
        x1 = x1_ref[...]
        r3 = _rms(x1)
        d3, dg3 = _norm_bwd(dh2, x1 * r3, r3, g3_ref[...])
        dx1 = dx2_ref[...] + d3
        dx1_ref[...] = dx1
        dg3_ref[...] += dg3
        y = y_ref[...]
        r2 = _rms(y)
        dy, dg2 = _norm_bwd(dx1, y * r2, r2, g2_ref[...])
        dy_ref[...] = dy.astype(BF16)
        dg2_ref[...] += dg2

    row = lambda i: (i, 0)
    blk = pl.BlockSpec((tm, D_MODEL), row)
    return pl.pallas_call(
        body, name="up_bwd", grid=(t // tm,),
        in_specs=[pl.BlockSpec((tm, D_FF), row), _wb_spec("w_up"),
                  blk, blk, blk, _full((1, D_MODEL)), _full((1, D_MODEL))],
        out_specs=[blk, blk, _full((1, D_MODEL)), _full((1, D_MODEL))],
        out_shape=[jax.ShapeDtypeStruct((t, D_MODEL), F32), jax.ShapeDtypeStruct((t, D_MODEL), BF16),
                   jax.ShapeDtypeStruct((1, D_MODEL), F32), jax.ShapeDtypeStruct((1, D_MODEL), F32)],
        compiler_params=_params(("arbitrary",)),
    )(du, wb, x1, dx2, y, g3, g2)


def _mix_out_bwd(dy, out_a, out_bt, proj, w_oa, w_ob, wb):
    t = dy.shape[0]
    tm = 256
    nb = t // _TQ

    def body(dy_ref, oa_ref, obt_ref, ga_ref, gb_ref, woa_ref, wob_ref, wout_ref,
             doa_ref, dob_ref, dga_ref, dgb_ref, da_ref, db_ref, dbt_ref, dela_ref, delb_ref):
        dm = _dot_nt(dy_ref[...], wout_ref[...].reshape(D_MODEL, D_MODEL))
        out_a_v = oa_ref[...]
        out_bt_v = obt_ref[...].reshape(N_HEADS_B * V_DIM_B, tm)
        oa = _dot(out_a_v.astype(BF16), woa_ref[...])
        ob = _dot_tn(out_bt_v.astype(BF16), wob_ref[...])
        sa, sb = jax.nn.sigmoid(ga_ref[...]), jax.nn.sigmoid(gb_ref[...])
        doa = (dm * sa).astype(BF16)
        dob = (dm * sb).astype(BF16)
        doa_ref[...] = doa
        dob_ref[...] = dob
        dga_ref[...] = (dm * oa * (sa * (1.0 - sa))).astype(BF16)
        dgb_ref[...] = (dm * ob * (sb * (1.0 - sb))).astype(BF16)
        d_out_a = _dot_nt(doa, woa_ref[...])
        da_ref[...] = d_out_a
        prod_at = (d_out_a * out_a_v).T
        dela_ref[...] = jnp.concatenate(
            [jnp.sum(_head_rows(prod_at, h), axis=0, keepdims=True) for h in range(N_HEADS_A)], axis=0)
        d_out_b = _dot_nt(dob, wob_ref[...])
        d_out_bt = _dot_nt(wob_ref[...], dob)
        prod_bt = d_out_bt * out_bt_v
        for h in range(N_HEADS_B):
            db_ref[h] = d_out_b[:, V_DIM_B * h:V_DIM_B * (h + 1)].astype(BF16)
            dbt_ref[h, 0] = d_out_bt[V_DIM_B * h:V_DIM_B * (h + 1), :].astype(BF16)
            delb_ref[h, 0] = jnp.sum(prod_bt[V_DIM_B * h:V_DIM_B * (h + 1), :], axis=0, keepdims=True)

    row = lambda i: (i, 0)
    blk = pl.BlockSpec((tm, D_MODEL), row)
    return pl.pallas_call(
        body, name="mix_out_bwd", grid=(t // tm,),
        in_specs=[blk, pl.BlockSpec((tm, WIDTH_A), row), _ot_spec(tm, V_DIM_B), pl.BlockSpec((tm, D_MODEL), lambda i: (i, 0)),
                  pl.BlockSpec((tm, D_MODEL), lambda i: (i, 1)),
                  _full((WIDTH_A, D_MODEL)), _full((N_HEADS_B * V_DIM_B, D_MODEL)), _wb_spec("w_out")],
        out_specs=[blk, blk, blk, blk, pl.BlockSpec((tm, WIDTH_A), row),
                   pl.BlockSpec((N_HEADS_B, tm, V_DIM_B), lambda i: (0, i, 0)), _ot_spec(tm, V_DIM_B),
                   pl.BlockSpec((N_HEADS_A, tm), lambda i: (0, i)), _ot_spec(tm, 1)],
        out_shape=[jax.ShapeDtypeStruct((t, D_MODEL), BF16)] * 4
        + [jax.ShapeDtypeStruct((t, WIDTH_A), F32), jax.ShapeDtypeStruct((N_HEADS_B, t, V_DIM_B), BF16),
           jax.ShapeDtypeStruct((N_HEADS_B, nb, V_DIM_B, _TQ), BF16),
           jax.ShapeDtypeStruct((N_HEADS_A, t), F32), jax.ShapeDtypeStruct((N_HEADS_B, nb, 1, _TQ), F32)],
        compiler_params=_params(("parallel",)),
    )(dy, out_a, out_bt, proj, proj, w_oa, w_ob, wb)


def _dw_ob(out_bt, dob):
    t = dob.shape[0]
    nb = t // _TQ

    def body(obt_ref, dob_ref, o_ref):
        @pl.when(pl.program_id(0) == 0)
        def _():
            o_ref[...] = jnp.zeros(o_ref.shape, F32)

        obt = obt_ref[...].reshape(N_HEADS_B * V_DIM_B, _TQ).astype(BF16)
        o_ref[...] += _dot(obt, dob_ref[...])

    return pl.pallas_call(
        body, name="dw_o_b", grid=(nb,),
        in_specs=[pl.BlockSpec((N_HEADS_B, 1, V_DIM_B, _TQ), lambda i: (0, i, 0, 0)),
                  pl.BlockSpec((_TQ, D_MODEL), lambda i: (i, 0))],
        out_specs=_full((N_HEADS_B * V_DIM_B, D_MODEL)),
        out_shape=jax.ShapeDtypeStruct((N_HEADS_B * V_DIM_B, D_MODEL), F32),
        compiler_params=_params(("arbitrary",)),
    )(out_bt, dob)


def _mla_bwd(q, k, qt, kt, v, d_out, d_out_t, lse, delta, gp):
    t = q.shape[0]
    nb = t // _TQ

    def body(k_ref, kt_ref, v_ref, q_ref, qt_ref, do_ref, dot_ref, l_ref, d_ref, gp_ref,
             dqt_ref, dk_ref, dv_ref, land_ref, send_sems, recv_sems):
        kj = pl.program_id(1)

        @pl.when((pl.program_id(0) == 0) & (kj == 0))
        def _():
            _scatter_start(gp_ref, land_ref, send_sems, recv_sems)

        @pl.when(kj == 0)
        def _():
            dqt_ref[...] = jnp.zeros(dqt_ref.shape, F32)

        kv, k_t, vv = k_ref[...], kt_ref[0, 0], v_ref[0]

        def products(qi, diagonal=False):
            return _scores_t(kv, qt_ref[0, qi], diagonal), _dot(vv, dot_ref[0, qi])

        def update(carry, prods, qi):
            dk, dv = carry
            st, dpt = prods
            rows = pl.ds(pl.multiple_of(qi * _TQ, _TQ), _TQ)
            pt = jnp.exp2(st - l_ref[0, qi])
            dv = dv + _dot(pt.astype(BF16), do_ref[0, rows, :])
            dst = (pt * (dpt - d_ref[0, qi]) * _MLA_SCALE).astype(BF16)
            dk = dk + _dot(dst, q_ref[rows, :])
            dqt_ref[0, qi] += _dot(k_t, dst)
            return dk, dv

        def pair(i, carry):
            qa = kj + 1 + 2 * i
            pa, pb = products(qa), products(qa + 1)
            return update(update(carry, pa, qa), pb, qa + 1)

        init = (jnp.zeros((_TQ, HEAD_PAD), F32), jnp.zeros((_TQ, V_DIM_B), F32))
        carry = update(init, products(kj, True), kj)
        pairs = (nb - 1 - kj) // 2
        carry = lax.fori_loop(0, pairs, pair, carry)
        dk, dv = lax.fori_loop(kj + 1 + 2 * pairs, nb, lambda qi, cr: update(cr, products(qi), qi), carry)
        dk_ref[...] = dk
        dv_ref[0] = dv

        @pl.when((pl.program_id(0) == N_HEADS_B - 1) & (kj == nb - 1))
        def _():
            _scatter_wait(gp_ref, land_ref, send_sems, recv_sems)

    head4 = lambda d: pl.BlockSpec((1, nb, d, _TQ), lambda h, kj: (h, 0, 0, 0))
    return pl.pallas_call(
        body, name="mla_bwd", grid=(N_HEADS_B, nb),
        in_specs=[pl.BlockSpec((_TQ, HEAD_PAD), lambda h, kj: (kj, h)),
                  pl.BlockSpec((1, 1, HEAD_PAD, _TQ), lambda h, kj: (h, kj, 0, 0)),
                  pl.BlockSpec((1, _TQ, V_DIM_B), lambda h, kj: (h, kj, 0)),
                  pl.BlockSpec((t, HEAD_PAD), lambda h, kj: (0, h)), head4(HEAD_PAD),
                  pl.BlockSpec((1, t, V_DIM_B), lambda h, kj: (h, 0, 0)), head4(V_DIM_B), head4(1), head4(1), _HBM],
        out_specs=[head4(HEAD_PAD), pl.BlockSpec((_TQ, HEAD_PAD), lambda h, kj: (kj, h)),
                   pl.BlockSpec((1, _TQ, V_DIM_B), lambda h, kj: (h, kj, 0)), _HBM],
        out_shape=[jax.ShapeDtypeStruct((N_HEADS_B, nb, HEAD_PAD, _TQ), F32), jax.ShapeDtypeStruct((t, MLA_W), F32),
                   jax.ShapeDtypeStruct((N_HEADS_B, t, V_DIM_B), F32),
                   jax.ShapeDtypeStruct((3,) + gp.shape[1:], gp.dtype)],
        scratch_shapes=[pltpu.SemaphoreType.DMA((3,)), pltpu.SemaphoreType.DMA((3,))],
        compiler_params=_params(("arbitrary", "arbitrary")),
    )(k, kt, v, q, qt, d_out, d_out_t, lse, delta, gp)


def _mla_prep_bwd(dqt, dk, dv, proj, posc, freq, qan, kvan, wq, wk, wv, swap_src):
    t = dk.shape[0]
    tm = _TQ

    def body(dqt_ref, dk_ref, dv_ref, cq_ref, ckv_ref, pos_ref, f_ref, qan_ref, kvan_ref, wq_ref, wk_ref, wv_ref, src_ref,
             dcq_ref, dckv_ref, dkr_ref, dwq_ref, dwk_ref, dwv_ref, dqan_ref, dkvan_ref, got_ref, send_sem, recv_sem):
        swap = _sibling_copy(src_ref, got_ref, send_sem, recv_sem)

        @pl.when(pl.program_id(0) == 0)
        def _():
            swap.start()
            for r in (dwq_ref, dwk_ref, dwv_ref, dqan_ref, dkvan_ref):
                r[...] = jnp.zeros(r.shape, F32)

        cq = cq_ref[...]
        rq = _rms(cq)
        nq_ = cq * rq
        cqn = (nq_ * qan_ref[...]).astype(BF16)
        ckv = ckv_ref[...]
        rkv = _rms(ckv)
        nkv = ckv * rkv
        ckvn = (nkv * kvan_ref[...]).astype(BF16)
        c, s, lo, hi = _rope_coeffs(pos_ref[...], f_ref[...])
        dkv = dk_ref[...]
        dkr = jnp.zeros((tm, LANES), F32)
        dqb = []
        for h in range(N_HEADS_B):
            dqb.append(_unrope(dqt_ref[h, 0].T, c, s, lo, hi).astype(BF16))
            dkr = dkr + dkv[:, HEAD_PAD * h:HEAD_PAD * (h + 1)]
        dqb = jnp.concatenate(dqb, axis=1)
        dkr_ref[...] = jnp.where(lo | hi, _unrope(dkr, c, s, lo, hi), 0.0).astype(BF16)
        dkb = dkv.astype(BF16)
        dvb = jnp.concatenate([dv_ref[h] for h in range(N_HEADS_B)], axis=1).astype(BF16)
        dwq_ref[...] += _dot_tn(cqn, dqb)
        dwk_ref[...] += _dot_tn(ckvn, dkb)
        dwv_ref[...] += _dot_tn(ckvn, dvb)
        dcqn = _dot_nt(dqb, wq_ref[...])
        dckvn = _dot_nt(dkb, wk_ref[...]) + _dot_nt(dvb, wv_ref[...])
        dcq, dqan = _norm_bwd(dcqn, nq_, rq, qan_ref[...])
        dckv, dkvan = _norm_bwd(dckvn, nkv, rkv, kvan_ref[...])
        dcq_ref[...] = dcq.astype(BF16)
        dckv_ref[...] = dckv.astype(BF16)
        dqan_ref[...] += dqan
        dkvan_ref[...] += dkvan

        @pl.when(pl.program_id(0) == t // tm - 1)
        def _():
            swap.wait_recv()
            swap.wait_send()

    row = lambda i: (i, 0)
    vw = N_HEADS_B * V_DIM_B
    return pl.pallas_call(
        body, name="mla_prep_bwd", grid=(t // tm,),
        in_specs=[pl.BlockSpec((N_HEADS_B, 1, HEAD_PAD, tm), lambda i: (0, i, 0, 0)), pl.BlockSpec((tm, MLA_W), row),
                  pl.BlockSpec((N_HEADS_B, tm, V_DIM_B), lambda i: (0, i, 0)),
                  pl.BlockSpec((tm, Q_LORA), lambda i: (i, _CQ_BLK)),
                  pl.BlockSpec((tm, LANES), lambda i: (i, _CKV_BLK)),
                  pl.BlockSpec((tm, 1), row), _full((1, LANES)), _full((1, Q_LORA)), _full((1, KV_LORA)),
                  _full((Q_LORA, MLA_W)), _full((KV_LORA, MLA_W)), _full((KV_LORA, vw)), _HBM],
        out_specs=[pl.BlockSpec((tm, Q_LORA), row), pl.BlockSpec((tm, LANES), row), pl.BlockSpec((tm, LANES), row),
                   _full((Q_LORA, MLA_W)), _full((KV_LORA, MLA_W)), _full((KV_LORA, vw)),
                   _full((1, Q_LORA)), _full((1, KV_LORA)), _HBM],
        out_shape=[jax.ShapeDtypeStruct((t, Q_LORA), BF16), jax.ShapeDtypeStruct((t, LANES), BF16),
                   jax.ShapeDtypeStruct((t, LANES), BF16),
                   jax.ShapeDtypeStruct((Q_LORA, MLA_W), F32), jax.ShapeDtypeStruct((KV_LORA, MLA_W), F32),
                   jax.ShapeDtypeStruct((KV_LORA, vw), F32),
                   jax.ShapeDtypeStruct((1, Q_LORA), F32), jax.ShapeDtypeStruct((1, KV_LORA), F32),
                   jax.ShapeDtypeStruct(swap_src.shape, swap_src.dtype)],
        scratch_shapes=[pltpu.SemaphoreType.DMA(()), pltpu.SemaphoreType.DMA(())],
        compiler_params=_params(("arbitrary",)),
    )(dqt, dk, dv, proj, proj, posc, freq, qan, kvan, wq, wk, wv, swap_src)


def _swa_bwd(proj, d_out, lse, delta, posc, posr, sinks):
    t = proj.shape[0]
    nb = t // BLOCK

    def body(q_ref, kc_ref, kp_ref, vc_ref, vp_ref, do_ref, l_ref, d_ref, pq_ref, pc_ref, pp_ref, sink_ref,
             dq_ref, dk_ref, dv_ref, ds_ref, dkb_s, dvb_s, dk_carry, dv_carry):
        n = pl.program_id(0)

        @pl.when(n == 0)
        def _():
            ds_ref[...] = jnp.zeros(ds_ref.shape, F32)
            dk_carry[...] = jnp.zeros(dk_carry.shape, F32)
            dv_carry[...] = jnp.zeros(dv_carry.shape, F32)

        @pl.when(n < nb)
        def _():
            kb, vb, dist, valid = _swa_band(n, kp_ref, kc_ref, vp_ref, vc_ref, pq_ref, pp_ref, pc_ref)
            qv, dov = q_ref[...], do_ref[...]
            q_t, do_t, kb_t = qv.T, dov.T, kb.T
            lane = lax.broadcasted_iota(jnp.int32, (1, LANES), 1)
            dsink = jnp.zeros((1, LANES), F32)
            dq_t = []
            for kh in range(N_KV_A):
                heads = range(_GROUP_A * kh, _GROUP_A * (kh + 1))
                st_g = _dot(_head_cols(kb, kh).astype(BF16), _group_t(q_t, kh))
                dpt_g = _dot(_head_cols(vb, kh).astype(BF16), _group_t(do_t, kh))
                pts, dsts = [], []
                for j, h in enumerate(heads):
                    st = _swa_scores_t(st_g, j, h, dist, valid)
                    l_h, d_h = l_ref[h:h + 1, :], d_ref[h:h + 1, :]
                    pt = jnp.exp2(st - l_h)
                    p_sink = jnp.exp2(sink_ref[0:1, h:h + 1] * _LOG2E - l_h)
                    dsink = jnp.where(lane == h, jnp.sum(-p_sink * d_h, axis=1, keepdims=True), dsink)
                    dst = pt * (dpt_g[:, BLOCK * j:BLOCK * (j + 1)] - d_h) * _SWA_SCALE
                    pts.append(pt.astype(BF16))
                    dsts.append(dst.astype(BF16))
                pt_g, dst_g = jnp.concatenate(pts, axis=1), jnp.concatenate(dsts, axis=1)
                q_g = jnp.concatenate([_head_cols(qv, h) for h in heads], axis=0).astype(BF16)
                do_g = jnp.concatenate([_head_cols(dov, h) for h in heads], axis=0).astype(BF16)
                dkb_s[:, HEAD_DIM_A * kh:HEAD_DIM_A * (kh + 1)] = _dot(dst_g, q_g)
                dvb_s[:, HEAD_DIM_A * kh:HEAD_DIM_A * (kh + 1)] = _dot(pt_g, do_g)
                dq_g = _dot(_head_rows(kb_t, kh).astype(BF16), dst_g)
                dq_t.extend(dq_g[:, BLOCK * j:BLOCK * (j + 1)] for j in range(_GROUP_A))
            dq_ref[...] = jnp.concatenate(dq_t, axis=0).T
            ds_ref[...] += dsink
            dk_ref[...] = dk_carry[...] + dkb_s[0:BLOCK, :]
            dv_ref[...] = dv_carry[...] + dvb_s[0:BLOCK, :]
            dk_carry[...] = dkb_s[BLOCK:2 * BLOCK, :]
            dv_carry[...] = dvb_s[BLOCK:2 * BLOCK, :]

        @pl.when(n == nb)
        def _():
            dk_ref[...] = dk_carry[...]
            dv_ref[...] = dv_carry[...]

    cur = lambda n: (jnp.minimum(n, nb - 1), 0)
    cur_t = lambda n: (0, jnp.minimum(n, nb - 1))
    prv = lambda n: jnp.maximum(jnp.minimum(n, nb - 1) - 1, 0)
    out_prev = lambda n: (jnp.maximum(n - 1, 0), 0)
    return pl.pallas_call(
        body, name="swa_bwd", grid=(nb + 1,),
        in_specs=[pl.BlockSpec((BLOCK, WIDTH_A), lambda n: (jnp.minimum(n, nb - 1), _QA_BLK)),
                  pl.BlockSpec((BLOCK, LANES), lambda n: (jnp.minimum(n, nb - 1), _KA_BLK)),
                  pl.BlockSpec((BLOCK, LANES), lambda n: (prv(n), _KA_BLK)),
                  pl.BlockSpec((BLOCK, LANES), lambda n: (jnp.minimum(n, nb - 1), _VA_BLK)),
                  pl.BlockSpec((BLOCK, LANES), lambda n: (prv(n), _VA_BLK)),
                  pl.BlockSpec((BLOCK, WIDTH_A), cur), pl.BlockSpec((N_HEADS_A, BLOCK), cur_t),
                  pl.BlockSpec((N_HEADS_A, BLOCK), cur_t), pl.BlockSpec((1, BLOCK), cur_t),
                  pl.BlockSpec((BLOCK, 1), cur), pl.BlockSpec((BLOCK, 1), lambda n: (prv(n), 0)),
                  _full((1, N_HEADS_A))],
        out_specs=[pl.BlockSpec((BLOCK, WIDTH_A), cur), pl.BlockSpec((BLOCK, LANES), out_prev),
                   pl.BlockSpec((BLOCK, LANES), out_prev), _full((1, LANES))],
        out_shape=[jax.ShapeDtypeStruct((t, WIDTH_A), F32), jax.ShapeDtypeStruct((t, LANES), F32),
                   jax.ShapeDtypeStruct((t, LANES), F32), jax.ShapeDtypeStruct((1, LANES), F32)],
        scratch_shapes=[pltpu.VMEM((2 * BLOCK, LANES), F32), pltpu.VMEM((2 * BLOCK, LANES), F32),
                        pltpu.VMEM((BLOCK, LANES), F32), pltpu.VMEM((BLOCK, LANES), F32)],
        compiler_params=_params(("arbitrary",)),
    )(proj, proj, proj, proj, proj, d_out, lse, delta, posr, posc, posc, sinks)


def _in_bwd(dproj, w_in_t, x, dx1, g1, gp):
    t = x.shape[0]
    tm = 256
    steps = t // tm

    def body(dp_ref, w_ref, x_ref, dx1_ref, g_ref, gp_ref, dx_ref, dg_ref, land_ref, send_sems, recv_sems):
        i = pl.program_id(0)

        @pl.when(i == 0)
        def _():
            dg_ref[...] = jnp.zeros(dg_ref.shape, F32)
            _scatter_start(gp_ref, land_ref, send_sems, recv_sems)

        dh = _dot(dp_ref[...], w_ref[...])
        xv = x_ref[...]
        r = _rms(xv)
        dx, dg = _norm_bwd(dh, xv * r, r, g_ref[...])
        dx_ref[...] = dx1_ref[...] + dx
        dg_ref[...] += dg

        @pl.when(i == steps - 1)
        def _():
            _scatter_wait(gp_ref, land_ref, send_sems, recv_sems)

    row = lambda i: (i, 0)
    blk = pl.BlockSpec((tm, D_MODEL), row)
    return pl.pallas_call(
        body, name="in_bwd", grid=(steps,),
        in_specs=[pl.BlockSpec((tm, D_IN_PAD), row), _full((D_IN_PAD, D_MODEL)), blk, blk, _full((1, D_MODEL)), _HBM],
        out_specs=[blk, _full((1, D_MODEL)), _HBM],
        out_shape=[jax.ShapeDtypeStruct((t, D_MODEL), F32), jax.ShapeDtypeStruct((1, D_MODEL), F32),
                   jax.ShapeDtypeStruct((3,) + gp.shape[1:], gp.dtype)],
        scratch_shapes=[pltpu.SemaphoreType.DMA((3,)), pltpu.SemaphoreType.DMA((3,))],
        compiler_params=_params(("arbitrary",)),
    )(dproj, w_in_t, x, dx1, g1, gp)


def _adamw_store(w, g, m, v, out_refs):
    g_out, d_out, m_out, v_out = out_refs
    m_new = ADAM_B1 * m + (1.0 - ADAM_B1) * g
    v_new = ADAM_B2 * v + (1.0 - ADAM_B2) * jnp.square(g)
    m_hat = m_new / (1.0 - ADAM_B1 ** ADAM_STEP)
    v_hat = v_new / (1.0 - ADAM_B2 ** ADAM_STEP)
    g_out[...] = g
    d_out[...] = -ADAM_LR * (m_hat / (jnp.sqrt(v_hat) + ADAM_EPS) + ADAM_WD * w)
    m_out[...] = m_new
    v_out[...] = v_new


_SMALL_SLOTS = {"pre_norm_mix": (0, 0, D_MODEL), "post_norm_mix": (1, 0, D_MODEL), "pre_norm_mlp": (2, 0, D_MODEL),
                "post_norm_mlp": (3, 0, D_MODEL), "q_a_norm": (4, 0, Q_LORA), "kv_a_norm": (4, Q_LORA, KV_LORA),
                "sinks": (4, Q_LORA + KV_LORA, N_HEADS_A)}
_LOSS_ROW = 5


def _adamw_small(red, w, m, v):
    names = tuple(_SMALL_SLOTS)
    n = len(names)

    def body(*refs):
        red_ref, ws, ms, vs, outs = refs[0], refs[1:1 + n], refs[1 + n:1 + 2 * n], refs[1 + 2 * n:1 + 3 * n], refs[1 + 3 * n:]
        for k, name in enumerate(names):
            row, lane, width = _SMALL_SLOTS[name]
            g = red_ref[row:row + 1, lane:lane + width]
            _adamw_store(ws[k][...], g, ms[k][...], vs[k][...], outs[4 * k:4 * k + 4])

    vmem = pl.BlockSpec(memory_space=pltpu.VMEM)
    res = pl.pallas_call(
        body, name="adamw_small", in_specs=[vmem] * (1 + 3 * n), out_specs=[vmem] * (4 * n),
        out_shape=[jax.ShapeDtypeStruct(w[name].shape, F32) for name in names for _ in range(4)],
    )(red, *[w[k] for k in names], *[m[k] for k in names], *[v[k] for k in names])
    return {name: res[4 * k:4 * k + 4] for k, name in enumerate(names)}


def _adamw(w, g_parts, m, v, name, block, g_row_off=0):
    r, c = w.shape
    br, bc = block
    ng = len(g_parts)

    def body(*refs):
        w_ref, g_refs, m_ref, v_ref = refs[0], refs[1:1 + ng], refs[1 + ng], refs[2 + ng]
        g = g_refs[0][...]
        for gr in g_refs[1:]:
            g = g + gr[...]
        _adamw_store(w_ref[...], g, m_ref[...], v_ref[...], refs[3 + ng:])

    assert g_row_off % br == 0 and r % br == 0 and c % bc == 0
    blk = pl.BlockSpec(block, lambda i, j: (i, j))
    g_blk = pl.BlockSpec(block, lambda i, j: (i + g_row_off // br, j))
    return pl.pallas_call(
        body, name=name, grid=(r // br, c // bc),
        in_specs=[blk] + [g_blk] * ng + [blk, blk], out_specs=[blk] * 4,
        out_shape=[jax.ShapeDtypeStruct((r, c), F32)] * 4,
        compiler_params=_params(("parallel", "parallel")),
    )(w, *g_parts, m, v)


_HBM = pl.BlockSpec(memory_space=pltpu.HBM)


def _other_chips(x, y):
    return ((1 - x, y), (x, 1 - y), (1 - x, 1 - y))


def _gather_copies(src, out, send_sems, recv_sems, local_sem):
    x, y, c = lax.axis_index("x"), lax.axis_index("y"), lax.axis_index("c")
    me = 2 * x + y
    local = pltpu.make_async_copy(src, out.at[me], local_sem)

    def copies(arriving):
        return [pltpu.make_async_remote_copy(src_ref=src, dst_ref=out.at[2 * px + py if arriving else me],
                                             send_sem=send_sems.at[j], recv_sem=recv_sems.at[j], device_id=(px, py, c),
                                             device_id_type=MESH)
                for j, (px, py) in enumerate(_other_chips(x, y))]

    return local, copies


def _gather_start(src, out, send_sems, recv_sems, local_sem):
    local, copies = _gather_copies(src, out, send_sems, recv_sems, local_sem)
    local.start()
    for cp in copies(False):
        cp.start()


def _gather_wait(src, out, send_sems, recv_sems, local_sem):
    local, copies = _gather_copies(src, out, send_sems, recv_sems, local_sem)
    for cp in copies(True):
        cp.wait_recv()
    for cp in copies(False):
        cp.wait_send()
    local.wait()


def _scatter_copies(src, land, send_sems, recv_sems):
    x, y, c = lax.axis_index("x"), lax.axis_index("y"), lax.axis_index("c")
    return [pltpu.make_async_remote_copy(src_ref=src.at[2 * px + py], dst_ref=land.at[j], send_sem=send_sems.at[j],
                                         recv_sem=recv_sems.at[j], device_id=(px, py, c), device_id_type=MESH)
            for j, (px, py) in enumerate(_other_chips(x, y))]


def _scatter_start(src, land, send_sems, recv_sems):
    for cp in _scatter_copies(src, land, send_sems, recv_sems):
        cp.start()


def _scatter_wait(src, land, send_sems, recv_sems):
    copies = _scatter_copies(src, land, send_sems, recv_sems)
    for cp in copies:
        cp.wait_recv()
    for cp in copies:
        cp.wait_send()


def _all_gather_chips(packed):
    r = packed.shape[0]
    half = r // 2

    def body(src, out, ici_send, ici_recv, d2d_send, d2d_recv, local_sem):
        x, y, c = lax.axis_index("x"), lax.axis_index("y"), lax.axis_index("c")
        me = 2 * x + y
        mine = pl.ds(pl.multiple_of(c * half, 16), half)
        theirs = pl.ds(pl.multiple_of((1 - c) * half, 16), half)
        chips = _other_chips(x, y)
        local = pltpu.make_async_copy(src, out.at[me], local_sem)
        local.start()
        sends = [pltpu.make_async_remote_copy(src_ref=src.at[mine], dst_ref=out.at[me, mine], send_sem=ici_send.at[j],
                                              recv_sem=ici_recv.at[j], device_id=(px, py, c), device_id_type=MESH)
                 for j, (px, py) in enumerate(chips)]
        for cp in sends:
            cp.start()
        passed = []
        for j, (px, py) in enumerate(chips):
            block = 2 * px + py
            pltpu.make_async_remote_copy(src_ref=src.at[mine], dst_ref=out.at[block, mine], send_sem=ici_send.at[j],
                                         recv_sem=ici_recv.at[j], device_id=(px, py, c), device_id_type=MESH).wait_recv()
            cp = pltpu.make_async_remote_copy(src_ref=out.at[block, mine], dst_ref=out.at[block, mine],
                                              send_sem=d2d_send.at[j], recv_sem=d2d_recv.at[j],
                                              device_id=(x, y, 1 - c), device_id_type=MESH)
            cp.start()
            passed.append(cp)
        for j, (px, py) in enumerate(chips):
            block = 2 * px + py
            pltpu.make_async_remote_copy(src_ref=out.at[block, theirs], dst_ref=out.at[block, theirs],
                                         send_sem=d2d_send.at[j], recv_sem=d2d_recv.at[j],
                                         device_id=(x, y, 1 - c), device_id_type=MESH).wait_recv()
        for cp in sends + passed:
            cp.wait_send()
        local.wait()

    sems = pltpu.SemaphoreType.DMA((3,))
    return pl.pallas_call(
        body, name="ag_weights", in_specs=[_HBM], out_specs=_HBM,
        out_shape=jax.ShapeDtypeStruct((N_CHIPS,) + packed.shape, packed.dtype),
        scratch_shapes=[sems, sems, sems, sems, pltpu.SemaphoreType.DMA(())],
    )(packed)


def _sum4(gp, land, chip, name):
    _, r, w = gp.shape
    tr = 128

    def body(chip_ref, o_ref, l_ref, s_ref):
        s_ref[...] = ((o_ref[0] + l_ref[0].astype(F32)) + l_ref[1].astype(F32)) + l_ref[2].astype(F32)

    return pl.pallas_call(
        body, name=name,
        grid_spec=pltpu.PrefetchScalarGridSpec(
            num_scalar_prefetch=1, grid=(r // tr,),
            in_specs=[pl.BlockSpec((1, tr, w), lambda i, chip_ref: (chip_ref[0], i, 0)),
                      pl.BlockSpec((3, tr, w), lambda i, chip_ref: (0, i, 0))],
            out_specs=pl.BlockSpec((tr, w), lambda i, chip_ref: (i, 0))),
        out_shape=jax.ShapeDtypeStruct((r, w), F32),
        compiler_params=_params(("parallel",)),
    )(chip, gp, land)


def _sibling_copy(src, got, send_sem, recv_sem):
    x, y, c = lax.axis_index("x"), lax.axis_index("y"), lax.axis_index("c")
    return pltpu.make_async_remote_copy(src_ref=src, dst_ref=got, send_sem=send_sem, recv_sem=recv_sem,
                                        device_id=(x, y, 1 - c), device_id_type=MESH)


def _swap_sibling(s, name):
    def body(src, got, send_sem, recv_sem):
        cp = _sibling_copy(src, got, send_sem, recv_sem)
        cp.start()
        cp.wait_recv()
        cp.wait_send()

    return pl.pallas_call(
        body, name=name, in_specs=[_HBM], out_specs=_HBM,
        out_shape=jax.ShapeDtypeStruct(s.shape, s.dtype),
        scratch_shapes=[pltpu.SemaphoreType.DMA(()), pltpu.SemaphoreType.DMA(())],
    )(s)


def _all_reduce_small(dsmall, loss):
    n_dev = 8
    names = tuple(_SMALL_SLOTS)
    shape = (8, D_MODEL)

    def body(*refs):
        parts, loss_ref = refs[:len(names)], refs[len(names)]
        out, src, gath, send_sems, recv_sems = refs[len(names) + 1:]
        x, y, c = lax.axis_index("x"), lax.axis_index("y"), lax.axis_index("c")
        me = 4 * x + 2 * y + c
        src[...] = jnp.zeros(shape, F32)
        for name, part in zip(names, parts):
            row, lane, _ = _SMALL_SLOTS[name]
            src[row:row + 1, lane:lane + part.shape[1]] = part[...]
        src[_LOSS_ROW:_LOSS_ROW + 1, 0:LANES] = loss_ref[...]
        gath[me] = src[...]
        peers = []
        for k in range(1, n_dev):
            px = 1 - x if (k >> 2) & 1 else x
            py = 1 - y if (k >> 1) & 1 else y
            pc = 1 - c if k & 1 else c
            peers.append((px, py, pc))
        sends = []
        for j, peer in enumerate(peers):
            cp = pltpu.make_async_remote_copy(src_ref=src, dst_ref=gath.at[me], send_sem=send_sems.at[j],
                                              recv_sem=recv_sems.at[j], device_id=peer, device_id_type=MESH)
            cp.start()
            sends.append(cp)
        for j, (px, py, pc) in enumerate(peers):
            pltpu.make_async_remote_copy(src_ref=src, dst_ref=gath.at[4 * px + 2 * py + pc], send_sem=send_sems.at[j],
                                         recv_sem=recv_sems.at[j], device_id=(px, py, pc), device_id_type=MESH).wait_recv()
        for cp in sends:
            cp.wait_send()
        acc = gath[0]
        for d in range(1, n_dev):
            acc = acc + gath[d]
        out[...] = acc

    vmem = pl.BlockSpec(memory_space=pltpu.VMEM)
    return pl.pallas_call(
        body, name="ar_small", in_specs=[vmem] * (len(names) + 1), out_specs=vmem,
        out_shape=jax.ShapeDtypeStruct(shape, F32),
        scratch_shapes=[pltpu.VMEM(shape, F32), pltpu.VMEM((n_dev,) + shape, F32),
                        pltpu.SemaphoreType.DMA((n_dev - 1,)), pltpu.SemaphoreType.DMA((n_dev - 1,))],
    )(*[dsmall[k] for k in names], loss)


_W_IN_ROWS = SHARD_SHAPES["w_in"][1]
_KR_ROW = 3200
_KR_PAD_ROW = _KR_BLK * LANES + QK_NOPE


def _shard_rows(name, a):
    return jnp.transpose(a) if name == "w_in" else a.reshape(PACK_ROWS[name], D_MODEL)


def _pack(group, shards, dtype):
    parts = [_shard_rows(n, shards[n]).astype(dtype) for n in group]
    pad = -sum(PACK_ROWS[n] for n in group) % LANES
    if pad:
        parts.append(jnp.zeros((pad, D_MODEL), dtype))
    return jnp.concatenate(parts, axis=0)


def _col_sharded_full(g, name, group):
    r, c = SHARD_SHAPES[name]
    off = _row_offset(group, name)
    blocks = g[:, off:off + PACK_ROWS[name]].reshape(N_CHIPS, r, c)
    return jnp.transpose(blocks, (1, 0, 2)).reshape(r, N_CHIPS * c)


def _col_sharded_blocks(d, name):
    r, c = SHARD_SHAPES[name]
    return jnp.transpose(d.reshape(r, N_CHIPS, c), (1, 0, 2)).reshape(N_CHIPS, PACK_ROWS[name], D_MODEL)


def _weights_a(g):
    dt = g.dtype
    w_in_t = g[:, :_W_IN_ROWS].reshape(N_CHIPS * _W_IN_ROWS, D_MODEL)
    z = lambda n: jnp.zeros((n, D_MODEL), dt)
    w_in_t = jnp.concatenate([w_in_t[:_KR_ROW], z(_KR_PAD_ROW - _KR_ROW), w_in_t[_KR_ROW:],
                              z(D_IN_PAD - _KR_PAD_ROW - QK_ROPE)], axis=0)
    wq = _col_sharded_full(g, "w_q_b", GROUP_A).reshape(Q_LORA, N_HEADS_B, Q_HEAD_B)
    wq_p = jnp.concatenate([wq, jnp.zeros((Q_LORA, N_HEADS_B, HEAD_PAD - Q_HEAD_B), dt)], axis=2).reshape(Q_LORA, MLA_W)
    wkv = _col_sharded_full(g, "w_kv_b", GROUP_A).reshape(KV_LORA, N_HEADS_B, QK_NOPE + V_DIM_B)
    zk = jnp.zeros((KV_LORA, N_HEADS_B, HEAD_PAD - QK_NOPE), dt)
    wk_p = jnp.concatenate([wkv[:, :, :QK_NOPE], zk], axis=2).reshape(KV_LORA, MLA_W)
    wv = wkv[:, :, QK_NOPE:].reshape(KV_LORA, N_HEADS_B * V_DIM_B)
    return dict(w_in=w_in_t, wq=wq_p, wk=wk_p, wv=wv)


def _grad_blocks_a(dw_in_t, dwq_p, dwk_p, dwv):
    dw_in = jnp.concatenate([dw_in_t[:_KR_ROW], dw_in_t[_KR_PAD_ROW:_KR_PAD_ROW + QK_ROPE]], axis=0)
    dwq = dwq_p.reshape(Q_LORA, N_HEADS_B, HEAD_PAD)[:, :, :Q_HEAD_B].reshape(Q_LORA, N_HEADS_B * Q_HEAD_B)
    dwk = dwk_p.reshape(KV_LORA, N_HEADS_B, HEAD_PAD)[:, :, :QK_NOPE]
    dwkv = jnp.concatenate([dwk, dwv.reshape(KV_LORA, N_HEADS_B, V_DIM_B)], axis=2)
    dwkv = dwkv.reshape(KV_LORA, N_HEADS_B * (QK_NOPE + V_DIM_B))
    pad = -sum(PACK_ROWS[n] for n in GROUP_A) % LANES
    return jnp.concatenate([dw_in.reshape(N_CHIPS, _W_IN_ROWS, D_MODEL), _col_sharded_blocks(dwq, "w_q_b"),
                            _col_sharded_blocks(dwkv, "w_kv_b"), jnp.zeros((N_CHIPS, pad, D_MODEL), F32)], axis=1)


def _rope_freq_lanes():
    freqs = ROPE_THETA ** (-jnp.arange(0, QK_ROPE, 2, dtype=F32) / QK_ROPE)
    return jnp.concatenate([jnp.zeros((QK_NOPE,), F32), freqs, freqs,
                            jnp.zeros((HEAD_PAD - Q_HEAD_B,), F32)]).reshape(1, LANES)


def _fwd_bwd(x, positions, target, w):
    t = x.shape[0]
    wa = _weights_a(_all_gather_chips(_pack(GROUP_A, w, BF16)))
    posr = positions.astype(F32).reshape(1, t)
    posc = posr.reshape(t, 1)
    freq = _rope_freq_lanes()
    g1, g2, g3, g4 = w["pre_norm_mix"], w["post_norm_mix"], w["pre_norm_mlp"], w["post_norm_mlp"]
    qan, kvan, sinks = w["q_a_norm"], w["kv_a_norm"], w["sinks"]

    h, proj = _proj_fwd(x, g1, wa["w_in"])
    out_a, lse_a = _swa_fwd(proj, posc, posr, sinks)
    qm, km, qt, kt, vm, vt = _mla_prep_fwd(proj, posc, freq, qan, kvan, wa["wq"], wa["wk"], wa["wv"])
    out_bt, lse_b, wb = _mla_fwd(km, qt, vt, _pack(GROUP_B, w, BF16))
    w_oa, w_ob = _col_sharded_full(wb, "w_o_a", GROUP_B), _col_sharded_full(wb, "w_o_b", GROUP_B)
    merged, y, x1, h2 = _mix_out_fwd(out_a, out_bt, proj, x, w_oa, w_ob, wb, g2, g3)
    a = _up_fwd(h2, wb)
    dx2, dyd, dg4, loss = _down_fwd_loss(a, wb, x1, target, g4)

    gp_b = _dw_into_blocks(a, dyd, "w_down", 1024, 512)
    du = _down_bwd(dyd, wb, a)
    gp_b = _dw_into_blocks(h2, du, "w_up", 1024, 512, gp_b)
    dx1, dy, dg3, dg2 = _up_bwd(du, wb, x1, dx2, y, g3, g2)
    gp_b = _dw_into_blocks(merged, dy, "w_out", 256, 1024, gp_b)
    doa, dob, dga, dgb, d_out_a, d_out_b, d_out_bt, del_a, del_b = _mix_out_bwd(dy, out_a, out_bt, proj, w_oa, w_ob, wb)
    dw_oa = _matmul_tn(out_a, doa, "dw_o_a", 512, 1024)
    dw_ob = _dw_ob(out_bt, dob)
    small_b = jnp.concatenate([_col_sharded_blocks(dw_oa, "w_o_a"), _col_sharded_blocks(dw_ob, "w_o_b")], axis=1)
    gp_b = lax.dynamic_update_slice(gp_b, small_b, (0, _row_offset(GROUP_B, "w_o_a"), 0))
    dqm, dkm, dvm, land_b = _mla_bwd(qm, km, qt, kt, vm, d_out_b, d_out_bt, lse_b, del_b, gp_b)
    chip = (2 * lax.axis_index("x") + lax.axis_index("y")).astype(jnp.int32).reshape(1)
    part_b = _sum4(gp_b, land_b, chip, "rs_sum_b")
    dcq, dckv, dkr, dwq, dwk, dwv, dqan, dkvan, sib_b = _mla_prep_bwd(
        dqm, dkm, dvm, proj, posc, freq, qan, kvan, wa["wq"], wa["wk"], wa["wv"], part_b)
    dqa, dka, dva, dsinks = _swa_bwd(proj, d_out_a, lse_a, del_a, posc, posr, sinks)
    dproj = jnp.concatenate([dga, dgb, dqa.astype(BF16), dka.astype(BF16), dva.astype(BF16), dcq, dckv, dkr], axis=1)
    dw_in_t = _matmul_tn(dproj, h, "dw_in", D_IN_PAD // 2, 1024, tk=512)
    gp_a = _grad_blocks_a(dw_in_t, dwq, dwk, dwv)
    grad_x, dg1, land_a = _in_bwd(dproj, wa["w_in"], x, dx1, g1, gp_a.astype(BF16))

    part_a = _sum4(gp_a, land_a, chip, "rs_sum_a")
    reduced = {GROUP_A: [part_a, _swap_sibling(part_a, "rs_swap_a")], GROUP_B: [part_b, sib_b]}
    dsmall = dict(pre_norm_mix=dg1, post_norm_mix=dg2, pre_norm_mlp=dg3, post_norm_mlp=dg4,
                  q_a_norm=dqan, kv_a_norm=dkvan, sinks=dsinks)
    return loss, grad_x, reduced, dsmall


def kernel(x, positions, pre_norm_mix, w_in, q_a_norm, w_q_b, kv_a_norm, w_kv_b, sinks, w_o_a, w_o_b, w_out, post_norm_mix, pre_norm_mlp, w_up, w_down, post_norm_mlp, loss_target, m_pre_norm_mix, m_w_in, m_q_a_norm, m_w_q_b, m_kv_a_norm, m_w_kv_b, m_sinks, m_w_o_a, m_w_o_b, m_w_out, m_post_norm_mix, m_pre_norm_mlp, m_w_up, m_w_down, m_post_norm_mlp, v_pre_norm_mix, v_w_in, v_q_a_norm, v_w_q_b, v_kv_a_norm, v_w_kv_b, v_sinks, v_w_o_a, v_w_o_b, v_w_out, v_post_norm_mix, v_pre_norm_mlp, v_w_up, v_w_down, v_post_norm_mlp):
    w = dict(pre_norm_mix=pre_norm_mix, w_in=w_in[0], q_a_norm=q_a_norm, w_q_b=w_q_b[0], kv_a_norm=kv_a_norm,
             w_kv_b=w_kv_b[0], sinks=sinks, w_o_a=w_o_a[0], w_o_b=w_o_b[0], w_out=w_out[0],
             post_norm_mix=post_norm_mix, pre_norm_mlp=pre_norm_mlp, w_up=w_up[0], w_down=w_down[0],
             post_norm_mlp=post_norm_mlp)
    m = dict(pre_norm_mix=m_pre_norm_mix, w_in=m_w_in[0], q_a_norm=m_q_a_norm, w_q_b=m_w_q_b[0],
             kv_a_norm=m_kv_a_norm, w_kv_b=m_w_kv_b[0], sinks=m_sinks, w_o_a=m_w_o_a[0], w_o_b=m_w_o_b[0],
             w_out=m_w_out[0], post_norm_mix=m_post_norm_mix, pre_norm_mlp=m_pre_norm_mlp, w_up=m_w_up[0],
             w_down=m_w_down[0], post_norm_mlp=m_post_norm_mlp)
    v = dict(pre_norm_mix=v_pre_norm_mix, w_in=v_w_in[0], q_a_norm=v_q_a_norm, w_q_b=v_w_q_b[0],
             kv_a_norm=v_kv_a_norm, w_kv_b=v_w_kv_b[0], sinks=v_sinks, w_o_a=v_w_o_a[0], w_o_b=v_w_o_b[0],
             w_out=v_w_out[0], post_norm_mix=v_post_norm_mix, pre_norm_mlp=v_pre_norm_mlp, w_up=v_w_up[0],
             w_down=v_w_down[0], post_norm_mlp=v_post_norm_mlp)

    loss, grad_x, reduced, dsmall = _fwd_bwd(x[0], positions, loss_target[0], w)

    red = _all_reduce_small(dsmall, loss)
    small = _adamw_small(red, w, m, v)

    big = {}
    tr = jnp.transpose
    big["w_in"] = [tr(o)[None] for o in _adamw(tr(w["w_in"]), reduced[GROUP_A], tr(m["w_in"]), tr(v["w_in"]),
                                               "adamw_w_in", (_W_IN_ROWS, 256))]
    for n in ("w_up", "w_down", "w_out"):
        big[n] = [o[None] for o in _adamw(w[n], reduced[GROUP_B], m[n], v[n], "adamw_" + n, (128, D_MODEL),
                                          _row_offset(GROUP_B, n))]
    for group, names in ((GROUP_A, ("w_q_b", "w_kv_b")), (GROUP_B, ("w_o_a", "w_o_b"))):
        for n in names:
            off = _row_offset(group, n)
            g_parts = [p[off:off + PACK_ROWS[n]].reshape(SHARD_SHAPES[n]) for p in reduced[group]]
            big[n] = [o[None] for o in _adamw(w[n], g_parts, m[n], v[n], "adamw_" + n, SHARD_SHAPES[n])]

    outs = [big[n][k] if n in big else small[n][k] for k in range(4) for n in WEIGHTS]
    return (red[_LOSS_ROW, 0], grad_x[None], *outs)
```

```python
import jax
import jax.numpy as jnp
from jax import lax
from jax.experimental import pallas as pl
from jax.experimental.pallas import tpu as pltpu

F32 = jnp.float32
BF16 = jnp.bfloat16
MESH = pl.DeviceIdType.MESH

D_MODEL = 1024
N_HEADS_A = 8
N_KV_A = 2
HEAD_DIM_A = 64
WINDOW = 128
BLOCK = 128
N_HEADS_B = 8
QK_NOPE = 64
QK_ROPE = 32
V_DIM_B = 64
Q_LORA = 256
KV_LORA = 128
ROPE_THETA = 10000.0
D_FF = 4 * D_MODEL
EPS = 1e-6
WIDTH_A = N_HEADS_A * HEAD_DIM_A
Q_HEAD_B = QK_NOPE + QK_ROPE
D_IN_PAD = 3328
HEAD_PAD = 128
MLA_W = N_HEADS_B * HEAD_PAD

ADAM_LR = 0.001
ADAM_B1 = 0.9
ADAM_B2 = 0.999
ADAM_EPS = 1e-08
ADAM_WD = 0.01
ADAM_STEP = 10

NEG = -1e30
N_CHIPS = 4
LANES = 128
VMEM_LIMIT = 56 * 1024 * 1024

SHARD_SHAPES = {"w_in": (1024, 808), "w_q_b": (256, 192), "w_kv_b": (128, 256), "w_o_a": (512, 256),
                "w_o_b": (512, 256), "w_out": (256, 1024), "w_up": (1024, 1024), "w_down": (1024, 1024)}
PACK_ROWS = {n: (s[0] * s[1]) // D_MODEL for n, s in SHARD_SHAPES.items()}
GROUP_A = ("w_in", "w_q_b", "w_kv_b")
GROUP_B = ("w_up", "w_down", "w_out", "w_o_a", "w_o_b")
WEIGHTS = ("pre_norm_mix", "w_in", "q_a_norm", "w_q_b", "kv_a_norm", "w_kv_b", "sinks", "w_o_a", "w_o_b", "w_out",
           "post_norm_mix", "pre_norm_mlp", "w_up", "w_down", "post_norm_mlp")


def _params(sem=None):
    return pltpu.CompilerParams(dimension_semantics=sem, vmem_limit_bytes=VMEM_LIMIT)


def _dot(a, b):
    return jnp.dot(a, b, preferred_element_type=F32)


def _dot_nt(a, b):
    return lax.dot_general(a, b, (((1,), (1,)), ((), ())), preferred_element_type=F32)


def _dot_tn(a, b):
    return lax.dot_general(a, b, (((0,), (0,)), ((), ())), preferred_element_type=F32)


def _rms(v):
    return lax.rsqrt(jnp.mean(v * v, axis=-1, keepdims=True) + EPS)


def _norm_bwd(dout, n, r, g):
    dn = dout * g
    dx = r * (dn - n * jnp.mean(dn * n, axis=-1, keepdims=True))
    return dx, jnp.sum(dout * n, axis=0, keepdims=True)


def _full(shape):
    return pl.BlockSpec(shape, lambda *_: (0,) * len(shape))


def _row_offset(group, name):
    return sum(PACK_ROWS[n] for n in group[:group.index(name)])


def _wb_spec(name):
    rows = PACK_ROWS[name]
    return pl.BlockSpec((N_CHIPS, rows, D_MODEL), lambda *_: (0, _row_offset(GROUP_B, name) // rows, 0))


def _proj_fwd(x, g1, w_in_t):
    t = x.shape[0]
    tm = 256

    def body(x_ref, g_ref, w_ref, h_ref, p_ref):
        xv = x_ref[...]
        h = ((xv * _rms(xv)) * g_ref[...]).astype(BF16)
        h_ref[...] = h
        p_ref[...] = _dot_nt(h, w_ref[...])

    return pl.pallas_call(
        body, name="proj_fwd", grid=(t // tm,),
        in_specs=[pl.BlockSpec((tm, D_MODEL), lambda i: (i, 0)), _full((1, D_MODEL)), _full((D_IN_PAD, D_MODEL))],
        out_specs=[pl.BlockSpec((tm, D_MODEL), lambda i: (i, 0)), pl.BlockSpec((tm, D_IN_PAD), lambda i: (i, 0))],
        out_shape=[jax.ShapeDtypeStruct((t, D_MODEL), BF16), jax.ShapeDtypeStruct((t, D_IN_PAD), F32)],
        compiler_params=_params(("parallel",)),
    )(x, g1, w_in_t)


_QA_BLK = 2048 // WIDTH_A
_KA_BLK = 2560 // LANES
_VA_BLK = 2688 // LANES
_CQ_BLK = 2816 // Q_LORA
_CKV_BLK = 3072 // LANES
_KR_BLK = 3200 // LANES


_GROUP_A = N_HEADS_A // N_KV_A
_SWA_SCALE = HEAD_DIM_A ** -0.5
_LOG2E = 1.4426950408889634


def _head_cols(v, h):
    return v[:, HEAD_DIM_A * h:HEAD_DIM_A * (h + 1)]


def _head_rows(v, h):
    return v[HEAD_DIM_A * h:HEAD_DIM_A * (h + 1), :]


def _swa_band(n, kp_ref, kc_ref, vp_ref, vc_ref, pq_ref, pp_ref, pc_ref):
    kb = jnp.concatenate([kp_ref[...], kc_ref[...]], axis=0)
    vb = jnp.concatenate([vp_ref[...], vc_ref[...]], axis=0)
    posk = jnp.concatenate([pp_ref[...], pc_ref[...]], axis=0)
    dist = jnp.abs(posk - pq_ref[...])
    ki = lax.broadcasted_iota(jnp.int32, (2 * BLOCK, BLOCK), 0)
    qi = lax.broadcasted_iota(jnp.int32, (2 * BLOCK, BLOCK), 1)
    valid = (ki > qi) & (ki <= qi + WINDOW) & ((n > 0) | (ki >= BLOCK))
    return kb, vb, dist, valid


def _swa_scores_t(st_g, j, h, dist, valid):
    slope = 2.0 ** (-8.0 * (h + 1) / N_HEADS_A)
    st = st_g[:, BLOCK * j:BLOCK * (j + 1)] * (_SWA_SCALE * _LOG2E) - (slope * _LOG2E) * dist
    return jnp.where(valid, st, NEG)


def _group_t(xt, kh):
    return jnp.concatenate([_head_rows(xt, _GROUP_A * kh + j) for j in range(_GROUP_A)], axis=1).astype(BF16)


def _swa_fwd(proj, posc, posr, sinks):
    t = proj.shape[0]
    nb = t // BLOCK

    def body(q_ref, kc_ref, kp_ref, vc_ref, vp_ref, pq_ref, pc_ref, pp_ref, sink_ref, o_ref, l_ref):
        n = pl.program_id(0)
        kb, vb, dist, valid = _swa_band(n, kp_ref, kc_ref, vp_ref, vc_ref, pq_ref, pp_ref, pc_ref)
        q_t, vb_t = q_ref[...].T, vb.T
        out_t, lse = [], []
        for kh in range(N_KV_A):
            st_g = _dot(_head_cols(kb, kh).astype(BF16), _group_t(q_t, kh))
            ps = []
            for j in range(_GROUP_A):
                h = _GROUP_A * kh + j
                st = _swa_scores_t(st_g, j, h, dist, valid)
                sink = sink_ref[0:1, h:h + 1] * _LOG2E
                m = jnp.maximum(jnp.max(st, axis=0, keepdims=True), sink)
                e = jnp.exp2(st - m)
                den = jnp.sum(e, axis=0, keepdims=True) + jnp.exp2(sink - m)
                ps.append((e * (1.0 / den)).astype(BF16))
                lse.append(m + jnp.log(den) * _LOG2E)
            o_g = _dot(_head_rows(vb_t, kh).astype(BF16), jnp.concatenate(ps, axis=1))
            out_t.extend(o_g[:, BLOCK * j:BLOCK * (j + 1)] for j in range(_GROUP_A))
        o_ref[...] = jnp.concatenate(out_t, axis=0).T
        l_ref[...] = jnp.concatenate(lse, axis=0)

    cur = lambda n: (n, 0)
    prev = lambda n: jnp.maximum(n - 1, 0)
    return pl.pallas_call(
        body, name="swa_fwd", grid=(nb,),
        in_specs=[pl.BlockSpec((BLOCK, WIDTH_A), lambda n: (n, _QA_BLK)),
                  pl.BlockSpec((BLOCK, LANES), lambda n: (n, _KA_BLK)),
                  pl.BlockSpec((BLOCK, LANES), lambda n: (prev(n), _KA_BLK)),
                  pl.BlockSpec((BLOCK, LANES), lambda n: (n, _VA_BLK)),
                  pl.BlockSpec((BLOCK, LANES), lambda n: (prev(n), _VA_BLK)),
                  pl.BlockSpec((1, BLOCK), lambda n: (0, n)),
                  pl.BlockSpec((BLOCK, 1), cur),
                  pl.BlockSpec((BLOCK, 1), lambda n: (prev(n), 0)),
                  _full((1, N_HEADS_A))],
        out_specs=[pl.BlockSpec((BLOCK, WIDTH_A), cur), pl.BlockSpec((N_HEADS_A, BLOCK), lambda n: (0, n))],
        out_shape=[jax.ShapeDtypeStruct((t, WIDTH_A), F32), jax.ShapeDtypeStruct((N_HEADS_A, t), F32)],
        compiler_params=_params(("parallel",)),
    )(proj, proj, proj, proj, proj, posr, posc, posc, sinks)


def _rope_coeffs(pos, freq):
    ang = pos * freq
    cosv, sinv = jnp.cos(ang), jnp.sin(ang)
    lane = lax.broadcasted_iota(jnp.int32, ang.shape, 1)
    lo = (lane >= QK_NOPE) & (lane < QK_NOPE + QK_ROPE // 2)
    hi = (lane >= QK_NOPE + QK_ROPE // 2) & (lane < QK_NOPE + QK_ROPE)
    c = jnp.where(lane < QK_NOPE, 1.0, jnp.where(lo | hi, cosv, 0.0))
    s = jnp.where(lo, -sinv, jnp.where(hi, sinv, 0.0))
    return c, s, lo, hi


def _rope(xh, c, s, lo):
    up = pltpu.roll(xh, LANES - QK_ROPE // 2, axis=1)
    dn = pltpu.roll(xh, QK_ROPE // 2, axis=1)
    return xh * c + jnp.where(lo, up, dn) * s


def _unrope(dh, c, s, lo, hi):
    g = dh * s
    up = pltpu.roll(g, LANES - QK_ROPE // 2, axis=1)
    dn = pltpu.roll(g, QK_ROPE // 2, axis=1)
    return dh * c + jnp.where(hi, dn, jnp.where(lo, up, 0.0))


_TQ = 512
_MLA_SCALE = Q_HEAD_B ** -0.5


def _mla_prep_fwd(proj, posc, freq, qan, kvan, wq, wk, wv):
    t = proj.shape[0]
    tm = _TQ
    nb = t // tm

    def body(cq_ref, ckv_ref, kr_ref, pos_ref, f_ref, qan_ref, kvan_ref, wq_ref, wk_ref, wv_ref,
             q_ref, k_ref, qt_ref, kt_ref, v_ref, vt_ref):
        cq = cq_ref[...]
        cqn = ((cq * _rms(cq)) * qan_ref[...]).astype(BF16)
        ckv = ckv_ref[...]
        ckvn = ((ckv * _rms(ckv)) * kvan_ref[...]).astype(BF16)
        qb = _dot(cqn, wq_ref[...])
        kb = _dot(ckvn, wk_ref[...])
        vb = _dot(ckvn, wv_ref[...])
        vbt = vb.T
        c, s, lo, _ = _rope_coeffs(pos_ref[...], f_ref[...])
        kr = _rope(kr_ref[...], c, s, lo)
        for h in range(N_HEADS_B):
            sl = slice(HEAD_PAD * h, HEAD_PAD * (h + 1))
            q_h = _rope(qb[:, sl], c, s, lo)
            k_h = kb[:, sl] + kr
            q_ref[:, sl] = q_h.astype(BF16)
            k_ref[:, sl] = k_h.astype(BF16)
            qt_ref[h, 0] = q_h.T.astype(BF16)
            kt_ref[h, 0] = k_h.T.astype(BF16)
            v_ref[h] = vb[:, V_DIM_B * h:V_DIM_B * (h + 1)].astype(BF16)
            vt_ref[h, 0] = vbt[V_DIM_B * h:V_DIM_B * (h + 1), :].astype(BF16)

    row = lambda i: (i, 0)
    blk4 = lambda d: pl.BlockSpec((N_HEADS_B, 1, d, tm), lambda i: (0, i, 0, 0))
    return pl.pallas_call(
        body, name="mla_prep_fwd", grid=(nb,),
        in_specs=[pl.BlockSpec((tm, Q_LORA), lambda i: (i, _CQ_BLK)),
                  pl.BlockSpec((tm, LANES), lambda i: (i, _CKV_BLK)),
                  pl.BlockSpec((tm, LANES), lambda i: (i, _KR_BLK)),
                  pl.BlockSpec((tm, 1), row), _full((1, LANES)), _full((1, Q_LORA)), _full((1, KV_LORA)),
                  _full((Q_LORA, MLA_W)), _full((KV_LORA, MLA_W)), _full((KV_LORA, N_HEADS_B * V_DIM_B))],
        out_specs=[pl.BlockSpec((tm, MLA_W), row), pl.BlockSpec((tm, MLA_W), row), blk4(HEAD_PAD), blk4(HEAD_PAD),
                   pl.BlockSpec((N_HEADS_B, tm, V_DIM_B), lambda i: (0, i, 0)), blk4(V_DIM_B)],
        out_shape=[jax.ShapeDtypeStruct((t, MLA_W), BF16), jax.ShapeDtypeStruct((t, MLA_W), BF16),
                   jax.ShapeDtypeStruct((N_HEADS_B, nb, HEAD_PAD, tm), BF16),
                   jax.ShapeDtypeStruct((N_HEADS_B, nb, HEAD_PAD, tm), BF16),
                   jax.ShapeDtypeStruct((N_HEADS_B, t, V_DIM_B), BF16),
                   jax.ShapeDtypeStruct((N_HEADS_B, nb, V_DIM_B, tm), BF16)],
        compiler_params=_params(("parallel",)),
    )(proj, proj, proj, posc, freq, qan, kvan, wq, wk, wv)


_MLA_SCALE2 = _MLA_SCALE * _LOG2E


def _scores_t(k, qt, diagonal):
    st = _dot(k, qt) * _MLA_SCALE2
    if diagonal:
        key = lax.broadcasted_iota(jnp.int32, st.shape, 0)
        qry = lax.broadcasted_iota(jnp.int32, st.shape, 1)
        st = jnp.where(key <= qry, st, NEG)
    return st


def _mla_fwd(k, qt, vt, w_src):
    t = k.shape[0]
    nb = t // _TQ

    def body(k_ref, qt_ref, vt_ref, w_ref, o_ref, l_ref, wg_ref, raw_a, raw_b, send_sems, recv_sems, local_sem):
        qi = pl.program_id(1)
        first = (pl.program_id(0) == 0) & (qi == 0)
        last = (pl.program_id(0) == N_HEADS_B - 1) & (qi == nb - 1)

        @pl.when(first)
        def _():
            _gather_start(w_ref, wg_ref, send_sems, recv_sems, local_sem)

        q_t = qt_ref[0, 0]

        def product(kj):
            return _dot(k_ref[pl.ds(pl.multiple_of(kj * _TQ, _TQ), _TQ), :], q_t)

        def update(stats, raw_ref, kj, diagonal=False):
            m, l, acc = stats
            raw = raw_ref[...]
            if diagonal:
                key = lax.broadcasted_iota(jnp.int32, raw.shape, 0)
                qry = lax.broadcasted_iota(jnp.int32, raw.shape, 1)
                raw = jnp.where(key <= qry, raw, NEG)
            m_new = jnp.maximum(m, jnp.max(raw, axis=0, keepdims=True) * _MLA_SCALE2)
            alpha = jnp.exp2(m - m_new)
            p = jnp.exp2(raw * _MLA_SCALE2 - m_new)
            l = alpha * l + jnp.sum(p, axis=0, keepdims=True)
            acc = alpha * acc + _dot(vt_ref[0, kj], p.astype(BF16))
            return m_new, l, acc

        def trip(i, stats):
            raw_b[...] = product(2 * i + 1)
            stats = update(stats, raw_a, 2 * i)
            raw_a[...] = product(2 * i + 2)
            return update(stats, raw_b, 2 * i + 1)

        def tail_even(stats):
            return update(stats, raw_a, qi, True)

        def tail_odd(stats):
            raw_b[...] = product(qi)
            return update(update(stats, raw_a, qi - 1), raw_b, qi, True)

        init = (jnp.full((1, _TQ), NEG, F32), jnp.zeros((1, _TQ), F32), jnp.zeros((V_DIM_B, _TQ), F32))
        raw_a[...] = product(0)
        stats = lax.fori_loop(0, qi // 2, trip, init)
        m, l, acc = lax.cond(qi % 2 == 0, tail_even, tail_odd, stats)
        o_ref[0, 0] = acc / l
        l_ref[0, 0] = m + jnp.log(l) * _LOG2E

        @pl.when(last)
        def _():
            _gather_wait(w_ref, wg_ref, send_sems, recv_sems, local_sem)

    return pl.pallas_call(
        body, name="mla_fwd", grid=(N_HEADS_B, nb),
        in_specs=[pl.BlockSpec((t, HEAD_PAD), lambda h, qi: (0, h)),
                  pl.BlockSpec((1, 1, HEAD_PAD, _TQ), lambda h, qi: (h, qi, 0, 0)),
                  pl.BlockSpec((1, nb, V_DIM_B, _TQ), lambda h, qi: (h, 0, 0, 0)), _HBM],
        out_specs=[pl.BlockSpec((1, 1, V_DIM_B, _TQ), lambda h, qi: (h, qi, 0, 0)),
                   pl.BlockSpec((1, 1, 1, _TQ), lambda h, qi: (h, qi, 0, 0)), _HBM],
        out_shape=[jax.ShapeDtypeStruct((N_HEADS_B, nb, V_DIM_B, _TQ), F32),
                   jax.ShapeDtypeStruct((N_HEADS_B, nb, 1, _TQ), F32),
                   jax.ShapeDtypeStruct((N_CHIPS,) + w_src.shape, w_src.dtype)],
        scratch_shapes=[pltpu.VMEM((_TQ, _TQ), F32), pltpu.VMEM((_TQ, _TQ), F32),
                        pltpu.SemaphoreType.DMA((3,)), pltpu.SemaphoreType.DMA((3,)), pltpu.SemaphoreType.DMA(())],
        compiler_params=_params(("arbitrary", "arbitrary")),
    )(k, qt, vt, w_src)


def _ot_spec(tm, d):
    per = _TQ // tm
    return pl.BlockSpec((N_HEADS_B, 1, d, tm), lambda i: (0, i // per, 0, i % per))


def _mix_out_fwd(out_a, out_bt, proj, x, w_oa, w_ob, wb, g2, g3):
    t = x.shape[0]
    tm = 256

    def body(oa_ref, obt_ref, ga_ref, gb_ref, x_ref, woa_ref, wob_ref, wout_ref, g2_ref, g3_ref,
             mg_ref, y_ref, x1_ref, h2_ref):
        oa = _dot(oa_ref[...].astype(BF16), woa_ref[...])
        obt = obt_ref[...].reshape(N_HEADS_B * V_DIM_B, tm).astype(BF16)
        ob = _dot_tn(obt, wob_ref[...])
        merged = (jax.nn.sigmoid(ga_ref[...]) * oa + jax.nn.sigmoid(gb_ref[...]) * ob).astype(BF16)
        mg_ref[...] = merged
        y = _dot(merged, wout_ref[...].reshape(D_MODEL, D_MODEL))
        y_ref[...] = y
        x1 = x_ref[...] + (y * _rms(y)) * g2_ref[...]
        x1_ref[...] = x1
        h2_ref[...] = ((x1 * _rms(x1)) * g3_ref[...]).astype(BF16)

    row = lambda i: (i, 0)
    blk = pl.BlockSpec((tm, D_MODEL), row)
    return pl.pallas_call(
        body, name="mix_out_fwd", grid=(t // tm,),
        in_specs=[pl.BlockSpec((tm, WIDTH_A), row), _ot_spec(tm, V_DIM_B), pl.BlockSpec((tm, D_MODEL), lambda i: (i, 0)),
                  pl.BlockSpec((tm, D_MODEL), lambda i: (i, 1)), blk,
                  _full((WIDTH_A, D_MODEL)), _full((N_HEADS_B * V_DIM_B, D_MODEL)), _wb_spec("w_out"),
                  _full((1, D_MODEL)), _full((1, D_MODEL))],
        out_specs=[blk, blk, blk, blk],
        out_shape=[jax.ShapeDtypeStruct((t, D_MODEL), BF16), jax.ShapeDtypeStruct((t, D_MODEL), F32),
                   jax.ShapeDtypeStruct((t, D_MODEL), F32), jax.ShapeDtypeStruct((t, D_MODEL), BF16)],
        compiler_params=_params(("parallel",)),
    )(out_a, out_bt, proj, proj, x, w_oa, w_ob, wb, g2, g3)


_TM_MLP = 512


def _up_fwd(h2, wb):
    t = h2.shape[0]
    tm = _TM_MLP

    def body(h_ref, w_ref, a_ref):
        hv = h_ref[...]
        for j in range(N_CHIPS):
            u = _dot(hv, w_ref[j])
            a_ref[:, D_MODEL * j:D_MODEL * (j + 1)] = jnp.square(jnp.maximum(u, 0.0)).astype(BF16)

    return pl.pallas_call(
        body, name="up_fwd", grid=(t // tm,),
        in_specs=[pl.BlockSpec((tm, D_MODEL), lambda i: (i, 0)), _wb_spec("w_up")],
        out_specs=pl.BlockSpec((tm, D_FF), lambda i: (i, 0)),
        out_shape=jax.ShapeDtypeStruct((t, D_FF), BF16),
        compiler_params=_params(("parallel",)),
    )(h2, wb)


def _down_fwd_loss(a, wb, x1, target, g4):
    t = a.shape[0]
    tm = _TM_MLP

    def body(a_ref, w_ref, x1_ref, tg_ref, g_ref, dx2_ref, dyd_ref, dg_ref, loss_ref):
        @pl.when(pl.program_id(0) == 0)
        def _():
            dg_ref[...] = jnp.zeros(dg_ref.shape, F32)
            loss_ref[...] = jnp.zeros(loss_ref.shape, F32)

        yd = _dot(a_ref[...], w_ref[...].reshape(D_FF, D_MODEL))
        r = _rms(yd)
        n = yd * r
        diff = (x1_ref[...] + n * g_ref[...]) - tg_ref[...]
        loss_ref[...] += 0.5 * jnp.sum(jnp.mean(diff * diff, axis=-1, keepdims=True), axis=0, keepdims=True)
        dx2 = diff * (1.0 / D_MODEL)
        dx2_ref[...] = dx2
        dyd, dg = _norm_bwd(dx2, n, r, g_ref[...])
        dyd_ref[...] = dyd.astype(BF16)
        dg_ref[...] += dg

    row = lambda i: (i, 0)
    blk = pl.BlockSpec((tm, D_MODEL), row)
    return pl.pallas_call(
        body, name="down_fwd_loss", grid=(t // tm,),
        in_specs=[pl.BlockSpec((tm, D_FF), row), _wb_spec("w_down"), blk, blk, _full((1, D_MODEL))],
        out_specs=[blk, blk, _full((1, D_MODEL)), _full((1, LANES))],
        out_shape=[jax.ShapeDtypeStruct((t, D_MODEL), F32), jax.ShapeDtypeStruct((t, D_MODEL), BF16),
                   jax.ShapeDtypeStruct((1, D_MODEL), F32), jax.ShapeDtypeStruct((1, LANES), F32)],
        compiler_params=_params(("arbitrary",)),
    )(a, wb, x1, target, g4)


def _matmul_tn(a, b, name, tm, tn, tk=1024):
    t, m = a.shape
    n = b.shape[1]
    tk = min(tk, t)
    nk = t // tk

    def body(a_ref, b_ref, o_ref):
        @pl.when(pl.program_id(2) == 0)
        def _():
            o_ref[...] = jnp.zeros(o_ref.shape, F32)

        o_ref[...] += _dot_tn(a_ref[...].astype(BF16), b_ref[...].astype(BF16))

    return pl.pallas_call(
        body, name=name, grid=(m // tm, n // tn, nk),
        in_specs=[pl.BlockSpec((tk, tm), lambda i, j, k: (k, i)), pl.BlockSpec((tk, tn), lambda i, j, k: (k, j))],
        out_specs=pl.BlockSpec((tm, tn), lambda i, j, k: (i, j)),
        out_shape=jax.ShapeDtypeStruct((m, n), F32),
        compiler_params=_params(("parallel", "parallel", "arbitrary")),
    )(a, b)


def _dw_into_blocks(a, b, weight, tm, tk, buf=None):
    t, m = a.shape
    n = b.shape[1]
    nk = t // tk
    rows = PACK_ROWS[weight]
    br = min(tm, rows)
    chips = tm // br
    first = _row_offset(GROUP_B, weight) // br
    per_chip = rows // br
    if weight == "w_up":
        out_map = lambda i, j, k: (j, first + i, 0)
    elif chips > 1:
        out_map = lambda i, j, k: (i, first, 0)
    else:
        out_map = lambda i, j, k: (i // per_chip, first + i % per_chip, 0)

    def body(a_ref, b_ref, *rest):
        o_ref = rest[-1]

        @pl.when(pl.program_id(2) == 0)
        def _():
            o_ref[...] = jnp.zeros(o_ref.shape, F32)

        o_ref[...] += _dot_tn(a_ref[...].astype(BF16), b_ref[...].astype(BF16)).reshape(o_ref.shape)

    in_specs = [pl.BlockSpec((tk, tm), lambda i, j, k: (k, i)), pl.BlockSpec((tk, D_MODEL), lambda i, j, k: (k, j))]
    operands = [a, b]
    if buf is not None:
        in_specs.append(pl.BlockSpec(memory_space=pl.ANY))
        operands.append(buf)
    total = sum(PACK_ROWS[w] for w in GROUP_B)
    return pl.pallas_call(
        body, name="dw_" + weight[2:], grid=(m // tm, n // D_MODEL, nk),
        in_specs=in_specs, out_specs=pl.BlockSpec((chips, br, D_MODEL), out_map),
        out_shape=jax.ShapeDtypeStruct((N_CHIPS, total, D_MODEL), F32),
        input_output_aliases={} if buf is None else {2: 0},
        compiler_params=_params(("parallel", "parallel", "arbitrary")),
    )(*operands)


def _down_bwd(dyd, wb, a):
    t = dyd.shape[0]
    tm = _TM_MLP

    def body(d_ref, w_ref, a_ref, du_ref):
        da = _dot_nt(d_ref[...], w_ref[...].reshape(D_FF, D_MODEL))
        du_ref[...] = (da * (2.0 * jnp.sqrt(a_ref[...].astype(F32)))).astype(BF16)

    row = lambda i: (i, 0)
    return pl.pallas_call(
        body, name="down_bwd", grid=(t // tm,),
        in_specs=[pl.BlockSpec((tm, D_MODEL), row), _wb_spec("w_down"), pl.BlockSpec((tm, D_FF), row)],
        out_specs=pl.BlockSpec((tm, D_FF), row),
        out_shape=jax.ShapeDtypeStruct((t, D_FF), BF16),
        compiler_params=_params(("parallel",)),
    )(dyd, wb, a)


def _up_bwd(du, wb, x1, dx2, y, g3, g2):
    t = du.shape[0]
    tm = _TM_MLP

    def body(du_ref, w_ref, x1_ref, dx2_ref, y_ref, g3_ref, g2_ref, dx1_ref, dy_ref, dg3_ref, dg2_ref):
        @pl.when(pl.program_id(0) == 0)
        def _():
            dg3_ref[...] = jnp.zeros(dg3_ref.shape, F32)
            dg2_ref[...] = jnp.zeros(dg2_ref.shape, F32)

        dh2 = _dot_nt(du_ref[:, 0:D_MODEL], w_ref[0])
        for j in range(1, N_CHIPS):
            dh2 = dh2 + _dot_nt(du_ref[:, D_MODEL * j:D_MODEL * (j + 1)], w_ref[j])
        x1 = x1_ref[...]
        r3 = _rms(x1)
        d3, dg3 = _norm_bwd(dh2, x1 * r3, r3, g3_ref[...])
        dx1 = dx2_ref[...] + d3
        dx1_ref[...] = dx1
        dg3_ref[...] += dg3
        y = y_ref[...]
        r2 = _rms(y)
        dy, dg2 = _norm_bwd(dx1, y * r2, r2, g2_ref[...])
        dy_ref[...] = dy.astype(BF16)
        dg2_ref[...] += dg2

    row = lambda i: (i, 0)
    blk = pl.BlockSpec((tm, D_MODEL), row)
    return pl.pallas_call(
        body, name="up_bwd", grid=(t // tm,),
        in_specs=[pl.BlockSpec((tm, D_FF), row), _wb_spec("w_up"),
                  blk, blk, blk, _full((1, D_MODEL)), _full((1, D_MODEL))],
        out_specs=[blk, blk, _full((1, D_MODEL)), _full((1, D_MODEL))],
        out_shape=[jax.ShapeDtypeStruct((t, D_MODEL), F32), jax.ShapeDtypeStruct((t, D_MODEL), BF16),
                   jax.ShapeDtypeStruct((1, D_MODEL), F32), jax.ShapeDtypeStruct((1, D_MODEL), F32)],
        compiler_params=_params(("arbitrary",)),
    )(du, wb, x1, dx2, y, g3, g2)


def _mix_out_bwd(dy, out_a, out_bt, lse_b, proj, w_oa, w_ob, wb):
    t = dy.shape[0]
    tm = 256
    nb = t // _TQ

    def body(dy_ref, oa_ref, obt_ref, lse_ref, ga_ref, gb_ref, woa_ref, wob_ref, wout_ref,
             doa_ref, dob_ref, dga_ref, dgb_ref, da_ref, db_ref, dbt_ref, dela_ref, lrep_ref, drep_ref):
        dm = _dot_nt(dy_ref[...], wout_ref[...].reshape(D_MODEL, D_MODEL))
        out_a_v = oa_ref[...]
        out_bt_v = obt_ref[...].reshape(N_HEADS_B * V_DIM_B, tm)
        oa = _dot(out_a_v.astype(BF16), woa_ref[...])
        ob = _dot_tn(out_bt_v.astype(BF16), wob_ref[...])
        sa, sb = jax.nn.sigmoid(ga_ref[...]), jax.nn.sigmoid(gb_ref[...])
        doa = (dm * sa).astype(BF16)
        dob = (dm * sb).astype(BF16)
        doa_ref[...] = doa
        dob_ref[...] = dob
        dga_ref[...] = (dm * oa * (sa * (1.0 - sa))).astype(BF16)
        dgb_ref[...] = (dm * ob * (sb * (1.0 - sb))).astype(BF16)
        d_out_a = _dot_nt(doa, woa_ref[...])
        da_ref[...] = d_out_a
        prod_at = (d_out_a * out_a_v).T
        dela_ref[...] = jnp.concatenate(
            [jnp.sum(_head_rows(prod_at, h), axis=0, keepdims=True) for h in range(N_HEADS_A)], axis=0)
        d_out_b = _dot_nt(dob, wob_ref[...])
        d_out_bt = _dot_nt(wob_ref[...], dob)
        prod_bt = d_out_bt * out_bt_v
        for h in range(N_HEADS_B):
            db_ref[h] = d_out_b[:, V_DIM_B * h:V_DIM_B * (h + 1)].astype(BF16)
            dbt_ref[h, 0] = d_out_bt[V_DIM_B * h:V_DIM_B * (h + 1), :].astype(BF16)
            delta = jnp.sum(prod_bt[V_DIM_B * h:V_DIM_B * (h + 1), :], axis=0, keepdims=True)
            drep_ref[h] = jnp.broadcast_to(delta, (LANES, tm)).T
            lrep_ref[h] = jnp.broadcast_to(lse_ref[h, 0], (LANES, tm)).T

    row = lambda i: (i, 0)
    blk = pl.BlockSpec((tm, D_MODEL), row)
    rep_spec = pl.BlockSpec((N_HEADS_B, tm, LANES), lambda i: (0, i, 0))
    return pl.pallas_call(
        body, name="mix_out_bwd", grid=(t // tm,),
        in_specs=[blk, pl.BlockSpec((tm, WIDTH_A), row), _ot_spec(tm, V_DIM_B), _ot_spec(tm, 1),
                  pl.BlockSpec((tm, D_MODEL), lambda i: (i, 0)), pl.BlockSpec((tm, D_MODEL), lambda i: (i, 1)),
                  _full((WIDTH_A, D_MODEL)), _full((N_HEADS_B * V_DIM_B, D_MODEL)), _wb_spec("w_out")],
        out_specs=[blk, blk, blk, blk, pl.BlockSpec((tm, WIDTH_A), row),
                   pl.BlockSpec((N_HEADS_B, tm, V_DIM_B), lambda i: (0, i, 0)), _ot_spec(tm, V_DIM_B),
                   pl.BlockSpec((N_HEADS_A, tm), lambda i: (0, i)), rep_spec, rep_spec],
        out_shape=[jax.ShapeDtypeStruct((t, D_MODEL), BF16)] * 4
        + [jax.ShapeDtypeStruct((t, WIDTH_A), F32), jax.ShapeDtypeStruct((N_HEADS_B, t, V_DIM_B), BF16),
           jax.ShapeDtypeStruct((N_HEADS_B, nb, V_DIM_B, _TQ), BF16), jax.ShapeDtypeStruct((N_HEADS_A, t), F32),
           jax.ShapeDtypeStruct((N_HEADS_B, t, LANES), F32), jax.ShapeDtypeStruct((N_HEADS_B, t, LANES), F32)],
        compiler_params=_params(("parallel",)),
    )(dy, out_a, out_bt, lse_b, proj, proj, w_oa, w_ob, wb)


def _dw_ob(out_bt, dob):
    t = dob.shape[0]
    nb = t // _TQ

    def body(obt_ref, dob_ref, o_ref):
        @pl.when(pl.program_id(0) == 0)
        def _():
            o_ref[...] = jnp.zeros(o_ref.shape, F32)

        obt = obt_ref[...].reshape(N_HEADS_B * V_DIM_B, _TQ).astype(BF16)
        o_ref[...] += _dot(obt, dob_ref[...])

    return pl.pallas_call(
        body, name="dw_o_b", grid=(nb,),
        in_specs=[pl.BlockSpec((N_HEADS_B, 1, V_DIM_B, _TQ), lambda i: (0, i, 0, 0)),
                  pl.BlockSpec((_TQ, D_MODEL), lambda i: (i, 0))],
        out_specs=_full((N_HEADS_B * V_DIM_B, D_MODEL)),
        out_shape=jax.ShapeDtypeStruct((N_HEADS_B * V_DIM_B, D_MODEL), F32),
        compiler_params=_params(("arbitrary",)),
    )(out_bt, dob)


def _mla_bwd(q, k, qt, kt, vt, d_out, d_out_t, lrep, drep, gp):
    t = q.shape[0]
    nb = t // _TQ

    def body(k_ref, kt_ref, vt_ref, q_ref, qt_ref, do_ref, dot_ref, l_ref, d_ref, gp_ref,
             dq_ref, dkt_ref, dvt_ref, land_ref, send_sems, recv_sems):
        kj = pl.program_id(1)

        @pl.when((pl.program_id(0) == 0) & (kj == 0))
        def _():
            _scatter_start(gp_ref, land_ref, send_sems, recv_sems)

        @pl.when(kj == 0)
        def _():
            dq_ref[...] = jnp.zeros(dq_ref.shape, F32)

        kv, k_t, v_t = k_ref[...], kt_ref[0, 0], vt_ref[0, 0]

        def rows_of(qi):
            return pl.ds(pl.multiple_of(qi * _TQ, _TQ), _TQ)

        def products(qi, diagonal=False):
            s = _dot(q_ref[rows_of(qi), :], k_t) * _MLA_SCALE2
            if diagonal:
                qry = lax.broadcasted_iota(jnp.int32, s.shape, 0)
                key = lax.broadcasted_iota(jnp.int32, s.shape, 1)
                s = jnp.where(key <= qry, s, NEG)
            return s, _dot(do_ref[0, rows_of(qi), :], v_t)

        def update(carry, prods, qi):
            dkt, dvt = carry
            s, dp = prods
            lse, delta = l_ref[0, rows_of(qi), :], d_ref[0, rows_of(qi), :]
            ps, dss = [], []
            for c in range(_TQ // LANES):
                strip = slice(LANES * c, LANES * (c + 1))
                p = jnp.exp2(s[:, strip] - lse)
                ps.append(p.astype(BF16))
                dss.append((p * (dp[:, strip] - delta) * _MLA_SCALE).astype(BF16))
            p_b, ds_b = jnp.concatenate(ps, axis=1), jnp.concatenate(dss, axis=1)
            dvt = dvt + _dot(dot_ref[0, qi], p_b)
            dkt = dkt + _dot(qt_ref[0, qi], ds_b)
            dq_ref[rows_of(qi), :] += _dot(ds_b, kv)
            return dkt, dvt

        def pair(i, carry):
            qa = kj + 1 + 2 * i
            pa, pb = products(qa), products(qa + 1)
            return update(update(carry, pa, qa), pb, qa + 1)

        init = (jnp.zeros((HEAD_PAD, _TQ), F32), jnp.zeros((V_DIM_B, _TQ), F32))
        carry = update(init, products(kj, True), kj)
        pairs = (nb - 1 - kj) // 2
        carry = lax.fori_loop(0, pairs, pair, carry)
        dkt, dvt = lax.fori_loop(kj + 1 + 2 * pairs, nb, lambda qi, cr: update(cr, products(qi), qi), carry)
        dkt_ref[0, 0] = dkt
        dvt_ref[0, 0] = dvt

        @pl.when((pl.program_id(0) == N_HEADS_B - 1) & (kj == nb - 1))
        def _():
            _scatter_wait(gp_ref, land_ref, send_sems, recv_sems)

    head4 = lambda d: pl.BlockSpec((1, nb, d, _TQ), lambda h, kj: (h, 0, 0, 0))
    blk4 = lambda d: pl.BlockSpec((1, 1, d, _TQ), lambda h, kj: (h, kj, 0, 0))
    head3 = lambda d: pl.BlockSpec((1, t, d), lambda h, kj: (h, 0, 0))
    per_head = pl.BlockSpec((t, HEAD_PAD), lambda h, kj: (0, h))
    return pl.pallas_call(
        body, name="mla_bwd", grid=(N_HEADS_B, nb),
        in_specs=[pl.BlockSpec((_TQ, HEAD_PAD), lambda h, kj: (kj, h)), blk4(HEAD_PAD), blk4(V_DIM_B),
                  per_head, head4(HEAD_PAD), head3(V_DIM_B), head4(V_DIM_B), head3(LANES), head3(LANES), _HBM],
        out_specs=[per_head, blk4(HEAD_PAD), blk4(V_DIM_B), _HBM],
        out_shape=[jax.ShapeDtypeStruct((t, MLA_W), F32), jax.ShapeDtypeStruct((N_HEADS_B, nb, HEAD_PAD, _TQ), F32),
                   jax.ShapeDtypeStruct((N_HEADS_B, nb, V_DIM_B, _TQ), F32),
                   jax.ShapeDtypeStruct((3,) + gp.shape[1:], gp.dtype)],
        scratch_shapes=[pltpu.SemaphoreType.DMA((3,)), pltpu.SemaphoreType.DMA((3,))],
        compiler_params=_params(("arbitrary", "arbitrary")),
    )(k, kt, vt, q, qt, d_out, d_out_t, lrep, drep, gp)


def _mla_prep_bwd(dq, dkt, dvt, proj, posc, freq, qan, kvan, wq, wk, wv, swap_src):
    t = dq.shape[0]
    tm = _TQ

    def body(dq_ref, dkt_ref, dvt_ref, cq_ref, ckv_ref, pos_ref, f_ref, qan_ref, kvan_ref, wq_ref, wk_ref, wv_ref, src_ref,
             dcq_ref, dckv_ref, dkr_ref, dwq_ref, dwk_ref, dwv_ref, dqan_ref, dkvan_ref, got_ref, send_sem, recv_sem):
        swap = _sibling_copy(src_ref, got_ref, send_sem, recv_sem)

        @pl.when(pl.program_id(0) == 0)
        def _():
            swap.start()
            for r in (dwq_ref, dwk_ref, dwv_ref, dqan_ref, dkvan_ref):
                r[...] = jnp.zeros(r.shape, F32)

        cq = cq_ref[...]
        rq = _rms(cq)
        nq_ = cq * rq
        cqn = (nq_ * qan_ref[...]).astype(BF16)
        ckv = ckv_ref[...]
        rkv = _rms(ckv)
        nkv = ckv * rkv
        ckvn = (nkv * kvan_ref[...]).astype(BF16)
        c, s, lo, hi = _rope_coeffs(pos_ref[...], f_ref[...])
        dkr = jnp.zeros((tm, LANES), F32)
        dqb, dkb = [], []
        for h in range(N_HEADS_B):
            dqb.append(_unrope(dq_ref[:, HEAD_PAD * h:HEAD_PAD * (h + 1)], c, s, lo, hi).astype(BF16))
            dk_h = dkt_ref[h, 0].T
            dkr = dkr + dk_h
            dkb.append(dk_h.astype(BF16))
        dqb, dkb = jnp.concatenate(dqb, axis=1), jnp.concatenate(dkb, axis=1)
        dkr_ref[...] = jnp.where(lo | hi, _unrope(dkr, c, s, lo, hi), 0.0).astype(BF16)
        dvb = dvt_ref[...].reshape(N_HEADS_B * V_DIM_B, tm).T.astype(BF16)
        dwq_ref[...] += _dot_tn(cqn, dqb)
        dwk_ref[...] += _dot_tn(ckvn, dkb)
        dwv_ref[...] += _dot_tn(ckvn, dvb)
        dcqn = _dot_nt(dqb, wq_ref[...])
        dckvn = _dot_nt(dkb, wk_ref[...]) + _dot_nt(dvb, wv_ref[...])
        dcq, dqan = _norm_bwd(dcqn, nq_, rq, qan_ref[...])
        dckv, dkvan = _norm_bwd(dckvn, nkv, rkv, kvan_ref[...])
        dcq_ref[...] = dcq.astype(BF16)
        dckv_ref[...] = dckv.astype(BF16)
        dqan_ref[...] += dqan
        dkvan_ref[...] += dkvan

        @pl.when(pl.program_id(0) == t // tm - 1)
        def _():
            swap.wait_recv()
            swap.wait_send()

    row = lambda i: (i, 0)
    vw = N_HEADS_B * V_DIM_B
    return pl.pallas_call(
        body, name="mla_prep_bwd", grid=(t // tm,),
        in_specs=[pl.BlockSpec((tm, MLA_W), row), pl.BlockSpec((N_HEADS_B, 1, HEAD_PAD, tm), lambda i: (0, i, 0, 0)),
                  pl.BlockSpec((N_HEADS_B, 1, V_DIM_B, tm), lambda i: (0, i, 0, 0)),
                  pl.BlockSpec((tm, Q_LORA), lambda i: (i, _CQ_BLK)),
                  pl.BlockSpec((tm, LANES), lambda i: (i, _CKV_BLK)),
                  pl.BlockSpec((tm, 1), row), _full((1, LANES)), _full((1, Q_LORA)), _full((1, KV_LORA)),
                  _full((Q_LORA, MLA_W)), _full((KV_LORA, MLA_W)), _full((KV_LORA, vw)), _HBM],
        out_specs=[pl.BlockSpec((tm, Q_LORA), row), pl.BlockSpec((tm, LANES), row), pl.BlockSpec((tm, LANES), row),
                   _full((Q_LORA, MLA_W)), _full((KV_LORA, MLA_W)), _full((KV_LORA, vw)),
                   _full((1, Q_LORA)), _full((1, KV_LORA)), _HBM],
        out_shape=[jax.ShapeDtypeStruct((t, Q_LORA), BF16), jax.ShapeDtypeStruct((t, LANES), BF16),
                   jax.ShapeDtypeStruct((t, LANES), BF16),
                   jax.ShapeDtypeStruct((Q_LORA, MLA_W), F32), jax.ShapeDtypeStruct((KV_LORA, MLA_W), F32),
                   jax.ShapeDtypeStruct((KV_LORA, vw), F32),
                   jax.ShapeDtypeStruct((1, Q_LORA), F32), jax.ShapeDtypeStruct((1, KV_LORA), F32),
                   jax.ShapeDtypeStruct(swap_src.shape, swap_src.dtype)],
        scratch_shapes=[pltpu.SemaphoreType.DMA(()), pltpu.SemaphoreType.DMA(())],
        compiler_params=_params(("arbitrary",)),
    )(dq, dkt, dvt, proj, proj, posc, freq, qan, kvan, wq, wk, wv, swap_src)


def _swa_bwd(proj, d_out, lse, delta, posc, posr, sinks):
    t = proj.shape[0]
    nb = t // BLOCK

    def body(q_ref, kc_ref, kp_ref, vc_ref, vp_ref, do_ref, l_ref, d_ref, pq_ref, pc_ref, pp_ref, sink_ref,
             dq_ref, dk_ref, dv_ref, ds_ref, dkb_s, dvb_s, dk_carry, dv_carry):
        n = pl.program_id(0)

        @pl.when(n == 0)
        def _():
            ds_ref[...] = jnp.zeros(ds_ref.shape, F32)
            dk_carry[...] = jnp.zeros(dk_carry.shape, F32)
            dv_carry[...] = jnp.zeros(dv_carry.shape, F32)

        @pl.when(n < nb)
        def _():
            kb, vb, dist, valid = _swa_band(n, kp_ref, kc_ref, vp_ref, vc_ref, pq_ref, pp_ref, pc_ref)
            qv, dov = q_ref[...], do_ref[...]
            q_t, do_t, kb_t = qv.T, dov.T, kb.T
            lane = lax.broadcasted_iota(jnp.int32, (1, LANES), 1)
            dsink = jnp.zeros((1, LANES), F32)
            dq_t = []
            for kh in range(N_KV_A):
                heads = range(_GROUP_A * kh, _GROUP_A * (kh + 1))
                st_g = _dot(_head_cols(kb, kh).astype(BF16), _group_t(q_t, kh))
                dpt_g = _dot(_head_cols(vb, kh).astype(BF16), _group_t(do_t, kh))
                pts, dsts = [], []
                for j, h in enumerate(heads):
                    st = _swa_scores_t(st_g, j, h, dist, valid)
                    l_h, d_h = l_ref[h:h + 1, :], d_ref[h:h + 1, :]
                    pt = jnp.exp2(st - l_h)
                    p_sink = jnp.exp2(sink_ref[0:1, h:h + 1] * _LOG2E - l_h)
                    dsink = jnp.where(lane == h, jnp.sum(-p_sink * d_h, axis=1, keepdims=True), dsink)
                    dst = pt * (dpt_g[:, BLOCK * j:BLOCK * (j + 1)] - d_h) * _SWA_SCALE
                    pts.append(pt.astype(BF16))
                    dsts.append(dst.astype(BF16))
                pt_g, dst_g = jnp.concatenate(pts, axis=1), jnp.concatenate(dsts, axis=1)
                q_g = jnp.concatenate([_head_cols(qv, h) for h in heads], axis=0).astype(BF16)
                do_g = jnp.concatenate([_head_cols(dov, h) for h in heads], axis=0).astype(BF16)
                dkb_s[:, HEAD_DIM_A * kh:HEAD_DIM_A * (kh + 1)] = _dot(dst_g, q_g)
                dvb_s[:, HEAD_DIM_A * kh:HEAD_DIM_A * (kh + 1)] = _dot(pt_g, do_g)
                dq_g = _dot(_head_rows(kb_t, kh).astype(BF16), dst_g)
                dq_t.extend(dq_g[:, BLOCK * j:BLOCK * (j + 1)] for j in range(_GROUP_A))
            dq_ref[...] = jnp.concatenate(dq_t, axis=0).T
            ds_ref[...] += dsink
            dk_ref[...] = dk_carry[...] + dkb_s[0:BLOCK, :]
            dv_ref[...] = dv_carry[...] + dvb_s[0:BLOCK, :]
            dk_carry[...] = dkb_s[BLOCK:2 * BLOCK, :]
            dv_carry[...] = dvb_s[BLOCK:2 * BLOCK, :]

        @pl.when(n == nb)
        def _():
            dk_ref[...] = dk_carry[...]
            dv_ref[...] = dv_carry[...]

    cur = lambda n: (jnp.minimum(n, nb - 1), 0)
    cur_t = lambda n: (0, jnp.minimum(n, nb - 1))
    prv = lambda n: jnp.maximum(jnp.minimum(n, nb - 1) - 1, 0)
    out_prev = lambda n: (jnp.maximum(n - 1, 0), 0)
    return pl.pallas_call(
        body, name="swa_bwd", grid=(nb + 1,),
        in_specs=[pl.BlockSpec((BLOCK, WIDTH_A), lambda n: (jnp.minimum(n, nb - 1), _QA_BLK)),
                  pl.BlockSpec((BLOCK, LANES), lambda n: (jnp.minimum(n, nb - 1), _KA_BLK)),
                  pl.BlockSpec((BLOCK, LANES), lambda n: (prv(n), _KA_BLK)),
                  pl.BlockSpec((BLOCK, LANES), lambda n: (jnp.minimum(n, nb - 1), _VA_BLK)),
                  pl.BlockSpec((BLOCK, LANES), lambda n: (prv(n), _VA_BLK)),
                  pl.BlockSpec((BLOCK, WIDTH_A), cur), pl.BlockSpec((N_HEADS_A, BLOCK), cur_t),
                  pl.BlockSpec((N_HEADS_A, BLOCK), cur_t), pl.BlockSpec((1, BLOCK), cur_t),
                  pl.BlockSpec((BLOCK, 1), cur), pl.BlockSpec((BLOCK, 1), lambda n: (prv(n), 0)),
                  _full((1, N_HEADS_A))],
        out_specs=[pl.BlockSpec((BLOCK, WIDTH_A), cur), pl.BlockSpec((BLOCK, LANES), out_prev),
                   pl.BlockSpec((BLOCK, LANES), out_prev), _full((1, LANES))],
        out_shape=[jax.ShapeDtypeStruct((t, WIDTH_A), F32), jax.ShapeDtypeStruct((t, LANES), F32),
                   jax.ShapeDtypeStruct((t, LANES), F32), jax.ShapeDtypeStruct((1, LANES), F32)],
        scratch_shapes=[pltpu.VMEM((2 * BLOCK, LANES), F32), pltpu.VMEM((2 * BLOCK, LANES), F32),
                        pltpu.VMEM((BLOCK, LANES), F32), pltpu.VMEM((BLOCK, LANES), F32)],
        compiler_params=_params(("arbitrary",)),
    )(proj, proj, proj, proj, proj, d_out, lse, delta, posr, posc, posc, sinks)


def _in_bwd(dproj, w_in_t, x, dx1, g1, gp):
    t = x.shape[0]
    tm = 256
    steps = t // tm

    def body(dp_ref, w_ref, x_ref, dx1_ref, g_ref, gp_ref, dx_ref, dg_ref, land_ref, send_sems, recv_sems):
        i = pl.program_id(0)

        @pl.when(i == 0)
        def _():
            dg_ref[...] = jnp.zeros(dg_ref.shape, F32)
            _scatter_start(gp_ref, land_ref, send_sems, recv_sems)

        dh = _dot(dp_ref[...], w_ref[...])
        xv = x_ref[...]
        r = _rms(xv)
        dx, dg = _norm_bwd(dh, xv * r, r, g_ref[...])
        dx_ref[...] = dx1_ref[...] + dx
        dg_ref[...] += dg

        @pl.when(i == steps - 1)
        def _():
            _scatter_wait(gp_ref, land_ref, send_sems, recv_sems)

    row = lambda i: (i, 0)
    blk = pl.BlockSpec((tm, D_MODEL), row)
    return pl.pallas_call(
        body, name="in_bwd", grid=(steps,),
        in_specs=[pl.BlockSpec((tm, D_IN_PAD), row), _full((D_IN_PAD, D_MODEL)), blk, blk, _full((1, D_MODEL)), _HBM],
        out_specs=[blk, _full((1, D_MODEL)), _HBM],
        out_shape=[jax.ShapeDtypeStruct((t, D_MODEL), F32), jax.ShapeDtypeStruct((1, D_MODEL), F32),
                   jax.ShapeDtypeStruct((3,) + gp.shape[1:], gp.dtype)],
        scratch_shapes=[pltpu.SemaphoreType.DMA((3,)), pltpu.SemaphoreType.DMA((3,))],
        compiler_params=_params(("arbitrary",)),
    )(dproj, w_in_t, x, dx1, g1, gp)


def _adamw_store(w, g, m, v, out_refs):
    g_out, d_out, m_out, v_out = out_refs
    m_new = ADAM_B1 * m + (1.0 - ADAM_B1) * g
    v_new = ADAM_B2 * v + (1.0 - ADAM_B2) * jnp.square(g)
    m_hat = m_new / (1.0 - ADAM_B1 ** ADAM_STEP)
    v_hat = v_new / (1.0 - ADAM_B2 ** ADAM_STEP)
    g_out[...] = g
    d_out[...] = -ADAM_LR * (m_hat / (jnp.sqrt(v_hat) + ADAM_EPS) + ADAM_WD * w)
    m_out[...] = m_new
    v_out[...] = v_new


_SMALL_SLOTS = {"pre_norm_mix": (0, 0, D_MODEL), "post_norm_mix": (1, 0, D_MODEL), "pre_norm_mlp": (2, 0, D_MODEL),
                "post_norm_mlp": (3, 0, D_MODEL), "q_a_norm": (4, 0, Q_LORA), "kv_a_norm": (4, Q_LORA, KV_LORA),
                "sinks": (4, Q_LORA + KV_LORA, N_HEADS_A)}
_LOSS_ROW = 5


def _adamw_small(red, w, m, v):
    names = tuple(_SMALL_SLOTS)
    n = len(names)

    def body(*refs):
        red_ref, ws, ms, vs, outs = refs[0], refs[1:1 + n], refs[1 + n:1 + 2 * n], refs[1 + 2 * n:1 + 3 * n], refs[1 + 3 * n:]
        for k, name in enumerate(names):
            row, lane, width = _SMALL_SLOTS[name]
            g = red_ref[row:row + 1, lane:lane + width]
            _adamw_store(ws[k][...], g, ms[k][...], vs[k][...], outs[4 * k:4 * k + 4])

    vmem = pl.BlockSpec(memory_space=pltpu.VMEM)
    res = pl.pallas_call(
        body, name="adamw_small", in_specs=[vmem] * (1 + 3 * n), out_specs=[vmem] * (4 * n),
        out_shape=[jax.ShapeDtypeStruct(w[name].shape, F32) for name in names for _ in range(4)],
    )(red, *[w[k] for k in names], *[m[k] for k in names], *[v[k] for k in names])
    return {name: res[4 * k:4 * k + 4] for k, name in enumerate(names)}


def _adamw(w, g_parts, m, v, name, block, g_row_off=0):
    r, c = w.shape
    br, bc = block
    ng = len(g_parts)

    def body(*refs):
        w_ref, g_refs, m_ref, v_ref = refs[0], refs[1:1 + ng], refs[1 + ng], refs[2 + ng]
        g = g_refs[0][...]
        for gr in g_refs[1:]:
            g = g + gr[...]
        _adamw_store(w_ref[...], g, m_ref[...], v_ref[...], refs[3 + ng:])

    assert g_row_off % br == 0 and r % br == 0 and c % bc == 0
    blk = pl.BlockSpec(block, lambda i, j: (i, j))
    g_blk = pl.BlockSpec(block, lambda i, j: (i + g_row_off // br, j))
    return pl.pallas_call(
        body, name=name, grid=(r // br, c // bc),
        in_specs=[blk] + [g_blk] * ng + [blk, blk], out_specs=[blk] * 4,
        out_shape=[jax.ShapeDtypeStruct((r, c), F32)] * 4,
        compiler_params=_params(("parallel", "parallel")),
    )(w, *g_parts, m, v)


_HBM = pl.BlockSpec(memory_space=pltpu.HBM)


def _other_chips(x, y):
    return ((1 - x, y), (x, 1 - y), (1 - x, 1 - y))


def _gather_copies(src, out, send_sems, recv_sems, local_sem):
    x, y, c = lax.axis_index("x"), lax.axis_index("y"), lax.axis_index("c")
    me = 2 * x + y
    local = pltpu.make_async_copy(src, out.at[me], local_sem)

    def copies(arriving):
        return [pltpu.make_async_remote_copy(src_ref=src, dst_ref=out.at[2 * px + py if arriving else me],
                                             send_sem=send_sems.at[j], recv_sem=recv_sems.at[j], device_id=(px, py, c),
                                             device_id_type=MESH)
                for j, (px, py) in enumerate(_other_chips(x, y))]

    return local, copies


def _gather_start(src, out, send_sems, recv_sems, local_sem):
    local, copies = _gather_copies(src, out, send_sems, recv_sems, local_sem)
    local.start()
    for cp in copies(False):
        cp.start()


def _gather_wait(src, out, send_sems, recv_sems, local_sem):
    local, copies = _gather_copies(src, out, send_sems, recv_sems, local_sem)
    for cp in copies(True):
        cp.wait_recv()
    for cp in copies(False):
        cp.wait_send()
    local.wait()


def _scatter_copies(src, land, send_sems, recv_sems):
    x, y, c = lax.axis_index("x"), lax.axis_index("y"), lax.axis_index("c")
    return [pltpu.make_async_remote_copy(src_ref=src.at[2 * px + py], dst_ref=land.at[j], send_sem=send_sems.at[j],
                                         recv_sem=recv_sems.at[j], device_id=(px, py, c), device_id_type=MESH)
            for j, (px, py) in enumerate(_other_chips(x, y))]


def _scatter_start(src, land, send_sems, recv_sems):
    for cp in _scatter_copies(src, land, send_sems, recv_sems):
        cp.start()


def _scatter_wait(src, land, send_sems, recv_sems):
    copies = _scatter_copies(src, land, send_sems, recv_sems)
    for cp in copies:
        cp.wait_recv()
    for cp in copies:
        cp.wait_send()


def _all_gather_chips(packed):
    r = packed.shape[0]
    half = r // 2

    def body(src, out, ici_send, ici_recv, d2d_send, d2d_recv, local_sem):
        x, y, c = lax.axis_index("x"), lax.axis_index("y"), lax.axis_index("c")
        me = 2 * x + y
        mine = pl.ds(pl.multiple_of(c * half, 16), half)
        theirs = pl.ds(pl.multiple_of((1 - c) * half, 16), half)
        chips = _other_chips(x, y)
        local = pltpu.make_async_copy(src, out.at[me], local_sem)
        local.start()
        sends = [pltpu.make_async_remote_copy(src_ref=src.at[mine], dst_ref=out.at[me, mine], send_sem=ici_send.at[j],
                                              recv_sem=ici_recv.at[j], device_id=(px, py, c), device_id_type=MESH)
                 for j, (px, py) in enumerate(chips)]
        for cp in sends:
            cp.start()
        passed = []
        for j, (px, py) in enumerate(chips):
            block = 2 * px + py
            pltpu.make_async_remote_copy(src_ref=src.at[mine], dst_ref=out.at[block, mine], send_sem=ici_send.at[j],
                                         recv_sem=ici_recv.at[j], device_id=(px, py, c), device_id_type=MESH).wait_recv()
            cp = pltpu.make_async_remote_copy(src_ref=out.at[block, mine], dst_ref=out.at[block, mine],
                                              send_sem=d2d_send.at[j], recv_sem=d2d_recv.at[j],
                                              device_id=(x, y, 1 - c), device_id_type=MESH)
            cp.start()
            passed.append(cp)
        for j, (px, py) in enumerate(chips):
            block = 2 * px + py
            pltpu.make_async_remote_copy(src_ref=out.at[block, theirs], dst_ref=out.at[block, theirs],
                                         send_sem=d2d_send.at[j], recv_sem=d2d_recv.at[j],
                                         device_id=(x, y, 1 - c), device_id_type=MESH).wait_recv()
        for cp in sends + passed:
            cp.wait_send()
        local.wait()

    sems = pltpu.SemaphoreType.DMA((3,))
    return pl.pallas_call(
        body, name="ag_weights", in_specs=[_HBM], out_specs=_HBM,
        out_shape=jax.ShapeDtypeStruct((N_CHIPS,) + packed.shape, packed.dtype),
        scratch_shapes=[sems, sems, sems, sems, pltpu.SemaphoreType.DMA(())],
    )(packed)


def _sum4(gp, land, chip, name):
    _, r, w = gp.shape
    tr = 128

    def body(chip_ref, o_ref, l_ref, s_ref):
        s_ref[...] = ((o_ref[0] + l_ref[0].astype(F32)) + l_ref[1].astype(F32)) + l_ref[2].astype(F32)

    return pl.pallas_call(
        body, name=name,
        grid_spec=pltpu.PrefetchScalarGridSpec(
            num_scalar_prefetch=1, grid=(r // tr,),
            in_specs=[pl.BlockSpec((1, tr, w), lambda i, chip_ref: (chip_ref[0], i, 0)),
                      pl.BlockSpec((3, tr, w), lambda i, chip_ref: (0, i, 0))],
            out_specs=pl.BlockSpec((tr, w), lambda i, chip_ref: (i, 0))),
        out_shape=jax.ShapeDtypeStruct((r, w), F32),
        compiler_params=_params(("parallel",)),
    )(chip, gp, land)


def _sibling_copy(src, got, send_sem, recv_sem):
    x, y, c = lax.axis_index("x"), lax.axis_index("y"), lax.axis_index("c")
    return pltpu.make_async_remote_copy(src_ref=src, dst_ref=got, send_sem=send_sem, recv_sem=recv_sem,
                                        device_id=(x, y, 1 - c), device_id_type=MESH)


def _swap_sibling(s, name):
    def body(src, got, send_sem, recv_sem):
        cp = _sibling_copy(src, got, send_sem, recv_sem)
        cp.start()
        cp.wait_recv()
        cp.wait_send()

    return pl.pallas_call(
        body, name=name, in_specs=[_HBM], out_specs=_HBM,
        out_shape=jax.ShapeDtypeStruct(s.shape, s.dtype),
        scratch_shapes=[pltpu.SemaphoreType.DMA(()), pltpu.SemaphoreType.DMA(())],
    )(s)


def _all_reduce_small(dsmall, loss):
    n_dev = 8
    names = tuple(_SMALL_SLOTS)
    shape = (8, D_MODEL)

    def body(*refs):
        parts, loss_ref = refs[:len(names)], refs[len(names)]
        out, src, gath, send_sems, recv_sems = refs[len(names) + 1:]
        x, y, c = lax.axis_index("x"), lax.axis_index("y"), lax.axis_index("c")
        me = 4 * x + 2 * y + c
        src[...] = jnp.zeros(shape, F32)
        for name, part in zip(names, parts):
            row, lane, _ = _SMALL_SLOTS[name]
            src[row:row + 1, lane:lane + part.shape[1]] = part[...]
        src[_LOSS_ROW:_LOSS_ROW + 1, 0:LANES] = loss_ref[...]
        gath[me] = src[...]
        peers = []
        for k in range(1, n_dev):
            px = 1 - x if (k >> 2) & 1 else x
            py = 1 - y if (k >> 1) & 1 else y
            pc = 1 - c if k & 1 else c
            peers.append((px, py, pc))
        sends = []
        for j, peer in enumerate(peers):
            cp = pltpu.make_async_remote_copy(src_ref=src, dst_ref=gath.at[me], send_sem=send_sems.at[j],
                                              recv_sem=recv_sems.at[j], device_id=peer, device_id_type=MESH)
            cp.start()
            sends.append(cp)
        for j, (px, py, pc) in enumerate(peers):
            pltpu.make_async_remote_copy(src_ref=src, dst_ref=gath.at[4 * px + 2 * py + pc], send_sem=send_sems.at[j],
                                         recv_sem=recv_sems.at[j], device_id=(px, py, pc), device_id_type=MESH).wait_recv()
        for cp in sends:
            cp.wait_send()
        acc = gath[0]
        for d in range(1, n_dev):
            acc = acc + gath[d]
        out[...] = acc

    vmem = pl.BlockSpec(memory_space=pltpu.VMEM)
    return pl.pallas_call(
        body, name="ar_small", in_specs=[vmem] * (len(names) + 1), out_specs=vmem,
        out_shape=jax.ShapeDtypeStruct(shape, F32),
        scratch_shapes=[pltpu.VMEM(shape, F32), pltpu.VMEM((n_dev,) + shape, F32),
                        pltpu.SemaphoreType.DMA((n_dev - 1,)), pltpu.SemaphoreType.DMA((n_dev - 1,))],
    )(*[dsmall[k] for k in names], loss)


_W_IN_ROWS = SHARD_SHAPES["w_in"][1]
_KR_ROW = 3200
_KR_PAD_ROW = _KR_BLK * LANES + QK_NOPE


def _shard_rows(name, a):
    return jnp.transpose(a) if name == "w_in" else a.reshape(PACK_ROWS[name], D_MODEL)


def _pack(group, shards, dtype):
    parts = [_shard_rows(n, shards[n]).astype(dtype) for n in group]
    pad = -sum(PACK_ROWS[n] for n in group) % LANES
    if pad:
        parts.append(jnp.zeros((pad, D_MODEL), dtype))
    return jnp.concatenate(parts, axis=0)


def _col_sharded_full(g, name, group):
    r, c = SHARD_SHAPES[name]
    off = _row_offset(group, name)
    blocks = g[:, off:off + PACK_ROWS[name]].reshape(N_CHIPS, r, c)
    return jnp.transpose(blocks, (1, 0, 2)).reshape(r, N_CHIPS * c)


def _col_sharded_blocks(d, name):
    r, c = SHARD_SHAPES[name]
    return jnp.transpose(d.reshape(r, N_CHIPS, c), (1, 0, 2)).reshape(N_CHIPS, PACK_ROWS[name], D_MODEL)


def _weights_a(g):
    dt = g.dtype
    w_in_t = g[:, :_W_IN_ROWS].reshape(N_CHIPS * _W_IN_ROWS, D_MODEL)
    z = lambda n: jnp.zeros((n, D_MODEL), dt)
    w_in_t = jnp.concatenate([w_in_t[:_KR_ROW], z(_KR_PAD_ROW - _KR_ROW), w_in_t[_KR_ROW:],
                              z(D_IN_PAD - _KR_PAD_ROW - QK_ROPE)], axis=0)
    wq = _col_sharded_full(g, "w_q_b", GROUP_A).reshape(Q_LORA, N_HEADS_B, Q_HEAD_B)
    wq_p = jnp.concatenate([wq, jnp.zeros((Q_LORA, N_HEADS_B, HEAD_PAD - Q_HEAD_B), dt)], axis=2).reshape(Q_LORA, MLA_W)
    wkv = _col_sharded_full(g, "w_kv_b", GROUP_A).reshape(KV_LORA, N_HEADS_B, QK_NOPE + V_DIM_B)
    zk = jnp.zeros((KV_LORA, N_HEADS_B, HEAD_PAD - QK_NOPE), dt)
    wk_p = jnp.concatenate([wkv[:, :, :QK_NOPE], zk], axis=2).reshape(KV_LORA, MLA_W)
    wv = wkv[:, :, QK_NOPE:].reshape(KV_LORA, N_HEADS_B * V_DIM_B)
    return dict(w_in=w_in_t, wq=wq_p, wk=wk_p, wv=wv)


def _grad_blocks_a(dw_in_t, dwq_p, dwk_p, dwv):
    dw_in = jnp.concatenate([dw_in_t[:_KR_ROW], dw_in_t[_KR_PAD_ROW:_KR_PAD_ROW + QK_ROPE]], axis=0)
    dwq = dwq_p.reshape(Q_LORA, N_HEADS_B, HEAD_PAD)[:, :, :Q_HEAD_B].reshape(Q_LORA, N_HEADS_B * Q_HEAD_B)
    dwk = dwk_p.reshape(KV_LORA, N_HEADS_B, HEAD_PAD)[:, :, :QK_NOPE]
    dwkv = jnp.concatenate([dwk, dwv.reshape(KV_LORA, N_HEADS_B, V_DIM_B)], axis=2)
    dwkv = dwkv.reshape(KV_LORA, N_HEADS_B * (QK_NOPE + V_DIM_B))
    pad = -sum(PACK_ROWS[n] for n in GROUP_A) % LANES
    return jnp.concatenate([dw_in.reshape(N_CHIPS, _W_IN_ROWS, D_MODEL), _col_sharded_blocks(dwq, "w_q_b"),
                            _col_sharded_blocks(dwkv, "w_kv_b"), jnp.zeros((N_CHIPS, pad, D_MODEL), F32)], axis=1)


def _rope_freq_lanes():
    freqs = ROPE_THETA ** (-jnp.arange(0, QK_ROPE, 2, dtype=F32) / QK_ROPE)
    return jnp.concatenate([jnp.zeros((QK_NOPE,), F32), freqs, freqs,
                            jnp.zeros((HEAD_PAD - Q_HEAD_B,), F32)]).reshape(1, LANES)


def _fwd_bwd(x, positions, target, w):
    t = x.shape[0]
    wa = _weights_a(_all_gather_chips(_pack(GROUP_A, w, BF16)))
    posr = positions.astype(F32).reshape(1, t)
    posc = posr.reshape(t, 1)
    freq = _rope_freq_lanes()
    g1, g2, g3, g4 = w["pre_norm_mix"], w["post_norm_mix"], w["pre_norm_mlp"], w["post_norm_mlp"]
    qan, kvan, sinks = w["q_a_norm"], w["kv_a_norm"], w["sinks"]

    h, proj = _proj_fwd(x, g1, wa["w_in"])
    out_a, lse_a = _swa_fwd(proj, posc, posr, sinks)
    qm, km, qt, kt, vm, vt = _mla_prep_fwd(proj, posc, freq, qan, kvan, wa["wq"], wa["wk"], wa["wv"])
    out_bt, lse_b, wb = _mla_fwd(km, qt, vt, _pack(GROUP_B, w, BF16))
    w_oa, w_ob = _col_sharded_full(wb, "w_o_a", GROUP_B), _col_sharded_full(wb, "w_o_b", GROUP_B)
    merged, y, x1, h2 = _mix_out_fwd(out_a, out_bt, proj, x, w_oa, w_ob, wb, g2, g3)
    a = _up_fwd(h2, wb)
    dx2, dyd, dg4, loss = _down_fwd_loss(a, wb, x1, target, g4)

    gp_b = _dw_into_blocks(a, dyd, "w_down", 1024, 512)
    du = _down_bwd(dyd, wb, a)
    gp_b = _dw_into_blocks(h2, du, "w_up", 1024, 512, gp_b)
    dx1, dy, dg3, dg2 = _up_bwd(du, wb, x1, dx2, y, g3, g2)
    gp_b = _dw_into_blocks(merged, dy, "w_out", 1024, 512, gp_b)
    doa, dob, dga, dgb, d_out_a, d_out_b, d_out_bt, del_a, lrep_b, drep_b = _mix_out_bwd(
        dy, out_a, out_bt, lse_b, proj, w_oa, w_ob, wb)
    dw_oa = _matmul_tn(out_a, doa, "dw_o_a", 512, 1024)
    dw_ob = _dw_ob(out_bt, dob)
    small_b = jnp.concatenate([_col_sharded_blocks(dw_oa, "w_o_a"), _col_sharded_blocks(dw_ob, "w_o_b")], axis=1)
    gp_b = lax.dynamic_update_slice(gp_b, small_b, (0, _row_offset(GROUP_B, "w_o_a"), 0))
    dqm, dkm, dvm, land_b = _mla_bwd(qm, km, qt, kt, vt, d_out_b, d_out_bt, lrep_b, drep_b, gp_b)
    chip = (2 * lax.axis_index("x") + lax.axis_index("y")).astype(jnp.int32).reshape(1)
    part_b = _sum4(gp_b, land_b, chip, "rs_sum_b")
    dcq, dckv, dkr, dwq, dwk, dwv, dqan, dkvan, sib_b = _mla_prep_bwd(
        dqm, dkm, dvm, proj, posc, freq, qan, kvan, wa["wq"], wa["wk"], wa["wv"], part_b)
    dqa, dka, dva, dsinks = _swa_bwd(proj, d_out_a, lse_a, del_a, posc, posr, sinks)
    dproj = jnp.concatenate([dga, dgb, dqa.astype(BF16), dka.astype(BF16), dva.astype(BF16), dcq, dckv, dkr], axis=1)
    dw_in_t = _matmul_tn(dproj, h, "dw_in", D_IN_PAD // 2, 1024, tk=512)
    gp_a = _grad_blocks_a(dw_in_t, dwq, dwk, dwv)
    grad_x, dg1, land_a = _in_bwd(dproj, wa["w_in"], x, dx1, g1, gp_a.astype(BF16))

    part_a = _sum4(gp_a, land_a, chip, "rs_sum_a")
    reduced = {GROUP_A: [part_a, _swap_sibling(part_a, "rs_swap_a")], GROUP_B: [part_b, sib_b]}
    dsmall = dict(pre_norm_mix=dg1, post_norm_mix=dg2, pre_norm_mlp=dg3, post_norm_mlp=dg4,
                  q_a_norm=dqan, kv_a_norm=dkvan, sinks=dsinks)
    return loss, grad_x, reduced, dsmall


def kernel(x, positions, pre_norm_mix, w_in, q_a_norm, w_q_b, kv_a_norm, w_kv_b, sinks, w_o_a, w_o_b, w_out, post_norm_mix, pre_norm_mlp, w_up, w_down, post_norm_mlp, loss_target, m_pre_norm_mix, m_w_in, m_q_a_norm, m_w_q_b, m_kv_a_norm, m_w_kv_b, m_sinks, m_w_o_a, m_w_o_b, m_w_out, m_post_norm_mix, m_pre_norm_mlp, m_w_up, m_w_down, m_post_norm_mlp, v_pre_norm_mix, v_w_in, v_q_a_norm, v_w_q_b, v_kv_a_norm, v_w_kv_b, v_sinks, v_w_o_a, v_w_o_b, v_w_out, v_post_norm_mix, v_pre_norm_mlp, v_w_up, v_w_down, v_post_norm_mlp):
    w = dict(pre_norm_mix=pre_norm_mix, w_in=w_in[0], q_a_norm=q_a_norm, w_q_b=w_q_b[0], kv_a_norm=kv_a_norm,
             w_kv_b=w_kv_b[0], sinks=sinks, w_o_a=w_o_a[0], w_o_b=w_o_b[0], w_out=w_out[0],
             post_norm_mix=post_norm_mix, pre_norm_mlp=pre_norm_mlp, w_up=w_up[0], w_down=w_down[0],
             post_norm_mlp=post_norm_mlp)
    m = dict(pre_norm_mix=m_pre_norm_mix, w_in=m_w_in[0], q_a_norm=m_q_a_norm, w_q_b=m_w_q_b[0],
             kv_a_norm=m_kv_a_norm, w_kv_b=m_w_kv_b[0], sinks=m_sinks, w_o_a=m_w_o_a[0], w_o_b=m_w_o_b[0],
             w_out=m_w_out[0], post_norm_mix=m_post_norm_mix, pre_norm_mlp=m_pre_norm_mlp, w_up=m_w_up[0],
             w_down=m_w_down[0], post_norm_mlp=m_post_norm_mlp)
    v = dict(pre_norm_mix=v_pre_norm_mix, w_in=v_w_in[0], q_a_norm=v_q_a_norm, w_q_b=v_w_q_b[0],
             kv_a_norm=v_kv_a_norm, w_kv_b=v_w_kv_b[0], sinks=v_sinks, w_o_a=v_w_o_a[0], w_o_b=v_w_o_b[0],
             w_out=v_w_out[0], post_norm_mix=v_post_norm_mix, pre_norm_mlp=v_pre_norm_mlp, w_up=v_w_up[0],
             w_down=v_w_down[0], post_norm_mlp=v_post_norm_mlp)

    loss, grad_x, reduced, dsmall = _fwd_bwd(x[0], positions, loss_target[0], w)

    red = _all_reduce_small(dsmall, loss)
    small = _adamw_small(red, w, m, v)

    big = {}
    tr = jnp.transpose
    big["w_in"] = [tr(o)[None] for o in _adamw(tr(w["w_in"]), reduced[GROUP_A], tr(m["w_in"]), tr(v["w_in"]),
                                               "adamw_w_in", (_W_IN_ROWS, 256))]
    for n in ("w_up", "w_down", "w_out"):
        big[n] = [o[None] for o in _adamw(w[n], reduced[GROUP_B], m[n], v[n], "adamw_" + n, (128, D_MODEL),
                                          _row_offset(GROUP_B, n))]
    for group, names in ((GROUP_A, ("w_q_b", "w_kv_b")), (GROUP_B, ("w_o_a", "w_o_b"))):
        for n in names:
            off = _row_offset(group, n)
            g_parts = [p[off:off + PACK_ROWS[n]].reshape(SHARD_SHAPES[n]) for p in reduced[group]]
            big[n] = [o[None] for o in _adamw(w[n], g_parts, m[n], v[n], "adamw_" + n, SHARD_SHAPES[n])]

    outs = [big[n][k] if n in big else small[n][k] for k in range(4) for n in WEIGHTS]
    return (red[_LOSS_ROW, 0], grad_x[None], *outs)
```

```python
import jax
import jax.numpy as jnp
from jax import lax
from jax.experimental import pallas as pl
from jax.experimental.pallas import tpu as pltpu

F32 = jnp.float32
BF16 = jnp.bfloat16
MESH = pl.DeviceIdType.MESH

D_MODEL = 1024
N_HEADS_A = 8
N_KV_A = 2
HEAD_DIM_A = 64
WINDOW = 128
BLOCK = 128
N_HEADS_B = 8
QK_NOPE = 64
QK_ROPE = 32
V_DIM_B = 64
Q_LORA = 256
KV_LORA = 128
ROPE_THETA = 10000.0
D_FF = 4 * D_MODEL
EPS = 1e-6
WIDTH_A = N_HEADS_A * HEAD_DIM_A
Q_HEAD_B = QK_NOPE + QK_ROPE
D_IN_PAD = 3328
HEAD_PAD = 128
MLA_W = N_HEADS_B * HEAD_PAD

ADAM_LR = 0.001
ADAM_B1 = 0.9
ADAM_B2 = 0.999
ADAM_EPS = 1e-08
ADAM_WD = 0.01
ADAM_STEP = 10

NEG = -1e30
N_CHIPS = 4
LANES = 128
VMEM_LIMIT = 56 * 1024 * 1024

SHARD_SHAPES = {"w_in": (1024, 808), "w_q_b": (256, 192), "w_kv_b": (128, 256), "w_o_a": (512, 256),
                "w_o_b": (512, 256), "w_out": (256, 1024), "w_up": (1024, 1024), "w_down": (1024, 1024)}
PACK_ROWS = {n: (s[0] * s[1]) // D_MODEL for n, s in SHARD_SHAPES.items()}
GROUP_A = ("w_in", "w_q_b", "w_kv_b")
GROUP_B = ("w_up", "w_down", "w_out", "w_o_a", "w_o_b")
WEIGHTS = ("pre_norm_mix", "w_in", "q_a_norm", "w_q_b", "kv_a_norm", "w_kv_b", "sinks", "w_o_a", "w_o_b", "w_out",
           "post_norm_mix", "pre_norm_mlp", "w_up", "w_down", "post_norm_mlp")


def _params(sem=None):
    return pltpu.CompilerParams(dimension_semantics=sem, vmem_limit_bytes=VMEM_LIMIT)


def _dot(a, b):
    return jnp.dot(a, b, preferred_element_type=F32)


def _dot_nt(a, b):
    return lax.dot_general(a, b, (((1,), (1,)), ((), ())), preferred_element_type=F32)


def _dot_tn(a, b):
    return lax.dot_general(a, b, (((0,), (0,)), ((), ())), preferred_element_type=F32)


def _rms(v):
    return lax.rsqrt(jnp.mean(v * v, axis=-1, keepdims=True) + EPS)


def _norm_bwd(dout, n, r, g):
    dn = dout * g
    dx = r * (dn - n * jnp.mean(dn * n, axis=-1, keepdims=True))
    return dx, jnp.sum(dout * n, axis=0, keepdims=True)


def _full(shape):
    return pl.BlockSpec(shape, lambda *_: (0,) * len(shape))


def _row_offset(group, name):
    return sum(PACK_ROWS[n] for n in group[:group.index(name)])


def _wb_spec(name):
    rows = PACK_ROWS[name]
    return pl.BlockSpec((N_CHIPS, rows, D_MODEL), lambda *_: (0, _row_offset(GROUP_B, name) // rows, 0))


def _proj_fwd(x, g1, w_in_t):
    t = x.shape[0]
    tm = 256

    def body(x_ref, g_ref, w_ref, h_ref, p_ref):
        xv = x_ref[...]
        h = ((xv * _rms(xv)) * g_ref[...]).astype(BF16)
        h_ref[...] = h
        p_ref[...] = _dot_nt(h, w_ref[...])

    return pl.pallas_call(
        body, name="proj_fwd", grid=(t // tm,),
        in_specs=[pl.BlockSpec((tm, D_MODEL), lambda i: (i, 0)), _full((1, D_MODEL)), _full((D_IN_PAD, D_MODEL))],
        out_specs=[pl.BlockSpec((tm, D_MODEL), lambda i: (i, 0)), pl.BlockSpec((tm, D_IN_PAD), lambda i: (i, 0))],
        out_shape=[jax.ShapeDtypeStruct((t, D_MODEL), BF16), jax.ShapeDtypeStruct((t, D_IN_PAD), F32)],
        compiler_params=_params(("parallel",)),
    )(x, g1, w_in_t)


_QA_BLK = 2048 // WIDTH_A
_KA_BLK = 2560 // LANES
_VA_BLK = 2688 // LANES
_CQ_BLK = 2816 // Q_LORA
_CKV_BLK = 3072 // LANES
_KR_BLK = 3200 // LANES


_GROUP_A = N_HEADS_A // N_KV_A
_SWA_SCALE = HEAD_DIM_A ** -0.5
_LOG2E = 1.4426950408889634


def _head_cols(v, h):
    return v[:, HEAD_DIM_A * h:HEAD_DIM_A * (h + 1)]


def _head_rows(v, h):
    return v[HEAD_DIM_A * h:HEAD_DIM_A * (h + 1), :]


def _swa_band(n, kp_ref, kc_ref, vp_ref, vc_ref, pq_ref, pp_ref, pc_ref):
    kb = jnp.concatenate([kp_ref[...], kc_ref[...]], axis=0)
    vb = jnp.concatenate([vp_ref[...], vc_ref[...]], axis=0)
    posk = jnp.concatenate([pp_ref[...], pc_ref[...]], axis=0)
    dist = jnp.abs(posk - pq_ref[...])
    ki = lax.broadcasted_iota(jnp.int32, (2 * BLOCK, BLOCK), 0)
    qi = lax.broadcasted_iota(jnp.int32, (2 * BLOCK, BLOCK), 1)
    valid = (ki > qi) & (ki <= qi + WINDOW) & ((n > 0) | (ki >= BLOCK))
    return kb, vb, dist, valid


def _swa_scores_t(st_g, j, h, dist, valid):
    slope = 2.0 ** (-8.0 * (h + 1) / N_HEADS_A)
    st = st_g[:, BLOCK * j:BLOCK * (j + 1)] * (_SWA_SCALE * _LOG2E) - (slope * _LOG2E) * dist
    return jnp.where(valid, st, NEG)


def _group_t(xt, kh):
    return jnp.concatenate([_head_rows(xt, _GROUP_A * kh + j) for j in range(_GROUP_A)], axis=1).astype(BF16)


def _swa_fwd(proj, posc, posr, sinks):
    t = proj.shape[0]
    nb = t // BLOCK

    def body(q_ref, kc_ref, kp_ref, vc_ref, vp_ref, pq_ref, pc_ref, pp_ref, sink_ref, o_ref, l_ref):
        n = pl.program_id(0)
        kb, vb, dist, valid = _swa_band(n, kp_ref, kc_ref, vp_ref, vc_ref, pq_ref, pp_ref, pc_ref)
        q_t, vb_t = q_ref[...].T, vb.T
        out_t, lse = [], []
        for kh in range(N_KV_A):
            st_g = _dot(_head_cols(kb, kh).astype(BF16), _group_t(q_t, kh))
            ps = []
            for j in range(_GROUP_A):
                h = _GROUP_A * kh + j
                st = _swa_scores_t(st_g, j, h, dist, valid)
                sink = sink_ref[0:1, h:h + 1] * _LOG2E
                m = jnp.maximum(jnp.max(st, axis=0, keepdims=True), sink)
                e = jnp.exp2(st - m)
                den = jnp.sum(e, axis=0, keepdims=True) + jnp.exp2(sink - m)
                ps.append((e * (1.0 / den)).astype(BF16))
                lse.append(m + jnp.log(den) * _LOG2E)
            o_g = _dot(_head_rows(vb_t, kh).astype(BF16), jnp.concatenate(ps, axis=1))
            out_t.extend(o_g[:, BLOCK * j:BLOCK * (j + 1)] for j in range(_GROUP_A))
        o_ref[...] = jnp.concatenate(out_t, axis=0).T
        l_ref[...] = jnp.concatenate(lse, axis=0)

    cur = lambda n: (n, 0)
    prev = lambda n: jnp.maximum(n - 1, 0)
    return pl.pallas_call(
        body, name="swa_fwd", grid=(nb,),
        in_specs=[pl.BlockSpec((BLOCK, WIDTH_A), lambda n: (n, _QA_BLK)),
                  pl.BlockSpec((BLOCK, LANES), lambda n: (n, _KA_BLK)),
                  pl.BlockSpec((BLOCK, LANES), lambda n: (prev(n), _KA_BLK)),
                  pl.BlockSpec((BLOCK, LANES), lambda n: (n, _VA_BLK)),
                  pl.BlockSpec((BLOCK, LANES), lambda n: (prev(n), _VA_BLK)),
                  pl.BlockSpec((1, BLOCK), lambda n: (0, n)),
                  pl.BlockSpec((BLOCK, 1), cur),
                  pl.BlockSpec((BLOCK, 1), lambda n: (prev(n), 0)),
                  _full((1, N_HEADS_A))],
        out_specs=[pl.BlockSpec((BLOCK, WIDTH_A), cur), pl.BlockSpec((N_HEADS_A, BLOCK), lambda n: (0, n))],
        out_shape=[jax.ShapeDtypeStruct((t, WIDTH_A), F32), jax.ShapeDtypeStruct((N_HEADS_A, t), F32)],
        compiler_params=_params(("parallel",)),
    )(proj, proj, proj, proj, proj, posr, posc, posc, sinks)


def _rope_coeffs(pos, freq):
    ang = pos * freq
    cosv, sinv = jnp.cos(ang), jnp.sin(ang)
    lane = lax.broadcasted_iota(jnp.int32, ang.shape, 1)
    lo = (lane >= QK_NOPE) & (lane < QK_NOPE + QK_ROPE // 2)
    hi = (lane >= QK_NOPE + QK_ROPE // 2) & (lane < QK_NOPE + QK_ROPE)
    c = jnp.where(lane < QK_NOPE, 1.0, jnp.where(lo | hi, cosv, 0.0))
    s = jnp.where(lo, -sinv, jnp.where(hi, sinv, 0.0))
    return c, s, lo, hi


def _rope(xh, c, s, lo):
    up = pltpu.roll(xh, LANES - QK_ROPE // 2, axis=1)
    dn = pltpu.roll(xh, QK_ROPE // 2, axis=1)
    return xh * c + jnp.where(lo, up, dn) * s


def _unrope(dh, c, s, lo, hi):
    g = dh * s
    up = pltpu.roll(g, LANES - QK_ROPE // 2, axis=1)
    dn = pltpu.roll(g, QK_ROPE // 2, axis=1)
    return dh * c + jnp.where(hi, dn, jnp.where(lo, up, 0.0))


_TQ = 512
_MLA_SCALE = Q_HEAD_B ** -0.5


def _mla_prep_fwd(proj, posc, freq, qan, kvan, wq, wk, wv):
    t = proj.shape[0]
    tm = _TQ
    nb = t // tm

    def body(cq_ref, ckv_ref, kr_ref, pos_ref, f_ref, qan_ref, kvan_ref, wq_ref, wk_ref, wv_ref,
             q_ref, k_ref, qt_ref, kt_ref, vt_ref):
        cq = cq_ref[...]
        cqn = ((cq * _rms(cq)) * qan_ref[...]).astype(BF16)
        ckv = ckv_ref[...]
        ckvn = ((ckv * _rms(ckv)) * kvan_ref[...]).astype(BF16)
        qb = _dot(cqn, wq_ref[...])
        kb = _dot(ckvn, wk_ref[...])
        vbt = _dot_nt(wv_ref[...], ckvn)
        c, s, lo, _ = _rope_coeffs(pos_ref[...], f_ref[...])
        kr = _rope(kr_ref[...], c, s, lo)
        for h in range(N_HEADS_B):
            sl = slice(HEAD_PAD * h, HEAD_PAD * (h + 1))
            q_h = _rope(qb[:, sl], c, s, lo)
            k_h = kb[:, sl] + kr
            q_ref[:, sl] = q_h.astype(BF16)
            k_ref[:, sl] = k_h.astype(BF16)
            qt_ref[h, 0] = q_h.T.astype(BF16)
            kt_ref[h, 0] = k_h.T.astype(BF16)
            vt_ref[h, 0] = vbt[V_DIM_B * h:V_DIM_B * (h + 1), :].astype(BF16)

    row = lambda i: (i, 0)
    blk4 = lambda d: pl.BlockSpec((N_HEADS_B, 1, d, tm), lambda i: (0, i, 0, 0))
    return pl.pallas_call(
        body, name="mla_prep_fwd", grid=(nb,),
        in_specs=[pl.BlockSpec((tm, Q_LORA), lambda i: (i, _CQ_BLK)),
                  pl.BlockSpec((tm, LANES), lambda i: (i, _CKV_BLK)),
                  pl.BlockSpec((tm, LANES), lambda i: (i, _KR_BLK)),
                  pl.BlockSpec((tm, 1), row), _full((1, LANES)), _full((1, Q_LORA)), _full((1, KV_LORA)),
                  _full((Q_LORA, MLA_W)), _full((KV_LORA, MLA_W)), _full((N_HEADS_B * V_DIM_B, KV_LORA))],
        out_specs=[pl.BlockSpec((tm, MLA_W), row), pl.BlockSpec((tm, MLA_W), row), blk4(HEAD_PAD), blk4(HEAD_PAD),
                   blk4(V_DIM_B)],
        out_shape=[jax.ShapeDtypeStruct((t, MLA_W), BF16), jax.ShapeDtypeStruct((t, MLA_W), BF16),
                   jax.ShapeDtypeStruct((N_HEADS_B, nb, HEAD_PAD, tm), BF16),
                   jax.ShapeDtypeStruct((N_HEADS_B, nb, HEAD_PAD, tm), BF16),
                   jax.ShapeDtypeStruct((N_HEADS_B, nb, V_DIM_B, tm), BF16)],
        compiler_params=_params(("parallel",)),
    )(proj, proj, proj, posc, freq, qan, kvan, wq, wk, wv)


_MLA_SCALE2 = _MLA_SCALE * _LOG2E


def _mla_fwd(k, qt, vt, w_src):
    t = k.shape[0]
    nb = t // _TQ

    def body(k_ref, qt_ref, vt_ref, w_ref, o_ref, l_ref, wg_ref, raw_a, raw_b, send_sems, recv_sems, local_sem):
        qi = pl.program_id(1)
        first = (pl.program_id(0) == 0) & (qi == 0)
        last = (pl.program_id(0) == N_HEADS_B - 1) & (qi == nb - 1)

        @pl.when(first)
        def _():
            _gather_start(w_ref, wg_ref, send_sems, recv_sems, local_sem)

        q_t = qt_ref[0, 0]

        def product(kj):
            return _dot(k_ref[pl.ds(pl.multiple_of(kj * _TQ, _TQ), _TQ), :], q_t)

        def update(stats, raw_ref, kj, diagonal=False):
            m, l, acc = stats
            raw = raw_ref[...]
            if diagonal:
                key = lax.broadcasted_iota(jnp.int32, raw.shape, 0)
                qry = lax.broadcasted_iota(jnp.int32, raw.shape, 1)
                raw = jnp.where(key <= qry, raw, NEG)
            m_new = jnp.maximum(m, jnp.max(raw, axis=0, keepdims=True) * _MLA_SCALE2)
            alpha = jnp.exp2(m - m_new)
            p = jnp.exp2(raw * _MLA_SCALE2 - m_new)
            l = alpha * l + jnp.sum(p, axis=0, keepdims=True)
            acc = alpha * acc + _dot(vt_ref[0, kj], p.astype(BF16))
            return m_new, l, acc

        def trip(i, stats):
            raw_b[...] = product(2 * i + 1)
            stats = update(stats, raw_a, 2 * i)
            raw_a[...] = product(2 * i + 2)
            return update(stats, raw_b, 2 * i + 1)

        def tail_even(stats):
            return update(stats, raw_a, qi, True)

        def tail_odd(stats):
            raw_b[...] = product(qi)
            return update(update(stats, raw_a, qi - 1), raw_b, qi, True)

        init = (jnp.full((1, _TQ), NEG, F32), jnp.zeros((1, _TQ), F32), jnp.zeros((V_DIM_B, _TQ), F32))
        raw_a[...] = product(0)
        stats = lax.fori_loop(0, qi // 2, trip, init)
        m, l, acc = lax.cond(qi % 2 == 0, tail_even, tail_odd, stats)
        o_ref[0, 0] = acc / l
        l_ref[0, 0] = m + jnp.log(l) * _LOG2E

        @pl.when(last)
        def _():
            _gather_wait(w_ref, wg_ref, send_sems, recv_sems, local_sem)

    return pl.pallas_call(
        body, name="mla_fwd", grid=(N_HEADS_B, nb),
        in_specs=[pl.BlockSpec((t, HEAD_PAD), lambda h, qi: (0, h)),
                  pl.BlockSpec((1, 1, HEAD_PAD, _TQ), lambda h, qi: (h, qi, 0, 0)),
                  pl.BlockSpec((1, nb, V_DIM_B, _TQ), lambda h, qi: (h, 0, 0, 0)), _HBM],
        out_specs=[pl.BlockSpec((1, 1, V_DIM_B, _TQ), lambda h, qi: (h, qi, 0, 0)),
                   pl.BlockSpec((1, 1, 1, _TQ), lambda h, qi: (h, qi, 0, 0)), _HBM],
        out_shape=[jax.ShapeDtypeStruct((N_HEADS_B, nb, V_DIM_B, _TQ), F32),
                   jax.ShapeDtypeStruct((N_HEADS_B, nb, 1, _TQ), F32),
                   jax.ShapeDtypeStruct((N_CHIPS,) + w_src.shape, w_src.dtype)],
        scratch_shapes=[pltpu.VMEM((_TQ, _TQ), F32), pltpu.VMEM((_TQ, _TQ), F32),
                        pltpu.SemaphoreType.DMA((3,)), pltpu.SemaphoreType.DMA((3,)), pltpu.SemaphoreType.DMA(())],
        compiler_params=_params(("arbitrary", "arbitrary")),
    )(k, qt, vt, w_src)


def _ot_spec(tm, d):
    per = _TQ // tm
    return pl.BlockSpec((N_HEADS_B, 1, d, tm), lambda i: (0, i // per, 0, i % per))


def _mix_out_fwd(out_a, out_bt, proj, x, w_oa, w_ob, wb, g2, g3):
    t = x.shape[0]
    tm = 256

    def body(oa_ref, obt_ref, ga_ref, gb_ref, x_ref, woa_ref, wob_ref, wout_ref, g2_ref, g3_ref,
             mg_ref, y_ref, x1_ref, h2_ref):
        oa = _dot(oa_ref[...].astype(BF16), woa_ref[...])
        obt = obt_ref[...].reshape(N_HEADS_B * V_DIM_B, tm).astype(BF16)
        ob = _dot_tn(obt, wob_ref[...])
        merged = (jax.nn.sigmoid(ga_ref[...]) * oa + jax.nn.sigmoid(gb_ref[...]) * ob).astype(BF16)
        mg_ref[...] = merged
        y = _dot(merged, wout_ref[...].reshape(D_MODEL, D_MODEL))
        y_ref[...] = y
        x1 = x_ref[...] + (y * _rms(y)) * g2_ref[...]
        x1_ref[...] = x1
        h2_ref[...] = ((x1 * _rms(x1)) * g3_ref[...]).astype(BF16)

    row = lambda i: (i, 0)
    blk = pl.BlockSpec((tm, D_MODEL), row)
    return pl.pallas_call(
        body, name="mix_out_fwd", grid=(t // tm,),
        in_specs=[pl.BlockSpec((tm, WIDTH_A), row), _ot_spec(tm, V_DIM_B), pl.BlockSpec((tm, D_MODEL), lambda i: (i, 0)),
                  pl.BlockSpec((tm, D_MODEL), lambda i: (i, 1)), blk,
                  _full((WIDTH_A, D_MODEL)), _full((N_HEADS_B * V_DIM_B, D_MODEL)), _wb_spec("w_out"),
                  _full((1, D_MODEL)), _full((1, D_MODEL))],
        out_specs=[blk, blk, blk, blk],
        out_shape=[jax.ShapeDtypeStruct((t, D_MODEL), BF16), jax.ShapeDtypeStruct((t, D_MODEL), F32),
                   jax.ShapeDtypeStruct((t, D_MODEL), F32), jax.ShapeDtypeStruct((t, D_MODEL), BF16)],
        compiler_params=_params(("parallel",)),
    )(out_a, out_bt, proj, proj, x, w_oa, w_ob, wb, g2, g3)


_TM_MLP = 512


def _up_fwd(h2, wb):
    t = h2.shape[0]
    tm = _TM_MLP

    def body(h_ref, w_ref, a_ref):
        hv = h_ref[...]
        for j in range(N_CHIPS):
            u = _dot(hv, w_ref[j])
            a_ref[:, D_MODEL * j:D_MODEL * (j + 1)] = jnp.square(jnp.maximum(u, 0.0)).astype(BF16)

    return pl.pallas_call(
        body, name="up_fwd", grid=(t // tm,),
        in_specs=[pl.BlockSpec((tm, D_MODEL), lambda i: (i, 0)), _wb_spec("w_up")],
        out_specs=pl.BlockSpec((tm, D_FF), lambda i: (i, 0)),
        out_shape=jax.ShapeDtypeStruct((t, D_FF), BF16),
        compiler_params=_params(("parallel",)),
    )(h2, wb)


def _down_fwd_loss(a, wb, x1, target, g4):
    t = a.shape[0]
    tm = _TM_MLP

    def body(a_ref, w_ref, x1_ref, tg_ref, g_ref, dx2_ref, dyd_ref, dg_ref, loss_ref):
        @pl.when(pl.program_id(0) == 0)
        def _():
            dg_ref[...] = jnp.zeros(dg_ref.shape, F32)
            loss_ref[...] = jnp.zeros(loss_ref.shape, F32)

        yd = _dot(a_ref[...], w_ref[...].reshape(D_FF, D_MODEL))
        r = _rms(yd)
        n = yd * r
        diff = (x1_ref[...] + n * g_ref[...]) - tg_ref[...]
        loss_ref[...] += 0.5 * jnp.sum(jnp.mean(diff * diff, axis=-1, keepdims=True), axis=0, keepdims=True)
        dx2 = diff * (1.0 / D_MODEL)
        dx2_ref[...] = dx2
        dyd, dg = _norm_bwd(dx2, n, r, g_ref[...])
        dyd_ref[...] = dyd.astype(BF16)
        dg_ref[...] += dg

    row = lambda i: (i, 0)
    blk = pl.BlockSpec((tm, D_MODEL), row)
    return pl.pallas_call(
        body, name="down_fwd_loss", grid=(t // tm,),
        in_specs=[pl.BlockSpec((tm, D_FF), row), _wb_spec("w_down"), blk, blk, _full((1, D_MODEL))],
        out_specs=[blk, blk, _full((1, D_MODEL)), _full((1, LANES))],
        out_shape=[jax.ShapeDtypeStruct((t, D_MODEL), F32), jax.ShapeDtypeStruct((t, D_MODEL), BF16),
                   jax.ShapeDtypeStruct((1, D_MODEL), F32), jax.ShapeDtypeStruct((1, LANES), F32)],
        compiler_params=_params(("arbitrary",)),
    )(a, wb, x1, target, g4)


def _matmul_tn(a, b, name, tm, tn, tk=1024):
    t, m = a.shape
    n = b.shape[1]
    tk = min(tk, t)
    nk = t // tk

    def body(a_ref, b_ref, o_ref):
        @pl.when(pl.program_id(2) == 0)
        def _():
            o_ref[...] = jnp.zeros(o_ref.shape, F32)

        o_ref[...] += _dot_tn(a_ref[...].astype(BF16), b_ref[...].astype(BF16))

    return pl.pallas_call(
        body, name=name, grid=(m // tm, n // tn, nk),
        in_specs=[pl.BlockSpec((tk, tm), lambda i, j, k: (k, i)), pl.BlockSpec((tk, tn), lambda i, j, k: (k, j))],
        out_specs=pl.BlockSpec((tm, tn), lambda i, j, k: (i, j)),
        out_shape=jax.ShapeDtypeStruct((m, n), F32),
        compiler_params=_params(("parallel", "parallel", "arbitrary")),
    )(a, b)


def _dw_into_blocks(a, b, weight, tm, tk, buf=None):
    t, m = a.shape
    n = b.shape[1]
    nk = t // tk
    rows = PACK_ROWS[weight]
    br = min(tm, rows)
    chips = tm // br
    first = _row_offset(GROUP_B, weight) // br
    per_chip = rows // br
    if weight == "w_up":
        out_map = lambda i, j, k: (j, first + i, 0)
    elif chips > 1:
        out_map = lambda i, j, k: (i, first, 0)
    else:
        out_map = lambda i, j, k: (i // per_chip, first + i % per_chip, 0)

    def body(a_ref, b_ref, *rest):
        o_ref = rest[-1]

        @pl.when(pl.program_id(2) == 0)
        def _():
            o_ref[...] = jnp.zeros(o_ref.shape, F32)

        o_ref[...] += _dot_tn(a_ref[...].astype(BF16), b_ref[...].astype(BF16)).reshape(o_ref.shape)

    in_specs = [pl.BlockSpec((tk, tm), lambda i, j, k: (k, i)), pl.BlockSpec((tk, D_MODEL), lambda i, j, k: (k, j))]
    operands = [a, b]
    if buf is not None:
        in_specs.append(pl.BlockSpec(memory_space=pl.ANY))
        operands.append(buf)
    total = sum(PACK_ROWS[w] for w in GROUP_B)
    return pl.pallas_call(
        body, name="dw_" + weight[2:], grid=(m // tm, n // D_MODEL, nk),
        in_specs=in_specs, out_specs=pl.BlockSpec((chips, br, D_MODEL), out_map),
        out_shape=jax.ShapeDtypeStruct((N_CHIPS, total, D_MODEL), F32),
        input_output_aliases={} if buf is None else {2: 0},
        compiler_params=_params(("parallel", "parallel", "arbitrary")),
    )(*operands)


def _down_bwd(dyd, wb, a):
    t = dyd.shape[0]
    tm = _TM_MLP

    def body(d_ref, w_ref, a_ref, du_ref):
        da = _dot_nt(d_ref[...], w_ref[...].reshape(D_FF, D_MODEL))
        du_ref[...] = (da * (2.0 * jnp.sqrt(a_ref[...].astype(F32)))).astype(BF16)

    row = lambda i: (i, 0)
    return pl.pallas_call(
        body, name="down_bwd", grid=(t // tm,),
        in_specs=[pl.BlockSpec((tm, D_MODEL), row), _wb_spec("w_down"), pl.BlockSpec((tm, D_FF), row)],
        out_specs=pl.BlockSpec((tm, D_FF), row),
        out_shape=jax.ShapeDtypeStruct((t, D_FF), BF16),
        compiler_params=_params(("parallel",)),
    )(dyd, wb, a)


def _up_bwd(du, wb, x1, dx2, y, g3, g2):
    t = du.shape[0]
    tm = _TM_MLP

    def body(du_ref, w_ref, x1_ref, dx2_ref, y_ref, g3_ref, g2_ref, dx1_ref, dy_ref, dg3_ref, dg2_ref):
        @pl.when(pl.program_id(0) == 0)
        def _():
            dg3_ref[...] = jnp.zeros(dg3_ref.shape, F32)
            dg2_ref[...] = jnp.zeros(dg2_ref.shape, F32)

        dh2 = _dot_nt(du_ref[:, 0:D_MODEL], w_ref[0])
        for j in range(1, N_CHIPS):
            dh2 = dh2 + _dot_nt(du_ref[:, D_MODEL * j:D_MODEL * (j + 1)], w_ref[j])
        x1 = x1_ref[...]
        r3 = _rms(x1)
        d3, dg3 = _norm_bwd(dh2, x1 * r3, r3, g3_ref[...])
        dx1 = dx2_ref[...] + d3
        dx1_ref[...] = dx1
        dg3_ref[...] += dg3
        y = y_ref[...]
        r2 = _rms(y)
        dy, dg2 = _norm_bwd(dx1, y * r2, r2, g2_ref[...])
        dy_ref[...] = dy.astype(BF16)
        dg2_ref[...] += dg2

    row = lambda i: (i, 0)
    blk = pl.BlockSpec((tm, D_MODEL), row)
    return pl.pallas_call(
        body, name="up_bwd", grid=(t // tm,),
        in_specs=[pl.BlockSpec((tm, D_FF), row), _wb_spec("w_up"),
                  blk, blk, blk, _full((1, D_MODEL)), _full((1, D_MODEL))],
        out_specs=[blk, blk, _full((1, D_MODEL)), _full((1, D_MODEL))],
        out_shape=[jax.ShapeDtypeStruct((t, D_MODEL), F32), jax.ShapeDtypeStruct((t, D_MODEL), BF16),
                   jax.ShapeDtypeStruct((1, D_MODEL), F32), jax.ShapeDtypeStruct((1, D_MODEL), F32)],
        compiler_params=_params(("arbitrary",)),
    )(du, wb, x1, dx2, y, g3, g2)


def _mix_out_bwd(dy, out_a, out_bt, lse_b, proj, w_oa, w_ob, wb):
    t = dy.shape[0]
    tm = 256
    nb = t // _TQ

    def body(dy_ref, oa_ref, obt_ref, lse_ref, ga_ref, gb_ref, woa_ref, wob_ref, wout_ref,
             doa_ref, dob_ref, dga_ref, dgb_ref, da_ref, db_ref, dbt_ref, dela_ref, lrep_ref, drep_ref):
        dm = _dot_nt(dy_ref[...], wout_ref[...].reshape(D_MODEL, D_MODEL))
        out_a_v = oa_ref[...]
        out_bt_v = obt_ref[...].reshape(N_HEADS_B * V_DIM_B, tm)
        oa = _dot(out_a_v.astype(BF16), woa_ref[...])
        ob = _dot_tn(out_bt_v.astype(BF16), wob_ref[...])
        sa, sb = jax.nn.sigmoid(ga_ref[...]), jax.nn.sigmoid(gb_ref[...])
        doa = (dm * sa).astype(BF16)
        dob = (dm * sb).astype(BF16)
        doa_ref[...] = doa
        dob_ref[...] = dob
        dga_ref[...] = (dm * oa * (sa * (1.0 - sa))).astype(BF16)
        dgb_ref[...] = (dm * ob * (sb * (1.0 - sb))).astype(BF16)
        d_out_a = _dot_nt(doa, woa_ref[...])
        da_ref[...] = d_out_a
        prod_at = (d_out_a * out_a_v).T
        dela_ref[...] = jnp.concatenate(
            [jnp.sum(_head_rows(prod_at, h), axis=0, keepdims=True) for h in range(N_HEADS_A)], axis=0)
        d_out_b = _dot_nt(dob, wob_ref[...])
        d_out_bt = _dot_nt(wob_ref[...], dob)
        prod_bt = d_out_bt * out_bt_v
        for h in range(N_HEADS_B):
            db_ref[h] = d_out_b[:, V_DIM_B * h:V_DIM_B * (h + 1)].astype(BF16)
            dbt_ref[h, 0] = d_out_bt[V_DIM_B * h:V_DIM_B * (h + 1), :].astype(BF16)
            delta = jnp.sum(prod_bt[V_DIM_B * h:V_DIM_B * (h + 1), :], axis=0, keepdims=True)
            drep_ref[h] = jnp.broadcast_to(delta, (LANES, tm)).T
            lrep_ref[h] = jnp.broadcast_to(lse_ref[h, 0], (LANES, tm)).T

    row = lambda i: (i, 0)
    blk = pl.BlockSpec((tm, D_MODEL), row)
    rep_spec = pl.BlockSpec((N_HEADS_B, tm, LANES), lambda i: (0, i, 0))
    return pl.pallas_call(
        body, name="mix_out_bwd", grid=(t // tm,),
        in_specs=[blk, pl.BlockSpec((tm, WIDTH_A), row), _ot_spec(tm, V_DIM_B), _ot_spec(tm, 1),
                  pl.BlockSpec((tm, D_MODEL), lambda i: (i, 0)), pl.BlockSpec((tm, D_MODEL), lambda i: (i, 1)),
                  _full((WIDTH_A, D_MODEL)), _full((N_HEADS_B * V_DIM_B, D_MODEL)), _wb_spec("w_out")],
        out_specs=[blk, blk, blk, blk, pl.BlockSpec((tm, WIDTH_A), row),
                   pl.BlockSpec((N_HEADS_B, tm, V_DIM_B), lambda i: (0, i, 0)), _ot_spec(tm, V_DIM_B),
                   pl.BlockSpec((N_HEADS_A, tm), lambda i: (0, i)), rep_spec, rep_spec],
        out_shape=[jax.ShapeDtypeStruct((t, D_MODEL), BF16)] * 4
        + [jax.ShapeDtypeStruct((t, WIDTH_A), F32), jax.ShapeDtypeStruct((N_HEADS_B, t, V_DIM_B), BF16),
           jax.ShapeDtypeStruct((N_HEADS_B, nb, V_DIM_B, _TQ), BF16), jax.ShapeDtypeStruct((N_HEADS_A, t), F32),
           jax.ShapeDtypeStruct((N_HEADS_B, t, LANES), F32), jax.ShapeDtypeStruct((N_HEADS_B, t, LANES), F32)],
        compiler_params=_params(("parallel",)),
    )(dy, out_a, out_bt, lse_b, proj, proj, w_oa, w_ob, wb)


def _dw_ob(out_bt, dob):
    t = dob.shape[0]
    nb = t // _TQ

    def body(obt_ref, dob_ref, o_ref):
        @pl.when(pl.program_id(0) == 0)
        def _():
            o_ref[...] = jnp.zeros(o_ref.shape, F32)

        obt = obt_ref[...].reshape(N_HEADS_B * V_DIM_B, _TQ).astype(BF16)
        o_ref[...] += _dot(obt, dob_ref[...])

    return pl.pallas_call(
        body, name="dw_o_b", grid=(nb,),
        in_specs=[pl.BlockSpec((N_HEADS_B, 1, V_DIM_B, _TQ), lambda i: (0, i, 0, 0)),
                  pl.BlockSpec((_TQ, D_MODEL), lambda i: (i, 0))],
        out_specs=_full((N_HEADS_B * V_DIM_B, D_MODEL)),
        out_shape=jax.ShapeDtypeStruct((N_HEADS_B * V_DIM_B, D_MODEL), F32),
        compiler_params=_params(("arbitrary",)),
    )(out_bt, dob)


def _mla_bwd(q, k, qt, kt, vt, d_out, d_out_t, lrep, drep, gp):
    t = q.shape[0]
    nb = t // _TQ

    def body(k_ref, kt_ref, vt_ref, q_ref, qt_ref, do_ref, dot_ref, l_ref, d_ref, gp_ref,
             dq_ref, dkt_ref, dvt_ref, land_ref, send_sems, recv_sems):
        step = pl.program_id(1)
        kj = nb - 1 - step

        @pl.when((pl.program_id(0) == 0) & (step == 0))
        def _():
            _scatter_start(gp_ref, land_ref, send_sems, recv_sems)

        @pl.when(step == 0)
        def _():
            dq_ref[...] = jnp.zeros(dq_ref.shape, F32)

        kv, k_t, v_t = k_ref[...], kt_ref[0, 0], vt_ref[0, 0]

        def rows_of(qi):
            return pl.ds(pl.multiple_of(qi * _TQ, _TQ), _TQ)

        def products(qi, diagonal=False):
            s = _dot(q_ref[rows_of(qi), :], k_t) * _MLA_SCALE2
            if diagonal:
                qry = lax.broadcasted_iota(jnp.int32, s.shape, 0)
                key = lax.broadcasted_iota(jnp.int32, s.shape, 1)
                s = jnp.where(key <= qry, s, NEG)
            return s, _dot(do_ref[0, rows_of(qi), :], v_t)

        def update(carry, prods, qi):
            dkt, dvt = carry
            s, dp = prods
            lse, delta = l_ref[0, rows_of(qi), :], d_ref[0, rows_of(qi), :]
            ps, dss = [], []
            for c in range(_TQ // LANES):
                strip = slice(LANES * c, LANES * (c + 1))
                p = jnp.exp2(s[:, strip] - lse)
                ps.append(p.astype(BF16))
                dss.append((p * (dp[:, strip] - delta) * _MLA_SCALE).astype(BF16))
            p_b, ds_b = jnp.concatenate(ps, axis=1), jnp.concatenate(dss, axis=1)
            dvt = dvt + _dot(dot_ref[0, qi], p_b)
            dkt = dkt + _dot(qt_ref[0, qi], ds_b)
            dq_ref[rows_of(qi), :] += _dot(ds_b, kv)
            return dkt, dvt

        def pair(i, carry):
            qa = kj + 1 + 2 * i
            pa, pb = products(qa), products(qa + 1)
            return update(update(carry, pa, qa), pb, qa + 1)

        init = (jnp.zeros((HEAD_PAD, _TQ), F32), jnp.zeros((V_DIM_B, _TQ), F32))
        carry = update(init, products(kj, True), kj)
        pairs = (nb - 1 - kj) // 2
        carry = lax.fori_loop(0, pairs, pair, carry)
        dkt, dvt = lax.fori_loop(kj + 1 + 2 * pairs, nb, lambda qi, cr: update(cr, products(qi), qi), carry)
        dkt_ref[0, 0] = dkt
        dvt_ref[0, 0] = dvt

        @pl.when((pl.program_id(0) == N_HEADS_B - 1) & (step == nb - 1))
        def _():
            _scatter_wait(gp_ref, land_ref, send_sems, recv_sems)

    head4 = lambda d: pl.BlockSpec((1, nb, d, _TQ), lambda h, s: (h, 0, 0, 0))
    blk4 = lambda d: pl.BlockSpec((1, 1, d, _TQ), lambda h, s: (h, nb - 1 - s, 0, 0))
    head3 = lambda d: pl.BlockSpec((1, t, d), lambda h, kj: (h, 0, 0))
    per_head = pl.BlockSpec((t, HEAD_PAD), lambda h, kj: (0, h))
    return pl.pallas_call(
        body, name="mla_bwd", grid=(N_HEADS_B, nb),
        in_specs=[pl.BlockSpec((_TQ, HEAD_PAD), lambda h, s: (nb - 1 - s, h)), blk4(HEAD_PAD), blk4(V_DIM_B),
                  per_head, head4(HEAD_PAD), head3(V_DIM_B), head4(V_DIM_B), head3(LANES), head3(LANES), _HBM],
        out_specs=[per_head, blk4(HEAD_PAD), blk4(V_DIM_B), _HBM],
        out_shape=[jax.ShapeDtypeStruct((t, MLA_W), F32), jax.ShapeDtypeStruct((N_HEADS_B, nb, HEAD_PAD, _TQ), F32),
                   jax.ShapeDtypeStruct((N_HEADS_B, nb, V_DIM_B, _TQ), F32),
                   jax.ShapeDtypeStruct((3,) + gp.shape[1:], gp.dtype)],
        scratch_shapes=[pltpu.SemaphoreType.DMA((3,)), pltpu.SemaphoreType.DMA((3,))],
        compiler_params=_params(("arbitrary", "arbitrary")),
    )(k, kt, vt, q, qt, d_out, d_out_t, lrep, drep, gp)


def _mla_prep_bwd(dq, dkt, dvt, proj, posc, freq, qan, kvan, wq, wk, wv, swap_src):
    t = dq.shape[0]
    tm = _TQ

    def body(dq_ref, dkt_ref, dvt_ref, cq_ref, ckv_ref, pos_ref, f_ref, qan_ref, kvan_ref, wq_ref, wk_ref, wv_ref, src_ref,
             dcq_ref, dckv_ref, dkr_ref, dwq_ref, dwk_ref, dwv_ref, dqan_ref, dkvan_ref, got_ref, send_sem, recv_sem):
        swap = _sibling_copy(src_ref, got_ref, send_sem, recv_sem)

        @pl.when(pl.program_id(0) == 0)
        def _():
            swap.start()
            for r in (dwq_ref, dwk_ref, dwv_ref, dqan_ref, dkvan_ref):
                r[...] = jnp.zeros(r.shape, F32)

        cq = cq_ref[...]
        rq = _rms(cq)
        nq_ = cq * rq
        cqn = (nq_ * qan_ref[...]).astype(BF16)
        ckv = ckv_ref[...]
        rkv = _rms(ckv)
        nkv = ckv * rkv
        ckvn = (nkv * kvan_ref[...]).astype(BF16)
        c, s, lo, hi = _rope_coeffs(pos_ref[...], f_ref[...])
        dkr = jnp.zeros((tm, LANES), F32)
        dqb, dkb = [], []
        for h in range(N_HEADS_B):
            dqb.append(_unrope(dq_ref[:, HEAD_PAD * h:HEAD_PAD * (h + 1)], c, s, lo, hi).astype(BF16))
            dk_h = dkt_ref[h, 0].T
            dkr = dkr + dk_h
            dkb.append(dk_h.astype(BF16))
        dqb, dkb = jnp.concatenate(dqb, axis=1), jnp.concatenate(dkb, axis=1)
        dkr_ref[...] = jnp.where(lo | hi, _unrope(dkr, c, s, lo, hi), 0.0).astype(BF16)
        dvb = dvt_ref[...].reshape(N_HEADS_B * V_DIM_B, tm).T.astype(BF16)
        dwq_ref[...] += _dot_tn(cqn, dqb)
        dwk_ref[...] += _dot_tn(ckvn, dkb)
        dwv_ref[...] += _dot_tn(ckvn, dvb)
        dcqn = _dot_nt(dqb, wq_ref[...])
        dckvn = _dot_nt(dkb, wk_ref[...]) + _dot_nt(dvb, wv_ref[...])
        dcq, dqan = _norm_bwd(dcqn, nq_, rq, qan_ref[...])
        dckv, dkvan = _norm_bwd(dckvn, nkv, rkv, kvan_ref[...])
        dcq_ref[...] = dcq.astype(BF16)
        dckv_ref[...] = dckv.astype(BF16)
        dqan_ref[...] += dqan
        dkvan_ref[...] += dkvan

        @pl.when(pl.program_id(0) == t // tm - 1)
        def _():
            swap.wait_recv()
            swap.wait_send()

    row = lambda i: (i, 0)
    vw = N_HEADS_B * V_DIM_B
    return pl.pallas_call(
        body, name="mla_prep_bwd", grid=(t // tm,),
        in_specs=[pl.BlockSpec((tm, MLA_W), row), pl.BlockSpec((N_HEADS_B, 1, HEAD_PAD, tm), lambda i: (0, i, 0, 0)),
                  pl.BlockSpec((N_HEADS_B, 1, V_DIM_B, tm), lambda i: (0, i, 0, 0)),
                  pl.BlockSpec((tm, Q_LORA), lambda i: (i, _CQ_BLK)),
                  pl.BlockSpec((tm, LANES), lambda i: (i, _CKV_BLK)),
                  pl.BlockSpec((tm, 1), row), _full((1, LANES)), _full((1, Q_LORA)), _full((1, KV_LORA)),
                  _full((Q_LORA, MLA_W)), _full((KV_LORA, MLA_W)), _full((KV_LORA, vw)), _HBM],
        out_specs=[pl.BlockSpec((tm, Q_LORA), row), pl.BlockSpec((tm, LANES), row), pl.BlockSpec((tm, LANES), row),
                   _full((Q_LORA, MLA_W)), _full((KV_LORA, MLA_W)), _full((KV_LORA, vw)),
                   _full((1, Q_LORA)), _full((1, KV_LORA)), _HBM],
        out_shape=[jax.ShapeDtypeStruct((t, Q_LORA), BF16), jax.ShapeDtypeStruct((t, LANES), BF16),
                   jax.ShapeDtypeStruct((t, LANES), BF16),
                   jax.ShapeDtypeStruct((Q_LORA, MLA_W), F32), jax.ShapeDtypeStruct((KV_LORA, MLA_W), F32),
                   jax.ShapeDtypeStruct((KV_LORA, vw), F32),
                   jax.ShapeDtypeStruct((1, Q_LORA), F32), jax.ShapeDtypeStruct((1, KV_LORA), F32),
                   jax.ShapeDtypeStruct(swap_src.shape, swap_src.dtype)],
        scratch_shapes=[pltpu.SemaphoreType.DMA(()), pltpu.SemaphoreType.DMA(())],
        compiler_params=_params(("arbitrary",)),
    )(dq, dkt, dvt, proj, proj, posc, freq, qan, kvan, wq, wk, wv, swap_src)


def _swa_bwd(proj, d_out, lse, delta, posc, posr, sinks):
    t = proj.shape[0]
    nb = t // BLOCK

    def body(q_ref, kc_ref, kp_ref, vc_ref, vp_ref, do_ref, l_ref, d_ref, pq_ref, pc_ref, pp_ref, sink_ref,
             dq_ref, dk_ref, dv_ref, ds_ref, dkb_s, dvb_s, dk_carry, dv_carry):
        n = pl.program_id(0)

        @pl.when(n == 0)
        def _():
            ds_ref[...] = jnp.zeros(ds_ref.shape, F32)
            dk_carry[...] = jnp.zeros(dk_carry.shape, F32)
            dv_carry[...] = jnp.zeros(dv_carry.shape, F32)

        @pl.when(n < nb)
        def _():
            kb, vb, dist, valid = _swa_band(n, kp_ref, kc_ref, vp_ref, vc_ref, pq_ref, pp_ref, pc_ref)
            qv, dov = q_ref[...], do_ref[...]
            q_t, do_t, kb_t = qv.T, dov.T, kb.T
            lane = lax.broadcasted_iota(jnp.int32, (1, LANES), 1)
            dsink = jnp.zeros((1, LANES), F32)
            dq_t = []
            for kh in range(N_KV_A):
                heads = range(_GROUP_A * kh, _GROUP_A * (kh + 1))
                st_g = _dot(_head_cols(kb, kh).astype(BF16), _group_t(q_t, kh))
                dpt_g = _dot(_head_cols(vb, kh).astype(BF16), _group_t(do_t, kh))
                pts, dsts = [], []
                for j, h in enumerate(heads):
                    st = _swa_scores_t(st_g, j, h, dist, valid)
                    l_h, d_h = l_ref[h:h + 1, :], d_ref[h:h + 1, :]
                    pt = jnp.exp2(st - l_h)
                    p_sink = jnp.exp2(sink_ref[0:1, h:h + 1] * _LOG2E - l_h)
                    dsink = jnp.where(lane == h, jnp.sum(-p_sink * d_h, axis=1, keepdims=True), dsink)
                    dst = pt * (dpt_g[:, BLOCK * j:BLOCK * (j + 1)] - d_h) * _SWA_SCALE
                    pts.append(pt.astype(BF16))
                    dsts.append(dst.astype(BF16))
                pt_g, dst_g = jnp.concatenate(pts, axis=1), jnp.concatenate(dsts, axis=1)
                q_g = jnp.concatenate([_head_cols(qv, h) for h in heads], axis=0).astype(BF16)
                do_g = jnp.concatenate([_head_cols(dov, h) for h in heads], axis=0).astype(BF16)
                dkb_s[:, HEAD_DIM_A * kh:HEAD_DIM_A * (kh + 1)] = _dot(dst_g, q_g)
                dvb_s[:, HEAD_DIM_A * kh:HEAD_DIM_A * (kh + 1)] = _dot(pt_g, do_g)
                dq_g = _dot(_head_rows(kb_t, kh).astype(BF16), dst_g)
                dq_t.extend(dq_g[:, BLOCK * j:BLOCK * (j + 1)] for j in range(_GROUP_A))
            dq_ref[...] = jnp.concatenate(dq_t, axis=0).T
            ds_ref[...] += dsink
            dk_ref[...] = dk_carry[...] + dkb_s[0:BLOCK, :]
            dv_ref[...] = dv_carry[...] + dvb_s[0:BLOCK, :]
            dk_carry[...] = dkb_s[BLOCK:2 * BLOCK, :]
            dv_carry[...] = dvb_s[BLOCK:2 * BLOCK, :]

        @pl.when(n == nb)
        def _():
            dk_ref[...] = dk_carry[...]
            dv_ref[...] = dv_carry[...]

    cur = lambda n: (jnp.minimum(n, nb - 1), 0)
    cur_t = lambda n: (0, jnp.minimum(n, nb - 1))
    prv = lambda n: jnp.maximum(jnp.minimum(n, nb - 1) - 1, 0)
    out_prev = lambda n: (jnp.maximum(n - 1, 0), 0)
    return pl.pallas_call(
        body, name="swa_bwd", grid=(nb + 1,),
        in_specs=[pl.BlockSpec((BLOCK, WIDTH_A), lambda n: (jnp.minimum(n, nb - 1), _QA_BLK)),
                  pl.BlockSpec((BLOCK, LANES), lambda n: (jnp.minimum(n, nb - 1), _KA_BLK)),
                  pl.BlockSpec((BLOCK, LANES), lambda n: (prv(n), _KA_BLK)),
                  pl.BlockSpec((BLOCK, LANES), lambda n: (jnp.minimum(n, nb - 1), _VA_BLK)),
                  pl.BlockSpec((BLOCK, LANES), lambda n: (prv(n), _VA_BLK)),
                  pl.BlockSpec((BLOCK, WIDTH_A), cur), pl.BlockSpec((N_HEADS_A, BLOCK), cur_t),
                  pl.BlockSpec((N_HEADS_A, BLOCK), cur_t), pl.BlockSpec((1, BLOCK), cur_t),
                  pl.BlockSpec((BLOCK, 1), cur), pl.BlockSpec((BLOCK, 1), lambda n: (prv(n), 0)),
                  _full((1, N_HEADS_A))],
        out_specs=[pl.BlockSpec((BLOCK, WIDTH_A), cur), pl.BlockSpec((BLOCK, LANES), out_prev),
                   pl.BlockSpec((BLOCK, LANES), out_prev), _full((1, LANES))],
        out_shape=[jax.ShapeDtypeStruct((t, WIDTH_A), F32), jax.ShapeDtypeStruct((t, LANES), F32),
                   jax.ShapeDtypeStruct((t, LANES), F32), jax.ShapeDtypeStruct((1, LANES), F32)],
        scratch_shapes=[pltpu.VMEM((2 * BLOCK, LANES), F32), pltpu.VMEM((2 * BLOCK, LANES), F32),
                        pltpu.VMEM((BLOCK, LANES), F32), pltpu.VMEM((BLOCK, LANES), F32)],
        compiler_params=_params(("arbitrary",)),
    )(proj, proj, proj, proj, proj, d_out, lse, delta, posr, posc, posc, sinks)


def _in_bwd(dproj, w_in_t, x, dx1, g1, gp):
    t = x.shape[0]
    tm = 256
    steps = t // tm

    def body(dp_ref, w_ref, x_ref, dx1_ref, g_ref, gp_ref, dx_ref, dg_ref, land_ref, send_sems, recv_sems):
        i = pl.program_id(0)

        @pl.when(i == 0)
        def _():
            dg_ref[...] = jnp.zeros(dg_ref.shape, F32)
            _scatter_start(gp_ref, land_ref, send_sems, recv_sems)

        dh = _dot(dp_ref[...], w_ref[...])
        xv = x_ref[...]
        r = _rms(xv)
        dx, dg = _norm_bwd(dh, xv * r, r, g_ref[...])
        dx_ref[...] = dx1_ref[...] + dx
        dg_ref[...] += dg

        @pl.when(i == steps - 1)
        def _():
            _scatter_wait(gp_ref, land_ref, send_sems, recv_sems)

    row = lambda i: (i, 0)
    blk = pl.BlockSpec((tm, D_MODEL), row)
    return pl.pallas_call(
        body, name="in_bwd", grid=(steps,),
        in_specs=[pl.BlockSpec((tm, D_IN_PAD), row), _full((D_IN_PAD, D_MODEL)), blk, blk, _full((1, D_MODEL)), _HBM],
        out_specs=[blk, _full((1, D_MODEL)), _HBM],
        out_shape=[jax.ShapeDtypeStruct((t, D_MODEL), F32), jax.ShapeDtypeStruct((1, D_MODEL), F32),
                   jax.ShapeDtypeStruct((3,) + gp.shape[1:], gp.dtype)],
        scratch_shapes=[pltpu.SemaphoreType.DMA((3,)), pltpu.SemaphoreType.DMA((3,))],
        compiler_params=_params(("arbitrary",)),
    )(dproj, w_in_t, x, dx1, g1, gp)


def _adamw_store(w, g, m, v, out_refs):
    g_out, d_out, m_out, v_out = out_refs
    m_new = ADAM_B1 * m + (1.0 - ADAM_B1) * g
    v_new = ADAM_B2 * v + (1.0 - ADAM_B2) * jnp.square(g)
    m_hat = m_new / (1.0 - ADAM_B1 ** ADAM_STEP)
    v_hat = v_new / (1.0 - ADAM_B2 ** ADAM_STEP)
    g_out[...] = g
    d_out[...] = -ADAM_LR * (m_hat / (jnp.sqrt(v_hat) + ADAM_EPS) + ADAM_WD * w)
    m_out[...] = m_new
    v_out[...] = v_new


_SMALL_SLOTS = {"pre_norm_mix": (0, 0, D_MODEL), "post_norm_mix": (1, 0, D_MODEL), "pre_norm_mlp": (2, 0, D_MODEL),
                "post_norm_mlp": (3, 0, D_MODEL), "q_a_norm": (4, 0, Q_LORA), "kv_a_norm": (4, Q_LORA, KV_LORA),
                "sinks": (4, Q_LORA + KV_LORA, N_HEADS_A)}
_LOSS_ROW = 5


def _adamw_small(red, w, m, v):
    names = tuple(_SMALL_SLOTS)
    n = len(names)

    def body(*refs):
        red_ref, ws, ms, vs, outs = refs[0], refs[1:1 + n], refs[1 + n:1 + 2 * n], refs[1 + 2 * n:1 + 3 * n], refs[1 + 3 * n:]
        for k, name in enumerate(names):
            row, lane, width = _SMALL_SLOTS[name]
            g = red_ref[row:row + 1, lane:lane + width]
            _adamw_store(ws[k][...], g, ms[k][...], vs[k][...], outs[4 * k:4 * k + 4])

    vmem = pl.BlockSpec(memory_space=pltpu.VMEM)
    res = pl.pallas_call(
        body, name="adamw_small", in_specs=[vmem] * (1 + 3 * n), out_specs=[vmem] * (4 * n),
        out_shape=[jax.ShapeDtypeStruct(w[name].shape, F32) for name in names for _ in range(4)],
    )(red, *[w[k] for k in names], *[m[k] for k in names], *[v[k] for k in names])
    return {name: res[4 * k:4 * k + 4] for k, name in enumerate(names)}


def _adamw(w, g_parts, m, v, name, block, g_row_off=0):
    r, c = w.shape
    br, bc = block
    ng = len(g_parts)

    def body(*refs):
        w_ref, g_refs, m_ref, v_ref = refs[0], refs[1:1 + ng], refs[1 + ng], refs[2 + ng]
        g = g_refs[0][...]
        for gr in g_refs[1:]:
            g = g + gr[...]
        _adamw_store(w_ref[...], g, m_ref[...], v_ref[...], refs[3 + ng:])

    assert g_row_off % br == 0 and r % br == 0 and c % bc == 0
    blk = pl.BlockSpec(block, lambda i, j: (i, j))
    g_blk = pl.BlockSpec(block, lambda i, j: (i + g_row_off // br, j))
    return pl.pallas_call(
        body, name=name, grid=(r // br, c // bc),
        in_specs=[blk] + [g_blk] * ng + [blk, blk], out_specs=[blk] * 4,
        out_shape=[jax.ShapeDtypeStruct((r, c), F32)] * 4,
        compiler_params=_params(("parallel", "parallel")),
    )(w, *g_parts, m, v)


_HBM = pl.BlockSpec(memory_space=pltpu.HBM)


def _other_chips(x, y):
    return ((1 - x, y), (x, 1 - y), (1 - x, 1 - y))


def _gather_copies(src, out, send_sems, recv_sems, local_sem):
    x, y, c = lax.axis_index("x"), lax.axis_index("y"), lax.axis_index("c")
    me = 2 * x + y
    local = pltpu.make_async_copy(src, out.at[me], local_sem)

    def copies(arriving):
        return [pltpu.make_async_remote_copy(src_ref=src, dst_ref=out.at[2 * px + py if arriving else me],
                                             send_sem=send_sems.at[j], recv_sem=recv_sems.at[j], device_id=(px, py, c),
                                             device_id_type=MESH)
                for j, (px, py) in enumerate(_other_chips(x, y))]

    return local, copies


def _gather_start(src, out, send_sems, recv_sems, local_sem):
    local, copies = _gather_copies(src, out, send_sems, recv_sems, local_sem)
    local.start()
    for cp in copies(False):
        cp.start()


def _gather_wait(src, out, send_sems, recv_sems, local_sem):
    local, copies = _gather_copies(src, out, send_sems, recv_sems, local_sem)
    for cp in copies(True):
        cp.wait_recv()
    for cp in copies(False):
        cp.wait_send()
    local.wait()


def _scatter_copies(src, land, send_sems, recv_sems):
    x, y, c = lax.axis_index("x"), lax.axis_index("y"), lax.axis_index("c")
    return [pltpu.make_async_remote_copy(src_ref=src.at[2 * px + py], dst_ref=land.at[j], send_sem=send_sems.at[j],
                                         recv_sem=recv_sems.at[j], device_id=(px, py, c), device_id_type=MESH)
            for j, (px, py) in enumerate(_other_chips(x, y))]


def _scatter_start(src, land, send_sems, recv_sems):
    for cp in _scatter_copies(src, land, send_sems, recv_sems):
        cp.start()


def _scatter_wait(src, land, send_sems, recv_sems):
    copies = _scatter_copies(src, land, send_sems, recv_sems)
    for cp in copies:
        cp.wait_recv()
    for cp in copies:
        cp.wait_send()


def _all_gather_chips(packed):
    r = packed.shape[0]
    half = r // 2

    def body(src, out, ici_send, ici_recv, d2d_send, d2d_recv, local_sem):
        x, y, c = lax.axis_index("x"), lax.axis_index("y"), lax.axis_index("c")
        me = 2 * x + y
        mine = pl.ds(pl.multiple_of(c * half, 16), half)
        theirs = pl.ds(pl.multiple_of((1 - c) * half, 16), half)
        chips = _other_chips(x, y)
        local = pltpu.make_async_copy(src, out.at[me], local_sem)
        local.start()
        sends = [pltpu.make_async_remote_copy(src_ref=src.at[mine], dst_ref=out.at[me, mine], send_sem=ici_send.at[j],
                                              recv_sem=ici_recv.at[j], device_id=(px, py, c), device_id_type=MESH)
                 for j, (px, py) in enumerate(chips)]
        for cp in sends:
            cp.start()
        passed = []
        for j, (px, py) in enumerate(chips):
            block = 2 * px + py
            pltpu.make_async_remote_copy(src_ref=src.at[mine], dst_ref=out.at[block, mine], send_sem=ici_send.at[j],
                                         recv_sem=ici_recv.at[j], device_id=(px, py, c), device_id_type=MESH).wait_recv()
            cp = pltpu.make_async_remote_copy(src_ref=out.at[block, mine], dst_ref=out.at[block, mine],
                                              send_sem=d2d_send.at[j], recv_sem=d2d_recv.at[j],
                                              device_id=(x, y, 1 - c), device_id_type=MESH)
            cp.start()
            passed.append(cp)
        for j, (px, py) in enumerate(chips):
            block = 2 * px + py
            pltpu.make_async_remote_copy(src_ref=out.at[block, theirs], dst_ref=out.at[block, theirs],
                                         send_sem=d2d_send.at[j], recv_sem=d2d_recv.at[j],
                                         device_id=(x, y, 1 - c), device_id_type=MESH).wait_recv()
        for cp in sends + passed:
            cp.wait_send()
        local.wait()

    sems = pltpu.SemaphoreType.DMA((3,))
    return pl.pallas_call(
        body, name="ag_weights", in_specs=[_HBM], out_specs=_HBM,
        out_shape=jax.ShapeDtypeStruct((N_CHIPS,) + packed.shape, packed.dtype),
        scratch_shapes=[sems, sems, sems, sems, pltpu.SemaphoreType.DMA(())],
    )(packed)


def _sum4(gp, land, chip, name):
    _, r, w = gp.shape
    tr = 128

    def body(chip_ref, o_ref, l_ref, s_ref):
        s_ref[...] = ((o_ref[0] + l_ref[0].astype(F32)) + l_ref[1].astype(F32)) + l_ref[2].astype(F32)

    return pl.pallas_call(
        body, name=name,
        grid_spec=pltpu.PrefetchScalarGridSpec(
            num_scalar_prefetch=1, grid=(r // tr,),
            in_specs=[pl.BlockSpec((1, tr, w), lambda i, chip_ref: (chip_ref[0], i, 0)),
                      pl.BlockSpec((3, tr, w), lambda i, chip_ref: (0, i, 0))],
            out_specs=pl.BlockSpec((tr, w), lambda i, chip_ref: (i, 0))),
        out_shape=jax.ShapeDtypeStruct((r, w), F32),
        compiler_params=_params(("parallel",)),
    )(chip, gp, land)


def _sibling_copy(src, got, send_sem, recv_sem):
    x, y, c = lax.axis_index("x"), lax.axis_index("y"), lax.axis_index("c")
    return pltpu.make_async_remote_copy(src_ref=src, dst_ref=got, send_sem=send_sem, recv_sem=recv_sem,
                                        device_id=(x, y, 1 - c), device_id_type=MESH)


def _swap_sibling(s, name):
    def body(src, got, send_sem, recv_sem):
        cp = _sibling_copy(src, got, send_sem, recv_sem)
        cp.start()
        cp.wait_recv()
        cp.wait_send()

    return pl.pallas_call(
        body, name=name, in_specs=[_HBM], out_specs=_HBM,
        out_shape=jax.ShapeDtypeStruct(s.shape, s.dtype),
        scratch_shapes=[pltpu.SemaphoreType.DMA(()), pltpu.SemaphoreType.DMA(())],
    )(s)


def _all_reduce_small(dsmall, loss):
    n_dev = 8
    names = tuple(_SMALL_SLOTS)
    shape = (8, D_MODEL)

    def body(*refs):
        parts, loss_ref = refs[:len(names)], refs[len(names)]
        out, src, gath, send_sems, recv_sems = refs[len(names) + 1:]
        x, y, c = lax.axis_index("x"), lax.axis_index("y"), lax.axis_index("c")
        me = 4 * x + 2 * y + c
        src[...] = jnp.zeros(shape, F32)
        for name, part in zip(names, parts):
            row, lane, _ = _SMALL_SLOTS[name]
            src[row:row + 1, lane:lane + part.shape[1]] = part[...]
        src[_LOSS_ROW:_LOSS_ROW + 1, 0:LANES] = loss_ref[...]
        gath[me] = src[...]
        peers = []
        for k in range(1, n_dev):
            px = 1 - x if (k >> 2) & 1 else x
            py = 1 - y if (k >> 1) & 1 else y
            pc = 1 - c if k & 1 else c
            peers.append((px, py, pc))
        sends = []
        for j, peer in enumerate(peers):
            cp = pltpu.make_async_remote_copy(src_ref=src, dst_ref=gath.at[me], send_sem=send_sems.at[j],
                                              recv_sem=recv_sems.at[j], device_id=peer, device_id_type=MESH)
            cp.start()
            sends.append(cp)
        for j, (px, py, pc) in enumerate(peers):
            pltpu.make_async_remote_copy(src_ref=src, dst_ref=gath.at[4 * px + 2 * py + pc], send_sem=send_sems.at[j],
                                         recv_sem=recv_sems.at[j], device_id=(px, py, pc), device_id_type=MESH).wait_recv()
        for cp in sends:
            cp.wait_send()
        acc = gath[0]
        for d in range(1, n_dev):
            acc = acc + gath[d]
        out[...] = acc

    vmem = pl.BlockSpec(memory_space=pltpu.VMEM)
    return pl.pallas_call(
        body, name="ar_small", in_specs=[vmem] * (len(names) + 1), out_specs=vmem,
        out_shape=jax.ShapeDtypeStruct(shape, F32),
        scratch_shapes=[pltpu.VMEM(shape, F32), pltpu.VMEM((n_dev,) + shape, F32),
                        pltpu.SemaphoreType.DMA((n_dev - 1,)), pltpu.SemaphoreType.DMA((n_dev - 1,))],
    )(*[dsmall[k] for k in names], loss)


_W_IN_ROWS = SHARD_SHAPES["w_in"][1]
_KR_ROW = 3200
_KR_PAD_ROW = _KR_BLK * LANES + QK_NOPE


def _shard_rows(name, a):
    return jnp.transpose(a) if name == "w_in" else a.reshape(PACK_ROWS[name], D_MODEL)


def _pack(group, shards, dtype):
    parts = [_shard_rows(n, shards[n]).astype(dtype) for n in group]
    pad = -sum(PACK_ROWS[n] for n in group) % LANES
    if pad:
        parts.append(jnp.zeros((pad, D_MODEL), dtype))
    return jnp.concatenate(parts, axis=0)


def _col_sharded_full(g, name, group):
    r, c = SHARD_SHAPES[name]
    off = _row_offset(group, name)
    blocks = g[:, off:off + PACK_ROWS[name]].reshape(N_CHIPS, r, c)
    return jnp.transpose(blocks, (1, 0, 2)).reshape(r, N_CHIPS * c)


def _col_sharded_blocks(d, name):
    r, c = SHARD_SHAPES[name]
    return jnp.transpose(d.reshape(r, N_CHIPS, c), (1, 0, 2)).reshape(N_CHIPS, PACK_ROWS[name], D_MODEL)


def _weights_a(g):
    dt = g.dtype
    w_in_t = g[:, :_W_IN_ROWS].reshape(N_CHIPS * _W_IN_ROWS, D_MODEL)
    z = lambda n: jnp.zeros((n, D_MODEL), dt)
    w_in_t = jnp.concatenate([w_in_t[:_KR_ROW], z(_KR_PAD_ROW - _KR_ROW), w_in_t[_KR_ROW:],
                              z(D_IN_PAD - _KR_PAD_ROW - QK_ROPE)], axis=0)
    wq = _col_sharded_full(g, "w_q_b", GROUP_A).reshape(Q_LORA, N_HEADS_B, Q_HEAD_B)
    wq_p = jnp.concatenate([wq, jnp.zeros((Q_LORA, N_HEADS_B, HEAD_PAD - Q_HEAD_B), dt)], axis=2).reshape(Q_LORA, MLA_W)
    wkv = _col_sharded_full(g, "w_kv_b", GROUP_A).reshape(KV_LORA, N_HEADS_B, QK_NOPE + V_DIM_B)
    zk = jnp.zeros((KV_LORA, N_HEADS_B, HEAD_PAD - QK_NOPE), dt)
    wk_p = jnp.concatenate([wkv[:, :, :QK_NOPE], zk], axis=2).reshape(KV_LORA, MLA_W)
    wv = wkv[:, :, QK_NOPE:].reshape(KV_LORA, N_HEADS_B * V_DIM_B)
    return dict(w_in=w_in_t, wq=wq_p, wk=wk_p, wv=wv, wv_t=jnp.transpose(wv))


def _grad_blocks_a(dw_in_t, dwq_p, dwk_p, dwv):
    dw_in = jnp.concatenate([dw_in_t[:_KR_ROW], dw_in_t[_KR_PAD_ROW:_KR_PAD_ROW + QK_ROPE]], axis=0)
    dwq = dwq_p.reshape(Q_LORA, N_HEADS_B, HEAD_PAD)[:, :, :Q_HEAD_B].reshape(Q_LORA, N_HEADS_B * Q_HEAD_B)
    dwk = dwk_p.reshape(KV_LORA, N_HEADS_B, HEAD_PAD)[:, :, :QK_NOPE]
    dwkv = jnp.concatenate([dwk, dwv.reshape(KV_LORA, N_HEADS_B, V_DIM_B)], axis=2)
    dwkv = dwkv.reshape(KV_LORA, N_HEADS_B * (QK_NOPE + V_DIM_B))
    pad = -sum(PACK_ROWS[n] for n in GROUP_A) % LANES
    return jnp.concatenate([dw_in.reshape(N_CHIPS, _W_IN_ROWS, D_MODEL), _col_sharded_blocks(dwq, "w_q_b"),
                            _col_sharded_blocks(dwkv, "w_kv_b"), jnp.zeros((N_CHIPS, pad, D_MODEL), F32)], axis=1)


def _rope_freq_lanes():
    freqs = ROPE_THETA ** (-jnp.arange(0, QK_ROPE, 2, dtype=F32) / QK_ROPE)
    return jnp.concatenate([jnp.zeros((QK_NOPE,), F32), freqs, freqs,
                            jnp.zeros((HEAD_PAD - Q_HEAD_B,), F32)]).reshape(1, LANES)


def _fwd_bwd(x, positions, target, w):
    t = x.shape[0]
    wa = _weights_a(_all_gather_chips(_pack(GROUP_A, w, BF16)))
    posr = positions.astype(F32).reshape(1, t)
    posc = posr.reshape(t, 1)
    freq = _rope_freq_lanes()
    g1, g2, g3, g4 = w["pre_norm_mix"], w["post_norm_mix"], w["pre_norm_mlp"], w["post_norm_mlp"]
    qan, kvan, sinks = w["q_a_norm"], w["kv_a_norm"], w["sinks"]

    h, proj = _proj_fwd(x, g1, wa["w_in"])
    out_a, lse_a = _swa_fwd(proj, posc, posr, sinks)
    qm, km, qt, kt, vt = _mla_prep_fwd(proj, posc, freq, qan, kvan, wa["wq"], wa["wk"], wa["wv_t"])
    out_bt, lse_b, wb = _mla_fwd(km, qt, vt, _pack(GROUP_B, w, BF16))
    w_oa, w_ob = _col_sharded_full(wb, "w_o_a", GROUP_B), _col_sharded_full(wb, "w_o_b", GROUP_B)
    merged, y, x1, h2 = _mix_out_fwd(out_a, out_bt, proj, x, w_oa, w_ob, wb, g2, g3)
    a = _up_fwd(h2, wb)
    dx2, dyd, dg4, loss = _down_fwd_loss(a, wb, x1, target, g4)

    gp_b = _dw_into_blocks(a, dyd, "w_down", 1024, 512)
    du = _down_bwd(dyd, wb, a)
    gp_b = _dw_into_blocks(h2, du, "w_up", 1024, 512, gp_b)
    dx1, dy, dg3, dg2 = _up_bwd(du, wb, x1, dx2, y, g3, g2)
    gp_b = _dw_into_blocks(merged, dy, "w_out", 1024, 512, gp_b)
    doa, dob, dga, dgb, d_out_a, d_out_b, d_out_bt, del_a, lrep_b, drep_b = _mix_out_bwd(
        dy, out_a, out_bt, lse_b, proj, w_oa, w_ob, wb)
    dw_oa = _matmul_tn(out_a, doa, "dw_o_a", 512, 1024)
    dw_ob = _dw_ob(out_bt, dob)
    small_b = jnp.concatenate([_col_sharded_blocks(dw_oa, "w_o_a"), _col_sharded_blocks(dw_ob, "w_o_b")], axis=1)
    gp_b = lax.dynamic_update_slice(gp_b, small_b, (0, _row_offset(GROUP_B, "w_o_a"), 0))
    dqm, dkm, dvm, land_b = _mla_bwd(qm, km, qt, kt, vt, d_out_b, d_out_bt, lrep_b, drep_b, gp_b)
    chip = (2 * lax.axis_index("x") + lax.axis_index("y")).astype(jnp.int32).reshape(1)
    part_b = _sum4(gp_b, land_b, chip, "rs_sum_b")
    dcq, dckv, dkr, dwq, dwk, dwv, dqan, dkvan, sib_b = _mla_prep_bwd(
        dqm, dkm, dvm, proj, posc, freq, qan, kvan, wa["wq"], wa["wk"], wa["wv"], part_b)
    dqa, dka, dva, dsinks = _swa_bwd(proj, d_out_a, lse_a, del_a, posc, posr, sinks)
    dproj = jnp.concatenate([dga, dgb, dqa.astype(BF16), dka.astype(BF16), dva.astype(BF16), dcq, dckv, dkr], axis=1)
    dw_in_t = _matmul_tn(dproj, h, "dw_in", D_IN_PAD // 2, 1024, tk=512)
    gp_a = _grad_blocks_a(dw_in_t, dwq, dwk, dwv)
    grad_x, dg1, land_a = _in_bwd(dproj, wa["w_in"], x, dx1, g1, gp_a.astype(BF16))

    part_a = _sum4(gp_a, land_a, chip, "rs_sum_a")
    reduced = {GROUP_A: [part_a, _swap_sibling(part_a, "rs_swap_a")], GROUP_B: [part_b, sib_b]}
    dsmall = dict(pre_norm_mix=dg1, post_norm_mix=dg2, pre_norm_mlp=dg3, post_norm_mlp=dg4,
                  q_a_norm=dqan, kv_a_norm=dkvan, sinks=dsinks)
    return loss, grad_x, reduced, dsmall


def kernel(x, positions, pre_norm_mix, w_in, q_a_norm, w_q_b, kv_a_norm, w_kv_b, sinks, w_o_a, w_o_b, w_out, post_norm_mix, pre_norm_mlp, w_up, w_down, post_norm_mlp, loss_target, m_pre_norm_mix, m_w_in, m_q_a_norm, m_w_q_b, m_kv_a_norm, m_w_kv_b, m_sinks, m_w_o_a, m_w_o_b, m_w_out, m_post_norm_mix, m_pre_norm_mlp, m_w_up, m_w_down, m_post_norm_mlp, v_pre_norm_mix, v_w_in, v_q_a_norm, v_w_q_b, v_kv_a_norm, v_w_kv_b, v_sinks, v_w_o_a, v_w_o_b, v_w_out, v_post_norm_mix, v_pre_norm_mlp, v_w_up, v_w_down, v_post_norm_mlp):
    w = dict(pre_norm_mix=pre_norm_mix, w_in=w_in[0], q_a_norm=q_a_norm, w_q_b=w_q_b[0], kv_a_norm=kv_a_norm,
             w_kv_b=w_kv_b[0], sinks=sinks, w_o_a=w_o_a[0], w_o_b=w_o_b[0], w_out=w_out[0],
             post_norm_mix=post_norm_mix, pre_norm_mlp=pre_norm_mlp, w_up=w_up[0], w_down=w_down[0],
             post_norm_mlp=post_norm_mlp)
    m = dict(pre_norm_mix=m_pre_norm_mix, w_in=m_w_in[0], q_a_norm=m_q_a_norm, w_q_b=m_w_q_b[0],
             kv_a_norm=m_kv_a_norm, w_kv_b=m_w_kv_b[0], sinks=m_sinks, w_o_a=m_w_o_a[0], w_o_b=m_w_o_b[0],
             w_out=m_w_out[0], post_norm_mix=m_post_norm_mix, pre_norm_mlp=m_pre_norm_mlp, w_up=m_w_up[0],
             w_down=m_w_down[0], post_norm_mlp=m_post_norm_mlp)
    v = dict(pre_norm_mix=v_pre_norm_mix, w_in=v_w_in[0], q_a_norm=v_q_a_norm, w_q_b=v_w_q_b[0],
             kv_a_norm=v_kv_a_norm, w_kv_b=v_w_kv_b[0], sinks=v_sinks, w_o_a=v_w_o_a[0], w_o_b=v_w_o_b[0],
             w_out=v_w_out[0], post_norm_mix=v_post_norm_mix, pre_norm_mlp=v_pre_norm_mlp, w_up=v_w_up[0],
             w_down=v_w_down[0], post_norm_mlp=v_post_norm_mlp)

    loss, grad_x, reduced, dsmall = _fwd_bwd(x[0], positions, loss_target[0], w)

    red = _all_reduce_small(dsmall, loss)
    small = _adamw_small(red, w, m, v)

    big = {}
    tr = jnp.transpose
    big["w_in"] = [tr(o)[None] for o in _adamw(tr(w["w_in"]), reduced[GROUP_A], tr(m["w_in"]), tr(v["w_in"]),
                                               "adamw_w_in", (_W_IN_ROWS, 256))]
    for n in ("w_up", "w_down", "w_out"):
        big[n] = [o[None] for o in _adamw(w[n], reduced[GROUP_B], m[n], v[n], "adamw_" + n, (128, D_MODEL),
                                          _row_offset(GROUP_B, n))]
    for group, names in ((GROUP_A, ("w_q_b", "w_kv_b")), (GROUP_B, ("w_o_a", "w_o_b"))):
        for n in names:
            off = _row_offset(group, n)
            g_parts = [p[off:off + PACK_ROWS[n]].reshape(SHARD_SHAPES[n]) for p in reduced[group]]
            big[n] = [o[None] for o in _adamw(w[n], g_parts, m[n], v[n], "adamw_" + n, SHARD_SHAPES[n])]

    outs = [big[n][k] if n in big else small[n][k] for k in range(4) for n in WEIGHTS]
    return (red[_LOSS_ROW, 0], grad_x[None], *outs)
```

```python
import jax
import jax.numpy as jnp
from jax import lax
from jax.experimental import pallas as pl
from jax.experimental.pallas import tpu as pltpu

F32 = jnp.float32
BF16 = jnp.bfloat16
MESH = pl.DeviceIdType.MESH

D_MODEL = 1024
N_HEADS_A = 8
N_KV_A = 2
HEAD_DIM_A = 64
WINDOW = 128
BLOCK = 128
N_HEADS_B = 8
QK_NOPE = 64
QK_ROPE = 32
V_DIM_B = 64
Q_LORA = 256
KV_LORA = 128
ROPE_THETA = 10000.0
D_FF = 4 * D_MODEL
EPS = 1e-6
WIDTH_A = N_HEADS_A * HEAD_DIM_A
Q_HEAD_B = QK_NOPE + QK_ROPE
D_IN_PAD = 3328
HEAD_PAD = 128
MLA_W = N_HEADS_B * HEAD_PAD

ADAM_LR = 0.001
ADAM_B1 = 0.9
ADAM_B2 = 0.999
ADAM_EPS = 1e-08
ADAM_WD = 0.01
ADAM_STEP = 10

NEG = -1e30
N_CHIPS = 4
LANES = 128
VMEM_LIMIT = 56 * 1024 * 1024

SHARD_SHAPES = {"w_in": (1024, 808), "w_q_b": (256, 192), "w_kv_b": (128, 256), "w_o_a": (512, 256),
                "w_o_b": (512, 256), "w_out": (256, 1024), "w_up": (1024, 1024), "w_down": (1024, 1024)}
PACK_ROWS = {n: (s[0] * s[1]) // D_MODEL for n, s in SHARD_SHAPES.items()}
GROUP_A = ("w_in", "w_q_b", "w_kv_b")
GROUP_B = ("w_up", "w_down", "w_out", "w_o_a", "w_o_b")
WEIGHTS = ("pre_norm_mix", "w_in", "q_a_norm", "w_q_b", "kv_a_norm", "w_kv_b", "sinks", "w_o_a", "w_o_b", "w_out",
           "post_norm_mix", "pre_norm_mlp", "w_up", "w_down", "post_norm_mlp")


def _params(sem=None):
    return pltpu.CompilerParams(dimension_semantics=sem, vmem_limit_bytes=VMEM_LIMIT)


def _dot(a, b):
    return jnp.dot(a, b, preferred_element_type=F32)


def _dot_nt(a, b):
    return lax.dot_general(a, b, (((1,), (1,)), ((), ())), preferred_element_type=F32)


def _dot_tn(a, b):
    return lax.dot_general(a, b, (((0,), (0,)), ((), ())), preferred_element_type=F32)


def _rms(v):
    return lax.rsqrt(jnp.mean(v * v, axis=-1, keepdims=True) + EPS)


def _norm_bwd(dout, n, r, g):
    dn = dout * g
    dx = r * (dn - n * jnp.mean(dn * n, axis=-1, keepdims=True))
    return dx, jnp.sum(dout * n, axis=0, keepdims=True)


def _full(shape):
    return pl.BlockSpec(shape, lambda *_: (0,) * len(shape))


def _row_offset(group, name):
    return sum(PACK_ROWS[n] for n in group[:group.index(name)])


def _wb_spec(name):
    rows = PACK_ROWS[name]
    return pl.BlockSpec((N_CHIPS, rows, D_MODEL), lambda *_: (0, _row_offset(GROUP_B, name) // rows, 0))


def _proj_fwd(x, g1, w_in_t):
    t = x.shape[0]
    tm = 256

    def body(x_ref, g_ref, w_ref, h_ref, p_ref):
        xv = x_ref[...]
        h = ((xv * _rms(xv)) * g_ref[...]).astype(BF16)
        h_ref[...] = h
        p_ref[...] = _dot_nt(h, w_ref[...])

    return pl.pallas_call(
        body, name="proj_fwd", grid=(t // tm,),
        in_specs=[pl.BlockSpec((tm, D_MODEL), lambda i: (i, 0)), _full((1, D_MODEL)), _full((D_IN_PAD, D_MODEL))],
        out_specs=[pl.BlockSpec((tm, D_MODEL), lambda i: (i, 0)), pl.BlockSpec((tm, D_IN_PAD), lambda i: (i, 0))],
        out_shape=[jax.ShapeDtypeStruct((t, D_MODEL), BF16), jax.ShapeDtypeStruct((t, D_IN_PAD), F32)],
        compiler_params=_params(("parallel",)),
    )(x, g1, w_in_t)


_QA_BLK = 2048 // WIDTH_A
_KA_BLK = 2560 // LANES
_VA_BLK = 2688 // LANES
_CQ_BLK = 2816 // Q_LORA
_CKV_BLK = 3072 // LANES
_KR_BLK = 3200 // LANES


_GROUP_A = N_HEADS_A // N_KV_A
_SWA_SCALE = HEAD_DIM_A ** -0.5
_LOG2E = 1.4426950408889634


def _head_cols(v, h):
    return v[:, HEAD_DIM_A * h:HEAD_DIM_A * (h + 1)]


def _head_rows(v, h):
    return v[HEAD_DIM_A * h:HEAD_DIM_A * (h + 1), :]


def _swa_band(n, kp_ref, kc_ref, vp_ref, vc_ref, pq_ref, pp_ref, pc_ref):
    kb = jnp.concatenate([kp_ref[...], kc_ref[...]], axis=0)
    vb = jnp.concatenate([vp_ref[...], vc_ref[...]], axis=0)
    posk = jnp.concatenate([pp_ref[...], pc_ref[...]], axis=0)
    dist = jnp.abs(posk - pq_ref[...])
    ki = lax.broadcasted_iota(jnp.int32, (2 * BLOCK, BLOCK), 0)
    qi = lax.broadcasted_iota(jnp.int32, (2 * BLOCK, BLOCK), 1)
    valid = (ki > qi) & (ki <= qi + WINDOW) & ((n > 0) | (ki >= BLOCK))
    return kb, vb, dist, valid


def _swa_scores_t(st_g, j, h, dist, valid):
    slope = 2.0 ** (-8.0 * (h + 1) / N_HEADS_A)
    st = st_g[:, BLOCK * j:BLOCK * (j + 1)] * (_SWA_SCALE * _LOG2E) - (slope * _LOG2E) * dist
    return jnp.where(valid, st, NEG)


def _group_t(xt, kh):
    return jnp.concatenate([_head_rows(xt, _GROUP_A * kh + j) for j in range(_GROUP_A)], axis=1).astype(BF16)


def _swa_fwd(proj, posc, posr, sinks):
    t = proj.shape[0]
    nb = t // BLOCK

    def body(q_ref, kc_ref, kp_ref, vc_ref, vp_ref, pq_ref, pc_ref, pp_ref, sink_ref, o_ref, l_ref):
        n = pl.program_id(0)
        kb, vb, dist, valid = _swa_band(n, kp_ref, kc_ref, vp_ref, vc_ref, pq_ref, pp_ref, pc_ref)
        q_t, vb_t = q_ref[...].T, vb.T
        out_t, lse = [], []
        for kh in range(N_KV_A):
            st_g = _dot(_head_cols(kb, kh).astype(BF16), _group_t(q_t, kh))
            ps = []
            for j in range(_GROUP_A):
                h = _GROUP_A * kh + j
                st = _swa_scores_t(st_g, j, h, dist, valid)
                sink = sink_ref[0:1, h:h + 1] * _LOG2E
                m = jnp.maximum(jnp.max(st, axis=0, keepdims=True), sink)
                e = jnp.exp2(st - m)
                den = jnp.sum(e, axis=0, keepdims=True) + jnp.exp2(sink - m)
                ps.append((e * (1.0 / den)).astype(BF16))
                lse.append(m + jnp.log(den) * _LOG2E)
            o_g = _dot(_head_rows(vb_t, kh).astype(BF16), jnp.concatenate(ps, axis=1))
            out_t.extend(o_g[:, BLOCK * j:BLOCK * (j + 1)] for j in range(_GROUP_A))
        o_ref[...] = jnp.concatenate(out_t, axis=0).T
        l_ref[...] = jnp.concatenate(lse, axis=0)

    cur = lambda n: (n, 0)
    prev = lambda n: jnp.maximum(n - 1, 0)
    return pl.pallas_call(
        body, name="swa_fwd", grid=(nb,),
        in_specs=[pl.BlockSpec((BLOCK, WIDTH_A), lambda n: (n, _QA_BLK)),
                  pl.BlockSpec((BLOCK, LANES), lambda n: (n, _KA_BLK)),
                  pl.BlockSpec((BLOCK, LANES), lambda n: (prev(n), _KA_BLK)),
                  pl.BlockSpec((BLOCK, LANES), lambda n: (n, _VA_BLK)),
                  pl.BlockSpec((BLOCK, LANES), lambda n: (prev(n), _VA_BLK)),
                  pl.BlockSpec((1, BLOCK), lambda n: (0, n)),
                  pl.BlockSpec((BLOCK, 1), cur),
                  pl.BlockSpec((BLOCK, 1), lambda n: (prev(n), 0)),
                  _full((1, N_HEADS_A))],
        out_specs=[pl.BlockSpec((BLOCK, WIDTH_A), cur), pl.BlockSpec((N_HEADS_A, BLOCK), lambda n: (0, n))],
        out_shape=[jax.ShapeDtypeStruct((t, WIDTH_A), F32), jax.ShapeDtypeStruct((N_HEADS_A, t), F32)],
        compiler_params=_params(("parallel",)),
    )(proj, proj, proj, proj, proj, posr, posc, posc, sinks)


def _rope_coeffs(pos, freq):
    ang = pos * freq
    cosv, sinv = jnp.cos(ang), jnp.sin(ang)
    lane = lax.broadcasted_iota(jnp.int32, ang.shape, 1)
    lo = (lane >= QK_NOPE) & (lane < QK_NOPE + QK_ROPE // 2)
    hi = (lane >= QK_NOPE + QK_ROPE // 2) & (lane < QK_NOPE + QK_ROPE)
    c = jnp.where(lane < QK_NOPE, 1.0, jnp.where(lo | hi, cosv, 0.0))
    s = jnp.where(lo, -sinv, jnp.where(hi, sinv, 0.0))
    return c, s, lo, hi


def _rope(xh, c, s, lo):
    up = pltpu.roll(xh, LANES - QK_ROPE // 2, axis=1)
    dn = pltpu.roll(xh, QK_ROPE // 2, axis=1)
    return xh * c + jnp.where(lo, up, dn) * s


def _unrope(dh, c, s, lo, hi):
    g = dh * s
    up = pltpu.roll(g, LANES - QK_ROPE // 2, axis=1)
    dn = pltpu.roll(g, QK_ROPE // 2, axis=1)
    return dh * c + jnp.where(hi, dn, jnp.where(lo, up, 0.0))


_TQ = 512
_MLA_SCALE = Q_HEAD_B ** -0.5


def _mla_prep_fwd(proj, posc, freq, qan, kvan, wq, wk, wv):
    t = proj.shape[0]
    tm = _TQ
    nb = t // tm

    def body(cq_ref, ckv_ref, kr_ref, pos_ref, f_ref, qan_ref, kvan_ref, wq_ref, wk_ref, wv_ref,
             q_ref, k_ref, qt_ref, kt_ref, vt_ref):
        cq = cq_ref[...]
        cqn = ((cq * _rms(cq)) * qan_ref[...]).astype(BF16)
        ckv = ckv_ref[...]
        ckvn = ((ckv * _rms(ckv)) * kvan_ref[...]).astype(BF16)
        qb = _dot(cqn, wq_ref[...])
        kb = _dot(ckvn, wk_ref[...])
        vbt = _dot_nt(wv_ref[...], ckvn)
        c, s, lo, _ = _rope_coeffs(pos_ref[...], f_ref[...])
        kr = _rope(kr_ref[...], c, s, lo)
        for h in range(N_HEADS_B):
            sl = slice(HEAD_PAD * h, HEAD_PAD * (h + 1))
            q_h = _rope(qb[:, sl], c, s, lo)
            k_h = kb[:, sl] + kr
            q_ref[:, sl] = q_h.astype(BF16)
            k_ref[:, sl] = k_h.astype(BF16)
            qt_ref[h, 0] = q_h.T.astype(BF16)
            kt_ref[h, 0] = k_h.T.astype(BF16)
            vt_ref[h, 0] = vbt[V_DIM_B * h:V_DIM_B * (h + 1), :].astype(BF16)

    row = lambda i: (i, 0)
    blk4 = lambda d: pl.BlockSpec((N_HEADS_B, 1, d, tm), lambda i: (0, i, 0, 0))
    return pl.pallas_call(
        body, name="mla_prep_fwd", grid=(nb,),
        in_specs=[pl.BlockSpec((tm, Q_LORA), lambda i: (i, _CQ_BLK)),
                  pl.BlockSpec((tm, LANES), lambda i: (i, _CKV_BLK)),
                  pl.BlockSpec((tm, LANES), lambda i: (i, _KR_BLK)),
                  pl.BlockSpec((tm, 1), row), _full((1, LANES)), _full((1, Q_LORA)), _full((1, KV_LORA)),
                  _full((Q_LORA, MLA_W)), _full((KV_LORA, MLA_W)), _full((N_HEADS_B * V_DIM_B, KV_LORA))],
        out_specs=[pl.BlockSpec((tm, MLA_W), row), pl.BlockSpec((tm, MLA_W), row), blk4(HEAD_PAD), blk4(HEAD_PAD),
                   blk4(V_DIM_B)],
        out_shape=[jax.ShapeDtypeStruct((t, MLA_W), BF16), jax.ShapeDtypeStruct((t, MLA_W), BF16),
                   jax.ShapeDtypeStruct((N_HEADS_B, nb, HEAD_PAD, tm), BF16),
                   jax.ShapeDtypeStruct((N_HEADS_B, nb, HEAD_PAD, tm), BF16),
                   jax.ShapeDtypeStruct((N_HEADS_B, nb, V_DIM_B, tm), BF16)],
        compiler_params=_params(("parallel",)),
    )(proj, proj, proj, posc, freq, qan, kvan, wq, wk, wv)


_MLA_SCALE2 = _MLA_SCALE * _LOG2E


def _mla_fwd(k, qt, vt, w_src):
    t = k.shape[0]
    nb = t // _TQ

    def body(k_ref, qt_ref, vt_ref, w_ref, o_ref, l_ref, wg_ref, raw_a, raw_b, send_sems, recv_sems, local_sem):
        qi = pl.program_id(1)
        first = (pl.program_id(0) == 0) & (qi == 0)
        last = (pl.program_id(0) == N_HEADS_B - 1) & (qi == nb - 1)

        @pl.when(first)
        def _():
            _gather_start(w_ref, wg_ref, send_sems, recv_sems, local_sem)

        q_t = qt_ref[0, 0]

        def product(kj):
            return _dot(k_ref[pl.ds(pl.multiple_of(kj * _TQ, _TQ), _TQ), :], q_t)

        def update(stats, raw_ref, kj, diagonal=False):
            m, l, acc = stats
            raw = raw_ref[...]
            if diagonal:
                key = lax.broadcasted_iota(jnp.int32, raw.shape, 0)
                qry = lax.broadcasted_iota(jnp.int32, raw.shape, 1)
                raw = jnp.where(key <= qry, raw, NEG)
            m_new = jnp.maximum(m, jnp.max(raw, axis=0, keepdims=True) * _MLA_SCALE2)
            alpha = jnp.exp2(m - m_new)
            p = jnp.exp2(raw * _MLA_SCALE2 - m_new)
            l = alpha * l + jnp.sum(p, axis=0, keepdims=True)
            acc = alpha * acc + _dot(vt_ref[0, kj], p.astype(BF16))
            return m_new, l, acc

        def trip(i, stats):
            raw_b[...] = product(2 * i + 1)
            stats = update(stats, raw_a, 2 * i)
            raw_a[...] = product(2 * i + 2)
            return update(stats, raw_b, 2 * i + 1)

        def tail_even(stats):
            return update(stats, raw_a, qi, True)

        def tail_odd(stats):
            raw_b[...] = product(qi)
            return update(update(stats, raw_a, qi - 1), raw_b, qi, True)

        init = (jnp.full((1, _TQ), NEG, F32), jnp.zeros((1, _TQ), F32), jnp.zeros((V_DIM_B, _TQ), F32))
        raw_a[...] = product(0)
        stats = lax.fori_loop(0, qi // 2, trip, init)
        m, l, acc = lax.cond(qi % 2 == 0, tail_even, tail_odd, stats)
        o_ref[0, 0] = acc / l
        l_ref[0, 0] = m + jnp.log(l) * _LOG2E

        @pl.when(last)
        def _():
            _gather_wait(w_ref, wg_ref, send_sems, recv_sems, local_sem)

    return pl.pallas_call(
        body, name="mla_fwd", grid=(N_HEADS_B, nb),
        in_specs=[pl.BlockSpec((t, HEAD_PAD), lambda h, qi: (0, h)),
                  pl.BlockSpec((1, 1, HEAD_PAD, _TQ), lambda h, qi: (h, qi, 0, 0)),
                  pl.BlockSpec((1, nb, V_DIM_B, _TQ), lambda h, qi: (h, 0, 0, 0)), _HBM],
        out_specs=[pl.BlockSpec((1, 1, V_DIM_B, _TQ), lambda h, qi: (h, qi, 0, 0)),
                   pl.BlockSpec((1, 1, 1, _TQ), lambda h, qi: (h, qi, 0, 0)), _HBM],
        out_shape=[jax.ShapeDtypeStruct((N_HEADS_B, nb, V_DIM_B, _TQ), F32),
                   jax.ShapeDtypeStruct((N_HEADS_B, nb, 1, _TQ), F32),
                   jax.ShapeDtypeStruct((N_CHIPS,) + w_src.shape, w_src.dtype)],
        scratch_shapes=[pltpu.VMEM((_TQ, _TQ), F32), pltpu.VMEM((_TQ, _TQ), F32),
                        pltpu.SemaphoreType.DMA((3,)), pltpu.SemaphoreType.DMA((3,)), pltpu.SemaphoreType.DMA(())],
        compiler_params=_params(("arbitrary", "arbitrary")),
    )(k, qt, vt, w_src)


def _ot_spec(tm, d):
    per = _TQ // tm
    return pl.BlockSpec((N_HEADS_B, 1, d, tm), lambda i: (0, i // per, 0, i % per))


def _mix_out_fwd(out_a, out_bt, proj, x, w_oa, w_ob, wb, g2, g3):
    t = x.shape[0]
    tm = 256

    def body(oa_ref, obt_ref, ga_ref, gb_ref, x_ref, woa_ref, wob_ref, wout_ref, g2_ref, g3_ref,
             mg_ref, y_ref, x1_ref, h2_ref):
        oa = _dot(oa_ref[...].astype(BF16), woa_ref[...])
        obt = obt_ref[...].reshape(N_HEADS_B * V_DIM_B, tm).astype(BF16)
        ob = _dot_tn(obt, wob_ref[...])
        merged = (jax.nn.sigmoid(ga_ref[...]) * oa + jax.nn.sigmoid(gb_ref[...]) * ob).astype(BF16)
        mg_ref[...] = merged
        y = _dot(merged, wout_ref[...].reshape(D_MODEL, D_MODEL))
        y_ref[...] = y
        x1 = x_ref[...] + (y * _rms(y)) * g2_ref[...]
        x1_ref[...] = x1
        h2_ref[...] = ((x1 * _rms(x1)) * g3_ref[...]).astype(BF16)

    row = lambda i: (i, 0)
    blk = pl.BlockSpec((tm, D_MODEL), row)
    return pl.pallas_call(
        body, name="mix_out_fwd", grid=(t // tm,),
        in_specs=[pl.BlockSpec((tm, WIDTH_A), row), _ot_spec(tm, V_DIM_B), pl.BlockSpec((tm, D_MODEL), lambda i: (i, 0)),
                  pl.BlockSpec((tm, D_MODEL), lambda i: (i, 1)), blk,
                  _full((WIDTH_A, D_MODEL)), _full((N_HEADS_B * V_DIM_B, D_MODEL)), _wb_spec("w_out"),
                  _full((1, D_MODEL)), _full((1, D_MODEL))],
        out_specs=[blk, blk, blk, blk],
        out_shape=[jax.ShapeDtypeStruct((t, D_MODEL), BF16), jax.ShapeDtypeStruct((t, D_MODEL), F32),
                   jax.ShapeDtypeStruct((t, D_MODEL), F32), jax.ShapeDtypeStruct((t, D_MODEL), BF16)],
        compiler_params=_params(("parallel",)),
    )(out_a, out_bt, proj, proj, x, w_oa, w_ob, wb, g2, g3)


_TM_MLP = 512


def _up_fwd(h2, wb):
    t = h2.shape[0]
    tm = _TM_MLP

    def body(h_ref, w_ref, a_ref):
        hv = h_ref[...]
        for j in range(N_CHIPS):
            u = _dot(hv, w_ref[j])
            a_ref[:, D_MODEL * j:D_MODEL * (j + 1)] = jnp.square(jnp.maximum(u, 0.0)).astype(BF16)

    return pl.pallas_call(
        body, name="up_fwd", grid=(t // tm,),
        in_specs=[pl.BlockSpec((tm, D_MODEL), lambda i: (i, 0)), _wb_spec("w_up")],
        out_specs=pl.BlockSpec((tm, D_FF), lambda i: (i, 0)),
        out_shape=jax.ShapeDtypeStruct((t, D_FF), BF16),
        compiler_params=_params(("parallel",)),
    )(h2, wb)


def _down_fwd_loss(a, wb, x1, target, g4):
    t = a.shape[0]
    tm = _TM_MLP

    def body(a_ref, w_ref, x1_ref, tg_ref, g_ref, dx2_ref, dyd_ref, dg_ref, loss_ref):
        @pl.when(pl.program_id(0) == 0)
        def _():
            dg_ref[...] = jnp.zeros(dg_ref.shape, F32)
            loss_ref[...] = jnp.zeros(loss_ref.shape, F32)

        yd = _dot(a_ref[...], w_ref[...].reshape(D_FF, D_MODEL))
        r = _rms(yd)
        n = yd * r
        diff = (x1_ref[...] + n * g_ref[...]) - tg_ref[...]
        loss_ref[...] += 0.5 * jnp.sum(jnp.mean(diff * diff, axis=-1, keepdims=True), axis=0, keepdims=True)
        dx2 = diff * (1.0 / D_MODEL)
        dx2_ref[...] = dx2
        dyd, dg = _norm_bwd(dx2, n, r, g_ref[...])
        dyd_ref[...] = dyd.astype(BF16)
        dg_ref[...] += dg

    row = lambda i: (i, 0)
    blk = pl.BlockSpec((tm, D_MODEL), row)
    return pl.pallas_call(
        body, name="down_fwd_loss", grid=(t // tm,),
        in_specs=[pl.BlockSpec((tm, D_FF), row), _wb_spec("w_down"), blk, blk, _full((1, D_MODEL))],
        out_specs=[blk, blk, _full((1, D_MODEL)), _full((1, LANES))],
        out_shape=[jax.ShapeDtypeStruct((t, D_MODEL), F32), jax.ShapeDtypeStruct((t, D_MODEL), BF16),
                   jax.ShapeDtypeStruct((1, D_MODEL), F32), jax.ShapeDtypeStruct((1, LANES), F32)],
        compiler_params=_params(("arbitrary",)),
    )(a, wb, x1, target, g4)


def _matmul_tn(a, b, name, tm, tn, tk=1024):
    t, m = a.shape
    n = b.shape[1]
    tk = min(tk, t)
    nk = t // tk

    def body(a_ref, b_ref, o_ref):
        @pl.when(pl.program_id(2) == 0)
        def _():
            o_ref[...] = jnp.zeros(o_ref.shape, F32)

        o_ref[...] += _dot_tn(a_ref[...].astype(BF16), b_ref[...].astype(BF16))

    return pl.pallas_call(
        body, name=name, grid=(m // tm, n // tn, nk),
        in_specs=[pl.BlockSpec((tk, tm), lambda i, j, k: (k, i)), pl.BlockSpec((tk, tn), lambda i, j, k: (k, j))],
        out_specs=pl.BlockSpec((tm, tn), lambda i, j, k: (i, j)),
        out_shape=jax.ShapeDtypeStruct((m, n), F32),
        compiler_params=_params(("parallel", "parallel", "arbitrary")),
    )(a, b)


def _dw_into_blocks(a, b, weight, tm, tk, buf=None):
    t, m = a.shape
    n = b.shape[1]
    nk = t // tk
    rows = PACK_ROWS[weight]
    br = min(tm, rows)
    chips = tm // br
    first = _row_offset(GROUP_B, weight) // br
    per_chip = rows // br
    if weight == "w_up":
        out_map = lambda i, j, k: (j, first + i, 0)
    elif chips > 1:
        out_map = lambda i, j, k: (i, first, 0)
    else:
        out_map = lambda i, j, k: (i // per_chip, first + i % per_chip, 0)

    def body(a_ref, b_ref, *rest):
        o_ref = rest[-1]

        @pl.when(pl.program_id(2) == 0)
        def _():
            o_ref[...] = jnp.zeros(o_ref.shape, F32)

        o_ref[...] += _dot_tn(a_ref[...].astype(BF16), b_ref[...].astype(BF16)).reshape(o_ref.shape)

    in_specs = [pl.BlockSpec((tk, tm), lambda i, j, k: (k, i)), pl.BlockSpec((tk, D_MODEL), lambda i, j, k: (k, j))]
    operands = [a, b]
    if buf is not None:
        in_specs.append(pl.BlockSpec(memory_space=pl.ANY))
        operands.append(buf)
    total = sum(PACK_ROWS[w] for w in GROUP_B)
    return pl.pallas_call(
        body, name="dw_" + weight[2:], grid=(m // tm, n // D_MODEL, nk),
        in_specs=in_specs, out_specs=pl.BlockSpec((chips, br, D_MODEL), out_map),
        out_shape=jax.ShapeDtypeStruct((N_CHIPS, total, D_MODEL), F32),
        input_output_aliases={} if buf is None else {2: 0},
        compiler_params=_params(("parallel", "parallel", "arbitrary")),
    )(*operands)


def _down_bwd(dyd, wb, a):
    t = dyd.shape[0]
    tm = _TM_MLP

    def body(d_ref, w_ref, a_ref, du_ref):
        da = _dot_nt(d_ref[...], w_ref[...].reshape(D_FF, D_MODEL))
        du_ref[...] = (da * (2.0 * jnp.sqrt(a_ref[...].astype(F32)))).astype(BF16)

    row = lambda i: (i, 0)
    return pl.pallas_call(
        body, name="down_bwd", grid=(t // tm,),
        in_specs=[pl.BlockSpec((tm, D_MODEL), row), _wb_spec("w_down"), pl.BlockSpec((tm, D_FF), row)],
        out_specs=pl.BlockSpec((tm, D_FF), row),
        out_shape=jax.ShapeDtypeStruct((t, D_FF), BF16),
        compiler_params=_params(("parallel",)),
    )(dyd, wb, a)


def _up_bwd(du, wb, x1, dx2, y, g3, g2):
    t = du.shape[0]
    tm = _TM_MLP

    def body(du_ref, w_ref, x1_ref, dx2_ref, y_ref, g3_ref, g2_ref, dx1_ref, dy_ref, dg3_ref, dg2_ref):
        @pl.when(pl.program_id(0) == 0)
        def _():
            dg3_ref[...] = jnp.zeros(dg3_ref.shape, F32)
            dg2_ref[...] = jnp.zeros(dg2_ref.shape, F32)

        dh2 = _dot_nt(du_ref[:, 0:D_MODEL], w_ref[0])
        for j in range(1, N_CHIPS):
            dh2 = dh2 + _dot_nt(du_ref[:, D_MODEL * j:D_MODEL * (j + 1)], w_ref[j])
        x1 = x1_ref[...]
        r3 = _rms(x1)
        d3, dg3 = _norm_bwd(dh2, x1 * r3, r3, g3_ref[...])
        dx1 = dx2_ref[...] + d3
        dx1_ref[...] = dx1
        dg3_ref[...] += dg3
        y = y_ref[...]
        r2 = _rms(y)
        dy, dg2 = _norm_bwd(dx1, y * r2, r2, g2_ref[...])
        dy_ref[...] = dy.astype(BF16)
        dg2_ref[...] += dg2

    row = lambda i: (i, 0)
    blk = pl.BlockSpec((tm, D_MODEL), row)
    return pl.pallas_call(
        body, name="up_bwd", grid=(t // tm,),
        in_specs=[pl.BlockSpec((tm, D_FF), row), _wb_spec("w_up"),
                  blk, blk, blk, _full((1, D_MODEL)), _full((1, D_MODEL))],
        out_specs=[blk, blk, _full((1, D_MODEL)), _full((1, D_MODEL))],
        out_shape=[jax.ShapeDtypeStruct((t, D_MODEL), F32), jax.ShapeDtypeStruct((t, D_MODEL), BF16),
                   jax.ShapeDtypeStruct((1, D_MODEL), F32), jax.ShapeDtypeStruct((1, D_MODEL), F32)],
        compiler_params=_params(("arbitrary",)),
    )(du, wb, x1, dx2, y, g3, g2)


def _mix_out_bwd(dy, out_a, out_bt, proj, w_oa, w_ob, wb):
    t = dy.shape[0]
    tm = 256
    nb = t // _TQ

    def body(dy_ref, oa_ref, obt_ref, ga_ref, gb_ref, woa_ref, wob_ref, wout_ref,
             doa_ref, dob_ref, dga_ref, dgb_ref, da_ref, db_ref, dbt_ref, dela_ref, delb_ref):
        dm = _dot_nt(dy_ref[...], wout_ref[...].reshape(D_MODEL, D_MODEL))
        out_a_v = oa_ref[...]
        out_bt_v = obt_ref[...].reshape(N_HEADS_B * V_DIM_B, tm)
        oa = _dot(out_a_v.astype(BF16), woa_ref[...])
        ob = _dot_tn(out_bt_v.astype(BF16), wob_ref[...])
        sa, sb = jax.nn.sigmoid(ga_ref[...]), jax.nn.sigmoid(gb_ref[...])
        doa = (dm * sa).astype(BF16)
        dob = (dm * sb).astype(BF16)
        doa_ref[...] = doa
        dob_ref[...] = dob
        dga_ref[...] = (dm * oa * (sa * (1.0 - sa))).astype(BF16)
        dgb_ref[...] = (dm * ob * (sb * (1.0 - sb))).astype(BF16)
        d_out_a = _dot_nt(doa, woa_ref[...])
        da_ref[...] = d_out_a
        prod_at = (d_out_a * out_a_v).T
        dela_ref[...] = jnp.concatenate(
            [jnp.sum(_head_rows(prod_at, h), axis=0, keepdims=True) for h in range(N_HEADS_A)], axis=0)
        d_out_b = _dot_nt(dob, wob_ref[...])
        d_out_bt = _dot_nt(wob_ref[...], dob)
        prod_bt = d_out_bt * out_bt_v
        for h in range(N_HEADS_B):
            db_ref[h] = d_out_b[:, V_DIM_B * h:V_DIM_B * (h + 1)].astype(BF16)
            dbt_ref[h, 0] = d_out_bt[V_DIM_B * h:V_DIM_B * (h + 1), :].astype(BF16)
            delb_ref[h, 0] = jnp.sum(prod_bt[V_DIM_B * h:V_DIM_B * (h + 1), :], axis=0, keepdims=True)

    row = lambda i: (i, 0)
    blk = pl.BlockSpec((tm, D_MODEL), row)
    return pl.pallas_call(
        body, name="mix_out_bwd", grid=(t // tm,),
        in_specs=[blk, pl.BlockSpec((tm, WIDTH_A), row), _ot_spec(tm, V_DIM_B),
                  pl.BlockSpec((tm, D_MODEL), lambda i: (i, 0)), pl.BlockSpec((tm, D_MODEL), lambda i: (i, 1)),
                  _full((WIDTH_A, D_MODEL)), _full((N_HEADS_B * V_DIM_B, D_MODEL)), _wb_spec("w_out")],
        out_specs=[blk, blk, blk, blk, pl.BlockSpec((tm, WIDTH_A), row),
                   pl.BlockSpec((N_HEADS_B, tm, V_DIM_B), lambda i: (0, i, 0)), _ot_spec(tm, V_DIM_B),
                   pl.BlockSpec((N_HEADS_A, tm), lambda i: (0, i)), _ot_spec(tm, 1)],
        out_shape=[jax.ShapeDtypeStruct((t, D_MODEL), BF16)] * 4
        + [jax.ShapeDtypeStruct((t, WIDTH_A), F32), jax.ShapeDtypeStruct((N_HEADS_B, t, V_DIM_B), BF16),
           jax.ShapeDtypeStruct((N_HEADS_B, nb, V_DIM_B, _TQ), BF16), jax.ShapeDtypeStruct((N_HEADS_A, t), F32),
           jax.ShapeDtypeStruct((N_HEADS_B, nb, 1, _TQ), F32)],
        compiler_params=_params(("parallel",)),
    )(dy, out_a, out_bt, proj, proj, w_oa, w_ob, wb)


def _dw_ob(out_bt, dob):
    t = dob.shape[0]
    nb = t // _TQ

    def body(obt_ref, dob_ref, o_ref):
        @pl.when(pl.program_id(0) == 0)
        def _():
            o_ref[...] = jnp.zeros(o_ref.shape, F32)

        obt = obt_ref[...].reshape(N_HEADS_B * V_DIM_B, _TQ).astype(BF16)
        o_ref[...] += _dot(obt, dob_ref[...])

    return pl.pallas_call(
        body, name="dw_o_b", grid=(nb,),
        in_specs=[pl.BlockSpec((N_HEADS_B, 1, V_DIM_B, _TQ), lambda i: (0, i, 0, 0)),
                  pl.BlockSpec((_TQ, D_MODEL), lambda i: (i, 0))],
        out_specs=_full((N_HEADS_B * V_DIM_B, D_MODEL)),
        out_shape=jax.ShapeDtypeStruct((N_HEADS_B * V_DIM_B, D_MODEL), F32),
        compiler_params=_params(("arbitrary",)),
    )(out_bt, dob)


def _mla_bwd(q, k, qt, kt, vt, d_out, d_out_t, lse, delta, gp):
    t = q.shape[0]
    nb = t // _TQ

    def body(k_ref, kt_ref, vt_ref, q_ref, qt_ref, do_ref, dot_ref, lrow_ref, drow_ref, gp_ref,
             dq_ref, dkt_ref, dvt_ref, land_ref, l_rep, d_rep, send_sems, recv_sems):
        step = pl.program_id(1)
        kj = nb - 1 - step

        @pl.when((pl.program_id(0) == 0) & (step == 0))
        def _():
            _scatter_start(gp_ref, land_ref, send_sems, recv_sems)

        @pl.when(step == 0)
        def _():
            dq_ref[...] = jnp.zeros(dq_ref.shape, F32)
            for b in range(nb):
                l_rep[_TQ * b:_TQ * (b + 1), :] = jnp.broadcast_to(lrow_ref[0, b], (LANES, _TQ)).T
                d_rep[_TQ * b:_TQ * (b + 1), :] = jnp.broadcast_to(drow_ref[0, b], (LANES, _TQ)).T

        kv, k_t, v_t = k_ref[...], kt_ref[0, 0], vt_ref[0, 0]

        def rows_of(qi):
            return pl.ds(pl.multiple_of(qi * _TQ, _TQ), _TQ)

        def products(qi, diagonal=False):
            s = _dot(q_ref[rows_of(qi), :], k_t) * _MLA_SCALE2
            if diagonal:
                qry = lax.broadcasted_iota(jnp.int32, s.shape, 0)
                key = lax.broadcasted_iota(jnp.int32, s.shape, 1)
                s = jnp.where(key <= qry, s, NEG)
            return s, _dot(do_ref[0, rows_of(qi), :], v_t)

        def update(carry, prods, qi):
            dkt, dvt = carry
            s, dp = prods
            lse, delta = l_rep[rows_of(qi), :], d_rep[rows_of(qi), :]
            ps, dss = [], []
            for c in range(_TQ // LANES):
                strip = slice(LANES * c, LANES * (c + 1))
                p = jnp.exp2(s[:, strip] - lse)
                ps.append(p.astype(BF16))
                dss.append((p * (dp[:, strip] - delta) * _MLA_SCALE).astype(BF16))
            p_b, ds_b = jnp.concatenate(ps, axis=1), jnp.concatenate(dss, axis=1)
            dvt = dvt + _dot(dot_ref[0, qi], p_b)
            dkt = dkt + _dot(qt_ref[0, qi], ds_b)
            dq_ref[rows_of(qi), :] += _dot(ds_b, kv)
            return dkt, dvt

        def pair(i, carry):
            qa = kj + 1 + 2 * i
            pa, pb = products(qa), products(qa + 1)
            return update(update(carry, pa, qa), pb, qa + 1)

        init = (jnp.zeros((HEAD_PAD, _TQ), F32), jnp.zeros((V_DIM_B, _TQ), F32))
        carry = update(init, products(kj, True), kj)
        pairs = (nb - 1 - kj) // 2
        carry = lax.fori_loop(0, pairs, pair, carry)
        dkt, dvt = lax.fori_loop(kj + 1 + 2 * pairs, nb, lambda qi, cr: update(cr, products(qi), qi), carry)
        dkt_ref[0, 0] = dkt
        dvt_ref[0, 0] = dvt

        @pl.when((pl.program_id(0) == N_HEADS_B - 1) & (step == nb - 1))
        def _():
            _scatter_wait(gp_ref, land_ref, send_sems, recv_sems)

    head4 = lambda d: pl.BlockSpec((1, nb, d, _TQ), lambda h, s: (h, 0, 0, 0))
    blk4 = lambda d: pl.BlockSpec((1, 1, d, _TQ), lambda h, s: (h, nb - 1 - s, 0, 0))
    head3 = lambda d: pl.BlockSpec((1, t, d), lambda h, kj: (h, 0, 0))
    per_head = pl.BlockSpec((t, HEAD_PAD), lambda h, kj: (0, h))
    return pl.pallas_call(
        body, name="mla_bwd", grid=(N_HEADS_B, nb),
        in_specs=[pl.BlockSpec((_TQ, HEAD_PAD), lambda h, s: (nb - 1 - s, h)), blk4(HEAD_PAD), blk4(V_DIM_B),
                  per_head, head4(HEAD_PAD), head3(V_DIM_B), head4(V_DIM_B), head4(1), head4(1), _HBM],
        out_specs=[per_head, blk4(HEAD_PAD), blk4(V_DIM_B), _HBM],
        out_shape=[jax.ShapeDtypeStruct((t, MLA_W), F32), jax.ShapeDtypeStruct((N_HEADS_B, nb, HEAD_PAD, _TQ), F32),
                   jax.ShapeDtypeStruct((N_HEADS_B, nb, V_DIM_B, _TQ), F32),
                   jax.ShapeDtypeStruct((3,) + gp.shape[1:], gp.dtype)],
        scratch_shapes=[pltpu.VMEM((t, LANES), F32), pltpu.VMEM((t, LANES), F32),
                        pltpu.SemaphoreType.DMA((3,)), pltpu.SemaphoreType.DMA((3,))],
        compiler_params=_params(("arbitrary", "arbitrary")),
    )(k, kt, vt, q, qt, d_out, d_out_t, lse, delta, gp)


def _mla_prep_bwd(dq, dkt, dvt, proj, posc, freq, qan, kvan, wq, wk, wv, swap_src):
    t = dq.shape[0]
    tm = _TQ

    def body(dq_ref, dkt_ref, dvt_ref, cq_ref, ckv_ref, pos_ref, f_ref, qan_ref, kvan_ref, wq_ref, wk_ref, wv_ref, src_ref,
             dcq_ref, dckv_ref, dkr_ref, dwq_ref, dwk_ref, dwv_ref, dqan_ref, dkvan_ref, got_ref, send_sem, recv_sem):
        swap = _sibling_copy(src_ref, got_ref, send_sem, recv_sem)

        @pl.when(pl.program_id(0) == 0)
        def _():
            swap.start()
            for r in (dwq_ref, dwk_ref, dwv_ref, dqan_ref, dkvan_ref):
                r[...] = jnp.zeros(r.shape, F32)

        cq = cq_ref[...]
        rq = _rms(cq)
        nq_ = cq * rq
        cqn = (nq_ * qan_ref[...]).astype(BF16)
        ckv = ckv_ref[...]
        rkv = _rms(ckv)
        nkv = ckv * rkv
        ckvn = (nkv * kvan_ref[...]).astype(BF16)
        c, s, lo, hi = _rope_coeffs(pos_ref[...], f_ref[...])
        dkr = jnp.zeros((tm, LANES), F32)
        dqb, dkb = [], []
        for h in range(N_HEADS_B):
            dqb.append(_unrope(dq_ref[:, HEAD_PAD * h:HEAD_PAD * (h + 1)], c, s, lo, hi).astype(BF16))
            dk_h = dkt_ref[h, 0].T
            dkr = dkr + dk_h
            dkb.append(dk_h.astype(BF16))
        dqb, dkb = jnp.concatenate(dqb, axis=1), jnp.concatenate(dkb, axis=1)
        dkr_ref[...] = jnp.where(lo | hi, _unrope(dkr, c, s, lo, hi), 0.0).astype(BF16)
        dvb = dvt_ref[...].reshape(N_HEADS_B * V_DIM_B, tm).T.astype(BF16)
        dwq_ref[...] += _dot_tn(cqn, dqb)
        dwk_ref[...] += _dot_tn(ckvn, dkb)
        dwv_ref[...] += _dot_tn(ckvn, dvb)
        dcqn = _dot_nt(dqb, wq_ref[...])
        dckvn = _dot_nt(dkb, wk_ref[...]) + _dot_nt(dvb, wv_ref[...])
        dcq, dqan = _norm_bwd(dcqn, nq_, rq, qan_ref[...])
        dckv, dkvan = _norm_bwd(dckvn, nkv, rkv, kvan_ref[...])
        dcq_ref[...] = dcq.astype(BF16)
        dckv_ref[...] = dckv.astype(BF16)
        dqan_ref[...] += dqan
        dkvan_ref[...] += dkvan

        @pl.when(pl.program_id(0) == t // tm - 1)
        def _():
            swap.wait_recv()
            swap.wait_send()

    row = lambda i: (i, 0)
    vw = N_HEADS_B * V_DIM_B
    return pl.pallas_call(
        body, name="mla_prep_bwd", grid=(t // tm,),
        in_specs=[pl.BlockSpec((tm, MLA_W), row), pl.BlockSpec((N_HEADS_B, 1, HEAD_PAD, tm), lambda i: (0, i, 0, 0)),
                  pl.BlockSpec((N_HEADS_B, 1, V_DIM_B, tm), lambda i: (0, i, 0, 0)),
                  pl.BlockSpec((tm, Q_LORA), lambda i: (i, _CQ_BLK)),
                  pl.BlockSpec((tm, LANES), lambda i: (i, _CKV_BLK)),
                  pl.BlockSpec((tm, 1), row), _full((1, LANES)), _full((1, Q_LORA)), _full((1, KV_LORA)),
                  _full((Q_LORA, MLA_W)), _full((KV_LORA, MLA_W)), _full((KV_LORA, vw)), _HBM],
        out_specs=[pl.BlockSpec((tm, Q_LORA), row), pl.BlockSpec((tm, LANES), row), pl.BlockSpec((tm, LANES), row),
                   _full((Q_LORA, MLA_W)), _full((KV_LORA, MLA_W)), _full((KV_LORA, vw)),
                   _full((1, Q_LORA)), _full((1, KV_LORA)), _HBM],
        out_shape=[jax.ShapeDtypeStruct((t, Q_LORA), BF16), jax.ShapeDtypeStruct((t, LANES), BF16),
                   jax.ShapeDtypeStruct((t, LANES), BF16),
                   jax.ShapeDtypeStruct((Q_LORA, MLA_W), F32), jax.ShapeDtypeStruct((KV_LORA, MLA_W), F32),
                   jax.ShapeDtypeStruct((KV_LORA, vw), F32),
                   jax.ShapeDtypeStruct((1, Q_LORA), F32), jax.ShapeDtypeStruct((1, KV_LORA), F32),
                   jax.ShapeDtypeStruct(swap_src.shape, swap_src.dtype)],
        scratch_shapes=[pltpu.SemaphoreType.DMA(()), pltpu.SemaphoreType.DMA(())],
        compiler_params=_params(("arbitrary",)),
    )(dq, dkt, dvt, proj, proj, posc, freq, qan, kvan, wq, wk, wv, swap_src)


def _swa_bwd(proj, d_out, lse, delta, posc, posr, sinks):
    t = proj.shape[0]
    nb = t // BLOCK

    def body(q_ref, kc_ref, kp_ref, vc_ref, vp_ref, do_ref, l_ref, d_ref, pq_ref, pc_ref, pp_ref, sink_ref,
             dq_ref, dk_ref, dv_ref, ds_ref, dkb_s, dvb_s, dk_carry, dv_carry):
        n = pl.program_id(0)

        @pl.when(n == 0)
        def _():
            ds_ref[...] = jnp.zeros(ds_ref.shape, F32)
            dk_carry[...] = jnp.zeros(dk_carry.shape, F32)
            dv_carry[...] = jnp.zeros(dv_carry.shape, F32)

        @pl.when(n < nb)
        def _():
            kb, vb, dist, valid = _swa_band(n, kp_ref, kc_ref, vp_ref, vc_ref, pq_ref, pp_ref, pc_ref)
            qv, dov = q_ref[...], do_ref[...]
            q_t, do_t, kb_t = qv.T, dov.T, kb.T
            lane = lax.broadcasted_iota(jnp.int32, (1, LANES), 1)
            dsink = jnp.zeros((1, LANES), F32)
            dq_t = []
            for kh in range(N_KV_A):
                heads = range(_GROUP_A * kh, _GROUP_A * (kh + 1))
                st_g = _dot(_head_cols(kb, kh).astype(BF16), _group_t(q_t, kh))
                dpt_g = _dot(_head_cols(vb, kh).astype(BF16), _group_t(do_t, kh))
                pts, dsts = [], []
                for j, h in enumerate(heads):
                    st = _swa_scores_t(st_g, j, h, dist, valid)
                    l_h, d_h = l_ref[h:h + 1, :], d_ref[h:h + 1, :]
                    pt = jnp.exp2(st - l_h)
                    p_sink = jnp.exp2(sink_ref[0:1, h:h + 1] * _LOG2E - l_h)
                    dsink = jnp.where(lane == h, jnp.sum(-p_sink * d_h, axis=1, keepdims=True), dsink)
                    dst = pt * (dpt_g[:, BLOCK * j:BLOCK * (j + 1)] - d_h) * _SWA_SCALE
                    pts.append(pt.astype(BF16))
                    dsts.append(dst.astype(BF16))
                pt_g, dst_g = jnp.concatenate(pts, axis=1), jnp.concatenate(dsts, axis=1)
                q_g = jnp.concatenate([_head_cols(qv, h) for h in heads], axis=0).astype(BF16)
                do_g = jnp.concatenate([_head_cols(dov, h) for h in heads], axis=0).astype(BF16)
                dkb_s[:, HEAD_DIM_A * kh:HEAD_DIM_A * (kh + 1)] = _dot(dst_g, q_g)
                dvb_s[:, HEAD_DIM_A * kh:HEAD_DIM_A * (kh + 1)] = _dot(pt_g, do_g)
                dq_g = _dot(_head_rows(kb_t, kh).astype(BF16), dst_g)
                dq_t.extend(dq_g[:, BLOCK * j:BLOCK * (j + 1)] for j in range(_GROUP_A))
            dq_ref[...] = jnp.concatenate(dq_t, axis=0).T
            ds_ref[...] += dsink
            dk_ref[...] = dk_carry[...] + dkb_s[0:BLOCK, :]
            dv_ref[...] = dv_carry[...] + dvb_s[0:BLOCK, :]
            dk_carry[...] = dkb_s[BLOCK:2 * BLOCK, :]
            dv_carry[...] = dvb_s[BLOCK:2 * BLOCK, :]

        @pl.when(n == nb)
        def _():
            dk_ref[...] = dk_carry[...]
            dv_ref[...] = dv_carry[...]

    cur = lambda n: (jnp.minimum(n, nb - 1), 0)
    cur_t = lambda n: (0, jnp.minimum(n, nb - 1))
    prv = lambda n: jnp.maximum(jnp.minimum(n, nb - 1) - 1, 0)
    out_prev = lambda n: (jnp.maximum(n - 1, 0), 0)
    return pl.pallas_call(
        body, name="swa_bwd", grid=(nb + 1,),
        in_specs=[pl.BlockSpec((BLOCK, WIDTH_A), lambda n: (jnp.minimum(n, nb - 1), _QA_BLK)),
                  pl.BlockSpec((BLOCK, LANES), lambda n: (jnp.minimum(n, nb - 1), _KA_BLK)),
                  pl.BlockSpec((BLOCK, LANES), lambda n: (prv(n), _KA_BLK)),
                  pl.BlockSpec((BLOCK, LANES), lambda n: (jnp.minimum(n, nb - 1), _VA_BLK)),
                  pl.BlockSpec((BLOCK, LANES), lambda n: (prv(n), _VA_BLK)),
                  pl.BlockSpec((BLOCK, WIDTH_A), cur), pl.BlockSpec((N_HEADS_A, BLOCK), cur_t),
                  pl.BlockSpec((N_HEADS_A, BLOCK), cur_t), pl.BlockSpec((1, BLOCK), cur_t),
                  pl.BlockSpec((BLOCK, 1), cur), pl.BlockSpec((BLOCK, 1), lambda n: (prv(n), 0)),
                  _full((1, N_HEADS_A))],
        out_specs=[pl.BlockSpec((BLOCK, WIDTH_A), cur), pl.BlockSpec((BLOCK, LANES), out_prev),
                   pl.BlockSpec((BLOCK, LANES), out_prev), _full((1, LANES))],
        out_shape=[jax.ShapeDtypeStruct((t, WIDTH_A), F32), jax.ShapeDtypeStruct((t, LANES), F32),
                   jax.ShapeDtypeStruct((t, LANES), F32), jax.ShapeDtypeStruct((1, LANES), F32)],
        scratch_shapes=[pltpu.VMEM((2 * BLOCK, LANES), F32), pltpu.VMEM((2 * BLOCK, LANES), F32),
                        pltpu.VMEM((BLOCK, LANES), F32), pltpu.VMEM((BLOCK, LANES), F32)],
        compiler_params=_params(("arbitrary",)),
    )(proj, proj, proj, proj, proj, d_out, lse, delta, posr, posc, posc, sinks)


def _in_bwd(dproj, w_in_t, x, dx1, g1, gp):
    t = x.shape[0]
    tm = 256
    steps = t // tm

    def body(dp_ref, w_ref, x_ref, dx1_ref, g_ref, gp_ref, dx_ref, dg_ref, land_ref, send_sems, recv_sems):
        i = pl.program_id(0)

        @pl.when(i == 0)
        def _():
            dg_ref[...] = jnp.zeros(dg_ref.shape, F32)
            _scatter_start(gp_ref, land_ref, send_sems, recv_sems)

        dh = _dot(dp_ref[...], w_ref[...])
        xv = x_ref[...]
        r = _rms(xv)
        dx, dg = _norm_bwd(dh, xv * r, r, g_ref[...])
        dx_ref[...] = dx1_ref[...] + dx
        dg_ref[...] += dg

        @pl.when(i == steps - 1)
        def _():
            _scatter_wait(gp_ref, land_ref, send_sems, recv_sems)

    row = lambda i: (i, 0)
    blk = pl.BlockSpec((tm, D_MODEL), row)
    return pl.pallas_call(
        body, name="in_bwd", grid=(steps,),
        in_specs=[pl.BlockSpec((tm, D_IN_PAD), row), _full((D_IN_PAD, D_MODEL)), blk, blk, _full((1, D_MODEL)), _HBM],
        out_specs=[blk, _full((1, D_MODEL)), _HBM],
        out_shape=[jax.ShapeDtypeStruct((t, D_MODEL), F32), jax.ShapeDtypeStruct((1, D_MODEL), F32),
                   jax.ShapeDtypeStruct((3,) + gp.shape[1:], gp.dtype)],
        scratch_shapes=[pltpu.SemaphoreType.DMA((3,)), pltpu.SemaphoreType.DMA((3,))],
        compiler_params=_params(("arbitrary",)),
    )(dproj, w_in_t, x, dx1, g1, gp)


def _adamw_store(w, g, m, v, out_refs):
    g_out, d_out, m_out, v_out = out_refs
    m_new = ADAM_B1 * m + (1.0 - ADAM_B1) * g
    v_new = ADAM_B2 * v + (1.0 - ADAM_B2) * jnp.square(g)
    m_hat = m_new / (1.0 - ADAM_B1 ** ADAM_STEP)
    v_hat = v_new / (1.0 - ADAM_B2 ** ADAM_STEP)
    g_out[...] = g
    d_out[...] = -ADAM_LR * (m_hat / (jnp.sqrt(v_hat) + ADAM_EPS) + ADAM_WD * w)
    m_out[...] = m_new
    v_out[...] = v_new


_SMALL_SLOTS = {"pre_norm_mix": (0, 0, D_MODEL), "post_norm_mix": (1, 0, D_MODEL), "pre_norm_mlp": (2, 0, D_MODEL),
                "post_norm_mlp": (3, 0, D_MODEL), "q_a_norm": (4, 0, Q_LORA), "kv_a_norm": (4, Q_LORA, KV_LORA),
                "sinks": (4, Q_LORA + KV_LORA, N_HEADS_A)}
_LOSS_ROW = 5


def _adamw_small(red, w, m, v):
    names = tuple(_SMALL_SLOTS)
    n = len(names)

    def body(*refs):
        red_ref, ws, ms, vs, outs = refs[0], refs[1:1 + n], refs[1 + n:1 + 2 * n], refs[1 + 2 * n:1 + 3 * n], refs[1 + 3 * n:]
        for k, name in enumerate(names):
            row, lane, width = _SMALL_SLOTS[name]
            g = red_ref[row:row + 1, lane:lane + width]
            _adamw_store(ws[k][...], g, ms[k][...], vs[k][...], outs[4 * k:4 * k + 4])

    vmem = pl.BlockSpec(memory_space=pltpu.VMEM)
    res = pl.pallas_call(
        body, name="adamw_small", in_specs=[vmem] * (1 + 3 * n), out_specs=[vmem] * (4 * n),
        out_shape=[jax.ShapeDtypeStruct(w[name].shape, F32) for name in names for _ in range(4)],
    )(red, *[w[k] for k in names], *[m[k] for k in names], *[v[k] for k in names])
    return {name: res[4 * k:4 * k + 4] for k, name in enumerate(names)}


def _adamw(w, g_parts, m, v, name, block, g_row_off=0):
    r, c = w.shape
    br, bc = block
    ng = len(g_parts)

    def body(*refs):
        w_ref, g_refs, m_ref, v_ref = refs[0], refs[1:1 + ng], refs[1 + ng], refs[2 + ng]
        g = g_refs[0][...]
        for gr in g_refs[1:]:
            g = g + gr[...]
        _adamw_store(w_ref[...], g, m_ref[...], v_ref[...], refs[3 + ng:])

    assert g_row_off % br == 0 and r % br == 0 and c % bc == 0
    blk = pl.BlockSpec(block, lambda i, j: (i, j))
    g_blk = pl.BlockSpec(block, lambda i, j: (i + g_row_off // br, j))
    return pl.pallas_call(
        body, name=name, grid=(r // br, c // bc),
        in_specs=[blk] + [g_blk] * ng + [blk, blk], out_specs=[blk] * 4,
        out_shape=[jax.ShapeDtypeStruct((r, c), F32)] * 4,
        compiler_params=_params(("parallel", "parallel")),
    )(w, *g_parts, m, v)


_HBM = pl.BlockSpec(memory_space=pltpu.HBM)


def _other_chips(x, y):
    return ((1 - x, y), (x, 1 - y), (1 - x, 1 - y))


def _gather_copies(src, out, send_sems, recv_sems, local_sem):
    x, y, c = lax.axis_index("x"), lax.axis_index("y"), lax.axis_index("c")
    me = 2 * x + y
    local = pltpu.make_async_copy(src, out.at[me], local_sem)

    def copies(arriving):
        return [pltpu.make_async_remote_copy(src_ref=src, dst_ref=out.at[2 * px + py if arriving else me],
                                             send_sem=send_sems.at[j], recv_sem=recv_sems.at[j], device_id=(px, py, c),
                                             device_id_type=MESH)
                for j, (px, py) in enumerate(_other_chips(x, y))]

    return local, copies


def _gather_start(src, out, send_sems, recv_sems, local_sem):
    local, copies = _gather_copies(src, out, send_sems, recv_sems, local_sem)
    local.start()
    for cp in copies(False):
        cp.start()


def _gather_wait(src, out, send_sems, recv_sems, local_sem):
    local, copies = _gather_copies(src, out, send_sems, recv_sems, local_sem)
    for cp in copies(True):
        cp.wait_recv()
    for cp in copies(False):
        cp.wait_send()
    local.wait()


def _scatter_copies(src, land, send_sems, recv_sems):
    x, y, c = lax.axis_index("x"), lax.axis_index("y"), lax.axis_index("c")
    return [pltpu.make_async_remote_copy(src_ref=src.at[2 * px + py], dst_ref=land.at[j], send_sem=send_sems.at[j],
                                         recv_sem=recv_sems.at[j], device_id=(px, py, c), device_id_type=MESH)
            for j, (px, py) in enumerate(_other_chips(x, y))]


def _scatter_start(src, land, send_sems, recv_sems):
    for cp in _scatter_copies(src, land, send_sems, recv_sems):
        cp.start()


def _scatter_wait(src, land, send_sems, recv_sems):
    copies = _scatter_copies(src, land, send_sems, recv_sems)
    for cp in copies:
        cp.wait_recv()
    for cp in copies:
        cp.wait_send()


def _all_gather_chips(packed):
    r = packed.shape[0]
    half = r // 2

    def body(src, out, ici_send, ici_recv, d2d_send, d2d_recv, local_sem):
        x, y, c = lax.axis_index("x"), lax.axis_index("y"), lax.axis_index("c")
        me = 2 * x + y
        mine = pl.ds(pl.multiple_of(c * half, 16), half)
        theirs = pl.ds(pl.multiple_of((1 - c) * half, 16), half)
        chips = _other_chips(x, y)
        local = pltpu.make_async_copy(src, out.at[me], local_sem)
        local.start()
        sends = [pltpu.make_async_remote_copy(src_ref=src.at[mine], dst_ref=out.at[me, mine], send_sem=ici_send.at[j],
                                              recv_sem=ici_recv.at[j], device_id=(px, py, c), device_id_type=MESH)
                 for j, (px, py) in enumerate(chips)]
        for cp in sends:
            cp.start()
        passed = []
        for j, (px, py) in enumerate(chips):
            block = 2 * px + py
            pltpu.make_async_remote_copy(src_ref=src.at[mine], dst_ref=out.at[block, mine], send_sem=ici_send.at[j],
                                         recv_sem=ici_recv.at[j], device_id=(px, py, c), device_id_type=MESH).wait_recv()
            cp = pltpu.make_async_remote_copy(src_ref=out.at[block, mine], dst_ref=out.at[block, mine],
                                              send_sem=d2d_send.at[j], recv_sem=d2d_recv.at[j],
                                              device_id=(x, y, 1 - c), device_id_type=MESH)
            cp.start()
            passed.append(cp)
        for j, (px, py) in enumerate(chips):
            block = 2 * px + py
            pltpu.make_async_remote_copy(src_ref=out.at[block, theirs], dst_ref=out.at[block, theirs],
                                         send_sem=d2d_send.at[j], recv_sem=d2d_recv.at[j],
                                         device_id=(x, y, 1 - c), device_id_type=MESH).wait_recv()
        for cp in sends + passed:
            cp.wait_send()
        local.wait()

    sems = pltpu.SemaphoreType.DMA((3,))
    return pl.pallas_call(
        body, name="ag_weights", in_specs=[_HBM], out_specs=_HBM,
        out_shape=jax.ShapeDtypeStruct((N_CHIPS,) + packed.shape, packed.dtype),
        scratch_shapes=[sems, sems, sems, sems, pltpu.SemaphoreType.DMA(())],
    )(packed)


def _sum4(gp, land, chip, name):
    _, r, w = gp.shape
    tr = 128

    def body(chip_ref, o_ref, l_ref, s_ref):
        s_ref[...] = ((o_ref[0] + l_ref[0].astype(F32)) + l_ref[1].astype(F32)) + l_ref[2].astype(F32)

    return pl.pallas_call(
        body, name=name,
        grid_spec=pltpu.PrefetchScalarGridSpec(
            num_scalar_prefetch=1, grid=(r // tr,),
            in_specs=[pl.BlockSpec((1, tr, w), lambda i, chip_ref: (chip_ref[0], i, 0)),
                      pl.BlockSpec((3, tr, w), lambda i, chip_ref: (0, i, 0))],
            out_specs=pl.BlockSpec((tr, w), lambda i, chip_ref: (i, 0))),
        out_shape=jax.ShapeDtypeStruct((r, w), F32),
        compiler_params=_params(("parallel",)),
    )(chip, gp, land)


def _sibling_copy(src, got, send_sem, recv_sem):
    x, y, c = lax.axis_index("x"), lax.axis_index("y"), lax.axis_index("c")
    return pltpu.make_async_remote_copy(src_ref=src, dst_ref=got, send_sem=send_sem, recv_sem=recv_sem,
                                        device_id=(x, y, 1 - c), device_id_type=MESH)


def _swap_sibling(s, name):
    def body(src, got, send_sem, recv_sem):
        cp = _sibling_copy(src, got, send_sem, recv_sem)
        cp.start()
        cp.wait_recv()
        cp.wait_send()

    return pl.pallas_call(
        body, name=name, in_specs=[_HBM], out_specs=_HBM,
        out_shape=jax.ShapeDtypeStruct(s.shape, s.dtype),
        scratch_shapes=[pltpu.SemaphoreType.DMA(()), pltpu.SemaphoreType.DMA(())],
    )(s)


def _all_reduce_small(dsmall, loss):
    n_dev = 8
    names = tuple(_SMALL_SLOTS)
    shape = (8, D_MODEL)

    def body(*refs):
        parts, loss_ref = refs[:len(names)], refs[len(names)]
        out, src, gath, send_sems, recv_sems = refs[len(names) + 1:]
        x, y, c = lax.axis_index("x"), lax.axis_index("y"), lax.axis_index("c")
        me = 4 * x + 2 * y + c
        src[...] = jnp.zeros(shape, F32)
        for name, part in zip(names, parts):
            row, lane, _ = _SMALL_SLOTS[name]
            src[row:row + 1, lane:lane + part.shape[1]] = part[...]
        src[_LOSS_ROW:_LOSS_ROW + 1, 0:LANES] = loss_ref[...]
        gath[me] = src[...]
        peers = []
        for k in range(1, n_dev):
            px = 1 - x if (k >> 2) & 1 else x
            py = 1 - y if (k >> 1) & 1 else y
            pc = 1 - c if k & 1 else c
            peers.append((px, py, pc))
        sends = []
        for j, peer in enumerate(peers):
            cp = pltpu.make_async_remote_copy(src_ref=src, dst_ref=gath.at[me], send_sem=send_sems.at[j],
                                              recv_sem=recv_sems.at[j], device_id=peer, device_id_type=MESH)
            cp.start()
            sends.append(cp)
        for j, (px, py, pc) in enumerate(peers):
            pltpu.make_async_remote_copy(src_ref=src, dst_ref=gath.at[4 * px + 2 * py + pc], send_sem=send_sems.at[j],
                                         recv_sem=recv_sems.at[j], device_id=(px, py, pc), device_id_type=MESH).wait_recv()
        for cp in sends:
            cp.wait_send()
        acc = gath[0]
        for d in range(1, n_dev):
            acc = acc + gath[d]
        out[...] = acc

    vmem = pl.BlockSpec(memory_space=pltpu.VMEM)
    return pl.pallas_call(
        body, name="ar_small", in_specs=[vmem] * (len(names) + 1), out_specs=vmem,
        out_shape=jax.ShapeDtypeStruct(shape, F32),
        scratch_shapes=[pltpu.VMEM(shape, F32), pltpu.VMEM((n_dev,) + shape, F32),
                        pltpu.SemaphoreType.DMA((n_dev - 1,)), pltpu.SemaphoreType.DMA((n_dev - 1,))],
    )(*[dsmall[k] for k in names], loss)


_W_IN_ROWS = SHARD_SHAPES["w_in"][1]
_KR_ROW = 3200
_KR_PAD_ROW = _KR_BLK * LANES + QK_NOPE


def _shard_rows(name, a):
    return jnp.transpose(a) if name == "w_in" else a.reshape(PACK_ROWS[name], D_MODEL)


def _pack(group, shards, dtype):
    parts = [_shard_rows(n, shards[n]).astype(dtype) for n in group]
    pad = -sum(PACK_ROWS[n] for n in group) % LANES
    if pad:
        parts.append(jnp.zeros((pad, D_MODEL), dtype))
    return jnp.concatenate(parts, axis=0)


def _col_sharded_full(g, name, group):
    r, c = SHARD_SHAPES[name]
    off = _row_offset(group, name)
    blocks = g[:, off:off + PACK_ROWS[name]].reshape(N_CHIPS, r, c)
    return jnp.transpose(blocks, (1, 0, 2)).reshape(r, N_CHIPS * c)


def _col_sharded_blocks(d, name):
    r, c = SHARD_SHAPES[name]
    return jnp.transpose(d.reshape(r, N_CHIPS, c), (1, 0, 2)).reshape(N_CHIPS, PACK_ROWS[name], D_MODEL)


def _weights_a(g):
    dt = g.dtype
    w_in_t = g[:, :_W_IN_ROWS].reshape(N_CHIPS * _W_IN_ROWS, D_MODEL)
    z = lambda n: jnp.zeros((n, D_MODEL), dt)
    w_in_t = jnp.concatenate([w_in_t[:_KR_ROW], z(_KR_PAD_ROW - _KR_ROW), w_in_t[_KR_ROW:],
                              z(D_IN_PAD - _KR_PAD_ROW - QK_ROPE)], axis=0)
    wq = _col_sharded_full(g, "w_q_b", GROUP_A).reshape(Q_LORA, N_HEADS_B, Q_HEAD_B)
    wq_p = jnp.concatenate([wq, jnp.zeros((Q_LORA, N_HEADS_B, HEAD_PAD - Q_HEAD_B), dt)], axis=2).reshape(Q_LORA, MLA_W)
    wkv = _col_sharded_full(g, "w_kv_b", GROUP_A).reshape(KV_LORA, N_HEADS_B, QK_NOPE + V_DIM_B)
    zk = jnp.zeros((KV_LORA, N_HEADS_B, HEAD_PAD - QK_NOPE), dt)
    wk_p = jnp.concatenate([wkv[:, :, :QK_NOPE], zk], axis=2).reshape(KV_LORA, MLA_W)
    wv = wkv[:, :, QK_NOPE:].reshape(KV_LORA, N_HEADS_B * V_DIM_B)
    return dict(w_in=w_in_t, wq=wq_p, wk=wk_p, wv=wv, wv_t=jnp.transpose(wv))


def _grad_blocks_a(dw_in_t, dwq_p, dwk_p, dwv):
    dw_in = jnp.concatenate([dw_in_t[:_KR_ROW], dw_in_t[_KR_PAD_ROW:_KR_PAD_ROW + QK_ROPE]], axis=0)
    dwq = dwq_p.reshape(Q_LORA, N_HEADS_B, HEAD_PAD)[:, :, :Q_HEAD_B].reshape(Q_LORA, N_HEADS_B * Q_HEAD_B)
    dwk = dwk_p.reshape(KV_LORA, N_HEADS_B, HEAD_PAD)[:, :, :QK_NOPE]
    dwkv = jnp.concatenate([dwk, dwv.reshape(KV_LORA, N_HEADS_B, V_DIM_B)], axis=2)
    dwkv = dwkv.reshape(KV_LORA, N_HEADS_B * (QK_NOPE + V_DIM_B))
    pad = -sum(PACK_ROWS[n] for n in GROUP_A) % LANES
    return jnp.concatenate([dw_in.reshape(N_CHIPS, _W_IN_ROWS, D_MODEL), _col_sharded_blocks(dwq, "w_q_b"),
                            _col_sharded_blocks(dwkv, "w_kv_b"), jnp.zeros((N_CHIPS, pad, D_MODEL), F32)], axis=1)


def _rope_freq_lanes():
    freqs = ROPE_THETA ** (-jnp.arange(0, QK_ROPE, 2, dtype=F32) / QK_ROPE)
    return jnp.concatenate([jnp.zeros((QK_NOPE,), F32), freqs, freqs,
                            jnp.zeros((HEAD_PAD - Q_HEAD_B,), F32)]).reshape(1, LANES)


def _fwd_bwd(x, positions, target, w):
    t = x.shape[0]
    wa = _weights_a(_all_gather_chips(_pack(GROUP_A, w, BF16)))
    posr = positions.astype(F32).reshape(1, t)
    posc = posr.reshape(t, 1)
    freq = _rope_freq_lanes()
    g1, g2, g3, g4 = w["pre_norm_mix"], w["post_norm_mix"], w["pre_norm_mlp"], w["post_norm_mlp"]
    qan, kvan, sinks = w["q_a_norm"], w["kv_a_norm"], w["sinks"]

    h, proj = _proj_fwd(x, g1, wa["w_in"])
    out_a, lse_a = _swa_fwd(proj, posc, posr, sinks)
    qm, km, qt, kt, vt = _mla_prep_fwd(proj, posc, freq, qan, kvan, wa["wq"], wa["wk"], wa["wv_t"])
    out_bt, lse_b, wb = _mla_fwd(km, qt, vt, _pack(GROUP_B, w, BF16))
    w_oa, w_ob = _col_sharded_full(wb, "w_o_a", GROUP_B), _col_sharded_full(wb, "w_o_b", GROUP_B)
    merged, y, x1, h2 = _mix_out_fwd(out_a, out_bt, proj, x, w_oa, w_ob, wb, g2, g3)
    a = _up_fwd(h2, wb)
    dx2, dyd, dg4, loss = _down_fwd_loss(a, wb, x1, target, g4)

    gp_b = _dw_into_blocks(a, dyd, "w_down", 1024, 512)
    du = _down_bwd(dyd, wb, a)
    gp_b = _dw_into_blocks(h2, du, "w_up", 1024, 512, gp_b)
    dx1, dy, dg3, dg2 = _up_bwd(du, wb, x1, dx2, y, g3, g2)
    gp_b = _dw_into_blocks(merged, dy, "w_out", 1024, 512, gp_b)
    doa, dob, dga, dgb, d_out_a, d_out_b, d_out_bt, del_a, del_b = _mix_out_bwd(dy, out_a, out_bt, proj, w_oa, w_ob, wb)
    dw_oa = _matmul_tn(out_a, doa, "dw_o_a", 512, 1024)
    dw_ob = _dw_ob(out_bt, dob)
    small_b = jnp.concatenate([_col_sharded_blocks(dw_oa, "w_o_a"), _col_sharded_blocks(dw_ob, "w_o_b")], axis=1)
    gp_b = lax.dynamic_update_slice(gp_b, small_b, (0, _row_offset(GROUP_B, "w_o_a"), 0))
    dqm, dkm, dvm, land_b = _mla_bwd(qm, km, qt, kt, vt, d_out_b, d_out_bt, lse_b, del_b, gp_b)
    chip = (2 * lax.axis_index("x") + lax.axis_index("y")).astype(jnp.int32).reshape(1)
    part_b = _sum4(gp_b, land_b, chip, "rs_sum_b")
    dcq, dckv, dkr, dwq, dwk, dwv, dqan, dkvan, sib_b = _mla_prep_bwd(
        dqm, dkm, dvm, proj, posc, freq, qan, kvan, wa["wq"], wa["wk"], wa["wv"], part_b)
    dqa, dka, dva, dsinks = _swa_bwd(proj, d_out_a, lse_a, del_a, posc, posr, sinks)
    dproj = jnp.concatenate([dga, dgb, dqa.astype(BF16), dka.astype(BF16), dva.astype(BF16), dcq, dckv, dkr], axis=1)
    dw_in_t = _matmul_tn(dproj, h, "dw_in", D_IN_PAD // 2, 1024, tk=512)
    gp_a = _grad_blocks_a(dw_in_t, dwq, dwk, dwv)
    grad_x, dg1, land_a = _in_bwd(dproj, wa["w_in"], x, dx1, g1, gp_a.astype(BF16))

    part_a = _sum4(gp_a, land_a, chip, "rs_sum_a")
    reduced = {GROUP_A: [part_a, _swap_sibling(part_a, "rs_swap_a")], GROUP_B: [part_b, sib_b]}
    dsmall = dict(pre_norm_mix=dg1, post_norm_mix=dg2, pre_norm_mlp=dg3, post_norm_mlp=dg4,
                  q_a_norm=dqan, kv_a_norm=dkvan, sinks=dsinks)
    return loss, grad_x, reduced, dsmall


def kernel(x, positions, pre_norm_mix, w_in, q_a_norm, w_q_b, kv_a_norm, w_kv_b, sinks, w_o_a, w_o_b, w_out, post_norm_mix, pre_norm_mlp, w_up, w_down, post_norm_mlp, loss_target, m_pre_norm_mix, m_w_in, m_q_a_norm, m_w_q_b, m_kv_a_norm, m_w_kv_b, m_sinks, m_w_o_a, m_w_o_b, m_w_out, m_post_norm_mix, m_pre_norm_mlp, m_w_up, m_w_down, m_post_norm_mlp, v_pre_norm_mix, v_w_in, v_q_a_norm, v_w_q_b, v_kv_a_norm, v_w_kv_b, v_sinks, v_w_o_a, v_w_o_b, v_w_out, v_post_norm_mix, v_pre_norm_mlp, v_w_up, v_w_down, v_post_norm_mlp):
    w = dict(pre_norm_mix=pre_norm_mix, w_in=w_in[0], q_a_norm=q_a_norm, w_q_b=w_q_b[0], kv_a_norm=kv_a_norm,
             w_kv_b=w_kv_b[0], sinks=sinks, w_o_a=w_o_a[0], w_o_b=w_o_b[0], w_out=w_out[0],
             post_norm_mix=post_norm_mix, pre_norm_mlp=pre_norm_mlp, w_up=w_up[0], w_down=w_down[0],
             post_norm_mlp=post_norm_mlp)
    m = dict(pre_norm_mix=m_pre_norm_mix, w_in=m_w_in[0], q_a_norm=m_q_a_norm, w_q_b=m_w_q_b[0],
             kv_a_norm=m_kv_a_norm, w_kv_b=m_w_kv_b[0], sinks=m_sinks, w_o_a=m_w_o_a[0], w_o_b=m_w_o_b[0],
             w_out=m_w_out[0], post_norm_mix=m_post_norm_mix, pre_norm_mlp=m_pre_norm_mlp, w_up=m_w_up[0],
             w_down=m_w_down[0], post_norm_mlp=m_post_norm_mlp)
    v = dict(pre_norm_mix=v_pre_norm_mix, w_in=v_w_in[0], q_a_norm=v_q_a_norm, w_q_b=v_w_q_b[0],
             kv_a_norm=v_kv_a_norm, w_kv_b=v_w_kv_b[0], sinks=v_sinks, w_o_a=v_w_o_a[0], w_o_b=v_w_o_b[0],
             w_out=v_w_out[0], post_norm_mix=v_post_norm_mix, pre_norm_mlp=v_pre_norm_mlp, w_up=v_w_up[0],
             w_down=v_w_down[0], post_norm_mlp=v_post_norm_mlp)

    loss, grad_x, reduced, dsmall = _fwd_bwd(x[0], positions, loss_target[0], w)

    red = _all_reduce_small(dsmall, loss)
    small = _adamw_small(red, w, m, v)

    big = {}
    tr = jnp.transpose
    big["w_in"] = [tr(o)[None] for o in _adamw(tr(w["w_in"]), reduced[GROUP_A], tr(m["w_in"]), tr(v["w_in"]),
                                               "adamw_w_in", (_W_IN_ROWS, 256))]
    for n in ("w_up", "w_down", "w_out"):
        big[n] = [o[None] for o in _adamw(w[n], reduced[GROUP_B], m[n], v[n], "adamw_" + n, (128, D_MODEL),
                                          _row_offset(GROUP_B, n))]
    for group, names in ((GROUP_A, ("w_q_b", "w_kv_b")), (GROUP_B, ("w_o_a", "w_o_b"))):
        for n in names:
            off = _row_offset(group, n)
            g_parts = [p[off:off + PACK_ROWS[n]].reshape(SHARD_SHAPES[n]) for p in reduced[group]]
            big[n] = [o[None] for o in _adamw(w[n], g_parts, m[n], v[n], "adamw_" + n, SHARD_SHAPES[n])]

    outs = [big[n][k] if n in big else small[n][k] for k in range(4) for n in WEIGHTS]
    return (red[_LOSS_ROW, 0], grad_x[None], *outs)
```

```python
import jax
import jax.numpy as jnp
from jax import lax
from jax.experimental import pallas as pl
from jax.experimental.pallas import tpu as pltpu

F32 = jnp.float32
BF16 = jnp.bfloat16
MESH = pl.DeviceIdType.MESH

D_MODEL = 1024
N_HEADS_A = 8
N_KV_A = 2
HEAD_DIM_A = 64
WINDOW = 128
BLOCK = 128
N_HEADS_B = 8
QK_NOPE = 64
QK_ROPE = 32
V_DIM_B = 64
Q_LORA = 256
KV_LORA = 128
ROPE_THETA = 10000.0
D_FF = 4 * D_MODEL
EPS = 1e-6
WIDTH_A = N_HEADS_A * HEAD_DIM_A
Q_HEAD_B = QK_NOPE + QK_ROPE
D_IN_PAD = 3328
HEAD_PAD = 128
MLA_W = N_HEADS_B * HEAD_PAD

ADAM_LR = 0.001
ADAM_B1 = 0.9
ADAM_B2 = 0.999
ADAM_EPS = 1e-08
ADAM_WD = 0.01
ADAM_STEP = 10

NEG = -1e30
N_CHIPS = 4
LANES = 128
VMEM_LIMIT = 56 * 1024 * 1024

SHARD_SHAPES = {"w_in": (1024, 808), "w_q_b": (256, 192), "w_kv_b": (128, 256), "w_o_a": (512, 256),
                "w_o_b": (512, 256), "w_out": (256, 1024), "w_up": (1024, 1024), "w_down": (1024, 1024)}
PACK_ROWS = {n: (s[0] * s[1]) // D_MODEL for n, s in SHARD_SHAPES.items()}
GROUP_A = ("w_in", "w_q_b", "w_kv_b")
GROUP_B = ("w_up", "w_down", "w_out", "w_o_a", "w_o_b")
WEIGHTS = ("pre_norm_mix", "w_in", "q_a_norm", "w_q_b", "kv_a_norm", "w_kv_b", "sinks", "w_o_a", "w_o_b", "w_out",
           "post_norm_mix", "pre_norm_mlp", "w_up", "w_down", "post_norm_mlp")


def _params(sem=None):
    return pltpu.CompilerParams(dimension_semantics=sem, vmem_limit_bytes=VMEM_LIMIT)


def _dot(a, b):
    return jnp.dot(a, b, preferred_element_type=F32)


def _dot_nt(a, b):
    return lax.dot_general(a, b, (((1,), (1,)), ((), ())), preferred_element_type=F32)


def _dot_tn(a, b):
    return lax.dot_general(a, b, (((0,), (0,)), ((), ())), preferred_element_type=F32)


def _rms(v):
    return lax.rsqrt(jnp.mean(v * v, axis=-1, keepdims=True) + EPS)


def _norm_bwd(dout, n, r, g):
    dn = dout * g
    dx = r * (dn - n * jnp.mean(dn * n, axis=-1, keepdims=True))
    return dx, jnp.sum(dout * n, axis=0, keepdims=True)


def _full(shape):
    return pl.BlockSpec(shape, lambda *_: (0,) * len(shape))


def _row_offset(group, name):
    return sum(PACK_ROWS[n] for n in group[:group.index(name)])


def _wb_spec(name):
    rows = PACK_ROWS[name]
    return pl.BlockSpec((N_CHIPS, rows, D_MODEL), lambda *_: (0, _row_offset(GROUP_B, name) // rows, 0))


def _proj_fwd(x, g1, w_in_t):
    t = x.shape[0]
    tm = 256

    def body(x_ref, g_ref, w_ref, h_ref, p_ref):
        xv = x_ref[...]
        h = ((xv * _rms(xv)) * g_ref[...]).astype(BF16)
        h_ref[...] = h
        p_ref[...] = _dot_nt(h, w_ref[...])

    return pl.pallas_call(
        body, name="proj_fwd", grid=(t // tm,),
        in_specs=[pl.BlockSpec((tm, D_MODEL), lambda i: (i, 0)), _full((1, D_MODEL)), _full((D_IN_PAD, D_MODEL))],
        out_specs=[pl.BlockSpec((tm, D_MODEL), lambda i: (i, 0)), pl.BlockSpec((tm, D_IN_PAD), lambda i: (i, 0))],
        out_shape=[jax.ShapeDtypeStruct((t, D_MODEL), BF16), jax.ShapeDtypeStruct((t, D_IN_PAD), F32)],
        compiler_params=_params(("parallel",)),
    )(x, g1, w_in_t)


_QA_BLK = 2048 // WIDTH_A
_KA_BLK = 2560 // LANES
_VA_BLK = 2688 // LANES
_CQ_BLK = 2816 // Q_LORA
_CKV_BLK = 3072 // LANES
_KR_BLK = 3200 // LANES


_GROUP_A = N_HEADS_A // N_KV_A
_SWA_SCALE = HEAD_DIM_A ** -0.5
_LOG2E = 1.4426950408889634


def _head_cols(v, h):
    return v[:, HEAD_DIM_A * h:HEAD_DIM_A * (h + 1)]


def _head_rows(v, h):
    return v[HEAD_DIM_A * h:HEAD_DIM_A * (h + 1), :]


def _swa_band(n, kp_ref, kc_ref, vp_ref, vc_ref, pq_ref, pp_ref, pc_ref):
    kb = jnp.concatenate([kp_ref[...], kc_ref[...]], axis=0)
    vb = jnp.concatenate([vp_ref[...], vc_ref[...]], axis=0)
    posk = jnp.concatenate([pp_ref[...], pc_ref[...]], axis=0)
    dist = jnp.abs(posk - pq_ref[...])
    ki = lax.broadcasted_iota(jnp.int32, (2 * BLOCK, BLOCK), 0)
    qi = lax.broadcasted_iota(jnp.int32, (2 * BLOCK, BLOCK), 1)
    valid = (ki > qi) & (ki <= qi + WINDOW) & ((n > 0) | (ki >= BLOCK))
    return kb, vb, dist, valid


def _swa_scores_t(st_g, j, h, dist, valid):
    slope = 2.0 ** (-8.0 * (h + 1) / N_HEADS_A)
    st = st_g[:, BLOCK * j:BLOCK * (j + 1)] * (_SWA_SCALE * _LOG2E) - (slope * _LOG2E) * dist
    return jnp.where(valid, st, NEG)


def _group_t(xt, kh):
    return jnp.concatenate([_head_rows(xt, _GROUP_A * kh + j) for j in range(_GROUP_A)], axis=1).astype(BF16)


def _swa_fwd(proj, posc, posr, sinks):
    t = proj.shape[0]
    nb = t // BLOCK

    def body(q_ref, kc_ref, kp_ref, vc_ref, vp_ref, pq_ref, pc_ref, pp_ref, sink_ref, o_ref, l_ref):
        n = pl.program_id(0)
        kb, vb, dist, valid = _swa_band(n, kp_ref, kc_ref, vp_ref, vc_ref, pq_ref, pp_ref, pc_ref)
        q_t, vb_t = q_ref[...].T, vb.T
        out_t, lse = [], []
        for kh in range(N_KV_A):
            st_g = _dot(_head_cols(kb, kh).astype(BF16), _group_t(q_t, kh))
            ps = []
            for j in range(_GROUP_A):
                h = _GROUP_A * kh + j
                st = _swa_scores_t(st_g, j, h, dist, valid)
                sink = sink_ref[0:1, h:h + 1] * _LOG2E
                m = jnp.maximum(jnp.max(st, axis=0, keepdims=True), sink)
                e = jnp.exp2(st - m)
                den = jnp.sum(e, axis=0, keepdims=True) + jnp.exp2(sink - m)
                ps.append((e * (1.0 / den)).astype(BF16))
                lse.append(m + jnp.log(den) * _LOG2E)
            o_g = _dot(_head_rows(vb_t, kh).astype(BF16), jnp.concatenate(ps, axis=1))
            out_t.extend(o_g[:, BLOCK * j:BLOCK * (j + 1)] for j in range(_GROUP_A))
        o_ref[...] = jnp.concatenate(out_t, axis=0).T
        l_ref[...] = jnp.concatenate(lse, axis=0)

    cur = lambda n: (n, 0)
    prev = lambda n: jnp.maximum(n - 1, 0)
    return pl.pallas_call(
        body, name="swa_fwd", grid=(nb,),
        in_specs=[pl.BlockSpec((BLOCK, WIDTH_A), lambda n: (n, _QA_BLK)),
                  pl.BlockSpec((BLOCK, LANES), lambda n: (n, _KA_BLK)),
                  pl.BlockSpec((BLOCK, LANES), lambda n: (prev(n), _KA_BLK)),
                  pl.BlockSpec((BLOCK, LANES), lambda n: (n, _VA_BLK)),
                  pl.BlockSpec((BLOCK, LANES), lambda n: (prev(n), _VA_BLK)),
                  pl.BlockSpec((1, BLOCK), lambda n: (0, n)),
                  pl.BlockSpec((BLOCK, 1), cur),
                  pl.BlockSpec((BLOCK, 1), lambda n: (prev(n), 0)),
                  _full((1, N_HEADS_A))],
        out_specs=[pl.BlockSpec((BLOCK, WIDTH_A), cur), pl.BlockSpec((N_HEADS_A, BLOCK), lambda n: (0, n))],
        out_shape=[jax.ShapeDtypeStruct((t, WIDTH_A), F32), jax.ShapeDtypeStruct((N_HEADS_A, t), F32)],
        compiler_params=_params(("parallel",)),
    )(proj, proj, proj, proj, proj, posr, posc, posc, sinks)


def _rope_coeffs(pos, freq):
    ang = pos * freq
    cosv, sinv = jnp.cos(ang), jnp.sin(ang)
    lane = lax.broadcasted_iota(jnp.int32, ang.shape, 1)
    lo = (lane >= QK_NOPE) & (lane < QK_NOPE + QK_ROPE // 2)
    hi = (lane >= QK_NOPE + QK_ROPE // 2) & (lane < QK_NOPE + QK_ROPE)
    c = jnp.where(lane < QK_NOPE, 1.0, jnp.where(lo | hi, cosv, 0.0))
    s = jnp.where(lo, -sinv, jnp.where(hi, sinv, 0.0))
    return c, s, lo, hi


def _rope(xh, c, s, lo):
    up = pltpu.roll(xh, LANES - QK_ROPE // 2, axis=1)
    dn = pltpu.roll(xh, QK_ROPE // 2, axis=1)
    return xh * c + jnp.where(lo, up, dn) * s


def _unrope(dh, c, s, lo, hi):
    g = dh * s
    up = pltpu.roll(g, LANES - QK_ROPE // 2, axis=1)
    dn = pltpu.roll(g, QK_ROPE // 2, axis=1)
    return dh * c + jnp.where(hi, dn, jnp.where(lo, up, 0.0))


_TQ = 512
_MLA_SCALE = Q_HEAD_B ** -0.5


def _mla_prep_fwd(proj, posc, freq, qan, kvan, wq, wk, wv):
    t = proj.shape[0]
    tm = _TQ
    nb = t // tm

    def body(cq_ref, ckv_ref, kr_ref, pos_ref, f_ref, qan_ref, kvan_ref, wq_ref, wk_ref, wv_ref,
             q_ref, k_ref, qt_ref, kt_ref, vt_ref):
        cq = cq_ref[...]
        cqn = ((cq * _rms(cq)) * qan_ref[...]).astype(BF16)
        ckv = ckv_ref[...]
        ckvn = ((ckv * _rms(ckv)) * kvan_ref[...]).astype(BF16)
        qb = _dot(cqn, wq_ref[...])
        kb = _dot(ckvn, wk_ref[...])
        vbt = _dot_nt(wv_ref[...], ckvn)
        c, s, lo, _ = _rope_coeffs(pos_ref[...], f_ref[...])
        kr = _rope(kr_ref[...], c, s, lo)
        for h in range(N_HEADS_B):
            sl = slice(HEAD_PAD * h, HEAD_PAD * (h + 1))
            q_h = _rope(qb[:, sl], c, s, lo)
            k_h = kb[:, sl] + kr
            q_ref[:, sl] = q_h.astype(BF16)
            k_ref[:, sl] = k_h.astype(BF16)
            qt_ref[h, 0] = q_h.T.astype(BF16)
            kt_ref[h, 0] = k_h.T.astype(BF16)
            vt_ref[h, 0] = vbt[V_DIM_B * h:V_DIM_B * (h + 1), :].astype(BF16)

    row = lambda i: (i, 0)
    blk4 = lambda d: pl.BlockSpec((N_HEADS_B, 1, d, tm), lambda i: (0, i, 0, 0))
    return pl.pallas_call(
        body, name="mla_prep_fwd", grid=(nb,),
        in_specs=[pl.BlockSpec((tm, Q_LORA), lambda i: (i, _CQ_BLK)),
                  pl.BlockSpec((tm, LANES), lambda i: (i, _CKV_BLK)),
                  pl.BlockSpec((tm, LANES), lambda i: (i, _KR_BLK)),
                  pl.BlockSpec((tm, 1), row), _full((1, LANES)), _full((1, Q_LORA)), _full((1, KV_LORA)),
                  _full((Q_LORA, MLA_W)), _full((KV_LORA, MLA_W)), _full((N_HEADS_B * V_DIM_B, KV_LORA))],
        out_specs=[pl.BlockSpec((tm, MLA_W), row), pl.BlockSpec((tm, MLA_W), row), blk4(HEAD_PAD), blk4(HEAD_PAD),
                   blk4(V_DIM_B)],
        out_shape=[jax.ShapeDtypeStruct((t, MLA_W), BF16), jax.ShapeDtypeStruct((t, MLA_W), BF16),
                   jax.ShapeDtypeStruct((N_HEADS_B, nb, HEAD_PAD, tm), BF16),
                   jax.ShapeDtypeStruct((N_HEADS_B, nb, HEAD_PAD, tm), BF16),
                   jax.ShapeDtypeStruct((N_HEADS_B, nb, V_DIM_B, tm), BF16)],
        compiler_params=_params(("parallel",)),
    )(proj, proj, proj, posc, freq, qan, kvan, wq, wk, wv)


_MLA_SCALE2 = _MLA_SCALE * _LOG2E


def _mla_fwd(k, qt, vt, w_src):
    t = k.shape[0]
    nb = t // _TQ

    def body(k_ref, qt_ref, vt_ref, w_ref, o_ref, l_ref, wg_ref, raw_a, raw_b, send_sems, recv_sems, local_sem):
        qi = pl.program_id(1)
        first = (pl.program_id(0) == 0) & (qi == 0)
        last = (pl.program_id(0) == N_HEADS_B - 1) & (qi == nb - 1)

        @pl.when(first)
        def _():
            _gather_start(w_ref, wg_ref, send_sems, recv_sems, local_sem)

        q_t = qt_ref[0, 0]

        def product(kj):
            return _dot(k_ref[pl.ds(pl.multiple_of(kj * _TQ, _TQ), _TQ), :], q_t)

        def update(stats, raw_ref, kj, diagonal=False):
            m, l, acc = stats
            raw = raw_ref[...]
            if diagonal:
                key = lax.broadcasted_iota(jnp.int32, raw.shape, 0)
                qry = lax.broadcasted_iota(jnp.int32, raw.shape, 1)
                raw = jnp.where(key <= qry, raw, NEG)
            m_new = jnp.maximum(m, jnp.max(raw, axis=0, keepdims=True) * _MLA_SCALE2)
            alpha = jnp.exp2(m - m_new)
            p = jnp.exp2(raw * _MLA_SCALE2 - m_new)
            l = alpha * l + jnp.sum(p, axis=0, keepdims=True)
            acc = alpha * acc + _dot(vt_ref[0, kj], p.astype(BF16))
            return m_new, l, acc

        def trip(i, stats):
            raw_b[...] = product(2 * i + 1)
            stats = update(stats, raw_a, 2 * i)
            raw_a[...] = product(2 * i + 2)
            return update(stats, raw_b, 2 * i + 1)

        def tail_even(stats):
            return update(stats, raw_a, qi, True)

        def tail_odd(stats):
            raw_b[...] = product(qi)
            return update(update(stats, raw_a, qi - 1), raw_b, qi, True)

        init = (jnp.full((1, _TQ), NEG, F32), jnp.zeros((1, _TQ), F32), jnp.zeros((V_DIM_B, _TQ), F32))
        raw_a[...] = product(0)
        stats = lax.fori_loop(0, qi // 2, trip, init)
        m, l, acc = lax.cond(qi % 2 == 0, tail_even, tail_odd, stats)
        o_ref[0, 0] = acc / l
        l_ref[0, 0] = m + jnp.log(l) * _LOG2E

        @pl.when(last)
        def _():
            _gather_wait(w_ref, wg_ref, send_sems, recv_sems, local_sem)

    return pl.pallas_call(
        body, name="mla_fwd", grid=(N_HEADS_B, nb),
        in_specs=[pl.BlockSpec((t, HEAD_PAD), lambda h, qi: (0, h)),
                  pl.BlockSpec((1, 1, HEAD_PAD, _TQ), lambda h, qi: (h, qi, 0, 0)),
                  pl.BlockSpec((1, nb, V_DIM_B, _TQ), lambda h, qi: (h, 0, 0, 0)), _HBM],
        out_specs=[pl.BlockSpec((1, 1, V_DIM_B, _TQ), lambda h, qi: (h, qi, 0, 0)),
                   pl.BlockSpec((1, 1, 1, _TQ), lambda h, qi: (h, qi, 0, 0)), _HBM],
        out_shape=[jax.ShapeDtypeStruct((N_HEADS_B, nb, V_DIM_B, _TQ), F32),
                   jax.ShapeDtypeStruct((N_HEADS_B, nb, 1, _TQ), F32),
                   jax.ShapeDtypeStruct((N_CHIPS,) + w_src.shape, w_src.dtype)],
        scratch_shapes=[pltpu.VMEM((_TQ, _TQ), F32), pltpu.VMEM((_TQ, _TQ), F32),
                        pltpu.SemaphoreType.DMA((3,)), pltpu.SemaphoreType.DMA((3,)), pltpu.SemaphoreType.DMA(())],
        compiler_params=_params(("arbitrary", "arbitrary")),
    )(k, qt, vt, w_src)


def _ot_spec(tm, d):
    per = _TQ // tm
    return pl.BlockSpec((N_HEADS_B, 1, d, tm), lambda i: (0, i // per, 0, i % per))


def _mix_out_fwd(out_a, out_bt, proj, x, w_oa, w_ob, wb, g2, g3):
    t = x.shape[0]
    tm = 256

    def body(oa_ref, obt_ref, ga_ref, gb_ref, x_ref, woa_ref, wob_ref, wout_ref, g2_ref, g3_ref,
             mg_ref, y_ref, x1_ref, h2_ref):
        oa = _dot(oa_ref[...].astype(BF16), woa_ref[...])
        obt = obt_ref[...].reshape(N_HEADS_B * V_DIM_B, tm).astype(BF16)
        ob = _dot_tn(obt, wob_ref[...])
        merged = (jax.nn.sigmoid(ga_ref[...]) * oa + jax.nn.sigmoid(gb_ref[...]) * ob).astype(BF16)
        mg_ref[...] = merged
        y = _dot(merged, wout_ref[...].reshape(D_MODEL, D_MODEL))
        y_ref[...] = y
        x1 = x_ref[...] + (y * _rms(y)) * g2_ref[...]
        x1_ref[...] = x1
        h2_ref[...] = ((x1 * _rms(x1)) * g3_ref[...]).astype(BF16)

    row = lambda i: (i, 0)
    blk = pl.BlockSpec((tm, D_MODEL), row)
    return pl.pallas_call(
        body, name="mix_out_fwd", grid=(t // tm,),
        in_specs=[pl.BlockSpec((tm, WIDTH_A), row), _ot_spec(tm, V_DIM_B), pl.BlockSpec((tm, D_MODEL), lambda i: (i, 0)),
                  pl.BlockSpec((tm, D_MODEL), lambda i: (i, 1)), blk,
                  _full((WIDTH_A, D_MODEL)), _full((N_HEADS_B * V_DIM_B, D_MODEL)), _wb_spec("w_out"),
                  _full((1, D_MODEL)), _full((1, D_MODEL))],
        out_specs=[blk, blk, blk, blk],
        out_shape=[jax.ShapeDtypeStruct((t, D_MODEL), BF16), jax.ShapeDtypeStruct((t, D_MODEL), F32),
                   jax.ShapeDtypeStruct((t, D_MODEL), F32), jax.ShapeDtypeStruct((t, D_MODEL), BF16)],
        compiler_params=_params(("parallel",)),
    )(out_a, out_bt, proj, proj, x, w_oa, w_ob, wb, g2, g3)


_TM_MLP = 512


def _up_fwd(h2, wb):
    t = h2.shape[0]
    tm = _TM_MLP

    def body(h_ref, w_ref, a_ref):
        hv = h_ref[...]
        for j in range(N_CHIPS):
            u = _dot(hv, w_ref[j])
            a_ref[:, D_MODEL * j:D_MODEL * (j + 1)] = jnp.square(jnp.maximum(u, 0.0)).astype(BF16)

    return pl.pallas_call(
        body, name="up_fwd", grid=(t // tm,),
        in_specs=[pl.BlockSpec((tm, D_MODEL), lambda i: (i, 0)), _wb_spec("w_up")],
        out_specs=pl.BlockSpec((tm, D_FF), lambda i: (i, 0)),
        out_shape=jax.ShapeDtypeStruct((t, D_FF), BF16),
        compiler_params=_params(("parallel",)),
    )(h2, wb)


def _down_fwd_loss(a, wb, x1, target, g4):
    t = a.shape[0]
    tm = _TM_MLP

    def body(a_ref, w_ref, x1_ref, tg_ref, g_ref, dx2_ref, dyd_ref, dg_ref, loss_ref):
        @pl.when(pl.program_id(0) == 0)
        def _():
            dg_ref[...] = jnp.zeros(dg_ref.shape, F32)
            loss_ref[...] = jnp.zeros(loss_ref.shape, F32)

        yd = _dot(a_ref[...], w_ref[...].reshape(D_FF, D_MODEL))
        r = _rms(yd)
        n = yd * r
        diff = (x1_ref[...] + n * g_ref[...]) - tg_ref[...]
        loss_ref[...] += 0.5 * jnp.sum(jnp.mean(diff * diff, axis=-1, keepdims=True), axis=0, keepdims=True)
        dx2 = diff * (1.0 / D_MODEL)
        dx2_ref[...] = dx2
        dyd, dg = _norm_bwd(dx2, n, r, g_ref[...])
        dyd_ref[...] = dyd.astype(BF16)
        dg_ref[...] += dg

    row = lambda i: (i, 0)
    blk = pl.BlockSpec((tm, D_MODEL), row)
    return pl.pallas_call(
        body, name="down_fwd_loss", grid=(t // tm,),
        in_specs=[pl.BlockSpec((tm, D_FF), row), _wb_spec("w_down"), blk, blk, _full((1, D_MODEL))],
        out_specs=[blk, blk, _full((1, D_MODEL)), _full((1, LANES))],
        out_shape=[jax.ShapeDtypeStruct((t, D_MODEL), F32), jax.ShapeDtypeStruct((t, D_MODEL), BF16),
                   jax.ShapeDtypeStruct((1, D_MODEL), F32), jax.ShapeDtypeStruct((1, LANES), F32)],
        compiler_params=_params(("arbitrary",)),
    )(a, wb, x1, target, g4)


def _matmul_tn(a, b, name, tm, tn, tk=1024):
    t, m = a.shape
    n = b.shape[1]
    tk = min(tk, t)
    nk = t // tk

    def body(a_ref, b_ref, o_ref):
        @pl.when(pl.program_id(2) == 0)
        def _():
            o_ref[...] = jnp.zeros(o_ref.shape, F32)

        o_ref[...] += _dot_tn(a_ref[...].astype(BF16), b_ref[...].astype(BF16))

    return pl.pallas_call(
        body, name=name, grid=(m // tm, n // tn, nk),
        in_specs=[pl.BlockSpec((tk, tm), lambda i, j, k: (k, i)), pl.BlockSpec((tk, tn), lambda i, j, k: (k, j))],
        out_specs=pl.BlockSpec((tm, tn), lambda i, j, k: (i, j)),
        out_shape=jax.ShapeDtypeStruct((m, n), F32),
        compiler_params=_params(("parallel", "parallel", "arbitrary")),
    )(a, b)


_TK_DW = 2048


def _dw_into_blocks(a, b, weight, tm, tk, buf=None):
    t, m = a.shape
    n = b.shape[1]
    tk = min(tk, t)
    nk = t // tk
    rows = PACK_ROWS[weight]
    br = min(tm, rows)
    chips = tm // br
    first = _row_offset(GROUP_B, weight) // br
    per_chip = rows // br
    if weight == "w_up":
        out_map = lambda i, j, k: (j, first + i, 0)
    elif chips > 1:
        out_map = lambda i, j, k: (i, first, 0)
    else:
        out_map = lambda i, j, k: (i // per_chip, first + i % per_chip, 0)

    def body(a_ref, b_ref, *rest):
        o_ref = rest[-1]

        @pl.when(pl.program_id(2) == 0)
        def _():
            o_ref[...] = jnp.zeros(o_ref.shape, F32)

        o_ref[...] += _dot_tn(a_ref[...].astype(BF16), b_ref[...].astype(BF16)).reshape(o_ref.shape)

    in_specs = [pl.BlockSpec((tk, tm), lambda i, j, k: (k, i)), pl.BlockSpec((tk, D_MODEL), lambda i, j, k: (k, j))]
    operands = [a, b]
    if buf is not None:
        in_specs.append(pl.BlockSpec(memory_space=pl.ANY))
        operands.append(buf)
    total = sum(PACK_ROWS[w] for w in GROUP_B)
    return pl.pallas_call(
        body, name="dw_" + weight[2:], grid=(m // tm, n // D_MODEL, nk),
        in_specs=in_specs, out_specs=pl.BlockSpec((chips, br, D_MODEL), out_map),
        out_shape=jax.ShapeDtypeStruct((N_CHIPS, total, D_MODEL), F32),
        input_output_aliases={} if buf is None else {2: 0},
        compiler_params=_params(("parallel", "parallel", "arbitrary")),
    )(*operands)


def _down_bwd(dyd, wb, a):
    t = dyd.shape[0]
    tm = _TM_MLP

    def body(d_ref, w_ref, a_ref, du_ref):
        da = _dot_nt(d_ref[...], w_ref[...].reshape(D_FF, D_MODEL))
        du_ref[...] = (da * (2.0 * jnp.sqrt(a_ref[...].astype(F32)))).astype(BF16)

    row = lambda i: (i, 0)
    return pl.pallas_call(
        body, name="down_bwd", grid=(t // tm,),
        in_specs=[pl.BlockSpec((tm, D_MODEL), row), _wb_spec("w_down"), pl.BlockSpec((tm, D_FF), row)],
        out_specs=pl.BlockSpec((tm, D_FF), row),
        out_shape=jax.ShapeDtypeStruct((t, D_FF), BF16),
        compiler_params=_params(("parallel",)),
    )(dyd, wb, a)


def _up_bwd(du, wb, x1, dx2, y, g3, g2):
    t = du.shape[0]
    tm = _TM_MLP

    def body(du_ref, w_ref, x1_ref, dx2_ref, y_ref, g3_ref, g2_ref, dx1_ref, dy_ref, dg3_ref, dg2_ref):
        @pl.when(pl.program_id(0) == 0)
        def _():
            dg3_ref[...] = jnp.zeros(dg3_ref.shape, F32)
            dg2_ref[...] = jnp.zeros(dg2_ref.shape, F32)

        dh2 = _dot_nt(du_ref[:, 0:D_MODEL], w_ref[0])
        for j in range(1, N_CHIPS):
            dh2 = dh2 + _dot_nt(du_ref[:, D_MODEL * j:D_MODEL * (j + 1)], w_ref[j])
        x1 = x1_ref[...]
        r3 = _rms(x1)
        d3, dg3 = _norm_bwd(dh2, x1 * r3, r3, g3_ref[...])
        dx1 = dx2_ref[...] + d3
        dx1_ref[...] = dx1
        dg3_ref[...] += dg3
        y = y_ref[...]
        r2 = _rms(y)
        dy, dg2 = _norm_bwd(dx1, y * r2, r2, g2_ref[...])
        dy_ref[...] = dy.astype(BF16)
        dg2_ref[...] += dg2

    row = lambda i: (i, 0)
    blk = pl.BlockSpec((tm, D_MODEL), row)
    return pl.pallas_call(
        body, name="up_bwd", grid=(t // tm,),
        in_specs=[pl.BlockSpec((tm, D_FF), row), _wb_spec("w_up"),
                  blk, blk, blk, _full((1, D_MODEL)), _full((1, D_MODEL))],
        out_specs=[blk, blk, _full((1, D_MODEL)), _full((1, D_MODEL))],
        out_shape=[jax.ShapeDtypeStruct((t, D_MODEL), F32), jax.ShapeDtypeStruct((t, D_MODEL), BF16),
                   jax.ShapeDtypeStruct((1, D_MODEL), F32), jax.ShapeDtypeStruct((1, D_MODEL), F32)],
        compiler_params=_params(("arbitrary",)),
    )(du, wb, x1, dx2, y, g3, g2)


def _mix_out_bwd(dy, out_a, out_bt, proj, w_oa, w_ob, wb):
    t = dy.shape[0]
    tm = 256
    nb = t // _TQ

    def body(dy_ref, oa_ref, obt_ref, ga_ref, gb_ref, woa_ref, wob_ref, wout_ref,
             doa_ref, dob_ref, dga_ref, dgb_ref, da_ref, db_ref, dbt_ref, dela_ref, delb_ref):
        dm = _dot_nt(dy_ref[...], wout_ref[...].reshape(D_MODEL, D_MODEL))
        out_a_v = oa_ref[...]
        out_bt_v = obt_ref[...].reshape(N_HEADS_B * V_DIM_B, tm)
        oa = _dot(out_a_v.astype(BF16), woa_ref[...])
        ob = _dot_tn(out_bt_v.astype(BF16), wob_ref[...])
        sa, sb = jax.nn.sigmoid(ga_ref[...]), jax.nn.sigmoid(gb_ref[...])
        doa = (dm * sa).astype(BF16)
        dob = (dm * sb).astype(BF16)
        doa_ref[...] = doa
        dob_ref[...] = dob
        dga_ref[...] = (dm * oa * (sa * (1.0 - sa))).astype(BF16)
        dgb_ref[...] = (dm * ob * (sb * (1.0 - sb))).astype(BF16)
        d_out_a = _dot_nt(doa, woa_ref[...])
        da_ref[...] = d_out_a
        prod_at = (d_out_a * out_a_v).T
        dela_ref[...] = jnp.concatenate(
            [jnp.sum(_head_rows(prod_at, h), axis=0, keepdims=True) for h in range(N_HEADS_A)], axis=0)
        d_out_b = _dot_nt(dob, wob_ref[...])
        d_out_bt = _dot_nt(wob_ref[...], dob)
        prod_bt = d_out_bt * out_bt_v
        for h in range(N_HEADS_B):
            db_ref[h] = d_out_b[:, V_DIM_B * h:V_DIM_B * (h + 1)].astype(BF16)
            dbt_ref[h, 0] = d_out_bt[V_DIM_B * h:V_DIM_B * (h + 1), :].astype(BF16)
            delb_ref[h, 0] = jnp.sum(prod_bt[V_DIM_B * h:V_DIM_B * (h + 1), :], axis=0, keepdims=True)

    row = lambda i: (i, 0)
    blk = pl.BlockSpec((tm, D_MODEL), row)
    return pl.pallas_call(
        body, name="mix_out_bwd", grid=(t // tm,),
        in_specs=[blk, pl.BlockSpec((tm, WIDTH_A), row), _ot_spec(tm, V_DIM_B),
                  pl.BlockSpec((tm, D_MODEL), lambda i: (i, 0)), pl.BlockSpec((tm, D_MODEL), lambda i: (i, 1)),
                  _full((WIDTH_A, D_MODEL)), _full((N_HEADS_B * V_DIM_B, D_MODEL)), _wb_spec("w_out")],
        out_specs=[blk, blk, blk, blk, pl.BlockSpec((tm, WIDTH_A), row),
                   pl.BlockSpec((N_HEADS_B, tm, V_DIM_B), lambda i: (0, i, 0)), _ot_spec(tm, V_DIM_B),
                   pl.BlockSpec((N_HEADS_A, tm), lambda i: (0, i)), _ot_spec(tm, 1)],
        out_shape=[jax.ShapeDtypeStruct((t, D_MODEL), BF16)] * 4
        + [jax.ShapeDtypeStruct((t, WIDTH_A), F32), jax.ShapeDtypeStruct((N_HEADS_B, t, V_DIM_B), BF16),
           jax.ShapeDtypeStruct((N_HEADS_B, nb, V_DIM_B, _TQ), BF16), jax.ShapeDtypeStruct((N_HEADS_A, t), F32),
           jax.ShapeDtypeStruct((N_HEADS_B, nb, 1, _TQ), F32)],
        compiler_params=_params(("parallel",)),
    )(dy, out_a, out_bt, proj, proj, w_oa, w_ob, wb)


def _dw_ob(out_bt, dob):
    t = dob.shape[0]
    nb = t // _TQ

    def body(obt_ref, dob_ref, o_ref):
        @pl.when(pl.program_id(0) == 0)
        def _():
            o_ref[...] = jnp.zeros(o_ref.shape, F32)

        obt = obt_ref[...].reshape(N_HEADS_B * V_DIM_B, _TQ).astype(BF16)
        o_ref[...] += _dot(obt, dob_ref[...])

    return pl.pallas_call(
        body, name="dw_o_b", grid=(nb,),
        in_specs=[pl.BlockSpec((N_HEADS_B, 1, V_DIM_B, _TQ), lambda i: (0, i, 0, 0)),
                  pl.BlockSpec((_TQ, D_MODEL), lambda i: (i, 0))],
        out_specs=_full((N_HEADS_B * V_DIM_B, D_MODEL)),
        out_shape=jax.ShapeDtypeStruct((N_HEADS_B * V_DIM_B, D_MODEL), F32),
        compiler_params=_params(("arbitrary",)),
    )(out_bt, dob)


def _mla_bwd(q, k, qt, kt, vt, d_out, d_out_t, lse, delta, gp):
    t = q.shape[0]
    nb = t // _TQ

    def body(k_ref, kt_ref, vt_ref, q_ref, qt_ref, do_ref, dot_ref, lrow_ref, drow_ref, gp_ref,
             dq_ref, dkt_ref, dvt_ref, land_ref, l_rep, d_rep, send_sems, recv_sems):
        step = pl.program_id(1)
        kj = nb - 1 - step

        @pl.when((pl.program_id(0) == 0) & (step == 0))
        def _():
            _scatter_start(gp_ref, land_ref, send_sems, recv_sems)

        @pl.when(step == 0)
        def _():
            dq_ref[...] = jnp.zeros(dq_ref.shape, F32)
            for b in range(nb):
                l_rep[_TQ * b:_TQ * (b + 1), :] = jnp.broadcast_to(lrow_ref[0, b], (LANES, _TQ)).T
                d_rep[_TQ * b:_TQ * (b + 1), :] = jnp.broadcast_to(drow_ref[0, b], (LANES, _TQ)).T

        kv, k_t, v_t = k_ref[...], kt_ref[0, 0], vt_ref[0, 0]

        def rows_of(qi):
            return pl.ds(pl.multiple_of(qi * _TQ, _TQ), _TQ)

        def products(qi, diagonal=False):
            s = _dot(q_ref[rows_of(qi), :], k_t) * _MLA_SCALE2
            if diagonal:
                qry = lax.broadcasted_iota(jnp.int32, s.shape, 0)
                key = lax.broadcasted_iota(jnp.int32, s.shape, 1)
                s = jnp.where(key <= qry, s, NEG)
            return s, _dot(do_ref[0, rows_of(qi), :], v_t)

        def update(carry, prods, qi):
            dkt, dvt = carry
            s, dp = prods
            lse, delta = l_rep[rows_of(qi), :], d_rep[rows_of(qi), :]
            ps, dss = [], []
            for c in range(_TQ // LANES):
                strip = slice(LANES * c, LANES * (c + 1))
                p = jnp.exp2(s[:, strip] - lse)
                ps.append(p.astype(BF16))
                dss.append((p * (dp[:, strip] - delta) * _MLA_SCALE).astype(BF16))
            p_b, ds_b = jnp.concatenate(ps, axis=1), jnp.concatenate(dss, axis=1)
            dvt = dvt + _dot(dot_ref[0, qi], p_b)
            dkt = dkt + _dot(qt_ref[0, qi], ds_b)
            dq_ref[rows_of(qi), :] += _dot(ds_b, kv)
            return dkt, dvt

        def pair(i, carry):
            qa = kj + 1 + 2 * i
            pa, pb = products(qa), products(qa + 1)
            return update(update(carry, pa, qa), pb, qa + 1)

        init = (jnp.zeros((HEAD_PAD, _TQ), F32), jnp.zeros((V_DIM_B, _TQ), F32))
        carry = update(init, products(kj, True), kj)
        pairs = (nb - 1 - kj) // 2
        carry = lax.fori_loop(0, pairs, pair, carry)
        dkt, dvt = lax.fori_loop(kj + 1 + 2 * pairs, nb, lambda qi, cr: update(cr, products(qi), qi), carry)
        dkt_ref[0, 0] = dkt
        dvt_ref[0, 0] = dvt

        @pl.when((pl.program_id(0) == N_HEADS_B - 1) & (step == nb - 1))
        def _():
            _scatter_wait(gp_ref, land_ref, send_sems, recv_sems)

    head4 = lambda d: pl.BlockSpec((1, nb, d, _TQ), lambda h, s: (h, 0, 0, 0))
    blk4 = lambda d: pl.BlockSpec((1, 1, d, _TQ), lambda h, s: (h, nb - 1 - s, 0, 0))
    head3 = lambda d: pl.BlockSpec((1, t, d), lambda h, kj: (h, 0, 0))
    per_head = pl.BlockSpec((t, HEAD_PAD), lambda h, kj: (0, h))
    return pl.pallas_call(
        body, name="mla_bwd", grid=(N_HEADS_B, nb),
        in_specs=[pl.BlockSpec((_TQ, HEAD_PAD), lambda h, s: (nb - 1 - s, h)), blk4(HEAD_PAD), blk4(V_DIM_B),
                  per_head, head4(HEAD_PAD), head3(V_DIM_B), head4(V_DIM_B), head4(1), head4(1), _HBM],
        out_specs=[per_head, blk4(HEAD_PAD), blk4(V_DIM_B), _HBM],
        out_shape=[jax.ShapeDtypeStruct((t, MLA_W), F32), jax.ShapeDtypeStruct((N_HEADS_B, nb, HEAD_PAD, _TQ), F32),
                   jax.ShapeDtypeStruct((N_HEADS_B, nb, V_DIM_B, _TQ), F32),
                   jax.ShapeDtypeStruct((3,) + gp.shape[1:], gp.dtype)],
        scratch_shapes=[pltpu.VMEM((t, LANES), F32), pltpu.VMEM((t, LANES), F32),
                        pltpu.SemaphoreType.DMA((3,)), pltpu.SemaphoreType.DMA((3,))],
        compiler_params=_params(("arbitrary", "arbitrary")),
    )(k, kt, vt, q, qt, d_out, d_out_t, lse, delta, gp)


def _mla_prep_bwd(dq, dkt, dvt, proj, posc, freq, qan, kvan, wq, wk, wv, swap_src):
    t = dq.shape[0]
    tm = _TQ

    def body(dq_ref, dkt_ref, dvt_ref, cq_ref, ckv_ref, pos_ref, f_ref, qan_ref, kvan_ref, wq_ref, wk_ref, wv_ref, src_ref,
             dcq_ref, dckv_ref, dkr_ref, dwq_ref, dwk_ref, dwv_ref, dqan_ref, dkvan_ref, got_ref, send_sem, recv_sem):
        swap = _sibling_copy(src_ref, got_ref, send_sem, recv_sem)

        @pl.when(pl.program_id(0) == 0)
        def _():
            swap.start()
            for r in (dwq_ref, dwk_ref, dwv_ref, dqan_ref, dkvan_ref):
                r[...] = jnp.zeros(r.shape, F32)

        cq = cq_ref[...]
        rq = _rms(cq)
        nq_ = cq * rq
        cqn = (nq_ * qan_ref[...]).astype(BF16)
        ckv = ckv_ref[...]
        rkv = _rms(ckv)
        nkv = ckv * rkv
        ckvn = (nkv * kvan_ref[...]).astype(BF16)
        c, s, lo, hi = _rope_coeffs(pos_ref[...], f_ref[...])
        dkr = jnp.zeros((tm, LANES), F32)
        dqb, dkb = [], []
        for h in range(N_HEADS_B):
            dqb.append(_unrope(dq_ref[:, HEAD_PAD * h:HEAD_PAD * (h + 1)], c, s, lo, hi).astype(BF16))
            dk_h = dkt_ref[h, 0].T
            dkr = dkr + dk_h
            dkb.append(dk_h.astype(BF16))
        dqb, dkb = jnp.concatenate(dqb, axis=1), jnp.concatenate(dkb, axis=1)
        dkr_ref[...] = jnp.where(lo | hi, _unrope(dkr, c, s, lo, hi), 0.0).astype(BF16)
        dvb = dvt_ref[...].reshape(N_HEADS_B * V_DIM_B, tm).T.astype(BF16)
        dwq_ref[...] += _dot_tn(cqn, dqb)
        dwk_ref[...] += _dot_tn(ckvn, dkb)
        dwv_ref[...] += _dot_tn(ckvn, dvb)
        dcqn = _dot_nt(dqb, wq_ref[...])
        dckvn = _dot_nt(dkb, wk_ref[...]) + _dot_nt(dvb, wv_ref[...])
        dcq, dqan = _norm_bwd(dcqn, nq_, rq, qan_ref[...])
        dckv, dkvan = _norm_bwd(dckvn, nkv, rkv, kvan_ref[...])
        dcq_ref[...] = dcq.astype(BF16)
        dckv_ref[...] = dckv.astype(BF16)
        dqan_ref[...] += dqan
        dkvan_ref[...] += dkvan

        @pl.when(pl.program_id(0) == t // tm - 1)
        def _():
            swap.wait_recv()
            swap.wait_send()

    row = lambda i: (i, 0)
    vw = N_HEADS_B * V_DIM_B
    return pl.pallas_call(
        body, name="mla_prep_bwd", grid=(t // tm,),
        in_specs=[pl.BlockSpec((tm, MLA_W), row), pl.BlockSpec((N_HEADS_B, 1, HEAD_PAD, tm), lambda i: (0, i, 0, 0)),
                  pl.BlockSpec((N_HEADS_B, 1, V_DIM_B, tm), lambda i: (0, i, 0, 0)),
                  pl.BlockSpec((tm, Q_LORA), lambda i: (i, _CQ_BLK)),
                  pl.BlockSpec((tm, LANES), lambda i: (i, _CKV_BLK)),
                  pl.BlockSpec((tm, 1), row), _full((1, LANES)), _full((1, Q_LORA)), _full((1, KV_LORA)),
                  _full((Q_LORA, MLA_W)), _full((KV_LORA, MLA_W)), _full((KV_LORA, vw)), _HBM],
        out_specs=[pl.BlockSpec((tm, Q_LORA), row), pl.BlockSpec((tm, LANES), row), pl.BlockSpec((tm, LANES), row),
                   _full((Q_LORA, MLA_W)), _full((KV_LORA, MLA_W)), _full((KV_LORA, vw)),
                   _full((1, Q_LORA)), _full((1, KV_LORA)), _HBM],
        out_shape=[jax.ShapeDtypeStruct((t, Q_LORA), BF16), jax.ShapeDtypeStruct((t, LANES), BF16),
                   jax.ShapeDtypeStruct((t, LANES), BF16),
                   jax.ShapeDtypeStruct((Q_LORA, MLA_W), F32), jax.ShapeDtypeStruct((KV_LORA, MLA_W), F32),
                   jax.ShapeDtypeStruct((KV_LORA, vw), F32),
                   jax.ShapeDtypeStruct((1, Q_LORA), F32), jax.ShapeDtypeStruct((1, KV_LORA), F32),
                   jax.ShapeDtypeStruct(swap_src.shape, swap_src.dtype)],
        scratch_shapes=[pltpu.SemaphoreType.DMA(()), pltpu.SemaphoreType.DMA(())],
        compiler_params=_params(("arbitrary",)),
    )(dq, dkt, dvt, proj, proj, posc, freq, qan, kvan, wq, wk, wv, swap_src)


def _swa_bwd(proj, d_out, lse, delta, posc, posr, sinks):
    t = proj.shape[0]
    nb = t // BLOCK

    def body(q_ref, kc_ref, kp_ref, vc_ref, vp_ref, do_ref, l_ref, d_ref, pq_ref, pc_ref, pp_ref, sink_ref,
             dq_ref, dk_ref, dv_ref, ds_ref, dkb_s, dvb_s, dk_carry, dv_carry):
        n = pl.program_id(0)

        @pl.when(n == 0)
        def _():
            ds_ref[...] = jnp.zeros(ds_ref.shape, F32)
            dk_carry[...] = jnp.zeros(dk_carry.shape, F32)
            dv_carry[...] = jnp.zeros(dv_carry.shape, F32)

        @pl.when(n < nb)
        def _():
            kb, vb, dist, valid = _swa_band(n, kp_ref, kc_ref, vp_ref, vc_ref, pq_ref, pp_ref, pc_ref)
            qv, dov = q_ref[...], do_ref[...]
            q_t, do_t, kb_t = qv.T, dov.T, kb.T
            lane = lax.broadcasted_iota(jnp.int32, (1, LANES), 1)
            dsink = jnp.zeros((1, LANES), F32)
            dq_t = []
            for kh in range(N_KV_A):
                heads = range(_GROUP_A * kh, _GROUP_A * (kh + 1))
                st_g = _dot(_head_cols(kb, kh).astype(BF16), _group_t(q_t, kh))
                dpt_g = _dot(_head_cols(vb, kh).astype(BF16), _group_t(do_t, kh))
                pts, dsts = [], []
                for j, h in enumerate(heads):
                    st = _swa_scores_t(st_g, j, h, dist, valid)
                    l_h, d_h = l_ref[h:h + 1, :], d_ref[h:h + 1, :]
                    pt = jnp.exp2(st - l_h)
                    p_sink = jnp.exp2(sink_ref[0:1, h:h + 1] * _LOG2E - l_h)
                    dsink = jnp.where(lane == h, jnp.sum(-p_sink * d_h, axis=1, keepdims=True), dsink)
                    dst = pt * (dpt_g[:, BLOCK * j:BLOCK * (j + 1)] - d_h) * _SWA_SCALE
                    pts.append(pt.astype(BF16))
                    dsts.append(dst.astype(BF16))
                pt_g, dst_g = jnp.concatenate(pts, axis=1), jnp.concatenate(dsts, axis=1)
                q_g = jnp.concatenate([_head_cols(qv, h) for h in heads], axis=0).astype(BF16)
                do_g = jnp.concatenate([_head_cols(dov, h) for h in heads], axis=0).astype(BF16)
                dkb_s[:, HEAD_DIM_A * kh:HEAD_DIM_A * (kh + 1)] = _dot(dst_g, q_g)
                dvb_s[:, HEAD_DIM_A * kh:HEAD_DIM_A * (kh + 1)] = _dot(pt_g, do_g)
                dq_g = _dot(_head_rows(kb_t, kh).astype(BF16), dst_g)
                dq_t.extend(dq_g[:, BLOCK * j:BLOCK * (j + 1)] for j in range(_GROUP_A))
            dq_ref[...] = jnp.concatenate(dq_t, axis=0).T
            ds_ref[...] += dsink
            dk_ref[...] = dk_carry[...] + dkb_s[0:BLOCK, :]
            dv_ref[...] = dv_carry[...] + dvb_s[0:BLOCK, :]
            dk_carry[...] = dkb_s[BLOCK:2 * BLOCK, :]
            dv_carry[...] = dvb_s[BLOCK:2 * BLOCK, :]

        @pl.when(n == nb)
        def _():
            dk_ref[...] = dk_carry[...]
            dv_ref[...] = dv_carry[...]

    cur = lambda n: (jnp.minimum(n, nb - 1), 0)
    cur_t = lambda n: (0, jnp.minimum(n, nb - 1))
    prv = lambda n: jnp.maximum(jnp.minimum(n, nb - 1) - 1, 0)
    out_prev = lambda n: (jnp.maximum(n - 1, 0), 0)
    return pl.pallas_call(
        body, name="swa_bwd", grid=(nb + 1,),
        in_specs=[pl.BlockSpec((BLOCK, WIDTH_A), lambda n: (jnp.minimum(n, nb - 1), _QA_BLK)),
                  pl.BlockSpec((BLOCK, LANES), lambda n: (jnp.minimum(n, nb - 1), _KA_BLK)),
                  pl.BlockSpec((BLOCK, LANES), lambda n: (prv(n), _KA_BLK)),
                  pl.BlockSpec((BLOCK, LANES), lambda n: (jnp.minimum(n, nb - 1), _VA_BLK)),
                  pl.BlockSpec((BLOCK, LANES), lambda n: (prv(n), _VA_BLK)),
                  pl.BlockSpec((BLOCK, WIDTH_A), cur), pl.BlockSpec((N_HEADS_A, BLOCK), cur_t),
                  pl.BlockSpec((N_HEADS_A, BLOCK), cur_t), pl.BlockSpec((1, BLOCK), cur_t),
                  pl.BlockSpec((BLOCK, 1), cur), pl.BlockSpec((BLOCK, 1), lambda n: (prv(n), 0)),
                  _full((1, N_HEADS_A))],
        out_specs=[pl.BlockSpec((BLOCK, WIDTH_A), cur), pl.BlockSpec((BLOCK, LANES), out_prev),
                   pl.BlockSpec((BLOCK, LANES), out_prev), _full((1, LANES))],
        out_shape=[jax.ShapeDtypeStruct((t, WIDTH_A), F32), jax.ShapeDtypeStruct((t, LANES), F32),
                   jax.ShapeDtypeStruct((t, LANES), F32), jax.ShapeDtypeStruct((1, LANES), F32)],
        scratch_shapes=[pltpu.VMEM((2 * BLOCK, LANES), F32), pltpu.VMEM((2 * BLOCK, LANES), F32),
                        pltpu.VMEM((BLOCK, LANES), F32), pltpu.VMEM((BLOCK, LANES), F32)],
        compiler_params=_params(("arbitrary",)),
    )(proj, proj, proj, proj, proj, d_out, lse, delta, posr, posc, posc, sinks)


def _in_bwd(dproj, w_in_t, x, dx1, g1, gp):
    t = x.shape[0]
    tm = 256
    steps = t // tm

    def body(dp_ref, w_ref, x_ref, dx1_ref, g_ref, gp_ref, dx_ref, dg_ref, land_ref, send_sems, recv_sems):
        i = pl.program_id(0)

        @pl.when(i == 0)
        def _():
            dg_ref[...] = jnp.zeros(dg_ref.shape, F32)
            _scatter_start(gp_ref, land_ref, send_sems, recv_sems)

        dh = _dot(dp_ref[...], w_ref[...])
        xv = x_ref[...]
        r = _rms(xv)
        dx, dg = _norm_bwd(dh, xv * r, r, g_ref[...])
        dx_ref[...] = dx1_ref[...] + dx
        dg_ref[...] += dg

        @pl.when(i == steps - 1)
        def _():
            _scatter_wait(gp_ref, land_ref, send_sems, recv_sems)

    row = lambda i: (i, 0)
    blk = pl.BlockSpec((tm, D_MODEL), row)
    return pl.pallas_call(
        body, name="in_bwd", grid=(steps,),
        in_specs=[pl.BlockSpec((tm, D_IN_PAD), row), _full((D_IN_PAD, D_MODEL)), blk, blk, _full((1, D_MODEL)), _HBM],
        out_specs=[blk, _full((1, D_MODEL)), _HBM],
        out_shape=[jax.ShapeDtypeStruct((t, D_MODEL), F32), jax.ShapeDtypeStruct((1, D_MODEL), F32),
                   jax.ShapeDtypeStruct((3,) + gp.shape[1:], gp.dtype)],
        scratch_shapes=[pltpu.SemaphoreType.DMA((3,)), pltpu.SemaphoreType.DMA((3,))],
        compiler_params=_params(("arbitrary",)),
    )(dproj, w_in_t, x, dx1, g1, gp)


def _adamw_store(w, g, m, v, out_refs):
    g_out, d_out, m_out, v_out = out_refs
    m_new = ADAM_B1 * m + (1.0 - ADAM_B1) * g
    v_new = ADAM_B2 * v + (1.0 - ADAM_B2) * jnp.square(g)
    m_hat = m_new / (1.0 - ADAM_B1 ** ADAM_STEP)
    v_hat = v_new / (1.0 - ADAM_B2 ** ADAM_STEP)
    g_out[...] = g
    d_out[...] = -ADAM_LR * (m_hat / (jnp.sqrt(v_hat) + ADAM_EPS) + ADAM_WD * w)
    m_out[...] = m_new
    v_out[...] = v_new


_SMALL_SLOTS = {"pre_norm_mix": (0, 0, D_MODEL), "post_norm_mix": (1, 0, D_MODEL), "pre_norm_mlp": (2, 0, D_MODEL),
                "post_norm_mlp": (3, 0, D_MODEL), "q_a_norm": (4, 0, Q_LORA), "kv_a_norm": (4, Q_LORA, KV_LORA),
                "sinks": (4, Q_LORA + KV_LORA, N_HEADS_A)}
_LOSS_ROW = 5


def _adamw_small(red, w, m, v):
    names = tuple(_SMALL_SLOTS)
    n = len(names)

    def body(*refs):
        red_ref, ws, ms, vs, outs = refs[0], refs[1:1 + n], refs[1 + n:1 + 2 * n], refs[1 + 2 * n:1 + 3 * n], refs[1 + 3 * n:]
        for k, name in enumerate(names):
            row, lane, width = _SMALL_SLOTS[name]
            g = red_ref[row:row + 1, lane:lane + width]
            _adamw_store(ws[k][...], g, ms[k][...], vs[k][...], outs[4 * k:4 * k + 4])

    vmem = pl.BlockSpec(memory_space=pltpu.VMEM)
    res = pl.pallas_call(
        body, name="adamw_small", in_specs=[vmem] * (1 + 3 * n), out_specs=[vmem] * (4 * n),
        out_shape=[jax.ShapeDtypeStruct(w[name].shape, F32) for name in names for _ in range(4)],
    )(red, *[w[k] for k in names], *[m[k] for k in names], *[v[k] for k in names])
    return {name: res[4 * k:4 * k + 4] for k, name in enumerate(names)}


def _adamw(w, g_parts, m, v, name, block, g_row_off=0):
    r, c = w.shape
    br, bc = block
    ng = len(g_parts)

    def body(*refs):
        w_ref, g_refs, m_ref, v_ref = refs[0], refs[1:1 + ng], refs[1 + ng], refs[2 + ng]
        g = g_refs[0][...]
        for gr in g_refs[1:]:
            g = g + gr[...]
        _adamw_store(w_ref[...], g, m_ref[...], v_ref[...], refs[3 + ng:])

    assert g_row_off % br == 0 and r % br == 0 and c % bc == 0
    blk = pl.BlockSpec(block, lambda i, j: (i, j))
    g_blk = pl.BlockSpec(block, lambda i, j: (i + g_row_off // br, j))
    return pl.pallas_call(
        body, name=name, grid=(r // br, c // bc),
        in_specs=[blk] + [g_blk] * ng + [blk, blk], out_specs=[blk] * 4,
        out_shape=[jax.ShapeDtypeStruct((r, c), F32)] * 4,
        compiler_params=_params(("parallel", "parallel")),
    )(w, *g_parts, m, v)


_HBM = pl.BlockSpec(memory_space=pltpu.HBM)


def _other_chips(x, y):
    return ((1 - x, y), (x, 1 - y), (1 - x, 1 - y))


def _gather_copies(src, out, send_sems, recv_sems, local_sem):
    x, y, c = lax.axis_index("x"), lax.axis_index("y"), lax.axis_index("c")
    me = 2 * x + y
    local = pltpu.make_async_copy(src, out.at[me], local_sem)

    def copies(arriving):
        return [pltpu.make_async_remote_copy(src_ref=src, dst_ref=out.at[2 * px + py if arriving else me],
                                             send_sem=send_sems.at[j], recv_sem=recv_sems.at[j], device_id=(px, py, c),
                                             device_id_type=MESH)
                for j, (px, py) in enumerate(_other_chips(x, y))]

    return local, copies


def _gather_start(src, out, send_sems, recv_sems, local_sem):
    local, copies = _gather_copies(src, out, send_sems, recv_sems, local_sem)
    local.start()
    for cp in copies(False):
        cp.start()


def _gather_wait(src, out, send_sems, recv_sems, local_sem):
    local, copies = _gather_copies(src, out, send_sems, recv_sems, local_sem)
    for cp in copies(True):
        cp.wait_recv()
    for cp in copies(False):
        cp.wait_send()
    local.wait()


def _scatter_copies(src, land, send_sems, recv_sems):
    x, y, c = lax.axis_index("x"), lax.axis_index("y"), lax.axis_index("c")
    return [pltpu.make_async_remote_copy(src_ref=src.at[2 * px + py], dst_ref=land.at[j], send_sem=send_sems.at[j],
                                         recv_sem=recv_sems.at[j], device_id=(px, py, c), device_id_type=MESH)
            for j, (px, py) in enumerate(_other_chips(x, y))]


def _scatter_start(src, land, send_sems, recv_sems):
    for cp in _scatter_copies(src, land, send_sems, recv_sems):
        cp.start()


def _scatter_wait(src, land, send_sems, recv_sems):
    copies = _scatter_copies(src, land, send_sems, recv_sems)
    for cp in copies:
        cp.wait_recv()
    for cp in copies:
        cp.wait_send()


def _all_gather_chips(packed):
    r = packed.shape[0]
    half = r // 2

    def body(src, out, ici_send, ici_recv, d2d_send, d2d_recv, local_sem):
        x, y, c = lax.axis_index("x"), lax.axis_index("y"), lax.axis_index("c")
        me = 2 * x + y
        mine = pl.ds(pl.multiple_of(c * half, 16), half)
        theirs = pl.ds(pl.multiple_of((1 - c) * half, 16), half)
        chips = _other_chips(x, y)
        local = pltpu.make_async_copy(src, out.at[me], local_sem)
        local.start()
        sends = [pltpu.make_async_remote_copy(src_ref=src.at[mine], dst_ref=out.at[me, mine], send_sem=ici_send.at[j],
                                              recv_sem=ici_recv.at[j], device_id=(px, py, c), device_id_type=MESH)
                 for j, (px, py) in enumerate(chips)]
        for cp in sends:
            cp.start()
        passed = []
        for j, (px, py) in enumerate(chips):
            block = 2 * px + py
            pltpu.make_async_remote_copy(src_ref=src.at[mine], dst_ref=out.at[block, mine], send_sem=ici_send.at[j],
                                         recv_sem=ici_recv.at[j], device_id=(px, py, c), device_id_type=MESH).wait_recv()
            cp = pltpu.make_async_remote_copy(src_ref=out.at[block, mine], dst_ref=out.at[block, mine],
                                              send_sem=d2d_send.at[j], recv_sem=d2d_recv.at[j],
                                              device_id=(x, y, 1 - c), device_id_type=MESH)
            cp.start()
            passed.append(cp)
        for j, (px, py) in enumerate(chips):
            block = 2 * px + py
            pltpu.make_async_remote_copy(src_ref=out.at[block, theirs], dst_ref=out.at[block, theirs],
                                         send_sem=d2d_send.at[j], recv_sem=d2d_recv.at[j],
                                         device_id=(x, y, 1 - c), device_id_type=MESH).wait_recv()
        for cp in sends + passed:
            cp.wait_send()
        local.wait()

    sems = pltpu.SemaphoreType.DMA((3,))
    return pl.pallas_call(
        body, name="ag_weights", in_specs=[_HBM], out_specs=_HBM,
        out_shape=jax.ShapeDtypeStruct((N_CHIPS,) + packed.shape, packed.dtype),
        scratch_shapes=[sems, sems, sems, sems, pltpu.SemaphoreType.DMA(())],
    )(packed)


def _sum4(gp, land, chip, name):
    _, r, w = gp.shape
    tr = 128

    def body(chip_ref, o_ref, l_ref, s_ref):
        s_ref[...] = ((o_ref[0] + l_ref[0].astype(F32)) + l_ref[1].astype(F32)) + l_ref[2].astype(F32)

    return pl.pallas_call(
        body, name=name,
        grid_spec=pltpu.PrefetchScalarGridSpec(
            num_scalar_prefetch=1, grid=(r // tr,),
            in_specs=[pl.BlockSpec((1, tr, w), lambda i, chip_ref: (chip_ref[0], i, 0)),
                      pl.BlockSpec((3, tr, w), lambda i, chip_ref: (0, i, 0))],
            out_specs=pl.BlockSpec((tr, w), lambda i, chip_ref: (i, 0))),
        out_shape=jax.ShapeDtypeStruct((r, w), F32),
        compiler_params=_params(("parallel",)),
    )(chip, gp, land)


def _sibling_copy(src, got, send_sem, recv_sem):
    x, y, c = lax.axis_index("x"), lax.axis_index("y"), lax.axis_index("c")
    return pltpu.make_async_remote_copy(src_ref=src, dst_ref=got, send_sem=send_sem, recv_sem=recv_sem,
                                        device_id=(x, y, 1 - c), device_id_type=MESH)


def _swap_sibling(s, name):
    def body(src, got, send_sem, recv_sem):
        cp = _sibling_copy(src, got, send_sem, recv_sem)
        cp.start()
        cp.wait_recv()
        cp.wait_send()

    return pl.pallas_call(
        body, name=name, in_specs=[_HBM], out_specs=_HBM,
        out_shape=jax.ShapeDtypeStruct(s.shape, s.dtype),
        scratch_shapes=[pltpu.SemaphoreType.DMA(()), pltpu.SemaphoreType.DMA(())],
    )(s)


def _all_reduce_small(dsmall, loss):
    n_dev = 8
    names = tuple(_SMALL_SLOTS)
    shape = (8, D_MODEL)

    def body(*refs):
        parts, loss_ref = refs[:len(names)], refs[len(names)]
        out, src, gath, send_sems, recv_sems = refs[len(names) + 1:]
        x, y, c = lax.axis_index("x"), lax.axis_index("y"), lax.axis_index("c")
        me = 4 * x + 2 * y + c
        src[...] = jnp.zeros(shape, F32)
        for name, part in zip(names, parts):
            row, lane, _ = _SMALL_SLOTS[name]
            src[row:row + 1, lane:lane + part.shape[1]] = part[...]
        src[_LOSS_ROW:_LOSS_ROW + 1, 0:LANES] = loss_ref[...]
        gath[me] = src[...]
        peers = []
        for k in range(1, n_dev):
            px = 1 - x if (k >> 2) & 1 else x
            py = 1 - y if (k >> 1) & 1 else y
            pc = 1 - c if k & 1 else c
            peers.append((px, py, pc))
        sends = []
        for j, peer in enumerate(peers):
            cp = pltpu.make_async_remote_copy(src_ref=src, dst_ref=gath.at[me], send_sem=send_sems.at[j],
                                              recv_sem=recv_sems.at[j], device_id=peer, device_id_type=MESH)
            cp.start()
            sends.append(cp)
        for j, (px, py, pc) in enumerate(peers):
            pltpu.make_async_remote_copy(src_ref=src, dst_ref=gath.at[4 * px + 2 * py + pc], send_sem=send_sems.at[j],
                                         recv_sem=recv_sems.at[j], device_id=(px, py, pc), device_id_type=MESH).wait_recv()
        for cp in sends:
            cp.wait_send()
        acc = gath[0]
        for d in range(1, n_dev):
            acc = acc + gath[d]
        out[...] = acc

    vmem = pl.BlockSpec(memory_space=pltpu.VMEM)
    return pl.pallas_call(
        body, name="ar_small", in_specs=[vmem] * (len(names) + 1), out_specs=vmem,
        out_shape=jax.ShapeDtypeStruct(shape, F32),
        scratch_shapes=[pltpu.VMEM(shape, F32), pltpu.VMEM((n_dev,) + shape, F32),
                        pltpu.SemaphoreType.DMA((n_dev - 1,)), pltpu.SemaphoreType.DMA((n_dev - 1,))],
    )(*[dsmall[k] for k in names], loss)


_W_IN_ROWS = SHARD_SHAPES["w_in"][1]
_KR_ROW = 3200
_KR_PAD_ROW = _KR_BLK * LANES + QK_NOPE


def _shard_rows(name, a):
    return jnp.transpose(a) if name == "w_in" else a.reshape(PACK_ROWS[name], D_MODEL)


def _pack(group, shards, dtype):
    parts = [_shard_rows(n, shards[n]).astype(dtype) for n in group]
    pad = -sum(PACK_ROWS[n] for n in group) % LANES
    if pad:
        parts.append(jnp.zeros((pad, D_MODEL), dtype))
    return jnp.concatenate(parts, axis=0)


def _col_sharded_full(g, name, group):
    r, c = SHARD_SHAPES[name]
    off = _row_offset(group, name)
    blocks = g[:, off:off + PACK_ROWS[name]].reshape(N_CHIPS, r, c)
    return jnp.transpose(blocks, (1, 0, 2)).reshape(r, N_CHIPS * c)


def _col_sharded_blocks(d, name):
    r, c = SHARD_SHAPES[name]
    return jnp.transpose(d.reshape(r, N_CHIPS, c), (1, 0, 2)).reshape(N_CHIPS, PACK_ROWS[name], D_MODEL)


def _weights_a(g):
    dt = g.dtype
    w_in_t = g[:, :_W_IN_ROWS].reshape(N_CHIPS * _W_IN_ROWS, D_MODEL)
    z = lambda n: jnp.zeros((n, D_MODEL), dt)
    w_in_t = jnp.concatenate([w_in_t[:_KR_ROW], z(_KR_PAD_ROW - _KR_ROW), w_in_t[_KR_ROW:],
                              z(D_IN_PAD - _KR_PAD_ROW - QK_ROPE)], axis=0)
    wq = _col_sharded_full(g, "w_q_b", GROUP_A).reshape(Q_LORA, N_HEADS_B, Q_HEAD_B)
    wq_p = jnp.concatenate([wq, jnp.zeros((Q_LORA, N_HEADS_B, HEAD_PAD - Q_HEAD_B), dt)], axis=2).reshape(Q_LORA, MLA_W)
    wkv = _col_sharded_full(g, "w_kv_b", GROUP_A).reshape(KV_LORA, N_HEADS_B, QK_NOPE + V_DIM_B)
    zk = jnp.zeros((KV_LORA, N_HEADS_B, HEAD_PAD - QK_NOPE), dt)
    wk_p = jnp.concatenate([wkv[:, :, :QK_NOPE], zk], axis=2).reshape(KV_LORA, MLA_W)
    wv = wkv[:, :, QK_NOPE:].reshape(KV_LORA, N_HEADS_B * V_DIM_B)
    return dict(w_in=w_in_t, wq=wq_p, wk=wk_p, wv=wv, wv_t=jnp.transpose(wv))


def _grad_blocks_a(dw_in_t, dwq_p, dwk_p, dwv):
    dw_in = jnp.concatenate([dw_in_t[:_KR_ROW], dw_in_t[_KR_PAD_ROW:_KR_PAD_ROW + QK_ROPE]], axis=0)
    dwq = dwq_p.reshape(Q_LORA, N_HEADS_B, HEAD_PAD)[:, :, :Q_HEAD_B].reshape(Q_LORA, N_HEADS_B * Q_HEAD_B)
    dwk = dwk_p.reshape(KV_LORA, N_HEADS_B, HEAD_PAD)[:, :, :QK_NOPE]
    dwkv = jnp.concatenate([dwk, dwv.reshape(KV_LORA, N_HEADS_B, V_DIM_B)], axis=2)
    dwkv = dwkv.reshape(KV_LORA, N_HEADS_B * (QK_NOPE + V_DIM_B))
    pad = -sum(PACK_ROWS[n] for n in GROUP_A) % LANES
    return jnp.concatenate([dw_in.reshape(N_CHIPS, _W_IN_ROWS, D_MODEL), _col_sharded_blocks(dwq, "w_q_b"),
                            _col_sharded_blocks(dwkv, "w_kv_b"), jnp.zeros((N_CHIPS, pad, D_MODEL), F32)], axis=1)


def _rope_freq_lanes():
    freqs = ROPE_THETA ** (-jnp.arange(0, QK_ROPE, 2, dtype=F32) / QK_ROPE)
    return jnp.concatenate([jnp.zeros((QK_NOPE,), F32), freqs, freqs,
                            jnp.zeros((HEAD_PAD - Q_HEAD_B,), F32)]).reshape(1, LANES)


def _fwd_bwd(x, positions, target, w):
    t = x.shape[0]
    wa = _weights_a(_all_gather_chips(_pack(GROUP_A, w, BF16)))
    posr = positions.astype(F32).reshape(1, t)
    posc = posr.reshape(t, 1)
    freq = _rope_freq_lanes()
    g1, g2, g3, g4 = w["pre_norm_mix"], w["post_norm_mix"], w["pre_norm_mlp"], w["post_norm_mlp"]
    qan, kvan, sinks = w["q_a_norm"], w["kv_a_norm"], w["sinks"]

    h, proj = _proj_fwd(x, g1, wa["w_in"])
    out_a, lse_a = _swa_fwd(proj, posc, posr, sinks)
    qm, km, qt, kt, vt = _mla_prep_fwd(proj, posc, freq, qan, kvan, wa["wq"], wa["wk"], wa["wv_t"])
    out_bt, lse_b, wb = _mla_fwd(km, qt, vt, _pack(GROUP_B, w, BF16))
    w_oa, w_ob = _col_sharded_full(wb, "w_o_a", GROUP_B), _col_sharded_full(wb, "w_o_b", GROUP_B)
    merged, y, x1, h2 = _mix_out_fwd(out_a, out_bt, proj, x, w_oa, w_ob, wb, g2, g3)
    a = _up_fwd(h2, wb)
    dx2, dyd, dg4, loss = _down_fwd_loss(a, wb, x1, target, g4)

    gp_b = _dw_into_blocks(a, dyd, "w_down", 1024, _TK_DW)
    du = _down_bwd(dyd, wb, a)
    gp_b = _dw_into_blocks(h2, du, "w_up", 1024, _TK_DW, gp_b)
    dx1, dy, dg3, dg2 = _up_bwd(du, wb, x1, dx2, y, g3, g2)
    gp_b = _dw_into_blocks(merged, dy, "w_out", 1024, _TK_DW, gp_b)
    doa, dob, dga, dgb, d_out_a, d_out_b, d_out_bt, del_a, del_b = _mix_out_bwd(dy, out_a, out_bt, proj, w_oa, w_ob, wb)
    dw_oa = _matmul_tn(out_a, doa, "dw_o_a", 512, 1024)
    dw_ob = _dw_ob(out_bt, dob)
    small_b = jnp.concatenate([_col_sharded_blocks(dw_oa, "w_o_a"), _col_sharded_blocks(dw_ob, "w_o_b")], axis=1)
    gp_b = lax.dynamic_update_slice(gp_b, small_b, (0, _row_offset(GROUP_B, "w_o_a"), 0))
    dqm, dkm, dvm, land_b = _mla_bwd(qm, km, qt, kt, vt, d_out_b, d_out_bt, lse_b, del_b, gp_b)
    chip = (2 * lax.axis_index("x") + lax.axis_index("y")).astype(jnp.int32).reshape(1)
    part_b = _sum4(gp_b, land_b, chip, "rs_sum_b")
    dcq, dckv, dkr, dwq, dwk, dwv, dqan, dkvan, sib_b = _mla_prep_bwd(
        dqm, dkm, dvm, proj, posc, freq, qan, kvan, wa["wq"], wa["wk"], wa["wv"], part_b)
    dqa, dka, dva, dsinks = _swa_bwd(proj, d_out_a, lse_a, del_a, posc, posr, sinks)
    dproj = jnp.concatenate([dga, dgb, dqa.astype(BF16), dka.astype(BF16), dva.astype(BF16), dcq, dckv, dkr], axis=1)
    dw_in_t = _matmul_tn(dproj, h, "dw_in", D_IN_PAD // 2, 1024, tk=1024)
    gp_a = _grad_blocks_a(dw_in_t, dwq, dwk, dwv)
    grad_x, dg1, land_a = _in_bwd(dproj, wa["w_in"], x, dx1, g1, gp_a.astype(BF16))

    part_a = _sum4(gp_a, land_a, chip, "rs_sum_a")
    reduced = {GROUP_A: [part_a, _swap_sibling(part_a, "rs_swap_a")], GROUP_B: [part_b, sib_b]}
    dsmall = dict(pre_norm_mix=dg1, post_norm_mix=dg2, pre_norm_mlp=dg3, post_norm_mlp=dg4,
                  q_a_norm=dqan, kv_a_norm=dkvan, sinks=dsinks)
    return loss, grad_x, reduced, dsmall


def kernel(x, positions, pre_norm_mix, w_in, q_a_norm, w_q_b, kv_a_norm, w_kv_b, sinks, w_o_a, w_o_b, w_out, post_norm_mix, pre_norm_mlp, w_up, w_down, post_norm_mlp, loss_target, m_pre_norm_mix, m_w_in, m_q_a_norm, m_w_q_b, m_kv_a_norm, m_w_kv_b, m_sinks, m_w_o_a, m_w_o_b, m_w_out, m_post_norm_mix, m_pre_norm_mlp, m_w_up, m_w_down, m_post_norm_mlp, v_pre_norm_mix, v_w_in, v_q_a_norm, v_w_q_b, v_kv_a_norm, v_w_kv_b, v_sinks, v_w_o_a, v_w_o_b, v_w_out, v_post_norm_mix, v_pre_norm_mlp, v_w_up, v_w_down, v_post_norm_mlp):
    w = dict(pre_norm_mix=pre_norm_mix, w_in=w_in[0], q_a_norm=q_a_norm, w_q_b=w_q_b[0], kv_a_norm=kv_a_norm,
             w_kv_b=w_kv_b[0], sinks=sinks, w_o_a=w_o_a[0], w_o_b=w_o_b[0], w_out=w_out[0],
             post_norm_mix=post_norm_mix, pre_norm_mlp=pre_norm_mlp, w_up=w_up[0], w_down=w_down[0],
             post_norm_mlp=post_norm_mlp)
    m = dict(pre_norm_mix=m_pre_norm_mix, w_in=m_w_in[0], q_a_norm=m_q_a_norm, w_q_b=m_w_q_b[0],
             kv_a_norm=m_kv_a_norm, w_kv_b=m_w_kv_b[0], sinks=m_sinks, w_o_a=m_w_o_a[0], w_o_b=m_w_o_b[0],
             w_out=m_w_out[0], post_norm_mix=m_post_norm_mix, pre_norm_mlp=m_pre_norm_mlp, w_up=m_w_up[0],
             w_down=m_w_down[0], post_norm_mlp=m_post_norm_mlp)
    v = dict(pre_norm_mix=v_pre_norm_mix, w_in=v_w_in[0], q_a_norm=v_q_a_norm, w_q_b=v_w_q_b[0],
             kv_a_norm=v_kv_a_norm, w_kv_b=v_w_kv_b[0], sinks=v_sinks, w_o_a=v_w_o_a[0], w_o_b=v_w_o_b[0],
             w_out=v_w_out[0], post_norm_mix=v_post_norm_mix, pre_norm_mlp=v_pre_norm_mlp, w_up=v_w_up[0],
             w_down=v_w_down[0], post_norm_mlp=v_post_norm_mlp)

    loss, grad_x, reduced, dsmall = _fwd_bwd(x[0], positions, loss_target[0], w)

    red = _all_reduce_small(dsmall, loss)
    small = _adamw_small(red, w, m, v)

    big = {}
    tr = jnp.transpose
    big["w_in"] = [tr(o)[None] for o in _adamw(tr(w["w_in"]), reduced[GROUP_A], tr(m["w_in"]), tr(v["w_in"]),
                                               "adamw_w_in", (_W_IN_ROWS, 256))]
    for n in ("w_up", "w_down", "w_out"):
        big[n] = [o[None] for o in _adamw(w[n], reduced[GROUP_B], m[n], v[n], "adamw_" + n, (128, D_MODEL),
                                          _row_offset(GROUP_B, n))]
    for group, names in ((GROUP_A, ("w_q_b", "w_kv_b")), (GROUP_B, ("w_o_a", "w_o_b"))):
        for n in names:
            off = _row_offset(group, n)
            g_parts = [p[off:off + PACK_ROWS[n]].reshape(SHARD_SHAPES[n]) for p in reduced[group]]
            big[n] = [o[None] for o in _adamw(w[n], g_parts, m[n], v[n], "adamw_" + n, SHARD_SHAPES[n])]

    outs = [big[n][k] if n in big else small[n][k] for k in range(4) for n in WEIGHTS]
    return (red[_LOSS_ROW, 0], grad_x[None], *outs)
```

```python
import jax
import jax.numpy as jnp
from jax import lax
from jax.experimental import pallas as pl
from jax.experimental.pallas import tpu as pltpu

F32 = jnp.float32
BF16 = jnp.bfloat16
MESH = pl.DeviceIdType.MESH

D_MODEL = 1024
N_HEADS_A = 8
N_KV_A = 2
HEAD_DIM_A = 64
WINDOW = 128
BLOCK = 128
N_HEADS_B = 8
QK_NOPE = 64
QK_ROPE = 32
V_DIM_B = 64
Q_LORA = 256
KV_LORA = 128
ROPE_THETA = 10000.0
D_FF = 4 * D_MODEL
EPS = 1e-6
WIDTH_A = N_HEADS_A * HEAD_DIM_A
Q_HEAD_B = QK_NOPE + QK_ROPE
D_IN_PAD = 3328
HEAD_PAD = 128
MLA_W = N_HEADS_B * HEAD_PAD

ADAM_LR = 0.001
ADAM_B1 = 0.9
ADAM_B2 = 0.999
ADAM_EPS = 1e-08
ADAM_WD = 0.01
ADAM_STEP = 10

NEG = -1e30
N_CHIPS = 4
LANES = 128
VMEM_LIMIT = 56 * 1024 * 1024

SHARD_SHAPES = {"w_in": (1024, 808), "w_q_b": (256, 192), "w_kv_b": (128, 256), "w_o_a": (512, 256),
                "w_o_b": (512, 256), "w_out": (256, 1024), "w_up": (1024, 1024), "w_down": (1024, 1024)}
PACK_ROWS = {n: (s[0] * s[1]) // D_MODEL for n, s in SHARD_SHAPES.items()}
GROUP_A = ("w_in", "w_q_b", "w_kv_b")
GROUP_B = ("w_up", "w_down", "w_out", "w_o_a", "w_o_b")
WEIGHTS = ("pre_norm_mix", "w_in", "q_a_norm", "w_q_b", "kv_a_norm", "w_kv_b", "sinks", "w_o_a", "w_o_b", "w_out",
           "post_norm_mix", "pre_norm_mlp", "w_up", "w_down", "post_norm_mlp")


def _params(sem=None):
    return pltpu.CompilerParams(dimension_semantics=sem, vmem_limit_bytes=VMEM_LIMIT)


def _dot(a, b):
    return jnp.dot(a, b, preferred_element_type=F32)


def _dot_nt(a, b):
    return lax.dot_general(a, b, (((1,), (1,)), ((), ())), preferred_element_type=F32)


def _dot_tn(a, b):
    return lax.dot_general(a, b, (((0,), (0,)), ((), ())), preferred_element_type=F32)


def _rms(v):
    return lax.rsqrt(jnp.mean(v * v, axis=-1, keepdims=True) + EPS)


def _norm_bwd(dout, n, r, g):
    dn = dout * g
    dx = r * (dn - n * jnp.mean(dn * n, axis=-1, keepdims=True))
    return dx, jnp.sum(dout * n, axis=0, keepdims=True)


def _full(shape):
    return pl.BlockSpec(shape, lambda *_: (0,) * len(shape))


def _row_offset(group, name):
    return sum(PACK_ROWS[n] for n in group[:group.index(name)])


def _wb_spec(name):
    rows = PACK_ROWS[name]
    return pl.BlockSpec((N_CHIPS, rows, D_MODEL), lambda *_: (0, _row_offset(GROUP_B, name) // rows, 0))


def _proj_fwd(x, g1, w_in_t):
    t = x.shape[0]
    tm = 512

    def body(x_ref, g_ref, w_ref, h_ref, p_ref):
        for rows in _row_halves(tm):
            xv = x_ref[rows, :]
            h = ((xv * _rms(xv)) * g_ref[...]).astype(BF16)
            h_ref[rows, :] = h
            p_ref[rows, :] = _dot_nt(h, w_ref[...])

    return pl.pallas_call(
        body, name="proj_fwd", grid=(t // tm,),
        in_specs=[pl.BlockSpec((tm, D_MODEL), lambda i: (i, 0)), _full((1, D_MODEL)), _full((D_IN_PAD, D_MODEL))],
        out_specs=[pl.BlockSpec((tm, D_MODEL), lambda i: (i, 0)), pl.BlockSpec((tm, D_IN_PAD), lambda i: (i, 0))],
        out_shape=[jax.ShapeDtypeStruct((t, D_MODEL), BF16), jax.ShapeDtypeStruct((t, D_IN_PAD), F32)],
        compiler_params=_params(("parallel",)),
    )(x, g1, w_in_t)


_QA_BLK = 2048 // WIDTH_A
_KA_BLK = 2560 // LANES
_VA_BLK = 2688 // LANES
_CQ_BLK = 2816 // Q_LORA
_CKV_BLK = 3072 // LANES
_KR_BLK = 3200 // LANES


_GROUP_A = N_HEADS_A // N_KV_A
_SWA_SCALE = HEAD_DIM_A ** -0.5
_LOG2E = 1.4426950408889634


def _head_cols(v, h):
    return v[:, HEAD_DIM_A * h:HEAD_DIM_A * (h + 1)]


def _head_rows(v, h):
    return v[HEAD_DIM_A * h:HEAD_DIM_A * (h + 1), :]


def _swa_band(n, kp_ref, kc_ref, vp_ref, vc_ref, pq_ref, pp_ref, pc_ref):
    kb = jnp.concatenate([kp_ref[...], kc_ref[...]], axis=0)
    vb = jnp.concatenate([vp_ref[...], vc_ref[...]], axis=0)
    posk = jnp.concatenate([pp_ref[...], pc_ref[...]], axis=0)
    dist = jnp.abs(posk - pq_ref[...])
    ki = lax.broadcasted_iota(jnp.int32, (2 * BLOCK, BLOCK), 0)
    qi = lax.broadcasted_iota(jnp.int32, (2 * BLOCK, BLOCK), 1)
    valid = (ki > qi) & (ki <= qi + WINDOW) & ((n > 0) | (ki >= BLOCK))
    return kb, vb, dist, valid


def _swa_scores_t(st_g, j, h, dist, valid):
    slope = 2.0 ** (-8.0 * (h + 1) / N_HEADS_A)
    st = st_g[:, BLOCK * j:BLOCK * (j + 1)] * (_SWA_SCALE * _LOG2E) - (slope * _LOG2E) * dist
    return jnp.where(valid, st, NEG)


def _group_t(xt, kh):
    return jnp.concatenate([_head_rows(xt, _GROUP_A * kh + j) for j in range(_GROUP_A)], axis=1).astype(BF16)


def _swa_fwd(proj, posc, posr, sinks):
    t = proj.shape[0]
    nb = t // BLOCK

    def body(q_ref, kc_ref, kp_ref, vc_ref, vp_ref, pq_ref, pc_ref, pp_ref, sink_ref, o_ref, l_ref):
        n = pl.program_id(0)
        kb, vb, dist, valid = _swa_band(n, kp_ref, kc_ref, vp_ref, vc_ref, pq_ref, pp_ref, pc_ref)
        q_t, vb_t = q_ref[...].T, vb.T
        out_t, lse = [], []
        for kh in range(N_KV_A):
            st_g = _dot(_head_cols(kb, kh).astype(BF16), _group_t(q_t, kh))
            ps = []
            for j in range(_GROUP_A):
                h = _GROUP_A * kh + j
                st = _swa_scores_t(st_g, j, h, dist, valid)
                sink = sink_ref[0:1, h:h + 1] * _LOG2E
                m = jnp.maximum(jnp.max(st, axis=0, keepdims=True), sink)
                e = jnp.exp2(st - m)
                den = jnp.sum(e, axis=0, keepdims=True) + jnp.exp2(sink - m)
                ps.append((e * (1.0 / den)).astype(BF16))
                lse.append(m + jnp.log(den) * _LOG2E)
            o_g = _dot(_head_rows(vb_t, kh).astype(BF16), jnp.concatenate(ps, axis=1))
            out_t.extend(o_g[:, BLOCK * j:BLOCK * (j + 1)] for j in range(_GROUP_A))
        o_ref[...] = jnp.concatenate(out_t, axis=0).T
        l_ref[...] = jnp.concatenate(lse, axis=0)

    cur = lambda n: (n, 0)
    prev = lambda n: jnp.maximum(n - 1, 0)
    return pl.pallas_call(
        body, name="swa_fwd", grid=(nb,),
        in_specs=[pl.BlockSpec((BLOCK, WIDTH_A), lambda n: (n, _QA_BLK)),
                  pl.BlockSpec((BLOCK, LANES), lambda n: (n, _KA_BLK)),
                  pl.BlockSpec((BLOCK, LANES), lambda n: (prev(n), _KA_BLK)),
                  pl.BlockSpec((BLOCK, LANES), lambda n: (n, _VA_BLK)),
                  pl.BlockSpec((BLOCK, LANES), lambda n: (prev(n), _VA_BLK)),
                  pl.BlockSpec((1, BLOCK), lambda n: (0, n)),
                  pl.BlockSpec((BLOCK, 1), cur),
                  pl.BlockSpec((BLOCK, 1), lambda n: (prev(n), 0)),
                  _full((1, N_HEADS_A))],
        out_specs=[pl.BlockSpec((BLOCK, WIDTH_A), cur), pl.BlockSpec((N_HEADS_A, BLOCK), lambda n: (0, n))],
        out_shape=[jax.ShapeDtypeStruct((t, WIDTH_A), F32), jax.ShapeDtypeStruct((N_HEADS_A, t), F32)],
        compiler_params=_params(("parallel",)),
    )(proj, proj, proj, proj, proj, posr, posc, posc, sinks)


def _rope_coeffs(pos, freq):
    ang = pos * freq
    cosv, sinv = jnp.cos(ang), jnp.sin(ang)
    lane = lax.broadcasted_iota(jnp.int32, ang.shape, 1)
    lo = (lane >= QK_NOPE) & (lane < QK_NOPE + QK_ROPE // 2)
    hi = (lane >= QK_NOPE + QK_ROPE // 2) & (lane < QK_NOPE + QK_ROPE)
    c = jnp.where(lane < QK_NOPE, 1.0, jnp.where(lo | hi, cosv, 0.0))
    s = jnp.where(lo, -sinv, jnp.where(hi, sinv, 0.0))
    return c, s, lo, hi


def _rope(xh, c, s, lo):
    up = pltpu.roll(xh, LANES - QK_ROPE // 2, axis=1)
    dn = pltpu.roll(xh, QK_ROPE // 2, axis=1)
    return xh * c + jnp.where(lo, up, dn) * s


def _unrope(dh, c, s, lo, hi):
    g = dh * s
    up = pltpu.roll(g, LANES - QK_ROPE // 2, axis=1)
    dn = pltpu.roll(g, QK_ROPE // 2, axis=1)
    return dh * c + jnp.where(hi, dn, jnp.where(lo, up, 0.0))


_TQ = 512
_MLA_SCALE = Q_HEAD_B ** -0.5


def _mla_prep_fwd(proj, posc, freq, qan, kvan, wq, wk, wv):
    t = proj.shape[0]
    tm = _TQ
    nb = t // tm

    def body(cq_ref, ckv_ref, kr_ref, pos_ref, f_ref, qan_ref, kvan_ref, wq_ref, wk_ref, wv_ref,
             q_ref, k_ref, qt_ref, kt_ref, vt_ref):
        cq = cq_ref[...]
        cqn = ((cq * _rms(cq)) * qan_ref[...]).astype(BF16)
        ckv = ckv_ref[...]
        ckvn = ((ckv * _rms(ckv)) * kvan_ref[...]).astype(BF16)
        qb = _dot(cqn, wq_ref[...])
        kb = _dot(ckvn, wk_ref[...])
        vbt = _dot_nt(wv_ref[...], ckvn)
        c, s, lo, _ = _rope_coeffs(pos_ref[...], f_ref[...])
        kr = _rope(kr_ref[...], c, s, lo)
        for h in range(N_HEADS_B):
            sl = slice(HEAD_PAD * h, HEAD_PAD * (h + 1))
            q_h = _rope(qb[:, sl], c, s, lo)
            k_h = kb[:, sl] + kr
            q_ref[:, sl] = q_h.astype(BF16)
            k_ref[:, sl] = k_h.astype(BF16)
            qt_ref[h, 0] = q_h.T.astype(BF16)
            kt_ref[h, 0] = k_h.T.astype(BF16)
            vt_ref[h, 0] = vbt[V_DIM_B * h:V_DIM_B * (h + 1), :].astype(BF16)

    row = lambda i: (i, 0)
    blk4 = lambda d: pl.BlockSpec((N_HEADS_B, 1, d, tm), lambda i: (0, i, 0, 0))
    return pl.pallas_call(
        body, name="mla_prep_fwd", grid=(nb,),
        in_specs=[pl.BlockSpec((tm, Q_LORA), lambda i: (i, _CQ_BLK)),
                  pl.BlockSpec((tm, LANES), lambda i: (i, _CKV_BLK)),
                  pl.BlockSpec((tm, LANES), lambda i: (i, _KR_BLK)),
                  pl.BlockSpec((tm, 1), row), _full((1, LANES)), _full((1, Q_LORA)), _full((1, KV_LORA)),
                  _full((Q_LORA, MLA_W)), _full((KV_LORA, MLA_W)), _full((N_HEADS_B * V_DIM_B, KV_LORA))],
        out_specs=[pl.BlockSpec((tm, MLA_W), row), pl.BlockSpec((tm, MLA_W), row), blk4(HEAD_PAD), blk4(HEAD_PAD),
                   blk4(V_DIM_B)],
        out_shape=[jax.ShapeDtypeStruct((t, MLA_W), BF16), jax.ShapeDtypeStruct((t, MLA_W), BF16),
                   jax.ShapeDtypeStruct((N_HEADS_B, nb, HEAD_PAD, tm), BF16),
                   jax.ShapeDtypeStruct((N_HEADS_B, nb, HEAD_PAD, tm), BF16),
                   jax.ShapeDtypeStruct((N_HEADS_B, nb, V_DIM_B, tm), BF16)],
        compiler_params=_params(("parallel",)),
    )(proj, proj, proj, posc, freq, qan, kvan, wq, wk, wv)


_MLA_SCALE2 = _MLA_SCALE * _LOG2E


def _mla_fwd(k, qt, vt, w_src):
    t = k.shape[0]
    nb = t // _TQ

    def body(k_ref, qt_ref, vt_ref, w_ref, o_ref, l_ref, wg_ref, raw_a, raw_b, send_sems, recv_sems, local_sem):
        qi = pl.program_id(1)
        first = (pl.program_id(0) == 0) & (qi == 0)
        last = (pl.program_id(0) == N_HEADS_B - 1) & (qi == nb - 1)

        @pl.when(first)
        def _():
            _gather_start(w_ref, wg_ref, send_sems, recv_sems, local_sem)

        q_t = qt_ref[0, 0]

        def product(kj):
            return _dot(k_ref[pl.ds(pl.multiple_of(kj * _TQ, _TQ), _TQ), :], q_t)

        def update(stats, raw_ref, kj, diagonal=False):
            m, l, acc = stats
            raw = raw_ref[...]
            if diagonal:
                key = lax.broadcasted_iota(jnp.int32, raw.shape, 0)
                qry = lax.broadcasted_iota(jnp.int32, raw.shape, 1)
                raw = jnp.where(key <= qry, raw, NEG)
            m_new = jnp.maximum(m, jnp.max(raw, axis=0, keepdims=True) * _MLA_SCALE2)
            alpha = jnp.exp2(m - m_new)
            p = jnp.exp2(raw * _MLA_SCALE2 - m_new)
            l = alpha * l + jnp.sum(p, axis=0, keepdims=True)
            acc = alpha * acc + _dot(vt_ref[0, kj], p.astype(BF16))
            return m_new, l, acc

        def trip(i, stats):
            raw_b[...] = product(2 * i + 1)
            stats = update(stats, raw_a, 2 * i)
            raw_a[...] = product(2 * i + 2)
            return update(stats, raw_b, 2 * i + 1)

        def tail_even(stats):
            return update(stats, raw_a, qi, True)

        def tail_odd(stats):
            raw_b[...] = product(qi)
            return update(update(stats, raw_a, qi - 1), raw_b, qi, True)

        init = (jnp.full((1, _TQ), NEG, F32), jnp.zeros((1, _TQ), F32), jnp.zeros((V_DIM_B, _TQ), F32))
        raw_a[...] = product(0)
        stats = lax.fori_loop(0, qi // 2, trip, init)
        m, l, acc = lax.cond(qi % 2 == 0, tail_even, tail_odd, stats)
        o_ref[0, 0] = acc / l
        l_ref[0, 0] = m + jnp.log(l) * _LOG2E

        @pl.when(last)
        def _():
            _gather_wait(w_ref, wg_ref, send_sems, recv_sems, local_sem)

    return pl.pallas_call(
        body, name="mla_fwd", grid=(N_HEADS_B, nb),
        in_specs=[pl.BlockSpec((t, HEAD_PAD), lambda h, qi: (0, h)),
                  pl.BlockSpec((1, 1, HEAD_PAD, _TQ), lambda h, qi: (h, qi, 0, 0)),
                  pl.BlockSpec((1, nb, V_DIM_B, _TQ), lambda h, qi: (h, 0, 0, 0)), _HBM],
        out_specs=[pl.BlockSpec((1, 1, V_DIM_B, _TQ), lambda h, qi: (h, qi, 0, 0)),
                   pl.BlockSpec((1, 1, 1, _TQ), lambda h, qi: (h, qi, 0, 0)), _HBM],
        out_shape=[jax.ShapeDtypeStruct((N_HEADS_B, nb, V_DIM_B, _TQ), F32),
                   jax.ShapeDtypeStruct((N_HEADS_B, nb, 1, _TQ), F32),
                   jax.ShapeDtypeStruct((N_CHIPS,) + w_src.shape, w_src.dtype)],
        scratch_shapes=[pltpu.VMEM((_TQ, _TQ), F32), pltpu.VMEM((_TQ, _TQ), F32),
                        pltpu.SemaphoreType.DMA((3,)), pltpu.SemaphoreType.DMA((3,)), pltpu.SemaphoreType.DMA(())],
        compiler_params=_params(("arbitrary", "arbitrary")),
    )(k, qt, vt, w_src)


def _ot_spec(tm, d):
    per = _TQ // tm
    return pl.BlockSpec((N_HEADS_B, 1, d, tm), lambda i: (0, i // per, 0, i % per))


def _mix_out_fwd(out_a, out_bt, proj, x, w_oa, w_ob, wb, g2, g3):
    t = x.shape[0]
    tm = 256

    def body(oa_ref, obt_ref, ga_ref, gb_ref, x_ref, woa_ref, wob_ref, wout_ref, g2_ref, g3_ref,
             mg_ref, y_ref, x1_ref, h2_ref):
        oa = _dot(oa_ref[...].astype(BF16), woa_ref[...])
        obt = obt_ref[...].reshape(N_HEADS_B * V_DIM_B, tm).astype(BF16)
        ob = _dot_tn(obt, wob_ref[...])
        merged = (jax.nn.sigmoid(ga_ref[...]) * oa + jax.nn.sigmoid(gb_ref[...]) * ob).astype(BF16)
        mg_ref[...] = merged
        y = _dot(merged, wout_ref[...].reshape(D_MODEL, D_MODEL))
        y_ref[...] = y
        x1 = x_ref[...] + (y * _rms(y)) * g2_ref[...]
        x1_ref[...] = x1
        h2_ref[...] = ((x1 * _rms(x1)) * g3_ref[...]).astype(BF16)

    row = lambda i: (i, 0)
    blk = pl.BlockSpec((tm, D_MODEL), row)
    return pl.pallas_call(
        body, name="mix_out_fwd", grid=(t // tm,),
        in_specs=[pl.BlockSpec((tm, WIDTH_A), row), _ot_spec(tm, V_DIM_B), pl.BlockSpec((tm, D_MODEL), lambda i: (i, 0)),
                  pl.BlockSpec((tm, D_MODEL), lambda i: (i, 1)), blk,
                  _full((WIDTH_A, D_MODEL)), _full((N_HEADS_B * V_DIM_B, D_MODEL)), _wb_spec("w_out"),
                  _full((1, D_MODEL)), _full((1, D_MODEL))],
        out_specs=[blk, blk, blk, blk],
        out_shape=[jax.ShapeDtypeStruct((t, D_MODEL), BF16), jax.ShapeDtypeStruct((t, D_MODEL), F32),
                   jax.ShapeDtypeStruct((t, D_MODEL), F32), jax.ShapeDtypeStruct((t, D_MODEL), BF16)],
        compiler_params=_params(("parallel",)),
    )(out_a, out_bt, proj, proj, x, w_oa, w_ob, wb, g2, g3)


_TM_MLP = 512


def _row_halves(tm):
    return slice(0, tm // 2), slice(tm // 2, tm)


def _up_fwd(h2, wb):
    t = h2.shape[0]
    tm = _TM_MLP

    def body(h_ref, w_ref, a_ref):
        hv = h_ref[...]
        for j in range(N_CHIPS):
            u = _dot(hv, w_ref[j])
            a_ref[:, D_MODEL * j:D_MODEL * (j + 1)] = jnp.square(jnp.maximum(u, 0.0)).astype(BF16)

    return pl.pallas_call(
        body, name="up_fwd", grid=(t // tm,),
        in_specs=[pl.BlockSpec((tm, D_MODEL), lambda i: (i, 0)), _wb_spec("w_up")],
        out_specs=pl.BlockSpec((tm, D_FF), lambda i: (i, 0)),
        out_shape=jax.ShapeDtypeStruct((t, D_FF), BF16),
        compiler_params=_params(("parallel",)),
    )(h2, wb)


def _down_fwd_loss(a, wb, x1, target, g4):
    t = a.shape[0]
    tm = _TM_MLP

    def body(a_ref, w_ref, x1_ref, tg_ref, g_ref, dx2_ref, dyd_ref, dg_ref, loss_ref):
        @pl.when(pl.program_id(0) == 0)
        def _():
            dg_ref[...] = jnp.zeros(dg_ref.shape, F32)
            loss_ref[...] = jnp.zeros(loss_ref.shape, F32)

        yd = _dot(a_ref[...], w_ref[...].reshape(D_FF, D_MODEL))
        r = _rms(yd)
        n = yd * r
        diff = (x1_ref[...] + n * g_ref[...]) - tg_ref[...]
        loss_ref[...] += 0.5 * jnp.sum(jnp.mean(diff * diff, axis=-1, keepdims=True), axis=0, keepdims=True)
        dx2 = diff * (1.0 / D_MODEL)
        dx2_ref[...] = dx2
        dyd, dg = _norm_bwd(dx2, n, r, g_ref[...])
        dyd_ref[...] = dyd.astype(BF16)
        dg_ref[...] += dg

    row = lambda i: (i, 0)
    blk = pl.BlockSpec((tm, D_MODEL), row)
    return pl.pallas_call(
        body, name="down_fwd_loss", grid=(t // tm,),
        in_specs=[pl.BlockSpec((tm, D_FF), row), _wb_spec("w_down"), blk, blk, _full((1, D_MODEL))],
        out_specs=[blk, blk, _full((1, D_MODEL)), _full((1, LANES))],
        out_shape=[jax.ShapeDtypeStruct((t, D_MODEL), F32), jax.ShapeDtypeStruct((t, D_MODEL), BF16),
                   jax.ShapeDtypeStruct((1, D_MODEL), F32), jax.ShapeDtypeStruct((1, LANES), F32)],
        compiler_params=_params(("arbitrary",)),
    )(a, wb, x1, target, g4)


def _matmul_tn(a, b, name, tm, tn, tk=1024):
    t, m = a.shape
    n = b.shape[1]
    tk = min(tk, t)
    nk = t // tk

    def body(a_ref, b_ref, o_ref):
        @pl.when(pl.program_id(2) == 0)
        def _():
            o_ref[...] = jnp.zeros(o_ref.shape, F32)

        o_ref[...] += _dot_tn(a_ref[...].astype(BF16), b_ref[...].astype(BF16))

    return pl.pallas_call(
        body, name=name, grid=(m // tm, n // tn, nk),
        in_specs=[pl.BlockSpec((tk, tm), lambda i, j, k: (k, i)), pl.BlockSpec((tk, tn), lambda i, j, k: (k, j))],
        out_specs=pl.BlockSpec((tm, tn), lambda i, j, k: (i, j)),
        out_shape=jax.ShapeDtypeStruct((m, n), F32),
        compiler_params=_params(("parallel", "parallel", "arbitrary")),
    )(a, b)


_TK_DW = 2048


def _dw_into_blocks(a, b, weight, tm, tk, buf=None):
    t, m = a.shape
    n = b.shape[1]
    tk = min(tk, t)
    nk = t // tk
    rows = PACK_ROWS[weight]
    br = min(tm, rows)
    chips = tm // br
    first = _row_offset(GROUP_B, weight) // br
    per_chip = rows // br
    if weight == "w_up":
        out_map = lambda i, j, k: (j, first + i, 0)
    elif chips > 1:
        out_map = lambda i, j, k: (i, first, 0)
    else:
        out_map = lambda i, j, k: (i // per_chip, first + i % per_chip, 0)

    def body(a_ref, b_ref, *rest):
        o_ref = rest[-1]

        @pl.when(pl.program_id(2) == 0)
        def _():
            o_ref[...] = jnp.zeros(o_ref.shape, F32)

        o_ref[...] += _dot_tn(a_ref[...].astype(BF16), b_ref[...].astype(BF16)).reshape(o_ref.shape)

    in_specs = [pl.BlockSpec((tk, tm), lambda i, j, k: (k, i)), pl.BlockSpec((tk, D_MODEL), lambda i, j, k: (k, j))]
    operands = [a, b]
    if buf is not None:
        in_specs.append(pl.BlockSpec(memory_space=pl.ANY))
        operands.append(buf)
    total = sum(PACK_ROWS[w] for w in GROUP_B)
    return pl.pallas_call(
        body, name="dw_" + weight[2:], grid=(m // tm, n // D_MODEL, nk),
        in_specs=in_specs, out_specs=pl.BlockSpec((chips, br, D_MODEL), out_map),
        out_shape=jax.ShapeDtypeStruct((N_CHIPS, total, D_MODEL), F32),
        input_output_aliases={} if buf is None else {2: 0},
        compiler_params=_params(("parallel", "parallel", "arbitrary")),
    )(*operands)


def _down_bwd(dyd, wb, a):
    t = dyd.shape[0]
    tm = _TM_MLP

    def body(d_ref, w_ref, a_ref, du_ref):
        dv = d_ref[...]
        for j in range(N_CHIPS):
            cols = slice(D_MODEL * j, D_MODEL * (j + 1))
            av = a_ref[:, cols].astype(F32)
            relu_u = jnp.where(av > 0.0, av * lax.rsqrt(av), 0.0)
            du_ref[:, cols] = (_dot_nt(dv, w_ref[j]) * (2.0 * relu_u)).astype(BF16)

    row = lambda i: (i, 0)
    return pl.pallas_call(
        body, name="down_bwd", grid=(t // tm,),
        in_specs=[pl.BlockSpec((tm, D_MODEL), row), _wb_spec("w_down"), pl.BlockSpec((tm, D_FF), row)],
        out_specs=pl.BlockSpec((tm, D_FF), row),
        out_shape=jax.ShapeDtypeStruct((t, D_FF), BF16),
        compiler_params=_params(("parallel",)),
    )(dyd, wb, a)


def _up_bwd(du, wb, x1, dx2, y, g3, g2):
    t = du.shape[0]
    tm = _TM_MLP

    def body(du_ref, w_ref, x1_ref, dx2_ref, y_ref, g3_ref, g2_ref, dx1_ref, dy_ref, dg3_ref, dg2_ref):
        @pl.when(pl.program_id(0) == 0)
        def _():
            dg3_ref[...] = jnp.zeros(dg3_ref.shape, F32)
            dg2_ref[...] = jnp.zeros(dg2_ref.shape, F32)

        dh2 = _dot_nt(du_ref[:, 0:D_MODEL], w_ref[0])
        for j in range(1, N_CHIPS):
            dh2 = dh2 + _dot_nt(du_ref[:, D_MODEL * j:D_MODEL * (j + 1)], w_ref[j])
        x1 = x1_ref[...]
        r3 = _rms(x1)
        d3, dg3 = _norm_bwd(dh2, x1 * r3, r3, g3_ref[...])
        dx1 = dx2_ref[...] + d3
        dx1_ref[...] = dx1
        dg3_ref[...] += dg3
        y = y_ref[...]
        r2 = _rms(y)
        dy, dg2 = _norm_bwd(dx1, y * r2, r2, g2_ref[...])
        dy_ref[...] = dy.astype(BF16)
        dg2_ref[...] += dg2

    row = lambda i: (i, 0)
    blk = pl.BlockSpec((tm, D_MODEL), row)
    return pl.pallas_call(
        body, name="up_bwd", grid=(t // tm,),
        in_specs=[pl.BlockSpec((tm, D_FF), row), _wb_spec("w_up"),
                  blk, blk, blk, _full((1, D_MODEL)), _full((1, D_MODEL))],
        out_specs=[blk, blk, _full((1, D_MODEL)), _full((1, D_MODEL))],
        out_shape=[jax.ShapeDtypeStruct((t, D_MODEL), F32), jax.ShapeDtypeStruct((t, D_MODEL), BF16),
                   jax.ShapeDtypeStruct((1, D_MODEL), F32), jax.ShapeDtypeStruct((1, D_MODEL), F32)],
        compiler_params=_params(("arbitrary",)),
    )(du, wb, x1, dx2, y, g3, g2)


def _mix_out_bwd(dy, out_a, out_bt, proj, w_oa, w_ob, wb):
    t = dy.shape[0]
    tm = 256
    nb = t // _TQ

    def body(dy_ref, oa_ref, obt_ref, ga_ref, gb_ref, woa_ref, wob_ref, wout_ref,
             doa_ref, dob_ref, dga_ref, dgb_ref, da_ref, db_ref, dbt_ref, dela_ref, delb_ref):
        dm = _dot_nt(dy_ref[...], wout_ref[...].reshape(D_MODEL, D_MODEL))
        out_a_v = oa_ref[...]
        out_bt_v = obt_ref[...].reshape(N_HEADS_B * V_DIM_B, tm)
        oa = _dot(out_a_v.astype(BF16), woa_ref[...])
        ob = _dot_tn(out_bt_v.astype(BF16), wob_ref[...])
        sa, sb = jax.nn.sigmoid(ga_ref[...]), jax.nn.sigmoid(gb_ref[...])
        doa = (dm * sa).astype(BF16)
        dob = (dm * sb).astype(BF16)
        doa_ref[...] = doa
        dob_ref[...] = dob
        dga_ref[...] = (dm * oa * (sa * (1.0 - sa))).astype(BF16)
        dgb_ref[...] = (dm * ob * (sb * (1.0 - sb))).astype(BF16)
        d_out_a = _dot_nt(doa, woa_ref[...])
        da_ref[...] = d_out_a
        prod_at = (d_out_a * out_a_v).T
        dela_ref[...] = jnp.concatenate(
            [jnp.sum(_head_rows(prod_at, h), axis=0, keepdims=True) for h in range(N_HEADS_A)], axis=0)
        d_out_b = _dot_nt(dob, wob_ref[...])
        d_out_bt = _dot_nt(wob_ref[...], dob)
        prod_bt = d_out_bt * out_bt_v
        for h in range(N_HEADS_B):
            db_ref[h] = d_out_b[:, V_DIM_B * h:V_DIM_B * (h + 1)].astype(BF16)
            dbt_ref[h, 0] = d_out_bt[V_DIM_B * h:V_DIM_B * (h + 1), :].astype(BF16)
            delb_ref[h, 0] = jnp.sum(prod_bt[V_DIM_B * h:V_DIM_B * (h + 1), :], axis=0, keepdims=True)

    row = lambda i: (i, 0)
    blk = pl.BlockSpec((tm, D_MODEL), row)
    return pl.pallas_call(
        body, name="mix_out_bwd", grid=(t // tm,),
        in_specs=[blk, pl.BlockSpec((tm, WIDTH_A), row), _ot_spec(tm, V_DIM_B),
                  pl.BlockSpec((tm, D_MODEL), lambda i: (i, 0)), pl.BlockSpec((tm, D_MODEL), lambda i: (i, 1)),
                  _full((WIDTH_A, D_MODEL)), _full((N_HEADS_B * V_DIM_B, D_MODEL)), _wb_spec("w_out")],
        out_specs=[blk, blk, blk, blk, pl.BlockSpec((tm, WIDTH_A), row),
                   pl.BlockSpec((N_HEADS_B, tm, V_DIM_B), lambda i: (0, i, 0)), _ot_spec(tm, V_DIM_B),
                   pl.BlockSpec((N_HEADS_A, tm), lambda i: (0, i)), _ot_spec(tm, 1)],
        out_shape=[jax.ShapeDtypeStruct((t, D_MODEL), BF16)] * 4
        + [jax.ShapeDtypeStruct((t, WIDTH_A), F32), jax.ShapeDtypeStruct((N_HEADS_B, t, V_DIM_B), BF16),
           jax.ShapeDtypeStruct((N_HEADS_B, nb, V_DIM_B, _TQ), BF16), jax.ShapeDtypeStruct((N_HEADS_A, t), F32),
           jax.ShapeDtypeStruct((N_HEADS_B, nb, 1, _TQ), F32)],
        compiler_params=_params(("parallel",)),
    )(dy, out_a, out_bt, proj, proj, w_oa, w_ob, wb)


def _dw_ob(out_bt, dob):
    t = dob.shape[0]
    nb = t // _TQ

    def body(obt_ref, dob_ref, o_ref):
        @pl.when(pl.program_id(0) == 0)
        def _():
            o_ref[...] = jnp.zeros(o_ref.shape, F32)

        obt = obt_ref[...].reshape(N_HEADS_B * V_DIM_B, _TQ).astype(BF16)
        o_ref[...] += _dot(obt, dob_ref[...])

    return pl.pallas_call(
        body, name="dw_o_b", grid=(nb,),
        in_specs=[pl.BlockSpec((N_HEADS_B, 1, V_DIM_B, _TQ), lambda i: (0, i, 0, 0)),
                  pl.BlockSpec((_TQ, D_MODEL), lambda i: (i, 0))],
        out_specs=_full((N_HEADS_B * V_DIM_B, D_MODEL)),
        out_shape=jax.ShapeDtypeStruct((N_HEADS_B * V_DIM_B, D_MODEL), F32),
        compiler_params=_params(("arbitrary",)),
    )(out_bt, dob)


def _mla_bwd(q, k, qt, kt, vt, d_out, d_out_t, lse, delta, gp):
    t = q.shape[0]
    nb = t // _TQ

    def body(k_ref, kt_ref, vt_ref, q_ref, qt_ref, do_ref, dot_ref, lrow_ref, drow_ref, gp_ref,
             dq_ref, dkt_ref, dvt_ref, land_ref, l_rep, d_rep, send_sems, recv_sems):
        step = pl.program_id(1)
        kj = nb - 1 - step

        @pl.when((pl.program_id(0) == 0) & (step == 0))
        def _():
            _scatter_start(gp_ref, land_ref, send_sems, recv_sems)

        @pl.when(step == 0)
        def _():
            dq_ref[...] = jnp.zeros(dq_ref.shape, F32)
            for b in range(nb):
                l_rep[_TQ * b:_TQ * (b + 1), :] = jnp.broadcast_to(lrow_ref[0, b], (LANES, _TQ)).T
                d_rep[_TQ * b:_TQ * (b + 1), :] = jnp.broadcast_to(drow_ref[0, b], (LANES, _TQ)).T

        kv, k_t, v_t = k_ref[...], kt_ref[0, 0], vt_ref[0, 0]

        def rows_of(qi):
            return pl.ds(pl.multiple_of(qi * _TQ, _TQ), _TQ)

        def products(qi, diagonal=False):
            s = _dot(q_ref[rows_of(qi), :], k_t) * _MLA_SCALE2
            if diagonal:
                qry = lax.broadcasted_iota(jnp.int32, s.shape, 0)
                key = lax.broadcasted_iota(jnp.int32, s.shape, 1)
                s = jnp.where(key <= qry, s, NEG)
            return s, _dot(do_ref[0, rows_of(qi), :], v_t)

        def update(carry, prods, qi):
            dkt, dvt = carry
            s, dp = prods
            lse, delta = l_rep[rows_of(qi), :], d_rep[rows_of(qi), :]
            ps, dss = [], []
            for c in range(_TQ // LANES):
                strip = slice(LANES * c, LANES * (c + 1))
                p = jnp.exp2(s[:, strip] - lse)
                ps.append(p.astype(BF16))
                dss.append((p * (dp[:, strip] - delta) * _MLA_SCALE).astype(BF16))
            p_b, ds_b = jnp.concatenate(ps, axis=1), jnp.concatenate(dss, axis=1)
            dvt = dvt + _dot(dot_ref[0, qi], p_b)
            dkt = dkt + _dot(qt_ref[0, qi], ds_b)
            dq_ref[rows_of(qi), :] += _dot(ds_b, kv)
            return dkt, dvt

        def pair(i, carry):
            qa = kj + 1 + 2 * i
            pa, pb = products(qa), products(qa + 1)
            return update(update(carry, pa, qa), pb, qa + 1)

        init = (jnp.zeros((HEAD_PAD, _TQ), F32), jnp.zeros((V_DIM_B, _TQ), F32))
        carry = update(init, products(kj, True), kj)
        pairs = (nb - 1 - kj) // 2
        carry = lax.fori_loop(0, pairs, pair, carry)
        dkt, dvt = lax.fori_loop(kj + 1 + 2 * pairs, nb, lambda qi, cr: update(cr, products(qi), qi), carry)
        dkt_ref[0, 0] = dkt
        dvt_ref[0, 0] = dvt

        @pl.when((pl.program_id(0) == N_HEADS_B - 1) & (step == nb - 1))
        def _():
            _scatter_wait(gp_ref, land_ref, send_sems, recv_sems)

    head4 = lambda d: pl.BlockSpec((1, nb, d, _TQ), lambda h, s: (h, 0, 0, 0))
    blk4 = lambda d: pl.BlockSpec((1, 1, d, _TQ), lambda h, s: (h, nb - 1 - s, 0, 0))
    head3 = lambda d: pl.BlockSpec((1, t, d), lambda h, kj: (h, 0, 0))
    per_head = pl.BlockSpec((t, HEAD_PAD), lambda h, kj: (0, h))
    return pl.pallas_call(
        body, name="mla_bwd", grid=(N_HEADS_B, nb),
        in_specs=[pl.BlockSpec((_TQ, HEAD_PAD), lambda h, s: (nb - 1 - s, h)), blk4(HEAD_PAD), blk4(V_DIM_B),
                  per_head, head4(HEAD_PAD), head3(V_DIM_B), head4(V_DIM_B), head4(1), head4(1), _HBM],
        out_specs=[per_head, blk4(HEAD_PAD), blk4(V_DIM_B), _HBM],
        out_shape=[jax.ShapeDtypeStruct((t, MLA_W), F32), jax.ShapeDtypeStruct((N_HEADS_B, nb, HEAD_PAD, _TQ), F32),
                   jax.ShapeDtypeStruct((N_HEADS_B, nb, V_DIM_B, _TQ), F32),
                   jax.ShapeDtypeStruct((3,) + gp.shape[1:], gp.dtype)],
        scratch_shapes=[pltpu.VMEM((t, LANES), F32), pltpu.VMEM((t, LANES), F32),
                        pltpu.SemaphoreType.DMA((3,)), pltpu.SemaphoreType.DMA((3,))],
        compiler_params=_params(("arbitrary", "arbitrary")),
    )(k, kt, vt, q, qt, d_out, d_out_t, lse, delta, gp)


def _mla_prep_bwd(dq, dkt, dvt, proj, posc, freq, qan, kvan, wq, wk, wv, swap_src):
    t = dq.shape[0]
    tm = _TQ

    def body(dq_ref, dkt_ref, dvt_ref, cq_ref, ckv_ref, pos_ref, f_ref, qan_ref, kvan_ref, wq_ref, wk_ref, wv_ref, src_ref,
             dcq_ref, dckv_ref, dkr_ref, dwq_ref, dwk_ref, dwv_ref, dqan_ref, dkvan_ref, got_ref, send_sem, recv_sem):
        swap = _sibling_copy(src_ref, got_ref, send_sem, recv_sem)

        @pl.when(pl.program_id(0) == 0)
        def _():
            swap.start()
            for r in (dwq_ref, dwk_ref, dwv_ref, dqan_ref, dkvan_ref):
                r[...] = jnp.zeros(r.shape, F32)

        cq = cq_ref[...]
        rq = _rms(cq)
        nq_ = cq * rq
        cqn = (nq_ * qan_ref[...]).astype(BF16)
        ckv = ckv_ref[...]
        rkv = _rms(ckv)
        nkv = ckv * rkv
        ckvn = (nkv * kvan_ref[...]).astype(BF16)
        c, s, lo, hi = _rope_coeffs(pos_ref[...], f_ref[...])
        dkr = jnp.zeros((tm, LANES), F32)
        dqb, dkb = [], []
        for h in range(N_HEADS_B):
            dqb.append(_unrope(dq_ref[:, HEAD_PAD * h:HEAD_PAD * (h + 1)], c, s, lo, hi).astype(BF16))
            dk_h = dkt_ref[h, 0].T
            dkr = dkr + dk_h
            dkb.append(dk_h.astype(BF16))
        dqb, dkb = jnp.concatenate(dqb, axis=1), jnp.concatenate(dkb, axis=1)
        dkr_ref[...] = jnp.where(lo | hi, _unrope(dkr, c, s, lo, hi), 0.0).astype(BF16)
        dvb = dvt_ref[...].reshape(N_HEADS_B * V_DIM_B, tm).T.astype(BF16)
        dwq_ref[...] += _dot_tn(cqn, dqb)
        dwk_ref[...] += _dot_tn(ckvn, dkb)
        dwv_ref[...] += _dot_tn(ckvn, dvb)
        dcqn = _dot_nt(dqb, wq_ref[...])
        dckvn = _dot_nt(dkb, wk_ref[...]) + _dot_nt(dvb, wv_ref[...])
        dcq, dqan = _norm_bwd(dcqn, nq_, rq, qan_ref[...])
        dckv, dkvan = _norm_bwd(dckvn, nkv, rkv, kvan_ref[...])
        dcq_ref[...] = dcq.astype(BF16)
        dckv_ref[...] = dckv.astype(BF16)
        dqan_ref[...] += dqan
        dkvan_ref[...] += dkvan

        @pl.when(pl.program_id(0) == t // tm - 1)
        def _():
            swap.wait_recv()
            swap.wait_send()

    row = lambda i: (i, 0)
    vw = N_HEADS_B * V_DIM_B
    return pl.pallas_call(
        body, name="mla_prep_bwd", grid=(t // tm,),
        in_specs=[pl.BlockSpec((tm, MLA_W), row), pl.BlockSpec((N_HEADS_B, 1, HEAD_PAD, tm), lambda i: (0, i, 0, 0)),
                  pl.BlockSpec((N_HEADS_B, 1, V_DIM_B, tm), lambda i: (0, i, 0, 0)),
                  pl.BlockSpec((tm, Q_LORA), lambda i: (i, _CQ_BLK)),
                  pl.BlockSpec((tm, LANES), lambda i: (i, _CKV_BLK)),
                  pl.BlockSpec((tm, 1), row), _full((1, LANES)), _full((1, Q_LORA)), _full((1, KV_LORA)),
                  _full((Q_LORA, MLA_W)), _full((KV_LORA, MLA_W)), _full((KV_LORA, vw)), _HBM],
        out_specs=[pl.BlockSpec((tm, Q_LORA), row), pl.BlockSpec((tm, LANES), row), pl.BlockSpec((tm, LANES), row),
                   _full((Q_LORA, MLA_W)), _full((KV_LORA, MLA_W)), _full((KV_LORA, vw)),
                   _full((1, Q_LORA)), _full((1, KV_LORA)), _HBM],
        out_shape=[jax.ShapeDtypeStruct((t, Q_LORA), BF16), jax.ShapeDtypeStruct((t, LANES), BF16),
                   jax.ShapeDtypeStruct((t, LANES), BF16),
                   jax.ShapeDtypeStruct((Q_LORA, MLA_W), F32), jax.ShapeDtypeStruct((KV_LORA, MLA_W), F32),
                   jax.ShapeDtypeStruct((KV_LORA, vw), F32),
                   jax.ShapeDtypeStruct((1, Q_LORA), F32), jax.ShapeDtypeStruct((1, KV_LORA), F32),
                   jax.ShapeDtypeStruct(swap_src.shape, swap_src.dtype)],
        scratch_shapes=[pltpu.SemaphoreType.DMA(()), pltpu.SemaphoreType.DMA(())],
        compiler_params=_params(("arbitrary",)),
    )(dq, dkt, dvt, proj, proj, posc, freq, qan, kvan, wq, wk, wv, swap_src)


def _swa_bwd(proj, d_out, lse, delta, posc, posr, sinks):
    t = proj.shape[0]
    nb = t // BLOCK

    def body(q_ref, kc_ref, kp_ref, vc_ref, vp_ref, do_ref, l_ref, d_ref, pq_ref, pc_ref, pp_ref, sink_ref,
             dq_ref, dk_ref, dv_ref, ds_ref, dkb_s, dvb_s, dk_carry, dv_carry):
        n = pl.program_id(0)

        @pl.when(n == 0)
        def _():
            ds_ref[...] = jnp.zeros(ds_ref.shape, F32)
            dk_carry[...] = jnp.zeros(dk_carry.shape, F32)
            dv_carry[...] = jnp.zeros(dv_carry.shape, F32)

        @pl.when(n < nb)
        def _():
            kb, vb, dist, valid = _swa_band(n, kp_ref, kc_ref, vp_ref, vc_ref, pq_ref, pp_ref, pc_ref)
            qv, dov = q_ref[...], do_ref[...]
            q_t, do_t, kb_t = qv.T, dov.T, kb.T
            lane = lax.broadcasted_iota(jnp.int32, (1, LANES), 1)
            dsink = jnp.zeros((1, LANES), F32)
            dq_t = []
            for kh in range(N_KV_A):
                heads = range(_GROUP_A * kh, _GROUP_A * (kh + 1))
                st_g = _dot(_head_cols(kb, kh).astype(BF16), _group_t(q_t, kh))
                dpt_g = _dot(_head_cols(vb, kh).astype(BF16), _group_t(do_t, kh))
                pts, dsts = [], []
                for j, h in enumerate(heads):
                    st = _swa_scores_t(st_g, j, h, dist, valid)
                    l_h, d_h = l_ref[h:h + 1, :], d_ref[h:h + 1, :]
                    pt = jnp.exp2(st - l_h)
                    p_sink = jnp.exp2(sink_ref[0:1, h:h + 1] * _LOG2E - l_h)
                    dsink = jnp.where(lane == h, jnp.sum(-p_sink * d_h, axis=1, keepdims=True), dsink)
                    dst = pt * (dpt_g[:, BLOCK * j:BLOCK * (j + 1)] - d_h) * _SWA_SCALE
                    pts.append(pt.astype(BF16))
                    dsts.append(dst.astype(BF16))
                pt_g, dst_g = jnp.concatenate(pts, axis=1), jnp.concatenate(dsts, axis=1)
                q_g = jnp.concatenate([_head_cols(qv, h) for h in heads], axis=0).astype(BF16)
                do_g = jnp.concatenate([_head_cols(dov, h) for h in heads], axis=0).astype(BF16)
                dkb_s[:, HEAD_DIM_A * kh:HEAD_DIM_A * (kh + 1)] = _dot(dst_g, q_g)
                dvb_s[:, HEAD_DIM_A * kh:HEAD_DIM_A * (kh + 1)] = _dot(pt_g, do_g)
                dq_g = _dot(_head_rows(kb_t, kh).astype(BF16), dst_g)
                dq_t.extend(dq_g[:, BLOCK * j:BLOCK * (j + 1)] for j in range(_GROUP_A))
            dq_ref[...] = jnp.concatenate(dq_t, axis=0).T
            ds_ref[...] += dsink
            dk_ref[...] = dk_carry[...] + dkb_s[0:BLOCK, :]
            dv_ref[...] = dv_carry[...] + dvb_s[0:BLOCK, :]
            dk_carry[...] = dkb_s[BLOCK:2 * BLOCK, :]
            dv_carry[...] = dvb_s[BLOCK:2 * BLOCK, :]

        @pl.when(n == nb)
        def _():
            dk_ref[...] = dk_carry[...]
            dv_ref[...] = dv_carry[...]

    cur = lambda n: (jnp.minimum(n, nb - 1), 0)
    cur_t = lambda n: (0, jnp.minimum(n, nb - 1))
    prv = lambda n: jnp.maximum(jnp.minimum(n, nb - 1) - 1, 0)
    out_prev = lambda n: (jnp.maximum(n - 1, 0), 0)
    return pl.pallas_call(
        body, name="swa_bwd", grid=(nb + 1,),
        in_specs=[pl.BlockSpec((BLOCK, WIDTH_A), lambda n: (jnp.minimum(n, nb - 1), _QA_BLK)),
                  pl.BlockSpec((BLOCK, LANES), lambda n: (jnp.minimum(n, nb - 1), _KA_BLK)),
                  pl.BlockSpec((BLOCK, LANES), lambda n: (prv(n), _KA_BLK)),
                  pl.BlockSpec((BLOCK, LANES), lambda n: (jnp.minimum(n, nb - 1), _VA_BLK)),
                  pl.BlockSpec((BLOCK, LANES), lambda n: (prv(n), _VA_BLK)),
                  pl.BlockSpec((BLOCK, WIDTH_A), cur), pl.BlockSpec((N_HEADS_A, BLOCK), cur_t),
                  pl.BlockSpec((N_HEADS_A, BLOCK), cur_t), pl.BlockSpec((1, BLOCK), cur_t),
                  pl.BlockSpec((BLOCK, 1), cur), pl.BlockSpec((BLOCK, 1), lambda n: (prv(n), 0)),
                  _full((1, N_HEADS_A))],
        out_specs=[pl.BlockSpec((BLOCK, WIDTH_A), cur), pl.BlockSpec((BLOCK, LANES), out_prev),
                   pl.BlockSpec((BLOCK, LANES), out_prev), _full((1, LANES))],
        out_shape=[jax.ShapeDtypeStruct((t, WIDTH_A), F32), jax.ShapeDtypeStruct((t, LANES), F32),
                   jax.ShapeDtypeStruct((t, LANES), F32), jax.ShapeDtypeStruct((1, LANES), F32)],
        scratch_shapes=[pltpu.VMEM((2 * BLOCK, LANES), F32), pltpu.VMEM((2 * BLOCK, LANES), F32),
                        pltpu.VMEM((BLOCK, LANES), F32), pltpu.VMEM((BLOCK, LANES), F32)],
        compiler_params=_params(("arbitrary",)),
    )(proj, proj, proj, proj, proj, d_out, lse, delta, posr, posc, posc, sinks)


def _in_bwd(dproj, w_in_t, x, dx1, g1, gp):
    t = x.shape[0]
    tm = 512
    steps = t // tm

    def body(dp_ref, w_ref, x_ref, dx1_ref, g_ref, gp_ref, dx_ref, dg_ref, land_ref, send_sems, recv_sems):
        i = pl.program_id(0)

        @pl.when(i == 0)
        def _():
            dg_ref[...] = jnp.zeros(dg_ref.shape, F32)
            _scatter_start(gp_ref, land_ref, send_sems, recv_sems)

        for rows in _row_halves(tm):
            dh = _dot(dp_ref[rows, :], w_ref[...])
            xv = x_ref[rows, :]
            r = _rms(xv)
            dx, dg = _norm_bwd(dh, xv * r, r, g_ref[...])
            dx_ref[rows, :] = dx1_ref[rows, :] + dx
            dg_ref[...] += dg

        @pl.when(i == steps - 1)
        def _():
            _scatter_wait(gp_ref, land_ref, send_sems, recv_sems)

    row = lambda i: (i, 0)
    blk = pl.BlockSpec((tm, D_MODEL), row)
    return pl.pallas_call(
        body, name="in_bwd", grid=(steps,),
        in_specs=[pl.BlockSpec((tm, D_IN_PAD), row), _full((D_IN_PAD, D_MODEL)), blk, blk, _full((1, D_MODEL)), _HBM],
        out_specs=[blk, _full((1, D_MODEL)), _HBM],
        out_shape=[jax.ShapeDtypeStruct((t, D_MODEL), F32), jax.ShapeDtypeStruct((1, D_MODEL), F32),
                   jax.ShapeDtypeStruct((3,) + gp.shape[1:], gp.dtype)],
        scratch_shapes=[pltpu.SemaphoreType.DMA((3,)), pltpu.SemaphoreType.DMA((3,))],
        compiler_params=_params(("arbitrary",)),
    )(dproj, w_in_t, x, dx1, g1, gp)


def _adamw_store(w, g, m, v, out_refs):
    g_out, d_out, m_out, v_out = out_refs
    m_new = ADAM_B1 * m + (1.0 - ADAM_B1) * g
    v_new = ADAM_B2 * v + (1.0 - ADAM_B2) * jnp.square(g)
    m_hat = m_new / (1.0 - ADAM_B1 ** ADAM_STEP)
    v_hat = v_new / (1.0 - ADAM_B2 ** ADAM_STEP)
    g_out[...] = g
    d_out[...] = -ADAM_LR * (m_hat / (jnp.sqrt(v_hat) + ADAM_EPS) + ADAM_WD * w)
    m_out[...] = m_new
    v_out[...] = v_new


_SMALL_SLOTS = {"pre_norm_mix": (0, 0, D_MODEL), "post_norm_mix": (1, 0, D_MODEL), "pre_norm_mlp": (2, 0, D_MODEL),
                "post_norm_mlp": (3, 0, D_MODEL), "q_a_norm": (4, 0, Q_LORA), "kv_a_norm": (4, Q_LORA, KV_LORA),
                "sinks": (4, Q_LORA + KV_LORA, N_HEADS_A)}
_LOSS_ROW = 5


def _adamw_small(red, w, m, v):
    names = tuple(_SMALL_SLOTS)
    n = len(names)

    def body(*refs):
        red_ref, ws, ms, vs, outs = refs[0], refs[1:1 + n], refs[1 + n:1 + 2 * n], refs[1 + 2 * n:1 + 3 * n], refs[1 + 3 * n:]
        for k, name in enumerate(names):
            row, lane, width = _SMALL_SLOTS[name]
            g = red_ref[row:row + 1, lane:lane + width]
            _adamw_store(ws[k][...], g, ms[k][...], vs[k][...], outs[4 * k:4 * k + 4])

    vmem = pl.BlockSpec(memory_space=pltpu.VMEM)
    res = pl.pallas_call(
        body, name="adamw_small", in_specs=[vmem] * (1 + 3 * n), out_specs=[vmem] * (4 * n),
        out_shape=[jax.ShapeDtypeStruct(w[name].shape, F32) for name in names for _ in range(4)],
    )(red, *[w[k] for k in names], *[m[k] for k in names], *[v[k] for k in names])
    return {name: res[4 * k:4 * k + 4] for k, name in enumerate(names)}


def _adamw(w, g_parts, m, v, name, block, g_row_off=0):
    r, c = w.shape
    br, bc = block
    ng = len(g_parts)

    def body(*refs):
        w_ref, g_refs, m_ref, v_ref = refs[0], refs[1:1 + ng], refs[1 + ng], refs[2 + ng]
        g = g_refs[0][...]
        for gr in g_refs[1:]:
            g = g + gr[...]
        _adamw_store(w_ref[...], g, m_ref[...], v_ref[...], refs[3 + ng:])

    assert g_row_off % br == 0 and r % br == 0 and c % bc == 0
    blk = pl.BlockSpec(block, lambda i, j: (i, j))
    g_blk = pl.BlockSpec(block, lambda i, j: (i + g_row_off // br, j))
    return pl.pallas_call(
        body, name=name, grid=(r // br, c // bc),
        in_specs=[blk] + [g_blk] * ng + [blk, blk], out_specs=[blk] * 4,
        out_shape=[jax.ShapeDtypeStruct((r, c), F32)] * 4,
        compiler_params=_params(("parallel", "parallel")),
    )(w, *g_parts, m, v)


_HBM = pl.BlockSpec(memory_space=pltpu.HBM)


def _other_chips(x, y):
    return ((1 - x, y), (x, 1 - y), (1 - x, 1 - y))


def _gather_copies(src, out, send_sems, recv_sems, local_sem):
    x, y, c = lax.axis_index("x"), lax.axis_index("y"), lax.axis_index("c")
    me = 2 * x + y
    local = pltpu.make_async_copy(src, out.at[me], local_sem)

    def copies(arriving):
        return [pltpu.make_async_remote_copy(src_ref=src, dst_ref=out.at[2 * px + py if arriving else me],
                                             send_sem=send_sems.at[j], recv_sem=recv_sems.at[j], device_id=(px, py, c),
                                             device_id_type=MESH)
                for j, (px, py) in enumerate(_other_chips(x, y))]

    return local, copies


def _gather_start(src, out, send_sems, recv_sems, local_sem):
    local, copies = _gather_copies(src, out, send_sems, recv_sems, local_sem)
    local.start()
    for cp in copies(False):
        cp.start()


def _gather_wait(src, out, send_sems, recv_sems, local_sem):
    local, copies = _gather_copies(src, out, send_sems, recv_sems, local_sem)
    for cp in copies(True):
        cp.wait_recv()
    for cp in copies(False):
        cp.wait_send()
    local.wait()


def _scatter_copies(src, land, send_sems, recv_sems):
    x, y, c = lax.axis_index("x"), lax.axis_index("y"), lax.axis_index("c")
    return [pltpu.make_async_remote_copy(src_ref=src.at[2 * px + py], dst_ref=land.at[j], send_sem=send_sems.at[j],
                                         recv_sem=recv_sems.at[j], device_id=(px, py, c), device_id_type=MESH)
            for j, (px, py) in enumerate(_other_chips(x, y))]


def _scatter_start(src, land, send_sems, recv_sems):
    for cp in _scatter_copies(src, land, send_sems, recv_sems):
        cp.start()


def _scatter_wait(src, land, send_sems, recv_sems):
    copies = _scatter_copies(src, land, send_sems, recv_sems)
    for cp in copies:
        cp.wait_recv()
    for cp in copies:
        cp.wait_send()


def _all_gather_chips(packed):
    r = packed.shape[0]
    half = r // 2

    def body(src, out, ici_send, ici_recv, d2d_send, d2d_recv, local_sem):
        x, y, c = lax.axis_index("x"), lax.axis_index("y"), lax.axis_index("c")
        me = 2 * x + y
        mine = pl.ds(pl.multiple_of(c * half, 16), half)
        theirs = pl.ds(pl.multiple_of((1 - c) * half, 16), half)
        chips = _other_chips(x, y)
        local = pltpu.make_async_copy(src, out.at[me], local_sem)
        local.start()
        sends = [pltpu.make_async_remote_copy(src_ref=src.at[mine], dst_ref=out.at[me, mine], send_sem=ici_send.at[j],
                                              recv_sem=ici_recv.at[j], device_id=(px, py, c), device_id_type=MESH)
                 for j, (px, py) in enumerate(chips)]
        for cp in sends:
            cp.start()
        passed = []
        for j, (px, py) in enumerate(chips):
            block = 2 * px + py
            pltpu.make_async_remote_copy(src_ref=src.at[mine], dst_ref=out.at[block, mine], send_sem=ici_send.at[j],
                                         recv_sem=ici_recv.at[j], device_id=(px, py, c), device_id_type=MESH).wait_recv()
            cp = pltpu.make_async_remote_copy(src_ref=out.at[block, mine], dst_ref=out.at[block, mine],
                                              send_sem=d2d_send.at[j], recv_sem=d2d_recv.at[j],
                                              device_id=(x, y, 1 - c), device_id_type=MESH)
            cp.start()
            passed.append(cp)
        for j, (px, py) in enumerate(chips):
            block = 2 * px + py
            pltpu.make_async_remote_copy(src_ref=out.at[block, theirs], dst_ref=out.at[block, theirs],
                                         send_sem=d2d_send.at[j], recv_sem=d2d_recv.at[j],
                                         device_id=(x, y, 1 - c), device_id_type=MESH).wait_recv()
        for cp in sends + passed:
            cp.wait_send()
        local.wait()

    sems = pltpu.SemaphoreType.DMA((3,))
    return pl.pallas_call(
        body, name="ag_weights", in_specs=[_HBM], out_specs=_HBM,
        out_shape=jax.ShapeDtypeStruct((N_CHIPS,) + packed.shape, packed.dtype),
        scratch_shapes=[sems, sems, sems, sems, pltpu.SemaphoreType.DMA(())],
    )(packed)


def _sum4(gp, land, chip, name):
    _, r, w = gp.shape
    tr = 128

    def body(chip_ref, o_ref, l_ref, s_ref):
        s_ref[...] = ((o_ref[0] + l_ref[0].astype(F32)) + l_ref[1].astype(F32)) + l_ref[2].astype(F32)

    return pl.pallas_call(
        body, name=name,
        grid_spec=pltpu.PrefetchScalarGridSpec(
            num_scalar_prefetch=1, grid=(r // tr,),
            in_specs=[pl.BlockSpec((1, tr, w), lambda i, chip_ref: (chip_ref[0], i, 0)),
                      pl.BlockSpec((3, tr, w), lambda i, chip_ref: (0, i, 0))],
            out_specs=pl.BlockSpec((tr, w), lambda i, chip_ref: (i, 0))),
        out_shape=jax.ShapeDtypeStruct((r, w), F32),
        compiler_params=_params(("parallel",)),
    )(chip, gp, land)


def _sibling_copy(src, got, send_sem, recv_sem):
    x, y, c = lax.axis_index("x"), lax.axis_index("y"), lax.axis_index("c")
    return pltpu.make_async_remote_copy(src_ref=src, dst_ref=got, send_sem=send_sem, recv_sem=recv_sem,
                                        device_id=(x, y, 1 - c), device_id_type=MESH)


def _swap_sibling(s, name):
    def body(src, got, send_sem, recv_sem):
        cp = _sibling_copy(src, got, send_sem, recv_sem)
        cp.start()
        cp.wait_recv()
        cp.wait_send()

    return pl.pallas_call(
        body, name=name, in_specs=[_HBM], out_specs=_HBM,
        out_shape=jax.ShapeDtypeStruct(s.shape, s.dtype),
        scratch_shapes=[pltpu.SemaphoreType.DMA(()), pltpu.SemaphoreType.DMA(())],
    )(s)


def _all_reduce_small(dsmall, loss):
    n_dev = 8
    names = tuple(_SMALL_SLOTS)
    shape = (8, D_MODEL)

    def body(*refs):
        parts, loss_ref = refs[:len(names)], refs[len(names)]
        out, src, gath, send_sems, recv_sems = refs[len(names) + 1:]
        x, y, c = lax.axis_index("x"), lax.axis_index("y"), lax.axis_index("c")
        me = 4 * x + 2 * y + c
        src[...] = jnp.zeros(shape, F32)
        for name, part in zip(names, parts):
            row, lane, _ = _SMALL_SLOTS[name]
            src[row:row + 1, lane:lane + part.shape[1]] = part[...]
        src[_LOSS_ROW:_LOSS_ROW + 1, 0:LANES] = loss_ref[...]
        gath[me] = src[...]
        peers = []
        for k in range(1, n_dev):
            px = 1 - x if (k >> 2) & 1 else x
            py = 1 - y if (k >> 1) & 1 else y
            pc = 1 - c if k & 1 else c
            peers.append((px, py, pc))
        sends = []
        for j, peer in enumerate(peers):
            cp = pltpu.make_async_remote_copy(src_ref=src, dst_ref=gath.at[me], send_sem=send_sems.at[j],
                                              recv_sem=recv_sems.at[j], device_id=peer, device_id_type=MESH)
            cp.start()
            sends.append(cp)
        for j, (px, py, pc) in enumerate(peers):
            pltpu.make_async_remote_copy(src_ref=src, dst_ref=gath.at[4 * px + 2 * py + pc], send_sem=send_sems.at[j],
                                         recv_sem=recv_sems.at[j], device_id=(px, py, pc), device_id_type=MESH).wait_recv()
        for cp in sends:
            cp.wait_send()
        acc = gath[0]
        for d in range(1, n_dev):
            acc = acc + gath[d]
        out[...] = acc

    vmem = pl.BlockSpec(memory_space=pltpu.VMEM)
    return pl.pallas_call(
        body, name="ar_small", in_specs=[vmem] * (len(names) + 1), out_specs=vmem,
        out_shape=jax.ShapeDtypeStruct(shape, F32),
        scratch_shapes=[pltpu.VMEM(shape, F32), pltpu.VMEM((n_dev,) + shape, F32),
                        pltpu.SemaphoreType.DMA((n_dev - 1,)), pltpu.SemaphoreType.DMA((n_dev - 1,))],
    )(*[dsmall[k] for k in names], loss)


_W_IN_ROWS = SHARD_SHAPES["w_in"][1]
_KR_ROW = 3200
_KR_PAD_ROW = _KR_BLK * LANES + QK_NOPE


def _shard_rows(name, a):
    return jnp.transpose(a) if name == "w_in" else a.reshape(PACK_ROWS[name], D_MODEL)


def _pack(group, shards, dtype):
    parts = [_shard_rows(n, shards[n]).astype(dtype) for n in group]
    pad = -sum(PACK_ROWS[n] for n in group) % LANES
    if pad:
        parts.append(jnp.zeros((pad, D_MODEL), dtype))
    return jnp.concatenate(parts, axis=0)


def _col_sharded_full(g, name, group):
    r, c = SHARD_SHAPES[name]
    off = _row_offset(group, name)
    blocks = g[:, off:off + PACK_ROWS[name]].reshape(N_CHIPS, r, c)
    return jnp.transpose(blocks, (1, 0, 2)).reshape(r, N_CHIPS * c)


def _col_sharded_blocks(d, name):
    r, c = SHARD_SHAPES[name]
    return jnp.transpose(d.reshape(r, N_CHIPS, c), (1, 0, 2)).reshape(N_CHIPS, PACK_ROWS[name], D_MODEL)


def _weights_a(g):
    dt = g.dtype
    w_in_t = g[:, :_W_IN_ROWS].reshape(N_CHIPS * _W_IN_ROWS, D_MODEL)
    z = lambda n: jnp.zeros((n, D_MODEL), dt)
    w_in_t = jnp.concatenate([w_in_t[:_KR_ROW], z(_KR_PAD_ROW - _KR_ROW), w_in_t[_KR_ROW:],
                              z(D_IN_PAD - _KR_PAD_ROW - QK_ROPE)], axis=0)
    wq = _col_sharded_full(g, "w_q_b", GROUP_A).reshape(Q_LORA, N_HEADS_B, Q_HEAD_B)
    wq_p = jnp.concatenate([wq, jnp.zeros((Q_LORA, N_HEADS_B, HEAD_PAD - Q_HEAD_B), dt)], axis=2).reshape(Q_LORA, MLA_W)
    wkv = _col_sharded_full(g, "w_kv_b", GROUP_A).reshape(KV_LORA, N_HEADS_B, QK_NOPE + V_DIM_B)
    zk = jnp.zeros((KV_LORA, N_HEADS_B, HEAD_PAD - QK_NOPE), dt)
    wk_p = jnp.concatenate([wkv[:, :, :QK_NOPE], zk], axis=2).reshape(KV_LORA, MLA_W)
    wv = wkv[:, :, QK_NOPE:].reshape(KV_LORA, N_HEADS_B * V_DIM_B)
    return dict(w_in=w_in_t, wq=wq_p, wk=wk_p, wv=wv, wv_t=jnp.transpose(wv))


def _grad_blocks_a(dw_in_t, dwq_p, dwk_p, dwv):
    dw_in = jnp.concatenate([dw_in_t[:_KR_ROW], dw_in_t[_KR_PAD_ROW:_KR_PAD_ROW + QK_ROPE]], axis=0)
    dwq = dwq_p.reshape(Q_LORA, N_HEADS_B, HEAD_PAD)[:, :, :Q_HEAD_B].reshape(Q_LORA, N_HEADS_B * Q_HEAD_B)
    dwk = dwk_p.reshape(KV_LORA, N_HEADS_B, HEAD_PAD)[:, :, :QK_NOPE]
    dwkv = jnp.concatenate([dwk, dwv.reshape(KV_LORA, N_HEADS_B, V_DIM_B)], axis=2)
    dwkv = dwkv.reshape(KV_LORA, N_HEADS_B * (QK_NOPE + V_DIM_B))
    pad = -sum(PACK_ROWS[n] for n in GROUP_A) % LANES
    return jnp.concatenate([dw_in.reshape(N_CHIPS, _W_IN_ROWS, D_MODEL), _col_sharded_blocks(dwq, "w_q_b"),
                            _col_sharded_blocks(dwkv, "w_kv_b"), jnp.zeros((N_CHIPS, pad, D_MODEL), F32)], axis=1)


def _rope_freq_lanes():
    freqs = ROPE_THETA ** (-jnp.arange(0, QK_ROPE, 2, dtype=F32) / QK_ROPE)
    return jnp.concatenate([jnp.zeros((QK_NOPE,), F32), freqs, freqs,
                            jnp.zeros((HEAD_PAD - Q_HEAD_B,), F32)]).reshape(1, LANES)


def _fwd_bwd(x, positions, target, w):
    t = x.shape[0]
    wa = _weights_a(_all_gather_chips(_pack(GROUP_A, w, BF16)))
    posr = positions.astype(F32).reshape(1, t)
    posc = posr.reshape(t, 1)
    freq = _rope_freq_lanes()
    g1, g2, g3, g4 = w["pre_norm_mix"], w["post_norm_mix"], w["pre_norm_mlp"], w["post_norm_mlp"]
    qan, kvan, sinks = w["q_a_norm"], w["kv_a_norm"], w["sinks"]

    h, proj = _proj_fwd(x, g1, wa["w_in"])
    out_a, lse_a = _swa_fwd(proj, posc, posr, sinks)
    qm, km, qt, kt, vt = _mla_prep_fwd(proj, posc, freq, qan, kvan, wa["wq"], wa["wk"], wa["wv_t"])
    out_bt, lse_b, wb = _mla_fwd(km, qt, vt, _pack(GROUP_B, w, BF16))
    w_oa, w_ob = _col_sharded_full(wb, "w_o_a", GROUP_B), _col_sharded_full(wb, "w_o_b", GROUP_B)
    merged, y, x1, h2 = _mix_out_fwd(out_a, out_bt, proj, x, w_oa, w_ob, wb, g2, g3)
    a = _up_fwd(h2, wb)
    dx2, dyd, dg4, loss = _down_fwd_loss(a, wb, x1, target, g4)

    gp_b = _dw_into_blocks(a, dyd, "w_down", 1024, _TK_DW)
    du = _down_bwd(dyd, wb, a)
    gp_b = _dw_into_blocks(h2, du, "w_up", 1024, _TK_DW, gp_b)
    dx1, dy, dg3, dg2 = _up_bwd(du, wb, x1, dx2, y, g3, g2)
    gp_b = _dw_into_blocks(merged, dy, "w_out", 1024, _TK_DW, gp_b)
    doa, dob, dga, dgb, d_out_a, d_out_b, d_out_bt, del_a, del_b = _mix_out_bwd(dy, out_a, out_bt, proj, w_oa, w_ob, wb)
    dw_oa = _matmul_tn(out_a, doa, "dw_o_a", 512, 1024)
    dw_ob = _dw_ob(out_bt, dob)
    small_b = jnp.concatenate([_col_sharded_blocks(dw_oa, "w_o_a"), _col_sharded_blocks(dw_ob, "w_o_b")], axis=1)
    gp_b = lax.dynamic_update_slice(gp_b, small_b, (0, _row_offset(GROUP_B, "w_o_a"), 0))
    dqm, dkm, dvm, land_b = _mla_bwd(qm, km, qt, kt, vt, d_out_b, d_out_bt, lse_b, del_b, gp_b)
    chip = (2 * lax.axis_index("x") + lax.axis_index("y")).astype(jnp.int32).reshape(1)
    part_b = _sum4(gp_b, land_b, chip, "rs_sum_b")
    dcq, dckv, dkr, dwq, dwk, dwv, dqan, dkvan, sib_b = _mla_prep_bwd(
        dqm, dkm, dvm, proj, posc, freq, qan, kvan, wa["wq"], wa["wk"], wa["wv"], part_b)
    dqa, dka, dva, dsinks = _swa_bwd(proj, d_out_a, lse_a, del_a, posc, posr, sinks)
    dproj = jnp.concatenate([dga, dgb, dqa.astype(BF16), dka.astype(BF16), dva.astype(BF16), dcq, dckv, dkr], axis=1)
    dw_in_t = _matmul_tn(dproj, h, "dw_in", D_IN_PAD // 2, 1024, tk=1024)
    gp_a = _grad_blocks_a(dw_in_t, dwq, dwk, dwv)
    grad_x, dg1, land_a = _in_bwd(dproj, wa["w_in"], x, dx1, g1, gp_a.astype(BF16))

    part_a = _sum4(gp_a, land_a, chip, "rs_sum_a")
    reduced = {GROUP_A: [part_a, _swap_sibling(part_a, "rs_swap_a")], GROUP_B: [part_b, sib_b]}
    dsmall = dict(pre_norm_mix=dg1, post_norm_mix=dg2, pre_norm_mlp=dg3, post_norm_mlp=dg4,
                  q_a_norm=dqan, kv_a_norm=dkvan, sinks=dsinks)
    return loss, grad_x, reduced, dsmall


def kernel(x, positions, pre_norm_mix, w_in, q_a_norm, w_q_b, kv_a_norm, w_kv_b, sinks, w_o_a, w_o_b, w_out, post_norm_mix, pre_norm_mlp, w_up, w_down, post_norm_mlp, loss_target, m_pre_norm_mix, m_w_in, m_q_a_norm, m_w_q_b, m_kv_a_norm, m_w_kv_b, m_sinks, m_w_o_a, m_w_o_b, m_w_out, m_post_norm_mix, m_pre_norm_mlp, m_w_up, m_w_down, m_post_norm_mlp, v_pre_norm_mix, v_w_in, v_q_a_norm, v_w_q_b, v_kv_a_norm, v_w_kv_b, v_sinks, v_w_o_a, v_w_o_b, v_w_out, v_post_norm_mix, v_pre_norm_mlp, v_w_up, v_w_down, v_post_norm_mlp):
    w = dict(pre_norm_mix=pre_norm_mix, w_in=w_in[0], q_a_norm=q_a_norm, w_q_b=w_q_b[0], kv_a_norm=kv_a_norm,
             w_kv_b=w_kv_b[0], sinks=sinks, w_o_a=w_o_a[0], w_o_b=w_o_b[0], w_out=w_out[0],
             post_norm_mix=post_norm_mix, pre_norm_mlp=pre_norm_mlp, w_up=w_up[0], w_down=w_down[0],
             post_norm_mlp=post_norm_mlp)
    m = dict(pre_norm_mix=m_pre_norm_mix, w_in=m_w_in[0], q_a_norm=m_q_a_norm, w_q_b=m_w_q_b[0],
             kv_a_norm=m_kv_a_norm, w_kv_b=m_w_kv_b[0], sinks=m_sinks, w_o_a=m_w_o_a[0], w_o_b=m_w_o_b[0],
             w_out=m_w_out[0], post_norm_mix=m_post_norm_mix, pre_norm_mlp=m_pre_norm_mlp, w_up=m_w_up[0],
             w_down=m_w_down[0], post_norm_mlp=m_post_norm_mlp)
    v = dict(pre_norm_mix=v_pre_norm_mix, w_in=v_w_in[0], q_a_norm=v_q_a_norm, w_q_b=v_w_q_b[0],
             kv_a_norm=v_kv_a_norm, w_kv_b=v_w_kv_b[0], sinks=v_sinks, w_o_a=v_w_o_a[0], w_o_b=v_w_o_b[0],
             w_out=v_w_out[0], post_norm_mix=v_post_norm_mix, pre_norm_mlp=v_pre_norm_mlp, w_up=v_w_up[0],
             w_down=v_w_down[0], post_norm_mlp=v_post_norm_mlp)

    loss, grad_x, reduced, dsmall = _fwd_bwd(x[0], positions, loss_target[0], w)

    red = _all_reduce_small(dsmall, loss)
    small = _adamw_small(red, w, m, v)

    big = {}
    tr = jnp.transpose
    big["w_in"] = [tr(o)[None] for o in _adamw(tr(w["w_in"]), reduced[GROUP_A], tr(m["w_in"]), tr(v["w_in"]),
                                               "adamw_w_in", (_W_IN_ROWS, 256))]
    for n in ("w_up", "w_down", "w_out"):
        big[n] = [o[None] for o in _adamw(w[n], reduced[GROUP_B], m[n], v[n], "adamw_" + n, (128, D_MODEL),
                                          _row_offset(GROUP_B, n))]
    for group, names in ((GROUP_A, ("w_q_b", "w_kv_b")), (GROUP_B, ("w_o_a", "w_o_b"))):
        for n in names:
            off = _row_offset(group, n)
            g_parts = [p[off:off + PACK_ROWS[n]].reshape(SHARD_SHAPES[n]) for p in reduced[group]]
            big[n] = [o[None] for o in _adamw(w[n], g_parts, m[n], v[n], "adamw_" + n, SHARD_SHAPES[n])]

    outs = [big[n][k] if n in big else small[n][k] for k in range(4) for n in WEIGHTS]
    return (red[_LOSS_ROW, 0], grad_x[None], *outs)
```

```python
import jax
import jax.numpy as jnp
from jax import lax
from jax.experimental import pallas as pl
from jax.experimental.pallas import tpu as pltpu

F32 = jnp.float32
BF16 = jnp.bfloat16
MESH = pl.DeviceIdType.MESH

D_MODEL = 1024
N_HEADS_A = 8
N_KV_A = 2
HEAD_DIM_A = 64
WINDOW = 128
BLOCK = 128
N_HEADS_B = 8
QK_NOPE = 64
QK_ROPE = 32
V_DIM_B = 64
Q_LORA = 256
KV_LORA = 128
ROPE_THETA = 10000.0
D_FF = 4 * D_MODEL
EPS = 1e-6
WIDTH_A = N_HEADS_A * HEAD_DIM_A
Q_HEAD_B = QK_NOPE + QK_ROPE
D_IN_PAD = 3328
HEAD_PAD = 128
MLA_W = N_HEADS_B * HEAD_PAD

ADAM_LR = 0.001
ADAM_B1 = 0.9
ADAM_B2 = 0.999
ADAM_EPS = 1e-08
ADAM_WD = 0.01
ADAM_STEP = 10

NEG = -1e30
N_CHIPS = 4
LANES = 128
VMEM_LIMIT = 56 * 1024 * 1024

SHARD_SHAPES = {"w_in": (1024, 808), "w_q_b": (256, 192), "w_kv_b": (128, 256), "w_o_a": (512, 256),
                "w_o_b": (512, 256), "w_out": (256, 1024), "w_up": (1024, 1024), "w_down": (1024, 1024)}
PACK_ROWS = {n: (s[0] * s[1]) // D_MODEL for n, s in SHARD_SHAPES.items()}
GROUP_A = ("w_in", "w_q_b", "w_kv_b")
GROUP_B = ("w_up", "w_down", "w_out", "w_o_a", "w_o_b")
WEIGHTS = ("pre_norm_mix", "w_in", "q_a_norm", "w_q_b", "kv_a_norm", "w_kv_b", "sinks", "w_o_a", "w_o_b", "w_out",
           "post_norm_mix", "pre_norm_mlp", "w_up", "w_down", "post_norm_mlp")


def _params(sem=None):
    return pltpu.CompilerParams(dimension_semantics=sem, vmem_limit_bytes=VMEM_LIMIT)


def _dot(a, b):
    return jnp.dot(a, b, preferred_element_type=F32)


def _dot_nt(a, b):
    return lax.dot_general(a, b, (((1,), (1,)), ((), ())), preferred_element_type=F32)


def _dot_tn(a, b):
    return lax.dot_general(a, b, (((0,), (0,)), ((), ())), preferred_element_type=F32)


def _rms(v):
    return lax.rsqrt(jnp.mean(v * v, axis=-1, keepdims=True) + EPS)


def _norm_bwd(dout, n, r, g):
    dn = dout * g
    dx = r * (dn - n * jnp.mean(dn * n, axis=-1, keepdims=True))
    return dx, jnp.sum(dout * n, axis=0, keepdims=True)


def _full(shape):
    return pl.BlockSpec(shape, lambda *_: (0,) * len(shape))


def _row_offset(group, name):
    return sum(PACK_ROWS[n] for n in group[:group.index(name)])


def _wb_spec(name):
    rows = PACK_ROWS[name]
    return pl.BlockSpec((N_CHIPS, rows, D_MODEL), lambda *_: (0, _row_offset(GROUP_B, name) // rows, 0))


def _proj_fwd(x, g1, w_in_t):
    t = x.shape[0]
    tm = 512

    def body(x_ref, g_ref, w_ref, h_ref, p_ref):
        for rows in _row_halves(tm):
            xv = x_ref[rows, :]
            h = ((xv * _rms(xv)) * g_ref[...]).astype(BF16)
            h_ref[rows, :] = h
            p_ref[rows, :] = _dot_nt(h, w_ref[...])

    return pl.pallas_call(
        body, name="proj_fwd", grid=(t // tm,),
        in_specs=[pl.BlockSpec((tm, D_MODEL), lambda i: (i, 0)), _full((1, D_MODEL)), _full((D_IN_PAD, D_MODEL))],
        out_specs=[pl.BlockSpec((tm, D_MODEL), lambda i: (i, 0)), pl.BlockSpec((tm, D_IN_PAD), lambda i: (i, 0))],
        out_shape=[jax.ShapeDtypeStruct((t, D_MODEL), BF16), jax.ShapeDtypeStruct((t, D_IN_PAD), F32)],
        compiler_params=_params(("parallel",)),
    )(x, g1, w_in_t)


_QA_BLK = 2048 // WIDTH_A
_KA_BLK = 2560 // LANES
_VA_BLK = 2688 // LANES
_CQ_BLK = 2816 // Q_LORA
_CKV_BLK = 3072 // LANES
_KR_BLK = 3200 // LANES


_GROUP_A = N_HEADS_A // N_KV_A
_SWA_SCALE = HEAD_DIM_A ** -0.5
_LOG2E = 1.4426950408889634


def _head_cols(v, h):
    return v[:, HEAD_DIM_A * h:HEAD_DIM_A * (h + 1)]


def _head_rows(v, h):
    return v[HEAD_DIM_A * h:HEAD_DIM_A * (h + 1), :]


def _swa_band(n, kp_ref, kc_ref, vp_ref, vc_ref, pq_ref, pp_ref, pc_ref):
    kb = jnp.concatenate([kp_ref[...], kc_ref[...]], axis=0)
    vb = jnp.concatenate([vp_ref[...], vc_ref[...]], axis=0)
    posk = jnp.concatenate([pp_ref[...], pc_ref[...]], axis=0)
    dist = jnp.abs(posk - pq_ref[...])
    ki = lax.broadcasted_iota(jnp.int32, (2 * BLOCK, BLOCK), 0)
    qi = lax.broadcasted_iota(jnp.int32, (2 * BLOCK, BLOCK), 1)
    valid = (ki > qi) & (ki <= qi + WINDOW) & ((n > 0) | (ki >= BLOCK))
    return kb, vb, dist, valid


def _swa_scores_t(st_g, j, h, dist, valid):
    slope = 2.0 ** (-8.0 * (h + 1) / N_HEADS_A)
    st = st_g[:, BLOCK * j:BLOCK * (j + 1)] * (_SWA_SCALE * _LOG2E) - (slope * _LOG2E) * dist
    return jnp.where(valid, st, NEG)


def _group_t(xt, kh):
    return jnp.concatenate([_head_rows(xt, _GROUP_A * kh + j) for j in range(_GROUP_A)], axis=1).astype(BF16)


def _swa_fwd(proj, posc, posr, sinks):
    t = proj.shape[0]
    nb = t // BLOCK

    def body(q_ref, kc_ref, kp_ref, vc_ref, vp_ref, pq_ref, pc_ref, pp_ref, sink_ref, o_ref, l_ref):
        n = pl.program_id(0)
        kb, vb, dist, valid = _swa_band(n, kp_ref, kc_ref, vp_ref, vc_ref, pq_ref, pp_ref, pc_ref)
        q_t, vb_t = q_ref[...].T, vb.T
        out_t, lse = [], []
        for kh in range(N_KV_A):
            st_g = _dot(_head_cols(kb, kh).astype(BF16), _group_t(q_t, kh))
            ps = []
            for j in range(_GROUP_A):
                h = _GROUP_A * kh + j
                st = _swa_scores_t(st_g, j, h, dist, valid)
                sink = sink_ref[0:1, h:h + 1] * _LOG2E
                m = jnp.maximum(jnp.max(st, axis=0, keepdims=True), sink)
                e = jnp.exp2(st - m)
                den = jnp.sum(e, axis=0, keepdims=True) + jnp.exp2(sink - m)
                ps.append((e * (1.0 / den)).astype(BF16))
                lse.append(m + jnp.log(den) * _LOG2E)
            o_g = _dot(_head_rows(vb_t, kh).astype(BF16), jnp.concatenate(ps, axis=1))
            out_t.extend(o_g[:, BLOCK * j:BLOCK * (j + 1)] for j in range(_GROUP_A))
        o_ref[...] = jnp.concatenate(out_t, axis=0).T
        l_ref[...] = jnp.concatenate(lse, axis=0)

    cur = lambda n: (n, 0)
    prev = lambda n: jnp.maximum(n - 1, 0)
    return pl.pallas_call(
        body, name="swa_fwd", grid=(nb,),
        in_specs=[pl.BlockSpec((BLOCK, WIDTH_A), lambda n: (n, _QA_BLK)),
                  pl.BlockSpec((BLOCK, LANES), lambda n: (n, _KA_BLK)),
                  pl.BlockSpec((BLOCK, LANES), lambda n: (prev(n), _KA_BLK)),
                  pl.BlockSpec((BLOCK, LANES), lambda n: (n, _VA_BLK)),
                  pl.BlockSpec((BLOCK, LANES), lambda n: (prev(n), _VA_BLK)),
                  pl.BlockSpec((1, BLOCK), lambda n: (0, n)),
                  pl.BlockSpec((BLOCK, 1), cur),
                  pl.BlockSpec((BLOCK, 1), lambda n: (prev(n), 0)),
                  _full((1, N_HEADS_A))],
        out_specs=[pl.BlockSpec((BLOCK, WIDTH_A), cur), pl.BlockSpec((N_HEADS_A, BLOCK), lambda n: (0, n))],
        out_shape=[jax.ShapeDtypeStruct((t, WIDTH_A), F32), jax.ShapeDtypeStruct((N_HEADS_A, t), F32)],
        compiler_params=_params(("parallel",)),
    )(proj, proj, proj, proj, proj, posr, posc, posc, sinks)


def _rope_coeffs(pos, freq):
    ang = pos * freq
    cosv, sinv = jnp.cos(ang), jnp.sin(ang)
    lane = lax.broadcasted_iota(jnp.int32, ang.shape, 1)
    lo = (lane >= QK_NOPE) & (lane < QK_NOPE + QK_ROPE // 2)
    hi = (lane >= QK_NOPE + QK_ROPE // 2) & (lane < QK_NOPE + QK_ROPE)
    c = jnp.where(lane < QK_NOPE, 1.0, jnp.where(lo | hi, cosv, 0.0))
    s = jnp.where(lo, -sinv, jnp.where(hi, sinv, 0.0))
    return c, s, lo, hi


def _rope(xh, c, s, lo):
    up = pltpu.roll(xh, LANES - QK_ROPE // 2, axis=1)
    dn = pltpu.roll(xh, QK_ROPE // 2, axis=1)
    return xh * c + jnp.where(lo, up, dn) * s


def _unrope(dh, c, s, lo, hi):
    g = dh * s
    up = pltpu.roll(g, LANES - QK_ROPE // 2, axis=1)
    dn = pltpu.roll(g, QK_ROPE // 2, axis=1)
    return dh * c + jnp.where(hi, dn, jnp.where(lo, up, 0.0))


_TQ = 512
_MLA_SCALE = Q_HEAD_B ** -0.5


def _mla_prep_fwd(proj, posc, freq, qan, kvan, wq, wk, wv):
    t = proj.shape[0]
    tm = _TQ
    nb = t // tm

    def body(cq_ref, ckv_ref, kr_ref, pos_ref, f_ref, qan_ref, kvan_ref, wq_ref, wk_ref, wv_ref,
             q_ref, k_ref, qt_ref, kt_ref, vt_ref):
        cq = cq_ref[...]
        cqn = ((cq * _rms(cq)) * qan_ref[...]).astype(BF16)
        ckv = ckv_ref[...]
        ckvn = ((ckv * _rms(ckv)) * kvan_ref[...]).astype(BF16)
        qb = _dot(cqn, wq_ref[...])
        kb = _dot(ckvn, wk_ref[...])
        vbt = _dot_nt(wv_ref[...], ckvn)
        c, s, lo, _ = _rope_coeffs(pos_ref[...], f_ref[...])
        kr = _rope(kr_ref[...], c, s, lo)
        for h in range(N_HEADS_B):
            sl = slice(HEAD_PAD * h, HEAD_PAD * (h + 1))
            q_h = _rope(qb[:, sl], c, s, lo)
            k_h = kb[:, sl] + kr
            q_ref[:, sl] = q_h.astype(BF16)
            k_ref[:, sl] = k_h.astype(BF16)
            qt_ref[h, 0] = q_h.T.astype(BF16)
            kt_ref[h, 0] = k_h.T.astype(BF16)
            vt_ref[h, 0] = vbt[V_DIM_B * h:V_DIM_B * (h + 1), :].astype(BF16)

    row = lambda i: (i, 0)
    blk4 = lambda d: pl.BlockSpec((N_HEADS_B, 1, d, tm), lambda i: (0, i, 0, 0))
    return pl.pallas_call(
        body, name="mla_prep_fwd", grid=(nb,),
        in_specs=[pl.BlockSpec((tm, Q_LORA), lambda i: (i, _CQ_BLK)),
                  pl.BlockSpec((tm, LANES), lambda i: (i, _CKV_BLK)),
                  pl.BlockSpec((tm, LANES), lambda i: (i, _KR_BLK)),
                  pl.BlockSpec((tm, 1), row), _full((1, LANES)), _full((1, Q_LORA)), _full((1, KV_LORA)),
                  _full((Q_LORA, MLA_W)), _full((KV_LORA, MLA_W)), _full((N_HEADS_B * V_DIM_B, KV_LORA))],
        out_specs=[pl.BlockSpec((tm, MLA_W), row), pl.BlockSpec((tm, MLA_W), row), blk4(HEAD_PAD), blk4(HEAD_PAD),
                   blk4(V_DIM_B)],
        out_shape=[jax.ShapeDtypeStruct((t, MLA_W), BF16), jax.ShapeDtypeStruct((t, MLA_W), BF16),
                   jax.ShapeDtypeStruct((N_HEADS_B, nb, HEAD_PAD, tm), BF16),
                   jax.ShapeDtypeStruct((N_HEADS_B, nb, HEAD_PAD, tm), BF16),
                   jax.ShapeDtypeStruct((N_HEADS_B, nb, V_DIM_B, tm), BF16)],
        compiler_params=_params(("parallel",)),
    )(proj, proj, proj, posc, freq, qan, kvan, wq, wk, wv)


_MLA_SCALE2 = _MLA_SCALE * _LOG2E


def _mla_fwd(k, qt, vt, w_src):
    t = k.shape[0]
    nb = t // _TQ

    def body(k_ref, qt_ref, vt_ref, w_ref, o_ref, l_ref, wg_ref, raw_a, raw_b, send_sems, recv_sems, local_sem):
        qi = pl.program_id(1)
        first = (pl.program_id(0) == 0) & (qi == 0)
        last = (pl.program_id(0) == N_HEADS_B - 1) & (qi == nb - 1)

        @pl.when(first)
        def _():
            _gather_start(w_ref, wg_ref, send_sems, recv_sems, local_sem)

        q_t = qt_ref[0, 0]

        def product(kj):
            return _dot(k_ref[pl.ds(pl.multiple_of(kj * _TQ, _TQ), _TQ), :], q_t)

        def update(stats, raw_ref, kj, diagonal=False):
            m, l, acc = stats
            raw = raw_ref[...]
            if diagonal:
                key = lax.broadcasted_iota(jnp.int32, raw.shape, 0)
                qry = lax.broadcasted_iota(jnp.int32, raw.shape, 1)
                raw = jnp.where(key <= qry, raw, NEG)
            m_new = jnp.maximum(m, jnp.max(raw, axis=0, keepdims=True) * _MLA_SCALE2)
            alpha = jnp.exp2(m - m_new)
            p = jnp.exp2(raw * _MLA_SCALE2 - m_new)
            l = alpha * l + jnp.sum(p, axis=0, keepdims=True)
            acc = alpha * acc + _dot(vt_ref[0, kj], p.astype(BF16))
            return m_new, l, acc

        def trip(i, stats):
            raw_b[...] = product(2 * i + 1)
            stats = update(stats, raw_a, 2 * i)
            raw_a[...] = product(2 * i + 2)
            return update(stats, raw_b, 2 * i + 1)

        def tail_even(stats):
            return update(stats, raw_a, qi, True)

        def tail_odd(stats):
            raw_b[...] = product(qi)
            return update(update(stats, raw_a, qi - 1), raw_b, qi, True)

        init = (jnp.full((1, _TQ), NEG, F32), jnp.zeros((1, _TQ), F32), jnp.zeros((V_DIM_B, _TQ), F32))
        raw_a[...] = product(0)
        stats = lax.fori_loop(0, qi // 2, trip, init)
        m, l, acc = lax.cond(qi % 2 == 0, tail_even, tail_odd, stats)
        o_ref[0, 0] = acc / l
        l_ref[0, 0] = m + jnp.log(l) * _LOG2E

        @pl.when(last)
        def _():
            _gather_wait(w_ref, wg_ref, send_sems, recv_sems, local_sem)

    return pl.pallas_call(
        body, name="mla_fwd", grid=(N_HEADS_B, nb),
        in_specs=[pl.BlockSpec((t, HEAD_PAD), lambda h, qi: (0, h)),
                  pl.BlockSpec((1, 1, HEAD_PAD, _TQ), lambda h, qi: (h, qi, 0, 0)),
                  pl.BlockSpec((1, nb, V_DIM_B, _TQ), lambda h, qi: (h, 0, 0, 0)), _HBM],
        out_specs=[pl.BlockSpec((1, 1, V_DIM_B, _TQ), lambda h, qi: (h, qi, 0, 0)),
                   pl.BlockSpec((1, 1, 1, _TQ), lambda h, qi: (h, qi, 0, 0)), _HBM],
        out_shape=[jax.ShapeDtypeStruct((N_HEADS_B, nb, V_DIM_B, _TQ), F32),
                   jax.ShapeDtypeStruct((N_HEADS_B, nb, 1, _TQ), F32),
                   jax.ShapeDtypeStruct((N_CHIPS,) + w_src.shape, w_src.dtype)],
        scratch_shapes=[pltpu.VMEM((_TQ, _TQ), F32), pltpu.VMEM((_TQ, _TQ), F32),
                        pltpu.SemaphoreType.DMA((3,)), pltpu.SemaphoreType.DMA((3,)), pltpu.SemaphoreType.DMA(())],
        compiler_params=_params(("arbitrary", "arbitrary")),
    )(k, qt, vt, w_src)


def _ot_spec(tm, d):
    per = _TQ // tm
    return pl.BlockSpec((N_HEADS_B, 1, d, tm), lambda i: (0, i // per, 0, i % per))


def _mix_out_fwd(out_a, out_bt, proj, x, w_oa, w_ob, wb, g2, g3):
    t = x.shape[0]
    tm = 256

    def body(oa_ref, obt_ref, ga_ref, gb_ref, x_ref, woa_ref, wob_ref, wout_ref, g2_ref, g3_ref,
             mg_ref, y_ref, x1_ref, h2_ref):
        oa = _dot(oa_ref[...].astype(BF16), woa_ref[...])
        obt = obt_ref[...].reshape(N_HEADS_B * V_DIM_B, tm).astype(BF16)
        ob = _dot_tn(obt, wob_ref[...])
        merged = (jax.nn.sigmoid(ga_ref[...]) * oa + jax.nn.sigmoid(gb_ref[...]) * ob).astype(BF16)
        mg_ref[...] = merged
        y = _dot(merged, wout_ref[...].reshape(D_MODEL, D_MODEL))
        y_ref[...] = y
        x1 = x_ref[...] + (y * _rms(y)) * g2_ref[...]
        x1_ref[...] = x1
        h2_ref[...] = ((x1 * _rms(x1)) * g3_ref[...]).astype(BF16)

    row = lambda i: (i, 0)
    blk = pl.BlockSpec((tm, D_MODEL), row)
    return pl.pallas_call(
        body, name="mix_out_fwd", grid=(t // tm,),
        in_specs=[pl.BlockSpec((tm, WIDTH_A), row), _ot_spec(tm, V_DIM_B), pl.BlockSpec((tm, D_MODEL), lambda i: (i, 0)),
                  pl.BlockSpec((tm, D_MODEL), lambda i: (i, 1)), blk,
                  _full((WIDTH_A, D_MODEL)), _full((N_HEADS_B * V_DIM_B, D_MODEL)), _wb_spec("w_out"),
                  _full((1, D_MODEL)), _full((1, D_MODEL))],
        out_specs=[blk, blk, blk, blk],
        out_shape=[jax.ShapeDtypeStruct((t, D_MODEL), BF16), jax.ShapeDtypeStruct((t, D_MODEL), F32),
                   jax.ShapeDtypeStruct((t, D_MODEL), F32), jax.ShapeDtypeStruct((t, D_MODEL), BF16)],
        compiler_params=_params(("parallel",)),
    )(out_a, out_bt, proj, proj, x, w_oa, w_ob, wb, g2, g3)


_TM_MLP = 512


def _row_halves(tm):
    return slice(0, tm // 2), slice(tm // 2, tm)


def _up_fwd(h2, wb):
    t = h2.shape[0]
    tm = _TM_MLP

    def body(h_ref, w_ref, a_ref):
        hv = h_ref[...]
        for j in range(N_CHIPS):
            u = _dot(hv, w_ref[j])
            a_ref[:, D_MODEL * j:D_MODEL * (j + 1)] = jnp.square(jnp.maximum(u, 0.0)).astype(BF16)

    return pl.pallas_call(
        body, name="up_fwd", grid=(t // tm,),
        in_specs=[pl.BlockSpec((tm, D_MODEL), lambda i: (i, 0)), _wb_spec("w_up")],
        out_specs=pl.BlockSpec((tm, D_FF), lambda i: (i, 0)),
        out_shape=jax.ShapeDtypeStruct((t, D_FF), BF16),
        compiler_params=_params(("parallel",)),
    )(h2, wb)


def _down_fwd_loss(a, wb, x1, target, g4):
    t = a.shape[0]
    tm = _TM_MLP

    def body(a_ref, w_ref, x1_ref, tg_ref, g_ref, dx2_ref, dyd_ref, dg_ref, loss_ref):
        @pl.when(pl.program_id(0) == 0)
        def _():
            dg_ref[...] = jnp.zeros(dg_ref.shape, F32)
            loss_ref[...] = jnp.zeros(loss_ref.shape, F32)

        yd = _dot(a_ref[...], w_ref[...].reshape(D_FF, D_MODEL))
        r = _rms(yd)
        n = yd * r
        diff = (x1_ref[...] + n * g_ref[...]) - tg_ref[...]
        loss_ref[...] += 0.5 * jnp.sum(jnp.mean(diff * diff, axis=-1, keepdims=True), axis=0, keepdims=True)
        dx2 = diff * (1.0 / D_MODEL)
        dx2_ref[...] = dx2
        dyd, dg = _norm_bwd(dx2, n, r, g_ref[...])
        dyd_ref[...] = dyd.astype(BF16)
        dg_ref[...] += dg

    row = lambda i: (i, 0)
    blk = pl.BlockSpec((tm, D_MODEL), row)
    return pl.pallas_call(
        body, name="down_fwd_loss", grid=(t // tm,),
        in_specs=[pl.BlockSpec((tm, D_FF), row), _wb_spec("w_down"), blk, blk, _full((1, D_MODEL))],
        out_specs=[blk, blk, _full((1, D_MODEL)), _full((1, LANES))],
        out_shape=[jax.ShapeDtypeStruct((t, D_MODEL), F32), jax.ShapeDtypeStruct((t, D_MODEL), BF16),
                   jax.ShapeDtypeStruct((1, D_MODEL), F32), jax.ShapeDtypeStruct((1, LANES), F32)],
        compiler_params=_params(("arbitrary",)),
    )(a, wb, x1, target, g4)


def _matmul_tn(a, b, name, tm, tn, tk=1024):
    t, m = a.shape
    n = b.shape[1]
    tk = min(tk, t)
    nk = t // tk

    def body(a_ref, b_ref, o_ref):
        @pl.when(pl.program_id(2) == 0)
        def _():
            o_ref[...] = jnp.zeros(o_ref.shape, F32)

        o_ref[...] += _dot_tn(a_ref[...].astype(BF16), b_ref[...].astype(BF16))

    return pl.pallas_call(
        body, name=name, grid=(m // tm, n // tn, nk),
        in_specs=[pl.BlockSpec((tk, tm), lambda i, j, k: (k, i)), pl.BlockSpec((tk, tn), lambda i, j, k: (k, j))],
        out_specs=pl.BlockSpec((tm, tn), lambda i, j, k: (i, j)),
        out_shape=jax.ShapeDtypeStruct((m, n), F32),
        compiler_params=_params(("parallel", "parallel", "arbitrary")),
    )(a, b)


_TK_DW = 2048


def _dw_into_blocks(a, b, weight, tm, tk, buf=None):
    t, m = a.shape
    n = b.shape[1]
    tk = min(tk, t)
    nk = t // tk
    rows = PACK_ROWS[weight]
    br = min(tm, rows)
    chips = tm // br
    first = _row_offset(GROUP_B, weight) // br
    per_chip = rows // br
    if weight == "w_up":
        out_map = lambda i, j, k: (j, first + i, 0)
    elif chips > 1:
        out_map = lambda i, j, k: (i, first, 0)
    else:
        out_map = lambda i, j, k: (i // per_chip, first + i % per_chip, 0)

    def body(a_ref, b_ref, *rest):
        o_ref = rest[-1]

        @pl.when(pl.program_id(2) == 0)
        def _():
            o_ref[...] = jnp.zeros(o_ref.shape, F32)

        o_ref[...] += _dot_tn(a_ref[...].astype(BF16), b_ref[...].astype(BF16)).reshape(o_ref.shape)

    in_specs = [pl.BlockSpec((tk, tm), lambda i, j, k: (k, i)), pl.BlockSpec((tk, D_MODEL), lambda i, j, k: (k, j))]
    operands = [a, b]
    if buf is not None:
        in_specs.append(pl.BlockSpec(memory_space=pl.ANY))
        operands.append(buf)
    total = sum(PACK_ROWS[w] for w in GROUP_B)
    return pl.pallas_call(
        body, name="dw_" + weight[2:], grid=(m // tm, n // D_MODEL, nk),
        in_specs=in_specs, out_specs=pl.BlockSpec((chips, br, D_MODEL), out_map),
        out_shape=jax.ShapeDtypeStruct((N_CHIPS, total, D_MODEL), F32),
        input_output_aliases={} if buf is None else {2: 0},
        compiler_params=_params(("parallel", "parallel", "arbitrary")),
    )(*operands)


def _down_bwd(dyd, wb, a):
    t = dyd.shape[0]
    tm = _TM_MLP

    def body(d_ref, w_ref, a_ref, du_ref):
        dv = d_ref[...]
        for j in range(N_CHIPS):
            cols = slice(D_MODEL * j, D_MODEL * (j + 1))
            av = a_ref[:, cols].astype(F32)
            relu_u = jnp.where(av > 0.0, av * lax.rsqrt(av), 0.0)
            du_ref[:, cols] = (_dot_nt(dv, w_ref[j]) * (2.0 * relu_u)).astype(BF16)

    row = lambda i: (i, 0)
    return pl.pallas_call(
        body, name="down_bwd", grid=(t // tm,),
        in_specs=[pl.BlockSpec((tm, D_MODEL), row), _wb_spec("w_down"), pl.BlockSpec((tm, D_FF), row)],
        out_specs=pl.BlockSpec((tm, D_FF), row),
        out_shape=jax.ShapeDtypeStruct((t, D_FF), BF16),
        compiler_params=_params(("parallel",)),
    )(dyd, wb, a)


def _up_bwd(du, wb, x1, dx2, y, g3, g2):
    t = du.shape[0]
    tm = _TM_MLP

    def body(du_ref, w_ref, x1_ref, dx2_ref, y_ref, g3_ref, g2_ref, dx1_ref, dy_ref, dg3_ref, dg2_ref):
        @pl.when(pl.program_id(0) == 0)
        def _():
            dg3_ref[...] = jnp.zeros(dg3_ref.shape, F32)
            dg2_ref[...] = jnp.zeros(dg2_ref.shape, F32)

        dh2 = _dot_nt(du_ref[:, 0:D_MODEL], w_ref[0])
        for j in range(1, N_CHIPS):
            dh2 = dh2 + _dot_nt(du_ref[:, D_MODEL * j:D_MODEL * (j + 1)], w_ref[j])
        x1 = x1_ref[...]
        r3 = _rms(x1)
        d3, dg3 = _norm_bwd(dh2, x1 * r3, r3, g3_ref[...])
        dx1 = dx2_ref[...] + d3
        dx1_ref[...] = dx1
        dg3_ref[...] += dg3
        y = y_ref[...]
        r2 = _rms(y)
        dy, dg2 = _norm_bwd(dx1, y * r2, r2, g2_ref[...])
        dy_ref[...] = dy.astype(BF16)
        dg2_ref[...] += dg2

    row = lambda i: (i, 0)
    blk = pl.BlockSpec((tm, D_MODEL), row)
    return pl.pallas_call(
        body, name="up_bwd", grid=(t // tm,),
        in_specs=[pl.BlockSpec((tm, D_FF), row), _wb_spec("w_up"),
                  blk, blk, blk, _full((1, D_MODEL)), _full((1, D_MODEL))],
        out_specs=[blk, blk, _full((1, D_MODEL)), _full((1, D_MODEL))],
        out_shape=[jax.ShapeDtypeStruct((t, D_MODEL), F32), jax.ShapeDtypeStruct((t, D_MODEL), BF16),
                   jax.ShapeDtypeStruct((1, D_MODEL), F32), jax.ShapeDtypeStruct((1, D_MODEL), F32)],
        compiler_params=_params(("arbitrary",)),
    )(du, wb, x1, dx2, y, g3, g2)


def _mix_out_bwd(dy, out_a, out_bt, proj, w_oa, w_ob, wb):
    t = dy.shape[0]
    tm = 256
    nb = t // _TQ

    def body(dy_ref, oa_ref, obt_ref, ga_ref, gb_ref, woa_ref, wob_ref, wout_ref,
             doa_ref, dob_ref, dga_ref, dgb_ref, da_ref, db_ref, dbt_ref, dela_ref, delb_ref):
        dm = _dot_nt(dy_ref[...], wout_ref[...].reshape(D_MODEL, D_MODEL))
        out_a_v = oa_ref[...]
        out_bt_v = obt_ref[...].reshape(N_HEADS_B * V_DIM_B, tm)
        oa = _dot(out_a_v.astype(BF16), woa_ref[...])
        ob = _dot_tn(out_bt_v.astype(BF16), wob_ref[...])
        sa, sb = jax.nn.sigmoid(ga_ref[...]), jax.nn.sigmoid(gb_ref[...])
        doa = (dm * sa).astype(BF16)
        dob = (dm * sb).astype(BF16)
        doa_ref[...] = doa
        dob_ref[...] = dob
        dga_ref[...] = (dm * oa * (sa * (1.0 - sa))).astype(BF16)
        dgb_ref[...] = (dm * ob * (sb * (1.0 - sb))).astype(BF16)
        d_out_a = _dot_nt(doa, woa_ref[...])
        da_ref[...] = d_out_a
        prod_at = (d_out_a * out_a_v).T
        dela_ref[...] = jnp.concatenate(
            [jnp.sum(_head_rows(prod_at, h), axis=0, keepdims=True) for h in range(N_HEADS_A)], axis=0)
        d_out_b = _dot_nt(dob, wob_ref[...])
        d_out_bt = _dot_nt(wob_ref[...], dob)
        prod_bt = d_out_bt * out_bt_v
        for h in range(N_HEADS_B):
            db_ref[h] = d_out_b[:, V_DIM_B * h:V_DIM_B * (h + 1)].astype(BF16)
            dbt_ref[h, 0] = d_out_bt[V_DIM_B * h:V_DIM_B * (h + 1), :].astype(BF16)
            delb_ref[h, 0] = jnp.sum(prod_bt[V_DIM_B * h:V_DIM_B * (h + 1), :], axis=0, keepdims=True)

    row = lambda i: (i, 0)
    blk = pl.BlockSpec((tm, D_MODEL), row)
    return pl.pallas_call(
        body, name="mix_out_bwd", grid=(t // tm,),
        in_specs=[blk, pl.BlockSpec((tm, WIDTH_A), row), _ot_spec(tm, V_DIM_B),
                  pl.BlockSpec((tm, D_MODEL), lambda i: (i, 0)), pl.BlockSpec((tm, D_MODEL), lambda i: (i, 1)),
                  _full((WIDTH_A, D_MODEL)), _full((N_HEADS_B * V_DIM_B, D_MODEL)), _wb_spec("w_out")],
        out_specs=[blk, blk, blk, blk, pl.BlockSpec((tm, WIDTH_A), row),
                   pl.BlockSpec((N_HEADS_B, tm, V_DIM_B), lambda i: (0, i, 0)), _ot_spec(tm, V_DIM_B),
                   pl.BlockSpec((N_HEADS_A, tm), lambda i: (0, i)), _ot_spec(tm, 1)],
        out_shape=[jax.ShapeDtypeStruct((t, D_MODEL), BF16)] * 4
        + [jax.ShapeDtypeStruct((t, WIDTH_A), F32), jax.ShapeDtypeStruct((N_HEADS_B, t, V_DIM_B), BF16),
           jax.ShapeDtypeStruct((N_HEADS_B, nb, V_DIM_B, _TQ), BF16), jax.ShapeDtypeStruct((N_HEADS_A, t), F32),
           jax.ShapeDtypeStruct((N_HEADS_B, nb, 1, _TQ), F32)],
        compiler_params=_params(("parallel",)),
    )(dy, out_a, out_bt, proj, proj, w_oa, w_ob, wb)


def _dw_ob(out_bt, dob):
    t = dob.shape[0]
    nb = t // _TQ

    def body(obt_ref, dob_ref, o_ref):
        @pl.when(pl.program_id(0) == 0)
        def _():
            o_ref[...] = jnp.zeros(o_ref.shape, F32)

        obt = obt_ref[...].reshape(N_HEADS_B * V_DIM_B, _TQ).astype(BF16)
        o_ref[...] += _dot(obt, dob_ref[...])

    return pl.pallas_call(
        body, name="dw_o_b", grid=(nb,),
        in_specs=[pl.BlockSpec((N_HEADS_B, 1, V_DIM_B, _TQ), lambda i: (0, i, 0, 0)),
                  pl.BlockSpec((_TQ, D_MODEL), lambda i: (i, 0))],
        out_specs=_full((N_HEADS_B * V_DIM_B, D_MODEL)),
        out_shape=jax.ShapeDtypeStruct((N_HEADS_B * V_DIM_B, D_MODEL), F32),
        compiler_params=_params(("arbitrary",)),
    )(out_bt, dob)


def _mla_bwd(q, k, qt, kt, vt, d_out, d_out_t, lse, delta, gp):
    t = q.shape[0]
    nb = t // _TQ

    def body(k_ref, kt_ref, vt_ref, q_ref, qt_ref, do_ref, dot_ref, lrow_ref, drow_ref, gp_ref,
             dq_ref, dkt_ref, dvt_ref, land_ref, l_rep, d_rep, send_sems, recv_sems):
        step = pl.program_id(1)
        kj = nb - 1 - step

        @pl.when((pl.program_id(0) == 0) & (step == 0))
        def _():
            _scatter_start(gp_ref, land_ref, send_sems, recv_sems)

        @pl.when(step == 0)
        def _():
            dq_ref[...] = jnp.zeros(dq_ref.shape, F32)
            for b in range(nb):
                l_rep[_TQ * b:_TQ * (b + 1), :] = jnp.broadcast_to(lrow_ref[0, b], (LANES, _TQ)).T
                d_rep[_TQ * b:_TQ * (b + 1), :] = jnp.broadcast_to(drow_ref[0, b], (LANES, _TQ)).T

        kv, k_t, v_t = k_ref[...], kt_ref[0, 0], vt_ref[0, 0]

        def rows_of(qi):
            return pl.ds(pl.multiple_of(qi * _TQ, _TQ), _TQ)

        def products(qi, diagonal=False):
            s = _dot(q_ref[rows_of(qi), :], k_t) * _MLA_SCALE2
            if diagonal:
                qry = lax.broadcasted_iota(jnp.int32, s.shape, 0)
                key = lax.broadcasted_iota(jnp.int32, s.shape, 1)
                s = jnp.where(key <= qry, s, NEG)
            return s, _dot(do_ref[0, rows_of(qi), :], v_t)

        def update(carry, prods, qi):
            dkt, dvt = carry
            s, dp = prods
            lse, delta = l_rep[rows_of(qi), :], d_rep[rows_of(qi), :]
            ps, dss = [], []
            for c in range(_TQ // LANES):
                strip = slice(LANES * c, LANES * (c + 1))
                p = jnp.exp2(s[:, strip] - lse)
                ps.append(p.astype(BF16))
                dss.append((p * (dp[:, strip] - delta) * _MLA_SCALE).astype(BF16))
            p_b, ds_b = jnp.concatenate(ps, axis=1), jnp.concatenate(dss, axis=1)
            dvt = dvt + _dot(dot_ref[0, qi], p_b)
            dkt = dkt + _dot(qt_ref[0, qi], ds_b)
            dq_ref[rows_of(qi), :] += _dot(ds_b, kv)
            return dkt, dvt

        def two_blocks(carry, qa, diagonal=False):
            pa, pb = products(qa, diagonal), products(qa + 1)
            return update(update(carry, pa, qa), pb, qa + 1)

        init = (jnp.zeros((HEAD_PAD, _TQ), F32), jnp.zeros((V_DIM_B, _TQ), F32))
        rest = nb - 1 - kj
        carry = lax.cond(rest >= 1, lambda cr: two_blocks(cr, kj, True),
                         lambda cr: update(cr, products(kj, True), kj), init)
        pairs = jnp.maximum(rest - 1, 0) // 2
        carry = lax.fori_loop(0, pairs, lambda i, cr: two_blocks(cr, kj + 2 + 2 * i), carry)
        dkt, dvt = lax.fori_loop(kj + 2 + 2 * pairs, nb, lambda qi, cr: update(cr, products(qi), qi), carry)
        dkt_ref[0, 0] = dkt
        dvt_ref[0, 0] = dvt

        @pl.when((pl.program_id(0) == N_HEADS_B - 1) & (step == nb - 1))
        def _():
            _scatter_wait(gp_ref, land_ref, send_sems, recv_sems)

    head4 = lambda d: pl.BlockSpec((1, nb, d, _TQ), lambda h, s: (h, 0, 0, 0))
    blk4 = lambda d: pl.BlockSpec((1, 1, d, _TQ), lambda h, s: (h, nb - 1 - s, 0, 0))
    head3 = lambda d: pl.BlockSpec((1, t, d), lambda h, kj: (h, 0, 0))
    per_head = pl.BlockSpec((t, HEAD_PAD), lambda h, kj: (0, h))
    return pl.pallas_call(
        body, name="mla_bwd", grid=(N_HEADS_B, nb),
        in_specs=[pl.BlockSpec((_TQ, HEAD_PAD), lambda h, s: (nb - 1 - s, h)), blk4(HEAD_PAD), blk4(V_DIM_B),
                  per_head, head4(HEAD_PAD), head3(V_DIM_B), head4(V_DIM_B), head4(1), head4(1), _HBM],
        out_specs=[per_head, blk4(HEAD_PAD), blk4(V_DIM_B), _HBM],
        out_shape=[jax.ShapeDtypeStruct((t, MLA_W), F32), jax.ShapeDtypeStruct((N_HEADS_B, nb, HEAD_PAD, _TQ), F32),
                   jax.ShapeDtypeStruct((N_HEADS_B, nb, V_DIM_B, _TQ), F32),
                   jax.ShapeDtypeStruct((3,) + gp.shape[1:], gp.dtype)],
        scratch_shapes=[pltpu.VMEM((t, LANES), F32), pltpu.VMEM((t, LANES), F32),
                        pltpu.SemaphoreType.DMA((3,)), pltpu.SemaphoreType.DMA((3,))],
        compiler_params=_params(("arbitrary", "arbitrary")),
    )(k, kt, vt, q, qt, d_out, d_out_t, lse, delta, gp)


def _mla_prep_bwd(dq, dkt, dvt, proj, posc, freq, qan, kvan, wq, wk, wv, swap_src):
    t = dq.shape[0]
    tm = _TQ

    def body(dq_ref, dkt_ref, dvt_ref, cq_ref, ckv_ref, pos_ref, f_ref, qan_ref, kvan_ref, wq_ref, wk_ref, wv_ref, src_ref,
             dcq_ref, dckv_ref, dkr_ref, dwq_ref, dwk_ref, dwv_ref, dqan_ref, dkvan_ref, got_ref, send_sem, recv_sem):
        swap = _sibling_copy(src_ref, got_ref, send_sem, recv_sem)

        @pl.when(pl.program_id(0) == 0)
        def _():
            swap.start()
            for r in (dwq_ref, dwk_ref, dwv_ref, dqan_ref, dkvan_ref):
                r[...] = jnp.zeros(r.shape, F32)

        cq = cq_ref[...]
        rq = _rms(cq)
        nq_ = cq * rq
        cqn = (nq_ * qan_ref[...]).astype(BF16)
        ckv = ckv_ref[...]
        rkv = _rms(ckv)
        nkv = ckv * rkv
        ckvn = (nkv * kvan_ref[...]).astype(BF16)
        c, s, lo, hi = _rope_coeffs(pos_ref[...], f_ref[...])
        dkr = jnp.zeros((tm, LANES), F32)
        dqb, dkb = [], []
        for h in range(N_HEADS_B):
            dqb.append(_unrope(dq_ref[:, HEAD_PAD * h:HEAD_PAD * (h + 1)], c, s, lo, hi).astype(BF16))
            dk_h = dkt_ref[h, 0].T
            dkr = dkr + dk_h
            dkb.append(dk_h.astype(BF16))
        dqb, dkb = jnp.concatenate(dqb, axis=1), jnp.concatenate(dkb, axis=1)
        dkr_ref[...] = jnp.where(lo | hi, _unrope(dkr, c, s, lo, hi), 0.0).astype(BF16)
        dvb = dvt_ref[...].reshape(N_HEADS_B * V_DIM_B, tm).T.astype(BF16)
        dwq_ref[...] += _dot_tn(cqn, dqb)
        dwk_ref[...] += _dot_tn(ckvn, dkb)
        dwv_ref[...] += _dot_tn(ckvn, dvb)
        dcqn = _dot_nt(dqb, wq_ref[...])
        dckvn = _dot_nt(dkb, wk_ref[...]) + _dot_nt(dvb, wv_ref[...])
        dcq, dqan = _norm_bwd(dcqn, nq_, rq, qan_ref[...])
        dckv, dkvan = _norm_bwd(dckvn, nkv, rkv, kvan_ref[...])
        dcq_ref[...] = dcq.astype(BF16)
        dckv_ref[...] = dckv.astype(BF16)
        dqan_ref[...] += dqan
        dkvan_ref[...] += dkvan

        @pl.when(pl.program_id(0) == t // tm - 1)
        def _():
            swap.wait_recv()
            swap.wait_send()

    row = lambda i: (i, 0)
    vw = N_HEADS_B * V_DIM_B
    return pl.pallas_call(
        body, name="mla_prep_bwd", grid=(t // tm,),
        in_specs=[pl.BlockSpec((tm, MLA_W), row), pl.BlockSpec((N_HEADS_B, 1, HEAD_PAD, tm), lambda i: (0, i, 0, 0)),
                  pl.BlockSpec((N_HEADS_B, 1, V_DIM_B, tm), lambda i: (0, i, 0, 0)),
                  pl.BlockSpec((tm, Q_LORA), lambda i: (i, _CQ_BLK)),
                  pl.BlockSpec((tm, LANES), lambda i: (i, _CKV_BLK)),
                  pl.BlockSpec((tm, 1), row), _full((1, LANES)), _full((1, Q_LORA)), _full((1, KV_LORA)),
                  _full((Q_LORA, MLA_W)), _full((KV_LORA, MLA_W)), _full((KV_LORA, vw)), _HBM],
        out_specs=[pl.BlockSpec((tm, Q_LORA), row), pl.BlockSpec((tm, LANES), row), pl.BlockSpec((tm, LANES), row),
                   _full((Q_LORA, MLA_W)), _full((KV_LORA, MLA_W)), _full((KV_LORA, vw)),
                   _full((1, Q_LORA)), _full((1, KV_LORA)), _HBM],
        out_shape=[jax.ShapeDtypeStruct((t, Q_LORA), BF16), jax.ShapeDtypeStruct((t, LANES), BF16),
                   jax.ShapeDtypeStruct((t, LANES), BF16),
                   jax.ShapeDtypeStruct((Q_LORA, MLA_W), F32), jax.ShapeDtypeStruct((KV_LORA, MLA_W), F32),
                   jax.ShapeDtypeStruct((KV_LORA, vw), F32),
                   jax.ShapeDtypeStruct((1, Q_LORA), F32), jax.ShapeDtypeStruct((1, KV_LORA), F32),
                   jax.ShapeDtypeStruct(swap_src.shape, swap_src.dtype)],
        scratch_shapes=[pltpu.SemaphoreType.DMA(()), pltpu.SemaphoreType.DMA(())],
        compiler_params=_params(("arbitrary",)),
    )(dq, dkt, dvt, proj, proj, posc, freq, qan, kvan, wq, wk, wv, swap_src)


def _swa_bwd(proj, d_out, lse, delta, posc, posr, sinks):
    t = proj.shape[0]
    nb = t // BLOCK

    def body(q_ref, kc_ref, kp_ref, vc_ref, vp_ref, do_ref, l_ref, d_ref, pq_ref, pc_ref, pp_ref, sink_ref,
             dq_ref, dk_ref, dv_ref, ds_ref, dkb_s, dvb_s, dk_carry, dv_carry):
        n = pl.program_id(0)

        @pl.when(n == 0)
        def _():
            ds_ref[...] = jnp.zeros(ds_ref.shape, F32)
            dk_carry[...] = jnp.zeros(dk_carry.shape, F32)
            dv_carry[...] = jnp.zeros(dv_carry.shape, F32)

        @pl.when(n < nb)
        def _():
            kb, vb, dist, valid = _swa_band(n, kp_ref, kc_ref, vp_ref, vc_ref, pq_ref, pp_ref, pc_ref)
            qv, dov = q_ref[...], do_ref[...]
            q_t, do_t, kb_t = qv.T, dov.T, kb.T
            lane = lax.broadcasted_iota(jnp.int32, (1, LANES), 1)
            dsink = jnp.zeros((1, LANES), F32)
            dq_t = []
            for kh in range(N_KV_A):
                heads = range(_GROUP_A * kh, _GROUP_A * (kh + 1))
                st_g = _dot(_head_cols(kb, kh).astype(BF16), _group_t(q_t, kh))
                dpt_g = _dot(_head_cols(vb, kh).astype(BF16), _group_t(do_t, kh))
                pts, dsts = [], []
                for j, h in enumerate(heads):
                    st = _swa_scores_t(st_g, j, h, dist, valid)
                    l_h, d_h = l_ref[h:h + 1, :], d_ref[h:h + 1, :]
                    pt = jnp.exp2(st - l_h)
                    p_sink = jnp.exp2(sink_ref[0:1, h:h + 1] * _LOG2E - l_h)
                    dsink = jnp.where(lane == h, jnp.sum(-p_sink * d_h, axis=1, keepdims=True), dsink)
                    dst = pt * (dpt_g[:, BLOCK * j:BLOCK * (j + 1)] - d_h) * _SWA_SCALE
                    pts.append(pt.astype(BF16))
                    dsts.append(dst.astype(BF16))
                pt_g, dst_g = jnp.concatenate(pts, axis=1), jnp.concatenate(dsts, axis=1)
                q_g = jnp.concatenate([_head_cols(qv, h) for h in heads], axis=0).astype(BF16)
                do_g = jnp.concatenate([_head_cols(dov, h) for h in heads], axis=0).astype(BF16)
                dkb_s[:, HEAD_DIM_A * kh:HEAD_DIM_A * (kh + 1)] = _dot(dst_g, q_g)
                dvb_s[:, HEAD_DIM_A * kh:HEAD_DIM_A * (kh + 1)] = _dot(pt_g, do_g)
                dq_g = _dot(_head_rows(kb_t, kh).astype(BF16), dst_g)
                dq_t.extend(dq_g[:, BLOCK * j:BLOCK * (j + 1)] for j in range(_GROUP_A))
            dq_ref[...] = jnp.concatenate(dq_t, axis=0).T
            ds_ref[...] += dsink
            dk_ref[...] = dk_carry[...] + dkb_s[0:BLOCK, :]
            dv_ref[...] = dv_carry[...] + dvb_s[0:BLOCK, :]
            dk_carry[...] = dkb_s[BLOCK:2 * BLOCK, :]
            dv_carry[...] = dvb_s[BLOCK:2 * BLOCK, :]

        @pl.when(n == nb)
        def _():
            dk_ref[...] = dk_carry[...]
            dv_ref[...] = dv_carry[...]

    cur = lambda n: (jnp.minimum(n, nb - 1), 0)
    cur_t = lambda n: (0, jnp.minimum(n, nb - 1))
    prv = lambda n: jnp.maximum(jnp.minimum(n, nb - 1) - 1, 0)
    out_prev = lambda n: (jnp.maximum(n - 1, 0), 0)
    return pl.pallas_call(
        body, name="swa_bwd", grid=(nb + 1,),
        in_specs=[pl.BlockSpec((BLOCK, WIDTH_A), lambda n: (jnp.minimum(n, nb - 1), _QA_BLK)),
                  pl.BlockSpec((BLOCK, LANES), lambda n: (jnp.minimum(n, nb - 1), _KA_BLK)),
                  pl.BlockSpec((BLOCK, LANES), lambda n: (prv(n), _KA_BLK)),
                  pl.BlockSpec((BLOCK, LANES), lambda n: (jnp.minimum(n, nb - 1), _VA_BLK)),
                  pl.BlockSpec((BLOCK, LANES), lambda n: (prv(n), _VA_BLK)),
                  pl.BlockSpec((BLOCK, WIDTH_A), cur), pl.BlockSpec((N_HEADS_A, BLOCK), cur_t),
                  pl.BlockSpec((N_HEADS_A, BLOCK), cur_t), pl.BlockSpec((1, BLOCK), cur_t),
                  pl.BlockSpec((BLOCK, 1), cur), pl.BlockSpec((BLOCK, 1), lambda n: (prv(n), 0)),
                  _full((1, N_HEADS_A))],
        out_specs=[pl.BlockSpec((BLOCK, WIDTH_A), cur), pl.BlockSpec((BLOCK, LANES), out_prev),
                   pl.BlockSpec((BLOCK, LANES), out_prev), _full((1, LANES))],
        out_shape=[jax.ShapeDtypeStruct((t, WIDTH_A), F32), jax.ShapeDtypeStruct((t, LANES), F32),
                   jax.ShapeDtypeStruct((t, LANES), F32), jax.ShapeDtypeStruct((1, LANES), F32)],
        scratch_shapes=[pltpu.VMEM((2 * BLOCK, LANES), F32), pltpu.VMEM((2 * BLOCK, LANES), F32),
                        pltpu.VMEM((BLOCK, LANES), F32), pltpu.VMEM((BLOCK, LANES), F32)],
        compiler_params=_params(("arbitrary",)),
    )(proj, proj, proj, proj, proj, d_out, lse, delta, posr, posc, posc, sinks)


def _in_bwd(dproj, w_in_t, x, dx1, g1, gp):
    t = x.shape[0]
    tm = 512
    steps = t // tm

    def body(dp_ref, w_ref, x_ref, dx1_ref, g_ref, gp_ref, dx_ref, dg_ref, land_ref, send_sems, recv_sems):
        i = pl.program_id(0)

        @pl.when(i == 0)
        def _():
            dg_ref[...] = jnp.zeros(dg_ref.shape, F32)
            _scatter_start(gp_ref, land_ref, send_sems, recv_sems)

        for rows in _row_halves(tm):
            dh = _dot(dp_ref[rows, :], w_ref[...])
            xv = x_ref[rows, :]
            r = _rms(xv)
            dx, dg = _norm_bwd(dh, xv * r, r, g_ref[...])
            dx_ref[rows, :] = dx1_ref[rows, :] + dx
            dg_ref[...] += dg

        @pl.when(i == steps - 1)
        def _():
            _scatter_wait(gp_ref, land_ref, send_sems, recv_sems)

    row = lambda i: (i, 0)
    blk = pl.BlockSpec((tm, D_MODEL), row)
    return pl.pallas_call(
        body, name="in_bwd", grid=(steps,),
        in_specs=[pl.BlockSpec((tm, D_IN_PAD), row), _full((D_IN_PAD, D_MODEL)), blk, blk, _full((1, D_MODEL)), _HBM],
        out_specs=[blk, _full((1, D_MODEL)), _HBM],
        out_shape=[jax.ShapeDtypeStruct((t, D_MODEL), F32), jax.ShapeDtypeStruct((1, D_MODEL), F32),
                   jax.ShapeDtypeStruct((3,) + gp.shape[1:], gp.dtype)],
        scratch_shapes=[pltpu.SemaphoreType.DMA((3,)), pltpu.SemaphoreType.DMA((3,))],
        compiler_params=_params(("arbitrary",)),
    )(dproj, w_in_t, x, dx1, g1, gp)


def _adamw_store(w, g, m, v, out_refs):
    g_out, d_out, m_out, v_out = out_refs
    m_new = ADAM_B1 * m + (1.0 - ADAM_B1) * g
    v_new = ADAM_B2 * v + (1.0 - ADAM_B2) * jnp.square(g)
    m_hat = m_new / (1.0 - ADAM_B1 ** ADAM_STEP)
    v_hat = v_new / (1.0 - ADAM_B2 ** ADAM_STEP)
    g_out[...] = g
    d_out[...] = -ADAM_LR * (m_hat / (jnp.sqrt(v_hat) + ADAM_EPS) + ADAM_WD * w)
    m_out[...] = m_new
    v_out[...] = v_new


_SMALL_SLOTS = {"pre_norm_mix": (0, 0, D_MODEL), "post_norm_mix": (1, 0, D_MODEL), "pre_norm_mlp": (2, 0, D_MODEL),
                "post_norm_mlp": (3, 0, D_MODEL), "q_a_norm": (4, 0, Q_LORA), "kv_a_norm": (4, Q_LORA, KV_LORA),
                "sinks": (4, Q_LORA + KV_LORA, N_HEADS_A)}
_LOSS_ROW = 5


def _adamw_small(red, w, m, v):
    names = tuple(_SMALL_SLOTS)
    n = len(names)

    def body(*refs):
        red_ref, ws, ms, vs, outs = refs[0], refs[1:1 + n], refs[1 + n:1 + 2 * n], refs[1 + 2 * n:1 + 3 * n], refs[1 + 3 * n:]
        for k, name in enumerate(names):
            row, lane, width = _SMALL_SLOTS[name]
            g = red_ref[row:row + 1, lane:lane + width]
            _adamw_store(ws[k][...], g, ms[k][...], vs[k][...], outs[4 * k:4 * k + 4])

    vmem = pl.BlockSpec(memory_space=pltpu.VMEM)
    res = pl.pallas_call(
        body, name="adamw_small", in_specs=[vmem] * (1 + 3 * n), out_specs=[vmem] * (4 * n),
        out_shape=[jax.ShapeDtypeStruct(w[name].shape, F32) for name in names for _ in range(4)],
    )(red, *[w[k] for k in names], *[m[k] for k in names], *[v[k] for k in names])
    return {name: res[4 * k:4 * k + 4] for k, name in enumerate(names)}


def _adamw(w, g_parts, m, v, name, block, g_row_off=0):
    r, c = w.shape
    br, bc = block
    ng = len(g_parts)

    def body(*refs):
        w_ref, g_refs, m_ref, v_ref = refs[0], refs[1:1 + ng], refs[1 + ng], refs[2 + ng]
        g = g_refs[0][...]
        for gr in g_refs[1:]:
            g = g + gr[...]
        _adamw_store(w_ref[...], g, m_ref[...], v_ref[...], refs[3 + ng:])

    assert g_row_off % br == 0 and r % br == 0 and c % bc == 0
    blk = pl.BlockSpec(block, lambda i, j: (i, j))
    g_blk = pl.BlockSpec(block, lambda i, j: (i + g_row_off // br, j))
    return pl.pallas_call(
        body, name=name, grid=(r // br, c // bc),
        in_specs=[blk] + [g_blk] * ng + [blk, blk], out_specs=[blk] * 4,
        out_shape=[jax.ShapeDtypeStruct((r, c), F32)] * 4,
        compiler_params=_params(("parallel", "parallel")),
    )(w, *g_parts, m, v)


_HBM = pl.BlockSpec(memory_space=pltpu.HBM)


def _other_chips(x, y):
    return ((1 - x, y), (x, 1 - y), (1 - x, 1 - y))


def _gather_copies(src, out, send_sems, recv_sems, local_sem):
    x, y, c = lax.axis_index("x"), lax.axis_index("y"), lax.axis_index("c")
    me = 2 * x + y
    local = pltpu.make_async_copy(src, out.at[me], local_sem)

    def copies(arriving):
        return [pltpu.make_async_remote_copy(src_ref=src, dst_ref=out.at[2 * px + py if arriving else me],
                                             send_sem=send_sems.at[j], recv_sem=recv_sems.at[j], device_id=(px, py, c),
                                             device_id_type=MESH)
                for j, (px, py) in enumerate(_other_chips(x, y))]

    return local, copies


def _gather_start(src, out, send_sems, recv_sems, local_sem):
    local, copies = _gather_copies(src, out, send_sems, recv_sems, local_sem)
    local.start()
    for cp in copies(False):
        cp.start()


def _gather_wait(src, out, send_sems, recv_sems, local_sem):
    local, copies = _gather_copies(src, out, send_sems, recv_sems, local_sem)
    for cp in copies(True):
        cp.wait_recv()
    for cp in copies(False):
        cp.wait_send()
    local.wait()


def _scatter_copies(src, land, send_sems, recv_sems):
    x, y, c = lax.axis_index("x"), lax.axis_index("y"), lax.axis_index("c")
    return [pltpu.make_async_remote_copy(src_ref=src.at[2 * px + py], dst_ref=land.at[j], send_sem=send_sems.at[j],
                                         recv_sem=recv_sems.at[j], device_id=(px, py, c), device_id_type=MESH)
            for j, (px, py) in enumerate(_other_chips(x, y))]


def _scatter_start(src, land, send_sems, recv_sems):
    for cp in _scatter_copies(src, land, send_sems, recv_sems):
        cp.start()


def _scatter_wait(src, land, send_sems, recv_sems):
    copies = _scatter_copies(src, land, send_sems, recv_sems)
    for cp in copies:
        cp.wait_recv()
    for cp in copies:
        cp.wait_send()


def _all_gather_chips(packed):
    r = packed.shape[0]
    half = r // 2

    def body(src, out, ici_send, ici_recv, d2d_send, d2d_recv, local_sem):
        x, y, c = lax.axis_index("x"), lax.axis_index("y"), lax.axis_index("c")
        me = 2 * x + y
        mine = pl.ds(pl.multiple_of(c * half, 16), half)
        theirs = pl.ds(pl.multiple_of((1 - c) * half, 16), half)
        chips = _other_chips(x, y)
        local = pltpu.make_async_copy(src, out.at[me], local_sem)
        local.start()
        sends = [pltpu.make_async_remote_copy(src_ref=src.at[mine], dst_ref=out.at[me, mine], send_sem=ici_send.at[j],
                                              recv_sem=ici_recv.at[j], device_id=(px, py, c), device_id_type=MESH)
                 for j, (px, py) in enumerate(chips)]
        for cp in sends:
            cp.start()
        passed = []
        for j, (px, py) in enumerate(chips):
            block = 2 * px + py
            pltpu.make_async_remote_copy(src_ref=src.at[mine], dst_ref=out.at[block, mine], send_sem=ici_send.at[j],
                                         recv_sem=ici_recv.at[j], device_id=(px, py, c), device_id_type=MESH).wait_recv()
            cp = pltpu.make_async_remote_copy(src_ref=out.at[block, mine], dst_ref=out.at[block, mine],
                                              send_sem=d2d_send.at[j], recv_sem=d2d_recv.at[j],
                                              device_id=(x, y, 1 - c), device_id_type=MESH)
            cp.start()
            passed.append(cp)
        for j, (px, py) in enumerate(chips):
            block = 2 * px + py
            pltpu.make_async_remote_copy(src_ref=out.at[block, theirs], dst_ref=out.at[block, theirs],
                                         send_sem=d2d_send.at[j], recv_sem=d2d_recv.at[j],
                                         device_id=(x, y, 1 - c), device_id_type=MESH).wait_recv()
        for cp in sends + passed:
            cp.wait_send()
        local.wait()

    sems = pltpu.SemaphoreType.DMA((3,))
    return pl.pallas_call(
        body, name="ag_weights", in_specs=[_HBM], out_specs=_HBM,
        out_shape=jax.ShapeDtypeStruct((N_CHIPS,) + packed.shape, packed.dtype),
        scratch_shapes=[sems, sems, sems, sems, pltpu.SemaphoreType.DMA(())],
    )(packed)


def _sum4(gp, land, chip, name):
    _, r, w = gp.shape
    tr = 128

    def body(chip_ref, o_ref, l_ref, s_ref):
        s_ref[...] = ((o_ref[0] + l_ref[0].astype(F32)) + l_ref[1].astype(F32)) + l_ref[2].astype(F32)

    return pl.pallas_call(
        body, name=name,
        grid_spec=pltpu.PrefetchScalarGridSpec(
            num_scalar_prefetch=1, grid=(r // tr,),
            in_specs=[pl.BlockSpec((1, tr, w), lambda i, chip_ref: (chip_ref[0], i, 0)),
                      pl.BlockSpec((3, tr, w), lambda i, chip_ref: (0, i, 0))],
            out_specs=pl.BlockSpec((tr, w), lambda i, chip_ref: (i, 0))),
        out_shape=jax.ShapeDtypeStruct((r, w), F32),
        compiler_params=_params(("parallel",)),
    )(chip, gp, land)


def _sibling_copy(src, got, send_sem, recv_sem):
    x, y, c = lax.axis_index("x"), lax.axis_index("y"), lax.axis_index("c")
    return pltpu.make_async_remote_copy(src_ref=src, dst_ref=got, send_sem=send_sem, recv_sem=recv_sem,
                                        device_id=(x, y, 1 - c), device_id_type=MESH)


def _swap_sibling(s, name):
    def body(src, got, send_sem, recv_sem):
        cp = _sibling_copy(src, got, send_sem, recv_sem)
        cp.start()
        cp.wait_recv()
        cp.wait_send()

    return pl.pallas_call(
        body, name=name, in_specs=[_HBM], out_specs=_HBM,
        out_shape=jax.ShapeDtypeStruct(s.shape, s.dtype),
        scratch_shapes=[pltpu.SemaphoreType.DMA(()), pltpu.SemaphoreType.DMA(())],
    )(s)


def _all_reduce_small(dsmall, loss):
    n_dev = 8
    names = tuple(_SMALL_SLOTS)
    shape = (8, D_MODEL)

    def body(*refs):
        parts, loss_ref = refs[:len(names)], refs[len(names)]
        out, src, gath, send_sems, recv_sems = refs[len(names) + 1:]
        x, y, c = lax.axis_index("x"), lax.axis_index("y"), lax.axis_index("c")
        me = 4 * x + 2 * y + c
        src[...] = jnp.zeros(shape, F32)
        for name, part in zip(names, parts):
            row, lane, _ = _SMALL_SLOTS[name]
            src[row:row + 1, lane:lane + part.shape[1]] = part[...]
        src[_LOSS_ROW:_LOSS_ROW + 1, 0:LANES] = loss_ref[...]
        gath[me] = src[...]
        peers = []
        for k in range(1, n_dev):
            px = 1 - x if (k >> 2) & 1 else x
            py = 1 - y if (k >> 1) & 1 else y
            pc = 1 - c if k & 1 else c
            peers.append((px, py, pc))
        sends = []
        for j, peer in enumerate(peers):
            cp = pltpu.make_async_remote_copy(src_ref=src, dst_ref=gath.at[me], send_sem=send_sems.at[j],
                                              recv_sem=recv_sems.at[j], device_id=peer, device_id_type=MESH)
            cp.start()
            sends.append(cp)
        for j, (px, py, pc) in enumerate(peers):
            pltpu.make_async_remote_copy(src_ref=src, dst_ref=gath.at[4 * px + 2 * py + pc], send_sem=send_sems.at[j],
                                         recv_sem=recv_sems.at[j], device_id=(px, py, pc), device_id_type=MESH).wait_recv()
        for cp in sends:
            cp.wait_send()
        acc = gath[0]
        for d in range(1, n_dev):
            acc = acc + gath[d]
        out[...] = acc

    vmem = pl.BlockSpec(memory_space=pltpu.VMEM)
    return pl.pallas_call(
        body, name="ar_small", in_specs=[vmem] * (len(names) + 1), out_specs=vmem,
        out_shape=jax.ShapeDtypeStruct(shape, F32),
        scratch_shapes=[pltpu.VMEM(shape, F32), pltpu.VMEM((n_dev,) + shape, F32),
                        pltpu.SemaphoreType.DMA((n_dev - 1,)), pltpu.SemaphoreType.DMA((n_dev - 1,))],
    )(*[dsmall[k] for k in names], loss)


_W_IN_ROWS = SHARD_SHAPES["w_in"][1]
_KR_ROW = 3200
_KR_PAD_ROW = _KR_BLK * LANES + QK_NOPE


def _shard_rows(name, a):
    return jnp.transpose(a) if name == "w_in" else a.reshape(PACK_ROWS[name], D_MODEL)


def _pack(group, shards, dtype):
    parts = [_shard_rows(n, shards[n]).astype(dtype) for n in group]
    pad = -sum(PACK_ROWS[n] for n in group) % LANES
    if pad:
        parts.append(jnp.zeros((pad, D_MODEL), dtype))
    return jnp.concatenate(parts, axis=0)


def _col_sharded_full(g, name, group):
    r, c = SHARD_SHAPES[name]
    off = _row_offset(group, name)
    blocks = g[:, off:off + PACK_ROWS[name]].reshape(N_CHIPS, r, c)
    return jnp.transpose(blocks, (1, 0, 2)).reshape(r, N_CHIPS * c)


def _col_sharded_blocks(d, name):
    r, c = SHARD_SHAPES[name]
    return jnp.transpose(d.reshape(r, N_CHIPS, c), (1, 0, 2)).reshape(N_CHIPS, PACK_ROWS[name], D_MODEL)


def _weights_a(g):
    dt = g.dtype
    w_in_t = g[:, :_W_IN_ROWS].reshape(N_CHIPS * _W_IN_ROWS, D_MODEL)
    z = lambda n: jnp.zeros((n, D_MODEL), dt)
    w_in_t = jnp.concatenate([w_in_t[:_KR_ROW], z(_KR_PAD_ROW - _KR_ROW), w_in_t[_KR_ROW:],
                              z(D_IN_PAD - _KR_PAD_ROW - QK_ROPE)], axis=0)
    wq = _col_sharded_full(g, "w_q_b", GROUP_A).reshape(Q_LORA, N_HEADS_B, Q_HEAD_B)
    wq_p = jnp.concatenate([wq, jnp.zeros((Q_LORA, N_HEADS_B, HEAD_PAD - Q_HEAD_B), dt)], axis=2).reshape(Q_LORA, MLA_W)
    wkv = _col_sharded_full(g, "w_kv_b", GROUP_A).reshape(KV_LORA, N_HEADS_B, QK_NOPE + V_DIM_B)
    zk = jnp.zeros((KV_LORA, N_HEADS_B, HEAD_PAD - QK_NOPE), dt)
    wk_p = jnp.concatenate([wkv[:, :, :QK_NOPE], zk], axis=2).reshape(KV_LORA, MLA_W)
    wv = wkv[:, :, QK_NOPE:].reshape(KV_LORA, N_HEADS_B * V_DIM_B)
    return dict(w_in=w_in_t, wq=wq_p, wk=wk_p, wv=wv, wv_t=jnp.transpose(wv))


def _grad_blocks_a(dw_in_t, dwq_p, dwk_p, dwv):
    dw_in = jnp.concatenate([dw_in_t[:_KR_ROW], dw_in_t[_KR_PAD_ROW:_KR_PAD_ROW + QK_ROPE]], axis=0)
    dwq = dwq_p.reshape(Q_LORA, N_HEADS_B, HEAD_PAD)[:, :, :Q_HEAD_B].reshape(Q_LORA, N_HEADS_B * Q_HEAD_B)
    dwk = dwk_p.reshape(KV_LORA, N_HEADS_B, HEAD_PAD)[:, :, :QK_NOPE]
    dwkv = jnp.concatenate([dwk, dwv.reshape(KV_LORA, N_HEADS_B, V_DIM_B)], axis=2)
    dwkv = dwkv.reshape(KV_LORA, N_HEADS_B * (QK_NOPE + V_DIM_B))
    pad = -sum(PACK_ROWS[n] for n in GROUP_A) % LANES
    return jnp.concatenate([dw_in.reshape(N_CHIPS, _W_IN_ROWS, D_MODEL), _col_sharded_blocks(dwq, "w_q_b"),
                            _col_sharded_blocks(dwkv, "w_kv_b"), jnp.zeros((N_CHIPS, pad, D_MODEL), F32)], axis=1)


def _rope_freq_lanes():
    freqs = ROPE_THETA ** (-jnp.arange(0, QK_ROPE, 2, dtype=F32) / QK_ROPE)
    return jnp.concatenate([jnp.zeros((QK_NOPE,), F32), freqs, freqs,
                            jnp.zeros((HEAD_PAD - Q_HEAD_B,), F32)]).reshape(1, LANES)


def _fwd_bwd(x, positions, target, w):
    t = x.shape[0]
    wa = _weights_a(_all_gather_chips(_pack(GROUP_A, w, BF16)))
    posr = positions.astype(F32).reshape(1, t)
    posc = posr.reshape(t, 1)
    freq = _rope_freq_lanes()
    g1, g2, g3, g4 = w["pre_norm_mix"], w["post_norm_mix"], w["pre_norm_mlp"], w["post_norm_mlp"]
    qan, kvan, sinks = w["q_a_norm"], w["kv_a_norm"], w["sinks"]

    h, proj = _proj_fwd(x, g1, wa["w_in"])
    out_a, lse_a = _swa_fwd(proj, posc, posr, sinks)
    qm, km, qt, kt, vt = _mla_prep_fwd(proj, posc, freq, qan, kvan, wa["wq"], wa["wk"], wa["wv_t"])
    out_bt, lse_b, wb = _mla_fwd(km, qt, vt, _pack(GROUP_B, w, BF16))
    w_oa, w_ob = _col_sharded_full(wb, "w_o_a", GROUP_B), _col_sharded_full(wb, "w_o_b", GROUP_B)
    merged, y, x1, h2 = _mix_out_fwd(out_a, out_bt, proj, x, w_oa, w_ob, wb, g2, g3)
    a = _up_fwd(h2, wb)
    dx2, dyd, dg4, loss = _down_fwd_loss(a, wb, x1, target, g4)

    gp_b = _dw_into_blocks(a, dyd, "w_down", 1024, _TK_DW)
    du = _down_bwd(dyd, wb, a)
    gp_b = _dw_into_blocks(h2, du, "w_up", 1024, _TK_DW, gp_b)
    dx1, dy, dg3, dg2 = _up_bwd(du, wb, x1, dx2, y, g3, g2)
    gp_b = _dw_into_blocks(merged, dy, "w_out", 1024, _TK_DW, gp_b)
    doa, dob, dga, dgb, d_out_a, d_out_b, d_out_bt, del_a, del_b = _mix_out_bwd(dy, out_a, out_bt, proj, w_oa, w_ob, wb)
    dw_oa = _matmul_tn(out_a, doa, "dw_o_a", 512, 1024)
    dw_ob = _dw_ob(out_bt, dob)
    small_b = jnp.concatenate([_col_sharded_blocks(dw_oa, "w_o_a"), _col_sharded_blocks(dw_ob, "w_o_b")], axis=1)
    gp_b = lax.dynamic_update_slice(gp_b, small_b, (0, _row_offset(GROUP_B, "w_o_a"), 0))
    dqm, dkm, dvm, land_b = _mla_bwd(qm, km, qt, kt, vt, d_out_b, d_out_bt, lse_b, del_b, gp_b)
    chip = (2 * lax.axis_index("x") + lax.axis_index("y")).astype(jnp.int32).reshape(1)
    part_b = _sum4(gp_b, land_b, chip, "rs_sum_b")
    dcq, dckv, dkr, dwq, dwk, dwv, dqan, dkvan, sib_b = _mla_prep_bwd(
        dqm, dkm, dvm, proj, posc, freq, qan, kvan, wa["wq"], wa["wk"], wa["wv"], part_b)
    dqa, dka, dva, dsinks = _swa_bwd(proj, d_out_a, lse_a, del_a, posc, posr, sinks)
    dproj = jnp.concatenate([dga, dgb, dqa.astype(BF16), dka.astype(BF16), dva.astype(BF16), dcq, dckv, dkr], axis=1)
    dw_in_t = _matmul_tn(dproj, h, "dw_in", D_IN_PAD // 2, 1024, tk=1024)
    gp_a = _grad_blocks_a(dw_in_t, dwq, dwk, dwv)
    grad_x, dg1, land_a = _in_bwd(dproj, wa["w_in"], x, dx1, g1, gp_a.astype(BF16))

    part_a = _sum4(gp_a, land_a, chip, "rs_sum_a")
    reduced = {GROUP_A: [part_a, _swap_sibling(part_a, "rs_swap_a")], GROUP_B: [part_b, sib_b]}
    dsmall = dict(pre_norm_mix=dg1, post_norm_mix=dg2, pre_norm_mlp=dg3, post_norm_mlp=dg4,
                  q_a_norm=dqan, kv_a_norm=dkvan, sinks=dsinks)
    return loss, grad_x, reduced, dsmall


def kernel(x, positions, pre_norm_mix, w_in, q_a_norm, w_q_b, kv_a_norm, w_kv_b, sinks, w_o_a, w_o_b, w_out, post_norm_mix, pre_norm_mlp, w_up, w_down, post_norm_mlp, loss_target, m_pre_norm_mix, m_w_in, m_q_a_norm, m_w_q_b, m_kv_a_norm, m_w_kv_b, m_sinks, m_w_o_a, m_w_o_b, m_w_out, m_post_norm_mix, m_pre_norm_mlp, m_w_up, m_w_down, m_post_norm_mlp, v_pre_norm_mix, v_w_in, v_q_a_norm, v_w_q_b, v_kv_a_norm, v_w_kv_b, v_sinks, v_w_o_a, v_w_o_b, v_w_out, v_post_norm_mix, v_pre_norm_mlp, v_w_up, v_w_down, v_post_norm_mlp):
    w = dict(pre_norm_mix=pre_norm_mix, w_in=w_in[0], q_a_norm=q_a_norm, w_q_b=w_q_b[0], kv_a_norm=kv_a_norm,
             w_kv_b=w_kv_b[0], sinks=sinks, w_o_a=w_o_a[0], w_o_b=w_o_b[0], w_out=w_out[0],
             post_norm_mix=post_norm_mix, pre_norm_mlp=pre_norm_mlp, w_up=w_up[0], w_down=w_down[0],
             post_norm_mlp=post_norm_mlp)
    m = dict(pre_norm_mix=m_pre_norm_mix, w_in=m_w_in[0], q_a_norm=m_q_a_norm, w_q_b=m_w_q_b[0],
             kv_a_norm=m_kv_a_norm, w_kv_b=m_w_kv_b[0], sinks=m_sinks, w_o_a=m_w_o_a[0], w_o_b=m_w_o_b[0],
             w_out=m_w_out[0], post_norm_mix=m_post_norm_mix, pre_norm_mlp=m_pre_norm_mlp, w_up=m_w_up[0],
             w_down=m_w_down[0], post_norm_mlp=m_post_norm_mlp)
    v = dict(pre_norm_mix=v_pre_norm_mix, w_in=v_w_in[0], q_a_norm=v_q_a_norm, w_q_b=v_w_q_b[0],
             kv_a_norm=v_kv_a_norm, w_kv_b=v_w_kv_b[0], sinks=v_sinks, w_o_a=v_w_o_a[0], w_o_b=v_w_o_b[0],
             w_out=v_w_out[0], post_norm_mix=v_post_norm_mix, pre_norm_mlp=v_pre_norm_mlp, w_up=v_w_up[0],
             w_down=v_w_down[0], post_norm_mlp=v_post_norm_mlp)

    loss, grad_x, reduced, dsmall = _fwd_bwd(x[0], positions, loss_target[0], w)

    red = _all_reduce_small(dsmall, loss)
    small = _adamw_small(red, w, m, v)

    big = {}
    tr = jnp.transpose
    big["w_in"] = [tr(o)[None] for o in _adamw(tr(w["w_in"]), reduced[GROUP_A], tr(m["w_in"]), tr(v["w_in"]),
                                               "adamw_w_in", (_W_IN_ROWS, 256))]
    for n in ("w_up", "w_down", "w_out"):
        big[n] = [o[None] for o in _adamw(w[n], reduced[GROUP_B], m[n], v[n], "adamw_" + n, (128, D_MODEL),
                                          _row_offset(GROUP_B, n))]
    for group, names in ((GROUP_A, ("w_q_b", "w_kv_b")), (GROUP_B, ("w_o_a", "w_o_b"))):
        for n in names:
            off = _row_offset(group, n)
            g_parts = [p[off:off + PACK_ROWS[n]].reshape(SHARD_SHAPES[n]) for p in reduced[group]]
            big[n] = [o[None] for o in _adamw(w[n], g_parts, m[n], v[n], "adamw_" + n, SHARD_SHAPES[n])]

    outs = [big[n][k] if n in big else small[n][k] for k in range(4) for n in WEIGHTS]
    return (red[_LOSS_ROW, 0], grad_x[None], *outs)
```

```python
import jax
import jax.numpy as jnp
from jax import lax
from jax.experimental import pallas as pl
from jax.experimental.pallas import tpu as pltpu

F32 = jnp.float32
BF16 = jnp.bfloat16
MESH = pl.DeviceIdType.MESH

D_MODEL = 1024
N_HEADS_A = 8
N_KV_A = 2
HEAD_DIM_A = 64
WINDOW = 128
BLOCK = 128
N_HEADS_B = 8
QK_NOPE = 64
QK_ROPE = 32
V_DIM_B = 64
Q_LORA = 256
KV_LORA = 128
ROPE_THETA = 10000.0
D_FF = 4 * D_MODEL
EPS = 1e-6
WIDTH_A = N_HEADS_A * HEAD_DIM_A
Q_HEAD_B = QK_NOPE + QK_ROPE
D_IN_PAD = 3328
HEAD_PAD = 128
MLA_W = N_HEADS_B * HEAD_PAD

ADAM_LR = 0.001
ADAM_B1 = 0.9
ADAM_B2 = 0.999
ADAM_EPS = 1e-08
ADAM_WD = 0.01
ADAM_STEP = 10

NEG = -1e30
N_CHIPS = 4
LANES = 128
VMEM_LIMIT = 56 * 1024 * 1024

SHARD_SHAPES = {"w_in": (1024, 808), "w_q_b": (256, 192), "w_kv_b": (128, 256), "w_o_a": (512, 256),
                "w_o_b": (512, 256), "w_out": (256, 1024), "w_up": (1024, 1024), "w_down": (1024, 1024)}
PACK_ROWS = {n: (s[0] * s[1]) // D_MODEL for n, s in SHARD_SHAPES.items()}
GROUP_A = ("w_in", "w_q_b", "w_kv_b")
GROUP_B = ("w_up", "w_down", "w_out", "w_o_a", "w_o_b")
WEIGHTS = ("pre_norm_mix", "w_in", "q_a_norm", "w_q_b", "kv_a_norm", "w_kv_b", "sinks", "w_o_a", "w_o_b", "w_out",
           "post_norm_mix", "pre_norm_mlp", "w_up", "w_down", "post_norm_mlp")


def _params(sem=None):
    return pltpu.CompilerParams(dimension_semantics=sem, vmem_limit_bytes=VMEM_LIMIT)


def _dot(a, b):
    return jnp.dot(a, b, preferred_element_type=F32)


def _dot_nt(a, b):
    return lax.dot_general(a, b, (((1,), (1,)), ((), ())), preferred_element_type=F32)


def _dot_tn(a, b):
    return lax.dot_general(a, b, (((0,), (0,)), ((), ())), preferred_element_type=F32)


def _rms(v):
    return lax.rsqrt(jnp.mean(v * v, axis=-1, keepdims=True) + EPS)


def _norm_bwd(dout, n, r, g):
    dn = dout * g
    dx = r * (dn - n * jnp.mean(dn * n, axis=-1, keepdims=True))
    return dx, jnp.sum(dout * n, axis=0, keepdims=True)


def _full(shape):
    return pl.BlockSpec(shape, lambda *_: (0,) * len(shape))


def _row_offset(group, name):
    return sum(PACK_ROWS[n] for n in group[:group.index(name)])


def _wb_spec(name):
    rows = PACK_ROWS[name]
    return pl.BlockSpec((N_CHIPS, rows, D_MODEL), lambda *_: (0, _row_offset(GROUP_B, name) // rows, 0))


def _proj_fwd(x, g1, w_in_t):
    t = x.shape[0]
    tm = 512

    def body(x_ref, g_ref, w_ref, h_ref, p_ref):
        for rows in _row_halves(tm):
            xv = x_ref[rows, :]
            h = ((xv * _rms(xv)) * g_ref[...]).astype(BF16)
            h_ref[rows, :] = h
            p_ref[rows, :] = _dot_nt(h, w_ref[...])

    return pl.pallas_call(
        body, name="proj_fwd", grid=(t // tm,),
        in_specs=[pl.BlockSpec((tm, D_MODEL), lambda i: (i, 0)), _full((1, D_MODEL)), _full((D_IN_PAD, D_MODEL))],
        out_specs=[pl.BlockSpec((tm, D_MODEL), lambda i: (i, 0)), pl.BlockSpec((tm, D_IN_PAD), lambda i: (i, 0))],
        out_shape=[jax.ShapeDtypeStruct((t, D_MODEL), BF16), jax.ShapeDtypeStruct((t, D_IN_PAD), F32)],
        compiler_params=_params(("parallel",)),
    )(x, g1, w_in_t)


_QA_BLK = 2048 // WIDTH_A
_KA_BLK = 2560 // LANES
_VA_BLK = 2688 // LANES
_CQ_BLK = 2816 // Q_LORA
_CKV_BLK = 3072 // LANES
_KR_BLK = 3200 // LANES


_GROUP_A = N_HEADS_A // N_KV_A
_SWA_SCALE = HEAD_DIM_A ** -0.5
_LOG2E = 1.4426950408889634


def _head_cols(v, h):
    return v[:, HEAD_DIM_A * h:HEAD_DIM_A * (h + 1)]


def _head_rows(v, h):
    return v[HEAD_DIM_A * h:HEAD_DIM_A * (h + 1), :]


def _swa_band(n, kp_ref, kc_ref, vp_ref, vc_ref, pq_ref, pp_ref, pc_ref):
    kb = jnp.concatenate([kp_ref[...], kc_ref[...]], axis=0)
    vb = jnp.concatenate([vp_ref[...], vc_ref[...]], axis=0)
    posk = jnp.concatenate([pp_ref[...], pc_ref[...]], axis=0)
    dist = jnp.abs(posk - pq_ref[...])
    ki = lax.broadcasted_iota(jnp.int32, (2 * BLOCK, BLOCK), 0)
    qi = lax.broadcasted_iota(jnp.int32, (2 * BLOCK, BLOCK), 1)
    valid = (ki > qi) & (ki <= qi + WINDOW) & ((n > 0) | (ki >= BLOCK))
    return kb, vb, dist, valid


def _swa_scores_t(st_g, j, h, dist, valid):
    slope = 2.0 ** (-8.0 * (h + 1) / N_HEADS_A)
    st = st_g[:, BLOCK * j:BLOCK * (j + 1)] * (_SWA_SCALE * _LOG2E) - (slope * _LOG2E) * dist
    return jnp.where(valid, st, NEG)


def _group_t(xt, kh):
    return jnp.concatenate([_head_rows(xt, _GROUP_A * kh + j) for j in range(_GROUP_A)], axis=1).astype(BF16)


def _swa_fwd(proj, posc, posr, sinks):
    t = proj.shape[0]
    nb = t // BLOCK

    def body(q_ref, kc_ref, kp_ref, vc_ref, vp_ref, pq_ref, pc_ref, pp_ref, sink_ref, o_ref, l_ref):
        n = pl.program_id(0)
        kb, vb, dist, valid = _swa_band(n, kp_ref, kc_ref, vp_ref, vc_ref, pq_ref, pp_ref, pc_ref)
        q_t, vb_t = q_ref[...].T, vb.T
        out_t, lse = [], []
        for kh in range(N_KV_A):
            st_g = _dot(_head_cols(kb, kh).astype(BF16), _group_t(q_t, kh))
            ps = []
            for j in range(_GROUP_A):
                h = _GROUP_A * kh + j
                st = _swa_scores_t(st_g, j, h, dist, valid)
                sink = sink_ref[0:1, h:h + 1] * _LOG2E
                m = jnp.maximum(jnp.max(st, axis=0, keepdims=True), sink)
                e = jnp.exp2(st - m)
                den = jnp.sum(e, axis=0, keepdims=True) + jnp.exp2(sink - m)
                ps.append((e * (1.0 / den)).astype(BF16))
                lse.append(m + jnp.log(den) * _LOG2E)
            o_g = _dot(_head_rows(vb_t, kh).astype(BF16), jnp.concatenate(ps, axis=1))
            out_t.extend(o_g[:, BLOCK * j:BLOCK * (j + 1)] for j in range(_GROUP_A))
        o_ref[...] = jnp.concatenate(out_t, axis=0).T
        l_ref[...] = jnp.concatenate(lse, axis=0)

    cur = lambda n: (n, 0)
    prev = lambda n: jnp.maximum(n - 1, 0)
    return pl.pallas_call(
        body, name="swa_fwd", grid=(nb,),
        in_specs=[pl.BlockSpec((BLOCK, WIDTH_A), lambda n: (n, _QA_BLK)),
                  pl.BlockSpec((BLOCK, LANES), lambda n: (n, _KA_BLK)),
                  pl.BlockSpec((BLOCK, LANES), lambda n: (prev(n), _KA_BLK)),
                  pl.BlockSpec((BLOCK, LANES), lambda n: (n, _VA_BLK)),
                  pl.BlockSpec((BLOCK, LANES), lambda n: (prev(n), _VA_BLK)),
                  pl.BlockSpec((1, BLOCK), lambda n: (0, n)),
                  pl.BlockSpec((BLOCK, 1), cur),
                  pl.BlockSpec((BLOCK, 1), lambda n: (prev(n), 0)),
                  _full((1, N_HEADS_A))],
        out_specs=[pl.BlockSpec((BLOCK, WIDTH_A), cur), pl.BlockSpec((N_HEADS_A, BLOCK), lambda n: (0, n))],
        out_shape=[jax.ShapeDtypeStruct((t, WIDTH_A), F32), jax.ShapeDtypeStruct((N_HEADS_A, t), F32)],
        compiler_params=_params(("parallel",)),
    )(proj, proj, proj, proj, proj, posr, posc, posc, sinks)


def _rope_coeffs(pos, freq):
    ang = pos * freq
    cosv, sinv = jnp.cos(ang), jnp.sin(ang)
    lane = lax.broadcasted_iota(jnp.int32, ang.shape, 1)
    lo = (lane >= QK_NOPE) & (lane < QK_NOPE + QK_ROPE // 2)
    hi = (lane >= QK_NOPE + QK_ROPE // 2) & (lane < QK_NOPE + QK_ROPE)
    c = jnp.where(lane < QK_NOPE, 1.0, jnp.where(lo | hi, cosv, 0.0))
    s = jnp.where(lo, -sinv, jnp.where(hi, sinv, 0.0))
    return c, s, lo, hi


def _rope(xh, c, s, lo):
    up = pltpu.roll(xh, LANES - QK_ROPE // 2, axis=1)
    dn = pltpu.roll(xh, QK_ROPE // 2, axis=1)
    return xh * c + jnp.where(lo, up, dn) * s


def _unrope(dh, c, s, lo, hi):
    g = dh * s
    up = pltpu.roll(g, LANES - QK_ROPE // 2, axis=1)
    dn = pltpu.roll(g, QK_ROPE // 2, axis=1)
    return dh * c + jnp.where(hi, dn, jnp.where(lo, up, 0.0))


_TQ = 512
_MLA_SCALE = Q_HEAD_B ** -0.5


def _mla_prep_fwd(proj, posc, freq, qan, kvan, wq, wk, wv):
    t = proj.shape[0]
    tm = _TQ
    nb = t // tm

    def body(cq_ref, ckv_ref, kr_ref, pos_ref, f_ref, qan_ref, kvan_ref, wq_ref, wk_ref, wv_ref,
             q_ref, k_ref, qt_ref, kt_ref, vt_ref):
        cq = cq_ref[...]
        cqn = ((cq * _rms(cq)) * qan_ref[...]).astype(BF16)
        ckv = ckv_ref[...]
        ckvn = ((ckv * _rms(ckv)) * kvan_ref[...]).astype(BF16)
        qb = _dot(cqn, wq_ref[...])
        kb = _dot(ckvn, wk_ref[...])
        vbt = _dot_nt(wv_ref[...], ckvn)
        c, s, lo, _ = _rope_coeffs(pos_ref[...], f_ref[...])
        kr = _rope(kr_ref[...], c, s, lo)
        for h in range(N_HEADS_B):
            sl = slice(HEAD_PAD * h, HEAD_PAD * (h + 1))
            q_h = _rope(qb[:, sl], c, s, lo)
            k_h = kb[:, sl] + kr
            q_ref[:, sl] = q_h.astype(BF16)
            k_ref[:, sl] = k_h.astype(BF16)
            qt_ref[h, 0] = q_h.T.astype(BF16)
            kt_ref[h, 0] = k_h.T.astype(BF16)
            vt_ref[h, 0] = vbt[V_DIM_B * h:V_DIM_B * (h + 1), :].astype(BF16)

    row = lambda i: (i, 0)
    blk4 = lambda d: pl.BlockSpec((N_HEADS_B, 1, d, tm), lambda i: (0, i, 0, 0))
    return pl.pallas_call(
        body, name="mla_prep_fwd", grid=(nb,),
        in_specs=[pl.BlockSpec((tm, Q_LORA), lambda i: (i, _CQ_BLK)),
                  pl.BlockSpec((tm, LANES), lambda i: (i, _CKV_BLK)),
                  pl.BlockSpec((tm, LANES), lambda i: (i, _KR_BLK)),
                  pl.BlockSpec((tm, 1), row), _full((1, LANES)), _full((1, Q_LORA)), _full((1, KV_LORA)),
                  _full((Q_LORA, MLA_W)), _full((KV_LORA, MLA_W)), _full((N_HEADS_B * V_DIM_B, KV_LORA))],
        out_specs=[pl.BlockSpec((tm, MLA_W), row), pl.BlockSpec((tm, MLA_W), row), blk4(HEAD_PAD), blk4(HEAD_PAD),
                   blk4(V_DIM_B)],
        out_shape=[jax.ShapeDtypeStruct((t, MLA_W), BF16), jax.ShapeDtypeStruct((t, MLA_W), BF16),
                   jax.ShapeDtypeStruct((N_HEADS_B, nb, HEAD_PAD, tm), BF16),
                   jax.ShapeDtypeStruct((N_HEADS_B, nb, HEAD_PAD, tm), BF16),
                   jax.ShapeDtypeStruct((N_HEADS_B, nb, V_DIM_B, tm), BF16)],
        compiler_params=_params(("parallel",)),
    )(proj, proj, proj, posc, freq, qan, kvan, wq, wk, wv)


_MLA_SCALE2 = _MLA_SCALE * _LOG2E


def _mla_fwd(k, qt, vt, w_src):
    t = k.shape[0]
    nb = t // _TQ

    def body(k_ref, qt_ref, vt_ref, w_ref, o_ref, l_ref, wg_ref, raw_a, raw_b, send_sems, recv_sems, local_sem):
        qi = pl.program_id(1)
        first = (pl.program_id(0) == 0) & (qi == 0)
        last = (pl.program_id(0) == N_HEADS_B - 1) & (qi == nb - 1)

        @pl.when(first)
        def _():
            _gather_start(w_ref, wg_ref, send_sems, recv_sems, local_sem)

        q_t = qt_ref[0, 0]

        def product(kj):
            return _dot(k_ref[pl.ds(pl.multiple_of(kj * _TQ, _TQ), _TQ), :], q_t)

        def update(stats, raw_ref, kj, diagonal=False):
            m, l, acc = stats
            raw = raw_ref[...]
            if diagonal:
                key = lax.broadcasted_iota(jnp.int32, raw.shape, 0)
                qry = lax.broadcasted_iota(jnp.int32, raw.shape, 1)
                raw = jnp.where(key <= qry, raw, NEG)
            m_new = jnp.maximum(m, jnp.max(raw, axis=0, keepdims=True) * _MLA_SCALE2)
            alpha = jnp.exp2(m - m_new)
            p = jnp.exp2(raw * _MLA_SCALE2 - m_new)
            l = alpha * l + jnp.sum(p, axis=0, keepdims=True)
            acc = alpha * acc + _dot(vt_ref[0, kj], p.astype(BF16))
            return m_new, l, acc

        def trip(i, stats):
            raw_b[...] = product(2 * i + 1)
            stats = update(stats, raw_a, 2 * i)
            raw_a[...] = product(2 * i + 2)
            return update(stats, raw_b, 2 * i + 1)

        def tail_even(stats):
            return update(stats, raw_a, qi, True)

        def tail_odd(stats):
            raw_b[...] = product(qi)
            return update(update(stats, raw_a, qi - 1), raw_b, qi, True)

        init = (jnp.full((1, _TQ), NEG, F32), jnp.zeros((1, _TQ), F32), jnp.zeros((V_DIM_B, _TQ), F32))
        raw_a[...] = product(0)
        stats = lax.fori_loop(0, qi // 2, trip, init)
        m, l, acc = lax.cond(qi % 2 == 0, tail_even, tail_odd, stats)
        o_ref[0, 0] = acc / l
        l_ref[0, 0] = m + jnp.log(l) * _LOG2E

        @pl.when(last)
        def _():
            _gather_wait(w_ref, wg_ref, send_sems, recv_sems, local_sem)

    return pl.pallas_call(
        body, name="mla_fwd", grid=(N_HEADS_B, nb),
        in_specs=[pl.BlockSpec((t, HEAD_PAD), lambda h, qi: (0, h)),
                  pl.BlockSpec((1, 1, HEAD_PAD, _TQ), lambda h, qi: (h, qi, 0, 0)),
                  pl.BlockSpec((1, nb, V_DIM_B, _TQ), lambda h, qi: (h, 0, 0, 0)), _HBM],
        out_specs=[pl.BlockSpec((1, 1, V_DIM_B, _TQ), lambda h, qi: (h, qi, 0, 0)),
                   pl.BlockSpec((1, 1, 1, _TQ), lambda h, qi: (h, qi, 0, 0)), _HBM],
        out_shape=[jax.ShapeDtypeStruct((N_HEADS_B, nb, V_DIM_B, _TQ), F32),
                   jax.ShapeDtypeStruct((N_HEADS_B, nb, 1, _TQ), F32),
                   jax.ShapeDtypeStruct((N_CHIPS,) + w_src.shape, w_src.dtype)],
        scratch_shapes=[pltpu.VMEM((_TQ, _TQ), F32), pltpu.VMEM((_TQ, _TQ), F32),
                        pltpu.SemaphoreType.DMA((3,)), pltpu.SemaphoreType.DMA((3,)), pltpu.SemaphoreType.DMA(())],
        compiler_params=_params(("arbitrary", "arbitrary")),
    )(k, qt, vt, w_src)


def _ot_spec(tm, d):
    per = _TQ // tm
    return pl.BlockSpec((N_HEADS_B, 1, d, tm), lambda i: (0, i // per, 0, i % per))


def _mix_out_fwd(out_a, out_bt, proj, x, w_oa, w_ob, wb, g2, g3):
    t = x.shape[0]
    tm = 512

    def body(oa_ref, obt_ref, ga_ref, gb_ref, x_ref, woa_ref, wob_ref, wout_ref, g2_ref, g3_ref,
             mg_ref, y_ref, x1_ref, h2_ref):
        oa = _dot(oa_ref[...].astype(BF16), woa_ref[...])
        obt = obt_ref[...].reshape(N_HEADS_B * V_DIM_B, tm).astype(BF16)
        ob = _dot_tn(obt, wob_ref[...])
        merged = (jax.nn.sigmoid(ga_ref[...]) * oa + jax.nn.sigmoid(gb_ref[...]) * ob).astype(BF16)
        mg_ref[...] = merged
        y = _dot(merged, wout_ref[...].reshape(D_MODEL, D_MODEL))
        y_ref[...] = y
        x1 = x_ref[...] + (y * _rms(y)) * g2_ref[...]
        x1_ref[...] = x1
        h2_ref[...] = ((x1 * _rms(x1)) * g3_ref[...]).astype(BF16)

    row = lambda i: (i, 0)
    blk = pl.BlockSpec((tm, D_MODEL), row)
    return pl.pallas_call(
        body, name="mix_out_fwd", grid=(t // tm,),
        in_specs=[pl.BlockSpec((tm, WIDTH_A), row), _ot_spec(tm, V_DIM_B), pl.BlockSpec((tm, D_MODEL), lambda i: (i, 0)),
                  pl.BlockSpec((tm, D_MODEL), lambda i: (i, 1)), blk,
                  _full((WIDTH_A, D_MODEL)), _full((N_HEADS_B * V_DIM_B, D_MODEL)), _wb_spec("w_out"),
                  _full((1, D_MODEL)), _full((1, D_MODEL))],
        out_specs=[blk, blk, blk, blk],
        out_shape=[jax.ShapeDtypeStruct((t, D_MODEL), BF16), jax.ShapeDtypeStruct((t, D_MODEL), F32),
                   jax.ShapeDtypeStruct((t, D_MODEL), F32), jax.ShapeDtypeStruct((t, D_MODEL), BF16)],
        compiler_params=_params(("parallel",)),
    )(out_a, out_bt, proj, proj, x, w_oa, w_ob, wb, g2, g3)


_TM_MLP = 512


def _row_halves(tm):
    return slice(0, tm // 2), slice(tm // 2, tm)


def _up_fwd(h2, wb):
    t = h2.shape[0]
    tm = _TM_MLP

    def body(h_ref, w_ref, a_ref):
        hv = h_ref[...]
        for j in range(N_CHIPS):
            u = _dot(hv, w_ref[j])
            a_ref[:, D_MODEL * j:D_MODEL * (j + 1)] = jnp.square(jnp.maximum(u, 0.0)).astype(BF16)

    return pl.pallas_call(
        body, name="up_fwd", grid=(t // tm,),
        in_specs=[pl.BlockSpec((tm, D_MODEL), lambda i: (i, 0)), _wb_spec("w_up")],
        out_specs=pl.BlockSpec((tm, D_FF), lambda i: (i, 0)),
        out_shape=jax.ShapeDtypeStruct((t, D_FF), BF16),
        compiler_params=_params(("parallel",)),
    )(h2, wb)


def _down_fwd_loss(a, wb, x1, target, g4):
    t = a.shape[0]
    tm = _TM_MLP

    def body(a_ref, w_ref, x1_ref, tg_ref, g_ref, dx2_ref, dyd_ref, dg_ref, loss_ref):
        @pl.when(pl.program_id(0) == 0)
        def _():
            dg_ref[...] = jnp.zeros(dg_ref.shape, F32)
            loss_ref[...] = jnp.zeros(loss_ref.shape, F32)

        yd = _dot(a_ref[...], w_ref[...].reshape(D_FF, D_MODEL))
        r = _rms(yd)
        n = yd * r
        diff = (x1_ref[...] + n * g_ref[...]) - tg_ref[...]
        loss_ref[...] += 0.5 * jnp.sum(jnp.mean(diff * diff, axis=-1, keepdims=True), axis=0, keepdims=True)
        dx2 = diff * (1.0 / D_MODEL)
        dx2_ref[...] = dx2
        dyd, dg = _norm_bwd(dx2, n, r, g_ref[...])
        dyd_ref[...] = dyd.astype(BF16)
        dg_ref[...] += dg

    row = lambda i: (i, 0)
    blk = pl.BlockSpec((tm, D_MODEL), row)
    return pl.pallas_call(
        body, name="down_fwd_loss", grid=(t // tm,),
        in_specs=[pl.BlockSpec((tm, D_FF), row), _wb_spec("w_down"), blk, blk, _full((1, D_MODEL))],
        out_specs=[blk, blk, _full((1, D_MODEL)), _full((1, LANES))],
        out_shape=[jax.ShapeDtypeStruct((t, D_MODEL), F32), jax.ShapeDtypeStruct((t, D_MODEL), BF16),
                   jax.ShapeDtypeStruct((1, D_MODEL), F32), jax.ShapeDtypeStruct((1, LANES), F32)],
        compiler_params=_params(("arbitrary",)),
    )(a, wb, x1, target, g4)


def _matmul_tn(a, b, name, tm, tn, tk=1024):
    t, m = a.shape
    n = b.shape[1]
    tk = min(tk, t)
    nk = t // tk

    def body(a_ref, b_ref, o_ref):
        @pl.when(pl.program_id(2) == 0)
        def _():
            o_ref[...] = jnp.zeros(o_ref.shape, F32)

        o_ref[...] += _dot_tn(a_ref[...].astype(BF16), b_ref[...].astype(BF16))

    return pl.pallas_call(
        body, name=name, grid=(m // tm, n // tn, nk),
        in_specs=[pl.BlockSpec((tk, tm), lambda i, j, k: (k, i)), pl.BlockSpec((tk, tn), lambda i, j, k: (k, j))],
        out_specs=pl.BlockSpec((tm, tn), lambda i, j, k: (i, j)),
        out_shape=jax.ShapeDtypeStruct((m, n), F32),
        compiler_params=_params(("parallel", "parallel", "arbitrary")),
    )(a, b)


_TK_DW = 2048


def _dw_into_blocks(a, b, weight, tm, tk, buf=None):
    t, m = a.shape
    n = b.shape[1]
    tk = min(tk, t)
    nk = t // tk
    rows = PACK_ROWS[weight]
    br = min(tm, rows)
    chips = tm // br
    first = _row_offset(GROUP_B, weight) // br
    per_chip = rows // br
    if weight == "w_up":
        out_map = lambda i, j, k: (j, first + i, 0)
    elif chips > 1:
        out_map = lambda i, j, k: (i, first, 0)
    else:
        out_map = lambda i, j, k: (i // per_chip, first + i % per_chip, 0)

    def body(a_ref, b_ref, *rest):
        o_ref = rest[-1]

        @pl.when(pl.program_id(2) == 0)
        def _():
            o_ref[...] = jnp.zeros(o_ref.shape, F32)

        o_ref[...] += _dot_tn(a_ref[...].astype(BF16), b_ref[...].astype(BF16)).reshape(o_ref.shape)

    in_specs = [pl.BlockSpec((tk, tm), lambda i, j, k: (k, i)), pl.BlockSpec((tk, D_MODEL), lambda i, j, k: (k, j))]
    operands = [a, b]
    if buf is not None:
        in_specs.append(pl.BlockSpec(memory_space=pl.ANY))
        operands.append(buf)
    total = sum(PACK_ROWS[w] for w in GROUP_B)
    return pl.pallas_call(
        body, name="dw_" + weight[2:], grid=(m // tm, n // D_MODEL, nk),
        in_specs=in_specs, out_specs=pl.BlockSpec((chips, br, D_MODEL), out_map),
        out_shape=jax.ShapeDtypeStruct((N_CHIPS, total, D_MODEL), F32),
        input_output_aliases={} if buf is None else {2: 0},
        compiler_params=_params(("parallel", "parallel", "arbitrary")),
    )(*operands)


def _down_bwd(dyd, wb, a):
    t = dyd.shape[0]
    tm = _TM_MLP

    def body(d_ref, w_ref, a_ref, du_ref):
        dv = d_ref[...]
        for j in range(N_CHIPS):
            cols = slice(D_MODEL * j, D_MODEL * (j + 1))
            av = a_ref[:, cols].astype(F32)
            relu_u = jnp.where(av > 0.0, av * lax.rsqrt(av), 0.0)
            du_ref[:, cols] = (_dot_nt(dv, w_ref[j]) * (2.0 * relu_u)).astype(BF16)

    row = lambda i: (i, 0)
    return pl.pallas_call(
        body, name="down_bwd", grid=(t // tm,),
        in_specs=[pl.BlockSpec((tm, D_MODEL), row), _wb_spec("w_down"), pl.BlockSpec((tm, D_FF), row)],
        out_specs=pl.BlockSpec((tm, D_FF), row),
        out_shape=jax.ShapeDtypeStruct((t, D_FF), BF16),
        compiler_params=_params(("parallel",)),
    )(dyd, wb, a)


def _up_bwd(du, wb, x1, dx2, y, g3, g2):
    t = du.shape[0]
    tm = _TM_MLP

    def body(du_ref, w_ref, x1_ref, dx2_ref, y_ref, g3_ref, g2_ref, dx1_ref, dy_ref, dg3_ref, dg2_ref):
        @pl.when(pl.program_id(0) == 0)
        def _():
            dg3_ref[...] = jnp.zeros(dg3_ref.shape, F32)
            dg2_ref[...] = jnp.zeros(dg2_ref.shape, F32)

        dh2 = _dot_nt(du_ref[:, 0:D_MODEL], w_ref[0])
        for j in range(1, N_CHIPS):
            dh2 = dh2 + _dot_nt(du_ref[:, D_MODEL * j:D_MODEL * (j + 1)], w_ref[j])
        x1 = x1_ref[...]
        r3 = _rms(x1)
        d3, dg3 = _norm_bwd(dh2, x1 * r3, r3, g3_ref[...])
        dx1 = dx2_ref[...] + d3
        dx1_ref[...] = dx1
        dg3_ref[...] += dg3
        y = y_ref[...]
        r2 = _rms(y)
        dy, dg2 = _norm_bwd(dx1, y * r2, r2, g2_ref[...])
        dy_ref[...] = dy.astype(BF16)
        dg2_ref[...] += dg2

    row = lambda i: (i, 0)
    blk = pl.BlockSpec((tm, D_MODEL), row)
    return pl.pallas_call(
        body, name="up_bwd", grid=(t // tm,),
        in_specs=[pl.BlockSpec((tm, D_FF), row), _wb_spec("w_up"),
                  blk, blk, blk, _full((1, D_MODEL)), _full((1, D_MODEL))],
        out_specs=[blk, blk, _full((1, D_MODEL)), _full((1, D_MODEL))],
        out_shape=[jax.ShapeDtypeStruct((t, D_MODEL), F32), jax.ShapeDtypeStruct((t, D_MODEL), BF16),
                   jax.ShapeDtypeStruct((1, D_MODEL), F32), jax.ShapeDtypeStruct((1, D_MODEL), F32)],
        compiler_params=_params(("arbitrary",)),
    )(du, wb, x1, dx2, y, g3, g2)


def _mix_out_bwd(dy, out_a, out_bt, proj, w_oa, w_ob, wb):
    t = dy.shape[0]
    tm = 256
    nb = t // _TQ

    def body(dy_ref, oa_ref, obt_ref, ga_ref, gb_ref, woa_ref, wob_ref, wout_ref,
             doa_ref, dob_ref, dga_ref, dgb_ref, da_ref, db_ref, dbt_ref, dela_ref, delb_ref):
        dm = _dot_nt(dy_ref[...], wout_ref[...].reshape(D_MODEL, D_MODEL))
        out_a_v = oa_ref[...]
        out_bt_v = obt_ref[...].reshape(N_HEADS_B * V_DIM_B, tm)
        oa = _dot(out_a_v.astype(BF16), woa_ref[...])
        ob = _dot_tn(out_bt_v.astype(BF16), wob_ref[...])
        sa, sb = jax.nn.sigmoid(ga_ref[...]), jax.nn.sigmoid(gb_ref[...])
        doa = (dm * sa).astype(BF16)
        dob = (dm * sb).astype(BF16)
        doa_ref[...] = doa
        dob_ref[...] = dob
        dga_ref[...] = (dm * oa * (sa * (1.0 - sa))).astype(BF16)
        dgb_ref[...] = (dm * ob * (sb * (1.0 - sb))).astype(BF16)
        d_out_a = _dot_nt(doa, woa_ref[...])
        da_ref[...] = d_out_a
        prod_at = (d_out_a * out_a_v).T
        dela_ref[...] = jnp.concatenate(
            [jnp.sum(_head_rows(prod_at, h), axis=0, keepdims=True) for h in range(N_HEADS_A)], axis=0)
        d_out_b = _dot_nt(dob, wob_ref[...])
        d_out_bt = _dot_nt(wob_ref[...], dob)
        prod_bt = d_out_bt * out_bt_v
        for h in range(N_HEADS_B):
            db_ref[h] = d_out_b[:, V_DIM_B * h:V_DIM_B * (h + 1)].astype(BF16)
            dbt_ref[h, 0] = d_out_bt[V_DIM_B * h:V_DIM_B * (h + 1), :].astype(BF16)
            delb_ref[h, 0] = jnp.sum(prod_bt[V_DIM_B * h:V_DIM_B * (h + 1), :], axis=0, keepdims=True)

    row = lambda i: (i, 0)
    blk = pl.BlockSpec((tm, D_MODEL), row)
    return pl.pallas_call(
        body, name="mix_out_bwd", grid=(t // tm,),
        in_specs=[blk, pl.BlockSpec((tm, WIDTH_A), row), _ot_spec(tm, V_DIM_B),
                  pl.BlockSpec((tm, D_MODEL), lambda i: (i, 0)), pl.BlockSpec((tm, D_MODEL), lambda i: (i, 1)),
                  _full((WIDTH_A, D_MODEL)), _full((N_HEADS_B * V_DIM_B, D_MODEL)), _wb_spec("w_out")],
        out_specs=[blk, blk, blk, blk, pl.BlockSpec((tm, WIDTH_A), row),
                   pl.BlockSpec((N_HEADS_B, tm, V_DIM_B), lambda i: (0, i, 0)), _ot_spec(tm, V_DIM_B),
                   pl.BlockSpec((N_HEADS_A, tm), lambda i: (0, i)), _ot_spec(tm, 1)],
        out_shape=[jax.ShapeDtypeStruct((t, D_MODEL), BF16)] * 4
        + [jax.ShapeDtypeStruct((t, WIDTH_A), F32), jax.ShapeDtypeStruct((N_HEADS_B, t, V_DIM_B), BF16),
           jax.ShapeDtypeStruct((N_HEADS_B, nb, V_DIM_B, _TQ), BF16), jax.ShapeDtypeStruct((N_HEADS_A, t), F32),
           jax.ShapeDtypeStruct((N_HEADS_B, nb, 1, _TQ), F32)],
        compiler_params=_params(("parallel",)),
    )(dy, out_a, out_bt, proj, proj, w_oa, w_ob, wb)


def _dw_ob(out_bt, dob):
    t = dob.shape[0]
    nb = t // _TQ

    def body(obt_ref, dob_ref, o_ref):
        @pl.when(pl.program_id(0) == 0)
        def _():
            o_ref[...] = jnp.zeros(o_ref.shape, F32)

        obt = obt_ref[...].reshape(N_HEADS_B * V_DIM_B, _TQ).astype(BF16)
        o_ref[...] += _dot(obt, dob_ref[...])

    return pl.pallas_call(
        body, name="dw_o_b", grid=(nb,),
        in_specs=[pl.BlockSpec((N_HEADS_B, 1, V_DIM_B, _TQ), lambda i: (0, i, 0, 0)),
                  pl.BlockSpec((_TQ, D_MODEL), lambda i: (i, 0))],
        out_specs=_full((N_HEADS_B * V_DIM_B, D_MODEL)),
        out_shape=jax.ShapeDtypeStruct((N_HEADS_B * V_DIM_B, D_MODEL), F32),
        compiler_params=_params(("arbitrary",)),
    )(out_bt, dob)


def _mla_bwd(q, k, qt, kt, vt, d_out, d_out_t, lse, delta, gp):
    t = q.shape[0]
    nb = t // _TQ

    def body(k_ref, kt_ref, vt_ref, q_ref, qt_ref, do_ref, dot_ref, lrow_ref, drow_ref, gp_ref,
             dq_ref, dkt_ref, dvt_ref, land_ref, l_rep, d_rep, send_sems, recv_sems):
        step = pl.program_id(1)
        kj = nb - 1 - step

        @pl.when((pl.program_id(0) == 0) & (step == 0))
        def _():
            _scatter_start(gp_ref, land_ref, send_sems, recv_sems)

        @pl.when(step == 0)
        def _():
            dq_ref[...] = jnp.zeros(dq_ref.shape, F32)
            for b in range(nb):
                l_rep[_TQ * b:_TQ * (b + 1), :] = jnp.broadcast_to(lrow_ref[0, b], (LANES, _TQ)).T
                d_rep[_TQ * b:_TQ * (b + 1), :] = jnp.broadcast_to(drow_ref[0, b], (LANES, _TQ)).T

        kv, k_t, v_t = k_ref[...], kt_ref[0, 0], vt_ref[0, 0]

        def rows_of(qi):
            return pl.ds(pl.multiple_of(qi * _TQ, _TQ), _TQ)

        def products(qi, diagonal=False):
            s = _dot(q_ref[rows_of(qi), :], k_t) * _MLA_SCALE2
            if diagonal:
                qry = lax.broadcasted_iota(jnp.int32, s.shape, 0)
                key = lax.broadcasted_iota(jnp.int32, s.shape, 1)
                s = jnp.where(key <= qry, s, NEG)
            return s, _dot(do_ref[0, rows_of(qi), :], v_t)

        def update(carry, prods, qi):
            dkt, dvt = carry
            s, dp = prods
            lse, delta = l_rep[rows_of(qi), :], d_rep[rows_of(qi), :]
            ps, dss = [], []
            for c in range(_TQ // LANES):
                strip = slice(LANES * c, LANES * (c + 1))
                p = jnp.exp2(s[:, strip] - lse)
                ps.append(p.astype(BF16))
                dss.append((p * (dp[:, strip] - delta) * _MLA_SCALE).astype(BF16))
            p_b, ds_b = jnp.concatenate(ps, axis=1), jnp.concatenate(dss, axis=1)
            dvt = dvt + _dot(dot_ref[0, qi], p_b)
            dkt = dkt + _dot(qt_ref[0, qi], ds_b)
            dq_ref[rows_of(qi), :] += _dot(ds_b, kv)
            return dkt, dvt

        def pair(i, carry):
            qa = kj + 1 + 2 * i
            pa, pb = products(qa), products(qa + 1)
            return update(update(carry, pa, qa), pb, qa + 1)

        init = (jnp.zeros((HEAD_PAD, _TQ), F32), jnp.zeros((V_DIM_B, _TQ), F32))
        carry = update(init, products(kj, True), kj)
        pairs = (nb - 1 - kj) // 2
        carry = lax.fori_loop(0, pairs, pair, carry)
        dkt, dvt = lax.fori_loop(kj + 1 + 2 * pairs, nb, lambda qi, cr: update(cr, products(qi), qi), carry)
        dkt_ref[0, 0] = dkt
        dvt_ref[0, 0] = dvt

        @pl.when((pl.program_id(0) == N_HEADS_B - 1) & (step == nb - 1))
        def _():
            _scatter_wait(gp_ref, land_ref, send_sems, recv_sems)

    head4 = lambda d: pl.BlockSpec((1, nb, d, _TQ), lambda h, s: (h, 0, 0, 0))
    blk4 = lambda d: pl.BlockSpec((1, 1, d, _TQ), lambda h, s: (h, nb - 1 - s, 0, 0))
    head3 = lambda d: pl.BlockSpec((1, t, d), lambda h, kj: (h, 0, 0))
    per_head = pl.BlockSpec((t, HEAD_PAD), lambda h, kj: (0, h))
    return pl.pallas_call(
        body, name="mla_bwd", grid=(N_HEADS_B, nb),
        in_specs=[pl.BlockSpec((_TQ, HEAD_PAD), lambda h, s: (nb - 1 - s, h)), blk4(HEAD_PAD), blk4(V_DIM_B),
                  per_head, head4(HEAD_PAD), head3(V_DIM_B), head4(V_DIM_B), head4(1), head4(1), _HBM],
        out_specs=[per_head, blk4(HEAD_PAD), blk4(V_DIM_B), _HBM],
        out_shape=[jax.ShapeDtypeStruct((t, MLA_W), F32), jax.ShapeDtypeStruct((N_HEADS_B, nb, HEAD_PAD, _TQ), F32),
                   jax.ShapeDtypeStruct((N_HEADS_B, nb, V_DIM_B, _TQ), F32),
                   jax.ShapeDtypeStruct((3,) + gp.shape[1:], gp.dtype)],
        scratch_shapes=[pltpu.VMEM((t, LANES), F32), pltpu.VMEM((t, LANES), F32),
                        pltpu.SemaphoreType.DMA((3,)), pltpu.SemaphoreType.DMA((3,))],
        compiler_params=_params(("arbitrary", "arbitrary")),
    )(k, kt, vt, q, qt, d_out, d_out_t, lse, delta, gp)


def _mla_prep_bwd(dq, dkt, dvt, proj, posc, freq, qan, kvan, wq, wk, wv, swap_src):
    t = dq.shape[0]
    tm = _TQ

    def body(dq_ref, dkt_ref, dvt_ref, cq_ref, ckv_ref, pos_ref, f_ref, qan_ref, kvan_ref, wq_ref, wk_ref, wv_ref, src_ref,
             dcq_ref, dckv_ref, dkr_ref, dwq_ref, dwk_ref, dwv_ref, dqan_ref, dkvan_ref, got_ref, send_sem, recv_sem):
        swap = _sibling_copy(src_ref, got_ref, send_sem, recv_sem)

        @pl.when(pl.program_id(0) == 0)
        def _():
            swap.start()
            for r in (dwq_ref, dwk_ref, dwv_ref, dqan_ref, dkvan_ref):
                r[...] = jnp.zeros(r.shape, F32)

        cq = cq_ref[...]
        rq = _rms(cq)
        nq_ = cq * rq
        cqn = (nq_ * qan_ref[...]).astype(BF16)
        ckv = ckv_ref[...]
        rkv = _rms(ckv)
        nkv = ckv * rkv
        ckvn = (nkv * kvan_ref[...]).astype(BF16)
        c, s, lo, hi = _rope_coeffs(pos_ref[...], f_ref[...])
        dkr = jnp.zeros((tm, LANES), F32)
        dqb, dkb = [], []
        for h in range(N_HEADS_B):
            dqb.append(_unrope(dq_ref[:, HEAD_PAD * h:HEAD_PAD * (h + 1)], c, s, lo, hi).astype(BF16))
            dk_h = dkt_ref[h, 0].T
            dkr = dkr + dk_h
            dkb.append(dk_h.astype(BF16))
        dqb, dkb = jnp.concatenate(dqb, axis=1), jnp.concatenate(dkb, axis=1)
        dkr_ref[...] = jnp.where(lo | hi, _unrope(dkr, c, s, lo, hi), 0.0).astype(BF16)
        dvb = dvt_ref[...].reshape(N_HEADS_B * V_DIM_B, tm).T.astype(BF16)
        dwq_ref[...] += _dot_tn(cqn, dqb)
        dwk_ref[...] += _dot_tn(ckvn, dkb)
        dwv_ref[...] += _dot_tn(ckvn, dvb)
        dcqn = _dot_nt(dqb, wq_ref[...])
        dckvn = _dot_nt(dkb, wk_ref[...]) + _dot_nt(dvb, wv_ref[...])
        dcq, dqan = _norm_bwd(dcqn, nq_, rq, qan_ref[...])
        dckv, dkvan = _norm_bwd(dckvn, nkv, rkv, kvan_ref[...])
        dcq_ref[...] = dcq.astype(BF16)
        dckv_ref[...] = dckv.astype(BF16)
        dqan_ref[...] += dqan
        dkvan_ref[...] += dkvan

        @pl.when(pl.program_id(0) == t // tm - 1)
        def _():
            swap.wait_recv()
            swap.wait_send()

    row = lambda i: (i, 0)
    vw = N_HEADS_B * V_DIM_B
    return pl.pallas_call(
        body, name="mla_prep_bwd", grid=(t // tm,),
        in_specs=[pl.BlockSpec((tm, MLA_W), row), pl.BlockSpec((N_HEADS_B, 1, HEAD_PAD, tm), lambda i: (0, i, 0, 0)),
                  pl.BlockSpec((N_HEADS_B, 1, V_DIM_B, tm), lambda i: (0, i, 0, 0)),
                  pl.BlockSpec((tm, Q_LORA), lambda i: (i, _CQ_BLK)),
                  pl.BlockSpec((tm, LANES), lambda i: (i, _CKV_BLK)),
                  pl.BlockSpec((tm, 1), row), _full((1, LANES)), _full((1, Q_LORA)), _full((1, KV_LORA)),
                  _full((Q_LORA, MLA_W)), _full((KV_LORA, MLA_W)), _full((KV_LORA, vw)), _HBM],
        out_specs=[pl.BlockSpec((tm, Q_LORA), row), pl.BlockSpec((tm, LANES), row), pl.BlockSpec((tm, LANES), row),
                   _full((Q_LORA, MLA_W)), _full((KV_LORA, MLA_W)), _full((KV_LORA, vw)),
                   _full((1, Q_LORA)), _full((1, KV_LORA)), _HBM],
        out_shape=[jax.ShapeDtypeStruct((t, Q_LORA), BF16), jax.ShapeDtypeStruct((t, LANES), BF16),
                   jax.ShapeDtypeStruct((t, LANES), BF16),
                   jax.ShapeDtypeStruct((Q_LORA, MLA_W), F32), jax.ShapeDtypeStruct((KV_LORA, MLA_W), F32),
                   jax.ShapeDtypeStruct((KV_LORA, vw), F32),
                   jax.ShapeDtypeStruct((1, Q_LORA), F32), jax.ShapeDtypeStruct((1, KV_LORA), F32),
                   jax.ShapeDtypeStruct(swap_src.shape, swap_src.dtype)],
        scratch_shapes=[pltpu.SemaphoreType.DMA(()), pltpu.SemaphoreType.DMA(())],
        compiler_params=_params(("arbitrary",)),
    )(dq, dkt, dvt, proj, proj, posc, freq, qan, kvan, wq, wk, wv, swap_src)


def _swa_bwd(proj, d_out, lse, delta, posc, posr, sinks):
    t = proj.shape[0]
    nb = t // BLOCK

    def body(q_ref, kc_ref, kp_ref, vc_ref, vp_ref, do_ref, l_ref, d_ref, pq_ref, pc_ref, pp_ref, sink_ref,
             dq_ref, dk_ref, dv_ref, ds_ref, dkb_s, dvb_s, dk_carry, dv_carry):
        n = pl.program_id(0)

        @pl.when(n == 0)
        def _():
            ds_ref[...] = jnp.zeros(ds_ref.shape, F32)
            dk_carry[...] = jnp.zeros(dk_carry.shape, F32)
            dv_carry[...] = jnp.zeros(dv_carry.shape, F32)

        @pl.when(n < nb)
        def _():
            kb, vb, dist, valid = _swa_band(n, kp_ref, kc_ref, vp_ref, vc_ref, pq_ref, pp_ref, pc_ref)
            qv, dov = q_ref[...], do_ref[...]
            q_t, do_t, kb_t = qv.T, dov.T, kb.T
            lane = lax.broadcasted_iota(jnp.int32, (1, LANES), 1)
            dsink = jnp.zeros((1, LANES), F32)
            dq_t = []
            for kh in range(N_KV_A):
                heads = range(_GROUP_A * kh, _GROUP_A * (kh + 1))
                st_g = _dot(_head_cols(kb, kh).astype(BF16), _group_t(q_t, kh))
                dpt_g = _dot(_head_cols(vb, kh).astype(BF16), _group_t(do_t, kh))
                pts, dsts = [], []
                for j, h in enumerate(heads):
                    st = _swa_scores_t(st_g, j, h, dist, valid)
                    l_h, d_h = l_ref[h:h + 1, :], d_ref[h:h + 1, :]
                    pt = jnp.exp2(st - l_h)
                    p_sink = jnp.exp2(sink_ref[0:1, h:h + 1] * _LOG2E - l_h)
                    dsink = jnp.where(lane == h, jnp.sum(-p_sink * d_h, axis=1, keepdims=True), dsink)
                    dst = pt * (dpt_g[:, BLOCK * j:BLOCK * (j + 1)] - d_h) * _SWA_SCALE
                    pts.append(pt.astype(BF16))
                    dsts.append(dst.astype(BF16))
                pt_g, dst_g = jnp.concatenate(pts, axis=1), jnp.concatenate(dsts, axis=1)
                q_g = jnp.concatenate([_head_cols(qv, h) for h in heads], axis=0).astype(BF16)
                do_g = jnp.concatenate([_head_cols(dov, h) for h in heads], axis=0).astype(BF16)
                dkb_s[:, HEAD_DIM_A * kh:HEAD_DIM_A * (kh + 1)] = _dot(dst_g, q_g)
                dvb_s[:, HEAD_DIM_A * kh:HEAD_DIM_A * (kh + 1)] = _dot(pt_g, do_g)
                dq_g = _dot(_head_rows(kb_t, kh).astype(BF16), dst_g)
                dq_t.extend(dq_g[:, BLOCK * j:BLOCK * (j + 1)] for j in range(_GROUP_A))
            dq_ref[...] = jnp.concatenate(dq_t, axis=0).T
            ds_ref[...] += dsink
            dk_ref[...] = dk_carry[...] + dkb_s[0:BLOCK, :]
            dv_ref[...] = dv_carry[...] + dvb_s[0:BLOCK, :]
            dk_carry[...] = dkb_s[BLOCK:2 * BLOCK, :]
            dv_carry[...] = dvb_s[BLOCK:2 * BLOCK, :]

        @pl.when(n == nb)
        def _():
            dk_ref[...] = dk_carry[...]
            dv_ref[...] = dv_carry[...]

    cur = lambda n: (jnp.minimum(n, nb - 1), 0)
    cur_t = lambda n: (0, jnp.minimum(n, nb - 1))
    prv = lambda n: jnp.maximum(jnp.minimum(n, nb - 1) - 1, 0)
    out_prev = lambda n: (jnp.maximum(n - 1, 0), 0)
    return pl.pallas_call(
        body, name="swa_bwd", grid=(nb + 1,),
        in_specs=[pl.BlockSpec((BLOCK, WIDTH_A), lambda n: (jnp.minimum(n, nb - 1), _QA_BLK)),
                  pl.BlockSpec((BLOCK, LANES), lambda n: (jnp.minimum(n, nb - 1), _KA_BLK)),
                  pl.BlockSpec((BLOCK, LANES), lambda n: (prv(n), _KA_BLK)),
                  pl.BlockSpec((BLOCK, LANES), lambda n: (jnp.minimum(n, nb - 1), _VA_BLK)),
                  pl.BlockSpec((BLOCK, LANES), lambda n: (prv(n), _VA_BLK)),
                  pl.BlockSpec((BLOCK, WIDTH_A), cur), pl.BlockSpec((N_HEADS_A, BLOCK), cur_t),
                  pl.BlockSpec((N_HEADS_A, BLOCK), cur_t), pl.BlockSpec((1, BLOCK), cur_t),
                  pl.BlockSpec((BLOCK, 1), cur), pl.BlockSpec((BLOCK, 1), lambda n: (prv(n), 0)),
                  _full((1, N_HEADS_A))],
        out_specs=[pl.BlockSpec((BLOCK, WIDTH_A), cur), pl.BlockSpec((BLOCK, LANES), out_prev),
                   pl.BlockSpec((BLOCK, LANES), out_prev), _full((1, LANES))],
        out_shape=[jax.ShapeDtypeStruct((t, WIDTH_A), F32), jax.ShapeDtypeStruct((t, LANES), F32),
                   jax.ShapeDtypeStruct((t, LANES), F32), jax.ShapeDtypeStruct((1, LANES), F32)],
        scratch_shapes=[pltpu.VMEM((2 * BLOCK, LANES), F32), pltpu.VMEM((2 * BLOCK, LANES), F32),
                        pltpu.VMEM((BLOCK, LANES), F32), pltpu.VMEM((BLOCK, LANES), F32)],
        compiler_params=_params(("arbitrary",)),
    )(proj, proj, proj, proj, proj, d_out, lse, delta, posr, posc, posc, sinks)


def _in_bwd(dproj, w_in_t, x, dx1, g1, gp):
    t = x.shape[0]
    tm = 512
    steps = t // tm

    def body(dp_ref, w_ref, x_ref, dx1_ref, g_ref, gp_ref, dx_ref, dg_ref, land_ref, send_sems, recv_sems):
        i = pl.program_id(0)

        @pl.when(i == 0)
        def _():
            dg_ref[...] = jnp.zeros(dg_ref.shape, F32)
            _scatter_start(gp_ref, land_ref, send_sems, recv_sems)

        for rows in _row_halves(tm):
            dh = _dot(dp_ref[rows, :], w_ref[...])
            xv = x_ref[rows, :]
            r = _rms(xv)
            dx, dg = _norm_bwd(dh, xv * r, r, g_ref[...])
            dx_ref[rows, :] = dx1_ref[rows, :] + dx
            dg_ref[...] += dg

        @pl.when(i == steps - 1)
        def _():
            _scatter_wait(gp_ref, land_ref, send_sems, recv_sems)

    row = lambda i: (i, 0)
    blk = pl.BlockSpec((tm, D_MODEL), row)
    return pl.pallas_call(
        body, name="in_bwd", grid=(steps,),
        in_specs=[pl.BlockSpec((tm, D_IN_PAD), row), _full((D_IN_PAD, D_MODEL)), blk, blk, _full((1, D_MODEL)), _HBM],
        out_specs=[blk, _full((1, D_MODEL)), _HBM],
        out_shape=[jax.ShapeDtypeStruct((t, D_MODEL), F32), jax.ShapeDtypeStruct((1, D_MODEL), F32),
                   jax.ShapeDtypeStruct((3,) + gp.shape[1:], gp.dtype)],
        scratch_shapes=[pltpu.SemaphoreType.DMA((3,)), pltpu.SemaphoreType.DMA((3,))],
        compiler_params=_params(("arbitrary",)),
    )(dproj, w_in_t, x, dx1, g1, gp)


def _adamw_store(w, g, m, v, out_refs):
    g_out, d_out, m_out, v_out = out_refs
    m_new = ADAM_B1 * m + (1.0 - ADAM_B1) * g
    v_new = ADAM_B2 * v + (1.0 - ADAM_B2) * jnp.square(g)
    m_hat = m_new / (1.0 - ADAM_B1 ** ADAM_STEP)
    v_hat = v_new / (1.0 - ADAM_B2 ** ADAM_STEP)
    g_out[...] = g
    d_out[...] = -ADAM_LR * (m_hat / (jnp.sqrt(v_hat) + ADAM_EPS) + ADAM_WD * w)
    m_out[...] = m_new
    v_out[...] = v_new


_SMALL_SLOTS = {"pre_norm_mix": (0, 0, D_MODEL), "post_norm_mix": (1, 0, D_MODEL), "pre_norm_mlp": (2, 0, D_MODEL),
                "post_norm_mlp": (3, 0, D_MODEL), "q_a_norm": (4, 0, Q_LORA), "kv_a_norm": (4, Q_LORA, KV_LORA),
                "sinks": (4, Q_LORA + KV_LORA, N_HEADS_A)}
_LOSS_ROW = 5


def _adamw_small(red, w, m, v):
    names = tuple(_SMALL_SLOTS)
    n = len(names)

    def body(*refs):
        red_ref, ws, ms, vs, outs = refs[0], refs[1:1 + n], refs[1 + n:1 + 2 * n], refs[1 + 2 * n:1 + 3 * n], refs[1 + 3 * n:]
        for k, name in enumerate(names):
            row, lane, width = _SMALL_SLOTS[name]
            g = red_ref[row:row + 1, lane:lane + width]
            _adamw_store(ws[k][...], g, ms[k][...], vs[k][...], outs[4 * k:4 * k + 4])

    vmem = pl.BlockSpec(memory_space=pltpu.VMEM)
    res = pl.pallas_call(
        body, name="adamw_small", in_specs=[vmem] * (1 + 3 * n), out_specs=[vmem] * (4 * n),
        out_shape=[jax.ShapeDtypeStruct(w[name].shape, F32) for name in names for _ in range(4)],
    )(red, *[w[k] for k in names], *[m[k] for k in names], *[v[k] for k in names])
    return {name: res[4 * k:4 * k + 4] for k, name in enumerate(names)}


_ADAMW_RIDERS = ("w_up", "w_down", "w_out")


def _dw_in_adamw(dproj, h, g_parts, w, m, v):
    t, rows_out = dproj.shape
    tm, tk = rows_out // 2, min(1024, t)
    nk = t // tk
    steps = 2 * nk
    names = _ADAMW_RIDERS
    n = len(names)

    def body(a_ref, b_ref, *rest):
        g1s, g2s, ws, ms, vs = (rest[n * j:n * (j + 1)] for j in range(5))
        o_ref, outs = rest[5 * n], rest[5 * n + 1:]

        @pl.when(pl.program_id(2) == 0)
        def _():
            o_ref[...] = jnp.zeros(o_ref.shape, F32)

        o_ref[...] += _dot_tn(a_ref[...], b_ref[...])
        for j in range(n):
            _adamw_store(ws[j][...], g1s[j][...] + g2s[j][...], ms[j][...], vs[j][...], outs[4 * j:4 * j + 4])

    def rider_spec(name, packed):
        br = SHARD_SHAPES[name][0] // steps
        first = _row_offset(GROUP_B, name) // br if packed else 0
        return pl.BlockSpec((br, D_MODEL), lambda i, j, k: (first + i * nk + k, 0))

    g_specs = [rider_spec(name, True) for name in names]
    own_specs = [rider_spec(name, False) for name in names]
    res = pl.pallas_call(
        body, name="dw_in", grid=(2, 1, nk),
        in_specs=[pl.BlockSpec((tk, tm), lambda i, j, k: (k, i)), pl.BlockSpec((tk, D_MODEL), lambda i, j, k: (k, 0))]
        + g_specs * 2 + own_specs * 3,
        out_specs=[pl.BlockSpec((tm, D_MODEL), lambda i, j, k: (i, 0))] + [s for s in own_specs for _ in range(4)],
        out_shape=[jax.ShapeDtypeStruct((rows_out, D_MODEL), F32)]
        + [jax.ShapeDtypeStruct(SHARD_SHAPES[name], F32) for name in names for _ in range(4)],
        compiler_params=_params(("arbitrary", "arbitrary", "arbitrary")),
    )(dproj, h, *[g_parts[0]] * n, *[g_parts[1]] * n, *[w[k] for k in names], *[m[k] for k in names],
      *[v[k] for k in names])
    return res[0], {name: res[1 + 4 * j:5 + 4 * j] for j, name in enumerate(names)}


def _adamw(w, g_parts, m, v, name, block, g_row_off=0):
    r, c = w.shape
    br, bc = block
    ng = len(g_parts)

    def body(*refs):
        w_ref, g_refs, m_ref, v_ref = refs[0], refs[1:1 + ng], refs[1 + ng], refs[2 + ng]
        g = g_refs[0][...]
        for gr in g_refs[1:]:
            g = g + gr[...]
        _adamw_store(w_ref[...], g, m_ref[...], v_ref[...], refs[3 + ng:])

    assert g_row_off % br == 0 and r % br == 0 and c % bc == 0
    blk = pl.BlockSpec(block, lambda i, j: (i, j))
    g_blk = pl.BlockSpec(block, lambda i, j: (i + g_row_off // br, j))
    return pl.pallas_call(
        body, name=name, grid=(r // br, c // bc),
        in_specs=[blk] + [g_blk] * ng + [blk, blk], out_specs=[blk] * 4,
        out_shape=[jax.ShapeDtypeStruct((r, c), F32)] * 4,
        compiler_params=_params(("parallel", "parallel")),
    )(w, *g_parts, m, v)


_HBM = pl.BlockSpec(memory_space=pltpu.HBM)


def _other_chips(x, y):
    return ((1 - x, y), (x, 1 - y), (1 - x, 1 - y))


def _gather_copies(src, out, send_sems, recv_sems, local_sem):
    x, y, c = lax.axis_index("x"), lax.axis_index("y"), lax.axis_index("c")
    me = 2 * x + y
    local = pltpu.make_async_copy(src, out.at[me], local_sem)

    def copies(arriving):
        return [pltpu.make_async_remote_copy(src_ref=src, dst_ref=out.at[2 * px + py if arriving else me],
                                             send_sem=send_sems.at[j], recv_sem=recv_sems.at[j], device_id=(px, py, c),
                                             device_id_type=MESH)
                for j, (px, py) in enumerate(_other_chips(x, y))]

    return local, copies


def _gather_start(src, out, send_sems, recv_sems, local_sem):
    local, copies = _gather_copies(src, out, send_sems, recv_sems, local_sem)
    local.start()
    for cp in copies(False):
        cp.start()


def _gather_wait(src, out, send_sems, recv_sems, local_sem):
    local, copies = _gather_copies(src, out, send_sems, recv_sems, local_sem)
    for cp in copies(True):
        cp.wait_recv()
    for cp in copies(False):
        cp.wait_send()
    local.wait()


def _scatter_copies(src, land, send_sems, recv_sems):
    x, y, c = lax.axis_index("x"), lax.axis_index("y"), lax.axis_index("c")
    return [pltpu.make_async_remote_copy(src_ref=src.at[2 * px + py], dst_ref=land.at[j], send_sem=send_sems.at[j],
                                         recv_sem=recv_sems.at[j], device_id=(px, py, c), device_id_type=MESH)
            for j, (px, py) in enumerate(_other_chips(x, y))]


def _scatter_start(src, land, send_sems, recv_sems):
    for cp in _scatter_copies(src, land, send_sems, recv_sems):
        cp.start()


def _scatter_wait(src, land, send_sems, recv_sems):
    copies = _scatter_copies(src, land, send_sems, recv_sems)
    for cp in copies:
        cp.wait_recv()
    for cp in copies:
        cp.wait_send()


def _all_gather_chips(packed):
    r = packed.shape[0]
    half = r // 2

    def body(src, out, ici_send, ici_recv, d2d_send, d2d_recv, local_sem):
        x, y, c = lax.axis_index("x"), lax.axis_index("y"), lax.axis_index("c")
        me = 2 * x + y
        mine = pl.ds(pl.multiple_of(c * half, 16), half)
        theirs = pl.ds(pl.multiple_of((1 - c) * half, 16), half)
        chips = _other_chips(x, y)
        local = pltpu.make_async_copy(src, out.at[me], local_sem)
        local.start()
        sends = [pltpu.make_async_remote_copy(src_ref=src.at[mine], dst_ref=out.at[me, mine], send_sem=ici_send.at[j],
                                              recv_sem=ici_recv.at[j], device_id=(px, py, c), device_id_type=MESH)
                 for j, (px, py) in enumerate(chips)]
        for cp in sends:
            cp.start()
        passed = []
        for j, (px, py) in enumerate(chips):
            block = 2 * px + py
            pltpu.make_async_remote_copy(src_ref=src.at[mine], dst_ref=out.at[block, mine], send_sem=ici_send.at[j],
                                         recv_sem=ici_recv.at[j], device_id=(px, py, c), device_id_type=MESH).wait_recv()
            cp = pltpu.make_async_remote_copy(src_ref=out.at[block, mine], dst_ref=out.at[block, mine],
                                              send_sem=d2d_send.at[j], recv_sem=d2d_recv.at[j],
                                              device_id=(x, y, 1 - c), device_id_type=MESH)
            cp.start()
            passed.append(cp)
        for j, (px, py) in enumerate(chips):
            block = 2 * px + py
            pltpu.make_async_remote_copy(src_ref=out.at[block, theirs], dst_ref=out.at[block, theirs],
                                         send_sem=d2d_send.at[j], recv_sem=d2d_recv.at[j],
                                         device_id=(x, y, 1 - c), device_id_type=MESH).wait_recv()
        for cp in sends + passed:
            cp.wait_send()
        local.wait()

    sems = pltpu.SemaphoreType.DMA((3,))
    return pl.pallas_call(
        body, name="ag_weights", in_specs=[_HBM], out_specs=_HBM,
        out_shape=jax.ShapeDtypeStruct((N_CHIPS,) + packed.shape, packed.dtype),
        scratch_shapes=[sems, sems, sems, sems, pltpu.SemaphoreType.DMA(())],
    )(packed)


def _sum4(gp, land, chip, name):
    _, r, w = gp.shape
    tr = 256 if r % 256 == 0 else 128

    def body(chip_ref, o_ref, l_ref, s_ref):
        s_ref[...] = ((o_ref[0] + l_ref[0].astype(F32)) + l_ref[1].astype(F32)) + l_ref[2].astype(F32)

    return pl.pallas_call(
        body, name=name,
        grid_spec=pltpu.PrefetchScalarGridSpec(
            num_scalar_prefetch=1, grid=(r // tr,),
            in_specs=[pl.BlockSpec((1, tr, w), lambda i, chip_ref: (chip_ref[0], i, 0)),
                      pl.BlockSpec((3, tr, w), lambda i, chip_ref: (0, i, 0))],
            out_specs=pl.BlockSpec((tr, w), lambda i, chip_ref: (i, 0))),
        out_shape=jax.ShapeDtypeStruct((r, w), F32),
        compiler_params=_params(("parallel",)),
    )(chip, gp, land)


def _sibling_copy(src, got, send_sem, recv_sem):
    x, y, c = lax.axis_index("x"), lax.axis_index("y"), lax.axis_index("c")
    return pltpu.make_async_remote_copy(src_ref=src, dst_ref=got, send_sem=send_sem, recv_sem=recv_sem,
                                        device_id=(x, y, 1 - c), device_id_type=MESH)


def _swap_sibling(s, name):
    def body(src, got, send_sem, recv_sem):
        cp = _sibling_copy(src, got, send_sem, recv_sem)
        cp.start()
        cp.wait_recv()
        cp.wait_send()

    return pl.pallas_call(
        body, name=name, in_specs=[_HBM], out_specs=_HBM,
        out_shape=jax.ShapeDtypeStruct(s.shape, s.dtype),
        scratch_shapes=[pltpu.SemaphoreType.DMA(()), pltpu.SemaphoreType.DMA(())],
    )(s)


def _all_reduce_small(dsmall, loss):
    n_dev = 8
    names = tuple(_SMALL_SLOTS)
    shape = (8, D_MODEL)

    def body(*refs):
        parts, loss_ref = refs[:len(names)], refs[len(names)]
        out, src, gath, send_sems, recv_sems = refs[len(names) + 1:]
        x, y, c = lax.axis_index("x"), lax.axis_index("y"), lax.axis_index("c")
        me = 4 * x + 2 * y + c
        src[...] = jnp.zeros(shape, F32)
        for name, part in zip(names, parts):
            row, lane, _ = _SMALL_SLOTS[name]
            src[row:row + 1, lane:lane + part.shape[1]] = part[...]
        src[_LOSS_ROW:_LOSS_ROW + 1, 0:LANES] = loss_ref[...]
        gath[me] = src[...]
        peers = []
        for k in range(1, n_dev):
            px = 1 - x if (k >> 2) & 1 else x
            py = 1 - y if (k >> 1) & 1 else y
            pc = 1 - c if k & 1 else c
            peers.append((px, py, pc))
        sends = []
        for j, peer in enumerate(peers):
            cp = pltpu.make_async_remote_copy(src_ref=src, dst_ref=gath.at[me], send_sem=send_sems.at[j],
                                              recv_sem=recv_sems.at[j], device_id=peer, device_id_type=MESH)
            cp.start()
            sends.append(cp)
        for j, (px, py, pc) in enumerate(peers):
            pltpu.make_async_remote_copy(src_ref=src, dst_ref=gath.at[4 * px + 2 * py + pc], send_sem=send_sems.at[j],
                                         recv_sem=recv_sems.at[j], device_id=(px, py, pc), device_id_type=MESH).wait_recv()
        for cp in sends:
            cp.wait_send()
        acc = gath[0]
        for d in range(1, n_dev):
            acc = acc + gath[d]
        out[...] = acc

    vmem = pl.BlockSpec(memory_space=pltpu.VMEM)
    return pl.pallas_call(
        body, name="ar_small", in_specs=[vmem] * (len(names) + 1), out_specs=vmem,
        out_shape=jax.ShapeDtypeStruct(shape, F32),
        scratch_shapes=[pltpu.VMEM(shape, F32), pltpu.VMEM((n_dev,) + shape, F32),
                        pltpu.SemaphoreType.DMA((n_dev - 1,)), pltpu.SemaphoreType.DMA((n_dev - 1,))],
    )(*[dsmall[k] for k in names], loss)


_W_IN_ROWS = SHARD_SHAPES["w_in"][1]
_KR_ROW = 3200
_KR_PAD_ROW = _KR_BLK * LANES + QK_NOPE


def _shard_rows(name, a):
    return jnp.transpose(a) if name == "w_in" else a.reshape(PACK_ROWS[name], D_MODEL)


def _pack(group, shards, dtype):
    parts = [_shard_rows(n, shards[n]).astype(dtype) for n in group]
    pad = -sum(PACK_ROWS[n] for n in group) % LANES
    if pad:
        parts.append(jnp.zeros((pad, D_MODEL), dtype))
    return jnp.concatenate(parts, axis=0)


def _col_sharded_full(g, name, group):
    r, c = SHARD_SHAPES[name]
    off = _row_offset(group, name)
    blocks = g[:, off:off + PACK_ROWS[name]].reshape(N_CHIPS, r, c)
    return jnp.transpose(blocks, (1, 0, 2)).reshape(r, N_CHIPS * c)


def _col_sharded_blocks(d, name):
    r, c = SHARD_SHAPES[name]
    return jnp.transpose(d.reshape(r, N_CHIPS, c), (1, 0, 2)).reshape(N_CHIPS, PACK_ROWS[name], D_MODEL)


def _weights_a(g):
    dt = g.dtype
    w_in_t = g[:, :_W_IN_ROWS].reshape(N_CHIPS * _W_IN_ROWS, D_MODEL)
    z = lambda n: jnp.zeros((n, D_MODEL), dt)
    w_in_t = jnp.concatenate([w_in_t[:_KR_ROW], z(_KR_PAD_ROW - _KR_ROW), w_in_t[_KR_ROW:],
                              z(D_IN_PAD - _KR_PAD_ROW - QK_ROPE)], axis=0)
    wq = _col_sharded_full(g, "w_q_b", GROUP_A).reshape(Q_LORA, N_HEADS_B, Q_HEAD_B)
    wq_p = jnp.concatenate([wq, jnp.zeros((Q_LORA, N_HEADS_B, HEAD_PAD - Q_HEAD_B), dt)], axis=2).reshape(Q_LORA, MLA_W)
    wkv = _col_sharded_full(g, "w_kv_b", GROUP_A).reshape(KV_LORA, N_HEADS_B, QK_NOPE + V_DIM_B)
    zk = jnp.zeros((KV_LORA, N_HEADS_B, HEAD_PAD - QK_NOPE), dt)
    wk_p = jnp.concatenate([wkv[:, :, :QK_NOPE], zk], axis=2).reshape(KV_LORA, MLA_W)
    wv = wkv[:, :, QK_NOPE:].reshape(KV_LORA, N_HEADS_B * V_DIM_B)
    return dict(w_in=w_in_t, wq=wq_p, wk=wk_p, wv=wv, wv_t=jnp.transpose(wv))


def _grad_blocks_a(dw_in_t, dwq_p, dwk_p, dwv):
    dw_in = jnp.concatenate([dw_in_t[:_KR_ROW], dw_in_t[_KR_PAD_ROW:_KR_PAD_ROW + QK_ROPE]], axis=0)
    dwq = dwq_p.reshape(Q_LORA, N_HEADS_B, HEAD_PAD)[:, :, :Q_HEAD_B].reshape(Q_LORA, N_HEADS_B * Q_HEAD_B)
    dwk = dwk_p.reshape(KV_LORA, N_HEADS_B, HEAD_PAD)[:, :, :QK_NOPE]
    dwkv = jnp.concatenate([dwk, dwv.reshape(KV_LORA, N_HEADS_B, V_DIM_B)], axis=2)
    dwkv = dwkv.reshape(KV_LORA, N_HEADS_B * (QK_NOPE + V_DIM_B))
    pad = -sum(PACK_ROWS[n] for n in GROUP_A) % LANES
    return jnp.concatenate([dw_in.reshape(N_CHIPS, _W_IN_ROWS, D_MODEL), _col_sharded_blocks(dwq, "w_q_b"),
                            _col_sharded_blocks(dwkv, "w_kv_b"), jnp.zeros((N_CHIPS, pad, D_MODEL), F32)], axis=1)


def _rope_freq_lanes():
    freqs = ROPE_THETA ** (-jnp.arange(0, QK_ROPE, 2, dtype=F32) / QK_ROPE)
    return jnp.concatenate([jnp.zeros((QK_NOPE,), F32), freqs, freqs,
                            jnp.zeros((HEAD_PAD - Q_HEAD_B,), F32)]).reshape(1, LANES)


def _fwd_bwd(x, positions, target, w, m, v):
    t = x.shape[0]
    wa = _weights_a(_all_gather_chips(_pack(GROUP_A, w, BF16)))
    posr = positions.astype(F32).reshape(1, t)
    posc = posr.reshape(t, 1)
    freq = _rope_freq_lanes()
    g1, g2, g3, g4 = w["pre_norm_mix"], w["post_norm_mix"], w["pre_norm_mlp"], w["post_norm_mlp"]
    qan, kvan, sinks = w["q_a_norm"], w["kv_a_norm"], w["sinks"]

    h, proj = _proj_fwd(x, g1, wa["w_in"])
    out_a, lse_a = _swa_fwd(proj, posc, posr, sinks)
    qm, km, qt, kt, vt = _mla_prep_fwd(proj, posc, freq, qan, kvan, wa["wq"], wa["wk"], wa["wv_t"])
    out_bt, lse_b, wb = _mla_fwd(km, qt, vt, _pack(GROUP_B, w, BF16))
    w_oa, w_ob = _col_sharded_full(wb, "w_o_a", GROUP_B), _col_sharded_full(wb, "w_o_b", GROUP_B)
    merged, y, x1, h2 = _mix_out_fwd(out_a, out_bt, proj, x, w_oa, w_ob, wb, g2, g3)
    a = _up_fwd(h2, wb)
    dx2, dyd, dg4, loss = _down_fwd_loss(a, wb, x1, target, g4)

    gp_b = _dw_into_blocks(a, dyd, "w_down", 1024, _TK_DW)
    du = _down_bwd(dyd, wb, a)
    gp_b = _dw_into_blocks(h2, du, "w_up", 1024, _TK_DW, gp_b)
    dx1, dy, dg3, dg2 = _up_bwd(du, wb, x1, dx2, y, g3, g2)
    gp_b = _dw_into_blocks(merged, dy, "w_out", 1024, _TK_DW, gp_b)
    doa, dob, dga, dgb, d_out_a, d_out_b, d_out_bt, del_a, del_b = _mix_out_bwd(dy, out_a, out_bt, proj, w_oa, w_ob, wb)
    dw_oa = _matmul_tn(out_a, doa, "dw_o_a", 512, 1024)
    dw_ob = _dw_ob(out_bt, dob)
    small_b = jnp.concatenate([_col_sharded_blocks(dw_oa, "w_o_a"), _col_sharded_blocks(dw_ob, "w_o_b")], axis=1)
    gp_b = lax.dynamic_update_slice(gp_b, small_b, (0, _row_offset(GROUP_B, "w_o_a"), 0))
    dqm, dkm, dvm, land_b = _mla_bwd(qm, km, qt, kt, vt, d_out_b, d_out_bt, lse_b, del_b, gp_b)
    chip = (2 * lax.axis_index("x") + lax.axis_index("y")).astype(jnp.int32).reshape(1)
    part_b = _sum4(gp_b, land_b, chip, "rs_sum_b")
    dcq, dckv, dkr, dwq, dwk, dwv, dqan, dkvan, sib_b = _mla_prep_bwd(
        dqm, dkm, dvm, proj, posc, freq, qan, kvan, wa["wq"], wa["wk"], wa["wv"], part_b)
    dqa, dka, dva, dsinks = _swa_bwd(proj, d_out_a, lse_a, del_a, posc, posr, sinks)
    dproj = jnp.concatenate([dga, dgb, dqa.astype(BF16), dka.astype(BF16), dva.astype(BF16), dcq, dckv, dkr], axis=1)
    dw_in_t, updated = _dw_in_adamw(dproj, h, [part_b, sib_b], w, m, v)
    gp_a = _grad_blocks_a(dw_in_t, dwq, dwk, dwv)
    grad_x, dg1, land_a = _in_bwd(dproj, wa["w_in"], x, dx1, g1, gp_a.astype(BF16))

    part_a = _sum4(gp_a, land_a, chip, "rs_sum_a")
    reduced = {GROUP_A: [part_a, _swap_sibling(part_a, "rs_swap_a")], GROUP_B: [part_b, sib_b]}
    dsmall = dict(pre_norm_mix=dg1, post_norm_mix=dg2, pre_norm_mlp=dg3, post_norm_mlp=dg4,
                  q_a_norm=dqan, kv_a_norm=dkvan, sinks=dsinks)
    return loss, grad_x, reduced, dsmall, updated


def kernel(x, positions, pre_norm_mix, w_in, q_a_norm, w_q_b, kv_a_norm, w_kv_b, sinks, w_o_a, w_o_b, w_out, post_norm_mix, pre_norm_mlp, w_up, w_down, post_norm_mlp, loss_target, m_pre_norm_mix, m_w_in, m_q_a_norm, m_w_q_b, m_kv_a_norm, m_w_kv_b, m_sinks, m_w_o_a, m_w_o_b, m_w_out, m_post_norm_mix, m_pre_norm_mlp, m_w_up, m_w_down, m_post_norm_mlp, v_pre_norm_mix, v_w_in, v_q_a_norm, v_w_q_b, v_kv_a_norm, v_w_kv_b, v_sinks, v_w_o_a, v_w_o_b, v_w_out, v_post_norm_mix, v_pre_norm_mlp, v_w_up, v_w_down, v_post_norm_mlp):
    w = dict(pre_norm_mix=pre_norm_mix, w_in=w_in[0], q_a_norm=q_a_norm, w_q_b=w_q_b[0], kv_a_norm=kv_a_norm,
             w_kv_b=w_kv_b[0], sinks=sinks, w_o_a=w_o_a[0], w_o_b=w_o_b[0], w_out=w_out[0],
             post_norm_mix=post_norm_mix, pre_norm_mlp=pre_norm_mlp, w_up=w_up[0], w_down=w_down[0],
             post_norm_mlp=post_norm_mlp)
    m = dict(pre_norm_mix=m_pre_norm_mix, w_in=m_w_in[0], q_a_norm=m_q_a_norm, w_q_b=m_w_q_b[0],
             kv_a_norm=m_kv_a_norm, w_kv_b=m_w_kv_b[0], sinks=m_sinks, w_o_a=m_w_o_a[0], w_o_b=m_w_o_b[0],
             w_out=m_w_out[0], post_norm_mix=m_post_norm_mix, pre_norm_mlp=m_pre_norm_mlp, w_up=m_w_up[0],
             w_down=m_w_down[0], post_norm_mlp=m_post_norm_mlp)
    v = dict(pre_norm_mix=v_pre_norm_mix, w_in=v_w_in[0], q_a_norm=v_q_a_norm, w_q_b=v_w_q_b[0],
             kv_a_norm=v_kv_a_norm, w_kv_b=v_w_kv_b[0], sinks=v_sinks, w_o_a=v_w_o_a[0], w_o_b=v_w_o_b[0],
             w_out=v_w_out[0], post_norm_mix=v_post_norm_mix, pre_norm_mlp=v_pre_norm_mlp, w_up=v_w_up[0],
             w_down=v_w_down[0], post_norm_mlp=v_post_norm_mlp)

    loss, grad_x, reduced, dsmall, updated = _fwd_bwd(x[0], positions, loss_target[0], w, m, v)

    red = _all_reduce_small(dsmall, loss)
    small = _adamw_small(red, w, m, v)

    big = {}
    tr = jnp.transpose
    big["w_in"] = [tr(o)[None] for o in _adamw(tr(w["w_in"]), reduced[GROUP_A], tr(m["w_in"]), tr(v["w_in"]),
                                               "adamw_w_in", (_W_IN_ROWS, 256))]
    for n in _ADAMW_RIDERS:
        big[n] = [o[None] for o in updated[n]]
    for group, names in ((GROUP_A, ("w_q_b", "w_kv_b")), (GROUP_B, ("w_o_a", "w_o_b"))):
        for n in names:
            off = _row_offset(group, n)
            g_parts = [p[off:off + PACK_ROWS[n]].reshape(SHARD_SHAPES[n]) for p in reduced[group]]
            big[n] = [o[None] for o in _adamw(w[n], g_parts, m[n], v[n], "adamw_" + n, SHARD_SHAPES[n])]

    outs = [big[n][k] if n in big else small[n][k] for k in range(4) for n in WEIGHTS]
    return (red[_LOSS_ROW, 0], grad_x[None], *outs)
```

```python
import jax
import jax.numpy as jnp
from jax import lax
from jax.experimental import pallas as pl
from jax.experimental.pallas import tpu as pltpu

F32 = jnp.float32
BF16 = jnp.bfloat16
MESH = pl.DeviceIdType.MESH

D_MODEL = 1024
N_HEADS_A = 8
N_KV_A = 2
HEAD_DIM_A = 64
WINDOW = 128
BLOCK = 128
N_HEADS_B = 8
QK_NOPE = 64
QK_ROPE = 32
V_DIM_B = 64
Q_LORA = 256
KV_LORA = 128
ROPE_THETA = 10000.0
D_FF = 4 * D_MODEL
EPS = 1e-6
WIDTH_A = N_HEADS_A * HEAD_DIM_A
Q_HEAD_B = QK_NOPE + QK_ROPE
D_IN_PAD = 3328
HEAD_PAD = 128
MLA_W = N_HEADS_B * HEAD_PAD

ADAM_LR = 0.001
ADAM_B1 = 0.9
ADAM_B2 = 0.999
ADAM_EPS = 1e-08
ADAM_WD = 0.01
ADAM_STEP = 10

NEG = -1e30
N_CHIPS = 4
LANES = 128
VMEM_LIMIT = 56 * 1024 * 1024

SHARD_SHAPES = {"w_in": (1024, 808), "w_q_b": (256, 192), "w_kv_b": (128, 256), "w_o_a": (512, 256),
                "w_o_b": (512, 256), "w_out": (256, 1024), "w_up": (1024, 1024), "w_down": (1024, 1024)}
PACK_ROWS = {n: (s[0] * s[1]) // D_MODEL for n, s in SHARD_SHAPES.items()}
GROUP_A = ("w_in", "w_q_b", "w_kv_b")
GROUP_B = ("w_up", "w_down", "w_out", "w_o_a", "w_o_b")
WEIGHTS = ("pre_norm_mix", "w_in", "q_a_norm", "w_q_b", "kv_a_norm", "w_kv_b", "sinks", "w_o_a", "w_o_b", "w_out",
           "post_norm_mix", "pre_norm_mlp", "w_up", "w_down", "post_norm_mlp")


def _params(sem=None):
    return pltpu.CompilerParams(dimension_semantics=sem, vmem_limit_bytes=VMEM_LIMIT)


def _dot(a, b):
    return jnp.dot(a, b, preferred_element_type=F32)


def _dot_nt(a, b):
    return lax.dot_general(a, b, (((1,), (1,)), ((), ())), preferred_element_type=F32)


def _dot_tn(a, b):
    return lax.dot_general(a, b, (((0,), (0,)), ((), ())), preferred_element_type=F32)


def _rms(v):
    return lax.rsqrt(jnp.mean(v * v, axis=-1, keepdims=True) + EPS)


def _norm_bwd(dout, n, r, g):
    dn = dout * g
    dx = r * (dn - n * jnp.mean(dn * n, axis=-1, keepdims=True))
    return dx, jnp.sum(dout * n, axis=0, keepdims=True)


def _full(shape):
    return pl.BlockSpec(shape, lambda *_: (0,) * len(shape))


def _row_offset(group, name):
    return sum(PACK_ROWS[n] for n in group[:group.index(name)])


def _wb_spec(name):
    rows = PACK_ROWS[name]
    return pl.BlockSpec((N_CHIPS, rows, D_MODEL), lambda *_: (0, _row_offset(GROUP_B, name) // rows, 0))


def _proj_fwd(x, g1, w_in_t):
    t = x.shape[0]
    tm = 512

    def body(x_ref, g_ref, w_ref, h_ref, p_ref):
        for rows in _row_halves(tm):
            xv = x_ref[rows, :]
            h = ((xv * _rms(xv)) * g_ref[...]).astype(BF16)
            h_ref[rows, :] = h
            p_ref[rows, :] = _dot_nt(h, w_ref[...])

    return pl.pallas_call(
        body, name="proj_fwd", grid=(t // tm,),
        in_specs=[pl.BlockSpec((tm, D_MODEL), lambda i: (i, 0)), _full((1, D_MODEL)), _full((D_IN_PAD, D_MODEL))],
        out_specs=[pl.BlockSpec((tm, D_MODEL), lambda i: (i, 0)), pl.BlockSpec((tm, D_IN_PAD), lambda i: (i, 0))],
        out_shape=[jax.ShapeDtypeStruct((t, D_MODEL), BF16), jax.ShapeDtypeStruct((t, D_IN_PAD), F32)],
        compiler_params=_params(("parallel",)),
    )(x, g1, w_in_t)


_QA_BLK = 2048 // WIDTH_A
_KA_BLK = 2560 // LANES
_VA_BLK = 2688 // LANES
_CQ_BLK = 2816 // Q_LORA
_CKV_BLK = 3072 // LANES
_KR_BLK = 3200 // LANES


_GROUP_A = N_HEADS_A // N_KV_A
_SWA_SCALE = HEAD_DIM_A ** -0.5
_LOG2E = 1.4426950408889634


def _head_cols(v, h):
    return v[:, HEAD_DIM_A * h:HEAD_DIM_A * (h + 1)]


def _head_rows(v, h):
    return v[HEAD_DIM_A * h:HEAD_DIM_A * (h + 1), :]


def _swa_scores_t(st_g, j, h, dist, valid):
    slope = 2.0 ** (-8.0 * (h + 1) / N_HEADS_A)
    st = st_g[:, BLOCK * j:BLOCK * (j + 1)] * (_SWA_SCALE * _LOG2E) - (slope * _LOG2E) * dist
    return jnp.where(valid, st, NEG)


def _group_t(xt, kh):
    return jnp.concatenate([_head_rows(xt, _GROUP_A * kh + j) for j in range(_GROUP_A)], axis=1).astype(BF16)


_SWA_PER_STEP = 4


def _swa_fwd(proj, posc, posr, sinks):
    t = proj.shape[0]
    span = _SWA_PER_STEP * BLOCK

    def body(q_ref, kc_ref, kp_ref, vc_ref, vp_ref, pq_ref, pc_ref, pp_ref, sink_ref, o_ref, l_ref):
        n = pl.program_id(0)
        k_all = jnp.concatenate([kp_ref[...], kc_ref[...]], axis=0)
        v_all = jnp.concatenate([vp_ref[...], vc_ref[...]], axis=0)
        pos_all = jnp.concatenate([pp_ref[...], pc_ref[...]], axis=0)
        ki = lax.broadcasted_iota(jnp.int32, (2 * BLOCK, BLOCK), 0)
        qi = lax.broadcasted_iota(jnp.int32, (2 * BLOCK, BLOCK), 1)
        window = (ki > qi) & (ki <= qi + WINDOW)
        for sub in range(_SWA_PER_STEP):
            band = slice(BLOCK * sub, BLOCK * (sub + 2))
            own = slice(BLOCK * sub, BLOCK * (sub + 1))
            kb, vb = k_all[band], v_all[band]
            dist = jnp.abs(pos_all[band] - pq_ref[:, own])
            valid = window & ((n > 0) | (ki >= BLOCK)) if sub == 0 else window
            q_t, vb_t = q_ref[own, :].T, vb.T
            out_t, lse = [], []
            for kh in range(N_KV_A):
                st_g = _dot(_head_cols(kb, kh).astype(BF16), _group_t(q_t, kh))
                ps = []
                for j in range(_GROUP_A):
                    h = _GROUP_A * kh + j
                    st = _swa_scores_t(st_g, j, h, dist, valid)
                    sink = sink_ref[0:1, h:h + 1] * _LOG2E
                    m = jnp.maximum(jnp.max(st, axis=0, keepdims=True), sink)
                    e = jnp.exp2(st - m)
                    den = jnp.sum(e, axis=0, keepdims=True) + jnp.exp2(sink - m)
                    ps.append((e * (1.0 / den)).astype(BF16))
                    lse.append(m + jnp.log(den) * _LOG2E)
                o_g = _dot(_head_rows(vb_t, kh).astype(BF16), jnp.concatenate(ps, axis=1))
                out_t.extend(o_g[:, BLOCK * j:BLOCK * (j + 1)] for j in range(_GROUP_A))
            o_ref[own, :] = jnp.concatenate(out_t, axis=0).T
            l_ref[:, own] = jnp.concatenate(lse, axis=0)

    cur = lambda n: (n, 0)
    prev = lambda n: jnp.maximum(_SWA_PER_STEP * n - 1, 0)
    return pl.pallas_call(
        body, name="swa_fwd", grid=(t // span,),
        in_specs=[pl.BlockSpec((span, WIDTH_A), lambda n: (n, _QA_BLK)),
                  pl.BlockSpec((span, LANES), lambda n: (n, _KA_BLK)),
                  pl.BlockSpec((BLOCK, LANES), lambda n: (prev(n), _KA_BLK)),
                  pl.BlockSpec((span, LANES), lambda n: (n, _VA_BLK)),
                  pl.BlockSpec((BLOCK, LANES), lambda n: (prev(n), _VA_BLK)),
                  pl.BlockSpec((1, span), lambda n: (0, n)),
                  pl.BlockSpec((span, 1), cur),
                  pl.BlockSpec((BLOCK, 1), lambda n: (prev(n), 0)),
                  _full((1, N_HEADS_A))],
        out_specs=[pl.BlockSpec((span, WIDTH_A), cur), pl.BlockSpec((N_HEADS_A, span), lambda n: (0, n))],
        out_shape=[jax.ShapeDtypeStruct((t, WIDTH_A), F32), jax.ShapeDtypeStruct((N_HEADS_A, t), F32)],
        compiler_params=_params(("parallel",)),
    )(proj, proj, proj, proj, proj, posr, posc, posc, sinks)


def _rope_coeffs(pos, freq):
    ang = pos * freq
    cosv, sinv = jnp.cos(ang), jnp.sin(ang)
    lane = lax.broadcasted_iota(jnp.int32, ang.shape, 1)
    lo = (lane >= QK_NOPE) & (lane < QK_NOPE + QK_ROPE // 2)
    hi = (lane >= QK_NOPE + QK_ROPE // 2) & (lane < QK_NOPE + QK_ROPE)
    c = jnp.where(lane < QK_NOPE, 1.0, jnp.where(lo | hi, cosv, 0.0))
    s = jnp.where(lo, -sinv, jnp.where(hi, sinv, 0.0))
    return c, s, lo, hi


def _rope(xh, c, s, lo):
    up = pltpu.roll(xh, LANES - QK_ROPE // 2, axis=1)
    dn = pltpu.roll(xh, QK_ROPE // 2, axis=1)
    return xh * c + jnp.where(lo, up, dn) * s


def _unrope(dh, c, s, lo, hi):
    g = dh * s
    up = pltpu.roll(g, LANES - QK_ROPE // 2, axis=1)
    dn = pltpu.roll(g, QK_ROPE // 2, axis=1)
    return dh * c + jnp.where(hi, dn, jnp.where(lo, up, 0.0))


_TQ = 512
_MLA_SCALE = Q_HEAD_B ** -0.5


def _mla_prep_fwd(proj, posc, freq, qan, kvan, wq, wk, wv):
    t = proj.shape[0]
    tm = _TQ
    nb = t // tm

    def body(cq_ref, ckv_ref, kr_ref, pos_ref, f_ref, qan_ref, kvan_ref, wq_ref, wk_ref, wv_ref,
             q_ref, k_ref, qt_ref, kt_ref, vt_ref):
        cq = cq_ref[...]
        cqn = ((cq * _rms(cq)) * qan_ref[...]).astype(BF16)
        ckv = ckv_ref[...]
        ckvn = ((ckv * _rms(ckv)) * kvan_ref[...]).astype(BF16)
        qb = _dot(cqn, wq_ref[...])
        kb = _dot(ckvn, wk_ref[...])
        vbt = _dot_nt(wv_ref[...], ckvn)
        c, s, lo, _ = _rope_coeffs(pos_ref[...], f_ref[...])
        kr = _rope(kr_ref[...], c, s, lo)
        for h in range(N_HEADS_B):
            sl = slice(HEAD_PAD * h, HEAD_PAD * (h + 1))
            q_h = _rope(qb[:, sl], c, s, lo)
            k_h = kb[:, sl] + kr
            q_ref[:, sl] = q_h.astype(BF16)
            k_ref[:, sl] = k_h.astype(BF16)
            qt_ref[h, 0] = q_h.T.astype(BF16)
            kt_ref[h, 0] = k_h.T.astype(BF16)
            vt_ref[h, 0] = vbt[V_DIM_B * h:V_DIM_B * (h + 1), :].astype(BF16)

    row = lambda i: (i, 0)
    blk4 = lambda d: pl.BlockSpec((N_HEADS_B, 1, d, tm), lambda i: (0, i, 0, 0))
    return pl.pallas_call(
        body, name="mla_prep_fwd", grid=(nb,),
        in_specs=[pl.BlockSpec((tm, Q_LORA), lambda i: (i, _CQ_BLK)),
                  pl.BlockSpec((tm, LANES), lambda i: (i, _CKV_BLK)),
                  pl.BlockSpec((tm, LANES), lambda i: (i, _KR_BLK)),
                  pl.BlockSpec((tm, 1), row), _full((1, LANES)), _full((1, Q_LORA)), _full((1, KV_LORA)),
                  _full((Q_LORA, MLA_W)), _full((KV_LORA, MLA_W)), _full((N_HEADS_B * V_DIM_B, KV_LORA))],
        out_specs=[pl.BlockSpec((tm, MLA_W), row), pl.BlockSpec((tm, MLA_W), row), blk4(HEAD_PAD), blk4(HEAD_PAD),
                   blk4(V_DIM_B)],
        out_shape=[jax.ShapeDtypeStruct((t, MLA_W), BF16), jax.ShapeDtypeStruct((t, MLA_W), BF16),
                   jax.ShapeDtypeStruct((N_HEADS_B, nb, HEAD_PAD, tm), BF16),
                   jax.ShapeDtypeStruct((N_HEADS_B, nb, HEAD_PAD, tm), BF16),
                   jax.ShapeDtypeStruct((N_HEADS_B, nb, V_DIM_B, tm), BF16)],
        compiler_params=_params(("parallel",)),
    )(proj, proj, proj, posc, freq, qan, kvan, wq, wk, wv)


_MLA_SCALE2 = _MLA_SCALE * _LOG2E


def _mla_fwd(k, qt, vt, w_src):
    t = k.shape[0]
    nb = t // _TQ

    def body(k_ref, qt_ref, vt_ref, w_ref, o_ref, l_ref, wg_ref, raw_a, raw_b, send_sems, recv_sems, local_sem):
        qi = pl.program_id(1)
        first = (pl.program_id(0) == 0) & (qi == 0)
        last = (pl.program_id(0) == N_HEADS_B - 1) & (qi == nb - 1)

        @pl.when(first)
        def _():
            _gather_start(w_ref, wg_ref, send_sems, recv_sems, local_sem)

        q_t = qt_ref[0, 0]

        def product(kj):
            return _dot(k_ref[pl.ds(pl.multiple_of(kj * _TQ, _TQ), _TQ), :], q_t)

        def update(stats, raw_ref, kj, diagonal=False):
            m, l, acc = stats
            raw = raw_ref[...]
            if diagonal:
                key = lax.broadcasted_iota(jnp.int32, raw.shape, 0)
                qry = lax.broadcasted_iota(jnp.int32, raw.shape, 1)
                raw = jnp.where(key <= qry, raw, NEG)
            m_new = jnp.maximum(m, jnp.max(raw, axis=0, keepdims=True) * _MLA_SCALE2)
            alpha = jnp.exp2(m - m_new)
            p = jnp.exp2(raw * _MLA_SCALE2 - m_new)
            l = alpha * l + jnp.sum(p, axis=0, keepdims=True)
            acc = alpha * acc + _dot(vt_ref[0, kj], p.astype(BF16))
            return m_new, l, acc

        def trip(i, stats):
            raw_b[...] = product(2 * i + 1)
            stats = update(stats, raw_a, 2 * i)
            raw_a[...] = product(2 * i + 2)
            return update(stats, raw_b, 2 * i + 1)

        def tail_even(stats):
            return update(stats, raw_a, qi, True)

        def tail_odd(stats):
            raw_b[...] = product(qi)
            return update(update(stats, raw_a, qi - 1), raw_b, qi, True)

        init = (jnp.full((1, _TQ), NEG, F32), jnp.zeros((1, _TQ), F32), jnp.zeros((V_DIM_B, _TQ), F32))
        raw_a[...] = product(0)
        stats = lax.fori_loop(0, qi // 2, trip, init)
        m, l, acc = lax.cond(qi % 2 == 0, tail_even, tail_odd, stats)
        o_ref[0, 0] = acc / l
        l_ref[0, 0] = m + jnp.log(l) * _LOG2E

        @pl.when(last)
        def _():
            _gather_wait(w_ref, wg_ref, send_sems, recv_sems, local_sem)

    return pl.pallas_call(
        body, name="mla_fwd", grid=(N_HEADS_B, nb),
        in_specs=[pl.BlockSpec((t, HEAD_PAD), lambda h, qi: (0, h)),
                  pl.BlockSpec((1, 1, HEAD_PAD, _TQ), lambda h, qi: (h, qi, 0, 0)),
                  pl.BlockSpec((1, nb, V_DIM_B, _TQ), lambda h, qi: (h, 0, 0, 0)), _HBM],
        out_specs=[pl.BlockSpec((1, 1, V_DIM_B, _TQ), lambda h, qi: (h, qi, 0, 0)),
                   pl.BlockSpec((1, 1, 1, _TQ), lambda h, qi: (h, qi, 0, 0)), _HBM],
        out_shape=[jax.ShapeDtypeStruct((N_HEADS_B, nb, V_DIM_B, _TQ), F32),
                   jax.ShapeDtypeStruct((N_HEADS_B, nb, 1, _TQ), F32),
                   jax.ShapeDtypeStruct((N_CHIPS,) + w_src.shape, w_src.dtype)],
        scratch_shapes=[pltpu.VMEM((_TQ, _TQ), F32), pltpu.VMEM((_TQ, _TQ), F32),
                        pltpu.SemaphoreType.DMA((3,)), pltpu.SemaphoreType.DMA((3,)), pltpu.SemaphoreType.DMA(())],
        compiler_params=_params(("arbitrary", "arbitrary")),
    )(k, qt, vt, w_src)


def _ot_spec(tm, d):
    per = _TQ // tm
    return pl.BlockSpec((N_HEADS_B, 1, d, tm), lambda i: (0, i // per, 0, i % per))


def _mix_out_fwd(out_a, out_bt, proj, x, w_oa, w_ob, wb, g2, g3):
    t = x.shape[0]
    tm = 512

    def body(oa_ref, obt_ref, ga_ref, gb_ref, x_ref, woa_ref, wob_ref, wout_ref, g2_ref, g3_ref,
             mg_ref, y_ref, x1_ref, h2_ref):
        oa = _dot(oa_ref[...].astype(BF16), woa_ref[...])
        obt = obt_ref[...].reshape(N_HEADS_B * V_DIM_B, tm).astype(BF16)
        ob = _dot_tn(obt, wob_ref[...])
        merged = (jax.nn.sigmoid(ga_ref[...]) * oa + jax.nn.sigmoid(gb_ref[...]) * ob).astype(BF16)
        mg_ref[...] = merged
        y = _dot(merged, wout_ref[...].reshape(D_MODEL, D_MODEL))
        y_ref[...] = y
        x1 = x_ref[...] + (y * _rms(y)) * g2_ref[...]
        x1_ref[...] = x1
        h2_ref[...] = ((x1 * _rms(x1)) * g3_ref[...]).astype(BF16)

    row = lambda i: (i, 0)
    blk = pl.BlockSpec((tm, D_MODEL), row)
    return pl.pallas_call(
        body, name="mix_out_fwd", grid=(t // tm,),
        in_specs=[pl.BlockSpec((tm, WIDTH_A), row), _ot_spec(tm, V_DIM_B), pl.BlockSpec((tm, D_MODEL), lambda i: (i, 0)),
                  pl.BlockSpec((tm, D_MODEL), lambda i: (i, 1)), blk,
                  _full((WIDTH_A, D_MODEL)), _full((N_HEADS_B * V_DIM_B, D_MODEL)), _wb_spec("w_out"),
                  _full((1, D_MODEL)), _full((1, D_MODEL))],
        out_specs=[blk, blk, blk, blk],
        out_shape=[jax.ShapeDtypeStruct((t, D_MODEL), BF16), jax.ShapeDtypeStruct((t, D_MODEL), F32),
                   jax.ShapeDtypeStruct((t, D_MODEL), F32), jax.ShapeDtypeStruct((t, D_MODEL), BF16)],
        compiler_params=_params(("parallel",)),
    )(out_a, out_bt, proj, proj, x, w_oa, w_ob, wb, g2, g3)


_TM_MLP = 512


def _row_halves(tm):
    return slice(0, tm // 2), slice(tm // 2, tm)


def _up_fwd(h2, wb):
    t = h2.shape[0]
    tm = _TM_MLP

    def body(h_ref, w_ref, a_ref):
        hv = h_ref[...]
        for j in range(N_CHIPS):
            u = _dot(hv, w_ref[j])
            a_ref[:, D_MODEL * j:D_MODEL * (j + 1)] = jnp.square(jnp.maximum(u, 0.0)).astype(BF16)

    return pl.pallas_call(
        body, name="up_fwd", grid=(t // tm,),
        in_specs=[pl.BlockSpec((tm, D_MODEL), lambda i: (i, 0)), _wb_spec("w_up")],
        out_specs=pl.BlockSpec((tm, D_FF), lambda i: (i, 0)),
        out_shape=jax.ShapeDtypeStruct((t, D_FF), BF16),
        compiler_params=_params(("parallel",)),
    )(h2, wb)


def _down_fwd_loss(a, wb, x1, target, g4):
    t = a.shape[0]
    tm = _TM_MLP

    def body(a_ref, w_ref, x1_ref, tg_ref, g_ref, dx2_ref, dyd_ref, dg_ref, loss_ref):
        @pl.when(pl.program_id(0) == 0)
        def _():
            dg_ref[...] = jnp.zeros(dg_ref.shape, F32)
            loss_ref[...] = jnp.zeros(loss_ref.shape, F32)

        yd = _dot(a_ref[...], w_ref[...].reshape(D_FF, D_MODEL))
        r = _rms(yd)
        n = yd * r
        diff = (x1_ref[...] + n * g_ref[...]) - tg_ref[...]
        loss_ref[...] += 0.5 * jnp.sum(jnp.mean(diff * diff, axis=-1, keepdims=True), axis=0, keepdims=True)
        dx2 = diff * (1.0 / D_MODEL)
        dx2_ref[...] = dx2
        dyd, dg = _norm_bwd(dx2, n, r, g_ref[...])
        dyd_ref[...] = dyd.astype(BF16)
        dg_ref[...] += dg

    row = lambda i: (i, 0)
    blk = pl.BlockSpec((tm, D_MODEL), row)
    return pl.pallas_call(
        body, name="down_fwd_loss", grid=(t // tm,),
        in_specs=[pl.BlockSpec((tm, D_FF), row), _wb_spec("w_down"), blk, blk, _full((1, D_MODEL))],
        out_specs=[blk, blk, _full((1, D_MODEL)), _full((1, LANES))],
        out_shape=[jax.ShapeDtypeStruct((t, D_MODEL), F32), jax.ShapeDtypeStruct((t, D_MODEL), BF16),
                   jax.ShapeDtypeStruct((1, D_MODEL), F32), jax.ShapeDtypeStruct((1, LANES), F32)],
        compiler_params=_params(("arbitrary",)),
    )(a, wb, x1, target, g4)


def _matmul_tn(a, b, name, tm, tn, tk=1024):
    t, m = a.shape
    n = b.shape[1]
    tk = min(tk, t)
    nk = t // tk

    def body(a_ref, b_ref, o_ref):
        @pl.when(pl.program_id(2) == 0)
        def _():
            o_ref[...] = jnp.zeros(o_ref.shape, F32)

        o_ref[...] += _dot_tn(a_ref[...].astype(BF16), b_ref[...].astype(BF16))

    return pl.pallas_call(
        body, name=name, grid=(m // tm, n // tn, nk),
        in_specs=[pl.BlockSpec((tk, tm), lambda i, j, k: (k, i)), pl.BlockSpec((tk, tn), lambda i, j, k: (k, j))],
        out_specs=pl.BlockSpec((tm, tn), lambda i, j, k: (i, j)),
        out_shape=jax.ShapeDtypeStruct((m, n), F32),
        compiler_params=_params(("parallel", "parallel", "arbitrary")),
    )(a, b)


_TK_DW = 2048


def _dw_into_blocks(a, b, weight, tm, tk, buf=None):
    t, m = a.shape
    n = b.shape[1]
    tk = min(tk, t)
    nk = t // tk
    rows = PACK_ROWS[weight]
    br = min(tm, rows)
    chips = tm // br
    first = _row_offset(GROUP_B, weight) // br
    per_chip = rows // br
    if weight == "w_up":
        out_map = lambda i, j, k: (j, first + i, 0)
    elif chips > 1:
        out_map = lambda i, j, k: (i, first, 0)
    else:
        out_map = lambda i, j, k: (i // per_chip, first + i % per_chip, 0)

    def body(a_ref, b_ref, *rest):
        o_ref = rest[-1]

        @pl.when(pl.program_id(2) == 0)
        def _():
            o_ref[...] = jnp.zeros(o_ref.shape, F32)

        o_ref[...] += _dot_tn(a_ref[...].astype(BF16), b_ref[...].astype(BF16)).reshape(o_ref.shape)

    in_specs = [pl.BlockSpec((tk, tm), lambda i, j, k: (k, i)), pl.BlockSpec((tk, D_MODEL), lambda i, j, k: (k, j))]
    operands = [a, b]
    if buf is not None:
        in_specs.append(pl.BlockSpec(memory_space=pl.ANY))
        operands.append(buf)
    total = sum(PACK_ROWS[w] for w in GROUP_B)
    return pl.pallas_call(
        body, name="dw_" + weight[2:], grid=(m // tm, n // D_MODEL, nk),
        in_specs=in_specs, out_specs=pl.BlockSpec((chips, br, D_MODEL), out_map),
        out_shape=jax.ShapeDtypeStruct((N_CHIPS, total, D_MODEL), F32),
        input_output_aliases={} if buf is None else {2: 0},
        compiler_params=_params(("parallel", "parallel", "arbitrary")),
    )(*operands)


def _down_bwd(dyd, wb, a):
    t = dyd.shape[0]
    tm = _TM_MLP

    def body(d_ref, w_ref, a_ref, du_ref):
        dv = d_ref[...]
        for j in range(N_CHIPS):
            cols = slice(D_MODEL * j, D_MODEL * (j + 1))
            av = a_ref[:, cols].astype(F32)
            relu_u = jnp.where(av > 0.0, av * lax.rsqrt(av), 0.0)
            du_ref[:, cols] = (_dot_nt(dv, w_ref[j]) * (2.0 * relu_u)).astype(BF16)

    row = lambda i: (i, 0)
    return pl.pallas_call(
        body, name="down_bwd", grid=(t // tm,),
        in_specs=[pl.BlockSpec((tm, D_MODEL), row), _wb_spec("w_down"), pl.BlockSpec((tm, D_FF), row)],
        out_specs=pl.BlockSpec((tm, D_FF), row),
        out_shape=jax.ShapeDtypeStruct((t, D_FF), BF16),
        compiler_params=_params(("parallel",)),
    )(dyd, wb, a)


def _up_bwd(du, wb, x1, dx2, y, g3, g2):
    t = du.shape[0]
    tm = _TM_MLP

    def body(du_ref, w_ref, x1_ref, dx2_ref, y_ref, g3_ref, g2_ref, dx1_ref, dy_ref, dg3_ref, dg2_ref):
        @pl.when(pl.program_id(0) == 0)
        def _():
            dg3_ref[...] = jnp.zeros(dg3_ref.shape, F32)
            dg2_ref[...] = jnp.zeros(dg2_ref.shape, F32)

        dh2 = _dot_nt(du_ref[:, 0:D_MODEL], w_ref[0])
        for j in range(1, N_CHIPS):
            dh2 = dh2 + _dot_nt(du_ref[:, D_MODEL * j:D_MODEL * (j + 1)], w_ref[j])
        x1 = x1_ref[...]
        r3 = _rms(x1)
        d3, dg3 = _norm_bwd(dh2, x1 * r3, r3, g3_ref[...])
        dx1 = dx2_ref[...] + d3
        dx1_ref[...] = dx1
        dg3_ref[...] += dg3
        y = y_ref[...]
        r2 = _rms(y)
        dy, dg2 = _norm_bwd(dx1, y * r2, r2, g2_ref[...])
        dy_ref[...] = dy.astype(BF16)
        dg2_ref[...] += dg2

    row = lambda i: (i, 0)
    blk = pl.BlockSpec((tm, D_MODEL), row)
    return pl.pallas_call(
        body, name="up_bwd", grid=(t // tm,),
        in_specs=[pl.BlockSpec((tm, D_FF), row), _wb_spec("w_up"),
                  blk, blk, blk, _full((1, D_MODEL)), _full((1, D_MODEL))],
        out_specs=[blk, blk, _full((1, D_MODEL)), _full((1, D_MODEL))],
        out_shape=[jax.ShapeDtypeStruct((t, D_MODEL), F32), jax.ShapeDtypeStruct((t, D_MODEL), BF16),
                   jax.ShapeDtypeStruct((1, D_MODEL), F32), jax.ShapeDtypeStruct((1, D_MODEL), F32)],
        compiler_params=_params(("arbitrary",)),
    )(du, wb, x1, dx2, y, g3, g2)


def _mix_out_bwd(dy, out_a, out_bt, proj, w_oa, w_ob, wb):
    t = dy.shape[0]
    tm = 256
    nb = t // _TQ

    def body(dy_ref, oa_ref, obt_ref, ga_ref, gb_ref, woa_ref, wob_ref, wout_ref,
             doa_ref, dob_ref, dga_ref, dgb_ref, da_ref, db_ref, dbt_ref, dela_ref, delb_ref):
        dm = _dot_nt(dy_ref[...], wout_ref[...].reshape(D_MODEL, D_MODEL))
        out_a_v = oa_ref[...]
        out_bt_v = obt_ref[...].reshape(N_HEADS_B * V_DIM_B, tm)
        oa = _dot(out_a_v.astype(BF16), woa_ref[...])
        ob = _dot_tn(out_bt_v.astype(BF16), wob_ref[...])
        sa, sb = jax.nn.sigmoid(ga_ref[...]), jax.nn.sigmoid(gb_ref[...])
        doa = (dm * sa).astype(BF16)
        dob = (dm * sb).astype(BF16)
        doa_ref[...] = doa
        dob_ref[...] = dob
        dga_ref[...] = (dm * oa * (sa * (1.0 - sa))).astype(BF16)
        dgb_ref[...] = (dm * ob * (sb * (1.0 - sb))).astype(BF16)
        d_out_a = _dot_nt(doa, woa_ref[...])
        da_ref[...] = d_out_a
        prod_at = (d_out_a * out_a_v).T
        dela_ref[...] = jnp.concatenate(
            [jnp.sum(_head_rows(prod_at, h), axis=0, keepdims=True) for h in range(N_HEADS_A)], axis=0)
        d_out_b = _dot_nt(dob, wob_ref[...])
        d_out_bt = _dot_nt(wob_ref[...], dob)
        prod_bt = d_out_bt * out_bt_v
        for h in range(N_HEADS_B):
            db_ref[h] = d_out_b[:, V_DIM_B * h:V_DIM_B * (h + 1)].astype(BF16)
            dbt_ref[h, 0] = d_out_bt[V_DIM_B * h:V_DIM_B * (h + 1), :].astype(BF16)
            delb_ref[h, 0] = jnp.sum(prod_bt[V_DIM_B * h:V_DIM_B * (h + 1), :], axis=0, keepdims=True)

    row = lambda i: (i, 0)
    blk = pl.BlockSpec((tm, D_MODEL), row)
    return pl.pallas_call(
        body, name="mix_out_bwd", grid=(t // tm,),
        in_specs=[blk, pl.BlockSpec((tm, WIDTH_A), row), _ot_spec(tm, V_DIM_B),
                  pl.BlockSpec((tm, D_MODEL), lambda i: (i, 0)), pl.BlockSpec((tm, D_MODEL), lambda i: (i, 1)),
                  _full((WIDTH_A, D_MODEL)), _full((N_HEADS_B * V_DIM_B, D_MODEL)), _wb_spec("w_out")],
        out_specs=[blk, blk, blk, blk, pl.BlockSpec((tm, WIDTH_A), row),
                   pl.BlockSpec((N_HEADS_B, tm, V_DIM_B), lambda i: (0, i, 0)), _ot_spec(tm, V_DIM_B),
                   pl.BlockSpec((N_HEADS_A, tm), lambda i: (0, i)), _ot_spec(tm, 1)],
        out_shape=[jax.ShapeDtypeStruct((t, D_MODEL), BF16)] * 4
        + [jax.ShapeDtypeStruct((t, WIDTH_A), F32), jax.ShapeDtypeStruct((N_HEADS_B, t, V_DIM_B), BF16),
           jax.ShapeDtypeStruct((N_HEADS_B, nb, V_DIM_B, _TQ), BF16), jax.ShapeDtypeStruct((N_HEADS_A, t), F32),
           jax.ShapeDtypeStruct((N_HEADS_B, nb, 1, _TQ), F32)],
        compiler_params=_params(("parallel",)),
    )(dy, out_a, out_bt, proj, proj, w_oa, w_ob, wb)


def _dw_ob(out_bt, dob):
    t = dob.shape[0]
    nb = t // _TQ

    def body(obt_ref, dob_ref, o_ref):
        @pl.when(pl.program_id(0) == 0)
        def _():
            o_ref[...] = jnp.zeros(o_ref.shape, F32)

        obt = obt_ref[...].reshape(N_HEADS_B * V_DIM_B, _TQ).astype(BF16)
        o_ref[...] += _dot(obt, dob_ref[...])

    return pl.pallas_call(
        body, name="dw_o_b", grid=(nb,),
        in_specs=[pl.BlockSpec((N_HEADS_B, 1, V_DIM_B, _TQ), lambda i: (0, i, 0, 0)),
                  pl.BlockSpec((_TQ, D_MODEL), lambda i: (i, 0))],
        out_specs=_full((N_HEADS_B * V_DIM_B, D_MODEL)),
        out_shape=jax.ShapeDtypeStruct((N_HEADS_B * V_DIM_B, D_MODEL), F32),
        compiler_params=_params(("arbitrary",)),
    )(out_bt, dob)


def _mla_bwd(q, k, qt, kt, vt, d_out, d_out_t, lse, delta, gp):
    t = q.shape[0]
    nb = t // _TQ

    def body(k_ref, kt_ref, vt_ref, q_ref, qt_ref, do_ref, dot_ref, lrow_ref, drow_ref, gp_ref,
             dq_ref, dkt_ref, dvt_ref, land_ref, l_rep, d_rep, send_sems, recv_sems):
        step = pl.program_id(1)
        kj = nb - 1 - step

        @pl.when((pl.program_id(0) == 0) & (step == 0))
        def _():
            _scatter_start(gp_ref, land_ref, send_sems, recv_sems)

        @pl.when(step == 0)
        def _():
            dq_ref[...] = jnp.zeros(dq_ref.shape, F32)
            for b in range(nb):
                l_rep[_TQ * b:_TQ * (b + 1), :] = jnp.broadcast_to(lrow_ref[0, b], (LANES, _TQ)).T
                d_rep[_TQ * b:_TQ * (b + 1), :] = jnp.broadcast_to(drow_ref[0, b], (LANES, _TQ)).T

        kv, k_t, v_t = k_ref[...], kt_ref[0, 0], vt_ref[0, 0]

        def rows_of(qi):
            return pl.ds(pl.multiple_of(qi * _TQ, _TQ), _TQ)

        def products(qi, diagonal=False):
            s = _dot(q_ref[rows_of(qi), :], k_t) * _MLA_SCALE2
            if diagonal:
                qry = lax.broadcasted_iota(jnp.int32, s.shape, 0)
                key = lax.broadcasted_iota(jnp.int32, s.shape, 1)
                s = jnp.where(key <= qry, s, NEG)
            return s, _dot(do_ref[0, rows_of(qi), :], v_t)

        def update(carry, prods, qi):
            dkt, dvt = carry
            s, dp = prods
            lse, delta = l_rep[rows_of(qi), :], d_rep[rows_of(qi), :]
            ps, dss = [], []
            for c in range(_TQ // LANES):
                strip = slice(LANES * c, LANES * (c + 1))
                p = jnp.exp2(s[:, strip] - lse)
                ps.append(p.astype(BF16))
                dss.append((p * (dp[:, strip] - delta) * _MLA_SCALE).astype(BF16))
            p_b, ds_b = jnp.concatenate(ps, axis=1), jnp.concatenate(dss, axis=1)
            dvt = dvt + _dot(dot_ref[0, qi], p_b)
            dkt = dkt + _dot(qt_ref[0, qi], ds_b)
            dq_ref[rows_of(qi), :] += _dot(ds_b, kv)
            return dkt, dvt

        def pair(i, carry):
            qa = kj + 1 + 2 * i
            pa, pb = products(qa), products(qa + 1)
            return update(update(carry, pa, qa), pb, qa + 1)

        init = (jnp.zeros((HEAD_PAD, _TQ), F32), jnp.zeros((V_DIM_B, _TQ), F32))
        carry = update(init, products(kj, True), kj)
        pairs = (nb - 1 - kj) // 2
        carry = lax.fori_loop(0, pairs, pair, carry)
        dkt, dvt = lax.fori_loop(kj + 1 + 2 * pairs, nb, lambda qi, cr: update(cr, products(qi), qi), carry)
        dkt_ref[0, 0] = dkt
        dvt_ref[0, 0] = dvt

        @pl.when((pl.program_id(0) == N_HEADS_B - 1) & (step == nb - 1))
        def _():
            _scatter_wait(gp_ref, land_ref, send_sems, recv_sems)

    head4 = lambda d: pl.BlockSpec((1, nb, d, _TQ), lambda h, s: (h, 0, 0, 0))
    blk4 = lambda d: pl.BlockSpec((1, 1, d, _TQ), lambda h, s: (h, nb - 1 - s, 0, 0))
    head3 = lambda d: pl.BlockSpec((1, t, d), lambda h, kj: (h, 0, 0))
    per_head = pl.BlockSpec((t, HEAD_PAD), lambda h, kj: (0, h))
    return pl.pallas_call(
        body, name="mla_bwd", grid=(N_HEADS_B, nb),
        in_specs=[pl.BlockSpec((_TQ, HEAD_PAD), lambda h, s: (nb - 1 - s, h)), blk4(HEAD_PAD), blk4(V_DIM_B),
                  per_head, head4(HEAD_PAD), head3(V_DIM_B), head4(V_DIM_B), head4(1), head4(1), _HBM],
        out_specs=[per_head, blk4(HEAD_PAD), blk4(V_DIM_B), _HBM],
        out_shape=[jax.ShapeDtypeStruct((t, MLA_W), F32), jax.ShapeDtypeStruct((N_HEADS_B, nb, HEAD_PAD, _TQ), F32),
                   jax.ShapeDtypeStruct((N_HEADS_B, nb, V_DIM_B, _TQ), F32),
                   jax.ShapeDtypeStruct((3,) + gp.shape[1:], gp.dtype)],
        scratch_shapes=[pltpu.VMEM((t, LANES), F32), pltpu.VMEM((t, LANES), F32),
                        pltpu.SemaphoreType.DMA((3,)), pltpu.SemaphoreType.DMA((3,))],
        compiler_params=_params(("arbitrary", "arbitrary")),
    )(k, kt, vt, q, qt, d_out, d_out_t, lse, delta, gp)


def _mla_prep_bwd(dq, dkt, dvt, proj, posc, freq, qan, kvan, wq, wk, wv, swap_src):
    t = dq.shape[0]
    tm = _TQ

    def body(dq_ref, dkt_ref, dvt_ref, cq_ref, ckv_ref, pos_ref, f_ref, qan_ref, kvan_ref, wq_ref, wk_ref, wv_ref, src_ref,
             dcq_ref, dckv_ref, dkr_ref, dwq_ref, dwk_ref, dwv_ref, dqan_ref, dkvan_ref, got_ref, send_sem, recv_sem):
        swap = _sibling_copy(src_ref, got_ref, send_sem, recv_sem)

        @pl.when(pl.program_id(0) == 0)
        def _():
            swap.start()
            for r in (dwq_ref, dwk_ref, dwv_ref, dqan_ref, dkvan_ref):
                r[...] = jnp.zeros(r.shape, F32)

        cq = cq_ref[...]
        rq = _rms(cq)
        nq_ = cq * rq
        cqn = (nq_ * qan_ref[...]).astype(BF16)
        ckv = ckv_ref[...]
        rkv = _rms(ckv)
        nkv = ckv * rkv
        ckvn = (nkv * kvan_ref[...]).astype(BF16)
        c, s, lo, hi = _rope_coeffs(pos_ref[...], f_ref[...])
        dkr = jnp.zeros((tm, LANES), F32)
        dqb, dkb = [], []
        for h in range(N_HEADS_B):
            dqb.append(_unrope(dq_ref[:, HEAD_PAD * h:HEAD_PAD * (h + 1)], c, s, lo, hi).astype(BF16))
            dk_h = dkt_ref[h, 0].T
            dkr = dkr + dk_h
            dkb.append(dk_h.astype(BF16))
        dqb, dkb = jnp.concatenate(dqb, axis=1), jnp.concatenate(dkb, axis=1)
        dkr_ref[...] = jnp.where(lo | hi, _unrope(dkr, c, s, lo, hi), 0.0).astype(BF16)
        dvb = dvt_ref[...].reshape(N_HEADS_B * V_DIM_B, tm).T.astype(BF16)
        dwq_ref[...] += _dot_tn(cqn, dqb)
        dwk_ref[...] += _dot_tn(ckvn, dkb)
        dwv_ref[...] += _dot_tn(ckvn, dvb)
        dcqn = _dot_nt(dqb, wq_ref[...])
        dckvn = _dot_nt(dkb, wk_ref[...]) + _dot_nt(dvb, wv_ref[...])
        dcq, dqan = _norm_bwd(dcqn, nq_, rq, qan_ref[...])
        dckv, dkvan = _norm_bwd(dckvn, nkv, rkv, kvan_ref[...])
        dcq_ref[...] = dcq.astype(BF16)
        dckv_ref[...] = dckv.astype(BF16)
        dqan_ref[...] += dqan
        dkvan_ref[...] += dkvan

        @pl.when(pl.program_id(0) == t // tm - 1)
        def _():
            swap.wait_recv()
            swap.wait_send()

    row = lambda i: (i, 0)
    vw = N_HEADS_B * V_DIM_B
    return pl.pallas_call(
        body, name="mla_prep_bwd", grid=(t // tm,),
        in_specs=[pl.BlockSpec((tm, MLA_W), row), pl.BlockSpec((N_HEADS_B, 1, HEAD_PAD, tm), lambda i: (0, i, 0, 0)),
                  pl.BlockSpec((N_HEADS_B, 1, V_DIM_B, tm), lambda i: (0, i, 0, 0)),
                  pl.BlockSpec((tm, Q_LORA), lambda i: (i, _CQ_BLK)),
                  pl.BlockSpec((tm, LANES), lambda i: (i, _CKV_BLK)),
                  pl.BlockSpec((tm, 1), row), _full((1, LANES)), _full((1, Q_LORA)), _full((1, KV_LORA)),
                  _full((Q_LORA, MLA_W)), _full((KV_LORA, MLA_W)), _full((KV_LORA, vw)), _HBM],
        out_specs=[pl.BlockSpec((tm, Q_LORA), row), pl.BlockSpec((tm, LANES), row), pl.BlockSpec((tm, LANES), row),
                   _full((Q_LORA, MLA_W)), _full((KV_LORA, MLA_W)), _full((KV_LORA, vw)),
                   _full((1, Q_LORA)), _full((1, KV_LORA)), _HBM],
        out_shape=[jax.ShapeDtypeStruct((t, Q_LORA), BF16), jax.ShapeDtypeStruct((t, LANES), BF16),
                   jax.ShapeDtypeStruct((t, LANES), BF16),
                   jax.ShapeDtypeStruct((Q_LORA, MLA_W), F32), jax.ShapeDtypeStruct((KV_LORA, MLA_W), F32),
                   jax.ShapeDtypeStruct((KV_LORA, vw), F32),
                   jax.ShapeDtypeStruct((1, Q_LORA), F32), jax.ShapeDtypeStruct((1, KV_LORA), F32),
                   jax.ShapeDtypeStruct(swap_src.shape, swap_src.dtype)],
        scratch_shapes=[pltpu.SemaphoreType.DMA(()), pltpu.SemaphoreType.DMA(())],
        compiler_params=_params(("arbitrary",)),
    )(dq, dkt, dvt, proj, proj, posc, freq, qan, kvan, wq, wk, wv, swap_src)


def _swa_bwd(proj, d_out, lse, delta, posc, posr, sinks):
    t = proj.shape[0]
    per = _SWA_PER_STEP
    span = per * BLOCK
    steps = t // span

    def body(q_ref, kc_ref, kp_ref, vc_ref, vp_ref, do_ref, l_ref, d_ref, pq_ref, pc_ref, pp_ref, sink_ref,
             dq_ref, dk_ref, dv_ref, ds_ref, dkb_s, dvb_s, dk_keep, dv_keep):
        n = pl.program_id(0)

        @pl.when(n == 0)
        def _():
            ds_ref[...] = jnp.zeros(ds_ref.shape, F32)
            dk_keep[...] = jnp.zeros(dk_keep.shape, F32)
            dv_keep[...] = jnp.zeros(dv_keep.shape, F32)

        @pl.when(n < steps)
        def _():
            k_all = jnp.concatenate([kp_ref[...], kc_ref[...]], axis=0)
            v_all = jnp.concatenate([vp_ref[...], vc_ref[...]], axis=0)
            pos_all = jnp.concatenate([pp_ref[...], pc_ref[...]], axis=0)
            ki = lax.broadcasted_iota(jnp.int32, (2 * BLOCK, BLOCK), 0)
            qi = lax.broadcasted_iota(jnp.int32, (2 * BLOCK, BLOCK), 1)
            window = (ki > qi) & (ki <= qi + WINDOW)
            lane = lax.broadcasted_iota(jnp.int32, (1, LANES), 1)
            dsink = jnp.zeros((1, LANES), F32)
            for sub in range(per):
                band = slice(BLOCK * sub, BLOCK * (sub + 2))
                own = slice(BLOCK * sub, BLOCK * (sub + 1))
                kb, vb = k_all[band], v_all[band]
                dist = jnp.abs(pos_all[band] - pq_ref[:, own])
                valid = window & ((n > 0) | (ki >= BLOCK)) if sub == 0 else window
                qv, dov = q_ref[own, :], do_ref[own, :]
                q_t, do_t, kb_t = qv.T, dov.T, kb.T
                dq_t = []
                for kh in range(N_KV_A):
                    heads = range(_GROUP_A * kh, _GROUP_A * (kh + 1))
                    st_g = _dot(_head_cols(kb, kh).astype(BF16), _group_t(q_t, kh))
                    dpt_g = _dot(_head_cols(vb, kh).astype(BF16), _group_t(do_t, kh))
                    pts, dsts = [], []
                    for j, h in enumerate(heads):
                        st = _swa_scores_t(st_g, j, h, dist, valid)
                        l_h, d_h = l_ref[h:h + 1, own], d_ref[h:h + 1, own]
                        pt = jnp.exp2(st - l_h)
                        p_sink = jnp.exp2(sink_ref[0:1, h:h + 1] * _LOG2E - l_h)
                        dsink = dsink + jnp.where(lane == h, jnp.sum(-p_sink * d_h, axis=1, keepdims=True), 0.0)
                        dst = pt * (dpt_g[:, BLOCK * j:BLOCK * (j + 1)] - d_h) * _SWA_SCALE
                        pts.append(pt.astype(BF16))
                        dsts.append(dst.astype(BF16))
                    pt_g, dst_g = jnp.concatenate(pts, axis=1), jnp.concatenate(dsts, axis=1)
                    q_g = jnp.concatenate([_head_cols(qv, h) for h in heads], axis=0).astype(BF16)
                    do_g = jnp.concatenate([_head_cols(dov, h) for h in heads], axis=0).astype(BF16)
                    dkb_s[sub, :, HEAD_DIM_A * kh:HEAD_DIM_A * (kh + 1)] = _dot(dst_g, q_g)
                    dvb_s[sub, :, HEAD_DIM_A * kh:HEAD_DIM_A * (kh + 1)] = _dot(pt_g, do_g)
                    dq_g = _dot(_head_rows(kb_t, kh).astype(BF16), dst_g)
                    dq_t.extend(dq_g[:, BLOCK * j:BLOCK * (j + 1)] for j in range(_GROUP_A))
                dq_ref[own, :] = jnp.concatenate(dq_t, axis=0).T
            ds_ref[...] += dsink
            for keep, out, parts in ((dk_keep, dk_ref, dkb_s), (dv_keep, dv_ref, dvb_s)):
                out[0:span - BLOCK, :] = keep[0:span - BLOCK, :]
                out[span - BLOCK:span, :] = keep[span - BLOCK:span, :] + parts[0, 0:BLOCK, :]
                for s in range(per - 1):
                    keep[BLOCK * s:BLOCK * (s + 1), :] = parts[s, BLOCK:2 * BLOCK, :] + parts[s + 1, 0:BLOCK, :]
                keep[span - BLOCK:span, :] = parts[per - 1, BLOCK:2 * BLOCK, :]

        @pl.when(n == steps)
        def _():
            dk_ref[...] = dk_keep[...]
            dv_ref[...] = dv_keep[...]

    last = steps - 1
    cur = lambda n: (jnp.minimum(n, last), 0)
    cur_t = lambda n: (0, jnp.minimum(n, last))
    prv = lambda n: jnp.maximum(per * jnp.minimum(n, last) - 1, 0)
    out_prev = lambda n: (jnp.maximum(n - 1, 0), 0)
    return pl.pallas_call(
        body, name="swa_bwd", grid=(steps + 1,),
        in_specs=[pl.BlockSpec((span, WIDTH_A), lambda n: (jnp.minimum(n, last), _QA_BLK)),
                  pl.BlockSpec((span, LANES), lambda n: (jnp.minimum(n, last), _KA_BLK)),
                  pl.BlockSpec((BLOCK, LANES), lambda n: (prv(n), _KA_BLK)),
                  pl.BlockSpec((span, LANES), lambda n: (jnp.minimum(n, last), _VA_BLK)),
                  pl.BlockSpec((BLOCK, LANES), lambda n: (prv(n), _VA_BLK)),
                  pl.BlockSpec((span, WIDTH_A), cur), pl.BlockSpec((N_HEADS_A, span), cur_t),
                  pl.BlockSpec((N_HEADS_A, span), cur_t), pl.BlockSpec((1, span), cur_t),
                  pl.BlockSpec((span, 1), cur), pl.BlockSpec((BLOCK, 1), lambda n: (prv(n), 0)),
                  _full((1, N_HEADS_A))],
        out_specs=[pl.BlockSpec((span, WIDTH_A), cur), pl.BlockSpec((span, LANES), out_prev),
                   pl.BlockSpec((span, LANES), out_prev), _full((1, LANES))],
        out_shape=[jax.ShapeDtypeStruct((t, WIDTH_A), F32), jax.ShapeDtypeStruct((t, LANES), F32),
                   jax.ShapeDtypeStruct((t, LANES), F32), jax.ShapeDtypeStruct((1, LANES), F32)],
        scratch_shapes=[pltpu.VMEM((per, 2 * BLOCK, LANES), F32), pltpu.VMEM((per, 2 * BLOCK, LANES), F32),
                        pltpu.VMEM((span, LANES), F32), pltpu.VMEM((span, LANES), F32)],
        compiler_params=_params(("arbitrary",)),
    )(proj, proj, proj, proj, proj, d_out, lse, delta, posr, posc, posc, sinks)


def _in_bwd(dproj, w_in_t, x, dx1, g1, gp):
    t = x.shape[0]
    tm = 512
    steps = t // tm

    def body(dp_ref, w_ref, x_ref, dx1_ref, g_ref, gp_ref, dx_ref, dg_ref, land_ref, send_sems, recv_sems):
        i = pl.program_id(0)

        @pl.when(i == 0)
        def _():
            dg_ref[...] = jnp.zeros(dg_ref.shape, F32)
            _scatter_start(gp_ref, land_ref, send_sems, recv_sems)

        for rows in _row_halves(tm):
            dh = _dot(dp_ref[rows, :], w_ref[...])
            xv = x_ref[rows, :]
            r = _rms(xv)
            dx, dg = _norm_bwd(dh, xv * r, r, g_ref[...])
            dx_ref[rows, :] = dx1_ref[rows, :] + dx
            dg_ref[...] += dg

        @pl.when(i == steps - 1)
        def _():
            _scatter_wait(gp_ref, land_ref, send_sems, recv_sems)

    row = lambda i: (i, 0)
    blk = pl.BlockSpec((tm, D_MODEL), row)
    return pl.pallas_call(
        body, name="in_bwd", grid=(steps,),
        in_specs=[pl.BlockSpec((tm, D_IN_PAD), row), _full((D_IN_PAD, D_MODEL)), blk, blk, _full((1, D_MODEL)), _HBM],
        out_specs=[blk, _full((1, D_MODEL)), _HBM],
        out_shape=[jax.ShapeDtypeStruct((t, D_MODEL), F32), jax.ShapeDtypeStruct((1, D_MODEL), F32),
                   jax.ShapeDtypeStruct((3,) + gp.shape[1:], gp.dtype)],
        scratch_shapes=[pltpu.SemaphoreType.DMA((3,)), pltpu.SemaphoreType.DMA((3,))],
        compiler_params=_params(("arbitrary",)),
    )(dproj, w_in_t, x, dx1, g1, gp)


def _adamw_store(w, g, m, v, out_refs):
    g_out, d_out, m_out, v_out = out_refs
    m_new = ADAM_B1 * m + (1.0 - ADAM_B1) * g
    v_new = ADAM_B2 * v + (1.0 - ADAM_B2) * jnp.square(g)
    m_hat = m_new / (1.0 - ADAM_B1 ** ADAM_STEP)
    v_hat = v_new / (1.0 - ADAM_B2 ** ADAM_STEP)
    g_out[...] = g
    d_out[...] = -ADAM_LR * (m_hat / (jnp.sqrt(v_hat) + ADAM_EPS) + ADAM_WD * w)
    m_out[...] = m_new
    v_out[...] = v_new


_SMALL_SLOTS = {"pre_norm_mix": (0, 0, D_MODEL), "post_norm_mix": (1, 0, D_MODEL), "pre_norm_mlp": (2, 0, D_MODEL),
                "post_norm_mlp": (3, 0, D_MODEL), "q_a_norm": (4, 0, Q_LORA), "kv_a_norm": (4, Q_LORA, KV_LORA),
                "sinks": (4, Q_LORA + KV_LORA, N_HEADS_A)}
_LOSS_ROW = 5


def _adamw_small(red, w, m, v):
    names = tuple(_SMALL_SLOTS)
    n = len(names)

    def body(*refs):
        red_ref, ws, ms, vs, outs = refs[0], refs[1:1 + n], refs[1 + n:1 + 2 * n], refs[1 + 2 * n:1 + 3 * n], refs[1 + 3 * n:]
        for k, name in enumerate(names):
            row, lane, width = _SMALL_SLOTS[name]
            g = red_ref[row:row + 1, lane:lane + width]
            _adamw_store(ws[k][...], g, ms[k][...], vs[k][...], outs[4 * k:4 * k + 4])

    vmem = pl.BlockSpec(memory_space=pltpu.VMEM)
    res = pl.pallas_call(
        body, name="adamw_small", in_specs=[vmem] * (1 + 3 * n), out_specs=[vmem] * (4 * n),
        out_shape=[jax.ShapeDtypeStruct(w[name].shape, F32) for name in names for _ in range(4)],
    )(red, *[w[k] for k in names], *[m[k] for k in names], *[v[k] for k in names])
    return {name: res[4 * k:4 * k + 4] for k, name in enumerate(names)}


_ADAMW_RIDERS = ("w_up", "w_down", "w_out")


def _dw_in_adamw(dproj, h, g_parts, w, m, v):
    t, rows_out = dproj.shape
    tm, tk = rows_out // 2, min(1024, t)
    nk = t // tk
    steps = 2 * nk
    names = _ADAMW_RIDERS
    n = len(names)

    def body(a_ref, b_ref, *rest):
        g1s, g2s, ws, ms, vs = (rest[n * j:n * (j + 1)] for j in range(5))
        o_ref, outs = rest[5 * n], rest[5 * n + 1:]

        @pl.when(pl.program_id(2) == 0)
        def _():
            o_ref[...] = jnp.zeros(o_ref.shape, F32)

        o_ref[...] += _dot_tn(a_ref[...], b_ref[...])
        for j in range(n):
            _adamw_store(ws[j][...], g1s[j][...] + g2s[j][...], ms[j][...], vs[j][...], outs[4 * j:4 * j + 4])

    def rider_spec(name, packed):
        br = SHARD_SHAPES[name][0] // steps
        first = _row_offset(GROUP_B, name) // br if packed else 0
        return pl.BlockSpec((br, D_MODEL), lambda i, j, k: (first + i * nk + k, 0))

    g_specs = [rider_spec(name, True) for name in names]
    own_specs = [rider_spec(name, False) for name in names]
    res = pl.pallas_call(
        body, name="dw_in", grid=(2, 1, nk),
        in_specs=[pl.BlockSpec((tk, tm), lambda i, j, k: (k, i)), pl.BlockSpec((tk, D_MODEL), lambda i, j, k: (k, 0))]
        + g_specs * 2 + own_specs * 3,
        out_specs=[pl.BlockSpec((tm, D_MODEL), lambda i, j, k: (i, 0))] + [s for s in own_specs for _ in range(4)],
        out_shape=[jax.ShapeDtypeStruct((rows_out, D_MODEL), F32)]
        + [jax.ShapeDtypeStruct(SHARD_SHAPES[name], F32) for name in names for _ in range(4)],
        compiler_params=_params(("arbitrary", "arbitrary", "arbitrary")),
    )(dproj, h, *[g_parts[0]] * n, *[g_parts[1]] * n, *[w[k] for k in names], *[m[k] for k in names],
      *[v[k] for k in names])
    return res[0], {name: res[1 + 4 * j:5 + 4 * j] for j, name in enumerate(names)}


def _adamw(w, g_parts, m, v, name, block, g_row_off=0):
    r, c = w.shape
    br, bc = block
    ng = len(g_parts)

    def body(*refs):
        w_ref, g_refs, m_ref, v_ref = refs[0], refs[1:1 + ng], refs[1 + ng], refs[2 + ng]
        g = g_refs[0][...]
        for gr in g_refs[1:]:
            g = g + gr[...]
        _adamw_store(w_ref[...], g, m_ref[...], v_ref[...], refs[3 + ng:])

    assert g_row_off % br == 0 and r % br == 0 and c % bc == 0
    blk = pl.BlockSpec(block, lambda i, j: (i, j))
    g_blk = pl.BlockSpec(block, lambda i, j: (i + g_row_off // br, j))
    return pl.pallas_call(
        body, name=name, grid=(r // br, c // bc),
        in_specs=[blk] + [g_blk] * ng + [blk, blk], out_specs=[blk] * 4,
        out_shape=[jax.ShapeDtypeStruct((r, c), F32)] * 4,
        compiler_params=_params(("parallel", "parallel")),
    )(w, *g_parts, m, v)


_HBM = pl.BlockSpec(memory_space=pltpu.HBM)


def _other_chips(x, y):
    return ((1 - x, y), (x, 1 - y), (1 - x, 1 - y))


def _gather_copies(src, out, send_sems, recv_sems, local_sem):
    x, y, c = lax.axis_index("x"), lax.axis_index("y"), lax.axis_index("c")
    me = 2 * x + y
    local = pltpu.make_async_copy(src, out.at[me], local_sem)

    def copies(arriving):
        return [pltpu.make_async_remote_copy(src_ref=src, dst_ref=out.at[2 * px + py if arriving else me],
                                             send_sem=send_sems.at[j], recv_sem=recv_sems.at[j], device_id=(px, py, c),
                                             device_id_type=MESH)
                for j, (px, py) in enumerate(_other_chips(x, y))]

    return local, copies


def _gather_start(src, out, send_sems, recv_sems, local_sem):
    local, copies = _gather_copies(src, out, send_sems, recv_sems, local_sem)
    local.start()
    for cp in copies(False):
        cp.start()


def _gather_wait(src, out, send_sems, recv_sems, local_sem):
    local, copies = _gather_copies(src, out, send_sems, recv_sems, local_sem)
    for cp in copies(True):
        cp.wait_recv()
    for cp in copies(False):
        cp.wait_send()
    local.wait()


def _scatter_copies(src, land, send_sems, recv_sems):
    x, y, c = lax.axis_index("x"), lax.axis_index("y"), lax.axis_index("c")
    return [pltpu.make_async_remote_copy(src_ref=src.at[2 * px + py], dst_ref=land.at[j], send_sem=send_sems.at[j],
                                         recv_sem=recv_sems.at[j], device_id=(px, py, c), device_id_type=MESH)
            for j, (px, py) in enumerate(_other_chips(x, y))]


def _scatter_start(src, land, send_sems, recv_sems):
    for cp in _scatter_copies(src, land, send_sems, recv_sems):
        cp.start()


def _scatter_wait(src, land, send_sems, recv_sems):
    copies = _scatter_copies(src, land, send_sems, recv_sems)
    for cp in copies:
        cp.wait_recv()
    for cp in copies:
        cp.wait_send()


def _all_gather_chips(packed):
    r = packed.shape[0]
    half = r // 2

    def body(src, out, ici_send, ici_recv, d2d_send, d2d_recv, local_sem):
        x, y, c = lax.axis_index("x"), lax.axis_index("y"), lax.axis_index("c")
        me = 2 * x + y
        mine = pl.ds(pl.multiple_of(c * half, 16), half)
        theirs = pl.ds(pl.multiple_of((1 - c) * half, 16), half)
        chips = _other_chips(x, y)
        local = pltpu.make_async_copy(src, out.at[me], local_sem)
        local.start()
        sends = [pltpu.make_async_remote_copy(src_ref=src.at[mine], dst_ref=out.at[me, mine], send_sem=ici_send.at[j],
                                              recv_sem=ici_recv.at[j], device_id=(px, py, c), device_id_type=MESH)
                 for j, (px, py) in enumerate(chips)]
        for cp in sends:
            cp.start()
        passed = []
        for j, (px, py) in enumerate(chips):
            block = 2 * px + py
            pltpu.make_async_remote_copy(src_ref=src.at[mine], dst_ref=out.at[block, mine], send_sem=ici_send.at[j],
                                         recv_sem=ici_recv.at[j], device_id=(px, py, c), device_id_type=MESH).wait_recv()
            cp = pltpu.make_async_remote_copy(src_ref=out.at[block, mine], dst_ref=out.at[block, mine],
                                              send_sem=d2d_send.at[j], recv_sem=d2d_recv.at[j],
                                              device_id=(x, y, 1 - c), device_id_type=MESH)
            cp.start()
            passed.append(cp)
        for j, (px, py) in enumerate(chips):
            block = 2 * px + py
            pltpu.make_async_remote_copy(src_ref=out.at[block, theirs], dst_ref=out.at[block, theirs],
                                         send_sem=d2d_send.at[j], recv_sem=d2d_recv.at[j],
                                         device_id=(x, y, 1 - c), device_id_type=MESH).wait_recv()
        for cp in sends + passed:
            cp.wait_send()
        local.wait()

    sems = pltpu.SemaphoreType.DMA((3,))
    return pl.pallas_call(
        body, name="ag_weights", in_specs=[_HBM], out_specs=_HBM,
        out_shape=jax.ShapeDtypeStruct((N_CHIPS,) + packed.shape, packed.dtype),
        scratch_shapes=[sems, sems, sems, sems, pltpu.SemaphoreType.DMA(())],
    )(packed)


def _sum4(gp, land, chip, name):
    _, r, w = gp.shape
    tr = 256 if r % 256 == 0 else 128

    def body(chip_ref, o_ref, l_ref, s_ref):
        s_ref[...] = ((o_ref[0] + l_ref[0].astype(F32)) + l_ref[1].astype(F32)) + l_ref[2].astype(F32)

    return pl.pallas_call(
        body, name=name,
        grid_spec=pltpu.PrefetchScalarGridSpec(
            num_scalar_prefetch=1, grid=(r // tr,),
            in_specs=[pl.BlockSpec((1, tr, w), lambda i, chip_ref: (chip_ref[0], i, 0)),
                      pl.BlockSpec((3, tr, w), lambda i, chip_ref: (0, i, 0))],
            out_specs=pl.BlockSpec((tr, w), lambda i, chip_ref: (i, 0))),
        out_shape=jax.ShapeDtypeStruct((r, w), F32),
        compiler_params=_params(("parallel",)),
    )(chip, gp, land)


def _sibling_copy(src, got, send_sem, recv_sem):
    x, y, c = lax.axis_index("x"), lax.axis_index("y"), lax.axis_index("c")
    return pltpu.make_async_remote_copy(src_ref=src, dst_ref=got, send_sem=send_sem, recv_sem=recv_sem,
                                        device_id=(x, y, 1 - c), device_id_type=MESH)


def _swap_sibling(s, name):
    def body(src, got, send_sem, recv_sem):
        cp = _sibling_copy(src, got, send_sem, recv_sem)
        cp.start()
        cp.wait_recv()
        cp.wait_send()

    return pl.pallas_call(
        body, name=name, in_specs=[_HBM], out_specs=_HBM,
        out_shape=jax.ShapeDtypeStruct(s.shape, s.dtype),
        scratch_shapes=[pltpu.SemaphoreType.DMA(()), pltpu.SemaphoreType.DMA(())],
    )(s)


def _all_reduce_small(dsmall, loss):
    n_dev = 8
    names = tuple(_SMALL_SLOTS)
    shape = (8, D_MODEL)

    def body(*refs):
        parts, loss_ref = refs[:len(names)], refs[len(names)]
        out, src, gath, send_sems, recv_sems = refs[len(names) + 1:]
        x, y, c = lax.axis_index("x"), lax.axis_index("y"), lax.axis_index("c")
        me = 4 * x + 2 * y + c
        src[...] = jnp.zeros(shape, F32)
        for name, part in zip(names, parts):
            row, lane, _ = _SMALL_SLOTS[name]
            src[row:row + 1, lane:lane + part.shape[1]] = part[...]
        src[_LOSS_ROW:_LOSS_ROW + 1, 0:LANES] = loss_ref[...]
        gath[me] = src[...]
        peers = []
        for k in range(1, n_dev):
            px = 1 - x if (k >> 2) & 1 else x
            py = 1 - y if (k >> 1) & 1 else y
            pc = 1 - c if k & 1 else c
            peers.append((px, py, pc))
        sends = []
        for j, peer in enumerate(peers):
            cp = pltpu.make_async_remote_copy(src_ref=src, dst_ref=gath.at[me], send_sem=send_sems.at[j],
                                              recv_sem=recv_sems.at[j], device_id=peer, device_id_type=MESH)
            cp.start()
            sends.append(cp)
        for j, (px, py, pc) in enumerate(peers):
            pltpu.make_async_remote_copy(src_ref=src, dst_ref=gath.at[4 * px + 2 * py + pc], send_sem=send_sems.at[j],
                                         recv_sem=recv_sems.at[j], device_id=(px, py, pc), device_id_type=MESH).wait_recv()
        for cp in sends:
            cp.wait_send()
        acc = gath[0]
        for d in range(1, n_dev):
            acc = acc + gath[d]
        out[...] = acc

    vmem = pl.BlockSpec(memory_space=pltpu.VMEM)
    return pl.pallas_call(
        body, name="ar_small", in_specs=[vmem] * (len(names) + 1), out_specs=vmem,
        out_shape=jax.ShapeDtypeStruct(shape, F32),
        scratch_shapes=[pltpu.VMEM(shape, F32), pltpu.VMEM((n_dev,) + shape, F32),
                        pltpu.SemaphoreType.DMA((n_dev - 1,)), pltpu.SemaphoreType.DMA((n_dev - 1,))],
    )(*[dsmall[k] for k in names], loss)


_W_IN_ROWS = SHARD_SHAPES["w_in"][1]
_KR_ROW = 3200
_KR_PAD_ROW = _KR_BLK * LANES + QK_NOPE


def _shard_rows(name, a):
    return jnp.transpose(a) if name == "w_in" else a.reshape(PACK_ROWS[name], D_MODEL)


def _pack(group, shards, dtype):
    parts = [_shard_rows(n, shards[n]).astype(dtype) for n in group]
    pad = -sum(PACK_ROWS[n] for n in group) % LANES
    if pad:
        parts.append(jnp.zeros((pad, D_MODEL), dtype))
    return jnp.concatenate(parts, axis=0)


def _col_sharded_full(g, name, group):
    r, c = SHARD_SHAPES[name]
    off = _row_offset(group, name)
    blocks = g[:, off:off + PACK_ROWS[name]].reshape(N_CHIPS, r, c)
    return jnp.transpose(blocks, (1, 0, 2)).reshape(r, N_CHIPS * c)


def _col_sharded_blocks(d, name):
    r, c = SHARD_SHAPES[name]
    return jnp.transpose(d.reshape(r, N_CHIPS, c), (1, 0, 2)).reshape(N_CHIPS, PACK_ROWS[name], D_MODEL)


def _weights_a(g):
    dt = g.dtype
    w_in_t = g[:, :_W_IN_ROWS].reshape(N_CHIPS * _W_IN_ROWS, D_MODEL)
    z = lambda n: jnp.zeros((n, D_MODEL), dt)
    w_in_t = jnp.concatenate([w_in_t[:_KR_ROW], z(_KR_PAD_ROW - _KR_ROW), w_in_t[_KR_ROW:],
                              z(D_IN_PAD - _KR_PAD_ROW - QK_ROPE)], axis=0)
    wq = _col_sharded_full(g, "w_q_b", GROUP_A).reshape(Q_LORA, N_HEADS_B, Q_HEAD_B)
    wq_p = jnp.concatenate([wq, jnp.zeros((Q_LORA, N_HEADS_B, HEAD_PAD - Q_HEAD_B), dt)], axis=2).reshape(Q_LORA, MLA_W)
    wkv = _col_sharded_full(g, "w_kv_b", GROUP_A).reshape(KV_LORA, N_HEADS_B, QK_NOPE + V_DIM_B)
    zk = jnp.zeros((KV_LORA, N_HEADS_B, HEAD_PAD - QK_NOPE), dt)
    wk_p = jnp.concatenate([wkv[:, :, :QK_NOPE], zk], axis=2).reshape(KV_LORA, MLA_W)
    wv = wkv[:, :, QK_NOPE:].reshape(KV_LORA, N_HEADS_B * V_DIM_B)
    return dict(w_in=w_in_t, wq=wq_p, wk=wk_p, wv=wv, wv_t=jnp.transpose(wv))


def _grad_blocks_a(dw_in_t, dwq_p, dwk_p, dwv):
    dw_in = jnp.concatenate([dw_in_t[:_KR_ROW], dw_in_t[_KR_PAD_ROW:_KR_PAD_ROW + QK_ROPE]], axis=0)
    dwq = dwq_p.reshape(Q_LORA, N_HEADS_B, HEAD_PAD)[:, :, :Q_HEAD_B].reshape(Q_LORA, N_HEADS_B * Q_HEAD_B)
    dwk = dwk_p.reshape(KV_LORA, N_HEADS_B, HEAD_PAD)[:, :, :QK_NOPE]
    dwkv = jnp.concatenate([dwk, dwv.reshape(KV_LORA, N_HEADS_B, V_DIM_B)], axis=2)
    dwkv = dwkv.reshape(KV_LORA, N_HEADS_B * (QK_NOPE + V_DIM_B))
    pad = -sum(PACK_ROWS[n] for n in GROUP_A) % LANES
    return jnp.concatenate([dw_in.reshape(N_CHIPS, _W_IN_ROWS, D_MODEL), _col_sharded_blocks(dwq, "w_q_b"),
                            _col_sharded_blocks(dwkv, "w_kv_b"), jnp.zeros((N_CHIPS, pad, D_MODEL), F32)], axis=1)


def _rope_freq_lanes():
    freqs = ROPE_THETA ** (-jnp.arange(0, QK_ROPE, 2, dtype=F32) / QK_ROPE)
    return jnp.concatenate([jnp.zeros((QK_NOPE,), F32), freqs, freqs,
                            jnp.zeros((HEAD_PAD - Q_HEAD_B,), F32)]).reshape(1, LANES)


def _fwd_bwd(x, positions, target, w, m, v):
    t = x.shape[0]
    wa = _weights_a(_all_gather_chips(_pack(GROUP_A, w, BF16)))
    posr = positions.astype(F32).reshape(1, t)
    posc = posr.reshape(t, 1)
    freq = _rope_freq_lanes()
    g1, g2, g3, g4 = w["pre_norm_mix"], w["post_norm_mix"], w["pre_norm_mlp"], w["post_norm_mlp"]
    qan, kvan, sinks = w["q_a_norm"], w["kv_a_norm"], w["sinks"]

    h, proj = _proj_fwd(x, g1, wa["w_in"])
    out_a, lse_a = _swa_fwd(proj, posc, posr, sinks)
    qm, km, qt, kt, vt = _mla_prep_fwd(proj, posc, freq, qan, kvan, wa["wq"], wa["wk"], wa["wv_t"])
    out_bt, lse_b, wb = _mla_fwd(km, qt, vt, _pack(GROUP_B, w, BF16))
    w_oa, w_ob = _col_sharded_full(wb, "w_o_a", GROUP_B), _col_sharded_full(wb, "w_o_b", GROUP_B)
    merged, y, x1, h2 = _mix_out_fwd(out_a, out_bt, proj, x, w_oa, w_ob, wb, g2, g3)
    a = _up_fwd(h2, wb)
    dx2, dyd, dg4, loss = _down_fwd_loss(a, wb, x1, target, g4)

    gp_b = _dw_into_blocks(a, dyd, "w_down", 1024, _TK_DW)
    du = _down_bwd(dyd, wb, a)
    gp_b = _dw_into_blocks(h2, du, "w_up", 1024, _TK_DW, gp_b)
    dx1, dy, dg3, dg2 = _up_bwd(du, wb, x1, dx2, y, g3, g2)
    gp_b = _dw_into_blocks(merged, dy, "w_out", 1024, _TK_DW, gp_b)
    doa, dob, dga, dgb, d_out_a, d_out_b, d_out_bt, del_a, del_b = _mix_out_bwd(dy, out_a, out_bt, proj, w_oa, w_ob, wb)
    dw_oa = _matmul_tn(out_a, doa, "dw_o_a", 512, 1024)
    dw_ob = _dw_ob(out_bt, dob)
    small_b = jnp.concatenate([_col_sharded_blocks(dw_oa, "w_o_a"), _col_sharded_blocks(dw_ob, "w_o_b")], axis=1)
    gp_b = lax.dynamic_update_slice(gp_b, small_b, (0, _row_offset(GROUP_B, "w_o_a"), 0))
    dqm, dkm, dvm, land_b = _mla_bwd(qm, km, qt, kt, vt, d_out_b, d_out_bt, lse_b, del_b, gp_b)
    chip = (2 * lax.axis_index("x") + lax.axis_index("y")).astype(jnp.int32).reshape(1)
    part_b = _sum4(gp_b, land_b, chip, "rs_sum_b")
    dcq, dckv, dkr, dwq, dwk, dwv, dqan, dkvan, sib_b = _mla_prep_bwd(
        dqm, dkm, dvm, proj, posc, freq, qan, kvan, wa["wq"], wa["wk"], wa["wv"], part_b)
    dqa, dka, dva, dsinks = _swa_bwd(proj, d_out_a, lse_a, del_a, posc, posr, sinks)
    dproj = jnp.concatenate([dga, dgb, dqa.astype(BF16), dka.astype(BF16), dva.astype(BF16), dcq, dckv, dkr], axis=1)
    dw_in_t, updated = _dw_in_adamw(dproj, h, [part_b, sib_b], w, m, v)
    gp_a = _grad_blocks_a(dw_in_t, dwq, dwk, dwv)
    grad_x, dg1, land_a = _in_bwd(dproj, wa["w_in"], x, dx1, g1, gp_a.astype(BF16))

    part_a = _sum4(gp_a, land_a, chip, "rs_sum_a")
    reduced = {GROUP_A: [part_a, _swap_sibling(part_a, "rs_swap_a")], GROUP_B: [part_b, sib_b]}
    dsmall = dict(pre_norm_mix=dg1, post_norm_mix=dg2, pre_norm_mlp=dg3, post_norm_mlp=dg4,
                  q_a_norm=dqan, kv_a_norm=dkvan, sinks=dsinks)
    return loss, grad_x, reduced, dsmall, updated


def kernel(x, positions, pre_norm_mix, w_in, q_a_norm, w_q_b, kv_a_norm, w_kv_b, sinks, w_o_a, w_o_b, w_out, post_norm_mix, pre_norm_mlp, w_up, w_down, post_norm_mlp, loss_target, m_pre_norm_mix, m_w_in, m_q_a_norm, m_w_q_b, m_kv_a_norm, m_w_kv_b, m_sinks, m_w_o_a, m_w_o_b, m_w_out, m_post_norm_mix, m_pre_norm_mlp, m_w_up, m_w_down, m_post_norm_mlp, v_pre_norm_mix, v_w_in, v_q_a_norm, v_w_q_b, v_kv_a_norm, v_w_kv_b, v_sinks, v_w_o_a, v_w_o_b, v_w_out, v_post_norm_mix, v_pre_norm_mlp, v_w_up, v_w_down, v_post_norm_mlp):
    w = dict(pre_norm_mix=pre_norm_mix, w_in=w_in[0], q_a_norm=q_a_norm, w_q_b=w_q_b[0], kv_a_norm=kv_a_norm,
             w_kv_b=w_kv_b[0], sinks=sinks, w_o_a=w_o_a[0], w_o_b=w_o_b[0], w_out=w_out[0],
             post_norm_mix=post_norm_mix, pre_norm_mlp=pre_norm_mlp, w_up=w_up[0], w_down=w_down[0],
             post_norm_mlp=post_norm_mlp)
    m = dict(pre_norm_mix=m_pre_norm_mix, w_in=m_w_in[0], q_a_norm=m_q_a_norm, w_q_b=m_w_q_b[0],
             kv_a_norm=m_kv_a_norm, w_kv_b=m_w_kv_b[0], sinks=m_sinks, w_o_a=m_w_o_a[0], w_o_b=m_w_o_b[0],
             w_out=m_w_out[0], post_norm_mix=m_post_norm_mix, pre_norm_mlp=m_pre_norm_mlp, w_up=m_w_up[0],
             w_down=m_w_down[0], post_norm_mlp=m_post_norm_mlp)
    v = dict(pre_norm_mix=v_pre_norm_mix, w_in=v_w_in[0], q_a_norm=v_q_a_norm, w_q_b=v_w_q_b[0],
             kv_a_norm=v_kv_a_norm, w_kv_b=v_w_kv_b[0], sinks=v_sinks, w_o_a=v_w_o_a[0], w_o_b=v_w_o_b[0],
             w_out=v_w_out[0], post_norm_mix=v_post_norm_mix, pre_norm_mlp=v_pre_norm_mlp, w_up=v_w_up[0],
             w_down=v_w_down[0], post_norm_mlp=v_post_norm_mlp)

    loss, grad_x, reduced, dsmall, updated = _fwd_bwd(x[0], positions, loss_target[0], w, m, v)

    red = _all_reduce_small(dsmall, loss)
    small = _adamw_small(red, w, m, v)

    big = {}
    tr = jnp.transpose
    big["w_in"] = [tr(o)[None] for o in _adamw(tr(w["w_in"]), reduced[GROUP_A], tr(m["w_in"]), tr(v["w_in"]),
                                               "adamw_w_in", (_W_IN_ROWS, 256))]
    for n in _ADAMW_RIDERS:
        big[n] = [o[None] for o in updated[n]]
    for group, names in ((GROUP_A, ("w_q_b", "w_kv_b")), (GROUP_B, ("w_o_a", "w_o_b"))):
        for n in names:
            off = _row_offset(group, n)
            g_parts = [p[off:off + PACK_ROWS[n]].reshape(SHARD_SHAPES[n]) for p in reduced[group]]
            big[n] = [o[None] for o in _adamw(w[n], g_parts, m[n], v[n], "adamw_" + n, SHARD_SHAPES[n])]

    outs = [big[n][k] if n in big else small[n][k] for k in range(4) for n in WEIGHTS]
    return (red[_LOSS_ROW, 0], grad_x[None], *outs)
```

```python
import jax
import jax.numpy as jnp
from jax import lax
from jax.experimental import pallas as pl
from jax.experimental.pallas import tpu as pltpu

F32 = jnp.float32
BF16 = jnp.bfloat16
MESH = pl.DeviceIdType.MESH

D_MODEL = 1024
N_HEADS_A = 8
N_KV_A = 2
HEAD_DIM_A = 64
WINDOW = 128
BLOCK = 128
N_HEADS_B = 8
QK_NOPE = 64
QK_ROPE = 32
V_DIM_B = 64
Q_LORA = 256
KV_LORA = 128
ROPE_THETA = 10000.0
D_FF = 4 * D_MODEL
EPS = 1e-6
WIDTH_A = N_HEADS_A * HEAD_DIM_A
Q_HEAD_B = QK_NOPE + QK_ROPE
D_IN_PAD = 3328
HEAD_PAD = 128
MLA_W = N_HEADS_B * HEAD_PAD

ADAM_LR = 0.001
ADAM_B1 = 0.9
ADAM_B2 = 0.999
ADAM_EPS = 1e-08
ADAM_WD = 0.01
ADAM_STEP = 10

NEG = -1e30
N_CHIPS = 4
LANES = 128
VMEM_LIMIT = 56 * 1024 * 1024

SHARD_SHAPES = {"w_in": (1024, 808), "w_q_b": (256, 192), "w_kv_b": (128, 256), "w_o_a": (512, 256),
                "w_o_b": (512, 256), "w_out": (256, 1024), "w_up": (1024, 1024), "w_down": (1024, 1024)}
PACK_ROWS = {n: (s[0] * s[1]) // D_MODEL for n, s in SHARD_SHAPES.items()}
GROUP_A = ("w_in", "w_q_b", "w_kv_b")
GROUP_B = ("w_up", "w_down", "w_out", "w_o_a", "w_o_b")
WEIGHTS = ("pre_norm_mix", "w_in", "q_a_norm", "w_q_b", "kv_a_norm", "w_kv_b", "sinks", "w_o_a", "w_o_b", "w_out",
           "post_norm_mix", "pre_norm_mlp", "w_up", "w_down", "post_norm_mlp")


def _params(sem=None):
    return pltpu.CompilerParams(dimension_semantics=sem, vmem_limit_bytes=VMEM_LIMIT)


def _dot(a, b):
    return jnp.dot(a, b, preferred_element_type=F32)


def _dot_nt(a, b):
    return lax.dot_general(a, b, (((1,), (1,)), ((), ())), preferred_element_type=F32)


def _dot_tn(a, b):
    return lax.dot_general(a, b, (((0,), (0,)), ((), ())), preferred_element_type=F32)


def _rms(v):
    return lax.rsqrt(jnp.mean(v * v, axis=-1, keepdims=True) + EPS)


def _norm_bwd(dout, n, r, g):
    dn = dout * g
    dx = r * (dn - n * jnp.mean(dn * n, axis=-1, keepdims=True))
    return dx, jnp.sum(dout * n, axis=0, keepdims=True)


def _full(shape):
    return pl.BlockSpec(shape, lambda *_: (0,) * len(shape))


def _row_offset(group, name):
    return sum(PACK_ROWS[n] for n in group[:group.index(name)])


def _wb_spec(name):
    rows = PACK_ROWS[name]
    return pl.BlockSpec((N_CHIPS, rows, D_MODEL), lambda *_: (0, _row_offset(GROUP_B, name) // rows, 0))


def _proj_fwd(x, g1, w_in_t):
    t = x.shape[0]
    tm = 512

    def body(x_ref, g_ref, w_ref, h_ref, p_ref):
        for rows in _row_halves(tm):
            xv = x_ref[rows, :]
            h = ((xv * _rms(xv)) * g_ref[...]).astype(BF16)
            h_ref[rows, :] = h
            p_ref[rows, :] = _dot_nt(h, w_ref[...])

    return pl.pallas_call(
        body, name="proj_fwd", grid=(t // tm,),
        in_specs=[pl.BlockSpec((tm, D_MODEL), lambda i: (i, 0)), _full((1, D_MODEL)), _full((D_IN_PAD, D_MODEL))],
        out_specs=[pl.BlockSpec((tm, D_MODEL), lambda i: (i, 0)), pl.BlockSpec((tm, D_IN_PAD), lambda i: (i, 0))],
        out_shape=[jax.ShapeDtypeStruct((t, D_MODEL), BF16), jax.ShapeDtypeStruct((t, D_IN_PAD), F32)],
        compiler_params=_params(("parallel",)),
    )(x, g1, w_in_t)


_QA_BLK = 2048 // WIDTH_A
_KA_BLK = 2560 // LANES
_VA_BLK = 2688 // LANES
_CQ_BLK = 2816 // Q_LORA
_CKV_BLK = 3072 // LANES
_KR_BLK = 3200 // LANES


_GROUP_A = N_HEADS_A // N_KV_A
_SWA_SCALE = HEAD_DIM_A ** -0.5
_LOG2E = 1.4426950408889634


def _head_cols(v, h):
    return v[:, HEAD_DIM_A * h:HEAD_DIM_A * (h + 1)]


def _head_rows(v, h):
    return v[HEAD_DIM_A * h:HEAD_DIM_A * (h + 1), :]


def _swa_scores_t(st_g, j, h, dist, valid):
    slope = 2.0 ** (-8.0 * (h + 1) / N_HEADS_A)
    st = st_g[:, BLOCK * j:BLOCK * (j + 1)] * (_SWA_SCALE * _LOG2E) - (slope * _LOG2E) * dist
    return jnp.where(valid, st, NEG)


def _group_t(xt, kh):
    return jnp.concatenate([_head_rows(xt, _GROUP_A * kh + j) for j in range(_GROUP_A)], axis=1).astype(BF16)


_SWA_PER_STEP = 4


def _swa_fwd(proj, posc, posr, sinks):
    t = proj.shape[0]
    span = _SWA_PER_STEP * BLOCK

    def body(q_ref, kc_ref, kp_ref, vc_ref, vp_ref, pq_ref, pc_ref, pp_ref, sink_ref, o_ref, l_ref):
        n = pl.program_id(0)
        k_all = jnp.concatenate([kp_ref[...], kc_ref[...]], axis=0)
        v_all = jnp.concatenate([vp_ref[...], vc_ref[...]], axis=0)
        pos_all = jnp.concatenate([pp_ref[...], pc_ref[...]], axis=0)
        ki = lax.broadcasted_iota(jnp.int32, (2 * BLOCK, BLOCK), 0)
        qi = lax.broadcasted_iota(jnp.int32, (2 * BLOCK, BLOCK), 1)
        window = (ki > qi) & (ki <= qi + WINDOW)
        for sub in range(_SWA_PER_STEP):
            band = slice(BLOCK * sub, BLOCK * (sub + 2))
            own = slice(BLOCK * sub, BLOCK * (sub + 1))
            kb, vb = k_all[band], v_all[band]
            dist = jnp.abs(pos_all[band] - pq_ref[:, own])
            valid = window & ((n > 0) | (ki >= BLOCK)) if sub == 0 else window
            q_t, vb_t = q_ref[own, :].T, vb.T
            out_t, lse = [], []
            for kh in range(N_KV_A):
                st_g = _dot(_head_cols(kb, kh).astype(BF16), _group_t(q_t, kh))
                ps = []
                for j in range(_GROUP_A):
                    h = _GROUP_A * kh + j
                    st = _swa_scores_t(st_g, j, h, dist, valid)
                    sink = sink_ref[0:1, h:h + 1] * _LOG2E
                    m = jnp.maximum(jnp.max(st, axis=0, keepdims=True), sink)
                    e = jnp.exp2(st - m)
                    den = jnp.sum(e, axis=0, keepdims=True) + jnp.exp2(sink - m)
                    ps.append((e * (1.0 / den)).astype(BF16))
                    lse.append(m + jnp.log(den) * _LOG2E)
                o_g = _dot(_head_rows(vb_t, kh).astype(BF16), jnp.concatenate(ps, axis=1))
                out_t.extend(o_g[:, BLOCK * j:BLOCK * (j + 1)] for j in range(_GROUP_A))
            o_ref[own, :] = jnp.concatenate(out_t, axis=0).T
            l_ref[:, own] = jnp.concatenate(lse, axis=0)

    cur = lambda n: (n, 0)
    prev = lambda n: jnp.maximum(_SWA_PER_STEP * n - 1, 0)
    return pl.pallas_call(
        body, name="swa_fwd", grid=(t // span,),
        in_specs=[pl.BlockSpec((span, WIDTH_A), lambda n: (n, _QA_BLK)),
                  pl.BlockSpec((span, LANES), lambda n: (n, _KA_BLK)),
                  pl.BlockSpec((BLOCK, LANES), lambda n: (prev(n), _KA_BLK)),
                  pl.BlockSpec((span, LANES), lambda n: (n, _VA_BLK)),
                  pl.BlockSpec((BLOCK, LANES), lambda n: (prev(n), _VA_BLK)),
                  pl.BlockSpec((1, span), lambda n: (0, n)),
                  pl.BlockSpec((span, 1), cur),
                  pl.BlockSpec((BLOCK, 1), lambda n: (prev(n), 0)),
                  _full((1, N_HEADS_A))],
        out_specs=[pl.BlockSpec((span, WIDTH_A), cur), pl.BlockSpec((N_HEADS_A, span), lambda n: (0, n))],
        out_shape=[jax.ShapeDtypeStruct((t, WIDTH_A), F32), jax.ShapeDtypeStruct((N_HEADS_A, t), F32)],
        compiler_params=_params(("parallel",)),
    )(proj, proj, proj, proj, proj, posr, posc, posc, sinks)


def _rope_coeffs(pos, freq):
    ang = pos * freq
    cosv, sinv = jnp.cos(ang), jnp.sin(ang)
    lane = lax.broadcasted_iota(jnp.int32, ang.shape, 1)
    lo = (lane >= QK_NOPE) & (lane < QK_NOPE + QK_ROPE // 2)
    hi = (lane >= QK_NOPE + QK_ROPE // 2) & (lane < QK_NOPE + QK_ROPE)
    c = jnp.where(lane < QK_NOPE, 1.0, jnp.where(lo | hi, cosv, 0.0))
    s = jnp.where(lo, -sinv, jnp.where(hi, sinv, 0.0))
    return c, s, lo, hi


def _rope(xh, c, s, lo):
    up = pltpu.roll(xh, LANES - QK_ROPE // 2, axis=1)
    dn = pltpu.roll(xh, QK_ROPE // 2, axis=1)
    return xh * c + jnp.where(lo, up, dn) * s


def _unrope(dh, c, s, lo, hi):
    g = dh * s
    up = pltpu.roll(g, LANES - QK_ROPE // 2, axis=1)
    dn = pltpu.roll(g, QK_ROPE // 2, axis=1)
    return dh * c + jnp.where(hi, dn, jnp.where(lo, up, 0.0))


_TQ = 512
_MLA_SCALE = Q_HEAD_B ** -0.5


def _mla_prep_fwd(proj, posc, freq, qan, kvan, wq, wk, wv):
    t = proj.shape[0]
    tm = _TQ
    nb = t // tm

    def body(cq_ref, ckv_ref, kr_ref, pos_ref, f_ref, qan_ref, kvan_ref, wq_ref, wk_ref, wv_ref,
             q_ref, k_ref, qt_ref, kt_ref, vt_ref):
        cq = cq_ref[...]
        cqn = ((cq * _rms(cq)) * qan_ref[...]).astype(BF16)
        ckv = ckv_ref[...]
        ckvn = ((ckv * _rms(ckv)) * kvan_ref[...]).astype(BF16)
        qb = _dot(cqn, wq_ref[...])
        kb = _dot(ckvn, wk_ref[...])
        vbt = _dot_nt(wv_ref[...], ckvn)
        c, s, lo, _ = _rope_coeffs(pos_ref[...], f_ref[...])
        kr = _rope(kr_ref[...], c, s, lo)
        for h in range(N_HEADS_B):
            sl = slice(HEAD_PAD * h, HEAD_PAD * (h + 1))
            q_h = _rope(qb[:, sl], c, s, lo)
            k_h = kb[:, sl] + kr
            q_ref[:, sl] = q_h.astype(BF16)
            k_ref[:, sl] = k_h.astype(BF16)
            qt_ref[h, 0] = q_h.T.astype(BF16)
            kt_ref[h, 0] = k_h.T.astype(BF16)
            vt_ref[h, 0] = vbt[V_DIM_B * h:V_DIM_B * (h + 1), :].astype(BF16)

    row = lambda i: (i, 0)
    blk4 = lambda d: pl.BlockSpec((N_HEADS_B, 1, d, tm), lambda i: (0, i, 0, 0))
    return pl.pallas_call(
        body, name="mla_prep_fwd", grid=(nb,),
        in_specs=[pl.BlockSpec((tm, Q_LORA), lambda i: (i, _CQ_BLK)),
                  pl.BlockSpec((tm, LANES), lambda i: (i, _CKV_BLK)),
                  pl.BlockSpec((tm, LANES), lambda i: (i, _KR_BLK)),
                  pl.BlockSpec((tm, 1), row), _full((1, LANES)), _full((1, Q_LORA)), _full((1, KV_LORA)),
                  _full((Q_LORA, MLA_W)), _full((KV_LORA, MLA_W)), _full((N_HEADS_B * V_DIM_B, KV_LORA))],
        out_specs=[pl.BlockSpec((tm, MLA_W), row), pl.BlockSpec((tm, MLA_W), row), blk4(HEAD_PAD), blk4(HEAD_PAD),
                   blk4(V_DIM_B)],
        out_shape=[jax.ShapeDtypeStruct((t, MLA_W), BF16), jax.ShapeDtypeStruct((t, MLA_W), BF16),
                   jax.ShapeDtypeStruct((N_HEADS_B, nb, HEAD_PAD, tm), BF16),
                   jax.ShapeDtypeStruct((N_HEADS_B, nb, HEAD_PAD, tm), BF16),
                   jax.ShapeDtypeStruct((N_HEADS_B, nb, V_DIM_B, tm), BF16)],
        compiler_params=_params(("parallel",)),
    )(proj, proj, proj, posc, freq, qan, kvan, wq, wk, wv)


_MLA_SCALE2 = _MLA_SCALE * _LOG2E


def _mla_fwd(k, qt, vt, w_src):
    t = k.shape[0]
    nb = t // _TQ
    groups = nb // 2

    def body(k_ref, qt_ref, vt_ref, w_ref, o_ref, l_ref, wg_ref, raw, send_sems, recv_sems, local_sem):
        g = pl.program_id(1)
        first = (pl.program_id(0) == 0) & (g == 0)
        last = (pl.program_id(0) == N_HEADS_B - 1) & (g == groups - 1)

        @pl.when(first)
        def _():
            _gather_start(w_ref, wg_ref, send_sems, recv_sems, local_sem)

        def keys(kj):
            return k_ref[pl.ds(pl.multiple_of(kj * _TQ, _TQ), _TQ), :]

        def products(kj, slot):
            kv = keys(kj)
            raw[slot, 0] = _dot(kv, qt_ref[0, 0])
            raw[slot, 1] = _dot(kv, qt_ref[0, 1])

        def update(stats, raw_ref, kj, diagonal=False):
            m, l, acc = stats
            scores = raw_ref[...]
            if diagonal:
                key = lax.broadcasted_iota(jnp.int32, scores.shape, 0)
                qry = lax.broadcasted_iota(jnp.int32, scores.shape, 1)
                scores = jnp.where(key <= qry, scores, NEG)
            m_new = jnp.maximum(m, jnp.max(scores, axis=0, keepdims=True) * _MLA_SCALE2)
            alpha = jnp.exp2(m - m_new)
            p = jnp.exp2(scores * _MLA_SCALE2 - m_new).astype(BF16)
            pv = _dot(jnp.concatenate([vt_ref[0, kj], jnp.ones((16, _TQ), BF16)], axis=0), p)
            return m_new, alpha * l + pv[V_DIM_B:V_DIM_B + 8], alpha * acc + pv[:V_DIM_B]

        def trip(i, stats):
            sa, sb = stats
            products(2 * i + 1, 1)
            sa, sb = update(sa, raw.at[0, 0], 2 * i), update(sb, raw.at[0, 1], 2 * i)
            products(2 * i + 2, 0)
            return update(sa, raw.at[1, 0], 2 * i + 1), update(sb, raw.at[1, 1], 2 * i + 1)

        init = (jnp.full((1, _TQ), NEG, F32), jnp.zeros((8, _TQ), F32), jnp.zeros((V_DIM_B, _TQ), F32))
        products(0, 0)
        sa, sb = lax.fori_loop(0, g, trip, (init, init))
        raw[1, 1] = _dot(keys(2 * g + 1), qt_ref[0, 1])
        sa = update(sa, raw.at[0, 0], 2 * g, True)
        sb = update(update(sb, raw.at[0, 1], 2 * g), raw.at[1, 1], 2 * g + 1, True)
        for which, (m, l, acc) in enumerate((sa, sb)):
            o_ref[0, which] = acc / l[0:1]
            l_ref[0, which] = m + jnp.log(l[0:1]) * _LOG2E

        @pl.when(last)
        def _():
            _gather_wait(w_ref, wg_ref, send_sems, recv_sems, local_sem)

    two = lambda d: pl.BlockSpec((1, 2, d, _TQ), lambda h, g: (h, g, 0, 0))
    return pl.pallas_call(
        body, name="mla_fwd", grid=(N_HEADS_B, groups),
        in_specs=[pl.BlockSpec((t, HEAD_PAD), lambda h, g: (0, h)), two(HEAD_PAD),
                  pl.BlockSpec((1, nb, V_DIM_B, _TQ), lambda h, g: (h, 0, 0, 0)), _HBM],
        out_specs=[two(V_DIM_B), two(1), _HBM],
        out_shape=[jax.ShapeDtypeStruct((N_HEADS_B, nb, V_DIM_B, _TQ), F32),
                   jax.ShapeDtypeStruct((N_HEADS_B, nb, 1, _TQ), F32),
                   jax.ShapeDtypeStruct((N_CHIPS,) + w_src.shape, w_src.dtype)],
        scratch_shapes=[pltpu.VMEM((2, 2, _TQ, _TQ), F32),
                        pltpu.SemaphoreType.DMA((3,)), pltpu.SemaphoreType.DMA((3,)), pltpu.SemaphoreType.DMA(())],
        compiler_params=_params(("arbitrary", "arbitrary")),
    )(k, qt, vt, w_src)


def _ot_spec(tm, d):
    per = _TQ // tm
    return pl.BlockSpec((N_HEADS_B, 1, d, tm), lambda i: (0, i // per, 0, i % per))


def _mix_out_fwd(out_a, out_bt, proj, x, w_oa, w_ob, wb, g2, g3):
    t = x.shape[0]
    tm = 512

    def body(oa_ref, obt_ref, ga_ref, gb_ref, x_ref, woa_ref, wob_ref, wout_ref, g2_ref, g3_ref,
             mg_ref, y_ref, x1_ref, h2_ref):
        oa = _dot(oa_ref[...].astype(BF16), woa_ref[...])
        obt = obt_ref[...].reshape(N_HEADS_B * V_DIM_B, tm).astype(BF16)
        ob = _dot_tn(obt, wob_ref[...])
        merged = (jax.nn.sigmoid(ga_ref[...]) * oa + jax.nn.sigmoid(gb_ref[...]) * ob).astype(BF16)
        mg_ref[...] = merged
        y = _dot(merged, wout_ref[...].reshape(D_MODEL, D_MODEL))
        y_ref[...] = y
        x1 = x_ref[...] + (y * _rms(y)) * g2_ref[...]
        x1_ref[...] = x1
        h2_ref[...] = ((x1 * _rms(x1)) * g3_ref[...]).astype(BF16)

    row = lambda i: (i, 0)
    blk = pl.BlockSpec((tm, D_MODEL), row)
    return pl.pallas_call(
        body, name="mix_out_fwd", grid=(t // tm,),
        in_specs=[pl.BlockSpec((tm, WIDTH_A), row), _ot_spec(tm, V_DIM_B), pl.BlockSpec((tm, D_MODEL), lambda i: (i, 0)),
                  pl.BlockSpec((tm, D_MODEL), lambda i: (i, 1)), blk,
                  _full((WIDTH_A, D_MODEL)), _full((N_HEADS_B * V_DIM_B, D_MODEL)), _wb_spec("w_out"),
                  _full((1, D_MODEL)), _full((1, D_MODEL))],
        out_specs=[blk, blk, blk, blk],
        out_shape=[jax.ShapeDtypeStruct((t, D_MODEL), BF16), jax.ShapeDtypeStruct((t, D_MODEL), F32),
                   jax.ShapeDtypeStruct((t, D_MODEL), F32), jax.ShapeDtypeStruct((t, D_MODEL), BF16)],
        compiler_params=_params(("parallel",)),
    )(out_a, out_bt, proj, proj, x, w_oa, w_ob, wb, g2, g3)


_TM_MLP = 512


def _row_halves(tm):
    return slice(0, tm // 2), slice(tm // 2, tm)


def _up_fwd(h2, wb):
    t = h2.shape[0]
    tm = _TM_MLP

    def body(h_ref, w_ref, a_ref):
        hv = h_ref[...]
        for j in range(N_CHIPS):
            u = _dot(hv, w_ref[j])
            a_ref[:, D_MODEL * j:D_MODEL * (j + 1)] = jnp.square(jnp.maximum(u, 0.0)).astype(BF16)

    return pl.pallas_call(
        body, name="up_fwd", grid=(t // tm,),
        in_specs=[pl.BlockSpec((tm, D_MODEL), lambda i: (i, 0)), _wb_spec("w_up")],
        out_specs=pl.BlockSpec((tm, D_FF), lambda i: (i, 0)),
        out_shape=jax.ShapeDtypeStruct((t, D_FF), BF16),
        compiler_params=_params(("parallel",)),
    )(h2, wb)


def _down_fwd_loss(a, wb, x1, target, g4):
    t = a.shape[0]
    tm = _TM_MLP

    def body(a_ref, w_ref, x1_ref, tg_ref, g_ref, dx2_ref, dyd_ref, dg_ref, loss_ref):
        @pl.when(pl.program_id(0) == 0)
        def _():
            dg_ref[...] = jnp.zeros(dg_ref.shape, F32)
            loss_ref[...] = jnp.zeros(loss_ref.shape, F32)

        yd = _dot(a_ref[...], w_ref[...].reshape(D_FF, D_MODEL))
        r = _rms(yd)
        n = yd * r
        diff = (x1_ref[...] + n * g_ref[...]) - tg_ref[...]
        loss_ref[...] += 0.5 * jnp.sum(jnp.mean(diff * diff, axis=-1, keepdims=True), axis=0, keepdims=True)
        dx2 = diff * (1.0 / D_MODEL)
        dx2_ref[...] = dx2
        dyd, dg = _norm_bwd(dx2, n, r, g_ref[...])
        dyd_ref[...] = dyd.astype(BF16)
        dg_ref[...] += dg

    row = lambda i: (i, 0)
    blk = pl.BlockSpec((tm, D_MODEL), row)
    return pl.pallas_call(
        body, name="down_fwd_loss", grid=(t // tm,),
        in_specs=[pl.BlockSpec((tm, D_FF), row), _wb_spec("w_down"), blk, blk, _full((1, D_MODEL))],
        out_specs=[blk, blk, _full((1, D_MODEL)), _full((1, LANES))],
        out_shape=[jax.ShapeDtypeStruct((t, D_MODEL), F32), jax.ShapeDtypeStruct((t, D_MODEL), BF16),
                   jax.ShapeDtypeStruct((1, D_MODEL), F32), jax.ShapeDtypeStruct((1, LANES), F32)],
        compiler_params=_params(("arbitrary",)),
    )(a, wb, x1, target, g4)


def _matmul_tn(a, b, name, tm, tn, tk=1024):
    t, m = a.shape
    n = b.shape[1]
    tk = min(tk, t)
    nk = t // tk

    def body(a_ref, b_ref, o_ref):
        @pl.when(pl.program_id(2) == 0)
        def _():
            o_ref[...] = jnp.zeros(o_ref.shape, F32)

        o_ref[...] += _dot_tn(a_ref[...].astype(BF16), b_ref[...].astype(BF16))

    return pl.pallas_call(
        body, name=name, grid=(m // tm, n // tn, nk),
        in_specs=[pl.BlockSpec((tk, tm), lambda i, j, k: (k, i)), pl.BlockSpec((tk, tn), lambda i, j, k: (k, j))],
        out_specs=pl.BlockSpec((tm, tn), lambda i, j, k: (i, j)),
        out_shape=jax.ShapeDtypeStruct((m, n), F32),
        compiler_params=_params(("parallel", "parallel", "arbitrary")),
    )(a, b)


_TK_DW = 2048


def _dw_into_blocks(a, b, weight, tm, tk, buf=None):
    t, m = a.shape
    n = b.shape[1]
    tk = min(tk, t)
    nk = t // tk
    rows = PACK_ROWS[weight]
    br = min(tm, rows)
    chips = tm // br
    first = _row_offset(GROUP_B, weight) // br
    per_chip = rows // br
    if weight == "w_up":
        out_map = lambda i, j, k: (j, first + i, 0)
    elif chips > 1:
        out_map = lambda i, j, k: (i, first, 0)
    else:
        out_map = lambda i, j, k: (i // per_chip, first + i % per_chip, 0)

    def body(a_ref, b_ref, *rest):
        o_ref = rest[-1]

        @pl.when(pl.program_id(2) == 0)
        def _():
            o_ref[...] = jnp.zeros(o_ref.shape, F32)

        o_ref[...] += _dot_tn(a_ref[...].astype(BF16), b_ref[...].astype(BF16)).reshape(o_ref.shape)

    in_specs = [pl.BlockSpec((tk, tm), lambda i, j, k: (k, i)), pl.BlockSpec((tk, D_MODEL), lambda i, j, k: (k, j))]
    operands = [a, b]
    if buf is not None:
        in_specs.append(pl.BlockSpec(memory_space=pl.ANY))
        operands.append(buf)
    total = sum(PACK_ROWS[w] for w in GROUP_B)
    return pl.pallas_call(
        body, name="dw_" + weight[2:], grid=(m // tm, n // D_MODEL, nk),
        in_specs=in_specs, out_specs=pl.BlockSpec((chips, br, D_MODEL), out_map),
        out_shape=jax.ShapeDtypeStruct((N_CHIPS, total, D_MODEL), F32),
        input_output_aliases={} if buf is None else {2: 0},
        compiler_params=_params(("parallel", "parallel", "arbitrary")),
    )(*operands)


def _down_bwd(dyd, wb, a):
    t = dyd.shape[0]
    tm = _TM_MLP

    def body(d_ref, w_ref, a_ref, du_ref):
        dv = d_ref[...]
        for j in range(N_CHIPS):
            cols = slice(D_MODEL * j, D_MODEL * (j + 1))
            av = a_ref[:, cols].astype(F32)
            relu_u = jnp.where(av > 0.0, av * lax.rsqrt(av), 0.0)
            du_ref[:, cols] = (_dot_nt(dv, w_ref[j]) * (2.0 * relu_u)).astype(BF16)

    row = lambda i: (i, 0)
    return pl.pallas_call(
        body, name="down_bwd", grid=(t // tm,),
        in_specs=[pl.BlockSpec((tm, D_MODEL), row), _wb_spec("w_down"), pl.BlockSpec((tm, D_FF), row)],
        out_specs=pl.BlockSpec((tm, D_FF), row),
        out_shape=jax.ShapeDtypeStruct((t, D_FF), BF16),
        compiler_params=_params(("parallel",)),
    )(dyd, wb, a)


def _up_bwd(du, wb, x1, dx2, y, g3, g2):
    t = du.shape[0]
    tm = _TM_MLP

    def body(du_ref, w_ref, x1_ref, dx2_ref, y_ref, g3_ref, g2_ref, dx1_ref, dy_ref, dg3_ref, dg2_ref):
        @pl.when(pl.program_id(0) == 0)
        def _():
            dg3_ref[...] = jnp.zeros(dg3_ref.shape, F32)
            dg2_ref[...] = jnp.zeros(dg2_ref.shape, F32)

        dh2 = _dot_nt(du_ref[:, 0:D_MODEL], w_ref[0])
        for j in range(1, N_CHIPS):
            dh2 = dh2 + _dot_nt(du_ref[:, D_MODEL * j:D_MODEL * (j + 1)], w_ref[j])
        x1 = x1_ref[...]
        r3 = _rms(x1)
        d3, dg3 = _norm_bwd(dh2, x1 * r3, r3, g3_ref[...])
        dx1 = dx2_ref[...] + d3
        dx1_ref[...] = dx1
        dg3_ref[...] += dg3
        y = y_ref[...]
        r2 = _rms(y)
        dy, dg2 = _norm_bwd(dx1, y * r2, r2, g2_ref[...])
        dy_ref[...] = dy.astype(BF16)
        dg2_ref[...] += dg2

    row = lambda i: (i, 0)
    blk = pl.BlockSpec((tm, D_MODEL), row)
    return pl.pallas_call(
        body, name="up_bwd", grid=(t // tm,),
        in_specs=[pl.BlockSpec((tm, D_FF), row), _wb_spec("w_up"),
                  blk, blk, blk, _full((1, D_MODEL)), _full((1, D_MODEL))],
        out_specs=[blk, blk, _full((1, D_MODEL)), _full((1, D_MODEL))],
        out_shape=[jax.ShapeDtypeStruct((t, D_MODEL), F32), jax.ShapeDtypeStruct((t, D_MODEL), BF16),
                   jax.ShapeDtypeStruct((1, D_MODEL), F32), jax.ShapeDtypeStruct((1, D_MODEL), F32)],
        compiler_params=_params(("arbitrary",)),
    )(du, wb, x1, dx2, y, g3, g2)


def _mix_out_bwd(dy, out_a, out_bt, proj, w_oa, w_ob, wb):
    t = dy.shape[0]
    tm = 256
    nb = t // _TQ

    def body(dy_ref, oa_ref, obt_ref, ga_ref, gb_ref, woa_ref, wob_ref, wout_ref,
             doa_ref, dob_ref, dga_ref, dgb_ref, da_ref, db_ref, dbt_ref, dela_ref, delb_ref):
        dm = _dot_nt(dy_ref[...], wout_ref[...].reshape(D_MODEL, D_MODEL))
        out_a_v = oa_ref[...]
        out_bt_v = obt_ref[...].reshape(N_HEADS_B * V_DIM_B, tm)
        oa = _dot(out_a_v.astype(BF16), woa_ref[...])
        ob = _dot_tn(out_bt_v.astype(BF16), wob_ref[...])
        sa, sb = jax.nn.sigmoid(ga_ref[...]), jax.nn.sigmoid(gb_ref[...])
        doa = (dm * sa).astype(BF16)
        dob = (dm * sb).astype(BF16)
        doa_ref[...] = doa
        dob_ref[...] = dob
        dga_ref[...] = (dm * oa * (sa * (1.0 - sa))).astype(BF16)
        dgb_ref[...] = (dm * ob * (sb * (1.0 - sb))).astype(BF16)
        d_out_a = _dot_nt(doa, woa_ref[...])
        da_ref[...] = d_out_a
        prod_at = (d_out_a * out_a_v).T
        dela_ref[...] = jnp.concatenate(
            [jnp.sum(_head_rows(prod_at, h), axis=0, keepdims=True) for h in range(N_HEADS_A)], axis=0)
        d_out_b = _dot_nt(dob, wob_ref[...])
        d_out_bt = _dot_nt(wob_ref[...], dob)
        prod_bt = d_out_bt * out_bt_v
        for h in range(N_HEADS_B):
            db_ref[h] = d_out_b[:, V_DIM_B * h:V_DIM_B * (h + 1)].astype(BF16)
            dbt_ref[h, 0] = d_out_bt[V_DIM_B * h:V_DIM_B * (h + 1), :].astype(BF16)
            delb_ref[h, 0] = jnp.sum(prod_bt[V_DIM_B * h:V_DIM_B * (h + 1), :], axis=0, keepdims=True)

    row = lambda i: (i, 0)
    blk = pl.BlockSpec((tm, D_MODEL), row)
    return pl.pallas_call(
        body, name="mix_out_bwd", grid=(t // tm,),
        in_specs=[blk, pl.BlockSpec((tm, WIDTH_A), row), _ot_spec(tm, V_DIM_B),
                  pl.BlockSpec((tm, D_MODEL), lambda i: (i, 0)), pl.BlockSpec((tm, D_MODEL), lambda i: (i, 1)),
                  _full((WIDTH_A, D_MODEL)), _full((N_HEADS_B * V_DIM_B, D_MODEL)), _wb_spec("w_out")],
        out_specs=[blk, blk, blk, blk, pl.BlockSpec((tm, WIDTH_A), row),
                   pl.BlockSpec((N_HEADS_B, tm, V_DIM_B), lambda i: (0, i, 0)), _ot_spec(tm, V_DIM_B),
                   pl.BlockSpec((N_HEADS_A, tm), lambda i: (0, i)), _ot_spec(tm, 1)],
        out_shape=[jax.ShapeDtypeStruct((t, D_MODEL), BF16)] * 4
        + [jax.ShapeDtypeStruct((t, WIDTH_A), F32), jax.ShapeDtypeStruct((N_HEADS_B, t, V_DIM_B), BF16),
           jax.ShapeDtypeStruct((N_HEADS_B, nb, V_DIM_B, _TQ), BF16), jax.ShapeDtypeStruct((N_HEADS_A, t), F32),
           jax.ShapeDtypeStruct((N_HEADS_B, nb, 1, _TQ), F32)],
        compiler_params=_params(("parallel",)),
    )(dy, out_a, out_bt, proj, proj, w_oa, w_ob, wb)


def _dw_ob(out_bt, dob):
    t = dob.shape[0]
    nb = t // _TQ

    def body(obt_ref, dob_ref, o_ref):
        @pl.when(pl.program_id(0) == 0)
        def _():
            o_ref[...] = jnp.zeros(o_ref.shape, F32)

        obt = obt_ref[...].reshape(N_HEADS_B * V_DIM_B, _TQ).astype(BF16)
        o_ref[...] += _dot(obt, dob_ref[...])

    return pl.pallas_call(
        body, name="dw_o_b", grid=(nb,),
        in_specs=[pl.BlockSpec((N_HEADS_B, 1, V_DIM_B, _TQ), lambda i: (0, i, 0, 0)),
                  pl.BlockSpec((_TQ, D_MODEL), lambda i: (i, 0))],
        out_specs=_full((N_HEADS_B * V_DIM_B, D_MODEL)),
        out_shape=jax.ShapeDtypeStruct((N_HEADS_B * V_DIM_B, D_MODEL), F32),
        compiler_params=_params(("arbitrary",)),
    )(out_bt, dob)


def _mla_bwd(q, k, qt, kt, vt, d_out, d_out_t, lse, delta, gp):
    t = q.shape[0]
    nb = t // _TQ

    def body(k_ref, kt_ref, vt_ref, q_ref, qt_ref, do_ref, dot_ref, lrow_ref, drow_ref, gp_ref,
             dq_ref, dkt_ref, dvt_ref, land_ref, l_rep, d_rep, send_sems, recv_sems):
        step = pl.program_id(1)
        kj = nb - 1 - step

        @pl.when((pl.program_id(0) == 0) & (step == 0))
        def _():
            _scatter_start(gp_ref, land_ref, send_sems, recv_sems)

        @pl.when(step == 0)
        def _():
            dq_ref[...] = jnp.zeros(dq_ref.shape, F32)
            for b in range(nb):
                l_rep[_TQ * b:_TQ * (b + 1), :] = jnp.broadcast_to(lrow_ref[0, b], (LANES, _TQ)).T
                d_rep[_TQ * b:_TQ * (b + 1), :] = jnp.broadcast_to(drow_ref[0, b], (LANES, _TQ)).T

        kv, k_t, v_t = k_ref[...], kt_ref[0, 0], vt_ref[0, 0]

        def rows_of(qi):
            return pl.ds(pl.multiple_of(qi * _TQ, _TQ), _TQ)

        def products(qi, diagonal=False):
            s = _dot(q_ref[rows_of(qi), :], k_t) * _MLA_SCALE2
            if diagonal:
                qry = lax.broadcasted_iota(jnp.int32, s.shape, 0)
                key = lax.broadcasted_iota(jnp.int32, s.shape, 1)
                s = jnp.where(key <= qry, s, NEG)
            return s, _dot(do_ref[0, rows_of(qi), :], v_t)

        def update(carry, prods, qi):
            dkt, dvt = carry
            s, dp = prods
            lse, delta = l_rep[rows_of(qi), :], d_rep[rows_of(qi), :]
            ps, dss = [], []
            for c in range(_TQ // LANES):
                strip = slice(LANES * c, LANES * (c + 1))
                p = jnp.exp2(s[:, strip] - lse)
                ps.append(p.astype(BF16))
                dss.append((p * (dp[:, strip] - delta) * _MLA_SCALE).astype(BF16))
            p_b, ds_b = jnp.concatenate(ps, axis=1), jnp.concatenate(dss, axis=1)
            dvt = dvt + _dot(dot_ref[0, qi], p_b)
            dkt = dkt + _dot(qt_ref[0, qi], ds_b)
            dq_ref[rows_of(qi), :] += _dot(ds_b, kv)
            return dkt, dvt

        def pair(i, carry):
            qa = kj + 1 + 2 * i
            pa, pb = products(qa), products(qa + 1)
            return update(update(carry, pa, qa), pb, qa + 1)

        init = (jnp.zeros((HEAD_PAD, _TQ), F32), jnp.zeros((V_DIM_B, _TQ), F32))
        carry = update(init, products(kj, True), kj)
        pairs = (nb - 1 - kj) // 2
        carry = lax.fori_loop(0, pairs, pair, carry)
        dkt, dvt = lax.fori_loop(kj + 1 + 2 * pairs, nb, lambda qi, cr: update(cr, products(qi), qi), carry)
        dkt_ref[0, 0] = dkt
        dvt_ref[0, 0] = dvt

        @pl.when((pl.program_id(0) == N_HEADS_B - 1) & (step == nb - 1))
        def _():
            _scatter_wait(gp_ref, land_ref, send_sems, recv_sems)

    head4 = lambda d: pl.BlockSpec((1, nb, d, _TQ), lambda h, s: (h, 0, 0, 0))
    blk4 = lambda d: pl.BlockSpec((1, 1, d, _TQ), lambda h, s: (h, nb - 1 - s, 0, 0))
    head3 = lambda d: pl.BlockSpec((1, t, d), lambda h, kj: (h, 0, 0))
    per_head = pl.BlockSpec((t, HEAD_PAD), lambda h, kj: (0, h))
    return pl.pallas_call(
        body, name="mla_bwd", grid=(N_HEADS_B, nb),
        in_specs=[pl.BlockSpec((_TQ, HEAD_PAD), lambda h, s: (nb - 1 - s, h)), blk4(HEAD_PAD), blk4(V_DIM_B),
                  per_head, head4(HEAD_PAD), head3(V_DIM_B), head4(V_DIM_B), head4(1), head4(1), _HBM],
        out_specs=[per_head, blk4(HEAD_PAD), blk4(V_DIM_B), _HBM],
        out_shape=[jax.ShapeDtypeStruct((t, MLA_W), F32), jax.ShapeDtypeStruct((N_HEADS_B, nb, HEAD_PAD, _TQ), F32),
                   jax.ShapeDtypeStruct((N_HEADS_B, nb, V_DIM_B, _TQ), F32),
                   jax.ShapeDtypeStruct((3,) + gp.shape[1:], gp.dtype)],
        scratch_shapes=[pltpu.VMEM((t, LANES), F32), pltpu.VMEM((t, LANES), F32),
                        pltpu.SemaphoreType.DMA((3,)), pltpu.SemaphoreType.DMA((3,))],
        compiler_params=_params(("arbitrary", "arbitrary")),
    )(k, kt, vt, q, qt, d_out, d_out_t, lse, delta, gp)


def _mla_prep_bwd(dq, dkt, dvt, proj, posc, freq, qan, kvan, wq, wk, wv, swap_src):
    t = dq.shape[0]
    tm = _TQ

    def body(dq_ref, dkt_ref, dvt_ref, cq_ref, ckv_ref, pos_ref, f_ref, qan_ref, kvan_ref, wq_ref, wk_ref, wv_ref, src_ref,
             dcq_ref, dckv_ref, dkr_ref, dwq_ref, dwk_ref, dwv_ref, dqan_ref, dkvan_ref, got_ref, send_sem, recv_sem):
        swap = _sibling_copy(src_ref, got_ref, send_sem, recv_sem)

        @pl.when(pl.program_id(0) == 0)
        def _():
            swap.start()
            for r in (dwq_ref, dwk_ref, dwv_ref, dqan_ref, dkvan_ref):
                r[...] = jnp.zeros(r.shape, F32)

        cq = cq_ref[...]
        rq = _rms(cq)
        nq_ = cq * rq
        cqn = (nq_ * qan_ref[...]).astype(BF16)
        ckv = ckv_ref[...]
        rkv = _rms(ckv)
        nkv = ckv * rkv
        ckvn = (nkv * kvan_ref[...]).astype(BF16)
        c, s, lo, hi = _rope_coeffs(pos_ref[...], f_ref[...])
        dkr = jnp.zeros((tm, LANES), F32)
        dqb, dkb = [], []
        for h in range(N_HEADS_B):
            dqb.append(_unrope(dq_ref[:, HEAD_PAD * h:HEAD_PAD * (h + 1)], c, s, lo, hi).astype(BF16))
            dk_h = dkt_ref[h, 0].T
            dkr = dkr + dk_h
            dkb.append(dk_h.astype(BF16))
        dqb, dkb = jnp.concatenate(dqb, axis=1), jnp.concatenate(dkb, axis=1)
        dkr_ref[...] = jnp.where(lo | hi, _unrope(dkr, c, s, lo, hi), 0.0).astype(BF16)
        dvb = dvt_ref[...].reshape(N_HEADS_B * V_DIM_B, tm).T.astype(BF16)
        dwq_ref[...] += _dot_tn(cqn, dqb)
        dwk_ref[...] += _dot_tn(ckvn, dkb)
        dwv_ref[...] += _dot_tn(ckvn, dvb)
        dcqn = _dot_nt(dqb, wq_ref[...])
        dckvn = _dot_nt(dkb, wk_ref[...]) + _dot_nt(dvb, wv_ref[...])
        dcq, dqan = _norm_bwd(dcqn, nq_, rq, qan_ref[...])
        dckv, dkvan = _norm_bwd(dckvn, nkv, rkv, kvan_ref[...])
        dcq_ref[...] = dcq.astype(BF16)
        dckv_ref[...] = dckv.astype(BF16)
        dqan_ref[...] += dqan
        dkvan_ref[...] += dkvan

        @pl.when(pl.program_id(0) == t // tm - 1)
        def _():
            swap.wait_recv()
            swap.wait_send()

    row = lambda i: (i, 0)
    vw = N_HEADS_B * V_DIM_B
    return pl.pallas_call(
        body, name="mla_prep_bwd", grid=(t // tm,),
        in_specs=[pl.BlockSpec((tm, MLA_W), row), pl.BlockSpec((N_HEADS_B, 1, HEAD_PAD, tm), lambda i: (0, i, 0, 0)),
                  pl.BlockSpec((N_HEADS_B, 1, V_DIM_B, tm), lambda i: (0, i, 0, 0)),
                  pl.BlockSpec((tm, Q_LORA), lambda i: (i, _CQ_BLK)),
                  pl.BlockSpec((tm, LANES), lambda i: (i, _CKV_BLK)),
                  pl.BlockSpec((tm, 1), row), _full((1, LANES)), _full((1, Q_LORA)), _full((1, KV_LORA)),
                  _full((Q_LORA, MLA_W)), _full((KV_LORA, MLA_W)), _full((KV_LORA, vw)), _HBM],
        out_specs=[pl.BlockSpec((tm, Q_LORA), row), pl.BlockSpec((tm, LANES), row), pl.BlockSpec((tm, LANES), row),
                   _full((Q_LORA, MLA_W)), _full((KV_LORA, MLA_W)), _full((KV_LORA, vw)),
                   _full((1, Q_LORA)), _full((1, KV_LORA)), _HBM],
        out_shape=[jax.ShapeDtypeStruct((t, Q_LORA), BF16), jax.ShapeDtypeStruct((t, LANES), BF16),
                   jax.ShapeDtypeStruct((t, LANES), BF16),
                   jax.ShapeDtypeStruct((Q_LORA, MLA_W), F32), jax.ShapeDtypeStruct((KV_LORA, MLA_W), F32),
                   jax.ShapeDtypeStruct((KV_LORA, vw), F32),
                   jax.ShapeDtypeStruct((1, Q_LORA), F32), jax.ShapeDtypeStruct((1, KV_LORA), F32),
                   jax.ShapeDtypeStruct(swap_src.shape, swap_src.dtype)],
        scratch_shapes=[pltpu.SemaphoreType.DMA(()), pltpu.SemaphoreType.DMA(())],
        compiler_params=_params(("arbitrary",)),
    )(dq, dkt, dvt, proj, proj, posc, freq, qan, kvan, wq, wk, wv, swap_src)


def _swa_bwd(proj, d_out, lse, delta, posc, posr, sinks):
    t = proj.shape[0]
    per = _SWA_PER_STEP
    span = per * BLOCK
    steps = t // span

    def body(q_ref, kc_ref, kp_ref, vc_ref, vp_ref, do_ref, l_ref, d_ref, pq_ref, pc_ref, pp_ref, sink_ref,
             dq_ref, dk_ref, dv_ref, ds_ref, dkb_s, dvb_s, dk_keep, dv_keep):
        n = pl.program_id(0)

        @pl.when(n == 0)
        def _():
            ds_ref[...] = jnp.zeros(ds_ref.shape, F32)
            dk_keep[...] = jnp.zeros(dk_keep.shape, F32)
            dv_keep[...] = jnp.zeros(dv_keep.shape, F32)

        @pl.when(n < steps)
        def _():
            k_all = jnp.concatenate([kp_ref[...], kc_ref[...]], axis=0)
            v_all = jnp.concatenate([vp_ref[...], vc_ref[...]], axis=0)
            pos_all = jnp.concatenate([pp_ref[...], pc_ref[...]], axis=0)
            ki = lax.broadcasted_iota(jnp.int32, (2 * BLOCK, BLOCK), 0)
            qi = lax.broadcasted_iota(jnp.int32, (2 * BLOCK, BLOCK), 1)
            window = (ki > qi) & (ki <= qi + WINDOW)
            lane = lax.broadcasted_iota(jnp.int32, (1, LANES), 1)
            dsink = jnp.zeros((1, LANES), F32)
            for sub in range(per):
                band = slice(BLOCK * sub, BLOCK * (sub + 2))
                own = slice(BLOCK * sub, BLOCK * (sub + 1))
                kb, vb = k_all[band], v_all[band]
                dist = jnp.abs(pos_all[band] - pq_ref[:, own])
                valid = window & ((n > 0) | (ki >= BLOCK)) if sub == 0 else window
                qv, dov = q_ref[own, :], do_ref[own, :]
                q_t, do_t, kb_t = qv.T, dov.T, kb.T
                dq_t = []
                for kh in range(N_KV_A):
                    heads = range(_GROUP_A * kh, _GROUP_A * (kh + 1))
                    st_g = _dot(_head_cols(kb, kh).astype(BF16), _group_t(q_t, kh))
                    dpt_g = _dot(_head_cols(vb, kh).astype(BF16), _group_t(do_t, kh))
                    pts, dsts = [], []
                    for j, h in enumerate(heads):
                        st = _swa_scores_t(st_g, j, h, dist, valid)
                        l_h, d_h = l_ref[h:h + 1, own], d_ref[h:h + 1, own]
                        pt = jnp.exp2(st - l_h)
                        p_sink = jnp.exp2(sink_ref[0:1, h:h + 1] * _LOG2E - l_h)
                        dsink = dsink + jnp.where(lane == h, jnp.sum(-p_sink * d_h, axis=1, keepdims=True), 0.0)
                        dst = pt * (dpt_g[:, BLOCK * j:BLOCK * (j + 1)] - d_h) * _SWA_SCALE
                        pts.append(pt.astype(BF16))
                        dsts.append(dst.astype(BF16))
                    pt_g, dst_g = jnp.concatenate(pts, axis=1), jnp.concatenate(dsts, axis=1)
                    q_g = jnp.concatenate([_head_cols(qv, h) for h in heads], axis=0).astype(BF16)
                    do_g = jnp.concatenate([_head_cols(dov, h) for h in heads], axis=0).astype(BF16)
                    dkb_s[sub, :, HEAD_DIM_A * kh:HEAD_DIM_A * (kh + 1)] = _dot(dst_g, q_g)
                    dvb_s[sub, :, HEAD_DIM_A * kh:HEAD_DIM_A * (kh + 1)] = _dot(pt_g, do_g)
                    dq_g = _dot(_head_rows(kb_t, kh).astype(BF16), dst_g)
                    dq_t.extend(dq_g[:, BLOCK * j:BLOCK * (j + 1)] for j in range(_GROUP_A))
                dq_ref[own, :] = jnp.concatenate(dq_t, axis=0).T
            ds_ref[...] += dsink
            for keep, out, parts in ((dk_keep, dk_ref, dkb_s), (dv_keep, dv_ref, dvb_s)):
                out[0:span - BLOCK, :] = keep[0:span - BLOCK, :]
                out[span - BLOCK:span, :] = keep[span - BLOCK:span, :] + parts[0, 0:BLOCK, :]
                for s in range(per - 1):
                    keep[BLOCK * s:BLOCK * (s + 1), :] = parts[s, BLOCK:2 * BLOCK, :] + parts[s + 1, 0:BLOCK, :]
                keep[span - BLOCK:span, :] = parts[per - 1, BLOCK:2 * BLOCK, :]

        @pl.when(n == steps)
        def _():
            dk_ref[...] = dk_keep[...]
            dv_ref[...] = dv_keep[...]

    last = steps - 1
    cur = lambda n: (jnp.minimum(n, last), 0)
    cur_t = lambda n: (0, jnp.minimum(n, last))
    prv = lambda n: jnp.maximum(per * jnp.minimum(n, last) - 1, 0)
    out_prev = lambda n: (jnp.maximum(n - 1, 0), 0)
    return pl.pallas_call(
        body, name="swa_bwd", grid=(steps + 1,),
        in_specs=[pl.BlockSpec((span, WIDTH_A), lambda n: (jnp.minimum(n, last), _QA_BLK)),
                  pl.BlockSpec((span, LANES), lambda n: (jnp.minimum(n, last), _KA_BLK)),
                  pl.BlockSpec((BLOCK, LANES), lambda n: (prv(n), _KA_BLK)),
                  pl.BlockSpec((span, LANES), lambda n: (jnp.minimum(n, last), _VA_BLK)),
                  pl.BlockSpec((BLOCK, LANES), lambda n: (prv(n), _VA_BLK)),
                  pl.BlockSpec((span, WIDTH_A), cur), pl.BlockSpec((N_HEADS_A, span), cur_t),
                  pl.BlockSpec((N_HEADS_A, span), cur_t), pl.BlockSpec((1, span), cur_t),
                  pl.BlockSpec((span, 1), cur), pl.BlockSpec((BLOCK, 1), lambda n: (prv(n), 0)),
                  _full((1, N_HEADS_A))],
        out_specs=[pl.BlockSpec((span, WIDTH_A), cur), pl.BlockSpec((span, LANES), out_prev),
                   pl.BlockSpec((span, LANES), out_prev), _full((1, LANES))],
        out_shape=[jax.ShapeDtypeStruct((t, WIDTH_A), F32), jax.ShapeDtypeStruct((t, LANES), F32),
                   jax.ShapeDtypeStruct((t, LANES), F32), jax.ShapeDtypeStruct((1, LANES), F32)],
        scratch_shapes=[pltpu.VMEM((per, 2 * BLOCK, LANES), F32), pltpu.VMEM((per, 2 * BLOCK, LANES), F32),
                        pltpu.VMEM((span, LANES), F32), pltpu.VMEM((span, LANES), F32)],
        compiler_params=_params(("arbitrary",)),
    )(proj, proj, proj, proj, proj, d_out, lse, delta, posr, posc, posc, sinks)


def _in_bwd(dproj, w_in_t, x, dx1, g1, gp):
    t = x.shape[0]
    tm = 512
    steps = t // tm

    def body(dp_ref, w_ref, x_ref, dx1_ref, g_ref, gp_ref, dx_ref, dg_ref, land_ref, send_sems, recv_sems):
        i = pl.program_id(0)

        @pl.when(i == 0)
        def _():
            dg_ref[...] = jnp.zeros(dg_ref.shape, F32)
            _scatter_start(gp_ref, land_ref, send_sems, recv_sems)

        for rows in _row_halves(tm):
            dh = _dot(dp_ref[rows, :], w_ref[...])
            xv = x_ref[rows, :]
            r = _rms(xv)
            dx, dg = _norm_bwd(dh, xv * r, r, g_ref[...])
            dx_ref[rows, :] = dx1_ref[rows, :] + dx
            dg_ref[...] += dg

        @pl.when(i == steps - 1)
        def _():
            _scatter_wait(gp_ref, land_ref, send_sems, recv_sems)

    row = lambda i: (i, 0)
    blk = pl.BlockSpec((tm, D_MODEL), row)
    return pl.pallas_call(
        body, name="in_bwd", grid=(steps,),
        in_specs=[pl.BlockSpec((tm, D_IN_PAD), row), _full((D_IN_PAD, D_MODEL)), blk, blk, _full((1, D_MODEL)), _HBM],
        out_specs=[blk, _full((1, D_MODEL)), _HBM],
        out_shape=[jax.ShapeDtypeStruct((t, D_MODEL), F32), jax.ShapeDtypeStruct((1, D_MODEL), F32),
                   jax.ShapeDtypeStruct((3,) + gp.shape[1:], gp.dtype)],
        scratch_shapes=[pltpu.SemaphoreType.DMA((3,)), pltpu.SemaphoreType.DMA((3,))],
        compiler_params=_params(("arbitrary",)),
    )(dproj, w_in_t, x, dx1, g1, gp)


def _adamw_store(w, g, m, v, out_refs):
    g_out, d_out, m_out, v_out = out_refs
    m_new = ADAM_B1 * m + (1.0 - ADAM_B1) * g
    v_new = ADAM_B2 * v + (1.0 - ADAM_B2) * jnp.square(g)
    m_hat = m_new / (1.0 - ADAM_B1 ** ADAM_STEP)
    v_hat = v_new / (1.0 - ADAM_B2 ** ADAM_STEP)
    g_out[...] = g
    d_out[...] = -ADAM_LR * (m_hat / (jnp.sqrt(v_hat) + ADAM_EPS) + ADAM_WD * w)
    m_out[...] = m_new
    v_out[...] = v_new


_SMALL_SLOTS = {"pre_norm_mix": (0, 0, D_MODEL), "post_norm_mix": (1, 0, D_MODEL), "pre_norm_mlp": (2, 0, D_MODEL),
                "post_norm_mlp": (3, 0, D_MODEL), "q_a_norm": (4, 0, Q_LORA), "kv_a_norm": (4, Q_LORA, KV_LORA),
                "sinks": (4, Q_LORA + KV_LORA, N_HEADS_A)}
_LOSS_ROW = 5


def _adamw_small(red, w, m, v):
    names = tuple(_SMALL_SLOTS)
    n = len(names)

    def body(*refs):
        red_ref, ws, ms, vs, outs = refs[0], refs[1:1 + n], refs[1 + n:1 + 2 * n], refs[1 + 2 * n:1 + 3 * n], refs[1 + 3 * n:]
        for k, name in enumerate(names):
            row, lane, width = _SMALL_SLOTS[name]
            g = red_ref[row:row + 1, lane:lane + width]
            _adamw_store(ws[k][...], g, ms[k][...], vs[k][...], outs[4 * k:4 * k + 4])

    vmem = pl.BlockSpec(memory_space=pltpu.VMEM)
    res = pl.pallas_call(
        body, name="adamw_small", in_specs=[vmem] * (1 + 3 * n), out_specs=[vmem] * (4 * n),
        out_shape=[jax.ShapeDtypeStruct(w[name].shape, F32) for name in names for _ in range(4)],
    )(red, *[w[k] for k in names], *[m[k] for k in names], *[v[k] for k in names])
    return {name: res[4 * k:4 * k + 4] for k, name in enumerate(names)}


_ADAMW_RIDERS = ("w_up", "w_down", "w_out")


def _dw_in_adamw(dproj, h, g_parts, w, m, v):
    t, rows_out = dproj.shape
    tm, tk = rows_out // 2, min(1024, t)
    nk = t // tk
    steps = 2 * nk
    names = _ADAMW_RIDERS
    n = len(names)

    def body(a_ref, b_ref, *rest):
        g1s, g2s, ws, ms, vs = (rest[n * j:n * (j + 1)] for j in range(5))
        o_ref, outs = rest[5 * n], rest[5 * n + 1:]

        @pl.when(pl.program_id(2) == 0)
        def _():
            o_ref[...] = jnp.zeros(o_ref.shape, F32)

        o_ref[...] += _dot_tn(a_ref[...], b_ref[...])
        for j in range(n):
            _adamw_store(ws[j][...], g1s[j][...] + g2s[j][...], ms[j][...], vs[j][...], outs[4 * j:4 * j + 4])

    def rider_spec(name, packed):
        br = SHARD_SHAPES[name][0] // steps
        first = _row_offset(GROUP_B, name) // br if packed else 0
        return pl.BlockSpec((br, D_MODEL), lambda i, j, k: (first + i * nk + k, 0))

    g_specs = [rider_spec(name, True) for name in names]
    own_specs = [rider_spec(name, False) for name in names]
    res = pl.pallas_call(
        body, name="dw_in", grid=(2, 1, nk),
        in_specs=[pl.BlockSpec((tk, tm), lambda i, j, k: (k, i)), pl.BlockSpec((tk, D_MODEL), lambda i, j, k: (k, 0))]
        + g_specs * 2 + own_specs * 3,
        out_specs=[pl.BlockSpec((tm, D_MODEL), lambda i, j, k: (i, 0))] + [s for s in own_specs for _ in range(4)],
        out_shape=[jax.ShapeDtypeStruct((rows_out, D_MODEL), F32)]
        + [jax.ShapeDtypeStruct(SHARD_SHAPES[name], F32) for name in names for _ in range(4)],
        compiler_params=_params(("arbitrary", "arbitrary", "arbitrary")),
    )(dproj, h, *[g_parts[0]] * n, *[g_parts[1]] * n, *[w[k] for k in names], *[m[k] for k in names],
      *[v[k] for k in names])
    return res[0], {name: res[1 + 4 * j:5 + 4 * j] for j, name in enumerate(names)}


def _adamw(w, g_parts, m, v, name, block, g_row_off=0):
    r, c = w.shape
    br, bc = block
    ng = len(g_parts)

    def body(*refs):
        w_ref, g_refs, m_ref, v_ref = refs[0], refs[1:1 + ng], refs[1 + ng], refs[2 + ng]
        g = g_refs[0][...]
        for gr in g_refs[1:]:
            g = g + gr[...]
        _adamw_store(w_ref[...], g, m_ref[...], v_ref[...], refs[3 + ng:])

    assert g_row_off % br == 0 and r % br == 0 and c % bc == 0
    blk = pl.BlockSpec(block, lambda i, j: (i, j))
    g_blk = pl.BlockSpec(block, lambda i, j: (i + g_row_off // br, j))
    return pl.pallas_call(
        body, name=name, grid=(r // br, c // bc),
        in_specs=[blk] + [g_blk] * ng + [blk, blk], out_specs=[blk] * 4,
        out_shape=[jax.ShapeDtypeStruct((r, c), F32)] * 4,
        compiler_params=_params(("parallel", "parallel")),
    )(w, *g_parts, m, v)


_HBM = pl.BlockSpec(memory_space=pltpu.HBM)


def _other_chips(x, y):
    return ((1 - x, y), (x, 1 - y), (1 - x, 1 - y))


def _gather_copies(src, out, send_sems, recv_sems, local_sem):
    x, y, c = lax.axis_index("x"), lax.axis_index("y"), lax.axis_index("c")
    me = 2 * x + y
    local = pltpu.make_async_copy(src, out.at[me], local_sem)

    def copies(arriving):
        return [pltpu.make_async_remote_copy(src_ref=src, dst_ref=out.at[2 * px + py if arriving else me],
                                             send_sem=send_sems.at[j], recv_sem=recv_sems.at[j], device_id=(px, py, c),
                                             device_id_type=MESH)
                for j, (px, py) in enumerate(_other_chips(x, y))]

    return local, copies


def _gather_start(src, out, send_sems, recv_sems, local_sem):
    local, copies = _gather_copies(src, out, send_sems, recv_sems, local_sem)
    local.start()
    for cp in copies(False):
        cp.start()


def _gather_wait(src, out, send_sems, recv_sems, local_sem):
    local, copies = _gather_copies(src, out, send_sems, recv_sems, local_sem)
    for cp in copies(True):
        cp.wait_recv()
    for cp in copies(False):
        cp.wait_send()
    local.wait()


def _scatter_copies(src, land, send_sems, recv_sems):
    x, y, c = lax.axis_index("x"), lax.axis_index("y"), lax.axis_index("c")
    return [pltpu.make_async_remote_copy(src_ref=src.at[2 * px + py], dst_ref=land.at[j], send_sem=send_sems.at[j],
                                         recv_sem=recv_sems.at[j], device_id=(px, py, c), device_id_type=MESH)
            for j, (px, py) in enumerate(_other_chips(x, y))]


def _scatter_start(src, land, send_sems, recv_sems):
    for cp in _scatter_copies(src, land, send_sems, recv_sems):
        cp.start()


def _scatter_wait(src, land, send_sems, recv_sems):
    copies = _scatter_copies(src, land, send_sems, recv_sems)
    for cp in copies:
        cp.wait_recv()
    for cp in copies:
        cp.wait_send()


def _all_gather_chips(packed):
    r = packed.shape[0]
    half = r // 2

    def body(src, out, ici_send, ici_recv, d2d_send, d2d_recv, local_sem):
        x, y, c = lax.axis_index("x"), lax.axis_index("y"), lax.axis_index("c")
        me = 2 * x + y
        mine = pl.ds(pl.multiple_of(c * half, 16), half)
        theirs = pl.ds(pl.multiple_of((1 - c) * half, 16), half)
        chips = _other_chips(x, y)
        local = pltpu.make_async_copy(src, out.at[me], local_sem)
        local.start()
        sends = [pltpu.make_async_remote_copy(src_ref=src.at[mine], dst_ref=out.at[me, mine], send_sem=ici_send.at[j],
                                              recv_sem=ici_recv.at[j], device_id=(px, py, c), device_id_type=MESH)
                 for j, (px, py) in enumerate(chips)]
        for cp in sends:
            cp.start()
        passed = []
        for j, (px, py) in enumerate(chips):
            block = 2 * px + py
            pltpu.make_async_remote_copy(src_ref=src.at[mine], dst_ref=out.at[block, mine], send_sem=ici_send.at[j],
                                         recv_sem=ici_recv.at[j], device_id=(px, py, c), device_id_type=MESH).wait_recv()
            cp = pltpu.make_async_remote_copy(src_ref=out.at[block, mine], dst_ref=out.at[block, mine],
                                              send_sem=d2d_send.at[j], recv_sem=d2d_recv.at[j],
                                              device_id=(x, y, 1 - c), device_id_type=MESH)
            cp.start()
            passed.append(cp)
        for j, (px, py) in enumerate(chips):
            block = 2 * px + py
            pltpu.make_async_remote_copy(src_ref=out.at[block, theirs], dst_ref=out.at[block, theirs],
                                         send_sem=d2d_send.at[j], recv_sem=d2d_recv.at[j],
                                         device_id=(x, y, 1 - c), device_id_type=MESH).wait_recv()
        for cp in sends + passed:
            cp.wait_send()
        local.wait()

    sems = pltpu.SemaphoreType.DMA((3,))
    return pl.pallas_call(
        body, name="ag_weights", in_specs=[_HBM], out_specs=_HBM,
        out_shape=jax.ShapeDtypeStruct((N_CHIPS,) + packed.shape, packed.dtype),
        scratch_shapes=[sems, sems, sems, sems, pltpu.SemaphoreType.DMA(())],
    )(packed)


def _sum4(gp, land, chip, name):
    _, r, w = gp.shape
    tr = 256 if r % 256 == 0 else 128

    def body(chip_ref, o_ref, l_ref, s_ref):
        s_ref[...] = ((o_ref[0] + l_ref[0].astype(F32)) + l_ref[1].astype(F32)) + l_ref[2].astype(F32)

    return pl.pallas_call(
        body, name=name,
        grid_spec=pltpu.PrefetchScalarGridSpec(
            num_scalar_prefetch=1, grid=(r // tr,),
            in_specs=[pl.BlockSpec((1, tr, w), lambda i, chip_ref: (chip_ref[0], i, 0)),
                      pl.BlockSpec((3, tr, w), lambda i, chip_ref: (0, i, 0))],
            out_specs=pl.BlockSpec((tr, w), lambda i, chip_ref: (i, 0))),
        out_shape=jax.ShapeDtypeStruct((r, w), F32),
        compiler_params=_params(("parallel",)),
    )(chip, gp, land)


def _sibling_copy(src, got, send_sem, recv_sem):
    x, y, c = lax.axis_index("x"), lax.axis_index("y"), lax.axis_index("c")
    return pltpu.make_async_remote_copy(src_ref=src, dst_ref=got, send_sem=send_sem, recv_sem=recv_sem,
                                        device_id=(x, y, 1 - c), device_id_type=MESH)


def _swap_sibling(s, name):
    def body(src, got, send_sem, recv_sem):
        cp = _sibling_copy(src, got, send_sem, recv_sem)
        cp.start()
        cp.wait_recv()
        cp.wait_send()

    return pl.pallas_call(
        body, name=name, in_specs=[_HBM], out_specs=_HBM,
        out_shape=jax.ShapeDtypeStruct(s.shape, s.dtype),
        scratch_shapes=[pltpu.SemaphoreType.DMA(()), pltpu.SemaphoreType.DMA(())],
    )(s)


def _all_reduce_small(dsmall, loss):
    n_dev = 8
    names = tuple(_SMALL_SLOTS)
    shape = (8, D_MODEL)

    def body(*refs):
        parts, loss_ref = refs[:len(names)], refs[len(names)]
        out, src, gath, send_sems, recv_sems = refs[len(names) + 1:]
        x, y, c = lax.axis_index("x"), lax.axis_index("y"), lax.axis_index("c")
        me = 4 * x + 2 * y + c
        src[...] = jnp.zeros(shape, F32)
        for name, part in zip(names, parts):
            row, lane, _ = _SMALL_SLOTS[name]
            src[row:row + 1, lane:lane + part.shape[1]] = part[...]
        src[_LOSS_ROW:_LOSS_ROW + 1, 0:LANES] = loss_ref[...]
        gath[me] = src[...]
        peers = []
        for k in range(1, n_dev):
            px = 1 - x if (k >> 2) & 1 else x
            py = 1 - y if (k >> 1) & 1 else y
            pc = 1 - c if k & 1 else c
            peers.append((px, py, pc))
        sends = []
        for j, peer in enumerate(peers):
            cp = pltpu.make_async_remote_copy(src_ref=src, dst_ref=gath.at[me], send_sem=send_sems.at[j],
                                              recv_sem=recv_sems.at[j], device_id=peer, device_id_type=MESH)
            cp.start()
            sends.append(cp)
        for j, (px, py, pc) in enumerate(peers):
            pltpu.make_async_remote_copy(src_ref=src, dst_ref=gath.at[4 * px + 2 * py + pc], send_sem=send_sems.at[j],
                                         recv_sem=recv_sems.at[j], device_id=(px, py, pc), device_id_type=MESH).wait_recv()
        for cp in sends:
            cp.wait_send()
        acc = gath[0]
        for d in range(1, n_dev):
            acc = acc + gath[d]
        out[...] = acc

    vmem = pl.BlockSpec(memory_space=pltpu.VMEM)
    return pl.pallas_call(
        body, name="ar_small", in_specs=[vmem] * (len(names) + 1), out_specs=vmem,
        out_shape=jax.ShapeDtypeStruct(shape, F32),
        scratch_shapes=[pltpu.VMEM(shape, F32), pltpu.VMEM((n_dev,) + shape, F32),
                        pltpu.SemaphoreType.DMA((n_dev - 1,)), pltpu.SemaphoreType.DMA((n_dev - 1,))],
    )(*[dsmall[k] for k in names], loss)


_W_IN_ROWS = SHARD_SHAPES["w_in"][1]
_KR_ROW = 3200
_KR_PAD_ROW = _KR_BLK * LANES + QK_NOPE


def _shard_rows(name, a):
    return jnp.transpose(a) if name == "w_in" else a.reshape(PACK_ROWS[name], D_MODEL)


def _pack(group, shards, dtype):
    parts = [_shard_rows(n, shards[n]).astype(dtype) for n in group]
    pad = -sum(PACK_ROWS[n] for n in group) % LANES
    if pad:
        parts.append(jnp.zeros((pad, D_MODEL), dtype))
    return jnp.concatenate(parts, axis=0)


def _col_sharded_full(g, name, group):
    r, c = SHARD_SHAPES[name]
    off = _row_offset(group, name)
    blocks = g[:, off:off + PACK_ROWS[name]].reshape(N_CHIPS, r, c)
    return jnp.transpose(blocks, (1, 0, 2)).reshape(r, N_CHIPS * c)


def _col_sharded_blocks(d, name):
    r, c = SHARD_SHAPES[name]
    return jnp.transpose(d.reshape(r, N_CHIPS, c), (1, 0, 2)).reshape(N_CHIPS, PACK_ROWS[name], D_MODEL)


def _weights_a(g):
    dt = g.dtype
    w_in_t = g[:, :_W_IN_ROWS].reshape(N_CHIPS * _W_IN_ROWS, D_MODEL)
    z = lambda n: jnp.zeros((n, D_MODEL), dt)
    w_in_t = jnp.concatenate([w_in_t[:_KR_ROW], z(_KR_PAD_ROW - _KR_ROW), w_in_t[_KR_ROW:],
                              z(D_IN_PAD - _KR_PAD_ROW - QK_ROPE)], axis=0)
    wq = _col_sharded_full(g, "w_q_b", GROUP_A).reshape(Q_LORA, N_HEADS_B, Q_HEAD_B)
    wq_p = jnp.concatenate([wq, jnp.zeros((Q_LORA, N_HEADS_B, HEAD_PAD - Q_HEAD_B), dt)], axis=2).reshape(Q_LORA, MLA_W)
    wkv = _col_sharded_full(g, "w_kv_b", GROUP_A).reshape(KV_LORA, N_HEADS_B, QK_NOPE + V_DIM_B)
    zk = jnp.zeros((KV_LORA, N_HEADS_B, HEAD_PAD - QK_NOPE), dt)
    wk_p = jnp.concatenate([wkv[:, :, :QK_NOPE], zk], axis=2).reshape(KV_LORA, MLA_W)
    wv = wkv[:, :, QK_NOPE:].reshape(KV_LORA, N_HEADS_B * V_DIM_B)
    return dict(w_in=w_in_t, wq=wq_p, wk=wk_p, wv=wv, wv_t=jnp.transpose(wv))


def _grad_blocks_a(dw_in_t, dwq_p, dwk_p, dwv):
    dw_in = jnp.concatenate([dw_in_t[:_KR_ROW], dw_in_t[_KR_PAD_ROW:_KR_PAD_ROW + QK_ROPE]], axis=0)
    dwq = dwq_p.reshape(Q_LORA, N_HEADS_B, HEAD_PAD)[:, :, :Q_HEAD_B].reshape(Q_LORA, N_HEADS_B * Q_HEAD_B)
    dwk = dwk_p.reshape(KV_LORA, N_HEADS_B, HEAD_PAD)[:, :, :QK_NOPE]
    dwkv = jnp.concatenate([dwk, dwv.reshape(KV_LORA, N_HEADS_B, V_DIM_B)], axis=2)
    dwkv = dwkv.reshape(KV_LORA, N_HEADS_B * (QK_NOPE + V_DIM_B))
    pad = -sum(PACK_ROWS[n] for n in GROUP_A) % LANES
    return jnp.concatenate([dw_in.reshape(N_CHIPS, _W_IN_ROWS, D_MODEL), _col_sharded_blocks(dwq, "w_q_b"),
                            _col_sharded_blocks(dwkv, "w_kv_b"), jnp.zeros((N_CHIPS, pad, D_MODEL), F32)], axis=1)


def _rope_freq_lanes():
    freqs = ROPE_THETA ** (-jnp.arange(0, QK_ROPE, 2, dtype=F32) / QK_ROPE)
    return jnp.concatenate([jnp.zeros((QK_NOPE,), F32), freqs, freqs,
                            jnp.zeros((HEAD_PAD - Q_HEAD_B,), F32)]).reshape(1, LANES)


def _fwd_bwd(x, positions, target, w, m, v):
    t = x.shape[0]
    wa = _weights_a(_all_gather_chips(_pack(GROUP_A, w, BF16)))
    posr = positions.astype(F32).reshape(1, t)
    posc = posr.reshape(t, 1)
    freq = _rope_freq_lanes()
    g1, g2, g3, g4 = w["pre_norm_mix"], w["post_norm_mix"], w["pre_norm_mlp"], w["post_norm_mlp"]
    qan, kvan, sinks = w["q_a_norm"], w["kv_a_norm"], w["sinks"]

    h, proj = _proj_fwd(x, g1, wa["w_in"])
    out_a, lse_a = _swa_fwd(proj, posc, posr, sinks)
    qm, km, qt, kt, vt = _mla_prep_fwd(proj, posc, freq, qan, kvan, wa["wq"], wa["wk"], wa["wv_t"])
    out_bt, lse_b, wb = _mla_fwd(km, qt, vt, _pack(GROUP_B, w, BF16))
    w_oa, w_ob = _col_sharded_full(wb, "w_o_a", GROUP_B), _col_sharded_full(wb, "w_o_b", GROUP_B)
    merged, y, x1, h2 = _mix_out_fwd(out_a, out_bt, proj, x, w_oa, w_ob, wb, g2, g3)
    a = _up_fwd(h2, wb)
    dx2, dyd, dg4, loss = _down_fwd_loss(a, wb, x1, target, g4)

    gp_b = _dw_into_blocks(a, dyd, "w_down", 1024, _TK_DW)
    du = _down_bwd(dyd, wb, a)
    gp_b = _dw_into_blocks(h2, du, "w_up", 1024, _TK_DW, gp_b)
    dx1, dy, dg3, dg2 = _up_bwd(du, wb, x1, dx2, y, g3, g2)
    gp_b = _dw_into_blocks(merged, dy, "w_out", 1024, _TK_DW, gp_b)
    doa, dob, dga, dgb, d_out_a, d_out_b, d_out_bt, del_a, del_b = _mix_out_bwd(dy, out_a, out_bt, proj, w_oa, w_ob, wb)
    dw_oa = _matmul_tn(out_a, doa, "dw_o_a", 512, 1024)
    dw_ob = _dw_ob(out_bt, dob)
    small_b = jnp.concatenate([_col_sharded_blocks(dw_oa, "w_o_a"), _col_sharded_blocks(dw_ob, "w_o_b")], axis=1)
    gp_b = lax.dynamic_update_slice(gp_b, small_b, (0, _row_offset(GROUP_B, "w_o_a"), 0))
    dqm, dkm, dvm, land_b = _mla_bwd(qm, km, qt, kt, vt, d_out_b, d_out_bt, lse_b, del_b, gp_b)
    chip = (2 * lax.axis_index("x") + lax.axis_index("y")).astype(jnp.int32).reshape(1)
    part_b = _sum4(gp_b, land_b, chip, "rs_sum_b")
    dcq, dckv, dkr, dwq, dwk, dwv, dqan, dkvan, sib_b = _mla_prep_bwd(
        dqm, dkm, dvm, proj, posc, freq, qan, kvan, wa["wq"], wa["wk"], wa["wv"], part_b)
    dqa, dka, dva, dsinks = _swa_bwd(proj, d_out_a, lse_a, del_a, posc, posr, sinks)
    dproj = jnp.concatenate([dga, dgb, dqa.astype(BF16), dka.astype(BF16), dva.astype(BF16), dcq, dckv, dkr], axis=1)
    dw_in_t, updated = _dw_in_adamw(dproj, h, [part_b, sib_b], w, m, v)
    gp_a = _grad_blocks_a(dw_in_t, dwq, dwk, dwv)
    grad_x, dg1, land_a = _in_bwd(dproj, wa["w_in"], x, dx1, g1, gp_a.astype(BF16))

    part_a = _sum4(gp_a, land_a, chip, "rs_sum_a")
    reduced = {GROUP_A: [part_a, _swap_sibling(part_a, "rs_swap_a")], GROUP_B: [part_b, sib_b]}
    dsmall = dict(pre_norm_mix=dg1, post_norm_mix=dg2, pre_norm_mlp=dg3, post_norm_mlp=dg4,
                  q_a_norm=dqan, kv_a_norm=dkvan, sinks=dsinks)
    return loss, grad_x, reduced, dsmall, updated


def kernel(x, positions, pre_norm_mix, w_in, q_a_norm, w_q_b, kv_a_norm, w_kv_b, sinks, w_o_a, w_o_b, w_out, post_norm_mix, pre_norm_mlp, w_up, w_down, post_norm_mlp, loss_target, m_pre_norm_mix, m_w_in, m_q_a_norm, m_w_q_b, m_kv_a_norm, m_w_kv_b, m_sinks, m_w_o_a, m_w_o_b, m_w_out, m_post_norm_mix, m_pre_norm_mlp, m_w_up, m_w_down, m_post_norm_mlp, v_pre_norm_mix, v_w_in, v_q_a_norm, v_w_q_b, v_kv_a_norm, v_w_kv_b, v_sinks, v_w_o_a, v_w_o_b, v_w_out, v_post_norm_mix, v_pre_norm_mlp, v_w_up, v_w_down, v_post_norm_mlp):
    w = dict(pre_norm_mix=pre_norm_mix, w_in=w_in[0], q_a_norm=q_a_norm, w_q_b=w_q_b[0], kv_a_norm=kv_a_norm,
             w_kv_b=w_kv_b[0], sinks=sinks, w_o_a=w_o_a[0], w_o_b=w_o_b[0], w_out=w_out[0],
             post_norm_mix=post_norm_mix, pre_norm_mlp=pre_norm_mlp, w_up=w_up[0], w_down=w_down[0],
             post_norm_mlp=post_norm_mlp)
    m = dict(pre_norm_mix=m_pre_norm_mix, w_in=m_w_in[0], q_a_norm=m_q_a_norm, w_q_b=m_w_q_b[0],
             kv_a_norm=m_kv_a_norm, w_kv_b=m_w_kv_b[0], sinks=m_sinks, w_o_a=m_w_o_a[0], w_o_b=m_w_o_b[0],
             w_out=m_w_out[0], post_norm_mix=m_post_norm_mix, pre_norm_mlp=m_pre_norm_mlp, w_up=m_w_up[0],
             w_down=m_w_down[0], post_norm_mlp=m_post_norm_mlp)
    v = dict(pre_norm_mix=v_pre_norm_mix, w_in=v_w_in[0], q_a_norm=v_q_a_norm, w_q_b=v_w_q_b[0],
             kv_a_norm=v_kv_a_norm, w_kv_b=v_w_kv_b[0], sinks=v_sinks, w_o_a=v_w_o_a[0], w_o_b=v_w_o_b[0],
             w_out=v_w_out[0], post_norm_mix=v_post_norm_mix, pre_norm_mlp=v_pre_norm_mlp, w_up=v_w_up[0],
             w_down=v_w_down[0], post_norm_mlp=v_post_norm_mlp)

    loss, grad_x, reduced, dsmall, updated = _fwd_bwd(x[0], positions, loss_target[0], w, m, v)

    red = _all_reduce_small(dsmall, loss)
    small = _adamw_small(red, w, m, v)

    big = {}
    tr = jnp.transpose
    big["w_in"] = [tr(o)[None] for o in _adamw(tr(w["w_in"]), reduced[GROUP_A], tr(m["w_in"]), tr(v["w_in"]),
                                               "adamw_w_in", (_W_IN_ROWS, 256))]
    for n in _ADAMW_RIDERS:
        big[n] = [o[None] for o in updated[n]]
    for group, names in ((GROUP_A, ("w_q_b", "w_kv_b")), (GROUP_B, ("w_o_a", "w_o_b"))):
        for n in names:
            off = _row_offset(group, n)
            g_parts = [p[off:off + PACK_ROWS[n]].reshape(SHARD_SHAPES[n]) for p in reduced[group]]
            big[n] = [o[None] for o in _adamw(w[n], g_parts, m[n], v[n], "adamw_" + n, SHARD_SHAPES[n])]

    outs = [big[n][k] if n in big else small[n][k] for k in range(4) for n in WEIGHTS]
    return (red[_LOSS_ROW, 0], grad_x[None], *outs)
```

```python
import jax
import jax.numpy as jnp
from jax import lax
from jax.experimental import pallas as pl
from jax.experimental.pallas import tpu as pltpu

F32 = jnp.float32
BF16 = jnp.bfloat16
MESH = pl.DeviceIdType.MESH

D_MODEL = 1024
N_HEADS_A = 8
N_KV_A = 2
HEAD_DIM_A = 64
WINDOW = 128
BLOCK = 128
N_HEADS_B = 8
QK_NOPE = 64
QK_ROPE = 32
V_DIM_B = 64
Q_LORA = 256
KV_LORA = 128
ROPE_THETA = 10000.0
D_FF = 4 * D_MODEL
EPS = 1e-6
WIDTH_A = N_HEADS_A * HEAD_DIM_A
Q_HEAD_B = QK_NOPE + QK_ROPE
D_IN_PAD = 3328
HEAD_PAD = 128
MLA_W = N_HEADS_B * HEAD_PAD

ADAM_LR = 0.001
ADAM_B1 = 0.9
ADAM_B2 = 0.999
ADAM_EPS = 1e-08
ADAM_WD = 0.01
ADAM_STEP = 10

NEG = -1e30
N_CHIPS = 4
LANES = 128
VMEM_LIMIT = 56 * 1024 * 1024

SHARD_SHAPES = {"w_in": (1024, 808), "w_q_b": (256, 192), "w_kv_b": (128, 256), "w_o_a": (512, 256),
                "w_o_b": (512, 256), "w_out": (256, 1024), "w_up": (1024, 1024), "w_down": (1024, 1024)}
PACK_ROWS = {n: (s[0] * s[1]) // D_MODEL for n, s in SHARD_SHAPES.items()}
GROUP_A = ("w_in", "w_q_b", "w_kv_b")
GROUP_B = ("w_up", "w_down", "w_out", "w_o_a", "w_o_b")
WEIGHTS = ("pre_norm_mix", "w_in", "q_a_norm", "w_q_b", "kv_a_norm", "w_kv_b", "sinks", "w_o_a", "w_o_b", "w_out",
           "post_norm_mix", "pre_norm_mlp", "w_up", "w_down", "post_norm_mlp")


def _params(sem=None):
    return pltpu.CompilerParams(dimension_semantics=sem, vmem_limit_bytes=VMEM_LIMIT)


def _dot(a, b):
    return jnp.dot(a, b, preferred_element_type=F32)


def _dot_nt(a, b):
    return lax.dot_general(a, b, (((1,), (1,)), ((), ())), preferred_element_type=F32)


def _dot_tn(a, b):
    return lax.dot_general(a, b, (((0,), (0,)), ((), ())), preferred_element_type=F32)


def _rms(v):
    return lax.rsqrt(jnp.mean(v * v, axis=-1, keepdims=True) + EPS)


def _norm_bwd(dout, n, r, g):
    dn = dout * g
    dx = r * (dn - n * jnp.mean(dn * n, axis=-1, keepdims=True))
    return dx, jnp.sum(dout * n, axis=0, keepdims=True)


def _full(shape):
    return pl.BlockSpec(shape, lambda *_: (0,) * len(shape))


def _row_offset(group, name):
    return sum(PACK_ROWS[n] for n in group[:group.index(name)])


def _wb_spec(name):
    rows = PACK_ROWS[name]
    return pl.BlockSpec((N_CHIPS, rows, D_MODEL), lambda *_: (0, _row_offset(GROUP_B, name) // rows, 0))


def _proj_fwd(x, g1, w_in_t):
    t = x.shape[0]
    tm = 512

    def body(x_ref, g_ref, w_ref, h_ref, p_ref):
        for rows in _row_halves(tm):
            xv = x_ref[rows, :]
            h = ((xv * _rms(xv)) * g_ref[...]).astype(BF16)
            h_ref[rows, :] = h
            p_ref[rows, :] = _dot_nt(h, w_ref[...])

    return pl.pallas_call(
        body, name="proj_fwd", grid=(t // tm,),
        in_specs=[pl.BlockSpec((tm, D_MODEL), lambda i: (i, 0)), _full((1, D_MODEL)), _full((D_IN_PAD, D_MODEL))],
        out_specs=[pl.BlockSpec((tm, D_MODEL), lambda i: (i, 0)), pl.BlockSpec((tm, D_IN_PAD), lambda i: (i, 0))],
        out_shape=[jax.ShapeDtypeStruct((t, D_MODEL), BF16), jax.ShapeDtypeStruct((t, D_IN_PAD), F32)],
        compiler_params=_params(("parallel",)),
    )(x, g1, w_in_t)


_QA_BLK = 2048 // WIDTH_A
_KA_BLK = 2560 // LANES
_VA_BLK = 2688 // LANES
_CQ_BLK = 2816 // Q_LORA
_CKV_BLK = 3072 // LANES
_KR_BLK = 3200 // LANES


_GROUP_A = N_HEADS_A // N_KV_A
_SWA_SCALE = HEAD_DIM_A ** -0.5
_LOG2E = 1.4426950408889634


def _head_cols(v, h):
    return v[:, HEAD_DIM_A * h:HEAD_DIM_A * (h + 1)]


def _head_rows(v, h):
    return v[HEAD_DIM_A * h:HEAD_DIM_A * (h + 1), :]


def _swa_scores_t(st_g, j, h, dist, valid):
    slope = 2.0 ** (-8.0 * (h + 1) / N_HEADS_A)
    st = st_g[:, BLOCK * j:BLOCK * (j + 1)] * (_SWA_SCALE * _LOG2E) - (slope * _LOG2E) * dist
    return jnp.where(valid, st, NEG)


def _group_t(xt, kh):
    return jnp.concatenate([_head_rows(xt, _GROUP_A * kh + j) for j in range(_GROUP_A)], axis=1).astype(BF16)


_SWA_PER_STEP = 4


def _swa_fwd(proj, posc, posr, sinks):
    t = proj.shape[0]
    span = _SWA_PER_STEP * BLOCK

    def body(q_ref, kc_ref, kp_ref, vc_ref, vp_ref, pq_ref, pc_ref, pp_ref, sink_ref, o_ref, l_ref):
        n = pl.program_id(0)
        k_all = jnp.concatenate([kp_ref[...], kc_ref[...]], axis=0)
        v_all = jnp.concatenate([vp_ref[...], vc_ref[...]], axis=0)
        pos_all = jnp.concatenate([pp_ref[...], pc_ref[...]], axis=0)
        ki = lax.broadcasted_iota(jnp.int32, (2 * BLOCK, BLOCK), 0)
        qi = lax.broadcasted_iota(jnp.int32, (2 * BLOCK, BLOCK), 1)
        window = (ki > qi) & (ki <= qi + WINDOW)
        for sub in range(_SWA_PER_STEP):
            band = slice(BLOCK * sub, BLOCK * (sub + 2))
            own = slice(BLOCK * sub, BLOCK * (sub + 1))
            kb, vb = k_all[band], v_all[band]
            dist = jnp.abs(pos_all[band] - pq_ref[:, own])
            valid = window & ((n > 0) | (ki >= BLOCK)) if sub == 0 else window
            q_t, vb_t = q_ref[own, :].T, vb.T
            out_t, lse = [], []
            for kh in range(N_KV_A):
                st_g = _dot(_head_cols(kb, kh).astype(BF16), _group_t(q_t, kh))
                ps = []
                for j in range(_GROUP_A):
                    h = _GROUP_A * kh + j
                    st = _swa_scores_t(st_g, j, h, dist, valid)
                    sink = sink_ref[0:1, h:h + 1] * _LOG2E
                    m = jnp.maximum(jnp.max(st, axis=0, keepdims=True), sink)
                    e = jnp.exp2(st - m)
                    den = jnp.sum(e, axis=0, keepdims=True) + jnp.exp2(sink - m)
                    ps.append((e * (1.0 / den)).astype(BF16))
                    lse.append(m + jnp.log(den) * _LOG2E)
                o_g = _dot(_head_rows(vb_t, kh).astype(BF16), jnp.concatenate(ps, axis=1))
                out_t.extend(o_g[:, BLOCK * j:BLOCK * (j + 1)] for j in range(_GROUP_A))
            o_ref[own, :] = jnp.concatenate(out_t, axis=0).T
            l_ref[:, own] = jnp.concatenate(lse, axis=0)

    cur = lambda n: (n, 0)
    prev = lambda n: jnp.maximum(_SWA_PER_STEP * n - 1, 0)
    return pl.pallas_call(
        body, name="swa_fwd", grid=(t // span,),
        in_specs=[pl.BlockSpec((span, WIDTH_A), lambda n: (n, _QA_BLK)),
                  pl.BlockSpec((span, LANES), lambda n: (n, _KA_BLK)),
                  pl.BlockSpec((BLOCK, LANES), lambda n: (prev(n), _KA_BLK)),
                  pl.BlockSpec((span, LANES), lambda n: (n, _VA_BLK)),
                  pl.BlockSpec((BLOCK, LANES), lambda n: (prev(n), _VA_BLK)),
                  pl.BlockSpec((1, span), lambda n: (0, n)),
                  pl.BlockSpec((span, 1), cur),
                  pl.BlockSpec((BLOCK, 1), lambda n: (prev(n), 0)),
                  _full((1, N_HEADS_A))],
        out_specs=[pl.BlockSpec((span, WIDTH_A), cur), pl.BlockSpec((N_HEADS_A, span), lambda n: (0, n))],
        out_shape=[jax.ShapeDtypeStruct((t, WIDTH_A), F32), jax.ShapeDtypeStruct((N_HEADS_A, t), F32)],
        compiler_params=_params(("parallel",)),
    )(proj, proj, proj, proj, proj, posr, posc, posc, sinks)


def _rope_coeffs(pos, freq):
    ang = pos * freq
    cosv, sinv = jnp.cos(ang), jnp.sin(ang)
    lane = lax.broadcasted_iota(jnp.int32, ang.shape, 1)
    lo = (lane >= QK_NOPE) & (lane < QK_NOPE + QK_ROPE // 2)
    hi = (lane >= QK_NOPE + QK_ROPE // 2) & (lane < QK_NOPE + QK_ROPE)
    c = jnp.where(lane < QK_NOPE, 1.0, jnp.where(lo | hi, cosv, 0.0))
    s = jnp.where(lo, -sinv, jnp.where(hi, sinv, 0.0))
    return c, s, lo, hi


def _rope(xh, c, s, lo):
    up = pltpu.roll(xh, LANES - QK_ROPE // 2, axis=1)
    dn = pltpu.roll(xh, QK_ROPE // 2, axis=1)
    return xh * c + jnp.where(lo, up, dn) * s


def _unrope(dh, c, s, lo, hi):
    g = dh * s
    up = pltpu.roll(g, LANES - QK_ROPE // 2, axis=1)
    dn = pltpu.roll(g, QK_ROPE // 2, axis=1)
    return dh * c + jnp.where(hi, dn, jnp.where(lo, up, 0.0))


_TQ = 512
_MLA_SCALE = Q_HEAD_B ** -0.5


def _mla_prep_fwd(proj, posc, freq, qan, kvan, wq, wk, wv):
    t = proj.shape[0]
    tm = _TQ
    nb = t // tm

    def body(cq_ref, ckv_ref, kr_ref, pos_ref, f_ref, qan_ref, kvan_ref, wq_ref, wk_ref, wv_ref,
             q_ref, k_ref, qt_ref, kt_ref, vt_ref):
        cq = cq_ref[...]
        cqn = ((cq * _rms(cq)) * qan_ref[...]).astype(BF16)
        ckv = ckv_ref[...]
        ckvn = ((ckv * _rms(ckv)) * kvan_ref[...]).astype(BF16)
        qb = _dot(cqn, wq_ref[...])
        kb = _dot(ckvn, wk_ref[...])
        vbt = _dot_nt(wv_ref[...], ckvn)
        c, s, lo, _ = _rope_coeffs(pos_ref[...], f_ref[...])
        kr = _rope(pltpu.roll(kr_ref[...], QK_NOPE, axis=1), c, s, lo)
        for h in range(N_HEADS_B):
            sl = slice(HEAD_PAD * h, HEAD_PAD * (h + 1))
            q_h = _rope(qb[:, sl], c, s, lo)
            k_h = kb[:, sl] + kr
            q_ref[:, sl] = q_h.astype(BF16)
            k_ref[:, sl] = k_h.astype(BF16)
            qt_ref[h, 0] = q_h.T.astype(BF16)
            kt_ref[h, 0] = k_h.T.astype(BF16)
            vt_ref[h, 0] = vbt[V_DIM_B * h:V_DIM_B * (h + 1), :].astype(BF16)

    row = lambda i: (i, 0)
    blk4 = lambda d: pl.BlockSpec((N_HEADS_B, 1, d, tm), lambda i: (0, i, 0, 0))
    return pl.pallas_call(
        body, name="mla_prep_fwd", grid=(nb,),
        in_specs=[pl.BlockSpec((tm, Q_LORA), lambda i: (i, _CQ_BLK)),
                  pl.BlockSpec((tm, LANES), lambda i: (i, _CKV_BLK)),
                  pl.BlockSpec((tm, LANES), lambda i: (i, _KR_BLK)),
                  pl.BlockSpec((tm, 1), row), _full((1, LANES)), _full((1, Q_LORA)), _full((1, KV_LORA)),
                  _full((Q_LORA, MLA_W)), _full((KV_LORA, MLA_W)), _full((N_HEADS_B * V_DIM_B, KV_LORA))],
        out_specs=[pl.BlockSpec((tm, MLA_W), row), pl.BlockSpec((tm, MLA_W), row), blk4(HEAD_PAD), blk4(HEAD_PAD),
                   blk4(V_DIM_B)],
        out_shape=[jax.ShapeDtypeStruct((t, MLA_W), BF16), jax.ShapeDtypeStruct((t, MLA_W), BF16),
                   jax.ShapeDtypeStruct((N_HEADS_B, nb, HEAD_PAD, tm), BF16),
                   jax.ShapeDtypeStruct((N_HEADS_B, nb, HEAD_PAD, tm), BF16),
                   jax.ShapeDtypeStruct((N_HEADS_B, nb, V_DIM_B, tm), BF16)],
        compiler_params=_params(("parallel",)),
    )(proj, proj, proj, posc, freq, qan, kvan, wq, wk, wv)


_MLA_SCALE2 = _MLA_SCALE * _LOG2E


def _mla_fwd(k, qt, vt, w_src):
    t = k.shape[0]
    nb = t // _TQ
    groups = nb // 2

    def body(k_ref, qt_ref, vt_ref, w_ref, o_ref, l_ref, wg_ref, raw, send_sems, recv_sems, local_sem):
        g = pl.program_id(1)
        first = (pl.program_id(0) == 0) & (g == 0)
        last = (pl.program_id(0) == N_HEADS_B - 1) & (g == groups - 1)

        @pl.when(first)
        def _():
            _gather_start(w_ref, wg_ref, send_sems, recv_sems, local_sem)

        def keys(kj):
            return k_ref[pl.ds(pl.multiple_of(kj * _TQ, _TQ), _TQ), :]

        def products(kj, slot):
            kv = keys(kj)
            raw[slot, 0] = _dot(kv, qt_ref[0, 0])
            raw[slot, 1] = _dot(kv, qt_ref[0, 1])

        def update(stats, raw_ref, kj, diagonal=False):
            m, l, acc = stats
            scores = raw_ref[...]
            if diagonal:
                key = lax.broadcasted_iota(jnp.int32, scores.shape, 0)
                qry = lax.broadcasted_iota(jnp.int32, scores.shape, 1)
                scores = jnp.where(key <= qry, scores, NEG)
            m_new = jnp.maximum(m, jnp.max(scores, axis=0, keepdims=True) * _MLA_SCALE2)
            alpha = jnp.exp2(m - m_new)
            p = jnp.exp2(scores * _MLA_SCALE2 - m_new).astype(BF16)
            pv = _dot(jnp.concatenate([vt_ref[0, kj], jnp.ones((16, _TQ), BF16)], axis=0), p)
            return m_new, alpha * l + pv[V_DIM_B:V_DIM_B + 8], alpha * acc + pv[:V_DIM_B]

        def trip(i, stats):
            sa, sb = stats
            products(2 * i + 1, 1)
            sa, sb = update(sa, raw.at[0, 0], 2 * i), update(sb, raw.at[0, 1], 2 * i)
            products(2 * i + 2, 0)
            return update(sa, raw.at[1, 0], 2 * i + 1), update(sb, raw.at[1, 1], 2 * i + 1)

        init = (jnp.full((1, _TQ), NEG, F32), jnp.zeros((8, _TQ), F32), jnp.zeros((V_DIM_B, _TQ), F32))
        products(0, 0)
        sa, sb = lax.fori_loop(0, g, trip, (init, init))
        raw[1, 1] = _dot(keys(2 * g + 1), qt_ref[0, 1])
        sa = update(sa, raw.at[0, 0], 2 * g, True)
        sb = update(update(sb, raw.at[0, 1], 2 * g), raw.at[1, 1], 2 * g + 1, True)
        for which, (m, l, acc) in enumerate((sa, sb)):
            o_ref[0, which] = acc / l[0:1]
            l_ref[0, which] = m + jnp.log(l[0:1]) * _LOG2E

        @pl.when(last)
        def _():
            _gather_wait(w_ref, wg_ref, send_sems, recv_sems, local_sem)

    two = lambda d: pl.BlockSpec((1, 2, d, _TQ), lambda h, g: (h, g, 0, 0))
    return pl.pallas_call(
        body, name="mla_fwd", grid=(N_HEADS_B, groups),
        in_specs=[pl.BlockSpec((t, HEAD_PAD), lambda h, g: (0, h)), two(HEAD_PAD),
                  pl.BlockSpec((1, nb, V_DIM_B, _TQ), lambda h, g: (h, 0, 0, 0)), _HBM],
        out_specs=[two(V_DIM_B), two(1), _HBM],
        out_shape=[jax.ShapeDtypeStruct((N_HEADS_B, nb, V_DIM_B, _TQ), F32),
                   jax.ShapeDtypeStruct((N_HEADS_B, nb, 1, _TQ), F32),
                   jax.ShapeDtypeStruct((N_CHIPS,) + w_src.shape, w_src.dtype)],
        scratch_shapes=[pltpu.VMEM((2, 2, _TQ, _TQ), F32),
                        pltpu.SemaphoreType.DMA((3,)), pltpu.SemaphoreType.DMA((3,)), pltpu.SemaphoreType.DMA(())],
        compiler_params=_params(("arbitrary", "arbitrary")),
    )(k, qt, vt, w_src)


def _ot_spec(tm, d):
    per = _TQ // tm
    return pl.BlockSpec((N_HEADS_B, 1, d, tm), lambda i: (0, i // per, 0, i % per))


def _mix_out_fwd(out_a, out_bt, proj, x, w_oa, w_ob, wb, g2, g3):
    t = x.shape[0]
    tm = 512

    def body(oa_ref, obt_ref, ga_ref, gb_ref, x_ref, woa_ref, wob_ref, wout_ref, g2_ref, g3_ref,
             mg_ref, y_ref, x1_ref, h2_ref):
        oa = _dot(oa_ref[...].astype(BF16), woa_ref[...])
        obt = obt_ref[...].reshape(N_HEADS_B * V_DIM_B, tm).astype(BF16)
        ob = _dot_tn(obt, wob_ref[...])
        merged = (jax.nn.sigmoid(ga_ref[...]) * oa + jax.nn.sigmoid(gb_ref[...]) * ob).astype(BF16)
        mg_ref[...] = merged
        y = _dot(merged, wout_ref[...].reshape(D_MODEL, D_MODEL))
        y_ref[...] = y
        x1 = x_ref[...] + (y * _rms(y)) * g2_ref[...]
        x1_ref[...] = x1
        h2_ref[...] = ((x1 * _rms(x1)) * g3_ref[...]).astype(BF16)

    row = lambda i: (i, 0)
    blk = pl.BlockSpec((tm, D_MODEL), row)
    return pl.pallas_call(
        body, name="mix_out_fwd", grid=(t // tm,),
        in_specs=[pl.BlockSpec((tm, WIDTH_A), row), _ot_spec(tm, V_DIM_B), pl.BlockSpec((tm, D_MODEL), lambda i: (i, 0)),
                  pl.BlockSpec((tm, D_MODEL), lambda i: (i, 1)), blk,
                  _full((WIDTH_A, D_MODEL)), _full((N_HEADS_B * V_DIM_B, D_MODEL)), _wb_spec("w_out"),
                  _full((1, D_MODEL)), _full((1, D_MODEL))],
        out_specs=[blk, blk, blk, blk],
        out_shape=[jax.ShapeDtypeStruct((t, D_MODEL), BF16), jax.ShapeDtypeStruct((t, D_MODEL), F32),
                   jax.ShapeDtypeStruct((t, D_MODEL), F32), jax.ShapeDtypeStruct((t, D_MODEL), BF16)],
        compiler_params=_params(("parallel",)),
    )(out_a, out_bt, proj, proj, x, w_oa, w_ob, wb, g2, g3)


_TM_MLP = 512


def _row_halves(tm):
    return slice(0, tm // 2), slice(tm // 2, tm)


def _up_fwd(h2, wb):
    t = h2.shape[0]
    tm = _TM_MLP

    def body(h_ref, w_ref, a_ref):
        hv = h_ref[...]
        for j in range(N_CHIPS):
            u = _dot(hv, w_ref[j])
            a_ref[:, D_MODEL * j:D_MODEL * (j + 1)] = jnp.square(jnp.maximum(u, 0.0)).astype(BF16)

    return pl.pallas_call(
        body, name="up_fwd", grid=(t // tm,),
        in_specs=[pl.BlockSpec((tm, D_MODEL), lambda i: (i, 0)), _wb_spec("w_up")],
        out_specs=pl.BlockSpec((tm, D_FF), lambda i: (i, 0)),
        out_shape=jax.ShapeDtypeStruct((t, D_FF), BF16),
        compiler_params=_params(("parallel",)),
    )(h2, wb)


def _down_fwd_loss(a, wb, x1, target, g4):
    t = a.shape[0]
    tm = _TM_MLP

    def body(a_ref, w_ref, x1_ref, tg_ref, g_ref, dx2_ref, dyd_ref, dg_ref, loss_ref):
        @pl.when(pl.program_id(0) == 0)
        def _():
            dg_ref[...] = jnp.zeros(dg_ref.shape, F32)
            loss_ref[...] = jnp.zeros(loss_ref.shape, F32)

        yd = _dot(a_ref[...], w_ref[...].reshape(D_FF, D_MODEL))
        r = _rms(yd)
        n = yd * r
        diff = (x1_ref[...] + n * g_ref[...]) - tg_ref[...]
        loss_ref[...] += 0.5 * jnp.sum(jnp.mean(diff * diff, axis=-1, keepdims=True), axis=0, keepdims=True)
        dx2 = diff * (1.0 / D_MODEL)
        dx2_ref[...] = dx2
        dyd, dg = _norm_bwd(dx2, n, r, g_ref[...])
        dyd_ref[...] = dyd.astype(BF16)
        dg_ref[...] += dg

    row = lambda i: (i, 0)
    blk = pl.BlockSpec((tm, D_MODEL), row)
    return pl.pallas_call(
        body, name="down_fwd_loss", grid=(t // tm,),
        in_specs=[pl.BlockSpec((tm, D_FF), row), _wb_spec("w_down"), blk, blk, _full((1, D_MODEL))],
        out_specs=[blk, blk, _full((1, D_MODEL)), _full((1, LANES))],
        out_shape=[jax.ShapeDtypeStruct((t, D_MODEL), F32), jax.ShapeDtypeStruct((t, D_MODEL), BF16),
                   jax.ShapeDtypeStruct((1, D_MODEL), F32), jax.ShapeDtypeStruct((1, LANES), F32)],
        compiler_params=_params(("arbitrary",)),
    )(a, wb, x1, target, g4)


def _matmul_tn(a, b, name, tm, tn, tk=1024):
    t, m = a.shape
    n = b.shape[1]
    tk = min(tk, t)
    nk = t // tk

    def body(a_ref, b_ref, o_ref):
        @pl.when(pl.program_id(2) == 0)
        def _():
            o_ref[...] = jnp.zeros(o_ref.shape, F32)

        o_ref[...] += _dot_tn(a_ref[...].astype(BF16), b_ref[...].astype(BF16))

    return pl.pallas_call(
        body, name=name, grid=(m // tm, n // tn, nk),
        in_specs=[pl.BlockSpec((tk, tm), lambda i, j, k: (k, i)), pl.BlockSpec((tk, tn), lambda i, j, k: (k, j))],
        out_specs=pl.BlockSpec((tm, tn), lambda i, j, k: (i, j)),
        out_shape=jax.ShapeDtypeStruct((m, n), F32),
        compiler_params=_params(("parallel", "parallel", "arbitrary")),
    )(a, b)


_TK_DW = 2048


def _dw_into_blocks(a, b, weight, tm, tk, buf=None):
    t, m = a.shape
    n = b.shape[1]
    tk = min(tk, t)
    nk = t // tk
    rows = PACK_ROWS[weight]
    br = min(tm, rows)
    chips = tm // br
    first = _row_offset(GROUP_B, weight) // br
    per_chip = rows // br
    if weight == "w_up":
        out_map = lambda i, j, k: (j, first + i, 0)
    elif chips > 1:
        out_map = lambda i, j, k: (i, first, 0)
    else:
        out_map = lambda i, j, k: (i // per_chip, first + i % per_chip, 0)

    def body(a_ref, b_ref, *rest):
        o_ref = rest[-1]

        @pl.when(pl.program_id(2) == 0)
        def _():
            o_ref[...] = jnp.zeros(o_ref.shape, F32)

        o_ref[...] += _dot_tn(a_ref[...].astype(BF16), b_ref[...].astype(BF16)).reshape(o_ref.shape)

    in_specs = [pl.BlockSpec((tk, tm), lambda i, j, k: (k, i)), pl.BlockSpec((tk, D_MODEL), lambda i, j, k: (k, j))]
    operands = [a, b]
    if buf is not None:
        in_specs.append(pl.BlockSpec(memory_space=pl.ANY))
        operands.append(buf)
    total = sum(PACK_ROWS[w] for w in GROUP_B)
    return pl.pallas_call(
        body, name="dw_" + weight[2:], grid=(m // tm, n // D_MODEL, nk),
        in_specs=in_specs, out_specs=pl.BlockSpec((chips, br, D_MODEL), out_map),
        out_shape=jax.ShapeDtypeStruct((N_CHIPS, total, D_MODEL), F32),
        input_output_aliases={} if buf is None else {2: 0},
        compiler_params=_params(("parallel", "parallel", "arbitrary")),
    )(*operands)


def _down_bwd(dyd, wb, a):
    t = dyd.shape[0]
    tm = _TM_MLP

    def body(d_ref, w_ref, a_ref, du_ref):
        dv = d_ref[...]
        for j in range(N_CHIPS):
            cols = slice(D_MODEL * j, D_MODEL * (j + 1))
            av = a_ref[:, cols].astype(F32)
            relu_u = jnp.where(av > 0.0, av * lax.rsqrt(av), 0.0)
            du_ref[:, cols] = (_dot_nt(dv, w_ref[j]) * (2.0 * relu_u)).astype(BF16)

    row = lambda i: (i, 0)
    return pl.pallas_call(
        body, name="down_bwd", grid=(t // tm,),
        in_specs=[pl.BlockSpec((tm, D_MODEL), row), _wb_spec("w_down"), pl.BlockSpec((tm, D_FF), row)],
        out_specs=pl.BlockSpec((tm, D_FF), row),
        out_shape=jax.ShapeDtypeStruct((t, D_FF), BF16),
        compiler_params=_params(("parallel",)),
    )(dyd, wb, a)


def _up_bwd(du, wb, x1, dx2, y, g3, g2):
    t = du.shape[0]
    tm = _TM_MLP

    def body(du_ref, w_ref, x1_ref, dx2_ref, y_ref, g3_ref, g2_ref, dx1_ref, dy_ref, dg3_ref, dg2_ref):
        @pl.when(pl.program_id(0) == 0)
        def _():
            dg3_ref[...] = jnp.zeros(dg3_ref.shape, F32)
            dg2_ref[...] = jnp.zeros(dg2_ref.shape, F32)

        dh2 = _dot_nt(du_ref[:, 0:D_MODEL], w_ref[0])
        for j in range(1, N_CHIPS):
            dh2 = dh2 + _dot_nt(du_ref[:, D_MODEL * j:D_MODEL * (j + 1)], w_ref[j])
        x1 = x1_ref[...]
        r3 = _rms(x1)
        d3, dg3 = _norm_bwd(dh2, x1 * r3, r3, g3_ref[...])
        dx1 = dx2_ref[...] + d3
        dx1_ref[...] = dx1
        dg3_ref[...] += dg3
        y = y_ref[...]
        r2 = _rms(y)
        dy, dg2 = _norm_bwd(dx1, y * r2, r2, g2_ref[...])
        dy_ref[...] = dy.astype(BF16)
        dg2_ref[...] += dg2

    row = lambda i: (i, 0)
    blk = pl.BlockSpec((tm, D_MODEL), row)
    return pl.pallas_call(
        body, name="up_bwd", grid=(t // tm,),
        in_specs=[pl.BlockSpec((tm, D_FF), row), _wb_spec("w_up"),
                  blk, blk, blk, _full((1, D_MODEL)), _full((1, D_MODEL))],
        out_specs=[blk, blk, _full((1, D_MODEL)), _full((1, D_MODEL))],
        out_shape=[jax.ShapeDtypeStruct((t, D_MODEL), F32), jax.ShapeDtypeStruct((t, D_MODEL), BF16),
                   jax.ShapeDtypeStruct((1, D_MODEL), F32), jax.ShapeDtypeStruct((1, D_MODEL), F32)],
        compiler_params=_params(("arbitrary",)),
    )(du, wb, x1, dx2, y, g3, g2)


def _mix_out_bwd(dy, out_a, out_bt, proj, w_oa, w_ob, wb):
    t = dy.shape[0]
    tm = 256
    nb = t // _TQ

    def body(dy_ref, oa_ref, obt_ref, ga_ref, gb_ref, woa_ref, wob_ref, wout_ref,
             doa_ref, dob_ref, dga_ref, dgb_ref, da_ref, db_ref, dbt_ref, dela_ref, delb_ref):
        dm = _dot_nt(dy_ref[...], wout_ref[...].reshape(D_MODEL, D_MODEL))
        out_a_v = oa_ref[...]
        out_bt_v = obt_ref[...].reshape(N_HEADS_B * V_DIM_B, tm)
        oa = _dot(out_a_v.astype(BF16), woa_ref[...])
        ob = _dot_tn(out_bt_v.astype(BF16), wob_ref[...])
        sa, sb = jax.nn.sigmoid(ga_ref[...]), jax.nn.sigmoid(gb_ref[...])
        doa = (dm * sa).astype(BF16)
        dob = (dm * sb).astype(BF16)
        doa_ref[...] = doa
        dob_ref[...] = dob
        dga_ref[...] = (dm * oa * (sa * (1.0 - sa))).astype(BF16)
        dgb_ref[...] = (dm * ob * (sb * (1.0 - sb))).astype(BF16)
        d_out_a = _dot_nt(doa, woa_ref[...])
        da_ref[...] = d_out_a
        prod_at = (d_out_a * out_a_v).T
        dela_ref[...] = jnp.concatenate(
            [jnp.sum(_head_rows(prod_at, h), axis=0, keepdims=True) for h in range(N_HEADS_A)], axis=0)
        d_out_b = _dot_nt(dob, wob_ref[...])
        d_out_bt = _dot_nt(wob_ref[...], dob)
        prod_bt = d_out_bt * out_bt_v
        for h in range(N_HEADS_B):
            db_ref[h] = d_out_b[:, V_DIM_B * h:V_DIM_B * (h + 1)].astype(BF16)
            dbt_ref[h, 0] = d_out_bt[V_DIM_B * h:V_DIM_B * (h + 1), :].astype(BF16)
            delb_ref[h, 0] = jnp.sum(prod_bt[V_DIM_B * h:V_DIM_B * (h + 1), :], axis=0, keepdims=True)

    row = lambda i: (i, 0)
    blk = pl.BlockSpec((tm, D_MODEL), row)
    return pl.pallas_call(
        body, name="mix_out_bwd", grid=(t // tm,),
        in_specs=[blk, pl.BlockSpec((tm, WIDTH_A), row), _ot_spec(tm, V_DIM_B),
                  pl.BlockSpec((tm, D_MODEL), lambda i: (i, 0)), pl.BlockSpec((tm, D_MODEL), lambda i: (i, 1)),
                  _full((WIDTH_A, D_MODEL)), _full((N_HEADS_B * V_DIM_B, D_MODEL)), _wb_spec("w_out")],
        out_specs=[blk, blk, blk, blk, pl.BlockSpec((tm, WIDTH_A), row),
                   pl.BlockSpec((N_HEADS_B, tm, V_DIM_B), lambda i: (0, i, 0)), _ot_spec(tm, V_DIM_B),
                   pl.BlockSpec((N_HEADS_A, tm), lambda i: (0, i)), _ot_spec(tm, 1)],
        out_shape=[jax.ShapeDtypeStruct((t, D_MODEL), BF16)] * 4
        + [jax.ShapeDtypeStruct((t, WIDTH_A), F32), jax.ShapeDtypeStruct((N_HEADS_B, t, V_DIM_B), BF16),
           jax.ShapeDtypeStruct((N_HEADS_B, nb, V_DIM_B, _TQ), BF16), jax.ShapeDtypeStruct((N_HEADS_A, t), F32),
           jax.ShapeDtypeStruct((N_HEADS_B, nb, 1, _TQ), F32)],
        compiler_params=_params(("parallel",)),
    )(dy, out_a, out_bt, proj, proj, w_oa, w_ob, wb)


def _dw_ob(out_bt, dob):
    t = dob.shape[0]
    nb = t // _TQ

    def body(obt_ref, dob_ref, o_ref):
        @pl.when(pl.program_id(0) == 0)
        def _():
            o_ref[...] = jnp.zeros(o_ref.shape, F32)

        obt = obt_ref[...].reshape(N_HEADS_B * V_DIM_B, _TQ).astype(BF16)
        o_ref[...] += _dot(obt, dob_ref[...])

    return pl.pallas_call(
        body, name="dw_o_b", grid=(nb,),
        in_specs=[pl.BlockSpec((N_HEADS_B, 1, V_DIM_B, _TQ), lambda i: (0, i, 0, 0)),
                  pl.BlockSpec((_TQ, D_MODEL), lambda i: (i, 0))],
        out_specs=_full((N_HEADS_B * V_DIM_B, D_MODEL)),
        out_shape=jax.ShapeDtypeStruct((N_HEADS_B * V_DIM_B, D_MODEL), F32),
        compiler_params=_params(("arbitrary",)),
    )(out_bt, dob)


def _mla_bwd(q, k, qt, kt, vt, d_out, d_out_t, lse, delta, gp):
    t = q.shape[0]
    nb = t // _TQ

    def body(k_ref, kt_ref, vt_ref, q_ref, qt_ref, do_ref, dot_ref, lrow_ref, drow_ref, gp_ref,
             dq_ref, dkt_ref, dvt_ref, land_ref, l_rep, d_rep, send_sems, recv_sems):
        step = pl.program_id(1)
        kj = nb - 1 - step

        @pl.when((pl.program_id(0) == 0) & (step == 0))
        def _():
            _scatter_start(gp_ref, land_ref, send_sems, recv_sems)

        @pl.when(step == 0)
        def _():
            dq_ref[...] = jnp.zeros(dq_ref.shape, F32)
            for b in range(nb):
                l_rep[_TQ * b:_TQ * (b + 1), :] = jnp.broadcast_to(lrow_ref[0, b], (LANES, _TQ)).T
                d_rep[_TQ * b:_TQ * (b + 1), :] = jnp.broadcast_to(drow_ref[0, b], (LANES, _TQ)).T

        kv, k_t, v_t = k_ref[...], kt_ref[0, 0], vt_ref[0, 0]

        def rows_of(qi):
            return pl.ds(pl.multiple_of(qi * _TQ, _TQ), _TQ)

        def products(qi, diagonal=False):
            s = _dot(q_ref[rows_of(qi), :], k_t) * _MLA_SCALE2
            if diagonal:
                qry = lax.broadcasted_iota(jnp.int32, s.shape, 0)
                key = lax.broadcasted_iota(jnp.int32, s.shape, 1)
                s = jnp.where(key <= qry, s, NEG)
            return s, _dot(do_ref[0, rows_of(qi), :], v_t)

        def update(carry, prods, qi):
            dkt, dvt = carry
            s, dp = prods
            lse, delta = l_rep[rows_of(qi), :], d_rep[rows_of(qi), :]
            ps, dss = [], []
            for c in range(_TQ // LANES):
                strip = slice(LANES * c, LANES * (c + 1))
                p = jnp.exp2(s[:, strip] - lse)
                ps.append(p.astype(BF16))
                dss.append((p * (dp[:, strip] - delta) * _MLA_SCALE).astype(BF16))
            p_b, ds_b = jnp.concatenate(ps, axis=1), jnp.concatenate(dss, axis=1)
            dvt = dvt + _dot(dot_ref[0, qi], p_b)
            dkt = dkt + _dot(qt_ref[0, qi], ds_b)
            dq_ref[rows_of(qi), :] += _dot(ds_b, kv)
            return dkt, dvt

        def pair(i, carry):
            qa = kj + 1 + 2 * i
            pa, pb = products(qa), products(qa + 1)
            return update(update(carry, pa, qa), pb, qa + 1)

        init = (jnp.zeros((HEAD_PAD, _TQ), F32), jnp.zeros((V_DIM_B, _TQ), F32))
        carry = update(init, products(kj, True), kj)
        pairs = (nb - 1 - kj) // 2
        carry = lax.fori_loop(0, pairs, pair, carry)
        dkt, dvt = lax.fori_loop(kj + 1 + 2 * pairs, nb, lambda qi, cr: update(cr, products(qi), qi), carry)
        dkt_ref[0, 0] = dkt
        dvt_ref[0, 0] = dvt

        @pl.when((pl.program_id(0) == N_HEADS_B - 1) & (step == nb - 1))
        def _():
            _scatter_wait(gp_ref, land_ref, send_sems, recv_sems)

    head4 = lambda d: pl.BlockSpec((1, nb, d, _TQ), lambda h, s: (h, 0, 0, 0))
    blk4 = lambda d: pl.BlockSpec((1, 1, d, _TQ), lambda h, s: (h, nb - 1 - s, 0, 0))
    head3 = lambda d: pl.BlockSpec((1, t, d), lambda h, kj: (h, 0, 0))
    per_head = pl.BlockSpec((t, HEAD_PAD), lambda h, kj: (0, h))
    return pl.pallas_call(
        body, name="mla_bwd", grid=(N_HEADS_B, nb),
        in_specs=[pl.BlockSpec((_TQ, HEAD_PAD), lambda h, s: (nb - 1 - s, h)), blk4(HEAD_PAD), blk4(V_DIM_B),
                  per_head, head4(HEAD_PAD), head3(V_DIM_B), head4(V_DIM_B), head4(1), head4(1), _HBM],
        out_specs=[per_head, blk4(HEAD_PAD), blk4(V_DIM_B), _HBM],
        out_shape=[jax.ShapeDtypeStruct((t, MLA_W), F32), jax.ShapeDtypeStruct((N_HEADS_B, nb, HEAD_PAD, _TQ), F32),
                   jax.ShapeDtypeStruct((N_HEADS_B, nb, V_DIM_B, _TQ), F32),
                   jax.ShapeDtypeStruct((3,) + gp.shape[1:], gp.dtype)],
        scratch_shapes=[pltpu.VMEM((t, LANES), F32), pltpu.VMEM((t, LANES), F32),
                        pltpu.SemaphoreType.DMA((3,)), pltpu.SemaphoreType.DMA((3,))],
        compiler_params=_params(("arbitrary", "arbitrary")),
    )(k, kt, vt, q, qt, d_out, d_out_t, lse, delta, gp)


def _mla_prep_bwd(dq, dkt, dvt, proj, posc, freq, qan, kvan, wq, wk, wv, swap_src):
    t = dq.shape[0]
    tm = _TQ

    def body(dq_ref, dkt_ref, dvt_ref, cq_ref, ckv_ref, pos_ref, f_ref, qan_ref, kvan_ref, wq_ref, wk_ref, wv_ref, src_ref,
             dcq_ref, dckv_ref, dkr_ref, dwq_ref, dwk_ref, dwv_ref, dqan_ref, dkvan_ref, got_ref, send_sem, recv_sem):
        swap = _sibling_copy(src_ref, got_ref, send_sem, recv_sem)

        @pl.when(pl.program_id(0) == 0)
        def _():
            swap.start()
            for r in (dwq_ref, dwk_ref, dwv_ref, dqan_ref, dkvan_ref):
                r[...] = jnp.zeros(r.shape, F32)

        cq = cq_ref[...]
        rq = _rms(cq)
        nq_ = cq * rq
        cqn = (nq_ * qan_ref[...]).astype(BF16)
        ckv = ckv_ref[...]
        rkv = _rms(ckv)
        nkv = ckv * rkv
        ckvn = (nkv * kvan_ref[...]).astype(BF16)
        c, s, lo, hi = _rope_coeffs(pos_ref[...], f_ref[...])
        dkr = jnp.zeros((tm, LANES), F32)
        dqb, dkb = [], []
        for h in range(N_HEADS_B):
            dqb.append(_unrope(dq_ref[:, HEAD_PAD * h:HEAD_PAD * (h + 1)], c, s, lo, hi).astype(BF16))
            dk_h = dkt_ref[h, 0].T
            dkr = dkr + dk_h
            dkb.append(dk_h.astype(BF16))
        dqb, dkb = jnp.concatenate(dqb, axis=1), jnp.concatenate(dkb, axis=1)
        dkr = jnp.where(lo | hi, _unrope(dkr, c, s, lo, hi), 0.0)
        dkr_ref[...] = pltpu.roll(dkr, LANES - QK_NOPE, axis=1).astype(BF16)
        dvb = dvt_ref[...].reshape(N_HEADS_B * V_DIM_B, tm).T.astype(BF16)
        dwq_ref[...] += _dot_tn(cqn, dqb)
        dwk_ref[...] += _dot_tn(ckvn, dkb)
        dwv_ref[...] += _dot_tn(ckvn, dvb)
        dcqn = _dot_nt(dqb, wq_ref[...])
        dckvn = _dot_nt(dkb, wk_ref[...]) + _dot_nt(dvb, wv_ref[...])
        dcq, dqan = _norm_bwd(dcqn, nq_, rq, qan_ref[...])
        dckv, dkvan = _norm_bwd(dckvn, nkv, rkv, kvan_ref[...])
        dcq_ref[...] = dcq.astype(BF16)
        dckv_ref[...] = dckv.astype(BF16)
        dqan_ref[...] += dqan
        dkvan_ref[...] += dkvan

        @pl.when(pl.program_id(0) == t // tm - 1)
        def _():
            swap.wait_recv()
            swap.wait_send()

    row = lambda i: (i, 0)
    vw = N_HEADS_B * V_DIM_B
    return pl.pallas_call(
        body, name="mla_prep_bwd", grid=(t // tm,),
        in_specs=[pl.BlockSpec((tm, MLA_W), row), pl.BlockSpec((N_HEADS_B, 1, HEAD_PAD, tm), lambda i: (0, i, 0, 0)),
                  pl.BlockSpec((N_HEADS_B, 1, V_DIM_B, tm), lambda i: (0, i, 0, 0)),
                  pl.BlockSpec((tm, Q_LORA), lambda i: (i, _CQ_BLK)),
                  pl.BlockSpec((tm, LANES), lambda i: (i, _CKV_BLK)),
                  pl.BlockSpec((tm, 1), row), _full((1, LANES)), _full((1, Q_LORA)), _full((1, KV_LORA)),
                  _full((Q_LORA, MLA_W)), _full((KV_LORA, MLA_W)), _full((KV_LORA, vw)), _HBM],
        out_specs=[pl.BlockSpec((tm, Q_LORA), row), pl.BlockSpec((tm, LANES), row), pl.BlockSpec((tm, LANES), row),
                   _full((Q_LORA, MLA_W)), _full((KV_LORA, MLA_W)), _full((KV_LORA, vw)),
                   _full((1, Q_LORA)), _full((1, KV_LORA)), _HBM],
        out_shape=[jax.ShapeDtypeStruct((t, Q_LORA), BF16), jax.ShapeDtypeStruct((t, LANES), BF16),
                   jax.ShapeDtypeStruct((t, LANES), BF16),
                   jax.ShapeDtypeStruct((Q_LORA, MLA_W), F32), jax.ShapeDtypeStruct((KV_LORA, MLA_W), F32),
                   jax.ShapeDtypeStruct((KV_LORA, vw), F32),
                   jax.ShapeDtypeStruct((1, Q_LORA), F32), jax.ShapeDtypeStruct((1, KV_LORA), F32),
                   jax.ShapeDtypeStruct(swap_src.shape, swap_src.dtype)],
        scratch_shapes=[pltpu.SemaphoreType.DMA(()), pltpu.SemaphoreType.DMA(())],
        compiler_params=_params(("arbitrary",)),
    )(dq, dkt, dvt, proj, proj, posc, freq, qan, kvan, wq, wk, wv, swap_src)


def _swa_bwd(proj, d_out, lse, delta, posc, posr, sinks):
    t = proj.shape[0]
    per = _SWA_PER_STEP
    span = per * BLOCK
    steps = t // span

    def body(q_ref, kc_ref, kp_ref, vc_ref, vp_ref, do_ref, l_ref, d_ref, pq_ref, pc_ref, pp_ref, sink_ref,
             dq_ref, dk_ref, dv_ref, ds_ref, dkb_s, dvb_s, dk_keep, dv_keep):
        n = pl.program_id(0)

        @pl.when(n == 0)
        def _():
            ds_ref[...] = jnp.zeros(ds_ref.shape, F32)
            dk_keep[...] = jnp.zeros(dk_keep.shape, F32)
            dv_keep[...] = jnp.zeros(dv_keep.shape, F32)

        @pl.when(n < steps)
        def _():
            k_all = jnp.concatenate([kp_ref[...], kc_ref[...]], axis=0)
            v_all = jnp.concatenate([vp_ref[...], vc_ref[...]], axis=0)
            pos_all = jnp.concatenate([pp_ref[...], pc_ref[...]], axis=0)
            ki = lax.broadcasted_iota(jnp.int32, (2 * BLOCK, BLOCK), 0)
            qi = lax.broadcasted_iota(jnp.int32, (2 * BLOCK, BLOCK), 1)
            window = (ki > qi) & (ki <= qi + WINDOW)
            lane = lax.broadcasted_iota(jnp.int32, (1, LANES), 1)
            dsink = jnp.zeros((1, LANES), F32)
            for sub in range(per):
                band = slice(BLOCK * sub, BLOCK * (sub + 2))
                own = slice(BLOCK * sub, BLOCK * (sub + 1))
                kb, vb = k_all[band], v_all[band]
                dist = jnp.abs(pos_all[band] - pq_ref[:, own])
                valid = window & ((n > 0) | (ki >= BLOCK)) if sub == 0 else window
                qv, dov = q_ref[own, :], do_ref[own, :]
                q_t, do_t, kb_t = qv.T, dov.T, kb.T
                dq_t = []
                for kh in range(N_KV_A):
                    heads = range(_GROUP_A * kh, _GROUP_A * (kh + 1))
                    st_g = _dot(_head_cols(kb, kh).astype(BF16), _group_t(q_t, kh))
                    dpt_g = _dot(_head_cols(vb, kh).astype(BF16), _group_t(do_t, kh))
                    pts, dsts = [], []
                    for j, h in enumerate(heads):
                        st = _swa_scores_t(st_g, j, h, dist, valid)
                        l_h, d_h = l_ref[h:h + 1, own], d_ref[h:h + 1, own]
                        pt = jnp.exp2(st - l_h)
                        p_sink = jnp.exp2(sink_ref[0:1, h:h + 1] * _LOG2E - l_h)
                        dsink = dsink + jnp.where(lane == h, jnp.sum(-p_sink * d_h, axis=1, keepdims=True), 0.0)
                        dst = pt * (dpt_g[:, BLOCK * j:BLOCK * (j + 1)] - d_h) * _SWA_SCALE
                        pts.append(pt.astype(BF16))
                        dsts.append(dst.astype(BF16))
                    pt_g, dst_g = jnp.concatenate(pts, axis=1), jnp.concatenate(dsts, axis=1)
                    q_g = jnp.concatenate([_head_cols(qv, h) for h in heads], axis=0).astype(BF16)
                    do_g = jnp.concatenate([_head_cols(dov, h) for h in heads], axis=0).astype(BF16)
                    dkb_s[sub, :, HEAD_DIM_A * kh:HEAD_DIM_A * (kh + 1)] = _dot(dst_g, q_g)
                    dvb_s[sub, :, HEAD_DIM_A * kh:HEAD_DIM_A * (kh + 1)] = _dot(pt_g, do_g)
                    dq_g = _dot(_head_rows(kb_t, kh).astype(BF16), dst_g)
                    dq_t.extend(dq_g[:, BLOCK * j:BLOCK * (j + 1)] for j in range(_GROUP_A))
                dq_ref[own, :] = jnp.concatenate(dq_t, axis=0).T
            ds_ref[...] += dsink
            for keep, out, parts in ((dk_keep, dk_ref, dkb_s), (dv_keep, dv_ref, dvb_s)):
                out[0:span - BLOCK, :] = keep[0:span - BLOCK, :]
                out[span - BLOCK:span, :] = keep[span - BLOCK:span, :] + parts[0, 0:BLOCK, :]
                for s in range(per - 1):
                    keep[BLOCK * s:BLOCK * (s + 1), :] = parts[s, BLOCK:2 * BLOCK, :] + parts[s + 1, 0:BLOCK, :]
                keep[span - BLOCK:span, :] = parts[per - 1, BLOCK:2 * BLOCK, :]

        @pl.when(n == steps)
        def _():
            dk_ref[...] = dk_keep[...]
            dv_ref[...] = dv_keep[...]

    last = steps - 1
    cur = lambda n: (jnp.minimum(n, last), 0)
    cur_t = lambda n: (0, jnp.minimum(n, last))
    prv = lambda n: jnp.maximum(per * jnp.minimum(n, last) - 1, 0)
    out_prev = lambda n: (jnp.maximum(n - 1, 0), 0)
    return pl.pallas_call(
        body, name="swa_bwd", grid=(steps + 1,),
        in_specs=[pl.BlockSpec((span, WIDTH_A), lambda n: (jnp.minimum(n, last), _QA_BLK)),
                  pl.BlockSpec((span, LANES), lambda n: (jnp.minimum(n, last), _KA_BLK)),
                  pl.BlockSpec((BLOCK, LANES), lambda n: (prv(n), _KA_BLK)),
                  pl.BlockSpec((span, LANES), lambda n: (jnp.minimum(n, last), _VA_BLK)),
                  pl.BlockSpec((BLOCK, LANES), lambda n: (prv(n), _VA_BLK)),
                  pl.BlockSpec((span, WIDTH_A), cur), pl.BlockSpec((N_HEADS_A, span), cur_t),
                  pl.BlockSpec((N_HEADS_A, span), cur_t), pl.BlockSpec((1, span), cur_t),
                  pl.BlockSpec((span, 1), cur), pl.BlockSpec((BLOCK, 1), lambda n: (prv(n), 0)),
                  _full((1, N_HEADS_A))],
        out_specs=[pl.BlockSpec((span, WIDTH_A), cur), pl.BlockSpec((span, LANES), out_prev),
                   pl.BlockSpec((span, LANES), out_prev), _full((1, LANES))],
        out_shape=[jax.ShapeDtypeStruct((t, WIDTH_A), F32), jax.ShapeDtypeStruct((t, LANES), F32),
                   jax.ShapeDtypeStruct((t, LANES), F32), jax.ShapeDtypeStruct((1, LANES), F32)],
        scratch_shapes=[pltpu.VMEM((per, 2 * BLOCK, LANES), F32), pltpu.VMEM((per, 2 * BLOCK, LANES), F32),
                        pltpu.VMEM((span, LANES), F32), pltpu.VMEM((span, LANES), F32)],
        compiler_params=_params(("arbitrary",)),
    )(proj, proj, proj, proj, proj, d_out, lse, delta, posr, posc, posc, sinks)


def _in_bwd(dproj, w_in_t, x, dx1, g1, gp):
    t = x.shape[0]
    tm = 512
    steps = t // tm

    def body(dp_ref, w_ref, x_ref, dx1_ref, g_ref, gp_ref, dx_ref, dg_ref, land_ref, send_sems, recv_sems):
        i = pl.program_id(0)

        @pl.when(i == 0)
        def _():
            dg_ref[...] = jnp.zeros(dg_ref.shape, F32)
            _scatter_start(gp_ref, land_ref, send_sems, recv_sems)

        for rows in _row_halves(tm):
            dh = _dot(dp_ref[rows, :], w_ref[...])
            xv = x_ref[rows, :]
            r = _rms(xv)
            dx, dg = _norm_bwd(dh, xv * r, r, g_ref[...])
            dx_ref[rows, :] = dx1_ref[rows, :] + dx
            dg_ref[...] += dg

        @pl.when(i == steps - 1)
        def _():
            _scatter_wait(gp_ref, land_ref, send_sems, recv_sems)

    row = lambda i: (i, 0)
    blk = pl.BlockSpec((tm, D_MODEL), row)
    return pl.pallas_call(
        body, name="in_bwd", grid=(steps,),
        in_specs=[pl.BlockSpec((tm, D_IN_PAD), row), _full((D_IN_PAD, D_MODEL)), blk, blk, _full((1, D_MODEL)), _HBM],
        out_specs=[blk, _full((1, D_MODEL)), _HBM],
        out_shape=[jax.ShapeDtypeStruct((t, D_MODEL), F32), jax.ShapeDtypeStruct((1, D_MODEL), F32),
                   jax.ShapeDtypeStruct((3,) + gp.shape[1:], gp.dtype)],
        scratch_shapes=[pltpu.SemaphoreType.DMA((3,)), pltpu.SemaphoreType.DMA((3,))],
        compiler_params=_params(("arbitrary",)),
    )(dproj, w_in_t, x, dx1, g1, gp)


def _adamw_store(w, g, m, v, out_refs):
    g_out, d_out, m_out, v_out = out_refs
    m_new = ADAM_B1 * m + (1.0 - ADAM_B1) * g
    v_new = ADAM_B2 * v + (1.0 - ADAM_B2) * jnp.square(g)
    m_hat = m_new / (1.0 - ADAM_B1 ** ADAM_STEP)
    v_hat = v_new / (1.0 - ADAM_B2 ** ADAM_STEP)
    g_out[...] = g
    d_out[...] = -ADAM_LR * (m_hat / (jnp.sqrt(v_hat) + ADAM_EPS) + ADAM_WD * w)
    m_out[...] = m_new
    v_out[...] = v_new


_SMALL_SLOTS = {"pre_norm_mix": (0, 0, D_MODEL), "post_norm_mix": (1, 0, D_MODEL), "pre_norm_mlp": (2, 0, D_MODEL),
                "post_norm_mlp": (3, 0, D_MODEL), "q_a_norm": (4, 0, Q_LORA), "kv_a_norm": (4, Q_LORA, KV_LORA),
                "sinks": (4, Q_LORA + KV_LORA, N_HEADS_A)}
_LOSS_ROW = 5


def _adamw_small(red, w, m, v):
    names = tuple(_SMALL_SLOTS)
    n = len(names)

    def body(*refs):
        red_ref, ws, ms, vs, outs = refs[0], refs[1:1 + n], refs[1 + n:1 + 2 * n], refs[1 + 2 * n:1 + 3 * n], refs[1 + 3 * n:]
        for k, name in enumerate(names):
            row, lane, width = _SMALL_SLOTS[name]
            g = red_ref[row:row + 1, lane:lane + width]
            _adamw_store(ws[k][...], g, ms[k][...], vs[k][...], outs[4 * k:4 * k + 4])

    vmem = pl.BlockSpec(memory_space=pltpu.VMEM)
    res = pl.pallas_call(
        body, name="adamw_small", in_specs=[vmem] * (1 + 3 * n), out_specs=[vmem] * (4 * n),
        out_shape=[jax.ShapeDtypeStruct(w[name].shape, F32) for name in names for _ in range(4)],
    )(red, *[w[k] for k in names], *[m[k] for k in names], *[v[k] for k in names])
    return {name: res[4 * k:4 * k + 4] for k, name in enumerate(names)}


_ADAMW_RIDERS = ("w_up", "w_down", "w_out")


def _dw_in_adamw(dproj, h, g_parts, w, m, v):
    t, cols = dproj.shape
    tm, tk = cols // 2, min(1024, t)
    rows_out = N_CHIPS * SHARD_SHAPES["w_in"][1]
    nk = t // tk
    steps = 2 * nk
    names = _ADAMW_RIDERS
    n = len(names)

    def body(a_ref, b_ref, *rest):
        g1s, g2s, ws, ms, vs = (rest[n * j:n * (j + 1)] for j in range(5))
        o_ref, outs = rest[5 * n], rest[5 * n + 1:]

        @pl.when(pl.program_id(2) == 0)
        def _():
            o_ref[...] = jnp.zeros(o_ref.shape, F32)

        o_ref[...] += _dot_tn(a_ref[...], b_ref[...])
        for j in range(n):
            _adamw_store(ws[j][...], g1s[j][...] + g2s[j][...], ms[j][...], vs[j][...], outs[4 * j:4 * j + 4])

    def rider_spec(name, packed):
        br = SHARD_SHAPES[name][0] // steps
        first = _row_offset(GROUP_B, name) // br if packed else 0
        return pl.BlockSpec((br, D_MODEL), lambda i, j, k: (first + i * nk + k, 0))

    g_specs = [rider_spec(name, True) for name in names]
    own_specs = [rider_spec(name, False) for name in names]
    res = pl.pallas_call(
        body, name="dw_in", grid=(2, 1, nk),
        in_specs=[pl.BlockSpec((tk, tm), lambda i, j, k: (k, i)), pl.BlockSpec((tk, D_MODEL), lambda i, j, k: (k, 0))]
        + g_specs * 2 + own_specs * 3,
        out_specs=[pl.BlockSpec((tm, D_MODEL), lambda i, j, k: (i, 0))] + [s for s in own_specs for _ in range(4)],
        out_shape=[jax.ShapeDtypeStruct((rows_out, D_MODEL), F32)]
        + [jax.ShapeDtypeStruct(SHARD_SHAPES[name], F32) for name in names for _ in range(4)],
        compiler_params=_params(("arbitrary", "arbitrary", "arbitrary")),
    )(dproj, h, *[g_parts[0]] * n, *[g_parts[1]] * n, *[w[k] for k in names], *[m[k] for k in names],
      *[v[k] for k in names])
    return res[0], {name: res[1 + 4 * j:5 + 4 * j] for j, name in enumerate(names)}


def _adamw(w, g_parts, m, v, name, block, g_row_off=0):
    r, c = w.shape
    br, bc = block
    ng = len(g_parts)

    def body(*refs):
        w_ref, g_refs, m_ref, v_ref = refs[0], refs[1:1 + ng], refs[1 + ng], refs[2 + ng]
        g = g_refs[0][...]
        for gr in g_refs[1:]:
            g = g + gr[...]
        _adamw_store(w_ref[...], g, m_ref[...], v_ref[...], refs[3 + ng:])

    assert g_row_off % br == 0 and r % br == 0 and c % bc == 0
    blk = pl.BlockSpec(block, lambda i, j: (i, j))
    g_blk = pl.BlockSpec(block, lambda i, j: (i + g_row_off // br, j))
    return pl.pallas_call(
        body, name=name, grid=(r // br, c // bc),
        in_specs=[blk] + [g_blk] * ng + [blk, blk], out_specs=[blk] * 4,
        out_shape=[jax.ShapeDtypeStruct((r, c), F32)] * 4,
        compiler_params=_params(("parallel", "parallel")),
    )(w, *g_parts, m, v)


_HBM = pl.BlockSpec(memory_space=pltpu.HBM)


def _other_chips(x, y):
    return ((1 - x, y), (x, 1 - y), (1 - x, 1 - y))


def _gather_copies(src, out, send_sems, recv_sems, local_sem):
    x, y, c = lax.axis_index("x"), lax.axis_index("y"), lax.axis_index("c")
    me = 2 * x + y
    local = pltpu.make_async_copy(src, out.at[me], local_sem)

    def copies(arriving):
        return [pltpu.make_async_remote_copy(src_ref=src, dst_ref=out.at[2 * px + py if arriving else me],
                                             send_sem=send_sems.at[j], recv_sem=recv_sems.at[j], device_id=(px, py, c),
                                             device_id_type=MESH)
                for j, (px, py) in enumerate(_other_chips(x, y))]

    return local, copies


def _gather_start(src, out, send_sems, recv_sems, local_sem):
    local, copies = _gather_copies(src, out, send_sems, recv_sems, local_sem)
    local.start()
    for cp in copies(False):
        cp.start()


def _gather_wait(src, out, send_sems, recv_sems, local_sem):
    local, copies = _gather_copies(src, out, send_sems, recv_sems, local_sem)
    for cp in copies(True):
        cp.wait_recv()
    for cp in copies(False):
        cp.wait_send()
    local.wait()


def _scatter_copies(src, land, send_sems, recv_sems):
    x, y, c = lax.axis_index("x"), lax.axis_index("y"), lax.axis_index("c")
    return [pltpu.make_async_remote_copy(src_ref=src.at[2 * px + py], dst_ref=land.at[j], send_sem=send_sems.at[j],
                                         recv_sem=recv_sems.at[j], device_id=(px, py, c), device_id_type=MESH)
            for j, (px, py) in enumerate(_other_chips(x, y))]


def _scatter_start(src, land, send_sems, recv_sems):
    for cp in _scatter_copies(src, land, send_sems, recv_sems):
        cp.start()


def _scatter_wait(src, land, send_sems, recv_sems):
    copies = _scatter_copies(src, land, send_sems, recv_sems)
    for cp in copies:
        cp.wait_recv()
    for cp in copies:
        cp.wait_send()


def _all_gather_chips(packed):
    r = packed.shape[0]
    half = r // 2

    def body(src, out, ici_send, ici_recv, d2d_send, d2d_recv, local_sem):
        x, y, c = lax.axis_index("x"), lax.axis_index("y"), lax.axis_index("c")
        me = 2 * x + y
        mine = pl.ds(pl.multiple_of(c * half, 16), half)
        theirs = pl.ds(pl.multiple_of((1 - c) * half, 16), half)
        chips = _other_chips(x, y)
        local = pltpu.make_async_copy(src, out.at[me], local_sem)
        local.start()
        sends = [pltpu.make_async_remote_copy(src_ref=src.at[mine], dst_ref=out.at[me, mine], send_sem=ici_send.at[j],
                                              recv_sem=ici_recv.at[j], device_id=(px, py, c), device_id_type=MESH)
                 for j, (px, py) in enumerate(chips)]
        for cp in sends:
            cp.start()
        passed = []
        for j, (px, py) in enumerate(chips):
            block = 2 * px + py
            pltpu.make_async_remote_copy(src_ref=src.at[mine], dst_ref=out.at[block, mine], send_sem=ici_send.at[j],
                                         recv_sem=ici_recv.at[j], device_id=(px, py, c), device_id_type=MESH).wait_recv()
            cp = pltpu.make_async_remote_copy(src_ref=out.at[block, mine], dst_ref=out.at[block, mine],
                                              send_sem=d2d_send.at[j], recv_sem=d2d_recv.at[j],
                                              device_id=(x, y, 1 - c), device_id_type=MESH)
            cp.start()
            passed.append(cp)
        for j, (px, py) in enumerate(chips):
            block = 2 * px + py
            pltpu.make_async_remote_copy(src_ref=out.at[block, theirs], dst_ref=out.at[block, theirs],
                                         send_sem=d2d_send.at[j], recv_sem=d2d_recv.at[j],
                                         device_id=(x, y, 1 - c), device_id_type=MESH).wait_recv()
        for cp in sends + passed:
            cp.wait_send()
        local.wait()

    sems = pltpu.SemaphoreType.DMA((3,))
    return pl.pallas_call(
        body, name="ag_weights", in_specs=[_HBM], out_specs=_HBM,
        out_shape=jax.ShapeDtypeStruct((N_CHIPS,) + packed.shape, packed.dtype),
        scratch_shapes=[sems, sems, sems, sems, pltpu.SemaphoreType.DMA(())],
    )(packed)


def _sum4(gp, land, chip, name):
    _, r, w = gp.shape
    tr = 256 if r % 256 == 0 else 128

    def body(chip_ref, o_ref, l_ref, s_ref):
        s_ref[...] = ((o_ref[0] + l_ref[0].astype(F32)) + l_ref[1].astype(F32)) + l_ref[2].astype(F32)

    return pl.pallas_call(
        body, name=name,
        grid_spec=pltpu.PrefetchScalarGridSpec(
            num_scalar_prefetch=1, grid=(r // tr,),
            in_specs=[pl.BlockSpec((1, tr, w), lambda i, chip_ref: (chip_ref[0], i, 0)),
                      pl.BlockSpec((3, tr, w), lambda i, chip_ref: (0, i, 0))],
            out_specs=pl.BlockSpec((tr, w), lambda i, chip_ref: (i, 0))),
        out_shape=jax.ShapeDtypeStruct((r, w), F32),
        compiler_params=_params(("parallel",)),
    )(chip, gp, land)


def _sibling_copy(src, got, send_sem, recv_sem):
    x, y, c = lax.axis_index("x"), lax.axis_index("y"), lax.axis_index("c")
    return pltpu.make_async_remote_copy(src_ref=src, dst_ref=got, send_sem=send_sem, recv_sem=recv_sem,
                                        device_id=(x, y, 1 - c), device_id_type=MESH)


def _swap_sibling(s, name):
    def body(src, got, send_sem, recv_sem):
        cp = _sibling_copy(src, got, send_sem, recv_sem)
        cp.start()
        cp.wait_recv()
        cp.wait_send()

    return pl.pallas_call(
        body, name=name, in_specs=[_HBM], out_specs=_HBM,
        out_shape=jax.ShapeDtypeStruct(s.shape, s.dtype),
        scratch_shapes=[pltpu.SemaphoreType.DMA(()), pltpu.SemaphoreType.DMA(())],
    )(s)


def _all_reduce_small(dsmall, loss):
    n_dev = 8
    names = tuple(_SMALL_SLOTS)
    shape = (8, D_MODEL)

    def body(*refs):
        parts, loss_ref = refs[:len(names)], refs[len(names)]
        out, src, gath, send_sems, recv_sems = refs[len(names) + 1:]
        x, y, c = lax.axis_index("x"), lax.axis_index("y"), lax.axis_index("c")
        me = 4 * x + 2 * y + c
        src[...] = jnp.zeros(shape, F32)
        for name, part in zip(names, parts):
            row, lane, _ = _SMALL_SLOTS[name]
            src[row:row + 1, lane:lane + part.shape[1]] = part[...]
        src[_LOSS_ROW:_LOSS_ROW + 1, 0:LANES] = loss_ref[...]
        gath[me] = src[...]
        peers = []
        for k in range(1, n_dev):
            px = 1 - x if (k >> 2) & 1 else x
            py = 1 - y if (k >> 1) & 1 else y
            pc = 1 - c if k & 1 else c
            peers.append((px, py, pc))
        sends = []
        for j, peer in enumerate(peers):
            cp = pltpu.make_async_remote_copy(src_ref=src, dst_ref=gath.at[me], send_sem=send_sems.at[j],
                                              recv_sem=recv_sems.at[j], device_id=peer, device_id_type=MESH)
            cp.start()
            sends.append(cp)
        for j, (px, py, pc) in enumerate(peers):
            pltpu.make_async_remote_copy(src_ref=src, dst_ref=gath.at[4 * px + 2 * py + pc], send_sem=send_sems.at[j],
                                         recv_sem=recv_sems.at[j], device_id=(px, py, pc), device_id_type=MESH).wait_recv()
        for cp in sends:
            cp.wait_send()
        acc = gath[0]
        for d in range(1, n_dev):
            acc = acc + gath[d]
        out[...] = acc

    vmem = pl.BlockSpec(memory_space=pltpu.VMEM)
    return pl.pallas_call(
        body, name="ar_small", in_specs=[vmem] * (len(names) + 1), out_specs=vmem,
        out_shape=jax.ShapeDtypeStruct(shape, F32),
        scratch_shapes=[pltpu.VMEM(shape, F32), pltpu.VMEM((n_dev,) + shape, F32),
                        pltpu.SemaphoreType.DMA((n_dev - 1,)), pltpu.SemaphoreType.DMA((n_dev - 1,))],
    )(*[dsmall[k] for k in names], loss)


_W_IN_ROWS = SHARD_SHAPES["w_in"][1]


def _shard_rows(name, a):
    return jnp.transpose(a) if name == "w_in" else a.reshape(PACK_ROWS[name], D_MODEL)


def _pack(group, shards, dtype):
    parts = [_shard_rows(n, shards[n]).astype(dtype) for n in group]
    pad = -sum(PACK_ROWS[n] for n in group) % LANES
    if pad:
        parts.append(jnp.zeros((pad, D_MODEL), dtype))
    return jnp.concatenate(parts, axis=0)


def _col_sharded_full(g, name, group):
    r, c = SHARD_SHAPES[name]
    off = _row_offset(group, name)
    blocks = g[:, off:off + PACK_ROWS[name]].reshape(N_CHIPS, r, c)
    return jnp.transpose(blocks, (1, 0, 2)).reshape(r, N_CHIPS * c)


def _col_sharded_blocks(d, name):
    r, c = SHARD_SHAPES[name]
    return jnp.transpose(d.reshape(r, N_CHIPS, c), (1, 0, 2)).reshape(N_CHIPS, PACK_ROWS[name], D_MODEL)


def _weights_a(g):
    dt = g.dtype
    w_in_t = g[:, :_W_IN_ROWS].reshape(N_CHIPS * _W_IN_ROWS, D_MODEL)
    w_in_t = jnp.concatenate([w_in_t, jnp.zeros((D_IN_PAD - N_CHIPS * _W_IN_ROWS, D_MODEL), dt)], axis=0)
    wq = _col_sharded_full(g, "w_q_b", GROUP_A).reshape(Q_LORA, N_HEADS_B, Q_HEAD_B)
    wq_p = jnp.concatenate([wq, jnp.zeros((Q_LORA, N_HEADS_B, HEAD_PAD - Q_HEAD_B), dt)], axis=2).reshape(Q_LORA, MLA_W)
    wkv = _col_sharded_full(g, "w_kv_b", GROUP_A).reshape(KV_LORA, N_HEADS_B, QK_NOPE + V_DIM_B)
    zk = jnp.zeros((KV_LORA, N_HEADS_B, HEAD_PAD - QK_NOPE), dt)
    wk_p = jnp.concatenate([wkv[:, :, :QK_NOPE], zk], axis=2).reshape(KV_LORA, MLA_W)
    wv = wkv[:, :, QK_NOPE:].reshape(KV_LORA, N_HEADS_B * V_DIM_B)
    return dict(w_in=w_in_t, wq=wq_p, wk=wk_p, wv=wv, wv_t=jnp.transpose(wv))


def _grad_blocks_a(dw_in_t, dwq_p, dwk_p, dwv):
    dwq = dwq_p.reshape(Q_LORA, N_HEADS_B, HEAD_PAD)[:, :, :Q_HEAD_B].reshape(Q_LORA, N_HEADS_B * Q_HEAD_B)
    dwk = dwk_p.reshape(KV_LORA, N_HEADS_B, HEAD_PAD)[:, :, :QK_NOPE]
    dwkv = jnp.concatenate([dwk, dwv.reshape(KV_LORA, N_HEADS_B, V_DIM_B)], axis=2)
    dwkv = dwkv.reshape(KV_LORA, N_HEADS_B * (QK_NOPE + V_DIM_B))
    pad = -sum(PACK_ROWS[n] for n in GROUP_A) % LANES
    return jnp.concatenate([dw_in_t.reshape(N_CHIPS, _W_IN_ROWS, D_MODEL), _col_sharded_blocks(dwq, "w_q_b"),
                            _col_sharded_blocks(dwkv, "w_kv_b"), jnp.zeros((N_CHIPS, pad, D_MODEL), F32)], axis=1)


def _rope_freq_lanes():
    freqs = ROPE_THETA ** (-jnp.arange(0, QK_ROPE, 2, dtype=F32) / QK_ROPE)
    return jnp.concatenate([jnp.zeros((QK_NOPE,), F32), freqs, freqs,
                            jnp.zeros((HEAD_PAD - Q_HEAD_B,), F32)]).reshape(1, LANES)


def _fwd_bwd(x, positions, target, w, m, v):
    t = x.shape[0]
    wa = _weights_a(_all_gather_chips(_pack(GROUP_A, w, BF16)))
    posr = positions.astype(F32).reshape(1, t)
    posc = posr.reshape(t, 1)
    freq = _rope_freq_lanes()
    g1, g2, g3, g4 = w["pre_norm_mix"], w["post_norm_mix"], w["pre_norm_mlp"], w["post_norm_mlp"]
    qan, kvan, sinks = w["q_a_norm"], w["kv_a_norm"], w["sinks"]

    h, proj = _proj_fwd(x, g1, wa["w_in"])
    out_a, lse_a = _swa_fwd(proj, posc, posr, sinks)
    qm, km, qt, kt, vt = _mla_prep_fwd(proj, posc, freq, qan, kvan, wa["wq"], wa["wk"], wa["wv_t"])
    out_bt, lse_b, wb = _mla_fwd(km, qt, vt, _pack(GROUP_B, w, BF16))
    w_oa, w_ob = _col_sharded_full(wb, "w_o_a", GROUP_B), _col_sharded_full(wb, "w_o_b", GROUP_B)
    merged, y, x1, h2 = _mix_out_fwd(out_a, out_bt, proj, x, w_oa, w_ob, wb, g2, g3)
    a = _up_fwd(h2, wb)
    dx2, dyd, dg4, loss = _down_fwd_loss(a, wb, x1, target, g4)

    gp_b = _dw_into_blocks(a, dyd, "w_down", 1024, _TK_DW)
    du = _down_bwd(dyd, wb, a)
    gp_b = _dw_into_blocks(h2, du, "w_up", 1024, _TK_DW, gp_b)
    dx1, dy, dg3, dg2 = _up_bwd(du, wb, x1, dx2, y, g3, g2)
    gp_b = _dw_into_blocks(merged, dy, "w_out", 1024, _TK_DW, gp_b)
    doa, dob, dga, dgb, d_out_a, d_out_b, d_out_bt, del_a, del_b = _mix_out_bwd(dy, out_a, out_bt, proj, w_oa, w_ob, wb)
    dw_oa = _matmul_tn(out_a, doa, "dw_o_a", 512, 1024)
    dw_ob = _dw_ob(out_bt, dob)
    small_b = jnp.concatenate([_col_sharded_blocks(dw_oa, "w_o_a"), _col_sharded_blocks(dw_ob, "w_o_b")], axis=1)
    gp_b = lax.dynamic_update_slice(gp_b, small_b, (0, _row_offset(GROUP_B, "w_o_a"), 0))
    dqm, dkm, dvm, land_b = _mla_bwd(qm, km, qt, kt, vt, d_out_b, d_out_bt, lse_b, del_b, gp_b)
    chip = (2 * lax.axis_index("x") + lax.axis_index("y")).astype(jnp.int32).reshape(1)
    part_b = _sum4(gp_b, land_b, chip, "rs_sum_b")
    dcq, dckv, dkr, dwq, dwk, dwv, dqan, dkvan, sib_b = _mla_prep_bwd(
        dqm, dkm, dvm, proj, posc, freq, qan, kvan, wa["wq"], wa["wk"], wa["wv"], part_b)
    dqa, dka, dva, dsinks = _swa_bwd(proj, d_out_a, lse_a, del_a, posc, posr, sinks)
    dproj = jnp.concatenate([dga, dgb, dqa.astype(BF16), dka.astype(BF16), dva.astype(BF16), dcq, dckv, dkr], axis=1)
    dw_in_t, updated = _dw_in_adamw(dproj, h, [part_b, sib_b], w, m, v)
    gp_a = _grad_blocks_a(dw_in_t, dwq, dwk, dwv)
    grad_x, dg1, land_a = _in_bwd(dproj, wa["w_in"], x, dx1, g1, gp_a.astype(BF16))

    part_a = _sum4(gp_a, land_a, chip, "rs_sum_a")
    reduced = {GROUP_A: [part_a, _swap_sibling(part_a, "rs_swap_a")], GROUP_B: [part_b, sib_b]}
    dsmall = dict(pre_norm_mix=dg1, post_norm_mix=dg2, pre_norm_mlp=dg3, post_norm_mlp=dg4,
                  q_a_norm=dqan, kv_a_norm=dkvan, sinks=dsinks)
    return loss, grad_x, reduced, dsmall, updated


def kernel(x, positions, pre_norm_mix, w_in, q_a_norm, w_q_b, kv_a_norm, w_kv_b, sinks, w_o_a, w_o_b, w_out, post_norm_mix, pre_norm_mlp, w_up, w_down, post_norm_mlp, loss_target, m_pre_norm_mix, m_w_in, m_q_a_norm, m_w_q_b, m_kv_a_norm, m_w_kv_b, m_sinks, m_w_o_a, m_w_o_b, m_w_out, m_post_norm_mix, m_pre_norm_mlp, m_w_up, m_w_down, m_post_norm_mlp, v_pre_norm_mix, v_w_in, v_q_a_norm, v_w_q_b, v_kv_a_norm, v_w_kv_b, v_sinks, v_w_o_a, v_w_o_b, v_w_out, v_post_norm_mix, v_pre_norm_mlp, v_w_up, v_w_down, v_post_norm_mlp):
    w = dict(pre_norm_mix=pre_norm_mix, w_in=w_in[0], q_a_norm=q_a_norm, w_q_b=w_q_b[0], kv_a_norm=kv_a_norm,
             w_kv_b=w_kv_b[0], sinks=sinks, w_o_a=w_o_a[0], w_o_b=w_o_b[0], w_out=w_out[0],
             post_norm_mix=post_norm_mix, pre_norm_mlp=pre_norm_mlp, w_up=w_up[0], w_down=w_down[0],
             post_norm_mlp=post_norm_mlp)
    m = dict(pre_norm_mix=m_pre_norm_mix, w_in=m_w_in[0], q_a_norm=m_q_a_norm, w_q_b=m_w_q_b[0],
             kv_a_norm=m_kv_a_norm, w_kv_b=m_w_kv_b[0], sinks=m_sinks, w_o_a=m_w_o_a[0], w_o_b=m_w_o_b[0],
             w_out=m_w_out[0], post_norm_mix=m_post_norm_mix, pre_norm_mlp=m_pre_norm_mlp, w_up=m_w_up[0],
             w_down=m_w_down[0], post_norm_mlp=m_post_norm_mlp)
    v = dict(pre_norm_mix=v_pre_norm_mix, w_in=v_w_in[0], q_a_norm=v_q_a_norm, w_q_b=v_w_q_b[0],
             kv_a_norm=v_kv_a_norm, w_kv_b=v_w_kv_b[0], sinks=v_sinks, w_o_a=v_w_o_a[0], w_o_b=v_w_o_b[0],
             w_out=v_w_out[0], post_norm_mix=v_post_norm_mix, pre_norm_mlp=v_pre_norm_mlp, w_up=v_w_up[0],
             w_down=v_w_down[0], post_norm_mlp=v_post_norm_mlp)

    loss, grad_x, reduced, dsmall, updated = _fwd_bwd(x[0], positions, loss_target[0], w, m, v)

    red = _all_reduce_small(dsmall, loss)
    small = _adamw_small(red, w, m, v)

    big = {}
    tr = jnp.transpose
    big["w_in"] = [tr(o)[None] for o in _adamw(tr(w["w_in"]), reduced[GROUP_A], tr(m["w_in"]), tr(v["w_in"]),
                                               "adamw_w_in", (_W_IN_ROWS, 256))]
    for n in _ADAMW_RIDERS:
        big[n] = [o[None] for o in updated[n]]
    for group, names in ((GROUP_A, ("w_q_b", "w_kv_b")), (GROUP_B, ("w_o_a", "w_o_b"))):
        for n in names:
            off = _row_offset(group, n)
            g_parts = [p[off:off + PACK_ROWS[n]].reshape(SHARD_SHAPES[n]) for p in reduced[group]]
            big[n] = [o[None] for o in _adamw(w[n], g_parts, m[n], v[n], "adamw_" + n, SHARD_SHAPES[n])]

    outs = [big[n][k] if n in big else small[n][k] for k in range(4) for n in WEIGHTS]
    return (red[_LOSS_ROW, 0], grad_x[None], *outs)
```

```python
import jax
import jax.numpy as jnp
from jax import lax
from jax.experimental import pallas as pl
from jax.experimental.pallas import tpu as pltpu

F32 = jnp.float32
BF16 = jnp.bfloat16
MESH = pl.DeviceIdType.MESH

D_MODEL = 1024
N_HEADS_A = 8
N_KV_A = 2
HEAD_DIM_A = 64
WINDOW = 128
BLOCK = 128
N_HEADS_B = 8
QK_NOPE = 64
QK_ROPE = 32
V_DIM_B = 64
Q_LORA = 256
KV_LORA = 128
ROPE_THETA = 10000.0
D_FF = 4 * D_MODEL
EPS = 1e-6
WIDTH_A = N_HEADS_A * HEAD_DIM_A
Q_HEAD_B = QK_NOPE + QK_ROPE
D_IN_PAD = 3328
HEAD_PAD = 128
MLA_W = N_HEADS_B * HEAD_PAD

ADAM_LR = 0.001
ADAM_B1 = 0.9
ADAM_B2 = 0.999
ADAM_EPS = 1e-08
ADAM_WD = 0.01
ADAM_STEP = 10

NEG = -1e30
N_CHIPS = 4
LANES = 128
VMEM_LIMIT = 56 * 1024 * 1024

SHARD_SHAPES = {"w_in": (1024, 808), "w_q_b": (256, 192), "w_kv_b": (128, 256), "w_o_a": (512, 256),
                "w_o_b": (512, 256), "w_out": (256, 1024), "w_up": (1024, 1024), "w_down": (1024, 1024)}
PACK_ROWS = {n: (s[0] * s[1]) // D_MODEL for n, s in SHARD_SHAPES.items()}
GROUP_A = ("w_in", "w_q_b", "w_kv_b")
GROUP_B = ("w_up", "w_down", "w_out", "w_o_a", "w_o_b")
WEIGHTS = ("pre_norm_mix", "w_in", "q_a_norm", "w_q_b", "kv_a_norm", "w_kv_b", "sinks", "w_o_a", "w_o_b", "w_out",
           "post_norm_mix", "pre_norm_mlp", "w_up", "w_down", "post_norm_mlp")


def _params(sem=None):
    return pltpu.CompilerParams(dimension_semantics=sem, vmem_limit_bytes=VMEM_LIMIT)


def _dot(a, b):
    return jnp.dot(a, b, preferred_element_type=F32)


def _dot_nt(a, b):
    return lax.dot_general(a, b, (((1,), (1,)), ((), ())), preferred_element_type=F32)


def _dot_tn(a, b):
    return lax.dot_general(a, b, (((0,), (0,)), ((), ())), preferred_element_type=F32)


def _rms(v):
    return lax.rsqrt(jnp.mean(v * v, axis=-1, keepdims=True) + EPS)


def _norm_bwd(dout, n, r, g):
    dn = dout * g
    dx = r * (dn - n * jnp.mean(dn * n, axis=-1, keepdims=True))
    return dx, jnp.sum(dout * n, axis=0, keepdims=True)


def _full(shape):
    return pl.BlockSpec(shape, lambda *_: (0,) * len(shape))


def _row_offset(group, name):
    return sum(PACK_ROWS[n] for n in group[:group.index(name)])


def _wb_spec(name):
    rows = PACK_ROWS[name]
    return pl.BlockSpec((N_CHIPS, rows, D_MODEL), lambda *_: (0, _row_offset(GROUP_B, name) // rows, 0))


def _proj_fwd(x, g1, w_in_t):
    t = x.shape[0]
    tm = 512

    def body(x_ref, g_ref, w_ref, h_ref, p_ref):
        for rows in _row_halves(tm):
            xv = x_ref[rows, :]
            h = ((xv * _rms(xv)) * g_ref[...]).astype(BF16)
            h_ref[rows, :] = h
            p_ref[rows, :] = _dot_nt(h, w_ref[...])

    return pl.pallas_call(
        body, name="proj_fwd", grid=(t // tm,),
        in_specs=[pl.BlockSpec((tm, D_MODEL), lambda i: (i, 0)), _full((1, D_MODEL)), _full((D_IN_PAD, D_MODEL))],
        out_specs=[pl.BlockSpec((tm, D_MODEL), lambda i: (i, 0)), pl.BlockSpec((tm, D_IN_PAD), lambda i: (i, 0))],
        out_shape=[jax.ShapeDtypeStruct((t, D_MODEL), BF16), jax.ShapeDtypeStruct((t, D_IN_PAD), F32)],
        compiler_params=_params(("parallel",)),
    )(x, g1, w_in_t)


_QA_BLK = 2048 // WIDTH_A
_KA_BLK = 2560 // LANES
_VA_BLK = 2688 // LANES
_CQ_BLK = 2816 // Q_LORA
_CKV_BLK = 3072 // LANES
_KR_BLK = 3200 // LANES


_GROUP_A = N_HEADS_A // N_KV_A
_SWA_SCALE = HEAD_DIM_A ** -0.5
_LOG2E = 1.4426950408889634


def _head_cols(v, h):
    return v[:, HEAD_DIM_A * h:HEAD_DIM_A * (h + 1)]


def _head_rows(v, h):
    return v[HEAD_DIM_A * h:HEAD_DIM_A * (h + 1), :]


def _swa_scores_t(st_g, j, h, dist, valid):
    slope = 2.0 ** (-8.0 * (h + 1) / N_HEADS_A)
    st = st_g[:, BLOCK * j:BLOCK * (j + 1)] * (_SWA_SCALE * _LOG2E) - (slope * _LOG2E) * dist
    return jnp.where(valid, st, NEG)


def _group_t(xt, kh):
    return jnp.concatenate([_head_rows(xt, _GROUP_A * kh + j) for j in range(_GROUP_A)], axis=1).astype(BF16)


_SWA_PER_STEP = 4


def _swa_fwd(proj, posc, posr, sinks):
    t = proj.shape[0]
    span = _SWA_PER_STEP * BLOCK

    def body(q_ref, kc_ref, kp_ref, vc_ref, vp_ref, pq_ref, pc_ref, pp_ref, sink_ref, o_ref, l_ref):
        n = pl.program_id(0)
        k_all = jnp.concatenate([kp_ref[...], kc_ref[...]], axis=0)
        v_all = jnp.concatenate([vp_ref[...], vc_ref[...]], axis=0)
        pos_all = jnp.concatenate([pp_ref[...], pc_ref[...]], axis=0)
        ki = lax.broadcasted_iota(jnp.int32, (2 * BLOCK, BLOCK), 0)
        qi = lax.broadcasted_iota(jnp.int32, (2 * BLOCK, BLOCK), 1)
        window = (ki > qi) & (ki <= qi + WINDOW)
        for sub in range(_SWA_PER_STEP):
            band = slice(BLOCK * sub, BLOCK * (sub + 2))
            own = slice(BLOCK * sub, BLOCK * (sub + 1))
            kb, vb = k_all[band], v_all[band]
            dist = jnp.abs(pos_all[band] - pq_ref[:, own])
            valid = window & ((n > 0) | (ki >= BLOCK)) if sub == 0 else window
            q_t, vb_t = q_ref[own, :].T, vb.T
            out_t, lse = [], []
            for kh in range(N_KV_A):
                st_g = _dot(_head_cols(kb, kh).astype(BF16), _group_t(q_t, kh))
                ps = []
                for j in range(_GROUP_A):
                    h = _GROUP_A * kh + j
                    st = _swa_scores_t(st_g, j, h, dist, valid)
                    sink = sink_ref[0:1, h:h + 1] * _LOG2E
                    m = jnp.maximum(jnp.max(st, axis=0, keepdims=True), sink)
                    e = jnp.exp2(st - m)
                    den = jnp.sum(e, axis=0, keepdims=True) + jnp.exp2(sink - m)
                    ps.append((e * (1.0 / den)).astype(BF16))
                    lse.append(m + jnp.log(den) * _LOG2E)
                o_g = _dot(_head_rows(vb_t, kh).astype(BF16), jnp.concatenate(ps, axis=1))
                out_t.extend(o_g[:, BLOCK * j:BLOCK * (j + 1)] for j in range(_GROUP_A))
            o_ref[own, :] = jnp.concatenate(out_t, axis=0).T
            l_ref[:, own] = jnp.concatenate(lse, axis=0)

    cur = lambda n: (n, 0)
    prev = lambda n: jnp.maximum(_SWA_PER_STEP * n - 1, 0)
    return pl.pallas_call(
        body, name="swa_fwd", grid=(t // span,),
        in_specs=[pl.BlockSpec((span, WIDTH_A), lambda n: (n, _QA_BLK)),
                  pl.BlockSpec((span, LANES), lambda n: (n, _KA_BLK)),
                  pl.BlockSpec((BLOCK, LANES), lambda n: (prev(n), _KA_BLK)),
                  pl.BlockSpec((span, LANES), lambda n: (n, _VA_BLK)),
                  pl.BlockSpec((BLOCK, LANES), lambda n: (prev(n), _VA_BLK)),
                  pl.BlockSpec((1, span), lambda n: (0, n)),
                  pl.BlockSpec((span, 1), cur),
                  pl.BlockSpec((BLOCK, 1), lambda n: (prev(n), 0)),
                  _full((1, N_HEADS_A))],
        out_specs=[pl.BlockSpec((span, WIDTH_A), cur), pl.BlockSpec((N_HEADS_A, span), lambda n: (0, n))],
        out_shape=[jax.ShapeDtypeStruct((t, WIDTH_A), F32), jax.ShapeDtypeStruct((N_HEADS_A, t), F32)],
        compiler_params=_params(("parallel",)),
    )(proj, proj, proj, proj, proj, posr, posc, posc, sinks)


def _rope_coeffs(pos, freq):
    ang = pos * freq
    cosv, sinv = jnp.cos(ang), jnp.sin(ang)
    lane = lax.broadcasted_iota(jnp.int32, ang.shape, 1)
    lo = (lane >= QK_NOPE) & (lane < QK_NOPE + QK_ROPE // 2)
    hi = (lane >= QK_NOPE + QK_ROPE // 2) & (lane < QK_NOPE + QK_ROPE)
    c = jnp.where(lane < QK_NOPE, 1.0, jnp.where(lo | hi, cosv, 0.0))
    s = jnp.where(lo, -sinv, jnp.where(hi, sinv, 0.0))
    return c, s, lo, hi


def _rope(xh, c, s, lo):
    up = pltpu.roll(xh, LANES - QK_ROPE // 2, axis=1)
    dn = pltpu.roll(xh, QK_ROPE // 2, axis=1)
    return xh * c + jnp.where(lo, up, dn) * s


def _unrope(dh, c, s, lo, hi):
    g = dh * s
    up = pltpu.roll(g, LANES - QK_ROPE // 2, axis=1)
    dn = pltpu.roll(g, QK_ROPE // 2, axis=1)
    return dh * c + jnp.where(hi, dn, jnp.where(lo, up, 0.0))


_TQ = 512
_MLA_SCALE = Q_HEAD_B ** -0.5


def _mla_prep_fwd(proj, posc, freq, qan, kvan, wq, wk, wv):
    t = proj.shape[0]
    tm = _TQ
    nb = t // tm

    def body(cq_ref, ckv_ref, kr_ref, pos_ref, f_ref, qan_ref, kvan_ref, wq_ref, wk_ref, wv_ref,
             q_ref, k_ref, qt_ref, kt_ref, vt_ref):
        cq = cq_ref[...]
        cqn = ((cq * _rms(cq)) * qan_ref[...]).astype(BF16)
        ckv = ckv_ref[...]
        ckvn = ((ckv * _rms(ckv)) * kvan_ref[...]).astype(BF16)
        qb = _dot(cqn, wq_ref[...])
        kb = _dot(ckvn, wk_ref[...])
        vbt = _dot_nt(wv_ref[...], ckvn)
        c, s, lo, _ = _rope_coeffs(pos_ref[...], f_ref[...])
        kr = _rope(pltpu.roll(kr_ref[...], QK_NOPE, axis=1), c, s, lo)
        for h in range(N_HEADS_B):
            sl = slice(HEAD_PAD * h, HEAD_PAD * (h + 1))
            q_h = _rope(qb[:, sl], c, s, lo)
            k_h = kb[:, sl] + kr
            q_ref[:, sl] = q_h.astype(BF16)
            k_ref[:, sl] = k_h.astype(BF16)
            qt_ref[h, 0] = q_h.T.astype(BF16)
            kt_ref[h, 0] = k_h.T.astype(BF16)
            vt_ref[h, 0] = vbt[V_DIM_B * h:V_DIM_B * (h + 1), :].astype(BF16)

    row = lambda i: (i, 0)
    blk4 = lambda d: pl.BlockSpec((N_HEADS_B, 1, d, tm), lambda i: (0, i, 0, 0))
    return pl.pallas_call(
        body, name="mla_prep_fwd", grid=(nb,),
        in_specs=[pl.BlockSpec((tm, Q_LORA), lambda i: (i, _CQ_BLK)),
                  pl.BlockSpec((tm, LANES), lambda i: (i, _CKV_BLK)),
                  pl.BlockSpec((tm, LANES), lambda i: (i, _KR_BLK)),
                  pl.BlockSpec((tm, 1), row), _full((1, LANES)), _full((1, Q_LORA)), _full((1, KV_LORA)),
                  _full((Q_LORA, MLA_W)), _full((KV_LORA, MLA_W)), _full((N_HEADS_B * V_DIM_B, KV_LORA))],
        out_specs=[pl.BlockSpec((tm, MLA_W), row), pl.BlockSpec((tm, MLA_W), row), blk4(HEAD_PAD), blk4(HEAD_PAD),
                   blk4(V_DIM_B)],
        out_shape=[jax.ShapeDtypeStruct((t, MLA_W), BF16), jax.ShapeDtypeStruct((t, MLA_W), BF16),
                   jax.ShapeDtypeStruct((N_HEADS_B, nb, HEAD_PAD, tm), BF16),
                   jax.ShapeDtypeStruct((N_HEADS_B, nb, HEAD_PAD, tm), BF16),
                   jax.ShapeDtypeStruct((N_HEADS_B, nb, V_DIM_B, tm), BF16)],
        compiler_params=_params(("parallel",)),
    )(proj, proj, proj, posc, freq, qan, kvan, wq, wk, wv)


_MLA_SCALE2 = _MLA_SCALE * _LOG2E


def _mla_fwd(k, qt, vt, w_src):
    t = k.shape[0]
    nb = t // _TQ
    groups = nb // 2

    def body(k_ref, qt_ref, vt_ref, w_ref, o_ref, l_ref, wg_ref, raw, send_sems, recv_sems, local_sem):
        g = pl.program_id(1)
        first = (pl.program_id(0) == 0) & (g == 0)
        last = (pl.program_id(0) == N_HEADS_B - 1) & (g == groups - 1)

        @pl.when(first)
        def _():
            _gather_start(w_ref, wg_ref, send_sems, recv_sems, local_sem)

        def keys(kj):
            return k_ref[pl.ds(pl.multiple_of(kj * _TQ, _TQ), _TQ), :]

        def products(kj, slot):
            kv = keys(kj)
            raw[slot, 0] = _dot(kv, qt_ref[0, 0])
            raw[slot, 1] = _dot(kv, qt_ref[0, 1])

        def update(stats, raw_ref, kj, diagonal=False):
            m, l, acc = stats
            scores = raw_ref[...]
            if diagonal:
                key = lax.broadcasted_iota(jnp.int32, scores.shape, 0)
                qry = lax.broadcasted_iota(jnp.int32, scores.shape, 1)
                scores = jnp.where(key <= qry, scores, NEG)
            m_new = jnp.maximum(m, jnp.max(scores, axis=0, keepdims=True) * _MLA_SCALE2)
            alpha = jnp.exp2(m - m_new)
            p = jnp.exp2(scores * _MLA_SCALE2 - m_new).astype(BF16)
            pv = _dot(jnp.concatenate([vt_ref[0, kj], jnp.ones((16, _TQ), BF16)], axis=0), p)
            return m_new, alpha * l + pv[V_DIM_B:V_DIM_B + 8], alpha * acc + pv[:V_DIM_B]

        def trip(i, stats):
            sa, sb = stats
            products(2 * i + 1, 1)
            sa, sb = update(sa, raw.at[0, 0], 2 * i), update(sb, raw.at[0, 1], 2 * i)
            products(2 * i + 2, 0)
            return update(sa, raw.at[1, 0], 2 * i + 1), update(sb, raw.at[1, 1], 2 * i + 1)

        init = (jnp.full((1, _TQ), NEG, F32), jnp.zeros((8, _TQ), F32), jnp.zeros((V_DIM_B, _TQ), F32))
        products(0, 0)
        sa, sb = lax.fori_loop(0, g, trip, (init, init))
        raw[1, 1] = _dot(keys(2 * g + 1), qt_ref[0, 1])
        sa = update(sa, raw.at[0, 0], 2 * g, True)
        sb = update(update(sb, raw.at[0, 1], 2 * g), raw.at[1, 1], 2 * g + 1, True)
        for which, (m, l, acc) in enumerate((sa, sb)):
            o_ref[0, which] = acc / l[0:1]
            l_ref[0, which] = m + jnp.log(l[0:1]) * _LOG2E

        @pl.when(last)
        def _():
            _gather_wait(w_ref, wg_ref, send_sems, recv_sems, local_sem)

    two = lambda d: pl.BlockSpec((1, 2, d, _TQ), lambda h, g: (h, g, 0, 0))
    return pl.pallas_call(
        body, name="mla_fwd", grid=(N_HEADS_B, groups),
        in_specs=[pl.BlockSpec((t, HEAD_PAD), lambda h, g: (0, h)), two(HEAD_PAD),
                  pl.BlockSpec((1, nb, V_DIM_B, _TQ), lambda h, g: (h, 0, 0, 0)), _HBM],
        out_specs=[two(V_DIM_B), two(1), _HBM],
        out_shape=[jax.ShapeDtypeStruct((N_HEADS_B, nb, V_DIM_B, _TQ), F32),
                   jax.ShapeDtypeStruct((N_HEADS_B, nb, 1, _TQ), F32),
                   jax.ShapeDtypeStruct((N_CHIPS,) + w_src.shape, w_src.dtype)],
        scratch_shapes=[pltpu.VMEM((2, 2, _TQ, _TQ), F32),
                        pltpu.SemaphoreType.DMA((3,)), pltpu.SemaphoreType.DMA((3,)), pltpu.SemaphoreType.DMA(())],
        compiler_params=_params(("arbitrary", "arbitrary")),
    )(k, qt, vt, w_src)


def _ot_spec(tm, d):
    per = _TQ // tm
    return pl.BlockSpec((N_HEADS_B, 1, d, tm), lambda i: (0, i // per, 0, i % per))


def _mix_out_fwd(out_a, out_bt, proj, x, w_oa, w_ob, wb, g2, g3):
    t = x.shape[0]
    tm = 512

    def body(oa_ref, obt_ref, ga_ref, gb_ref, x_ref, woa_ref, wob_ref, wout_ref, g2_ref, g3_ref,
             mg_ref, y_ref, x1_ref, h2_ref):
        oa = _dot(oa_ref[...].astype(BF16), woa_ref[...])
        obt = obt_ref[...].reshape(N_HEADS_B * V_DIM_B, tm).astype(BF16)
        ob = _dot_tn(obt, wob_ref[...])
        merged = (jax.nn.sigmoid(ga_ref[...]) * oa + jax.nn.sigmoid(gb_ref[...]) * ob).astype(BF16)
        mg_ref[...] = merged
        y = _dot(merged, wout_ref[...].reshape(D_MODEL, D_MODEL))
        y_ref[...] = y
        x1 = x_ref[...] + (y * _rms(y)) * g2_ref[...]
        x1_ref[...] = x1
        h2_ref[...] = ((x1 * _rms(x1)) * g3_ref[...]).astype(BF16)

    row = lambda i: (i, 0)
    blk = pl.BlockSpec((tm, D_MODEL), row)
    return pl.pallas_call(
        body, name="mix_out_fwd", grid=(t // tm,),
        in_specs=[pl.BlockSpec((tm, WIDTH_A), row), _ot_spec(tm, V_DIM_B), pl.BlockSpec((tm, D_MODEL), lambda i: (i, 0)),
                  pl.BlockSpec((tm, D_MODEL), lambda i: (i, 1)), blk,
                  _full((WIDTH_A, D_MODEL)), _full((N_HEADS_B * V_DIM_B, D_MODEL)), _wb_spec("w_out"),
                  _full((1, D_MODEL)), _full((1, D_MODEL))],
        out_specs=[blk, blk, blk, blk],
        out_shape=[jax.ShapeDtypeStruct((t, D_MODEL), BF16), jax.ShapeDtypeStruct((t, D_MODEL), F32),
                   jax.ShapeDtypeStruct((t, D_MODEL), F32), jax.ShapeDtypeStruct((t, D_MODEL), BF16)],
        compiler_params=_params(("parallel",)),
    )(out_a, out_bt, proj, proj, x, w_oa, w_ob, wb, g2, g3)


_TM_MLP = 512


def _row_halves(tm):
    return slice(0, tm // 2), slice(tm // 2, tm)


def _up_fwd(h2, wb):
    t = h2.shape[0]
    tm = _TM_MLP

    def body(h_ref, w_ref, a_ref):
        hv = h_ref[...]
        for j in range(N_CHIPS):
            u = _dot(hv, w_ref[j])
            a_ref[:, D_MODEL * j:D_MODEL * (j + 1)] = jnp.square(jnp.maximum(u, 0.0)).astype(BF16)

    return pl.pallas_call(
        body, name="up_fwd", grid=(t // tm,),
        in_specs=[pl.BlockSpec((tm, D_MODEL), lambda i: (i, 0)), _wb_spec("w_up")],
        out_specs=pl.BlockSpec((tm, D_FF), lambda i: (i, 0)),
        out_shape=jax.ShapeDtypeStruct((t, D_FF), BF16),
        compiler_params=_params(("parallel",)),
    )(h2, wb)


def _down_fwd_loss(a, wb, x1, target, g4):
    t = a.shape[0]
    tm = _TM_MLP

    def body(a_ref, w_ref, x1_ref, tg_ref, g_ref, dx2_ref, dyd_ref, dg_ref, loss_ref):
        @pl.when(pl.program_id(0) == 0)
        def _():
            dg_ref[...] = jnp.zeros(dg_ref.shape, F32)
            loss_ref[...] = jnp.zeros(loss_ref.shape, F32)

        yd = _dot(a_ref[...], w_ref[...].reshape(D_FF, D_MODEL))
        r = _rms(yd)
        n = yd * r
        diff = (x1_ref[...] + n * g_ref[...]) - tg_ref[...]
        loss_ref[...] += 0.5 * jnp.sum(jnp.mean(diff * diff, axis=-1, keepdims=True), axis=0, keepdims=True)
        dx2 = diff * (1.0 / D_MODEL)
        dx2_ref[...] = dx2
        dyd, dg = _norm_bwd(dx2, n, r, g_ref[...])
        dyd_ref[...] = dyd.astype(BF16)
        dg_ref[...] += dg

    row = lambda i: (i, 0)
    blk = pl.BlockSpec((tm, D_MODEL), row)
    return pl.pallas_call(
        body, name="down_fwd_loss", grid=(t // tm,),
        in_specs=[pl.BlockSpec((tm, D_FF), row), _wb_spec("w_down"), blk, blk, _full((1, D_MODEL))],
        out_specs=[blk, blk, _full((1, D_MODEL)), _full((1, LANES))],
        out_shape=[jax.ShapeDtypeStruct((t, D_MODEL), F32), jax.ShapeDtypeStruct((t, D_MODEL), BF16),
                   jax.ShapeDtypeStruct((1, D_MODEL), F32), jax.ShapeDtypeStruct((1, LANES), F32)],
        compiler_params=_params(("arbitrary",)),
    )(a, wb, x1, target, g4)


def _matmul_tn(a, b, name, tm, tn, tk=1024):
    t, m = a.shape
    n = b.shape[1]
    tk = min(tk, t)
    nk = t // tk

    def body(a_ref, b_ref, o_ref):
        @pl.when(pl.program_id(2) == 0)
        def _():
            o_ref[...] = jnp.zeros(o_ref.shape, F32)

        o_ref[...] += _dot_tn(a_ref[...].astype(BF16), b_ref[...].astype(BF16))

    return pl.pallas_call(
        body, name=name, grid=(m // tm, n // tn, nk),
        in_specs=[pl.BlockSpec((tk, tm), lambda i, j, k: (k, i)), pl.BlockSpec((tk, tn), lambda i, j, k: (k, j))],
        out_specs=pl.BlockSpec((tm, tn), lambda i, j, k: (i, j)),
        out_shape=jax.ShapeDtypeStruct((m, n), F32),
        compiler_params=_params(("parallel", "parallel", "arbitrary")),
    )(a, b)


_TK_DW = 2048


def _dw_into_blocks(a, b, weight, tm, tk, buf=None):
    t, m = a.shape
    n = b.shape[1]
    tk = min(tk, t)
    nk = t // tk
    rows = PACK_ROWS[weight]
    br = min(tm, rows)
    chips = tm // br
    first = _row_offset(GROUP_B, weight) // br
    per_chip = rows // br
    if weight == "w_up":
        out_map = lambda i, j, k: (j, first + i, 0)
    elif chips > 1:
        out_map = lambda i, j, k: (i, first, 0)
    else:
        out_map = lambda i, j, k: (i // per_chip, first + i % per_chip, 0)

    def body(a_ref, b_ref, *rest):
        o_ref = rest[-1]

        @pl.when(pl.program_id(2) == 0)
        def _():
            o_ref[...] = jnp.zeros(o_ref.shape, F32)

        o_ref[...] += _dot_tn(a_ref[...].astype(BF16), b_ref[...].astype(BF16)).reshape(o_ref.shape)

    in_specs = [pl.BlockSpec((tk, tm), lambda i, j, k: (k, i)), pl.BlockSpec((tk, D_MODEL), lambda i, j, k: (k, j))]
    operands = [a, b]
    if buf is not None:
        in_specs.append(pl.BlockSpec(memory_space=pl.ANY))
        operands.append(buf)
    total = sum(PACK_ROWS[w] for w in GROUP_B)
    return pl.pallas_call(
        body, name="dw_" + weight[2:], grid=(m // tm, n // D_MODEL, nk),
        in_specs=in_specs, out_specs=pl.BlockSpec((chips, br, D_MODEL), out_map),
        out_shape=jax.ShapeDtypeStruct((N_CHIPS, total, D_MODEL), F32),
        input_output_aliases={} if buf is None else {2: 0},
        compiler_params=_params(("parallel", "parallel", "arbitrary")),
    )(*operands)


def _down_bwd(dyd, wb, a):
    t = dyd.shape[0]
    tm = _TM_MLP

    def body(d_ref, w_ref, a_ref, du_ref):
        dv = d_ref[...]
        for j in range(N_CHIPS):
            cols = slice(D_MODEL * j, D_MODEL * (j + 1))
            av = a_ref[:, cols].astype(F32)
            relu_u = jnp.where(av > 0.0, av * lax.rsqrt(av), 0.0)
            du_ref[:, cols] = (_dot_nt(dv, w_ref[j]) * (2.0 * relu_u)).astype(BF16)

    row = lambda i: (i, 0)
    return pl.pallas_call(
        body, name="down_bwd", grid=(t // tm,),
        in_specs=[pl.BlockSpec((tm, D_MODEL), row), _wb_spec("w_down"), pl.BlockSpec((tm, D_FF), row)],
        out_specs=pl.BlockSpec((tm, D_FF), row),
        out_shape=jax.ShapeDtypeStruct((t, D_FF), BF16),
        compiler_params=_params(("parallel",)),
    )(dyd, wb, a)


def _up_bwd(du, wb, x1, dx2, y, g3, g2):
    t = du.shape[0]
    tm = _TM_MLP

    def body(du_ref, w_ref, x1_ref, dx2_ref, y_ref, g3_ref, g2_ref, dx1_ref, dy_ref, dg3_ref, dg2_ref):
        @pl.when(pl.program_id(0) == 0)
        def _():
            dg3_ref[...] = jnp.zeros(dg3_ref.shape, F32)
            dg2_ref[...] = jnp.zeros(dg2_ref.shape, F32)

        dh2 = _dot_nt(du_ref[:, 0:D_MODEL], w_ref[0])
        for j in range(1, N_CHIPS):
            dh2 = dh2 + _dot_nt(du_ref[:, D_MODEL * j:D_MODEL * (j + 1)], w_ref[j])
        x1 = x1_ref[...]
        r3 = _rms(x1)
        d3, dg3 = _norm_bwd(dh2, x1 * r3, r3, g3_ref[...])
        dx1 = dx2_ref[...] + d3
        dx1_ref[...] = dx1
        dg3_ref[...] += dg3
        y = y_ref[...]
        r2 = _rms(y)
        dy, dg2 = _norm_bwd(dx1, y * r2, r2, g2_ref[...])
        dy_ref[...] = dy.astype(BF16)
        dg2_ref[...] += dg2

    row = lambda i: (i, 0)
    blk = pl.BlockSpec((tm, D_MODEL), row)
    return pl.pallas_call(
        body, name="up_bwd", grid=(t // tm,),
        in_specs=[pl.BlockSpec((tm, D_FF), row), _wb_spec("w_up"),
                  blk, blk, blk, _full((1, D_MODEL)), _full((1, D_MODEL))],
        out_specs=[blk, blk, _full((1, D_MODEL)), _full((1, D_MODEL))],
        out_shape=[jax.ShapeDtypeStruct((t, D_MODEL), F32), jax.ShapeDtypeStruct((t, D_MODEL), BF16),
                   jax.ShapeDtypeStruct((1, D_MODEL), F32), jax.ShapeDtypeStruct((1, D_MODEL), F32)],
        compiler_params=_params(("arbitrary",)),
    )(du, wb, x1, dx2, y, g3, g2)


def _mix_out_bwd(dy, out_a, out_bt, proj, w_oa, w_ob, wb):
    t = dy.shape[0]
    tm = 256
    nb = t // _TQ

    def body(dy_ref, oa_ref, obt_ref, ga_ref, gb_ref, woa_ref, wob_ref, wout_ref,
             doa_ref, dob_ref, dg_ref, da_ref, db_ref, dbt_ref, dela_ref, delb_ref):
        dm = _dot_nt(dy_ref[...], wout_ref[...].reshape(D_MODEL, D_MODEL))
        out_a_v = oa_ref[...]
        out_bt_v = obt_ref[...].reshape(N_HEADS_B * V_DIM_B, tm)
        oa = _dot(out_a_v.astype(BF16), woa_ref[...])
        ob = _dot_tn(out_bt_v.astype(BF16), wob_ref[...])
        sa, sb = jax.nn.sigmoid(ga_ref[...]), jax.nn.sigmoid(gb_ref[...])
        doa = (dm * sa).astype(BF16)
        dob = (dm * sb).astype(BF16)
        doa_ref[...] = doa
        dob_ref[...] = dob
        dg_ref[:, :D_MODEL] = (dm * oa * (sa * (1.0 - sa))).astype(BF16)
        dg_ref[:, D_MODEL:] = (dm * ob * (sb * (1.0 - sb))).astype(BF16)
        d_out_a = _dot_nt(doa, woa_ref[...])
        da_ref[...] = d_out_a
        prod_at = (d_out_a * out_a_v).T
        dela_ref[...] = jnp.concatenate(
            [jnp.sum(_head_rows(prod_at, h), axis=0, keepdims=True) for h in range(N_HEADS_A)], axis=0)
        d_out_b = _dot_nt(dob, wob_ref[...])
        d_out_bt = _dot_nt(wob_ref[...], dob)
        prod_bt = d_out_bt * out_bt_v
        for h in range(N_HEADS_B):
            db_ref[h] = d_out_b[:, V_DIM_B * h:V_DIM_B * (h + 1)].astype(BF16)
            dbt_ref[h, 0] = d_out_bt[V_DIM_B * h:V_DIM_B * (h + 1), :].astype(BF16)
            delb_ref[h, 0] = jnp.sum(prod_bt[V_DIM_B * h:V_DIM_B * (h + 1), :], axis=0, keepdims=True)

    row = lambda i: (i, 0)
    blk = pl.BlockSpec((tm, D_MODEL), row)
    return pl.pallas_call(
        body, name="mix_out_bwd", grid=(t // tm,),
        in_specs=[blk, pl.BlockSpec((tm, WIDTH_A), row), _ot_spec(tm, V_DIM_B),
                  pl.BlockSpec((tm, D_MODEL), lambda i: (i, 0)), pl.BlockSpec((tm, D_MODEL), lambda i: (i, 1)),
                  _full((WIDTH_A, D_MODEL)), _full((N_HEADS_B * V_DIM_B, D_MODEL)), _wb_spec("w_out")],
        out_specs=[blk, blk, pl.BlockSpec((tm, 2 * D_MODEL), row), pl.BlockSpec((tm, WIDTH_A), row),
                   pl.BlockSpec((N_HEADS_B, tm, V_DIM_B), lambda i: (0, i, 0)), _ot_spec(tm, V_DIM_B),
                   pl.BlockSpec((N_HEADS_A, tm), lambda i: (0, i)), _ot_spec(tm, 1)],
        out_shape=[jax.ShapeDtypeStruct((t, D_MODEL), BF16)] * 2
        + [jax.ShapeDtypeStruct((t, D_IN_PAD), BF16), jax.ShapeDtypeStruct((t, WIDTH_A), F32), jax.ShapeDtypeStruct((N_HEADS_B, t, V_DIM_B), BF16),
           jax.ShapeDtypeStruct((N_HEADS_B, nb, V_DIM_B, _TQ), BF16), jax.ShapeDtypeStruct((N_HEADS_A, t), F32),
           jax.ShapeDtypeStruct((N_HEADS_B, nb, 1, _TQ), F32)],
        compiler_params=_params(("parallel",)),
    )(dy, out_a, out_bt, proj, proj, w_oa, w_ob, wb)


def _dw_ob(out_bt, dob):
    t = dob.shape[0]
    nb = t // _TQ

    def body(obt_ref, dob_ref, o_ref):
        @pl.when(pl.program_id(0) == 0)
        def _():
            o_ref[...] = jnp.zeros(o_ref.shape, F32)

        obt = obt_ref[...].reshape(N_HEADS_B * V_DIM_B, _TQ).astype(BF16)
        o_ref[...] += _dot(obt, dob_ref[...])

    return pl.pallas_call(
        body, name="dw_o_b", grid=(nb,),
        in_specs=[pl.BlockSpec((N_HEADS_B, 1, V_DIM_B, _TQ), lambda i: (0, i, 0, 0)),
                  pl.BlockSpec((_TQ, D_MODEL), lambda i: (i, 0))],
        out_specs=_full((N_HEADS_B * V_DIM_B, D_MODEL)),
        out_shape=jax.ShapeDtypeStruct((N_HEADS_B * V_DIM_B, D_MODEL), F32),
        compiler_params=_params(("arbitrary",)),
    )(out_bt, dob)


def _mla_bwd(q, k, qt, kt, vt, d_out, d_out_t, lse, delta, gp):
    t = q.shape[0]
    nb = t // _TQ

    def body(k_ref, kt_ref, vt_ref, q_ref, qt_ref, do_ref, dot_ref, lrow_ref, drow_ref, gp_ref,
             dq_ref, dkt_ref, dvt_ref, land_ref, l_rep, d_rep, send_sems, recv_sems):
        step = pl.program_id(1)
        kj = nb - 1 - step

        @pl.when((pl.program_id(0) == 0) & (step == 0))
        def _():
            _scatter_start(gp_ref, land_ref, send_sems, recv_sems)

        @pl.when(step == 0)
        def _():
            dq_ref[...] = jnp.zeros(dq_ref.shape, F32)
            for b in range(nb):
                l_rep[_TQ * b:_TQ * (b + 1), :] = jnp.broadcast_to(lrow_ref[0, b], (LANES, _TQ)).T
                d_rep[_TQ * b:_TQ * (b + 1), :] = jnp.broadcast_to(drow_ref[0, b], (LANES, _TQ)).T

        kv, k_t, v_t = k_ref[...], kt_ref[0, 0], vt_ref[0, 0]

        def rows_of(qi):
            return pl.ds(pl.multiple_of(qi * _TQ, _TQ), _TQ)

        def products(qi, diagonal=False):
            s = _dot(q_ref[rows_of(qi), :], k_t) * _MLA_SCALE2
            if diagonal:
                qry = lax.broadcasted_iota(jnp.int32, s.shape, 0)
                key = lax.broadcasted_iota(jnp.int32, s.shape, 1)
                s = jnp.where(key <= qry, s, NEG)
            return s, _dot(do_ref[0, rows_of(qi), :], v_t)

        def update(carry, prods, qi):
            dkt, dvt = carry
            s, dp = prods
            lse, delta = l_rep[rows_of(qi), :], d_rep[rows_of(qi), :]
            ps, dss = [], []
            for c in range(_TQ // LANES):
                strip = slice(LANES * c, LANES * (c + 1))
                p = jnp.exp2(s[:, strip] - lse)
                ps.append(p.astype(BF16))
                dss.append((p * (dp[:, strip] - delta) * _MLA_SCALE).astype(BF16))
            p_b, ds_b = jnp.concatenate(ps, axis=1), jnp.concatenate(dss, axis=1)
            dvt = dvt + _dot(dot_ref[0, qi], p_b)
            dkt = dkt + _dot(qt_ref[0, qi], ds_b)
            dq_ref[rows_of(qi), :] += _dot(ds_b, kv)
            return dkt, dvt

        def pair(i, carry):
            qa = kj + 1 + 2 * i
            pa, pb = products(qa), products(qa + 1)
            return update(update(carry, pa, qa), pb, qa + 1)

        init = (jnp.zeros((HEAD_PAD, _TQ), F32), jnp.zeros((V_DIM_B, _TQ), F32))
        carry = update(init, products(kj, True), kj)
        pairs = (nb - 1 - kj) // 2
        carry = lax.fori_loop(0, pairs, pair, carry)
        dkt, dvt = lax.fori_loop(kj + 1 + 2 * pairs, nb, lambda qi, cr: update(cr, products(qi), qi), carry)
        dkt_ref[0, 0] = dkt
        dvt_ref[0, 0] = dvt

        @pl.when((pl.program_id(0) == N_HEADS_B - 1) & (step == nb - 1))
        def _():
            _scatter_wait(gp_ref, land_ref, send_sems, recv_sems)

    head4 = lambda d: pl.BlockSpec((1, nb, d, _TQ), lambda h, s: (h, 0, 0, 0))
    blk4 = lambda d: pl.BlockSpec((1, 1, d, _TQ), lambda h, s: (h, nb - 1 - s, 0, 0))
    head3 = lambda d: pl.BlockSpec((1, t, d), lambda h, kj: (h, 0, 0))
    per_head = pl.BlockSpec((t, HEAD_PAD), lambda h, kj: (0, h))
    return pl.pallas_call(
        body, name="mla_bwd", grid=(N_HEADS_B, nb),
        in_specs=[pl.BlockSpec((_TQ, HEAD_PAD), lambda h, s: (nb - 1 - s, h)), blk4(HEAD_PAD), blk4(V_DIM_B),
                  per_head, head4(HEAD_PAD), head3(V_DIM_B), head4(V_DIM_B), head4(1), head4(1), _HBM],
        out_specs=[per_head, blk4(HEAD_PAD), blk4(V_DIM_B), _HBM],
        out_shape=[jax.ShapeDtypeStruct((t, MLA_W), F32), jax.ShapeDtypeStruct((N_HEADS_B, nb, HEAD_PAD, _TQ), F32),
                   jax.ShapeDtypeStruct((N_HEADS_B, nb, V_DIM_B, _TQ), F32),
                   jax.ShapeDtypeStruct((3,) + gp.shape[1:], gp.dtype)],
        scratch_shapes=[pltpu.VMEM((t, LANES), F32), pltpu.VMEM((t, LANES), F32),
                        pltpu.SemaphoreType.DMA((3,)), pltpu.SemaphoreType.DMA((3,))],
        compiler_params=_params(("arbitrary", "arbitrary")),
    )(k, kt, vt, q, qt, d_out, d_out_t, lse, delta, gp)


def _mla_prep_bwd(dq, dkt, dvt, proj, posc, freq, qan, kvan, wq, wk, wv, swap_src):
    t = dq.shape[0]
    tm = _TQ

    def body(dq_ref, dkt_ref, dvt_ref, cq_ref, ckv_ref, pos_ref, f_ref, qan_ref, kvan_ref, wq_ref, wk_ref, wv_ref, src_ref,
             dcq_ref, dckv_ref, dkr_ref, dwq_ref, dwk_ref, dwv_ref, dqan_ref, dkvan_ref, got_ref, send_sem, recv_sem):
        swap = _sibling_copy(src_ref, got_ref, send_sem, recv_sem)

        @pl.when(pl.program_id(0) == 0)
        def _():
            swap.start()
            for r in (dwq_ref, dwk_ref, dwv_ref, dqan_ref, dkvan_ref):
                r[...] = jnp.zeros(r.shape, F32)

        cq = cq_ref[...]
        rq = _rms(cq)
        nq_ = cq * rq
        cqn = (nq_ * qan_ref[...]).astype(BF16)
        ckv = ckv_ref[...]
        rkv = _rms(ckv)
        nkv = ckv * rkv
        ckvn = (nkv * kvan_ref[...]).astype(BF16)
        c, s, lo, hi = _rope_coeffs(pos_ref[...], f_ref[...])
        dkr = jnp.zeros((tm, LANES), F32)
        dqb, dkb = [], []
        for h in range(N_HEADS_B):
            dqb.append(_unrope(dq_ref[:, HEAD_PAD * h:HEAD_PAD * (h + 1)], c, s, lo, hi).astype(BF16))
            dk_h = dkt_ref[h, 0].T
            dkr = dkr + dk_h
            dkb.append(dk_h.astype(BF16))
        dqb, dkb = jnp.concatenate(dqb, axis=1), jnp.concatenate(dkb, axis=1)
        dkr = jnp.where(lo | hi, _unrope(dkr, c, s, lo, hi), 0.0)
        dkr_ref[...] = pltpu.roll(dkr, LANES - QK_NOPE, axis=1).astype(BF16)
        dvb = dvt_ref[...].reshape(N_HEADS_B * V_DIM_B, tm).T.astype(BF16)
        dwq_ref[...] += _dot_tn(cqn, dqb)
        dwk_ref[...] += _dot_tn(ckvn, dkb)
        dwv_ref[...] += _dot_tn(ckvn, dvb)
        dcqn = _dot_nt(dqb, wq_ref[...])
        dckvn = _dot_nt(dkb, wk_ref[...]) + _dot_nt(dvb, wv_ref[...])
        dcq, dqan = _norm_bwd(dcqn, nq_, rq, qan_ref[...])
        dckv, dkvan = _norm_bwd(dckvn, nkv, rkv, kvan_ref[...])
        dcq_ref[...] = dcq.astype(BF16)
        dckv_ref[...] = dckv.astype(BF16)
        dqan_ref[...] += dqan
        dkvan_ref[...] += dkvan

        @pl.when(pl.program_id(0) == t // tm - 1)
        def _():
            swap.wait_recv()
            swap.wait_send()

    row = lambda i: (i, 0)
    vw = N_HEADS_B * V_DIM_B
    return pl.pallas_call(
        body, name="mla_prep_bwd", grid=(t // tm,),
        in_specs=[pl.BlockSpec((tm, MLA_W), row), pl.BlockSpec((N_HEADS_B, 1, HEAD_PAD, tm), lambda i: (0, i, 0, 0)),
                  pl.BlockSpec((N_HEADS_B, 1, V_DIM_B, tm), lambda i: (0, i, 0, 0)),
                  pl.BlockSpec((tm, Q_LORA), lambda i: (i, _CQ_BLK)),
                  pl.BlockSpec((tm, LANES), lambda i: (i, _CKV_BLK)),
                  pl.BlockSpec((tm, 1), row), _full((1, LANES)), _full((1, Q_LORA)), _full((1, KV_LORA)),
                  _full((Q_LORA, MLA_W)), _full((KV_LORA, MLA_W)), _full((KV_LORA, vw)), _HBM],
        out_specs=[pl.BlockSpec((tm, Q_LORA), row), pl.BlockSpec((tm, LANES), row), pl.BlockSpec((tm, LANES), row),
                   _full((Q_LORA, MLA_W)), _full((KV_LORA, MLA_W)), _full((KV_LORA, vw)),
                   _full((1, Q_LORA)), _full((1, KV_LORA)), _HBM],
        out_shape=[jax.ShapeDtypeStruct((t, Q_LORA), BF16), jax.ShapeDtypeStruct((t, LANES), BF16),
                   jax.ShapeDtypeStruct((t, LANES), BF16),
                   jax.ShapeDtypeStruct((Q_LORA, MLA_W), F32), jax.ShapeDtypeStruct((KV_LORA, MLA_W), F32),
                   jax.ShapeDtypeStruct((KV_LORA, vw), F32),
                   jax.ShapeDtypeStruct((1, Q_LORA), F32), jax.ShapeDtypeStruct((1, KV_LORA), F32),
                   jax.ShapeDtypeStruct(swap_src.shape, swap_src.dtype)],
        scratch_shapes=[pltpu.SemaphoreType.DMA(()), pltpu.SemaphoreType.DMA(())],
        compiler_params=_params(("arbitrary",)),
    )(dq, dkt, dvt, proj, proj, posc, freq, qan, kvan, wq, wk, wv, swap_src)


def _swa_bwd(proj, d_out, lse, delta, posc, posr, sinks):
    t = proj.shape[0]
    per = _SWA_PER_STEP
    span = per * BLOCK
    steps = t // span

    def body(q_ref, kc_ref, kp_ref, vc_ref, vp_ref, do_ref, l_ref, d_ref, pq_ref, pc_ref, pp_ref, sink_ref,
             dq_ref, dk_ref, dv_ref, ds_ref, dkb_s, dvb_s, dk_keep, dv_keep):
        n = pl.program_id(0)

        @pl.when(n == 0)
        def _():
            ds_ref[...] = jnp.zeros(ds_ref.shape, F32)
            dk_keep[...] = jnp.zeros(dk_keep.shape, F32)
            dv_keep[...] = jnp.zeros(dv_keep.shape, F32)

        @pl.when(n < steps)
        def _():
            k_all = jnp.concatenate([kp_ref[...], kc_ref[...]], axis=0)
            v_all = jnp.concatenate([vp_ref[...], vc_ref[...]], axis=0)
            pos_all = jnp.concatenate([pp_ref[...], pc_ref[...]], axis=0)
            ki = lax.broadcasted_iota(jnp.int32, (2 * BLOCK, BLOCK), 0)
            qi = lax.broadcasted_iota(jnp.int32, (2 * BLOCK, BLOCK), 1)
            window = (ki > qi) & (ki <= qi + WINDOW)
            lane = lax.broadcasted_iota(jnp.int32, (1, LANES), 1)
            dsink = jnp.zeros((1, LANES), F32)
            for sub in range(per):
                band = slice(BLOCK * sub, BLOCK * (sub + 2))
                own = slice(BLOCK * sub, BLOCK * (sub + 1))
                kb, vb = k_all[band], v_all[band]
                dist = jnp.abs(pos_all[band] - pq_ref[:, own])
                valid = window & ((n > 0) | (ki >= BLOCK)) if sub == 0 else window
                qv, dov = q_ref[own, :], do_ref[own, :]
                q_t, do_t, kb_t = qv.T, dov.T, kb.T
                dq_t = []
                for kh in range(N_KV_A):
                    heads = range(_GROUP_A * kh, _GROUP_A * (kh + 1))
                    st_g = _dot(_head_cols(kb, kh).astype(BF16), _group_t(q_t, kh))
                    dpt_g = _dot(_head_cols(vb, kh).astype(BF16), _group_t(do_t, kh))
                    pts, dsts = [], []
                    for j, h in enumerate(heads):
                        st = _swa_scores_t(st_g, j, h, dist, valid)
                        l_h, d_h = l_ref[h:h + 1, own], d_ref[h:h + 1, own]
                        pt = jnp.exp2(st - l_h)
                        p_sink = jnp.exp2(sink_ref[0:1, h:h + 1] * _LOG2E - l_h)
                        dsink = dsink + jnp.where(lane == h, jnp.sum(-p_sink * d_h, axis=1, keepdims=True), 0.0)
                        dst = pt * (dpt_g[:, BLOCK * j:BLOCK * (j + 1)] - d_h) * _SWA_SCALE
                        pts.append(pt.astype(BF16))
                        dsts.append(dst.astype(BF16))
                    pt_g, dst_g = jnp.concatenate(pts, axis=1), jnp.concatenate(dsts, axis=1)
                    q_g = jnp.concatenate([_head_cols(qv, h) for h in heads], axis=0).astype(BF16)
                    do_g = jnp.concatenate([_head_cols(dov, h) for h in heads], axis=0).astype(BF16)
                    dkb_s[sub, :, HEAD_DIM_A * kh:HEAD_DIM_A * (kh + 1)] = _dot(dst_g, q_g)
                    dvb_s[sub, :, HEAD_DIM_A * kh:HEAD_DIM_A * (kh + 1)] = _dot(pt_g, do_g)
                    dq_g = _dot(_head_rows(kb_t, kh).astype(BF16), dst_g)
                    dq_t.extend(dq_g[:, BLOCK * j:BLOCK * (j + 1)] for j in range(_GROUP_A))
                dq_ref[own, :] = jnp.concatenate(dq_t, axis=0).T
            ds_ref[...] += dsink
            for keep, out, parts in ((dk_keep, dk_ref, dkb_s), (dv_keep, dv_ref, dvb_s)):
                out[0:span - BLOCK, :] = keep[0:span - BLOCK, :]
                out[span - BLOCK:span, :] = keep[span - BLOCK:span, :] + parts[0, 0:BLOCK, :]
                for s in range(per - 1):
                    keep[BLOCK * s:BLOCK * (s + 1), :] = parts[s, BLOCK:2 * BLOCK, :] + parts[s + 1, 0:BLOCK, :]
                keep[span - BLOCK:span, :] = parts[per - 1, BLOCK:2 * BLOCK, :]

        @pl.when(n == steps)
        def _():
            dk_ref[...] = dk_keep[...]
            dv_ref[...] = dv_keep[...]

    last = steps - 1
    cur = lambda n: (jnp.minimum(n, last), 0)
    cur_t = lambda n: (0, jnp.minimum(n, last))
    prv = lambda n: jnp.maximum(per * jnp.minimum(n, last) - 1, 0)
    out_prev = lambda n: (jnp.maximum(n - 1, 0), 0)
    return pl.pallas_call(
        body, name="swa_bwd", grid=(steps + 1,),
        in_specs=[pl.BlockSpec((span, WIDTH_A), lambda n: (jnp.minimum(n, last), _QA_BLK)),
                  pl.BlockSpec((span, LANES), lambda n: (jnp.minimum(n, last), _KA_BLK)),
                  pl.BlockSpec((BLOCK, LANES), lambda n: (prv(n), _KA_BLK)),
                  pl.BlockSpec((span, LANES), lambda n: (jnp.minimum(n, last), _VA_BLK)),
                  pl.BlockSpec((BLOCK, LANES), lambda n: (prv(n), _VA_BLK)),
                  pl.BlockSpec((span, WIDTH_A), cur), pl.BlockSpec((N_HEADS_A, span), cur_t),
                  pl.BlockSpec((N_HEADS_A, span), cur_t), pl.BlockSpec((1, span), cur_t),
                  pl.BlockSpec((span, 1), cur), pl.BlockSpec((BLOCK, 1), lambda n: (prv(n), 0)),
                  _full((1, N_HEADS_A))],
        out_specs=[pl.BlockSpec((span, WIDTH_A), cur), pl.BlockSpec((span, LANES), out_prev),
                   pl.BlockSpec((span, LANES), out_prev), _full((1, LANES))],
        out_shape=[jax.ShapeDtypeStruct((t, WIDTH_A), F32), jax.ShapeDtypeStruct((t, LANES), F32),
                   jax.ShapeDtypeStruct((t, LANES), F32), jax.ShapeDtypeStruct((1, LANES), F32)],
        scratch_shapes=[pltpu.VMEM((per, 2 * BLOCK, LANES), F32), pltpu.VMEM((per, 2 * BLOCK, LANES), F32),
                        pltpu.VMEM((span, LANES), F32), pltpu.VMEM((span, LANES), F32)],
        compiler_params=_params(("arbitrary",)),
    )(proj, proj, proj, proj, proj, d_out, lse, delta, posr, posc, posc, sinks)


def _in_bwd(dproj, w_in_t, x, dx1, g1, gp):
    t = x.shape[0]
    tm = 512
    steps = t // tm

    def body(dp_ref, w_ref, x_ref, dx1_ref, g_ref, gp_ref, dx_ref, dg_ref, land_ref, send_sems, recv_sems):
        i = pl.program_id(0)

        @pl.when(i == 0)
        def _():
            dg_ref[...] = jnp.zeros(dg_ref.shape, F32)
            _scatter_start(gp_ref, land_ref, send_sems, recv_sems)

        for rows in _row_halves(tm):
            dh = _dot(dp_ref[rows, :], w_ref[...])
            xv = x_ref[rows, :]
            r = _rms(xv)
            dx, dg = _norm_bwd(dh, xv * r, r, g_ref[...])
            dx_ref[rows, :] = dx1_ref[rows, :] + dx
            dg_ref[...] += dg

        @pl.when(i == steps - 1)
        def _():
            _scatter_wait(gp_ref, land_ref, send_sems, recv_sems)

    row = lambda i: (i, 0)
    blk = pl.BlockSpec((tm, D_MODEL), row)
    return pl.pallas_call(
        body, name="in_bwd", grid=(steps,),
        in_specs=[pl.BlockSpec((tm, D_IN_PAD), row), _full((D_IN_PAD, D_MODEL)), blk, blk, _full((1, D_MODEL)), _HBM],
        out_specs=[blk, _full((1, D_MODEL)), _HBM],
        out_shape=[jax.ShapeDtypeStruct((t, D_MODEL), F32), jax.ShapeDtypeStruct((1, D_MODEL), F32),
                   jax.ShapeDtypeStruct((3,) + gp.shape[1:], gp.dtype)],
        scratch_shapes=[pltpu.SemaphoreType.DMA((3,)), pltpu.SemaphoreType.DMA((3,))],
        compiler_params=_params(("arbitrary",)),
    )(dproj, w_in_t, x, dx1, g1, gp)


def _adamw_store(w, g, m, v, out_refs):
    g_out, d_out, m_out, v_out = out_refs
    m_new = ADAM_B1 * m + (1.0 - ADAM_B1) * g
    v_new = ADAM_B2 * v + (1.0 - ADAM_B2) * jnp.square(g)
    m_hat = m_new / (1.0 - ADAM_B1 ** ADAM_STEP)
    v_hat = v_new / (1.0 - ADAM_B2 ** ADAM_STEP)
    g_out[...] = g
    d_out[...] = -ADAM_LR * (m_hat / (jnp.sqrt(v_hat) + ADAM_EPS) + ADAM_WD * w)
    m_out[...] = m_new
    v_out[...] = v_new


_SMALL_SLOTS = {"pre_norm_mix": (0, 0, D_MODEL), "post_norm_mix": (1, 0, D_MODEL), "pre_norm_mlp": (2, 0, D_MODEL),
                "post_norm_mlp": (3, 0, D_MODEL), "q_a_norm": (4, 0, Q_LORA), "kv_a_norm": (4, Q_LORA, KV_LORA),
                "sinks": (4, Q_LORA + KV_LORA, N_HEADS_A)}
_LOSS_ROW = 5


def _adamw_small(red, w, m, v):
    names = tuple(_SMALL_SLOTS)
    n = len(names)

    def body(*refs):
        red_ref, ws, ms, vs, outs = refs[0], refs[1:1 + n], refs[1 + n:1 + 2 * n], refs[1 + 2 * n:1 + 3 * n], refs[1 + 3 * n:]
        for k, name in enumerate(names):
            row, lane, width = _SMALL_SLOTS[name]
            g = red_ref[row:row + 1, lane:lane + width]
            _adamw_store(ws[k][...], g, ms[k][...], vs[k][...], outs[4 * k:4 * k + 4])

    vmem = pl.BlockSpec(memory_space=pltpu.VMEM)
    res = pl.pallas_call(
        body, name="adamw_small", in_specs=[vmem] * (1 + 3 * n), out_specs=[vmem] * (4 * n),
        out_shape=[jax.ShapeDtypeStruct(w[name].shape, F32) for name in names for _ in range(4)],
    )(red, *[w[k] for k in names], *[m[k] for k in names], *[v[k] for k in names])
    return {name: res[4 * k:4 * k + 4] for k, name in enumerate(names)}


_ADAMW_RIDERS = ("w_up", "w_down", "w_out")


def _dw_in_adamw(dproj, h, g_parts, w, m, v):
    t, cols = dproj.shape
    tm, tk = cols // 2, min(1024, t)
    rows_out = N_CHIPS * SHARD_SHAPES["w_in"][1]
    nk = t // tk
    steps = 2 * nk
    names = _ADAMW_RIDERS
    n = len(names)

    def body(a_ref, b_ref, *rest):
        g1s, g2s, ws, ms, vs = (rest[n * j:n * (j + 1)] for j in range(5))
        o_ref, outs = rest[5 * n], rest[5 * n + 1:]

        @pl.when(pl.program_id(2) == 0)
        def _():
            o_ref[...] = jnp.zeros(o_ref.shape, F32)

        o_ref[...] += _dot_tn(a_ref[...], b_ref[...])
        for j in range(n):
            _adamw_store(ws[j][...], g1s[j][...] + g2s[j][...], ms[j][...], vs[j][...], outs[4 * j:4 * j + 4])

    def rider_spec(name, packed):
        br = SHARD_SHAPES[name][0] // steps
        first = _row_offset(GROUP_B, name) // br if packed else 0
        return pl.BlockSpec((br, D_MODEL), lambda i, j, k: (first + i * nk + k, 0))

    g_specs = [rider_spec(name, True) for name in names]
    own_specs = [rider_spec(name, False) for name in names]
    res = pl.pallas_call(
        body, name="dw_in", grid=(2, 1, nk),
        in_specs=[pl.BlockSpec((tk, tm), lambda i, j, k: (k, i)), pl.BlockSpec((tk, D_MODEL), lambda i, j, k: (k, 0))]
        + g_specs * 2 + own_specs * 3,
        out_specs=[pl.BlockSpec((tm, D_MODEL), lambda i, j, k: (i, 0))] + [s for s in own_specs for _ in range(4)],
        out_shape=[jax.ShapeDtypeStruct((rows_out, D_MODEL), F32)]
        + [jax.ShapeDtypeStruct(SHARD_SHAPES[name], F32) for name in names for _ in range(4)],
        compiler_params=_params(("arbitrary", "arbitrary", "arbitrary")),
    )(dproj, h, *[g_parts[0]] * n, *[g_parts[1]] * n, *[w[k] for k in names], *[m[k] for k in names],
      *[v[k] for k in names])
    return res[0], {name: res[1 + 4 * j:5 + 4 * j] for j, name in enumerate(names)}


def _adamw(w, g_parts, m, v, name, block, g_row_off=0):
    r, c = w.shape
    br, bc = block
    ng = len(g_parts)

    def body(*refs):
        w_ref, g_refs, m_ref, v_ref = refs[0], refs[1:1 + ng], refs[1 + ng], refs[2 + ng]
        g = g_refs[0][...]
        for gr in g_refs[1:]:
            g = g + gr[...]
        _adamw_store(w_ref[...], g, m_ref[...], v_ref[...], refs[3 + ng:])

    assert g_row_off % br == 0 and r % br == 0 and c % bc == 0
    blk = pl.BlockSpec(block, lambda i, j: (i, j))
    g_blk = pl.BlockSpec(block, lambda i, j: (i + g_row_off // br, j))
    return pl.pallas_call(
        body, name=name, grid=(r // br, c // bc),
        in_specs=[blk] + [g_blk] * ng + [blk, blk], out_specs=[blk] * 4,
        out_shape=[jax.ShapeDtypeStruct((r, c), F32)] * 4,
        compiler_params=_params(("parallel", "parallel")),
    )(w, *g_parts, m, v)


_HBM = pl.BlockSpec(memory_space=pltpu.HBM)


def _other_chips(x, y):
    return ((1 - x, y), (x, 1 - y), (1 - x, 1 - y))


def _gather_copies(src, out, send_sems, recv_sems, local_sem):
    x, y, c = lax.axis_index("x"), lax.axis_index("y"), lax.axis_index("c")
    me = 2 * x + y
    local = pltpu.make_async_copy(src, out.at[me], local_sem)

    def copies(arriving):
        return [pltpu.make_async_remote_copy(src_ref=src, dst_ref=out.at[2 * px + py if arriving else me],
                                             send_sem=send_sems.at[j], recv_sem=recv_sems.at[j], device_id=(px, py, c),
                                             device_id_type=MESH)
                for j, (px, py) in enumerate(_other_chips(x, y))]

    return local, copies


def _gather_start(src, out, send_sems, recv_sems, local_sem):
    local, copies = _gather_copies(src, out, send_sems, recv_sems, local_sem)
    local.start()
    for cp in copies(False):
        cp.start()


def _gather_wait(src, out, send_sems, recv_sems, local_sem):
    local, copies = _gather_copies(src, out, send_sems, recv_sems, local_sem)
    for cp in copies(True):
        cp.wait_recv()
    for cp in copies(False):
        cp.wait_send()
    local.wait()


def _scatter_copies(src, land, send_sems, recv_sems):
    x, y, c = lax.axis_index("x"), lax.axis_index("y"), lax.axis_index("c")
    return [pltpu.make_async_remote_copy(src_ref=src.at[2 * px + py], dst_ref=land.at[j], send_sem=send_sems.at[j],
                                         recv_sem=recv_sems.at[j], device_id=(px, py, c), device_id_type=MESH)
            for j, (px, py) in enumerate(_other_chips(x, y))]


def _scatter_start(src, land, send_sems, recv_sems):
    for cp in _scatter_copies(src, land, send_sems, recv_sems):
        cp.start()


def _scatter_wait(src, land, send_sems, recv_sems):
    copies = _scatter_copies(src, land, send_sems, recv_sems)
    for cp in copies:
        cp.wait_recv()
    for cp in copies:
        cp.wait_send()


def _all_gather_chips(packed):
    r = packed.shape[0]
    half = r // 2

    def body(src, out, ici_send, ici_recv, d2d_send, d2d_recv, local_sem):
        x, y, c = lax.axis_index("x"), lax.axis_index("y"), lax.axis_index("c")
        me = 2 * x + y
        mine = pl.ds(pl.multiple_of(c * half, 16), half)
        theirs = pl.ds(pl.multiple_of((1 - c) * half, 16), half)
        chips = _other_chips(x, y)
        local = pltpu.make_async_copy(src, out.at[me], local_sem)
        local.start()
        sends = [pltpu.make_async_remote_copy(src_ref=src.at[mine], dst_ref=out.at[me, mine], send_sem=ici_send.at[j],
                                              recv_sem=ici_recv.at[j], device_id=(px, py, c), device_id_type=MESH)
                 for j, (px, py) in enumerate(chips)]
        for cp in sends:
            cp.start()
        passed = []
        for j, (px, py) in enumerate(chips):
            block = 2 * px + py
            pltpu.make_async_remote_copy(src_ref=src.at[mine], dst_ref=out.at[block, mine], send_sem=ici_send.at[j],
                                         recv_sem=ici_recv.at[j], device_id=(px, py, c), device_id_type=MESH).wait_recv()
            cp = pltpu.make_async_remote_copy(src_ref=out.at[block, mine], dst_ref=out.at[block, mine],
                                              send_sem=d2d_send.at[j], recv_sem=d2d_recv.at[j],
                                              device_id=(x, y, 1 - c), device_id_type=MESH)
            cp.start()
            passed.append(cp)
        for j, (px, py) in enumerate(chips):
            block = 2 * px + py
            pltpu.make_async_remote_copy(src_ref=out.at[block, theirs], dst_ref=out.at[block, theirs],
                                         send_sem=d2d_send.at[j], recv_sem=d2d_recv.at[j],
                                         device_id=(x, y, 1 - c), device_id_type=MESH).wait_recv()
        for cp in sends + passed:
            cp.wait_send()
        local.wait()

    sems = pltpu.SemaphoreType.DMA((3,))
    return pl.pallas_call(
        body, name="ag_weights", in_specs=[_HBM], out_specs=_HBM,
        out_shape=jax.ShapeDtypeStruct((N_CHIPS,) + packed.shape, packed.dtype),
        scratch_shapes=[sems, sems, sems, sems, pltpu.SemaphoreType.DMA(())],
    )(packed)


def _sum4(gp, land, chip, name):
    _, r, w = gp.shape
    tr = 256 if r % 256 == 0 else 128

    def body(chip_ref, o_ref, l_ref, s_ref):
        s_ref[...] = ((o_ref[0] + l_ref[0].astype(F32)) + l_ref[1].astype(F32)) + l_ref[2].astype(F32)

    return pl.pallas_call(
        body, name=name,
        grid_spec=pltpu.PrefetchScalarGridSpec(
            num_scalar_prefetch=1, grid=(r // tr,),
            in_specs=[pl.BlockSpec((1, tr, w), lambda i, chip_ref: (chip_ref[0], i, 0)),
                      pl.BlockSpec((3, tr, w), lambda i, chip_ref: (0, i, 0))],
            out_specs=pl.BlockSpec((tr, w), lambda i, chip_ref: (i, 0))),
        out_shape=jax.ShapeDtypeStruct((r, w), F32),
        compiler_params=_params(("parallel",)),
    )(chip, gp, land)


def _sibling_copy(src, got, send_sem, recv_sem):
    x, y, c = lax.axis_index("x"), lax.axis_index("y"), lax.axis_index("c")
    return pltpu.make_async_remote_copy(src_ref=src, dst_ref=got, send_sem=send_sem, recv_sem=recv_sem,
                                        device_id=(x, y, 1 - c), device_id_type=MESH)


def _swap_sibling(s, name):
    def body(src, got, send_sem, recv_sem):
        cp = _sibling_copy(src, got, send_sem, recv_sem)
        cp.start()
        cp.wait_recv()
        cp.wait_send()

    return pl.pallas_call(
        body, name=name, in_specs=[_HBM], out_specs=_HBM,
        out_shape=jax.ShapeDtypeStruct(s.shape, s.dtype),
        scratch_shapes=[pltpu.SemaphoreType.DMA(()), pltpu.SemaphoreType.DMA(())],
    )(s)


def _all_reduce_small(dsmall, loss):
    n_dev = 8
    names = tuple(_SMALL_SLOTS)
    shape = (8, D_MODEL)

    def body(*refs):
        parts, loss_ref = refs[:len(names)], refs[len(names)]
        out, src, gath, send_sems, recv_sems = refs[len(names) + 1:]
        x, y, c = lax.axis_index("x"), lax.axis_index("y"), lax.axis_index("c")
        me = 4 * x + 2 * y + c
        src[...] = jnp.zeros(shape, F32)
        for name, part in zip(names, parts):
            row, lane, _ = _SMALL_SLOTS[name]
            src[row:row + 1, lane:lane + part.shape[1]] = part[...]
        src[_LOSS_ROW:_LOSS_ROW + 1, 0:LANES] = loss_ref[...]
        gath[me] = src[...]
        peers = []
        for k in range(1, n_dev):
            px = 1 - x if (k >> 2) & 1 else x
            py = 1 - y if (k >> 1) & 1 else y
            pc = 1 - c if k & 1 else c
            peers.append((px, py, pc))
        sends = []
        for j, peer in enumerate(peers):
            cp = pltpu.make_async_remote_copy(src_ref=src, dst_ref=gath.at[me], send_sem=send_sems.at[j],
                                              recv_sem=recv_sems.at[j], device_id=peer, device_id_type=MESH)
            cp.start()
            sends.append(cp)
        for j, (px, py, pc) in enumerate(peers):
            pltpu.make_async_remote_copy(src_ref=src, dst_ref=gath.at[4 * px + 2 * py + pc], send_sem=send_sems.at[j],
                                         recv_sem=recv_sems.at[j], device_id=(px, py, pc), device_id_type=MESH).wait_recv()
        for cp in sends:
            cp.wait_send()
        acc = gath[0]
        for d in range(1, n_dev):
            acc = acc + gath[d]
        out[...] = acc

    vmem = pl.BlockSpec(memory_space=pltpu.VMEM)
    return pl.pallas_call(
        body, name="ar_small", in_specs=[vmem] * (len(names) + 1), out_specs=vmem,
        out_shape=jax.ShapeDtypeStruct(shape, F32),
        scratch_shapes=[pltpu.VMEM(shape, F32), pltpu.VMEM((n_dev,) + shape, F32),
                        pltpu.SemaphoreType.DMA((n_dev - 1,)), pltpu.SemaphoreType.DMA((n_dev - 1,))],
    )(*[dsmall[k] for k in names], loss)


_W_IN_ROWS = SHARD_SHAPES["w_in"][1]


def _shard_rows(name, a):
    return jnp.transpose(a) if name == "w_in" else a.reshape(PACK_ROWS[name], D_MODEL)


def _pack(group, shards, dtype):
    parts = [_shard_rows(n, shards[n]).astype(dtype) for n in group]
    pad = -sum(PACK_ROWS[n] for n in group) % LANES
    if pad:
        parts.append(jnp.zeros((pad, D_MODEL), dtype))
    return jnp.concatenate(parts, axis=0)


def _col_sharded_full(g, name, group):
    r, c = SHARD_SHAPES[name]
    off = _row_offset(group, name)
    blocks = g[:, off:off + PACK_ROWS[name]].reshape(N_CHIPS, r, c)
    return jnp.transpose(blocks, (1, 0, 2)).reshape(r, N_CHIPS * c)


def _col_sharded_blocks(d, name):
    r, c = SHARD_SHAPES[name]
    return jnp.transpose(d.reshape(r, N_CHIPS, c), (1, 0, 2)).reshape(N_CHIPS, PACK_ROWS[name], D_MODEL)


def _weights_a(g):
    dt = g.dtype
    w_in_t = g[:, :_W_IN_ROWS].reshape(N_CHIPS * _W_IN_ROWS, D_MODEL)
    w_in_t = jnp.concatenate([w_in_t, jnp.zeros((D_IN_PAD - N_CHIPS * _W_IN_ROWS, D_MODEL), dt)], axis=0)
    wq = _col_sharded_full(g, "w_q_b", GROUP_A).reshape(Q_LORA, N_HEADS_B, Q_HEAD_B)
    wq_p = jnp.concatenate([wq, jnp.zeros((Q_LORA, N_HEADS_B, HEAD_PAD - Q_HEAD_B), dt)], axis=2).reshape(Q_LORA, MLA_W)
    wkv = _col_sharded_full(g, "w_kv_b", GROUP_A).reshape(KV_LORA, N_HEADS_B, QK_NOPE + V_DIM_B)
    zk = jnp.zeros((KV_LORA, N_HEADS_B, HEAD_PAD - QK_NOPE), dt)
    wk_p = jnp.concatenate([wkv[:, :, :QK_NOPE], zk], axis=2).reshape(KV_LORA, MLA_W)
    wv = wkv[:, :, QK_NOPE:].reshape(KV_LORA, N_HEADS_B * V_DIM_B)
    return dict(w_in=w_in_t, wq=wq_p, wk=wk_p, wv=wv, wv_t=jnp.transpose(wv))


def _grad_blocks_a(dw_in_t, dwq_p, dwk_p, dwv):
    dwq = dwq_p.reshape(Q_LORA, N_HEADS_B, HEAD_PAD)[:, :, :Q_HEAD_B].reshape(Q_LORA, N_HEADS_B * Q_HEAD_B)
    dwk = dwk_p.reshape(KV_LORA, N_HEADS_B, HEAD_PAD)[:, :, :QK_NOPE]
    dwkv = jnp.concatenate([dwk, dwv.reshape(KV_LORA, N_HEADS_B, V_DIM_B)], axis=2)
    dwkv = dwkv.reshape(KV_LORA, N_HEADS_B * (QK_NOPE + V_DIM_B))
    pad = -sum(PACK_ROWS[n] for n in GROUP_A) % LANES
    return jnp.concatenate([dw_in_t.reshape(N_CHIPS, _W_IN_ROWS, D_MODEL), _col_sharded_blocks(dwq, "w_q_b"),
                            _col_sharded_blocks(dwkv, "w_kv_b"), jnp.zeros((N_CHIPS, pad, D_MODEL), F32)], axis=1)


def _rope_freq_lanes():
    freqs = ROPE_THETA ** (-jnp.arange(0, QK_ROPE, 2, dtype=F32) / QK_ROPE)
    return jnp.concatenate([jnp.zeros((QK_NOPE,), F32), freqs, freqs,
                            jnp.zeros((HEAD_PAD - Q_HEAD_B,), F32)]).reshape(1, LANES)


def _fwd_bwd(x, positions, target, w, m, v):
    t = x.shape[0]
    wa = _weights_a(_all_gather_chips(_pack(GROUP_A, w, BF16)))
    posr = positions.astype(F32).reshape(1, t)
    posc = posr.reshape(t, 1)
    freq = _rope_freq_lanes()
    g1, g2, g3, g4 = w["pre_norm_mix"], w["post_norm_mix"], w["pre_norm_mlp"], w["post_norm_mlp"]
    qan, kvan, sinks = w["q_a_norm"], w["kv_a_norm"], w["sinks"]

    h, proj = _proj_fwd(x, g1, wa["w_in"])
    out_a, lse_a = _swa_fwd(proj, posc, posr, sinks)
    qm, km, qt, kt, vt = _mla_prep_fwd(proj, posc, freq, qan, kvan, wa["wq"], wa["wk"], wa["wv_t"])
    out_bt, lse_b, wb = _mla_fwd(km, qt, vt, _pack(GROUP_B, w, BF16))
    w_oa, w_ob = _col_sharded_full(wb, "w_o_a", GROUP_B), _col_sharded_full(wb, "w_o_b", GROUP_B)
    merged, y, x1, h2 = _mix_out_fwd(out_a, out_bt, proj, x, w_oa, w_ob, wb, g2, g3)
    a = _up_fwd(h2, wb)
    dx2, dyd, dg4, loss = _down_fwd_loss(a, wb, x1, target, g4)

    gp_b = _dw_into_blocks(a, dyd, "w_down", 1024, _TK_DW)
    du = _down_bwd(dyd, wb, a)
    gp_b = _dw_into_blocks(h2, du, "w_up", 1024, _TK_DW, gp_b)
    dx1, dy, dg3, dg2 = _up_bwd(du, wb, x1, dx2, y, g3, g2)
    gp_b = _dw_into_blocks(merged, dy, "w_out", 1024, _TK_DW, gp_b)
    doa, dob, dproj, d_out_a, d_out_b, d_out_bt, del_a, del_b = _mix_out_bwd(dy, out_a, out_bt, proj, w_oa, w_ob, wb)
    dw_oa = _matmul_tn(out_a, doa, "dw_o_a", 512, 1024)
    dw_ob = _dw_ob(out_bt, dob)
    small_b = jnp.concatenate([_col_sharded_blocks(dw_oa, "w_o_a"), _col_sharded_blocks(dw_ob, "w_o_b")], axis=1)
    gp_b = lax.dynamic_update_slice(gp_b, small_b, (0, _row_offset(GROUP_B, "w_o_a"), 0))
    dqm, dkm, dvm, land_b = _mla_bwd(qm, km, qt, kt, vt, d_out_b, d_out_bt, lse_b, del_b, gp_b)
    chip = (2 * lax.axis_index("x") + lax.axis_index("y")).astype(jnp.int32).reshape(1)
    part_b = _sum4(gp_b, land_b, chip, "rs_sum_b")
    dcq, dckv, dkr, dwq, dwk, dwv, dqan, dkvan, sib_b = _mla_prep_bwd(
        dqm, dkm, dvm, proj, posc, freq, qan, kvan, wa["wq"], wa["wk"], wa["wv"], part_b)
    dqa, dka, dva, dsinks = _swa_bwd(proj, d_out_a, lse_a, del_a, posc, posr, sinks)
    col = 2 * D_MODEL
    for piece in (dqa, dka, dva, dcq, dckv, dkr):
        dproj = lax.dynamic_update_slice(dproj, piece.astype(BF16), (0, col))
        col += piece.shape[1]
    dw_in_t, updated = _dw_in_adamw(dproj, h, [part_b, sib_b], w, m, v)
    gp_a = _grad_blocks_a(dw_in_t, dwq, dwk, dwv)
    grad_x, dg1, land_a = _in_bwd(dproj, wa["w_in"], x, dx1, g1, gp_a.astype(BF16))

    own_a = lax.dynamic_slice_in_dim(gp_a, chip[0], 1, axis=0)
    part_a = _sum4(own_a, land_a, jnp.zeros((1,), jnp.int32), "rs_sum_a")
    reduced = {GROUP_A: [part_a, _swap_sibling(part_a, "rs_swap_a")], GROUP_B: [part_b, sib_b]}
    dsmall = dict(pre_norm_mix=dg1, post_norm_mix=dg2, pre_norm_mlp=dg3, post_norm_mlp=dg4,
                  q_a_norm=dqan, kv_a_norm=dkvan, sinks=dsinks)
    return loss, grad_x, reduced, dsmall, updated


def kernel(x, positions, pre_norm_mix, w_in, q_a_norm, w_q_b, kv_a_norm, w_kv_b, sinks, w_o_a, w_o_b, w_out, post_norm_mix, pre_norm_mlp, w_up, w_down, post_norm_mlp, loss_target, m_pre_norm_mix, m_w_in, m_q_a_norm, m_w_q_b, m_kv_a_norm, m_w_kv_b, m_sinks, m_w_o_a, m_w_o_b, m_w_out, m_post_norm_mix, m_pre_norm_mlp, m_w_up, m_w_down, m_post_norm_mlp, v_pre_norm_mix, v_w_in, v_q_a_norm, v_w_q_b, v_kv_a_norm, v_w_kv_b, v_sinks, v_w_o_a, v_w_o_b, v_w_out, v_post_norm_mix, v_pre_norm_mlp, v_w_up, v_w_down, v_post_norm_mlp):
    w = dict(pre_norm_mix=pre_norm_mix, w_in=w_in[0], q_a_norm=q_a_norm, w_q_b=w_q_b[0], kv_a_norm=kv_a_norm,
             w_kv_b=w_kv_b[0], sinks=sinks, w_o_a=w_o_a[0], w_o_b=w_o_b[0], w_out=w_out[0],
             post_norm_mix=post_norm_mix, pre_norm_mlp=pre_norm_mlp, w_up=w_up[0], w_down=w_down[0],
             post_norm_mlp=post_norm_mlp)
    m = dict(pre_norm_mix=m_pre_norm_mix, w_in=m_w_in[0], q_a_norm=m_q_a_norm, w_q_b=m_w_q_b[0],
             kv_a_norm=m_kv_a_norm, w_kv_b=m_w_kv_b[0], sinks=m_sinks, w_o_a=m_w_o_a[0], w_o_b=m_w_o_b[0],
             w_out=m_w_out[0], post_norm_mix=m_post_norm_mix, pre_norm_mlp=m_pre_norm_mlp, w_up=m_w_up[0],
             w_down=m_w_down[0], post_norm_mlp=m_post_norm_mlp)
    v = dict(pre_norm_mix=v_pre_norm_mix, w_in=v_w_in[0], q_a_norm=v_q_a_norm, w_q_b=v_w_q_b[0],
             kv_a_norm=v_kv_a_norm, w_kv_b=v_w_kv_b[0], sinks=v_sinks, w_o_a=v_w_o_a[0], w_o_b=v_w_o_b[0],
             w_out=v_w_out[0], post_norm_mix=v_post_norm_mix, pre_norm_mlp=v_pre_norm_mlp, w_up=v_w_up[0],
             w_down=v_w_down[0], post_norm_mlp=v_post_norm_mlp)

    loss, grad_x, reduced, dsmall, updated = _fwd_bwd(x[0], positions, loss_target[0], w, m, v)

    red = _all_reduce_small(dsmall, loss)
    small = _adamw_small(red, w, m, v)

    big = {}
    tr = jnp.transpose
    big["w_in"] = [tr(o)[None] for o in _adamw(tr(w["w_in"]), reduced[GROUP_A], tr(m["w_in"]), tr(v["w_in"]),
                                               "adamw_w_in", (_W_IN_ROWS, 256))]
    for n in _ADAMW_RIDERS:
        big[n] = [o[None] for o in updated[n]]
    for group, names in ((GROUP_A, ("w_q_b", "w_kv_b")), (GROUP_B, ("w_o_a", "w_o_b"))):
        for n in names:
            off = _row_offset(group, n)
            g_parts = [p[off:off + PACK_ROWS[n]].reshape(SHARD_SHAPES[n]) for p in reduced[group]]
            big[n] = [o[None] for o in _adamw(w[n], g_parts, m[n], v[n], "adamw_" + n, SHARD_SHAPES[n])]

    outs = [big[n][k] if n in big else small[n][k] for k in range(4) for n in WEIGHTS]
    return (red[_LOSS_ROW, 0], grad_x[None], *outs)
```

```python
import jax
import jax.numpy as jnp
from jax import lax
from jax.experimental import pallas as pl
from jax.experimental.pallas import tpu as pltpu

F32 = jnp.float32
BF16 = jnp.bfloat16
MESH = pl.DeviceIdType.MESH

D_MODEL = 1024
N_HEADS_A = 8
N_KV_A = 2
HEAD_DIM_A = 64
WINDOW = 128
BLOCK = 128
N_HEADS_B = 8
QK_NOPE = 64
QK_ROPE = 32
V_DIM_B = 64
Q_LORA = 256
KV_LORA = 128
ROPE_THETA = 10000.0
D_FF = 4 * D_MODEL
EPS = 1e-6
WIDTH_A = N_HEADS_A * HEAD_DIM_A
Q_HEAD_B = QK_NOPE + QK_ROPE
D_IN_PAD = 3328
HEAD_PAD = 128
MLA_W = N_HEADS_B * HEAD_PAD

ADAM_LR = 0.001
ADAM_B1 = 0.9
ADAM_B2 = 0.999
ADAM_EPS = 1e-08
ADAM_WD = 0.01
ADAM_STEP = 10

NEG = -1e30
N_CHIPS = 4
LANES = 128
VMEM_LIMIT = 56 * 1024 * 1024

SHARD_SHAPES = {"w_in": (1024, 808), "w_q_b": (256, 192), "w_kv_b": (128, 256), "w_o_a": (512, 256),
                "w_o_b": (512, 256), "w_out": (256, 1024), "w_up": (1024, 1024), "w_down": (1024, 1024)}
PACK_ROWS = {n: (s[0] * s[1]) // D_MODEL for n, s in SHARD_SHAPES.items()}
GROUP_A = ("w_in", "w_q_b", "w_kv_b")
GROUP_B = ("w_up", "w_down", "w_out", "w_o_a", "w_o_b")
WEIGHTS = ("pre_norm_mix", "w_in", "q_a_norm", "w_q_b", "kv_a_norm", "w_kv_b", "sinks", "w_o_a", "w_o_b", "w_out",
           "post_norm_mix", "pre_norm_mlp", "w_up", "w_down", "post_norm_mlp")


def _params(sem=None):
    return pltpu.CompilerParams(dimension_semantics=sem, vmem_limit_bytes=VMEM_LIMIT)


def _dot(a, b):
    return jnp.dot(a, b, preferred_element_type=F32)


def _dot_nt(a, b):
    return lax.dot_general(a, b, (((1,), (1,)), ((), ())), preferred_element_type=F32)


def _dot_tn(a, b):
    return lax.dot_general(a, b, (((0,), (0,)), ((), ())), preferred_element_type=F32)


def _rms(v):
    return lax.rsqrt(jnp.mean(v * v, axis=-1, keepdims=True) + EPS)


def _norm_bwd(dout, n, r, g):
    dn = dout * g
    dx = r * (dn - n * jnp.mean(dn * n, axis=-1, keepdims=True))
    return dx, jnp.sum(dout * n, axis=0, keepdims=True)


def _full(shape):
    return pl.BlockSpec(shape, lambda *_: (0,) * len(shape))


def _row_offset(group, name):
    return sum(PACK_ROWS[n] for n in group[:group.index(name)])


def _wb_spec(name):
    rows = PACK_ROWS[name]
    return pl.BlockSpec((N_CHIPS, rows, D_MODEL), lambda *_: (0, _row_offset(GROUP_B, name) // rows, 0))


def _proj_fwd(x, g1, w_in_t):
    t = x.shape[0]
    tm = 512

    def body(x_ref, g_ref, w_ref, h_ref, p_ref):
        for rows in _row_halves(tm):
            xv = x_ref[rows, :]
            h = ((xv * _rms(xv)) * g_ref[...]).astype(BF16)
            h_ref[rows, :] = h
            p_ref[rows, :] = _dot_nt(h, w_ref[...])

    return pl.pallas_call(
        body, name="proj_fwd", grid=(t // tm,),
        in_specs=[pl.BlockSpec((tm, D_MODEL), lambda i: (i, 0)), _full((1, D_MODEL)), _full((D_IN_PAD, D_MODEL))],
        out_specs=[pl.BlockSpec((tm, D_MODEL), lambda i: (i, 0)), pl.BlockSpec((tm, D_IN_PAD), lambda i: (i, 0))],
        out_shape=[jax.ShapeDtypeStruct((t, D_MODEL), BF16), jax.ShapeDtypeStruct((t, D_IN_PAD), F32)],
        compiler_params=_params(("parallel",)),
    )(x, g1, w_in_t)


_QA_BLK = 2048 // WIDTH_A
_KA_BLK = 2560 // LANES
_VA_BLK = 2688 // LANES
_CQ_BLK = 2816 // Q_LORA
_CKV_BLK = 3072 // LANES
_KR_BLK = 3200 // LANES


_GROUP_A = N_HEADS_A // N_KV_A
_SWA_SCALE = HEAD_DIM_A ** -0.5
_LOG2E = 1.4426950408889634


def _head_cols(v, h):
    return v[:, HEAD_DIM_A * h:HEAD_DIM_A * (h + 1)]


def _head_rows(v, h):
    return v[HEAD_DIM_A * h:HEAD_DIM_A * (h + 1), :]


def _swa_scores_t(st_g, j, h, dist, valid):
    slope = 2.0 ** (-8.0 * (h + 1) / N_HEADS_A)
    st = st_g[:, BLOCK * j:BLOCK * (j + 1)] * (_SWA_SCALE * _LOG2E) - (slope * _LOG2E) * dist
    return jnp.where(valid, st, NEG)


def _group_t(xt, kh):
    return jnp.concatenate([_head_rows(xt, _GROUP_A * kh + j) for j in range(_GROUP_A)], axis=1).astype(BF16)


_SWA_PER_STEP = 4


def _swa_fwd(proj, posc, posr, sinks):
    t = proj.shape[0]
    span = _SWA_PER_STEP * BLOCK

    def body(q_ref, kc_ref, kp_ref, vc_ref, vp_ref, pq_ref, pc_ref, pp_ref, sink_ref, o_ref, l_ref):
        n = pl.program_id(0)
        k_all = jnp.concatenate([kp_ref[...], kc_ref[...]], axis=0)
        v_all = jnp.concatenate([vp_ref[...], vc_ref[...]], axis=0)
        pos_all = jnp.concatenate([pp_ref[...], pc_ref[...]], axis=0)
        ki = lax.broadcasted_iota(jnp.int32, (2 * BLOCK, BLOCK), 0)
        qi = lax.broadcasted_iota(jnp.int32, (2 * BLOCK, BLOCK), 1)
        window = (ki > qi) & (ki <= qi + WINDOW)
        for sub in range(_SWA_PER_STEP):
            band = slice(BLOCK * sub, BLOCK * (sub + 2))
            own = slice(BLOCK * sub, BLOCK * (sub + 1))
            kb, vb = k_all[band], v_all[band]
            dist = jnp.abs(pos_all[band] - pq_ref[:, own])
            valid = window & ((n > 0) | (ki >= BLOCK)) if sub == 0 else window
            q_t, vb_t = q_ref[own, :].T, vb.T
            out_t, lse = [], []
            for kh in range(N_KV_A):
                st_g = _dot(_head_cols(kb, kh).astype(BF16), _group_t(q_t, kh))
                ps = []
                for j in range(_GROUP_A):
                    h = _GROUP_A * kh + j
                    st = _swa_scores_t(st_g, j, h, dist, valid)
                    sink = sink_ref[0:1, h:h + 1] * _LOG2E
                    m = jnp.maximum(jnp.max(st, axis=0, keepdims=True), sink)
                    e = jnp.exp2(st - m)
                    den = jnp.sum(e, axis=0, keepdims=True) + jnp.exp2(sink - m)
                    ps.append((e * (1.0 / den)).astype(BF16))
                    lse.append(m + jnp.log(den) * _LOG2E)
                o_g = _dot(_head_rows(vb_t, kh).astype(BF16), jnp.concatenate(ps, axis=1))
                out_t.extend(o_g[:, BLOCK * j:BLOCK * (j + 1)] for j in range(_GROUP_A))
            o_ref[own, :] = jnp.concatenate(out_t, axis=0).T
            l_ref[:, own] = jnp.concatenate(lse, axis=0)

    cur = lambda n: (n, 0)
    prev = lambda n: jnp.maximum(_SWA_PER_STEP * n - 1, 0)
    return pl.pallas_call(
        body, name="swa_fwd", grid=(t // span,),
        in_specs=[pl.BlockSpec((span, WIDTH_A), lambda n: (n, _QA_BLK)),
                  pl.BlockSpec((span, LANES), lambda n: (n, _KA_BLK)),
                  pl.BlockSpec((BLOCK, LANES), lambda n: (prev(n), _KA_BLK)),
                  pl.BlockSpec((span, LANES), lambda n: (n, _VA_BLK)),
                  pl.BlockSpec((BLOCK, LANES), lambda n: (prev(n), _VA_BLK)),
                  pl.BlockSpec((1, span), lambda n: (0, n)),
                  pl.BlockSpec((span, 1), cur),
                  pl.BlockSpec((BLOCK, 1), lambda n: (prev(n), 0)),
                  _full((1, N_HEADS_A))],
        out_specs=[pl.BlockSpec((span, WIDTH_A), cur), pl.BlockSpec((N_HEADS_A, span), lambda n: (0, n))],
        out_shape=[jax.ShapeDtypeStruct((t, WIDTH_A), F32), jax.ShapeDtypeStruct((N_HEADS_A, t), F32)],
        compiler_params=_params(("parallel",)),
    )(proj, proj, proj, proj, proj, posr, posc, posc, sinks)


def _rope_coeffs(pos, freq):
    ang = pos * freq
    cosv, sinv = jnp.cos(ang), jnp.sin(ang)
    lane = lax.broadcasted_iota(jnp.int32, ang.shape, 1)
    lo = (lane >= QK_NOPE) & (lane < QK_NOPE + QK_ROPE // 2)
    hi = (lane >= QK_NOPE + QK_ROPE // 2) & (lane < QK_NOPE + QK_ROPE)
    c = jnp.where(lane < QK_NOPE, 1.0, jnp.where(lo | hi, cosv, 0.0))
    s = jnp.where(lo, -sinv, jnp.where(hi, sinv, 0.0))
    return c, s, lo, hi


def _rope(xh, c, s, lo):
    up = pltpu.roll(xh, LANES - QK_ROPE // 2, axis=1)
    dn = pltpu.roll(xh, QK_ROPE // 2, axis=1)
    return xh * c + jnp.where(lo, up, dn) * s


def _unrope(dh, c, s, lo, hi):
    g = dh * s
    up = pltpu.roll(g, LANES - QK_ROPE // 2, axis=1)
    dn = pltpu.roll(g, QK_ROPE // 2, axis=1)
    return dh * c + jnp.where(hi, dn, jnp.where(lo, up, 0.0))


_TQ = 512
_MLA_SCALE = Q_HEAD_B ** -0.5


def _mla_prep_fwd(proj, posc, freq, qan, kvan, wq, wk, wv):
    t = proj.shape[0]
    tm = _TQ
    nb = t // tm

    def body(cq_ref, ckv_ref, kr_ref, pos_ref, f_ref, qan_ref, kvan_ref, wq_ref, wk_ref, wv_ref,
             q_ref, k_ref, qt_ref, kt_ref, vt_ref):
        cq = cq_ref[...]
        cqn = ((cq * _rms(cq)) * qan_ref[...]).astype(BF16)
        ckv = ckv_ref[...]
        ckvn = ((ckv * _rms(ckv)) * kvan_ref[...]).astype(BF16)
        qb = _dot(cqn, wq_ref[...])
        kb = _dot(ckvn, wk_ref[...])
        vbt = _dot_nt(wv_ref[...], ckvn)
        c, s, lo, _ = _rope_coeffs(pos_ref[...], f_ref[...])
        kr = _rope(pltpu.roll(kr_ref[...], QK_NOPE, axis=1), c, s, lo)
        for h in range(N_HEADS_B):
            sl = slice(HEAD_PAD * h, HEAD_PAD * (h + 1))
            q_h = _rope(qb[:, sl], c, s, lo)
            k_h = kb[:, sl] + kr
            q_ref[:, sl] = q_h.astype(BF16)
            k_ref[:, sl] = k_h.astype(BF16)
            qt_ref[h, 0] = q_h.T.astype(BF16)
            kt_ref[h, 0] = k_h.T.astype(BF16)
            vt_ref[h, 0] = vbt[V_DIM_B * h:V_DIM_B * (h + 1), :].astype(BF16)

    row = lambda i: (i, 0)
    blk4 = lambda d: pl.BlockSpec((N_HEADS_B, 1, d, tm), lambda i: (0, i, 0, 0))
    return pl.pallas_call(
        body, name="mla_prep_fwd", grid=(nb,),
        in_specs=[pl.BlockSpec((tm, Q_LORA), lambda i: (i, _CQ_BLK)),
                  pl.BlockSpec((tm, LANES), lambda i: (i, _CKV_BLK)),
                  pl.BlockSpec((tm, LANES), lambda i: (i, _KR_BLK)),
                  pl.BlockSpec((tm, 1), row), _full((1, LANES)), _full((1, Q_LORA)), _full((1, KV_LORA)),
                  _full((Q_LORA, MLA_W)), _full((KV_LORA, MLA_W)), _full((N_HEADS_B * V_DIM_B, KV_LORA))],
        out_specs=[pl.BlockSpec((tm, MLA_W), row), pl.BlockSpec((tm, MLA_W), row), blk4(HEAD_PAD), blk4(HEAD_PAD),
                   blk4(V_DIM_B)],
        out_shape=[jax.ShapeDtypeStruct((t, MLA_W), BF16), jax.ShapeDtypeStruct((t, MLA_W), BF16),
                   jax.ShapeDtypeStruct((N_HEADS_B, nb, HEAD_PAD, tm), BF16),
                   jax.ShapeDtypeStruct((N_HEADS_B, nb, HEAD_PAD, tm), BF16),
                   jax.ShapeDtypeStruct((N_HEADS_B, nb, V_DIM_B, tm), BF16)],
        compiler_params=_params(("parallel",)),
    )(proj, proj, proj, posc, freq, qan, kvan, wq, wk, wv)


_MLA_SCALE2 = _MLA_SCALE * _LOG2E


def _mla_fwd(k, qt, vt, w_src):
    t = k.shape[0]
    nb = t // _TQ
    groups = nb // 2

    def body(k_ref, qt_ref, vt_ref, w_ref, o_ref, l_ref, wg_ref, raw, send_sems, recv_sems, local_sem):
        g = pl.program_id(1)
        first = (pl.program_id(0) == 0) & (g == 0)
        last = (pl.program_id(0) == N_HEADS_B - 1) & (g == groups - 1)

        @pl.when(first)
        def _():
            _gather_start(w_ref, wg_ref, send_sems, recv_sems, local_sem)

        def keys(kj):
            return k_ref[pl.ds(pl.multiple_of(kj * _TQ, _TQ), _TQ), :]

        def products(kj, slot):
            kv = keys(kj)
            raw[slot, 0] = _dot(kv, qt_ref[0, 0])
            raw[slot, 1] = _dot(kv, qt_ref[0, 1])

        def update(stats, raw_ref, kj, diagonal=False):
            m, l, acc = stats
            scores = raw_ref[...]
            if diagonal:
                key = lax.broadcasted_iota(jnp.int32, scores.shape, 0)
                qry = lax.broadcasted_iota(jnp.int32, scores.shape, 1)
                scores = jnp.where(key <= qry, scores, NEG)
            m_new = jnp.maximum(m, jnp.max(scores, axis=0, keepdims=True) * _MLA_SCALE2)
            alpha = jnp.exp2(m - m_new)
            p = jnp.exp2(scores * _MLA_SCALE2 - m_new).astype(BF16)
            pv = _dot(jnp.concatenate([vt_ref[0, kj], jnp.ones((16, _TQ), BF16)], axis=0), p)
            return m_new, alpha * l + pv[V_DIM_B:V_DIM_B + 8], alpha * acc + pv[:V_DIM_B]

        def trip(i, stats):
            sa, sb = stats
            products(2 * i + 1, 1)
            sa, sb = update(sa, raw.at[0, 0], 2 * i), update(sb, raw.at[0, 1], 2 * i)
            products(2 * i + 2, 0)
            return update(sa, raw.at[1, 0], 2 * i + 1), update(sb, raw.at[1, 1], 2 * i + 1)

        init = (jnp.full((1, _TQ), NEG, F32), jnp.zeros((8, _TQ), F32), jnp.zeros((V_DIM_B, _TQ), F32))
        products(0, 0)
        sa, sb = lax.fori_loop(0, g, trip, (init, init))
        raw[1, 1] = _dot(keys(2 * g + 1), qt_ref[0, 1])
        sa = update(sa, raw.at[0, 0], 2 * g, True)
        sb = update(update(sb, raw.at[0, 1], 2 * g), raw.at[1, 1], 2 * g + 1, True)
        for which, (m, l, acc) in enumerate((sa, sb)):
            o_ref[0, which] = acc / l[0:1]
            l_ref[0, which] = m + jnp.log(l[0:1]) * _LOG2E

        @pl.when(last)
        def _():
            _gather_wait(w_ref, wg_ref, send_sems, recv_sems, local_sem)

    two = lambda d: pl.BlockSpec((1, 2, d, _TQ), lambda h, g: (h, g, 0, 0))
    return pl.pallas_call(
        body, name="mla_fwd", grid=(N_HEADS_B, groups),
        in_specs=[pl.BlockSpec((t, HEAD_PAD), lambda h, g: (0, h)), two(HEAD_PAD),
                  pl.BlockSpec((1, nb, V_DIM_B, _TQ), lambda h, g: (h, 0, 0, 0)), _HBM],
        out_specs=[two(V_DIM_B), two(1), _HBM],
        out_shape=[jax.ShapeDtypeStruct((N_HEADS_B, nb, V_DIM_B, _TQ), F32),
                   jax.ShapeDtypeStruct((N_HEADS_B, nb, 1, _TQ), F32),
                   jax.ShapeDtypeStruct((N_CHIPS,) + w_src.shape, w_src.dtype)],
        scratch_shapes=[pltpu.VMEM((2, 2, _TQ, _TQ), F32),
                        pltpu.SemaphoreType.DMA((3,)), pltpu.SemaphoreType.DMA((3,)), pltpu.SemaphoreType.DMA(())],
        compiler_params=_params(("arbitrary", "arbitrary")),
    )(k, qt, vt, w_src)


def _ot_spec(tm, d):
    per = _TQ // tm
    return pl.BlockSpec((N_HEADS_B, 1, d, tm), lambda i: (0, i // per, 0, i % per))


def _mix_out_fwd(out_a, out_bt, proj, x, w_oa, w_ob, wb, g2, g3):
    t = x.shape[0]
    tm = 512

    def body(oa_ref, obt_ref, ga_ref, gb_ref, x_ref, woa_ref, wob_ref, wout_ref, g2_ref, g3_ref,
             mg_ref, y_ref, x1_ref, h2_ref):
        oa = _dot(oa_ref[...].astype(BF16), woa_ref[...])
        obt = obt_ref[...].reshape(N_HEADS_B * V_DIM_B, tm).astype(BF16)
        ob = _dot_tn(obt, wob_ref[...])
        merged = (jax.nn.sigmoid(ga_ref[...]) * oa + jax.nn.sigmoid(gb_ref[...]) * ob).astype(BF16)
        mg_ref[...] = merged
        y = _dot(merged, wout_ref[...].reshape(D_MODEL, D_MODEL))
        y_ref[...] = y
        x1 = x_ref[...] + (y * _rms(y)) * g2_ref[...]
        x1_ref[...] = x1
        h2_ref[...] = ((x1 * _rms(x1)) * g3_ref[...]).astype(BF16)

    row = lambda i: (i, 0)
    blk = pl.BlockSpec((tm, D_MODEL), row)
    return pl.pallas_call(
        body, name="mix_out_fwd", grid=(t // tm,),
        in_specs=[pl.BlockSpec((tm, WIDTH_A), row), _ot_spec(tm, V_DIM_B), pl.BlockSpec((tm, D_MODEL), lambda i: (i, 0)),
                  pl.BlockSpec((tm, D_MODEL), lambda i: (i, 1)), blk,
                  _full((WIDTH_A, D_MODEL)), _full((N_HEADS_B * V_DIM_B, D_MODEL)), _wb_spec("w_out"),
                  _full((1, D_MODEL)), _full((1, D_MODEL))],
        out_specs=[blk, blk, blk, blk],
        out_shape=[jax.ShapeDtypeStruct((t, D_MODEL), BF16), jax.ShapeDtypeStruct((t, D_MODEL), F32),
                   jax.ShapeDtypeStruct((t, D_MODEL), F32), jax.ShapeDtypeStruct((t, D_MODEL), BF16)],
        compiler_params=_params(("parallel",)),
    )(out_a, out_bt, proj, proj, x, w_oa, w_ob, wb, g2, g3)


_TM_MLP = 512


def _row_halves(tm):
    return slice(0, tm // 2), slice(tm // 2, tm)


def _up_fwd(h2, wb):
    t = h2.shape[0]
    tm = _TM_MLP

    def body(h_ref, w_ref, a_ref):
        hv = h_ref[...]
        for j in range(N_CHIPS):
            u = _dot(hv, w_ref[j])
            a_ref[:, D_MODEL * j:D_MODEL * (j + 1)] = jnp.square(jnp.maximum(u, 0.0)).astype(BF16)

    return pl.pallas_call(
        body, name="up_fwd", grid=(t // tm,),
        in_specs=[pl.BlockSpec((tm, D_MODEL), lambda i: (i, 0)), _wb_spec("w_up")],
        out_specs=pl.BlockSpec((tm, D_FF), lambda i: (i, 0)),
        out_shape=jax.ShapeDtypeStruct((t, D_FF), BF16),
        compiler_params=_params(("parallel",)),
    )(h2, wb)


def _down_fwd_loss(a, wb, x1, target, g4):
    t = a.shape[0]
    tm = _TM_MLP

    def body(a_ref, w_ref, x1_ref, tg_ref, g_ref, dx2_ref, dyd_ref, dg_ref, loss_ref):
        @pl.when(pl.program_id(0) == 0)
        def _():
            dg_ref[...] = jnp.zeros(dg_ref.shape, F32)
            loss_ref[...] = jnp.zeros(loss_ref.shape, F32)

        yd = _dot(a_ref[...], w_ref[...].reshape(D_FF, D_MODEL))
        r = _rms(yd)
        n = yd * r
        diff = (x1_ref[...] + n * g_ref[...]) - tg_ref[...]
        loss_ref[...] += 0.5 * jnp.sum(jnp.mean(diff * diff, axis=-1, keepdims=True), axis=0, keepdims=True)
        dx2 = diff * (1.0 / D_MODEL)
        dx2_ref[...] = dx2
        dyd, dg = _norm_bwd(dx2, n, r, g_ref[...])
        dyd_ref[...] = dyd.astype(BF16)
        dg_ref[...] += dg

    row = lambda i: (i, 0)
    blk = pl.BlockSpec((tm, D_MODEL), row)
    return pl.pallas_call(
        body, name="down_fwd_loss", grid=(t // tm,),
        in_specs=[pl.BlockSpec((tm, D_FF), row), _wb_spec("w_down"), blk, blk, _full((1, D_MODEL))],
        out_specs=[blk, blk, _full((1, D_MODEL)), _full((1, LANES))],
        out_shape=[jax.ShapeDtypeStruct((t, D_MODEL), F32), jax.ShapeDtypeStruct((t, D_MODEL), BF16),
                   jax.ShapeDtypeStruct((1, D_MODEL), F32), jax.ShapeDtypeStruct((1, LANES), F32)],
        compiler_params=_params(("arbitrary",)),
    )(a, wb, x1, target, g4)


def _matmul_tn(a, b, name, tm, tn, tk=1024):
    t, m = a.shape
    n = b.shape[1]
    tk = min(tk, t)
    nk = t // tk

    def body(a_ref, b_ref, o_ref):
        @pl.when(pl.program_id(2) == 0)
        def _():
            o_ref[...] = jnp.zeros(o_ref.shape, F32)

        o_ref[...] += _dot_tn(a_ref[...].astype(BF16), b_ref[...].astype(BF16))

    return pl.pallas_call(
        body, name=name, grid=(m // tm, n // tn, nk),
        in_specs=[pl.BlockSpec((tk, tm), lambda i, j, k: (k, i)), pl.BlockSpec((tk, tn), lambda i, j, k: (k, j))],
        out_specs=pl.BlockSpec((tm, tn), lambda i, j, k: (i, j)),
        out_shape=jax.ShapeDtypeStruct((m, n), F32),
        compiler_params=_params(("parallel", "parallel", "arbitrary")),
    )(a, b)


_TK_DW = 2048


def _dw_into_blocks(a, b, weight, tm, tk, buf=None):
    t, m = a.shape
    n = b.shape[1]
    tk = min(tk, t)
    nk = t // tk
    rows = PACK_ROWS[weight]
    br = min(tm, rows)
    chips = tm // br
    first = _row_offset(GROUP_B, weight) // br
    per_chip = rows // br
    if weight == "w_up":
        out_map = lambda i, j, k: (j, first + i, 0)
    elif chips > 1:
        out_map = lambda i, j, k: (i, first, 0)
    else:
        out_map = lambda i, j, k: (i // per_chip, first + i % per_chip, 0)

    def body(a_ref, b_ref, *rest):
        o_ref = rest[-1]

        @pl.when(pl.program_id(2) == 0)
        def _():
            o_ref[...] = jnp.zeros(o_ref.shape, F32)

        o_ref[...] += _dot_tn(a_ref[...].astype(BF16), b_ref[...].astype(BF16)).reshape(o_ref.shape)

    in_specs = [pl.BlockSpec((tk, tm), lambda i, j, k: (k, i)), pl.BlockSpec((tk, D_MODEL), lambda i, j, k: (k, j))]
    operands = [a, b]
    if buf is not None:
        in_specs.append(pl.BlockSpec(memory_space=pl.ANY))
        operands.append(buf)
    total = sum(PACK_ROWS[w] for w in GROUP_B)
    return pl.pallas_call(
        body, name="dw_" + weight[2:], grid=(m // tm, n // D_MODEL, nk),
        in_specs=in_specs, out_specs=pl.BlockSpec((chips, br, D_MODEL), out_map),
        out_shape=jax.ShapeDtypeStruct((N_CHIPS, total, D_MODEL), F32),
        input_output_aliases={} if buf is None else {2: 0},
        compiler_params=_params(("parallel", "parallel", "arbitrary")),
    )(*operands)


def _down_bwd(dyd, wb, a):
    t = dyd.shape[0]
    tm = _TM_MLP

    def body(d_ref, w_ref, a_ref, du_ref):
        dv = d_ref[...]
        for j in range(N_CHIPS):
            cols = slice(D_MODEL * j, D_MODEL * (j + 1))
            av = a_ref[:, cols].astype(F32)
            relu_u = jnp.where(av > 0.0, av * lax.rsqrt(av), 0.0)
            du_ref[:, cols] = (_dot_nt(dv, w_ref[j]) * (2.0 * relu_u)).astype(BF16)

    row = lambda i: (i, 0)
    return pl.pallas_call(
        body, name="down_bwd", grid=(t // tm,),
        in_specs=[pl.BlockSpec((tm, D_MODEL), row), _wb_spec("w_down"), pl.BlockSpec((tm, D_FF), row)],
        out_specs=pl.BlockSpec((tm, D_FF), row),
        out_shape=jax.ShapeDtypeStruct((t, D_FF), BF16),
        compiler_params=_params(("parallel",)),
    )(dyd, wb, a)


def _up_bwd(du, wb, x1, dx2, y, g3, g2):
    t = du.shape[0]
    tm = _TM_MLP

    def body(du_ref, w_ref, x1_ref, dx2_ref, y_ref, g3_ref, g2_ref, dx1_ref, dy_ref, dg3_ref, dg2_ref):
        @pl.when(pl.program_id(0) == 0)
        def _():
            dg3_ref[...] = jnp.zeros(dg3_ref.shape, F32)
            dg2_ref[...] = jnp.zeros(dg2_ref.shape, F32)

        dh2 = _dot_nt(du_ref[:, 0:D_MODEL], w_ref[0])
        for j in range(1, N_CHIPS):
            dh2 = dh2 + _dot_nt(du_ref[:, D_MODEL * j:D_MODEL * (j + 1)], w_ref[j])
        x1 = x1_ref[...]
        r3 = _rms(x1)
        d3, dg3 = _norm_bwd(dh2, x1 * r3, r3, g3_ref[...])
        dx1 = dx2_ref[...] + d3
        dx1_ref[...] = dx1
        dg3_ref[...] += dg3
        y = y_ref[...]
        r2 = _rms(y)
        dy, dg2 = _norm_bwd(dx1, y * r2, r2, g2_ref[...])
        dy_ref[...] = dy.astype(BF16)
        dg2_ref[...] += dg2

    row = lambda i: (i, 0)
    blk = pl.BlockSpec((tm, D_MODEL), row)
    return pl.pallas_call(
        body, name="up_bwd", grid=(t // tm,),
        in_specs=[pl.BlockSpec((tm, D_FF), row), _wb_spec("w_up"),
                  blk, blk, blk, _full((1, D_MODEL)), _full((1, D_MODEL))],
        out_specs=[blk, blk, _full((1, D_MODEL)), _full((1, D_MODEL))],
        out_shape=[jax.ShapeDtypeStruct((t, D_MODEL), F32), jax.ShapeDtypeStruct((t, D_MODEL), BF16),
                   jax.ShapeDtypeStruct((1, D_MODEL), F32), jax.ShapeDtypeStruct((1, D_MODEL), F32)],
        compiler_params=_params(("arbitrary",)),
    )(du, wb, x1, dx2, y, g3, g2)


def _mix_out_bwd(dy, out_a, out_bt, proj, w_oa, w_ob, wb):
    t = dy.shape[0]
    tm = 256
    nb = t // _TQ

    def body(dy_ref, oa_ref, obt_ref, ga_ref, gb_ref, woa_ref, wob_ref, wout_ref,
             doa_ref, dob_ref, dg_ref, da_ref, db_ref, dbt_ref, dela_ref, delb_ref):
        dm = _dot_nt(dy_ref[...], wout_ref[...].reshape(D_MODEL, D_MODEL))
        out_a_v = oa_ref[...]
        out_bt_v = obt_ref[...].reshape(N_HEADS_B * V_DIM_B, tm)
        oa = _dot(out_a_v.astype(BF16), woa_ref[...])
        ob = _dot_tn(out_bt_v.astype(BF16), wob_ref[...])
        sa, sb = jax.nn.sigmoid(ga_ref[...]), jax.nn.sigmoid(gb_ref[...])
        doa = (dm * sa).astype(BF16)
        dob = (dm * sb).astype(BF16)
        doa_ref[...] = doa
        dob_ref[...] = dob
        dg_ref[:, :D_MODEL] = (dm * oa * (sa * (1.0 - sa))).astype(BF16)
        dg_ref[:, D_MODEL:] = (dm * ob * (sb * (1.0 - sb))).astype(BF16)
        d_out_a = _dot_nt(doa, woa_ref[...])
        da_ref[...] = d_out_a
        prod_at = (d_out_a * out_a_v).T
        dela_ref[...] = jnp.concatenate(
            [jnp.sum(_head_rows(prod_at, h), axis=0, keepdims=True) for h in range(N_HEADS_A)], axis=0)
        d_out_b = _dot_nt(dob, wob_ref[...])
        d_out_bt = _dot_nt(wob_ref[...], dob)
        prod_bt = d_out_bt * out_bt_v
        for h in range(N_HEADS_B):
            db_ref[h] = d_out_b[:, V_DIM_B * h:V_DIM_B * (h + 1)].astype(BF16)
            dbt_ref[h, 0] = d_out_bt[V_DIM_B * h:V_DIM_B * (h + 1), :].astype(BF16)
            delb_ref[h, 0] = jnp.sum(prod_bt[V_DIM_B * h:V_DIM_B * (h + 1), :], axis=0, keepdims=True)

    row = lambda i: (i, 0)
    blk = pl.BlockSpec((tm, D_MODEL), row)
    return pl.pallas_call(
        body, name="mix_out_bwd", grid=(t // tm,),
        in_specs=[blk, pl.BlockSpec((tm, WIDTH_A), row), _ot_spec(tm, V_DIM_B),
                  pl.BlockSpec((tm, D_MODEL), lambda i: (i, 0)), pl.BlockSpec((tm, D_MODEL), lambda i: (i, 1)),
                  _full((WIDTH_A, D_MODEL)), _full((N_HEADS_B * V_DIM_B, D_MODEL)), _wb_spec("w_out")],
        out_specs=[blk, blk, pl.BlockSpec((tm, 2 * D_MODEL), row), pl.BlockSpec((tm, WIDTH_A), row),
                   pl.BlockSpec((N_HEADS_B, tm, V_DIM_B), lambda i: (0, i, 0)), _ot_spec(tm, V_DIM_B),
                   pl.BlockSpec((N_HEADS_A, tm), lambda i: (0, i)), _ot_spec(tm, 1)],
        out_shape=[jax.ShapeDtypeStruct((t, D_MODEL), BF16)] * 2
        + [jax.ShapeDtypeStruct((t, D_IN_PAD), BF16), jax.ShapeDtypeStruct((t, WIDTH_A), F32), jax.ShapeDtypeStruct((N_HEADS_B, t, V_DIM_B), BF16),
           jax.ShapeDtypeStruct((N_HEADS_B, nb, V_DIM_B, _TQ), BF16), jax.ShapeDtypeStruct((N_HEADS_A, t), F32),
           jax.ShapeDtypeStruct((N_HEADS_B, nb, 1, _TQ), F32)],
        compiler_params=_params(("parallel",)),
    )(dy, out_a, out_bt, proj, proj, w_oa, w_ob, wb)


def _dw_ob(out_bt, dob):
    t = dob.shape[0]
    nb = t // _TQ

    def body(obt_ref, dob_ref, o_ref):
        @pl.when(pl.program_id(0) == 0)
        def _():
            o_ref[...] = jnp.zeros(o_ref.shape, F32)

        obt = obt_ref[...].reshape(N_HEADS_B * V_DIM_B, _TQ).astype(BF16)
        o_ref[...] += _dot(obt, dob_ref[...])

    return pl.pallas_call(
        body, name="dw_o_b", grid=(nb,),
        in_specs=[pl.BlockSpec((N_HEADS_B, 1, V_DIM_B, _TQ), lambda i: (0, i, 0, 0)),
                  pl.BlockSpec((_TQ, D_MODEL), lambda i: (i, 0))],
        out_specs=_full((N_HEADS_B * V_DIM_B, D_MODEL)),
        out_shape=jax.ShapeDtypeStruct((N_HEADS_B * V_DIM_B, D_MODEL), F32),
        compiler_params=_params(("arbitrary",)),
    )(out_bt, dob)


def _mla_bwd(q, k, qt, kt, vt, d_out, d_out_t, lse, delta, gp):
    t = q.shape[0]
    nb = t // _TQ

    def body(k_ref, kt_ref, vt_ref, q_ref, qt_ref, do_ref, dot_ref, lrow_ref, drow_ref, gp_ref,
             dq_ref, dkt_ref, dvt_ref, land_ref, l_rep, d_rep, send_sems, recv_sems):
        step = pl.program_id(1)
        kj = nb - 1 - step

        @pl.when((pl.program_id(0) == 0) & (step == 0))
        def _():
            _scatter_start(gp_ref, land_ref, send_sems, recv_sems)

        @pl.when(step == 0)
        def _():
            dq_ref[...] = jnp.zeros(dq_ref.shape, F32)
            for b in range(nb):
                l_rep[_TQ * b:_TQ * (b + 1), :] = jnp.broadcast_to(lrow_ref[0, b], (LANES, _TQ)).T
                d_rep[_TQ * b:_TQ * (b + 1), :] = jnp.broadcast_to(drow_ref[0, b], (LANES, _TQ)).T

        kv, k_t, v_t = k_ref[...], kt_ref[0, 0], vt_ref[0, 0]

        def rows_of(qi):
            return pl.ds(pl.multiple_of(qi * _TQ, _TQ), _TQ)

        def products(qi, diagonal=False):
            s = _dot(q_ref[rows_of(qi), :], k_t) * _MLA_SCALE2
            if diagonal:
                qry = lax.broadcasted_iota(jnp.int32, s.shape, 0)
                key = lax.broadcasted_iota(jnp.int32, s.shape, 1)
                s = jnp.where(key <= qry, s, NEG)
            return s, _dot(do_ref[0, rows_of(qi), :], v_t)

        def update(carry, prods, qi):
            dkt, dvt = carry
            s, dp = prods
            lse, delta = l_rep[rows_of(qi), :], d_rep[rows_of(qi), :]
            ps, dss = [], []
            for c in range(_TQ // LANES):
                strip = slice(LANES * c, LANES * (c + 1))
                p = jnp.exp2(s[:, strip] - lse)
                ps.append(p.astype(BF16))
                dss.append((p * (dp[:, strip] - delta) * _MLA_SCALE).astype(BF16))
            p_b, ds_b = jnp.concatenate(ps, axis=1), jnp.concatenate(dss, axis=1)
            dvt = dvt + _dot(dot_ref[0, qi], p_b)
            dkt = dkt + _dot(qt_ref[0, qi], ds_b)
            dq_ref[rows_of(qi), :] += _dot(ds_b, kv)
            return dkt, dvt

        def pair(i, carry):
            qa = kj + 1 + 2 * i
            pa, pb = products(qa), products(qa + 1)
            return update(update(carry, pa, qa), pb, qa + 1)

        init = (jnp.zeros((HEAD_PAD, _TQ), F32), jnp.zeros((V_DIM_B, _TQ), F32))
        carry = update(init, products(kj, True), kj)
        pairs = (nb - 1 - kj) // 2
        carry = lax.fori_loop(0, pairs, pair, carry)
        dkt, dvt = lax.fori_loop(kj + 1 + 2 * pairs, nb, lambda qi, cr: update(cr, products(qi), qi), carry)
        dkt_ref[0, 0] = dkt
        dvt_ref[0, 0] = dvt

        @pl.when((pl.program_id(0) == N_HEADS_B - 1) & (step == nb - 1))
        def _():
            _scatter_wait(gp_ref, land_ref, send_sems, recv_sems)

    head4 = lambda d: pl.BlockSpec((1, nb, d, _TQ), lambda h, s: (h, 0, 0, 0))
    blk4 = lambda d: pl.BlockSpec((1, 1, d, _TQ), lambda h, s: (h, nb - 1 - s, 0, 0))
    head3 = lambda d: pl.BlockSpec((1, t, d), lambda h, kj: (h, 0, 0))
    per_head = pl.BlockSpec((t, HEAD_PAD), lambda h, kj: (0, h))
    return pl.pallas_call(
        body, name="mla_bwd", grid=(N_HEADS_B, nb),
        in_specs=[pl.BlockSpec((_TQ, HEAD_PAD), lambda h, s: (nb - 1 - s, h)), blk4(HEAD_PAD), blk4(V_DIM_B),
                  per_head, head4(HEAD_PAD), head3(V_DIM_B), head4(V_DIM_B), head4(1), head4(1), _HBM],
        out_specs=[per_head, blk4(HEAD_PAD), blk4(V_DIM_B), _HBM],
        out_shape=[jax.ShapeDtypeStruct((t, MLA_W), F32), jax.ShapeDtypeStruct((N_HEADS_B, nb, HEAD_PAD, _TQ), F32),
                   jax.ShapeDtypeStruct((N_HEADS_B, nb, V_DIM_B, _TQ), F32),
                   jax.ShapeDtypeStruct((3,) + gp.shape[1:], gp.dtype)],
        scratch_shapes=[pltpu.VMEM((t, LANES), F32), pltpu.VMEM((t, LANES), F32),
                        pltpu.SemaphoreType.DMA((3,)), pltpu.SemaphoreType.DMA((3,))],
        compiler_params=_params(("arbitrary", "arbitrary")),
    )(k, kt, vt, q, qt, d_out, d_out_t, lse, delta, gp)


def _mla_prep_bwd(dq, dkt, dvt, proj, posc, freq, qan, kvan, wq, wk, wv, swap_src):
    t = dq.shape[0]
    tm = _TQ

    def body(dq_ref, dkt_ref, dvt_ref, cq_ref, ckv_ref, pos_ref, f_ref, qan_ref, kvan_ref, wq_ref, wk_ref, wv_ref, src_ref,
             dcq_ref, dckv_ref, dkr_ref, dwq_ref, dwk_ref, dwv_ref, dqan_ref, dkvan_ref, got_ref, send_sem, recv_sem):
        swap = _sibling_copy(src_ref, got_ref, send_sem, recv_sem)

        @pl.when(pl.program_id(0) == 0)
        def _():
            swap.start()
            for r in (dwq_ref, dwk_ref, dwv_ref, dqan_ref, dkvan_ref):
                r[...] = jnp.zeros(r.shape, F32)

        cq = cq_ref[...]
        rq = _rms(cq)
        nq_ = cq * rq
        cqn = (nq_ * qan_ref[...]).astype(BF16)
        ckv = ckv_ref[...]
        rkv = _rms(ckv)
        nkv = ckv * rkv
        ckvn = (nkv * kvan_ref[...]).astype(BF16)
        c, s, lo, hi = _rope_coeffs(pos_ref[...], f_ref[...])
        dkr = jnp.zeros((tm, LANES), F32)
        dqb, dkb = [], []
        for h in range(N_HEADS_B):
            dqb.append(_unrope(dq_ref[:, HEAD_PAD * h:HEAD_PAD * (h + 1)], c, s, lo, hi).astype(BF16))
            dk_h = dkt_ref[h, 0].T
            dkr = dkr + dk_h
            dkb.append(dk_h.astype(BF16))
        dqb, dkb = jnp.concatenate(dqb, axis=1), jnp.concatenate(dkb, axis=1)
        dkr = jnp.where(lo | hi, _unrope(dkr, c, s, lo, hi), 0.0)
        dkr_ref[...] = pltpu.roll(dkr, LANES - QK_NOPE, axis=1).astype(BF16)
        dvb = dvt_ref[...].reshape(N_HEADS_B * V_DIM_B, tm).T.astype(BF16)
        dwq_ref[...] += _dot_tn(cqn, dqb)
        dwk_ref[...] += _dot_tn(ckvn, dkb)
        dwv_ref[...] += _dot_tn(ckvn, dvb)
        dcqn = _dot_nt(dqb, wq_ref[...])
        dckvn = _dot_nt(dkb, wk_ref[...]) + _dot_nt(dvb, wv_ref[...])
        dcq, dqan = _norm_bwd(dcqn, nq_, rq, qan_ref[...])
        dckv, dkvan = _norm_bwd(dckvn, nkv, rkv, kvan_ref[...])
        dcq_ref[...] = dcq.astype(BF16)
        dckv_ref[...] = dckv.astype(BF16)
        dqan_ref[...] += dqan
        dkvan_ref[...] += dkvan

        @pl.when(pl.program_id(0) == t // tm - 1)
        def _():
            swap.wait_recv()
            swap.wait_send()

    row = lambda i: (i, 0)
    vw = N_HEADS_B * V_DIM_B
    return pl.pallas_call(
        body, name="mla_prep_bwd", grid=(t // tm,),
        in_specs=[pl.BlockSpec((tm, MLA_W), row), pl.BlockSpec((N_HEADS_B, 1, HEAD_PAD, tm), lambda i: (0, i, 0, 0)),
                  pl.BlockSpec((N_HEADS_B, 1, V_DIM_B, tm), lambda i: (0, i, 0, 0)),
                  pl.BlockSpec((tm, Q_LORA), lambda i: (i, _CQ_BLK)),
                  pl.BlockSpec((tm, LANES), lambda i: (i, _CKV_BLK)),
                  pl.BlockSpec((tm, 1), row), _full((1, LANES)), _full((1, Q_LORA)), _full((1, KV_LORA)),
                  _full((Q_LORA, MLA_W)), _full((KV_LORA, MLA_W)), _full((KV_LORA, vw)), _HBM],
        out_specs=[pl.BlockSpec((tm, Q_LORA), row), pl.BlockSpec((tm, LANES), row), pl.BlockSpec((tm, LANES), row),
                   _full((Q_LORA, MLA_W)), _full((KV_LORA, MLA_W)), _full((KV_LORA, vw)),
                   _full((1, Q_LORA)), _full((1, KV_LORA)), _HBM],
        out_shape=[jax.ShapeDtypeStruct((t, Q_LORA), BF16), jax.ShapeDtypeStruct((t, LANES), BF16),
                   jax.ShapeDtypeStruct((t, LANES), BF16),
                   jax.ShapeDtypeStruct((Q_LORA, MLA_W), F32), jax.ShapeDtypeStruct((KV_LORA, MLA_W), F32),
                   jax.ShapeDtypeStruct((KV_LORA, vw), F32),
                   jax.ShapeDtypeStruct((1, Q_LORA), F32), jax.ShapeDtypeStruct((1, KV_LORA), F32),
                   jax.ShapeDtypeStruct(swap_src.shape, swap_src.dtype)],
        scratch_shapes=[pltpu.SemaphoreType.DMA(()), pltpu.SemaphoreType.DMA(())],
        compiler_params=_params(("arbitrary",)),
    )(dq, dkt, dvt, proj, proj, posc, freq, qan, kvan, wq, wk, wv, swap_src)


def _swa_bwd(proj, d_out, lse, delta, posc, posr, sinks):
    t = proj.shape[0]
    per = _SWA_PER_STEP
    span = per * BLOCK
    steps = t // span

    def body(q_ref, kc_ref, kp_ref, vc_ref, vp_ref, do_ref, l_ref, d_ref, pq_ref, pc_ref, pp_ref, sink_ref,
             dq_ref, dk_ref, dv_ref, ds_ref, dkb_s, dvb_s, dk_keep, dv_keep):
        n = pl.program_id(0)

        @pl.when(n == 0)
        def _():
            ds_ref[...] = jnp.zeros(ds_ref.shape, F32)
            dk_keep[...] = jnp.zeros(dk_keep.shape, F32)
            dv_keep[...] = jnp.zeros(dv_keep.shape, F32)

        @pl.when(n < steps)
        def _():
            k_all = jnp.concatenate([kp_ref[...], kc_ref[...]], axis=0)
            v_all = jnp.concatenate([vp_ref[...], vc_ref[...]], axis=0)
            pos_all = jnp.concatenate([pp_ref[...], pc_ref[...]], axis=0)
            ki = lax.broadcasted_iota(jnp.int32, (2 * BLOCK, BLOCK), 0)
            qi = lax.broadcasted_iota(jnp.int32, (2 * BLOCK, BLOCK), 1)
            window = (ki > qi) & (ki <= qi + WINDOW)
            lane = lax.broadcasted_iota(jnp.int32, (1, LANES), 1)
            dsink = jnp.zeros((1, LANES), F32)
            for sub in range(per):
                band = slice(BLOCK * sub, BLOCK * (sub + 2))
                own = slice(BLOCK * sub, BLOCK * (sub + 1))
                kb, vb = k_all[band], v_all[band]
                dist = jnp.abs(pos_all[band] - pq_ref[:, own])
                valid = window & ((n > 0) | (ki >= BLOCK)) if sub == 0 else window
                qv, dov = q_ref[own, :], do_ref[own, :]
                q_t, do_t, kb_t = qv.T, dov.T, kb.T
                dq_t = []
                for kh in range(N_KV_A):
                    heads = range(_GROUP_A * kh, _GROUP_A * (kh + 1))
                    st_g = _dot(_head_cols(kb, kh).astype(BF16), _group_t(q_t, kh))
                    dpt_g = _dot(_head_cols(vb, kh).astype(BF16), _group_t(do_t, kh))
                    pts, dsts = [], []
                    for j, h in enumerate(heads):
                        st = _swa_scores_t(st_g, j, h, dist, valid)
                        l_h, d_h = l_ref[h:h + 1, own], d_ref[h:h + 1, own]
                        pt = jnp.exp2(st - l_h)
                        p_sink = jnp.exp2(sink_ref[0:1, h:h + 1] * _LOG2E - l_h)
                        dsink = dsink + jnp.where(lane == h, jnp.sum(-p_sink * d_h, axis=1, keepdims=True), 0.0)
                        dst = pt * (dpt_g[:, BLOCK * j:BLOCK * (j + 1)] - d_h) * _SWA_SCALE
                        pts.append(pt.astype(BF16))
                        dsts.append(dst.astype(BF16))
                    pt_g, dst_g = jnp.concatenate(pts, axis=1), jnp.concatenate(dsts, axis=1)
                    q_g = jnp.concatenate([_head_cols(qv, h) for h in heads], axis=0).astype(BF16)
                    do_g = jnp.concatenate([_head_cols(dov, h) for h in heads], axis=0).astype(BF16)
                    dkb_s[sub, :, HEAD_DIM_A * kh:HEAD_DIM_A * (kh + 1)] = _dot(dst_g, q_g)
                    dvb_s[sub, :, HEAD_DIM_A * kh:HEAD_DIM_A * (kh + 1)] = _dot(pt_g, do_g)
                    dq_g = _dot(_head_rows(kb_t, kh).astype(BF16), dst_g)
                    dq_t.extend(dq_g[:, BLOCK * j:BLOCK * (j + 1)] for j in range(_GROUP_A))
                dq_ref[own, :] = jnp.concatenate(dq_t, axis=0).T
            ds_ref[...] += dsink
            for keep, out, parts in ((dk_keep, dk_ref, dkb_s), (dv_keep, dv_ref, dvb_s)):
                out[0:span - BLOCK, :] = keep[0:span - BLOCK, :]
                out[span - BLOCK:span, :] = keep[span - BLOCK:span, :] + parts[0, 0:BLOCK, :]
                for s in range(per - 1):
                    keep[BLOCK * s:BLOCK * (s + 1), :] = parts[s, BLOCK:2 * BLOCK, :] + parts[s + 1, 0:BLOCK, :]
                keep[span - BLOCK:span, :] = parts[per - 1, BLOCK:2 * BLOCK, :]

        @pl.when(n == steps)
        def _():
            dk_ref[...] = dk_keep[...]
            dv_ref[...] = dv_keep[...]

    last = steps - 1
    cur = lambda n: (jnp.minimum(n, last), 0)
    cur_t = lambda n: (0, jnp.minimum(n, last))
    prv = lambda n: jnp.maximum(per * jnp.minimum(n, last) - 1, 0)
    out_prev = lambda n: (jnp.maximum(n - 1, 0), 0)
    return pl.pallas_call(
        body, name="swa_bwd", grid=(steps + 1,),
        in_specs=[pl.BlockSpec((span, WIDTH_A), lambda n: (jnp.minimum(n, last), _QA_BLK)),
                  pl.BlockSpec((span, LANES), lambda n: (jnp.minimum(n, last), _KA_BLK)),
                  pl.BlockSpec((BLOCK, LANES), lambda n: (prv(n), _KA_BLK)),
                  pl.BlockSpec((span, LANES), lambda n: (jnp.minimum(n, last), _VA_BLK)),
                  pl.BlockSpec((BLOCK, LANES), lambda n: (prv(n), _VA_BLK)),
                  pl.BlockSpec((span, WIDTH_A), cur), pl.BlockSpec((N_HEADS_A, span), cur_t),
                  pl.BlockSpec((N_HEADS_A, span), cur_t), pl.BlockSpec((1, span), cur_t),
                  pl.BlockSpec((span, 1), cur), pl.BlockSpec((BLOCK, 1), lambda n: (prv(n), 0)),
                  _full((1, N_HEADS_A))],
        out_specs=[pl.BlockSpec((span, WIDTH_A), cur), pl.BlockSpec((span, LANES), out_prev),
                   pl.BlockSpec((span, LANES), out_prev), _full((1, LANES))],
        out_shape=[jax.ShapeDtypeStruct((t, WIDTH_A), F32), jax.ShapeDtypeStruct((t, LANES), F32),
                   jax.ShapeDtypeStruct((t, LANES), F32), jax.ShapeDtypeStruct((1, LANES), F32)],
        scratch_shapes=[pltpu.VMEM((per, 2 * BLOCK, LANES), F32), pltpu.VMEM((per, 2 * BLOCK, LANES), F32),
                        pltpu.VMEM((span, LANES), F32), pltpu.VMEM((span, LANES), F32)],
        compiler_params=_params(("arbitrary",)),
    )(proj, proj, proj, proj, proj, d_out, lse, delta, posr, posc, posc, sinks)


def _in_bwd(dproj, w_in_t, x, dx1, g1, gp):
    t = x.shape[0]
    tm = 512
    steps = t // tm

    def body(dp_ref, w_ref, x_ref, dx1_ref, g_ref, gp_ref, dx_ref, dg_ref, land_ref, send_sems, recv_sems):
        i = pl.program_id(0)

        @pl.when(i == 0)
        def _():
            dg_ref[...] = jnp.zeros(dg_ref.shape, F32)
            _scatter_start(gp_ref, land_ref, send_sems, recv_sems)

        for rows in _row_halves(tm):
            dh = _dot(dp_ref[rows, :], w_ref[...])
            xv = x_ref[rows, :]
            r = _rms(xv)
            dx, dg = _norm_bwd(dh, xv * r, r, g_ref[...])
            dx_ref[rows, :] = dx1_ref[rows, :] + dx
            dg_ref[...] += dg

        @pl.when(i == steps - 1)
        def _():
            _scatter_wait(gp_ref, land_ref, send_sems, recv_sems)

    row = lambda i: (i, 0)
    blk = pl.BlockSpec((tm, D_MODEL), row)
    return pl.pallas_call(
        body, name="in_bwd", grid=(steps,),
        in_specs=[pl.BlockSpec((tm, D_IN_PAD), row), _full((D_IN_PAD, D_MODEL)), blk, blk, _full((1, D_MODEL)), _HBM],
        out_specs=[blk, _full((1, D_MODEL)), _HBM],
        out_shape=[jax.ShapeDtypeStruct((t, D_MODEL), F32), jax.ShapeDtypeStruct((1, D_MODEL), F32),
                   jax.ShapeDtypeStruct((3,) + gp.shape[1:], gp.dtype)],
        scratch_shapes=[pltpu.SemaphoreType.DMA((3,)), pltpu.SemaphoreType.DMA((3,))],
        compiler_params=_params(("arbitrary",)),
    )(dproj, w_in_t, x, dx1, g1, gp)


def _adamw_store(w, g, m, v, out_refs):
    g_out, d_out, m_out, v_out = out_refs
    m_new = ADAM_B1 * m + (1.0 - ADAM_B1) * g
    v_new = ADAM_B2 * v + (1.0 - ADAM_B2) * jnp.square(g)
    m_hat = m_new / (1.0 - ADAM_B1 ** ADAM_STEP)
    v_hat = v_new / (1.0 - ADAM_B2 ** ADAM_STEP)
    g_out[...] = g
    d_out[...] = -ADAM_LR * (m_hat / (jnp.sqrt(v_hat) + ADAM_EPS) + ADAM_WD * w)
    m_out[...] = m_new
    v_out[...] = v_new


_SMALL_SLOTS = {"pre_norm_mix": (0, 0, D_MODEL), "post_norm_mix": (1, 0, D_MODEL), "pre_norm_mlp": (2, 0, D_MODEL),
                "post_norm_mlp": (3, 0, D_MODEL), "q_a_norm": (4, 0, Q_LORA), "kv_a_norm": (4, Q_LORA, KV_LORA),
                "sinks": (4, Q_LORA + KV_LORA, N_HEADS_A)}
_LOSS_ROW = 5


def _adamw_small(red, w, m, v):
    names = tuple(_SMALL_SLOTS)
    n = len(names)

    def body(*refs):
        red_ref, ws, ms, vs, outs = refs[0], refs[1:1 + n], refs[1 + n:1 + 2 * n], refs[1 + 2 * n:1 + 3 * n], refs[1 + 3 * n:]
        for k, name in enumerate(names):
            row, lane, width = _SMALL_SLOTS[name]
            g = red_ref[row:row + 1, lane:lane + width]
            _adamw_store(ws[k][...], g, ms[k][...], vs[k][...], outs[4 * k:4 * k + 4])

    vmem = pl.BlockSpec(memory_space=pltpu.VMEM)
    res = pl.pallas_call(
        body, name="adamw_small", in_specs=[vmem] * (1 + 3 * n), out_specs=[vmem] * (4 * n),
        out_shape=[jax.ShapeDtypeStruct(w[name].shape, F32) for name in names for _ in range(4)],
    )(red, *[w[k] for k in names], *[m[k] for k in names], *[v[k] for k in names])
    return {name: res[4 * k:4 * k + 4] for k, name in enumerate(names)}


_ADAMW_RIDERS = ("w_up", "w_down", "w_out")


def _dw_in_adamw(dproj, h, g_parts, w, m, v):
    t, cols = dproj.shape
    tm, tk = cols // 2, min(1024, t)
    rows_out = N_CHIPS * SHARD_SHAPES["w_in"][1]
    nk = t // tk
    steps = 2 * nk
    names = _ADAMW_RIDERS
    n = len(names)

    def body(a_ref, b_ref, *rest):
        g1s, g2s, ws, ms, vs = (rest[n * j:n * (j + 1)] for j in range(5))
        o_ref, outs = rest[5 * n], rest[5 * n + 1:]

        @pl.when(pl.program_id(2) == 0)
        def _():
            o_ref[...] = jnp.zeros(o_ref.shape, F32)

        o_ref[...] += _dot_tn(a_ref[...], b_ref[...])
        for j in range(n):
            _adamw_store(ws[j][...], g1s[j][...] + g2s[j][...], ms[j][...], vs[j][...], outs[4 * j:4 * j + 4])

    def rider_spec(name, packed):
        br = SHARD_SHAPES[name][0] // steps
        first = _row_offset(GROUP_B, name) // br if packed else 0
        return pl.BlockSpec((br, D_MODEL), lambda i, j, k: (first + i * nk + k, 0))

    g_specs = [rider_spec(name, True) for name in names]
    own_specs = [rider_spec(name, False) for name in names]
    res = pl.pallas_call(
        body, name="dw_in", grid=(2, 1, nk),
        in_specs=[pl.BlockSpec((tk, tm), lambda i, j, k: (k, i)), pl.BlockSpec((tk, D_MODEL), lambda i, j, k: (k, 0))]
        + g_specs * 2 + own_specs * 3,
        out_specs=[pl.BlockSpec((tm, D_MODEL), lambda i, j, k: (i, 0))] + [s for s in own_specs for _ in range(4)],
        out_shape=[jax.ShapeDtypeStruct((rows_out, D_MODEL), F32)]
        + [jax.ShapeDtypeStruct(SHARD_SHAPES[name], F32) for name in names for _ in range(4)],
        compiler_params=_params(("arbitrary", "arbitrary", "arbitrary")),
    )(dproj, h, *[g_parts[0]] * n, *[g_parts[1]] * n, *[w[k] for k in names], *[m[k] for k in names],
      *[v[k] for k in names])
    return res[0], {name: res[1 + 4 * j:5 + 4 * j] for j, name in enumerate(names)}


def _adamw(w, g_parts, m, v, name, block, g_row_off=0):
    r, c = w.shape
    br, bc = block
    ng = len(g_parts)

    def body(*refs):
        w_ref, g_refs, m_ref, v_ref = refs[0], refs[1:1 + ng], refs[1 + ng], refs[2 + ng]
        g = g_refs[0][...]
        for gr in g_refs[1:]:
            g = g + gr[...]
        _adamw_store(w_ref[...], g, m_ref[...], v_ref[...], refs[3 + ng:])

    assert g_row_off % br == 0 and r % br == 0 and c % bc == 0
    blk = pl.BlockSpec(block, lambda i, j: (i, j))
    g_blk = pl.BlockSpec(block, lambda i, j: (i + g_row_off // br, j))
    return pl.pallas_call(
        body, name=name, grid=(r // br, c // bc),
        in_specs=[blk] + [g_blk] * ng + [blk, blk], out_specs=[blk] * 4,
        out_shape=[jax.ShapeDtypeStruct((r, c), F32)] * 4,
        compiler_params=_params(("parallel", "parallel")),
    )(w, *g_parts, m, v)


_HBM = pl.BlockSpec(memory_space=pltpu.HBM)


def _other_chips(x, y):
    return ((1 - x, y), (x, 1 - y), (1 - x, 1 - y))


def _gather_copies(src, out, send_sems, recv_sems, local_sem):
    x, y, c = lax.axis_index("x"), lax.axis_index("y"), lax.axis_index("c")
    me = 2 * x + y
    local = pltpu.make_async_copy(src, out.at[me], local_sem)

    def copies(arriving):
        return [pltpu.make_async_remote_copy(src_ref=src, dst_ref=out.at[2 * px + py if arriving else me],
                                             send_sem=send_sems.at[j], recv_sem=recv_sems.at[j], device_id=(px, py, c),
                                             device_id_type=MESH)
                for j, (px, py) in enumerate(_other_chips(x, y))]

    return local, copies


def _gather_start(src, out, send_sems, recv_sems, local_sem):
    local, copies = _gather_copies(src, out, send_sems, recv_sems, local_sem)
    local.start()
    for cp in copies(False):
        cp.start()


def _gather_wait(src, out, send_sems, recv_sems, local_sem):
    local, copies = _gather_copies(src, out, send_sems, recv_sems, local_sem)
    for cp in copies(True):
        cp.wait_recv()
    for cp in copies(False):
        cp.wait_send()
    local.wait()


def _scatter_copies(src, land, send_sems, recv_sems):
    x, y, c = lax.axis_index("x"), lax.axis_index("y"), lax.axis_index("c")
    return [pltpu.make_async_remote_copy(src_ref=src.at[2 * px + py], dst_ref=land.at[j], send_sem=send_sems.at[j],
                                         recv_sem=recv_sems.at[j], device_id=(px, py, c), device_id_type=MESH)
            for j, (px, py) in enumerate(_other_chips(x, y))]


def _scatter_start(src, land, send_sems, recv_sems):
    for cp in _scatter_copies(src, land, send_sems, recv_sems):
        cp.start()


def _scatter_wait(src, land, send_sems, recv_sems):
    copies = _scatter_copies(src, land, send_sems, recv_sems)
    for cp in copies:
        cp.wait_recv()
    for cp in copies:
        cp.wait_send()


def _all_gather_chips(packed):
    r = packed.shape[0]
    half = r // 2

    def body(src, out, ici_send, ici_recv, d2d_send, d2d_recv, local_sem):
        x, y, c = lax.axis_index("x"), lax.axis_index("y"), lax.axis_index("c")
        me = 2 * x + y
        mine = pl.ds(pl.multiple_of(c * half, 16), half)
        theirs = pl.ds(pl.multiple_of((1 - c) * half, 16), half)
        chips = _other_chips(x, y)
        local = pltpu.make_async_copy(src, out.at[me], local_sem)
        local.start()
        sends = [pltpu.make_async_remote_copy(src_ref=src.at[mine], dst_ref=out.at[me, mine], send_sem=ici_send.at[j],
                                              recv_sem=ici_recv.at[j], device_id=(px, py, c), device_id_type=MESH)
                 for j, (px, py) in enumerate(chips)]
        for cp in sends:
            cp.start()
        passed = []
        for j, (px, py) in enumerate(chips):
            block = 2 * px + py
            pltpu.make_async_remote_copy(src_ref=src.at[mine], dst_ref=out.at[block, mine], send_sem=ici_send.at[j],
                                         recv_sem=ici_recv.at[j], device_id=(px, py, c), device_id_type=MESH).wait_recv()
            cp = pltpu.make_async_remote_copy(src_ref=out.at[block, mine], dst_ref=out.at[block, mine],
                                              send_sem=d2d_send.at[j], recv_sem=d2d_recv.at[j],
                                              device_id=(x, y, 1 - c), device_id_type=MESH)
            cp.start()
            passed.append(cp)
        for j, (px, py) in enumerate(chips):
            block = 2 * px + py
            pltpu.make_async_remote_copy(src_ref=out.at[block, theirs], dst_ref=out.at[block, theirs],
                                         send_sem=d2d_send.at[j], recv_sem=d2d_recv.at[j],
                                         device_id=(x, y, 1 - c), device_id_type=MESH).wait_recv()
        for cp in sends + passed:
            cp.wait_send()
        local.wait()

    sems = pltpu.SemaphoreType.DMA((3,))
    return pl.pallas_call(
        body, name="ag_weights", in_specs=[_HBM], out_specs=_HBM,
        out_shape=jax.ShapeDtypeStruct((N_CHIPS,) + packed.shape, packed.dtype),
        scratch_shapes=[sems, sems, sems, sems, pltpu.SemaphoreType.DMA(())],
    )(packed)


def _sum4(gp, land, chip, name):
    _, r, w = gp.shape
    tr = 256 if r % 256 == 0 else 128

    def body(chip_ref, o_ref, l_ref, s_ref):
        s_ref[...] = ((o_ref[0] + l_ref[0].astype(F32)) + l_ref[1].astype(F32)) + l_ref[2].astype(F32)

    return pl.pallas_call(
        body, name=name,
        grid_spec=pltpu.PrefetchScalarGridSpec(
            num_scalar_prefetch=1, grid=(r // tr,),
            in_specs=[pl.BlockSpec((1, tr, w), lambda i, chip_ref: (chip_ref[0], i, 0)),
                      pl.BlockSpec((3, tr, w), lambda i, chip_ref: (0, i, 0))],
            out_specs=pl.BlockSpec((tr, w), lambda i, chip_ref: (i, 0))),
        out_shape=jax.ShapeDtypeStruct((r, w), F32),
        compiler_params=_params(("parallel",)),
    )(chip, gp, land)


def _sibling_copy(src, got, send_sem, recv_sem):
    x, y, c = lax.axis_index("x"), lax.axis_index("y"), lax.axis_index("c")
    return pltpu.make_async_remote_copy(src_ref=src, dst_ref=got, send_sem=send_sem, recv_sem=recv_sem,
                                        device_id=(x, y, 1 - c), device_id_type=MESH)


def _swap_sibling(s, name):
    def body(src, got, send_sem, recv_sem):
        cp = _sibling_copy(src, got, send_sem, recv_sem)
        cp.start()
        cp.wait_recv()
        cp.wait_send()

    return pl.pallas_call(
        body, name=name, in_specs=[_HBM], out_specs=_HBM,
        out_shape=jax.ShapeDtypeStruct(s.shape, s.dtype),
        scratch_shapes=[pltpu.SemaphoreType.DMA(()), pltpu.SemaphoreType.DMA(())],
    )(s)


def _all_reduce_small(dsmall, loss):
    n_dev = 8
    names = tuple(_SMALL_SLOTS)
    shape = (8, D_MODEL)

    def body(*refs):
        parts, loss_ref = refs[:len(names)], refs[len(names)]
        out, src, gath, send_sems, recv_sems = refs[len(names) + 1:]
        x, y, c = lax.axis_index("x"), lax.axis_index("y"), lax.axis_index("c")
        me = 4 * x + 2 * y + c
        src[...] = jnp.zeros(shape, F32)
        for name, part in zip(names, parts):
            row, lane, _ = _SMALL_SLOTS[name]
            src[row:row + 1, lane:lane + part.shape[1]] = part[...]
        src[_LOSS_ROW:_LOSS_ROW + 1, 0:LANES] = loss_ref[...]
        gath[me] = src[...]
        peers = []
        for k in range(1, n_dev):
            px = 1 - x if (k >> 2) & 1 else x
            py = 1 - y if (k >> 1) & 1 else y
            pc = 1 - c if k & 1 else c
            peers.append((px, py, pc))
        sends = []
        for j, peer in enumerate(peers):
            cp = pltpu.make_async_remote_copy(src_ref=src, dst_ref=gath.at[me], send_sem=send_sems.at[j],
                                              recv_sem=recv_sems.at[j], device_id=peer, device_id_type=MESH)
            cp.start()
            sends.append(cp)
        for j, (px, py, pc) in enumerate(peers):
            pltpu.make_async_remote_copy(src_ref=src, dst_ref=gath.at[4 * px + 2 * py + pc], send_sem=send_sems.at[j],
                                         recv_sem=recv_sems.at[j], device_id=(px, py, pc), device_id_type=MESH).wait_recv()
        for cp in sends:
            cp.wait_send()
        acc = gath[0]
        for d in range(1, n_dev):
            acc = acc + gath[d]
        out[...] = acc

    vmem = pl.BlockSpec(memory_space=pltpu.VMEM)
    return pl.pallas_call(
        body, name="ar_small", in_specs=[vmem] * (len(names) + 1), out_specs=vmem,
        out_shape=jax.ShapeDtypeStruct(shape, F32),
        scratch_shapes=[pltpu.VMEM(shape, F32), pltpu.VMEM((n_dev,) + shape, F32),
                        pltpu.SemaphoreType.DMA((n_dev - 1,)), pltpu.SemaphoreType.DMA((n_dev - 1,))],
    )(*[dsmall[k] for k in names], loss)


_W_IN_ROWS = SHARD_SHAPES["w_in"][1]


def _shard_rows(name, a):
    return jnp.transpose(a) if name == "w_in" else a.reshape(PACK_ROWS[name], D_MODEL)


def _pack(group, shards, dtype):
    parts = [_shard_rows(n, shards[n]).astype(dtype) for n in group]
    pad = -sum(PACK_ROWS[n] for n in group) % LANES
    if pad:
        parts.append(jnp.zeros((pad, D_MODEL), dtype))
    return jnp.concatenate(parts, axis=0)


def _col_sharded_full(g, name, group):
    r, c = SHARD_SHAPES[name]
    off = _row_offset(group, name)
    blocks = g[:, off:off + PACK_ROWS[name]].reshape(N_CHIPS, r, c)
    return jnp.transpose(blocks, (1, 0, 2)).reshape(r, N_CHIPS * c)


def _col_sharded_blocks(d, name):
    r, c = SHARD_SHAPES[name]
    return jnp.transpose(d.reshape(r, N_CHIPS, c), (1, 0, 2)).reshape(N_CHIPS, PACK_ROWS[name], D_MODEL)


def _weights_a(g):
    dt = g.dtype
    w_in_t = jnp.concatenate([g[c, :_W_IN_ROWS] for c in range(N_CHIPS)]
                             + [jnp.zeros((D_IN_PAD - N_CHIPS * _W_IN_ROWS, D_MODEL), dt)], axis=0)
    wq = _col_sharded_full(g, "w_q_b", GROUP_A).reshape(Q_LORA, N_HEADS_B, Q_HEAD_B)
    wq_p = jnp.concatenate([wq, jnp.zeros((Q_LORA, N_HEADS_B, HEAD_PAD - Q_HEAD_B), dt)], axis=2).reshape(Q_LORA, MLA_W)
    wkv = _col_sharded_full(g, "w_kv_b", GROUP_A).reshape(KV_LORA, N_HEADS_B, QK_NOPE + V_DIM_B)
    zk = jnp.zeros((KV_LORA, N_HEADS_B, HEAD_PAD - QK_NOPE), dt)
    wk_p = jnp.concatenate([wkv[:, :, :QK_NOPE], zk], axis=2).reshape(KV_LORA, MLA_W)
    wv = wkv[:, :, QK_NOPE:].reshape(KV_LORA, N_HEADS_B * V_DIM_B)
    return dict(w_in=w_in_t, wq=wq_p, wk=wk_p, wv=wv, wv_t=jnp.transpose(wv))


def _grad_blocks_a(dw_in_t, dwq_p, dwk_p, dwv):
    dwq = dwq_p.reshape(Q_LORA, N_HEADS_B, HEAD_PAD)[:, :, :Q_HEAD_B].reshape(Q_LORA, N_HEADS_B * Q_HEAD_B)
    dwk = dwk_p.reshape(KV_LORA, N_HEADS_B, HEAD_PAD)[:, :, :QK_NOPE]
    dwkv = jnp.concatenate([dwk, dwv.reshape(KV_LORA, N_HEADS_B, V_DIM_B)], axis=2)
    dwkv = dwkv.reshape(KV_LORA, N_HEADS_B * (QK_NOPE + V_DIM_B))
    pad = -sum(PACK_ROWS[n] for n in GROUP_A) % LANES
    return [dw_in_t.reshape(N_CHIPS, _W_IN_ROWS, D_MODEL), _col_sharded_blocks(dwq, "w_q_b"),
            _col_sharded_blocks(dwkv, "w_kv_b"), jnp.zeros((N_CHIPS, pad, D_MODEL), F32)]


def _rope_freq_lanes():
    freqs = ROPE_THETA ** (-jnp.arange(0, QK_ROPE, 2, dtype=F32) / QK_ROPE)
    return jnp.concatenate([jnp.zeros((QK_NOPE,), F32), freqs, freqs,
                            jnp.zeros((HEAD_PAD - Q_HEAD_B,), F32)]).reshape(1, LANES)


def _fwd_bwd(x, positions, target, w, m, v):
    t = x.shape[0]
    wa = _weights_a(_all_gather_chips(_pack(GROUP_A, w, BF16)))
    posr = positions.astype(F32).reshape(1, t)
    posc = posr.reshape(t, 1)
    freq = _rope_freq_lanes()
    g1, g2, g3, g4 = w["pre_norm_mix"], w["post_norm_mix"], w["pre_norm_mlp"], w["post_norm_mlp"]
    qan, kvan, sinks = w["q_a_norm"], w["kv_a_norm"], w["sinks"]

    h, proj = _proj_fwd(x, g1, wa["w_in"])
    out_a, lse_a = _swa_fwd(proj, posc, posr, sinks)
    qm, km, qt, kt, vt = _mla_prep_fwd(proj, posc, freq, qan, kvan, wa["wq"], wa["wk"], wa["wv_t"])
    out_bt, lse_b, wb = _mla_fwd(km, qt, vt, _pack(GROUP_B, w, BF16))
    w_oa, w_ob = _col_sharded_full(wb, "w_o_a", GROUP_B), _col_sharded_full(wb, "w_o_b", GROUP_B)
    merged, y, x1, h2 = _mix_out_fwd(out_a, out_bt, proj, x, w_oa, w_ob, wb, g2, g3)
    a = _up_fwd(h2, wb)
    dx2, dyd, dg4, loss = _down_fwd_loss(a, wb, x1, target, g4)

    gp_b = _dw_into_blocks(a, dyd, "w_down", 1024, _TK_DW)
    du = _down_bwd(dyd, wb, a)
    gp_b = _dw_into_blocks(h2, du, "w_up", 1024, _TK_DW, gp_b)
    dx1, dy, dg3, dg2 = _up_bwd(du, wb, x1, dx2, y, g3, g2)
    gp_b = _dw_into_blocks(merged, dy, "w_out", 1024, _TK_DW, gp_b)
    doa, dob, dproj, d_out_a, d_out_b, d_out_bt, del_a, del_b = _mix_out_bwd(dy, out_a, out_bt, proj, w_oa, w_ob, wb)
    dw_oa = _matmul_tn(out_a, doa, "dw_o_a", 512, 1024)
    dw_ob = _dw_ob(out_bt, dob)
    small_b = jnp.concatenate([_col_sharded_blocks(dw_oa, "w_o_a"), _col_sharded_blocks(dw_ob, "w_o_b")], axis=1)
    gp_b = lax.dynamic_update_slice(gp_b, small_b, (0, _row_offset(GROUP_B, "w_o_a"), 0))
    dqm, dkm, dvm, land_b = _mla_bwd(qm, km, qt, kt, vt, d_out_b, d_out_bt, lse_b, del_b, gp_b)
    chip = (2 * lax.axis_index("x") + lax.axis_index("y")).astype(jnp.int32).reshape(1)
    part_b = _sum4(gp_b, land_b, chip, "rs_sum_b")
    dcq, dckv, dkr, dwq, dwk, dwv, dqan, dkvan, sib_b = _mla_prep_bwd(
        dqm, dkm, dvm, proj, posc, freq, qan, kvan, wa["wq"], wa["wk"], wa["wv"], part_b)
    dqa, dka, dva, dsinks = _swa_bwd(proj, d_out_a, lse_a, del_a, posc, posr, sinks)
    col = 2 * D_MODEL
    for piece in (dqa, dka, dva, dcq, dckv, dkr):
        dproj = lax.dynamic_update_slice(dproj, piece.astype(BF16), (0, col))
        col += piece.shape[1]
    dw_in_t, updated = _dw_in_adamw(dproj, h, [part_b, sib_b], w, m, v)
    parts_a = _grad_blocks_a(dw_in_t, dwq, dwk, dwv)
    gp_a = jnp.concatenate([p.astype(BF16) for p in parts_a], axis=1)
    grad_x, dg1, land_a = _in_bwd(dproj, wa["w_in"], x, dx1, g1, gp_a)

    own_a = jnp.concatenate([lax.dynamic_slice_in_dim(p, chip[0], 1, axis=0) for p in parts_a], axis=1)
    part_a = _sum4(own_a, land_a, jnp.zeros((1,), jnp.int32), "rs_sum_a")
    reduced = {GROUP_A: [part_a, _swap_sibling(part_a, "rs_swap_a")], GROUP_B: [part_b, sib_b]}
    dsmall = dict(pre_norm_mix=dg1, post_norm_mix=dg2, pre_norm_mlp=dg3, post_norm_mlp=dg4,
                  q_a_norm=dqan, kv_a_norm=dkvan, sinks=dsinks)
    return loss, grad_x, reduced, dsmall, updated


def kernel(x, positions, pre_norm_mix, w_in, q_a_norm, w_q_b, kv_a_norm, w_kv_b, sinks, w_o_a, w_o_b, w_out, post_norm_mix, pre_norm_mlp, w_up, w_down, post_norm_mlp, loss_target, m_pre_norm_mix, m_w_in, m_q_a_norm, m_w_q_b, m_kv_a_norm, m_w_kv_b, m_sinks, m_w_o_a, m_w_o_b, m_w_out, m_post_norm_mix, m_pre_norm_mlp, m_w_up, m_w_down, m_post_norm_mlp, v_pre_norm_mix, v_w_in, v_q_a_norm, v_w_q_b, v_kv_a_norm, v_w_kv_b, v_sinks, v_w_o_a, v_w_o_b, v_w_out, v_post_norm_mix, v_pre_norm_mlp, v_w_up, v_w_down, v_post_norm_mlp):
    w = dict(pre_norm_mix=pre_norm_mix, w_in=w_in[0], q_a_norm=q_a_norm, w_q_b=w_q_b[0], kv_a_norm=kv_a_norm,
             w_kv_b=w_kv_b[0], sinks=sinks, w_o_a=w_o_a[0], w_o_b=w_o_b[0], w_out=w_out[0],
             post_norm_mix=post_norm_mix, pre_norm_mlp=pre_norm_mlp, w_up=w_up[0], w_down=w_down[0],
             post_norm_mlp=post_norm_mlp)
    m = dict(pre_norm_mix=m_pre_norm_mix, w_in=m_w_in[0], q_a_norm=m_q_a_norm, w_q_b=m_w_q_b[0],
             kv_a_norm=m_kv_a_norm, w_kv_b=m_w_kv_b[0], sinks=m_sinks, w_o_a=m_w_o_a[0], w_o_b=m_w_o_b[0],
             w_out=m_w_out[0], post_norm_mix=m_post_norm_mix, pre_norm_mlp=m_pre_norm_mlp, w_up=m_w_up[0],
             w_down=m_w_down[0], post_norm_mlp=m_post_norm_mlp)
    v = dict(pre_norm_mix=v_pre_norm_mix, w_in=v_w_in[0], q_a_norm=v_q_a_norm, w_q_b=v_w_q_b[0],
             kv_a_norm=v_kv_a_norm, w_kv_b=v_w_kv_b[0], sinks=v_sinks, w_o_a=v_w_o_a[0], w_o_b=v_w_o_b[0],
             w_out=v_w_out[0], post_norm_mix=v_post_norm_mix, pre_norm_mlp=v_pre_norm_mlp, w_up=v_w_up[0],
             w_down=v_w_down[0], post_norm_mlp=v_post_norm_mlp)

    loss, grad_x, reduced, dsmall, updated = _fwd_bwd(x[0], positions, loss_target[0], w, m, v)

    red = _all_reduce_small(dsmall, loss)
    small = _adamw_small(red, w, m, v)

    big = {}
    tr = jnp.transpose
    big["w_in"] = [tr(o)[None] for o in _adamw(tr(w["w_in"]), reduced[GROUP_A], tr(m["w_in"]), tr(v["w_in"]),
                                               "adamw_w_in", (_W_IN_ROWS, 256))]
    for n in _ADAMW_RIDERS:
        big[n] = [o[None] for o in updated[n]]
    for group, names in ((GROUP_A, ("w_q_b", "w_kv_b")), (GROUP_B, ("w_o_a", "w_o_b"))):
        for n in names:
            off = _row_offset(group, n)
            g_parts = [p[off:off + PACK_ROWS[n]].reshape(SHARD_SHAPES[n]) for p in reduced[group]]
            big[n] = [o[None] for o in _adamw(w[n], g_parts, m[n], v[n], "adamw_" + n, SHARD_SHAPES[n])]

    outs = [big[n][k] if n in big else small[n][k] for k in range(4) for n in WEIGHTS]
    return (red[_LOSS_ROW, 0], grad_x[None], *outs)
```

```python
import jax
import jax.numpy as jnp
from jax import lax
from jax.experimental import pallas as pl
from jax.experimental.pallas import tpu as pltpu

F32 = jnp.float32
BF16 = jnp.bfloat16
MESH = pl.DeviceIdType.MESH

D_MODEL = 1024
N_HEADS_A = 8
N_KV_A = 2
HEAD_DIM_A = 64
WINDOW = 128
BLOCK = 128
N_HEADS_B = 8
QK_NOPE = 64
QK_ROPE = 32
V_DIM_B = 64
Q_LORA = 256
KV_LORA = 128
ROPE_THETA = 10000.0
D_FF = 4 * D_MODEL
EPS = 1e-6
WIDTH_A = N_HEADS_A * HEAD_DIM_A
Q_HEAD_B = QK_NOPE + QK_ROPE
D_IN_PAD = 3328
HEAD_PAD = 128
MLA_W = N_HEADS_B * HEAD_PAD

ADAM_LR = 0.001
ADAM_B1 = 0.9
ADAM_B2 = 0.999
ADAM_EPS = 1e-08
ADAM_WD = 0.01
ADAM_STEP = 10

NEG = -1e30
N_CHIPS = 4
LANES = 128
VMEM_LIMIT = 56 * 1024 * 1024

SHARD_SHAPES = {"w_in": (1024, 808), "w_q_b": (256, 192), "w_kv_b": (128, 256), "w_o_a": (512, 256),
                "w_o_b": (512, 256), "w_out": (256, 1024), "w_up": (1024, 1024), "w_down": (1024, 1024)}
PACK_ROWS = {n: (s[0] * s[1]) // D_MODEL for n, s in SHARD_SHAPES.items()}
GROUP_A = ("w_in", "w_q_b", "w_kv_b")
GROUP_B = ("w_up", "w_down", "w_out", "w_o_a", "w_o_b")
WEIGHTS = ("pre_norm_mix", "w_in", "q_a_norm", "w_q_b", "kv_a_norm", "w_kv_b", "sinks", "w_o_a", "w_o_b", "w_out",
           "post_norm_mix", "pre_norm_mlp", "w_up", "w_down", "post_norm_mlp")


def _params(sem=None):
    return pltpu.CompilerParams(dimension_semantics=sem, vmem_limit_bytes=VMEM_LIMIT)


def _dot(a, b):
    return jnp.dot(a, b, preferred_element_type=F32)


def _dot_nt(a, b):
    return lax.dot_general(a, b, (((1,), (1,)), ((), ())), preferred_element_type=F32)


def _dot_tn(a, b):
    return lax.dot_general(a, b, (((0,), (0,)), ((), ())), preferred_element_type=F32)


def _rms(v):
    return lax.rsqrt(jnp.mean(v * v, axis=-1, keepdims=True) + EPS)


def _norm_bwd(dout, n, r, g):
    dn = dout * g
    dx = r * (dn - n * jnp.mean(dn * n, axis=-1, keepdims=True))
    return dx, jnp.sum(dout * n, axis=0, keepdims=True)


def _full(shape):
    return pl.BlockSpec(shape, lambda *_: (0,) * len(shape))


def _row_offset(group, name):
    return sum(PACK_ROWS[n] for n in group[:group.index(name)])


def _wb_spec(name):
    rows = PACK_ROWS[name]
    return pl.BlockSpec((N_CHIPS, rows, D_MODEL), lambda *_: (0, _row_offset(GROUP_B, name) // rows, 0))


def _proj_fwd(x, g1, w_in_t):
    t = x.shape[0]
    tm = 512

    def body(x_ref, g_ref, w_ref, h_ref, p_ref):
        for rows in _row_halves(tm):
            xv = x_ref[rows, :]
            h = ((xv * _rms(xv)) * g_ref[...]).astype(BF16)
            h_ref[rows, :] = h
            p_ref[rows, :] = _dot_nt(h, w_ref[...])

    return pl.pallas_call(
        body, name="proj_fwd", grid=(t // tm,),
        in_specs=[pl.BlockSpec((tm, D_MODEL), lambda i: (i, 0)), _full((1, D_MODEL)), _full((D_IN_PAD, D_MODEL))],
        out_specs=[pl.BlockSpec((tm, D_MODEL), lambda i: (i, 0)), pl.BlockSpec((tm, D_IN_PAD), lambda i: (i, 0))],
        out_shape=[jax.ShapeDtypeStruct((t, D_MODEL), BF16), jax.ShapeDtypeStruct((t, D_IN_PAD), F32)],
        compiler_params=_params(("parallel",)),
    )(x, g1, w_in_t)


_QA_BLK = 2048 // WIDTH_A
_KA_BLK = 2560 // LANES
_VA_BLK = 2688 // LANES
_CQ_BLK = 2816 // Q_LORA
_CKV_BLK = 3072 // LANES
_KR_BLK = 3200 // LANES


_GROUP_A = N_HEADS_A // N_KV_A
_SWA_SCALE = HEAD_DIM_A ** -0.5
_LOG2E = 1.4426950408889634


def _head_cols(v, h):
    return v[:, HEAD_DIM_A * h:HEAD_DIM_A * (h + 1)]


def _head_rows(v, h):
    return v[HEAD_DIM_A * h:HEAD_DIM_A * (h + 1), :]


def _swa_scores_t(st_g, j, h, dist, valid):
    slope = 2.0 ** (-8.0 * (h + 1) / N_HEADS_A)
    st = st_g[:, BLOCK * j:BLOCK * (j + 1)] * (_SWA_SCALE * _LOG2E) - (slope * _LOG2E) * dist
    return jnp.where(valid, st, NEG)


def _group_t(xt, kh):
    return jnp.concatenate([_head_rows(xt, _GROUP_A * kh + j) for j in range(_GROUP_A)], axis=1).astype(BF16)


_SWA_PER_STEP = 4


def _swa_fwd(proj, posc, posr, sinks):
    t = proj.shape[0]
    span = _SWA_PER_STEP * BLOCK

    def body(q_ref, kc_ref, kp_ref, vc_ref, vp_ref, pq_ref, pc_ref, pp_ref, sink_ref, o_ref, l_ref):
        n = pl.program_id(0)
        k_all = jnp.concatenate([kp_ref[...], kc_ref[...]], axis=0)
        v_all = jnp.concatenate([vp_ref[...], vc_ref[...]], axis=0)
        pos_all = jnp.concatenate([pp_ref[...], pc_ref[...]], axis=0)
        ki = lax.broadcasted_iota(jnp.int32, (2 * BLOCK, BLOCK), 0)
        qi = lax.broadcasted_iota(jnp.int32, (2 * BLOCK, BLOCK), 1)
        window = (ki > qi) & (ki <= qi + WINDOW)
        for sub in range(_SWA_PER_STEP):
            band = slice(BLOCK * sub, BLOCK * (sub + 2))
            own = slice(BLOCK * sub, BLOCK * (sub + 1))
            kb, vb = k_all[band], v_all[band]
            dist = jnp.abs(pos_all[band] - pq_ref[:, own])
            valid = window & ((n > 0) | (ki >= BLOCK)) if sub == 0 else window
            q_t, vb_t = q_ref[own, :].T, vb.T
            out_t, lse = [], []
            for kh in range(N_KV_A):
                st_g = _dot(_head_cols(kb, kh).astype(BF16), _group_t(q_t, kh))
                ps = []
                for j in range(_GROUP_A):
                    h = _GROUP_A * kh + j
                    st = _swa_scores_t(st_g, j, h, dist, valid)
                    sink = sink_ref[0:1, h:h + 1] * _LOG2E
                    m = jnp.maximum(jnp.max(st, axis=0, keepdims=True), sink)
                    e = jnp.exp2(st - m)
                    den = jnp.sum(e, axis=0, keepdims=True) + jnp.exp2(sink - m)
                    ps.append((e * (1.0 / den)).astype(BF16))
                    lse.append(m + jnp.log(den) * _LOG2E)
                o_g = _dot(_head_rows(vb_t, kh).astype(BF16), jnp.concatenate(ps, axis=1))
                out_t.extend(o_g[:, BLOCK * j:BLOCK * (j + 1)] for j in range(_GROUP_A))
            o_ref[own, :] = jnp.concatenate(out_t, axis=0).T
            l_ref[:, own] = jnp.concatenate(lse, axis=0)

    cur = lambda n: (n, 0)
    prev = lambda n: jnp.maximum(_SWA_PER_STEP * n - 1, 0)
    return pl.pallas_call(
        body, name="swa_fwd", grid=(t // span,),
        in_specs=[pl.BlockSpec((span, WIDTH_A), lambda n: (n, _QA_BLK)),
                  pl.BlockSpec((span, LANES), lambda n: (n, _KA_BLK)),
                  pl.BlockSpec((BLOCK, LANES), lambda n: (prev(n), _KA_BLK)),
                  pl.BlockSpec((span, LANES), lambda n: (n, _VA_BLK)),
                  pl.BlockSpec((BLOCK, LANES), lambda n: (prev(n), _VA_BLK)),
                  pl.BlockSpec((1, span), lambda n: (0, n)),
                  pl.BlockSpec((span, 1), cur),
                  pl.BlockSpec((BLOCK, 1), lambda n: (prev(n), 0)),
                  _full((1, N_HEADS_A))],
        out_specs=[pl.BlockSpec((span, WIDTH_A), cur), pl.BlockSpec((N_HEADS_A, span), lambda n: (0, n))],
        out_shape=[jax.ShapeDtypeStruct((t, WIDTH_A), F32), jax.ShapeDtypeStruct((N_HEADS_A, t), F32)],
        compiler_params=_params(("parallel",)),
    )(proj, proj, proj, proj, proj, posr, posc, posc, sinks)


def _rope_coeffs(pos, freq):
    ang = pos * freq
    cosv, sinv = jnp.cos(ang), jnp.sin(ang)
    lane = lax.broadcasted_iota(jnp.int32, ang.shape, 1)
    lo = (lane >= QK_NOPE) & (lane < QK_NOPE + QK_ROPE // 2)
    hi = (lane >= QK_NOPE + QK_ROPE // 2) & (lane < QK_NOPE + QK_ROPE)
    c = jnp.where(lane < QK_NOPE, 1.0, jnp.where(lo | hi, cosv, 0.0))
    s = jnp.where(lo, -sinv, jnp.where(hi, sinv, 0.0))
    return c, s, lo, hi


def _rope(xh, c, s, lo):
    up = pltpu.roll(xh, LANES - QK_ROPE // 2, axis=1)
    dn = pltpu.roll(xh, QK_ROPE // 2, axis=1)
    return xh * c + jnp.where(lo, up, dn) * s


def _unrope(dh, c, s, lo, hi):
    g = dh * s
    up = pltpu.roll(g, LANES - QK_ROPE // 2, axis=1)
    dn = pltpu.roll(g, QK_ROPE // 2, axis=1)
    return dh * c + jnp.where(hi, dn, jnp.where(lo, up, 0.0))


_TQ = 512
_MLA_SCALE = Q_HEAD_B ** -0.5


def _mla_prep_fwd(proj, posc, freq, qan, kvan, wq, wk, wv):
    t = proj.shape[0]
    tm = _TQ
    nb = t // tm

    def body(cq_ref, ckv_ref, kr_ref, pos_ref, f_ref, qan_ref, kvan_ref, wq_ref, wk_ref, wv_ref,
             q_ref, k_ref, qt_ref, kt_ref, vt_ref):
        cq = cq_ref[...]
        cqn = ((cq * _rms(cq)) * qan_ref[...]).astype(BF16)
        ckv = ckv_ref[...]
        ckvn = ((ckv * _rms(ckv)) * kvan_ref[...]).astype(BF16)
        qb = _dot(cqn, wq_ref[...])
        kb = _dot(ckvn, wk_ref[...])
        vbt = _dot_nt(wv_ref[...], ckvn)
        c, s, lo, _ = _rope_coeffs(pos_ref[...], f_ref[...])
        kr = _rope(pltpu.roll(kr_ref[...], QK_NOPE, axis=1), c, s, lo)
        for h in range(N_HEADS_B):
            sl = slice(HEAD_PAD * h, HEAD_PAD * (h + 1))
            q_h = _rope(qb[:, sl], c, s, lo)
            k_h = kb[:, sl] + kr
            q_ref[:, sl] = q_h.astype(BF16)
            k_ref[:, sl] = k_h.astype(BF16)
            qt_ref[h, 0] = q_h.T.astype(BF16)
            kt_ref[h, 0] = k_h.T.astype(BF16)
            vt_ref[h, 0] = vbt[V_DIM_B * h:V_DIM_B * (h + 1), :].astype(BF16)

    row = lambda i: (i, 0)
    blk4 = lambda d: pl.BlockSpec((N_HEADS_B, 1, d, tm), lambda i: (0, i, 0, 0))
    return pl.pallas_call(
        body, name="mla_prep_fwd", grid=(nb,),
        in_specs=[pl.BlockSpec((tm, Q_LORA), lambda i: (i, _CQ_BLK)),
                  pl.BlockSpec((tm, LANES), lambda i: (i, _CKV_BLK)),
                  pl.BlockSpec((tm, LANES), lambda i: (i, _KR_BLK)),
                  pl.BlockSpec((tm, 1), row), _full((1, LANES)), _full((1, Q_LORA)), _full((1, KV_LORA)),
                  _full((Q_LORA, MLA_W)), _full((KV_LORA, MLA_W)), _full((N_HEADS_B * V_DIM_B, KV_LORA))],
        out_specs=[pl.BlockSpec((tm, MLA_W), row), pl.BlockSpec((tm, MLA_W), row), blk4(HEAD_PAD), blk4(HEAD_PAD),
                   blk4(V_DIM_B)],
        out_shape=[jax.ShapeDtypeStruct((t, MLA_W), BF16), jax.ShapeDtypeStruct((t, MLA_W), BF16),
                   jax.ShapeDtypeStruct((N_HEADS_B, nb, HEAD_PAD, tm), BF16),
                   jax.ShapeDtypeStruct((N_HEADS_B, nb, HEAD_PAD, tm), BF16),
                   jax.ShapeDtypeStruct((N_HEADS_B, nb, V_DIM_B, tm), BF16)],
        compiler_params=_params(("parallel",)),
    )(proj, proj, proj, posc, freq, qan, kvan, wq, wk, wv)


_MLA_SCALE2 = _MLA_SCALE * _LOG2E


def _mla_fwd(k, qt, vt, w_src):
    t = k.shape[0]
    nb = t // _TQ
    groups = nb // 2

    def body(k_ref, qt_ref, vt_ref, w_ref, o_ref, l_ref, wg_ref, raw, send_sems, recv_sems, local_sem):
        g = pl.program_id(1)
        first = (pl.program_id(0) == 0) & (g == 0)
        last = (pl.program_id(0) == N_HEADS_B - 1) & (g == groups - 1)

        @pl.when(first)
        def _():
            _gather_start(w_ref, wg_ref, send_sems, recv_sems, local_sem)

        def keys(kj):
            return k_ref[pl.ds(pl.multiple_of(kj * _TQ, _TQ), _TQ), :]

        def products(kj, slot):
            kv = keys(kj)
            raw[slot, 0] = _dot(kv, qt_ref[0, 0])
            raw[slot, 1] = _dot(kv, qt_ref[0, 1])

        def update(stats, raw_ref, kj, diagonal=False):
            m, l, acc = stats
            scores = raw_ref[...]
            if diagonal:
                key = lax.broadcasted_iota(jnp.int32, scores.shape, 0)
                qry = lax.broadcasted_iota(jnp.int32, scores.shape, 1)
                scores = jnp.where(key <= qry, scores, NEG)
            m_new = jnp.maximum(m, jnp.max(scores, axis=0, keepdims=True) * _MLA_SCALE2)
            alpha = jnp.exp2(m - m_new)
            p = jnp.exp2(scores * _MLA_SCALE2 - m_new).astype(BF16)
            pv = _dot(jnp.concatenate([vt_ref[0, kj], jnp.ones((16, _TQ), BF16)], axis=0), p)
            return m_new, alpha * l + pv[V_DIM_B:V_DIM_B + 8], alpha * acc + pv[:V_DIM_B]

        def trip(i, stats):
            sa, sb = stats
            products(2 * i + 1, 1)
            sa, sb = update(sa, raw.at[0, 0], 2 * i), update(sb, raw.at[0, 1], 2 * i)
            products(2 * i + 2, 0)
            return update(sa, raw.at[1, 0], 2 * i + 1), update(sb, raw.at[1, 1], 2 * i + 1)

        init = (jnp.full((1, _TQ), NEG, F32), jnp.zeros((8, _TQ), F32), jnp.zeros((V_DIM_B, _TQ), F32))
        products(0, 0)
        sa, sb = lax.fori_loop(0, g, trip, (init, init))
        raw[1, 1] = _dot(keys(2 * g + 1), qt_ref[0, 1])
        sa = update(sa, raw.at[0, 0], 2 * g, True)
        sb = update(update(sb, raw.at[0, 1], 2 * g), raw.at[1, 1], 2 * g + 1, True)
        for which, (m, l, acc) in enumerate((sa, sb)):
            o_ref[0, which] = acc / l[0:1]
            l_ref[0, which] = m + jnp.log(l[0:1]) * _LOG2E

        @pl.when(last)
        def _():
            _gather_wait(w_ref, wg_ref, send_sems, recv_sems, local_sem)

    two = lambda d: pl.BlockSpec((1, 2, d, _TQ), lambda h, g: (h, g, 0, 0))
    return pl.pallas_call(
        body, name="mla_fwd", grid=(N_HEADS_B, groups),
        in_specs=[pl.BlockSpec((t, HEAD_PAD), lambda h, g: (0, h)), two(HEAD_PAD),
                  pl.BlockSpec((1, nb, V_DIM_B, _TQ), lambda h, g: (h, 0, 0, 0)), _HBM],
        out_specs=[two(V_DIM_B), two(1), _HBM],
        out_shape=[jax.ShapeDtypeStruct((N_HEADS_B, nb, V_DIM_B, _TQ), F32),
                   jax.ShapeDtypeStruct((N_HEADS_B, nb, 1, _TQ), F32),
                   jax.ShapeDtypeStruct((N_CHIPS,) + w_src.shape, w_src.dtype)],
        scratch_shapes=[pltpu.VMEM((2, 2, _TQ, _TQ), F32),
                        pltpu.SemaphoreType.DMA((3,)), pltpu.SemaphoreType.DMA((3,)), pltpu.SemaphoreType.DMA(())],
        compiler_params=_params(("arbitrary", "arbitrary")),
    )(k, qt, vt, w_src)


def _ot_spec(tm, d):
    per = _TQ // tm
    return pl.BlockSpec((N_HEADS_B, 1, d, tm), lambda i: (0, i // per, 0, i % per))


def _mix_out_fwd(out_a, out_bt, proj, x, w_oa, w_ob, wb, g2, g3):
    t = x.shape[0]
    tm = 512

    def body(oa_ref, obt_ref, ga_ref, gb_ref, x_ref, woa_ref, wob_ref, wout_ref, g2_ref, g3_ref,
             mg_ref, y_ref, x1_ref, h2_ref):
        oa = _dot(oa_ref[...].astype(BF16), woa_ref[...])
        obt = obt_ref[...].reshape(N_HEADS_B * V_DIM_B, tm).astype(BF16)
        ob = _dot_tn(obt, wob_ref[...])
        merged = (jax.nn.sigmoid(ga_ref[...]) * oa + jax.nn.sigmoid(gb_ref[...]) * ob).astype(BF16)
        mg_ref[...] = merged
        y = _dot(merged, wout_ref[...].reshape(D_MODEL, D_MODEL))
        y_ref[...] = y
        x1 = x_ref[...] + (y * _rms(y)) * g2_ref[...]
        x1_ref[...] = x1
        h2_ref[...] = ((x1 * _rms(x1)) * g3_ref[...]).astype(BF16)

    row = lambda i: (i, 0)
    blk = pl.BlockSpec((tm, D_MODEL), row)
    return pl.pallas_call(
        body, name="mix_out_fwd", grid=(t // tm,),
        in_specs=[pl.BlockSpec((tm, WIDTH_A), row), _ot_spec(tm, V_DIM_B), pl.BlockSpec((tm, D_MODEL), lambda i: (i, 0)),
                  pl.BlockSpec((tm, D_MODEL), lambda i: (i, 1)), blk,
                  _full((WIDTH_A, D_MODEL)), _full((N_HEADS_B * V_DIM_B, D_MODEL)), _wb_spec("w_out"),
                  _full((1, D_MODEL)), _full((1, D_MODEL))],
        out_specs=[blk, blk, blk, blk],
        out_shape=[jax.ShapeDtypeStruct((t, D_MODEL), BF16), jax.ShapeDtypeStruct((t, D_MODEL), F32),
                   jax.ShapeDtypeStruct((t, D_MODEL), F32), jax.ShapeDtypeStruct((t, D_MODEL), BF16)],
        compiler_params=_params(("parallel",)),
    )(out_a, out_bt, proj, proj, x, w_oa, w_ob, wb, g2, g3)


_TM_MLP = 512


def _row_halves(tm):
    return slice(0, tm // 2), slice(tm // 2, tm)


def _up_fwd(h2, wb):
    t = h2.shape[0]
    tm = _TM_MLP

    def body(h_ref, w_ref, a_ref):
        hv = h_ref[...]
        for j in range(N_CHIPS):
            u = _dot(hv, w_ref[j])
            a_ref[:, D_MODEL * j:D_MODEL * (j + 1)] = jnp.square(jnp.maximum(u, 0.0)).astype(BF16)

    return pl.pallas_call(
        body, name="up_fwd", grid=(t // tm,),
        in_specs=[pl.BlockSpec((tm, D_MODEL), lambda i: (i, 0)), _wb_spec("w_up")],
        out_specs=pl.BlockSpec((tm, D_FF), lambda i: (i, 0)),
        out_shape=jax.ShapeDtypeStruct((t, D_FF), BF16),
        compiler_params=_params(("parallel",)),
    )(h2, wb)


def _down_fwd_loss(a, wb, x1, target, g4):
    t = a.shape[0]
    tm = _TM_MLP

    def body(a_ref, w_ref, x1_ref, tg_ref, g_ref, dx2_ref, dyd_ref, dg_ref, loss_ref):
        @pl.when(pl.program_id(0) == 0)
        def _():
            dg_ref[...] = jnp.zeros(dg_ref.shape, F32)
            loss_ref[...] = jnp.zeros(loss_ref.shape, F32)

        yd = _dot(a_ref[...], w_ref[...].reshape(D_FF, D_MODEL))
        r = _rms(yd)
        n = yd * r
        diff = (x1_ref[...] + n * g_ref[...]) - tg_ref[...]
        loss_ref[...] += 0.5 * jnp.sum(jnp.mean(diff * diff, axis=-1, keepdims=True), axis=0, keepdims=True)
        dx2 = diff * (1.0 / D_MODEL)
        dx2_ref[...] = dx2
        dyd, dg = _norm_bwd(dx2, n, r, g_ref[...])
        dyd_ref[...] = dyd.astype(BF16)
        dg_ref[...] += dg

    row = lambda i: (i, 0)
    blk = pl.BlockSpec((tm, D_MODEL), row)
    return pl.pallas_call(
        body, name="down_fwd_loss", grid=(t // tm,),
        in_specs=[pl.BlockSpec((tm, D_FF), row), _wb_spec("w_down"), blk, blk, _full((1, D_MODEL))],
        out_specs=[blk, blk, _full((1, D_MODEL)), _full((1, LANES))],
        out_shape=[jax.ShapeDtypeStruct((t, D_MODEL), F32), jax.ShapeDtypeStruct((t, D_MODEL), BF16),
                   jax.ShapeDtypeStruct((1, D_MODEL), F32), jax.ShapeDtypeStruct((1, LANES), F32)],
        compiler_params=_params(("arbitrary",)),
    )(a, wb, x1, target, g4)


def _matmul_tn(a, b, name, tm, tn, tk=1024):
    t, m = a.shape
    n = b.shape[1]
    tk = min(tk, t)
    nk = t // tk

    def body(a_ref, b_ref, o_ref):
        @pl.when(pl.program_id(2) == 0)
        def _():
            o_ref[...] = jnp.zeros(o_ref.shape, F32)

        o_ref[...] += _dot_tn(a_ref[...].astype(BF16), b_ref[...].astype(BF16))

    return pl.pallas_call(
        body, name=name, grid=(m // tm, n // tn, nk),
        in_specs=[pl.BlockSpec((tk, tm), lambda i, j, k: (k, i)), pl.BlockSpec((tk, tn), lambda i, j, k: (k, j))],
        out_specs=pl.BlockSpec((tm, tn), lambda i, j, k: (i, j)),
        out_shape=jax.ShapeDtypeStruct((m, n), F32),
        compiler_params=_params(("parallel", "parallel", "arbitrary")),
    )(a, b)


_TK_DW = 2048


def _dw_into_blocks(a, b, weight, tm, tk, buf=None):
    t, m = a.shape
    n = b.shape[1]
    tk = min(tk, t)
    nk = t // tk
    rows = PACK_ROWS[weight]
    br = min(tm, rows)
    chips = tm // br
    first = _row_offset(GROUP_B, weight) // br
    per_chip = rows // br
    if weight == "w_up":
        out_map = lambda i, j, k: (j, first + i, 0)
    elif chips > 1:
        out_map = lambda i, j, k: (i, first, 0)
    else:
        out_map = lambda i, j, k: (i // per_chip, first + i % per_chip, 0)

    def body(a_ref, b_ref, *rest):
        o_ref = rest[-1]

        @pl.when(pl.program_id(2) == 0)
        def _():
            o_ref[...] = jnp.zeros(o_ref.shape, F32)

        o_ref[...] += _dot_tn(a_ref[...].astype(BF16), b_ref[...].astype(BF16)).reshape(o_ref.shape)

    in_specs = [pl.BlockSpec((tk, tm), lambda i, j, k: (k, i)), pl.BlockSpec((tk, D_MODEL), lambda i, j, k: (k, j))]
    operands = [a, b]
    if buf is not None:
        in_specs.append(pl.BlockSpec(memory_space=pl.ANY))
        operands.append(buf)
    total = sum(PACK_ROWS[w] for w in GROUP_B)
    return pl.pallas_call(
        body, name="dw_" + weight[2:], grid=(m // tm, n // D_MODEL, nk),
        in_specs=in_specs, out_specs=pl.BlockSpec((chips, br, D_MODEL), out_map),
        out_shape=jax.ShapeDtypeStruct((N_CHIPS, total, D_MODEL), F32),
        input_output_aliases={} if buf is None else {2: 0},
        compiler_params=_params(("parallel", "parallel", "arbitrary")),
    )(*operands)


def _down_bwd(dyd, wb, a):
    t = dyd.shape[0]
    tm = _TM_MLP

    def body(d_ref, w_ref, a_ref, du_ref):
        dv = d_ref[...]
        for j in range(N_CHIPS):
            cols = slice(D_MODEL * j, D_MODEL * (j + 1))
            av = a_ref[:, cols].astype(F32)
            relu_u = jnp.where(av > 0.0, av * lax.rsqrt(av), 0.0)
            du_ref[:, cols] = (_dot_nt(dv, w_ref[j]) * (2.0 * relu_u)).astype(BF16)

    row = lambda i: (i, 0)
    return pl.pallas_call(
        body, name="down_bwd", grid=(t // tm,),
        in_specs=[pl.BlockSpec((tm, D_MODEL), row), _wb_spec("w_down"), pl.BlockSpec((tm, D_FF), row)],
        out_specs=pl.BlockSpec((tm, D_FF), row),
        out_shape=jax.ShapeDtypeStruct((t, D_FF), BF16),
        compiler_params=_params(("parallel",)),
    )(dyd, wb, a)


def _up_bwd(du, wb, x1, dx2, y, g3, g2):
    t = du.shape[0]
    tm = _TM_MLP

    def body(du_ref, w_ref, x1_ref, dx2_ref, y_ref, g3_ref, g2_ref, dx1_ref, dy_ref, dg3_ref, dg2_ref):
        @pl.when(pl.program_id(0) == 0)
        def _():
            dg3_ref[...] = jnp.zeros(dg3_ref.shape, F32)
            dg2_ref[...] = jnp.zeros(dg2_ref.shape, F32)

        dh2 = _dot_nt(du_ref[:, 0:D_MODEL], w_ref[0])
        for j in range(1, N_CHIPS):
            dh2 = dh2 + _dot_nt(du_ref[:, D_MODEL * j:D_MODEL * (j + 1)], w_ref[j])
        x1 = x1_ref[...]
        r3 = _rms(x1)
        d3, dg3 = _norm_bwd(dh2, x1 * r3, r3, g3_ref[...])
        dx1 = dx2_ref[...] + d3
        dx1_ref[...] = dx1
        dg3_ref[...] += dg3
        y = y_ref[...]
        r2 = _rms(y)
        dy, dg2 = _norm_bwd(dx1, y * r2, r2, g2_ref[...])
        dy_ref[...] = dy.astype(BF16)
        dg2_ref[...] += dg2

    row = lambda i: (i, 0)
    blk = pl.BlockSpec((tm, D_MODEL), row)
    return pl.pallas_call(
        body, name="up_bwd", grid=(t // tm,),
        in_specs=[pl.BlockSpec((tm, D_FF), row), _wb_spec("w_up"),
                  blk, blk, blk, _full((1, D_MODEL)), _full((1, D_MODEL))],
        out_specs=[blk, blk, _full((1, D_MODEL)), _full((1, D_MODEL))],
        out_shape=[jax.ShapeDtypeStruct((t, D_MODEL), F32), jax.ShapeDtypeStruct((t, D_MODEL), BF16),
                   jax.ShapeDtypeStruct((1, D_MODEL), F32), jax.ShapeDtypeStruct((1, D_MODEL), F32)],
        compiler_params=_params(("arbitrary",)),
    )(du, wb, x1, dx2, y, g3, g2)


def _mix_out_bwd(dy, out_a, out_bt, proj, w_oa, w_ob, wb):
    t = dy.shape[0]
    tm = 256
    nb = t // _TQ

    def body(dy_ref, oa_ref, obt_ref, ga_ref, gb_ref, woa_ref, wob_ref, wout_ref,
             dwoa_ref, dwob_ref, dg_ref, da_ref, db_ref, dbt_ref, dela_ref, delb_ref):
        @pl.when(pl.program_id(0) == 0)
        def _():
            dwoa_ref[...] = jnp.zeros(dwoa_ref.shape, F32)
            dwob_ref[...] = jnp.zeros(dwob_ref.shape, F32)

        dm = _dot_nt(dy_ref[...], wout_ref[...].reshape(D_MODEL, D_MODEL))
        out_a_v = oa_ref[...]
        out_bt_v = obt_ref[...].reshape(N_HEADS_B * V_DIM_B, tm)
        oa = _dot(out_a_v.astype(BF16), woa_ref[...])
        ob = _dot_tn(out_bt_v.astype(BF16), wob_ref[...])
        sa, sb = jax.nn.sigmoid(ga_ref[...]), jax.nn.sigmoid(gb_ref[...])
        doa = (dm * sa).astype(BF16)
        dob = (dm * sb).astype(BF16)
        dwoa_ref[...] += _dot_tn(out_a_v.astype(BF16), doa)
        dwob_ref[...] += _dot(out_bt_v.astype(BF16), dob)
        dg_ref[:, :D_MODEL] = (dm * oa * (sa * (1.0 - sa))).astype(BF16)
        dg_ref[:, D_MODEL:] = (dm * ob * (sb * (1.0 - sb))).astype(BF16)
        d_out_a = _dot_nt(doa, woa_ref[...])
        da_ref[...] = d_out_a
        prod_at = (d_out_a * out_a_v).T
        dela_ref[...] = jnp.concatenate(
            [jnp.sum(_head_rows(prod_at, h), axis=0, keepdims=True) for h in range(N_HEADS_A)], axis=0)
        d_out_b = _dot_nt(dob, wob_ref[...])
        d_out_bt = _dot_nt(wob_ref[...], dob)
        prod_bt = d_out_bt * out_bt_v
        for h in range(N_HEADS_B):
            db_ref[h] = d_out_b[:, V_DIM_B * h:V_DIM_B * (h + 1)].astype(BF16)
            dbt_ref[h, 0] = d_out_bt[V_DIM_B * h:V_DIM_B * (h + 1), :].astype(BF16)
            delb_ref[h, 0] = jnp.sum(prod_bt[V_DIM_B * h:V_DIM_B * (h + 1), :], axis=0, keepdims=True)

    row = lambda i: (i, 0)
    blk = pl.BlockSpec((tm, D_MODEL), row)
    return pl.pallas_call(
        body, name="mix_out_bwd", grid=(t // tm,),
        in_specs=[blk, pl.BlockSpec((tm, WIDTH_A), row), _ot_spec(tm, V_DIM_B),
                  pl.BlockSpec((tm, D_MODEL), lambda i: (i, 0)), pl.BlockSpec((tm, D_MODEL), lambda i: (i, 1)),
                  _full((WIDTH_A, D_MODEL)), _full((N_HEADS_B * V_DIM_B, D_MODEL)), _wb_spec("w_out")],
        out_specs=[_full((WIDTH_A, D_MODEL)), _full((N_HEADS_B * V_DIM_B, D_MODEL)),
                   pl.BlockSpec((tm, 2 * D_MODEL), row), pl.BlockSpec((tm, WIDTH_A), row),
                   pl.BlockSpec((N_HEADS_B, tm, V_DIM_B), lambda i: (0, i, 0)), _ot_spec(tm, V_DIM_B),
                   pl.BlockSpec((N_HEADS_A, tm), lambda i: (0, i)), _ot_spec(tm, 1)],
        out_shape=[jax.ShapeDtypeStruct((WIDTH_A, D_MODEL), F32),
                   jax.ShapeDtypeStruct((N_HEADS_B * V_DIM_B, D_MODEL), F32),
                   jax.ShapeDtypeStruct((t, D_IN_PAD), BF16), jax.ShapeDtypeStruct((t, WIDTH_A), F32),
                   jax.ShapeDtypeStruct((N_HEADS_B, t, V_DIM_B), BF16),
                   jax.ShapeDtypeStruct((N_HEADS_B, nb, V_DIM_B, _TQ), BF16), jax.ShapeDtypeStruct((N_HEADS_A, t), F32),
                   jax.ShapeDtypeStruct((N_HEADS_B, nb, 1, _TQ), F32)],
        compiler_params=_params(("arbitrary",)),
    )(dy, out_a, out_bt, proj, proj, w_oa, w_ob, wb)


def _mla_bwd(q, k, qt, kt, vt, d_out, d_out_t, lse, delta, gp):
    t = q.shape[0]
    nb = t // _TQ

    def body(k_ref, kt_ref, vt_ref, q_ref, qt_ref, do_ref, dot_ref, lrow_ref, drow_ref, gp_ref,
             dq_ref, dkt_ref, dvt_ref, land_ref, l_rep, d_rep, send_sems, recv_sems):
        step = pl.program_id(1)
        kj = nb - 1 - step

        @pl.when((pl.program_id(0) == 0) & (step == 0))
        def _():
            _scatter_start(gp_ref, land_ref, send_sems, recv_sems)

        @pl.when(step == 0)
        def _():
            dq_ref[...] = jnp.zeros(dq_ref.shape, F32)
            for b in range(nb):
                l_rep[_TQ * b:_TQ * (b + 1), :] = jnp.broadcast_to(lrow_ref[0, b], (LANES, _TQ)).T
                d_rep[_TQ * b:_TQ * (b + 1), :] = jnp.broadcast_to(drow_ref[0, b], (LANES, _TQ)).T

        kv, k_t, v_t = k_ref[...], kt_ref[0, 0], vt_ref[0, 0]

        def rows_of(qi):
            return pl.ds(pl.multiple_of(qi * _TQ, _TQ), _TQ)

        def products(qi, diagonal=False):
            s = _dot(q_ref[rows_of(qi), :], k_t) * _MLA_SCALE2
            if diagonal:
                qry = lax.broadcasted_iota(jnp.int32, s.shape, 0)
                key = lax.broadcasted_iota(jnp.int32, s.shape, 1)
                s = jnp.where(key <= qry, s, NEG)
            return s, _dot(do_ref[0, rows_of(qi), :], v_t)

        def update(carry, prods, qi):
            dkt, dvt = carry
            s, dp = prods
            lse, delta = l_rep[rows_of(qi), :], d_rep[rows_of(qi), :]
            ps, dss = [], []
            for c in range(_TQ // LANES):
                strip = slice(LANES * c, LANES * (c + 1))
                p = jnp.exp2(s[:, strip] - lse)
                ps.append(p.astype(BF16))
                dss.append((p * (dp[:, strip] - delta) * _MLA_SCALE).astype(BF16))
            p_b, ds_b = jnp.concatenate(ps, axis=1), jnp.concatenate(dss, axis=1)
            dvt = dvt + _dot(dot_ref[0, qi], p_b)
            dkt = dkt + _dot(qt_ref[0, qi], ds_b)
            dq_ref[rows_of(qi), :] += _dot(ds_b, kv)
            return dkt, dvt

        def pair(i, carry):
            qa = kj + 1 + 2 * i
            pa, pb = products(qa), products(qa + 1)
            return update(update(carry, pa, qa), pb, qa + 1)

        init = (jnp.zeros((HEAD_PAD, _TQ), F32), jnp.zeros((V_DIM_B, _TQ), F32))
        carry = update(init, products(kj, True), kj)
        pairs = (nb - 1 - kj) // 2
        carry = lax.fori_loop(0, pairs, pair, carry)
        dkt, dvt = lax.fori_loop(kj + 1 + 2 * pairs, nb, lambda qi, cr: update(cr, products(qi), qi), carry)
        dkt_ref[0, 0] = dkt
        dvt_ref[0, 0] = dvt

        @pl.when((pl.program_id(0) == N_HEADS_B - 1) & (step == nb - 1))
        def _():
            _scatter_wait(gp_ref, land_ref, send_sems, recv_sems)

    head4 = lambda d: pl.BlockSpec((1, nb, d, _TQ), lambda h, s: (h, 0, 0, 0))
    blk4 = lambda d: pl.BlockSpec((1, 1, d, _TQ), lambda h, s: (h, nb - 1 - s, 0, 0))
    head3 = lambda d: pl.BlockSpec((1, t, d), lambda h, kj: (h, 0, 0))
    per_head = pl.BlockSpec((t, HEAD_PAD), lambda h, kj: (0, h))
    return pl.pallas_call(
        body, name="mla_bwd", grid=(N_HEADS_B, nb),
        in_specs=[pl.BlockSpec((_TQ, HEAD_PAD), lambda h, s: (nb - 1 - s, h)), blk4(HEAD_PAD), blk4(V_DIM_B),
                  per_head, head4(HEAD_PAD), head3(V_DIM_B), head4(V_DIM_B), head4(1), head4(1), _HBM],
        out_specs=[per_head, blk4(HEAD_PAD), blk4(V_DIM_B), _HBM],
        out_shape=[jax.ShapeDtypeStruct((t, MLA_W), F32), jax.ShapeDtypeStruct((N_HEADS_B, nb, HEAD_PAD, _TQ), F32),
                   jax.ShapeDtypeStruct((N_HEADS_B, nb, V_DIM_B, _TQ), F32),
                   jax.ShapeDtypeStruct((3,) + gp.shape[1:], gp.dtype)],
        scratch_shapes=[pltpu.VMEM((t, LANES), F32), pltpu.VMEM((t, LANES), F32),
                        pltpu.SemaphoreType.DMA((3,)), pltpu.SemaphoreType.DMA((3,))],
        compiler_params=_params(("arbitrary", "arbitrary")),
    )(k, kt, vt, q, qt, d_out, d_out_t, lse, delta, gp)


def _mla_prep_bwd(dq, dkt, dvt, proj, posc, freq, qan, kvan, wq, wk, wv, swap_src):
    t = dq.shape[0]
    tm = _TQ

    def body(dq_ref, dkt_ref, dvt_ref, cq_ref, ckv_ref, pos_ref, f_ref, qan_ref, kvan_ref, wq_ref, wk_ref, wv_ref, src_ref,
             dcq_ref, dckv_ref, dkr_ref, dwq_ref, dwk_ref, dwv_ref, dqan_ref, dkvan_ref, got_ref, send_sem, recv_sem):
        swap = _sibling_copy(src_ref, got_ref, send_sem, recv_sem)

        @pl.when(pl.program_id(0) == 0)
        def _():
            swap.start()
            for r in (dwq_ref, dwk_ref, dwv_ref, dqan_ref, dkvan_ref):
                r[...] = jnp.zeros(r.shape, F32)

        cq = cq_ref[...]
        rq = _rms(cq)
        nq_ = cq * rq
        cqn = (nq_ * qan_ref[...]).astype(BF16)
        ckv = ckv_ref[...]
        rkv = _rms(ckv)
        nkv = ckv * rkv
        ckvn = (nkv * kvan_ref[...]).astype(BF16)
        c, s, lo, hi = _rope_coeffs(pos_ref[...], f_ref[...])
        dkr = jnp.zeros((tm, LANES), F32)
        dqb, dkb = [], []
        for h in range(N_HEADS_B):
            dqb.append(_unrope(dq_ref[:, HEAD_PAD * h:HEAD_PAD * (h + 1)], c, s, lo, hi).astype(BF16))
            dk_h = dkt_ref[h, 0].T
            dkr = dkr + dk_h
            dkb.append(dk_h.astype(BF16))
        dqb, dkb = jnp.concatenate(dqb, axis=1), jnp.concatenate(dkb, axis=1)
        dkr = jnp.where(lo | hi, _unrope(dkr, c, s, lo, hi), 0.0)
        dkr_ref[...] = pltpu.roll(dkr, LANES - QK_NOPE, axis=1).astype(BF16)
        dvb = dvt_ref[...].reshape(N_HEADS_B * V_DIM_B, tm).T.astype(BF16)
        dwq_ref[...] += _dot_tn(cqn, dqb)
        dwk_ref[...] += _dot_tn(ckvn, dkb)
        dwv_ref[...] += _dot_tn(ckvn, dvb)
        dcqn = _dot_nt(dqb, wq_ref[...])
        dckvn = _dot_nt(dkb, wk_ref[...]) + _dot_nt(dvb, wv_ref[...])
        dcq, dqan = _norm_bwd(dcqn, nq_, rq, qan_ref[...])
        dckv, dkvan = _norm_bwd(dckvn, nkv, rkv, kvan_ref[...])
        dcq_ref[...] = dcq.astype(BF16)
        dckv_ref[...] = dckv.astype(BF16)
        dqan_ref[...] += dqan
        dkvan_ref[...] += dkvan

        @pl.when(pl.program_id(0) == t // tm - 1)
        def _():
            swap.wait_recv()
            swap.wait_send()

    row = lambda i: (i, 0)
    vw = N_HEADS_B * V_DIM_B
    return pl.pallas_call(
        body, name="mla_prep_bwd", grid=(t // tm,),
        in_specs=[pl.BlockSpec((tm, MLA_W), row), pl.BlockSpec((N_HEADS_B, 1, HEAD_PAD, tm), lambda i: (0, i, 0, 0)),
                  pl.BlockSpec((N_HEADS_B, 1, V_DIM_B, tm), lambda i: (0, i, 0, 0)),
                  pl.BlockSpec((tm, Q_LORA), lambda i: (i, _CQ_BLK)),
                  pl.BlockSpec((tm, LANES), lambda i: (i, _CKV_BLK)),
                  pl.BlockSpec((tm, 1), row), _full((1, LANES)), _full((1, Q_LORA)), _full((1, KV_LORA)),
                  _full((Q_LORA, MLA_W)), _full((KV_LORA, MLA_W)), _full((KV_LORA, vw)), _HBM],
        out_specs=[pl.BlockSpec((tm, Q_LORA), row), pl.BlockSpec((tm, LANES), row), pl.BlockSpec((tm, LANES), row),
                   _full((Q_LORA, MLA_W)), _full((KV_LORA, MLA_W)), _full((KV_LORA, vw)),
                   _full((1, Q_LORA)), _full((1, KV_LORA)), _HBM],
        out_shape=[jax.ShapeDtypeStruct((t, Q_LORA), BF16), jax.ShapeDtypeStruct((t, LANES), BF16),
                   jax.ShapeDtypeStruct((t, LANES), BF16),
                   jax.ShapeDtypeStruct((Q_LORA, MLA_W), F32), jax.ShapeDtypeStruct((KV_LORA, MLA_W), F32),
                   jax.ShapeDtypeStruct((KV_LORA, vw), F32),
                   jax.ShapeDtypeStruct((1, Q_LORA), F32), jax.ShapeDtypeStruct((1, KV_LORA), F32),
                   jax.ShapeDtypeStruct(swap_src.shape, swap_src.dtype)],
        scratch_shapes=[pltpu.SemaphoreType.DMA(()), pltpu.SemaphoreType.DMA(())],
        compiler_params=_params(("arbitrary",)),
    )(dq, dkt, dvt, proj, proj, posc, freq, qan, kvan, wq, wk, wv, swap_src)


def _swa_bwd(proj, d_out, lse, delta, posc, posr, sinks):
    t = proj.shape[0]
    per = _SWA_PER_STEP
    span = per * BLOCK
    steps = t // span

    def body(q_ref, kc_ref, kp_ref, vc_ref, vp_ref, do_ref, l_ref, d_ref, pq_ref, pc_ref, pp_ref, sink_ref,
             dq_ref, dk_ref, dv_ref, ds_ref, dkb_s, dvb_s, dk_keep, dv_keep):
        n = pl.program_id(0)

        @pl.when(n == 0)
        def _():
            ds_ref[...] = jnp.zeros(ds_ref.shape, F32)
            dk_keep[...] = jnp.zeros(dk_keep.shape, F32)
            dv_keep[...] = jnp.zeros(dv_keep.shape, F32)

        @pl.when(n < steps)
        def _():
            k_all = jnp.concatenate([kp_ref[...], kc_ref[...]], axis=0)
            v_all = jnp.concatenate([vp_ref[...], vc_ref[...]], axis=0)
            pos_all = jnp.concatenate([pp_ref[...], pc_ref[...]], axis=0)
            ki = lax.broadcasted_iota(jnp.int32, (2 * BLOCK, BLOCK), 0)
            qi = lax.broadcasted_iota(jnp.int32, (2 * BLOCK, BLOCK), 1)
            window = (ki > qi) & (ki <= qi + WINDOW)
            lane = lax.broadcasted_iota(jnp.int32, (1, LANES), 1)
            dsink = jnp.zeros((1, LANES), F32)
            for sub in range(per):
                band = slice(BLOCK * sub, BLOCK * (sub + 2))
                own = slice(BLOCK * sub, BLOCK * (sub + 1))
                kb, vb = k_all[band], v_all[band]
                dist = jnp.abs(pos_all[band] - pq_ref[:, own])
                valid = window & ((n > 0) | (ki >= BLOCK)) if sub == 0 else window
                qv, dov = q_ref[own, :], do_ref[own, :]
                q_t, do_t, kb_t = qv.T, dov.T, kb.T
                dq_t = []
                for kh in range(N_KV_A):
                    heads = range(_GROUP_A * kh, _GROUP_A * (kh + 1))
                    st_g = _dot(_head_cols(kb, kh).astype(BF16), _group_t(q_t, kh))
                    dpt_g = _dot(_head_cols(vb, kh).astype(BF16), _group_t(do_t, kh))
                    pts, dsts = [], []
                    for j, h in enumerate(heads):
                        st = _swa_scores_t(st_g, j, h, dist, valid)
                        l_h, d_h = l_ref[h:h + 1, own], d_ref[h:h + 1, own]
                        pt = jnp.exp2(st - l_h)
                        p_sink = jnp.exp2(sink_ref[0:1, h:h + 1] * _LOG2E - l_h)
                        dsink = dsink + jnp.where(lane == h, jnp.sum(-p_sink * d_h, axis=1, keepdims=True), 0.0)
                        dst = pt * (dpt_g[:, BLOCK * j:BLOCK * (j + 1)] - d_h) * _SWA_SCALE
                        pts.append(pt.astype(BF16))
                        dsts.append(dst.astype(BF16))
                    pt_g, dst_g = jnp.concatenate(pts, axis=1), jnp.concatenate(dsts, axis=1)
                    q_g = jnp.concatenate([_head_cols(qv, h) for h in heads], axis=0).astype(BF16)
                    do_g = jnp.concatenate([_head_cols(dov, h) for h in heads], axis=0).astype(BF16)
                    dkb_s[sub, :, HEAD_DIM_A * kh:HEAD_DIM_A * (kh + 1)] = _dot(dst_g, q_g)
                    dvb_s[sub, :, HEAD_DIM_A * kh:HEAD_DIM_A * (kh + 1)] = _dot(pt_g, do_g)
                    dq_g = _dot(_head_rows(kb_t, kh).astype(BF16), dst_g)
                    dq_t.extend(dq_g[:, BLOCK * j:BLOCK * (j + 1)] for j in range(_GROUP_A))
                dq_ref[own, :] = jnp.concatenate(dq_t, axis=0).T
            ds_ref[...] += dsink
            for keep, out, parts in ((dk_keep, dk_ref, dkb_s), (dv_keep, dv_ref, dvb_s)):
                out[0:span - BLOCK, :] = keep[0:span - BLOCK, :]
                out[span - BLOCK:span, :] = keep[span - BLOCK:span, :] + parts[0, 0:BLOCK, :]
                for s in range(per - 1):
                    keep[BLOCK * s:BLOCK * (s + 1), :] = parts[s, BLOCK:2 * BLOCK, :] + parts[s + 1, 0:BLOCK, :]
                keep[span - BLOCK:span, :] = parts[per - 1, BLOCK:2 * BLOCK, :]

        @pl.when(n == steps)
        def _():
            dk_ref[...] = dk_keep[...]
            dv_ref[...] = dv_keep[...]

    last = steps - 1
    cur = lambda n: (jnp.minimum(n, last), 0)
    cur_t = lambda n: (0, jnp.minimum(n, last))
    prv = lambda n: jnp.maximum(per * jnp.minimum(n, last) - 1, 0)
    out_prev = lambda n: (jnp.maximum(n - 1, 0), 0)
    return pl.pallas_call(
        body, name="swa_bwd", grid=(steps + 1,),
        in_specs=[pl.BlockSpec((span, WIDTH_A), lambda n: (jnp.minimum(n, last), _QA_BLK)),
                  pl.BlockSpec((span, LANES), lambda n: (jnp.minimum(n, last), _KA_BLK)),
                  pl.BlockSpec((BLOCK, LANES), lambda n: (prv(n), _KA_BLK)),
                  pl.BlockSpec((span, LANES), lambda n: (jnp.minimum(n, last), _VA_BLK)),
                  pl.BlockSpec((BLOCK, LANES), lambda n: (prv(n), _VA_BLK)),
                  pl.BlockSpec((span, WIDTH_A), cur), pl.BlockSpec((N_HEADS_A, span), cur_t),
                  pl.BlockSpec((N_HEADS_A, span), cur_t), pl.BlockSpec((1, span), cur_t),
                  pl.BlockSpec((span, 1), cur), pl.BlockSpec((BLOCK, 1), lambda n: (prv(n), 0)),
                  _full((1, N_HEADS_A))],
        out_specs=[pl.BlockSpec((span, WIDTH_A), cur), pl.BlockSpec((span, LANES), out_prev),
                   pl.BlockSpec((span, LANES), out_prev), _full((1, LANES))],
        out_shape=[jax.ShapeDtypeStruct((t, WIDTH_A), F32), jax.ShapeDtypeStruct((t, LANES), F32),
                   jax.ShapeDtypeStruct((t, LANES), F32), jax.ShapeDtypeStruct((1, LANES), F32)],
        scratch_shapes=[pltpu.VMEM((per, 2 * BLOCK, LANES), F32), pltpu.VMEM((per, 2 * BLOCK, LANES), F32),
                        pltpu.VMEM((span, LANES), F32), pltpu.VMEM((span, LANES), F32)],
        compiler_params=_params(("arbitrary",)),
    )(proj, proj, proj, proj, proj, d_out, lse, delta, posr, posc, posc, sinks)


def _in_bwd(dproj, w_in_t, x, dx1, g1, gp):
    t = x.shape[0]
    tm = 512
    steps = t // tm

    def body(dp_ref, w_ref, x_ref, dx1_ref, g_ref, gp_ref, dx_ref, dg_ref, land_ref, send_sems, recv_sems):
        i = pl.program_id(0)

        @pl.when(i == 0)
        def _():
            dg_ref[...] = jnp.zeros(dg_ref.shape, F32)
            _scatter_start(gp_ref, land_ref, send_sems, recv_sems)

        for rows in _row_halves(tm):
            dh = _dot(dp_ref[rows, :], w_ref[...])
            xv = x_ref[rows, :]
            r = _rms(xv)
            dx, dg = _norm_bwd(dh, xv * r, r, g_ref[...])
            dx_ref[rows, :] = dx1_ref[rows, :] + dx
            dg_ref[...] += dg

        @pl.when(i == steps - 1)
        def _():
            _scatter_wait(gp_ref, land_ref, send_sems, recv_sems)

    row = lambda i: (i, 0)
    blk = pl.BlockSpec((tm, D_MODEL), row)
    return pl.pallas_call(
        body, name="in_bwd", grid=(steps,),
        in_specs=[pl.BlockSpec((tm, D_IN_PAD), row), _full((D_IN_PAD, D_MODEL)), blk, blk, _full((1, D_MODEL)), _HBM],
        out_specs=[blk, _full((1, D_MODEL)), _HBM],
        out_shape=[jax.ShapeDtypeStruct((t, D_MODEL), F32), jax.ShapeDtypeStruct((1, D_MODEL), F32),
                   jax.ShapeDtypeStruct((3,) + gp.shape[1:], gp.dtype)],
        scratch_shapes=[pltpu.SemaphoreType.DMA((3,)), pltpu.SemaphoreType.DMA((3,))],
        compiler_params=_params(("arbitrary",)),
    )(dproj, w_in_t, x, dx1, g1, gp)


def _adamw_store(w, g, m, v, out_refs):
    g_out, d_out, m_out, v_out = out_refs
    m_new = ADAM_B1 * m + (1.0 - ADAM_B1) * g
    v_new = ADAM_B2 * v + (1.0 - ADAM_B2) * jnp.square(g)
    m_hat = m_new / (1.0 - ADAM_B1 ** ADAM_STEP)
    v_hat = v_new / (1.0 - ADAM_B2 ** ADAM_STEP)
    g_out[...] = g
    d_out[...] = -ADAM_LR * (m_hat / (jnp.sqrt(v_hat) + ADAM_EPS) + ADAM_WD * w)
    m_out[...] = m_new
    v_out[...] = v_new


_SMALL_SLOTS = {"pre_norm_mix": (0, 0, D_MODEL), "post_norm_mix": (1, 0, D_MODEL), "pre_norm_mlp": (2, 0, D_MODEL),
                "post_norm_mlp": (3, 0, D_MODEL), "q_a_norm": (4, 0, Q_LORA), "kv_a_norm": (4, Q_LORA, KV_LORA),
                "sinks": (4, Q_LORA + KV_LORA, N_HEADS_A)}
_LOSS_ROW = 5


def _adamw_small(red, w, m, v):
    names = tuple(_SMALL_SLOTS)
    n = len(names)

    def body(*refs):
        red_ref, ws, ms, vs, outs = refs[0], refs[1:1 + n], refs[1 + n:1 + 2 * n], refs[1 + 2 * n:1 + 3 * n], refs[1 + 3 * n:]
        for k, name in enumerate(names):
            row, lane, width = _SMALL_SLOTS[name]
            g = red_ref[row:row + 1, lane:lane + width]
            _adamw_store(ws[k][...], g, ms[k][...], vs[k][...], outs[4 * k:4 * k + 4])

    vmem = pl.BlockSpec(memory_space=pltpu.VMEM)
    res = pl.pallas_call(
        body, name="adamw_small", in_specs=[vmem] * (1 + 3 * n), out_specs=[vmem] * (4 * n),
        out_shape=[jax.ShapeDtypeStruct(w[name].shape, F32) for name in names for _ in range(4)],
    )(red, *[w[k] for k in names], *[m[k] for k in names], *[v[k] for k in names])
    return {name: res[4 * k:4 * k + 4] for k, name in enumerate(names)}


_ADAMW_RIDERS = ("w_up", "w_down", "w_out")


def _dw_in_adamw(dproj, h, g_parts, w, m, v):
    t, cols = dproj.shape
    tm, tk = cols // 2, min(1024, t)
    rows_out = N_CHIPS * SHARD_SHAPES["w_in"][1]
    nk = t // tk
    steps = 2 * nk
    names = _ADAMW_RIDERS
    n = len(names)

    def body(a_ref, b_ref, *rest):
        g1s, g2s, ws, ms, vs = (rest[n * j:n * (j + 1)] for j in range(5))
        o_ref, outs = rest[5 * n], rest[5 * n + 1:]

        @pl.when(pl.program_id(2) == 0)
        def _():
            o_ref[...] = jnp.zeros(o_ref.shape, F32)

        o_ref[...] += _dot_tn(a_ref[...], b_ref[...])
        for j in range(n):
            _adamw_store(ws[j][...], g1s[j][...] + g2s[j][...], ms[j][...], vs[j][...], outs[4 * j:4 * j + 4])

    def rider_spec(name, packed):
        br = SHARD_SHAPES[name][0] // steps
        first = _row_offset(GROUP_B, name) // br if packed else 0
        return pl.BlockSpec((br, D_MODEL), lambda i, j, k: (first + i * nk + k, 0))

    g_specs = [rider_spec(name, True) for name in names]
    own_specs = [rider_spec(name, False) for name in names]
    res = pl.pallas_call(
        body, name="dw_in", grid=(2, 1, nk),
        in_specs=[pl.BlockSpec((tk, tm), lambda i, j, k: (k, i)), pl.BlockSpec((tk, D_MODEL), lambda i, j, k: (k, 0))]
        + g_specs * 2 + own_specs * 3,
        out_specs=[pl.BlockSpec((tm, D_MODEL), lambda i, j, k: (i, 0))] + [s for s in own_specs for _ in range(4)],
        out_shape=[jax.ShapeDtypeStruct((rows_out, D_MODEL), F32)]
        + [jax.ShapeDtypeStruct(SHARD_SHAPES[name], F32) for name in names for _ in range(4)],
        compiler_params=_params(("arbitrary", "arbitrary", "arbitrary")),
    )(dproj, h, *[g_parts[0]] * n, *[g_parts[1]] * n, *[w[k] for k in names], *[m[k] for k in names],
      *[v[k] for k in names])
    return res[0], {name: res[1 + 4 * j:5 + 4 * j] for j, name in enumerate(names)}


def _adamw(w, g_parts, m, v, name, block, g_row_off=0):
    r, c = w.shape
    br, bc = block
    ng = len(g_parts)

    def body(*refs):
        w_ref, g_refs, m_ref, v_ref = refs[0], refs[1:1 + ng], refs[1 + ng], refs[2 + ng]
        g = g_refs[0][...]
        for gr in g_refs[1:]:
            g = g + gr[...]
        _adamw_store(w_ref[...], g, m_ref[...], v_ref[...], refs[3 + ng:])

    assert g_row_off % br == 0 and r % br == 0 and c % bc == 0
    blk = pl.BlockSpec(block, lambda i, j: (i, j))
    g_blk = pl.BlockSpec(block, lambda i, j: (i + g_row_off // br, j))
    return pl.pallas_call(
        body, name=name, grid=(r // br, c // bc),
        in_specs=[blk] + [g_blk] * ng + [blk, blk], out_specs=[blk] * 4,
        out_shape=[jax.ShapeDtypeStruct((r, c), F32)] * 4,
        compiler_params=_params(("parallel", "parallel")),
    )(w, *g_parts, m, v)


_HBM = pl.BlockSpec(memory_space=pltpu.HBM)


def _other_chips(x, y):
    return ((1 - x, y), (x, 1 - y), (1 - x, 1 - y))


def _gather_copies(src, out, send_sems, recv_sems, local_sem):
    x, y, c = lax.axis_index("x"), lax.axis_index("y"), lax.axis_index("c")
    me = 2 * x + y
    local = pltpu.make_async_copy(src, out.at[me], local_sem)

    def copies(arriving):
        return [pltpu.make_async_remote_copy(src_ref=src, dst_ref=out.at[2 * px + py if arriving else me],
                                             send_sem=send_sems.at[j], recv_sem=recv_sems.at[j], device_id=(px, py, c),
                                             device_id_type=MESH)
                for j, (px, py) in enumerate(_other_chips(x, y))]

    return local, copies


def _gather_start(src, out, send_sems, recv_sems, local_sem):
    local, copies = _gather_copies(src, out, send_sems, recv_sems, local_sem)
    local.start()
    for cp in copies(False):
        cp.start()


def _gather_wait(src, out, send_sems, recv_sems, local_sem):
    local, copies = _gather_copies(src, out, send_sems, recv_sems, local_sem)
    for cp in copies(True):
        cp.wait_recv()
    for cp in copies(False):
        cp.wait_send()
    local.wait()


def _scatter_copies(src, land, send_sems, recv_sems):
    x, y, c = lax.axis_index("x"), lax.axis_index("y"), lax.axis_index("c")
    return [pltpu.make_async_remote_copy(src_ref=src.at[2 * px + py], dst_ref=land.at[j], send_sem=send_sems.at[j],
                                         recv_sem=recv_sems.at[j], device_id=(px, py, c), device_id_type=MESH)
            for j, (px, py) in enumerate(_other_chips(x, y))]


def _scatter_start(src, land, send_sems, recv_sems):
    for cp in _scatter_copies(src, land, send_sems, recv_sems):
        cp.start()


def _scatter_wait(src, land, send_sems, recv_sems):
    copies = _scatter_copies(src, land, send_sems, recv_sems)
    for cp in copies:
        cp.wait_recv()
    for cp in copies:
        cp.wait_send()


def _all_gather_chips(packed):
    r = packed.shape[0]
    half = r // 2

    def body(src, out, ici_send, ici_recv, d2d_send, d2d_recv, local_sem):
        x, y, c = lax.axis_index("x"), lax.axis_index("y"), lax.axis_index("c")
        me = 2 * x + y
        mine = pl.ds(pl.multiple_of(c * half, 16), half)
        theirs = pl.ds(pl.multiple_of((1 - c) * half, 16), half)
        chips = _other_chips(x, y)
        local = pltpu.make_async_copy(src, out.at[me], local_sem)
        local.start()
        sends = [pltpu.make_async_remote_copy(src_ref=src.at[mine], dst_ref=out.at[me, mine], send_sem=ici_send.at[j],
                                              recv_sem=ici_recv.at[j], device_id=(px, py, c), device_id_type=MESH)
                 for j, (px, py) in enumerate(chips)]
        for cp in sends:
            cp.start()
        passed = []
        for j, (px, py) in enumerate(chips):
            block = 2 * px + py
            pltpu.make_async_remote_copy(src_ref=src.at[mine], dst_ref=out.at[block, mine], send_sem=ici_send.at[j],
                                         recv_sem=ici_recv.at[j], device_id=(px, py, c), device_id_type=MESH).wait_recv()
            cp = pltpu.make_async_remote_copy(src_ref=out.at[block, mine], dst_ref=out.at[block, mine],
                                              send_sem=d2d_send.at[j], recv_sem=d2d_recv.at[j],
                                              device_id=(x, y, 1 - c), device_id_type=MESH)
            cp.start()
            passed.append(cp)
        for j, (px, py) in enumerate(chips):
            block = 2 * px + py
            pltpu.make_async_remote_copy(src_ref=out.at[block, theirs], dst_ref=out.at[block, theirs],
                                         send_sem=d2d_send.at[j], recv_sem=d2d_recv.at[j],
                                         device_id=(x, y, 1 - c), device_id_type=MESH).wait_recv()
        for cp in sends + passed:
            cp.wait_send()
        local.wait()

    sems = pltpu.SemaphoreType.DMA((3,))
    return pl.pallas_call(
        body, name="ag_weights", in_specs=[_HBM], out_specs=_HBM,
        out_shape=jax.ShapeDtypeStruct((N_CHIPS,) + packed.shape, packed.dtype),
        scratch_shapes=[sems, sems, sems, sems, pltpu.SemaphoreType.DMA(())],
    )(packed)


def _sum4(gp, land, chip, name):
    _, r, w = gp.shape
    tr = 256 if r % 256 == 0 else 128

    def body(chip_ref, o_ref, l_ref, s_ref):
        s_ref[...] = ((o_ref[0] + l_ref[0].astype(F32)) + l_ref[1].astype(F32)) + l_ref[2].astype(F32)

    return pl.pallas_call(
        body, name=name,
        grid_spec=pltpu.PrefetchScalarGridSpec(
            num_scalar_prefetch=1, grid=(r // tr,),
            in_specs=[pl.BlockSpec((1, tr, w), lambda i, chip_ref: (chip_ref[0], i, 0)),
                      pl.BlockSpec((3, tr, w), lambda i, chip_ref: (0, i, 0))],
            out_specs=pl.BlockSpec((tr, w), lambda i, chip_ref: (i, 0))),
        out_shape=jax.ShapeDtypeStruct((r, w), F32),
        compiler_params=_params(("parallel",)),
    )(chip, gp, land)


def _sibling_copy(src, got, send_sem, recv_sem):
    x, y, c = lax.axis_index("x"), lax.axis_index("y"), lax.axis_index("c")
    return pltpu.make_async_remote_copy(src_ref=src, dst_ref=got, send_sem=send_sem, recv_sem=recv_sem,
                                        device_id=(x, y, 1 - c), device_id_type=MESH)


def _swap_sibling(s, name):
    def body(src, got, send_sem, recv_sem):
        cp = _sibling_copy(src, got, send_sem, recv_sem)
        cp.start()
        cp.wait_recv()
        cp.wait_send()

    return pl.pallas_call(
        body, name=name, in_specs=[_HBM], out_specs=_HBM,
        out_shape=jax.ShapeDtypeStruct(s.shape, s.dtype),
        scratch_shapes=[pltpu.SemaphoreType.DMA(()), pltpu.SemaphoreType.DMA(())],
    )(s)


def _all_reduce_small(dsmall, loss):
    n_dev = 8
    names = tuple(_SMALL_SLOTS)
    shape = (8, D_MODEL)

    def body(*refs):
        parts, loss_ref = refs[:len(names)], refs[len(names)]
        out, src, gath, send_sems, recv_sems = refs[len(names) + 1:]
        x, y, c = lax.axis_index("x"), lax.axis_index("y"), lax.axis_index("c")
        me = 4 * x + 2 * y + c
        src[...] = jnp.zeros(shape, F32)
        for name, part in zip(names, parts):
            row, lane, _ = _SMALL_SLOTS[name]
            src[row:row + 1, lane:lane + part.shape[1]] = part[...]
        src[_LOSS_ROW:_LOSS_ROW + 1, 0:LANES] = loss_ref[...]
        gath[me] = src[...]
        peers = []
        for k in range(1, n_dev):
            px = 1 - x if (k >> 2) & 1 else x
            py = 1 - y if (k >> 1) & 1 else y
            pc = 1 - c if k & 1 else c
            peers.append((px, py, pc))
        sends = []
        for j, peer in enumerate(peers):
            cp = pltpu.make_async_remote_copy(src_ref=src, dst_ref=gath.at[me], send_sem=send_sems.at[j],
                                              recv_sem=recv_sems.at[j], device_id=peer, device_id_type=MESH)
            cp.start()
            sends.append(cp)
        for j, (px, py, pc) in enumerate(peers):
            pltpu.make_async_remote_copy(src_ref=src, dst_ref=gath.at[4 * px + 2 * py + pc], send_sem=send_sems.at[j],
                                         recv_sem=recv_sems.at[j], device_id=(px, py, pc), device_id_type=MESH).wait_recv()
        for cp in sends:
            cp.wait_send()
        acc = gath[0]
        for d in range(1, n_dev):
            acc = acc + gath[d]
        out[...] = acc

    vmem = pl.BlockSpec(memory_space=pltpu.VMEM)
    return pl.pallas_call(
        body, name="ar_small", in_specs=[vmem] * (len(names) + 1), out_specs=vmem,
        out_shape=jax.ShapeDtypeStruct(shape, F32),
        scratch_shapes=[pltpu.VMEM(shape, F32), pltpu.VMEM((n_dev,) + shape, F32),
                        pltpu.SemaphoreType.DMA((n_dev - 1,)), pltpu.SemaphoreType.DMA((n_dev - 1,))],
    )(*[dsmall[k] for k in names], loss)


_W_IN_ROWS = SHARD_SHAPES["w_in"][1]


def _shard_rows(name, a):
    return jnp.transpose(a) if name == "w_in" else a.reshape(PACK_ROWS[name], D_MODEL)


def _pack(group, shards, dtype):
    parts = [_shard_rows(n, shards[n]).astype(dtype) for n in group]
    pad = -sum(PACK_ROWS[n] for n in group) % LANES
    if pad:
        parts.append(jnp.zeros((pad, D_MODEL), dtype))
    return jnp.concatenate(parts, axis=0)


def _col_sharded_full(g, name, group):
    r, c = SHARD_SHAPES[name]
    off = _row_offset(group, name)
    blocks = g[:, off:off + PACK_ROWS[name]].reshape(N_CHIPS, r, c)
    return jnp.transpose(blocks, (1, 0, 2)).reshape(r, N_CHIPS * c)


def _col_sharded_blocks(d, name):
    r, c = SHARD_SHAPES[name]
    return jnp.transpose(d.reshape(r, N_CHIPS, c), (1, 0, 2)).reshape(N_CHIPS, PACK_ROWS[name], D_MODEL)


def _weights_a(g):
    dt = g.dtype
    w_in_t = jnp.concatenate([g[c, :_W_IN_ROWS] for c in range(N_CHIPS)]
                             + [jnp.zeros((D_IN_PAD - N_CHIPS * _W_IN_ROWS, D_MODEL), dt)], axis=0)
    wq = _col_sharded_full(g, "w_q_b", GROUP_A).reshape(Q_LORA, N_HEADS_B, Q_HEAD_B)
    wq_p = jnp.concatenate([wq, jnp.zeros((Q_LORA, N_HEADS_B, HEAD_PAD - Q_HEAD_B), dt)], axis=2).reshape(Q_LORA, MLA_W)
    wkv = _col_sharded_full(g, "w_kv_b", GROUP_A).reshape(KV_LORA, N_HEADS_B, QK_NOPE + V_DIM_B)
    zk = jnp.zeros((KV_LORA, N_HEADS_B, HEAD_PAD - QK_NOPE), dt)
    wk_p = jnp.concatenate([wkv[:, :, :QK_NOPE], zk], axis=2).reshape(KV_LORA, MLA_W)
    wv = wkv[:, :, QK_NOPE:].reshape(KV_LORA, N_HEADS_B * V_DIM_B)
    return dict(w_in=w_in_t, wq=wq_p, wk=wk_p, wv=wv, wv_t=jnp.transpose(wv))


def _grad_blocks_a(dw_in_t, dwq_p, dwk_p, dwv):
    dwq = dwq_p.reshape(Q_LORA, N_HEADS_B, HEAD_PAD)[:, :, :Q_HEAD_B].reshape(Q_LORA, N_HEADS_B * Q_HEAD_B)
    dwk = dwk_p.reshape(KV_LORA, N_HEADS_B, HEAD_PAD)[:, :, :QK_NOPE]
    dwkv = jnp.concatenate([dwk, dwv.reshape(KV_LORA, N_HEADS_B, V_DIM_B)], axis=2)
    dwkv = dwkv.reshape(KV_LORA, N_HEADS_B * (QK_NOPE + V_DIM_B))
    pad = -sum(PACK_ROWS[n] for n in GROUP_A) % LANES
    return [dw_in_t.reshape(N_CHIPS, _W_IN_ROWS, D_MODEL), _col_sharded_blocks(dwq, "w_q_b"),
            _col_sharded_blocks(dwkv, "w_kv_b"), jnp.zeros((N_CHIPS, pad, D_MODEL), F32)]


def _rope_freq_lanes():
    freqs = ROPE_THETA ** (-jnp.arange(0, QK_ROPE, 2, dtype=F32) / QK_ROPE)
    return jnp.concatenate([jnp.zeros((QK_NOPE,), F32), freqs, freqs,
                            jnp.zeros((HEAD_PAD - Q_HEAD_B,), F32)]).reshape(1, LANES)


def _fwd_bwd(x, positions, target, w, m, v):
    t = x.shape[0]
    wa = _weights_a(_all_gather_chips(_pack(GROUP_A, w, BF16)))
    posr = positions.astype(F32).reshape(1, t)
    posc = posr.reshape(t, 1)
    freq = _rope_freq_lanes()
    g1, g2, g3, g4 = w["pre_norm_mix"], w["post_norm_mix"], w["pre_norm_mlp"], w["post_norm_mlp"]
    qan, kvan, sinks = w["q_a_norm"], w["kv_a_norm"], w["sinks"]

    h, proj = _proj_fwd(x, g1, wa["w_in"])
    out_a, lse_a = _swa_fwd(proj, posc, posr, sinks)
    qm, km, qt, kt, vt = _mla_prep_fwd(proj, posc, freq, qan, kvan, wa["wq"], wa["wk"], wa["wv_t"])
    out_bt, lse_b, wb = _mla_fwd(km, qt, vt, _pack(GROUP_B, w, BF16))
    w_oa, w_ob = _col_sharded_full(wb, "w_o_a", GROUP_B), _col_sharded_full(wb, "w_o_b", GROUP_B)
    merged, y, x1, h2 = _mix_out_fwd(out_a, out_bt, proj, x, w_oa, w_ob, wb, g2, g3)
    a = _up_fwd(h2, wb)
    dx2, dyd, dg4, loss = _down_fwd_loss(a, wb, x1, target, g4)

    gp_b = _dw_into_blocks(a, dyd, "w_down", 1024, _TK_DW)
    du = _down_bwd(dyd, wb, a)
    gp_b = _dw_into_blocks(h2, du, "w_up", 1024, _TK_DW, gp_b)
    dx1, dy, dg3, dg2 = _up_bwd(du, wb, x1, dx2, y, g3, g2)
    gp_b = _dw_into_blocks(merged, dy, "w_out", 1024, _TK_DW, gp_b)
    dw_oa, dw_ob, dproj, d_out_a, d_out_b, d_out_bt, del_a, del_b = _mix_out_bwd(dy, out_a, out_bt, proj, w_oa, w_ob, wb)
    small_b = jnp.concatenate([_col_sharded_blocks(dw_oa, "w_o_a"), _col_sharded_blocks(dw_ob, "w_o_b")], axis=1)
    gp_b = lax.dynamic_update_slice(gp_b, small_b, (0, _row_offset(GROUP_B, "w_o_a"), 0))
    dqm, dkm, dvm, land_b = _mla_bwd(qm, km, qt, kt, vt, d_out_b, d_out_bt, lse_b, del_b, gp_b)
    chip = (2 * lax.axis_index("x") + lax.axis_index("y")).astype(jnp.int32).reshape(1)
    part_b = _sum4(gp_b, land_b, chip, "rs_sum_b")
    dcq, dckv, dkr, dwq, dwk, dwv, dqan, dkvan, sib_b = _mla_prep_bwd(
        dqm, dkm, dvm, proj, posc, freq, qan, kvan, wa["wq"], wa["wk"], wa["wv"], part_b)
    dqa, dka, dva, dsinks = _swa_bwd(proj, d_out_a, lse_a, del_a, posc, posr, sinks)
    col = 2 * D_MODEL
    for piece in (dqa, dka, dva, dcq, dckv, dkr):
        dproj = lax.dynamic_update_slice(dproj, piece.astype(BF16), (0, col))
        col += piece.shape[1]
    dw_in_t, updated = _dw_in_adamw(dproj, h, [part_b, sib_b], w, m, v)
    parts_a = _grad_blocks_a(dw_in_t, dwq, dwk, dwv)
    gp_a = jnp.concatenate([p.astype(BF16) for p in parts_a], axis=1)
    grad_x, dg1, land_a = _in_bwd(dproj, wa["w_in"], x, dx1, g1, gp_a)

    own_a = jnp.concatenate([lax.dynamic_slice_in_dim(p, chip[0], 1, axis=0) for p in parts_a], axis=1)
    part_a = _sum4(own_a, land_a, jnp.zeros((1,), jnp.int32), "rs_sum_a")
    reduced = {GROUP_A: [part_a, _swap_sibling(part_a, "rs_swap_a")], GROUP_B: [part_b, sib_b]}
    dsmall = dict(pre_norm_mix=dg1, post_norm_mix=dg2, pre_norm_mlp=dg3, post_norm_mlp=dg4,
                  q_a_norm=dqan, kv_a_norm=dkvan, sinks=dsinks)
    return loss, grad_x, reduced, dsmall, updated


def kernel(x, positions, pre_norm_mix, w_in, q_a_norm, w_q_b, kv_a_norm, w_kv_b, sinks, w_o_a, w_o_b, w_out, post_norm_mix, pre_norm_mlp, w_up, w_down, post_norm_mlp, loss_target, m_pre_norm_mix, m_w_in, m_q_a_norm, m_w_q_b, m_kv_a_norm, m_w_kv_b, m_sinks, m_w_o_a, m_w_o_b, m_w_out, m_post_norm_mix, m_pre_norm_mlp, m_w_up, m_w_down, m_post_norm_mlp, v_pre_norm_mix, v_w_in, v_q_a_norm, v_w_q_b, v_kv_a_norm, v_w_kv_b, v_sinks, v_w_o_a, v_w_o_b, v_w_out, v_post_norm_mix, v_pre_norm_mlp, v_w_up, v_w_down, v_post_norm_mlp):
    w = dict(pre_norm_mix=pre_norm_mix, w_in=w_in[0], q_a_norm=q_a_norm, w_q_b=w_q_b[0], kv_a_norm=kv_a_norm,
             w_kv_b=w_kv_b[0], sinks=sinks, w_o_a=w_o_a[0], w_o_b=w_o_b[0], w_out=w_out[0],
             post_norm_mix=post_norm_mix, pre_norm_mlp=pre_norm_mlp, w_up=w_up[0], w_down=w_down[0],
             post_norm_mlp=post_norm_mlp)
    m = dict(pre_norm_mix=m_pre_norm_mix, w_in=m_w_in[0], q_a_norm=m_q_a_norm, w_q_b=m_w_q_b[0],
             kv_a_norm=m_kv_a_norm, w_kv_b=m_w_kv_b[0], sinks=m_sinks, w_o_a=m_w_o_a[0], w_o_b=m_w_o_b[0],
             w_out=m_w_out[0], post_norm_mix=m_post_norm_mix, pre_norm_mlp=m_pre_norm_mlp, w_up=m_w_up[0],
             w_down=m_w_down[0], post_norm_mlp=m_post_norm_mlp)
    v = dict(pre_norm_mix=v_pre_norm_mix, w_in=v_w_in[0], q_a_norm=v_q_a_norm, w_q_b=v_w_q_b[0],
             kv_a_norm=v_kv_a_norm, w_kv_b=v_w_kv_b[0], sinks=v_sinks, w_o_a=v_w_o_a[0], w_o_b=v_w_o_b[0],
             w_out=v_w_out[0], post_norm_mix=v_post_norm_mix, pre_norm_mlp=v_pre_norm_mlp, w_up=v_w_up[0],
             w_down=v_w_down[0], post_norm_mlp=v_post_norm_mlp)

    loss, grad_x, reduced, dsmall, updated = _fwd_bwd(x[0], positions, loss_target[0], w, m, v)

    red = _all_reduce_small(dsmall, loss)
    small = _adamw_small(red, w, m, v)

    big = {}
    tr = jnp.transpose
    big["w_in"] = [tr(o)[None] for o in _adamw(tr(w["w_in"]), reduced[GROUP_A], tr(m["w_in"]), tr(v["w_in"]),
                                               "adamw_w_in", (_W_IN_ROWS, 256))]
    for n in _ADAMW_RIDERS:
        big[n] = [o[None] for o in updated[n]]
    for group, names in ((GROUP_A, ("w_q_b", "w_kv_b")), (GROUP_B, ("w_o_a", "w_o_b"))):
        for n in names:
            off = _row_offset(group, n)
            g_parts = [p[off:off + PACK_ROWS[n]].reshape(SHARD_SHAPES[n]) for p in reduced[group]]
            big[n] = [o[None] for o in _adamw(w[n], g_parts, m[n], v[n], "adamw_" + n, SHARD_SHAPES[n])]

    outs = [big[n][k] if n in big else small[n][k] for k in range(4) for n in WEIGHTS]
    return (red[_LOSS_ROW, 0], grad_x[None], *outs)
```

```python
import jax
import jax.numpy as jnp
from jax import lax
from jax.experimental import pallas as pl
from jax.experimental.pallas import tpu as pltpu

F32 = jnp.float32
BF16 = jnp.bfloat16
MESH = pl.DeviceIdType.MESH

D_MODEL = 1024
N_HEADS_A = 8
N_KV_A = 2
HEAD_DIM_A = 64
WINDOW = 128
BLOCK = 128
N_HEADS_B = 8
QK_NOPE = 64
QK_ROPE = 32
V_DIM_B = 64
Q_LORA = 256
KV_LORA = 128
ROPE_THETA = 10000.0
D_FF = 4 * D_MODEL
EPS = 1e-6
WIDTH_A = N_HEADS_A * HEAD_DIM_A
Q_HEAD_B = QK_NOPE + QK_ROPE
D_IN_PAD = 3328
HEAD_PAD = 128
MLA_W = N_HEADS_B * HEAD_PAD

ADAM_LR = 0.001
ADAM_B1 = 0.9
ADAM_B2 = 0.999
ADAM_EPS = 1e-08
ADAM_WD = 0.01
ADAM_STEP = 10

NEG = -1e30
N_CHIPS = 4
LANES = 128
VMEM_LIMIT = 56 * 1024 * 1024

SHARD_SHAPES = {"w_in": (1024, 808), "w_q_b": (256, 192), "w_kv_b": (128, 256), "w_o_a": (512, 256),
                "w_o_b": (512, 256), "w_out": (256, 1024), "w_up": (1024, 1024), "w_down": (1024, 1024)}
PACK_ROWS = {n: (s[0] * s[1]) // D_MODEL for n, s in SHARD_SHAPES.items()}
GROUP_A = ("w_in", "w_q_b", "w_kv_b")
GROUP_B = ("w_up", "w_down", "w_out", "w_o_a", "w_o_b")
WEIGHTS = ("pre_norm_mix", "w_in", "q_a_norm", "w_q_b", "kv_a_norm", "w_kv_b", "sinks", "w_o_a", "w_o_b", "w_out",
           "post_norm_mix", "pre_norm_mlp", "w_up", "w_down", "post_norm_mlp")


def _params(sem=None):
    return pltpu.CompilerParams(dimension_semantics=sem, vmem_limit_bytes=VMEM_LIMIT)


def _dot(a, b):
    return jnp.dot(a, b, preferred_element_type=F32)


def _dot_nt(a, b):
    return lax.dot_general(a, b, (((1,), (1,)), ((), ())), preferred_element_type=F32)


def _dot_tn(a, b):
    return lax.dot_general(a, b, (((0,), (0,)), ((), ())), preferred_element_type=F32)


def _rms(v):
    return lax.rsqrt(jnp.mean(v * v, axis=-1, keepdims=True) + EPS)


def _norm_bwd(dout, n, r, g):
    dn = dout * g
    dx = r * (dn - n * jnp.mean(dn * n, axis=-1, keepdims=True))
    return dx, jnp.sum(dout * n, axis=0, keepdims=True)


def _full(shape):
    return pl.BlockSpec(shape, lambda *_: (0,) * len(shape))


def _row_offset(group, name):
    return sum(PACK_ROWS[n] for n in group[:group.index(name)])


def _wb_spec(name):
    rows = PACK_ROWS[name]
    return pl.BlockSpec((N_CHIPS, rows, D_MODEL), lambda *_: (0, _row_offset(GROUP_B, name) // rows, 0))


def _proj_fwd(x, g1, w_in_t):
    t = x.shape[0]
    tm = 512

    def body(x_ref, g_ref, w_ref, h_ref, p_ref):
        for rows in _row_halves(tm):
            xv = x_ref[rows, :]
            h = ((xv * _rms(xv)) * g_ref[...]).astype(BF16)
            h_ref[rows, :] = h
            p_ref[rows, :] = _dot_nt(h, w_ref[...])

    return pl.pallas_call(
        body, name="proj_fwd", grid=(t // tm,),
        in_specs=[pl.BlockSpec((tm, D_MODEL), lambda i: (i, 0)), _full((1, D_MODEL)), _full((D_IN_PAD, D_MODEL))],
        out_specs=[pl.BlockSpec((tm, D_MODEL), lambda i: (i, 0)), pl.BlockSpec((tm, D_IN_PAD), lambda i: (i, 0))],
        out_shape=[jax.ShapeDtypeStruct((t, D_MODEL), BF16), jax.ShapeDtypeStruct((t, D_IN_PAD), F32)],
        compiler_params=_params(("parallel",)),
    )(x, g1, w_in_t)


_QA_BLK = 2048 // WIDTH_A
_KA_BLK = 2560 // LANES
_VA_BLK = 2688 // LANES
_CQ_BLK = 2816 // Q_LORA
_CKV_BLK = 3072 // LANES
_KR_BLK = 3200 // LANES


_GROUP_A = N_HEADS_A // N_KV_A
_SWA_SCALE = HEAD_DIM_A ** -0.5
_LOG2E = 1.4426950408889634


def _head_cols(v, h):
    return v[:, HEAD_DIM_A * h:HEAD_DIM_A * (h + 1)]


def _head_rows(v, h):
    return v[HEAD_DIM_A * h:HEAD_DIM_A * (h + 1), :]


def _swa_scores_t(st_g, j, h, dist, valid):
    slope = 2.0 ** (-8.0 * (h + 1) / N_HEADS_A)
    st = st_g[:, BLOCK * j:BLOCK * (j + 1)] * (_SWA_SCALE * _LOG2E) - (slope * _LOG2E) * dist
    return jnp.where(valid, st, NEG)


def _group_t(xt, kh):
    return jnp.concatenate([_head_rows(xt, _GROUP_A * kh + j) for j in range(_GROUP_A)], axis=1).astype(BF16)


_SWA_PER_STEP = 4


def _swa_fwd(proj, posc, posr, sinks):
    t = proj.shape[0]
    span = _SWA_PER_STEP * BLOCK

    def body(q_ref, kc_ref, kp_ref, vc_ref, vp_ref, pq_ref, pc_ref, pp_ref, sink_ref, o_ref, l_ref):
        n = pl.program_id(0)
        k_all = jnp.concatenate([kp_ref[...], kc_ref[...]], axis=0)
        v_all = jnp.concatenate([vp_ref[...], vc_ref[...]], axis=0)
        pos_all = jnp.concatenate([pp_ref[...], pc_ref[...]], axis=0)
        ki = lax.broadcasted_iota(jnp.int32, (2 * BLOCK, BLOCK), 0)
        qi = lax.broadcasted_iota(jnp.int32, (2 * BLOCK, BLOCK), 1)
        window = (ki > qi) & (ki <= qi + WINDOW)
        for sub in range(_SWA_PER_STEP):
            band = slice(BLOCK * sub, BLOCK * (sub + 2))
            own = slice(BLOCK * sub, BLOCK * (sub + 1))
            kb, vb = k_all[band], v_all[band]
            dist = jnp.abs(pos_all[band] - pq_ref[:, own])
            valid = window & ((n > 0) | (ki >= BLOCK)) if sub == 0 else window
            q_t, vb_t = q_ref[own, :].T, vb.T
            out_t, lse = [], []
            for kh in range(N_KV_A):
                st_g = _dot(_head_cols(kb, kh).astype(BF16), _group_t(q_t, kh))
                ps = []
                for j in range(_GROUP_A):
                    h = _GROUP_A * kh + j
                    st = _swa_scores_t(st_g, j, h, dist, valid)
                    sink = sink_ref[0:1, h:h + 1] * _LOG2E
                    m = jnp.maximum(jnp.max(st, axis=0, keepdims=True), sink)
                    e = jnp.exp2(st - m)
                    den = jnp.sum(e, axis=0, keepdims=True) + jnp.exp2(sink - m)
                    ps.append((e * (1.0 / den)).astype(BF16))
                    lse.append(m + jnp.log(den) * _LOG2E)
                o_g = _dot(_head_rows(vb_t, kh).astype(BF16), jnp.concatenate(ps, axis=1))
                out_t.extend(o_g[:, BLOCK * j:BLOCK * (j + 1)] for j in range(_GROUP_A))
            o_ref[own, :] = jnp.concatenate(out_t, axis=0).T
            l_ref[:, own] = jnp.concatenate(lse, axis=0)

    cur = lambda n: (n, 0)
    prev = lambda n: jnp.maximum(_SWA_PER_STEP * n - 1, 0)
    return pl.pallas_call(
        body, name="swa_fwd", grid=(t // span,),
        in_specs=[pl.BlockSpec((span, WIDTH_A), lambda n: (n, _QA_BLK)),
                  pl.BlockSpec((span, LANES), lambda n: (n, _KA_BLK)),
                  pl.BlockSpec((BLOCK, LANES), lambda n: (prev(n), _KA_BLK)),
                  pl.BlockSpec((span, LANES), lambda n: (n, _VA_BLK)),
                  pl.BlockSpec((BLOCK, LANES), lambda n: (prev(n), _VA_BLK)),
                  pl.BlockSpec((1, span), lambda n: (0, n)),
                  pl.BlockSpec((span, 1), cur),
                  pl.BlockSpec((BLOCK, 1), lambda n: (prev(n), 0)),
                  _full((1, N_HEADS_A))],
        out_specs=[pl.BlockSpec((span, WIDTH_A), cur), pl.BlockSpec((N_HEADS_A, span), lambda n: (0, n))],
        out_shape=[jax.ShapeDtypeStruct((t, WIDTH_A), F32), jax.ShapeDtypeStruct((N_HEADS_A, t), F32)],
        compiler_params=_params(("parallel",)),
    )(proj, proj, proj, proj, proj, posr, posc, posc, sinks)


def _rope_coeffs(pos, freq):
    ang = pos * freq
    cosv, sinv = jnp.cos(ang), jnp.sin(ang)
    lane = lax.broadcasted_iota(jnp.int32, ang.shape, 1)
    lo = (lane >= QK_NOPE) & (lane < QK_NOPE + QK_ROPE // 2)
    hi = (lane >= QK_NOPE + QK_ROPE // 2) & (lane < QK_NOPE + QK_ROPE)
    c = jnp.where(lane < QK_NOPE, 1.0, jnp.where(lo | hi, cosv, 0.0))
    s = jnp.where(lo, -sinv, jnp.where(hi, sinv, 0.0))
    return c, s, lo, hi


def _rope(xh, c, s, lo):
    up = pltpu.roll(xh, LANES - QK_ROPE // 2, axis=1)
    dn = pltpu.roll(xh, QK_ROPE // 2, axis=1)
    return xh * c + jnp.where(lo, up, dn) * s


def _unrope(dh, c, s, lo, hi):
    g = dh * s
    up = pltpu.roll(g, LANES - QK_ROPE // 2, axis=1)
    dn = pltpu.roll(g, QK_ROPE // 2, axis=1)
    return dh * c + jnp.where(hi, dn, jnp.where(lo, up, 0.0))


_TQ = 512
_MLA_SCALE = Q_HEAD_B ** -0.5


def _mla_prep_fwd(proj, posc, freq, qan, kvan, wq, wk, wv):
    t = proj.shape[0]
    tm = _TQ
    nb = t // tm

    def body(cq_ref, ckv_ref, kr_ref, pos_ref, f_ref, qan_ref, kvan_ref, wq_ref, wk_ref, wv_ref,
             q_ref, k_ref, qt_ref, kt_ref, vt_ref):
        cq = cq_ref[...]
        cqn = ((cq * _rms(cq)) * qan_ref[...]).astype(BF16)
        ckv = ckv_ref[...]
        ckvn = ((ckv * _rms(ckv)) * kvan_ref[...]).astype(BF16)
        qb = _dot(cqn, wq_ref[...])
        kb = _dot(ckvn, wk_ref[...])
        vbt = _dot_nt(wv_ref[...], ckvn)
        c, s, lo, _ = _rope_coeffs(pos_ref[...], f_ref[...])
        kr = _rope(pltpu.roll(kr_ref[...], QK_NOPE, axis=1), c, s, lo)
        for h in range(N_HEADS_B):
            sl = slice(HEAD_PAD * h, HEAD_PAD * (h + 1))
            q_h = _rope(qb[:, sl], c, s, lo)
            k_h = kb[:, sl] + kr
            q_ref[:, sl] = q_h.astype(BF16)
            k_ref[:, sl] = k_h.astype(BF16)
            qt_ref[h, 0] = q_h.T.astype(BF16)
            kt_ref[h, 0] = k_h.T.astype(BF16)
            vt_ref[h, 0] = vbt[V_DIM_B * h:V_DIM_B * (h + 1), :].astype(BF16)

    row = lambda i: (i, 0)
    blk4 = lambda d: pl.BlockSpec((N_HEADS_B, 1, d, tm), lambda i: (0, i, 0, 0))
    return pl.pallas_call(
        body, name="mla_prep_fwd", grid=(nb,),
        in_specs=[pl.BlockSpec((tm, Q_LORA), lambda i: (i, _CQ_BLK)),
                  pl.BlockSpec((tm, LANES), lambda i: (i, _CKV_BLK)),
                  pl.BlockSpec((tm, LANES), lambda i: (i, _KR_BLK)),
                  pl.BlockSpec((tm, 1), row), _full((1, LANES)), _full((1, Q_LORA)), _full((1, KV_LORA)),
                  _full((Q_LORA, MLA_W)), _full((KV_LORA, MLA_W)), _full((N_HEADS_B * V_DIM_B, KV_LORA))],
        out_specs=[pl.BlockSpec((tm, MLA_W), row), pl.BlockSpec((tm, MLA_W), row), blk4(HEAD_PAD), blk4(HEAD_PAD),
                   blk4(V_DIM_B)],
        out_shape=[jax.ShapeDtypeStruct((t, MLA_W), BF16), jax.ShapeDtypeStruct((t, MLA_W), BF16),
                   jax.ShapeDtypeStruct((N_HEADS_B, nb, HEAD_PAD, tm), BF16),
                   jax.ShapeDtypeStruct((N_HEADS_B, nb, HEAD_PAD, tm), BF16),
                   jax.ShapeDtypeStruct((N_HEADS_B, nb, V_DIM_B, tm), BF16)],
        compiler_params=_params(("parallel",)),
    )(proj, proj, proj, posc, freq, qan, kvan, wq, wk, wv)


_MLA_SCALE2 = _MLA_SCALE * _LOG2E


def _mla_fwd(k, qt, vt, w_src):
    t = k.shape[0]
    nb = t // _TQ
    groups = nb // 2

    def body(k_ref, qt_ref, vt_ref, w_ref, o_ref, l_ref, wg_ref, raw, send_sems, recv_sems, local_sem):
        g = pl.program_id(1)
        first = (pl.program_id(0) == 0) & (g == 0)
        last = (pl.program_id(0) == N_HEADS_B - 1) & (g == groups - 1)

        @pl.when(first)
        def _():
            _gather_start(w_ref, wg_ref, send_sems, recv_sems, local_sem)

        def keys(kj):
            return k_ref[pl.ds(pl.multiple_of(kj * _TQ, _TQ), _TQ), :]

        def products(kj, slot):
            kv = keys(kj)
            raw[slot, 0] = _dot(kv, qt_ref[0, 0])
            raw[slot, 1] = _dot(kv, qt_ref[0, 1])

        def update(stats, raw_ref, kj, diagonal=False):
            m, l, acc = stats
            scores = raw_ref[...]
            if diagonal:
                key = lax.broadcasted_iota(jnp.int32, scores.shape, 0)
                qry = lax.broadcasted_iota(jnp.int32, scores.shape, 1)
                scores = jnp.where(key <= qry, scores, NEG)
            m_new = jnp.maximum(m, jnp.max(scores, axis=0, keepdims=True) * _MLA_SCALE2)
            alpha = jnp.exp2(m - m_new)
            p = jnp.exp2(scores * _MLA_SCALE2 - m_new).astype(BF16)
            pv = _dot(jnp.concatenate([vt_ref[0, kj], jnp.ones((16, _TQ), BF16)], axis=0), p)
            return m_new, alpha * l + pv[V_DIM_B:V_DIM_B + 8], alpha * acc + pv[:V_DIM_B]

        def trip(i, stats):
            sa, sb = stats
            products(2 * i + 1, 1)
            sa, sb = update(sa, raw.at[0, 0], 2 * i), update(sb, raw.at[0, 1], 2 * i)
            products(2 * i + 2, 0)
            return update(sa, raw.at[1, 0], 2 * i + 1), update(sb, raw.at[1, 1], 2 * i + 1)

        init = (jnp.full((1, _TQ), NEG, F32), jnp.zeros((8, _TQ), F32), jnp.zeros((V_DIM_B, _TQ), F32))
        products(0, 0)
        sa, sb = lax.fori_loop(0, g, trip, (init, init))
        raw[1, 1] = _dot(keys(2 * g + 1), qt_ref[0, 1])
        sa = update(sa, raw.at[0, 0], 2 * g, True)
        sb = update(update(sb, raw.at[0, 1], 2 * g), raw.at[1, 1], 2 * g + 1, True)
        for which, (m, l, acc) in enumerate((sa, sb)):
            o_ref[0, which] = acc / l[0:1]
            l_ref[0, which] = m + jnp.log(l[0:1]) * _LOG2E

        @pl.when(last)
        def _():
            _gather_wait(w_ref, wg_ref, send_sems, recv_sems, local_sem)

    two = lambda d: pl.BlockSpec((1, 2, d, _TQ), lambda h, g: (h, g, 0, 0))
    return pl.pallas_call(
        body, name="mla_fwd", grid=(N_HEADS_B, groups),
        in_specs=[pl.BlockSpec((t, HEAD_PAD), lambda h, g: (0, h)), two(HEAD_PAD),
                  pl.BlockSpec((1, nb, V_DIM_B, _TQ), lambda h, g: (h, 0, 0, 0)), _HBM],
        out_specs=[two(V_DIM_B), two(1), _HBM],
        out_shape=[jax.ShapeDtypeStruct((N_HEADS_B, nb, V_DIM_B, _TQ), F32),
                   jax.ShapeDtypeStruct((N_HEADS_B, nb, 1, _TQ), F32),
                   jax.ShapeDtypeStruct((N_CHIPS,) + w_src.shape, w_src.dtype)],
        scratch_shapes=[pltpu.VMEM((2, 2, _TQ, _TQ), F32),
                        pltpu.SemaphoreType.DMA((3,)), pltpu.SemaphoreType.DMA((3,)), pltpu.SemaphoreType.DMA(())],
        compiler_params=_params(("arbitrary", "arbitrary")),
    )(k, qt, vt, w_src)


def _ot_spec(tm, d):
    per = _TQ // tm
    return pl.BlockSpec((N_HEADS_B, 1, d, tm), lambda i: (0, i // per, 0, i % per))


def _mix_out_fwd(out_a, out_bt, proj, x, w_oa, w_ob, wb, g2, g3):
    t = x.shape[0]
    tm = 512

    def body(oa_ref, obt_ref, ga_ref, gb_ref, x_ref, woa_ref, wob_ref, wout_ref, g2_ref, g3_ref,
             mg_ref, y_ref, x1_ref, h2_ref):
        oa = _dot(oa_ref[...].astype(BF16), woa_ref[...])
        obt = obt_ref[...].reshape(N_HEADS_B * V_DIM_B, tm).astype(BF16)
        ob = _dot_tn(obt, wob_ref[...])
        merged = (jax.nn.sigmoid(ga_ref[...]) * oa + jax.nn.sigmoid(gb_ref[...]) * ob).astype(BF16)
        mg_ref[...] = merged
        y = _dot(merged, wout_ref[...].reshape(D_MODEL, D_MODEL))
        y_ref[...] = y
        x1 = x_ref[...] + (y * _rms(y)) * g2_ref[...]
        x1_ref[...] = x1
        h2_ref[...] = ((x1 * _rms(x1)) * g3_ref[...]).astype(BF16)

    row = lambda i: (i, 0)
    blk = pl.BlockSpec((tm, D_MODEL), row)
    return pl.pallas_call(
        body, name="mix_out_fwd", grid=(t // tm,),
        in_specs=[pl.BlockSpec((tm, WIDTH_A), row), _ot_spec(tm, V_DIM_B), pl.BlockSpec((tm, D_MODEL), lambda i: (i, 0)),
                  pl.BlockSpec((tm, D_MODEL), lambda i: (i, 1)), blk,
                  _full((WIDTH_A, D_MODEL)), _full((N_HEADS_B * V_DIM_B, D_MODEL)), _wb_spec("w_out"),
                  _full((1, D_MODEL)), _full((1, D_MODEL))],
        out_specs=[blk, blk, blk, blk],
        out_shape=[jax.ShapeDtypeStruct((t, D_MODEL), BF16), jax.ShapeDtypeStruct((t, D_MODEL), F32),
                   jax.ShapeDtypeStruct((t, D_MODEL), F32), jax.ShapeDtypeStruct((t, D_MODEL), BF16)],
        compiler_params=_params(("parallel",)),
    )(out_a, out_bt, proj, proj, x, w_oa, w_ob, wb, g2, g3)


_TM_MLP = 512


def _row_halves(tm):
    return slice(0, tm // 2), slice(tm // 2, tm)


def _up_fwd(h2, wb):
    t = h2.shape[0]
    tm = _TM_MLP

    def body(h_ref, w_ref, a_ref):
        hv = h_ref[...]
        for j in range(N_CHIPS):
            u = _dot(hv, w_ref[j])
            a_ref[:, D_MODEL * j:D_MODEL * (j + 1)] = jnp.square(jnp.maximum(u, 0.0)).astype(BF16)

    return pl.pallas_call(
        body, name="up_fwd", grid=(t // tm,),
        in_specs=[pl.BlockSpec((tm, D_MODEL), lambda i: (i, 0)), _wb_spec("w_up")],
        out_specs=pl.BlockSpec((tm, D_FF), lambda i: (i, 0)),
        out_shape=jax.ShapeDtypeStruct((t, D_FF), BF16),
        compiler_params=_params(("parallel",)),
    )(h2, wb)


def _down_fwd_loss(a, wb, x1, target, g4):
    t = a.shape[0]
    tm = _TM_MLP

    def body(a_ref, w_ref, x1_ref, tg_ref, g_ref, dx2_ref, dyd_ref, dg_ref, loss_ref):
        @pl.when(pl.program_id(0) == 0)
        def _():
            dg_ref[...] = jnp.zeros(dg_ref.shape, F32)
            loss_ref[...] = jnp.zeros(loss_ref.shape, F32)

        yd = _dot(a_ref[...], w_ref[...].reshape(D_FF, D_MODEL))
        r = _rms(yd)
        n = yd * r
        diff = (x1_ref[...] + n * g_ref[...]) - tg_ref[...]
        loss_ref[...] += 0.5 * jnp.sum(jnp.mean(diff * diff, axis=-1, keepdims=True), axis=0, keepdims=True)
        dx2 = diff * (1.0 / D_MODEL)
        dx2_ref[...] = dx2
        dyd, dg = _norm_bwd(dx2, n, r, g_ref[...])
        dyd_ref[...] = dyd.astype(BF16)
        dg_ref[...] += dg

    row = lambda i: (i, 0)
    blk = pl.BlockSpec((tm, D_MODEL), row)
    return pl.pallas_call(
        body, name="down_fwd_loss", grid=(t // tm,),
        in_specs=[pl.BlockSpec((tm, D_FF), row), _wb_spec("w_down"), blk, blk, _full((1, D_MODEL))],
        out_specs=[blk, blk, _full((1, D_MODEL)), _full((1, LANES))],
        out_shape=[jax.ShapeDtypeStruct((t, D_MODEL), F32), jax.ShapeDtypeStruct((t, D_MODEL), BF16),
                   jax.ShapeDtypeStruct((1, D_MODEL), F32), jax.ShapeDtypeStruct((1, LANES), F32)],
        compiler_params=_params(("arbitrary",)),
    )(a, wb, x1, target, g4)


def _matmul_tn(a, b, name, tm, tn, tk=1024):
    t, m = a.shape
    n = b.shape[1]
    tk = min(tk, t)
    nk = t // tk

    def body(a_ref, b_ref, o_ref):
        @pl.when(pl.program_id(2) == 0)
        def _():
            o_ref[...] = jnp.zeros(o_ref.shape, F32)

        o_ref[...] += _dot_tn(a_ref[...].astype(BF16), b_ref[...].astype(BF16))

    return pl.pallas_call(
        body, name=name, grid=(m // tm, n // tn, nk),
        in_specs=[pl.BlockSpec((tk, tm), lambda i, j, k: (k, i)), pl.BlockSpec((tk, tn), lambda i, j, k: (k, j))],
        out_specs=pl.BlockSpec((tm, tn), lambda i, j, k: (i, j)),
        out_shape=jax.ShapeDtypeStruct((m, n), F32),
        compiler_params=_params(("parallel", "parallel", "arbitrary")),
    )(a, b)


_TK_DW = 2048


def _dw_into_blocks(a, b, weight, tm, tk, buf=None):
    t, m = a.shape
    n = b.shape[1]
    tk = min(tk, t)
    nk = t // tk
    rows = PACK_ROWS[weight]
    br = min(tm, rows)
    chips = tm // br
    first = _row_offset(GROUP_B, weight) // br
    per_chip = rows // br
    if weight == "w_up":
        out_map = lambda i, j, k: (j, first + i, 0)
    elif chips > 1:
        out_map = lambda i, j, k: (i, first, 0)
    else:
        out_map = lambda i, j, k: (i // per_chip, first + i % per_chip, 0)

    def body(a_ref, b_ref, *rest):
        o_ref = rest[-1]

        @pl.when(pl.program_id(2) == 0)
        def _():
            o_ref[...] = jnp.zeros(o_ref.shape, F32)

        o_ref[...] += _dot_tn(a_ref[...].astype(BF16), b_ref[...].astype(BF16)).reshape(o_ref.shape)

    in_specs = [pl.BlockSpec((tk, tm), lambda i, j, k: (k, i)), pl.BlockSpec((tk, D_MODEL), lambda i, j, k: (k, j))]
    operands = [a, b]
    if buf is not None:
        in_specs.append(pl.BlockSpec(memory_space=pl.ANY))
        operands.append(buf)
    total = sum(PACK_ROWS[w] for w in GROUP_B)
    return pl.pallas_call(
        body, name="dw_" + weight[2:], grid=(m // tm, n // D_MODEL, nk),
        in_specs=in_specs, out_specs=pl.BlockSpec((chips, br, D_MODEL), out_map),
        out_shape=jax.ShapeDtypeStruct((N_CHIPS, total, D_MODEL), F32),
        input_output_aliases={} if buf is None else {2: 0},
        compiler_params=_params(("parallel", "parallel", "arbitrary")),
    )(*operands)


def _down_bwd(dyd, wb, a):
    t = dyd.shape[0]
    tm = _TM_MLP

    def body(d_ref, w_ref, a_ref, du_ref):
        dv = d_ref[...]
        for j in range(N_CHIPS):
            cols = slice(D_MODEL * j, D_MODEL * (j + 1))
            av = a_ref[:, cols].astype(F32)
            relu_u = jnp.where(av > 0.0, av * lax.rsqrt(av), 0.0)
            du_ref[:, cols] = (_dot_nt(dv, w_ref[j]) * (2.0 * relu_u)).astype(BF16)

    row = lambda i: (i, 0)
    return pl.pallas_call(
        body, name="down_bwd", grid=(t // tm,),
        in_specs=[pl.BlockSpec((tm, D_MODEL), row), _wb_spec("w_down"), pl.BlockSpec((tm, D_FF), row)],
        out_specs=pl.BlockSpec((tm, D_FF), row),
        out_shape=jax.ShapeDtypeStruct((t, D_FF), BF16),
        compiler_params=_params(("parallel",)),
    )(dyd, wb, a)


def _up_bwd(du, wb, x1, dx2, y, g3, g2):
    t = du.shape[0]
    tm = _TM_MLP

    def body(du_ref, w_ref, x1_ref, dx2_ref, y_ref, g3_ref, g2_ref, dx1_ref, dy_ref, dg3_ref, dg2_ref):
        @pl.when(pl.program_id(0) == 0)
        def _():
            dg3_ref[...] = jnp.zeros(dg3_ref.shape, F32)
            dg2_ref[...] = jnp.zeros(dg2_ref.shape, F32)

        dh2 = _dot_nt(du_ref[:, 0:D_MODEL], w_ref[0])
        for j in range(1, N_CHIPS):
            dh2 = dh2 + _dot_nt(du_ref[:, D_MODEL * j:D_MODEL * (j + 1)], w_ref[j])
        x1 = x1_ref[...]
        r3 = _rms(x1)
        d3, dg3 = _norm_bwd(dh2, x1 * r3, r3, g3_ref[...])
        dx1 = dx2_ref[...] + d3
        dx1_ref[...] = dx1
        dg3_ref[...] += dg3
        y = y_ref[...]
        r2 = _rms(y)
        dy, dg2 = _norm_bwd(dx1, y * r2, r2, g2_ref[...])
        dy_ref[...] = dy.astype(BF16)
        dg2_ref[...] += dg2

    row = lambda i: (i, 0)
    blk = pl.BlockSpec((tm, D_MODEL), row)
    return pl.pallas_call(
        body, name="up_bwd", grid=(t // tm,),
        in_specs=[pl.BlockSpec((tm, D_FF), row), _wb_spec("w_up"),
                  blk, blk, blk, _full((1, D_MODEL)), _full((1, D_MODEL))],
        out_specs=[blk, blk, _full((1, D_MODEL)), _full((1, D_MODEL))],
        out_shape=[jax.ShapeDtypeStruct((t, D_MODEL), F32), jax.ShapeDtypeStruct((t, D_MODEL), BF16),
                   jax.ShapeDtypeStruct((1, D_MODEL), F32), jax.ShapeDtypeStruct((1, D_MODEL), F32)],
        compiler_params=_params(("arbitrary",)),
    )(du, wb, x1, dx2, y, g3, g2)


def _mix_out_bwd(dy, out_a, out_bt, proj, w_oa, w_ob, wb):
    t = dy.shape[0]
    tm = 512
    nb = t // _TQ

    def body(dy_ref, oa_ref, obt_ref, ga_ref, gb_ref, woa_ref, wob_ref, wout_ref,
             dwoa_ref, dwob_ref, dg_ref, da_ref, db_ref, dbt_ref, dela_ref, delb_ref):
        @pl.when(pl.program_id(0) == 0)
        def _():
            dwoa_ref[...] = jnp.zeros(dwoa_ref.shape, F32)
            dwob_ref[...] = jnp.zeros(dwob_ref.shape, F32)

        dm = _dot_nt(dy_ref[...], wout_ref[...].reshape(D_MODEL, D_MODEL))
        out_a_v = oa_ref[...]
        out_bt_v = obt_ref[...].reshape(N_HEADS_B * V_DIM_B, tm)
        oa = _dot(out_a_v.astype(BF16), woa_ref[...])
        ob = _dot_tn(out_bt_v.astype(BF16), wob_ref[...])
        sa, sb = jax.nn.sigmoid(ga_ref[...]), jax.nn.sigmoid(gb_ref[...])
        doa = (dm * sa).astype(BF16)
        dob = (dm * sb).astype(BF16)
        dwoa_ref[...] += _dot_tn(out_a_v.astype(BF16), doa)
        dwob_ref[...] += _dot(out_bt_v.astype(BF16), dob)
        dg_ref[:, :D_MODEL] = (dm * oa * (sa * (1.0 - sa))).astype(BF16)
        dg_ref[:, D_MODEL:] = (dm * ob * (sb * (1.0 - sb))).astype(BF16)
        d_out_a = _dot_nt(doa, woa_ref[...])
        da_ref[...] = d_out_a
        prod_at = (d_out_a * out_a_v).T
        dela_ref[...] = jnp.concatenate(
            [jnp.sum(_head_rows(prod_at, h), axis=0, keepdims=True) for h in range(N_HEADS_A)], axis=0)
        d_out_b = _dot_nt(dob, wob_ref[...])
        d_out_bt = _dot_nt(wob_ref[...], dob)
        prod_bt = d_out_bt * out_bt_v
        for h in range(N_HEADS_B):
            db_ref[h] = d_out_b[:, V_DIM_B * h:V_DIM_B * (h + 1)].astype(BF16)
            dbt_ref[h, 0] = d_out_bt[V_DIM_B * h:V_DIM_B * (h + 1), :].astype(BF16)
            delb_ref[h, 0] = jnp.sum(prod_bt[V_DIM_B * h:V_DIM_B * (h + 1), :], axis=0, keepdims=True)

    row = lambda i: (i, 0)
    blk = pl.BlockSpec((tm, D_MODEL), row)
    return pl.pallas_call(
        body, name="mix_out_bwd", grid=(t // tm,),
        in_specs=[blk, pl.BlockSpec((tm, WIDTH_A), row), _ot_spec(tm, V_DIM_B),
                  pl.BlockSpec((tm, D_MODEL), lambda i: (i, 0)), pl.BlockSpec((tm, D_MODEL), lambda i: (i, 1)),
                  _full((WIDTH_A, D_MODEL)), _full((N_HEADS_B * V_DIM_B, D_MODEL)), _wb_spec("w_out")],
        out_specs=[_full((WIDTH_A, D_MODEL)), _full((N_HEADS_B * V_DIM_B, D_MODEL)),
                   pl.BlockSpec((tm, 2 * D_MODEL), row), pl.BlockSpec((tm, WIDTH_A), row),
                   pl.BlockSpec((N_HEADS_B, tm, V_DIM_B), lambda i: (0, i, 0)), _ot_spec(tm, V_DIM_B),
                   pl.BlockSpec((N_HEADS_A, tm), lambda i: (0, i)), _ot_spec(tm, 1)],
        out_shape=[jax.ShapeDtypeStruct((WIDTH_A, D_MODEL), F32),
                   jax.ShapeDtypeStruct((N_HEADS_B * V_DIM_B, D_MODEL), F32),
                   jax.ShapeDtypeStruct((t, D_IN_PAD), BF16), jax.ShapeDtypeStruct((t, WIDTH_A), F32),
                   jax.ShapeDtypeStruct((N_HEADS_B, t, V_DIM_B), BF16),
                   jax.ShapeDtypeStruct((N_HEADS_B, nb, V_DIM_B, _TQ), BF16), jax.ShapeDtypeStruct((N_HEADS_A, t), F32),
                   jax.ShapeDtypeStruct((N_HEADS_B, nb, 1, _TQ), F32)],
        compiler_params=_params(("arbitrary",)),
    )(dy, out_a, out_bt, proj, proj, w_oa, w_ob, wb)


def _mla_bwd(q, k, qt, kt, vt, d_out, d_out_t, lse, delta, gp):
    t = q.shape[0]
    nb = t // _TQ

    def body(k_ref, kt_ref, vt_ref, q_ref, qt_ref, do_ref, dot_ref, lrow_ref, drow_ref, gp_ref,
             dq_ref, dkt_ref, dvt_ref, land_ref, l_rep, d_rep, send_sems, recv_sems):
        step = pl.program_id(1)
        kj = nb - 1 - step

        @pl.when((pl.program_id(0) == 0) & (step == 0))
        def _():
            _scatter_start(gp_ref, land_ref, send_sems, recv_sems)

        @pl.when(step == 0)
        def _():
            dq_ref[...] = jnp.zeros(dq_ref.shape, F32)
            for b in range(nb):
                l_rep[_TQ * b:_TQ * (b + 1), :] = jnp.broadcast_to(lrow_ref[0, b], (LANES, _TQ)).T
                d_rep[_TQ * b:_TQ * (b + 1), :] = jnp.broadcast_to(drow_ref[0, b], (LANES, _TQ)).T

        kv, k_t, v_t = k_ref[...], kt_ref[0, 0], vt_ref[0, 0]

        def rows_of(qi):
            return pl.ds(pl.multiple_of(qi * _TQ, _TQ), _TQ)

        def products(qi, diagonal=False):
            s = _dot(q_ref[rows_of(qi), :], k_t) * _MLA_SCALE2
            if diagonal:
                qry = lax.broadcasted_iota(jnp.int32, s.shape, 0)
                key = lax.broadcasted_iota(jnp.int32, s.shape, 1)
                s = jnp.where(key <= qry, s, NEG)
            return s, _dot(do_ref[0, rows_of(qi), :], v_t)

        def update(carry, prods, qi):
            dkt, dvt = carry
            s, dp = prods
            lse, delta = l_rep[rows_of(qi), :], d_rep[rows_of(qi), :]
            ps, dss = [], []
            for c in range(_TQ // LANES):
                strip = slice(LANES * c, LANES * (c + 1))
                p = jnp.exp2(s[:, strip] - lse)
                ps.append(p.astype(BF16))
                dss.append((p * (dp[:, strip] - delta) * _MLA_SCALE).astype(BF16))
            p_b, ds_b = jnp.concatenate(ps, axis=1), jnp.concatenate(dss, axis=1)
            dvt = dvt + _dot(dot_ref[0, qi], p_b)
            dkt = dkt + _dot(qt_ref[0, qi], ds_b)
            dq_ref[rows_of(qi), :] += _dot(ds_b, kv)
            return dkt, dvt

        def pair(i, carry):
            qa = kj + 1 + 2 * i
            pa, pb = products(qa), products(qa + 1)
            return update(update(carry, pa, qa), pb, qa + 1)

        init = (jnp.zeros((HEAD_PAD, _TQ), F32), jnp.zeros((V_DIM_B, _TQ), F32))
        carry = update(init, products(kj, True), kj)
        pairs = (nb - 1 - kj) // 2
        carry = lax.fori_loop(0, pairs, pair, carry)
        dkt, dvt = lax.fori_loop(kj + 1 + 2 * pairs, nb, lambda qi, cr: update(cr, products(qi), qi), carry)
        dkt_ref[0, 0] = dkt
        dvt_ref[0, 0] = dvt

        @pl.when((pl.program_id(0) == N_HEADS_B - 1) & (step == nb - 1))
        def _():
            _scatter_wait(gp_ref, land_ref, send_sems, recv_sems)

    head4 = lambda d: pl.BlockSpec((1, nb, d, _TQ), lambda h, s: (h, 0, 0, 0))
    blk4 = lambda d: pl.BlockSpec((1, 1, d, _TQ), lambda h, s: (h, nb - 1 - s, 0, 0))
    head3 = lambda d: pl.BlockSpec((1, t, d), lambda h, kj: (h, 0, 0))
    per_head = pl.BlockSpec((t, HEAD_PAD), lambda h, kj: (0, h))
    return pl.pallas_call(
        body, name="mla_bwd", grid=(N_HEADS_B, nb),
        in_specs=[pl.BlockSpec((_TQ, HEAD_PAD), lambda h, s: (nb - 1 - s, h)), blk4(HEAD_PAD), blk4(V_DIM_B),
                  per_head, head4(HEAD_PAD), head3(V_DIM_B), head4(V_DIM_B), head4(1), head4(1), _HBM],
        out_specs=[per_head, blk4(HEAD_PAD), blk4(V_DIM_B), _HBM],
        out_shape=[jax.ShapeDtypeStruct((t, MLA_W), F32), jax.ShapeDtypeStruct((N_HEADS_B, nb, HEAD_PAD, _TQ), F32),
                   jax.ShapeDtypeStruct((N_HEADS_B, nb, V_DIM_B, _TQ), F32),
                   jax.ShapeDtypeStruct((3,) + gp.shape[1:], gp.dtype)],
        scratch_shapes=[pltpu.VMEM((t, LANES), F32), pltpu.VMEM((t, LANES), F32),
                        pltpu.SemaphoreType.DMA((3,)), pltpu.SemaphoreType.DMA((3,))],
        compiler_params=_params(("arbitrary", "arbitrary")),
    )(k, kt, vt, q, qt, d_out, d_out_t, lse, delta, gp)


def _mla_prep_bwd(dq, dkt, dvt, proj, posc, freq, qan, kvan, wq, wk, wv, swap_src):
    t = dq.shape[0]
    tm = _TQ

    def body(dq_ref, dkt_ref, dvt_ref, cq_ref, ckv_ref, pos_ref, f_ref, qan_ref, kvan_ref, wq_ref, wk_ref, wv_ref, src_ref,
             dcq_ref, dckv_ref, dkr_ref, dwq_ref, dwk_ref, dwv_ref, dqan_ref, dkvan_ref, got_ref, send_sem, recv_sem):
        swap = _sibling_copy(src_ref, got_ref, send_sem, recv_sem)

        @pl.when(pl.program_id(0) == 0)
        def _():
            swap.start()
            for r in (dwq_ref, dwk_ref, dwv_ref, dqan_ref, dkvan_ref):
                r[...] = jnp.zeros(r.shape, F32)

        cq = cq_ref[...]
        rq = _rms(cq)
        nq_ = cq * rq
        cqn = (nq_ * qan_ref[...]).astype(BF16)
        ckv = ckv_ref[...]
        rkv = _rms(ckv)
        nkv = ckv * rkv
        ckvn = (nkv * kvan_ref[...]).astype(BF16)
        c, s, lo, hi = _rope_coeffs(pos_ref[...], f_ref[...])
        dkr = jnp.zeros((tm, LANES), F32)
        dqb, dkb = [], []
        for h in range(N_HEADS_B):
            dqb.append(_unrope(dq_ref[:, HEAD_PAD * h:HEAD_PAD * (h + 1)], c, s, lo, hi).astype(BF16))
            dk_h = dkt_ref[h, 0].T
            dkr = dkr + dk_h
            dkb.append(dk_h.astype(BF16))
        dqb, dkb = jnp.concatenate(dqb, axis=1), jnp.concatenate(dkb, axis=1)
        dkr = jnp.where(lo | hi, _unrope(dkr, c, s, lo, hi), 0.0)
        dkr_ref[...] = pltpu.roll(dkr, LANES - QK_NOPE, axis=1).astype(BF16)
        dvb = dvt_ref[...].reshape(N_HEADS_B * V_DIM_B, tm).T.astype(BF16)
        dwq_ref[...] += _dot_tn(cqn, dqb)
        dwk_ref[...] += _dot_tn(ckvn, dkb)
        dwv_ref[...] += _dot_tn(ckvn, dvb)
        dcqn = _dot_nt(dqb, wq_ref[...])
        dckvn = _dot_nt(dkb, wk_ref[...]) + _dot_nt(dvb, wv_ref[...])
        dcq, dqan = _norm_bwd(dcqn, nq_, rq, qan_ref[...])
        dckv, dkvan = _norm_bwd(dckvn, nkv, rkv, kvan_ref[...])
        dcq_ref[...] = dcq.astype(BF16)
        dckv_ref[...] = dckv.astype(BF16)
        dqan_ref[...] += dqan
        dkvan_ref[...] += dkvan

        @pl.when(pl.program_id(0) == t // tm - 1)
        def _():
            swap.wait_recv()
            swap.wait_send()

    row = lambda i: (i, 0)
    vw = N_HEADS_B * V_DIM_B
    return pl.pallas_call(
        body, name="mla_prep_bwd", grid=(t // tm,),
        in_specs=[pl.BlockSpec((tm, MLA_W), row), pl.BlockSpec((N_HEADS_B, 1, HEAD_PAD, tm), lambda i: (0, i, 0, 0)),
                  pl.BlockSpec((N_HEADS_B, 1, V_DIM_B, tm), lambda i: (0, i, 0, 0)),
                  pl.BlockSpec((tm, Q_LORA), lambda i: (i, _CQ_BLK)),
                  pl.BlockSpec((tm, LANES), lambda i: (i, _CKV_BLK)),
                  pl.BlockSpec((tm, 1), row), _full((1, LANES)), _full((1, Q_LORA)), _full((1, KV_LORA)),
                  _full((Q_LORA, MLA_W)), _full((KV_LORA, MLA_W)), _full((KV_LORA, vw)), _HBM],
        out_specs=[pl.BlockSpec((tm, Q_LORA), row), pl.BlockSpec((tm, LANES), row), pl.BlockSpec((tm, LANES), row),
                   _full((Q_LORA, MLA_W)), _full((KV_LORA, MLA_W)), _full((KV_LORA, vw)),
                   _full((1, Q_LORA)), _full((1, KV_LORA)), _HBM],
        out_shape=[jax.ShapeDtypeStruct((t, Q_LORA), BF16), jax.ShapeDtypeStruct((t, LANES), BF16),
                   jax.ShapeDtypeStruct((t, LANES), BF16),
                   jax.ShapeDtypeStruct((Q_LORA, MLA_W), F32), jax.ShapeDtypeStruct((KV_LORA, MLA_W), F32),
                   jax.ShapeDtypeStruct((KV_LORA, vw), F32),
                   jax.ShapeDtypeStruct((1, Q_LORA), F32), jax.ShapeDtypeStruct((1, KV_LORA), F32),
                   jax.ShapeDtypeStruct(swap_src.shape, swap_src.dtype)],
        scratch_shapes=[pltpu.SemaphoreType.DMA(()), pltpu.SemaphoreType.DMA(())],
        compiler_params=_params(("arbitrary",)),
    )(dq, dkt, dvt, proj, proj, posc, freq, qan, kvan, wq, wk, wv, swap_src)


def _swa_bwd(proj, d_out, lse, delta, posc, posr, sinks):
    t = proj.shape[0]
    per = _SWA_PER_STEP
    span = per * BLOCK
    steps = t // span

    def body(q_ref, kc_ref, kp_ref, vc_ref, vp_ref, do_ref, l_ref, d_ref, pq_ref, pc_ref, pp_ref, sink_ref,
             dq_ref, dk_ref, dv_ref, ds_ref, dkb_s, dvb_s, dk_keep, dv_keep):
        n = pl.program_id(0)

        @pl.when(n == 0)
        def _():
            ds_ref[...] = jnp.zeros(ds_ref.shape, F32)
            dk_keep[...] = jnp.zeros(dk_keep.shape, F32)
            dv_keep[...] = jnp.zeros(dv_keep.shape, F32)

        @pl.when(n < steps)
        def _():
            k_all = jnp.concatenate([kp_ref[...], kc_ref[...]], axis=0)
            v_all = jnp.concatenate([vp_ref[...], vc_ref[...]], axis=0)
            pos_all = jnp.concatenate([pp_ref[...], pc_ref[...]], axis=0)
            ki = lax.broadcasted_iota(jnp.int32, (2 * BLOCK, BLOCK), 0)
            qi = lax.broadcasted_iota(jnp.int32, (2 * BLOCK, BLOCK), 1)
            window = (ki > qi) & (ki <= qi + WINDOW)
            lane = lax.broadcasted_iota(jnp.int32, (1, LANES), 1)
            dsink = jnp.zeros((1, LANES), F32)
            for sub in range(per):
                band = slice(BLOCK * sub, BLOCK * (sub + 2))
                own = slice(BLOCK * sub, BLOCK * (sub + 1))
                kb, vb = k_all[band], v_all[band]
                dist = jnp.abs(pos_all[band] - pq_ref[:, own])
                valid = window & ((n > 0) | (ki >= BLOCK)) if sub == 0 else window
                qv, dov = q_ref[own, :], do_ref[own, :]
                q_t, do_t, kb_t = qv.T, dov.T, kb.T
                dq_t = []
                for kh in range(N_KV_A):
                    heads = range(_GROUP_A * kh, _GROUP_A * (kh + 1))
                    st_g = _dot(_head_cols(kb, kh).astype(BF16), _group_t(q_t, kh))
                    dpt_g = _dot(_head_cols(vb, kh).astype(BF16), _group_t(do_t, kh))
                    pts, dsts = [], []
                    for j, h in enumerate(heads):
                        st = _swa_scores_t(st_g, j, h, dist, valid)
                        l_h, d_h = l_ref[h:h + 1, own], d_ref[h:h + 1, own]
                        pt = jnp.exp2(st - l_h)
                        p_sink = jnp.exp2(sink_ref[0:1, h:h + 1] * _LOG2E - l_h)
                        dsink = dsink + jnp.where(lane == h, jnp.sum(-p_sink * d_h, axis=1, keepdims=True), 0.0)
                        dst = pt * (dpt_g[:, BLOCK * j:BLOCK * (j + 1)] - d_h) * _SWA_SCALE
                        pts.append(pt.astype(BF16))
                        dsts.append(dst.astype(BF16))
                    pt_g, dst_g = jnp.concatenate(pts, axis=1), jnp.concatenate(dsts, axis=1)
                    q_g = jnp.concatenate([_head_cols(qv, h) for h in heads], axis=0).astype(BF16)
                    do_g = jnp.concatenate([_head_cols(dov, h) for h in heads], axis=0).astype(BF16)
                    dkb_s[sub, :, HEAD_DIM_A * kh:HEAD_DIM_A * (kh + 1)] = _dot(dst_g, q_g)
                    dvb_s[sub, :, HEAD_DIM_A * kh:HEAD_DIM_A * (kh + 1)] = _dot(pt_g, do_g)
                    dq_g = _dot(_head_rows(kb_t, kh).astype(BF16), dst_g)
                    dq_t.extend(dq_g[:, BLOCK * j:BLOCK * (j + 1)] for j in range(_GROUP_A))
                dq_ref[own, :] = jnp.concatenate(dq_t, axis=0).T
            ds_ref[...] += dsink
            for keep, out, parts in ((dk_keep, dk_ref, dkb_s), (dv_keep, dv_ref, dvb_s)):
                out[0:span - BLOCK, :] = keep[0:span - BLOCK, :]
                out[span - BLOCK:span, :] = keep[span - BLOCK:span, :] + parts[0, 0:BLOCK, :]
                for s in range(per - 1):
                    keep[BLOCK * s:BLOCK * (s + 1), :] = parts[s, BLOCK:2 * BLOCK, :] + parts[s + 1, 0:BLOCK, :]
                keep[span - BLOCK:span, :] = parts[per - 1, BLOCK:2 * BLOCK, :]

        @pl.when(n == steps)
        def _():
            dk_ref[...] = dk_keep[...]
            dv_ref[...] = dv_keep[...]

    last = steps - 1
    cur = lambda n: (jnp.minimum(n, last), 0)
    cur_t = lambda n: (0, jnp.minimum(n, last))
    prv = lambda n: jnp.maximum(per * jnp.minimum(n, last) - 1, 0)
    out_prev = lambda n: (jnp.maximum(n - 1, 0), 0)
    return pl.pallas_call(
        body, name="swa_bwd", grid=(steps + 1,),
        in_specs=[pl.BlockSpec((span, WIDTH_A), lambda n: (jnp.minimum(n, last), _QA_BLK)),
                  pl.BlockSpec((span, LANES), lambda n: (jnp.minimum(n, last), _KA_BLK)),
                  pl.BlockSpec((BLOCK, LANES), lambda n: (prv(n), _KA_BLK)),
                  pl.BlockSpec((span, LANES), lambda n: (jnp.minimum(n, last), _VA_BLK)),
                  pl.BlockSpec((BLOCK, LANES), lambda n: (prv(n), _VA_BLK)),
                  pl.BlockSpec((span, WIDTH_A), cur), pl.BlockSpec((N_HEADS_A, span), cur_t),
                  pl.BlockSpec((N_HEADS_A, span), cur_t), pl.BlockSpec((1, span), cur_t),
                  pl.BlockSpec((span, 1), cur), pl.BlockSpec((BLOCK, 1), lambda n: (prv(n), 0)),
                  _full((1, N_HEADS_A))],
        out_specs=[pl.BlockSpec((span, WIDTH_A), cur), pl.BlockSpec((span, LANES), out_prev),
                   pl.BlockSpec((span, LANES), out_prev), _full((1, LANES))],
        out_shape=[jax.ShapeDtypeStruct((t, WIDTH_A), F32), jax.ShapeDtypeStruct((t, LANES), F32),
                   jax.ShapeDtypeStruct((t, LANES), F32), jax.ShapeDtypeStruct((1, LANES), F32)],
        scratch_shapes=[pltpu.VMEM((per, 2 * BLOCK, LANES), F32), pltpu.VMEM((per, 2 * BLOCK, LANES), F32),
                        pltpu.VMEM((span, LANES), F32), pltpu.VMEM((span, LANES), F32)],
        compiler_params=_params(("arbitrary",)),
    )(proj, proj, proj, proj, proj, d_out, lse, delta, posr, posc, posc, sinks)


def _in_bwd(dproj, w_in_t, x, dx1, g1, gp):
    t = x.shape[0]
    tm = 512
    steps = t // tm

    def body(dp_ref, w_ref, x_ref, dx1_ref, g_ref, gp_ref, dx_ref, dg_ref, land_ref, send_sems, recv_sems):
        i = pl.program_id(0)

        @pl.when(i == 0)
        def _():
            dg_ref[...] = jnp.zeros(dg_ref.shape, F32)
            _scatter_start(gp_ref, land_ref, send_sems, recv_sems)

        for rows in _row_halves(tm):
            dh = _dot(dp_ref[rows, :], w_ref[...])
            xv = x_ref[rows, :]
            r = _rms(xv)
            dx, dg = _norm_bwd(dh, xv * r, r, g_ref[...])
            dx_ref[rows, :] = dx1_ref[rows, :] + dx
            dg_ref[...] += dg

        @pl.when(i == steps - 1)
        def _():
            _scatter_wait(gp_ref, land_ref, send_sems, recv_sems)

    row = lambda i: (i, 0)
    blk = pl.BlockSpec((tm, D_MODEL), row)
    return pl.pallas_call(
        body, name="in_bwd", grid=(steps,),
        in_specs=[pl.BlockSpec((tm, D_IN_PAD), row), _full((D_IN_PAD, D_MODEL)), blk, blk, _full((1, D_MODEL)), _HBM],
        out_specs=[blk, _full((1, D_MODEL)), _HBM],
        out_shape=[jax.ShapeDtypeStruct((t, D_MODEL), F32), jax.ShapeDtypeStruct((1, D_MODEL), F32),
                   jax.ShapeDtypeStruct((3,) + gp.shape[1:], gp.dtype)],
        scratch_shapes=[pltpu.SemaphoreType.DMA((3,)), pltpu.SemaphoreType.DMA((3,))],
        compiler_params=_params(("arbitrary",)),
    )(dproj, w_in_t, x, dx1, g1, gp)


def _adamw_store(w, g, m, v, out_refs):
    g_out, d_out, m_out, v_out = out_refs
    m_new = ADAM_B1 * m + (1.0 - ADAM_B1) * g
    v_new = ADAM_B2 * v + (1.0 - ADAM_B2) * jnp.square(g)
    m_hat = m_new / (1.0 - ADAM_B1 ** ADAM_STEP)
    v_hat = v_new / (1.0 - ADAM_B2 ** ADAM_STEP)
    g_out[...] = g
    d_out[...] = -ADAM_LR * (m_hat / (jnp.sqrt(v_hat) + ADAM_EPS) + ADAM_WD * w)
    m_out[...] = m_new
    v_out[...] = v_new


_SMALL_SLOTS = {"pre_norm_mix": (0, 0, D_MODEL), "post_norm_mix": (1, 0, D_MODEL), "pre_norm_mlp": (2, 0, D_MODEL),
                "post_norm_mlp": (3, 0, D_MODEL), "q_a_norm": (4, 0, Q_LORA), "kv_a_norm": (4, Q_LORA, KV_LORA),
                "sinks": (4, Q_LORA + KV_LORA, N_HEADS_A)}
_LOSS_ROW = 5


def _adamw_small(red, w, m, v):
    names = tuple(_SMALL_SLOTS)
    n = len(names)

    def body(*refs):
        red_ref, ws, ms, vs, outs = refs[0], refs[1:1 + n], refs[1 + n:1 + 2 * n], refs[1 + 2 * n:1 + 3 * n], refs[1 + 3 * n:]
        for k, name in enumerate(names):
            row, lane, width = _SMALL_SLOTS[name]
            g = red_ref[row:row + 1, lane:lane + width]
            _adamw_store(ws[k][...], g, ms[k][...], vs[k][...], outs[4 * k:4 * k + 4])

    vmem = pl.BlockSpec(memory_space=pltpu.VMEM)
    res = pl.pallas_call(
        body, name="adamw_small", in_specs=[vmem] * (1 + 3 * n), out_specs=[vmem] * (4 * n),
        out_shape=[jax.ShapeDtypeStruct(w[name].shape, F32) for name in names for _ in range(4)],
    )(red, *[w[k] for k in names], *[m[k] for k in names], *[v[k] for k in names])
    return {name: res[4 * k:4 * k + 4] for k, name in enumerate(names)}


_ADAMW_RIDERS = ("w_up", "w_down", "w_out")


def _dw_in_adamw(dproj, h, g_parts, w, m, v):
    t, cols = dproj.shape
    tm, tk = cols // 2, min(1024, t)
    rows_out = N_CHIPS * SHARD_SHAPES["w_in"][1]
    nk = t // tk
    steps = 2 * nk
    names = _ADAMW_RIDERS
    n = len(names)

    def body(a_ref, b_ref, *rest):
        g1s, g2s, ws, ms, vs = (rest[n * j:n * (j + 1)] for j in range(5))
        o_ref, outs = rest[5 * n], rest[5 * n + 1:]

        @pl.when(pl.program_id(2) == 0)
        def _():
            o_ref[...] = jnp.zeros(o_ref.shape, F32)

        o_ref[...] += _dot_tn(a_ref[...], b_ref[...])
        for j in range(n):
            _adamw_store(ws[j][...], g1s[j][...] + g2s[j][...], ms[j][...], vs[j][...], outs[4 * j:4 * j + 4])

    def rider_spec(name, packed):
        br = SHARD_SHAPES[name][0] // steps
        first = _row_offset(GROUP_B, name) // br if packed else 0
        return pl.BlockSpec((br, D_MODEL), lambda i, j, k: (first + i * nk + k, 0))

    g_specs = [rider_spec(name, True) for name in names]
    own_specs = [rider_spec(name, False) for name in names]
    res = pl.pallas_call(
        body, name="dw_in", grid=(2, 1, nk),
        in_specs=[pl.BlockSpec((tk, tm), lambda i, j, k: (k, i)), pl.BlockSpec((tk, D_MODEL), lambda i, j, k: (k, 0))]
        + g_specs * 2 + own_specs * 3,
        out_specs=[pl.BlockSpec((tm, D_MODEL), lambda i, j, k: (i, 0))] + [s for s in own_specs for _ in range(4)],
        out_shape=[jax.ShapeDtypeStruct((rows_out, D_MODEL), F32)]
        + [jax.ShapeDtypeStruct(SHARD_SHAPES[name], F32) for name in names for _ in range(4)],
        compiler_params=_params(("arbitrary", "arbitrary", "arbitrary")),
    )(dproj, h, *[g_parts[0]] * n, *[g_parts[1]] * n, *[w[k] for k in names], *[m[k] for k in names],
      *[v[k] for k in names])
    return res[0], {name: res[1 + 4 * j:5 + 4 * j] for j, name in enumerate(names)}


def _adamw(w, g_parts, m, v, name, block, g_row_off=0):
    r, c = w.shape
    br, bc = block
    ng = len(g_parts)

    def body(*refs):
        w_ref, g_refs, m_ref, v_ref = refs[0], refs[1:1 + ng], refs[1 + ng], refs[2 + ng]
        g = g_refs[0][...]
        for gr in g_refs[1:]:
            g = g + gr[...]
        _adamw_store(w_ref[...], g, m_ref[...], v_ref[...], refs[3 + ng:])

    assert g_row_off % br == 0 and r % br == 0 and c % bc == 0
    blk = pl.BlockSpec(block, lambda i, j: (i, j))
    g_blk = pl.BlockSpec(block, lambda i, j: (i + g_row_off // br, j))
    return pl.pallas_call(
        body, name=name, grid=(r // br, c // bc),
        in_specs=[blk] + [g_blk] * ng + [blk, blk], out_specs=[blk] * 4,
        out_shape=[jax.ShapeDtypeStruct((r, c), F32)] * 4,
        compiler_params=_params(("parallel", "parallel")),
    )(w, *g_parts, m, v)


_HBM = pl.BlockSpec(memory_space=pltpu.HBM)


def _other_chips(x, y):
    return ((1 - x, y), (x, 1 - y), (1 - x, 1 - y))


def _gather_copies(src, out, send_sems, recv_sems, local_sem):
    x, y, c = lax.axis_index("x"), lax.axis_index("y"), lax.axis_index("c")
    me = 2 * x + y
    local = pltpu.make_async_copy(src, out.at[me], local_sem)

    def copies(arriving):
        return [pltpu.make_async_remote_copy(src_ref=src, dst_ref=out.at[2 * px + py if arriving else me],
                                             send_sem=send_sems.at[j], recv_sem=recv_sems.at[j], device_id=(px, py, c),
                                             device_id_type=MESH)
                for j, (px, py) in enumerate(_other_chips(x, y))]

    return local, copies


def _gather_start(src, out, send_sems, recv_sems, local_sem):
    local, copies = _gather_copies(src, out, send_sems, recv_sems, local_sem)
    local.start()
    for cp in copies(False):
        cp.start()


def _gather_wait(src, out, send_sems, recv_sems, local_sem):
    local, copies = _gather_copies(src, out, send_sems, recv_sems, local_sem)
    for cp in copies(True):
        cp.wait_recv()
    for cp in copies(False):
        cp.wait_send()
    local.wait()


def _scatter_copies(src, land, send_sems, recv_sems):
    x, y, c = lax.axis_index("x"), lax.axis_index("y"), lax.axis_index("c")
    return [pltpu.make_async_remote_copy(src_ref=src.at[2 * px + py], dst_ref=land.at[j], send_sem=send_sems.at[j],
                                         recv_sem=recv_sems.at[j], device_id=(px, py, c), device_id_type=MESH)
            for j, (px, py) in enumerate(_other_chips(x, y))]


def _scatter_start(src, land, send_sems, recv_sems):
    for cp in _scatter_copies(src, land, send_sems, recv_sems):
        cp.start()


def _scatter_wait(src, land, send_sems, recv_sems):
    copies = _scatter_copies(src, land, send_sems, recv_sems)
    for cp in copies:
        cp.wait_recv()
    for cp in copies:
        cp.wait_send()


def _all_gather_chips(packed):
    r = packed.shape[0]
    half = r // 2

    def body(src, out, ici_send, ici_recv, d2d_send, d2d_recv, local_sem):
        x, y, c = lax.axis_index("x"), lax.axis_index("y"), lax.axis_index("c")
        me = 2 * x + y
        mine = pl.ds(pl.multiple_of(c * half, 16), half)
        theirs = pl.ds(pl.multiple_of((1 - c) * half, 16), half)
        chips = _other_chips(x, y)
        local = pltpu.make_async_copy(src, out.at[me], local_sem)
        local.start()
        sends = [pltpu.make_async_remote_copy(src_ref=src.at[mine], dst_ref=out.at[me, mine], send_sem=ici_send.at[j],
                                              recv_sem=ici_recv.at[j], device_id=(px, py, c), device_id_type=MESH)
                 for j, (px, py) in enumerate(chips)]
        for cp in sends:
            cp.start()
        passed = []
        for j, (px, py) in enumerate(chips):
            block = 2 * px + py
            pltpu.make_async_remote_copy(src_ref=src.at[mine], dst_ref=out.at[block, mine], send_sem=ici_send.at[j],
                                         recv_sem=ici_recv.at[j], device_id=(px, py, c), device_id_type=MESH).wait_recv()
            cp = pltpu.make_async_remote_copy(src_ref=out.at[block, mine], dst_ref=out.at[block, mine],
                                              send_sem=d2d_send.at[j], recv_sem=d2d_recv.at[j],
                                              device_id=(x, y, 1 - c), device_id_type=MESH)
            cp.start()
            passed.append(cp)
        for j, (px, py) in enumerate(chips):
            block = 2 * px + py
            pltpu.make_async_remote_copy(src_ref=out.at[block, theirs], dst_ref=out.at[block, theirs],
                                         send_sem=d2d_send.at[j], recv_sem=d2d_recv.at[j],
                                         device_id=(x, y, 1 - c), device_id_type=MESH).wait_recv()
        for cp in sends + passed:
            cp.wait_send()
        local.wait()

    sems = pltpu.SemaphoreType.DMA((3,))
    return pl.pallas_call(
        body, name="ag_weights", in_specs=[_HBM], out_specs=_HBM,
        out_shape=jax.ShapeDtypeStruct((N_CHIPS,) + packed.shape, packed.dtype),
        scratch_shapes=[sems, sems, sems, sems, pltpu.SemaphoreType.DMA(())],
    )(packed)


def _sum4(gp, land, chip, name):
    _, r, w = gp.shape
    tr = 256 if r % 256 == 0 else 128

    def body(chip_ref, o_ref, l_ref, s_ref):
        s_ref[...] = ((o_ref[0] + l_ref[0].astype(F32)) + l_ref[1].astype(F32)) + l_ref[2].astype(F32)

    return pl.pallas_call(
        body, name=name,
        grid_spec=pltpu.PrefetchScalarGridSpec(
            num_scalar_prefetch=1, grid=(r // tr,),
            in_specs=[pl.BlockSpec((1, tr, w), lambda i, chip_ref: (chip_ref[0], i, 0)),
                      pl.BlockSpec((3, tr, w), lambda i, chip_ref: (0, i, 0))],
            out_specs=pl.BlockSpec((tr, w), lambda i, chip_ref: (i, 0))),
        out_shape=jax.ShapeDtypeStruct((r, w), F32),
        compiler_params=_params(("parallel",)),
    )(chip, gp, land)


def _sibling_copy(src, got, send_sem, recv_sem):
    x, y, c = lax.axis_index("x"), lax.axis_index("y"), lax.axis_index("c")
    return pltpu.make_async_remote_copy(src_ref=src, dst_ref=got, send_sem=send_sem, recv_sem=recv_sem,
                                        device_id=(x, y, 1 - c), device_id_type=MESH)


def _swap_sibling(s, name):
    def body(src, got, send_sem, recv_sem):
        cp = _sibling_copy(src, got, send_sem, recv_sem)
        cp.start()
        cp.wait_recv()
        cp.wait_send()

    return pl.pallas_call(
        body, name=name, in_specs=[_HBM], out_specs=_HBM,
        out_shape=jax.ShapeDtypeStruct(s.shape, s.dtype),
        scratch_shapes=[pltpu.SemaphoreType.DMA(()), pltpu.SemaphoreType.DMA(())],
    )(s)


def _all_reduce_small(dsmall, loss):
    n_dev = 8
    names = tuple(_SMALL_SLOTS)
    shape = (8, D_MODEL)

    def body(*refs):
        parts, loss_ref = refs[:len(names)], refs[len(names)]
        out, src, gath, send_sems, recv_sems = refs[len(names) + 1:]
        x, y, c = lax.axis_index("x"), lax.axis_index("y"), lax.axis_index("c")
        me = 4 * x + 2 * y + c
        src[...] = jnp.zeros(shape, F32)
        for name, part in zip(names, parts):
            row, lane, _ = _SMALL_SLOTS[name]
            src[row:row + 1, lane:lane + part.shape[1]] = part[...]
        src[_LOSS_ROW:_LOSS_ROW + 1, 0:LANES] = loss_ref[...]
        gath[me] = src[...]
        peers = []
        for k in range(1, n_dev):
            px = 1 - x if (k >> 2) & 1 else x
            py = 1 - y if (k >> 1) & 1 else y
            pc = 1 - c if k & 1 else c
            peers.append((px, py, pc))
        sends = []
        for j, peer in enumerate(peers):
            cp = pltpu.make_async_remote_copy(src_ref=src, dst_ref=gath.at[me], send_sem=send_sems.at[j],
                                              recv_sem=recv_sems.at[j], device_id=peer, device_id_type=MESH)
            cp.start()
            sends.append(cp)
        for j, (px, py, pc) in enumerate(peers):
            pltpu.make_async_remote_copy(src_ref=src, dst_ref=gath.at[4 * px + 2 * py + pc], send_sem=send_sems.at[j],
                                         recv_sem=recv_sems.at[j], device_id=(px, py, pc), device_id_type=MESH).wait_recv()
        for cp in sends:
            cp.wait_send()
        acc = gath[0]
        for d in range(1, n_dev):
            acc = acc + gath[d]
        out[...] = acc

    vmem = pl.BlockSpec(memory_space=pltpu.VMEM)
    return pl.pallas_call(
        body, name="ar_small", in_specs=[vmem] * (len(names) + 1), out_specs=vmem,
        out_shape=jax.ShapeDtypeStruct(shape, F32),
        scratch_shapes=[pltpu.VMEM(shape, F32), pltpu.VMEM((n_dev,) + shape, F32),
                        pltpu.SemaphoreType.DMA((n_dev - 1,)), pltpu.SemaphoreType.DMA((n_dev - 1,))],
    )(*[dsmall[k] for k in names], loss)


_W_IN_ROWS = SHARD_SHAPES["w_in"][1]


def _shard_rows(name, a):
    return jnp.transpose(a) if name == "w_in" else a.reshape(PACK_ROWS[name], D_MODEL)


def _pack(group, shards, dtype):
    parts = [_shard_rows(n, shards[n]).astype(dtype) for n in group]
    pad = -sum(PACK_ROWS[n] for n in group) % LANES
    if pad:
        parts.append(jnp.zeros((pad, D_MODEL), dtype))
    return jnp.concatenate(parts, axis=0)


def _col_sharded_full(g, name, group):
    r, c = SHARD_SHAPES[name]
    off = _row_offset(group, name)
    blocks = g[:, off:off + PACK_ROWS[name]].reshape(N_CHIPS, r, c)
    return jnp.transpose(blocks, (1, 0, 2)).reshape(r, N_CHIPS * c)


def _col_sharded_blocks(d, name):
    r, c = SHARD_SHAPES[name]
    return jnp.transpose(d.reshape(r, N_CHIPS, c), (1, 0, 2)).reshape(N_CHIPS, PACK_ROWS[name], D_MODEL)


def _weights_a(g):
    dt = g.dtype
    w_in_t = jnp.concatenate([g[c, :_W_IN_ROWS] for c in range(N_CHIPS)]
                             + [jnp.zeros((D_IN_PAD - N_CHIPS * _W_IN_ROWS, D_MODEL), dt)], axis=0)
    wq = _col_sharded_full(g, "w_q_b", GROUP_A).reshape(Q_LORA, N_HEADS_B, Q_HEAD_B)
    wq_p = jnp.concatenate([wq, jnp.zeros((Q_LORA, N_HEADS_B, HEAD_PAD - Q_HEAD_B), dt)], axis=2).reshape(Q_LORA, MLA_W)
    wkv = _col_sharded_full(g, "w_kv_b", GROUP_A).reshape(KV_LORA, N_HEADS_B, QK_NOPE + V_DIM_B)
    zk = jnp.zeros((KV_LORA, N_HEADS_B, HEAD_PAD - QK_NOPE), dt)
    wk_p = jnp.concatenate([wkv[:, :, :QK_NOPE], zk], axis=2).reshape(KV_LORA, MLA_W)
    wv = wkv[:, :, QK_NOPE:].reshape(KV_LORA, N_HEADS_B * V_DIM_B)
    return dict(w_in=w_in_t, wq=wq_p, wk=wk_p, wv=wv, wv_t=jnp.transpose(wv))


def _grad_blocks_a(dw_in_t, dwq_p, dwk_p, dwv):
    dwq = dwq_p.reshape(Q_LORA, N_HEADS_B, HEAD_PAD)[:, :, :Q_HEAD_B].reshape(Q_LORA, N_HEADS_B * Q_HEAD_B)
    dwk = dwk_p.reshape(KV_LORA, N_HEADS_B, HEAD_PAD)[:, :, :QK_NOPE]
    dwkv = jnp.concatenate([dwk, dwv.reshape(KV_LORA, N_HEADS_B, V_DIM_B)], axis=2)
    dwkv = dwkv.reshape(KV_LORA, N_HEADS_B * (QK_NOPE + V_DIM_B))
    pad = -sum(PACK_ROWS[n] for n in GROUP_A) % LANES
    return [dw_in_t.reshape(N_CHIPS, _W_IN_ROWS, D_MODEL), _col_sharded_blocks(dwq, "w_q_b"),
            _col_sharded_blocks(dwkv, "w_kv_b"), jnp.zeros((N_CHIPS, pad, D_MODEL), F32)]


def _rope_freq_lanes():
    freqs = ROPE_THETA ** (-jnp.arange(0, QK_ROPE, 2, dtype=F32) / QK_ROPE)
    return jnp.concatenate([jnp.zeros((QK_NOPE,), F32), freqs, freqs,
                            jnp.zeros((HEAD_PAD - Q_HEAD_B,), F32)]).reshape(1, LANES)


def _fwd_bwd(x, positions, target, w, m, v):
    t = x.shape[0]
    wa = _weights_a(_all_gather_chips(_pack(GROUP_A, w, BF16)))
    posr = positions.astype(F32).reshape(1, t)
    posc = posr.reshape(t, 1)
    freq = _rope_freq_lanes()
    g1, g2, g3, g4 = w["pre_norm_mix"], w["post_norm_mix"], w["pre_norm_mlp"], w["post_norm_mlp"]
    qan, kvan, sinks = w["q_a_norm"], w["kv_a_norm"], w["sinks"]

    h, proj = _proj_fwd(x, g1, wa["w_in"])
    out_a, lse_a = _swa_fwd(proj, posc, posr, sinks)
    qm, km, qt, kt, vt = _mla_prep_fwd(proj, posc, freq, qan, kvan, wa["wq"], wa["wk"], wa["wv_t"])
    out_bt, lse_b, wb = _mla_fwd(km, qt, vt, _pack(GROUP_B, w, BF16))
    w_oa, w_ob = _col_sharded_full(wb, "w_o_a", GROUP_B), _col_sharded_full(wb, "w_o_b", GROUP_B)
    merged, y, x1, h2 = _mix_out_fwd(out_a, out_bt, proj, x, w_oa, w_ob, wb, g2, g3)
    a = _up_fwd(h2, wb)
    dx2, dyd, dg4, loss = _down_fwd_loss(a, wb, x1, target, g4)

    gp_b = _dw_into_blocks(a, dyd, "w_down", 1024, _TK_DW)
    du = _down_bwd(dyd, wb, a)
    gp_b = _dw_into_blocks(h2, du, "w_up", 1024, _TK_DW, gp_b)
    dx1, dy, dg3, dg2 = _up_bwd(du, wb, x1, dx2, y, g3, g2)
    gp_b = _dw_into_blocks(merged, dy, "w_out", 1024, _TK_DW, gp_b)
    dw_oa, dw_ob, dproj, d_out_a, d_out_b, d_out_bt, del_a, del_b = _mix_out_bwd(dy, out_a, out_bt, proj, w_oa, w_ob, wb)
    small_b = jnp.concatenate([_col_sharded_blocks(dw_oa, "w_o_a"), _col_sharded_blocks(dw_ob, "w_o_b")], axis=1)
    gp_b = lax.dynamic_update_slice(gp_b, small_b, (0, _row_offset(GROUP_B, "w_o_a"), 0))
    dqm, dkm, dvm, land_b = _mla_bwd(qm, km, qt, kt, vt, d_out_b, d_out_bt, lse_b, del_b, gp_b)
    chip = (2 * lax.axis_index("x") + lax.axis_index("y")).astype(jnp.int32).reshape(1)
    part_b = _sum4(gp_b, land_b, chip, "rs_sum_b")
    dcq, dckv, dkr, dwq, dwk, dwv, dqan, dkvan, sib_b = _mla_prep_bwd(
        dqm, dkm, dvm, proj, posc, freq, qan, kvan, wa["wq"], wa["wk"], wa["wv"], part_b)
    dqa, dka, dva, dsinks = _swa_bwd(proj, d_out_a, lse_a, del_a, posc, posr, sinks)
    col = 2 * D_MODEL
    for piece in (dqa, dka, dva, dcq, dckv, dkr):
        dproj = lax.dynamic_update_slice(dproj, piece.astype(BF16), (0, col))
        col += piece.shape[1]
    dw_in_t, updated = _dw_in_adamw(dproj, h, [part_b, sib_b], w, m, v)
    parts_a = _grad_blocks_a(dw_in_t, dwq, dwk, dwv)
    gp_a = jnp.concatenate([p.astype(BF16) for p in parts_a], axis=1)
    grad_x, dg1, land_a = _in_bwd(dproj, wa["w_in"], x, dx1, g1, gp_a)

    own_a = jnp.concatenate([lax.dynamic_slice_in_dim(p, chip[0], 1, axis=0) for p in parts_a], axis=1)
    part_a = _sum4(own_a, land_a, jnp.zeros((1,), jnp.int32), "rs_sum_a")
    reduced = {GROUP_A: [part_a, _swap_sibling(part_a, "rs_swap_a")], GROUP_B: [part_b, sib_b]}
    dsmall = dict(pre_norm_mix=dg1, post_norm_mix=dg2, pre_norm_mlp=dg3, post_norm_mlp=dg4,
                  q_a_norm=dqan, kv_a_norm=dkvan, sinks=dsinks)
    return loss, grad_x, reduced, dsmall, updated


def kernel(x, positions, pre_norm_mix, w_in, q_a_norm, w_q_b, kv_a_norm, w_kv_b, sinks, w_o_a, w_o_b, w_out, post_norm_mix, pre_norm_mlp, w_up, w_down, post_norm_mlp, loss_target, m_pre_norm_mix, m_w_in, m_q_a_norm, m_w_q_b, m_kv_a_norm, m_w_kv_b, m_sinks, m_w_o_a, m_w_o_b, m_w_out, m_post_norm_mix, m_pre_norm_mlp, m_w_up, m_w_down, m_post_norm_mlp, v_pre_norm_mix, v_w_in, v_q_a_norm, v_w_q_b, v_kv_a_norm, v_w_kv_b, v_sinks, v_w_o_a, v_w_o_b, v_w_out, v_post_norm_mix, v_pre_norm_mlp, v_w_up, v_w_down, v_post_norm_mlp):
    w = dict(pre_norm_mix=pre_norm_mix, w_in=w_in[0], q_a_norm=q_a_norm, w_q_b=w_q_b[0], kv_a_norm=kv_a_norm,
             w_kv_b=w_kv_b[0], sinks=sinks, w_o_a=w_o_a[0], w_o_b=w_o_b[0], w_out=w_out[0],
             post_norm_mix=post_norm_mix, pre_norm_mlp=pre_norm_mlp, w_up=w_up[0], w_down=w_down[0],
             post_norm_mlp=post_norm_mlp)
    m = dict(pre_norm_mix=m_pre_norm_mix, w_in=m_w_in[0], q_a_norm=m_q_a_norm, w_q_b=m_w_q_b[0],
             kv_a_norm=m_kv_a_norm, w_kv_b=m_w_kv_b[0], sinks=m_sinks, w_o_a=m_w_o_a[0], w_o_b=m_w_o_b[0],
             w_out=m_w_out[0], post_norm_mix=m_post_norm_mix, pre_norm_mlp=m_pre_norm_mlp, w_up=m_w_up[0],
             w_down=m_w_down[0], post_norm_mlp=m_post_norm_mlp)
    v = dict(pre_norm_mix=v_pre_norm_mix, w_in=v_w_in[0], q_a_norm=v_q_a_norm, w_q_b=v_w_q_b[0],
             kv_a_norm=v_kv_a_norm, w_kv_b=v_w_kv_b[0], sinks=v_sinks, w_o_a=v_w_o_a[0], w_o_b=v_w_o_b[0],
             w_out=v_w_out[0], post_norm_mix=v_post_norm_mix, pre_norm_mlp=v_pre_norm_mlp, w_up=v_w_up[0],
             w_down=v_w_down[0], post_norm_mlp=v_post_norm_mlp)

    loss, grad_x, reduced, dsmall, updated = _fwd_bwd(x[0], positions, loss_target[0], w, m, v)

    red = _all_reduce_small(dsmall, loss)
    small = _adamw_small(red, w, m, v)

    big = {}
    tr = jnp.transpose
    big["w_in"] = [tr(o)[None] for o in _adamw(tr(w["w_in"]), reduced[GROUP_A], tr(m["w_in"]), tr(v["w_in"]),
                                               "adamw_w_in", (_W_IN_ROWS, 256))]
    for n in _ADAMW_RIDERS:
        big[n] = [o[None] for o in updated[n]]
    for group, names in ((GROUP_A, ("w_q_b", "w_kv_b")), (GROUP_B, ("w_o_a", "w_o_b"))):
        for n in names:
            off = _row_offset(group, n)
            g_parts = [p[off:off + PACK_ROWS[n]].reshape(SHARD_SHAPES[n]) for p in reduced[group]]
            big[n] = [o[None] for o in _adamw(w[n], g_parts, m[n], v[n], "adamw_" + n, SHARD_SHAPES[n])]

    outs = [big[n][k] if n in big else small[n][k] for k in range(4) for n in WEIGHTS]
    return (red[_LOSS_ROW, 0], grad_x[None], *outs)
```

```python
import jax
import jax.numpy as jnp
from jax import lax
from jax.experimental import pallas as pl
from jax.experimental.pallas import tpu as pltpu

F32 = jnp.float32
BF16 = jnp.bfloat16
MESH = pl.DeviceIdType.MESH

D_MODEL = 1024
N_HEADS_A = 8
N_KV_A = 2
HEAD_DIM_A = 64
WINDOW = 128
BLOCK = 128
N_HEADS_B = 8
QK_NOPE = 64
QK_ROPE = 32
V_DIM_B = 64
Q_LORA = 256
KV_LORA = 128
ROPE_THETA = 10000.0
D_FF = 4 * D_MODEL
EPS = 1e-6
WIDTH_A = N_HEADS_A * HEAD_DIM_A
Q_HEAD_B = QK_NOPE + QK_ROPE
D_IN_PAD = 3328
HEAD_PAD = 128
MLA_W = N_HEADS_B * HEAD_PAD

ADAM_LR = 0.001
ADAM_B1 = 0.9
ADAM_B2 = 0.999
ADAM_EPS = 1e-08
ADAM_WD = 0.01
ADAM_STEP = 10

NEG = -1e30
N_CHIPS = 4
LANES = 128
VMEM_LIMIT = 56 * 1024 * 1024

SHARD_SHAPES = {"w_in": (1024, 808), "w_q_b": (256, 192), "w_kv_b": (128, 256), "w_o_a": (512, 256),
                "w_o_b": (512, 256), "w_out": (256, 1024), "w_up": (1024, 1024), "w_down": (1024, 1024)}
PACK_ROWS = {n: (s[0] * s[1]) // D_MODEL for n, s in SHARD_SHAPES.items()}
GROUP_A = ("w_in", "w_q_b", "w_kv_b")
GROUP_B = ("w_up", "w_down", "w_out", "w_o_a", "w_o_b")
WEIGHTS = ("pre_norm_mix", "w_in", "q_a_norm", "w_q_b", "kv_a_norm", "w_kv_b", "sinks", "w_o_a", "w_o_b", "w_out",
           "post_norm_mix", "pre_norm_mlp", "w_up", "w_down", "post_norm_mlp")


def _params(sem=None):
    return pltpu.CompilerParams(dimension_semantics=sem, vmem_limit_bytes=VMEM_LIMIT)


def _dot(a, b):
    return jnp.dot(a, b, preferred_element_type=F32)


def _dot_nt(a, b):
    return lax.dot_general(a, b, (((1,), (1,)), ((), ())), preferred_element_type=F32)


def _dot_tn(a, b):
    return lax.dot_general(a, b, (((0,), (0,)), ((), ())), preferred_element_type=F32)


def _rms(v):
    return lax.rsqrt(jnp.mean(v * v, axis=-1, keepdims=True) + EPS)


def _norm_bwd(dout, n, r, g):
    dn = dout * g
    dx = r * (dn - n * jnp.mean(dn * n, axis=-1, keepdims=True))
    return dx, jnp.sum(dout * n, axis=0, keepdims=True)


def _full(shape):
    return pl.BlockSpec(shape, lambda *_: (0,) * len(shape))


def _row_offset(group, name):
    return sum(PACK_ROWS[n] for n in group[:group.index(name)])


def _wb_spec(name):
    rows = PACK_ROWS[name]
    return pl.BlockSpec((N_CHIPS, rows, D_MODEL), lambda *_: (0, _row_offset(GROUP_B, name) // rows, 0))


def _proj_fwd(x, g1, w_in_t):
    t = x.shape[0]
    tm = 512

    def body(x_ref, g_ref, w_ref, h_ref, p_ref):
        for rows in _row_halves(tm):
            xv = x_ref[rows, :]
            h = ((xv * _rms(xv)) * g_ref[...]).astype(BF16)
            h_ref[rows, :] = h
            p_ref[rows, :] = _dot_nt(h, w_ref[...])

    return pl.pallas_call(
        body, name="proj_fwd", grid=(t // tm,),
        in_specs=[pl.BlockSpec((tm, D_MODEL), lambda i: (i, 0)), _full((1, D_MODEL)), _full((D_IN_PAD, D_MODEL))],
        out_specs=[pl.BlockSpec((tm, D_MODEL), lambda i: (i, 0)), pl.BlockSpec((tm, D_IN_PAD), lambda i: (i, 0))],
        out_shape=[jax.ShapeDtypeStruct((t, D_MODEL), BF16), jax.ShapeDtypeStruct((t, D_IN_PAD), F32)],
        compiler_params=_params(("parallel",)),
    )(x, g1, w_in_t)


_QA_BLK = 2048 // WIDTH_A
_KA_BLK = 2560 // LANES
_VA_BLK = 2688 // LANES
_CQ_BLK = 2816 // Q_LORA
_CKV_BLK = 3072 // LANES
_KR_BLK = 3200 // LANES


_GROUP_A = N_HEADS_A // N_KV_A
_SWA_SCALE = HEAD_DIM_A ** -0.5
_LOG2E = 1.4426950408889634


def _head_cols(v, h):
    return v[:, HEAD_DIM_A * h:HEAD_DIM_A * (h + 1)]


def _head_rows(v, h):
    return v[HEAD_DIM_A * h:HEAD_DIM_A * (h + 1), :]


def _swa_scores_t(st_g, j, h, dist, valid):
    slope = 2.0 ** (-8.0 * (h + 1) / N_HEADS_A)
    st = st_g[:, BLOCK * j:BLOCK * (j + 1)] * (_SWA_SCALE * _LOG2E) - (slope * _LOG2E) * dist
    return jnp.where(valid, st, NEG)


def _group_t(xt, kh):
    return jnp.concatenate([_head_rows(xt, _GROUP_A * kh + j) for j in range(_GROUP_A)], axis=1).astype(BF16)


_SWA_PER_STEP = 4


def _swa_fwd(proj, posc, posr, sinks):
    t = proj.shape[0]
    span = _SWA_PER_STEP * BLOCK

    def body(q_ref, kc_ref, kp_ref, vc_ref, vp_ref, pq_ref, pc_ref, pp_ref, sink_ref, o_ref, l_ref):
        n = pl.program_id(0)
        k_all = jnp.concatenate([kp_ref[...], kc_ref[...]], axis=0)
        v_all = jnp.concatenate([vp_ref[...], vc_ref[...]], axis=0)
        pos_all = jnp.concatenate([pp_ref[...], pc_ref[...]], axis=0)
        ki = lax.broadcasted_iota(jnp.int32, (2 * BLOCK, BLOCK), 0)
        qi = lax.broadcasted_iota(jnp.int32, (2 * BLOCK, BLOCK), 1)
        window = (ki > qi) & (ki <= qi + WINDOW)
        for sub in range(_SWA_PER_STEP):
            band = slice(BLOCK * sub, BLOCK * (sub + 2))
            own = slice(BLOCK * sub, BLOCK * (sub + 1))
            kb, vb = k_all[band], v_all[band]
            dist = jnp.abs(pos_all[band] - pq_ref[:, own])
            valid = window & ((n > 0) | (ki >= BLOCK)) if sub == 0 else window
            q_t, vb_t = q_ref[own, :].T, vb.T
            out_t, lse = [], []
            for kh in range(N_KV_A):
                st_g = _dot(_head_cols(kb, kh).astype(BF16), _group_t(q_t, kh))
                ps = []
                for j in range(_GROUP_A):
                    h = _GROUP_A * kh + j
                    st = _swa_scores_t(st_g, j, h, dist, valid)
                    sink = sink_ref[0:1, h:h + 1] * _LOG2E
                    m = jnp.maximum(jnp.max(st, axis=0, keepdims=True), sink)
                    e = jnp.exp2(st - m)
                    den = jnp.sum(e, axis=0, keepdims=True) + jnp.exp2(sink - m)
                    ps.append((e * (1.0 / den)).astype(BF16))
                    lse.append(m + jnp.log(den) * _LOG2E)
                o_g = _dot(_head_rows(vb_t, kh).astype(BF16), jnp.concatenate(ps, axis=1))
                out_t.extend(o_g[:, BLOCK * j:BLOCK * (j + 1)] for j in range(_GROUP_A))
            o_ref[own, :] = jnp.concatenate(out_t, axis=0).T
            l_ref[:, own] = jnp.concatenate(lse, axis=0)

    cur = lambda n: (n, 0)
    prev = lambda n: jnp.maximum(_SWA_PER_STEP * n - 1, 0)
    return pl.pallas_call(
        body, name="swa_fwd", grid=(t // span,),
        in_specs=[pl.BlockSpec((span, WIDTH_A), lambda n: (n, _QA_BLK)),
                  pl.BlockSpec((span, LANES), lambda n: (n, _KA_BLK)),
                  pl.BlockSpec((BLOCK, LANES), lambda n: (prev(n), _KA_BLK)),
                  pl.BlockSpec((span, LANES), lambda n: (n, _VA_BLK)),
                  pl.BlockSpec((BLOCK, LANES), lambda n: (prev(n), _VA_BLK)),
                  pl.BlockSpec((1, span), lambda n: (0, n)),
                  pl.BlockSpec((span, 1), cur),
                  pl.BlockSpec((BLOCK, 1), lambda n: (prev(n), 0)),
                  _full((1, N_HEADS_A))],
        out_specs=[pl.BlockSpec((span, WIDTH_A), cur), pl.BlockSpec((N_HEADS_A, span), lambda n: (0, n))],
        out_shape=[jax.ShapeDtypeStruct((t, WIDTH_A), F32), jax.ShapeDtypeStruct((N_HEADS_A, t), F32)],
        compiler_params=_params(("parallel",)),
    )(proj, proj, proj, proj, proj, posr, posc, posc, sinks)


def _rope_coeffs(pos, freq):
    ang = pos * freq
    cosv, sinv = jnp.cos(ang), jnp.sin(ang)
    lane = lax.broadcasted_iota(jnp.int32, ang.shape, 1)
    lo = (lane >= QK_NOPE) & (lane < QK_NOPE + QK_ROPE // 2)
    hi = (lane >= QK_NOPE + QK_ROPE // 2) & (lane < QK_NOPE + QK_ROPE)
    c = jnp.where(lane < QK_NOPE, 1.0, jnp.where(lo | hi, cosv, 0.0))
    s = jnp.where(lo, -sinv, jnp.where(hi, sinv, 0.0))
    return c, s, lo, hi


def _rope(xh, c, s, lo):
    up = pltpu.roll(xh, LANES - QK_ROPE // 2, axis=1)
    dn = pltpu.roll(xh, QK_ROPE // 2, axis=1)
    return xh * c + jnp.where(lo, up, dn) * s


def _unrope(dh, c, s, lo, hi):
    g = dh * s
    up = pltpu.roll(g, LANES - QK_ROPE // 2, axis=1)
    dn = pltpu.roll(g, QK_ROPE // 2, axis=1)
    return dh * c + jnp.where(hi, dn, jnp.where(lo, up, 0.0))


_TQ = 512
_MLA_SCALE = Q_HEAD_B ** -0.5


def _mla_prep_fwd(proj, posc, freq, qan, kvan, wq, wk, wv):
    t = proj.shape[0]
    tm = _TQ
    nb = t // tm

    def body(cq_ref, ckv_ref, kr_ref, pos_ref, f_ref, qan_ref, kvan_ref, wq_ref, wk_ref, wv_ref,
             q_ref, k_ref, qt_ref, kt_ref, vt_ref):
        cq = cq_ref[...]
        cqn = ((cq * _rms(cq)) * qan_ref[...]).astype(BF16)
        ckv = ckv_ref[...]
        ckvn = ((ckv * _rms(ckv)) * kvan_ref[...]).astype(BF16)
        qb = _dot(cqn, wq_ref[...])
        kb = _dot(ckvn, wk_ref[...])
        vbt = _dot_nt(wv_ref[...], ckvn)
        c, s, lo, _ = _rope_coeffs(pos_ref[...], f_ref[...])
        kr = _rope(pltpu.roll(kr_ref[...], QK_NOPE, axis=1), c, s, lo)
        for h in range(N_HEADS_B):
            sl = slice(HEAD_PAD * h, HEAD_PAD * (h + 1))
            q_h = _rope(qb[:, sl], c, s, lo)
            k_h = kb[:, sl] + kr
            q_ref[:, sl] = q_h.astype(BF16)
            k_ref[:, sl] = k_h.astype(BF16)
            qt_ref[h, 0] = q_h.T.astype(BF16)
            kt_ref[h, 0] = k_h.T.astype(BF16)
            vt_ref[h, 0] = vbt[V_DIM_B * h:V_DIM_B * (h + 1), :].astype(BF16)

    row = lambda i: (i, 0)
    blk4 = lambda d: pl.BlockSpec((N_HEADS_B, 1, d, tm), lambda i: (0, i, 0, 0))
    return pl.pallas_call(
        body, name="mla_prep_fwd", grid=(nb,),
        in_specs=[pl.BlockSpec((tm, Q_LORA), lambda i: (i, _CQ_BLK)),
                  pl.BlockSpec((tm, LANES), lambda i: (i, _CKV_BLK)),
                  pl.BlockSpec((tm, LANES), lambda i: (i, _KR_BLK)),
                  pl.BlockSpec((tm, 1), row), _full((1, LANES)), _full((1, Q_LORA)), _full((1, KV_LORA)),
                  _full((Q_LORA, MLA_W)), _full((KV_LORA, MLA_W)), _full((N_HEADS_B * V_DIM_B, KV_LORA))],
        out_specs=[pl.BlockSpec((tm, MLA_W), row), pl.BlockSpec((tm, MLA_W), row), blk4(HEAD_PAD), blk4(HEAD_PAD),
                   blk4(V_DIM_B)],
        out_shape=[jax.ShapeDtypeStruct((t, MLA_W), BF16), jax.ShapeDtypeStruct((t, MLA_W), BF16),
                   jax.ShapeDtypeStruct((N_HEADS_B, nb, HEAD_PAD, tm), BF16),
                   jax.ShapeDtypeStruct((N_HEADS_B, nb, HEAD_PAD, tm), BF16),
                   jax.ShapeDtypeStruct((N_HEADS_B, nb, V_DIM_B, tm), BF16)],
        compiler_params=_params(("parallel",)),
    )(proj, proj, proj, posc, freq, qan, kvan, wq, wk, wv)


_MLA_SCALE2 = _MLA_SCALE * _LOG2E


def _mla_fwd(k, qt, vt, w_src):
    t = k.shape[0]
    nb = t // _TQ
    pairs = min(2, nb // 2)
    groups = nb // (2 * pairs)

    def body(k_ref, qt_ref, vt_ref, w_ref, o_ref, l_ref, wg_ref, raw, send_sems, recv_sems, local_sem):
        first = (pl.program_id(0) == 0) & (pl.program_id(1) == 0)
        last = (pl.program_id(0) == N_HEADS_B - 1) & (pl.program_id(1) == groups - 1)

        @pl.when(first)
        def _():
            _gather_start(w_ref, wg_ref, send_sems, recv_sems, local_sem)

        for pair in range(pairs):
            one_pair(k_ref, qt_ref, vt_ref, o_ref, l_ref, raw, pairs * pl.program_id(1) + pair, 2 * pair)

        @pl.when(last)
        def _():
            _gather_wait(w_ref, wg_ref, send_sems, recv_sems, local_sem)

    def one_pair(k_ref, qt_ref, vt_ref, o_ref, l_ref, raw, g, at):
        def keys(kj):
            return k_ref[pl.ds(pl.multiple_of(kj * _TQ, _TQ), _TQ), :]

        def products(kj, slot):
            kv = keys(kj)
            raw[slot, 0] = _dot(kv, qt_ref[0, at])
            raw[slot, 1] = _dot(kv, qt_ref[0, at + 1])

        def update(stats, raw_ref, kj, diagonal=False):
            m, l, acc = stats
            scores = raw_ref[...]
            if diagonal:
                key = lax.broadcasted_iota(jnp.int32, scores.shape, 0)
                qry = lax.broadcasted_iota(jnp.int32, scores.shape, 1)
                scores = jnp.where(key <= qry, scores, NEG)
            m_new = jnp.maximum(m, jnp.max(scores, axis=0, keepdims=True) * _MLA_SCALE2)
            alpha = jnp.exp2(m - m_new)
            p = jnp.exp2(scores * _MLA_SCALE2 - m_new).astype(BF16)
            pv = _dot(jnp.concatenate([vt_ref[0, kj], jnp.ones((16, _TQ), BF16)], axis=0), p)
            return m_new, alpha * l + pv[V_DIM_B:V_DIM_B + 8], alpha * acc + pv[:V_DIM_B]

        def trip(i, stats):
            sa, sb = stats
            products(2 * i + 1, 1)
            sa, sb = update(sa, raw.at[0, 0], 2 * i), update(sb, raw.at[0, 1], 2 * i)
            products(2 * i + 2, 0)
            return update(sa, raw.at[1, 0], 2 * i + 1), update(sb, raw.at[1, 1], 2 * i + 1)

        init = (jnp.full((1, _TQ), NEG, F32), jnp.zeros((8, _TQ), F32), jnp.zeros((V_DIM_B, _TQ), F32))
        products(0, 0)
        sa, sb = lax.fori_loop(0, g, trip, (init, init))
        raw[1, 1] = _dot(keys(2 * g + 1), qt_ref[0, at + 1])
        sa = update(sa, raw.at[0, 0], 2 * g, True)
        sb = update(update(sb, raw.at[0, 1], 2 * g), raw.at[1, 1], 2 * g + 1, True)
        for which, (m, l, acc) in enumerate((sa, sb)):
            o_ref[0, at + which] = acc / l[0:1]
            l_ref[0, at + which] = m + jnp.log(l[0:1]) * _LOG2E

    two = lambda d: pl.BlockSpec((1, 2 * pairs, d, _TQ), lambda h, g: (h, g, 0, 0))
    return pl.pallas_call(
        body, name="mla_fwd", grid=(N_HEADS_B, groups),
        in_specs=[pl.BlockSpec((t, HEAD_PAD), lambda h, g: (0, h)), two(HEAD_PAD),
                  pl.BlockSpec((1, nb, V_DIM_B, _TQ), lambda h, g: (h, 0, 0, 0)), _HBM],
        out_specs=[two(V_DIM_B), two(1), _HBM],
        out_shape=[jax.ShapeDtypeStruct((N_HEADS_B, nb, V_DIM_B, _TQ), F32),
                   jax.ShapeDtypeStruct((N_HEADS_B, nb, 1, _TQ), F32),
                   jax.ShapeDtypeStruct((N_CHIPS,) + w_src.shape, w_src.dtype)],
        scratch_shapes=[pltpu.VMEM((2, 2, _TQ, _TQ), F32),
                        pltpu.SemaphoreType.DMA((3,)), pltpu.SemaphoreType.DMA((3,)), pltpu.SemaphoreType.DMA(())],
        compiler_params=_params(("arbitrary", "arbitrary")),
    )(k, qt, vt, w_src)


def _ot_spec(tm, d):
    per = _TQ // tm
    return pl.BlockSpec((N_HEADS_B, 1, d, tm), lambda i: (0, i // per, 0, i % per))


def _mix_out_fwd(out_a, out_bt, proj, x, w_oa, w_ob, wb, g2, g3):
    t = x.shape[0]
    tm = 512

    def body(oa_ref, obt_ref, ga_ref, gb_ref, x_ref, woa_ref, wob_ref, wout_ref, g2_ref, g3_ref,
             mg_ref, y_ref, x1_ref, h2_ref):
        oa = _dot(oa_ref[...].astype(BF16), woa_ref[...])
        obt = obt_ref[...].reshape(N_HEADS_B * V_DIM_B, tm).astype(BF16)
        ob = _dot_tn(obt, wob_ref[...])
        merged = (jax.nn.sigmoid(ga_ref[...]) * oa + jax.nn.sigmoid(gb_ref[...]) * ob).astype(BF16)
        mg_ref[...] = merged
        y = _dot(merged, wout_ref[...].reshape(D_MODEL, D_MODEL))
        y_ref[...] = y
        x1 = x_ref[...] + (y * _rms(y)) * g2_ref[...]
        x1_ref[...] = x1
        h2_ref[...] = ((x1 * _rms(x1)) * g3_ref[...]).astype(BF16)

    row = lambda i: (i, 0)
    blk = pl.BlockSpec((tm, D_MODEL), row)
    return pl.pallas_call(
        body, name="mix_out_fwd", grid=(t // tm,),
        in_specs=[pl.BlockSpec((tm, WIDTH_A), row), _ot_spec(tm, V_DIM_B), pl.BlockSpec((tm, D_MODEL), lambda i: (i, 0)),
                  pl.BlockSpec((tm, D_MODEL), lambda i: (i, 1)), blk,
                  _full((WIDTH_A, D_MODEL)), _full((N_HEADS_B * V_DIM_B, D_MODEL)), _wb_spec("w_out"),
                  _full((1, D_MODEL)), _full((1, D_MODEL))],
        out_specs=[blk, blk, blk, blk],
        out_shape=[jax.ShapeDtypeStruct((t, D_MODEL), BF16), jax.ShapeDtypeStruct((t, D_MODEL), F32),
                   jax.ShapeDtypeStruct((t, D_MODEL), F32), jax.ShapeDtypeStruct((t, D_MODEL), BF16)],
        compiler_params=_params(("parallel",)),
    )(out_a, out_bt, proj, proj, x, w_oa, w_ob, wb, g2, g3)


_TM_MLP = 512


def _row_halves(tm):
    return slice(0, tm // 2), slice(tm // 2, tm)


def _up_fwd(h2, wb):
    t = h2.shape[0]
    tm = _TM_MLP

    def body(h_ref, w_ref, a_ref):
        hv = h_ref[...]
        for j in range(N_CHIPS):
            u = _dot(hv, w_ref[j])
            a_ref[:, D_MODEL * j:D_MODEL * (j + 1)] = jnp.square(jnp.maximum(u, 0.0)).astype(BF16)

    return pl.pallas_call(
        body, name="up_fwd", grid=(t // tm,),
        in_specs=[pl.BlockSpec((tm, D_MODEL), lambda i: (i, 0)), _wb_spec("w_up")],
        out_specs=pl.BlockSpec((tm, D_FF), lambda i: (i, 0)),
        out_shape=jax.ShapeDtypeStruct((t, D_FF), BF16),
        compiler_params=_params(("parallel",)),
    )(h2, wb)


def _down_fwd_loss(a, wb, x1, target, g4):
    t = a.shape[0]
    tm = _TM_MLP

    def body(a_ref, w_ref, x1_ref, tg_ref, g_ref, dx2_ref, dyd_ref, dg_ref, loss_ref):
        @pl.when(pl.program_id(0) == 0)
        def _():
            dg_ref[...] = jnp.zeros(dg_ref.shape, F32)
            loss_ref[...] = jnp.zeros(loss_ref.shape, F32)

        yd = _dot(a_ref[...], w_ref[...].reshape(D_FF, D_MODEL))
        r = _rms(yd)
        n = yd * r
        diff = (x1_ref[...] + n * g_ref[...]) - tg_ref[...]
        loss_ref[...] += 0.5 * jnp.sum(jnp.mean(diff * diff, axis=-1, keepdims=True), axis=0, keepdims=True)
        dx2 = diff * (1.0 / D_MODEL)
        dx2_ref[...] = dx2
        dyd, dg = _norm_bwd(dx2, n, r, g_ref[...])
        dyd_ref[...] = dyd.astype(BF16)
        dg_ref[...] += dg

    row = lambda i: (i, 0)
    blk = pl.BlockSpec((tm, D_MODEL), row)
    return pl.pallas_call(
        body, name="down_fwd_loss", grid=(t // tm,),
        in_specs=[pl.BlockSpec((tm, D_FF), row), _wb_spec("w_down"), blk, blk, _full((1, D_MODEL))],
        out_specs=[blk, blk, _full((1, D_MODEL)), _full((1, LANES))],
        out_shape=[jax.ShapeDtypeStruct((t, D_MODEL), F32), jax.ShapeDtypeStruct((t, D_MODEL), BF16),
                   jax.ShapeDtypeStruct((1, D_MODEL), F32), jax.ShapeDtypeStruct((1, LANES), F32)],
        compiler_params=_params(("arbitrary",)),
    )(a, wb, x1, target, g4)


def _matmul_tn(a, b, name, tm, tn, tk=1024):
    t, m = a.shape
    n = b.shape[1]
    tk = min(tk, t)
    nk = t // tk

    def body(a_ref, b_ref, o_ref):
        @pl.when(pl.program_id(2) == 0)
        def _():
            o_ref[...] = jnp.zeros(o_ref.shape, F32)

        o_ref[...] += _dot_tn(a_ref[...].astype(BF16), b_ref[...].astype(BF16))

    return pl.pallas_call(
        body, name=name, grid=(m // tm, n // tn, nk),
        in_specs=[pl.BlockSpec((tk, tm), lambda i, j, k: (k, i)), pl.BlockSpec((tk, tn), lambda i, j, k: (k, j))],
        out_specs=pl.BlockSpec((tm, tn), lambda i, j, k: (i, j)),
        out_shape=jax.ShapeDtypeStruct((m, n), F32),
        compiler_params=_params(("parallel", "parallel", "arbitrary")),
    )(a, b)


_TK_DW = 2048


def _dw_into_blocks(a, b, weight, tm, tk, buf=None):
    t, m = a.shape
    n = b.shape[1]
    tk = min(tk, t)
    nk = t // tk
    rows = PACK_ROWS[weight]
    br = min(tm, rows)
    chips = tm // br
    first = _row_offset(GROUP_B, weight) // br
    per_chip = rows // br
    if weight == "w_up":
        out_map = lambda i, j, k: (j, first + i, 0)
    elif chips > 1:
        out_map = lambda i, j, k: (i, first, 0)
    else:
        out_map = lambda i, j, k: (i // per_chip, first + i % per_chip, 0)

    def body(a_ref, b_ref, *rest):
        o_ref = rest[-1]

        @pl.when(pl.program_id(2) == 0)
        def _():
            o_ref[...] = jnp.zeros(o_ref.shape, F32)

        o_ref[...] += _dot_tn(a_ref[...].astype(BF16), b_ref[...].astype(BF16)).reshape(o_ref.shape)

    in_specs = [pl.BlockSpec((tk, tm), lambda i, j, k: (k, i)), pl.BlockSpec((tk, D_MODEL), lambda i, j, k: (k, j))]
    operands = [a, b]
    if buf is not None:
        in_specs.append(pl.BlockSpec(memory_space=pl.ANY))
        operands.append(buf)
    total = sum(PACK_ROWS[w] for w in GROUP_B)
    return pl.pallas_call(
        body, name="dw_" + weight[2:], grid=(m // tm, n // D_MODEL, nk),
        in_specs=in_specs, out_specs=pl.BlockSpec((chips, br, D_MODEL), out_map),
        out_shape=jax.ShapeDtypeStruct((N_CHIPS, total, D_MODEL), F32),
        input_output_aliases={} if buf is None else {2: 0},
        compiler_params=_params(("parallel", "parallel", "arbitrary")),
    )(*operands)


def _down_bwd(dyd, wb, a):
    t = dyd.shape[0]
    tm = _TM_MLP

    def body(d_ref, w_ref, a_ref, du_ref):
        dv = d_ref[...]
        for j in range(N_CHIPS):
            cols = slice(D_MODEL * j, D_MODEL * (j + 1))
            av = a_ref[:, cols].astype(F32)
            relu_u = jnp.where(av > 0.0, av * lax.rsqrt(av), 0.0)
            du_ref[:, cols] = (_dot_nt(dv, w_ref[j]) * (2.0 * relu_u)).astype(BF16)

    row = lambda i: (i, 0)
    return pl.pallas_call(
        body, name="down_bwd", grid=(t // tm,),
        in_specs=[pl.BlockSpec((tm, D_MODEL), row), _wb_spec("w_down"), pl.BlockSpec((tm, D_FF), row)],
        out_specs=pl.BlockSpec((tm, D_FF), row),
        out_shape=jax.ShapeDtypeStruct((t, D_FF), BF16),
        compiler_params=_params(("parallel",)),
    )(dyd, wb, a)


def _up_bwd(du, wb, x1, dx2, y, g3, g2):
    t = du.shape[0]
    tm = _TM_MLP

    def body(du_ref, w_ref, x1_ref, dx2_ref, y_ref, g3_ref, g2_ref, dx1_ref, dy_ref, dg3_ref, dg2_ref):
        @pl.when(pl.program_id(0) == 0)
        def _():
            dg3_ref[...] = jnp.zeros(dg3_ref.shape, F32)
            dg2_ref[...] = jnp.zeros(dg2_ref.shape, F32)

        dh2 = _dot_nt(du_ref[:, 0:D_MODEL], w_ref[0])
        for j in range(1, N_CHIPS):
            dh2 = dh2 + _dot_nt(du_ref[:, D_MODEL * j:D_MODEL * (j + 1)], w_ref[j])
        x1 = x1_ref[...]
        r3 = _rms(x1)
        d3, dg3 = _norm_bwd(dh2, x1 * r3, r3, g3_ref[...])
        dx1 = dx2_ref[...] + d3
        dx1_ref[...] = dx1
        dg3_ref[...] += dg3
        y = y_ref[...]
        r2 = _rms(y)
        dy, dg2 = _norm_bwd(dx1, y * r2, r2, g2_ref[...])
        dy_ref[...] = dy.astype(BF16)
        dg2_ref[...] += dg2

    row = lambda i: (i, 0)
    blk = pl.BlockSpec((tm, D_MODEL), row)
    return pl.pallas_call(
        body, name="up_bwd", grid=(t // tm,),
        in_specs=[pl.BlockSpec((tm, D_FF), row), _wb_spec("w_up"),
                  blk, blk, blk, _full((1, D_MODEL)), _full((1, D_MODEL))],
        out_specs=[blk, blk, _full((1, D_MODEL)), _full((1, D_MODEL))],
        out_shape=[jax.ShapeDtypeStruct((t, D_MODEL), F32), jax.ShapeDtypeStruct((t, D_MODEL), BF16),
                   jax.ShapeDtypeStruct((1, D_MODEL), F32), jax.ShapeDtypeStruct((1, D_MODEL), F32)],
        compiler_params=_params(("arbitrary",)),
    )(du, wb, x1, dx2, y, g3, g2)


def _mix_out_bwd(dy, out_a, out_bt, proj, w_oa, w_ob, wb):
    t = dy.shape[0]
    tm = 512
    nb = t // _TQ

    def body(dy_ref, oa_ref, obt_ref, ga_ref, gb_ref, woa_ref, wob_ref, wout_ref,
             dwoa_ref, dwob_ref, dg_ref, da_ref, db_ref, dbt_ref, dela_ref, delb_ref):
        @pl.when(pl.program_id(0) == 0)
        def _():
            dwoa_ref[...] = jnp.zeros(dwoa_ref.shape, F32)
            dwob_ref[...] = jnp.zeros(dwob_ref.shape, F32)

        dm = _dot_nt(dy_ref[...], wout_ref[...].reshape(D_MODEL, D_MODEL))
        out_a_v = oa_ref[...]
        out_bt_v = obt_ref[...].reshape(N_HEADS_B * V_DIM_B, tm)
        oa = _dot(out_a_v.astype(BF16), woa_ref[...])
        ob = _dot_tn(out_bt_v.astype(BF16), wob_ref[...])
        sa, sb = jax.nn.sigmoid(ga_ref[...]), jax.nn.sigmoid(gb_ref[...])
        doa = (dm * sa).astype(BF16)
        dob = (dm * sb).astype(BF16)
        dwoa_ref[...] += _dot_tn(out_a_v.astype(BF16), doa)
        dwob_ref[...] += _dot(out_bt_v.astype(BF16), dob)
        dg_ref[:, :D_MODEL] = (dm * oa * (sa * (1.0 - sa))).astype(BF16)
        dg_ref[:, D_MODEL:] = (dm * ob * (sb * (1.0 - sb))).astype(BF16)
        d_out_a = _dot_nt(doa, woa_ref[...])
        da_ref[...] = d_out_a
        prod_at = (d_out_a * out_a_v).T
        dela_ref[...] = jnp.concatenate(
            [jnp.sum(_head_rows(prod_at, h), axis=0, keepdims=True) for h in range(N_HEADS_A)], axis=0)
        d_out_b = _dot_nt(dob, wob_ref[...])
        d_out_bt = _dot_nt(wob_ref[...], dob)
        prod_bt = d_out_bt * out_bt_v
        for h in range(N_HEADS_B):
            db_ref[h] = d_out_b[:, V_DIM_B * h:V_DIM_B * (h + 1)].astype(BF16)
            dbt_ref[h, 0] = d_out_bt[V_DIM_B * h:V_DIM_B * (h + 1), :].astype(BF16)
            delb_ref[h, 0] = jnp.sum(prod_bt[V_DIM_B * h:V_DIM_B * (h + 1), :], axis=0, keepdims=True)

    row = lambda i: (i, 0)
    blk = pl.BlockSpec((tm, D_MODEL), row)
    return pl.pallas_call(
        body, name="mix_out_bwd", grid=(t // tm,),
        in_specs=[blk, pl.BlockSpec((tm, WIDTH_A), row), _ot_spec(tm, V_DIM_B),
                  pl.BlockSpec((tm, D_MODEL), lambda i: (i, 0)), pl.BlockSpec((tm, D_MODEL), lambda i: (i, 1)),
                  _full((WIDTH_A, D_MODEL)), _full((N_HEADS_B * V_DIM_B, D_MODEL)), _wb_spec("w_out")],
        out_specs=[_full((WIDTH_A, D_MODEL)), _full((N_HEADS_B * V_DIM_B, D_MODEL)),
                   pl.BlockSpec((tm, 2 * D_MODEL), row), pl.BlockSpec((tm, WIDTH_A), row),
                   pl.BlockSpec((N_HEADS_B, tm, V_DIM_B), lambda i: (0, i, 0)), _ot_spec(tm, V_DIM_B),
                   pl.BlockSpec((N_HEADS_A, tm), lambda i: (0, i)), _ot_spec(tm, 1)],
        out_shape=[jax.ShapeDtypeStruct((WIDTH_A, D_MODEL), F32),
                   jax.ShapeDtypeStruct((N_HEADS_B * V_DIM_B, D_MODEL), F32),
                   jax.ShapeDtypeStruct((t, D_IN_PAD), BF16), jax.ShapeDtypeStruct((t, WIDTH_A), F32),
                   jax.ShapeDtypeStruct((N_HEADS_B, t, V_DIM_B), BF16),
                   jax.ShapeDtypeStruct((N_HEADS_B, nb, V_DIM_B, _TQ), BF16), jax.ShapeDtypeStruct((N_HEADS_A, t), F32),
                   jax.ShapeDtypeStruct((N_HEADS_B, nb, 1, _TQ), F32)],
        compiler_params=_params(("arbitrary",)),
    )(dy, out_a, out_bt, proj, proj, w_oa, w_ob, wb)


def _mla_bwd(q, k, qt, kt, vt, d_out, d_out_t, lse, delta, gp):
    t = q.shape[0]
    nb = t // _TQ

    def body(k_ref, kt_ref, vt_ref, q_ref, qt_ref, do_ref, dot_ref, lrow_ref, drow_ref, gp_ref,
             dq_ref, dkt_ref, dvt_ref, land_ref, l_rep, d_rep, send_sems, recv_sems):
        step = pl.program_id(1)
        kj = nb - 1 - step

        @pl.when((pl.program_id(0) == 0) & (step == 0))
        def _():
            _scatter_start(gp_ref, land_ref, send_sems, recv_sems)

        @pl.when(step == 0)
        def _():
            dq_ref[...] = jnp.zeros(dq_ref.shape, F32)
            for b in range(nb):
                l_rep[_TQ * b:_TQ * (b + 1), :] = jnp.broadcast_to(lrow_ref[0, b], (LANES, _TQ)).T
                d_rep[_TQ * b:_TQ * (b + 1), :] = jnp.broadcast_to(drow_ref[0, b], (LANES, _TQ)).T

        kv, k_t, v_t = k_ref[...], kt_ref[0, 0], vt_ref[0, 0]

        def rows_of(qi):
            return pl.ds(pl.multiple_of(qi * _TQ, _TQ), _TQ)

        def products(qi, diagonal=False):
            s = _dot(q_ref[rows_of(qi), :], k_t) * _MLA_SCALE2
            if diagonal:
                qry = lax.broadcasted_iota(jnp.int32, s.shape, 0)
                key = lax.broadcasted_iota(jnp.int32, s.shape, 1)
                s = jnp.where(key <= qry, s, NEG)
            return s, _dot(do_ref[0, rows_of(qi), :], v_t)

        def update(carry, prods, qi):
            dkt, dvt = carry
            s, dp = prods
            lse, delta = l_rep[rows_of(qi), :], d_rep[rows_of(qi), :]
            ps, dss = [], []
            for c in range(_TQ // LANES):
                strip = slice(LANES * c, LANES * (c + 1))
                p = jnp.exp2(s[:, strip] - lse)
                ps.append(p.astype(BF16))
                dss.append((p * (dp[:, strip] - delta) * _MLA_SCALE).astype(BF16))
            p_b, ds_b = jnp.concatenate(ps, axis=1), jnp.concatenate(dss, axis=1)
            dvt = dvt + _dot(dot_ref[0, qi], p_b)
            dkt = dkt + _dot(qt_ref[0, qi], ds_b)
            dq_ref[rows_of(qi), :] += _dot(ds_b, kv)
            return dkt, dvt

        def pair(i, carry):
            qa = kj + 1 + 2 * i
            pa, pb = products(qa), products(qa + 1)
            return update(update(carry, pa, qa), pb, qa + 1)

        init = (jnp.zeros((HEAD_PAD, _TQ), F32), jnp.zeros((V_DIM_B, _TQ), F32))
        carry = update(init, products(kj, True), kj)
        pairs = (nb - 1 - kj) // 2
        carry = lax.fori_loop(0, pairs, pair, carry)
        dkt, dvt = lax.fori_loop(kj + 1 + 2 * pairs, nb, lambda qi, cr: update(cr, products(qi), qi), carry)
        dkt_ref[0, 0] = dkt
        dvt_ref[0, 0] = dvt

        @pl.when((pl.program_id(0) == N_HEADS_B - 1) & (step == nb - 1))
        def _():
            _scatter_wait(gp_ref, land_ref, send_sems, recv_sems)

    head4 = lambda d: pl.BlockSpec((1, nb, d, _TQ), lambda h, s: (h, 0, 0, 0))
    blk4 = lambda d: pl.BlockSpec((1, 1, d, _TQ), lambda h, s: (h, nb - 1 - s, 0, 0))
    head3 = lambda d: pl.BlockSpec((1, t, d), lambda h, kj: (h, 0, 0))
    per_head = pl.BlockSpec((t, HEAD_PAD), lambda h, kj: (0, h))
    return pl.pallas_call(
        body, name="mla_bwd", grid=(N_HEADS_B, nb),
        in_specs=[pl.BlockSpec((_TQ, HEAD_PAD), lambda h, s: (nb - 1 - s, h)), blk4(HEAD_PAD), blk4(V_DIM_B),
                  per_head, head4(HEAD_PAD), head3(V_DIM_B), head4(V_DIM_B), head4(1), head4(1), _HBM],
        out_specs=[per_head, blk4(HEAD_PAD), blk4(V_DIM_B), _HBM],
        out_shape=[jax.ShapeDtypeStruct((t, MLA_W), F32), jax.ShapeDtypeStruct((N_HEADS_B, nb, HEAD_PAD, _TQ), F32),
                   jax.ShapeDtypeStruct((N_HEADS_B, nb, V_DIM_B, _TQ), F32),
                   jax.ShapeDtypeStruct((3,) + gp.shape[1:], gp.dtype)],
        scratch_shapes=[pltpu.VMEM((t, LANES), F32), pltpu.VMEM((t, LANES), F32),
                        pltpu.SemaphoreType.DMA((3,)), pltpu.SemaphoreType.DMA((3,))],
        compiler_params=_params(("arbitrary", "arbitrary")),
    )(k, kt, vt, q, qt, d_out, d_out_t, lse, delta, gp)


def _mla_prep_bwd(dq, dkt, dvt, proj, posc, freq, qan, kvan, wq, wk, wv, swap_src):
    t = dq.shape[0]
    tm = _TQ

    def body(dq_ref, dkt_ref, dvt_ref, cq_ref, ckv_ref, pos_ref, f_ref, qan_ref, kvan_ref, wq_ref, wk_ref, wv_ref, src_ref,
             dcq_ref, dckv_ref, dkr_ref, dwq_ref, dwk_ref, dwv_ref, dqan_ref, dkvan_ref, got_ref, send_sem, recv_sem):
        swap = _sibling_copy(src_ref, got_ref, send_sem, recv_sem)

        @pl.when(pl.program_id(0) == 0)
        def _():
            swap.start()
            for r in (dwq_ref, dwk_ref, dwv_ref, dqan_ref, dkvan_ref):
                r[...] = jnp.zeros(r.shape, F32)

        cq = cq_ref[...]
        rq = _rms(cq)
        nq_ = cq * rq
        cqn = (nq_ * qan_ref[...]).astype(BF16)
        ckv = ckv_ref[...]
        rkv = _rms(ckv)
        nkv = ckv * rkv
        ckvn = (nkv * kvan_ref[...]).astype(BF16)
        c, s, lo, hi = _rope_coeffs(pos_ref[...], f_ref[...])
        dkr = jnp.zeros((tm, LANES), F32)
        dqb, dkb = [], []
        for h in range(N_HEADS_B):
            dqb.append(_unrope(dq_ref[:, HEAD_PAD * h:HEAD_PAD * (h + 1)], c, s, lo, hi).astype(BF16))
            dk_h = dkt_ref[h, 0].T
            dkr = dkr + dk_h
            dkb.append(dk_h.astype(BF16))
        dqb, dkb = jnp.concatenate(dqb, axis=1), jnp.concatenate(dkb, axis=1)
        dkr = jnp.where(lo | hi, _unrope(dkr, c, s, lo, hi), 0.0)
        dkr_ref[...] = pltpu.roll(dkr, LANES - QK_NOPE, axis=1).astype(BF16)
        dvb = dvt_ref[...].reshape(N_HEADS_B * V_DIM_B, tm).T.astype(BF16)
        dwq_ref[...] += _dot_tn(cqn, dqb)
        dwk_ref[...] += _dot_tn(ckvn, dkb)
        dwv_ref[...] += _dot_tn(ckvn, dvb)
        dcqn = _dot_nt(dqb, wq_ref[...])
        dckvn = _dot_nt(dkb, wk_ref[...]) + _dot_nt(dvb, wv_ref[...])
        dcq, dqan = _norm_bwd(dcqn, nq_, rq, qan_ref[...])
        dckv, dkvan = _norm_bwd(dckvn, nkv, rkv, kvan_ref[...])
        dcq_ref[...] = dcq.astype(BF16)
        dckv_ref[...] = dckv.astype(BF16)
        dqan_ref[...] += dqan
        dkvan_ref[...] += dkvan

        @pl.when(pl.program_id(0) == t // tm - 1)
        def _():
            swap.wait_recv()
            swap.wait_send()

    row = lambda i: (i, 0)
    vw = N_HEADS_B * V_DIM_B
    return pl.pallas_call(
        body, name="mla_prep_bwd", grid=(t // tm,),
        in_specs=[pl.BlockSpec((tm, MLA_W), row), pl.BlockSpec((N_HEADS_B, 1, HEAD_PAD, tm), lambda i: (0, i, 0, 0)),
                  pl.BlockSpec((N_HEADS_B, 1, V_DIM_B, tm), lambda i: (0, i, 0, 0)),
                  pl.BlockSpec((tm, Q_LORA), lambda i: (i, _CQ_BLK)),
                  pl.BlockSpec((tm, LANES), lambda i: (i, _CKV_BLK)),
                  pl.BlockSpec((tm, 1), row), _full((1, LANES)), _full((1, Q_LORA)), _full((1, KV_LORA)),
                  _full((Q_LORA, MLA_W)), _full((KV_LORA, MLA_W)), _full((KV_LORA, vw)), _HBM],
        out_specs=[pl.BlockSpec((tm, Q_LORA), row), pl.BlockSpec((tm, LANES), row), pl.BlockSpec((tm, LANES), row),
                   _full((Q_LORA, MLA_W)), _full((KV_LORA, MLA_W)), _full((KV_LORA, vw)),
                   _full((1, Q_LORA)), _full((1, KV_LORA)), _HBM],
        out_shape=[jax.ShapeDtypeStruct((t, Q_LORA), BF16), jax.ShapeDtypeStruct((t, LANES), BF16),
                   jax.ShapeDtypeStruct((t, LANES), BF16),
                   jax.ShapeDtypeStruct((Q_LORA, MLA_W), F32), jax.ShapeDtypeStruct((KV_LORA, MLA_W), F32),
                   jax.ShapeDtypeStruct((KV_LORA, vw), F32),
                   jax.ShapeDtypeStruct((1, Q_LORA), F32), jax.ShapeDtypeStruct((1, KV_LORA), F32),
                   jax.ShapeDtypeStruct(swap_src.shape, swap_src.dtype)],
        scratch_shapes=[pltpu.SemaphoreType.DMA(()), pltpu.SemaphoreType.DMA(())],
        compiler_params=_params(("arbitrary",)),
    )(dq, dkt, dvt, proj, proj, posc, freq, qan, kvan, wq, wk, wv, swap_src)


def _swa_bwd(proj, d_out, lse, delta, posc, posr, sinks):
    t = proj.shape[0]
    per = _SWA_PER_STEP
    span = per * BLOCK
    steps = t // span

    def body(q_ref, kc_ref, kp_ref, vc_ref, vp_ref, do_ref, l_ref, d_ref, pq_ref, pc_ref, pp_ref, sink_ref,
             dq_ref, dk_ref, dv_ref, ds_ref, dkb_s, dvb_s, dk_keep, dv_keep):
        n = pl.program_id(0)

        @pl.when(n == 0)
        def _():
            ds_ref[...] = jnp.zeros(ds_ref.shape, F32)
            dk_keep[...] = jnp.zeros(dk_keep.shape, F32)
            dv_keep[...] = jnp.zeros(dv_keep.shape, F32)

        @pl.when(n < steps)
        def _():
            k_all = jnp.concatenate([kp_ref[...], kc_ref[...]], axis=0)
            v_all = jnp.concatenate([vp_ref[...], vc_ref[...]], axis=0)
            pos_all = jnp.concatenate([pp_ref[...], pc_ref[...]], axis=0)
            ki = lax.broadcasted_iota(jnp.int32, (2 * BLOCK, BLOCK), 0)
            qi = lax.broadcasted_iota(jnp.int32, (2 * BLOCK, BLOCK), 1)
            window = (ki > qi) & (ki <= qi + WINDOW)
            lane = lax.broadcasted_iota(jnp.int32, (1, LANES), 1)
            dsink = jnp.zeros((1, LANES), F32)
            for sub in range(per):
                band = slice(BLOCK * sub, BLOCK * (sub + 2))
                own = slice(BLOCK * sub, BLOCK * (sub + 1))
                kb, vb = k_all[band], v_all[band]
                dist = jnp.abs(pos_all[band] - pq_ref[:, own])
                valid = window & ((n > 0) | (ki >= BLOCK)) if sub == 0 else window
                qv, dov = q_ref[own, :], do_ref[own, :]
                q_t, do_t, kb_t = qv.T, dov.T, kb.T
                dq_t = []
                for kh in range(N_KV_A):
                    heads = range(_GROUP_A * kh, _GROUP_A * (kh + 1))
                    st_g = _dot(_head_cols(kb, kh).astype(BF16), _group_t(q_t, kh))
                    dpt_g = _dot(_head_cols(vb, kh).astype(BF16), _group_t(do_t, kh))
                    pts, dsts = [], []
                    for j, h in enumerate(heads):
                        st = _swa_scores_t(st_g, j, h, dist, valid)
                        l_h, d_h = l_ref[h:h + 1, own], d_ref[h:h + 1, own]
                        pt = jnp.exp2(st - l_h)
                        p_sink = jnp.exp2(sink_ref[0:1, h:h + 1] * _LOG2E - l_h)
                        dsink = dsink + jnp.where(lane == h, jnp.sum(-p_sink * d_h, axis=1, keepdims=True), 0.0)
                        dst = pt * (dpt_g[:, BLOCK * j:BLOCK * (j + 1)] - d_h) * _SWA_SCALE
                        pts.append(pt.astype(BF16))
                        dsts.append(dst.astype(BF16))
                    pt_g, dst_g = jnp.concatenate(pts, axis=1), jnp.concatenate(dsts, axis=1)
                    q_g = jnp.concatenate([_head_cols(qv, h) for h in heads], axis=0).astype(BF16)
                    do_g = jnp.concatenate([_head_cols(dov, h) for h in heads], axis=0).astype(BF16)
                    dkb_s[sub, :, HEAD_DIM_A * kh:HEAD_DIM_A * (kh + 1)] = _dot(dst_g, q_g)
                    dvb_s[sub, :, HEAD_DIM_A * kh:HEAD_DIM_A * (kh + 1)] = _dot(pt_g, do_g)
                    dq_g = _dot(_head_rows(kb_t, kh).astype(BF16), dst_g)
                    dq_t.extend(dq_g[:, BLOCK * j:BLOCK * (j + 1)] for j in range(_GROUP_A))
                dq_ref[own, :] = jnp.concatenate(dq_t, axis=0).T
            ds_ref[...] += dsink
            for keep, out, parts in ((dk_keep, dk_ref, dkb_s), (dv_keep, dv_ref, dvb_s)):
                out[0:span - BLOCK, :] = keep[0:span - BLOCK, :]
                out[span - BLOCK:span, :] = keep[span - BLOCK:span, :] + parts[0, 0:BLOCK, :]
                for s in range(per - 1):
                    keep[BLOCK * s:BLOCK * (s + 1), :] = parts[s, BLOCK:2 * BLOCK, :] + parts[s + 1, 0:BLOCK, :]
                keep[span - BLOCK:span, :] = parts[per - 1, BLOCK:2 * BLOCK, :]

        @pl.when(n == steps)
        def _():
            dk_ref[...] = dk_keep[...]
            dv_ref[...] = dv_keep[...]

    last = steps - 1
    cur = lambda n: (jnp.minimum(n, last), 0)
    cur_t = lambda n: (0, jnp.minimum(n, last))
    prv = lambda n: jnp.maximum(per * jnp.minimum(n, last) - 1, 0)
    out_prev = lambda n: (jnp.maximum(n - 1, 0), 0)
    return pl.pallas_call(
        body, name="swa_bwd", grid=(steps + 1,),
        in_specs=[pl.BlockSpec((span, WIDTH_A), lambda n: (jnp.minimum(n, last), _QA_BLK)),
                  pl.BlockSpec((span, LANES), lambda n: (jnp.minimum(n, last), _KA_BLK)),
                  pl.BlockSpec((BLOCK, LANES), lambda n: (prv(n), _KA_BLK)),
                  pl.BlockSpec((span, LANES), lambda n: (jnp.minimum(n, last), _VA_BLK)),
                  pl.BlockSpec((BLOCK, LANES), lambda n: (prv(n), _VA_BLK)),
                  pl.BlockSpec((span, WIDTH_A), cur), pl.BlockSpec((N_HEADS_A, span), cur_t),
                  pl.BlockSpec((N_HEADS_A, span), cur_t), pl.BlockSpec((1, span), cur_t),
                  pl.BlockSpec((span, 1), cur), pl.BlockSpec((BLOCK, 1), lambda n: (prv(n), 0)),
                  _full((1, N_HEADS_A))],
        out_specs=[pl.BlockSpec((span, WIDTH_A), cur), pl.BlockSpec((span, LANES), out_prev),
                   pl.BlockSpec((span, LANES), out_prev), _full((1, LANES))],
        out_shape=[jax.ShapeDtypeStruct((t, WIDTH_A), F32), jax.ShapeDtypeStruct((t, LANES), F32),
                   jax.ShapeDtypeStruct((t, LANES), F32), jax.ShapeDtypeStruct((1, LANES), F32)],
        scratch_shapes=[pltpu.VMEM((per, 2 * BLOCK, LANES), F32), pltpu.VMEM((per, 2 * BLOCK, LANES), F32),
                        pltpu.VMEM((span, LANES), F32), pltpu.VMEM((span, LANES), F32)],
        compiler_params=_params(("arbitrary",)),
    )(proj, proj, proj, proj, proj, d_out, lse, delta, posr, posc, posc, sinks)


def _in_bwd(dproj, w_in_t, x, dx1, g1, gp):
    t = x.shape[0]
    tm = 512
    steps = t // tm

    def body(dp_ref, w_ref, x_ref, dx1_ref, g_ref, gp_ref, dx_ref, dg_ref, land_ref, send_sems, recv_sems):
        i = pl.program_id(0)

        @pl.when(i == 0)
        def _():
            dg_ref[...] = jnp.zeros(dg_ref.shape, F32)
            _scatter_start(gp_ref, land_ref, send_sems, recv_sems)

        for rows in _row_halves(tm):
            dh = _dot(dp_ref[rows, :], w_ref[...])
            xv = x_ref[rows, :]
            r = _rms(xv)
            dx, dg = _norm_bwd(dh, xv * r, r, g_ref[...])
            dx_ref[rows, :] = dx1_ref[rows, :] + dx
            dg_ref[...] += dg

        @pl.when(i == steps - 1)
        def _():
            _scatter_wait(gp_ref, land_ref, send_sems, recv_sems)

    row = lambda i: (i, 0)
    blk = pl.BlockSpec((tm, D_MODEL), row)
    return pl.pallas_call(
        body, name="in_bwd", grid=(steps,),
        in_specs=[pl.BlockSpec((tm, D_IN_PAD), row), _full((D_IN_PAD, D_MODEL)), blk, blk, _full((1, D_MODEL)), _HBM],
        out_specs=[blk, _full((1, D_MODEL)), _HBM],
        out_shape=[jax.ShapeDtypeStruct((t, D_MODEL), F32), jax.ShapeDtypeStruct((1, D_MODEL), F32),
                   jax.ShapeDtypeStruct((3,) + gp.shape[1:], gp.dtype)],
        scratch_shapes=[pltpu.SemaphoreType.DMA((3,)), pltpu.SemaphoreType.DMA((3,))],
        compiler_params=_params(("arbitrary",)),
    )(dproj, w_in_t, x, dx1, g1, gp)


def _adamw_store(w, g, m, v, out_refs):
    g_out, d_out, m_out, v_out = out_refs
    m_new = ADAM_B1 * m + (1.0 - ADAM_B1) * g
    v_new = ADAM_B2 * v + (1.0 - ADAM_B2) * jnp.square(g)
    m_hat = m_new / (1.0 - ADAM_B1 ** ADAM_STEP)
    v_hat = v_new / (1.0 - ADAM_B2 ** ADAM_STEP)
    g_out[...] = g
    d_out[...] = -ADAM_LR * (m_hat / (jnp.sqrt(v_hat) + ADAM_EPS) + ADAM_WD * w)
    m_out[...] = m_new
    v_out[...] = v_new


_SMALL_SLOTS = {"pre_norm_mix": (0, 0, D_MODEL), "post_norm_mix": (1, 0, D_MODEL), "pre_norm_mlp": (2, 0, D_MODEL),
                "post_norm_mlp": (3, 0, D_MODEL), "q_a_norm": (4, 0, Q_LORA), "kv_a_norm": (4, Q_LORA, KV_LORA),
                "sinks": (4, Q_LORA + KV_LORA, N_HEADS_A)}
_LOSS_ROW = 5


def _adamw_small(red, w, m, v):
    names = tuple(_SMALL_SLOTS)
    n = len(names)

    def body(*refs):
        red_ref, ws, ms, vs, outs = refs[0], refs[1:1 + n], refs[1 + n:1 + 2 * n], refs[1 + 2 * n:1 + 3 * n], refs[1 + 3 * n:]
        for k, name in enumerate(names):
            row, lane, width = _SMALL_SLOTS[name]
            g = red_ref[row:row + 1, lane:lane + width]
            _adamw_store(ws[k][...], g, ms[k][...], vs[k][...], outs[4 * k:4 * k + 4])

    vmem = pl.BlockSpec(memory_space=pltpu.VMEM)
    res = pl.pallas_call(
        body, name="adamw_small", in_specs=[vmem] * (1 + 3 * n), out_specs=[vmem] * (4 * n),
        out_shape=[jax.ShapeDtypeStruct(w[name].shape, F32) for name in names for _ in range(4)],
    )(red, *[w[k] for k in names], *[m[k] for k in names], *[v[k] for k in names])
    return {name: res[4 * k:4 * k + 4] for k, name in enumerate(names)}


_ADAMW_RIDERS = ("w_up", "w_down", "w_out")


def _dw_in_adamw(dproj, h, g_parts, w, m, v):
    t, cols = dproj.shape
    tm, tk = cols // 2, min(1024, t)
    rows_out = N_CHIPS * SHARD_SHAPES["w_in"][1]
    nk = t // tk
    steps = 2 * nk
    names = _ADAMW_RIDERS
    n = len(names)

    def body(a_ref, b_ref, *rest):
        g1s, g2s, ws, ms, vs = (rest[n * j:n * (j + 1)] for j in range(5))
        o_ref, outs = rest[5 * n], rest[5 * n + 1:]

        @pl.when(pl.program_id(2) == 0)
        def _():
            o_ref[...] = jnp.zeros(o_ref.shape, F32)

        o_ref[...] += _dot_tn(a_ref[...], b_ref[...])
        for j in range(n):
            _adamw_store(ws[j][...], g1s[j][...] + g2s[j][...], ms[j][...], vs[j][...], outs[4 * j:4 * j + 4])

    def rider_spec(name, packed):
        br = SHARD_SHAPES[name][0] // steps
        first = _row_offset(GROUP_B, name) // br if packed else 0
        return pl.BlockSpec((br, D_MODEL), lambda i, j, k: (first + i * nk + k, 0))

    g_specs = [rider_spec(name, True) for name in names]
    own_specs = [rider_spec(name, False) for name in names]
    res = pl.pallas_call(
        body, name="dw_in", grid=(2, 1, nk),
        in_specs=[pl.BlockSpec((tk, tm), lambda i, j, k: (k, i)), pl.BlockSpec((tk, D_MODEL), lambda i, j, k: (k, 0))]
        + g_specs * 2 + own_specs * 3,
        out_specs=[pl.BlockSpec((tm, D_MODEL), lambda i, j, k: (i, 0))] + [s for s in own_specs for _ in range(4)],
        out_shape=[jax.ShapeDtypeStruct((rows_out, D_MODEL), F32)]
        + [jax.ShapeDtypeStruct(SHARD_SHAPES[name], F32) for name in names for _ in range(4)],
        compiler_params=_params(("arbitrary", "arbitrary", "arbitrary")),
    )(dproj, h, *[g_parts[0]] * n, *[g_parts[1]] * n, *[w[k] for k in names], *[m[k] for k in names],
      *[v[k] for k in names])
    return res[0], {name: res[1 + 4 * j:5 + 4 * j] for j, name in enumerate(names)}


def _adamw(w, g_parts, m, v, name, block, g_row_off=0):
    r, c = w.shape
    br, bc = block
    ng = len(g_parts)

    def body(*refs):
        w_ref, g_refs, m_ref, v_ref = refs[0], refs[1:1 + ng], refs[1 + ng], refs[2 + ng]
        g = g_refs[0][...]
        for gr in g_refs[1:]:
            g = g + gr[...]
        _adamw_store(w_ref[...], g, m_ref[...], v_ref[...], refs[3 + ng:])

    assert g_row_off % br == 0 and r % br == 0 and c % bc == 0
    blk = pl.BlockSpec(block, lambda i, j: (i, j))
    g_blk = pl.BlockSpec(block, lambda i, j: (i + g_row_off // br, j))
    return pl.pallas_call(
        body, name=name, grid=(r // br, c // bc),
        in_specs=[blk] + [g_blk] * ng + [blk, blk], out_specs=[blk] * 4,
        out_shape=[jax.ShapeDtypeStruct((r, c), F32)] * 4,
        compiler_params=_params(("parallel", "parallel")),
    )(w, *g_parts, m, v)


_HBM = pl.BlockSpec(memory_space=pltpu.HBM)


def _other_chips(x, y):
    return ((1 - x, y), (x, 1 - y), (1 - x, 1 - y))


def _gather_copies(src, out, send_sems, recv_sems, local_sem):
    x, y, c = lax.axis_index("x"), lax.axis_index("y"), lax.axis_index("c")
    me = 2 * x + y
    local = pltpu.make_async_copy(src, out.at[me], local_sem)

    def copies(arriving):
        return [pltpu.make_async_remote_copy(src_ref=src, dst_ref=out.at[2 * px + py if arriving else me],
                                             send_sem=send_sems.at[j], recv_sem=recv_sems.at[j], device_id=(px, py, c),
                                             device_id_type=MESH)
                for j, (px, py) in enumerate(_other_chips(x, y))]

    return local, copies


def _gather_start(src, out, send_sems, recv_sems, local_sem):
    local, copies = _gather_copies(src, out, send_sems, recv_sems, local_sem)
    local.start()
    for cp in copies(False):
        cp.start()


def _gather_wait(src, out, send_sems, recv_sems, local_sem):
    local, copies = _gather_copies(src, out, send_sems, recv_sems, local_sem)
    for cp in copies(True):
        cp.wait_recv()
    for cp in copies(False):
        cp.wait_send()
    local.wait()


def _scatter_copies(src, land, send_sems, recv_sems):
    x, y, c = lax.axis_index("x"), lax.axis_index("y"), lax.axis_index("c")
    return [pltpu.make_async_remote_copy(src_ref=src.at[2 * px + py], dst_ref=land.at[j], send_sem=send_sems.at[j],
                                         recv_sem=recv_sems.at[j], device_id=(px, py, c), device_id_type=MESH)
            for j, (px, py) in enumerate(_other_chips(x, y))]


def _scatter_start(src, land, send_sems, recv_sems):
    for cp in _scatter_copies(src, land, send_sems, recv_sems):
        cp.start()


def _scatter_wait(src, land, send_sems, recv_sems):
    copies = _scatter_copies(src, land, send_sems, recv_sems)
    for cp in copies:
        cp.wait_recv()
    for cp in copies:
        cp.wait_send()


def _all_gather_chips(packed):
    r = packed.shape[0]
    half = r // 2

    def body(src, out, ici_send, ici_recv, d2d_send, d2d_recv, local_sem):
        x, y, c = lax.axis_index("x"), lax.axis_index("y"), lax.axis_index("c")
        me = 2 * x + y
        mine = pl.ds(pl.multiple_of(c * half, 16), half)
        theirs = pl.ds(pl.multiple_of((1 - c) * half, 16), half)
        chips = _other_chips(x, y)
        local = pltpu.make_async_copy(src, out.at[me], local_sem)
        local.start()
        sends = [pltpu.make_async_remote_copy(src_ref=src.at[mine], dst_ref=out.at[me, mine], send_sem=ici_send.at[j],
                                              recv_sem=ici_recv.at[j], device_id=(px, py, c), device_id_type=MESH)
                 for j, (px, py) in enumerate(chips)]
        for cp in sends:
            cp.start()
        passed = []
        for j, (px, py) in enumerate(chips):
            block = 2 * px + py
            pltpu.make_async_remote_copy(src_ref=src.at[mine], dst_ref=out.at[block, mine], send_sem=ici_send.at[j],
                                         recv_sem=ici_recv.at[j], device_id=(px, py, c), device_id_type=MESH).wait_recv()
            cp = pltpu.make_async_remote_copy(src_ref=out.at[block, mine], dst_ref=out.at[block, mine],
                                              send_sem=d2d_send.at[j], recv_sem=d2d_recv.at[j],
                                              device_id=(x, y, 1 - c), device_id_type=MESH)
            cp.start()
            passed.append(cp)
        for j, (px, py) in enumerate(chips):
            block = 2 * px + py
            pltpu.make_async_remote_copy(src_ref=out.at[block, theirs], dst_ref=out.at[block, theirs],
                                         send_sem=d2d_send.at[j], recv_sem=d2d_recv.at[j],
                                         device_id=(x, y, 1 - c), device_id_type=MESH).wait_recv()
        for cp in sends + passed:
            cp.wait_send()
        local.wait()

    sems = pltpu.SemaphoreType.DMA((3,))
    return pl.pallas_call(
        body, name="ag_weights", in_specs=[_HBM], out_specs=_HBM,
        out_shape=jax.ShapeDtypeStruct((N_CHIPS,) + packed.shape, packed.dtype),
        scratch_shapes=[sems, sems, sems, sems, pltpu.SemaphoreType.DMA(())],
    )(packed)


def _sum4(gp, land, chip, name):
    _, r, w = gp.shape
    tr = 256 if r % 256 == 0 else 128

    def body(chip_ref, o_ref, l_ref, s_ref):
        s_ref[...] = ((o_ref[0] + l_ref[0].astype(F32)) + l_ref[1].astype(F32)) + l_ref[2].astype(F32)

    return pl.pallas_call(
        body, name=name,
        grid_spec=pltpu.PrefetchScalarGridSpec(
            num_scalar_prefetch=1, grid=(r // tr,),
            in_specs=[pl.BlockSpec((1, tr, w), lambda i, chip_ref: (chip_ref[0], i, 0)),
                      pl.BlockSpec((3, tr, w), lambda i, chip_ref: (0, i, 0))],
            out_specs=pl.BlockSpec((tr, w), lambda i, chip_ref: (i, 0))),
        out_shape=jax.ShapeDtypeStruct((r, w), F32),
        compiler_params=_params(("parallel",)),
    )(chip, gp, land)


def _sibling_copy(src, got, send_sem, recv_sem):
    x, y, c = lax.axis_index("x"), lax.axis_index("y"), lax.axis_index("c")
    return pltpu.make_async_remote_copy(src_ref=src, dst_ref=got, send_sem=send_sem, recv_sem=recv_sem,
                                        device_id=(x, y, 1 - c), device_id_type=MESH)


def _swap_sibling(s, name):
    def body(src, got, send_sem, recv_sem):
        cp = _sibling_copy(src, got, send_sem, recv_sem)
        cp.start()
        cp.wait_recv()
        cp.wait_send()

    return pl.pallas_call(
        body, name=name, in_specs=[_HBM], out_specs=_HBM,
        out_shape=jax.ShapeDtypeStruct(s.shape, s.dtype),
        scratch_shapes=[pltpu.SemaphoreType.DMA(()), pltpu.SemaphoreType.DMA(())],
    )(s)


def _all_reduce_small(dsmall, loss):
    n_dev = 8
    names = tuple(_SMALL_SLOTS)
    shape = (8, D_MODEL)

    def body(*refs):
        parts, loss_ref = refs[:len(names)], refs[len(names)]
        out, src, gath, send_sems, recv_sems = refs[len(names) + 1:]
        x, y, c = lax.axis_index("x"), lax.axis_index("y"), lax.axis_index("c")
        me = 4 * x + 2 * y + c
        src[...] = jnp.zeros(shape, F32)
        for name, part in zip(names, parts):
            row, lane, _ = _SMALL_SLOTS[name]
            src[row:row + 1, lane:lane + part.shape[1]] = part[...]
        src[_LOSS_ROW:_LOSS_ROW + 1, 0:LANES] = loss_ref[...]
        gath[me] = src[...]
        peers = []
        for k in range(1, n_dev):
            px = 1 - x if (k >> 2) & 1 else x
            py = 1 - y if (k >> 1) & 1 else y
            pc = 1 - c if k & 1 else c
            peers.append((px, py, pc))
        sends = []
        for j, peer in enumerate(peers):
            cp = pltpu.make_async_remote_copy(src_ref=src, dst_ref=gath.at[me], send_sem=send_sems.at[j],
                                              recv_sem=recv_sems.at[j], device_id=peer, device_id_type=MESH)
            cp.start()
            sends.append(cp)
        for j, (px, py, pc) in enumerate(peers):
            pltpu.make_async_remote_copy(src_ref=src, dst_ref=gath.at[4 * px + 2 * py + pc], send_sem=send_sems.at[j],
                                         recv_sem=recv_sems.at[j], device_id=(px, py, pc), device_id_type=MESH).wait_recv()
        for cp in sends:
            cp.wait_send()
        acc = gath[0]
        for d in range(1, n_dev):
            acc = acc + gath[d]
        out[...] = acc

    vmem = pl.BlockSpec(memory_space=pltpu.VMEM)
    return pl.pallas_call(
        body, name="ar_small", in_specs=[vmem] * (len(names) + 1), out_specs=vmem,
        out_shape=jax.ShapeDtypeStruct(shape, F32),
        scratch_shapes=[pltpu.VMEM(shape, F32), pltpu.VMEM((n_dev,) + shape, F32),
                        pltpu.SemaphoreType.DMA((n_dev - 1,)), pltpu.SemaphoreType.DMA((n_dev - 1,))],
    )(*[dsmall[k] for k in names], loss)


_W_IN_ROWS = SHARD_SHAPES["w_in"][1]


def _shard_rows(name, a):
    return jnp.transpose(a) if name == "w_in" else a.reshape(PACK_ROWS[name], D_MODEL)


def _pack(group, shards, dtype):
    parts = [_shard_rows(n, shards[n]).astype(dtype) for n in group]
    pad = -sum(PACK_ROWS[n] for n in group) % LANES
    if pad:
        parts.append(jnp.zeros((pad, D_MODEL), dtype))
    return jnp.concatenate(parts, axis=0)


def _col_sharded_full(g, name, group):
    r, c = SHARD_SHAPES[name]
    off = _row_offset(group, name)
    blocks = g[:, off:off + PACK_ROWS[name]].reshape(N_CHIPS, r, c)
    return jnp.transpose(blocks, (1, 0, 2)).reshape(r, N_CHIPS * c)


def _col_sharded_blocks(d, name):
    r, c = SHARD_SHAPES[name]
    return jnp.transpose(d.reshape(r, N_CHIPS, c), (1, 0, 2)).reshape(N_CHIPS, PACK_ROWS[name], D_MODEL)


def _weights_a(g):
    dt = g.dtype
    w_in_t = jnp.concatenate([g[c, :_W_IN_ROWS] for c in range(N_CHIPS)]
                             + [jnp.zeros((D_IN_PAD - N_CHIPS * _W_IN_ROWS, D_MODEL), dt)], axis=0)
    wq = _col_sharded_full(g, "w_q_b", GROUP_A).reshape(Q_LORA, N_HEADS_B, Q_HEAD_B)
    wq_p = jnp.concatenate([wq, jnp.zeros((Q_LORA, N_HEADS_B, HEAD_PAD - Q_HEAD_B), dt)], axis=2).reshape(Q_LORA, MLA_W)
    wkv = _col_sharded_full(g, "w_kv_b", GROUP_A).reshape(KV_LORA, N_HEADS_B, QK_NOPE + V_DIM_B)
    zk = jnp.zeros((KV_LORA, N_HEADS_B, HEAD_PAD - QK_NOPE), dt)
    wk_p = jnp.concatenate([wkv[:, :, :QK_NOPE], zk], axis=2).reshape(KV_LORA, MLA_W)
    wv = wkv[:, :, QK_NOPE:].reshape(KV_LORA, N_HEADS_B * V_DIM_B)
    return dict(w_in=w_in_t, wq=wq_p, wk=wk_p, wv=wv, wv_t=jnp.transpose(wv))


def _grad_blocks_a(dw_in_t, dwq_p, dwk_p, dwv):
    dwq = dwq_p.reshape(Q_LORA, N_HEADS_B, HEAD_PAD)[:, :, :Q_HEAD_B].reshape(Q_LORA, N_HEADS_B * Q_HEAD_B)
    dwk = dwk_p.reshape(KV_LORA, N_HEADS_B, HEAD_PAD)[:, :, :QK_NOPE]
    dwkv = jnp.concatenate([dwk, dwv.reshape(KV_LORA, N_HEADS_B, V_DIM_B)], axis=2)
    dwkv = dwkv.reshape(KV_LORA, N_HEADS_B * (QK_NOPE + V_DIM_B))
    pad = -sum(PACK_ROWS[n] for n in GROUP_A) % LANES
    return [dw_in_t.reshape(N_CHIPS, _W_IN_ROWS, D_MODEL), _col_sharded_blocks(dwq, "w_q_b"),
            _col_sharded_blocks(dwkv, "w_kv_b"), jnp.zeros((N_CHIPS, pad, D_MODEL), F32)]


def _rope_freq_lanes():
    freqs = ROPE_THETA ** (-jnp.arange(0, QK_ROPE, 2, dtype=F32) / QK_ROPE)
    return jnp.concatenate([jnp.zeros((QK_NOPE,), F32), freqs, freqs,
                            jnp.zeros((HEAD_PAD - Q_HEAD_B,), F32)]).reshape(1, LANES)


def _fwd_bwd(x, positions, target, w, m, v):
    t = x.shape[0]
    wa = _weights_a(_all_gather_chips(_pack(GROUP_A, w, BF16)))
    posr = positions.astype(F32).reshape(1, t)
    posc = posr.reshape(t, 1)
    freq = _rope_freq_lanes()
    g1, g2, g3, g4 = w["pre_norm_mix"], w["post_norm_mix"], w["pre_norm_mlp"], w["post_norm_mlp"]
    qan, kvan, sinks = w["q_a_norm"], w["kv_a_norm"], w["sinks"]

    h, proj = _proj_fwd(x, g1, wa["w_in"])
    out_a, lse_a = _swa_fwd(proj, posc, posr, sinks)
    qm, km, qt, kt, vt = _mla_prep_fwd(proj, posc, freq, qan, kvan, wa["wq"], wa["wk"], wa["wv_t"])
    out_bt, lse_b, wb = _mla_fwd(km, qt, vt, _pack(GROUP_B, w, BF16))
    w_oa, w_ob = _col_sharded_full(wb, "w_o_a", GROUP_B), _col_sharded_full(wb, "w_o_b", GROUP_B)
    merged, y, x1, h2 = _mix_out_fwd(out_a, out_bt, proj, x, w_oa, w_ob, wb, g2, g3)
    a = _up_fwd(h2, wb)
    dx2, dyd, dg4, loss = _down_fwd_loss(a, wb, x1, target, g4)

    gp_b = _dw_into_blocks(a, dyd, "w_down", 1024, _TK_DW)
    du = _down_bwd(dyd, wb, a)
    gp_b = _dw_into_blocks(h2, du, "w_up", 1024, _TK_DW, gp_b)
    dx1, dy, dg3, dg2 = _up_bwd(du, wb, x1, dx2, y, g3, g2)
    gp_b = _dw_into_blocks(merged, dy, "w_out", 1024, _TK_DW, gp_b)
    dw_oa, dw_ob, dproj, d_out_a, d_out_b, d_out_bt, del_a, del_b = _mix_out_bwd(dy, out_a, out_bt, proj, w_oa, w_ob, wb)
    small_b = jnp.concatenate([_col_sharded_blocks(dw_oa, "w_o_a"), _col_sharded_blocks(dw_ob, "w_o_b")], axis=1)
    gp_b = lax.dynamic_update_slice(gp_b, small_b, (0, _row_offset(GROUP_B, "w_o_a"), 0))
    dqm, dkm, dvm, land_b = _mla_bwd(qm, km, qt, kt, vt, d_out_b, d_out_bt, lse_b, del_b, gp_b)
    chip = (2 * lax.axis_index("x") + lax.axis_index("y")).astype(jnp.int32).reshape(1)
    part_b = _sum4(gp_b, land_b, chip, "rs_sum_b")
    dcq, dckv, dkr, dwq, dwk, dwv, dqan, dkvan, sib_b = _mla_prep_bwd(
        dqm, dkm, dvm, proj, posc, freq, qan, kvan, wa["wq"], wa["wk"], wa["wv"], part_b)
    dqa, dka, dva, dsinks = _swa_bwd(proj, d_out_a, lse_a, del_a, posc, posr, sinks)
    col = 2 * D_MODEL
    for piece in (dqa, dka, dva, dcq, dckv, dkr):
        dproj = lax.dynamic_update_slice(dproj, piece.astype(BF16), (0, col))
        col += piece.shape[1]
    dw_in_t, updated = _dw_in_adamw(dproj, h, [part_b, sib_b], w, m, v)
    parts_a = _grad_blocks_a(dw_in_t, dwq, dwk, dwv)
    gp_a = jnp.concatenate([p.astype(BF16) for p in parts_a], axis=1)
    grad_x, dg1, land_a = _in_bwd(dproj, wa["w_in"], x, dx1, g1, gp_a)

    own_a = jnp.concatenate([lax.dynamic_slice_in_dim(p, chip[0], 1, axis=0) for p in parts_a], axis=1)
    part_a = _sum4(own_a, land_a, jnp.zeros((1,), jnp.int32), "rs_sum_a")
    reduced = {GROUP_A: [part_a, _swap_sibling(part_a, "rs_swap_a")], GROUP_B: [part_b, sib_b]}
    dsmall = dict(pre_norm_mix=dg1, post_norm_mix=dg2, pre_norm_mlp=dg3, post_norm_mlp=dg4,
                  q_a_norm=dqan, kv_a_norm=dkvan, sinks=dsinks)
    return loss, grad_x, reduced, dsmall, updated


def kernel(x, positions, pre_norm_mix, w_in, q_a_norm, w_q_b, kv_a_norm, w_kv_b, sinks, w_o_a, w_o_b, w_out, post_norm_mix, pre_norm_mlp, w_up, w_down, post_norm_mlp, loss_target, m_pre_norm_mix, m_w_in, m_q_a_norm, m_w_q_b, m_kv_a_norm, m_w_kv_b, m_sinks, m_w_o_a, m_w_o_b, m_w_out, m_post_norm_mix, m_pre_norm_mlp, m_w_up, m_w_down, m_post_norm_mlp, v_pre_norm_mix, v_w_in, v_q_a_norm, v_w_q_b, v_kv_a_norm, v_w_kv_b, v_sinks, v_w_o_a, v_w_o_b, v_w_out, v_post_norm_mix, v_pre_norm_mlp, v_w_up, v_w_down, v_post_norm_mlp):
    w = dict(pre_norm_mix=pre_norm_mix, w_in=w_in[0], q_a_norm=q_a_norm, w_q_b=w_q_b[0], kv_a_norm=kv_a_norm,
             w_kv_b=w_kv_b[0], sinks=sinks, w_o_a=w_o_a[0], w_o_b=w_o_b[0], w_out=w_out[0],
             post_norm_mix=post_norm_mix, pre_norm_mlp=pre_norm_mlp, w_up=w_up[0], w_down=w_down[0],
             post_norm_mlp=post_norm_mlp)
    m = dict(pre_norm_mix=m_pre_norm_mix, w_in=m_w_in[0], q_a_norm=m_q_a_norm, w_q_b=m_w_q_b[0],
             kv_a_norm=m_kv_a_norm, w_kv_b=m_w_kv_b[0], sinks=m_sinks, w_o_a=m_w_o_a[0], w_o_b=m_w_o_b[0],
             w_out=m_w_out[0], post_norm_mix=m_post_norm_mix, pre_norm_mlp=m_pre_norm_mlp, w_up=m_w_up[0],
             w_down=m_w_down[0], post_norm_mlp=m_post_norm_mlp)
    v = dict(pre_norm_mix=v_pre_norm_mix, w_in=v_w_in[0], q_a_norm=v_q_a_norm, w_q_b=v_w_q_b[0],
             kv_a_norm=v_kv_a_norm, w_kv_b=v_w_kv_b[0], sinks=v_sinks, w_o_a=v_w_o_a[0], w_o_b=v_w_o_b[0],
             w_out=v_w_out[0], post_norm_mix=v_post_norm_mix, pre_norm_mlp=v_pre_norm_mlp, w_up=v_w_up[0],
             w_down=v_w_down[0], post_norm_mlp=v_post_norm_mlp)

    loss, grad_x, reduced, dsmall, updated = _fwd_bwd(x[0], positions, loss_target[0], w, m, v)

    red = _all_reduce_small(dsmall, loss)
    small = _adamw_small(red, w, m, v)

    big = {}
    tr = jnp.transpose
    big["w_in"] = [tr(o)[None] for o in _adamw(tr(w["w_in"]), reduced[GROUP_A], tr(m["w_in"]), tr(v["w_in"]),
                                               "adamw_w_in", (_W_IN_ROWS, 256))]
    for n in _ADAMW_RIDERS:
        big[n] = [o[None] for o in updated[n]]
    for group, names in ((GROUP_A, ("w_q_b", "w_kv_b")), (GROUP_B, ("w_o_a", "w_o_b"))):
        for n in names:
            off = _row_offset(group, n)
            g_parts = [p[off:off + PACK_ROWS[n]].reshape(SHARD_SHAPES[n]) for p in reduced[group]]
            big[n] = [o[None] for o in _adamw(w[n], g_parts, m[n], v[n], "adamw_" + n, SHARD_SHAPES[n])]

    outs = [big[n][k] if n in big else small[n][k] for k in range(4) for n in WEIGHTS]
    return (red[_LOSS_ROW, 0], grad_x[None], *outs)
```

```python
import jax
import jax.numpy as jnp
from jax import lax
from jax.experimental import pallas as pl
from jax.experimental.pallas import tpu as pltpu

F32 = jnp.float32
BF16 = jnp.bfloat16
MESH = pl.DeviceIdType.MESH

D_MODEL = 1024
N_HEADS_A = 8
N_KV_A = 2
HEAD_DIM_A = 64
WINDOW = 128
BLOCK = 128
N_HEADS_B = 8
QK_NOPE = 64
QK_ROPE = 32
V_DIM_B = 64
Q_LORA = 256
KV_LORA = 128
ROPE_THETA = 10000.0
D_FF = 4 * D_MODEL
EPS = 1e-6
WIDTH_A = N_HEADS_A * HEAD_DIM_A
Q_HEAD_B = QK_NOPE + QK_ROPE
D_IN_PAD = 3328
HEAD_PAD = 128
MLA_W = N_HEADS_B * HEAD_PAD

ADAM_LR = 0.001
ADAM_B1 = 0.9
ADAM_B2 = 0.999
ADAM_EPS = 1e-08
ADAM_WD = 0.01
ADAM_STEP = 10

NEG = -1e30
N_CHIPS = 4
LANES = 128
VMEM_LIMIT = 56 * 1024 * 1024

SHARD_SHAPES = {"w_in": (1024, 808), "w_q_b": (256, 192), "w_kv_b": (128, 256), "w_o_a": (512, 256),
                "w_o_b": (512, 256), "w_out": (256, 1024), "w_up": (1024, 1024), "w_down": (1024, 1024)}
PACK_ROWS = {n: (s[0] * s[1]) // D_MODEL for n, s in SHARD_SHAPES.items()}
GROUP_A = ("w_in", "w_q_b", "w_kv_b")
GROUP_B = ("w_up", "w_down", "w_out", "w_o_a", "w_o_b")
WEIGHTS = ("pre_norm_mix", "w_in", "q_a_norm", "w_q_b", "kv_a_norm", "w_kv_b", "sinks", "w_o_a", "w_o_b", "w_out",
           "post_norm_mix", "pre_norm_mlp", "w_up", "w_down", "post_norm_mlp")


def _params(sem=None):
    return pltpu.CompilerParams(dimension_semantics=sem, vmem_limit_bytes=VMEM_LIMIT)


def _dot(a, b):
    return jnp.dot(a, b, preferred_element_type=F32)


def _dot_nt(a, b):
    return lax.dot_general(a, b, (((1,), (1,)), ((), ())), preferred_element_type=F32)


def _dot_tn(a, b):
    return lax.dot_general(a, b, (((0,), (0,)), ((), ())), preferred_element_type=F32)


def _rms(v):
    return lax.rsqrt(jnp.mean(v * v, axis=-1, keepdims=True) + EPS)


def _norm_bwd(dout, n, r, g):
    dn = dout * g
    dx = r * (dn - n * jnp.mean(dn * n, axis=-1, keepdims=True))
    return dx, jnp.sum(dout * n, axis=0, keepdims=True)


def _full(shape):
    return pl.BlockSpec(shape, lambda *_: (0,) * len(shape))


def _row_offset(group, name):
    return sum(PACK_ROWS[n] for n in group[:group.index(name)])


def _wb_spec(name):
    rows = PACK_ROWS[name]
    return pl.BlockSpec((N_CHIPS, rows, D_MODEL), lambda *_: (0, _row_offset(GROUP_B, name) // rows, 0))


def _proj_fwd(x, g1, w_in_t):
    t = x.shape[0]
    tm = 512

    def body(x_ref, g_ref, w_ref, h_ref, p_ref):
        for rows in _row_halves(tm):
            xv = x_ref[rows, :]
            h = ((xv * _rms(xv)) * g_ref[...]).astype(BF16)
            h_ref[rows, :] = h
            p_ref[rows, :] = _dot_nt(h, w_ref[...])

    return pl.pallas_call(
        body, name="proj_fwd", grid=(t // tm,),
        in_specs=[pl.BlockSpec((tm, D_MODEL), lambda i: (i, 0)), _full((1, D_MODEL)), _full((D_IN_PAD, D_MODEL))],
        out_specs=[pl.BlockSpec((tm, D_MODEL), lambda i: (i, 0)), pl.BlockSpec((tm, D_IN_PAD), lambda i: (i, 0))],
        out_shape=[jax.ShapeDtypeStruct((t, D_MODEL), BF16), jax.ShapeDtypeStruct((t, D_IN_PAD), F32)],
        compiler_params=_params(("parallel",)),
    )(x, g1, w_in_t)


_QA_BLK = 2048 // WIDTH_A
_KA_BLK = 2560 // LANES
_VA_BLK = 2688 // LANES
_CQ_BLK = 2816 // Q_LORA
_CKV_BLK = 3072 // LANES
_KR_BLK = 3200 // LANES


_GROUP_A = N_HEADS_A // N_KV_A
_SWA_SCALE = HEAD_DIM_A ** -0.5
_LOG2E = 1.4426950408889634


def _head_cols(v, h):
    return v[:, HEAD_DIM_A * h:HEAD_DIM_A * (h + 1)]


def _head_rows(v, h):
    return v[HEAD_DIM_A * h:HEAD_DIM_A * (h + 1), :]


def _swa_scores_t(st_g, j, h, dist, valid):
    slope = 2.0 ** (-8.0 * (h + 1) / N_HEADS_A)
    st = st_g[:, BLOCK * j:BLOCK * (j + 1)] * (_SWA_SCALE * _LOG2E) - (slope * _LOG2E) * dist
    return jnp.where(valid, st, NEG)


def _group_t(xt, kh):
    return jnp.concatenate([_head_rows(xt, _GROUP_A * kh + j) for j in range(_GROUP_A)], axis=1).astype(BF16)


_SWA_PER_STEP = 4


def _swa_fwd(proj, posc, posr, sinks):
    t = proj.shape[0]
    span = _SWA_PER_STEP * BLOCK

    def body(q_ref, kc_ref, kp_ref, vc_ref, vp_ref, pq_ref, pc_ref, pp_ref, sink_ref, o_ref, l_ref):
        n = pl.program_id(0)
        k_all = jnp.concatenate([kp_ref[...], kc_ref[...]], axis=0)
        v_all = jnp.concatenate([vp_ref[...], vc_ref[...]], axis=0)
        pos_all = jnp.concatenate([pp_ref[...], pc_ref[...]], axis=0)
        ki = lax.broadcasted_iota(jnp.int32, (2 * BLOCK, BLOCK), 0)
        qi = lax.broadcasted_iota(jnp.int32, (2 * BLOCK, BLOCK), 1)
        window = (ki > qi) & (ki <= qi + WINDOW)
        for sub in range(_SWA_PER_STEP):
            band = slice(BLOCK * sub, BLOCK * (sub + 2))
            own = slice(BLOCK * sub, BLOCK * (sub + 1))
            kb, vb = k_all[band], v_all[band]
            dist = jnp.abs(pos_all[band] - pq_ref[:, own])
            valid = window & ((n > 0) | (ki >= BLOCK)) if sub == 0 else window
            q_t, vb_t = q_ref[own, :].T, vb.T
            out_t, lse = [], []
            for kh in range(N_KV_A):
                st_g = _dot(_head_cols(kb, kh).astype(BF16), _group_t(q_t, kh))
                ps = []
                for j in range(_GROUP_A):
                    h = _GROUP_A * kh + j
                    st = _swa_scores_t(st_g, j, h, dist, valid)
                    sink = sink_ref[0:1, h:h + 1] * _LOG2E
                    m = jnp.maximum(jnp.max(st, axis=0, keepdims=True), sink)
                    e = jnp.exp2(st - m)
                    den = jnp.sum(e, axis=0, keepdims=True) + jnp.exp2(sink - m)
                    ps.append((e * (1.0 / den)).astype(BF16))
                    lse.append(m + jnp.log(den) * _LOG2E)
                o_g = _dot(_head_rows(vb_t, kh).astype(BF16), jnp.concatenate(ps, axis=1))
                out_t.extend(o_g[:, BLOCK * j:BLOCK * (j + 1)] for j in range(_GROUP_A))
            o_ref[own, :] = jnp.concatenate(out_t, axis=0).T
            l_ref[:, own] = jnp.concatenate(lse, axis=0)

    cur = lambda n: (n, 0)
    prev = lambda n: jnp.maximum(_SWA_PER_STEP * n - 1, 0)
    return pl.pallas_call(
        body, name="swa_fwd", grid=(t // span,),
        in_specs=[pl.BlockSpec((span, WIDTH_A), lambda n: (n, _QA_BLK)),
                  pl.BlockSpec((span, LANES), lambda n: (n, _KA_BLK)),
                  pl.BlockSpec((BLOCK, LANES), lambda n: (prev(n), _KA_BLK)),
                  pl.BlockSpec((span, LANES), lambda n: (n, _VA_BLK)),
                  pl.BlockSpec((BLOCK, LANES), lambda n: (prev(n), _VA_BLK)),
                  pl.BlockSpec((1, span), lambda n: (0, n)),
                  pl.BlockSpec((span, 1), cur),
                  pl.BlockSpec((BLOCK, 1), lambda n: (prev(n), 0)),
                  _full((1, N_HEADS_A))],
        out_specs=[pl.BlockSpec((span, WIDTH_A), cur), pl.BlockSpec((N_HEADS_A, span), lambda n: (0, n))],
        out_shape=[jax.ShapeDtypeStruct((t, WIDTH_A), F32), jax.ShapeDtypeStruct((N_HEADS_A, t), F32)],
        compiler_params=_params(("parallel",)),
    )(proj, proj, proj, proj, proj, posr, posc, posc, sinks)


def _rope_coeffs(pos, freq):
    ang = pos * freq
    cosv, sinv = jnp.cos(ang), jnp.sin(ang)
    lane = lax.broadcasted_iota(jnp.int32, ang.shape, 1)
    lo = (lane >= QK_NOPE) & (lane < QK_NOPE + QK_ROPE // 2)
    hi = (lane >= QK_NOPE + QK_ROPE // 2) & (lane < QK_NOPE + QK_ROPE)
    c = jnp.where(lane < QK_NOPE, 1.0, jnp.where(lo | hi, cosv, 0.0))
    s = jnp.where(lo, -sinv, jnp.where(hi, sinv, 0.0))
    return c, s, lo, hi


def _rope(xh, c, s, lo):
    up = pltpu.roll(xh, LANES - QK_ROPE // 2, axis=1)
    dn = pltpu.roll(xh, QK_ROPE // 2, axis=1)
    return xh * c + jnp.where(lo, up, dn) * s


def _unrope(dh, c, s, lo, hi):
    g = dh * s
    up = pltpu.roll(g, LANES - QK_ROPE // 2, axis=1)
    dn = pltpu.roll(g, QK_ROPE // 2, axis=1)
    return dh * c + jnp.where(hi, dn, jnp.where(lo, up, 0.0))


_TQ = 512
_MLA_SCALE = Q_HEAD_B ** -0.5


def _mla_prep_fwd(proj, posc, freq, qan, kvan, wq, wk, wv):
    t = proj.shape[0]
    tm = _TQ
    nb = t // tm

    def body(cq_ref, ckv_ref, kr_ref, pos_ref, f_ref, qan_ref, kvan_ref, wq_ref, wk_ref, wv_ref,
             q_ref, k_ref, qt_ref, kt_ref, vt_ref):
        cq = cq_ref[...]
        cqn = ((cq * _rms(cq)) * qan_ref[...]).astype(BF16)
        ckv = ckv_ref[...]
        ckvn = ((ckv * _rms(ckv)) * kvan_ref[...]).astype(BF16)
        qb = _dot(cqn, wq_ref[...])
        kb = _dot(ckvn, wk_ref[...])
        vbt = _dot_nt(wv_ref[...], ckvn)
        c, s, lo, _ = _rope_coeffs(pos_ref[...], f_ref[...])
        kr = _rope(pltpu.roll(kr_ref[...], QK_NOPE, axis=1), c, s, lo)
        for h in range(N_HEADS_B):
            sl = slice(HEAD_PAD * h, HEAD_PAD * (h + 1))
            q_h = _rope(qb[:, sl], c, s, lo)
            k_h = kb[:, sl] + kr
            q_ref[:, sl] = q_h.astype(BF16)
            k_ref[:, sl] = k_h.astype(BF16)
            qt_ref[h, 0] = q_h.T.astype(BF16)
            kt_ref[h, 0] = k_h.T.astype(BF16)
            vt_ref[h, 0] = vbt[V_DIM_B * h:V_DIM_B * (h + 1), :].astype(BF16)

    row = lambda i: (i, 0)
    blk4 = lambda d: pl.BlockSpec((N_HEADS_B, 1, d, tm), lambda i: (0, i, 0, 0))
    return pl.pallas_call(
        body, name="mla_prep_fwd", grid=(nb,),
        in_specs=[pl.BlockSpec((tm, Q_LORA), lambda i: (i, _CQ_BLK)),
                  pl.BlockSpec((tm, LANES), lambda i: (i, _CKV_BLK)),
                  pl.BlockSpec((tm, LANES), lambda i: (i, _KR_BLK)),
                  pl.BlockSpec((tm, 1), row), _full((1, LANES)), _full((1, Q_LORA)), _full((1, KV_LORA)),
                  _full((Q_LORA, MLA_W)), _full((KV_LORA, MLA_W)), _full((N_HEADS_B * V_DIM_B, KV_LORA))],
        out_specs=[pl.BlockSpec((tm, MLA_W), row), pl.BlockSpec((tm, MLA_W), row), blk4(HEAD_PAD), blk4(HEAD_PAD),
                   blk4(V_DIM_B)],
        out_shape=[jax.ShapeDtypeStruct((t, MLA_W), BF16), jax.ShapeDtypeStruct((t, MLA_W), BF16),
                   jax.ShapeDtypeStruct((N_HEADS_B, nb, HEAD_PAD, tm), BF16),
                   jax.ShapeDtypeStruct((N_HEADS_B, nb, HEAD_PAD, tm), BF16),
                   jax.ShapeDtypeStruct((N_HEADS_B, nb, V_DIM_B, tm), BF16)],
        compiler_params=_params(("parallel",)),
    )(proj, proj, proj, posc, freq, qan, kvan, wq, wk, wv)


_MLA_SCALE2 = _MLA_SCALE * _LOG2E


def _mla_fwd(k, qt, vt, w_src):
    t = k.shape[0]
    nb = t // _TQ
    pairs = min(4, nb // 2)
    groups = nb // (2 * pairs)

    def body(k_ref, qt_ref, vt_ref, w_ref, o_ref, l_ref, wg_ref, raw, send_sems, recv_sems, local_sem):
        first = (pl.program_id(0) == 0) & (pl.program_id(1) == 0)
        last = (pl.program_id(0) == N_HEADS_B - 1) & (pl.program_id(1) == groups - 1)

        @pl.when(first)
        def _():
            _gather_start(w_ref, wg_ref, send_sems, recv_sems, local_sem)

        for pair in range(pairs):
            one_pair(k_ref, qt_ref, vt_ref, o_ref, l_ref, raw, pairs * pl.program_id(1) + pair, 2 * pair)

        @pl.when(last)
        def _():
            _gather_wait(w_ref, wg_ref, send_sems, recv_sems, local_sem)

    def one_pair(k_ref, qt_ref, vt_ref, o_ref, l_ref, raw, g, at):
        def keys(kj):
            return k_ref[pl.ds(pl.multiple_of(kj * _TQ, _TQ), _TQ), :]

        def products(kj, slot):
            kv = keys(kj)
            raw[slot, 0] = _dot(kv, qt_ref[0, at])
            raw[slot, 1] = _dot(kv, qt_ref[0, at + 1])

        def update(stats, raw_ref, kj, diagonal=False):
            m, l, acc = stats
            scores = raw_ref[...]
            if diagonal:
                key = lax.broadcasted_iota(jnp.int32, scores.shape, 0)
                qry = lax.broadcasted_iota(jnp.int32, scores.shape, 1)
                scores = jnp.where(key <= qry, scores, NEG)
            m_new = jnp.maximum(m, jnp.max(scores, axis=0, keepdims=True) * _MLA_SCALE2)
            alpha = jnp.exp2(m - m_new)
            p = jnp.exp2(scores * _MLA_SCALE2 - m_new).astype(BF16)
            pv = _dot(jnp.concatenate([vt_ref[0, kj], jnp.ones((16, _TQ), BF16)], axis=0), p)
            return m_new, alpha * l + pv[V_DIM_B:V_DIM_B + 8], alpha * acc + pv[:V_DIM_B]

        def trip(i, stats):
            sa, sb = stats
            products(2 * i + 1, 1)
            sa, sb = update(sa, raw.at[0, 0], 2 * i), update(sb, raw.at[0, 1], 2 * i)
            products(2 * i + 2, 0)
            return update(sa, raw.at[1, 0], 2 * i + 1), update(sb, raw.at[1, 1], 2 * i + 1)

        init = (jnp.full((1, _TQ), NEG, F32), jnp.zeros((8, _TQ), F32), jnp.zeros((V_DIM_B, _TQ), F32))
        products(0, 0)
        sa, sb = lax.fori_loop(0, g, trip, (init, init))
        raw[1, 1] = _dot(keys(2 * g + 1), qt_ref[0, at + 1])
        sa = update(sa, raw.at[0, 0], 2 * g, True)
        sb = update(update(sb, raw.at[0, 1], 2 * g), raw.at[1, 1], 2 * g + 1, True)
        for which, (m, l, acc) in enumerate((sa, sb)):
            o_ref[0, at + which] = acc / l[0:1]
            l_ref[0, at + which] = m + jnp.log(l[0:1]) * _LOG2E

    two = lambda d: pl.BlockSpec((1, 2 * pairs, d, _TQ), lambda h, g: (h, g, 0, 0))
    return pl.pallas_call(
        body, name="mla_fwd", grid=(N_HEADS_B, groups),
        in_specs=[pl.BlockSpec((t, HEAD_PAD), lambda h, g: (0, h)), two(HEAD_PAD),
                  pl.BlockSpec((1, nb, V_DIM_B, _TQ), lambda h, g: (h, 0, 0, 0)), _HBM],
        out_specs=[two(V_DIM_B), two(1), _HBM],
        out_shape=[jax.ShapeDtypeStruct((N_HEADS_B, nb, V_DIM_B, _TQ), F32),
                   jax.ShapeDtypeStruct((N_HEADS_B, nb, 1, _TQ), F32),
                   jax.ShapeDtypeStruct((N_CHIPS,) + w_src.shape, w_src.dtype)],
        scratch_shapes=[pltpu.VMEM((2, 2, _TQ, _TQ), F32),
                        pltpu.SemaphoreType.DMA((3,)), pltpu.SemaphoreType.DMA((3,)), pltpu.SemaphoreType.DMA(())],
        compiler_params=_params(("arbitrary", "arbitrary")),
    )(k, qt, vt, w_src)


def _ot_spec(tm, d):
    per = _TQ // tm
    return pl.BlockSpec((N_HEADS_B, 1, d, tm), lambda i: (0, i // per, 0, i % per))


def _mix_out_fwd(out_a, out_bt, proj, x, w_oa, w_ob, wb, g2, g3):
    t = x.shape[0]
    tm = 512

    def body(oa_ref, obt_ref, ga_ref, gb_ref, x_ref, woa_ref, wob_ref, wout_ref, g2_ref, g3_ref,
             mg_ref, y_ref, x1_ref, h2_ref):
        oa = _dot(oa_ref[...].astype(BF16), woa_ref[...])
        obt = obt_ref[...].reshape(N_HEADS_B * V_DIM_B, tm).astype(BF16)
        ob = _dot_tn(obt, wob_ref[...])
        merged = (jax.nn.sigmoid(ga_ref[...]) * oa + jax.nn.sigmoid(gb_ref[...]) * ob).astype(BF16)
        mg_ref[...] = merged
        y = _dot(merged, wout_ref[...].reshape(D_MODEL, D_MODEL))
        y_ref[...] = y
        x1 = x_ref[...] + (y * _rms(y)) * g2_ref[...]
        x1_ref[...] = x1
        h2_ref[...] = ((x1 * _rms(x1)) * g3_ref[...]).astype(BF16)

    row = lambda i: (i, 0)
    blk = pl.BlockSpec((tm, D_MODEL), row)
    return pl.pallas_call(
        body, name="mix_out_fwd", grid=(t // tm,),
        in_specs=[pl.BlockSpec((tm, WIDTH_A), row), _ot_spec(tm, V_DIM_B), pl.BlockSpec((tm, D_MODEL), lambda i: (i, 0)),
                  pl.BlockSpec((tm, D_MODEL), lambda i: (i, 1)), blk,
                  _full((WIDTH_A, D_MODEL)), _full((N_HEADS_B * V_DIM_B, D_MODEL)), _wb_spec("w_out"),
                  _full((1, D_MODEL)), _full((1, D_MODEL))],
        out_specs=[blk, blk, blk, blk],
        out_shape=[jax.ShapeDtypeStruct((t, D_MODEL), BF16), jax.ShapeDtypeStruct((t, D_MODEL), F32),
                   jax.ShapeDtypeStruct((t, D_MODEL), F32), jax.ShapeDtypeStruct((t, D_MODEL), BF16)],
        compiler_params=_params(("parallel",)),
    )(out_a, out_bt, proj, proj, x, w_oa, w_ob, wb, g2, g3)


_TM_MLP = 512


def _row_halves(tm):
    return slice(0, tm // 2), slice(tm // 2, tm)


def _up_fwd(h2, wb):
    t = h2.shape[0]
    tm = _TM_MLP

    def body(h_ref, w_ref, a_ref):
        hv = h_ref[...]
        for j in range(N_CHIPS):
            u = _dot(hv, w_ref[j])
            a_ref[:, D_MODEL * j:D_MODEL * (j + 1)] = jnp.square(jnp.maximum(u, 0.0)).astype(BF16)

    return pl.pallas_call(
        body, name="up_fwd", grid=(t // tm,),
        in_specs=[pl.BlockSpec((tm, D_MODEL), lambda i: (i, 0)), _wb_spec("w_up")],
        out_specs=pl.BlockSpec((tm, D_FF), lambda i: (i, 0)),
        out_shape=jax.ShapeDtypeStruct((t, D_FF), BF16),
        compiler_params=_params(("parallel",)),
    )(h2, wb)


def _down_fwd_loss(a, wb, x1, target, g4):
    t = a.shape[0]
    tm = _TM_MLP

    def body(a_ref, w_ref, x1_ref, tg_ref, g_ref, dx2_ref, dyd_ref, dg_ref, loss_ref):
        @pl.when(pl.program_id(0) == 0)
        def _():
            dg_ref[...] = jnp.zeros(dg_ref.shape, F32)
            loss_ref[...] = jnp.zeros(loss_ref.shape, F32)

        yd = _dot(a_ref[...], w_ref[...].reshape(D_FF, D_MODEL))
        r = _rms(yd)
        n = yd * r
        diff = (x1_ref[...] + n * g_ref[...]) - tg_ref[...]
        loss_ref[...] += 0.5 * jnp.sum(jnp.mean(diff * diff, axis=-1, keepdims=True), axis=0, keepdims=True)
        dx2 = diff * (1.0 / D_MODEL)
        dx2_ref[...] = dx2
        dyd, dg = _norm_bwd(dx2, n, r, g_ref[...])
        dyd_ref[...] = dyd.astype(BF16)
        dg_ref[...] += dg

    row = lambda i: (i, 0)
    blk = pl.BlockSpec((tm, D_MODEL), row)
    return pl.pallas_call(
        body, name="down_fwd_loss", grid=(t // tm,),
        in_specs=[pl.BlockSpec((tm, D_FF), row), _wb_spec("w_down"), blk, blk, _full((1, D_MODEL))],
        out_specs=[blk, blk, _full((1, D_MODEL)), _full((1, LANES))],
        out_shape=[jax.ShapeDtypeStruct((t, D_MODEL), F32), jax.ShapeDtypeStruct((t, D_MODEL), BF16),
                   jax.ShapeDtypeStruct((1, D_MODEL), F32), jax.ShapeDtypeStruct((1, LANES), F32)],
        compiler_params=_params(("arbitrary",)),
    )(a, wb, x1, target, g4)


def _matmul_tn(a, b, name, tm, tn, tk=1024):
    t, m = a.shape
    n = b.shape[1]
    tk = min(tk, t)
    nk = t // tk

    def body(a_ref, b_ref, o_ref):
        @pl.when(pl.program_id(2) == 0)
        def _():
            o_ref[...] = jnp.zeros(o_ref.shape, F32)

        o_ref[...] += _dot_tn(a_ref[...].astype(BF16), b_ref[...].astype(BF16))

    return pl.pallas_call(
        body, name=name, grid=(m // tm, n // tn, nk),
        in_specs=[pl.BlockSpec((tk, tm), lambda i, j, k: (k, i)), pl.BlockSpec((tk, tn), lambda i, j, k: (k, j))],
        out_specs=pl.BlockSpec((tm, tn), lambda i, j, k: (i, j)),
        out_shape=jax.ShapeDtypeStruct((m, n), F32),
        compiler_params=_params(("parallel", "parallel", "arbitrary")),
    )(a, b)


_TK_DW = 2048


def _dw_into_blocks(a, b, weight, tm, tk, buf=None):
    t, m = a.shape
    n = b.shape[1]
    tk = min(tk, t)
    nk = t // tk
    rows = PACK_ROWS[weight]
    br = min(tm, rows)
    chips = tm // br
    first = _row_offset(GROUP_B, weight) // br
    per_chip = rows // br
    if weight == "w_up":
        out_map = lambda i, j, k: (j, first + i, 0)
    elif chips > 1:
        out_map = lambda i, j, k: (i, first, 0)
    else:
        out_map = lambda i, j, k: (i // per_chip, first + i % per_chip, 0)

    def body(a_ref, b_ref, *rest):
        o_ref = rest[-1]

        @pl.when(pl.program_id(2) == 0)
        def _():
            o_ref[...] = jnp.zeros(o_ref.shape, F32)

        o_ref[...] += _dot_tn(a_ref[...].astype(BF16), b_ref[...].astype(BF16)).reshape(o_ref.shape)

    in_specs = [pl.BlockSpec((tk, tm), lambda i, j, k: (k, i)), pl.BlockSpec((tk, D_MODEL), lambda i, j, k: (k, j))]
    operands = [a, b]
    if buf is not None:
        in_specs.append(pl.BlockSpec(memory_space=pl.ANY))
        operands.append(buf)
    total = sum(PACK_ROWS[w] for w in GROUP_B)
    return pl.pallas_call(
        body, name="dw_" + weight[2:], grid=(m // tm, n // D_MODEL, nk),
        in_specs=in_specs, out_specs=pl.BlockSpec((chips, br, D_MODEL), out_map),
        out_shape=jax.ShapeDtypeStruct((N_CHIPS, total, D_MODEL), F32),
        input_output_aliases={} if buf is None else {2: 0},
        compiler_params=_params(("parallel", "parallel", "arbitrary")),
    )(*operands)


def _down_bwd(dyd, wb, a):
    t = dyd.shape[0]
    tm = _TM_MLP

    def body(d_ref, w_ref, a_ref, du_ref):
        dv = d_ref[...]
        for j in range(N_CHIPS):
            cols = slice(D_MODEL * j, D_MODEL * (j + 1))
            av = a_ref[:, cols].astype(F32)
            relu_u = jnp.where(av > 0.0, av * lax.rsqrt(av), 0.0)
            du_ref[:, cols] = (_dot_nt(dv, w_ref[j]) * (2.0 * relu_u)).astype(BF16)

    row = lambda i: (i, 0)
    return pl.pallas_call(
        body, name="down_bwd", grid=(t // tm,),
        in_specs=[pl.BlockSpec((tm, D_MODEL), row), _wb_spec("w_down"), pl.BlockSpec((tm, D_FF), row)],
        out_specs=pl.BlockSpec((tm, D_FF), row),
        out_shape=jax.ShapeDtypeStruct((t, D_FF), BF16),
        compiler_params=_params(("parallel",)),
    )(dyd, wb, a)


def _up_bwd(du, wb, x1, dx2, y, g3, g2):
    t = du.shape[0]
    tm = _TM_MLP

    def body(du_ref, w_ref, x1_ref, dx2_ref, y_ref, g3_ref, g2_ref, dx1_ref, dy_ref, dg3_ref, dg2_ref):
        @pl.when(pl.program_id(0) == 0)
        def _():
            dg3_ref[...] = jnp.zeros(dg3_ref.shape, F32)
            dg2_ref[...] = jnp.zeros(dg2_ref.shape, F32)

        dh2 = _dot_nt(du_ref[:, 0:D_MODEL], w_ref[0])
        for j in range(1, N_CHIPS):
            dh2 = dh2 + _dot_nt(du_ref[:, D_MODEL * j:D_MODEL * (j + 1)], w_ref[j])
        x1 = x1_ref[...]
        r3 = _rms(x1)
        d3, dg3 = _norm_bwd(dh2, x1 * r3, r3, g3_ref[...])
        dx1 = dx2_ref[...] + d3
        dx1_ref[...] = dx1
        dg3_ref[...] += dg3
        y = y_ref[...]
        r2 = _rms(y)
        dy, dg2 = _norm_bwd(dx1, y * r2, r2, g2_ref[...])
        dy_ref[...] = dy.astype(BF16)
        dg2_ref[...] += dg2

    row = lambda i: (i, 0)
    blk = pl.BlockSpec((tm, D_MODEL), row)
    return pl.pallas_call(
        body, name="up_bwd", grid=(t // tm,),
        in_specs=[pl.BlockSpec((tm, D_FF), row), _wb_spec("w_up"),
                  blk, blk, blk, _full((1, D_MODEL)), _full((1, D_MODEL))],
        out_specs=[blk, blk, _full((1, D_MODEL)), _full((1, D_MODEL))],
        out_shape=[jax.ShapeDtypeStruct((t, D_MODEL), F32), jax.ShapeDtypeStruct((t, D_MODEL), BF16),
                   jax.ShapeDtypeStruct((1, D_MODEL), F32), jax.ShapeDtypeStruct((1, D_MODEL), F32)],
        compiler_params=_params(("arbitrary",)),
    )(du, wb, x1, dx2, y, g3, g2)


def _mix_out_bwd(dy, out_a, out_bt, proj, w_oa, w_ob, wb):
    t = dy.shape[0]
    tm = 512
    nb = t // _TQ

    def body(dy_ref, oa_ref, obt_ref, ga_ref, gb_ref, woa_ref, wob_ref, wout_ref,
             dwoa_ref, dwob_ref, dg_ref, da_ref, db_ref, dbt_ref, dela_ref, delb_ref):
        @pl.when(pl.program_id(0) == 0)
        def _():
            dwoa_ref[...] = jnp.zeros(dwoa_ref.shape, F32)
            dwob_ref[...] = jnp.zeros(dwob_ref.shape, F32)

        dm = _dot_nt(dy_ref[...], wout_ref[...].reshape(D_MODEL, D_MODEL))
        out_a_v = oa_ref[...]
        out_bt_v = obt_ref[...].reshape(N_HEADS_B * V_DIM_B, tm)
        oa = _dot(out_a_v.astype(BF16), woa_ref[...])
        ob = _dot_tn(out_bt_v.astype(BF16), wob_ref[...])
        sa, sb = jax.nn.sigmoid(ga_ref[...]), jax.nn.sigmoid(gb_ref[...])
        doa = (dm * sa).astype(BF16)
        dob = (dm * sb).astype(BF16)
        dwoa_ref[...] += _dot_tn(out_a_v.astype(BF16), doa)
        dwob_ref[...] += _dot(out_bt_v.astype(BF16), dob)
        dg_ref[:, :D_MODEL] = (dm * oa * (sa * (1.0 - sa))).astype(BF16)
        dg_ref[:, D_MODEL:] = (dm * ob * (sb * (1.0 - sb))).astype(BF16)
        d_out_a = _dot_nt(doa, woa_ref[...])
        da_ref[...] = d_out_a
        prod_at = (d_out_a * out_a_v).T
        dela_ref[...] = jnp.concatenate(
            [jnp.sum(_head_rows(prod_at, h), axis=0, keepdims=True) for h in range(N_HEADS_A)], axis=0)
        d_out_b = _dot_nt(dob, wob_ref[...])
        d_out_bt = _dot_nt(wob_ref[...], dob)
        prod_bt = d_out_bt * out_bt_v
        for h in range(N_HEADS_B):
            db_ref[h] = d_out_b[:, V_DIM_B * h:V_DIM_B * (h + 1)].astype(BF16)
            dbt_ref[h, 0] = d_out_bt[V_DIM_B * h:V_DIM_B * (h + 1), :].astype(BF16)
            delb_ref[h, 0] = jnp.sum(prod_bt[V_DIM_B * h:V_DIM_B * (h + 1), :], axis=0, keepdims=True)

    row = lambda i: (i, 0)
    blk = pl.BlockSpec((tm, D_MODEL), row)
    return pl.pallas_call(
        body, name="mix_out_bwd", grid=(t // tm,),
        in_specs=[blk, pl.BlockSpec((tm, WIDTH_A), row), _ot_spec(tm, V_DIM_B),
                  pl.BlockSpec((tm, D_MODEL), lambda i: (i, 0)), pl.BlockSpec((tm, D_MODEL), lambda i: (i, 1)),
                  _full((WIDTH_A, D_MODEL)), _full((N_HEADS_B * V_DIM_B, D_MODEL)), _wb_spec("w_out")],
        out_specs=[_full((WIDTH_A, D_MODEL)), _full((N_HEADS_B * V_DIM_B, D_MODEL)),
                   pl.BlockSpec((tm, 2 * D_MODEL), row), pl.BlockSpec((tm, WIDTH_A), row),
                   pl.BlockSpec((N_HEADS_B, tm, V_DIM_B), lambda i: (0, i, 0)), _ot_spec(tm, V_DIM_B),
                   pl.BlockSpec((N_HEADS_A, tm), lambda i: (0, i)), _ot_spec(tm, 1)],
        out_shape=[jax.ShapeDtypeStruct((WIDTH_A, D_MODEL), F32),
                   jax.ShapeDtypeStruct((N_HEADS_B * V_DIM_B, D_MODEL), F32),
                   jax.ShapeDtypeStruct((t, D_IN_PAD), BF16), jax.ShapeDtypeStruct((t, WIDTH_A), F32),
                   jax.ShapeDtypeStruct((N_HEADS_B, t, V_DIM_B), BF16),
                   jax.ShapeDtypeStruct((N_HEADS_B, nb, V_DIM_B, _TQ), BF16), jax.ShapeDtypeStruct((N_HEADS_A, t), F32),
                   jax.ShapeDtypeStruct((N_HEADS_B, nb, 1, _TQ), F32)],
        compiler_params=_params(("arbitrary",)),
    )(dy, out_a, out_bt, proj, proj, w_oa, w_ob, wb)


def _mla_bwd(q, k, qt, kt, vt, d_out, d_out_t, lse, delta, gp):
    t = q.shape[0]
    nb = t // _TQ

    def body(k_ref, kt_ref, vt_ref, q_ref, qt_ref, do_ref, dot_ref, lrow_ref, drow_ref, gp_ref,
             dq_ref, dkt_ref, dvt_ref, land_ref, l_rep, d_rep, send_sems, recv_sems):
        step = pl.program_id(1)
        kj = nb - 1 - step

        @pl.when((pl.program_id(0) == 0) & (step == 0))
        def _():
            _scatter_start(gp_ref, land_ref, send_sems, recv_sems)

        @pl.when(step == 0)
        def _():
            dq_ref[...] = jnp.zeros(dq_ref.shape, F32)
            for b in range(nb):
                l_rep[_TQ * b:_TQ * (b + 1), :] = jnp.broadcast_to(lrow_ref[0, b], (LANES, _TQ)).T
                d_rep[_TQ * b:_TQ * (b + 1), :] = jnp.broadcast_to(drow_ref[0, b], (LANES, _TQ)).T

        kv, k_t, v_t = k_ref[...], kt_ref[0, 0], vt_ref[0, 0]

        def rows_of(qi):
            return pl.ds(pl.multiple_of(qi * _TQ, _TQ), _TQ)

        def products(qi, diagonal=False):
            s = _dot(q_ref[rows_of(qi), :], k_t) * _MLA_SCALE2
            if diagonal:
                qry = lax.broadcasted_iota(jnp.int32, s.shape, 0)
                key = lax.broadcasted_iota(jnp.int32, s.shape, 1)
                s = jnp.where(key <= qry, s, NEG)
            return s, _dot(do_ref[0, rows_of(qi), :], v_t)

        def update(carry, prods, qi):
            dkt, dvt = carry
            s, dp = prods
            lse, delta = l_rep[rows_of(qi), :], d_rep[rows_of(qi), :]
            ps, dss = [], []
            for c in range(_TQ // LANES):
                strip = slice(LANES * c, LANES * (c + 1))
                p = jnp.exp2(s[:, strip] - lse)
                ps.append(p.astype(BF16))
                dss.append((p * (dp[:, strip] - delta) * _MLA_SCALE).astype(BF16))
            p_b, ds_b = jnp.concatenate(ps, axis=1), jnp.concatenate(dss, axis=1)
            dvt = dvt + _dot(dot_ref[0, qi], p_b)
            dkt = dkt + _dot(qt_ref[0, qi], ds_b)
            dq_ref[rows_of(qi), :] += _dot(ds_b, kv)
            return dkt, dvt

        def pair(i, carry):
            qa = kj + 1 + 2 * i
            pa, pb = products(qa), products(qa + 1)
            return update(update(carry, pa, qa), pb, qa + 1)

        init = (jnp.zeros((HEAD_PAD, _TQ), F32), jnp.zeros((V_DIM_B, _TQ), F32))
        carry = update(init, products(kj, True), kj)
        pairs = (nb - 1 - kj) // 2
        carry = lax.fori_loop(0, pairs, pair, carry)
        dkt, dvt = lax.fori_loop(kj + 1 + 2 * pairs, nb, lambda qi, cr: update(cr, products(qi), qi), carry)
        dkt_ref[0, 0] = dkt
        dvt_ref[0, 0] = dvt

        @pl.when((pl.program_id(0) == N_HEADS_B - 1) & (step == nb - 1))
        def _():
            _scatter_wait(gp_ref, land_ref, send_sems, recv_sems)

    head4 = lambda d: pl.BlockSpec((1, nb, d, _TQ), lambda h, s: (h, 0, 0, 0))
    blk4 = lambda d: pl.BlockSpec((1, 1, d, _TQ), lambda h, s: (h, nb - 1 - s, 0, 0))
    head3 = lambda d: pl.BlockSpec((1, t, d), lambda h, kj: (h, 0, 0))
    per_head = pl.BlockSpec((t, HEAD_PAD), lambda h, kj: (0, h))
    return pl.pallas_call(
        body, name="mla_bwd", grid=(N_HEADS_B, nb),
        in_specs=[pl.BlockSpec((_TQ, HEAD_PAD), lambda h, s: (nb - 1 - s, h)), blk4(HEAD_PAD), blk4(V_DIM_B),
                  per_head, head4(HEAD_PAD), head3(V_DIM_B), head4(V_DIM_B), head4(1), head4(1), _HBM],
        out_specs=[per_head, blk4(HEAD_PAD), blk4(V_DIM_B), _HBM],
        out_shape=[jax.ShapeDtypeStruct((t, MLA_W), F32), jax.ShapeDtypeStruct((N_HEADS_B, nb, HEAD_PAD, _TQ), F32),
                   jax.ShapeDtypeStruct((N_HEADS_B, nb, V_DIM_B, _TQ), F32),
                   jax.ShapeDtypeStruct((3,) + gp.shape[1:], gp.dtype)],
        scratch_shapes=[pltpu.VMEM((t, LANES), F32), pltpu.VMEM((t, LANES), F32),
                        pltpu.SemaphoreType.DMA((3,)), pltpu.SemaphoreType.DMA((3,))],
        compiler_params=_params(("arbitrary", "arbitrary")),
    )(k, kt, vt, q, qt, d_out, d_out_t, lse, delta, gp)


def _mla_prep_bwd(dq, dkt, dvt, proj, posc, freq, qan, kvan, wq, wk, wv, swap_src):
    t = dq.shape[0]
    tm = _TQ

    def body(dq_ref, dkt_ref, dvt_ref, cq_ref, ckv_ref, pos_ref, f_ref, qan_ref, kvan_ref, wq_ref, wk_ref, wv_ref, src_ref,
             dcq_ref, dckv_ref, dkr_ref, dwq_ref, dwk_ref, dwv_ref, dqan_ref, dkvan_ref, got_ref, send_sem, recv_sem):
        swap = _sibling_copy(src_ref, got_ref, send_sem, recv_sem)

        @pl.when(pl.program_id(0) == 0)
        def _():
            swap.start()
            for r in (dwq_ref, dwk_ref, dwv_ref, dqan_ref, dkvan_ref):
                r[...] = jnp.zeros(r.shape, F32)

        cq = cq_ref[...]
        rq = _rms(cq)
        nq_ = cq * rq
        cqn = (nq_ * qan_ref[...]).astype(BF16)
        ckv = ckv_ref[...]
        rkv = _rms(ckv)
        nkv = ckv * rkv
        ckvn = (nkv * kvan_ref[...]).astype(BF16)
        c, s, lo, hi = _rope_coeffs(pos_ref[...], f_ref[...])
        dkr = jnp.zeros((tm, LANES), F32)
        dqb, dkb = [], []
        for h in range(N_HEADS_B):
            dqb.append(_unrope(dq_ref[:, HEAD_PAD * h:HEAD_PAD * (h + 1)], c, s, lo, hi).astype(BF16))
            dk_h = dkt_ref[h, 0].T
            dkr = dkr + dk_h
            dkb.append(dk_h.astype(BF16))
        dqb, dkb = jnp.concatenate(dqb, axis=1), jnp.concatenate(dkb, axis=1)
        dkr = jnp.where(lo | hi, _unrope(dkr, c, s, lo, hi), 0.0)
        dkr_ref[...] = pltpu.roll(dkr, LANES - QK_NOPE, axis=1).astype(BF16)
        dvb = dvt_ref[...].reshape(N_HEADS_B * V_DIM_B, tm).T.astype(BF16)
        dwq_ref[...] += _dot_tn(cqn, dqb)
        dwk_ref[...] += _dot_tn(ckvn, dkb)
        dwv_ref[...] += _dot_tn(ckvn, dvb)
        dcqn = _dot_nt(dqb, wq_ref[...])
        dckvn = _dot_nt(dkb, wk_ref[...]) + _dot_nt(dvb, wv_ref[...])
        dcq, dqan = _norm_bwd(dcqn, nq_, rq, qan_ref[...])
        dckv, dkvan = _norm_bwd(dckvn, nkv, rkv, kvan_ref[...])
        dcq_ref[...] = dcq.astype(BF16)
        dckv_ref[...] = dckv.astype(BF16)
        dqan_ref[...] += dqan
        dkvan_ref[...] += dkvan

        @pl.when(pl.program_id(0) == t // tm - 1)
        def _():
            swap.wait_recv()
            swap.wait_send()

    row = lambda i: (i, 0)
    vw = N_HEADS_B * V_DIM_B
    return pl.pallas_call(
        body, name="mla_prep_bwd", grid=(t // tm,),
        in_specs=[pl.BlockSpec((tm, MLA_W), row), pl.BlockSpec((N_HEADS_B, 1, HEAD_PAD, tm), lambda i: (0, i, 0, 0)),
                  pl.BlockSpec((N_HEADS_B, 1, V_DIM_B, tm), lambda i: (0, i, 0, 0)),
                  pl.BlockSpec((tm, Q_LORA), lambda i: (i, _CQ_BLK)),
                  pl.BlockSpec((tm, LANES), lambda i: (i, _CKV_BLK)),
                  pl.BlockSpec((tm, 1), row), _full((1, LANES)), _full((1, Q_LORA)), _full((1, KV_LORA)),
                  _full((Q_LORA, MLA_W)), _full((KV_LORA, MLA_W)), _full((KV_LORA, vw)), _HBM],
        out_specs=[pl.BlockSpec((tm, Q_LORA), row), pl.BlockSpec((tm, LANES), row), pl.BlockSpec((tm, LANES), row),
                   _full((Q_LORA, MLA_W)), _full((KV_LORA, MLA_W)), _full((KV_LORA, vw)),
                   _full((1, Q_LORA)), _full((1, KV_LORA)), _HBM],
        out_shape=[jax.ShapeDtypeStruct((t, Q_LORA), BF16), jax.ShapeDtypeStruct((t, LANES), BF16),
                   jax.ShapeDtypeStruct((t, LANES), BF16),
                   jax.ShapeDtypeStruct((Q_LORA, MLA_W), F32), jax.ShapeDtypeStruct((KV_LORA, MLA_W), F32),
                   jax.ShapeDtypeStruct((KV_LORA, vw), F32),
                   jax.ShapeDtypeStruct((1, Q_LORA), F32), jax.ShapeDtypeStruct((1, KV_LORA), F32),
                   jax.ShapeDtypeStruct(swap_src.shape, swap_src.dtype)],
        scratch_shapes=[pltpu.SemaphoreType.DMA(()), pltpu.SemaphoreType.DMA(())],
        compiler_params=_params(("arbitrary",)),
    )(dq, dkt, dvt, proj, proj, posc, freq, qan, kvan, wq, wk, wv, swap_src)


def _swa_bwd(proj, d_out, lse, delta, posc, posr, sinks):
    t = proj.shape[0]
    per = _SWA_PER_STEP
    span = per * BLOCK
    steps = t // span

    def body(q_ref, kc_ref, kp_ref, vc_ref, vp_ref, do_ref, l_ref, d_ref, pq_ref, pc_ref, pp_ref, sink_ref,
             dq_ref, dk_ref, dv_ref, ds_ref, dkb_s, dvb_s, dk_keep, dv_keep):
        n = pl.program_id(0)

        @pl.when(n == 0)
        def _():
            ds_ref[...] = jnp.zeros(ds_ref.shape, F32)
            dk_keep[...] = jnp.zeros(dk_keep.shape, F32)
            dv_keep[...] = jnp.zeros(dv_keep.shape, F32)

        @pl.when(n < steps)
        def _():
            k_all = jnp.concatenate([kp_ref[...], kc_ref[...]], axis=0)
            v_all = jnp.concatenate([vp_ref[...], vc_ref[...]], axis=0)
            pos_all = jnp.concatenate([pp_ref[...], pc_ref[...]], axis=0)
            ki = lax.broadcasted_iota(jnp.int32, (2 * BLOCK, BLOCK), 0)
            qi = lax.broadcasted_iota(jnp.int32, (2 * BLOCK, BLOCK), 1)
            window = (ki > qi) & (ki <= qi + WINDOW)
            lane = lax.broadcasted_iota(jnp.int32, (1, LANES), 1)
            dsink = jnp.zeros((1, LANES), F32)
            for sub in range(per):
                band = slice(BLOCK * sub, BLOCK * (sub + 2))
                own = slice(BLOCK * sub, BLOCK * (sub + 1))
                kb, vb = k_all[band], v_all[band]
                dist = jnp.abs(pos_all[band] - pq_ref[:, own])
                valid = window & ((n > 0) | (ki >= BLOCK)) if sub == 0 else window
                qv, dov = q_ref[own, :], do_ref[own, :]
                q_t, do_t, kb_t = qv.T, dov.T, kb.T
                dq_t = []
                for kh in range(N_KV_A):
                    heads = range(_GROUP_A * kh, _GROUP_A * (kh + 1))
                    st_g = _dot(_head_cols(kb, kh).astype(BF16), _group_t(q_t, kh))
                    dpt_g = _dot(_head_cols(vb, kh).astype(BF16), _group_t(do_t, kh))
                    pts, dsts = [], []
                    for j, h in enumerate(heads):
                        st = _swa_scores_t(st_g, j, h, dist, valid)
                        l_h, d_h = l_ref[h:h + 1, own], d_ref[h:h + 1, own]
                        pt = jnp.exp2(st - l_h)
                        p_sink = jnp.exp2(sink_ref[0:1, h:h + 1] * _LOG2E - l_h)
                        dsink = dsink + jnp.where(lane == h, jnp.sum(-p_sink * d_h, axis=1, keepdims=True), 0.0)
                        dst = pt * (dpt_g[:, BLOCK * j:BLOCK * (j + 1)] - d_h) * _SWA_SCALE
                        pts.append(pt.astype(BF16))
                        dsts.append(dst.astype(BF16))
                    pt_g, dst_g = jnp.concatenate(pts, axis=1), jnp.concatenate(dsts, axis=1)
                    q_g = jnp.concatenate([_head_cols(qv, h) for h in heads], axis=0).astype(BF16)
                    do_g = jnp.concatenate([_head_cols(dov, h) for h in heads], axis=0).astype(BF16)
                    dkb_s[sub, :, HEAD_DIM_A * kh:HEAD_DIM_A * (kh + 1)] = _dot(dst_g, q_g)
                    dvb_s[sub, :, HEAD_DIM_A * kh:HEAD_DIM_A * (kh + 1)] = _dot(pt_g, do_g)
                    dq_g = _dot(_head_rows(kb_t, kh).astype(BF16), dst_g)
                    dq_t.extend(dq_g[:, BLOCK * j:BLOCK * (j + 1)] for j in range(_GROUP_A))
                dq_ref[own, :] = jnp.concatenate(dq_t, axis=0).T
            ds_ref[...] += dsink
            for keep, out, parts in ((dk_keep, dk_ref, dkb_s), (dv_keep, dv_ref, dvb_s)):
                out[0:span - BLOCK, :] = keep[0:span - BLOCK, :]
                out[span - BLOCK:span, :] = keep[span - BLOCK:span, :] + parts[0, 0:BLOCK, :]
                for s in range(per - 1):
                    keep[BLOCK * s:BLOCK * (s + 1), :] = parts[s, BLOCK:2 * BLOCK, :] + parts[s + 1, 0:BLOCK, :]
                keep[span - BLOCK:span, :] = parts[per - 1, BLOCK:2 * BLOCK, :]

        @pl.when(n == steps)
        def _():
            dk_ref[...] = dk_keep[...]
            dv_ref[...] = dv_keep[...]

    last = steps - 1
    cur = lambda n: (jnp.minimum(n, last), 0)
    cur_t = lambda n: (0, jnp.minimum(n, last))
    prv = lambda n: jnp.maximum(per * jnp.minimum(n, last) - 1, 0)
    out_prev = lambda n: (jnp.maximum(n - 1, 0), 0)
    return pl.pallas_call(
        body, name="swa_bwd", grid=(steps + 1,),
        in_specs=[pl.BlockSpec((span, WIDTH_A), lambda n: (jnp.minimum(n, last), _QA_BLK)),
                  pl.BlockSpec((span, LANES), lambda n: (jnp.minimum(n, last), _KA_BLK)),
                  pl.BlockSpec((BLOCK, LANES), lambda n: (prv(n), _KA_BLK)),
                  pl.BlockSpec((span, LANES), lambda n: (jnp.minimum(n, last), _VA_BLK)),
                  pl.BlockSpec((BLOCK, LANES), lambda n: (prv(n), _VA_BLK)),
                  pl.BlockSpec((span, WIDTH_A), cur), pl.BlockSpec((N_HEADS_A, span), cur_t),
                  pl.BlockSpec((N_HEADS_A, span), cur_t), pl.BlockSpec((1, span), cur_t),
                  pl.BlockSpec((span, 1), cur), pl.BlockSpec((BLOCK, 1), lambda n: (prv(n), 0)),
                  _full((1, N_HEADS_A))],
        out_specs=[pl.BlockSpec((span, WIDTH_A), cur), pl.BlockSpec((span, LANES), out_prev),
                   pl.BlockSpec((span, LANES), out_prev), _full((1, LANES))],
        out_shape=[jax.ShapeDtypeStruct((t, WIDTH_A), F32), jax.ShapeDtypeStruct((t, LANES), F32),
                   jax.ShapeDtypeStruct((t, LANES), F32), jax.ShapeDtypeStruct((1, LANES), F32)],
        scratch_shapes=[pltpu.VMEM((per, 2 * BLOCK, LANES), F32), pltpu.VMEM((per, 2 * BLOCK, LANES), F32),
                        pltpu.VMEM((span, LANES), F32), pltpu.VMEM((span, LANES), F32)],
        compiler_params=_params(("arbitrary",)),
    )(proj, proj, proj, proj, proj, d_out, lse, delta, posr, posc, posc, sinks)


def _in_bwd(dproj, w_in_t, x, dx1, g1, gp):
    t = x.shape[0]
    tm = 512
    steps = t // tm

    def body(dp_ref, w_ref, x_ref, dx1_ref, g_ref, gp_ref, dx_ref, dg_ref, land_ref, send_sems, recv_sems):
        i = pl.program_id(0)

        @pl.when(i == 0)
        def _():
            dg_ref[...] = jnp.zeros(dg_ref.shape, F32)
            _scatter_start(gp_ref, land_ref, send_sems, recv_sems)

        for rows in _row_halves(tm):
            dh = _dot(dp_ref[rows, :], w_ref[...])
            xv = x_ref[rows, :]
            r = _rms(xv)
            dx, dg = _norm_bwd(dh, xv * r, r, g_ref[...])
            dx_ref[rows, :] = dx1_ref[rows, :] + dx
            dg_ref[...] += dg

        @pl.when(i == steps - 1)
        def _():
            _scatter_wait(gp_ref, land_ref, send_sems, recv_sems)

    row = lambda i: (i, 0)
    blk = pl.BlockSpec((tm, D_MODEL), row)
    return pl.pallas_call(
        body, name="in_bwd", grid=(steps,),
        in_specs=[pl.BlockSpec((tm, D_IN_PAD), row), _full((D_IN_PAD, D_MODEL)), blk, blk, _full((1, D_MODEL)), _HBM],
        out_specs=[blk, _full((1, D_MODEL)), _HBM],
        out_shape=[jax.ShapeDtypeStruct((t, D_MODEL), F32), jax.ShapeDtypeStruct((1, D_MODEL), F32),
                   jax.ShapeDtypeStruct((3,) + gp.shape[1:], gp.dtype)],
        scratch_shapes=[pltpu.SemaphoreType.DMA((3,)), pltpu.SemaphoreType.DMA((3,))],
        compiler_params=_params(("arbitrary",)),
    )(dproj, w_in_t, x, dx1, g1, gp)


def _adamw_store(w, g, m, v, out_refs):
    g_out, d_out, m_out, v_out = out_refs
    m_new = ADAM_B1 * m + (1.0 - ADAM_B1) * g
    v_new = ADAM_B2 * v + (1.0 - ADAM_B2) * jnp.square(g)
    m_hat = m_new / (1.0 - ADAM_B1 ** ADAM_STEP)
    v_hat = v_new / (1.0 - ADAM_B2 ** ADAM_STEP)
    g_out[...] = g
    d_out[...] = -ADAM_LR * (m_hat / (jnp.sqrt(v_hat) + ADAM_EPS) + ADAM_WD * w)
    m_out[...] = m_new
    v_out[...] = v_new


_SMALL_SLOTS = {"pre_norm_mix": (0, 0, D_MODEL), "post_norm_mix": (1, 0, D_MODEL), "pre_norm_mlp": (2, 0, D_MODEL),
                "post_norm_mlp": (3, 0, D_MODEL), "q_a_norm": (4, 0, Q_LORA), "kv_a_norm": (4, Q_LORA, KV_LORA),
                "sinks": (4, Q_LORA + KV_LORA, N_HEADS_A)}
_LOSS_ROW = 5


def _adamw_small(red, w, m, v):
    names = tuple(_SMALL_SLOTS)
    n = len(names)

    def body(*refs):
        red_ref, ws, ms, vs, outs = refs[0], refs[1:1 + n], refs[1 + n:1 + 2 * n], refs[1 + 2 * n:1 + 3 * n], refs[1 + 3 * n:]
        for k, name in enumerate(names):
            row, lane, width = _SMALL_SLOTS[name]
            g = red_ref[row:row + 1, lane:lane + width]
            _adamw_store(ws[k][...], g, ms[k][...], vs[k][...], outs[4 * k:4 * k + 4])

    vmem = pl.BlockSpec(memory_space=pltpu.VMEM)
    res = pl.pallas_call(
        body, name="adamw_small", in_specs=[vmem] * (1 + 3 * n), out_specs=[vmem] * (4 * n),
        out_shape=[jax.ShapeDtypeStruct(w[name].shape, F32) for name in names for _ in range(4)],
    )(red, *[w[k] for k in names], *[m[k] for k in names], *[v[k] for k in names])
    return {name: res[4 * k:4 * k + 4] for k, name in enumerate(names)}


_ADAMW_RIDERS = ("w_up", "w_down", "w_out")


def _dw_in_adamw(dproj, h, g_parts, w, m, v):
    t, cols = dproj.shape
    tm, tk = cols // 2, min(1024, t)
    rows_out = N_CHIPS * SHARD_SHAPES["w_in"][1]
    nk = t // tk
    steps = 2 * nk
    names = _ADAMW_RIDERS
    n = len(names)

    def body(a_ref, b_ref, *rest):
        g1s, g2s, ws, ms, vs = (rest[n * j:n * (j + 1)] for j in range(5))
        o_ref, outs = rest[5 * n], rest[5 * n + 1:]

        @pl.when(pl.program_id(2) == 0)
        def _():
            o_ref[...] = jnp.zeros(o_ref.shape, F32)

        o_ref[...] += _dot_tn(a_ref[...], b_ref[...])
        for j in range(n):
            _adamw_store(ws[j][...], g1s[j][...] + g2s[j][...], ms[j][...], vs[j][...], outs[4 * j:4 * j + 4])

    def rider_spec(name, packed):
        br = SHARD_SHAPES[name][0] // steps
        first = _row_offset(GROUP_B, name) // br if packed else 0
        return pl.BlockSpec((br, D_MODEL), lambda i, j, k: (first + i * nk + k, 0))

    g_specs = [rider_spec(name, True) for name in names]
    own_specs = [rider_spec(name, False) for name in names]
    res = pl.pallas_call(
        body, name="dw_in", grid=(2, 1, nk),
        in_specs=[pl.BlockSpec((tk, tm), lambda i, j, k: (k, i)), pl.BlockSpec((tk, D_MODEL), lambda i, j, k: (k, 0))]
        + g_specs * 2 + own_specs * 3,
        out_specs=[pl.BlockSpec((tm, D_MODEL), lambda i, j, k: (i, 0))] + [s for s in own_specs for _ in range(4)],
        out_shape=[jax.ShapeDtypeStruct((rows_out, D_MODEL), F32)]
        + [jax.ShapeDtypeStruct(SHARD_SHAPES[name], F32) for name in names for _ in range(4)],
        compiler_params=_params(("arbitrary", "arbitrary", "arbitrary")),
    )(dproj, h, *[g_parts[0]] * n, *[g_parts[1]] * n, *[w[k] for k in names], *[m[k] for k in names],
      *[v[k] for k in names])
    return res[0], {name: res[1 + 4 * j:5 + 4 * j] for j, name in enumerate(names)}


def _adamw(w, g_parts, m, v, name, block, g_row_off=0):
    r, c = w.shape
    br, bc = block
    ng = len(g_parts)

    def body(*refs):
        w_ref, g_refs, m_ref, v_ref = refs[0], refs[1:1 + ng], refs[1 + ng], refs[2 + ng]
        g = g_refs[0][...]
        for gr in g_refs[1:]:
            g = g + gr[...]
        _adamw_store(w_ref[...], g, m_ref[...], v_ref[...], refs[3 + ng:])

    assert g_row_off % br == 0 and r % br == 0 and c % bc == 0
    blk = pl.BlockSpec(block, lambda i, j: (i, j))
    g_blk = pl.BlockSpec(block, lambda i, j: (i + g_row_off // br, j))
    return pl.pallas_call(
        body, name=name, grid=(r // br, c // bc),
        in_specs=[blk] + [g_blk] * ng + [blk, blk], out_specs=[blk] * 4,
        out_shape=[jax.ShapeDtypeStruct((r, c), F32)] * 4,
        compiler_params=_params(("parallel", "parallel")),
    )(w, *g_parts, m, v)


_HBM = pl.BlockSpec(memory_space=pltpu.HBM)


def _other_chips(x, y):
    return ((1 - x, y), (x, 1 - y), (1 - x, 1 - y))


def _gather_copies(src, out, send_sems, recv_sems, local_sem):
    x, y, c = lax.axis_index("x"), lax.axis_index("y"), lax.axis_index("c")
    me = 2 * x + y
    local = pltpu.make_async_copy(src, out.at[me], local_sem)

    def copies(arriving):
        return [pltpu.make_async_remote_copy(src_ref=src, dst_ref=out.at[2 * px + py if arriving else me],
                                             send_sem=send_sems.at[j], recv_sem=recv_sems.at[j], device_id=(px, py, c),
                                             device_id_type=MESH)
                for j, (px, py) in enumerate(_other_chips(x, y))]

    return local, copies


def _gather_start(src, out, send_sems, recv_sems, local_sem):
    local, copies = _gather_copies(src, out, send_sems, recv_sems, local_sem)
    local.start()
    for cp in copies(False):
        cp.start()


def _gather_wait(src, out, send_sems, recv_sems, local_sem):
    local, copies = _gather_copies(src, out, send_sems, recv_sems, local_sem)
    for cp in copies(True):
        cp.wait_recv()
    for cp in copies(False):
        cp.wait_send()
    local.wait()


def _scatter_copies(src, land, send_sems, recv_sems):
    x, y, c = lax.axis_index("x"), lax.axis_index("y"), lax.axis_index("c")
    return [pltpu.make_async_remote_copy(src_ref=src.at[2 * px + py], dst_ref=land.at[j], send_sem=send_sems.at[j],
                                         recv_sem=recv_sems.at[j], device_id=(px, py, c), device_id_type=MESH)
            for j, (px, py) in enumerate(_other_chips(x, y))]


def _scatter_start(src, land, send_sems, recv_sems):
    for cp in _scatter_copies(src, land, send_sems, recv_sems):
        cp.start()


def _scatter_wait(src, land, send_sems, recv_sems):
    copies = _scatter_copies(src, land, send_sems, recv_sems)
    for cp in copies:
        cp.wait_recv()
    for cp in copies:
        cp.wait_send()


def _all_gather_chips(packed):
    r = packed.shape[0]
    half = r // 2

    def body(src, out, ici_send, ici_recv, d2d_send, d2d_recv, local_sem):
        x, y, c = lax.axis_index("x"), lax.axis_index("y"), lax.axis_index("c")
        me = 2 * x + y
        mine = pl.ds(pl.multiple_of(c * half, 16), half)
        theirs = pl.ds(pl.multiple_of((1 - c) * half, 16), half)
        chips = _other_chips(x, y)
        local = pltpu.make_async_copy(src, out.at[me], local_sem)
        local.start()
        sends = [pltpu.make_async_remote_copy(src_ref=src.at[mine], dst_ref=out.at[me, mine], send_sem=ici_send.at[j],
                                              recv_sem=ici_recv.at[j], device_id=(px, py, c), device_id_type=MESH)
                 for j, (px, py) in enumerate(chips)]
        for cp in sends:
            cp.start()
        passed = []
        for j, (px, py) in enumerate(chips):
            block = 2 * px + py
            pltpu.make_async_remote_copy(src_ref=src.at[mine], dst_ref=out.at[block, mine], send_sem=ici_send.at[j],
                                         recv_sem=ici_recv.at[j], device_id=(px, py, c), device_id_type=MESH).wait_recv()
            cp = pltpu.make_async_remote_copy(src_ref=out.at[block, mine], dst_ref=out.at[block, mine],
                                              send_sem=d2d_send.at[j], recv_sem=d2d_recv.at[j],
                                              device_id=(x, y, 1 - c), device_id_type=MESH)
            cp.start()
            passed.append(cp)
        for j, (px, py) in enumerate(chips):
            block = 2 * px + py
            pltpu.make_async_remote_copy(src_ref=out.at[block, theirs], dst_ref=out.at[block, theirs],
                                         send_sem=d2d_send.at[j], recv_sem=d2d_recv.at[j],
                                         device_id=(x, y, 1 - c), device_id_type=MESH).wait_recv()
        for cp in sends + passed:
            cp.wait_send()
        local.wait()

    sems = pltpu.SemaphoreType.DMA((3,))
    return pl.pallas_call(
        body, name="ag_weights", in_specs=[_HBM], out_specs=_HBM,
        out_shape=jax.ShapeDtypeStruct((N_CHIPS,) + packed.shape, packed.dtype),
        scratch_shapes=[sems, sems, sems, sems, pltpu.SemaphoreType.DMA(())],
    )(packed)


def _sum4(gp, land, chip, name):
    _, r, w = gp.shape
    tr = 256 if r % 256 == 0 else 128

    def body(chip_ref, o_ref, l_ref, s_ref):
        s_ref[...] = ((o_ref[0] + l_ref[0].astype(F32)) + l_ref[1].astype(F32)) + l_ref[2].astype(F32)

    return pl.pallas_call(
        body, name=name,
        grid_spec=pltpu.PrefetchScalarGridSpec(
            num_scalar_prefetch=1, grid=(r // tr,),
            in_specs=[pl.BlockSpec((1, tr, w), lambda i, chip_ref: (chip_ref[0], i, 0)),
                      pl.BlockSpec((3, tr, w), lambda i, chip_ref: (0, i, 0))],
            out_specs=pl.BlockSpec((tr, w), lambda i, chip_ref: (i, 0))),
        out_shape=jax.ShapeDtypeStruct((r, w), F32),
        compiler_params=_params(("parallel",)),
    )(chip, gp, land)


def _sibling_copy(src, got, send_sem, recv_sem):
    x, y, c = lax.axis_index("x"), lax.axis_index("y"), lax.axis_index("c")
    return pltpu.make_async_remote_copy(src_ref=src, dst_ref=got, send_sem=send_sem, recv_sem=recv_sem,
                                        device_id=(x, y, 1 - c), device_id_type=MESH)


def _swap_sibling(s, name):
    def body(src, got, send_sem, recv_sem):
        cp = _sibling_copy(src, got, send_sem, recv_sem)
        cp.start()
        cp.wait_recv()
        cp.wait_send()

    return pl.pallas_call(
        body, name=name, in_specs=[_HBM], out_specs=_HBM,
        out_shape=jax.ShapeDtypeStruct(s.shape, s.dtype),
        scratch_shapes=[pltpu.SemaphoreType.DMA(()), pltpu.SemaphoreType.DMA(())],
    )(s)


def _all_reduce_small(dsmall, loss):
    n_dev = 8
    names = tuple(_SMALL_SLOTS)
    shape = (8, D_MODEL)

    def body(*refs):
        parts, loss_ref = refs[:len(names)], refs[len(names)]
        out, src, gath, send_sems, recv_sems = refs[len(names) + 1:]
        x, y, c = lax.axis_index("x"), lax.axis_index("y"), lax.axis_index("c")
        me = 4 * x + 2 * y + c
        src[...] = jnp.zeros(shape, F32)
        for name, part in zip(names, parts):
            row, lane, _ = _SMALL_SLOTS[name]
            src[row:row + 1, lane:lane + part.shape[1]] = part[...]
        src[_LOSS_ROW:_LOSS_ROW + 1, 0:LANES] = loss_ref[...]
        gath[me] = src[...]
        peers = []
        for k in range(1, n_dev):
            px = 1 - x if (k >> 2) & 1 else x
            py = 1 - y if (k >> 1) & 1 else y
            pc = 1 - c if k & 1 else c
            peers.append((px, py, pc))
        sends = []
        for j, peer in enumerate(peers):
            cp = pltpu.make_async_remote_copy(src_ref=src, dst_ref=gath.at[me], send_sem=send_sems.at[j],
                                              recv_sem=recv_sems.at[j], device_id=peer, device_id_type=MESH)
            cp.start()
            sends.append(cp)
        for j, (px, py, pc) in enumerate(peers):
            pltpu.make_async_remote_copy(src_ref=src, dst_ref=gath.at[4 * px + 2 * py + pc], send_sem=send_sems.at[j],
                                         recv_sem=recv_sems.at[j], device_id=(px, py, pc), device_id_type=MESH).wait_recv()
        for cp in sends:
            cp.wait_send()
        acc = gath[0]
        for d in range(1, n_dev):
            acc = acc + gath[d]
        out[...] = acc

    vmem = pl.BlockSpec(memory_space=pltpu.VMEM)
    return pl.pallas_call(
        body, name="ar_small", in_specs=[vmem] * (len(names) + 1), out_specs=vmem,
        out_shape=jax.ShapeDtypeStruct(shape, F32),
        scratch_shapes=[pltpu.VMEM(shape, F32), pltpu.VMEM((n_dev,) + shape, F32),
                        pltpu.SemaphoreType.DMA((n_dev - 1,)), pltpu.SemaphoreType.DMA((n_dev - 1,))],
    )(*[dsmall[k] for k in names], loss)


_W_IN_ROWS = SHARD_SHAPES["w_in"][1]


def _shard_rows(name, a):
    return jnp.transpose(a) if name == "w_in" else a.reshape(PACK_ROWS[name], D_MODEL)


def _pack(group, shards, dtype):
    parts = [_shard_rows(n, shards[n]).astype(dtype) for n in group]
    pad = -sum(PACK_ROWS[n] for n in group) % LANES
    if pad:
        parts.append(jnp.zeros((pad, D_MODEL), dtype))
    return jnp.concatenate(parts, axis=0)


def _col_sharded_full(g, name, group):
    r, c = SHARD_SHAPES[name]
    off = _row_offset(group, name)
    blocks = g[:, off:off + PACK_ROWS[name]].reshape(N_CHIPS, r, c)
    return jnp.transpose(blocks, (1, 0, 2)).reshape(r, N_CHIPS * c)


def _col_sharded_blocks(d, name):
    r, c = SHARD_SHAPES[name]
    return jnp.transpose(d.reshape(r, N_CHIPS, c), (1, 0, 2)).reshape(N_CHIPS, PACK_ROWS[name], D_MODEL)


def _weights_a(g):
    dt = g.dtype
    w_in_t = jnp.concatenate([g[c, :_W_IN_ROWS] for c in range(N_CHIPS)]
                             + [jnp.zeros((D_IN_PAD - N_CHIPS * _W_IN_ROWS, D_MODEL), dt)], axis=0)
    wq = _col_sharded_full(g, "w_q_b", GROUP_A).reshape(Q_LORA, N_HEADS_B, Q_HEAD_B)
    wq_p = jnp.concatenate([wq, jnp.zeros((Q_LORA, N_HEADS_B, HEAD_PAD - Q_HEAD_B), dt)], axis=2).reshape(Q_LORA, MLA_W)
    wkv = _col_sharded_full(g, "w_kv_b", GROUP_A).reshape(KV_LORA, N_HEADS_B, QK_NOPE + V_DIM_B)
    zk = jnp.zeros((KV_LORA, N_HEADS_B, HEAD_PAD - QK_NOPE), dt)
    wk_p = jnp.concatenate([wkv[:, :, :QK_NOPE], zk], axis=2).reshape(KV_LORA, MLA_W)
    wv = wkv[:, :, QK_NOPE:].reshape(KV_LORA, N_HEADS_B * V_DIM_B)
    return dict(w_in=w_in_t, wq=wq_p, wk=wk_p, wv=wv, wv_t=jnp.transpose(wv))


def _grad_blocks_a(dw_in_t, dwq_p, dwk_p, dwv):
    dwq = dwq_p.reshape(Q_LORA, N_HEADS_B, HEAD_PAD)[:, :, :Q_HEAD_B].reshape(Q_LORA, N_HEADS_B * Q_HEAD_B)
    dwk = dwk_p.reshape(KV_LORA, N_HEADS_B, HEAD_PAD)[:, :, :QK_NOPE]
    dwkv = jnp.concatenate([dwk, dwv.reshape(KV_LORA, N_HEADS_B, V_DIM_B)], axis=2)
    dwkv = dwkv.reshape(KV_LORA, N_HEADS_B * (QK_NOPE + V_DIM_B))
    pad = -sum(PACK_ROWS[n] for n in GROUP_A) % LANES
    return [dw_in_t.reshape(N_CHIPS, _W_IN_ROWS, D_MODEL), _col_sharded_blocks(dwq, "w_q_b"),
            _col_sharded_blocks(dwkv, "w_kv_b"), jnp.zeros((N_CHIPS, pad, D_MODEL), F32)]


def _rope_freq_lanes():
    freqs = ROPE_THETA ** (-jnp.arange(0, QK_ROPE, 2, dtype=F32) / QK_ROPE)
    return jnp.concatenate([jnp.zeros((QK_NOPE,), F32), freqs, freqs,
                            jnp.zeros((HEAD_PAD - Q_HEAD_B,), F32)]).reshape(1, LANES)


def _fwd_bwd(x, positions, target, w, m, v):
    t = x.shape[0]
    wa = _weights_a(_all_gather_chips(_pack(GROUP_A, w, BF16)))
    posr = positions.astype(F32).reshape(1, t)
    posc = posr.reshape(t, 1)
    freq = _rope_freq_lanes()
    g1, g2, g3, g4 = w["pre_norm_mix"], w["post_norm_mix"], w["pre_norm_mlp"], w["post_norm_mlp"]
    qan, kvan, sinks = w["q_a_norm"], w["kv_a_norm"], w["sinks"]

    h, proj = _proj_fwd(x, g1, wa["w_in"])
    out_a, lse_a = _swa_fwd(proj, posc, posr, sinks)
    qm, km, qt, kt, vt = _mla_prep_fwd(proj, posc, freq, qan, kvan, wa["wq"], wa["wk"], wa["wv_t"])
    out_bt, lse_b, wb = _mla_fwd(km, qt, vt, _pack(GROUP_B, w, BF16))
    w_oa, w_ob = _col_sharded_full(wb, "w_o_a", GROUP_B), _col_sharded_full(wb, "w_o_b", GROUP_B)
    merged, y, x1, h2 = _mix_out_fwd(out_a, out_bt, proj, x, w_oa, w_ob, wb, g2, g3)
    a = _up_fwd(h2, wb)
    dx2, dyd, dg4, loss = _down_fwd_loss(a, wb, x1, target, g4)

    gp_b = _dw_into_blocks(a, dyd, "w_down", 1024, _TK_DW)
    du = _down_bwd(dyd, wb, a)
    gp_b = _dw_into_blocks(h2, du, "w_up", 1024, _TK_DW, gp_b)
    dx1, dy, dg3, dg2 = _up_bwd(du, wb, x1, dx2, y, g3, g2)
    gp_b = _dw_into_blocks(merged, dy, "w_out", 1024, _TK_DW, gp_b)
    dw_oa, dw_ob, dproj, d_out_a, d_out_b, d_out_bt, del_a, del_b = _mix_out_bwd(dy, out_a, out_bt, proj, w_oa, w_ob, wb)
    small_b = jnp.concatenate([_col_sharded_blocks(dw_oa, "w_o_a"), _col_sharded_blocks(dw_ob, "w_o_b")], axis=1)
    gp_b = lax.dynamic_update_slice(gp_b, small_b, (0, _row_offset(GROUP_B, "w_o_a"), 0))
    dqm, dkm, dvm, land_b = _mla_bwd(qm, km, qt, kt, vt, d_out_b, d_out_bt, lse_b, del_b, gp_b)
    chip = (2 * lax.axis_index("x") + lax.axis_index("y")).astype(jnp.int32).reshape(1)
    part_b = _sum4(gp_b, land_b, chip, "rs_sum_b")
    dcq, dckv, dkr, dwq, dwk, dwv, dqan, dkvan, sib_b = _mla_prep_bwd(
        dqm, dkm, dvm, proj, posc, freq, qan, kvan, wa["wq"], wa["wk"], wa["wv"], part_b)
    dqa, dka, dva, dsinks = _swa_bwd(proj, d_out_a, lse_a, del_a, posc, posr, sinks)
    col = 2 * D_MODEL
    for piece in (dqa, dka, dva, dcq, dckv, dkr):
        dproj = lax.dynamic_update_slice(dproj, piece.astype(BF16), (0, col))
        col += piece.shape[1]
    dw_in_t, updated = _dw_in_adamw(dproj, h, [part_b, sib_b], w, m, v)
    parts_a = _grad_blocks_a(dw_in_t, dwq, dwk, dwv)
    gp_a = jnp.concatenate([p.astype(BF16) for p in parts_a], axis=1)
    grad_x, dg1, land_a = _in_bwd(dproj, wa["w_in"], x, dx1, g1, gp_a)

    own_a = jnp.concatenate([lax.dynamic_slice_in_dim(p, chip[0], 1, axis=0) for p in parts_a], axis=1)
    part_a = _sum4(own_a, land_a, jnp.zeros((1,), jnp.int32), "rs_sum_a")
    reduced = {GROUP_A: [part_a, _swap_sibling(part_a, "rs_swap_a")], GROUP_B: [part_b, sib_b]}
    dsmall = dict(pre_norm_mix=dg1, post_norm_mix=dg2, pre_norm_mlp=dg3, post_norm_mlp=dg4,
                  q_a_norm=dqan, kv_a_norm=dkvan, sinks=dsinks)
    return loss, grad_x, reduced, dsmall, updated


def kernel(x, positions, pre_norm_mix, w_in, q_a_norm, w_q_b, kv_a_norm, w_kv_b, sinks, w_o_a, w_o_b, w_out, post_norm_mix, pre_norm_mlp, w_up, w_down, post_norm_mlp, loss_target, m_pre_norm_mix, m_w_in, m_q_a_norm, m_w_q_b, m_kv_a_norm, m_w_kv_b, m_sinks, m_w_o_a, m_w_o_b, m_w_out, m_post_norm_mix, m_pre_norm_mlp, m_w_up, m_w_down, m_post_norm_mlp, v_pre_norm_mix, v_w_in, v_q_a_norm, v_w_q_b, v_kv_a_norm, v_w_kv_b, v_sinks, v_w_o_a, v_w_o_b, v_w_out, v_post_norm_mix, v_pre_norm_mlp, v_w_up, v_w_down, v_post_norm_mlp):
    w = dict(pre_norm_mix=pre_norm_mix, w_in=w_in[0], q_a_norm=q_a_norm, w_q_b=w_q_b[0], kv_a_norm=kv_a_norm,
             w_kv_b=w_kv_b[0], sinks=sinks, w_o_a=w_o_a[0], w_o_b=w_o_b[0], w_out=w_out[0],
             post_norm_mix=post_norm_mix, pre_norm_mlp=pre_norm_mlp, w_up=w_up[0], w_down=w_down[0],
             post_norm_mlp=post_norm_mlp)
    m = dict(pre_norm_mix=m_pre_norm_mix, w_in=m_w_in[0], q_a_norm=m_q_a_norm, w_q_b=m_w_q_b[0],
             kv_a_norm=m_kv_a_norm, w_kv_b=m_w_kv_b[0], sinks=m_sinks, w_o_a=m_w_o_a[0], w_o_b=m_w_o_b[0],
             w_out=m_w_out[0], post_norm_mix=m_post_norm_mix, pre_norm_mlp=m_pre_norm_mlp, w_up=m_w_up[0],
             w_down=m_w_down[0], post_norm_mlp=m_post_norm_mlp)
    v = dict(pre_norm_mix=v_pre_norm_mix, w_in=v_w_in[0], q_a_norm=v_q_a_norm, w_q_b=v_w_q_b[0],
             kv_a_norm=v_kv_a_norm, w_kv_b=v_w_kv_b[0], sinks=v_sinks, w_o_a=v_w_o_a[0], w_o_b=v_w_o_b[0],
             w_out=v_w_out[0], post_norm_mix=v_post_norm_mix, pre_norm_mlp=v_pre_norm_mlp, w_up=v_w_up[0],
             w_down=v_w_down[0], post_norm_mlp=v_post_norm_mlp)

    loss, grad_x, reduced, dsmall, updated = _fwd_bwd(x[0], positions, loss_target[0], w, m, v)

    red = _all_reduce_small(dsmall, loss)
    small = _adamw_small(red, w, m, v)

    big = {}
    tr = jnp.transpose
    big["w_in"] = [tr(o)[None] for o in _adamw(tr(w["w_in"]), reduced[GROUP_A], tr(m["w_in"]), tr(v["w_in"]),
                                               "adamw_w_in", (_W_IN_ROWS, 256))]
    for n in _ADAMW_RIDERS:
        big[n] = [o[None] for o in updated[n]]
    for group, names in ((GROUP_A, ("w_q_b", "w_kv_b")), (GROUP_B, ("w_o_a", "w_o_b"))):
        for n in names:
            off = _row_offset(group, n)
            g_parts = [p[off:off + PACK_ROWS[n]].reshape(SHARD_SHAPES[n]) for p in reduced[group]]
            big[n] = [o[None] for o in _adamw(w[n], g_parts, m[n], v[n], "adamw_" + n, SHARD_SHAPES[n])]

    outs = [big[n][k] if n in big else small[n][k] for k in range(4) for n in WEIGHTS]
    return (red[_LOSS_ROW, 0], grad_x[None], *outs)
```
